```python
import jax, jax.numpy as jnp
from jax import lax
import numpy as np

D_MODEL = 1024
BATCH = 8
SEQ = 4096
DEPTH = 1

D_MIX = 2 * D_MODEL
D_SSD = D_MIX // 2
D_ATT = D_MIX - D_SSD
SSD_HEADDIM = 64
SSD_HEADS = D_SSD // SSD_HEADDIM
SSD_GROUPS = 2
SSD_STATE = 128
CONV_K = 4
CHUNK = 128
ATT_HEAD_DIM = 64
ATT_Q_HEADS = D_ATT // ATT_HEAD_DIM
ATT_KV_HEADS = 4
WINDOW = 128
ROPE_THETA = 500000.0
ROPE_DIM = ATT_HEAD_DIM // 4
ALPHA = (2.0 * DEPTH) ** 0.25
BETA = (8.0 * DEPTH) ** -0.25
LN_EPS = 1e-5
RMS_EPS = 1e-5

D_BC = SSD_GROUPS * SSD_STATE
D_XBC = D_SSD + 2 * D_BC
D_KV = ATT_KV_HEADS * ATT_HEAD_DIM
OFF_Z = 0
OFF_XBC = OFF_Z + D_SSD
OFF_DT = OFF_XBC + D_XBC
OFF_Q = OFF_DT + SSD_HEADS
OFF_K = OFF_Q + D_ATT
OFF_V = OFF_K + D_KV
OFF_G = OFF_V + D_KV
D_IN_PROJ = OFF_G + D_ATT
SPLIT_IDX = (OFF_XBC, OFF_DT, OFF_Q, OFF_K, OFF_V, OFF_G)

kernel_name = "hybrid_ssd_swa_sink_deepnorm"


def layer_norm(x, g, b):
    xf = x.astype(jnp.float32)
    mu = jnp.mean(xf, axis=-1, keepdims=True)
    var = jnp.mean(jnp.square(xf - mu), axis=-1, keepdims=True)
    y = (xf - mu) * lax.rsqrt(var + LN_EPS) * g.astype(jnp.float32) + b.astype(jnp.float32)
    return y.astype(x.dtype)


def rope_tables(positions):
    inv = ROPE_THETA ** (-jnp.arange(0, ROPE_DIM, 2, dtype=jnp.float32) / ROPE_DIM)
    ang = positions.astype(jnp.float32)[..., None] * inv
    return jnp.cos(ang), jnp.sin(ang)


def apply_partial_rope(t, cos, sin):
    rot = t[..., :ROPE_DIM].astype(jnp.float32)
    rest = t[..., ROPE_DIM:]
    r1, r2 = rot[..., :ROPE_DIM // 2], rot[..., ROPE_DIM // 2:]
    c, s = cos[:, :, None, :], sin[:, :, None, :]
    rot = jnp.concatenate([r1 * c - r2 * s, r2 * c + r1 * s], axis=-1)
    return jnp.concatenate([rot.astype(t.dtype), rest], axis=-1)


def causal_depthwise_conv(u, w, b):
    c = u.shape[-1]
    out = lax.conv_general_dilated(
        u, w[:, None, :].astype(u.dtype), window_strides=(1,),
        padding=[(CONV_K - 1, 0)], dimension_numbers=("NWC", "WIO", "NWC"),
        feature_group_count=c)
    return out + b.astype(u.dtype)


def gated_rmsnorm(y, z, w):
    yf = (y.astype(jnp.float32) * jax.nn.silu(z.astype(jnp.float32)))
    yg = yf.reshape(*yf.shape[:-1], SSD_GROUPS, D_SSD // SSD_GROUPS)
    yg = yg * lax.rsqrt(jnp.mean(jnp.square(yg), axis=-1, keepdims=True) + RMS_EPS)
    return yg.reshape(yf.shape) * w.astype(jnp.float32)


def ssd_chunked(X, dA, Bc, Cc):
    a = jnp.moveaxis(dA, 2, -1).astype(jnp.float32)
    a_cs = jnp.cumsum(a, axis=-1)
    T = a.shape[-1]
    causal = jnp.tril(jnp.ones((T, T), dtype=bool))
    seg = a_cs[..., :, None] - a_cs[..., None, :]
    Lmat = jnp.where(causal, jnp.exp(jnp.where(causal, seg, 0.0)), 0.0)
    cb = jnp.einsum("bclgn,bcsgn->bcgls", Cc, Bc).astype(jnp.float32)
    M = cb[:, :, :, None] * Lmat
    y_diag = jnp.einsum("bcgrls,bcsgrp->bclgrp", M, X)
    decay_states = jnp.moveaxis(jnp.exp(a_cs[..., -1:] - a_cs), -1, 2)
    states = jnp.einsum("bclgn,bclgrp->bcgrpn", Bc, X * decay_states[..., None]).astype(jnp.float32)
    chunk_decay = jnp.exp(a_cs[..., -1])

    def step(h, inp):
        s_c, d_c = inp
        return h * d_c[..., None, None] + s_c, h

    h0 = jnp.zeros(states.shape[:1] + states.shape[2:], jnp.float32)
    _, prev = lax.scan(step, h0, (jnp.moveaxis(states, 1, 0), jnp.moveaxis(chunk_decay, 1, 0)))
    prev = jnp.moveaxis(prev, 0, 1)
    decay_out = jnp.moveaxis(jnp.exp(a_cs), -1, 2)
    y_off = jnp.einsum("bclgn,bcgrpn->bclgrp", Cc, prev) * decay_out[..., None]
    return y_diag + y_off


def ssd_branch(z, xbc, dt, conv_w, conv_b, dt_bias, a_log, d_skip, norm_w):
    b, L, _ = xbc.shape
    R = SSD_HEADS // SSD_GROUPS
    nc = L // CHUNK
    xbc = jax.nn.silu(causal_depthwise_conv(xbc, conv_w, conv_b))
    xs, Bm, Cm = jnp.split(xbc, [D_SSD, D_SSD + D_BC], axis=-1)
    dt = jax.nn.softplus(dt.astype(jnp.float32) + dt_bias.astype(jnp.float32))
    A = -jnp.exp(a_log.astype(jnp.float32))
    xh = xs.reshape(b, L, SSD_GROUPS, R, SSD_HEADDIM)
    dth = dt.reshape(b, L, SSD_GROUPS, R)
    X = (xh.astype(jnp.float32) * dth[..., None]).reshape(b, nc, CHUNK, SSD_GROUPS, R, SSD_HEADDIM)
    dA = (dth * A.reshape(SSD_GROUPS, R)).reshape(b, nc, CHUNK, SSD_GROUPS, R)
    Bc = Bm.reshape(b, nc, CHUNK, SSD_GROUPS, SSD_STATE)
    Cc = Cm.reshape(b, nc, CHUNK, SSD_GROUPS, SSD_STATE)
    y = ssd_chunked(X, dA, Bc, Cc).reshape(b, L, SSD_GROUPS, R, SSD_HEADDIM)
    y = y + d_skip.astype(jnp.float32).reshape(SSD_GROUPS, R, 1) * xh.astype(jnp.float32)
    y = gated_rmsnorm(y.reshape(b, L, D_SSD), z, norm_w)
    return y.astype(z.dtype)


def swa_branch(q, k, v, g, cos, sin, sinks):
    b, L, _ = q.shape
    nb = L // WINDOW
    R = ATT_Q_HEADS // ATT_KV_HEADS
    q = apply_partial_rope(q.reshape(b, L, ATT_Q_HEADS, ATT_HEAD_DIM), cos, sin)
    k = apply_partial_rope(k.reshape(b, L, ATT_KV_HEADS, ATT_HEAD_DIM), cos, sin)
    v = v.reshape(b, L, ATT_KV_HEADS, ATT_HEAD_DIM)
    qb = q.reshape(b, nb, WINDOW, ATT_KV_HEADS, R, ATT_HEAD_DIM)
    kb = k.reshape(b, nb, WINDOW, ATT_KV_HEADS, ATT_HEAD_DIM)
    vb = v.reshape(b, nb, WINDOW, ATT_KV_HEADS, ATT_HEAD_DIM)
    pad = ((0, 0), (1, 0), (0, 0), (0, 0), (0, 0))
    kk = jnp.concatenate([jnp.pad(kb[:, :-1], pad), kb], axis=2)
    vv = jnp.concatenate([jnp.pad(vb[:, :-1], pad), vb], axis=2)
    scale = ATT_HEAD_DIM ** -0.5
    s = jnp.einsum("bnqkrd,bnskd->bnkrqs", qb, kk).astype(jnp.float32) * scale
    qi = jnp.arange(WINDOW)[:, None]
    si = jnp.arange(2 * WINDOW)[None, :]
    band = (si > qi) & (si <= qi + WINDOW)
    valid = band[None] & ((jnp.arange(nb)[:, None, None] > 0) | (si[None] >= WINDOW))
    s = jnp.where(valid[None, :, None, None], s, -jnp.inf)
    sink = sinks.astype(jnp.float32).reshape(ATT_KV_HEADS, R)[None, None, :, :, None, None]
    m = jnp.maximum(jnp.max(s, axis=-1, keepdims=True), sink)
    p = jnp.exp(s - m)
    denom = jnp.sum(p, axis=-1, keepdims=True) + jnp.exp(sink - m)
    o = jnp.einsum("bnkrqs,bnskd->bnqkrd", (p / denom).astype(vv.dtype), vv)
    o = o.reshape(b, L, D_ATT)
    return (o * jax.nn.silu(g.astype(jnp.float32))).astype(q.dtype)


def hybrid_mixer(x, cos, sin, w_in, conv_w, conv_b, dt_bias, a_log, d_skip, ssd_norm_w, attn_sinks, w_out):
    proj = jnp.einsum("bld,de->ble", x, w_in)
    z, xbc, dt, q, k, v, g = jnp.split(proj, SPLIT_IDX, axis=-1)
    y_ssd = ssd_branch(z, xbc, dt, conv_w, conv_b, dt_bias, a_log, d_skip, ssd_norm_w)
    y_att = swa_branch(q, k, v, g, cos, sin, attn_sinks)
    y = jnp.concatenate([y_ssd, y_att], axis=-1)
    return jnp.einsum("ble,ed->bld", y, w_out)


def _fwd_setup_inputs(seed: int = 0) -> dict:
    key = jax.random.key(seed)
    ks = jax.random.split(key, 12)
    x = jax.random.normal(ks[0], (BATCH, SEQ, D_MODEL), jnp.float32)
    positions = jnp.broadcast_to(jnp.arange(SEQ, dtype=jnp.int32)[None, :], (BATCH, SEQ))
    col_scale = jnp.ones((D_IN_PROJ,), jnp.float32).at[OFF_V:OFF_V + D_KV].set(BETA)
    w_in = jax.random.normal(ks[1], (DEPTH, D_MODEL, D_IN_PROJ), jnp.float32) * (D_MODEL ** -0.5) * col_scale
    conv_w = jax.random.normal(ks[2], (DEPTH, CONV_K, D_XBC), jnp.float32) * (CONV_K ** -0.5)
    conv_b = 0.01 * jax.random.normal(ks[3], (DEPTH, D_XBC), jnp.float32)
    dt0 = jnp.exp(jax.random.uniform(ks[4], (DEPTH, SSD_HEADS), jnp.float32,
                                     minval=float(np.log(1e-3)), maxval=float(np.log(1e-1))))
    dt_bias = dt0 + jnp.log(-jnp.expm1(-dt0))
    a_log = jnp.log(jax.random.uniform(ks[5], (DEPTH, SSD_HEADS), jnp.float32, minval=1.0, maxval=16.0))
    d_skip = 1.0 + 0.1 * jax.random.normal(ks[6], (DEPTH, SSD_HEADS), jnp.float32)
    ssd_norm_w = 1.0 + 0.02 * jax.random.normal(ks[7], (DEPTH, D_SSD), jnp.float32)
    attn_sinks = 0.5 * jax.random.normal(ks[8], (DEPTH, ATT_Q_HEADS), jnp.float32)
    w_out = jax.random.normal(ks[9], (DEPTH, D_MIX, D_MODEL), jnp.float32) * (D_MIX ** -0.5) * BETA
    ln_g = 1.0 + 0.02 * jax.random.normal(ks[10], (DEPTH, D_MODEL), jnp.float32)
    ln_b = 0.02 * jax.random.normal(ks[11], (DEPTH, D_MODEL), jnp.float32)
    return {"x": x, "positions": positions, "w_in": w_in, "conv_w": conv_w, "conv_b": conv_b,
            "dt_bias": dt_bias, "a_log": a_log, "d_skip": d_skip, "ssd_norm_w": ssd_norm_w,
            "attn_sinks": attn_sinks, "w_out": w_out, "ln_g": ln_g, "ln_b": ln_b}


def _fwd_reference(x, positions, w_in, conv_w, conv_b, dt_bias, a_log, d_skip, ssd_norm_w,
              attn_sinks, w_out, ln_g, ln_b):
    cos, sin = rope_tables(positions)
    for l in range(DEPTH):
        h = hybrid_mixer(x, cos, sin, w_in[l], conv_w[l], conv_b[l], dt_bias[l], a_log[l],
                         d_skip[l], ssd_norm_w[l], attn_sinks[l], w_out[l])
        x = layer_norm(ALPHA * x + h.astype(x.dtype), ln_g[l], ln_b[l])
    return x


import jax as _jax
import jax.numpy as _jnp

TWIN_FORMAT = 'train_step'
FWD_PARAMS = ['x', 'positions', 'w_in', 'conv_w', 'conv_b', 'dt_bias', 'a_log', 'd_skip', 'ssd_norm_w', 'attn_sinks', 'w_out', 'ln_g', 'ln_b']
TWIN_WEIGHTS = ['w_in', 'conv_w', 'conv_b', 'dt_bias', 'a_log', 'd_skip', 'ssd_norm_w', 'attn_sinks', 'w_out', 'ln_g', 'ln_b']
TWIN_DIFF_INPUT = 'x'
TWIN_INPUTS = ['x', 'positions', 'w_in', 'conv_w', 'conv_b', 'dt_bias', 'a_log', 'd_skip', 'ssd_norm_w', 'attn_sinks', 'w_out', 'ln_g', 'ln_b', 'loss_target', 'm_w_in', 'm_conv_w', 'm_conv_b', 'm_dt_bias', 'm_a_log', 'm_d_skip', 'm_ssd_norm_w', 'm_attn_sinks', 'm_w_out', 'm_ln_g', 'm_ln_b', 'v_w_in', 'v_conv_w', 'v_conv_b', 'v_dt_bias', 'v_a_log', 'v_d_skip', 'v_ssd_norm_w', 'v_attn_sinks', 'v_w_out', 'v_ln_g', 'v_ln_b']
TWIN_OUTPUTS = ['loss', 'grad_x', 'grad_w_in', 'grad_conv_w', 'grad_conv_b', 'grad_dt_bias', 'grad_a_log', 'grad_d_skip', 'grad_ssd_norm_w', 'grad_attn_sinks', 'grad_w_out', 'grad_ln_g', 'grad_ln_b', 'delta_w_in', 'delta_conv_w', 'delta_conv_b', 'delta_dt_bias', 'delta_a_log', 'delta_d_skip', 'delta_ssd_norm_w', 'delta_attn_sinks', 'delta_w_out', 'delta_ln_g', 'delta_ln_b', 'new_m_w_in', 'new_m_conv_w', 'new_m_conv_b', 'new_m_dt_bias', 'new_m_a_log', 'new_m_d_skip', 'new_m_ssd_norm_w', 'new_m_attn_sinks', 'new_m_w_out', 'new_m_ln_g', 'new_m_ln_b', 'new_v_w_in', 'new_v_conv_w', 'new_v_conv_b', 'new_v_dt_bias', 'new_v_a_log', 'new_v_d_skip', 'new_v_ssd_norm_w', 'new_v_attn_sinks', 'new_v_w_out', 'new_v_ln_g', 'new_v_ln_b']
TWIN_LEAF_KINDS = {'loss': 'loss', 'grad_x': 'grad_x', 'grad_w_in': 'grad_w', 'grad_conv_w': 'grad_w', 'grad_conv_b': 'grad_w', 'grad_dt_bias': 'grad_w', 'grad_a_log': 'grad_w', 'grad_d_skip': 'grad_w', 'grad_ssd_norm_w': 'grad_w', 'grad_attn_sinks': 'grad_w', 'grad_w_out': 'grad_w', 'grad_ln_g': 'grad_w', 'grad_ln_b': 'grad_w', 'delta_w_in': 'delta_w', 'delta_conv_w': 'delta_w', 'delta_conv_b': 'delta_w', 'delta_dt_bias': 'delta_w', 'delta_a_log': 'delta_w', 'delta_d_skip': 'delta_w', 'delta_ssd_norm_w': 'delta_w', 'delta_attn_sinks': 'delta_w', 'delta_w_out': 'delta_w', 'delta_ln_g': 'delta_w', 'delta_ln_b': 'delta_w', 'new_m_w_in': 'new_m', 'new_m_conv_w': 'new_m', 'new_m_conv_b': 'new_m', 'new_m_dt_bias': 'new_m', 'new_m_a_log': 'new_m', 'new_m_d_skip': 'new_m', 'new_m_ssd_norm_w': 'new_m', 'new_m_attn_sinks': 'new_m', 'new_m_w_out': 'new_m', 'new_m_ln_g': 'new_m', 'new_m_ln_b': 'new_m', 'new_v_w_in': 'new_v', 'new_v_conv_w': 'new_v', 'new_v_conv_b': 'new_v', 'new_v_dt_bias': 'new_v', 'new_v_a_log': 'new_v', 'new_v_d_skip': 'new_v', 'new_v_ssd_norm_w': 'new_v', 'new_v_attn_sinks': 'new_v', 'new_v_w_out': 'new_v', 'new_v_ln_g': 'new_v', 'new_v_ln_b': 'new_v'}


def _forward(args):
    return _fwd_reference(*[args[k] for k in FWD_PARAMS])


def _output_shape():
    out = _jax.eval_shape(lambda: _forward(_fwd_setup_inputs(0)))
    return out.shape, out.dtype

N_MICROBATCH = 1
ADAM_LR = 0.001
ADAM_B1 = 0.9
ADAM_B2 = 0.999
ADAM_EPS = 1e-08
ADAM_WD = 0.01
ADAM_STEP = 10
PER_EXAMPLE_BATCH_AXIS = {'x': 0, 'positions': 0, 'loss_target': 0}
SHARED_INPUTS = []
_WEIGHT_DTYPES = {'w_in': _jnp.float32, 'conv_w': _jnp.float32, 'conv_b': _jnp.float32, 'dt_bias': _jnp.float32, 'a_log': _jnp.float32, 'd_skip': _jnp.float32, 'ssd_norm_w': _jnp.float32, 'attn_sinks': _jnp.float32, 'w_out': _jnp.float32, 'ln_g': _jnp.float32, 'ln_b': _jnp.float32}
MOMENT_SCALE = {'w_in': 3.999599e-02, 'conv_w': 5.180834e-02, 'conv_b': 7.658579e-02, 'dt_bias': 1.267887e-01, 'a_log': 1.401364e-01, 'd_skip': 4.052877e-01, 'ssd_norm_w': 6.048443e-02, 'attn_sinks': 2.906346e-03, 'w_out': 1.049944e-01, 'ln_g': 3.198548e+01, 'ln_b': 1.949424e+00}


def _to_microbatches(a, axis):
    t = _jnp.moveaxis(a, axis, 0)
    t = t.reshape((N_MICROBATCH, t.shape[0] // N_MICROBATCH) + t.shape[1:])
    return _jnp.moveaxis(t, 1, axis + 1)


def setup_inputs(seed: int = 0) -> dict:
    inp = _fwd_setup_inputs(seed)
    key = _jax.random.fold_in(_jax.random.key(seed), 7919)
    shape, _ = _output_shape()
    out = dict(inp)
    out["loss_target"] = _jax.random.normal(_jax.random.fold_in(key, 0), shape, _jnp.float32)
    for i, name in enumerate(TWIN_WEIGHTS):
        w = inp[name].astype(_jnp.float32)
        if MOMENT_SCALE is None:
            s = _jnp.sqrt(_jnp.mean(_jnp.square(w)) + 1e-30)
        else:
            s = MOMENT_SCALE[name]
        km, kv = _jax.random.split(_jax.random.fold_in(key, i + 1))
        out[name] = w
        out["m_" + name] = s * _jax.random.normal(km, w.shape, _jnp.float32)
        out["v_" + name] = (s * s) * _jax.random.uniform(kv, w.shape, _jnp.float32, 0.5, 1.5)
    if N_MICROBATCH > 1:
        for name, axis in PER_EXAMPLE_BATCH_AXIS.items():
            out[name] = _to_microbatches(out[name], axis)
    return {'x': out['x'], 'positions': out['positions'], 'w_in': out['w_in'], 'conv_w': out['conv_w'], 'conv_b': out['conv_b'], 'dt_bias': out['dt_bias'], 'a_log': out['a_log'], 'd_skip': out['d_skip'], 'ssd_norm_w': out['ssd_norm_w'], 'attn_sinks': out['attn_sinks'], 'w_out': out['w_out'], 'ln_g': out['ln_g'], 'ln_b': out['ln_b'], 'loss_target': out['loss_target'], 'm_w_in': out['m_w_in'], 'm_conv_w': out['m_conv_w'], 'm_conv_b': out['m_conv_b'], 'm_dt_bias': out['m_dt_bias'], 'm_a_log': out['m_a_log'], 'm_d_skip': out['m_d_skip'], 'm_ssd_norm_w': out['m_ssd_norm_w'], 'm_attn_sinks': out['m_attn_sinks'], 'm_w_out': out['m_w_out'], 'm_ln_g': out['m_ln_g'], 'm_ln_b': out['m_ln_b'], 'v_w_in': out['v_w_in'], 'v_conv_w': out['v_conv_w'], 'v_conv_b': out['v_conv_b'], 'v_dt_bias': out['v_dt_bias'], 'v_a_log': out['v_a_log'], 'v_d_skip': out['v_d_skip'], 'v_ssd_norm_w': out['v_ssd_norm_w'], 'v_attn_sinks': out['v_attn_sinks'], 'v_w_out': out['v_w_out'], 'v_ln_g': out['v_ln_g'], 'v_ln_b': out['v_ln_b']}


def _loss(weights, diff, rest, loss_target):
    with _jax.named_scope("forward"):
        args = {**rest, TWIN_DIFF_INPUT: diff, **{k: w.astype(_WEIGHT_DTYPES[k]) for k, w in weights.items()}}
        y = _forward(args)
    with _jax.named_scope("loss_head"):
        err = _jnp.square(y.astype(_jnp.float32) - loss_target)
        return 0.5 * _jnp.sum(_jnp.mean(err, axis=-1)) if err.ndim else 0.5 * err


def _adamw(w, g, m, v):
    m = ADAM_B1 * m + (1.0 - ADAM_B1) * g
    v = ADAM_B2 * v + (1.0 - ADAM_B2) * _jnp.square(g)
    m_hat = m / (1.0 - ADAM_B1 ** ADAM_STEP)
    v_hat = v / (1.0 - ADAM_B2 ** ADAM_STEP)
    delta = -ADAM_LR * (m_hat / (_jnp.sqrt(v_hat) + ADAM_EPS) + ADAM_WD * w)
    return delta, m, v


def reference(x, positions, w_in, conv_w, conv_b, dt_bias, a_log, d_skip, ssd_norm_w, attn_sinks, w_out, ln_g, ln_b, loss_target, m_w_in, m_conv_w, m_conv_b, m_dt_bias, m_a_log, m_d_skip, m_ssd_norm_w, m_attn_sinks, m_w_out, m_ln_g, m_ln_b, v_w_in, v_conv_w, v_conv_b, v_dt_bias, v_a_log, v_d_skip, v_ssd_norm_w, v_attn_sinks, v_w_out, v_ln_g, v_ln_b):
    given = dict(x=x, positions=positions, w_in=w_in, conv_w=conv_w, conv_b=conv_b, dt_bias=dt_bias, a_log=a_log, d_skip=d_skip, ssd_norm_w=ssd_norm_w, attn_sinks=attn_sinks, w_out=w_out, ln_g=ln_g, ln_b=ln_b, loss_target=loss_target, m_w_in=m_w_in, m_conv_w=m_conv_w, m_conv_b=m_conv_b, m_dt_bias=m_dt_bias, m_a_log=m_a_log, m_d_skip=m_d_skip, m_ssd_norm_w=m_ssd_norm_w, m_attn_sinks=m_attn_sinks, m_w_out=m_w_out, m_ln_g=m_ln_g, m_ln_b=m_ln_b, v_w_in=v_w_in, v_conv_w=v_conv_w, v_conv_b=v_conv_b, v_dt_bias=v_dt_bias, v_a_log=v_a_log, v_d_skip=v_d_skip, v_ssd_norm_w=v_ssd_norm_w, v_attn_sinks=v_attn_sinks, v_w_out=v_w_out, v_ln_g=v_ln_g, v_ln_b=v_ln_b)
    weights = {n: given[n] for n in TWIN_WEIGHTS}
    shared = {n: given[n] for n in SHARED_INPUTS}
    per_example = {n: given[n] for n in ['x', 'positions']}
    grad_fn = _jax.value_and_grad(_loss, argnums=(0, 1))

    def one_microbatch(ex, loss_target):
        ex = dict(ex)
        diff = ex.pop(TWIN_DIFF_INPUT)
        return grad_fn(weights, diff, {**shared, **ex}, loss_target)

    if N_MICROBATCH == 1:
        loss, (grad_w, grad_x) = one_microbatch(per_example, given["loss_target"])
    else:
        def body(carry, xs):
            loss_sum, grad_sum = carry
            l_k, (gw_k, gx_k) = one_microbatch(xs[0], xs[1])
            with _jax.named_scope("update"):
                return (loss_sum + l_k, _jax.tree.map(_jnp.add, grad_sum, gw_k)), gx_k

        init = (_jnp.zeros((), _jnp.float32), _jax.tree.map(_jnp.zeros_like, weights))
        (loss, grad_w), grad_x = _jax.lax.scan(body, init, (per_example, given["loss_target"]))
    with _jax.named_scope("update"):
        delta_w, new_m, new_v = {}, {}, {}
        for n in TWIN_WEIGHTS:
            delta_w[n], new_m[n], new_v[n] = _adamw(weights[n], grad_w[n], given["m_" + n], given["v_" + n])
    return (loss, grad_x, *[grad_w[n] for n in TWIN_WEIGHTS], *[delta_w[n] for n in TWIN_WEIGHTS],
            *[new_m[n] for n in TWIN_WEIGHTS], *[new_v[n] for n in TWIN_WEIGHTS])
```

```python
import jax
import jax.numpy as jnp
from jax import lax
from jax.experimental import pallas as pl
from jax.experimental.pallas import tpu as pltpu

F32 = jnp.float32
BF16 = jnp.bfloat16
_MXU = jnp.bfloat16

N_DEV = 8
D_MODEL = 1024
D_SSD = 1024
D_ATT = 1024
HEAD_DIM = 64
N_HEADS = 16
SSD_GROUPS = 2
KV_HEADS = 4
CHUNK = 128
D_XBC = 1536
D_IN_PROJ = 5136
ROPE_DIM = 16
ROPE_THETA = 500000.0
ALPHA = (2.0 * 1) ** 0.25
LN_EPS = 1e-5
RMS_EPS = 1e-5
ATT_SCALE = HEAD_DIM ** -0.5
NEG = -1e30

S_Z, S_XS, S_B, S_C, S_DT, S_W = 0, 1024, 2048, 2304, 2560, 2816
N_SSD_REAL = 2576
A_Q, A_K, A_V, A_G, A_W = 0, 1024, 1280, 1536, 2560

ADAM_LR = 0.001
ADAM_B1 = 0.9
ADAM_B2 = 0.999
ADAM_EPS = 1e-08
ADAM_WD = 0.01
ADAM_STEP = 10

P_LOSS, P_CONVB, P_DTB, P_ALOG, P_DSKIP, P_SINK, P_NORMW, P_LNG, P_LNB, P_CONVW, P_END = (
    0, 128, 1664, 1792, 1920, 2048, 2176, 3200, 4224, 5248, 11392)
P_ROWS = 96

VMEM_LIMIT = 48 * 1024 * 1024
MESH = pl.DeviceIdType.MESH


def _params(sem=None):
    return pltpu.CompilerParams(dimension_semantics=sem, vmem_limit_bytes=VMEM_LIMIT)


def _mm(a, b):
    return jnp.dot(a.astype(_MXU), b.astype(_MXU), preferred_element_type=F32)


def _mm_nt(a, b):
    return lax.dot_general(a.astype(_MXU), b.astype(_MXU), (((1,), (1,)), ((), ())),
                           preferred_element_type=F32)


def _split3(v):
    hi = v.astype(BF16)
    r = v - hi.astype(F32)
    mid = r.astype(BF16)
    lo = (r - mid.astype(F32)).astype(BF16)
    return hi, mid, lo


def _mm_exact_r(v, p01):
    hi, mid, lo = _split3(v)
    d = lambda a: jnp.dot(a, p01, preferred_element_type=F32)
    return d(hi) + d(mid) + d(lo)


def _mm_exact_l(p01, v):
    hi, mid, lo = _split3(v)
    d = lambda a: jnp.dot(p01, a, preferred_element_type=F32)
    return d(hi) + d(mid) + d(lo)


def _sigmoid(x):
    return 1.0 / (1.0 + jnp.exp(-x))


def _softplus(x):
    e = jnp.exp(-jnp.abs(x))
    u = 1.0 + e
    log1p = jnp.where(u == 1.0, e, jnp.log(u) * (e / (u - 1.0)))
    return jnp.maximum(x, 0.0) + log1p


def _rows8(rows):
    n = rows[0].shape[1]
    rid = lax.broadcasted_iota(jnp.int32, (8, n), 0)
    out = jnp.zeros((8, n), F32)
    for k, r in enumerate(rows):
        out = out + jnp.where(rid == k, r, 0.0)
    return out


def _colsum(a):
    return jnp.sum(a, axis=0, keepdims=True)


def _matmul(a, b, *, tm, tn, name, addend=None, scale=1.0):
    M, K = a.shape
    N = b.shape[1]

    def body(*refs):
        if addend is None:
            a_ref, b_ref, o_ref = refs
        else:
            a_ref, b_ref, c_ref, o_ref = refs
        acc = _mm(a_ref[...], b_ref[...])
        if addend is not None:
            acc = acc + scale * c_ref[...]
        o_ref[...] = acc

    in_specs = [pl.BlockSpec((tm, K), lambda i, j: (i, 0)), pl.BlockSpec((K, tn), lambda i, j: (0, j))]
    args = [a, b]
    if addend is not None:
        in_specs.append(pl.BlockSpec((tm, tn), lambda i, j: (i, j)))
        args.append(addend)
    return pl.pallas_call(
        body, name=name, grid=(M // tm, N // tn), in_specs=in_specs,
        out_specs=pl.BlockSpec((tm, tn), lambda i, j: (i, j)),
        out_shape=jax.ShapeDtypeStruct((M, N), F32),
        compiler_params=_params(("arbitrary", "arbitrary")),
    )(*args)


def _matmul_tn(a, g, *, tl, tn, name):
    L, M = a.shape
    N = g.shape[1]

    def body(a_ref, g_ref, o_ref):
        @pl.when(pl.program_id(1) == 0)
        def _():
            o_ref[...] = jnp.zeros_like(o_ref)

        o_ref[...] += lax.dot_general(a_ref[...].astype(_MXU), g_ref[...].astype(_MXU),
                                      (((0,), (0,)), ((), ())), preferred_element_type=F32)

    return pl.pallas_call(
        body, name=name, grid=(N // tn, L // tl),
        in_specs=[pl.BlockSpec((tl, M), lambda j, l: (l, 0)), pl.BlockSpec((tl, tn), lambda j, l: (l, j))],
        out_specs=pl.BlockSpec((M, tn), lambda j, l: (0, j)),
        out_shape=jax.ShapeDtypeStruct((M, N), F32),
        compiler_params=_params(("arbitrary", "arbitrary")),
    )(a, g)


def _ssd_recompute(first, p_ref, halo_ref, cw_ref, cb_ref, dtb_ref, alog_ref, e_ref, ext_scr):
    ext_scr[0:8, :] = jnp.where(first, 0.0, halo_ref[:, S_XS:S_DT])
    ext_scr[8:136, :] = p_ref[:, S_XS:S_DT]
    cw = cw_ref[...]
    pre = (cb_ref[0:1, :] + cw[3:4, :] * ext_scr[8:136, :] + cw[2:3, :] * ext_scr[7:135, :]
           + cw[1:2, :] * ext_scr[6:134, :] + cw[0:1, :] * ext_scr[5:133, :])
    sg = _sigmoid(pre)
    act = pre * sg
    lane = lax.broadcasted_iota(jnp.int32, (1, 128), 1)
    A = jnp.where(lane < N_HEADS, -jnp.exp(alog_ref[0:1, :]), 0.0)
    raw = p_ref[:, S_DT:S_DT + 128] + dtb_ref[0:1, :]
    dt = _softplus(raw)
    dA = dt * A
    row = lax.broadcasted_iota(jnp.int32, (128, 128), 0)
    col = lax.broadcasted_iota(jnp.int32, (128, 128), 1)
    tril = (row >= col).astype(BF16)
    acs = _mm_exact_l(tril, dA)
    last = acs[127:128, :]
    ds = jnp.exp(last - acs)
    eo = jnp.exp(acs)
    E = e_ref[...]
    ex = _mm_exact_r(jnp.concatenate([dt, ds, eo], axis=0), E)
    dt_e, ds_e, eo_e = ex[0:128], ex[128:256], ex[256:384]
    xs_c = act[:, 0:1024]
    X = xs_c * dt_e
    return dict(pre=pre, sg=sg, xs_c=xs_c, Bc=act[:, 1024:1280], Cc=act[:, 1280:1536], A=A, raw=raw, dt=dt,
                acs=acs, acsT=acs.T, eo_e=eo_e, ds_e=ds_e, dt_e=dt_e, cd_e=eo_e[127:128, :],
                X=X, Xd=X * ds_e, row=row, col=col)


def _ssd_core(R, hprev):
    causal = R["row"] >= R["col"]
    lo_half = R["col"] < HEAD_DIM
    acs, acsT, X = R["acs"], R["acsT"], R["X"]
    ydiag, yoff, snew, cbs = [], [], [], []
    for g in range(SSD_GROUPS):
        Bg = R["Bc"][:, g * 128:(g + 1) * 128]
        Cg = R["Cc"][:, g * 128:(g + 1) * 128]
        cols = slice(g * 512, (g + 1) * 512)
        CB = _mm_nt(Cg, Bg)
        cbs.append(CB)
        snew.append(_mm(Bg.T, R["Xd"][:, cols]))
        yoff.append(_mm(Cg, hprev[:, cols]))
        for j in range(4):
            h0 = g * 8 + 2 * j
            Xp = X[:, h0 * HEAD_DIM:h0 * HEAD_DIM + 128]
            ys = []
            for h in (h0, h0 + 1):
                seg = acs[:, h:h + 1] - acsT[h:h + 1, :]
                Lm = jnp.exp(jnp.where(causal, seg, NEG))
                ys.append(_mm(CB * Lm, Xp))
            ydiag.append(jnp.where(lo_half, ys[0], ys[1]))
    Yoff = jnp.concatenate(yoff, axis=1) * R["eo_e"]
    Y = jnp.concatenate(ydiag, axis=1) + Yoff
    return Y, Yoff, jnp.concatenate(snew, axis=1), cbs


def _ssd_forward(proj_ssd, conv_w8, conv_b8, dtb8, alog8, dskip_e, norm_w, E):
    L = proj_ssd.shape[0]
    nc = L // CHUNK

    def body(p_ref, halo_ref, cw_ref, cb_ref, dtb_ref, alog_ref, dsk_ref, nw_ref, e_ref,
             y_ref, hprev_ref, h_scr, ext_scr):
        c = pl.program_id(0)
        first = c == 0

        @pl.when(first)
        def _():
            h_scr[...] = jnp.zeros_like(h_scr)

        R = _ssd_recompute(first, p_ref, halo_ref, cw_ref, cb_ref, dtb_ref, alog_ref, e_ref, ext_scr)
        hprev = h_scr[...]
        hprev_ref[...] = hprev
        Y, _, snew, _ = _ssd_core(R, hprev)
        h_scr[...] = hprev * R["cd_e"] + snew
        Y = Y + dsk_ref[0:1, :] * R["xs_c"]
        z = p_ref[:, S_Z:S_Z + 1024]
        yf = Y * (z * _sigmoid(z))
        outs = []
        for g in range(SSD_GROUPS):
            yg = yf[:, g * 512:(g + 1) * 512]
            r = lax.rsqrt(jnp.mean(yg * yg, axis=-1, keepdims=True) + RMS_EPS)
            outs.append(yg * r)
        y_ref[...] = (jnp.concatenate(outs, axis=1) * nw_ref[0:1, :]).astype(y_ref.dtype)

    const = lambda shape: pl.BlockSpec(shape, lambda c: (0, 0))
    return pl.pallas_call(
        body, name="ssd_fwd", grid=(nc,),
        in_specs=[pl.BlockSpec((CHUNK, S_W), lambda c: (c, 0)),
                  pl.BlockSpec((8, S_W), lambda c: (jnp.maximum(c * 16 - 1, 0), 0)),
                  const((8, D_XBC)), const((8, D_XBC)), const((8, 128)), const((8, 128)),
                  const((8, 1024)), const((8, 1024)), const((128, 1024))],
        out_specs=[pl.BlockSpec((CHUNK, D_SSD), lambda c: (c, 0)), pl.BlockSpec((128, 1024), lambda c: (c, 0))],
        out_shape=[jax.ShapeDtypeStruct((L, D_SSD), _MXU), jax.ShapeDtypeStruct((nc * 128, 1024), F32)],
        scratch_shapes=[pltpu.VMEM((128, 1024), F32), pltpu.VMEM((136, D_XBC), F32)],
        compiler_params=_params(("arbitrary",)),
    )(proj_ssd, proj_ssd, conv_w8, conv_b8, dtb8, alog8, dskip_e, norm_w, E)


def _ssd_backward(proj_ssd, hprev_all, dy, conv_w8, conv_b8, dtb8, alog8, dskip_e, norm_w, E, ET):
    L = proj_ssd.shape[0]
    nc = L // CHUNK

    def body(p_ref, halo_ref, hprev_ref, dy_ref, cw_ref, cb_ref, dtb_ref, alog_ref, dsk_ref, nw_ref, e_ref, et_ref,
             dp_ref, acc_cw_ref, acc_w_ref, acc_s_ref, dh_scr, ext_scr, ext2_scr, nxt_scr):
        i = pl.program_id(0)
        c = nc - 1 - i
        first = c == 0

        @pl.when(i == 0)
        def _():
            dh_scr[...] = jnp.zeros_like(dh_scr)
            nxt_scr[...] = jnp.zeros_like(nxt_scr)
            acc_cw_ref[...] = jnp.zeros_like(acc_cw_ref)
            acc_w_ref[...] = jnp.zeros_like(acc_w_ref)
            acc_s_ref[...] = jnp.zeros_like(acc_s_ref)

        R = _ssd_recompute(first, p_ref, halo_ref, cw_ref, cb_ref, dtb_ref, alog_ref, e_ref, ext_scr)
        hprev = hprev_ref[...]
        Y0, Yoff, _, cbs = _ssd_core(R, hprev)
        xs_c, X, Xd = R["xs_c"], R["X"], R["Xd"]
        acs, acsT = R["acs"], R["acsT"]
        ET = et_ref[...]
        dsk = dsk_ref[0:1, :]
        Y = Y0 + dsk * xs_c

        z = p_ref[:, S_Z:S_Z + 1024]
        sz = _sigmoid(z)
        silz = z * sz
        yf = Y * silz
        dyv = dy_ref[...]
        nw = nw_ref[0:1, :]
        dyf_parts, dnw_parts = [], []
        for g in range(SSD_GROUPS):
            cols = slice(g * 512, (g + 1) * 512)
            yg = yf[:, cols]
            r = lax.rsqrt(jnp.mean(yg * yg, axis=-1, keepdims=True) + RMS_EPS)
            yn = yg * r
            dyn = dyv[:, cols] * nw[:, cols]
            dnw_parts.append(_colsum(dyv[:, cols] * yn))
            dyf_parts.append(r * (dyn - yn * jnp.mean(dyn * yn, axis=-1, keepdims=True)))
        dyf = jnp.concatenate(dyf_parts, axis=1)
        dY = dyf * silz
        dz = dyf * Y * (sz * (1.0 + z * (1.0 - sz)))

        dhn = dh_scr[...]
        dYo = dY * R["eo_e"]
        causal = R["row"] >= R["col"]
        causal_t = R["col"] >= R["row"]
        lo_half = R["col"] < HEAD_DIM
        dacs = jnp.zeros((128, 128), F32)
        dacs_t = jnp.zeros((128, 128), F32)
        dxdiag, dxd, dhprev, dBs, dCs = [], [], [], [], []
        for g in range(SSD_GROUPS):
            Bg = R["Bc"][:, g * 128:(g + 1) * 128]
            Cg = R["Cc"][:, g * 128:(g + 1) * 128]
            cols = slice(g * 512, (g + 1) * 512)
            CB = cbs[g]
            CBt = CB.T
            dCB = jnp.zeros((128, 128), F32)
            for j in range(4):
                h0 = g * 8 + 2 * j
                pc = slice(h0 * HEAD_DIM, h0 * HEAD_DIM + 128)
                Xp = X[:, pc]
                dYp = dY[:, pc]
                dxs = []
                for a, h in enumerate((h0, h0 + 1)):
                    keep = lo_half if a == 0 else jnp.logical_not(lo_half)
                    acol = acs[:, h:h + 1]
                    arow = acsT[h:h + 1, :]
                    Lm = jnp.exp(jnp.where(causal, acol - arow, NEG))
                    Lt = jnp.exp(jnp.where(causal_t, arow - acol, NEG))
                    dM = _mm_nt(jnp.where(keep, dYp, 0.0), Xp)
                    dL = dM * Lm
                    dCB = dCB + dL
                    G = dL * CB
                    dacs = dacs + jnp.where(R["col"] == h, jnp.sum(G, axis=1, keepdims=True), 0.0)
                    dacs_t = dacs_t + jnp.where(R["row"] == h, jnp.sum(G, axis=0, keepdims=True), 0.0)
                    dxs.append(_mm(CBt * Lt, dYp))
                dxdiag.append(jnp.where(lo_half, dxs[0], dxs[1]))
            dS = dhn[:, cols]
            dXd_g = _mm(Bg, dS)
            dxd.append(dXd_g)
            dhprev.append(_mm(Cg.T, dYo[:, cols]))
            dCs.append(_mm_nt(dYo[:, cols], hprev[:, cols]) + _mm(dCB, Bg))
            dBs.append(_mm(dCB.T, Cg) + _mm_nt(Xd[:, cols], dS))
        dXd = jnp.concatenate(dxd, axis=1)
        dX = jnp.concatenate(dxdiag, axis=1) + dXd * R["ds_e"]
        t_state = dXd * Xd
        hs = _mm_exact_r(jnp.concatenate([dY * Yoff - t_state, dX * xs_c], axis=0), ET)
        dacs = dacs + hs[0:128] - dacs_t.T
        v_last = _colsum(t_state + dhn * hprev * R["cd_e"])
        dlast = _mm_exact_r(jnp.broadcast_to(v_last, (8, 1024)), ET)[0:1, :]
        dacs = dacs + jnp.where(R["row"] == 127, dlast, 0.0)
        triu = (R["col"] >= R["row"]).astype(BF16)
        da = _mm_exact_l(triu, dacs)
        ddt = da * R["A"] + hs[128:256]
        ddt_raw = ddt * _sigmoid(R["raw"])
        dxs_c = dX * R["dt_e"] + dY * dsk
        dh_scr[...] = jnp.concatenate(dhprev, axis=1) + dhn * R["cd_e"]

        dact = jnp.concatenate([dxs_c] + dBs + dCs, axis=1)
        pre, sg = R["pre"], R["sg"]
        dpre = dact * (sg * (1.0 + pre * (1.0 - sg)))
        ext2_scr[0:128, :] = dpre
        ext2_scr[128:136, :] = nxt_scr[...]
        nxt_scr[...] = dpre[0:8, :]
        cw = cw_ref[...]
        dxbc = (cw[3:4, :] * dpre + cw[2:3, :] * ext2_scr[1:129, :] + cw[1:2, :] * ext2_scr[2:130, :]
                + cw[0:1, :] * ext2_scr[3:131, :])
        acc_cw_ref[...] += _rows8([_colsum(dpre * ext_scr[5 + k:133 + k, :]) for k in range(4)] + [_colsum(dpre)])
        acc_w_ref[...] += _rows8([jnp.concatenate(dnw_parts, axis=1), _colsum(dY * xs_c)])
        acc_s_ref[...] += _rows8([_colsum(ddt_raw), _colsum(da * R["dt"])])

        lane = lax.broadcasted_iota(jnp.int32, (128, 128), 1)
        dp_ref[:, S_Z:S_Z + 1024] = dz
        dp_ref[:, S_XS:S_DT] = dxbc
        dp_ref[:, S_DT:S_DT + 128] = jnp.where(lane < N_HEADS, ddt_raw, 0.0)
        dp_ref[:, S_DT + 128:S_W] = jnp.zeros((128, 128), F32)

        @pl.when(i == nc - 1)
        def _():
            acc = acc_s_ref[...]
            dskip = _mm_exact_r(acc_w_ref[...], ET)[1:2, :]
            acc_s_ref[...] = _rows8([acc[0:1, :], acc[1:2, :] * R["A"], dskip])

    const = lambda shape: pl.BlockSpec(shape, lambda i: (0, 0))
    rev = lambda i: (nc - 1 - i, 0)
    return pl.pallas_call(
        body, name="ssd_bwd", grid=(nc,),
        in_specs=[pl.BlockSpec((CHUNK, S_W), rev),
                  pl.BlockSpec((8, S_W), lambda i: (jnp.maximum((nc - 1 - i) * 16 - 1, 0), 0)),
                  pl.BlockSpec((128, 1024), rev),
                  pl.BlockSpec((CHUNK, D_SSD), rev),
                  const((8, D_XBC)), const((8, D_XBC)), const((8, 128)), const((8, 128)),
                  const((8, 1024)), const((8, 1024)), const((128, 1024)), const((1024, 128))],
        out_specs=[pl.BlockSpec((CHUNK, S_W), rev), const((8, D_XBC)), const((8, 1024)), const((8, 128))],
        out_shape=[jax.ShapeDtypeStruct((L, S_W), F32), jax.ShapeDtypeStruct((8, D_XBC), F32),
                   jax.ShapeDtypeStruct((8, 1024), F32), jax.ShapeDtypeStruct((8, 128), F32)],
        scratch_shapes=[pltpu.VMEM((128, 1024), F32), pltpu.VMEM((136, D_XBC), F32),
                        pltpu.VMEM((136, D_XBC), F32), pltpu.VMEM((8, D_XBC), F32)],
        compiler_params=_params(("arbitrary",)),
    )(proj_ssd, proj_ssd, hprev_all, dy, conv_w8, conv_b8, dtb8, alog8, dskip_e, norm_w, E, ET)


def _rope(t, tab):
    cos, sa, sb = tab[:, 0:128], tab[:, 128:256], tab[:, 256:384]
    outs = []
    for i in range(t.shape[1] // 128):
        tg = t[:, i * 128:(i + 1) * 128]
        outs.append(tg * cos + pltpu.roll(tg, 8, 1) * sa + pltpu.roll(tg, 120, 1) * sb)
    return jnp.concatenate(outs, axis=1)


def _rope_transposed(d, tab):
    cos, sa, sb = tab[:, 0:128], tab[:, 128:256], tab[:, 256:384]
    outs = []
    for i in range(d.shape[1] // 128):
        dg = d[:, i * 128:(i + 1) * 128]
        outs.append(dg * cos + pltpu.roll(dg * sa, 120, 1) + pltpu.roll(dg * sb, 8, 1))
    return jnp.concatenate(outs, axis=1)


def _half_masks():
    lane = lax.broadcasted_iota(jnp.int32, (2 * CHUNK, 128), 1)
    return (lane < HEAD_DIM, lane >= HEAD_DIM)


def _kv_placed(kk, vv, j):
    p, b = j // 2, j % 2
    keep = _half_masks()[b]
    k_nat = jnp.where(keep, kk[:, p * 128:(p + 1) * 128], 0.0)
    v_nat = jnp.where(keep, vv[:, p * 128:(p + 1) * 128], 0.0)
    k_sw = pltpu.roll(k_nat, HEAD_DIM, 1)
    v_sw = pltpu.roll(v_nat, HEAD_DIM, 1)
    return ((k_nat, v_nat), (k_sw, v_sw)) if b == 0 else ((k_sw, v_sw), (k_nat, v_nat))


def _band_mask(blk):
    qi = lax.broadcasted_iota(jnp.int32, (CHUNK, 2 * CHUNK), 0)
    si = lax.broadcasted_iota(jnp.int32, (CHUNK, 2 * CHUNK), 1)
    return (si > qi) & (si <= qi + CHUNK) & ((blk > 0) | (si >= CHUNK))


def _softmax_sink(s, valid, sink):
    s = jnp.where(valid, s, NEG)
    mx = jnp.maximum(jnp.max(s, axis=-1, keepdims=True), sink)
    p = jnp.exp(s - mx)
    esink = jnp.exp(sink - mx)
    den = jnp.sum(p, axis=-1, keepdims=True) + esink
    return p / den, esink / den


def _swa_forward(proj_att, tabs, sinks):
    L = proj_att.shape[0]
    nb = L // CHUNK

    def body(sink_ref, p_ref, prev_ref, tab_ref, ptab_ref, y_ref):
        n = pl.program_id(0)
        tab = tab_ref[...]
        qr = _rope(p_ref[:, A_Q:A_Q + 1024], tab)
        k_cur = _rope(p_ref[:, A_K:A_K + 256], tab)
        k_prev = _rope(prev_ref[:, 0:256], ptab_ref[...])
        kk = jnp.concatenate([k_prev, k_cur], axis=0)
        vv = jnp.concatenate([prev_ref[:, 256:512], p_ref[:, A_V:A_V + 256]], axis=0)
        valid = _band_mask(n)
        o_pairs = [jnp.zeros((CHUNK, 128), F32) for _ in range(8)]
        for j in range(KV_HEADS):
            placed = _kv_placed(kk, vv, j)
            for hh in range(4):
                h = 4 * j + hh
                m_, a = h // 2, h % 2
                kkm, vvm = placed[a]
                s = _mm_nt(qr[:, m_ * 128:(m_ + 1) * 128], kkm) * ATT_SCALE
                P, _ = _softmax_sink(s, valid, sink_ref[h])
                o_pairs[m_] = o_pairs[m_] + _mm(P, vvm)
        g = p_ref[:, A_G:A_G + 1024]
        y_ref[...] = (jnp.concatenate(o_pairs, axis=1) * (g * _sigmoid(g))).astype(y_ref.dtype)

    return pl.pallas_call(
        body, name="swa_fwd", grid=(nb,),
        in_specs=[pl.BlockSpec(memory_space=pltpu.SMEM),
                  pl.BlockSpec((CHUNK, A_W), lambda n: (n, 0)),
                  pl.BlockSpec((CHUNK, 512), lambda n: (jnp.maximum(n - 1, 0), 2)),
                  pl.BlockSpec((CHUNK, 384), lambda n: (n, 0)),
                  pl.BlockSpec((CHUNK, 384), lambda n: (jnp.maximum(n - 1, 0), 0))],
        out_specs=pl.BlockSpec((CHUNK, D_ATT), lambda n: (n, 0)),
        out_shape=jax.ShapeDtypeStruct((L, D_ATT), _MXU),
        compiler_params=_params(("arbitrary",)),
    )(sinks, proj_att, proj_att, tabs, tabs)


def _swa_backward(proj_att, tabs, sinks, dy):
    L = proj_att.shape[0]
    nb = L // CHUNK

    def body(sink_ref, p_ref, prev_ref, tab_ref, ptab_ref, dy_ref, dp_ref, dsink_ref, carry_k, carry_v):
        i = pl.program_id(0)
        n = nb - 1 - i

        @pl.when(i == 0)
        def _():
            carry_k[...] = jnp.zeros_like(carry_k)
            carry_v[...] = jnp.zeros_like(carry_v)
            dsink_ref[...] = jnp.zeros_like(dsink_ref)

        tab = tab_ref[...]
        qr = _rope(p_ref[:, A_Q:A_Q + 1024], tab)
        k_cur = _rope(p_ref[:, A_K:A_K + 256], tab)
        k_prev = _rope(prev_ref[:, 0:256], ptab_ref[...])
        kk = jnp.concatenate([k_prev, k_cur], axis=0)
        vv = jnp.concatenate([prev_ref[:, 256:512], p_ref[:, A_V:A_V + 256]], axis=0)
        valid = _band_mask(n)
        g = p_ref[:, A_G:A_G + 1024]
        sgm = _sigmoid(g)
        dyv = dy_ref[...]
        do_all = dyv * (g * sgm)
        halves = _half_masks()
        lane8 = lax.broadcasted_iota(jnp.int32, (8, 128), 1)
        o_pairs = [jnp.zeros((CHUNK, 128), F32) for _ in range(8)]
        dq_pairs = [jnp.zeros((CHUNK, 128), F32) for _ in range(8)]
        dk_nat = [jnp.zeros((2 * CHUNK, 128), F32) for _ in range(2)]
        dv_nat = [jnp.zeros((2 * CHUNK, 128), F32) for _ in range(2)]
        dsink = jnp.zeros((8, 128), F32)
        for j in range(KV_HEADS):
            placed = _kv_placed(kk, vv, j)
            pj, b = j // 2, j % 2
            for hh in range(4):
                h = 4 * j + hh
                m_, a = h // 2, h % 2
                kkm, vvm = placed[a]
                qp = qr[:, m_ * 128:(m_ + 1) * 128]
                s = _mm_nt(qp, kkm) * ATT_SCALE
                P, psink = _softmax_sink(s, valid, sink_ref[h])
                o_h = _mm(P, vvm)
                o_pairs[m_] = o_pairs[m_] + o_h
                do_p = do_all[:, m_ * 128:(m_ + 1) * 128]
                D = jnp.sum(do_p * o_h, axis=-1, keepdims=True)
                dS = P * (_mm_nt(do_p, vvm) - D)
                dsink = dsink + jnp.where(lane8 == h, -jnp.sum(psink * D), 0.0)
                dq_pairs[m_] = dq_pairs[m_] + _mm(dS, kkm) * ATT_SCALE
                dk_a = jnp.where(halves[a], _mm(dS.T, qp) * ATT_SCALE, 0.0)
                dv_a = jnp.where(halves[a], _mm(P.T, do_p), 0.0)
                if a != b:
                    dk_a = pltpu.roll(dk_a, HEAD_DIM, 1)
                    dv_a = pltpu.roll(dv_a, HEAD_DIM, 1)
                dk_nat[pj] = dk_nat[pj] + dk_a
                dv_nat[pj] = dv_nat[pj] + dv_a
        o = jnp.concatenate(o_pairs, axis=1)
        dkk = jnp.concatenate(dk_nat, axis=1)
        dvv = jnp.concatenate(dv_nat, axis=1)
        dp_ref[:, A_Q:A_Q + 1024] = _rope_transposed(jnp.concatenate(dq_pairs, axis=1), tab)
        dp_ref[:, A_K:A_K + 256] = _rope_transposed(dkk[CHUNK:2 * CHUNK] + carry_k[...], tab)
        dp_ref[:, A_V:A_V + 256] = dvv[CHUNK:2 * CHUNK] + carry_v[...]
        dp_ref[:, A_G:A_G + 1024] = dyv * o * (sgm * (1.0 + g * (1.0 - sgm)))
        carry_k[...] = dkk[0:CHUNK]
        carry_v[...] = dvv[0:CHUNK]
        dsink_ref[...] += dsink

    rev = lambda i: (nb - 1 - i, 0)
    prev = lambda i: jnp.maximum(nb - 2 - i, 0)
    return pl.pallas_call(
        body, name="swa_bwd", grid=(nb,),
        in_specs=[pl.BlockSpec(memory_space=pltpu.SMEM),
                  pl.BlockSpec((CHUNK, A_W), rev),
                  pl.BlockSpec((CHUNK, 512), lambda i: (prev(i), 2)),
                  pl.BlockSpec((CHUNK, 384), rev),
                  pl.BlockSpec((CHUNK, 384), lambda i: (prev(i), 0)),
                  pl.BlockSpec((CHUNK, D_ATT), lambda i: (nb - 1 - i, 1))],
        out_specs=[pl.BlockSpec((CHUNK, A_W), rev), pl.BlockSpec((8, 128), lambda i: (0, 0))],
        out_shape=[jax.ShapeDtypeStruct((L, A_W), F32), jax.ShapeDtypeStruct((8, 128), F32)],
        scratch_shapes=[pltpu.VMEM((CHUNK, 256), F32), pltpu.VMEM((CHUNK, 256), F32)],
        compiler_params=_params(("arbitrary",)),
    )(sinks, proj_att, proj_att, tabs, tabs, dy)


def _head(y_ssd, y_att, x, target, w_out, w_out_t, ln_g8, ln_b8, *, tm):
    L = x.shape[0]
    nsteps = L // tm

    def body(ys_ref, ya_ref, x_ref, t_ref, wo_ref, wot_ref, g_ref, b_ref, dr_ref, dy_ref, acc_ref):
        i = pl.program_id(0)

        @pl.when(i == 0)
        def _():
            acc_ref[...] = jnp.zeros_like(acc_ref)

        h = _mm(ys_ref[...], wo_ref[0:1024, :]) + _mm(ya_ref[...], wo_ref[1024:2048, :])
        r = ALPHA * x_ref[...] + h
        mu = jnp.mean(r, axis=-1, keepdims=True)
        d = r - mu
        rstd = lax.rsqrt(jnp.mean(d * d, axis=-1, keepdims=True) + LN_EPS)
        xh = d * rstd
        gam = g_ref[0:1, :]
        e = xh * gam + b_ref[0:1, :] - t_ref[...]
        dout = e * (1.0 / D_MODEL)
        dxh = dout * gam
        dr = rstd * (dxh - jnp.mean(dxh, axis=-1, keepdims=True)
                     - xh * jnp.mean(dxh * xh, axis=-1, keepdims=True))
        dr_ref[...] = dr
        dy_ref[...] = _mm(dr, wot_ref[...])
        acc_ref[...] += _rows8([_colsum(dout * xh), _colsum(dout), _colsum(e * e) * (0.5 / D_MODEL)])

        @pl.when(i == nsteps - 1)
        def _():
            acc = acc_ref[...]
            tot = jnp.sum(acc[2:3, :])
            rid = lax.broadcasted_iota(jnp.int32, (8, 1024), 0)
            acc_ref[...] = jnp.where(rid == 3, tot, acc)

    const = lambda shape: pl.BlockSpec(shape, lambda i: (0, 0))
    row = lambda w: pl.BlockSpec((tm, w), lambda i: (i, 0))
    return pl.pallas_call(
        body, name="head", grid=(nsteps,),
        in_specs=[row(1024), row(1024), row(1024), row(1024), const((2048, 1024)), const((1024, 2048)),
                  const((8, 1024)), const((8, 1024))],
        out_specs=[row(1024), row(2048), const((8, 1024))],
        out_shape=[jax.ShapeDtypeStruct((L, D_MODEL), F32), jax.ShapeDtypeStruct((L, 2048), F32),
                   jax.ShapeDtypeStruct((8, 1024), F32)],
        compiler_params=_params(("arbitrary",)),
    )(y_ssd, y_att, x, target, w_out, w_out_t, ln_g8, ln_b8)


def _position():
    return lax.axis_index("x"), lax.axis_index("y"), lax.axis_index("c")


def _index(px, py, pc):
    return 4 * px + 2 * py + pc


def _all_gather(shards):
    n = len(shards)
    any_spec = pl.BlockSpec(memory_space=pl.ANY)

    def body(*refs):
        ins, outs = refs[:n], refs[n:2 * n]
        send_sems, recv_sems, local_sems = refs[2 * n:]
        x, y, c = _position()
        me, sibling = (x, y, c), (x, y, 1 - c)
        chips = [(1 - x, y), (x, 1 - y), (1 - x, 1 - y)]

        def copy(a, k, block, to, src=None):
            slot = outs[a].at[_index(*block)]
            return pltpu.make_async_remote_copy(
                src_ref=slot if src is None else src, dst_ref=slot,
                send_sem=send_sems.at[a, k], recv_sem=recv_sems.at[a, k],
                device_id=to, device_id_type=MESH)

        mine = [pltpu.make_async_copy(ins[a], outs[a].at[_index(*me)], local_sems.at[a]) for a in range(n)]
        for cp in mine:
            cp.start()
        first = []
        for a in range(n):
            first.append(copy(a, 0, me, sibling, src=ins[a]))
            first += [copy(a, 1 + j, me, (*chip, c), src=ins[a]) for j, chip in enumerate(chips)]
        for cp in first:
            cp.start()
        passed = []
        for j, chip in enumerate(chips):
            for a in range(n):
                copy(a, 1 + j, (*chip, c), me).wait_recv()
                fwd = copy(a, 4 + j, (*chip, c), sibling)
                fwd.start()
                passed.append(fwd)
        for a in range(n):
            copy(a, 0, sibling, me).wait_recv()
            for j, chip in enumerate(chips):
                copy(a, 4 + j, (*chip, 1 - c), me).wait_recv()
        for cp in first + passed:
            cp.wait_send()
        for cp in mine:
            cp.wait()

    return pl.pallas_call(
        body, name="weight_all_gather",
        in_specs=[any_spec] * n, out_specs=[any_spec] * n,
        out_shape=[jax.ShapeDtypeStruct((N_DEV,) + s.shape, s.dtype) for s in shards],
        scratch_shapes=[pltpu.SemaphoreType.DMA((n, 7)), pltpu.SemaphoreType.DMA((n, 7)),
                        pltpu.SemaphoreType.DMA((n,))],
    )(*shards)


def _grad_exchange(parts, small):
    n = len(parts)
    any_spec = pl.BlockSpec(memory_space=pl.ANY)

    def body(*refs):
        ins, small_ref = refs[:n], refs[n]
        outs, small_out = refs[n + 1:2 * n + 1], refs[2 * n + 1]
        send_sems, recv_sems, local_sem = refs[2 * n + 2:]
        x, y, c = _position()
        me = _index(x, y, c)
        peers = []
        for k in range(1, N_DEV):
            fx, fy, fc = (k >> 2) & 1, (k >> 1) & 1, k & 1
            peers.append(((1 - x) if fx else x, (1 - y) if fy else y, (1 - c) if fc else c))

        def copies(k):
            peer = peers[k - 1]
            out = [pltpu.make_async_remote_copy(
                src_ref=ins[a].at[_index(*peer)], dst_ref=outs[a].at[k - 1],
                send_sem=send_sems.at[a, k - 1], recv_sem=recv_sems.at[a, k - 1],
                device_id=peer, device_id_type=MESH) for a in range(n)]
            out.append(pltpu.make_async_remote_copy(
                src_ref=small_ref, dst_ref=small_out.at[me],
                send_sem=send_sems.at[n, k - 1], recv_sem=recv_sems.at[n, k - 1],
                device_id=peer, device_id_type=MESH))
            return out

        def arrivals(k):
            peer = peers[k - 1]
            out = [pltpu.make_async_remote_copy(
                src_ref=ins[a].at[me], dst_ref=outs[a].at[k - 1],
                send_sem=send_sems.at[a, k - 1], recv_sem=recv_sems.at[a, k - 1],
                device_id=peer, device_id_type=MESH) for a in range(n)]
            out.append(pltpu.make_async_remote_copy(
                src_ref=small_ref, dst_ref=small_out.at[_index(*peer)],
                send_sem=send_sems.at[n, k - 1], recv_sem=recv_sems.at[n, k - 1],
                device_id=peer, device_id_type=MESH))
            return out

        mine = pltpu.make_async_copy(small_ref, small_out.at[me], local_sem)
        mine.start()
        sent = []
        for k in range(1, N_DEV):
            for cp in copies(k):
                cp.start()
                sent.append(cp)
        for k in range(1, N_DEV):
            for cp in arrivals(k):
                cp.wait_recv()
        for cp in sent:
            cp.wait_send()
        mine.wait()

    return pl.pallas_call(
        body, name="grad_exchange",
        in_specs=[any_spec] * (n + 1), out_specs=[any_spec] * (n + 1),
        out_shape=[jax.ShapeDtypeStruct((N_DEV - 1,) + p.shape[1:], p.dtype) for p in parts]
        + [jax.ShapeDtypeStruct((N_DEV,) + small.shape, small.dtype)],
        scratch_shapes=[pltpu.SemaphoreType.DMA((n + 1, 7)), pltpu.SemaphoreType.DMA((n + 1, 7)),
                        pltpu.SemaphoreType.DMA],
    )(*parts, small)


def _adamw_math(w, g, m, v):
    m = ADAM_B1 * m + (1.0 - ADAM_B1) * g
    v = ADAM_B2 * v + (1.0 - ADAM_B2) * (g * g)
    m_hat = m / (1.0 - ADAM_B1 ** ADAM_STEP)
    v_hat = v / (1.0 - ADAM_B2 ** ADAM_STEP)
    delta = -ADAM_LR * (m_hat / (jnp.sqrt(v_hat) + ADAM_EPS) + ADAM_WD * w)
    return delta, m, v


def _adamw_shard(g_own, recv, w, m, v, *, rows, name):
    R, C = g_own.shape

    def body(g_ref, r_ref, w_ref, m_ref, v_ref, go_ref, d_ref, mo_ref, vo_ref):
        g = g_ref[...]
        for k in range(N_DEV - 1):
            g = g + r_ref[k].astype(F32)
        d, mn, vn = _adamw_math(w_ref[...], g, m_ref[...], v_ref[...])
        go_ref[...] = g
        d_ref[...] = d
        mo_ref[...] = mn
        vo_ref[...] = vn

    blk = pl.BlockSpec((rows, C), lambda i: (i, 0))
    return pl.pallas_call(
        body, name=name, grid=(R // rows,),
        in_specs=[blk, pl.BlockSpec((N_DEV - 1, rows, C), lambda i: (0, i, 0)), blk, blk, blk],
        out_specs=[blk] * 4, out_shape=[jax.ShapeDtypeStruct((R, C), F32)] * 4,
        compiler_params=_params(("arbitrary",)),
    )(g_own, recv, w, m, v)


def _adamw_small(gathered, w, m, v):
    def body(r_ref, w_ref, m_ref, v_ref, go_ref, d_ref, mo_ref, vo_ref):
        g = r_ref[0]
        for k in range(1, N_DEV):
            g = g + r_ref[k]
        d, mn, vn = _adamw_math(w_ref[...], g, m_ref[...], v_ref[...])
        go_ref[...] = g
        d_ref[...] = d
        mo_ref[...] = mn
        vo_ref[...] = vn

    return pl.pallas_call(
        body, name="adamw_small", out_shape=[jax.ShapeDtypeStruct(w.shape, F32)] * 4,
        compiler_params=_params(),
    )(gathered, w, m, v)


def _adamw_plain(g, w, m, v):
    def body(g_ref, w_ref, m_ref, v_ref, d_ref, mo_ref, vo_ref):
        d, mn, vn = _adamw_math(w_ref[...], g_ref[...], m_ref[...], v_ref[...])
        d_ref[...] = d
        mo_ref[...] = mn
        vo_ref[...] = vn

    return pl.pallas_call(
        body, name="adamw_conv_w", out_shape=[jax.ShapeDtypeStruct(w.shape, F32)] * 3,
        compiler_params=_params(),
    )(g, w, m, v)


def _pad_rows8(v):
    v = v.reshape(-1, v.shape[-1])
    return jnp.pad(v, ((0, 8 - v.shape[0]), (0, 0)))


def _pad_lanes(v, n):
    return jnp.pad(v, ((0, 0), (0, n - v.shape[1])))


def _pack_small(loss, conv_b, dt_bias, a_log, d_skip, sinks, norm_w, ln_g, ln_b, conv_w):
    def seg(v, n):
        v = v.reshape(-1).astype(F32)
        return jnp.pad(v, (0, n - v.shape[0]))

    flat = jnp.concatenate([seg(loss, 128), seg(conv_b, 1536), seg(dt_bias, 128), seg(a_log, 128),
                            seg(d_skip, 128), seg(sinks, 128), seg(norm_w, 1024), seg(ln_g, 1024),
                            seg(ln_b, 1024), seg(conv_w, 6144), jnp.zeros((P_ROWS * 128 - P_END,), F32)])
    return flat.reshape(P_ROWS, 128)


def _unpack_small(p):
    f = p.reshape(-1)
    return dict(conv_b=f[P_CONVB:P_CONVB + 1536].reshape(1, 1536), dt_bias=f[P_DTB:P_DTB + 16].reshape(1, 16),
                a_log=f[P_ALOG:P_ALOG + 16].reshape(1, 16), d_skip=f[P_DSKIP:P_DSKIP + 16].reshape(1, 16),
                attn_sinks=f[P_SINK:P_SINK + 16].reshape(1, 16), ssd_norm_w=f[P_NORMW:P_NORMW + 1024].reshape(1, 1024),
                ln_g=f[P_LNG:P_LNG + 1024].reshape(1, 1024), ln_b=f[P_LNB:P_LNB + 1024].reshape(1, 1024),
                conv_w=f[P_CONVW:P_CONVW + 6144].reshape(4, 1536), loss=f[P_LOSS])


def _rope_tables(positions):
    inv = ROPE_THETA ** (-jnp.arange(0, ROPE_DIM, 2, dtype=F32) / ROPE_DIM)
    ang = positions.astype(F32)[:, None] * inv
    cs, sn = jnp.cos(ang), jnp.sin(ang)
    L = positions.shape[0]
    one48, z8, z48 = jnp.ones((L, 48), F32), jnp.zeros((L, 8), F32), jnp.zeros((L, 48), F32)
    cos64 = jnp.concatenate([cs, cs, one48], axis=1)
    sina64 = jnp.concatenate([z8, sn, z48], axis=1)
    sinb64 = jnp.concatenate([-sn, z8, z48], axis=1)
    return jnp.concatenate([cos64, cos64, sina64, sina64, sinb64, sinb64], axis=1)


def _expansion():
    h = lax.broadcasted_iota(jnp.int32, (128, 1024), 0)
    col = lax.broadcasted_iota(jnp.int32, (128, 1024), 1)
    E = (col // HEAD_DIM == h).astype(BF16)
    return E, E.T


def _local_step(x, positions, target, w_ssd, w_att, w_ssd_t, w_att_t, w_out, w_out_t,
                conv_w, conv_b, dt_bias, a_log, d_skip, norm_w, sinks, ln_g, ln_b):
    E, ET = _expansion()
    conv_w8, conv_b8 = _pad_rows8(conv_w), _pad_rows8(conv_b)
    dtb8, alog8 = _pad_rows8(_pad_lanes(dt_bias, 128)), _pad_rows8(_pad_lanes(a_log, 128))
    dskip_e = _pad_rows8(jnp.repeat(d_skip, HEAD_DIM, axis=1))
    norm_w8, ln_g8, ln_b8 = _pad_rows8(norm_w), _pad_rows8(ln_g), _pad_rows8(ln_b)
    sinks1 = sinks.reshape(-1)
    tabs = _rope_tables(positions)
    xb = x.astype(_MXU)
    L = x.shape[0]
    tm = min(1024, L)

    proj_ssd = _matmul(xb, w_ssd, tm=tm, tn=S_W // 2, name="in_proj_ssd")
    proj_att = _matmul(xb, w_att, tm=tm, tn=A_W // 2, name="in_proj_att")
    y_ssd, hprev = _ssd_forward(proj_ssd, conv_w8, conv_b8, dtb8, alog8, dskip_e, norm_w8, E)
    y_att = _swa_forward(proj_att, tabs, sinks1)
    dr, dy, acc_head = _head(y_ssd, y_att, x, target, w_out, w_out_t, ln_g8, ln_b8, tm=min(256, L))
    d_ssd, acc_cw, acc_w, acc_s = _ssd_backward(proj_ssd, hprev, dy, conv_w8, conv_b8, dtb8, alog8,
                                                dskip_e, norm_w8, E, ET)
    d_att, dsink = _swa_backward(proj_att, tabs, sinks1, dy)
    tl = min(512, L)
    dw_ssd = _matmul_tn(xb, d_ssd, tl=tl, tn=S_W // 2, name="dw_in_ssd")
    dw_att = _matmul_tn(xb, d_att, tl=tl, tn=A_W // 2, name="dw_in_att")
    dw_out_s = _matmul_tn(y_ssd, dr, tl=tl, tn=1024, name="dw_out_ssd")
    dw_out_a = _matmul_tn(y_att, dr, tl=tl, tn=1024, name="dw_out_att")
    dx = _matmul(d_ssd, w_ssd_t, tm=tl, tn=1024, name="dx_ssd", addend=dr, scale=ALPHA)
    dx = _matmul(d_att, w_att_t, tm=tl, tn=1024, name="dx_att", addend=dx, scale=1.0)
    dw_in = jnp.concatenate([dw_ssd[:, :N_SSD_REAL], dw_att], axis=1)
    dw_out = jnp.concatenate([dw_out_s, dw_out_a], axis=0)
    small = _pack_small(acc_head[3, 0], acc_cw[4], acc_s[0, :16], acc_s[1, :16], acc_s[2, :16], dsink[0, :16],
                        acc_w[0], acc_head[0], acc_head[1], acc_cw[0:4])
    return dx, dw_in, dw_out, small


def _full_weights(w_in_all, w_out_all, conv_w_all):
    w_in = jnp.transpose(w_in_all, (1, 0, 2)).reshape(D_MODEL, D_IN_PROJ)
    w_ssd = jnp.pad(w_in[:, :N_SSD_REAL], ((0, 0), (0, S_W - N_SSD_REAL)))
    w_att = w_in[:, N_SSD_REAL:]
    w_out = w_out_all.reshape(2 * D_MODEL, D_MODEL)
    conv_w = jnp.transpose(conv_w_all, (1, 0, 2)).reshape(4, D_XBC)
    return w_ssd, w_att, w_ssd.T, w_att.T, w_out, w_out.T, conv_w


def kernel(x, positions, w_in, conv_w, conv_b, dt_bias, a_log, d_skip, ssd_norm_w, attn_sinks, w_out, ln_g, ln_b, loss_target, m_w_in, m_conv_w, m_conv_b, m_dt_bias, m_a_log, m_d_skip, m_ssd_norm_w, m_attn_sinks, m_w_out, m_ln_g, m_ln_b, v_w_in, v_conv_w, v_conv_b, v_dt_bias, v_a_log, v_d_skip, v_ssd_norm_w, v_attn_sinks, v_w_out, v_ln_g, v_ln_b):
    me = _index(*_position())
    w_in_all, w_out_all, conv_w_all = _all_gather([w_in[0].astype(BF16), w_out[0].astype(BF16), conv_w[0]])
    w_ssd, w_att, w_ssd_t, w_att_t, w_out_f, w_out_t, conv_w_f = _full_weights(w_in_all, w_out_all, conv_w_all)
    dx, dw_in, dw_out, small = _local_step(
        x[0], positions[0], loss_target[0], w_ssd, w_att, w_ssd_t, w_att_t, w_out_f, w_out_t,
        conv_w_f, conv_b, dt_bias, a_log, d_skip, ssd_norm_w, attn_sinks, ln_g, ln_b)

    cols = D_IN_PROJ // N_DEV
    rows = 2 * D_MODEL // N_DEV
    dw_in_parts = jnp.transpose(dw_in.reshape(D_MODEL, N_DEV, cols), (1, 0, 2))
    dw_out_parts = dw_out.reshape(N_DEV, rows, D_MODEL)
    own_in = lax.dynamic_index_in_dim(dw_in_parts, me, axis=0, keepdims=False)
    own_out = lax.dynamic_index_in_dim(dw_out_parts, me, axis=0, keepdims=False)
    recv_in, recv_out, small_all = _grad_exchange([dw_in_parts.astype(BF16), dw_out_parts.astype(BF16)], small)

    g_in, d_in, nm_in, nv_in = _adamw_shard(own_in, recv_in, w_in[0], m_w_in[0], v_w_in[0], rows=256, name="adamw_w_in")
    g_out, d_out, nm_out, nv_out = _adamw_shard(own_out, recv_out, w_out[0], m_w_out[0], v_w_out[0], rows=256,
                                                name="adamw_w_out")
    zero = jnp.zeros((), F32)
    pack = lambda d, cw: _pack_small(zero, d["conv_b"], d["dt_bias"], d["a_log"], d["d_skip"], d["attn_sinks"],
                                     d["ssd_norm_w"], d["ln_g"], d["ln_b"], cw)
    zero_cw = jnp.zeros((4, D_XBC), F32)
    wp = pack(dict(conv_b=conv_b, dt_bias=dt_bias, a_log=a_log, d_skip=d_skip, attn_sinks=attn_sinks,
                   ssd_norm_w=ssd_norm_w, ln_g=ln_g, ln_b=ln_b), zero_cw)
    mp = pack(dict(conv_b=m_conv_b, dt_bias=m_dt_bias, a_log=m_a_log, d_skip=m_d_skip, attn_sinks=m_attn_sinks,
                   ssd_norm_w=m_ssd_norm_w, ln_g=m_ln_g, ln_b=m_ln_b), zero_cw)
    vp = pack(dict(conv_b=v_conv_b, dt_bias=v_dt_bias, a_log=v_a_log, d_skip=v_d_skip, attn_sinks=v_attn_sinks,
                   ssd_norm_w=v_ssd_norm_w, ln_g=v_ln_g, ln_b=v_ln_b), zero_cw)
    gs, ds_, ms_, vs_ = [_unpack_small(p) for p in _adamw_small(small_all, wp, mp, vp)]
    g_cw = lax.dynamic_slice_in_dim(gs["conv_w"], me * (D_XBC // N_DEV), D_XBC // N_DEV, axis=1)
    d_cw, nm_cw, nv_cw = _adamw_plain(g_cw, conv_w[0], m_conv_w[0], v_conv_w[0])

    names = ["conv_b", "dt_bias", "a_log", "d_skip", "ssd_norm_w", "attn_sinks"]

    def leaves(big_in, cw, small_d, big_out):
        return ([big_in[None], cw[None]] + [small_d[k] for k in names] + [big_out[None], small_d["ln_g"], small_d["ln_b"]])

    return (gs["loss"], dx[None],
            *leaves(g_in, g_cw, gs, g_out), *leaves(d_in, d_cw, ds_, d_out),
            *leaves(nm_in, nm_cw, ms_, nm_out), *leaves(nv_in, nv_cw, vs_, nv_out))
```

```python
import jax
import jax.numpy as jnp
from jax import lax
from jax.experimental import pallas as pl
from jax.experimental.pallas import tpu as pltpu
import numpy as np

F32 = jnp.float32
BF16 = jnp.bfloat16
_MXU = jnp.bfloat16

N_DEV = 8
D_MODEL = 1024
D_SSD = 1024
D_ATT = 1024
HEAD_DIM = 64
N_HEADS = 16
SSD_GROUPS = 2
KV_HEADS = 4
CHUNK = 128
D_XBC = 1536
D_IN_PROJ = 5136
ROPE_DIM = 16
ROPE_THETA = 500000.0
ALPHA = (2.0 * 1) ** 0.25
LN_EPS = 1e-5
RMS_EPS = 1e-5
ATT_SCALE = HEAD_DIM ** -0.5
NEG = -1e30

S_Z, S_XS, S_B, S_C, S_DT, S_W = 0, 1024, 2048, 2304, 2560, 2816
N_SSD_REAL = 2576
A_Q, A_K, A_V, A_G, A_W = 0, 1024, 1280, 1536, 2560

ADAM_LR = 0.001
ADAM_B1 = 0.9
ADAM_B2 = 0.999
ADAM_EPS = 1e-08
ADAM_WD = 0.01
ADAM_STEP = 10

P_LOSS, P_CONVB, P_DTB, P_ALOG, P_DSKIP, P_SINK, P_NORMW, P_LNG, P_LNB, P_CONVW, P_END = (
    0, 128, 1664, 1792, 1920, 2048, 2176, 3200, 4224, 5248, 11392)
P_ROWS = 96

VMEM_LIMIT = 48 * 1024 * 1024
MESH = pl.DeviceIdType.MESH


def _params(sem=None):
    return pltpu.CompilerParams(dimension_semantics=sem, vmem_limit_bytes=VMEM_LIMIT)


def _mm(a, b):
    return jnp.dot(a.astype(_MXU), b.astype(_MXU), preferred_element_type=F32)


def _mm_nt(a, b):
    return lax.dot_general(a.astype(_MXU), b.astype(_MXU), (((1,), (1,)), ((), ())),
                           preferred_element_type=F32)


def _mm_tn(a, b):
    return lax.dot_general(a.astype(_MXU), b.astype(_MXU), (((0,), (0,)), ((), ())),
                           preferred_element_type=F32)


def _split3(v):
    hi = v.astype(BF16)
    r = v - hi.astype(F32)
    mid = r.astype(BF16)
    lo = (r - mid.astype(F32)).astype(BF16)
    return hi, mid, lo


def _mm_exact_r(v, p01):
    hi, mid, lo = _split3(v)
    d = lambda a: jnp.dot(a, p01, preferred_element_type=F32)
    return d(hi) + d(mid) + d(lo)


def _mm_exact_l(p01, v):
    hi, mid, lo = _split3(v)
    d = lambda a: jnp.dot(p01, a, preferred_element_type=F32)
    return d(hi) + d(mid) + d(lo)


def _sigmoid(x):
    return 1.0 / (1.0 + jnp.exp(-x))


def _softplus(x):
    e = jnp.exp(-jnp.abs(x))
    u = 1.0 + e
    log1p = jnp.where(u == 1.0, e, jnp.log(u) * (e / (u - 1.0)))
    return jnp.maximum(x, 0.0) + log1p


def _rows8(rows):
    n = rows[0].shape[1]
    rid = lax.broadcasted_iota(jnp.int32, (8, n), 0)
    out = jnp.zeros((8, n), F32)
    for k, r in enumerate(rows):
        out = out + jnp.where(rid == k, r, 0.0)
    return out


def _colsum(a):
    return jnp.sum(a, axis=0, keepdims=True)


def _matmul(a, b, *, tm, tn, name, addend=None, scale=1.0, b_transposed=False):
    M, K = a.shape
    N = b.shape[0] if b_transposed else b.shape[1]

    def body(*refs):
        if addend is None:
            a_ref, b_ref, o_ref = refs
        else:
            a_ref, b_ref, c_ref, o_ref = refs
        acc = _mm_nt(a_ref[...], b_ref[...]) if b_transposed else _mm(a_ref[...], b_ref[...])
        if addend is not None:
            acc = acc + scale * c_ref[...]
        o_ref[...] = acc

    b_spec = pl.BlockSpec((tn, K), lambda i, j: (j, 0)) if b_transposed else pl.BlockSpec((K, tn), lambda i, j: (0, j))
    in_specs = [pl.BlockSpec((tm, K), lambda i, j: (i, 0)), b_spec]
    args = [a, b]
    if addend is not None:
        in_specs.append(pl.BlockSpec((tm, tn), lambda i, j: (i, j)))
        args.append(addend)
    return pl.pallas_call(
        body, name=name, grid=(M // tm, N // tn), in_specs=in_specs,
        out_specs=pl.BlockSpec((tm, tn), lambda i, j: (i, j)),
        out_shape=jax.ShapeDtypeStruct((M, N), F32),
        compiler_params=_params(("arbitrary", "arbitrary")),
    )(*args)


def _matmul_tn(a, g, *, tl, tn, name):
    L, M = a.shape
    N = g.shape[1]

    def body(a_ref, g_ref, o_ref):
        @pl.when(pl.program_id(1) == 0)
        def _():
            o_ref[...] = jnp.zeros_like(o_ref)

        o_ref[...] += lax.dot_general(a_ref[...].astype(_MXU), g_ref[...].astype(_MXU),
                                      (((0,), (0,)), ((), ())), preferred_element_type=F32)

    return pl.pallas_call(
        body, name=name, grid=(N // tn, L // tl),
        in_specs=[pl.BlockSpec((tl, M), lambda j, l: (l, 0)), pl.BlockSpec((tl, tn), lambda j, l: (l, j))],
        out_specs=pl.BlockSpec((M, tn), lambda j, l: (0, j)),
        out_shape=jax.ShapeDtypeStruct((M, N), F32),
        compiler_params=_params(("arbitrary", "arbitrary")),
    )(a, g)


def _ssd_recompute(first, p_ref, halo_ref, cw_ref, cb_ref, dtb_ref, alog_ref, e_ref, ext_scr):
    ext_scr[0:8, :] = jnp.where(first, 0.0, halo_ref[:, S_XS:S_DT])
    ext_scr[8:136, :] = p_ref[:, S_XS:S_DT]
    cw = cw_ref[...]
    pre = (cb_ref[0:1, :] + cw[3:4, :] * ext_scr[8:136, :] + cw[2:3, :] * ext_scr[7:135, :]
           + cw[1:2, :] * ext_scr[6:134, :] + cw[0:1, :] * ext_scr[5:133, :])
    sg = _sigmoid(pre)
    act = pre * sg
    lane = lax.broadcasted_iota(jnp.int32, (1, 128), 1)
    A = jnp.where(lane < N_HEADS, -jnp.exp(alog_ref[0:1, :]), 0.0)
    raw = p_ref[:, S_DT:S_DT + 128] + dtb_ref[0:1, :]
    dt = _softplus(raw)
    dA = dt * A
    row = lax.broadcasted_iota(jnp.int32, (128, 128), 0)
    col = lax.broadcasted_iota(jnp.int32, (128, 128), 1)
    tril = (row >= col).astype(BF16)
    acs = _mm_exact_l(tril, dA)
    last = acs[127:128, :]
    ds = jnp.exp(last - acs)
    eo = jnp.exp(acs)
    E = e_ref[...]
    ex = _mm_exact_r(jnp.concatenate([dt, ds, eo], axis=0), E)
    dt_e, ds_e, eo_e = ex[0:128], ex[128:256], ex[256:384]
    xs_c = act[:, 0:1024]
    X = xs_c * dt_e
    return dict(pre=pre, sg=sg, xs_c=xs_c, Bc=act[:, 1024:1280], Cc=act[:, 1280:1536], A=A, raw=raw, dt=dt,
                acs=acs, acsT=acs.T, eo_e=eo_e, ds_e=ds_e, dt_e=dt_e, cd_e=eo_e[127:128, :],
                X=X, Xd=X * ds_e, row=row, col=col)


def _split_halves(t):
    lo = _lo_half(CHUNK)
    return jnp.concatenate([jnp.where(lo, t, 0.0), jnp.where(lo, 0.0, t)], axis=0)


def _ssd_core(R, hprev):
    causal = R["row"] >= R["col"]
    acs, acsT, X = R["acs"], R["acsT"], R["X"]
    ydiag, yoff, snew = [], [], []
    for g in range(SSD_GROUPS):
        Bg = R["Bc"][:, g * 128:(g + 1) * 128]
        Cg = R["Cc"][:, g * 128:(g + 1) * 128]
        cols = slice(g * 512, (g + 1) * 512)
        CB = _mm_nt(Cg, Bg)
        snew.append(_mm_tn(Bg, R["Xd"][:, cols]))
        yoff.append(_mm(Cg, hprev[:, cols]))
        for j in range(4):
            h0 = g * 8 + 2 * j
            ms = [CB * jnp.exp(jnp.where(causal, acs[:, h:h + 1] - acsT[h:h + 1, :], NEG)) for h in (h0, h0 + 1)]
            ydiag.append(_mm(jnp.concatenate(ms, axis=1), _split_halves(X[:, h0 * HEAD_DIM:h0 * HEAD_DIM + 128])))
    Y = jnp.concatenate(ydiag, axis=1) + jnp.concatenate(yoff, axis=1) * R["eo_e"]
    return Y, jnp.concatenate(snew, axis=1)


def _ssd_forward(proj_ssd, conv_w8, conv_b8, dtb8, alog8, dskip_e, norm_w, E):
    L = proj_ssd.shape[0]
    nc = L // CHUNK

    def body(p_ref, halo_ref, cw_ref, cb_ref, dtb_ref, alog_ref, dsk_ref, nw_ref, e_ref,
             y_ref, ypre_ref, hprev_ref, h_scr, ext_scr):
        c = pl.program_id(0)
        first = c == 0

        @pl.when(first)
        def _():
            h_scr[...] = jnp.zeros_like(h_scr)

        R = _ssd_recompute(first, p_ref, halo_ref, cw_ref, cb_ref, dtb_ref, alog_ref, e_ref, ext_scr)
        hprev = h_scr[...]
        hprev_ref[...] = hprev
        Y, snew = _ssd_core(R, hprev)
        h_scr[...] = hprev * R["cd_e"] + snew
        Y = Y + dsk_ref[0:1, :] * R["xs_c"]
        ypre_ref[...] = Y
        z = p_ref[:, S_Z:S_Z + 1024]
        yf = Y * (z * _sigmoid(z))
        outs = []
        for g in range(SSD_GROUPS):
            yg = yf[:, g * 512:(g + 1) * 512]
            r = lax.rsqrt(jnp.mean(yg * yg, axis=-1, keepdims=True) + RMS_EPS)
            outs.append(yg * r)
        y_ref[...] = (jnp.concatenate(outs, axis=1) * nw_ref[0:1, :]).astype(y_ref.dtype)

    const = lambda shape: pl.BlockSpec(shape, lambda c: (0, 0))
    return pl.pallas_call(
        body, name="ssd_fwd", grid=(nc,),
        in_specs=[pl.BlockSpec((CHUNK, S_W), lambda c: (c, 0)),
                  pl.BlockSpec((8, S_W), lambda c: (jnp.maximum(c * 16 - 1, 0), 0)),
                  const((8, D_XBC)), const((8, D_XBC)), const((8, 128)), const((8, 128)),
                  const((8, 1024)), const((8, 1024)), const((128, 1024))],
        out_specs=[pl.BlockSpec((CHUNK, D_SSD), lambda c: (c, 0)), pl.BlockSpec((CHUNK, D_SSD), lambda c: (c, 0)),
                   pl.BlockSpec((128, 1024), lambda c: (c, 0))],
        out_shape=[jax.ShapeDtypeStruct((L, D_SSD), _MXU), jax.ShapeDtypeStruct((L, D_SSD), F32),
                   jax.ShapeDtypeStruct((nc * 128, 1024), F32)],
        scratch_shapes=[pltpu.VMEM((128, 1024), F32), pltpu.VMEM((136, D_XBC), F32)],
        compiler_params=_params(("arbitrary",)),
    )(proj_ssd, proj_ssd, conv_w8, conv_b8, dtb8, alog8, dskip_e, norm_w, E)


def _ssd_backward(proj_ssd, hprev_all, ypre, dy, conv_w8, conv_b8, dtb8, alog8, dskip_e, norm_w, E, ET):
    L = proj_ssd.shape[0]
    nc = L // CHUNK

    def body(p_ref, halo_ref, hprev_ref, ypre_ref, dy_ref, cw_ref, cb_ref, dtb_ref, alog_ref, dsk_ref, nw_ref, e_ref,
             et_ref, dp_ref, acc_cw_ref, acc_w_ref, acc_s_ref, dh_scr, ext_scr, ext2_scr, nxt_scr):
        i = pl.program_id(0)
        c = nc - 1 - i
        first = c == 0

        @pl.when(i == 0)
        def _():
            dh_scr[...] = jnp.zeros_like(dh_scr)
            nxt_scr[...] = jnp.zeros_like(nxt_scr)
            acc_cw_ref[...] = jnp.zeros_like(acc_cw_ref)
            acc_w_ref[...] = jnp.zeros_like(acc_w_ref)
            acc_s_ref[...] = jnp.zeros_like(acc_s_ref)

        R = _ssd_recompute(first, p_ref, halo_ref, cw_ref, cb_ref, dtb_ref, alog_ref, e_ref, ext_scr)
        hprev = hprev_ref[...]
        xs_c, X, Xd = R["xs_c"], R["X"], R["Xd"]
        acs, acsT = R["acs"], R["acsT"]
        ET = et_ref[...]
        dsk = dsk_ref[0:1, :]
        Y = ypre_ref[...]

        z = p_ref[:, S_Z:S_Z + 1024]
        sz = _sigmoid(z)
        silz = z * sz
        yf = Y * silz
        dyv = dy_ref[...]
        nw = nw_ref[0:1, :]
        dyf_parts, dnw_parts = [], []
        for g in range(SSD_GROUPS):
            cols = slice(g * 512, (g + 1) * 512)
            yg = yf[:, cols]
            r = lax.rsqrt(jnp.mean(yg * yg, axis=-1, keepdims=True) + RMS_EPS)
            yn = yg * r
            dyn = dyv[:, cols] * nw[:, cols]
            dnw_parts.append(_colsum(dyv[:, cols] * yn))
            dyf_parts.append(r * (dyn - yn * jnp.mean(dyn * yn, axis=-1, keepdims=True)))
        dyf = jnp.concatenate(dyf_parts, axis=1)
        dY = dyf * silz
        dz = dyf * Y * (sz * (1.0 + z * (1.0 - sz)))

        dhn = dh_scr[...]
        dYo = dY * R["eo_e"]
        causal = R["row"] >= R["col"]
        causal_t = R["col"] >= R["row"]
        dacs = jnp.zeros((128, 128), F32)
        dacs_t = jnp.zeros((128, 128), F32)
        dxdiag, dxd, dhprev, dBs, dCs, yoff = [], [], [], [], [], []
        for g in range(SSD_GROUPS):
            Bg = R["Bc"][:, g * 128:(g + 1) * 128]
            Cg = R["Cc"][:, g * 128:(g + 1) * 128]
            cols = slice(g * 512, (g + 1) * 512)
            CB = _mm_nt(Cg, Bg)
            CBt = CB.T
            dCB = jnp.zeros((128, 128), F32)
            for j in range(4):
                h0 = g * 8 + 2 * j
                pc = slice(h0 * HEAD_DIM, h0 * HEAD_DIM + 128)
                dYst = _split_halves(dY[:, pc])
                dMst = _mm_nt(dYst, X[:, pc])
                mts = []
                for a, h in enumerate((h0, h0 + 1)):
                    acol = acs[:, h:h + 1]
                    arow = acsT[h:h + 1, :]
                    dL = dMst[a * 128:(a + 1) * 128] * jnp.exp(jnp.where(causal, acol - arow, NEG))
                    dCB = dCB + dL
                    G = dL * CB
                    dacs = dacs + jnp.where(R["col"] == h, jnp.sum(G, axis=1, keepdims=True), 0.0)
                    dacs_t = dacs_t + jnp.where(R["row"] == h, jnp.sum(G, axis=0, keepdims=True), 0.0)
                    mts.append(CBt * jnp.exp(jnp.where(causal_t, arow - acol, NEG)))
                dxdiag.append(_mm(jnp.concatenate(mts, axis=1), dYst))
            dS = dhn[:, cols]
            dxd.append(_mm(Bg, dS))
            yoff.append(_mm(Cg, hprev[:, cols]))
            dhprev.append(_mm_tn(Cg, dYo[:, cols]))
            dCs.append(_mm_nt(dYo[:, cols], hprev[:, cols]) + _mm(dCB, Bg))
            dBs.append(_mm_tn(dCB, Cg) + _mm_nt(Xd[:, cols], dS))
        Yoff = jnp.concatenate(yoff, axis=1) * R["eo_e"]
        dXd = jnp.concatenate(dxd, axis=1)
        dX = jnp.concatenate(dxdiag, axis=1) + dXd * R["ds_e"]
        t_state = dXd * Xd
        hs = _mm_exact_r(jnp.concatenate([dY * Yoff - t_state, dX * xs_c], axis=0), ET)
        dacs = dacs + hs[0:128] - dacs_t.T
        v_last = _colsum(t_state + dhn * hprev * R["cd_e"])
        dlast = _mm_exact_r(jnp.broadcast_to(v_last, (8, 1024)), ET)[0:1, :]
        dacs = dacs + jnp.where(R["row"] == 127, dlast, 0.0)
        triu = (R["col"] >= R["row"]).astype(BF16)
        da = _mm_exact_l(triu, dacs)
        ddt = da * R["A"] + hs[128:256]
        ddt_raw = ddt * _sigmoid(R["raw"])
        dxs_c = dX * R["dt_e"] + dY * dsk
        dh_scr[...] = jnp.concatenate(dhprev, axis=1) + dhn * R["cd_e"]

        dact = jnp.concatenate([dxs_c] + dBs + dCs, axis=1)
        pre, sg = R["pre"], R["sg"]
        dpre = dact * (sg * (1.0 + pre * (1.0 - sg)))
        ext2_scr[0:128, :] = dpre
        ext2_scr[128:136, :] = nxt_scr[...]
        nxt_scr[...] = dpre[0:8, :]
        cw = cw_ref[...]
        dxbc = (cw[3:4, :] * dpre + cw[2:3, :] * ext2_scr[1:129, :] + cw[1:2, :] * ext2_scr[2:130, :]
                + cw[0:1, :] * ext2_scr[3:131, :])
        acc_cw_ref[...] += _rows8([_colsum(dpre * ext_scr[5 + k:133 + k, :]) for k in range(4)] + [_colsum(dpre)])
        acc_w_ref[...] += _rows8([jnp.concatenate(dnw_parts, axis=1), _colsum(dY * xs_c)])
        acc_s_ref[...] += _rows8([_colsum(ddt_raw), _colsum(da * R["dt"])])

        lane = lax.broadcasted_iota(jnp.int32, (128, 128), 1)
        dp_ref[:, S_Z:S_Z + 1024] = dz
        dp_ref[:, S_XS:S_DT] = dxbc
        dp_ref[:, S_DT:S_DT + 128] = jnp.where(lane < N_HEADS, ddt_raw, 0.0)
        dp_ref[:, S_DT + 128:S_W] = jnp.zeros((128, 128), F32)

        @pl.when(i == nc - 1)
        def _():
            acc = acc_s_ref[...]
            dskip = _mm_exact_r(acc_w_ref[...], ET)[1:2, :]
            acc_s_ref[...] = _rows8([acc[0:1, :], acc[1:2, :] * R["A"], dskip])

    const = lambda shape: pl.BlockSpec(shape, lambda i: (0, 0))
    rev = lambda i: (nc - 1 - i, 0)
    return pl.pallas_call(
        body, name="ssd_bwd", grid=(nc,),
        in_specs=[pl.BlockSpec((CHUNK, S_W), rev),
                  pl.BlockSpec((8, S_W), lambda i: (jnp.maximum((nc - 1 - i) * 16 - 1, 0), 0)),
                  pl.BlockSpec((128, 1024), rev),
                  pl.BlockSpec((CHUNK, D_SSD), rev),
                  pl.BlockSpec((CHUNK, D_SSD), rev),
                  const((8, D_XBC)), const((8, D_XBC)), const((8, 128)), const((8, 128)),
                  const((8, 1024)), const((8, 1024)), const((128, 1024)), const((1024, 128))],
        out_specs=[pl.BlockSpec((CHUNK, S_W), rev), const((8, D_XBC)), const((8, 1024)), const((8, 128))],
        out_shape=[jax.ShapeDtypeStruct((L, S_W), F32), jax.ShapeDtypeStruct((8, D_XBC), F32),
                   jax.ShapeDtypeStruct((8, 1024), F32), jax.ShapeDtypeStruct((8, 128), F32)],
        scratch_shapes=[pltpu.VMEM((128, 1024), F32), pltpu.VMEM((136, D_XBC), F32),
                        pltpu.VMEM((136, D_XBC), F32), pltpu.VMEM((8, D_XBC), F32)],
        compiler_params=_params(("arbitrary",)),
    )(proj_ssd, proj_ssd, hprev_all, ypre, dy, conv_w8, conv_b8, dtb8, alog8, dskip_e, norm_w, E, ET)


def _rope(t, tab):
    cos, sa, sb = tab[:, 0:128], tab[:, 128:256], tab[:, 256:384]
    outs = []
    for i in range(t.shape[1] // 128):
        tg = t[:, i * 128:(i + 1) * 128]
        outs.append(tg * cos + pltpu.roll(tg, 8, 1) * sa + pltpu.roll(tg, 120, 1) * sb)
    return jnp.concatenate(outs, axis=1)


def _rope_transposed(d, tab):
    cos, sa, sb = tab[:, 0:128], tab[:, 128:256], tab[:, 256:384]
    outs = []
    for i in range(d.shape[1] // 128):
        dg = d[:, i * 128:(i + 1) * 128]
        outs.append(dg * cos + pltpu.roll(dg * sa, 120, 1) + pltpu.roll(dg * sb, 8, 1))
    return jnp.concatenate(outs, axis=1)


def _lo_half(rows):
    return lax.broadcasted_iota(jnp.int32, (rows, 128), 1) < HEAD_DIM


def _kv_both(t, j):
    p, b = j // 2, j % 2
    lo = _lo_half(2 * CHUNK)
    nat = jnp.where(lo if b == 0 else jnp.logical_not(lo), t[:, p * 128:(p + 1) * 128], 0.0)
    return nat + pltpu.roll(nat, HEAD_DIM, 1)


def _stack_heads(t, j):
    lo = _lo_half(CHUNK)
    hi = jnp.logical_not(lo)
    a, b = t[:, 2 * j * 128:(2 * j + 1) * 128], t[:, (2 * j + 1) * 128:(2 * j + 2) * 128]
    return jnp.concatenate([jnp.where(lo, a, 0.0), jnp.where(hi, a, 0.0),
                            jnp.where(lo, b, 0.0), jnp.where(hi, b, 0.0)], axis=0)


def _unstack_heads(s):
    lo = _lo_half(CHUNK)
    return jnp.concatenate([jnp.where(lo, s[0:128], s[128:256]), jnp.where(lo, s[256:384], s[384:512])], axis=1)


def _fold_kv(r, j):
    lo = _lo_half(2 * CHUNK)
    return jnp.where(lo if j % 2 == 0 else jnp.logical_not(lo), r + pltpu.roll(r, HEAD_DIM, 1), 0.0)


def _sink_column(sink_ref, j):
    rid = lax.broadcasted_iota(jnp.int32, (4 * CHUNK, 1), 0) // CHUNK
    col = jnp.zeros((4 * CHUNK, 1), F32)
    for hh in range(4):
        col = jnp.where(rid == hh, sink_ref[4 * j + hh], col)
    return col


def _band_mask(blk):
    qi = lax.broadcasted_iota(jnp.int32, (4 * CHUNK, 2 * CHUNK), 0) % CHUNK
    si = lax.broadcasted_iota(jnp.int32, (4 * CHUNK, 2 * CHUNK), 1)
    return (si > qi) & (si <= qi + CHUNK) & ((blk > 0) | (si >= CHUNK))


def _softmax_sink(s, valid, sink):
    s = jnp.where(valid, s, NEG)
    mx = jnp.maximum(jnp.max(s, axis=-1, keepdims=True), sink)
    p = jnp.exp(s - mx)
    esink = jnp.exp(sink - mx)
    den = jnp.sum(p, axis=-1, keepdims=True) + esink
    return p / den, esink / den


def _swa_forward(proj_att, tabs, sinks):
    L = proj_att.shape[0]
    nb = L // CHUNK

    def body(sink_ref, p_ref, prev_ref, tab_ref, ptab_ref, y_ref):
        n = pl.program_id(0)
        tab = tab_ref[...]
        qr = _rope(p_ref[:, A_Q:A_Q + 1024], tab)
        k_cur = _rope(p_ref[:, A_K:A_K + 256], tab)
        k_prev = _rope(prev_ref[:, 0:256], ptab_ref[...])
        kk = jnp.concatenate([k_prev, k_cur], axis=0)
        vv = jnp.concatenate([prev_ref[:, 256:512], p_ref[:, A_V:A_V + 256]], axis=0)
        valid = _band_mask(n)
        outs = []
        for j in range(KV_HEADS):
            s = _mm_nt(_stack_heads(qr, j), _kv_both(kk, j)) * ATT_SCALE
            P, _ = _softmax_sink(s, valid, _sink_column(sink_ref, j))
            outs.append(_unstack_heads(_mm(P, _kv_both(vv, j))))
        g = p_ref[:, A_G:A_G + 1024]
        y_ref[...] = (jnp.concatenate(outs, axis=1) * (g * _sigmoid(g))).astype(y_ref.dtype)

    return pl.pallas_call(
        body, name="swa_fwd", grid=(nb,),
        in_specs=[pl.BlockSpec(memory_space=pltpu.SMEM),
                  pl.BlockSpec((CHUNK, A_W), lambda n: (n, 0)),
                  pl.BlockSpec((CHUNK, 512), lambda n: (jnp.maximum(n - 1, 0), 2)),
                  pl.BlockSpec((CHUNK, 384), lambda n: (n, 0)),
                  pl.BlockSpec((CHUNK, 384), lambda n: (jnp.maximum(n - 1, 0), 0))],
        out_specs=pl.BlockSpec((CHUNK, D_ATT), lambda n: (n, 0)),
        out_shape=jax.ShapeDtypeStruct((L, D_ATT), _MXU),
        compiler_params=_params(("arbitrary",)),
    )(sinks, proj_att, proj_att, tabs, tabs)


def _swa_backward(proj_att, tabs, sinks, dy):
    L = proj_att.shape[0]
    nb = L // CHUNK

    def body(sink_ref, p_ref, prev_ref, tab_ref, ptab_ref, dy_ref, dp_ref, dsink_ref, carry_k, carry_v):
        i = pl.program_id(0)
        n = nb - 1 - i

        @pl.when(i == 0)
        def _():
            carry_k[...] = jnp.zeros_like(carry_k)
            carry_v[...] = jnp.zeros_like(carry_v)
            dsink_ref[...] = jnp.zeros_like(dsink_ref)

        tab = tab_ref[...]
        qr = _rope(p_ref[:, A_Q:A_Q + 1024], tab)
        k_cur = _rope(p_ref[:, A_K:A_K + 256], tab)
        k_prev = _rope(prev_ref[:, 0:256], ptab_ref[...])
        kk = jnp.concatenate([k_prev, k_cur], axis=0)
        vv = jnp.concatenate([prev_ref[:, 256:512], p_ref[:, A_V:A_V + 256]], axis=0)
        valid = _band_mask(n)
        g = p_ref[:, A_G:A_G + 1024]
        sgm = _sigmoid(g)
        dyv = dy_ref[...]
        do_all = dyv * (g * sgm)
        lane8 = lax.broadcasted_iota(jnp.int32, (8, 128), 1)
        rid = lax.broadcasted_iota(jnp.int32, (4 * CHUNK, 1), 0) // CHUNK
        o_parts, dq_parts = [], []
        dk_nat = [jnp.zeros((2 * CHUNK, 128), F32) for _ in range(2)]
        dv_nat = [jnp.zeros((2 * CHUNK, 128), F32) for _ in range(2)]
        dsink = jnp.zeros((8, 128), F32)
        for j in range(KV_HEADS):
            qs = _stack_heads(qr, j)
            kkb, vvb = _kv_both(kk, j), _kv_both(vv, j)
            s = _mm_nt(qs, kkb) * ATT_SCALE
            P, psink = _softmax_sink(s, valid, _sink_column(sink_ref, j))
            o_s = _mm(P, vvb)
            o_parts.append(_unstack_heads(o_s))
            do_s = _stack_heads(do_all, j)
            D = jnp.sum(do_s * o_s, axis=-1, keepdims=True)
            dS = P * (_mm_nt(do_s, vvb) - D)
            sd = psink * D
            for hh in range(4):
                dsink = dsink + jnp.where(lane8 == 4 * j + hh, -jnp.sum(jnp.where(rid == hh, sd, 0.0)), 0.0)
            dq_parts.append(_unstack_heads(_mm(dS, kkb)) * ATT_SCALE)
            dk_nat[j // 2] = dk_nat[j // 2] + _fold_kv(_mm_tn(dS, qs), j) * ATT_SCALE
            dv_nat[j // 2] = dv_nat[j // 2] + _fold_kv(_mm_tn(P, do_s), j)
        o = jnp.concatenate(o_parts, axis=1)
        dkk = jnp.concatenate(dk_nat, axis=1)
        dvv = jnp.concatenate(dv_nat, axis=1)
        dp_ref[:, A_Q:A_Q + 1024] = _rope_transposed(jnp.concatenate(dq_parts, axis=1), tab)
        dp_ref[:, A_K:A_K + 256] = _rope_transposed(dkk[CHUNK:2 * CHUNK] + carry_k[...], tab)
        dp_ref[:, A_V:A_V + 256] = dvv[CHUNK:2 * CHUNK] + carry_v[...]
        dp_ref[:, A_G:A_G + 1024] = dyv * o * (sgm * (1.0 + g * (1.0 - sgm)))
        carry_k[...] = dkk[0:CHUNK]
        carry_v[...] = dvv[0:CHUNK]
        dsink_ref[...] += dsink

    rev = lambda i: (nb - 1 - i, 0)
    prev = lambda i: jnp.maximum(nb - 2 - i, 0)
    return pl.pallas_call(
        body, name="swa_bwd", grid=(nb,),
        in_specs=[pl.BlockSpec(memory_space=pltpu.SMEM),
                  pl.BlockSpec((CHUNK, A_W), rev),
                  pl.BlockSpec((CHUNK, 512), lambda i: (prev(i), 2)),
                  pl.BlockSpec((CHUNK, 384), rev),
                  pl.BlockSpec((CHUNK, 384), lambda i: (prev(i), 0)),
                  pl.BlockSpec((CHUNK, D_ATT), lambda i: (nb - 1 - i, 1))],
        out_specs=[pl.BlockSpec((CHUNK, A_W), rev), pl.BlockSpec((8, 128), lambda i: (0, 0))],
        out_shape=[jax.ShapeDtypeStruct((L, A_W), F32), jax.ShapeDtypeStruct((8, 128), F32)],
        scratch_shapes=[pltpu.VMEM((CHUNK, 256), F32), pltpu.VMEM((CHUNK, 256), F32)],
        compiler_params=_params(("arbitrary",)),
    )(sinks, proj_att, proj_att, tabs, tabs, dy)


def _head(y_ssd, y_att, x, target, w_out, ln_g8, ln_b8, *, tm):
    L = x.shape[0]
    nsteps = L // tm

    def body(ys_ref, ya_ref, x_ref, t_ref, wo_ref, g_ref, b_ref, dr_ref, dy_ref, acc_ref):
        i = pl.program_id(0)

        @pl.when(i == 0)
        def _():
            acc_ref[...] = jnp.zeros_like(acc_ref)

        h = _mm(ys_ref[...], wo_ref[0:1024, :]) + _mm(ya_ref[...], wo_ref[1024:2048, :])
        r = ALPHA * x_ref[...] + h
        mu = jnp.mean(r, axis=-1, keepdims=True)
        d = r - mu
        rstd = lax.rsqrt(jnp.mean(d * d, axis=-1, keepdims=True) + LN_EPS)
        xh = d * rstd
        gam = g_ref[0:1, :]
        e = xh * gam + b_ref[0:1, :] - t_ref[...]
        dout = e * (1.0 / D_MODEL)
        dxh = dout * gam
        dr = rstd * (dxh - jnp.mean(dxh, axis=-1, keepdims=True)
                     - xh * jnp.mean(dxh * xh, axis=-1, keepdims=True))
        dr_ref[...] = dr
        dy_ref[...] = _mm_nt(dr, wo_ref[...])
        acc_ref[...] += _rows8([_colsum(dout * xh), _colsum(dout), _colsum(e * e) * (0.5 / D_MODEL)])

        @pl.when(i == nsteps - 1)
        def _():
            acc = acc_ref[...]
            tot = jnp.sum(acc[2:3, :])
            rid = lax.broadcasted_iota(jnp.int32, (8, 1024), 0)
            acc_ref[...] = jnp.where(rid == 3, tot, acc)

    const = lambda shape: pl.BlockSpec(shape, lambda i: (0, 0))
    row = lambda w: pl.BlockSpec((tm, w), lambda i: (i, 0))
    return pl.pallas_call(
        body, name="head", grid=(nsteps,),
        in_specs=[row(1024), row(1024), row(1024), row(1024), const((2048, 1024)),
                  const((8, 1024)), const((8, 1024))],
        out_specs=[row(1024), row(2048), const((8, 1024))],
        out_shape=[jax.ShapeDtypeStruct((L, D_MODEL), F32), jax.ShapeDtypeStruct((L, 2048), F32),
                   jax.ShapeDtypeStruct((8, 1024), F32)],
        compiler_params=_params(("arbitrary",)),
    )(y_ssd, y_att, x, target, w_out, ln_g8, ln_b8)


def _position():
    return lax.axis_index("x"), lax.axis_index("y"), lax.axis_index("c")


def _index(px, py, pc):
    return 4 * px + 2 * py + pc


def _all_gather(shards):
    n = len(shards)
    any_spec = pl.BlockSpec(memory_space=pl.ANY)

    def body(*refs):
        ins, outs = refs[:n], refs[n:2 * n]
        send_sems, recv_sems, local_sems = refs[2 * n:]
        x, y, c = _position()
        me, sibling = (x, y, c), (x, y, 1 - c)
        chips = [(1 - x, y), (x, 1 - y), (1 - x, 1 - y)]

        def copy(a, k, block, to, src=None):
            slot = outs[a].at[_index(*block)]
            return pltpu.make_async_remote_copy(
                src_ref=slot if src is None else src, dst_ref=slot,
                send_sem=send_sems.at[a, k], recv_sem=recv_sems.at[a, k],
                device_id=to, device_id_type=MESH)

        mine = [pltpu.make_async_copy(ins[a], outs[a].at[_index(*me)], local_sems.at[a]) for a in range(n)]
        for cp in mine:
            cp.start()
        first = []
        for a in range(n):
            first.append(copy(a, 0, me, sibling, src=ins[a]))
            first += [copy(a, 1 + j, me, (*chip, c), src=ins[a]) for j, chip in enumerate(chips)]
        for cp in first:
            cp.start()
        passed = []
        for j, chip in enumerate(chips):
            for a in range(n):
                copy(a, 1 + j, (*chip, c), me).wait_recv()
                fwd = copy(a, 4 + j, (*chip, c), sibling)
                fwd.start()
                passed.append(fwd)
        for a in range(n):
            copy(a, 0, sibling, me).wait_recv()
            for j, chip in enumerate(chips):
                copy(a, 4 + j, (*chip, 1 - c), me).wait_recv()
        for cp in first + passed:
            cp.wait_send()
        for cp in mine:
            cp.wait()

    return pl.pallas_call(
        body, name="weight_all_gather",
        in_specs=[any_spec] * n, out_specs=[any_spec] * n,
        out_shape=[jax.ShapeDtypeStruct((N_DEV,) + s.shape, s.dtype) for s in shards],
        scratch_shapes=[pltpu.SemaphoreType.DMA((n, 7)), pltpu.SemaphoreType.DMA((n, 7)),
                        pltpu.SemaphoreType.DMA((n,))],
    )(*shards)


def _grad_exchange(parts, small):
    n = len(parts)
    any_spec = pl.BlockSpec(memory_space=pl.ANY)

    def body(*refs):
        ins, small_ref = refs[:n], refs[n]
        outs, small_out = refs[n + 1:2 * n + 1], refs[2 * n + 1]
        send_sems, recv_sems, local_sem = refs[2 * n + 2:]
        x, y, c = _position()
        me = _index(x, y, c)
        peers = []
        for k in range(1, N_DEV):
            fx, fy, fc = (k >> 2) & 1, (k >> 1) & 1, k & 1
            peers.append(((1 - x) if fx else x, (1 - y) if fy else y, (1 - c) if fc else c))

        def copies(k):
            peer = peers[k - 1]
            out = [pltpu.make_async_remote_copy(
                src_ref=ins[a].at[_index(*peer)], dst_ref=outs[a].at[k - 1],
                send_sem=send_sems.at[a, k - 1], recv_sem=recv_sems.at[a, k - 1],
                device_id=peer, device_id_type=MESH) for a in range(n)]
            out.append(pltpu.make_async_remote_copy(
                src_ref=small_ref, dst_ref=small_out.at[me],
                send_sem=send_sems.at[n, k - 1], recv_sem=recv_sems.at[n, k - 1],
                device_id=peer, device_id_type=MESH))
            return out

        def arrivals(k):
            peer = peers[k - 1]
            out = [pltpu.make_async_remote_copy(
                src_ref=ins[a].at[me], dst_ref=outs[a].at[k - 1],
                send_sem=send_sems.at[a, k - 1], recv_sem=recv_sems.at[a, k - 1],
                device_id=peer, device_id_type=MESH) for a in range(n)]
            out.append(pltpu.make_async_remote_copy(
                src_ref=small_ref, dst_ref=small_out.at[_index(*peer)],
                send_sem=send_sems.at[n, k - 1], recv_sem=recv_sems.at[n, k - 1],
                device_id=peer, device_id_type=MESH))
            return out

        mine = pltpu.make_async_copy(small_ref, small_out.at[me], local_sem)
        mine.start()
        sent = []
        for k in range(1, N_DEV):
            for cp in copies(k):
                cp.start()
                sent.append(cp)
        for k in range(1, N_DEV):
            for cp in arrivals(k):
                cp.wait_recv()
        for cp in sent:
            cp.wait_send()
        mine.wait()

    return pl.pallas_call(
        body, name="grad_exchange",
        in_specs=[any_spec] * (n + 1), out_specs=[any_spec] * (n + 1),
        out_shape=[jax.ShapeDtypeStruct((N_DEV - 1,) + p.shape[1:], p.dtype) for p in parts]
        + [jax.ShapeDtypeStruct((N_DEV,) + small.shape, small.dtype)],
        scratch_shapes=[pltpu.SemaphoreType.DMA((n + 1, 7)), pltpu.SemaphoreType.DMA((n + 1, 7)),
                        pltpu.SemaphoreType.DMA],
    )(*parts, small)


def _adamw_math(w, g, m, v):
    m = ADAM_B1 * m + (1.0 - ADAM_B1) * g
    v = ADAM_B2 * v + (1.0 - ADAM_B2) * (g * g)
    m_hat = m / (1.0 - ADAM_B1 ** ADAM_STEP)
    v_hat = v / (1.0 - ADAM_B2 ** ADAM_STEP)
    delta = -ADAM_LR * (m_hat / (jnp.sqrt(v_hat) + ADAM_EPS) + ADAM_WD * w)
    return delta, m, v


def _adamw_shard(g_own, recv, w, m, v, *, rows, name):
    R, C = g_own.shape

    def body(g_ref, r_ref, w_ref, m_ref, v_ref, go_ref, d_ref, mo_ref, vo_ref):
        g = g_ref[...]
        for k in range(N_DEV - 1):
            g = g + r_ref[k].astype(F32)
        d, mn, vn = _adamw_math(w_ref[...], g, m_ref[...], v_ref[...])
        go_ref[...] = g
        d_ref[...] = d
        mo_ref[...] = mn
        vo_ref[...] = vn

    blk = pl.BlockSpec((rows, C), lambda i: (i, 0))
    return pl.pallas_call(
        body, name=name, grid=(R // rows,),
        in_specs=[blk, pl.BlockSpec((N_DEV - 1, rows, C), lambda i: (0, i, 0)), blk, blk, blk],
        out_specs=[blk] * 4, out_shape=[jax.ShapeDtypeStruct((R, C), F32)] * 4,
        compiler_params=_params(("arbitrary",)),
    )(g_own, recv, w, m, v)


def _adamw_small(gathered, w, m, v):
    def body(r_ref, w_ref, m_ref, v_ref, go_ref, d_ref, mo_ref, vo_ref):
        g = r_ref[0]
        for k in range(1, N_DEV):
            g = g + r_ref[k]
        d, mn, vn = _adamw_math(w_ref[...], g, m_ref[...], v_ref[...])
        go_ref[...] = g
        d_ref[...] = d
        mo_ref[...] = mn
        vo_ref[...] = vn

    return pl.pallas_call(
        body, name="adamw_small", out_shape=[jax.ShapeDtypeStruct(w.shape, F32)] * 4,
        compiler_params=_params(),
    )(gathered, w, m, v)


def _adamw_plain(g, w, m, v):
    def body(g_ref, w_ref, m_ref, v_ref, d_ref, mo_ref, vo_ref):
        d, mn, vn = _adamw_math(w_ref[...], g_ref[...], m_ref[...], v_ref[...])
        d_ref[...] = d
        mo_ref[...] = mn
        vo_ref[...] = vn

    return pl.pallas_call(
        body, name="adamw_conv_w", out_shape=[jax.ShapeDtypeStruct(w.shape, F32)] * 3,
        compiler_params=_params(),
    )(g, w, m, v)


def _pad_rows8(v):
    v = v.reshape(-1, v.shape[-1])
    return jnp.pad(v, ((0, 8 - v.shape[0]), (0, 0)))


def _pad_lanes(v, n):
    return jnp.pad(v, ((0, 0), (0, n - v.shape[1])))


def _pack_small(loss, conv_b, dt_bias, a_log, d_skip, sinks, norm_w, ln_g, ln_b, conv_w):
    def seg(v, n):
        v = v.reshape(-1).astype(F32)
        return jnp.pad(v, (0, n - v.shape[0]))

    flat = jnp.concatenate([seg(loss, 128), seg(conv_b, 1536), seg(dt_bias, 128), seg(a_log, 128),
                            seg(d_skip, 128), seg(sinks, 128), seg(norm_w, 1024), seg(ln_g, 1024),
                            seg(ln_b, 1024), seg(conv_w, 6144), jnp.zeros((P_ROWS * 128 - P_END,), F32)])
    return flat.reshape(P_ROWS, 128)


def _unpack_small(p):
    f = p.reshape(-1)
    return dict(conv_b=f[P_CONVB:P_CONVB + 1536].reshape(1, 1536), dt_bias=f[P_DTB:P_DTB + 16].reshape(1, 16),
                a_log=f[P_ALOG:P_ALOG + 16].reshape(1, 16), d_skip=f[P_DSKIP:P_DSKIP + 16].reshape(1, 16),
                attn_sinks=f[P_SINK:P_SINK + 16].reshape(1, 16), ssd_norm_w=f[P_NORMW:P_NORMW + 1024].reshape(1, 1024),
                ln_g=f[P_LNG:P_LNG + 1024].reshape(1, 1024), ln_b=f[P_LNB:P_LNB + 1024].reshape(1, 1024),
                conv_w=f[P_CONVW:P_CONVW + 6144].reshape(4, 1536), loss=f[P_LOSS])


def _lane_pattern(fn):
    return np.asarray([fn(l % HEAD_DIM) for l in range(128)], np.float32)


ROPE_INV = _lane_pattern(lambda r: ROPE_THETA ** (-2.0 * (r % 8) / ROPE_DIM) if r < ROPE_DIM else 0.0)
ROPE_SIN_A = _lane_pattern(lambda r: 1.0 if 8 <= r < ROPE_DIM else 0.0)
ROPE_SIN_B = _lane_pattern(lambda r: -1.0 if r < 8 else 0.0)


def _rope_tables(positions):
    ang = positions.astype(F32)[:, None] * ROPE_INV[None, :]
    sn = jnp.sin(ang)
    return jnp.concatenate([jnp.cos(ang), sn * ROPE_SIN_A[None, :], sn * ROPE_SIN_B[None, :]], axis=1)


def _expansion():
    E = np.arange(1024)[None, :] // HEAD_DIM == np.arange(128)[:, None]
    return jnp.asarray(E, BF16), jnp.asarray(E.T, BF16)


def _local_step(x, positions, target, w_ssd, w_att, w_out,
                conv_w, conv_b, dt_bias, a_log, d_skip, norm_w, sinks, ln_g, ln_b):
    E, ET = _expansion()
    conv_w8, conv_b8 = _pad_rows8(conv_w), _pad_rows8(conv_b)
    dtb8, alog8 = _pad_rows8(_pad_lanes(dt_bias, 128)), _pad_rows8(_pad_lanes(a_log, 128))
    dskip_e = _pad_rows8(jnp.repeat(d_skip, HEAD_DIM, axis=1))
    norm_w8, ln_g8, ln_b8 = _pad_rows8(norm_w), _pad_rows8(ln_g), _pad_rows8(ln_b)
    sinks1 = sinks.reshape(-1)
    tabs = _rope_tables(positions)
    xb = x.astype(_MXU)
    L = x.shape[0]
    tm = min(1024, L)

    proj_ssd = _matmul(xb, w_ssd, tm=tm, tn=S_W // 2, name="in_proj_ssd")
    proj_att = _matmul(xb, w_att, tm=tm, tn=A_W // 2, name="in_proj_att")
    y_ssd, ypre, hprev = _ssd_forward(proj_ssd, conv_w8, conv_b8, dtb8, alog8, dskip_e, norm_w8, E)
    y_att = _swa_forward(proj_att, tabs, sinks1)
    dr, dy, acc_head = _head(y_ssd, y_att, x, target, w_out, ln_g8, ln_b8, tm=min(256, L))
    d_ssd, acc_cw, acc_w, acc_s = _ssd_backward(proj_ssd, hprev, ypre, dy, conv_w8, conv_b8, dtb8, alog8,
                                                dskip_e, norm_w8, E, ET)
    d_att, dsink = _swa_backward(proj_att, tabs, sinks1, dy)
    tl = min(512, L)
    dw_ssd = _matmul_tn(xb, d_ssd, tl=tl, tn=S_W // 2, name="dw_in_ssd")
    dw_att = _matmul_tn(xb, d_att, tl=tl, tn=A_W // 2, name="dw_in_att")
    dw_out_s = _matmul_tn(y_ssd, dr, tl=tl, tn=1024, name="dw_out_ssd")
    dw_out_a = _matmul_tn(y_att, dr, tl=tl, tn=1024, name="dw_out_att")
    dx = _matmul(d_ssd, w_ssd, tm=tl, tn=1024, name="dx_ssd", addend=dr, scale=ALPHA, b_transposed=True)
    dx = _matmul(d_att, w_att, tm=tl, tn=1024, name="dx_att", addend=dx, scale=1.0, b_transposed=True)
    dw_in = jnp.concatenate([dw_ssd[:, :N_SSD_REAL], dw_att], axis=1)
    dw_out = jnp.concatenate([dw_out_s, dw_out_a], axis=0)
    small = _pack_small(acc_head[3, 0], acc_cw[4], acc_s[0, :16], acc_s[1, :16], acc_s[2, :16], dsink[0, :16],
                        acc_w[0], acc_head[0], acc_head[1], acc_cw[0:4])
    return dx, dw_in, dw_out, small


def _full_weights(w_in_all, w_out_all, conv_w_all):
    w_in = jnp.transpose(w_in_all, (1, 0, 2)).reshape(D_MODEL, D_IN_PROJ)
    w_ssd = jnp.pad(w_in[:, :N_SSD_REAL], ((0, 0), (0, S_W - N_SSD_REAL)))
    w_att = w_in[:, N_SSD_REAL:]
    w_out = w_out_all.reshape(2 * D_MODEL, D_MODEL)
    conv_w = jnp.transpose(conv_w_all, (1, 0, 2)).reshape(4, D_XBC)
    return w_ssd, w_att, w_out, conv_w


def kernel(x, positions, w_in, conv_w, conv_b, dt_bias, a_log, d_skip, ssd_norm_w, attn_sinks, w_out, ln_g, ln_b, loss_target, m_w_in, m_conv_w, m_conv_b, m_dt_bias, m_a_log, m_d_skip, m_ssd_norm_w, m_attn_sinks, m_w_out, m_ln_g, m_ln_b, v_w_in, v_conv_w, v_conv_b, v_dt_bias, v_a_log, v_d_skip, v_ssd_norm_w, v_attn_sinks, v_w_out, v_ln_g, v_ln_b):
    me = _index(*_position())
    w_in_all, w_out_all, conv_w_all = _all_gather([w_in[0].astype(BF16), w_out[0].astype(BF16), conv_w[0]])
    w_ssd, w_att, w_out_f, conv_w_f = _full_weights(w_in_all, w_out_all, conv_w_all)
    dx, dw_in, dw_out, small = _local_step(
        x[0], positions[0], loss_target[0], w_ssd, w_att, w_out_f,
        conv_w_f, conv_b, dt_bias, a_log, d_skip, ssd_norm_w, attn_sinks, ln_g, ln_b)

    cols = D_IN_PROJ // N_DEV
    rows = 2 * D_MODEL // N_DEV
    dw_in_parts = jnp.transpose(dw_in.reshape(D_MODEL, N_DEV, cols), (1, 0, 2))
    dw_out_parts = dw_out.reshape(N_DEV, rows, D_MODEL)
    own_in = lax.dynamic_index_in_dim(dw_in_parts, me, axis=0, keepdims=False)
    own_out = lax.dynamic_index_in_dim(dw_out_parts, me, axis=0, keepdims=False)
    recv_in, recv_out, small_all = _grad_exchange([dw_in_parts.astype(BF16), dw_out_parts.astype(BF16)], small)

    g_in, d_in, nm_in, nv_in = _adamw_shard(own_in, recv_in, w_in[0], m_w_in[0], v_w_in[0], rows=256, name="adamw_w_in")
    g_out, d_out, nm_out, nv_out = _adamw_shard(own_out, recv_out, w_out[0], m_w_out[0], v_w_out[0], rows=256,
                                                name="adamw_w_out")
    zero = jnp.zeros((), F32)
    pack = lambda d, cw: _pack_small(zero, d["conv_b"], d["dt_bias"], d["a_log"], d["d_skip"], d["attn_sinks"],
                                     d["ssd_norm_w"], d["ln_g"], d["ln_b"], cw)
    zero_cw = jnp.zeros((4, D_XBC), F32)
    wp = pack(dict(conv_b=conv_b, dt_bias=dt_bias, a_log=a_log, d_skip=d_skip, attn_sinks=attn_sinks,
                   ssd_norm_w=ssd_norm_w, ln_g=ln_g, ln_b=ln_b), zero_cw)
    mp = pack(dict(conv_b=m_conv_b, dt_bias=m_dt_bias, a_log=m_a_log, d_skip=m_d_skip, attn_sinks=m_attn_sinks,
                   ssd_norm_w=m_ssd_norm_w, ln_g=m_ln_g, ln_b=m_ln_b), zero_cw)
    vp = pack(dict(conv_b=v_conv_b, dt_bias=v_dt_bias, a_log=v_a_log, d_skip=v_d_skip, attn_sinks=v_attn_sinks,
                   ssd_norm_w=v_ssd_norm_w, ln_g=v_ln_g, ln_b=v_ln_b), zero_cw)
    gs, ds_, ms_, vs_ = [_unpack_small(p) for p in _adamw_small(small_all, wp, mp, vp)]
    g_cw = lax.dynamic_slice_in_dim(gs["conv_w"], me * (D_XBC // N_DEV), D_XBC // N_DEV, axis=1)
    d_cw, nm_cw, nv_cw = _adamw_plain(g_cw, conv_w[0], m_conv_w[0], v_conv_w[0])

    names = ["conv_b", "dt_bias", "a_log", "d_skip", "ssd_norm_w", "attn_sinks"]

    def leaves(big_in, cw, small_d, big_out):
        return ([big_in[None], cw[None]] + [small_d[k] for k in names] + [big_out[None], small_d["ln_g"], small_d["ln_b"]])

    return (gs["loss"], dx[None],
            *leaves(g_in, g_cw, gs, g_out), *leaves(d_in, d_cw, ds_, d_out),
            *leaves(nm_in, nm_cw, ms_, nm_out), *leaves(nv_in, nv_cw, vs_, nv_out))
```

```python
import jax
import jax.numpy as jnp
from jax import lax
from jax.experimental import pallas as pl
from jax.experimental.pallas import tpu as pltpu
import numpy as np

F32 = jnp.float32
BF16 = jnp.bfloat16
_MXU = jnp.bfloat16

N_DEV = 8
D_MODEL = 1024
D_SSD = 1024
D_ATT = 1024
HEAD_DIM = 64
N_HEADS = 16
SSD_GROUPS = 2
KV_HEADS = 4
CHUNK = 128
D_XBC = 1536
D_IN_PROJ = 5136
ROPE_DIM = 16
ROPE_THETA = 500000.0
ALPHA = (2.0 * 1) ** 0.25
LN_EPS = 1e-5
RMS_EPS = 1e-5
ATT_SCALE = HEAD_DIM ** -0.5
NEG = -1e30

S_Z, S_XS, S_B, S_C, S_DT, S_W = 0, 1024, 2048, 2304, 2560, 2816
N_SSD_REAL = 2576
A_Q, A_K, A_V, A_G, A_W = 0, 1024, 1280, 1536, 2560

ADAM_LR = 0.001
ADAM_B1 = 0.9
ADAM_B2 = 0.999
ADAM_EPS = 1e-08
ADAM_WD = 0.01
ADAM_STEP = 10

P_LOSS, P_CONVB, P_DTB, P_ALOG, P_DSKIP, P_SINK, P_NORMW, P_LNG, P_LNB, P_CONVW, P_END = (
    0, 128, 1664, 1792, 1920, 2048, 2176, 3200, 4224, 5248, 11392)
P_ROWS = 96

VMEM_LIMIT = 48 * 1024 * 1024
MESH = pl.DeviceIdType.MESH


def _params(sem=None):
    return pltpu.CompilerParams(dimension_semantics=sem, vmem_limit_bytes=VMEM_LIMIT)


def _mm(a, b):
    return jnp.dot(a.astype(_MXU), b.astype(_MXU), preferred_element_type=F32)


def _mm_nt(a, b):
    return lax.dot_general(a.astype(_MXU), b.astype(_MXU), (((1,), (1,)), ((), ())),
                           preferred_element_type=F32)


def _mm_tn(a, b):
    return lax.dot_general(a.astype(_MXU), b.astype(_MXU), (((0,), (0,)), ((), ())),
                           preferred_element_type=F32)


def _split3(v):
    hi = v.astype(BF16)
    r = v - hi.astype(F32)
    mid = r.astype(BF16)
    lo = (r - mid.astype(F32)).astype(BF16)
    return hi, mid, lo


def _mm_exact_r(v, p01):
    hi, mid, lo = _split3(v)
    d = lambda a: jnp.dot(a, p01, preferred_element_type=F32)
    return d(hi) + d(mid) + d(lo)


def _mm_exact_l(p01, v):
    hi, mid, lo = _split3(v)
    d = lambda a: jnp.dot(p01, a, preferred_element_type=F32)
    return d(hi) + d(mid) + d(lo)


def _sigmoid(x):
    return 1.0 / (1.0 + jnp.exp(-x))


def _softplus(x):
    e = jnp.exp(-jnp.abs(x))
    u = 1.0 + e
    log1p = jnp.where(u == 1.0, e, jnp.log(u) * (e / (u - 1.0)))
    return jnp.maximum(x, 0.0) + log1p


def _rows8(rows):
    n = rows[0].shape[1]
    rid = lax.broadcasted_iota(jnp.int32, (8, n), 0)
    out = jnp.zeros((8, n), F32)
    for k, r in enumerate(rows):
        out = out + jnp.where(rid == k, r, 0.0)
    return out


def _colsum(a):
    return jnp.sum(a, axis=0, keepdims=True)


def _matmul(a, b, *, tm, tn, name, addend=None, scale=1.0, b_transposed=False):
    M, K = a.shape
    N = b.shape[0] if b_transposed else b.shape[1]

    def body(*refs):
        if addend is None:
            a_ref, b_ref, o_ref = refs
        else:
            a_ref, b_ref, c_ref, o_ref = refs
        acc = _mm_nt(a_ref[...], b_ref[...]) if b_transposed else _mm(a_ref[...], b_ref[...])
        if addend is not None:
            acc = acc + scale * c_ref[...]
        o_ref[...] = acc

    b_spec = pl.BlockSpec((tn, K), lambda i, j: (j, 0)) if b_transposed else pl.BlockSpec((K, tn), lambda i, j: (0, j))
    in_specs = [pl.BlockSpec((tm, K), lambda i, j: (i, 0)), b_spec]
    args = [a, b]
    if addend is not None:
        in_specs.append(pl.BlockSpec((tm, tn), lambda i, j: (i, j)))
        args.append(addend)
    return pl.pallas_call(
        body, name=name, grid=(M // tm, N // tn), in_specs=in_specs,
        out_specs=pl.BlockSpec((tm, tn), lambda i, j: (i, j)),
        out_shape=jax.ShapeDtypeStruct((M, N), F32),
        compiler_params=_params(("arbitrary", "arbitrary")),
    )(*args)


def _matmul_tn(a, g, *, tl, tn, name):
    L, M = a.shape
    N = g.shape[1]

    def body(a_ref, g_ref, o_ref):
        @pl.when(pl.program_id(1) == 0)
        def _():
            o_ref[...] = jnp.zeros_like(o_ref)

        o_ref[...] += lax.dot_general(a_ref[...].astype(_MXU), g_ref[...].astype(_MXU),
                                      (((0,), (0,)), ((), ())), preferred_element_type=F32)

    return pl.pallas_call(
        body, name=name, grid=(N // tn, L // tl),
        in_specs=[pl.BlockSpec((tl, M), lambda j, l: (l, 0)), pl.BlockSpec((tl, tn), lambda j, l: (l, j))],
        out_specs=pl.BlockSpec((M, tn), lambda j, l: (0, j)),
        out_shape=jax.ShapeDtypeStruct((M, N), F32),
        compiler_params=_params(("arbitrary", "arbitrary")),
    )(a, g)


def _position():
    return lax.axis_index("x"), lax.axis_index("y"), lax.axis_index("c")


def _index(px, py, pc):
    return 4 * px + 2 * py + pc


def _flip(pos, k):
    x, y, c = pos
    return ((1 - x) if (k >> 2) & 1 else x, (1 - y) if (k >> 1) & 1 else y, (1 - c) if k & 1 else c)


def _when(cond, fn):
    if cond is True:
        fn()
    else:
        pl.when(cond)(fn)


def _remote(src, dst, send_sem, recv_sem, peer):
    return pltpu.make_async_remote_copy(src_ref=src, dst_ref=dst, send_sem=send_sem, recv_sem=recv_sem,
                                        device_id=peer, device_id_type=MESH)


class _Flow:
    def __init__(self, kind, operand, result, target_x=None):
        self.kind, self.operand, self.result, self.target_x = kind, operand, result, target_x


class _Hosted:
    def __init__(self, operands, out_shapes, flows, aliases=None):
        self.operands, self.out_shapes, self.flows = operands, out_shapes, flows
        self.aliases = aliases or {}

    def plan(self, ins, outs, send_sems, recv_sems, local_sems):
        me = _position()
        mi = _index(*me)
        sends, recvs, locals_ = [], [], []
        for row, f in enumerate(self.flows):
            src, dst = ins[f.operand], outs[f.result]
            for k in range(1, N_DEV):
                peer = _flip(me, k)
                sems = (send_sems.at[row, k - 1], recv_sems.at[row, k - 1])
                if f.kind == "exchange":
                    owner = _index(*peer) if f.target_x is None else 2 * peer[1] + peer[2]
                    cp = _remote(src.at[owner], dst.at[k - 1], *sems, peer)
                    to_peer = True if f.target_x is None else peer[0] == f.target_x
                    to_me = True if f.target_x is None else me[0] == f.target_x
                    sends.append((to_peer, cp))
                    recvs.append((to_me, cp))
                else:
                    sends.append((True, _remote(src, dst.at[mi], *sems, peer)))
                    recvs.append((True, _remote(src, dst.at[_index(*peer)], *sems, peer)))
            if f.kind == "gather":
                locals_.append(pltpu.make_async_copy(src, dst.at[mi], local_sems.at[row]))

        def start():
            for cp in locals_:
                cp.start()
            for cond, cp in sends:
                _when(cond, cp.start)

        def wait():
            for cond, cp in recvs:
                _when(cond, cp.wait_recv)
            for cond, cp in sends:
                _when(cond, cp.wait_send)
            for cp in locals_:
                cp.wait()

        return start, wait


def _call(body, comm, *, name, grid, in_specs, out_specs, out_shape, scratch_shapes, args, aliases=None):
    io_alias = dict(aliases or {})
    if comm is None:
        return pl.pallas_call(body, name=name, grid=grid, in_specs=in_specs, out_specs=out_specs, out_shape=out_shape,
                              scratch_shapes=scratch_shapes, input_output_aliases=io_alias,
                              compiler_params=_params(("arbitrary",)))(*args)
    n_in, n_out, n_scr = len(args), len(out_shape), len(scratch_shapes)
    c_in, c_out, rows = len(comm.operands), len(comm.out_shapes), len(comm.flows)

    def hosted(*refs):
        ins, refs = refs[:n_in], refs[n_in:]
        cins, refs = refs[:c_in], refs[c_in:]
        outs, refs = refs[:n_out], refs[n_out:]
        couts, refs = refs[:c_out], refs[c_out:]
        scr, (send_sems, recv_sems, local_sems) = refs[:n_scr], refs[n_scr:]
        start, wait = comm.plan(cins, couts, send_sems, recv_sems, local_sems)
        pl.when(pl.program_id(0) == 0)(start)
        body(*ins, *outs, *scr)
        pl.when(pl.program_id(0) == grid[0] - 1)(wait)

    for ci, co in comm.aliases.items():
        io_alias[n_in + ci] = n_out + co
    any_spec = pl.BlockSpec(memory_space=pl.ANY)
    sems = [pltpu.SemaphoreType.DMA((rows, N_DEV - 1)), pltpu.SemaphoreType.DMA((rows, N_DEV - 1)),
            pltpu.SemaphoreType.DMA((rows,))]
    return pl.pallas_call(
        hosted, name=name, grid=grid, in_specs=list(in_specs) + [any_spec] * c_in,
        out_specs=list(out_specs) + [any_spec] * c_out, out_shape=list(out_shape) + list(comm.out_shapes),
        scratch_shapes=list(scratch_shapes) + sems, input_output_aliases=io_alias,
        compiler_params=_params(("arbitrary",)))(*args, *comm.operands)


def _ssd_recompute(first, p_ref, halo_ref, cw_ref, cb_ref, dtb_ref, alog_ref, e_ref, ext_scr):
    ext_scr[0:8, :] = jnp.where(first, 0.0, halo_ref[:, S_XS:S_DT])
    ext_scr[8:136, :] = p_ref[:, S_XS:S_DT]
    cw = cw_ref[...]
    pre = (cb_ref[0:1, :] + cw[3:4, :] * ext_scr[8:136, :] + cw[2:3, :] * ext_scr[7:135, :]
           + cw[1:2, :] * ext_scr[6:134, :] + cw[0:1, :] * ext_scr[5:133, :])
    sg = _sigmoid(pre)
    act = pre * sg
    lane = lax.broadcasted_iota(jnp.int32, (1, 128), 1)
    A = jnp.where(lane < N_HEADS, -jnp.exp(alog_ref[0:1, :]), 0.0)
    raw = p_ref[:, S_DT:S_DT + 128] + dtb_ref[0:1, :]
    dt = _softplus(raw)
    dA = dt * A
    row = lax.broadcasted_iota(jnp.int32, (128, 128), 0)
    col = lax.broadcasted_iota(jnp.int32, (128, 128), 1)
    tril = (row >= col).astype(BF16)
    acs = _mm_exact_l(tril, dA)
    last = acs[127:128, :]
    ds = jnp.exp(last - acs)
    eo = jnp.exp(acs)
    E = e_ref[...]
    ex = _mm_exact_r(jnp.concatenate([dt, ds, eo], axis=0), E)
    dt_e, ds_e, eo_e = ex[0:128], ex[128:256], ex[256:384]
    xs_c = act[:, 0:1024]
    X = xs_c * dt_e
    return dict(pre=pre, sg=sg, xs_c=xs_c, Bc=act[:, 1024:1280], Cc=act[:, 1280:1536], A=A, raw=raw, dt=dt,
                acs=acs, acsT=acs.T, eo_e=eo_e, ds_e=ds_e, dt_e=dt_e, cd_e=eo_e[127:128, :],
                X=X, Xd=X * ds_e, row=row, col=col)


def _split_halves(t):
    lo = _lo_half(CHUNK)
    return jnp.concatenate([jnp.where(lo, t, 0.0), jnp.where(lo, 0.0, t)], axis=0)


def _ssd_core(R, hprev):
    causal = R["row"] >= R["col"]
    acs, acsT, X = R["acs"], R["acsT"], R["X"]
    ydiag, yoff, snew = [], [], []
    for g in range(SSD_GROUPS):
        Bg = R["Bc"][:, g * 128:(g + 1) * 128]
        Cg = R["Cc"][:, g * 128:(g + 1) * 128]
        cols = slice(g * 512, (g + 1) * 512)
        CB = _mm_nt(Cg, Bg)
        snew.append(_mm_tn(Bg, R["Xd"][:, cols]))
        yoff.append(_mm(Cg, hprev[:, cols]))
        for j in range(4):
            h0 = g * 8 + 2 * j
            ms = [CB * jnp.exp(jnp.where(causal, acs[:, h:h + 1] - acsT[h:h + 1, :], NEG)) for h in (h0, h0 + 1)]
            ydiag.append(_mm(jnp.concatenate(ms, axis=1), _split_halves(X[:, h0 * HEAD_DIM:h0 * HEAD_DIM + 128])))
    Y = jnp.concatenate(ydiag, axis=1) + jnp.concatenate(yoff, axis=1) * R["eo_e"]
    return Y, jnp.concatenate(snew, axis=1)


def _ssd_forward(proj_ssd, conv_w8, conv_b8, dtb8, alog8, dskip_e, norm_w, E, comm=None):
    L = proj_ssd.shape[0]
    nc = L // CHUNK

    def body(p_ref, halo_ref, cw_ref, cb_ref, dtb_ref, alog_ref, dsk_ref, nw_ref, e_ref,
             y_ref, ypre_ref, hprev_ref, h_scr, ext_scr):
        c = pl.program_id(0)
        first = c == 0

        @pl.when(first)
        def _():
            h_scr[...] = jnp.zeros_like(h_scr)

        R = _ssd_recompute(first, p_ref, halo_ref, cw_ref, cb_ref, dtb_ref, alog_ref, e_ref, ext_scr)
        hprev = h_scr[...]
        hprev_ref[...] = hprev
        Y, snew = _ssd_core(R, hprev)
        h_scr[...] = hprev * R["cd_e"] + snew
        Y = Y + dsk_ref[0:1, :] * R["xs_c"]
        ypre_ref[...] = Y
        z = p_ref[:, S_Z:S_Z + 1024]
        yf = Y * (z * _sigmoid(z))
        outs = []
        for g in range(SSD_GROUPS):
            yg = yf[:, g * 512:(g + 1) * 512]
            r = lax.rsqrt(jnp.mean(yg * yg, axis=-1, keepdims=True) + RMS_EPS)
            outs.append(yg * r)
        y_ref[...] = (jnp.concatenate(outs, axis=1) * nw_ref[0:1, :]).astype(y_ref.dtype)

    const = lambda shape: pl.BlockSpec(shape, lambda c: (0, 0))
    return _call(
        body, comm, name="ssd_fwd", grid=(nc,),
        in_specs=[pl.BlockSpec((CHUNK, S_W), lambda c: (c, 0)),
                  pl.BlockSpec((8, S_W), lambda c: (jnp.maximum(c * 16 - 1, 0), 0)),
                  const((8, D_XBC)), const((8, D_XBC)), const((8, 128)), const((8, 128)),
                  const((8, 1024)), const((8, 1024)), const((128, 1024))],
        out_specs=[pl.BlockSpec((CHUNK, D_SSD), lambda c: (c, 0)), pl.BlockSpec((CHUNK, D_SSD), lambda c: (c, 0)),
                   pl.BlockSpec((128, 1024), lambda c: (c, 0))],
        out_shape=[jax.ShapeDtypeStruct((L, D_SSD + D_ATT), _MXU), jax.ShapeDtypeStruct((L, D_SSD), F32),
                   jax.ShapeDtypeStruct((nc * 128, 1024), F32)],
        scratch_shapes=[pltpu.VMEM((128, 1024), F32), pltpu.VMEM((136, D_XBC), F32)],
        args=(proj_ssd, proj_ssd, conv_w8, conv_b8, dtb8, alog8, dskip_e, norm_w, E))


def _ssd_backward(proj_ssd, hprev_all, ypre, dy, conv_w8, conv_b8, dtb8, alog8, dskip_e, norm_w, E, ET, comm=None):
    L = proj_ssd.shape[0]
    nc = L // CHUNK

    def body(p_ref, halo_ref, hprev_ref, ypre_ref, dy_ref, cw_ref, cb_ref, dtb_ref, alog_ref, dsk_ref, nw_ref, e_ref,
             et_ref, dp_ref, acc_cw_ref, acc_w_ref, acc_s_ref, dh_scr, ext_scr, ext2_scr, nxt_scr):
        i = pl.program_id(0)
        c = nc - 1 - i
        first = c == 0

        @pl.when(i == 0)
        def _():
            dh_scr[...] = jnp.zeros_like(dh_scr)
            nxt_scr[...] = jnp.zeros_like(nxt_scr)
            acc_cw_ref[...] = jnp.zeros_like(acc_cw_ref)
            acc_w_ref[...] = jnp.zeros_like(acc_w_ref)
            acc_s_ref[...] = jnp.zeros_like(acc_s_ref)

        R = _ssd_recompute(first, p_ref, halo_ref, cw_ref, cb_ref, dtb_ref, alog_ref, e_ref, ext_scr)
        hprev = hprev_ref[...]
        xs_c, X, Xd = R["xs_c"], R["X"], R["Xd"]
        acs, acsT = R["acs"], R["acsT"]
        ET = et_ref[...]
        dsk = dsk_ref[0:1, :]
        Y = ypre_ref[...]

        z = p_ref[:, S_Z:S_Z + 1024]
        sz = _sigmoid(z)
        silz = z * sz
        yf = Y * silz
        dyv = dy_ref[...]
        nw = nw_ref[0:1, :]
        dyf_parts, dnw_parts = [], []
        for g in range(SSD_GROUPS):
            cols = slice(g * 512, (g + 1) * 512)
            yg = yf[:, cols]
            r = lax.rsqrt(jnp.mean(yg * yg, axis=-1, keepdims=True) + RMS_EPS)
            yn = yg * r
            dyn = dyv[:, cols] * nw[:, cols]
            dnw_parts.append(_colsum(dyv[:, cols] * yn))
            dyf_parts.append(r * (dyn - yn * jnp.mean(dyn * yn, axis=-1, keepdims=True)))
        dyf = jnp.concatenate(dyf_parts, axis=1)
        dY = dyf * silz
        dz = dyf * Y * (sz * (1.0 + z * (1.0 - sz)))

        dhn = dh_scr[...]
        dYo = dY * R["eo_e"]
        causal = R["row"] >= R["col"]
        causal_t = R["col"] >= R["row"]
        dacs = jnp.zeros((128, 128), F32)
        dacs_t = jnp.zeros((128, 128), F32)
        dxdiag, dxd, dhprev, dBs, dCs, yoff = [], [], [], [], [], []
        for g in range(SSD_GROUPS):
            Bg = R["Bc"][:, g * 128:(g + 1) * 128]
            Cg = R["Cc"][:, g * 128:(g + 1) * 128]
            cols = slice(g * 512, (g + 1) * 512)
            CB = _mm_nt(Cg, Bg)
            CBt = CB.T
            dCB = jnp.zeros((128, 128), F32)
            for j in range(4):
                h0 = g * 8 + 2 * j
                pc = slice(h0 * HEAD_DIM, h0 * HEAD_DIM + 128)
                dYst = _split_halves(dY[:, pc])
                dMst = _mm_nt(dYst, X[:, pc])
                mts = []
                for a, h in enumerate((h0, h0 + 1)):
                    acol = acs[:, h:h + 1]
                    arow = acsT[h:h + 1, :]
                    dL = dMst[a * 128:(a + 1) * 128] * jnp.exp(jnp.where(causal, acol - arow, NEG))
                    dCB = dCB + dL
                    G = dL * CB
                    dacs = dacs + jnp.where(R["col"] == h, jnp.sum(G, axis=1, keepdims=True), 0.0)
                    dacs_t = dacs_t + jnp.where(R["row"] == h, jnp.sum(G, axis=0, keepdims=True), 0.0)
                    mts.append(CBt * jnp.exp(jnp.where(causal_t, arow - acol, NEG)))
                dxdiag.append(_mm(jnp.concatenate(mts, axis=1), dYst))
            dS = dhn[:, cols]
            dxd.append(_mm(Bg, dS))
            yoff.append(_mm(Cg, hprev[:, cols]))
            dhprev.append(_mm_tn(Cg, dYo[:, cols]))
            dCs.append(_mm_nt(dYo[:, cols], hprev[:, cols]) + _mm(dCB, Bg))
            dBs.append(_mm_tn(dCB, Cg) + _mm_nt(Xd[:, cols], dS))
        Yoff = jnp.concatenate(yoff, axis=1) * R["eo_e"]
        dXd = jnp.concatenate(dxd, axis=1)
        dX = jnp.concatenate(dxdiag, axis=1) + dXd * R["ds_e"]
        t_state = dXd * Xd
        hs = _mm_exact_r(jnp.concatenate([dY * Yoff - t_state, dX * xs_c], axis=0), ET)
        dacs = dacs + hs[0:128] - dacs_t.T
        v_last = _colsum(t_state + dhn * hprev * R["cd_e"])
        dlast = _mm_exact_r(jnp.broadcast_to(v_last, (8, 1024)), ET)[0:1, :]
        dacs = dacs + jnp.where(R["row"] == 127, dlast, 0.0)
        triu = (R["col"] >= R["row"]).astype(BF16)
        da = _mm_exact_l(triu, dacs)
        ddt = da * R["A"] + hs[128:256]
        ddt_raw = ddt * _sigmoid(R["raw"])
        dxs_c = dX * R["dt_e"] + dY * dsk
        dh_scr[...] = jnp.concatenate(dhprev, axis=1) + dhn * R["cd_e"]

        dact = jnp.concatenate([dxs_c] + dBs + dCs, axis=1)
        pre, sg = R["pre"], R["sg"]
        dpre = dact * (sg * (1.0 + pre * (1.0 - sg)))
        ext2_scr[0:128, :] = dpre
        ext2_scr[128:136, :] = nxt_scr[...]
        nxt_scr[...] = dpre[0:8, :]
        cw = cw_ref[...]
        dxbc = (cw[3:4, :] * dpre + cw[2:3, :] * ext2_scr[1:129, :] + cw[1:2, :] * ext2_scr[2:130, :]
                + cw[0:1, :] * ext2_scr[3:131, :])
        acc_cw_ref[...] += _rows8([_colsum(dpre * ext_scr[5 + k:133 + k, :]) for k in range(4)] + [_colsum(dpre)])
        acc_w_ref[...] += _rows8([jnp.concatenate(dnw_parts, axis=1), _colsum(dY * xs_c)])
        acc_s_ref[...] += _rows8([_colsum(ddt_raw), _colsum(da * R["dt"])])

        lane = lax.broadcasted_iota(jnp.int32, (128, 128), 1)
        dp_ref[:, S_Z:S_Z + 1024] = dz
        dp_ref[:, S_XS:S_DT] = dxbc
        dp_ref[:, S_DT:S_DT + 128] = jnp.where(lane < N_HEADS, ddt_raw, 0.0)
        dp_ref[:, S_DT + 128:S_W] = jnp.zeros((128, 128), F32)

        @pl.when(i == nc - 1)
        def _():
            acc = acc_s_ref[...]
            dskip = _mm_exact_r(acc_w_ref[...], ET)[1:2, :]
            acc_s_ref[...] = _rows8([acc[0:1, :], acc[1:2, :] * R["A"], dskip])

    const = lambda shape: pl.BlockSpec(shape, lambda i: (0, 0))
    rev = lambda i: (nc - 1 - i, 0)
    return _call(
        body, comm, name="ssd_bwd", grid=(nc,),
        in_specs=[pl.BlockSpec((CHUNK, S_W), rev),
                  pl.BlockSpec((8, S_W), lambda i: (jnp.maximum((nc - 1 - i) * 16 - 1, 0), 0)),
                  pl.BlockSpec((128, 1024), rev),
                  pl.BlockSpec((CHUNK, D_SSD), rev),
                  pl.BlockSpec((CHUNK, D_SSD), rev),
                  const((8, D_XBC)), const((8, D_XBC)), const((8, 128)), const((8, 128)),
                  const((8, 1024)), const((8, 1024)), const((128, 1024)), const((1024, 128))],
        out_specs=[pl.BlockSpec((CHUNK, S_W), rev), const((8, D_XBC)), const((8, 1024)), const((8, 128))],
        out_shape=[jax.ShapeDtypeStruct((L, S_W), F32), jax.ShapeDtypeStruct((8, D_XBC), F32),
                   jax.ShapeDtypeStruct((8, 1024), F32), jax.ShapeDtypeStruct((8, 128), F32)],
        scratch_shapes=[pltpu.VMEM((128, 1024), F32), pltpu.VMEM((136, D_XBC), F32),
                        pltpu.VMEM((136, D_XBC), F32), pltpu.VMEM((8, D_XBC), F32)],
        args=(proj_ssd, proj_ssd, hprev_all, ypre, dy, conv_w8, conv_b8, dtb8, alog8, dskip_e, norm_w, E, ET))


def _rope(t, tab):
    cos, sa, sb = tab[:, 0:128], tab[:, 128:256], tab[:, 256:384]
    outs = []
    for i in range(t.shape[1] // 128):
        tg = t[:, i * 128:(i + 1) * 128]
        outs.append(tg * cos + pltpu.roll(tg, 8, 1) * sa + pltpu.roll(tg, 120, 1) * sb)
    return jnp.concatenate(outs, axis=1)


def _rope_transposed(d, tab):
    cos, sa, sb = tab[:, 0:128], tab[:, 128:256], tab[:, 256:384]
    outs = []
    for i in range(d.shape[1] // 128):
        dg = d[:, i * 128:(i + 1) * 128]
        outs.append(dg * cos + pltpu.roll(dg * sa, 120, 1) + pltpu.roll(dg * sb, 8, 1))
    return jnp.concatenate(outs, axis=1)


def _lo_half(rows):
    return lax.broadcasted_iota(jnp.int32, (rows, 128), 1) < HEAD_DIM


def _kv_both(t, j):
    p, b = j // 2, j % 2
    lo = _lo_half(2 * CHUNK)
    nat = jnp.where(lo if b == 0 else jnp.logical_not(lo), t[:, p * 128:(p + 1) * 128], 0.0)
    return nat + pltpu.roll(nat, HEAD_DIM, 1)


def _stack_heads(t, j):
    lo = _lo_half(CHUNK)
    hi = jnp.logical_not(lo)
    a, b = t[:, 2 * j * 128:(2 * j + 1) * 128], t[:, (2 * j + 1) * 128:(2 * j + 2) * 128]
    return jnp.concatenate([jnp.where(lo, a, 0.0), jnp.where(hi, a, 0.0),
                            jnp.where(lo, b, 0.0), jnp.where(hi, b, 0.0)], axis=0)


def _unstack_heads(s):
    lo = _lo_half(CHUNK)
    return jnp.concatenate([jnp.where(lo, s[0:128], s[128:256]), jnp.where(lo, s[256:384], s[384:512])], axis=1)


def _fold_kv(r, j):
    lo = _lo_half(2 * CHUNK)
    return jnp.where(lo if j % 2 == 0 else jnp.logical_not(lo), r + pltpu.roll(r, HEAD_DIM, 1), 0.0)


def _sink_column(sink_ref, j):
    rid = lax.broadcasted_iota(jnp.int32, (4 * CHUNK, 1), 0) // CHUNK
    col = jnp.zeros((4 * CHUNK, 1), F32)
    for hh in range(4):
        col = jnp.where(rid == hh, sink_ref[4 * j + hh], col)
    return col


def _band_mask(blk):
    qi = lax.broadcasted_iota(jnp.int32, (4 * CHUNK, 2 * CHUNK), 0) % CHUNK
    si = lax.broadcasted_iota(jnp.int32, (4 * CHUNK, 2 * CHUNK), 1)
    return (si > qi) & (si <= qi + CHUNK) & ((blk > 0) | (si >= CHUNK))


def _softmax_sink(s, valid, sink):
    s = jnp.where(valid, s, NEG)
    mx = jnp.maximum(jnp.max(s, axis=-1, keepdims=True), sink)
    p = jnp.exp(s - mx)
    esink = jnp.exp(sink - mx)
    den = jnp.sum(p, axis=-1, keepdims=True) + esink
    return p / den, esink / den


def _swa_forward(proj_att, tabs, sinks, y):
    L = proj_att.shape[0]
    nb = L // CHUNK

    def body(sink_ref, p_ref, prev_ref, tab_ref, ptab_ref, y_in_ref, y_ref):
        n = pl.program_id(0)
        tab = tab_ref[...]
        qr = _rope(p_ref[:, A_Q:A_Q + 1024], tab)
        k_cur = _rope(p_ref[:, A_K:A_K + 256], tab)
        k_prev = _rope(prev_ref[:, 0:256], ptab_ref[...])
        kk = jnp.concatenate([k_prev, k_cur], axis=0)
        vv = jnp.concatenate([prev_ref[:, 256:512], p_ref[:, A_V:A_V + 256]], axis=0)
        valid = _band_mask(n)
        outs = []
        for j in range(KV_HEADS):
            s = _mm_nt(_stack_heads(qr, j), _kv_both(kk, j)) * ATT_SCALE
            P, _ = _softmax_sink(s, valid, _sink_column(sink_ref, j))
            outs.append(_unstack_heads(_mm(P, _kv_both(vv, j))))
        g = p_ref[:, A_G:A_G + 1024]
        y_ref[...] = (jnp.concatenate(outs, axis=1) * (g * _sigmoid(g))).astype(y_ref.dtype)

    return pl.pallas_call(
        body, name="swa_fwd", grid=(nb,),
        in_specs=[pl.BlockSpec(memory_space=pltpu.SMEM),
                  pl.BlockSpec((CHUNK, A_W), lambda n: (n, 0)),
                  pl.BlockSpec((CHUNK, 512), lambda n: (jnp.maximum(n - 1, 0), 2)),
                  pl.BlockSpec((CHUNK, 384), lambda n: (n, 0)),
                  pl.BlockSpec((CHUNK, 384), lambda n: (jnp.maximum(n - 1, 0), 0)),
                  pl.BlockSpec(memory_space=pl.ANY)],
        out_specs=pl.BlockSpec((CHUNK, D_ATT), lambda n: (n, 1)),
        out_shape=jax.ShapeDtypeStruct(y.shape, y.dtype),
        input_output_aliases={5: 0},
        compiler_params=_params(("arbitrary",)),
    )(sinks, proj_att, proj_att, tabs, tabs, y)


def _swa_backward(proj_att, tabs, sinks, dy, comm=None):
    L = proj_att.shape[0]
    nb = L // CHUNK

    def body(sink_ref, p_ref, prev_ref, tab_ref, ptab_ref, dy_ref, dp_ref, dsink_ref, carry_k, carry_v):
        i = pl.program_id(0)
        n = nb - 1 - i

        @pl.when(i == 0)
        def _():
            carry_k[...] = jnp.zeros_like(carry_k)
            carry_v[...] = jnp.zeros_like(carry_v)
            dsink_ref[...] = jnp.zeros_like(dsink_ref)

        tab = tab_ref[...]
        qr = _rope(p_ref[:, A_Q:A_Q + 1024], tab)
        k_cur = _rope(p_ref[:, A_K:A_K + 256], tab)
        k_prev = _rope(prev_ref[:, 0:256], ptab_ref[...])
        kk = jnp.concatenate([k_prev, k_cur], axis=0)
        vv = jnp.concatenate([prev_ref[:, 256:512], p_ref[:, A_V:A_V + 256]], axis=0)
        valid = _band_mask(n)
        g = p_ref[:, A_G:A_G + 1024]
        sgm = _sigmoid(g)
        dyv = dy_ref[...]
        do_all = dyv * (g * sgm)
        lane8 = lax.broadcasted_iota(jnp.int32, (8, 128), 1)
        rid = lax.broadcasted_iota(jnp.int32, (4 * CHUNK, 1), 0) // CHUNK
        o_parts, dq_parts = [], []
        dk_nat = [jnp.zeros((2 * CHUNK, 128), F32) for _ in range(2)]
        dv_nat = [jnp.zeros((2 * CHUNK, 128), F32) for _ in range(2)]
        dsink = jnp.zeros((8, 128), F32)
        for j in range(KV_HEADS):
            qs = _stack_heads(qr, j)
            kkb, vvb = _kv_both(kk, j), _kv_both(vv, j)
            s = _mm_nt(qs, kkb) * ATT_SCALE
            P, psink = _softmax_sink(s, valid, _sink_column(sink_ref, j))
            o_s = _mm(P, vvb)
            o_parts.append(_unstack_heads(o_s))
            do_s = _stack_heads(do_all, j)
            D = jnp.sum(do_s * o_s, axis=-1, keepdims=True)
            dS = P * (_mm_nt(do_s, vvb) - D)
            sd = psink * D
            for hh in range(4):
                dsink = dsink + jnp.where(lane8 == 4 * j + hh, -jnp.sum(jnp.where(rid == hh, sd, 0.0)), 0.0)
            dq_parts.append(_unstack_heads(_mm(dS, kkb)) * ATT_SCALE)
            dk_nat[j // 2] = dk_nat[j // 2] + _fold_kv(_mm_tn(dS, qs), j) * ATT_SCALE
            dv_nat[j // 2] = dv_nat[j // 2] + _fold_kv(_mm_tn(P, do_s), j)
        o = jnp.concatenate(o_parts, axis=1)
        dkk = jnp.concatenate(dk_nat, axis=1)
        dvv = jnp.concatenate(dv_nat, axis=1)
        dp_ref[:, A_Q:A_Q + 1024] = _rope_transposed(jnp.concatenate(dq_parts, axis=1), tab)
        dp_ref[:, A_K:A_K + 256] = _rope_transposed(dkk[CHUNK:2 * CHUNK] + carry_k[...], tab)
        dp_ref[:, A_V:A_V + 256] = dvv[CHUNK:2 * CHUNK] + carry_v[...]
        dp_ref[:, A_G:A_G + 1024] = dyv * o * (sgm * (1.0 + g * (1.0 - sgm)))
        carry_k[...] = dkk[0:CHUNK]
        carry_v[...] = dvv[0:CHUNK]
        dsink_ref[...] += dsink

    rev = lambda i: (nb - 1 - i, 0)
    prev = lambda i: jnp.maximum(nb - 2 - i, 0)
    return _call(
        body, comm, name="swa_bwd", grid=(nb,),
        in_specs=[pl.BlockSpec(memory_space=pltpu.SMEM),
                  pl.BlockSpec((CHUNK, A_W), rev),
                  pl.BlockSpec((CHUNK, 512), lambda i: (prev(i), 2)),
                  pl.BlockSpec((CHUNK, 384), rev),
                  pl.BlockSpec((CHUNK, 384), lambda i: (prev(i), 0)),
                  pl.BlockSpec((CHUNK, D_ATT), lambda i: (nb - 1 - i, 1))],
        out_specs=[pl.BlockSpec((CHUNK, A_W), rev), pl.BlockSpec((8, 128), lambda i: (0, 0))],
        out_shape=[jax.ShapeDtypeStruct((L, A_W), F32), jax.ShapeDtypeStruct((8, 128), F32)],
        scratch_shapes=[pltpu.VMEM((CHUNK, 256), F32), pltpu.VMEM((CHUNK, 256), F32)],
        args=(sinks, proj_att, proj_att, tabs, tabs, dy))


def _head(y, x, target, w_out, ln_g8, ln_b8, *, tm):
    L = x.shape[0]
    nsteps = L // tm

    def body(y_ref, x_ref, t_ref, wo_ref, g_ref, b_ref, dr_ref, dy_ref, acc_ref):
        i = pl.program_id(0)

        @pl.when(i == 0)
        def _():
            acc_ref[...] = jnp.zeros_like(acc_ref)

        r = ALPHA * x_ref[...] + _mm(y_ref[...], wo_ref[...])
        mu = jnp.mean(r, axis=-1, keepdims=True)
        d = r - mu
        rstd = lax.rsqrt(jnp.mean(d * d, axis=-1, keepdims=True) + LN_EPS)
        xh = d * rstd
        gam = g_ref[0:1, :]
        e = xh * gam + b_ref[0:1, :] - t_ref[...]
        dout = e * (1.0 / D_MODEL)
        dxh = dout * gam
        dr = rstd * (dxh - jnp.mean(dxh, axis=-1, keepdims=True)
                     - xh * jnp.mean(dxh * xh, axis=-1, keepdims=True))
        dr_ref[...] = dr
        dy_ref[...] = _mm_nt(dr, wo_ref[...])
        acc_ref[...] += _rows8([_colsum(dout * xh), _colsum(dout), _colsum(e * e) * (0.5 / D_MODEL)])

        @pl.when(i == nsteps - 1)
        def _():
            acc = acc_ref[...]
            tot = jnp.sum(acc[2:3, :])
            rid = lax.broadcasted_iota(jnp.int32, (8, 1024), 0)
            acc_ref[...] = jnp.where(rid == 3, tot, acc)

    const = lambda shape: pl.BlockSpec(shape, lambda i: (0, 0))
    row = lambda w: pl.BlockSpec((tm, w), lambda i: (i, 0))
    return pl.pallas_call(
        body, name="head", grid=(nsteps,),
        in_specs=[row(2048), row(1024), row(1024), const((2048, 1024)), const((8, 1024)), const((8, 1024))],
        out_specs=[row(1024), row(2048), const((8, 1024))],
        out_shape=[jax.ShapeDtypeStruct((L, D_MODEL), F32), jax.ShapeDtypeStruct((L, 2048), F32),
                   jax.ShapeDtypeStruct((8, 1024), F32)],
        compiler_params=_params(("arbitrary",)),
    )(y, x, target, w_out, ln_g8, ln_b8)


def _all_gather(shards):
    n = len(shards)
    any_spec = pl.BlockSpec(memory_space=pl.ANY)

    def body(*refs):
        ins, outs = refs[:n], refs[n:2 * n]
        send_sems, recv_sems, local_sems = refs[2 * n:]
        x, y, c = _position()
        me, sibling = (x, y, c), (x, y, 1 - c)
        chips = [(1 - x, y), (x, 1 - y), (1 - x, 1 - y)]

        def copy(a, k, block, to, src=None):
            slot = outs[a].at[_index(*block)]
            return pltpu.make_async_remote_copy(
                src_ref=slot if src is None else src, dst_ref=slot,
                send_sem=send_sems.at[a, k], recv_sem=recv_sems.at[a, k],
                device_id=to, device_id_type=MESH)

        mine = [pltpu.make_async_copy(ins[a], outs[a].at[_index(*me)], local_sems.at[a]) for a in range(n)]
        for cp in mine:
            cp.start()
        first = []
        for a in range(n):
            first.append(copy(a, 0, me, sibling, src=ins[a]))
            first += [copy(a, 1 + j, me, (*chip, c), src=ins[a]) for j, chip in enumerate(chips)]
        for cp in first:
            cp.start()
        passed = []
        for j, chip in enumerate(chips):
            for a in range(n):
                copy(a, 1 + j, (*chip, c), me).wait_recv()
                fwd = copy(a, 4 + j, (*chip, c), sibling)
                fwd.start()
                passed.append(fwd)
        for a in range(n):
            copy(a, 0, sibling, me).wait_recv()
            for j, chip in enumerate(chips):
                copy(a, 4 + j, (*chip, 1 - c), me).wait_recv()
        for cp in first + passed:
            cp.wait_send()
        for cp in mine:
            cp.wait()

    return pl.pallas_call(
        body, name="weight_all_gather",
        in_specs=[any_spec] * n, out_specs=[any_spec] * n,
        out_shape=[jax.ShapeDtypeStruct((N_DEV,) + s.shape, s.dtype) for s in shards],
        scratch_shapes=[pltpu.SemaphoreType.DMA((n, 7)), pltpu.SemaphoreType.DMA((n, 7)),
                        pltpu.SemaphoreType.DMA((n,))],
    )(*shards)


def _input_gradient(d_ssd, d_att, w_ssd, w_att, dr, *, tm, comm=None):
    L = dr.shape[0]

    def body(ds_ref, da_ref, ws_ref, wa_ref, dr_ref, o_ref):
        o_ref[...] = ALPHA * dr_ref[...] + _mm_nt(ds_ref[...], ws_ref[...]) + _mm_nt(da_ref[...], wa_ref[...])

    row = lambda w: pl.BlockSpec((tm, w), lambda i: (i, 0))
    const = lambda shape: pl.BlockSpec(shape, lambda i: (0, 0))
    return _call(body, comm, name="dx", grid=(L // tm,),
                 in_specs=[row(S_W), row(A_W), const((D_MODEL, S_W)), const((D_MODEL, A_W)), row(D_MODEL)],
                 out_specs=[row(D_MODEL)], out_shape=[jax.ShapeDtypeStruct((L, D_MODEL), F32)],
                 scratch_shapes=[], args=(d_ssd, d_att, w_ssd, w_att, dr))


SHARD_COLS = D_IN_PROJ // N_DEV
SPLIT = N_SSD_REAL - 4 * SHARD_COLS
RELAYOUT_ROWS = 256


def _unpack_w_in(w_all):
    def body(g_ref, ws_ref, wa_ref):
        for j in range(4):
            ws_ref[:, SHARD_COLS * j:SHARD_COLS * (j + 1)] = g_ref[j]
        ws_ref[:, 4 * SHARD_COLS:N_SSD_REAL] = g_ref[4, :, 0:SPLIT]
        ws_ref[:, N_SSD_REAL:S_W] = jnp.zeros((RELAYOUT_ROWS, S_W - N_SSD_REAL), ws_ref.dtype)
        wa_ref[:, 0:SHARD_COLS - SPLIT] = g_ref[4, :, SPLIT:SHARD_COLS]
        for j in range(5, N_DEV):
            lo = SHARD_COLS * (j - 4) - SPLIT
            wa_ref[:, lo:lo + SHARD_COLS] = g_ref[j]

    return pl.pallas_call(
        body, name="unpack_w_in", grid=(D_MODEL // RELAYOUT_ROWS,),
        in_specs=[pl.BlockSpec((N_DEV, RELAYOUT_ROWS, SHARD_COLS), lambda i: (0, i, 0))],
        out_specs=[pl.BlockSpec((RELAYOUT_ROWS, S_W), lambda i: (i, 0)), pl.BlockSpec((RELAYOUT_ROWS, A_W), lambda i: (i, 0))],
        out_shape=[jax.ShapeDtypeStruct((D_MODEL, S_W), w_all.dtype), jax.ShapeDtypeStruct((D_MODEL, A_W), w_all.dtype)],
        compiler_params=_params(("arbitrary",)),
    )(w_all)


def _pack_dw_in(me1, dw_ssd, dw_att, half):
    def body(me_ref, *refs):
        if half == 0:
            (ds_ref, p_ref, own_ref), da_ref = refs, None
        else:
            ds_ref, da_ref, p_ref, own_ref = refs
        me = me_ref[0]

        @pl.when((me < 4) if half == 1 else (me >= 4))
        def _():
            own_ref[...] = jnp.zeros_like(own_ref)

        for j in range(4):
            if half == 0:
                pieces = [(0, ds_ref[:, SHARD_COLS * j:SHARD_COLS * (j + 1)])]
            elif j == 0:
                pieces = [(0, ds_ref[:, 4 * SHARD_COLS:N_SSD_REAL]), (SPLIT, da_ref[:, 0:SHARD_COLS - SPLIT])]
            else:
                lo = SHARD_COLS * j - SPLIT
                pieces = [(0, da_ref[:, lo:lo + SHARD_COLS])]
            for off, blk in pieces:
                p_ref[j, :, off:off + blk.shape[1]] = blk.astype(p_ref.dtype)

                @pl.when(me == 4 * half + j)
                def _(off=off, blk=blk):
                    own_ref[:, off:off + blk.shape[1]] = blk

    ins = [dw_ssd] if half == 0 else [dw_ssd, dw_att]
    row = lambda a: pl.BlockSpec((RELAYOUT_ROWS, a.shape[1]), lambda i: (i, 0))
    return pl.pallas_call(
        body, name="pack_dw_in_%d" % half, grid=(D_MODEL // RELAYOUT_ROWS,),
        in_specs=[pl.BlockSpec(memory_space=pltpu.SMEM)] + [row(a) for a in ins],
        out_specs=[pl.BlockSpec((4, RELAYOUT_ROWS, SHARD_COLS), lambda i: (0, i, 0)),
                   pl.BlockSpec((RELAYOUT_ROWS, SHARD_COLS), lambda i: (i, 0))],
        out_shape=[jax.ShapeDtypeStruct((4, D_MODEL, SHARD_COLS), BF16), jax.ShapeDtypeStruct((D_MODEL, SHARD_COLS), F32)],
        compiler_params=_params(("arbitrary",)),
    )(me1, *ins)


def _adamw_math(w, g, m, v):
    m = ADAM_B1 * m + (1.0 - ADAM_B1) * g
    v = ADAM_B2 * v + (1.0 - ADAM_B2) * (g * g)
    m_hat = m / (1.0 - ADAM_B1 ** ADAM_STEP)
    v_hat = v / (1.0 - ADAM_B2 ** ADAM_STEP)
    delta = -ADAM_LR * (m_hat / (jnp.sqrt(v_hat) + ADAM_EPS) + ADAM_WD * w)
    return delta, m, v


def _adamw_shard(g_own, recv, w, m, v, *, rows, name):
    R, C = g_own.shape

    def body(g_ref, r_ref, w_ref, m_ref, v_ref, go_ref, d_ref, mo_ref, vo_ref):
        g = g_ref[...]
        for k in range(N_DEV - 1):
            g = g + r_ref[k].astype(F32)
        d, mn, vn = _adamw_math(w_ref[...], g, m_ref[...], v_ref[...])
        go_ref[...] = g
        d_ref[...] = d
        mo_ref[...] = mn
        vo_ref[...] = vn

    blk = pl.BlockSpec((rows, C), lambda i: (i, 0))
    return pl.pallas_call(
        body, name=name, grid=(R // rows,),
        in_specs=[blk, pl.BlockSpec((N_DEV - 1, rows, C), lambda i: (0, i, 0)), blk, blk, blk],
        out_specs=[blk] * 4, out_shape=[jax.ShapeDtypeStruct((R, C), F32)] * 4,
        compiler_params=_params(("arbitrary",)),
    )(g_own, recv, w, m, v)


def _adamw_small(gathered, w, m, v):
    def body(r_ref, w_ref, m_ref, v_ref, go_ref, d_ref, mo_ref, vo_ref):
        g = r_ref[0]
        for k in range(1, N_DEV):
            g = g + r_ref[k]
        d, mn, vn = _adamw_math(w_ref[...], g, m_ref[...], v_ref[...])
        go_ref[...] = g
        d_ref[...] = d
        mo_ref[...] = mn
        vo_ref[...] = vn

    return pl.pallas_call(
        body, name="adamw_small", out_shape=[jax.ShapeDtypeStruct(w.shape, F32)] * 4,
        compiler_params=_params(),
    )(gathered, w, m, v)


def _adamw_plain(g, w, m, v):
    def body(g_ref, w_ref, m_ref, v_ref, d_ref, mo_ref, vo_ref):
        d, mn, vn = _adamw_math(w_ref[...], g_ref[...], m_ref[...], v_ref[...])
        d_ref[...] = d
        mo_ref[...] = mn
        vo_ref[...] = vn

    return pl.pallas_call(
        body, name="adamw_conv_w", out_shape=[jax.ShapeDtypeStruct(w.shape, F32)] * 3,
        compiler_params=_params(),
    )(g, w, m, v)


def _pad_rows8(v):
    v = v.reshape(-1, v.shape[-1])
    return jnp.pad(v, ((0, 8 - v.shape[0]), (0, 0)))


def _pad_lanes(v, n):
    return jnp.pad(v, ((0, 0), (0, n - v.shape[1])))


def _pack_small(loss, conv_b, dt_bias, a_log, d_skip, sinks, norm_w, ln_g, ln_b, conv_w):
    def seg(v, n):
        v = v.reshape(-1).astype(F32)
        return jnp.pad(v, (0, n - v.shape[0]))

    flat = jnp.concatenate([seg(loss, 128), seg(conv_b, 1536), seg(dt_bias, 128), seg(a_log, 128),
                            seg(d_skip, 128), seg(sinks, 128), seg(norm_w, 1024), seg(ln_g, 1024),
                            seg(ln_b, 1024), seg(conv_w, 6144), jnp.zeros((P_ROWS * 128 - P_END,), F32)])
    return flat.reshape(P_ROWS, 128)


def _unpack_small(p):
    f = p.reshape(-1)
    return dict(conv_b=f[P_CONVB:P_CONVB + 1536].reshape(1, 1536), dt_bias=f[P_DTB:P_DTB + 16].reshape(1, 16),
                a_log=f[P_ALOG:P_ALOG + 16].reshape(1, 16), d_skip=f[P_DSKIP:P_DSKIP + 16].reshape(1, 16),
                attn_sinks=f[P_SINK:P_SINK + 16].reshape(1, 16), ssd_norm_w=f[P_NORMW:P_NORMW + 1024].reshape(1, 1024),
                ln_g=f[P_LNG:P_LNG + 1024].reshape(1, 1024), ln_b=f[P_LNB:P_LNB + 1024].reshape(1, 1024),
                conv_w=f[P_CONVW:P_CONVW + 6144].reshape(4, 1536), loss=f[P_LOSS])


def _lane_pattern(fn):
    return np.asarray([fn(l % HEAD_DIM) for l in range(128)], np.float32)


ROPE_INV = _lane_pattern(lambda r: ROPE_THETA ** (-2.0 * (r % 8) / ROPE_DIM) if r < ROPE_DIM else 0.0)
ROPE_SIN_A = _lane_pattern(lambda r: 1.0 if 8 <= r < ROPE_DIM else 0.0)
ROPE_SIN_B = _lane_pattern(lambda r: -1.0 if r < 8 else 0.0)


def _rope_tables(positions):
    ang = positions.astype(F32)[:, None] * ROPE_INV[None, :]
    sn = jnp.sin(ang)
    return jnp.concatenate([jnp.cos(ang), sn * ROPE_SIN_A[None, :], sn * ROPE_SIN_B[None, :]], axis=1)


def _expansion():
    E = np.arange(1024)[None, :] // HEAD_DIM == np.arange(128)[:, None]
    return jnp.asarray(E, BF16), jnp.asarray(E.T, BF16)


def _small_operands(positions, conv_w, conv_b, dt_bias, a_log, d_skip, norm_w, sinks, ln_g, ln_b):
    E, ET = _expansion()
    return dict(conv_w8=_pad_rows8(conv_w), conv_b8=_pad_rows8(conv_b),
                dtb8=_pad_rows8(_pad_lanes(dt_bias, 128)), alog8=_pad_rows8(_pad_lanes(a_log, 128)),
                dskip_e=_pad_rows8(jnp.repeat(d_skip, HEAD_DIM, axis=1)), norm_w8=_pad_rows8(norm_w),
                ln_g8=_pad_rows8(ln_g), ln_b8=_pad_rows8(ln_b), sinks=sinks.reshape(-1),
                tabs=_rope_tables(positions), E=E, ET=ET)


def _ssd_args(s):
    return (s["conv_w8"], s["conv_b8"], s["dtb8"], s["alog8"], s["dskip_e"], s["norm_w8"], s["E"])


def _pack_small_grads(acc_head, acc_cw, acc_w, acc_s, dsink):
    return _pack_small(acc_head[3, 0], acc_cw[4], acc_s[0, :16], acc_s[1, :16], acc_s[2, :16], dsink[0, :16],
                       acc_w[0], acc_head[0], acc_head[1], acc_cw[0:4])


def kernel(x, positions, w_in, conv_w, conv_b, dt_bias, a_log, d_skip, ssd_norm_w, attn_sinks, w_out, ln_g, ln_b, loss_target, m_w_in, m_conv_w, m_conv_b, m_dt_bias, m_a_log, m_d_skip, m_ssd_norm_w, m_attn_sinks, m_w_out, m_ln_g, m_ln_b, v_w_in, v_conv_w, v_conv_b, v_dt_bias, v_a_log, v_d_skip, v_ssd_norm_w, v_attn_sinks, v_w_out, v_ln_g, v_ln_b):
    me = _index(*_position())
    me1 = me.reshape(1).astype(jnp.int32)
    x0, target = x[0], loss_target[0]
    xb = x0.astype(_MXU)
    bf16_shard = lambda shape: jax.ShapeDtypeStruct(shape, BF16)

    w_in_all, conv_w_all = _all_gather([w_in[0].astype(BF16), conv_w[0]])
    w_ssd, w_att = _unpack_w_in(w_in_all)
    conv_w_f = jnp.transpose(conv_w_all, (1, 0, 2)).reshape(4, D_XBC)
    s = _small_operands(positions[0], conv_w_f, conv_b, dt_bias, a_log, d_skip, ssd_norm_w, attn_sinks, ln_g, ln_b)

    proj_ssd = _matmul(xb, w_ssd, tm=1024, tn=S_W // 2, name="in_proj_ssd")
    proj_att = _matmul(xb, w_att, tm=1024, tn=A_W // 2, name="in_proj_att")
    gather_w_out = _Hosted([w_out[0].astype(BF16)], [bf16_shard((N_DEV, 256, D_MODEL))], [_Flow("gather", 0, 0)])
    y, ypre, hprev, w_out_all = _ssd_forward(proj_ssd, *_ssd_args(s), comm=gather_w_out)
    w_out_f = w_out_all.reshape(2 * D_MODEL, D_MODEL)
    y = _swa_forward(proj_att, s["tabs"], s["sinks"], y)
    dr, dy, acc_head = _head(y, x0, target, w_out_f, s["ln_g8"], s["ln_b8"], tm=256)

    dw_out_parts = _matmul_tn(y, dr, tl=512, tn=D_MODEL, name="dw_out").reshape(N_DEV, 256, D_MODEL)
    own_out = lax.dynamic_index_in_dim(dw_out_parts, me, axis=0, keepdims=False)
    send_out = _Hosted([dw_out_parts.astype(BF16)], [bf16_shard((N_DEV - 1, 256, D_MODEL))], [_Flow("exchange", 0, 0)])
    d_ssd, acc_cw, acc_w, acc_s, recv_out = _ssd_backward(proj_ssd, hprev, ypre, dy, *_ssd_args(s), s["ET"],
                                                          comm=send_out)
    dw_ssd = _matmul_tn(xb, d_ssd, tl=512, tn=S_W // 2, name="dw_in_ssd")
    parts_lo, own_lo = _pack_dw_in(me1, dw_ssd, None, 0)
    recv_shape = bf16_shard((N_DEV - 1, D_MODEL, SHARD_COLS))
    send_lo = _Hosted([parts_lo], [recv_shape], [_Flow("exchange", 0, 0, target_x=0)])
    d_att, dsink, recv_in = _swa_backward(proj_att, s["tabs"], s["sinks"], dy, comm=send_lo)
    dw_att = _matmul_tn(xb, d_att, tl=512, tn=A_W // 2, name="dw_in_att")
    parts_hi, own_hi = _pack_dw_in(me1, dw_ssd, dw_att, 1)
    small = _pack_small_grads(acc_head, acc_cw, acc_w, acc_s, dsink)
    send_hi = _Hosted([parts_hi, small, recv_in],
                      [recv_shape, jax.ShapeDtypeStruct((N_DEV,) + small.shape, F32)],
                      [_Flow("exchange", 0, 0, target_x=1), _Flow("gather", 1, 1)], aliases={2: 0})
    dx, recv_in, small_all = _input_gradient(d_ssd, d_att, w_ssd, w_att, dr, tm=256, comm=send_hi)
    own_in = jnp.where(me < 4, own_lo, own_hi)

    g_in, d_in, nm_in, nv_in = _adamw_shard(own_in, recv_in, w_in[0], m_w_in[0], v_w_in[0], rows=256, name="adamw_w_in")
    g_out, d_out, nm_out, nv_out = _adamw_shard(own_out, recv_out, w_out[0], m_w_out[0], v_w_out[0], rows=256,
                                                name="adamw_w_out")
    zero = jnp.zeros((), F32)
    pack = lambda d, cw: _pack_small(zero, d["conv_b"], d["dt_bias"], d["a_log"], d["d_skip"], d["attn_sinks"],
                                     d["ssd_norm_w"], d["ln_g"], d["ln_b"], cw)
    zero_cw = jnp.zeros((4, D_XBC), F32)
    wp = pack(dict(conv_b=conv_b, dt_bias=dt_bias, a_log=a_log, d_skip=d_skip, attn_sinks=attn_sinks,
                   ssd_norm_w=ssd_norm_w, ln_g=ln_g, ln_b=ln_b), zero_cw)
    mp = pack(dict(conv_b=m_conv_b, dt_bias=m_dt_bias, a_log=m_a_log, d_skip=m_d_skip, attn_sinks=m_attn_sinks,
                   ssd_norm_w=m_ssd_norm_w, ln_g=m_ln_g, ln_b=m_ln_b), zero_cw)
    vp = pack(dict(conv_b=v_conv_b, dt_bias=v_dt_bias, a_log=v_a_log, d_skip=v_d_skip, attn_sinks=v_attn_sinks,
                   ssd_norm_w=v_ssd_norm_w, ln_g=v_ln_g, ln_b=v_ln_b), zero_cw)
    gs, ds_, ms_, vs_ = [_unpack_small(p) for p in _adamw_small(small_all, wp, mp, vp)]
    g_cw = lax.dynamic_slice_in_dim(gs["conv_w"], me * (D_XBC // N_DEV), D_XBC // N_DEV, axis=1)
    d_cw, nm_cw, nv_cw = _adamw_plain(g_cw, conv_w[0], m_conv_w[0], v_conv_w[0])

    names = ["conv_b", "dt_bias", "a_log", "d_skip", "ssd_norm_w", "attn_sinks"]

    def leaves(big_in, cw, small_d, big_out):
        return ([big_in[None], cw[None]] + [small_d[k] for k in names] + [big_out[None], small_d["ln_g"], small_d["ln_b"]])

    return (gs["loss"], dx[None],
            *leaves(g_in, g_cw, gs, g_out), *leaves(d_in, d_cw, ds_, d_out),
            *leaves(nm_in, nm_cw, ms_, nm_out), *leaves(nv_in, nv_cw, vs_, nv_out))
```

```python
import jax
import jax.numpy as jnp
from jax import lax
from jax.experimental import pallas as pl
from jax.experimental.pallas import tpu as pltpu
import numpy as np

F32 = jnp.float32
BF16 = jnp.bfloat16
_MXU = jnp.bfloat16

N_DEV = 8
D_MODEL = 1024
D_SSD = 1024
D_ATT = 1024
HEAD_DIM = 64
N_HEADS = 16
SSD_GROUPS = 2
KV_HEADS = 4
CHUNK = 128
D_XBC = 1536
D_IN_PROJ = 5136
ROPE_DIM = 16
ROPE_THETA = 500000.0
ALPHA = (2.0 * 1) ** 0.25
LN_EPS = 1e-5
RMS_EPS = 1e-5
ATT_SCALE = HEAD_DIM ** -0.5
NEG = -1e30

S_Z, S_XS, S_B, S_C, S_DT, S_W = 0, 1024, 2048, 2304, 2560, 2816
N_SSD_REAL = 2576
A_Q, A_K, A_V, A_G, A_W = 0, 1024, 1280, 1536, 2560

ADAM_LR = 0.001
ADAM_B1 = 0.9
ADAM_B2 = 0.999
ADAM_EPS = 1e-08
ADAM_WD = 0.01
ADAM_STEP = 10

P_LOSS, P_CONVB, P_DTB, P_ALOG, P_DSKIP, P_SINK, P_NORMW, P_LNG, P_LNB, P_CONVW, P_END = (
    0, 128, 1664, 1792, 1920, 2048, 2176, 3200, 4224, 5248, 11392)
P_ROWS = 96

VMEM_LIMIT = 48 * 1024 * 1024
MESH = pl.DeviceIdType.MESH


def _params(sem=None):
    return pltpu.CompilerParams(dimension_semantics=sem, vmem_limit_bytes=VMEM_LIMIT)


def _mm(a, b):
    return jnp.dot(a.astype(_MXU), b.astype(_MXU), preferred_element_type=F32)


def _mm_nt(a, b):
    return lax.dot_general(a.astype(_MXU), b.astype(_MXU), (((1,), (1,)), ((), ())),
                           preferred_element_type=F32)


def _mm_tn(a, b):
    return lax.dot_general(a.astype(_MXU), b.astype(_MXU), (((0,), (0,)), ((), ())),
                           preferred_element_type=F32)


def _split3(v):
    hi = v.astype(BF16)
    r = v - hi.astype(F32)
    mid = r.astype(BF16)
    lo = (r - mid.astype(F32)).astype(BF16)
    return hi, mid, lo


def _mm_exact_r(v, p01):
    hi, mid, lo = _split3(v)
    d = lambda a: jnp.dot(a, p01, preferred_element_type=F32)
    return d(hi) + d(mid) + d(lo)


def _mm_exact_l(p01, v):
    hi, mid, lo = _split3(v)
    d = lambda a: jnp.dot(p01, a, preferred_element_type=F32)
    return d(hi) + d(mid) + d(lo)


def _sigmoid(x):
    return 1.0 / (1.0 + jnp.exp(-x))


def _softplus(x):
    e = jnp.exp(-jnp.abs(x))
    u = 1.0 + e
    log1p = jnp.where(u == 1.0, e, jnp.log(u) * (e / (u - 1.0)))
    return jnp.maximum(x, 0.0) + log1p


def _rows8(rows):
    n = rows[0].shape[1]
    rid = lax.broadcasted_iota(jnp.int32, (8, n), 0)
    out = jnp.zeros((8, n), F32)
    for k, r in enumerate(rows):
        out = out + jnp.where(rid == k, r, 0.0)
    return out


def _colsum(a):
    return jnp.sum(a, axis=0, keepdims=True)


def _matmul(a, b, *, tm, tn, name, addend=None, scale=1.0, b_transposed=False):
    M, K = a.shape
    N = b.shape[0] if b_transposed else b.shape[1]

    def body(*refs):
        if addend is None:
            a_ref, b_ref, o_ref = refs
        else:
            a_ref, b_ref, c_ref, o_ref = refs
        acc = _mm_nt(a_ref[...], b_ref[...]) if b_transposed else _mm(a_ref[...], b_ref[...])
        if addend is not None:
            acc = acc + scale * c_ref[...]
        o_ref[...] = acc

    b_spec = pl.BlockSpec((tn, K), lambda i, j: (j, 0)) if b_transposed else pl.BlockSpec((K, tn), lambda i, j: (0, j))
    in_specs = [pl.BlockSpec((tm, K), lambda i, j: (i, 0)), b_spec]
    args = [a, b]
    if addend is not None:
        in_specs.append(pl.BlockSpec((tm, tn), lambda i, j: (i, j)))
        args.append(addend)
    return pl.pallas_call(
        body, name=name, grid=(M // tm, N // tn), in_specs=in_specs,
        out_specs=pl.BlockSpec((tm, tn), lambda i, j: (i, j)),
        out_shape=jax.ShapeDtypeStruct((M, N), F32),
        compiler_params=_params(("arbitrary", "arbitrary")),
    )(*args)


def _matmul_tn(a, g, *, tl, tn, name):
    L, M = a.shape
    N = g.shape[1]

    def body(a_ref, g_ref, o_ref):
        @pl.when(pl.program_id(1) == 0)
        def _():
            o_ref[...] = jnp.zeros_like(o_ref)

        o_ref[...] += lax.dot_general(a_ref[...].astype(_MXU), g_ref[...].astype(_MXU),
                                      (((0,), (0,)), ((), ())), preferred_element_type=F32)

    return pl.pallas_call(
        body, name=name, grid=(N // tn, L // tl),
        in_specs=[pl.BlockSpec((tl, M), lambda j, l: (l, 0)), pl.BlockSpec((tl, tn), lambda j, l: (l, j))],
        out_specs=pl.BlockSpec((M, tn), lambda j, l: (0, j)),
        out_shape=jax.ShapeDtypeStruct((M, N), F32),
        compiler_params=_params(("arbitrary", "arbitrary")),
    )(a, g)


def _position():
    return lax.axis_index("x"), lax.axis_index("y"), lax.axis_index("c")


def _index(px, py, pc):
    return 4 * px + 2 * py + pc


def _flip(pos, k):
    x, y, c = pos
    return ((1 - x) if (k >> 2) & 1 else x, (1 - y) if (k >> 1) & 1 else y, (1 - c) if k & 1 else c)


def _when(cond, fn):
    if cond is True:
        fn()
    else:
        pl.when(cond)(fn)


def _remote(src, dst, send_sem, recv_sem, peer):
    return pltpu.make_async_remote_copy(src_ref=src, dst_ref=dst, send_sem=send_sem, recv_sem=recv_sem,
                                        device_id=peer, device_id_type=MESH)


class _Flow:
    def __init__(self, kind, operand, result, target_x=None):
        self.kind, self.operand, self.result, self.target_x = kind, operand, result, target_x


class _Hosted:
    def __init__(self, operands, out_shapes, flows, aliases=None):
        self.operands, self.out_shapes, self.flows = operands, out_shapes, flows
        self.aliases = aliases or {}

    def plan(self, ins, outs, send_sems, recv_sems, local_sems):
        me = _position()
        mi = _index(*me)
        sends, recvs, locals_ = [], [], []
        for row, f in enumerate(self.flows):
            src, dst = ins[f.operand], outs[f.result]
            for k in range(1, N_DEV):
                peer = _flip(me, k)
                sems = (send_sems.at[row, k - 1], recv_sems.at[row, k - 1])
                if f.kind == "exchange":
                    owner = _index(*peer) if f.target_x is None else 2 * peer[1] + peer[2]
                    cp = _remote(src.at[owner], dst.at[k - 1], *sems, peer)
                    to_peer = True if f.target_x is None else peer[0] == f.target_x
                    to_me = True if f.target_x is None else me[0] == f.target_x
                    sends.append((to_peer, cp))
                    recvs.append((to_me, cp))
                else:
                    sends.append((True, _remote(src, dst.at[mi], *sems, peer)))
                    recvs.append((True, _remote(src, dst.at[_index(*peer)], *sems, peer)))
            if f.kind == "gather":
                locals_.append(pltpu.make_async_copy(src, dst.at[mi], local_sems.at[row]))

        def start():
            for cp in locals_:
                cp.start()
            for cond, cp in sends:
                _when(cond, cp.start)

        def wait():
            for cond, cp in recvs:
                _when(cond, cp.wait_recv)
            for cond, cp in sends:
                _when(cond, cp.wait_send)
            for cp in locals_:
                cp.wait()

        return start, wait


def _call(body, comm, *, name, grid, in_specs, out_specs, out_shape, scratch_shapes, args, aliases=None):
    io_alias = dict(aliases or {})
    if comm is None:
        return pl.pallas_call(body, name=name, grid=grid, in_specs=in_specs, out_specs=out_specs, out_shape=out_shape,
                              scratch_shapes=scratch_shapes, input_output_aliases=io_alias,
                              compiler_params=_params(("arbitrary",)))(*args)
    n_in, n_out, n_scr = len(args), len(out_shape), len(scratch_shapes)
    c_in, c_out, rows = len(comm.operands), len(comm.out_shapes), len(comm.flows)

    def hosted(*refs):
        ins, refs = refs[:n_in], refs[n_in:]
        cins, refs = refs[:c_in], refs[c_in:]
        outs, refs = refs[:n_out], refs[n_out:]
        couts, refs = refs[:c_out], refs[c_out:]
        scr, (send_sems, recv_sems, local_sems) = refs[:n_scr], refs[n_scr:]
        start, wait = comm.plan(cins, couts, send_sems, recv_sems, local_sems)
        pl.when(pl.program_id(0) == 0)(start)
        body(*ins, *outs, *scr)
        pl.when(pl.program_id(0) == grid[0] - 1)(wait)

    for ci, co in comm.aliases.items():
        io_alias[n_in + ci] = n_out + co
    any_spec = pl.BlockSpec(memory_space=pl.ANY)
    sems = [pltpu.SemaphoreType.DMA((rows, N_DEV - 1)), pltpu.SemaphoreType.DMA((rows, N_DEV - 1)),
            pltpu.SemaphoreType.DMA((rows,))]
    return pl.pallas_call(
        hosted, name=name, grid=grid, in_specs=list(in_specs) + [any_spec] * c_in,
        out_specs=list(out_specs) + [any_spec] * c_out, out_shape=list(out_shape) + list(comm.out_shapes),
        scratch_shapes=list(scratch_shapes) + sems, input_output_aliases=io_alias,
        compiler_params=_params(("arbitrary",)))(*args, *comm.operands)


def _ssd_recompute(first, p_ref, halo_ref, cw_ref, cb_ref, dtb_ref, alog_ref, e_ref, ext_scr):
    ext_scr[0:8, :] = jnp.where(first, 0.0, halo_ref[:, S_XS:S_DT])
    ext_scr[8:136, :] = p_ref[:, S_XS:S_DT]
    cw = cw_ref[...]
    pre = (cb_ref[0:1, :] + cw[3:4, :] * ext_scr[8:136, :] + cw[2:3, :] * ext_scr[7:135, :]
           + cw[1:2, :] * ext_scr[6:134, :] + cw[0:1, :] * ext_scr[5:133, :])
    sg = _sigmoid(pre)
    act = pre * sg
    lane = lax.broadcasted_iota(jnp.int32, (1, 128), 1)
    A = jnp.where(lane < N_HEADS, -jnp.exp(alog_ref[0:1, :]), 0.0)
    raw = p_ref[:, S_DT:S_DT + 128] + dtb_ref[0:1, :]
    dt = _softplus(raw)
    dA = dt * A
    row = lax.broadcasted_iota(jnp.int32, (128, 128), 0)
    col = lax.broadcasted_iota(jnp.int32, (128, 128), 1)
    tril = (row >= col).astype(BF16)
    acs = _mm_exact_l(tril, dA)
    last = acs[127:128, :]
    ds = jnp.exp(last - acs)
    eo = jnp.exp(acs)
    E = e_ref[...]
    ex = _mm_exact_r(jnp.concatenate([dt, ds, eo], axis=0), E)
    dt_e, ds_e, eo_e = ex[0:128], ex[128:256], ex[256:384]
    xs_c = act[:, 0:1024]
    X = xs_c * dt_e
    return dict(pre=pre, sg=sg, xs_c=xs_c, Bc=act[:, 1024:1280], Cc=act[:, 1280:1536], A=A, raw=raw, dt=dt,
                acs=acs, acsT=acs.T, eo_e=eo_e, ds_e=ds_e, dt_e=dt_e, cd_e=eo_e[127:128, :],
                X=X, Xd=X * ds_e, row=row, col=col)


def _split_halves(t):
    lo = _lo_half(CHUNK)
    return jnp.concatenate([jnp.where(lo, t, 0.0), jnp.where(lo, 0.0, t)], axis=0)


def _ssd_core(R, hprev):
    causal = R["row"] >= R["col"]
    acs, acsT, X = R["acs"], R["acsT"], R["X"]
    ydiag, yoff, snew = [], [], []
    for g in range(SSD_GROUPS):
        Bg = R["Bc"][:, g * 128:(g + 1) * 128]
        Cg = R["Cc"][:, g * 128:(g + 1) * 128]
        cols = slice(g * 512, (g + 1) * 512)
        CB = _mm_nt(Cg, Bg)
        snew.append(_mm_tn(Bg, R["Xd"][:, cols]))
        yoff.append(_mm(Cg, hprev[:, cols]))
        for j in range(4):
            h0 = g * 8 + 2 * j
            ms = [CB * jnp.exp(jnp.where(causal, acs[:, h:h + 1] - acsT[h:h + 1, :], NEG)) for h in (h0, h0 + 1)]
            ydiag.append(_mm(jnp.concatenate(ms, axis=1), _split_halves(X[:, h0 * HEAD_DIM:h0 * HEAD_DIM + 128])))
    Y = jnp.concatenate(ydiag, axis=1) + jnp.concatenate(yoff, axis=1) * R["eo_e"]
    return Y, jnp.concatenate(snew, axis=1)


def _ssd_forward(proj_ssd, conv_w8, conv_b8, dtb8, alog8, dskip_e, norm_w, E, comm=None):
    L = proj_ssd.shape[0]
    nc = L // CHUNK

    def body(p_ref, halo_ref, cw_ref, cb_ref, dtb_ref, alog_ref, dsk_ref, nw_ref, e_ref,
             y_ref, ypre_ref, hprev_ref, h_scr, ext_scr):
        c = pl.program_id(0)
        first = c == 0

        @pl.when(first)
        def _():
            h_scr[...] = jnp.zeros_like(h_scr)

        R = _ssd_recompute(first, p_ref, halo_ref, cw_ref, cb_ref, dtb_ref, alog_ref, e_ref, ext_scr)
        hprev = h_scr[...]
        hprev_ref[...] = hprev
        Y, snew = _ssd_core(R, hprev)
        h_scr[...] = hprev * R["cd_e"] + snew
        Y = Y + dsk_ref[0:1, :] * R["xs_c"]
        ypre_ref[...] = Y
        z = p_ref[:, S_Z:S_Z + 1024]
        yf = Y * (z * _sigmoid(z))
        outs = []
        for g in range(SSD_GROUPS):
            yg = yf[:, g * 512:(g + 1) * 512]
            r = lax.rsqrt(jnp.mean(yg * yg, axis=-1, keepdims=True) + RMS_EPS)
            outs.append(yg * r)
        y_ref[...] = (jnp.concatenate(outs, axis=1) * nw_ref[0:1, :]).astype(y_ref.dtype)

    const = lambda shape: pl.BlockSpec(shape, lambda c: (0, 0))
    return _call(
        body, comm, name="ssd_fwd", grid=(nc,),
        in_specs=[pl.BlockSpec((CHUNK, S_W), lambda c: (c, 0)),
                  pl.BlockSpec((8, S_W), lambda c: (jnp.maximum(c * 16 - 1, 0), 0)),
                  const((8, D_XBC)), const((8, D_XBC)), const((8, 128)), const((8, 128)),
                  const((8, 1024)), const((8, 1024)), const((128, 1024))],
        out_specs=[pl.BlockSpec((CHUNK, D_SSD), lambda c: (c, 0)), pl.BlockSpec((CHUNK, D_SSD), lambda c: (c, 0)),
                   pl.BlockSpec((128, 1024), lambda c: (c, 0))],
        out_shape=[jax.ShapeDtypeStruct((L, D_SSD + D_ATT), _MXU), jax.ShapeDtypeStruct((L, D_SSD), F32),
                   jax.ShapeDtypeStruct((nc * 128, 1024), F32)],
        scratch_shapes=[pltpu.VMEM((128, 1024), F32), pltpu.VMEM((136, D_XBC), F32)],
        args=(proj_ssd, proj_ssd, conv_w8, conv_b8, dtb8, alog8, dskip_e, norm_w, E))


def _ssd_backward(proj_ssd, hprev_all, ypre, dy, conv_w8, conv_b8, dtb8, alog8, dskip_e, norm_w, E, ET, comm=None):
    L = proj_ssd.shape[0]
    nc = L // CHUNK

    def body(p_ref, halo_ref, hprev_ref, ypre_ref, dy_ref, cw_ref, cb_ref, dtb_ref, alog_ref, dsk_ref, nw_ref, e_ref,
             et_ref, dp_ref, acc_cw_ref, acc_w_ref, acc_s_ref, dh_scr, ext_scr, ext2_scr, nxt_scr):
        i = pl.program_id(0)
        c = nc - 1 - i
        first = c == 0

        @pl.when(i == 0)
        def _():
            dh_scr[...] = jnp.zeros_like(dh_scr)
            nxt_scr[...] = jnp.zeros_like(nxt_scr)
            acc_cw_ref[...] = jnp.zeros_like(acc_cw_ref)
            acc_w_ref[...] = jnp.zeros_like(acc_w_ref)
            acc_s_ref[...] = jnp.zeros_like(acc_s_ref)

        R = _ssd_recompute(first, p_ref, halo_ref, cw_ref, cb_ref, dtb_ref, alog_ref, e_ref, ext_scr)
        hprev = hprev_ref[...]
        xs_c, X, Xd = R["xs_c"], R["X"], R["Xd"]
        acs, acsT = R["acs"], R["acsT"]
        ET = et_ref[...]
        dsk = dsk_ref[0:1, :]
        Y = ypre_ref[...]

        z = p_ref[:, S_Z:S_Z + 1024]
        sz = _sigmoid(z)
        silz = z * sz
        yf = Y * silz
        dyv = dy_ref[...]
        nw = nw_ref[0:1, :]
        dyf_parts, dnw_parts = [], []
        for g in range(SSD_GROUPS):
            cols = slice(g * 512, (g + 1) * 512)
            yg = yf[:, cols]
            r = lax.rsqrt(jnp.mean(yg * yg, axis=-1, keepdims=True) + RMS_EPS)
            yn = yg * r
            dyn = dyv[:, cols] * nw[:, cols]
            dnw_parts.append(_colsum(dyv[:, cols] * yn))
            dyf_parts.append(r * (dyn - yn * jnp.mean(dyn * yn, axis=-1, keepdims=True)))
        dyf = jnp.concatenate(dyf_parts, axis=1)
        dY = dyf * silz
        dz = dyf * Y * (sz * (1.0 + z * (1.0 - sz)))

        dhn = dh_scr[...]
        dYo = dY * R["eo_e"]
        causal = R["row"] >= R["col"]
        dacs = jnp.zeros((128, 128), F32)
        dacs_t = jnp.zeros((128, 128), F32)
        dxdiag, dxd, dhprev, dBs, dCs, yoff = [], [], [], [], [], []
        for g in range(SSD_GROUPS):
            Bg = R["Bc"][:, g * 128:(g + 1) * 128]
            Cg = R["Cc"][:, g * 128:(g + 1) * 128]
            cols = slice(g * 512, (g + 1) * 512)
            CB = _mm_nt(Cg, Bg)
            dCB = jnp.zeros((128, 128), F32)
            for j in range(4):
                h0 = g * 8 + 2 * j
                pc = slice(h0 * HEAD_DIM, h0 * HEAD_DIM + 128)
                dYst = _split_halves(dY[:, pc])
                dMst = _mm_nt(dYst, X[:, pc])
                mts = []
                for a, h in enumerate((h0, h0 + 1)):
                    acol = acs[:, h:h + 1]
                    arow = acsT[h:h + 1, :]
                    Lm = jnp.exp(jnp.where(causal, acol - arow, NEG))
                    M = CB * Lm
                    dM = dMst[a * 128:(a + 1) * 128]
                    dCB = dCB + dM * Lm
                    G = dM * M
                    dacs = dacs + jnp.where(R["col"] == h, jnp.sum(G, axis=1, keepdims=True), 0.0)
                    dacs_t = dacs_t + jnp.where(R["row"] == h, jnp.sum(G, axis=0, keepdims=True), 0.0)
                    mts.append(M.T)
                dxdiag.append(_mm(jnp.concatenate(mts, axis=1), dYst))
            dS = dhn[:, cols]
            dxd.append(_mm(Bg, dS))
            yoff.append(_mm(Cg, hprev[:, cols]))
            dhprev.append(_mm_tn(Cg, dYo[:, cols]))
            dCs.append(_mm_nt(dYo[:, cols], hprev[:, cols]) + _mm(dCB, Bg))
            dBs.append(_mm_tn(dCB, Cg) + _mm_nt(Xd[:, cols], dS))
        Yoff = jnp.concatenate(yoff, axis=1) * R["eo_e"]
        dXd = jnp.concatenate(dxd, axis=1)
        dX = jnp.concatenate(dxdiag, axis=1) + dXd * R["ds_e"]
        t_state = dXd * Xd
        hs = _mm_exact_r(jnp.concatenate([dY * Yoff - t_state, dX * xs_c], axis=0), ET)
        dacs = dacs + hs[0:128] - dacs_t.T
        v_last = _colsum(t_state + dhn * hprev * R["cd_e"])
        dlast = _mm_exact_r(jnp.broadcast_to(v_last, (8, 1024)), ET)[0:1, :]
        dacs = dacs + jnp.where(R["row"] == 127, dlast, 0.0)
        triu = (R["col"] >= R["row"]).astype(BF16)
        da = _mm_exact_l(triu, dacs)
        ddt = da * R["A"] + hs[128:256]
        ddt_raw = ddt * _sigmoid(R["raw"])
        dxs_c = dX * R["dt_e"] + dY * dsk
        dh_scr[...] = jnp.concatenate(dhprev, axis=1) + dhn * R["cd_e"]

        dact = jnp.concatenate([dxs_c] + dBs + dCs, axis=1)
        pre, sg = R["pre"], R["sg"]
        dpre = dact * (sg * (1.0 + pre * (1.0 - sg)))
        ext2_scr[0:128, :] = dpre
        ext2_scr[128:136, :] = nxt_scr[...]
        nxt_scr[...] = dpre[0:8, :]
        cw = cw_ref[...]
        dxbc = (cw[3:4, :] * dpre + cw[2:3, :] * ext2_scr[1:129, :] + cw[1:2, :] * ext2_scr[2:130, :]
                + cw[0:1, :] * ext2_scr[3:131, :])
        acc_cw_ref[...] += _rows8([_colsum(dpre * ext_scr[5 + k:133 + k, :]) for k in range(4)] + [_colsum(dpre)])
        acc_w_ref[...] += _rows8([jnp.concatenate(dnw_parts, axis=1), _colsum(dY * xs_c)])
        acc_s_ref[...] += _rows8([_colsum(ddt_raw), _colsum(da * R["dt"])])

        lane = lax.broadcasted_iota(jnp.int32, (128, 128), 1)
        dp_ref[:, S_Z:S_Z + 1024] = dz
        dp_ref[:, S_XS:S_DT] = dxbc
        dp_ref[:, S_DT:S_DT + 128] = jnp.where(lane < N_HEADS, ddt_raw, 0.0)
        dp_ref[:, S_DT + 128:S_W] = jnp.zeros((128, 128), F32)

        @pl.when(i == nc - 1)
        def _():
            acc = acc_s_ref[...]
            dskip = _mm_exact_r(acc_w_ref[...], ET)[1:2, :]
            acc_s_ref[...] = _rows8([acc[0:1, :], acc[1:2, :] * R["A"], dskip])

    const = lambda shape: pl.BlockSpec(shape, lambda i: (0, 0))
    rev = lambda i: (nc - 1 - i, 0)
    return _call(
        body, comm, name="ssd_bwd", grid=(nc,),
        in_specs=[pl.BlockSpec((CHUNK, S_W), rev),
                  pl.BlockSpec((8, S_W), lambda i: (jnp.maximum((nc - 1 - i) * 16 - 1, 0), 0)),
                  pl.BlockSpec((128, 1024), rev),
                  pl.BlockSpec((CHUNK, D_SSD), rev),
                  pl.BlockSpec((CHUNK, D_SSD), rev),
                  const((8, D_XBC)), const((8, D_XBC)), const((8, 128)), const((8, 128)),
                  const((8, 1024)), const((8, 1024)), const((128, 1024)), const((1024, 128))],
        out_specs=[pl.BlockSpec((CHUNK, S_W), rev), const((8, D_XBC)), const((8, 1024)), const((8, 128))],
        out_shape=[jax.ShapeDtypeStruct((L, S_W), F32), jax.ShapeDtypeStruct((8, D_XBC), F32),
                   jax.ShapeDtypeStruct((8, 1024), F32), jax.ShapeDtypeStruct((8, 128), F32)],
        scratch_shapes=[pltpu.VMEM((128, 1024), F32), pltpu.VMEM((136, D_XBC), F32),
                        pltpu.VMEM((136, D_XBC), F32), pltpu.VMEM((8, D_XBC), F32)],
        args=(proj_ssd, proj_ssd, hprev_all, ypre, dy, conv_w8, conv_b8, dtb8, alog8, dskip_e, norm_w, E, ET))


def _rope(t, tab):
    cos, sa, sb = tab[:, 0:128], tab[:, 128:256], tab[:, 256:384]
    outs = []
    for i in range(t.shape[1] // 128):
        tg = t[:, i * 128:(i + 1) * 128]
        outs.append(tg * cos + pltpu.roll(tg, 8, 1) * sa + pltpu.roll(tg, 120, 1) * sb)
    return jnp.concatenate(outs, axis=1)


def _rope_transposed(d, tab):
    cos, sa, sb = tab[:, 0:128], tab[:, 128:256], tab[:, 256:384]
    outs = []
    for i in range(d.shape[1] // 128):
        dg = d[:, i * 128:(i + 1) * 128]
        outs.append(dg * cos + pltpu.roll(dg * sa, 120, 1) + pltpu.roll(dg * sb, 8, 1))
    return jnp.concatenate(outs, axis=1)


def _lo_half(rows):
    return lax.broadcasted_iota(jnp.int32, (rows, 128), 1) < HEAD_DIM


def _kv_both(t, j):
    p, b = j // 2, j % 2
    lo = _lo_half(2 * CHUNK)
    nat = jnp.where(lo if b == 0 else jnp.logical_not(lo), t[:, p * 128:(p + 1) * 128], 0.0)
    return nat + pltpu.roll(nat, HEAD_DIM, 1)


def _stack_heads(t, j):
    lo = _lo_half(CHUNK)
    hi = jnp.logical_not(lo)
    a, b = t[:, 2 * j * 128:(2 * j + 1) * 128], t[:, (2 * j + 1) * 128:(2 * j + 2) * 128]
    return jnp.concatenate([jnp.where(lo, a, 0.0), jnp.where(hi, a, 0.0),
                            jnp.where(lo, b, 0.0), jnp.where(hi, b, 0.0)], axis=0)


def _unstack_heads(s):
    lo = _lo_half(CHUNK)
    return jnp.concatenate([jnp.where(lo, s[0:128], s[128:256]), jnp.where(lo, s[256:384], s[384:512])], axis=1)


def _fold_kv(r, j):
    lo = _lo_half(2 * CHUNK)
    return jnp.where(lo if j % 2 == 0 else jnp.logical_not(lo), r + pltpu.roll(r, HEAD_DIM, 1), 0.0)


def _sink_row(sink_ref, j):
    hid = lax.broadcasted_iota(jnp.int32, (1, 4 * CHUNK), 1) // CHUNK
    row = jnp.zeros((1, 4 * CHUNK), F32)
    for hh in range(4):
        row = jnp.where(hid == hh, sink_ref[4 * j + hh], row)
    return row


def _band_mask(blk):
    si = lax.broadcasted_iota(jnp.int32, (2 * CHUNK, 4 * CHUNK), 0)
    qi = lax.broadcasted_iota(jnp.int32, (2 * CHUNK, 4 * CHUNK), 1) % CHUNK
    return (si > qi) & (si <= qi + CHUNK) & ((blk > 0) | (si >= CHUNK))


def _softmax_sink(s, valid, sink):
    s = jnp.where(valid, s, NEG)
    mx = jnp.maximum(jnp.max(s, axis=0, keepdims=True), sink)
    p = jnp.exp(s - mx)
    esink = jnp.exp(sink - mx)
    inv = 1.0 / (jnp.sum(p, axis=0, keepdims=True) + esink)
    return p * inv, esink * inv


def _swa_forward(proj_att, tabs, sinks, y):
    L = proj_att.shape[0]
    nb = L // CHUNK

    def body(sink_ref, p_ref, prev_ref, tab_ref, ptab_ref, y_in_ref, y_ref):
        n = pl.program_id(0)
        tab = tab_ref[...]
        qr = _rope(p_ref[:, A_Q:A_Q + 1024], tab)
        k_cur = _rope(p_ref[:, A_K:A_K + 256], tab)
        k_prev = _rope(prev_ref[:, 0:256], ptab_ref[...])
        kk = jnp.concatenate([k_prev, k_cur], axis=0)
        vv = jnp.concatenate([prev_ref[:, 256:512], p_ref[:, A_V:A_V + 256]], axis=0)
        valid = _band_mask(n)
        outs = []
        for j in range(KV_HEADS):
            s = _mm_nt(_kv_both(kk, j), _stack_heads(qr, j)) * ATT_SCALE
            P, _ = _softmax_sink(s, valid, _sink_row(sink_ref, j))
            outs.append(_unstack_heads(_mm_tn(P, _kv_both(vv, j))))
        g = p_ref[:, A_G:A_G + 1024]
        y_ref[...] = (jnp.concatenate(outs, axis=1) * (g * _sigmoid(g))).astype(y_ref.dtype)

    return pl.pallas_call(
        body, name="swa_fwd", grid=(nb,),
        in_specs=[pl.BlockSpec(memory_space=pltpu.SMEM),
                  pl.BlockSpec((CHUNK, A_W), lambda n: (n, 0)),
                  pl.BlockSpec((CHUNK, 512), lambda n: (jnp.maximum(n - 1, 0), 2)),
                  pl.BlockSpec((CHUNK, 384), lambda n: (n, 0)),
                  pl.BlockSpec((CHUNK, 384), lambda n: (jnp.maximum(n - 1, 0), 0)),
                  pl.BlockSpec(memory_space=pl.ANY)],
        out_specs=pl.BlockSpec((CHUNK, D_ATT), lambda n: (n, 1)),
        out_shape=jax.ShapeDtypeStruct(y.shape, y.dtype),
        input_output_aliases={5: 0},
        compiler_params=_params(("arbitrary",)),
    )(sinks, proj_att, proj_att, tabs, tabs, y)


def _swa_backward(proj_att, tabs, sinks, dy, comm=None):
    L = proj_att.shape[0]
    nb = L // CHUNK

    def body(sink_ref, p_ref, prev_ref, tab_ref, ptab_ref, dy_ref, dp_ref, dsink_ref, carry_k, carry_v):
        i = pl.program_id(0)
        n = nb - 1 - i

        @pl.when(i == 0)
        def _():
            carry_k[...] = jnp.zeros_like(carry_k)
            carry_v[...] = jnp.zeros_like(carry_v)
            dsink_ref[...] = jnp.zeros_like(dsink_ref)

        tab = tab_ref[...]
        qr = _rope(p_ref[:, A_Q:A_Q + 1024], tab)
        k_cur = _rope(p_ref[:, A_K:A_K + 256], tab)
        k_prev = _rope(prev_ref[:, 0:256], ptab_ref[...])
        kk = jnp.concatenate([k_prev, k_cur], axis=0)
        vv = jnp.concatenate([prev_ref[:, 256:512], p_ref[:, A_V:A_V + 256]], axis=0)
        valid = _band_mask(n)
        g = p_ref[:, A_G:A_G + 1024]
        sgm = _sigmoid(g)
        dyv = dy_ref[...]
        do_all = dyv * (g * sgm)
        lane8 = lax.broadcasted_iota(jnp.int32, (8, 128), 1)
        hid = lax.broadcasted_iota(jnp.int32, (1, 4 * CHUNK), 1) // CHUNK
        o_parts, dq_parts = [], []
        dk_nat = [jnp.zeros((2 * CHUNK, 128), F32) for _ in range(2)]
        dv_nat = [jnp.zeros((2 * CHUNK, 128), F32) for _ in range(2)]
        dsink = jnp.zeros((8, 128), F32)
        for j in range(KV_HEADS):
            qs = _stack_heads(qr, j)
            kkb, vvb = _kv_both(kk, j), _kv_both(vv, j)
            s = _mm_nt(kkb, qs) * ATT_SCALE
            P, psink = _softmax_sink(s, valid, _sink_row(sink_ref, j))
            o_parts.append(_unstack_heads(_mm_tn(P, vvb)))
            do_s = _stack_heads(do_all, j)
            dP = _mm_nt(vvb, do_s)
            D = jnp.sum(P * dP, axis=0, keepdims=True)
            dS = P * (dP - D)
            sd = psink * D
            for hh in range(4):
                dsink = dsink + jnp.where(lane8 == 4 * j + hh, -jnp.sum(jnp.where(hid == hh, sd, 0.0)), 0.0)
            dq_parts.append(_unstack_heads(_mm_tn(dS, kkb)) * ATT_SCALE)
            dk_nat[j // 2] = dk_nat[j // 2] + _fold_kv(_mm(dS, qs), j) * ATT_SCALE
            dv_nat[j // 2] = dv_nat[j // 2] + _fold_kv(_mm(P, do_s), j)
        o = jnp.concatenate(o_parts, axis=1)
        dkk = jnp.concatenate(dk_nat, axis=1)
        dvv = jnp.concatenate(dv_nat, axis=1)
        dp_ref[:, A_Q:A_Q + 1024] = _rope_transposed(jnp.concatenate(dq_parts, axis=1), tab)
        dp_ref[:, A_K:A_K + 256] = _rope_transposed(dkk[CHUNK:2 * CHUNK] + carry_k[...], tab)
        dp_ref[:, A_V:A_V + 256] = dvv[CHUNK:2 * CHUNK] + carry_v[...]
        dp_ref[:, A_G:A_G + 1024] = dyv * o * (sgm * (1.0 + g * (1.0 - sgm)))
        carry_k[...] = dkk[0:CHUNK]
        carry_v[...] = dvv[0:CHUNK]
        dsink_ref[...] += dsink

    rev = lambda i: (nb - 1 - i, 0)
    prev = lambda i: jnp.maximum(nb - 2 - i, 0)
    return _call(
        body, comm, name="swa_bwd", grid=(nb,),
        in_specs=[pl.BlockSpec(memory_space=pltpu.SMEM),
                  pl.BlockSpec((CHUNK, A_W), rev),
                  pl.BlockSpec((CHUNK, 512), lambda i: (prev(i), 2)),
                  pl.BlockSpec((CHUNK, 384), rev),
                  pl.BlockSpec((CHUNK, 384), lambda i: (prev(i), 0)),
                  pl.BlockSpec((CHUNK, D_ATT), lambda i: (nb - 1 - i, 1))],
        out_specs=[pl.BlockSpec((CHUNK, A_W), rev), pl.BlockSpec((8, 128), lambda i: (0, 0))],
        out_shape=[jax.ShapeDtypeStruct((L, A_W), F32), jax.ShapeDtypeStruct((8, 128), F32)],
        scratch_shapes=[pltpu.VMEM((CHUNK, 256), F32), pltpu.VMEM((CHUNK, 256), F32)],
        args=(sinks, proj_att, proj_att, tabs, tabs, dy))


def _head(y, x, target, w_out, ln_g8, ln_b8, *, tm):
    L = x.shape[0]
    nsteps = L // tm

    def body(y_ref, x_ref, t_ref, wo_ref, g_ref, b_ref, dr_ref, dy_ref, acc_ref):
        i = pl.program_id(0)

        @pl.when(i == 0)
        def _():
            acc_ref[...] = jnp.zeros_like(acc_ref)

        r = ALPHA * x_ref[...] + _mm(y_ref[...], wo_ref[...])
        mu = jnp.mean(r, axis=-1, keepdims=True)
        d = r - mu
        rstd = lax.rsqrt(jnp.mean(d * d, axis=-1, keepdims=True) + LN_EPS)
        xh = d * rstd
        gam = g_ref[0:1, :]
        e = xh * gam + b_ref[0:1, :] - t_ref[...]
        dout = e * (1.0 / D_MODEL)
        dxh = dout * gam
        dr = rstd * (dxh - jnp.mean(dxh, axis=-1, keepdims=True)
                     - xh * jnp.mean(dxh * xh, axis=-1, keepdims=True))
        dr_ref[...] = dr
        dy_ref[...] = _mm_nt(dr, wo_ref[...])
        acc_ref[...] += _rows8([_colsum(dout * xh), _colsum(dout), _colsum(e * e) * (0.5 / D_MODEL)])

        @pl.when(i == nsteps - 1)
        def _():
            acc = acc_ref[...]
            tot = jnp.sum(acc[2:3, :])
            rid = lax.broadcasted_iota(jnp.int32, (8, 1024), 0)
            acc_ref[...] = jnp.where(rid == 3, tot, acc)

    const = lambda shape: pl.BlockSpec(shape, lambda i: (0, 0))
    row = lambda w: pl.BlockSpec((tm, w), lambda i: (i, 0))
    return pl.pallas_call(
        body, name="head", grid=(nsteps,),
        in_specs=[row(2048), row(1024), row(1024), const((2048, 1024)), const((8, 1024)), const((8, 1024))],
        out_specs=[row(1024), row(2048), const((8, 1024))],
        out_shape=[jax.ShapeDtypeStruct((L, D_MODEL), F32), jax.ShapeDtypeStruct((L, 2048), F32),
                   jax.ShapeDtypeStruct((8, 1024), F32)],
        compiler_params=_params(("arbitrary",)),
    )(y, x, target, w_out, ln_g8, ln_b8)


def _all_gather(shards):
    n = len(shards)
    any_spec = pl.BlockSpec(memory_space=pl.ANY)

    def body(*refs):
        ins, outs = refs[:n], refs[n:2 * n]
        send_sems, recv_sems, local_sems = refs[2 * n:]
        x, y, c = _position()
        me, sibling = (x, y, c), (x, y, 1 - c)
        chips = [(1 - x, y), (x, 1 - y), (1 - x, 1 - y)]

        def copy(a, k, block, to, src=None):
            slot = outs[a].at[_index(*block)]
            return pltpu.make_async_remote_copy(
                src_ref=slot if src is None else src, dst_ref=slot,
                send_sem=send_sems.at[a, k], recv_sem=recv_sems.at[a, k],
                device_id=to, device_id_type=MESH)

        mine = [pltpu.make_async_copy(ins[a], outs[a].at[_index(*me)], local_sems.at[a]) for a in range(n)]
        for cp in mine:
            cp.start()
        first = []
        for a in range(n):
            first.append(copy(a, 0, me, sibling, src=ins[a]))
            first += [copy(a, 1 + j, me, (*chip, c), src=ins[a]) for j, chip in enumerate(chips)]
        for cp in first:
            cp.start()
        passed = []
        for j, chip in enumerate(chips):
            for a in range(n):
                copy(a, 1 + j, (*chip, c), me).wait_recv()
                fwd = copy(a, 4 + j, (*chip, c), sibling)
                fwd.start()
                passed.append(fwd)
        for a in range(n):
            copy(a, 0, sibling, me).wait_recv()
            for j, chip in enumerate(chips):
                copy(a, 4 + j, (*chip, 1 - c), me).wait_recv()
        for cp in first + passed:
            cp.wait_send()
        for cp in mine:
            cp.wait()

    return pl.pallas_call(
        body, name="weight_all_gather",
        in_specs=[any_spec] * n, out_specs=[any_spec] * n,
        out_shape=[jax.ShapeDtypeStruct((N_DEV,) + s.shape, s.dtype) for s in shards],
        scratch_shapes=[pltpu.SemaphoreType.DMA((n, 7)), pltpu.SemaphoreType.DMA((n, 7)),
                        pltpu.SemaphoreType.DMA((n,))],
    )(*shards)


def _input_gradient(d_ssd, d_att, w_ssd, w_att, dr, *, tm, comm=None):
    L = dr.shape[0]

    def body(ds_ref, da_ref, ws_ref, wa_ref, dr_ref, o_ref):
        o_ref[...] = ALPHA * dr_ref[...] + _mm_nt(ds_ref[...], ws_ref[...]) + _mm_nt(da_ref[...], wa_ref[...])

    row = lambda w: pl.BlockSpec((tm, w), lambda i: (i, 0))
    const = lambda shape: pl.BlockSpec(shape, lambda i: (0, 0))
    return _call(body, comm, name="dx", grid=(L // tm,),
                 in_specs=[row(S_W), row(A_W), const((D_MODEL, S_W)), const((D_MODEL, A_W)), row(D_MODEL)],
                 out_specs=[row(D_MODEL)], out_shape=[jax.ShapeDtypeStruct((L, D_MODEL), F32)],
                 scratch_shapes=[], args=(d_ssd, d_att, w_ssd, w_att, dr))


SHARD_COLS = D_IN_PROJ // N_DEV
SPLIT = N_SSD_REAL - 4 * SHARD_COLS
RELAYOUT_ROWS = 256


def _unpack_w_in(w_all):
    def body(g_ref, ws_ref, wa_ref):
        for j in range(4):
            ws_ref[:, SHARD_COLS * j:SHARD_COLS * (j + 1)] = g_ref[j]
        ws_ref[:, 4 * SHARD_COLS:N_SSD_REAL] = g_ref[4, :, 0:SPLIT]
        ws_ref[:, N_SSD_REAL:S_W] = jnp.zeros((RELAYOUT_ROWS, S_W - N_SSD_REAL), ws_ref.dtype)
        wa_ref[:, 0:SHARD_COLS - SPLIT] = g_ref[4, :, SPLIT:SHARD_COLS]
        for j in range(5, N_DEV):
            lo = SHARD_COLS * (j - 4) - SPLIT
            wa_ref[:, lo:lo + SHARD_COLS] = g_ref[j]

    return pl.pallas_call(
        body, name="unpack_w_in", grid=(D_MODEL // RELAYOUT_ROWS,),
        in_specs=[pl.BlockSpec((N_DEV, RELAYOUT_ROWS, SHARD_COLS), lambda i: (0, i, 0))],
        out_specs=[pl.BlockSpec((RELAYOUT_ROWS, S_W), lambda i: (i, 0)), pl.BlockSpec((RELAYOUT_ROWS, A_W), lambda i: (i, 0))],
        out_shape=[jax.ShapeDtypeStruct((D_MODEL, S_W), w_all.dtype), jax.ShapeDtypeStruct((D_MODEL, A_W), w_all.dtype)],
        compiler_params=_params(("arbitrary",)),
    )(w_all)


def _pack_dw_in(me1, dw_ssd, dw_att, half):
    def body(me_ref, *refs):
        if half == 0:
            (ds_ref, p_ref, own_ref), da_ref = refs, None
        else:
            ds_ref, da_ref, p_ref, own_ref = refs
        me = me_ref[0]

        @pl.when((me < 4) if half == 1 else (me >= 4))
        def _():
            own_ref[...] = jnp.zeros_like(own_ref)

        for j in range(4):
            if half == 0:
                pieces = [(0, ds_ref[:, SHARD_COLS * j:SHARD_COLS * (j + 1)])]
            elif j == 0:
                pieces = [(0, ds_ref[:, 4 * SHARD_COLS:N_SSD_REAL]), (SPLIT, da_ref[:, 0:SHARD_COLS - SPLIT])]
            else:
                lo = SHARD_COLS * j - SPLIT
                pieces = [(0, da_ref[:, lo:lo + SHARD_COLS])]
            for off, blk in pieces:
                p_ref[j, :, off:off + blk.shape[1]] = blk.astype(p_ref.dtype)

                @pl.when(me == 4 * half + j)
                def _(off=off, blk=blk):
                    own_ref[:, off:off + blk.shape[1]] = blk

    ins = [dw_ssd] if half == 0 else [dw_ssd, dw_att]
    row = lambda a: pl.BlockSpec((RELAYOUT_ROWS, a.shape[1]), lambda i: (i, 0))
    return pl.pallas_call(
        body, name="pack_dw_in_%d" % half, grid=(D_MODEL // RELAYOUT_ROWS,),
        in_specs=[pl.BlockSpec(memory_space=pltpu.SMEM)] + [row(a) for a in ins],
        out_specs=[pl.BlockSpec((4, RELAYOUT_ROWS, SHARD_COLS), lambda i: (0, i, 0)),
                   pl.BlockSpec((RELAYOUT_ROWS, SHARD_COLS), lambda i: (i, 0))],
        out_shape=[jax.ShapeDtypeStruct((4, D_MODEL, SHARD_COLS), BF16), jax.ShapeDtypeStruct((D_MODEL, SHARD_COLS), F32)],
        compiler_params=_params(("arbitrary",)),
    )(me1, *ins)


def _adamw_math(w, g, m, v):
    m = ADAM_B1 * m + (1.0 - ADAM_B1) * g
    v = ADAM_B2 * v + (1.0 - ADAM_B2) * (g * g)
    m_hat = m / (1.0 - ADAM_B1 ** ADAM_STEP)
    v_hat = v / (1.0 - ADAM_B2 ** ADAM_STEP)
    delta = -ADAM_LR * (m_hat / (jnp.sqrt(v_hat) + ADAM_EPS) + ADAM_WD * w)
    return delta, m, v


def _adamw_shard(g_own, recv, w, m, v, *, rows, name):
    R, C = g_own.shape

    def body(g_ref, r_ref, w_ref, m_ref, v_ref, go_ref, d_ref, mo_ref, vo_ref):
        g = g_ref[...]
        for k in range(N_DEV - 1):
            g = g + r_ref[k].astype(F32)
        d, mn, vn = _adamw_math(w_ref[...], g, m_ref[...], v_ref[...])
        go_ref[...] = g
        d_ref[...] = d
        mo_ref[...] = mn
        vo_ref[...] = vn

    blk = pl.BlockSpec((rows, C), lambda i: (i, 0))
    return pl.pallas_call(
        body, name=name, grid=(R // rows,),
        in_specs=[blk, pl.BlockSpec((N_DEV - 1, rows, C), lambda i: (0, i, 0)), blk, blk, blk],
        out_specs=[blk] * 4, out_shape=[jax.ShapeDtypeStruct((R, C), F32)] * 4,
        compiler_params=_params(("arbitrary",)),
    )(g_own, recv, w, m, v)


def _adamw_small(gathered, w, m, v):
    def body(r_ref, w_ref, m_ref, v_ref, go_ref, d_ref, mo_ref, vo_ref):
        g = r_ref[0]
        for k in range(1, N_DEV):
            g = g + r_ref[k]
        d, mn, vn = _adamw_math(w_ref[...], g, m_ref[...], v_ref[...])
        go_ref[...] = g
        d_ref[...] = d
        mo_ref[...] = mn
        vo_ref[...] = vn

    return pl.pallas_call(
        body, name="adamw_small", out_shape=[jax.ShapeDtypeStruct(w.shape, F32)] * 4,
        compiler_params=_params(),
    )(gathered, w, m, v)


def _adamw_plain(g, w, m, v):
    def body(g_ref, w_ref, m_ref, v_ref, d_ref, mo_ref, vo_ref):
        d, mn, vn = _adamw_math(w_ref[...], g_ref[...], m_ref[...], v_ref[...])
        d_ref[...] = d
        mo_ref[...] = mn
        vo_ref[...] = vn

    return pl.pallas_call(
        body, name="adamw_conv_w", out_shape=[jax.ShapeDtypeStruct(w.shape, F32)] * 3,
        compiler_params=_params(),
    )(g, w, m, v)


def _pad_rows8(v):
    v = v.reshape(-1, v.shape[-1])
    return jnp.pad(v, ((0, 8 - v.shape[0]), (0, 0)))


def _pad_lanes(v, n):
    return jnp.pad(v, ((0, 0), (0, n - v.shape[1])))


def _pack_small(loss, conv_b, dt_bias, a_log, d_skip, sinks, norm_w, ln_g, ln_b, conv_w):
    def seg(v, n):
        v = v.reshape(-1).astype(F32)
        return jnp.pad(v, (0, n - v.shape[0]))

    flat = jnp.concatenate([seg(loss, 128), seg(conv_b, 1536), seg(dt_bias, 128), seg(a_log, 128),
                            seg(d_skip, 128), seg(sinks, 128), seg(norm_w, 1024), seg(ln_g, 1024),
                            seg(ln_b, 1024), seg(conv_w, 6144), jnp.zeros((P_ROWS * 128 - P_END,), F32)])
    return flat.reshape(P_ROWS, 128)


def _unpack_small(p):
    f = p.reshape(-1)
    return dict(conv_b=f[P_CONVB:P_CONVB + 1536].reshape(1, 1536), dt_bias=f[P_DTB:P_DTB + 16].reshape(1, 16),
                a_log=f[P_ALOG:P_ALOG + 16].reshape(1, 16), d_skip=f[P_DSKIP:P_DSKIP + 16].reshape(1, 16),
                attn_sinks=f[P_SINK:P_SINK + 16].reshape(1, 16), ssd_norm_w=f[P_NORMW:P_NORMW + 1024].reshape(1, 1024),
                ln_g=f[P_LNG:P_LNG + 1024].reshape(1, 1024), ln_b=f[P_LNB:P_LNB + 1024].reshape(1, 1024),
                conv_w=f[P_CONVW:P_CONVW + 6144].reshape(4, 1536), loss=f[P_LOSS])


def _lane_pattern(fn):
    return np.asarray([fn(l % HEAD_DIM) for l in range(128)], np.float32)


ROPE_INV = _lane_pattern(lambda r: ROPE_THETA ** (-2.0 * (r % 8) / ROPE_DIM) if r < ROPE_DIM else 0.0)
ROPE_SIN_A = _lane_pattern(lambda r: 1.0 if 8 <= r < ROPE_DIM else 0.0)
ROPE_SIN_B = _lane_pattern(lambda r: -1.0 if r < 8 else 0.0)


def _rope_tables(positions):
    ang = positions.astype(F32)[:, None] * ROPE_INV[None, :]
    sn = jnp.sin(ang)
    return jnp.concatenate([jnp.cos(ang), sn * ROPE_SIN_A[None, :], sn * ROPE_SIN_B[None, :]], axis=1)


def _expansion():
    E = np.arange(1024)[None, :] // HEAD_DIM == np.arange(128)[:, None]
    return jnp.asarray(E, BF16), jnp.asarray(E.T, BF16)


def _small_operands(positions, conv_w, conv_b, dt_bias, a_log, d_skip, norm_w, sinks, ln_g, ln_b):
    E, ET = _expansion()
    return dict(conv_w8=_pad_rows8(conv_w), conv_b8=_pad_rows8(conv_b),
                dtb8=_pad_rows8(_pad_lanes(dt_bias, 128)), alog8=_pad_rows8(_pad_lanes(a_log, 128)),
                dskip_e=_pad_rows8(jnp.repeat(d_skip, HEAD_DIM, axis=1)), norm_w8=_pad_rows8(norm_w),
                ln_g8=_pad_rows8(ln_g), ln_b8=_pad_rows8(ln_b), sinks=sinks.reshape(-1),
                tabs=_rope_tables(positions), E=E, ET=ET)


def _ssd_args(s):
    return (s["conv_w8"], s["conv_b8"], s["dtb8"], s["alog8"], s["dskip_e"], s["norm_w8"], s["E"])


def _pack_small_grads(acc_head, acc_cw, acc_w, acc_s, dsink):
    return _pack_small(acc_head[3, 0], acc_cw[4], acc_s[0, :16], acc_s[1, :16], acc_s[2, :16], dsink[0, :16],
                       acc_w[0], acc_head[0], acc_head[1], acc_cw[0:4])


def kernel(x, positions, w_in, conv_w, conv_b, dt_bias, a_log, d_skip, ssd_norm_w, attn_sinks, w_out, ln_g, ln_b, loss_target, m_w_in, m_conv_w, m_conv_b, m_dt_bias, m_a_log, m_d_skip, m_ssd_norm_w, m_attn_sinks, m_w_out, m_ln_g, m_ln_b, v_w_in, v_conv_w, v_conv_b, v_dt_bias, v_a_log, v_d_skip, v_ssd_norm_w, v_attn_sinks, v_w_out, v_ln_g, v_ln_b):
    me = _index(*_position())
    me1 = me.reshape(1).astype(jnp.int32)
    x0, target = x[0], loss_target[0]
    xb = x0.astype(_MXU)
    bf16_shard = lambda shape: jax.ShapeDtypeStruct(shape, BF16)

    w_in_all, conv_w_all = _all_gather([w_in[0].astype(BF16), conv_w[0]])
    w_ssd, w_att = _unpack_w_in(w_in_all)
    conv_w_f = jnp.transpose(conv_w_all, (1, 0, 2)).reshape(4, D_XBC)
    s = _small_operands(positions[0], conv_w_f, conv_b, dt_bias, a_log, d_skip, ssd_norm_w, attn_sinks, ln_g, ln_b)

    proj_ssd = _matmul(xb, w_ssd, tm=1024, tn=S_W // 2, name="in_proj_ssd")
    proj_att = _matmul(xb, w_att, tm=1024, tn=A_W // 2, name="in_proj_att")
    gather_w_out = _Hosted([w_out[0].astype(BF16)], [bf16_shard((N_DEV, 256, D_MODEL))], [_Flow("gather", 0, 0)])
    y, ypre, hprev, w_out_all = _ssd_forward(proj_ssd, *_ssd_args(s), comm=gather_w_out)
    w_out_f = w_out_all.reshape(2 * D_MODEL, D_MODEL)
    y = _swa_forward(proj_att, s["tabs"], s["sinks"], y)
    dr, dy, acc_head = _head(y, x0, target, w_out_f, s["ln_g8"], s["ln_b8"], tm=256)

    dw_out_parts = _matmul_tn(y, dr, tl=512, tn=D_MODEL, name="dw_out").reshape(N_DEV, 256, D_MODEL)
    own_out = lax.dynamic_index_in_dim(dw_out_parts, me, axis=0, keepdims=False)
    send_out = _Hosted([dw_out_parts.astype(BF16)], [bf16_shard((N_DEV - 1, 256, D_MODEL))], [_Flow("exchange", 0, 0)])
    d_ssd, acc_cw, acc_w, acc_s, recv_out = _ssd_backward(proj_ssd, hprev, ypre, dy, *_ssd_args(s), s["ET"],
                                                          comm=send_out)
    dw_ssd = _matmul_tn(xb, d_ssd, tl=512, tn=S_W // 2, name="dw_in_ssd")
    parts_lo, own_lo = _pack_dw_in(me1, dw_ssd, None, 0)
    recv_shape = bf16_shard((N_DEV - 1, D_MODEL, SHARD_COLS))
    send_lo = _Hosted([parts_lo], [recv_shape], [_Flow("exchange", 0, 0, target_x=0)])
    d_att, dsink, recv_in = _swa_backward(proj_att, s["tabs"], s["sinks"], dy, comm=send_lo)
    dw_att = _matmul_tn(xb, d_att, tl=512, tn=A_W // 2, name="dw_in_att")
    parts_hi, own_hi = _pack_dw_in(me1, dw_ssd, dw_att, 1)
    small = _pack_small_grads(acc_head, acc_cw, acc_w, acc_s, dsink)
    send_hi = _Hosted([parts_hi, small, recv_in],
                      [recv_shape, jax.ShapeDtypeStruct((N_DEV,) + small.shape, F32)],
                      [_Flow("exchange", 0, 0, target_x=1), _Flow("gather", 1, 1)], aliases={2: 0})
    dx, recv_in, small_all = _input_gradient(d_ssd, d_att, w_ssd, w_att, dr, tm=256, comm=send_hi)
    own_in = jnp.where(me < 4, own_lo, own_hi)

    g_in, d_in, nm_in, nv_in = _adamw_shard(own_in, recv_in, w_in[0], m_w_in[0], v_w_in[0], rows=256, name="adamw_w_in")
    g_out, d_out, nm_out, nv_out = _adamw_shard(own_out, recv_out, w_out[0], m_w_out[0], v_w_out[0], rows=256,
                                                name="adamw_w_out")
    zero = jnp.zeros((), F32)
    pack = lambda d, cw: _pack_small(zero, d["conv_b"], d["dt_bias"], d["a_log"], d["d_skip"], d["attn_sinks"],
                                     d["ssd_norm_w"], d["ln_g"], d["ln_b"], cw)
    zero_cw = jnp.zeros((4, D_XBC), F32)
    wp = pack(dict(conv_b=conv_b, dt_bias=dt_bias, a_log=a_log, d_skip=d_skip, attn_sinks=attn_sinks,
                   ssd_norm_w=ssd_norm_w, ln_g=ln_g, ln_b=ln_b), zero_cw)
    mp = pack(dict(conv_b=m_conv_b, dt_bias=m_dt_bias, a_log=m_a_log, d_skip=m_d_skip, attn_sinks=m_attn_sinks,
                   ssd_norm_w=m_ssd_norm_w, ln_g=m_ln_g, ln_b=m_ln_b), zero_cw)
    vp = pack(dict(conv_b=v_conv_b, dt_bias=v_dt_bias, a_log=v_a_log, d_skip=v_d_skip, attn_sinks=v_attn_sinks,
                   ssd_norm_w=v_ssd_norm_w, ln_g=v_ln_g, ln_b=v_ln_b), zero_cw)
    gs, ds_, ms_, vs_ = [_unpack_small(p) for p in _adamw_small(small_all, wp, mp, vp)]
    g_cw = lax.dynamic_slice_in_dim(gs["conv_w"], me * (D_XBC // N_DEV), D_XBC // N_DEV, axis=1)
    d_cw, nm_cw, nv_cw = _adamw_plain(g_cw, conv_w[0], m_conv_w[0], v_conv_w[0])

    names = ["conv_b", "dt_bias", "a_log", "d_skip", "ssd_norm_w", "attn_sinks"]

    def leaves(big_in, cw, small_d, big_out):
        return ([big_in[None], cw[None]] + [small_d[k] for k in names] + [big_out[None], small_d["ln_g"], small_d["ln_b"]])

    return (gs["loss"], dx[None],
            *leaves(g_in, g_cw, gs, g_out), *leaves(d_in, d_cw, ds_, d_out),
            *leaves(nm_in, nm_cw, ms_, nm_out), *leaves(nv_in, nv_cw, vs_, nv_out))
```

```python
import jax
import jax.numpy as jnp
from jax import lax
from jax.experimental import pallas as pl
from jax.experimental.pallas import tpu as pltpu
import numpy as np

F32 = jnp.float32
BF16 = jnp.bfloat16
_MXU = jnp.bfloat16

N_DEV = 8
D_MODEL = 1024
D_SSD = 1024
D_ATT = 1024
HEAD_DIM = 64
N_HEADS = 16
SSD_GROUPS = 2
KV_HEADS = 4
CHUNK = 128
D_XBC = 1536
D_IN_PROJ = 5136
ROPE_DIM = 16
ROPE_THETA = 500000.0
ALPHA = (2.0 * 1) ** 0.25
LN_EPS = 1e-5
RMS_EPS = 1e-5
ATT_SCALE = HEAD_DIM ** -0.5
NEG = -1e30

S_Z, S_XS, S_B, S_C, S_DT, S_W = 0, 1024, 2048, 2304, 2560, 2816
N_SSD_REAL = 2576
A_Q, A_K, A_V, A_G, A_W = 0, 1024, 1280, 1536, 2560

ADAM_LR = 0.001
ADAM_B1 = 0.9
ADAM_B2 = 0.999
ADAM_EPS = 1e-08
ADAM_WD = 0.01
ADAM_STEP = 10

P_LOSS, P_CONVB, P_DTB, P_ALOG, P_DSKIP, P_SINK, P_NORMW, P_LNG, P_LNB, P_CONVW, P_END = (
    0, 128, 1664, 1792, 1920, 2048, 2176, 3200, 4224, 5248, 11392)
P_ROWS = 96

VMEM_LIMIT = 48 * 1024 * 1024
MESH = pl.DeviceIdType.MESH


def _params(sem=None):
    return pltpu.CompilerParams(dimension_semantics=sem, vmem_limit_bytes=VMEM_LIMIT)


def _mm(a, b):
    return jnp.dot(a.astype(_MXU), b.astype(_MXU), preferred_element_type=F32)


def _mm_nt(a, b):
    return lax.dot_general(a.astype(_MXU), b.astype(_MXU), (((1,), (1,)), ((), ())),
                           preferred_element_type=F32)


def _mm_tn(a, b):
    return lax.dot_general(a.astype(_MXU), b.astype(_MXU), (((0,), (0,)), ((), ())),
                           preferred_element_type=F32)


def _split3(v):
    hi = v.astype(BF16)
    r = v - hi.astype(F32)
    mid = r.astype(BF16)
    lo = (r - mid.astype(F32)).astype(BF16)
    return hi, mid, lo


def _mm_exact_r(v, p01):
    hi, mid, lo = _split3(v)
    d = lambda a: jnp.dot(a, p01, preferred_element_type=F32)
    return d(hi) + d(mid) + d(lo)


def _mm_exact_l(p01, v):
    hi, mid, lo = _split3(v)
    d = lambda a: jnp.dot(p01, a, preferred_element_type=F32)
    return d(hi) + d(mid) + d(lo)


def _sigmoid(x):
    return 1.0 / (1.0 + jnp.exp(-x))


def _softplus(x):
    e = jnp.exp(-jnp.abs(x))
    u = 1.0 + e
    log1p = jnp.where(u == 1.0, e, jnp.log(u) * (e / (u - 1.0)))
    return jnp.maximum(x, 0.0) + log1p


def _rows8(rows):
    n = rows[0].shape[1]
    rid = lax.broadcasted_iota(jnp.int32, (8, n), 0)
    out = jnp.zeros((8, n), F32)
    for k, r in enumerate(rows):
        out = out + jnp.where(rid == k, r, 0.0)
    return out


def _colsum(a):
    return jnp.sum(a, axis=0, keepdims=True)


def _matmul(a, b, *, tm, tn, name, addend=None, scale=1.0, b_transposed=False):
    M, K = a.shape
    N = b.shape[0] if b_transposed else b.shape[1]

    def body(*refs):
        if addend is None:
            a_ref, b_ref, o_ref = refs
        else:
            a_ref, b_ref, c_ref, o_ref = refs
        acc = _mm_nt(a_ref[...], b_ref[...]) if b_transposed else _mm(a_ref[...], b_ref[...])
        if addend is not None:
            acc = acc + scale * c_ref[...]
        o_ref[...] = acc

    b_spec = pl.BlockSpec((tn, K), lambda i, j: (j, 0)) if b_transposed else pl.BlockSpec((K, tn), lambda i, j: (0, j))
    in_specs = [pl.BlockSpec((tm, K), lambda i, j: (i, 0)), b_spec]
    args = [a, b]
    if addend is not None:
        in_specs.append(pl.BlockSpec((tm, tn), lambda i, j: (i, j)))
        args.append(addend)
    return pl.pallas_call(
        body, name=name, grid=(M // tm, N // tn), in_specs=in_specs,
        out_specs=pl.BlockSpec((tm, tn), lambda i, j: (i, j)),
        out_shape=jax.ShapeDtypeStruct((M, N), F32),
        compiler_params=_params(("arbitrary", "arbitrary")),
    )(*args)


def _matmul_tn(a, g, *, tl, tn, name):
    L, M = a.shape
    N = g.shape[1]

    def body(a_ref, g_ref, o_ref):
        @pl.when(pl.program_id(1) == 0)
        def _():
            o_ref[...] = jnp.zeros_like(o_ref)

        o_ref[...] += lax.dot_general(a_ref[...].astype(_MXU), g_ref[...].astype(_MXU),
                                      (((0,), (0,)), ((), ())), preferred_element_type=F32)

    return pl.pallas_call(
        body, name=name, grid=(N // tn, L // tl),
        in_specs=[pl.BlockSpec((tl, M), lambda j, l: (l, 0)), pl.BlockSpec((tl, tn), lambda j, l: (l, j))],
        out_specs=pl.BlockSpec((M, tn), lambda j, l: (0, j)),
        out_shape=jax.ShapeDtypeStruct((M, N), F32),
        compiler_params=_params(("arbitrary", "arbitrary")),
    )(a, g)


def _position():
    return lax.axis_index("x"), lax.axis_index("y"), lax.axis_index("c")


def _index(px, py, pc):
    return 4 * px + 2 * py + pc


def _flip(pos, k):
    x, y, c = pos
    return ((1 - x) if (k >> 2) & 1 else x, (1 - y) if (k >> 1) & 1 else y, (1 - c) if k & 1 else c)


def _when(cond, fn):
    if cond is True:
        fn()
    else:
        pl.when(cond)(fn)


def _remote(src, dst, send_sem, recv_sem, peer):
    return pltpu.make_async_remote_copy(src_ref=src, dst_ref=dst, send_sem=send_sem, recv_sem=recv_sem,
                                        device_id=peer, device_id_type=MESH)


class _Flow:
    def __init__(self, kind, operand, result, target_x=None):
        self.kind, self.operand, self.result, self.target_x = kind, operand, result, target_x


class _Hosted:
    def __init__(self, operands, out_shapes, flows, aliases=None):
        self.operands, self.out_shapes, self.flows = operands, out_shapes, flows
        self.aliases = aliases or {}

    def plan(self, ins, outs, send_sems, recv_sems, local_sems):
        me = _position()
        mi = _index(*me)
        sends, recvs, locals_ = [], [], []
        for row, f in enumerate(self.flows):
            src, dst = ins[f.operand], outs[f.result]
            for k in range(1, N_DEV):
                peer = _flip(me, k)
                sems = (send_sems.at[row, k - 1], recv_sems.at[row, k - 1])
                if f.kind == "exchange":
                    owner = _index(*peer) if f.target_x is None else 2 * peer[1] + peer[2]
                    cp = _remote(src.at[owner], dst.at[k - 1], *sems, peer)
                    to_peer = True if f.target_x is None else peer[0] == f.target_x
                    to_me = True if f.target_x is None else me[0] == f.target_x
                    sends.append((to_peer, cp))
                    recvs.append((to_me, cp))
                elif f.kind == "chip_exchange":
                    if k & 1:
                        continue
                    cp = _remote(src.at[peer[1]], dst.at[k // 2 - 1], *sems, peer)
                    sends.append((peer[0] == f.target_x, cp))
                    recvs.append((me[0] == f.target_x, cp))
                else:
                    sends.append((True, _remote(src, dst.at[mi], *sems, peer)))
                    recvs.append((True, _remote(src, dst.at[_index(*peer)], *sems, peer)))
            if f.kind == "gather":
                locals_.append(pltpu.make_async_copy(src, dst.at[mi], local_sems.at[row]))

        def start():
            for cp in locals_:
                cp.start()
            for cond, cp in sends:
                _when(cond, cp.start)

        def wait():
            for cond, cp in recvs:
                _when(cond, cp.wait_recv)
            for cond, cp in sends:
                _when(cond, cp.wait_send)
            for cp in locals_:
                cp.wait()

        return start, wait


def _call(body, comm, *, name, grid, in_specs, out_specs, out_shape, scratch_shapes, args, aliases=None):
    io_alias = dict(aliases or {})
    if comm is None:
        return pl.pallas_call(body, name=name, grid=grid, in_specs=in_specs, out_specs=out_specs, out_shape=out_shape,
                              scratch_shapes=scratch_shapes, input_output_aliases=io_alias,
                              compiler_params=_params(("arbitrary",)))(*args)
    n_in, n_out, n_scr = len(args), len(out_shape), len(scratch_shapes)
    c_in, c_out, rows = len(comm.operands), len(comm.out_shapes), len(comm.flows)

    def hosted(*refs):
        ins, refs = refs[:n_in], refs[n_in:]
        cins, refs = refs[:c_in], refs[c_in:]
        outs, refs = refs[:n_out], refs[n_out:]
        couts, refs = refs[:c_out], refs[c_out:]
        scr, (send_sems, recv_sems, local_sems) = refs[:n_scr], refs[n_scr:]
        start, wait = comm.plan(cins, couts, send_sems, recv_sems, local_sems)
        pl.when(pl.program_id(0) == 0)(start)
        body(*ins, *outs, *scr)
        pl.when(pl.program_id(0) == grid[0] - 1)(wait)

    for ci, co in comm.aliases.items():
        io_alias[n_in + ci] = n_out + co
    any_spec = pl.BlockSpec(memory_space=pl.ANY)
    sems = [pltpu.SemaphoreType.DMA((rows, N_DEV - 1)), pltpu.SemaphoreType.DMA((rows, N_DEV - 1)),
            pltpu.SemaphoreType.DMA((rows,))]
    return pl.pallas_call(
        hosted, name=name, grid=grid, in_specs=list(in_specs) + [any_spec] * c_in,
        out_specs=list(out_specs) + [any_spec] * c_out, out_shape=list(out_shape) + list(comm.out_shapes),
        scratch_shapes=list(scratch_shapes) + sems, input_output_aliases=io_alias,
        compiler_params=_params(("arbitrary",)))(*args, *comm.operands)


def _ssd_recompute(first, p_ref, halo_ref, cw_ref, cb_ref, dtb_ref, alog_ref, e_ref, ext_scr):
    ext_scr[0:8, :] = jnp.where(first, 0.0, halo_ref[:, S_XS:S_DT])
    ext_scr[8:136, :] = p_ref[:, S_XS:S_DT]
    cw = cw_ref[...]
    pre = (cb_ref[0:1, :] + cw[3:4, :] * ext_scr[8:136, :] + cw[2:3, :] * ext_scr[7:135, :]
           + cw[1:2, :] * ext_scr[6:134, :] + cw[0:1, :] * ext_scr[5:133, :])
    sg = _sigmoid(pre)
    act = pre * sg
    lane = lax.broadcasted_iota(jnp.int32, (1, 128), 1)
    A = jnp.where(lane < N_HEADS, -jnp.exp(alog_ref[0:1, :]), 0.0)
    raw = p_ref[:, S_DT:S_DT + 128] + dtb_ref[0:1, :]
    dt = _softplus(raw)
    dA = dt * A
    row = lax.broadcasted_iota(jnp.int32, (128, 128), 0)
    col = lax.broadcasted_iota(jnp.int32, (128, 128), 1)
    tril = (row >= col).astype(BF16)
    acs = _mm_exact_l(tril, dA)
    last = acs[127:128, :]
    ds = jnp.exp(last - acs)
    eo = jnp.exp(acs)
    E = e_ref[...]
    ex = _mm_exact_r(jnp.concatenate([dt, ds, eo], axis=0), E)
    dt_e, ds_e, eo_e = ex[0:128], ex[128:256], ex[256:384]
    xs_c = act[:, 0:1024]
    X = xs_c * dt_e
    return dict(pre=pre, sg=sg, xs_c=xs_c, Bc=act[:, 1024:1280], Cc=act[:, 1280:1536], A=A, raw=raw, dt=dt,
                acs=acs, acsT=acs.T, eo_e=eo_e, ds_e=ds_e, dt_e=dt_e, cd_e=eo_e[127:128, :],
                X=X, Xd=X * ds_e, row=row, col=col)


def _split_halves(t):
    lo = _lo_half(CHUNK)
    return jnp.concatenate([jnp.where(lo, t, 0.0), jnp.where(lo, 0.0, t)], axis=0)


def _ssd_core(R, hprev):
    causal = R["row"] >= R["col"]
    acs, acsT, X = R["acs"], R["acsT"], R["X"]
    ydiag, yoff, snew = [], [], []
    for g in range(SSD_GROUPS):
        Bg = R["Bc"][:, g * 128:(g + 1) * 128]
        Cg = R["Cc"][:, g * 128:(g + 1) * 128]
        cols = slice(g * 512, (g + 1) * 512)
        CB = _mm_nt(Cg, Bg)
        snew.append(_mm_tn(Bg, R["Xd"][:, cols]))
        yoff.append(_mm(Cg, hprev[:, cols]))
        for j in range(4):
            h0 = g * 8 + 2 * j
            ms = [CB * jnp.exp(jnp.where(causal, acs[:, h:h + 1] - acsT[h:h + 1, :], NEG)) for h in (h0, h0 + 1)]
            ydiag.append(_mm(jnp.concatenate(ms, axis=1), _split_halves(X[:, h0 * HEAD_DIM:h0 * HEAD_DIM + 128])))
    Y = jnp.concatenate(ydiag, axis=1) + jnp.concatenate(yoff, axis=1) * R["eo_e"]
    return Y, jnp.concatenate(snew, axis=1)


def _ssd_forward(proj_ssd, conv_w8, conv_b8, dtb8, alog8, dskip_e, norm_w, E, comm=None):
    L = proj_ssd.shape[0]
    nc = L // CHUNK

    def body(p_ref, halo_ref, cw_ref, cb_ref, dtb_ref, alog_ref, dsk_ref, nw_ref, e_ref,
             y_ref, ypre_ref, hprev_ref, h_scr, ext_scr):
        c = pl.program_id(0)
        first = c == 0

        @pl.when(first)
        def _():
            h_scr[...] = jnp.zeros_like(h_scr)

        R = _ssd_recompute(first, p_ref, halo_ref, cw_ref, cb_ref, dtb_ref, alog_ref, e_ref, ext_scr)
        hprev = h_scr[...]
        hprev_ref[...] = hprev
        Y, snew = _ssd_core(R, hprev)
        h_scr[...] = hprev * R["cd_e"] + snew
        Y = Y + dsk_ref[0:1, :] * R["xs_c"]
        ypre_ref[...] = Y
        z = p_ref[:, S_Z:S_Z + 1024]
        yf = Y * (z * _sigmoid(z))
        outs = []
        for g in range(SSD_GROUPS):
            yg = yf[:, g * 512:(g + 1) * 512]
            r = lax.rsqrt(jnp.mean(yg * yg, axis=-1, keepdims=True) + RMS_EPS)
            outs.append(yg * r)
        y_ref[...] = (jnp.concatenate(outs, axis=1) * nw_ref[0:1, :]).astype(y_ref.dtype)

    const = lambda shape: pl.BlockSpec(shape, lambda c: (0, 0))
    return _call(
        body, comm, name="ssd_fwd", grid=(nc,),
        in_specs=[pl.BlockSpec((CHUNK, S_W), lambda c: (c, 0)),
                  pl.BlockSpec((8, S_W), lambda c: (jnp.maximum(c * 16 - 1, 0), 0)),
                  const((8, D_XBC)), const((8, D_XBC)), const((8, 128)), const((8, 128)),
                  const((8, 1024)), const((8, 1024)), const((128, 1024))],
        out_specs=[pl.BlockSpec((CHUNK, D_SSD), lambda c: (c, 0)), pl.BlockSpec((CHUNK, D_SSD), lambda c: (c, 0)),
                   pl.BlockSpec((128, 1024), lambda c: (c, 0))],
        out_shape=[jax.ShapeDtypeStruct((L, D_SSD + D_ATT), _MXU), jax.ShapeDtypeStruct((L, D_SSD), F32),
                   jax.ShapeDtypeStruct((nc * 128, 1024), F32)],
        scratch_shapes=[pltpu.VMEM((128, 1024), F32), pltpu.VMEM((136, D_XBC), F32)],
        args=(proj_ssd, proj_ssd, conv_w8, conv_b8, dtb8, alog8, dskip_e, norm_w, E))


def _ssd_backward(proj_ssd, hprev_all, ypre, dy, conv_w8, conv_b8, dtb8, alog8, dskip_e, norm_w, E, ET, comm=None):
    L = proj_ssd.shape[0]
    nc = L // CHUNK

    def body(p_ref, halo_ref, hprev_ref, ypre_ref, dy_ref, cw_ref, cb_ref, dtb_ref, alog_ref, dsk_ref, nw_ref, e_ref,
             et_ref, dp_ref, acc_cw_ref, acc_w_ref, acc_s_ref, dh_scr, ext_scr, ext2_scr, nxt_scr):
        i = pl.program_id(0)
        c = nc - 1 - i
        first = c == 0

        @pl.when(i == 0)
        def _():
            dh_scr[...] = jnp.zeros_like(dh_scr)
            nxt_scr[...] = jnp.zeros_like(nxt_scr)
            acc_cw_ref[...] = jnp.zeros_like(acc_cw_ref)
            acc_w_ref[...] = jnp.zeros_like(acc_w_ref)
            acc_s_ref[...] = jnp.zeros_like(acc_s_ref)

        R = _ssd_recompute(first, p_ref, halo_ref, cw_ref, cb_ref, dtb_ref, alog_ref, e_ref, ext_scr)
        hprev = hprev_ref[...]
        xs_c, X, Xd = R["xs_c"], R["X"], R["Xd"]
        acs, acsT = R["acs"], R["acsT"]
        ET = et_ref[...]
        dsk = dsk_ref[0:1, :]
        Y = ypre_ref[...]

        z = p_ref[:, S_Z:S_Z + 1024]
        sz = _sigmoid(z)
        silz = z * sz
        yf = Y * silz
        dyv = dy_ref[...]
        nw = nw_ref[0:1, :]
        dyf_parts, dnw_parts = [], []
        for g in range(SSD_GROUPS):
            cols = slice(g * 512, (g + 1) * 512)
            yg = yf[:, cols]
            r = lax.rsqrt(jnp.mean(yg * yg, axis=-1, keepdims=True) + RMS_EPS)
            yn = yg * r
            dyn = dyv[:, cols] * nw[:, cols]
            dnw_parts.append(_colsum(dyv[:, cols] * yn))
            dyf_parts.append(r * (dyn - yn * jnp.mean(dyn * yn, axis=-1, keepdims=True)))
        dyf = jnp.concatenate(dyf_parts, axis=1)
        dY = dyf * silz
        dz = dyf * Y * (sz * (1.0 + z * (1.0 - sz)))

        dhn = dh_scr[...]
        dYo = dY * R["eo_e"]
        causal = R["row"] >= R["col"]
        dacs = jnp.zeros((128, 128), F32)
        dacs_t = jnp.zeros((128, 128), F32)
        dxdiag, dxd, dhprev, dBs, dCs, yoff = [], [], [], [], [], []
        for g in range(SSD_GROUPS):
            Bg = R["Bc"][:, g * 128:(g + 1) * 128]
            Cg = R["Cc"][:, g * 128:(g + 1) * 128]
            cols = slice(g * 512, (g + 1) * 512)
            CB = _mm_nt(Cg, Bg)
            dCB = jnp.zeros((128, 128), F32)
            for j in range(4):
                h0 = g * 8 + 2 * j
                pc = slice(h0 * HEAD_DIM, h0 * HEAD_DIM + 128)
                dYst = _split_halves(dY[:, pc])
                dMst = _mm_nt(dYst, X[:, pc])
                mts = []
                for a, h in enumerate((h0, h0 + 1)):
                    acol = acs[:, h:h + 1]
                    arow = acsT[h:h + 1, :]
                    Lm = jnp.exp(jnp.where(causal, acol - arow, NEG))
                    M = CB * Lm
                    dM = dMst[a * 128:(a + 1) * 128]
                    dCB = dCB + dM * Lm
                    G = dM * M
                    dacs = dacs + jnp.where(R["col"] == h, jnp.sum(G, axis=1, keepdims=True), 0.0)
                    dacs_t = dacs_t + jnp.where(R["row"] == h, jnp.sum(G, axis=0, keepdims=True), 0.0)
                    mts.append(M.T)
                dxdiag.append(_mm(jnp.concatenate(mts, axis=1), dYst))
            dS = dhn[:, cols]
            dxd.append(_mm(Bg, dS))
            yoff.append(_mm(Cg, hprev[:, cols]))
            dhprev.append(_mm_tn(Cg, dYo[:, cols]))
            dCs.append(_mm_nt(dYo[:, cols], hprev[:, cols]) + _mm(dCB, Bg))
            dBs.append(_mm_tn(dCB, Cg) + _mm_nt(Xd[:, cols], dS))
        Yoff = jnp.concatenate(yoff, axis=1) * R["eo_e"]
        dXd = jnp.concatenate(dxd, axis=1)
        dX = jnp.concatenate(dxdiag, axis=1) + dXd * R["ds_e"]
        t_state = dXd * Xd
        hs = _mm_exact_r(jnp.concatenate([dY * Yoff - t_state, dX * xs_c], axis=0), ET)
        dacs = dacs + hs[0:128] - dacs_t.T
        v_last = _colsum(t_state + dhn * hprev * R["cd_e"])
        dlast = _mm_exact_r(jnp.broadcast_to(v_last, (8, 1024)), ET)[0:1, :]
        dacs = dacs + jnp.where(R["row"] == 127, dlast, 0.0)
        triu = (R["col"] >= R["row"]).astype(BF16)
        da = _mm_exact_l(triu, dacs)
        ddt = da * R["A"] + hs[128:256]
        ddt_raw = ddt * _sigmoid(R["raw"])
        dxs_c = dX * R["dt_e"] + dY * dsk
        dh_scr[...] = jnp.concatenate(dhprev, axis=1) + dhn * R["cd_e"]

        dact = jnp.concatenate([dxs_c] + dBs + dCs, axis=1)
        pre, sg = R["pre"], R["sg"]
        dpre = dact * (sg * (1.0 + pre * (1.0 - sg)))
        ext2_scr[0:128, :] = dpre
        ext2_scr[128:136, :] = nxt_scr[...]
        nxt_scr[...] = dpre[0:8, :]
        cw = cw_ref[...]
        dxbc = (cw[3:4, :] * dpre + cw[2:3, :] * ext2_scr[1:129, :] + cw[1:2, :] * ext2_scr[2:130, :]
                + cw[0:1, :] * ext2_scr[3:131, :])
        acc_cw_ref[...] += _rows8([_colsum(dpre * ext_scr[5 + k:133 + k, :]) for k in range(4)] + [_colsum(dpre)])
        acc_w_ref[...] += _rows8([jnp.concatenate(dnw_parts, axis=1), _colsum(dY * xs_c)])
        acc_s_ref[...] += _rows8([_colsum(ddt_raw), _colsum(da * R["dt"])])

        lane = lax.broadcasted_iota(jnp.int32, (128, 128), 1)
        dp_ref[:, S_Z:S_Z + 1024] = dz
        dp_ref[:, S_XS:S_DT] = dxbc
        dp_ref[:, S_DT:S_DT + 128] = jnp.where(lane < N_HEADS, ddt_raw, 0.0)
        dp_ref[:, S_DT + 128:S_W] = jnp.zeros((128, 128), F32)

        @pl.when(i == nc - 1)
        def _():
            acc = acc_s_ref[...]
            dskip = _mm_exact_r(acc_w_ref[...], ET)[1:2, :]
            acc_s_ref[...] = _rows8([acc[0:1, :], acc[1:2, :] * R["A"], dskip])

    const = lambda shape: pl.BlockSpec(shape, lambda i: (0, 0))
    rev = lambda i: (nc - 1 - i, 0)
    return _call(
        body, comm, name="ssd_bwd", grid=(nc,),
        in_specs=[pl.BlockSpec((CHUNK, S_W), rev),
                  pl.BlockSpec((8, S_W), lambda i: (jnp.maximum((nc - 1 - i) * 16 - 1, 0), 0)),
                  pl.BlockSpec((128, 1024), rev),
                  pl.BlockSpec((CHUNK, D_SSD), rev),
                  pl.BlockSpec((CHUNK, D_SSD), rev),
                  const((8, D_XBC)), const((8, D_XBC)), const((8, 128)), const((8, 128)),
                  const((8, 1024)), const((8, 1024)), const((128, 1024)), const((1024, 128))],
        out_specs=[pl.BlockSpec((CHUNK, S_W), rev), const((8, D_XBC)), const((8, 1024)), const((8, 128))],
        out_shape=[jax.ShapeDtypeStruct((L, S_W), F32), jax.ShapeDtypeStruct((8, D_XBC), F32),
                   jax.ShapeDtypeStruct((8, 1024), F32), jax.ShapeDtypeStruct((8, 128), F32)],
        scratch_shapes=[pltpu.VMEM((128, 1024), F32), pltpu.VMEM((136, D_XBC), F32),
                        pltpu.VMEM((136, D_XBC), F32), pltpu.VMEM((8, D_XBC), F32)],
        args=(proj_ssd, proj_ssd, hprev_all, ypre, dy, conv_w8, conv_b8, dtb8, alog8, dskip_e, norm_w, E, ET))


def _rope(t, tab):
    cos, sa, sb = tab[:, 0:128], tab[:, 128:256], tab[:, 256:384]
    outs = []
    for i in range(t.shape[1] // 128):
        tg = t[:, i * 128:(i + 1) * 128]
        outs.append(tg * cos + pltpu.roll(tg, 8, 1) * sa + pltpu.roll(tg, 120, 1) * sb)
    return jnp.concatenate(outs, axis=1)


def _rope_transposed(d, tab):
    cos, sa, sb = tab[:, 0:128], tab[:, 128:256], tab[:, 256:384]
    outs = []
    for i in range(d.shape[1] // 128):
        dg = d[:, i * 128:(i + 1) * 128]
        outs.append(dg * cos + pltpu.roll(dg * sa, 120, 1) + pltpu.roll(dg * sb, 8, 1))
    return jnp.concatenate(outs, axis=1)


def _lo_half(rows):
    return lax.broadcasted_iota(jnp.int32, (rows, 128), 1) < HEAD_DIM


def _kv_both(t, j):
    p, b = j // 2, j % 2
    lo = _lo_half(2 * CHUNK)
    nat = jnp.where(lo if b == 0 else jnp.logical_not(lo), t[:, p * 128:(p + 1) * 128], 0.0)
    return nat + pltpu.roll(nat, HEAD_DIM, 1)


def _stack_heads(t, j):
    lo = _lo_half(CHUNK)
    hi = jnp.logical_not(lo)
    a, b = t[:, 2 * j * 128:(2 * j + 1) * 128], t[:, (2 * j + 1) * 128:(2 * j + 2) * 128]
    return jnp.concatenate([jnp.where(lo, a, 0.0), jnp.where(hi, a, 0.0),
                            jnp.where(lo, b, 0.0), jnp.where(hi, b, 0.0)], axis=0)


def _unstack_heads(s):
    lo = _lo_half(CHUNK)
    return jnp.concatenate([jnp.where(lo, s[0:128], s[128:256]), jnp.where(lo, s[256:384], s[384:512])], axis=1)


def _fold_kv(r, j):
    lo = _lo_half(2 * CHUNK)
    return jnp.where(lo if j % 2 == 0 else jnp.logical_not(lo), r + pltpu.roll(r, HEAD_DIM, 1), 0.0)


def _sink_row(sink_ref, j):
    hid = lax.broadcasted_iota(jnp.int32, (1, 4 * CHUNK), 1) // CHUNK
    row = jnp.zeros((1, 4 * CHUNK), F32)
    for hh in range(4):
        row = jnp.where(hid == hh, sink_ref[4 * j + hh], row)
    return row


def _band_mask(blk):
    si = lax.broadcasted_iota(jnp.int32, (2 * CHUNK, 4 * CHUNK), 0)
    qi = lax.broadcasted_iota(jnp.int32, (2 * CHUNK, 4 * CHUNK), 1) % CHUNK
    return (si > qi) & (si <= qi + CHUNK) & ((blk > 0) | (si >= CHUNK))


def _softmax_sink(s, valid, sink):
    s = jnp.where(valid, s, NEG)
    mx = jnp.maximum(jnp.max(s, axis=0, keepdims=True), sink)
    p = jnp.exp(s - mx)
    esink = jnp.exp(sink - mx)
    inv = 1.0 / (jnp.sum(p, axis=0, keepdims=True) + esink)
    return p * inv, esink * inv


def _swa_forward(proj_att, tabs, sinks, y):
    L = proj_att.shape[0]
    nb = L // CHUNK

    def body(sink_ref, p_ref, prev_ref, tab_ref, ptab_ref, y_in_ref, y_ref):
        n = pl.program_id(0)
        tab = tab_ref[...]
        qr = _rope(p_ref[:, A_Q:A_Q + 1024], tab)
        k_cur = _rope(p_ref[:, A_K:A_K + 256], tab)
        k_prev = _rope(prev_ref[:, 0:256], ptab_ref[...])
        kk = jnp.concatenate([k_prev, k_cur], axis=0)
        vv = jnp.concatenate([prev_ref[:, 256:512], p_ref[:, A_V:A_V + 256]], axis=0)
        valid = _band_mask(n)
        outs = []
        for j in range(KV_HEADS):
            s = _mm_nt(_kv_both(kk, j), _stack_heads(qr, j)) * ATT_SCALE
            P, _ = _softmax_sink(s, valid, _sink_row(sink_ref, j))
            outs.append(_unstack_heads(_mm_tn(P, _kv_both(vv, j))))
        g = p_ref[:, A_G:A_G + 1024]
        y_ref[...] = (jnp.concatenate(outs, axis=1) * (g * _sigmoid(g))).astype(y_ref.dtype)

    return pl.pallas_call(
        body, name="swa_fwd", grid=(nb,),
        in_specs=[pl.BlockSpec(memory_space=pltpu.SMEM),
                  pl.BlockSpec((CHUNK, A_W), lambda n: (n, 0)),
                  pl.BlockSpec((CHUNK, 512), lambda n: (jnp.maximum(n - 1, 0), 2)),
                  pl.BlockSpec((CHUNK, 384), lambda n: (n, 0)),
                  pl.BlockSpec((CHUNK, 384), lambda n: (jnp.maximum(n - 1, 0), 0)),
                  pl.BlockSpec(memory_space=pl.ANY)],
        out_specs=pl.BlockSpec((CHUNK, D_ATT), lambda n: (n, 1)),
        out_shape=jax.ShapeDtypeStruct(y.shape, y.dtype),
        input_output_aliases={5: 0},
        compiler_params=_params(("arbitrary",)),
    )(sinks, proj_att, proj_att, tabs, tabs, y)


def _swa_backward(proj_att, tabs, sinks, dy, comm=None):
    L = proj_att.shape[0]
    nb = L // CHUNK

    def body(sink_ref, p_ref, prev_ref, tab_ref, ptab_ref, dy_ref, dp_ref, dsink_ref, carry_k, carry_v):
        i = pl.program_id(0)
        n = nb - 1 - i

        @pl.when(i == 0)
        def _():
            carry_k[...] = jnp.zeros_like(carry_k)
            carry_v[...] = jnp.zeros_like(carry_v)
            dsink_ref[...] = jnp.zeros_like(dsink_ref)

        tab = tab_ref[...]
        qr = _rope(p_ref[:, A_Q:A_Q + 1024], tab)
        k_cur = _rope(p_ref[:, A_K:A_K + 256], tab)
        k_prev = _rope(prev_ref[:, 0:256], ptab_ref[...])
        kk = jnp.concatenate([k_prev, k_cur], axis=0)
        vv = jnp.concatenate([prev_ref[:, 256:512], p_ref[:, A_V:A_V + 256]], axis=0)
        valid = _band_mask(n)
        g = p_ref[:, A_G:A_G + 1024]
        sgm = _sigmoid(g)
        dyv = dy_ref[...]
        do_all = dyv * (g * sgm)
        lane8 = lax.broadcasted_iota(jnp.int32, (8, 128), 1)
        hid = lax.broadcasted_iota(jnp.int32, (1, 4 * CHUNK), 1) // CHUNK
        o_parts, dq_parts = [], []
        dk_nat = [jnp.zeros((2 * CHUNK, 128), F32) for _ in range(2)]
        dv_nat = [jnp.zeros((2 * CHUNK, 128), F32) for _ in range(2)]
        dsink = jnp.zeros((8, 128), F32)
        for j in range(KV_HEADS):
            qs = _stack_heads(qr, j)
            kkb, vvb = _kv_both(kk, j), _kv_both(vv, j)
            s = _mm_nt(kkb, qs) * ATT_SCALE
            P, psink = _softmax_sink(s, valid, _sink_row(sink_ref, j))
            o_parts.append(_unstack_heads(_mm_tn(P, vvb)))
            do_s = _stack_heads(do_all, j)
            dP = _mm_nt(vvb, do_s)
            D = jnp.sum(P * dP, axis=0, keepdims=True)
            dS = P * (dP - D)
            sd = psink * D
            for hh in range(4):
                dsink = dsink + jnp.where(lane8 == 4 * j + hh, -jnp.sum(jnp.where(hid == hh, sd, 0.0)), 0.0)
            dq_parts.append(_unstack_heads(_mm_tn(dS, kkb)) * ATT_SCALE)
            dk_nat[j // 2] = dk_nat[j // 2] + _fold_kv(_mm(dS, qs), j) * ATT_SCALE
            dv_nat[j // 2] = dv_nat[j // 2] + _fold_kv(_mm(P, do_s), j)
        o = jnp.concatenate(o_parts, axis=1)
        dkk = jnp.concatenate(dk_nat, axis=1)
        dvv = jnp.concatenate(dv_nat, axis=1)
        dp_ref[:, A_Q:A_Q + 1024] = _rope_transposed(jnp.concatenate(dq_parts, axis=1), tab)
        dp_ref[:, A_K:A_K + 256] = _rope_transposed(dkk[CHUNK:2 * CHUNK] + carry_k[...], tab)
        dp_ref[:, A_V:A_V + 256] = dvv[CHUNK:2 * CHUNK] + carry_v[...]
        dp_ref[:, A_G:A_G + 1024] = dyv * o * (sgm * (1.0 + g * (1.0 - sgm)))
        carry_k[...] = dkk[0:CHUNK]
        carry_v[...] = dvv[0:CHUNK]
        dsink_ref[...] += dsink

    rev = lambda i: (nb - 1 - i, 0)
    prev = lambda i: jnp.maximum(nb - 2 - i, 0)
    return _call(
        body, comm, name="swa_bwd", grid=(nb,),
        in_specs=[pl.BlockSpec(memory_space=pltpu.SMEM),
                  pl.BlockSpec((CHUNK, A_W), rev),
                  pl.BlockSpec((CHUNK, 512), lambda i: (prev(i), 2)),
                  pl.BlockSpec((CHUNK, 384), rev),
                  pl.BlockSpec((CHUNK, 384), lambda i: (prev(i), 0)),
                  pl.BlockSpec((CHUNK, D_ATT), lambda i: (nb - 1 - i, 1))],
        out_specs=[pl.BlockSpec((CHUNK, A_W), rev), pl.BlockSpec((8, 128), lambda i: (0, 0))],
        out_shape=[jax.ShapeDtypeStruct((L, A_W), F32), jax.ShapeDtypeStruct((8, 128), F32)],
        scratch_shapes=[pltpu.VMEM((CHUNK, 256), F32), pltpu.VMEM((CHUNK, 256), F32)],
        args=(sinks, proj_att, proj_att, tabs, tabs, dy))


def _head(y, x, target, w_out, ln_g8, ln_b8, *, tm):
    L = x.shape[0]
    nsteps = L // tm

    def body(y_ref, x_ref, t_ref, wo_ref, g_ref, b_ref, dr_ref, dy_ref, acc_ref):
        i = pl.program_id(0)

        @pl.when(i == 0)
        def _():
            acc_ref[...] = jnp.zeros_like(acc_ref)

        r = ALPHA * x_ref[...] + _mm(y_ref[...], wo_ref[...])
        mu = jnp.mean(r, axis=-1, keepdims=True)
        d = r - mu
        rstd = lax.rsqrt(jnp.mean(d * d, axis=-1, keepdims=True) + LN_EPS)
        xh = d * rstd
        gam = g_ref[0:1, :]
        e = xh * gam + b_ref[0:1, :] - t_ref[...]
        dout = e * (1.0 / D_MODEL)
        dxh = dout * gam
        dr = rstd * (dxh - jnp.mean(dxh, axis=-1, keepdims=True)
                     - xh * jnp.mean(dxh * xh, axis=-1, keepdims=True))
        dr_ref[...] = dr
        dy_ref[...] = _mm_nt(dr, wo_ref[...])
        acc_ref[...] += _rows8([_colsum(dout * xh), _colsum(dout), _colsum(e * e) * (0.5 / D_MODEL)])

        @pl.when(i == nsteps - 1)
        def _():
            acc = acc_ref[...]
            tot = jnp.sum(acc[2:3, :])
            rid = lax.broadcasted_iota(jnp.int32, (8, 1024), 0)
            acc_ref[...] = jnp.where(rid == 3, tot, acc)

    const = lambda shape: pl.BlockSpec(shape, lambda i: (0, 0))
    row = lambda w: pl.BlockSpec((tm, w), lambda i: (i, 0))
    return pl.pallas_call(
        body, name="head", grid=(nsteps,),
        in_specs=[row(2048), row(1024), row(1024), const((2048, 1024)), const((8, 1024)), const((8, 1024))],
        out_specs=[row(1024), row(2048), const((8, 1024))],
        out_shape=[jax.ShapeDtypeStruct((L, D_MODEL), F32), jax.ShapeDtypeStruct((L, 2048), F32),
                   jax.ShapeDtypeStruct((8, 1024), F32)],
        compiler_params=_params(("arbitrary",)),
    )(y, x, target, w_out, ln_g8, ln_b8)


def _all_gather(shards):
    n = len(shards)
    any_spec = pl.BlockSpec(memory_space=pl.ANY)

    def body(*refs):
        ins, outs = refs[:n], refs[n:2 * n]
        send_sems, recv_sems, local_sems = refs[2 * n:]
        x, y, c = _position()
        me, sibling = (x, y, c), (x, y, 1 - c)
        chips = [(1 - x, y), (x, 1 - y), (1 - x, 1 - y)]

        def copy(a, k, block, to, src=None):
            slot = outs[a].at[_index(*block)]
            return pltpu.make_async_remote_copy(
                src_ref=slot if src is None else src, dst_ref=slot,
                send_sem=send_sems.at[a, k], recv_sem=recv_sems.at[a, k],
                device_id=to, device_id_type=MESH)

        mine = [pltpu.make_async_copy(ins[a], outs[a].at[_index(*me)], local_sems.at[a]) for a in range(n)]
        for cp in mine:
            cp.start()
        first = []
        for a in range(n):
            first.append(copy(a, 0, me, sibling, src=ins[a]))
            first += [copy(a, 1 + j, me, (*chip, c), src=ins[a]) for j, chip in enumerate(chips)]
        for cp in first:
            cp.start()
        passed = []
        for j, chip in enumerate(chips):
            for a in range(n):
                copy(a, 1 + j, (*chip, c), me).wait_recv()
                fwd = copy(a, 4 + j, (*chip, c), sibling)
                fwd.start()
                passed.append(fwd)
        for a in range(n):
            copy(a, 0, sibling, me).wait_recv()
            for j, chip in enumerate(chips):
                copy(a, 4 + j, (*chip, 1 - c), me).wait_recv()
        for cp in first + passed:
            cp.wait_send()
        for cp in mine:
            cp.wait()

    return pl.pallas_call(
        body, name="weight_all_gather",
        in_specs=[any_spec] * n, out_specs=[any_spec] * n,
        out_shape=[jax.ShapeDtypeStruct((N_DEV,) + s.shape, s.dtype) for s in shards],
        scratch_shapes=[pltpu.SemaphoreType.DMA((n, 7)), pltpu.SemaphoreType.DMA((n, 7)),
                        pltpu.SemaphoreType.DMA((n,))],
    )(*shards)


def _input_gradient(d_ssd, d_att, w_ssd, w_att, dr, *, tm, comm=None):
    L = dr.shape[0]

    def body(ds_ref, da_ref, ws_ref, wa_ref, dr_ref, o_ref):
        o_ref[...] = ALPHA * dr_ref[...] + _mm_nt(ds_ref[...], ws_ref[...]) + _mm_nt(da_ref[...], wa_ref[...])

    row = lambda w: pl.BlockSpec((tm, w), lambda i: (i, 0))
    const = lambda shape: pl.BlockSpec(shape, lambda i: (0, 0))
    return _call(body, comm, name="dx", grid=(L // tm,),
                 in_specs=[row(S_W), row(A_W), const((D_MODEL, S_W)), const((D_MODEL, A_W)), row(D_MODEL)],
                 out_specs=[row(D_MODEL)], out_shape=[jax.ShapeDtypeStruct((L, D_MODEL), F32)],
                 scratch_shapes=[], args=(d_ssd, d_att, w_ssd, w_att, dr))


SHARD_COLS = D_IN_PROJ // N_DEV
SPLIT = N_SSD_REAL - 4 * SHARD_COLS
RELAYOUT_ROWS = 256


def _unpack_w_in(w_all):
    def body(g_ref, ws_ref, wa_ref):
        for j in range(4):
            ws_ref[:, SHARD_COLS * j:SHARD_COLS * (j + 1)] = g_ref[j]
        ws_ref[:, 4 * SHARD_COLS:N_SSD_REAL] = g_ref[4, :, 0:SPLIT]
        ws_ref[:, N_SSD_REAL:S_W] = jnp.zeros((RELAYOUT_ROWS, S_W - N_SSD_REAL), ws_ref.dtype)
        wa_ref[:, 0:SHARD_COLS - SPLIT] = g_ref[4, :, SPLIT:SHARD_COLS]
        for j in range(5, N_DEV):
            lo = SHARD_COLS * (j - 4) - SPLIT
            wa_ref[:, lo:lo + SHARD_COLS] = g_ref[j]

    return pl.pallas_call(
        body, name="unpack_w_in", grid=(D_MODEL // RELAYOUT_ROWS,),
        in_specs=[pl.BlockSpec((N_DEV, RELAYOUT_ROWS, SHARD_COLS), lambda i: (0, i, 0))],
        out_specs=[pl.BlockSpec((RELAYOUT_ROWS, S_W), lambda i: (i, 0)), pl.BlockSpec((RELAYOUT_ROWS, A_W), lambda i: (i, 0))],
        out_shape=[jax.ShapeDtypeStruct((D_MODEL, S_W), w_all.dtype), jax.ShapeDtypeStruct((D_MODEL, A_W), w_all.dtype)],
        compiler_params=_params(("arbitrary",)),
    )(w_all)


def _pack_dw_in(me1, dw_ssd, dw_att, half):
    def body(me_ref, *refs):
        if half == 0:
            ds_ref, p_ref, own_ref = refs
            me = me_ref[0]

            @pl.when(me >= 4)
            def _():
                own_ref[...] = jnp.zeros_like(own_ref)
        else:
            ds_ref, da_ref, p_ref = refs

        for j in range(4):
            if half == 0:
                pieces = [(0, ds_ref[:, SHARD_COLS * j:SHARD_COLS * (j + 1)])]
            elif j == 0:
                pieces = [(0, ds_ref[:, 4 * SHARD_COLS:N_SSD_REAL]), (SPLIT, da_ref[:, 0:SHARD_COLS - SPLIT])]
            else:
                lo = SHARD_COLS * j - SPLIT
                pieces = [(0, da_ref[:, lo:lo + SHARD_COLS])]
            for off, blk in pieces:
                p_ref[j, :, off:off + blk.shape[1]] = blk.astype(p_ref.dtype)
                if half == 0:
                    @pl.when(me == j)
                    def _(off=off, blk=blk):
                        own_ref[:, off:off + blk.shape[1]] = blk

    ins = [dw_ssd] if half == 0 else [dw_ssd, dw_att]
    row = lambda a: pl.BlockSpec((RELAYOUT_ROWS, a.shape[1]), lambda i: (i, 0))
    out_specs = [pl.BlockSpec((4, RELAYOUT_ROWS, SHARD_COLS), lambda i: (0, i, 0))]
    out_shape = [jax.ShapeDtypeStruct((4, D_MODEL, SHARD_COLS), BF16 if half == 0 else F32)]
    if half == 0:
        out_specs.append(pl.BlockSpec((RELAYOUT_ROWS, SHARD_COLS), lambda i: (i, 0)))
        out_shape.append(jax.ShapeDtypeStruct((D_MODEL, SHARD_COLS), F32))
    return pl.pallas_call(
        body, name="pack_dw_in_%d" % half, grid=(D_MODEL // RELAYOUT_ROWS,),
        in_specs=[pl.BlockSpec(memory_space=pltpu.SMEM)] + [row(a) for a in ins],
        out_specs=out_specs, out_shape=out_shape, compiler_params=_params(("arbitrary",)),
    )(me1, *ins)


def _pair_swap(stack):
    def body(in_ref, out_ref, send_sems, recv_sems):
        x, y, c = _position()
        cps = [_remote(in_ref.at[2 * oy + (1 - c)], out_ref.at[oy], send_sems.at[oy], recv_sems.at[oy], (x, y, 1 - c))
               for oy in range(2)]
        for cp in cps:
            cp.start()
        for cp in cps:
            cp.wait_recv()
        for cp in cps:
            cp.wait_send()

    any_spec = pl.BlockSpec(memory_space=pl.ANY)
    return pl.pallas_call(
        body, name="pair_swap", in_specs=[any_spec], out_specs=any_spec,
        out_shape=jax.ShapeDtypeStruct((2,) + stack.shape[1:], stack.dtype),
        scratch_shapes=[pltpu.SemaphoreType.DMA((2,)), pltpu.SemaphoreType.DMA((2,))],
    )(stack)


def _pair_sum(pos3, stack, swapped):
    def body(pos_ref, a_ref, b_ref, chip_ref, own_ref):
        oy = pl.program_id(1)
        t = a_ref[0] + b_ref[0]
        chip_ref[0] = t.astype(chip_ref.dtype)

        @pl.when((pos_ref[0] == 0) & (oy == 0))
        def _():
            own_ref[...] = jnp.zeros_like(own_ref)

        @pl.when((pos_ref[0] == 1) & (oy == pos_ref[1]))
        def _():
            own_ref[...] = t

    blk = (1, RELAYOUT_ROWS, SHARD_COLS)
    return pl.pallas_call(
        body, name="pair_sum",
        grid_spec=pltpu.PrefetchScalarGridSpec(
            num_scalar_prefetch=1, grid=(D_MODEL // RELAYOUT_ROWS, 2),
            in_specs=[pl.BlockSpec(blk, lambda i, oy, pos: (2 * oy + pos[2], i, 0)),
                      pl.BlockSpec(blk, lambda i, oy, pos: (oy, i, 0))],
            out_specs=[pl.BlockSpec(blk, lambda i, oy, pos: (oy, i, 0)),
                       pl.BlockSpec((RELAYOUT_ROWS, SHARD_COLS), lambda i, oy, pos: (i, 0))]),
        out_shape=[jax.ShapeDtypeStruct((2, D_MODEL, SHARD_COLS), BF16), jax.ShapeDtypeStruct((D_MODEL, SHARD_COLS), F32)],
        compiler_params=_params(("arbitrary", "arbitrary")),
    )(pos3, stack, swapped)


def _adamw_math(w, g, m, v):
    m = ADAM_B1 * m + (1.0 - ADAM_B1) * g
    v = ADAM_B2 * v + (1.0 - ADAM_B2) * (g * g)
    m_hat = m / (1.0 - ADAM_B1 ** ADAM_STEP)
    v_hat = v / (1.0 - ADAM_B2 ** ADAM_STEP)
    delta = -ADAM_LR * (m_hat / (jnp.sqrt(v_hat) + ADAM_EPS) + ADAM_WD * w)
    return delta, m, v


def _adamw_shard(n_recv, g_own, recv, w, m, v, *, rows, name):
    R, C = g_own.shape

    def body(n_ref, g_ref, r_ref, w_ref, m_ref, v_ref, go_ref, d_ref, mo_ref, vo_ref):
        g = g_ref[...]
        for k in range(N_DEV - 1):
            g = g + jnp.where(k < n_ref[0], r_ref[k].astype(F32), 0.0)
        d, mn, vn = _adamw_math(w_ref[...], g, m_ref[...], v_ref[...])
        go_ref[...] = g
        d_ref[...] = d
        mo_ref[...] = mn
        vo_ref[...] = vn

    blk = pl.BlockSpec((rows, C), lambda i: (i, 0))
    return pl.pallas_call(
        body, name=name, grid=(R // rows,),
        in_specs=[pl.BlockSpec(memory_space=pltpu.SMEM), blk,
                  pl.BlockSpec((N_DEV - 1, rows, C), lambda i: (0, i, 0)), blk, blk, blk],
        out_specs=[blk] * 4, out_shape=[jax.ShapeDtypeStruct((R, C), F32)] * 4,
        compiler_params=_params(("arbitrary",)),
    )(n_recv, g_own, recv, w, m, v)


def _adamw_small(gathered, w, m, v):
    def body(r_ref, w_ref, m_ref, v_ref, go_ref, d_ref, mo_ref, vo_ref):
        g = r_ref[0]
        for k in range(1, N_DEV):
            g = g + r_ref[k]
        d, mn, vn = _adamw_math(w_ref[...], g, m_ref[...], v_ref[...])
        go_ref[...] = g
        d_ref[...] = d
        mo_ref[...] = mn
        vo_ref[...] = vn

    return pl.pallas_call(
        body, name="adamw_small", out_shape=[jax.ShapeDtypeStruct(w.shape, F32)] * 4,
        compiler_params=_params(),
    )(gathered, w, m, v)


def _adamw_plain(g, w, m, v):
    def body(g_ref, w_ref, m_ref, v_ref, d_ref, mo_ref, vo_ref):
        d, mn, vn = _adamw_math(w_ref[...], g_ref[...], m_ref[...], v_ref[...])
        d_ref[...] = d
        mo_ref[...] = mn
        vo_ref[...] = vn

    return pl.pallas_call(
        body, name="adamw_conv_w", out_shape=[jax.ShapeDtypeStruct(w.shape, F32)] * 3,
        compiler_params=_params(),
    )(g, w, m, v)


def _pad_rows8(v):
    v = v.reshape(-1, v.shape[-1])
    return jnp.pad(v, ((0, 8 - v.shape[0]), (0, 0)))


def _pad_lanes(v, n):
    return jnp.pad(v, ((0, 0), (0, n - v.shape[1])))


def _pack_small(loss, conv_b, dt_bias, a_log, d_skip, sinks, norm_w, ln_g, ln_b, conv_w):
    def seg(v, n):
        v = v.reshape(-1).astype(F32)
        return jnp.pad(v, (0, n - v.shape[0]))

    flat = jnp.concatenate([seg(loss, 128), seg(conv_b, 1536), seg(dt_bias, 128), seg(a_log, 128),
                            seg(d_skip, 128), seg(sinks, 128), seg(norm_w, 1024), seg(ln_g, 1024),
                            seg(ln_b, 1024), seg(conv_w, 6144), jnp.zeros((P_ROWS * 128 - P_END,), F32)])
    return flat.reshape(P_ROWS, 128)


def _unpack_small(p):
    f = p.reshape(-1)
    return dict(conv_b=f[P_CONVB:P_CONVB + 1536].reshape(1, 1536), dt_bias=f[P_DTB:P_DTB + 16].reshape(1, 16),
                a_log=f[P_ALOG:P_ALOG + 16].reshape(1, 16), d_skip=f[P_DSKIP:P_DSKIP + 16].reshape(1, 16),
                attn_sinks=f[P_SINK:P_SINK + 16].reshape(1, 16), ssd_norm_w=f[P_NORMW:P_NORMW + 1024].reshape(1, 1024),
                ln_g=f[P_LNG:P_LNG + 1024].reshape(1, 1024), ln_b=f[P_LNB:P_LNB + 1024].reshape(1, 1024),
                conv_w=f[P_CONVW:P_CONVW + 6144].reshape(4, 1536), loss=f[P_LOSS])


def _lane_pattern(fn):
    return np.asarray([fn(l % HEAD_DIM) for l in range(128)], np.float32)


ROPE_INV = _lane_pattern(lambda r: ROPE_THETA ** (-2.0 * (r % 8) / ROPE_DIM) if r < ROPE_DIM else 0.0)
ROPE_SIN_A = _lane_pattern(lambda r: 1.0 if 8 <= r < ROPE_DIM else 0.0)
ROPE_SIN_B = _lane_pattern(lambda r: -1.0 if r < 8 else 0.0)


def _rope_tables(positions):
    ang = positions.astype(F32)[:, None] * ROPE_INV[None, :]
    sn = jnp.sin(ang)
    return jnp.concatenate([jnp.cos(ang), sn * ROPE_SIN_A[None, :], sn * ROPE_SIN_B[None, :]], axis=1)


def _expansion():
    E = np.arange(1024)[None, :] // HEAD_DIM == np.arange(128)[:, None]
    return jnp.asarray(E, BF16), jnp.asarray(E.T, BF16)


def _small_operands(positions, conv_w, conv_b, dt_bias, a_log, d_skip, norm_w, sinks, ln_g, ln_b):
    E, ET = _expansion()
    return dict(conv_w8=_pad_rows8(conv_w), conv_b8=_pad_rows8(conv_b),
                dtb8=_pad_rows8(_pad_lanes(dt_bias, 128)), alog8=_pad_rows8(_pad_lanes(a_log, 128)),
                dskip_e=_pad_rows8(jnp.repeat(d_skip, HEAD_DIM, axis=1)), norm_w8=_pad_rows8(norm_w),
                ln_g8=_pad_rows8(ln_g), ln_b8=_pad_rows8(ln_b), sinks=sinks.reshape(-1),
                tabs=_rope_tables(positions), E=E, ET=ET)


def _ssd_args(s):
    return (s["conv_w8"], s["conv_b8"], s["dtb8"], s["alog8"], s["dskip_e"], s["norm_w8"], s["E"])


def _pack_small_grads(acc_head, acc_cw, acc_w, acc_s, dsink):
    return _pack_small(acc_head[3, 0], acc_cw[4], acc_s[0, :16], acc_s[1, :16], acc_s[2, :16], dsink[0, :16],
                       acc_w[0], acc_head[0], acc_head[1], acc_cw[0:4])


def kernel(x, positions, w_in, conv_w, conv_b, dt_bias, a_log, d_skip, ssd_norm_w, attn_sinks, w_out, ln_g, ln_b, loss_target, m_w_in, m_conv_w, m_conv_b, m_dt_bias, m_a_log, m_d_skip, m_ssd_norm_w, m_attn_sinks, m_w_out, m_ln_g, m_ln_b, v_w_in, v_conv_w, v_conv_b, v_dt_bias, v_a_log, v_d_skip, v_ssd_norm_w, v_attn_sinks, v_w_out, v_ln_g, v_ln_b):
    me = _index(*_position())
    me1 = me.reshape(1).astype(jnp.int32)
    x0, target = x[0], loss_target[0]
    xb = x0.astype(_MXU)
    bf16_shard = lambda shape: jax.ShapeDtypeStruct(shape, BF16)

    w_in_all, conv_w_all = _all_gather([w_in[0].astype(BF16), conv_w[0]])
    w_ssd, w_att = _unpack_w_in(w_in_all)
    conv_w_f = jnp.transpose(conv_w_all, (1, 0, 2)).reshape(4, D_XBC)
    s = _small_operands(positions[0], conv_w_f, conv_b, dt_bias, a_log, d_skip, ssd_norm_w, attn_sinks, ln_g, ln_b)

    proj_ssd = _matmul(xb, w_ssd, tm=1024, tn=S_W // 2, name="in_proj_ssd")
    proj_att = _matmul(xb, w_att, tm=1024, tn=A_W // 2, name="in_proj_att")
    gather_w_out = _Hosted([w_out[0].astype(BF16)], [bf16_shard((N_DEV, 256, D_MODEL))], [_Flow("gather", 0, 0)])
    y, ypre, hprev, w_out_all = _ssd_forward(proj_ssd, *_ssd_args(s), comm=gather_w_out)
    w_out_f = w_out_all.reshape(2 * D_MODEL, D_MODEL)
    y = _swa_forward(proj_att, s["tabs"], s["sinks"], y)
    dr, dy, acc_head = _head(y, x0, target, w_out_f, s["ln_g8"], s["ln_b8"], tm=256)

    dw_out_parts = _matmul_tn(y, dr, tl=512, tn=D_MODEL, name="dw_out").reshape(N_DEV, 256, D_MODEL)
    own_out = lax.dynamic_index_in_dim(dw_out_parts, me, axis=0, keepdims=False)
    send_out = _Hosted([dw_out_parts.astype(BF16)], [bf16_shard((N_DEV - 1, 256, D_MODEL))], [_Flow("exchange", 0, 0)])
    d_ssd, acc_cw, acc_w, acc_s, recv_out = _ssd_backward(proj_ssd, hprev, ypre, dy, *_ssd_args(s), s["ET"],
                                                          comm=send_out)
    dw_ssd = _matmul_tn(xb, d_ssd, tl=512, tn=S_W // 2, name="dw_in_ssd")
    parts_lo, own_lo = _pack_dw_in(me1, dw_ssd, None, 0)
    recv_shape = bf16_shard((N_DEV - 1, D_MODEL, SHARD_COLS))
    send_lo = _Hosted([parts_lo], [recv_shape], [_Flow("exchange", 0, 0, target_x=0)])
    d_att, dsink, recv_in = _swa_backward(proj_att, s["tabs"], s["sinks"], dy, comm=send_lo)
    dw_att = _matmul_tn(xb, d_att, tl=512, tn=A_W // 2, name="dw_in_att")
    (stack_hi,) = _pack_dw_in(me1, dw_ssd, dw_att, 1)
    pos3 = jnp.stack(_position()).astype(jnp.int32)
    chip_hi, own_hi = _pair_sum(pos3, stack_hi, _pair_swap(stack_hi))
    small = _pack_small_grads(acc_head, acc_cw, acc_w, acc_s, dsink)
    send_hi = _Hosted([chip_hi, small, recv_in],
                      [recv_shape, jax.ShapeDtypeStruct((N_DEV,) + small.shape, F32)],
                      [_Flow("chip_exchange", 0, 0, target_x=1), _Flow("gather", 1, 1)], aliases={2: 0})
    dx, recv_in, small_all = _input_gradient(d_ssd, d_att, w_ssd, w_att, dr, tm=256, comm=send_hi)
    own_in = jnp.where(me < 4, own_lo, own_hi)
    n_recv_in = jnp.where(me < 4, N_DEV - 1, 3).reshape(1).astype(jnp.int32)
    n_recv_out = jnp.full((1,), N_DEV - 1, jnp.int32)

    g_in, d_in, nm_in, nv_in = _adamw_shard(n_recv_in, own_in, recv_in, w_in[0], m_w_in[0], v_w_in[0], rows=256,
                                            name="adamw_w_in")
    g_out, d_out, nm_out, nv_out = _adamw_shard(n_recv_out, own_out, recv_out, w_out[0], m_w_out[0], v_w_out[0],
                                                rows=256, name="adamw_w_out")
    zero = jnp.zeros((), F32)
    pack = lambda d, cw: _pack_small(zero, d["conv_b"], d["dt_bias"], d["a_log"], d["d_skip"], d["attn_sinks"],
                                     d["ssd_norm_w"], d["ln_g"], d["ln_b"], cw)
    zero_cw = jnp.zeros((4, D_XBC), F32)
    wp = pack(dict(conv_b=conv_b, dt_bias=dt_bias, a_log=a_log, d_skip=d_skip, attn_sinks=attn_sinks,
                   ssd_norm_w=ssd_norm_w, ln_g=ln_g, ln_b=ln_b), zero_cw)
    mp = pack(dict(conv_b=m_conv_b, dt_bias=m_dt_bias, a_log=m_a_log, d_skip=m_d_skip, attn_sinks=m_attn_sinks,
                   ssd_norm_w=m_ssd_norm_w, ln_g=m_ln_g, ln_b=m_ln_b), zero_cw)
    vp = pack(dict(conv_b=v_conv_b, dt_bias=v_dt_bias, a_log=v_a_log, d_skip=v_d_skip, attn_sinks=v_attn_sinks,
                   ssd_norm_w=v_ssd_norm_w, ln_g=v_ln_g, ln_b=v_ln_b), zero_cw)
    gs, ds_, ms_, vs_ = [_unpack_small(p) for p in _adamw_small(small_all, wp, mp, vp)]
    g_cw = lax.dynamic_slice_in_dim(gs["conv_w"], me * (D_XBC // N_DEV), D_XBC // N_DEV, axis=1)
    d_cw, nm_cw, nv_cw = _adamw_plain(g_cw, conv_w[0], m_conv_w[0], v_conv_w[0])

    names = ["conv_b", "dt_bias", "a_log", "d_skip", "ssd_norm_w", "attn_sinks"]

    def leaves(big_in, cw, small_d, big_out):
        return ([big_in[None], cw[None]] + [small_d[k] for k in names] + [big_out[None], small_d["ln_g"], small_d["ln_b"]])

    return (gs["loss"], dx[None],
            *leaves(g_in, g_cw, gs, g_out), *leaves(d_in, d_cw, ds_, d_out),
            *leaves(nm_in, nm_cw, ms_, nm_out), *leaves(nv_in, nv_cw, vs_, nv_out))
```

```python
import jax
import jax.numpy as jnp
from jax import lax
from jax.experimental import pallas as pl
from jax.experimental.pallas import tpu as pltpu
import numpy as np

F32 = jnp.float32
BF16 = jnp.bfloat16
_MXU = jnp.bfloat16

N_DEV = 8
D_MODEL = 1024
D_SSD = 1024
D_ATT = 1024
HEAD_DIM = 64
N_HEADS = 16
SSD_GROUPS = 2
KV_HEADS = 4
CHUNK = 128
D_XBC = 1536
D_IN_PROJ = 5136
ROPE_DIM = 16
ROPE_THETA = 500000.0
ALPHA = (2.0 * 1) ** 0.25
LN_EPS = 1e-5
RMS_EPS = 1e-5
ATT_SCALE = HEAD_DIM ** -0.5
NEG = -1e30

S_Z, S_XS, S_B, S_C, S_DT, S_W = 0, 1024, 2048, 2304, 2560, 2816
N_SSD_REAL = 2576
A_Q, A_K, A_V, A_G, A_W = 0, 1024, 1280, 1536, 2560

ADAM_LR = 0.001
ADAM_B1 = 0.9
ADAM_B2 = 0.999
ADAM_EPS = 1e-08
ADAM_WD = 0.01
ADAM_STEP = 10

VMEM_LIMIT = 48 * 1024 * 1024
MESH = pl.DeviceIdType.MESH


def _params(sem=None):
    return pltpu.CompilerParams(dimension_semantics=sem, vmem_limit_bytes=VMEM_LIMIT)


def _mm(a, b):
    return jnp.dot(a.astype(_MXU), b.astype(_MXU), preferred_element_type=F32)


def _mm_nt(a, b):
    return lax.dot_general(a.astype(_MXU), b.astype(_MXU), (((1,), (1,)), ((), ())),
                           preferred_element_type=F32)


def _mm_tn(a, b):
    return lax.dot_general(a.astype(_MXU), b.astype(_MXU), (((0,), (0,)), ((), ())),
                           preferred_element_type=F32)


def _split3(v):
    hi = v.astype(BF16)
    r = v - hi.astype(F32)
    mid = r.astype(BF16)
    lo = (r - mid.astype(F32)).astype(BF16)
    return hi, mid, lo


def _mm_exact_r(v, p01):
    hi, mid, lo = _split3(v)
    d = lambda a: jnp.dot(a, p01, preferred_element_type=F32)
    return d(hi) + d(mid) + d(lo)


def _mm_exact_l(p01, v):
    hi, mid, lo = _split3(v)
    d = lambda a: jnp.dot(p01, a, preferred_element_type=F32)
    return d(hi) + d(mid) + d(lo)


def _sigmoid(x):
    return 1.0 / (1.0 + jnp.exp(-x))


def _softplus(x):
    e = jnp.exp(-jnp.abs(x))
    u = 1.0 + e
    log1p = jnp.where(u == 1.0, e, jnp.log(u) * (e / (u - 1.0)))
    return jnp.maximum(x, 0.0) + log1p


def _rows8(rows):
    n = rows[0].shape[1]
    rid = lax.broadcasted_iota(jnp.int32, (8, n), 0)
    out = jnp.zeros((8, n), F32)
    for k, r in enumerate(rows):
        out = out + jnp.where(rid == k, r, 0.0)
    return out


def _colsum(a):
    return jnp.sum(a, axis=0, keepdims=True)


def _matmul(a, b, *, tm, tn, name, emit_a=False):
    M, K = a.shape
    N = b.shape[1]

    def body(a_ref, b_ref, o_ref, *rest):
        am = a_ref[...].astype(_MXU)
        o_ref[...] = jnp.dot(am, b_ref[...].astype(_MXU), preferred_element_type=F32)
        if emit_a:
            rest[0][...] = am

    out_specs = [pl.BlockSpec((tm, tn), lambda i, j: (i, j))]
    out_shape = [jax.ShapeDtypeStruct((M, N), F32)]
    if emit_a:
        out_specs.append(pl.BlockSpec((tm, K), lambda i, j: (i, 0)))
        out_shape.append(jax.ShapeDtypeStruct((M, K), _MXU))
    res = pl.pallas_call(
        body, name=name, grid=(M // tm, N // tn),
        in_specs=[pl.BlockSpec((tm, K), lambda i, j: (i, 0)), pl.BlockSpec((K, tn), lambda i, j: (0, j))],
        out_specs=out_specs, out_shape=out_shape, compiler_params=_params(("arbitrary", "arbitrary")),
    )(a, b)
    return res if emit_a else res[0]


def _matmul_tn(a, g, *, tl, tn, name, emit_bf16=False):
    L, M = a.shape
    N = g.shape[1]
    last = L // tl - 1

    def body(a_ref, g_ref, o_ref, *rest):
        @pl.when(pl.program_id(1) == 0)
        def _():
            o_ref[...] = jnp.zeros_like(o_ref)

        o_ref[...] += _mm_tn(a_ref[...], g_ref[...])
        if emit_bf16:
            @pl.when(pl.program_id(1) == last)
            def _():
                rest[0][...] = o_ref[...].astype(BF16)

    spec = pl.BlockSpec((M, tn), lambda j, l: (0, j))
    res = pl.pallas_call(
        body, name=name, grid=(N // tn, L // tl),
        in_specs=[pl.BlockSpec((tl, M), lambda j, l: (l, 0)), pl.BlockSpec((tl, tn), lambda j, l: (l, j))],
        out_specs=[spec, spec] if emit_bf16 else [spec],
        out_shape=[jax.ShapeDtypeStruct((M, N), F32)] + ([jax.ShapeDtypeStruct((M, N), BF16)] if emit_bf16 else []),
        compiler_params=_params(("arbitrary", "arbitrary")),
    )(a, g)
    return res if emit_bf16 else res[0]


def _position():
    return lax.axis_index("x"), lax.axis_index("y"), lax.axis_index("c")


def _index(px, py, pc):
    return 4 * px + 2 * py + pc


def _flip(pos, k):
    x, y, c = pos
    return ((1 - x) if (k >> 2) & 1 else x, (1 - y) if (k >> 1) & 1 else y, (1 - c) if k & 1 else c)


def _when(cond, fn):
    if cond is True:
        fn()
    else:
        pl.when(cond)(fn)


def _remote(src, dst, send_sem, recv_sem, peer):
    return pltpu.make_async_remote_copy(src_ref=src, dst_ref=dst, send_sem=send_sem, recv_sem=recv_sem,
                                        device_id=peer, device_id_type=MESH)


class _Flow:
    def __init__(self, kind, operand, result, target_x=None):
        self.kind, self.operand, self.result, self.target_x = kind, operand, result, target_x


class _Hosted:
    def __init__(self, operands, out_shapes, flows, aliases=None):
        self.operands, self.out_shapes, self.flows = operands, out_shapes, flows
        self.aliases = aliases or {}

    def plan(self, ins, outs, send_sems, recv_sems, local_sems):
        me = _position()
        mi = _index(*me)
        sends, recvs, locals_ = [], [], []
        for row, f in enumerate(self.flows):
            src, dst = ins[f.operand], outs[f.result]
            for k in range(1, N_DEV):
                peer = _flip(me, k)
                sems = (send_sems.at[row, k - 1], recv_sems.at[row, k - 1])
                if f.kind == "exchange":
                    owner = _index(*peer) if f.target_x is None else 2 * peer[1] + peer[2]
                    cp = _remote(src.at[owner], dst.at[k - 1], *sems, peer)
                    to_peer = True if f.target_x is None else peer[0] == f.target_x
                    to_me = True if f.target_x is None else me[0] == f.target_x
                    sends.append((to_peer, cp))
                    recvs.append((to_me, cp))
                elif f.kind == "chip_exchange":
                    if k & 1:
                        continue
                    cp = _remote(src.at[peer[1]], dst.at[k // 2 - 1], *sems, peer)
                    sends.append((peer[0] == f.target_x, cp))
                    recvs.append((me[0] == f.target_x, cp))
                else:
                    sends.append((True, _remote(src, dst.at[mi], *sems, peer)))
                    recvs.append((True, _remote(src, dst.at[_index(*peer)], *sems, peer)))
            if f.kind == "gather":
                locals_.append(pltpu.make_async_copy(src, dst.at[mi], local_sems.at[row]))

        def start():
            for cp in locals_:
                cp.start()
            for cond, cp in sends:
                _when(cond, cp.start)

        def wait():
            for cond, cp in recvs:
                _when(cond, cp.wait_recv)
            for cond, cp in sends:
                _when(cond, cp.wait_send)
            for cp in locals_:
                cp.wait()

        return start, wait


def _call(body, comm, *, name, grid, in_specs, out_specs, out_shape, scratch_shapes, args, aliases=None):
    io_alias = dict(aliases or {})
    if comm is None:
        return pl.pallas_call(body, name=name, grid=grid, in_specs=in_specs, out_specs=out_specs, out_shape=out_shape,
                              scratch_shapes=scratch_shapes, input_output_aliases=io_alias,
                              compiler_params=_params(("arbitrary",)))(*args)
    n_in, n_out, n_scr = len(args), len(out_shape), len(scratch_shapes)
    c_in, c_out, rows = len(comm.operands), len(comm.out_shapes), len(comm.flows)

    def hosted(*refs):
        ins, refs = refs[:n_in], refs[n_in:]
        cins, refs = refs[:c_in], refs[c_in:]
        outs, refs = refs[:n_out], refs[n_out:]
        couts, refs = refs[:c_out], refs[c_out:]
        scr, (send_sems, recv_sems, local_sems) = refs[:n_scr], refs[n_scr:]
        start, wait = comm.plan(cins, couts, send_sems, recv_sems, local_sems)
        pl.when(pl.program_id(0) == 0)(start)
        body(*ins, *outs, *scr)
        pl.when(pl.program_id(0) == grid[0] - 1)(wait)

    for ci, co in comm.aliases.items():
        io_alias[n_in + ci] = n_out + co
    any_spec = pl.BlockSpec(memory_space=pl.ANY)
    sems = [pltpu.SemaphoreType.DMA((rows, N_DEV - 1)), pltpu.SemaphoreType.DMA((rows, N_DEV - 1)),
            pltpu.SemaphoreType.DMA((rows,))]
    return pl.pallas_call(
        hosted, name=name, grid=grid, in_specs=list(in_specs) + [any_spec] * c_in,
        out_specs=list(out_specs) + [any_spec] * c_out, out_shape=list(out_shape) + list(comm.out_shapes),
        scratch_shapes=list(scratch_shapes) + sems, input_output_aliases=io_alias,
        compiler_params=_params(("arbitrary",)))(*args, *comm.operands)


def _head_row(ref, width, rep):
    hid = lax.broadcasted_iota(jnp.int32, (1, width), 1) // rep
    row = jnp.zeros((1, width), F32)
    for h in range(N_HEADS):
        row = jnp.where(hid == h, ref[h], row)
    return row


def _ssd_recompute(first, p_ref, halo_ref, cw_ref, cb_ref, dtb_ref, alog_ref, e_ref, ext_scr):
    ext_scr[0:8, :] = jnp.where(first, 0.0, halo_ref[:, S_XS:S_DT])
    ext_scr[8:136, :] = p_ref[:, S_XS:S_DT]
    cw = cw_ref[...]
    pre = (cb_ref[0:1, :] + cw[3:4, :] * ext_scr[8:136, :] + cw[2:3, :] * ext_scr[7:135, :]
           + cw[1:2, :] * ext_scr[6:134, :] + cw[0:1, :] * ext_scr[5:133, :])
    sg = _sigmoid(pre)
    act = pre * sg
    lane = lax.broadcasted_iota(jnp.int32, (1, 128), 1)
    A = jnp.where(lane < N_HEADS, -jnp.exp(_head_row(alog_ref, 128, 1)), 0.0)
    raw = p_ref[:, S_DT:S_DT + 128] + _head_row(dtb_ref, 128, 1)
    dt = _softplus(raw)
    dA = dt * A
    row = lax.broadcasted_iota(jnp.int32, (128, 128), 0)
    col = lax.broadcasted_iota(jnp.int32, (128, 128), 1)
    tril = (row >= col).astype(BF16)
    acs = _mm_exact_l(tril, dA)
    last = acs[127:128, :]
    ds = jnp.exp(last - acs)
    eo = jnp.exp(acs)
    E = e_ref[...]
    ex = _mm_exact_r(jnp.concatenate([dt, ds, eo], axis=0), E)
    dt_e, ds_e, eo_e = ex[0:128], ex[128:256], ex[256:384]
    xs_c = act[:, 0:1024]
    X = xs_c * dt_e
    return dict(pre=pre, sg=sg, xs_c=xs_c, Bc=act[:, 1024:1280], Cc=act[:, 1280:1536], A=A, raw=raw, dt=dt,
                acs=acs, acsT=acs.T, eo_e=eo_e, ds_e=ds_e, dt_e=dt_e, cd_e=eo_e[127:128, :],
                X=X, Xd=X * ds_e, row=row, col=col)


def _split_halves(t):
    lo = _lo_half(CHUNK)
    return jnp.concatenate([jnp.where(lo, t, 0.0), jnp.where(lo, 0.0, t)], axis=0)


def _ssd_core(R, hprev):
    causal = R["row"] >= R["col"]
    acs, acsT, X = R["acs"], R["acsT"], R["X"]
    ydiag, yoff, snew = [], [], []
    for g in range(SSD_GROUPS):
        Bg = R["Bc"][:, g * 128:(g + 1) * 128]
        Cg = R["Cc"][:, g * 128:(g + 1) * 128]
        cols = slice(g * 512, (g + 1) * 512)
        CB = _mm_nt(Cg, Bg)
        snew.append(_mm_tn(Bg, R["Xd"][:, cols]))
        yoff.append(_mm(Cg, hprev[:, cols]))
        for j in range(4):
            h0 = g * 8 + 2 * j
            ms = [CB * jnp.exp(jnp.where(causal, acs[:, h:h + 1] - acsT[h:h + 1, :], NEG)) for h in (h0, h0 + 1)]
            ydiag.append(_mm(jnp.concatenate(ms, axis=1), _split_halves(X[:, h0 * HEAD_DIM:h0 * HEAD_DIM + 128])))
    Y = jnp.concatenate(ydiag, axis=1) + jnp.concatenate(yoff, axis=1) * R["eo_e"]
    return Y, jnp.concatenate(snew, axis=1)


def _ssd_forward(proj_ssd, conv_w8, conv_b8, dtb8, alog8, dskip_e, norm_w, E, comm=None):
    L = proj_ssd.shape[0]
    nc = L // CHUNK

    def body(p_ref, halo_ref, cw_ref, cb_ref, dtb_ref, alog_ref, dsk_ref, nw_ref, e_ref,
             y_ref, ypre_ref, hprev_ref, h_scr, ext_scr):
        c = pl.program_id(0)
        first = c == 0

        @pl.when(first)
        def _():
            h_scr[...] = jnp.zeros_like(h_scr)

        R = _ssd_recompute(first, p_ref, halo_ref, cw_ref, cb_ref, dtb_ref, alog_ref, e_ref, ext_scr)
        hprev = h_scr[...]
        hprev_ref[...] = hprev
        Y, snew = _ssd_core(R, hprev)
        h_scr[...] = hprev * R["cd_e"] + snew
        Y = Y + _head_row(dsk_ref, D_SSD, HEAD_DIM) * R["xs_c"]
        ypre_ref[...] = Y
        z = p_ref[:, S_Z:S_Z + 1024]
        yf = Y * (z * _sigmoid(z))
        outs = []
        for g in range(SSD_GROUPS):
            yg = yf[:, g * 512:(g + 1) * 512]
            r = lax.rsqrt(jnp.mean(yg * yg, axis=-1, keepdims=True) + RMS_EPS)
            outs.append(yg * r)
        y_ref[...] = (jnp.concatenate(outs, axis=1) * nw_ref[0:1, :]).astype(y_ref.dtype)

    const = lambda shape: pl.BlockSpec(shape, lambda c: (0, 0))
    smem = pl.BlockSpec(memory_space=pltpu.SMEM)
    return _call(
        body, comm, name="ssd_fwd", grid=(nc,),
        in_specs=[pl.BlockSpec((CHUNK, S_W), lambda c: (c, 0)),
                  pl.BlockSpec((8, S_W), lambda c: (jnp.maximum(c * 16 - 1, 0), 0)),
                  const((4, D_XBC)), const((1, D_XBC)), smem, smem, smem, const((1, 1024)), const((128, 1024))],
        out_specs=[pl.BlockSpec((CHUNK, D_SSD), lambda c: (c, 0)), pl.BlockSpec((CHUNK, D_SSD), lambda c: (c, 0)),
                   pl.BlockSpec((128, 1024), lambda c: (c, 0))],
        out_shape=[jax.ShapeDtypeStruct((L, D_SSD + D_ATT), _MXU), jax.ShapeDtypeStruct((L, D_SSD), F32),
                   jax.ShapeDtypeStruct((nc * 128, 1024), F32)],
        scratch_shapes=[pltpu.VMEM((128, 1024), F32), pltpu.VMEM((136, D_XBC), F32)],
        args=(proj_ssd, proj_ssd, conv_w8, conv_b8, dtb8, alog8, dskip_e, norm_w, E))


def _ssd_backward(proj_ssd, hprev_all, ypre, dy, conv_w8, conv_b8, dtb8, alog8, dskip_e, norm_w, E, ET, comm=None):
    L = proj_ssd.shape[0]
    nc = L // CHUNK

    def body(p_ref, halo_ref, hprev_ref, ypre_ref, dy_ref, cw_ref, cb_ref, dtb_ref, alog_ref, dsk_ref, nw_ref, e_ref,
             et_ref, dp_ref, acc_cw_ref, acc_w_ref, acc_s_ref, dh_scr, ext_scr, ext2_scr, nxt_scr):
        i = pl.program_id(0)
        c = nc - 1 - i
        first = c == 0

        @pl.when(i == 0)
        def _():
            dh_scr[...] = jnp.zeros_like(dh_scr)
            nxt_scr[...] = jnp.zeros_like(nxt_scr)
            acc_cw_ref[...] = jnp.zeros_like(acc_cw_ref)
            acc_w_ref[...] = jnp.zeros_like(acc_w_ref)
            acc_s_ref[...] = jnp.zeros_like(acc_s_ref)

        R = _ssd_recompute(first, p_ref, halo_ref, cw_ref, cb_ref, dtb_ref, alog_ref, e_ref, ext_scr)
        hprev = hprev_ref[...]
        xs_c, X, Xd = R["xs_c"], R["X"], R["Xd"]
        acs, acsT = R["acs"], R["acsT"]
        ET = et_ref[...]
        dsk = _head_row(dsk_ref, D_SSD, HEAD_DIM)
        Y = ypre_ref[...]

        z = p_ref[:, S_Z:S_Z + 1024]
        sz = _sigmoid(z)
        silz = z * sz
        yf = Y * silz
        dyv = dy_ref[...]
        nw = nw_ref[0:1, :]
        dyf_parts, dnw_parts = [], []
        for g in range(SSD_GROUPS):
            cols = slice(g * 512, (g + 1) * 512)
            yg = yf[:, cols]
            r = lax.rsqrt(jnp.mean(yg * yg, axis=-1, keepdims=True) + RMS_EPS)
            yn = yg * r
            dyn = dyv[:, cols] * nw[:, cols]
            dnw_parts.append(_colsum(dyv[:, cols] * yn))
            dyf_parts.append(r * (dyn - yn * jnp.mean(dyn * yn, axis=-1, keepdims=True)))
        dyf = jnp.concatenate(dyf_parts, axis=1)
        dY = dyf * silz
        dz = dyf * Y * (sz * (1.0 + z * (1.0 - sz)))

        dhn = dh_scr[...]
        dYo = dY * R["eo_e"]
        causal = R["row"] >= R["col"]
        dacs = jnp.zeros((128, 128), F32)
        dacs_t = jnp.zeros((128, 128), F32)
        dxdiag, dxd, dhprev, dBs, dCs, yoff = [], [], [], [], [], []
        for g in range(SSD_GROUPS):
            Bg = R["Bc"][:, g * 128:(g + 1) * 128]
            Cg = R["Cc"][:, g * 128:(g + 1) * 128]
            cols = slice(g * 512, (g + 1) * 512)
            CB = _mm_nt(Cg, Bg)
            dCB = jnp.zeros((128, 128), F32)
            for j in range(4):
                h0 = g * 8 + 2 * j
                pc = slice(h0 * HEAD_DIM, h0 * HEAD_DIM + 128)
                dYst = _split_halves(dY[:, pc])
                dMst = _mm_nt(dYst, X[:, pc])
                mts = []
                for a, h in enumerate((h0, h0 + 1)):
                    acol = acs[:, h:h + 1]
                    arow = acsT[h:h + 1, :]
                    Lm = jnp.exp(jnp.where(causal, acol - arow, NEG))
                    M = CB * Lm
                    dM = dMst[a * 128:(a + 1) * 128]
                    dCB = dCB + dM * Lm
                    G = dM * M
                    dacs = dacs + jnp.where(R["col"] == h, jnp.sum(G, axis=1, keepdims=True), 0.0)
                    dacs_t = dacs_t + jnp.where(R["row"] == h, jnp.sum(G, axis=0, keepdims=True), 0.0)
                    mts.append(M.T)
                dxdiag.append(_mm(jnp.concatenate(mts, axis=1), dYst))
            dS = dhn[:, cols]
            dxd.append(_mm(Bg, dS))
            yoff.append(_mm(Cg, hprev[:, cols]))
            dhprev.append(_mm_tn(Cg, dYo[:, cols]))
            dCs.append(_mm_nt(dYo[:, cols], hprev[:, cols]) + _mm(dCB, Bg))
            dBs.append(_mm_tn(dCB, Cg) + _mm_nt(Xd[:, cols], dS))
        Yoff = jnp.concatenate(yoff, axis=1) * R["eo_e"]
        dXd = jnp.concatenate(dxd, axis=1)
        dX = jnp.concatenate(dxdiag, axis=1) + dXd * R["ds_e"]
        t_state = dXd * Xd
        hs = _mm_exact_r(jnp.concatenate([dY * Yoff - t_state, dX * xs_c], axis=0), ET)
        dacs = dacs + hs[0:128] - dacs_t.T
        v_last = _colsum(t_state + dhn * hprev * R["cd_e"])
        dlast = _mm_exact_r(jnp.broadcast_to(v_last, (8, 1024)), ET)[0:1, :]
        dacs = dacs + jnp.where(R["row"] == 127, dlast, 0.0)
        triu = (R["col"] >= R["row"]).astype(BF16)
        da = _mm_exact_l(triu, dacs)
        ddt = da * R["A"] + hs[128:256]
        ddt_raw = ddt * _sigmoid(R["raw"])
        dxs_c = dX * R["dt_e"] + dY * dsk
        dh_scr[...] = jnp.concatenate(dhprev, axis=1) + dhn * R["cd_e"]

        dact = jnp.concatenate([dxs_c] + dBs + dCs, axis=1)
        pre, sg = R["pre"], R["sg"]
        dpre = dact * (sg * (1.0 + pre * (1.0 - sg)))
        ext2_scr[0:128, :] = dpre
        ext2_scr[128:136, :] = nxt_scr[...]
        nxt_scr[...] = dpre[0:8, :]
        cw = cw_ref[...]
        dxbc = (cw[3:4, :] * dpre + cw[2:3, :] * ext2_scr[1:129, :] + cw[1:2, :] * ext2_scr[2:130, :]
                + cw[0:1, :] * ext2_scr[3:131, :])
        acc_cw_ref[...] += _rows8([_colsum(dpre * ext_scr[5 + k:133 + k, :]) for k in range(4)] + [_colsum(dpre)])
        acc_w_ref[...] += _rows8([jnp.concatenate(dnw_parts, axis=1), _colsum(dY * xs_c)])
        acc_s_ref[...] += _rows8([_colsum(ddt_raw), _colsum(da * R["dt"])])

        lane = lax.broadcasted_iota(jnp.int32, (128, 128), 1)
        dp_ref[:, S_Z:S_Z + 1024] = dz
        dp_ref[:, S_XS:S_DT] = dxbc
        dp_ref[:, S_DT:S_DT + 128] = jnp.where(lane < N_HEADS, ddt_raw, 0.0)
        dp_ref[:, S_DT + 128:S_W] = jnp.zeros((128, 128), F32)

        @pl.when(i == nc - 1)
        def _():
            acc = acc_s_ref[...]
            dskip = _mm_exact_r(acc_w_ref[...], ET)[1:2, :]
            acc_s_ref[...] = _rows8([acc[0:1, :], acc[1:2, :] * R["A"], dskip])

    const = lambda shape: pl.BlockSpec(shape, lambda i: (0, 0))
    smem = pl.BlockSpec(memory_space=pltpu.SMEM)
    rev = lambda i: (nc - 1 - i, 0)
    return _call(
        body, comm, name="ssd_bwd", grid=(nc,),
        in_specs=[pl.BlockSpec((CHUNK, S_W), rev),
                  pl.BlockSpec((8, S_W), lambda i: (jnp.maximum((nc - 1 - i) * 16 - 1, 0), 0)),
                  pl.BlockSpec((128, 1024), rev),
                  pl.BlockSpec((CHUNK, D_SSD), rev),
                  pl.BlockSpec((CHUNK, D_SSD), rev),
                  const((4, D_XBC)), const((1, D_XBC)), smem, smem, smem, const((1, 1024)),
                  const((128, 1024)), const((1024, 128))],
        out_specs=[pl.BlockSpec((CHUNK, S_W), rev), const((8, D_XBC)), const((8, 1024)), const((8, 128))],
        out_shape=[jax.ShapeDtypeStruct((L, S_W), F32), jax.ShapeDtypeStruct((8, D_XBC), F32),
                   jax.ShapeDtypeStruct((8, 1024), F32), jax.ShapeDtypeStruct((8, 128), F32)],
        scratch_shapes=[pltpu.VMEM((128, 1024), F32), pltpu.VMEM((136, D_XBC), F32),
                        pltpu.VMEM((136, D_XBC), F32), pltpu.VMEM((8, D_XBC), F32)],
        args=(proj_ssd, proj_ssd, hprev_all, ypre, dy, conv_w8, conv_b8, dtb8, alog8, dskip_e, norm_w, E, ET))


def _rope(t, tab):
    cos, sa, sb = tab[:, 0:128], tab[:, 128:256], tab[:, 256:384]
    outs = []
    for i in range(t.shape[1] // 128):
        tg = t[:, i * 128:(i + 1) * 128]
        outs.append(tg * cos + pltpu.roll(tg, 8, 1) * sa + pltpu.roll(tg, 120, 1) * sb)
    return jnp.concatenate(outs, axis=1)


def _rope_transposed(d, tab):
    cos, sa, sb = tab[:, 0:128], tab[:, 128:256], tab[:, 256:384]
    outs = []
    for i in range(d.shape[1] // 128):
        dg = d[:, i * 128:(i + 1) * 128]
        outs.append(dg * cos + pltpu.roll(dg * sa, 120, 1) + pltpu.roll(dg * sb, 8, 1))
    return jnp.concatenate(outs, axis=1)


def _lo_half(rows):
    return lax.broadcasted_iota(jnp.int32, (rows, 128), 1) < HEAD_DIM


def _kv_both(t, j):
    p, b = j // 2, j % 2
    lo = _lo_half(2 * CHUNK)
    nat = jnp.where(lo if b == 0 else jnp.logical_not(lo), t[:, p * 128:(p + 1) * 128], 0.0)
    return nat + pltpu.roll(nat, HEAD_DIM, 1)


def _stack_heads(t, j):
    lo = _lo_half(CHUNK)
    hi = jnp.logical_not(lo)
    a, b = t[:, 2 * j * 128:(2 * j + 1) * 128], t[:, (2 * j + 1) * 128:(2 * j + 2) * 128]
    return jnp.concatenate([jnp.where(lo, a, 0.0), jnp.where(hi, a, 0.0),
                            jnp.where(lo, b, 0.0), jnp.where(hi, b, 0.0)], axis=0)


def _unstack_heads(s):
    lo = _lo_half(CHUNK)
    return jnp.concatenate([jnp.where(lo, s[0:128], s[128:256]), jnp.where(lo, s[256:384], s[384:512])], axis=1)


def _fold_kv(r, j):
    lo = _lo_half(2 * CHUNK)
    return jnp.where(lo if j % 2 == 0 else jnp.logical_not(lo), r + pltpu.roll(r, HEAD_DIM, 1), 0.0)


def _sink_row(sink_ref, j):
    hid = lax.broadcasted_iota(jnp.int32, (1, 4 * CHUNK), 1) // CHUNK
    row = jnp.zeros((1, 4 * CHUNK), F32)
    for hh in range(4):
        row = jnp.where(hid == hh, sink_ref[4 * j + hh], row)
    return row


def _band_mask(blk):
    si = lax.broadcasted_iota(jnp.int32, (2 * CHUNK, 4 * CHUNK), 0)
    qi = lax.broadcasted_iota(jnp.int32, (2 * CHUNK, 4 * CHUNK), 1) % CHUNK
    return (si > qi) & (si <= qi + CHUNK) & ((blk > 0) | (si >= CHUNK))


def _softmax_sink(s, valid, sink):
    s = jnp.where(valid, s, NEG)
    mx = jnp.maximum(jnp.max(s, axis=0, keepdims=True), sink)
    p = jnp.exp(s - mx)
    esink = jnp.exp(sink - mx)
    inv = 1.0 / (jnp.sum(p, axis=0, keepdims=True) + esink)
    return p * inv, esink * inv


def _swa_forward(proj_att, tabs, sinks, y):
    L = proj_att.shape[0]
    nb = L // CHUNK

    def body(sink_ref, p_ref, prev_ref, tab_ref, ptab_ref, y_in_ref, y_ref):
        n = pl.program_id(0)
        tab = tab_ref[...]
        qr = _rope(p_ref[:, A_Q:A_Q + 1024], tab)
        k_cur = _rope(p_ref[:, A_K:A_K + 256], tab)
        k_prev = _rope(prev_ref[:, 0:256], ptab_ref[...])
        kk = jnp.concatenate([k_prev, k_cur], axis=0)
        vv = jnp.concatenate([prev_ref[:, 256:512], p_ref[:, A_V:A_V + 256]], axis=0)
        valid = _band_mask(n)
        outs = []
        for j in range(KV_HEADS):
            s = _mm_nt(_kv_both(kk, j), _stack_heads(qr, j)) * ATT_SCALE
            P, _ = _softmax_sink(s, valid, _sink_row(sink_ref, j))
            outs.append(_unstack_heads(_mm_tn(P, _kv_both(vv, j))))
        g = p_ref[:, A_G:A_G + 1024]
        y_ref[...] = (jnp.concatenate(outs, axis=1) * (g * _sigmoid(g))).astype(y_ref.dtype)

    return pl.pallas_call(
        body, name="swa_fwd", grid=(nb,),
        in_specs=[pl.BlockSpec(memory_space=pltpu.SMEM),
                  pl.BlockSpec((CHUNK, A_W), lambda n: (n, 0)),
                  pl.BlockSpec((CHUNK, 512), lambda n: (jnp.maximum(n - 1, 0), 2)),
                  pl.BlockSpec((CHUNK, 384), lambda n: (n, 0)),
                  pl.BlockSpec((CHUNK, 384), lambda n: (jnp.maximum(n - 1, 0), 0)),
                  pl.BlockSpec(memory_space=pl.ANY)],
        out_specs=pl.BlockSpec((CHUNK, D_ATT), lambda n: (n, 1)),
        out_shape=jax.ShapeDtypeStruct(y.shape, y.dtype),
        input_output_aliases={5: 0},
        compiler_params=_params(("arbitrary",)),
    )(sinks, proj_att, proj_att, tabs, tabs, y)


def _swa_backward(proj_att, tabs, sinks, dy, comm=None):
    L = proj_att.shape[0]
    nb = L // CHUNK

    def body(sink_ref, p_ref, prev_ref, tab_ref, ptab_ref, dy_ref, dp_ref, dsink_ref, carry_k, carry_v):
        i = pl.program_id(0)
        n = nb - 1 - i

        @pl.when(i == 0)
        def _():
            carry_k[...] = jnp.zeros_like(carry_k)
            carry_v[...] = jnp.zeros_like(carry_v)
            dsink_ref[...] = jnp.zeros_like(dsink_ref)

        tab = tab_ref[...]
        qr = _rope(p_ref[:, A_Q:A_Q + 1024], tab)
        k_cur = _rope(p_ref[:, A_K:A_K + 256], tab)
        k_prev = _rope(prev_ref[:, 0:256], ptab_ref[...])
        kk = jnp.concatenate([k_prev, k_cur], axis=0)
        vv = jnp.concatenate([prev_ref[:, 256:512], p_ref[:, A_V:A_V + 256]], axis=0)
        valid = _band_mask(n)
        g = p_ref[:, A_G:A_G + 1024]
        sgm = _sigmoid(g)
        dyv = dy_ref[...]
        do_all = dyv * (g * sgm)
        lane8 = lax.broadcasted_iota(jnp.int32, (8, 128), 1)
        hid = lax.broadcasted_iota(jnp.int32, (1, 4 * CHUNK), 1) // CHUNK
        o_parts, dq_parts = [], []
        dk_nat = [jnp.zeros((2 * CHUNK, 128), F32) for _ in range(2)]
        dv_nat = [jnp.zeros((2 * CHUNK, 128), F32) for _ in range(2)]
        dsink = jnp.zeros((8, 128), F32)
        for j in range(KV_HEADS):
            qs = _stack_heads(qr, j)
            kkb, vvb = _kv_both(kk, j), _kv_both(vv, j)
            s = _mm_nt(kkb, qs) * ATT_SCALE
            P, psink = _softmax_sink(s, valid, _sink_row(sink_ref, j))
            o_parts.append(_unstack_heads(_mm_tn(P, vvb)))
            do_s = _stack_heads(do_all, j)
            dP = _mm_nt(vvb, do_s)
            D = jnp.sum(P * dP, axis=0, keepdims=True)
            dS = P * (dP - D)
            sd = psink * D
            for hh in range(4):
                dsink = dsink + jnp.where(lane8 == 4 * j + hh, -jnp.sum(jnp.where(hid == hh, sd, 0.0)), 0.0)
            dq_parts.append(_unstack_heads(_mm_tn(dS, kkb)) * ATT_SCALE)
            dk_nat[j // 2] = dk_nat[j // 2] + _fold_kv(_mm(dS, qs), j) * ATT_SCALE
            dv_nat[j // 2] = dv_nat[j // 2] + _fold_kv(_mm(P, do_s), j)
        o = jnp.concatenate(o_parts, axis=1)
        dkk = jnp.concatenate(dk_nat, axis=1)
        dvv = jnp.concatenate(dv_nat, axis=1)
        dp_ref[:, A_Q:A_Q + 1024] = _rope_transposed(jnp.concatenate(dq_parts, axis=1), tab)
        dp_ref[:, A_K:A_K + 256] = _rope_transposed(dkk[CHUNK:2 * CHUNK] + carry_k[...], tab)
        dp_ref[:, A_V:A_V + 256] = dvv[CHUNK:2 * CHUNK] + carry_v[...]
        dp_ref[:, A_G:A_G + 1024] = dyv * o * (sgm * (1.0 + g * (1.0 - sgm)))
        carry_k[...] = dkk[0:CHUNK]
        carry_v[...] = dvv[0:CHUNK]
        dsink_ref[...] += dsink

    rev = lambda i: (nb - 1 - i, 0)
    prev = lambda i: jnp.maximum(nb - 2 - i, 0)
    return _call(
        body, comm, name="swa_bwd", grid=(nb,),
        in_specs=[pl.BlockSpec(memory_space=pltpu.SMEM),
                  pl.BlockSpec((CHUNK, A_W), rev),
                  pl.BlockSpec((CHUNK, 512), lambda i: (prev(i), 2)),
                  pl.BlockSpec((CHUNK, 384), rev),
                  pl.BlockSpec((CHUNK, 384), lambda i: (prev(i), 0)),
                  pl.BlockSpec((CHUNK, D_ATT), lambda i: (nb - 1 - i, 1))],
        out_specs=[pl.BlockSpec((CHUNK, A_W), rev), pl.BlockSpec((8, 128), lambda i: (0, 0))],
        out_shape=[jax.ShapeDtypeStruct((L, A_W), F32), jax.ShapeDtypeStruct((8, 128), F32)],
        scratch_shapes=[pltpu.VMEM((CHUNK, 256), F32), pltpu.VMEM((CHUNK, 256), F32)],
        args=(sinks, proj_att, proj_att, tabs, tabs, dy))


def _head(y, x, target, w_out, ln_g8, ln_b8, *, tm):
    L = x.shape[0]
    nsteps = L // tm

    def body(y_ref, x_ref, t_ref, wo_ref, g_ref, b_ref, dr_ref, dy_ref, acc_ref):
        i = pl.program_id(0)

        @pl.when(i == 0)
        def _():
            acc_ref[...] = jnp.zeros_like(acc_ref)

        r = ALPHA * x_ref[...] + _mm(y_ref[...], wo_ref[...])
        mu = jnp.mean(r, axis=-1, keepdims=True)
        d = r - mu
        rstd = lax.rsqrt(jnp.mean(d * d, axis=-1, keepdims=True) + LN_EPS)
        xh = d * rstd
        gam = g_ref[0:1, :]
        e = xh * gam + b_ref[0:1, :] - t_ref[...]
        dout = e * (1.0 / D_MODEL)
        dxh = dout * gam
        dr = rstd * (dxh - jnp.mean(dxh, axis=-1, keepdims=True)
                     - xh * jnp.mean(dxh * xh, axis=-1, keepdims=True))
        dr_ref[...] = dr
        dy_ref[...] = _mm_nt(dr, wo_ref[...])
        acc_ref[...] += _rows8([_colsum(dout * xh), _colsum(dout), _colsum(e * e) * (0.5 / D_MODEL)])

        @pl.when(i == nsteps - 1)
        def _():
            acc = acc_ref[...]
            tot = jnp.sum(acc[2:3, :])
            rid = lax.broadcasted_iota(jnp.int32, (8, 1024), 0)
            acc_ref[...] = jnp.where(rid == 3, tot, acc)

    const = lambda shape: pl.BlockSpec(shape, lambda i: (0, 0))
    row = lambda w: pl.BlockSpec((tm, w), lambda i: (i, 0))
    return pl.pallas_call(
        body, name="head", grid=(nsteps,),
        in_specs=[row(2048), row(1024), row(1024), const((2048, 1024)), const((1, 1024)), const((1, 1024))],
        out_specs=[row(1024), row(2048), const((8, 1024))],
        out_shape=[jax.ShapeDtypeStruct((L, D_MODEL), F32), jax.ShapeDtypeStruct((L, 2048), F32),
                   jax.ShapeDtypeStruct((8, 1024), F32)],
        compiler_params=_params(("arbitrary",)),
    )(y, x, target, w_out, ln_g8, ln_b8)


def _all_gather(shards):
    n = len(shards)
    any_spec = pl.BlockSpec(memory_space=pl.ANY)

    def body(*refs):
        ins, outs = refs[:n], refs[n:2 * n]
        send_sems, recv_sems, local_sems = refs[2 * n:]
        x, y, c = _position()
        me, sibling = (x, y, c), (x, y, 1 - c)
        chips = [(1 - x, y), (x, 1 - y), (1 - x, 1 - y)]

        def copy(a, k, block, to, src=None):
            slot = outs[a].at[_index(*block)]
            return pltpu.make_async_remote_copy(
                src_ref=slot if src is None else src, dst_ref=slot,
                send_sem=send_sems.at[a, k], recv_sem=recv_sems.at[a, k],
                device_id=to, device_id_type=MESH)

        mine = [pltpu.make_async_copy(ins[a], outs[a].at[_index(*me)], local_sems.at[a]) for a in range(n)]
        for cp in mine:
            cp.start()
        first = []
        for a in range(n):
            first.append(copy(a, 0, me, sibling, src=ins[a]))
            first += [copy(a, 1 + j, me, (*chip, c), src=ins[a]) for j, chip in enumerate(chips)]
        for cp in first:
            cp.start()
        passed = []
        for j, chip in enumerate(chips):
            for a in range(n):
                copy(a, 1 + j, (*chip, c), me).wait_recv()
                fwd = copy(a, 4 + j, (*chip, c), sibling)
                fwd.start()
                passed.append(fwd)
        for a in range(n):
            copy(a, 0, sibling, me).wait_recv()
            for j, chip in enumerate(chips):
                copy(a, 4 + j, (*chip, 1 - c), me).wait_recv()
        for cp in first + passed:
            cp.wait_send()
        for cp in mine:
            cp.wait()

    return pl.pallas_call(
        body, name="weight_all_gather",
        in_specs=[any_spec] * n, out_specs=[any_spec] * n,
        out_shape=[jax.ShapeDtypeStruct((N_DEV,) + s.shape, s.dtype) for s in shards],
        scratch_shapes=[pltpu.SemaphoreType.DMA((n, 7)), pltpu.SemaphoreType.DMA((n, 7)),
                        pltpu.SemaphoreType.DMA((n,))],
    )(*shards)


def _input_gradient(d_ssd, d_att, w_ssd, w_att, dr, *, tm, comm=None):
    L = dr.shape[0]

    def body(ds_ref, da_ref, ws_ref, wa_ref, dr_ref, o_ref):
        o_ref[...] = ALPHA * dr_ref[...] + _mm_nt(ds_ref[...], ws_ref[...]) + _mm_nt(da_ref[...], wa_ref[...])

    row = lambda w: pl.BlockSpec((tm, w), lambda i: (i, 0))
    const = lambda shape: pl.BlockSpec(shape, lambda i: (0, 0))
    return _call(body, comm, name="dx", grid=(L // tm,),
                 in_specs=[row(S_W), row(A_W), const((D_MODEL, S_W)), const((D_MODEL, A_W)), row(D_MODEL)],
                 out_specs=[row(D_MODEL)], out_shape=[jax.ShapeDtypeStruct((L, D_MODEL), F32)],
                 scratch_shapes=[], args=(d_ssd, d_att, w_ssd, w_att, dr))


SHARD_COLS = D_IN_PROJ // N_DEV
SPLIT = N_SSD_REAL - 4 * SHARD_COLS
RELAYOUT_ROWS = 256


def _unpack_w_in(w_all):
    def body(g_ref, ws_ref, wa_ref):
        for j in range(4):
            ws_ref[:, SHARD_COLS * j:SHARD_COLS * (j + 1)] = g_ref[j]
        ws_ref[:, 4 * SHARD_COLS:N_SSD_REAL] = g_ref[4, :, 0:SPLIT]
        ws_ref[:, N_SSD_REAL:S_W] = jnp.zeros((RELAYOUT_ROWS, S_W - N_SSD_REAL), ws_ref.dtype)
        wa_ref[:, 0:SHARD_COLS - SPLIT] = g_ref[4, :, SPLIT:SHARD_COLS]
        for j in range(5, N_DEV):
            lo = SHARD_COLS * (j - 4) - SPLIT
            wa_ref[:, lo:lo + SHARD_COLS] = g_ref[j]

    return pl.pallas_call(
        body, name="unpack_w_in", grid=(D_MODEL // RELAYOUT_ROWS,),
        in_specs=[pl.BlockSpec((N_DEV, RELAYOUT_ROWS, SHARD_COLS), lambda i: (0, i, 0))],
        out_specs=[pl.BlockSpec((RELAYOUT_ROWS, S_W), lambda i: (i, 0)), pl.BlockSpec((RELAYOUT_ROWS, A_W), lambda i: (i, 0))],
        out_shape=[jax.ShapeDtypeStruct((D_MODEL, S_W), w_all.dtype), jax.ShapeDtypeStruct((D_MODEL, A_W), w_all.dtype)],
        compiler_params=_params(("arbitrary",)),
    )(w_all)


def _pack_dw_in(me1, dw_ssd, dw_att, half):
    def body(me_ref, *refs):
        if half == 0:
            ds_ref, p_ref, own_ref = refs
            me = me_ref[0]

            @pl.when(me >= 4)
            def _():
                own_ref[...] = jnp.zeros_like(own_ref)
        else:
            ds_ref, da_ref, p_ref = refs

        for j in range(4):
            if half == 0:
                pieces = [(0, ds_ref[:, SHARD_COLS * j:SHARD_COLS * (j + 1)])]
            elif j == 0:
                pieces = [(0, ds_ref[:, 4 * SHARD_COLS:N_SSD_REAL]), (SPLIT, da_ref[:, 0:SHARD_COLS - SPLIT])]
            else:
                lo = SHARD_COLS * j - SPLIT
                pieces = [(0, da_ref[:, lo:lo + SHARD_COLS])]
            for off, blk in pieces:
                p_ref[j, :, off:off + blk.shape[1]] = blk.astype(p_ref.dtype)
                if half == 0:
                    @pl.when(me == j)
                    def _(off=off, blk=blk):
                        own_ref[:, off:off + blk.shape[1]] = blk

    ins = [dw_ssd] if half == 0 else [dw_ssd, dw_att]
    row = lambda a: pl.BlockSpec((RELAYOUT_ROWS, a.shape[1]), lambda i: (i, 0))
    out_specs = [pl.BlockSpec((4, RELAYOUT_ROWS, SHARD_COLS), lambda i: (0, i, 0))]
    out_shape = [jax.ShapeDtypeStruct((4, D_MODEL, SHARD_COLS), BF16 if half == 0 else F32)]
    if half == 0:
        out_specs.append(pl.BlockSpec((RELAYOUT_ROWS, SHARD_COLS), lambda i: (i, 0)))
        out_shape.append(jax.ShapeDtypeStruct((D_MODEL, SHARD_COLS), F32))
    return pl.pallas_call(
        body, name="pack_dw_in_%d" % half, grid=(D_MODEL // RELAYOUT_ROWS,),
        in_specs=[pl.BlockSpec(memory_space=pltpu.SMEM)] + [row(a) for a in ins],
        out_specs=out_specs, out_shape=out_shape, compiler_params=_params(("arbitrary",)),
    )(me1, *ins)


def _pair_swap(stack):
    def body(in_ref, out_ref, send_sems, recv_sems):
        x, y, c = _position()
        cps = [_remote(in_ref.at[2 * oy + (1 - c)], out_ref.at[oy], send_sems.at[oy], recv_sems.at[oy], (x, y, 1 - c))
               for oy in range(2)]
        for cp in cps:
            cp.start()
        for cp in cps:
            cp.wait_recv()
        for cp in cps:
            cp.wait_send()

    any_spec = pl.BlockSpec(memory_space=pl.ANY)
    return pl.pallas_call(
        body, name="pair_swap", in_specs=[any_spec], out_specs=any_spec,
        out_shape=jax.ShapeDtypeStruct((2,) + stack.shape[1:], stack.dtype),
        scratch_shapes=[pltpu.SemaphoreType.DMA((2,)), pltpu.SemaphoreType.DMA((2,))],
    )(stack)


def _pair_sum(pos3, stack, swapped, own_lo):
    def body(pos_ref, a_ref, b_ref, lo_ref, chip_ref, own_ref):
        oy = pl.program_id(1)
        t = a_ref[0] + b_ref[0]
        chip_ref[0] = t.astype(chip_ref.dtype)

        @pl.when((pos_ref[0] == 0) & (oy == 0))
        def _():
            own_ref[...] = lo_ref[...]

        @pl.when((pos_ref[0] == 1) & (oy == pos_ref[1]))
        def _():
            own_ref[...] = t

    blk = (1, RELAYOUT_ROWS, SHARD_COLS)
    flat = pl.BlockSpec((RELAYOUT_ROWS, SHARD_COLS), lambda i, oy, pos: (i, 0))
    return pl.pallas_call(
        body, name="pair_sum",
        grid_spec=pltpu.PrefetchScalarGridSpec(
            num_scalar_prefetch=1, grid=(D_MODEL // RELAYOUT_ROWS, 2),
            in_specs=[pl.BlockSpec(blk, lambda i, oy, pos: (2 * oy + pos[2], i, 0)),
                      pl.BlockSpec(blk, lambda i, oy, pos: (oy, i, 0)), flat],
            out_specs=[pl.BlockSpec(blk, lambda i, oy, pos: (oy, i, 0)), flat]),
        out_shape=[jax.ShapeDtypeStruct((2, D_MODEL, SHARD_COLS), BF16), jax.ShapeDtypeStruct((D_MODEL, SHARD_COLS), F32)],
        compiler_params=_params(("arbitrary", "arbitrary")),
    )(pos3, stack, swapped, own_lo)


def _adamw_math(w, g, m, v):
    m = ADAM_B1 * m + (1.0 - ADAM_B1) * g
    v = ADAM_B2 * v + (1.0 - ADAM_B2) * (g * g)
    m_hat = m / (1.0 - ADAM_B1 ** ADAM_STEP)
    v_hat = v / (1.0 - ADAM_B2 ** ADAM_STEP)
    delta = -ADAM_LR * (m_hat / (jnp.sqrt(v_hat) + ADAM_EPS) + ADAM_WD * w)
    return delta, m, v


def _adamw_shard(n_recv, g_own, recv, w, m, v, *, rows, name):
    R, C = g_own.shape

    def body(n_ref, g_ref, r_ref, w_ref, m_ref, v_ref, go_ref, d_ref, mo_ref, vo_ref):
        g = g_ref[...]
        for k in range(N_DEV - 1):
            g = g + jnp.where(k < n_ref[0], r_ref[k].astype(F32), 0.0)
        d, mn, vn = _adamw_math(w_ref[...], g, m_ref[...], v_ref[...])
        go_ref[...] = g
        d_ref[...] = d
        mo_ref[...] = mn
        vo_ref[...] = vn

    blk = pl.BlockSpec((rows, C), lambda i: (i, 0))
    return pl.pallas_call(
        body, name=name, grid=(R // rows,),
        in_specs=[pl.BlockSpec(memory_space=pltpu.SMEM), blk,
                  pl.BlockSpec((N_DEV - 1, rows, C), lambda i: (0, i, 0)), blk, blk, blk],
        out_specs=[blk] * 4, out_shape=[jax.ShapeDtypeStruct((R, C), F32)] * 4,
        compiler_params=_params(("arbitrary",)),
    )(n_recv, g_own, recv, w, m, v)


SMALL = ("conv_b", "dt_bias", "a_log", "d_skip", "ssd_norm_w", "attn_sinks", "ln_g", "ln_b")


def _adamw_small(gathered, params):
    n_p = len(SMALL)

    def body(*refs):
        acc = []
        for r in refs[:5]:
            t = r[0]
            for k in range(1, N_DEV):
                t = t + r[k]
            acc.append(t)
        head, conv, norm, scal, sink = acc
        grads = dict(conv_b=conv[4:5, :], dt_bias=scal[0:1, 0:N_HEADS], a_log=scal[1:2, 0:N_HEADS],
                     d_skip=scal[2:3, 0:N_HEADS], ssd_norm_w=norm[0:1, :], attn_sinks=sink[0:1, 0:N_HEADS],
                     ln_g=head[0:1, :], ln_b=head[1:2, :])
        wmv = refs[5:5 + 3 * n_p]
        outs = refs[5 + 3 * n_p:]
        outs[0][...] = head[3:4, 0:1]
        outs[1][...] = conv[0:4, :]
        for i, name in enumerate(SMALL):
            w_ref, m_ref, v_ref = wmv[3 * i:3 * i + 3]
            g = grads[name]
            d, mn, vn = _adamw_math(w_ref[...], g, m_ref[...], v_ref[...])
            for o_ref, val in zip(outs[2 + 4 * i:6 + 4 * i], (g, d, mn, vn)):
                o_ref[...] = val

    flat = [a for name in SMALL for a in params[name]]
    out_shape = [jax.ShapeDtypeStruct((1, 1), F32), jax.ShapeDtypeStruct((4, D_XBC), F32)]
    for name in SMALL:
        out_shape += [jax.ShapeDtypeStruct(params[name][0].shape, F32)] * 4
    res = pl.pallas_call(body, name="adamw_small", out_shape=out_shape, compiler_params=_params())(*gathered, *flat)
    return res[0], res[1], {name: res[2 + 4 * i:6 + 4 * i] for i, name in enumerate(SMALL)}


def _adamw_plain(g, w, m, v):
    def body(g_ref, w_ref, m_ref, v_ref, d_ref, mo_ref, vo_ref):
        d, mn, vn = _adamw_math(w_ref[...], g_ref[...], m_ref[...], v_ref[...])
        d_ref[...] = d
        mo_ref[...] = mn
        vo_ref[...] = vn

    return pl.pallas_call(
        body, name="adamw_conv_w", out_shape=[jax.ShapeDtypeStruct(w.shape, F32)] * 3,
        compiler_params=_params(),
    )(g, w, m, v)


def _lane_pattern(fn):
    return np.asarray([fn(l % HEAD_DIM) for l in range(128)], np.float32)


ROPE_INV = _lane_pattern(lambda r: ROPE_THETA ** (-2.0 * (r % 8) / ROPE_DIM) if r < ROPE_DIM else 0.0)
ROPE_SIN_A = _lane_pattern(lambda r: 1.0 if 8 <= r < ROPE_DIM else 0.0)
ROPE_SIN_B = _lane_pattern(lambda r: -1.0 if r < 8 else 0.0)


def _rope_tables(positions):
    ang = positions.astype(F32)[:, None] * ROPE_INV[None, :]
    sn = jnp.sin(ang)
    return jnp.concatenate([jnp.cos(ang), sn * ROPE_SIN_A[None, :], sn * ROPE_SIN_B[None, :]], axis=1)


def _expansion():
    E = np.arange(1024)[None, :] // HEAD_DIM == np.arange(128)[:, None]
    return jnp.asarray(E, BF16), jnp.asarray(E.T, BF16)


def _ssd_args(conv_w, conv_b, dt_bias, a_log, d_skip, norm_w, E):
    return (conv_w, conv_b, dt_bias.reshape(-1), a_log.reshape(-1), d_skip.reshape(-1), norm_w, E)


def kernel(x, positions, w_in, conv_w, conv_b, dt_bias, a_log, d_skip, ssd_norm_w, attn_sinks, w_out, ln_g, ln_b, loss_target, m_w_in, m_conv_w, m_conv_b, m_dt_bias, m_a_log, m_d_skip, m_ssd_norm_w, m_attn_sinks, m_w_out, m_ln_g, m_ln_b, v_w_in, v_conv_w, v_conv_b, v_dt_bias, v_a_log, v_d_skip, v_ssd_norm_w, v_attn_sinks, v_w_out, v_ln_g, v_ln_b):
    me = _index(*_position())
    me1 = me.reshape(1).astype(jnp.int32)
    x0, target = x[0], loss_target[0]
    bf16_shard = lambda shape: jax.ShapeDtypeStruct(shape, BF16)
    E, ET = _expansion()
    tabs = _rope_tables(positions[0])
    sinks = attn_sinks.reshape(-1)

    w_in_all, conv_w_all = _all_gather([w_in[0].astype(BF16), conv_w[0]])
    w_ssd, w_att = _unpack_w_in(w_in_all)
    conv_w_f = jnp.transpose(conv_w_all, (1, 0, 2)).reshape(4, D_XBC)
    ssd_args = _ssd_args(conv_w_f, conv_b, dt_bias, a_log, d_skip, ssd_norm_w, E)

    proj_ssd, xb = _matmul(x0, w_ssd, tm=1024, tn=S_W // 2, name="in_proj_ssd", emit_a=True)
    proj_att = _matmul(xb, w_att, tm=1024, tn=A_W // 2, name="in_proj_att")
    gather_w_out = _Hosted([w_out[0].astype(BF16)], [bf16_shard((N_DEV, 256, D_MODEL))], [_Flow("gather", 0, 0)])
    y, ypre, hprev, w_out_all = _ssd_forward(proj_ssd, *ssd_args, comm=gather_w_out)
    w_out_f = w_out_all.reshape(2 * D_MODEL, D_MODEL)
    y = _swa_forward(proj_att, tabs, sinks, y)
    dr, dy, acc_head = _head(y, x0, target, w_out_f, ln_g, ln_b, tm=256)

    dw_out, dw_out_bf16 = _matmul_tn(y, dr, tl=512, tn=D_MODEL, name="dw_out", emit_bf16=True)
    own_out = lax.dynamic_index_in_dim(dw_out.reshape(N_DEV, 256, D_MODEL), me, axis=0, keepdims=False)
    send_out = _Hosted([dw_out_bf16.reshape(N_DEV, 256, D_MODEL)], [bf16_shard((N_DEV - 1, 256, D_MODEL))],
                       [_Flow("exchange", 0, 0)])
    d_ssd, acc_cw, acc_w, acc_s, recv_out = _ssd_backward(proj_ssd, hprev, ypre, dy, *ssd_args, ET, comm=send_out)
    dw_ssd = _matmul_tn(xb, d_ssd, tl=512, tn=S_W // 2, name="dw_in_ssd")
    parts_lo, own_lo = _pack_dw_in(me1, dw_ssd, None, 0)
    recv_shape = bf16_shard((N_DEV - 1, D_MODEL, SHARD_COLS))
    send_lo = _Hosted([parts_lo], [recv_shape], [_Flow("exchange", 0, 0, target_x=0)])
    d_att, dsink, recv_in = _swa_backward(proj_att, tabs, sinks, dy, comm=send_lo)
    dw_att = _matmul_tn(xb, d_att, tl=512, tn=A_W // 2, name="dw_in_att")
    (stack_hi,) = _pack_dw_in(me1, dw_ssd, dw_att, 1)
    pos3 = jnp.stack(_position()).astype(jnp.int32)
    chip_hi, own_in = _pair_sum(pos3, stack_hi, _pair_swap(stack_hi), own_lo)
    accs = [acc_head, acc_cw, acc_w, acc_s, dsink]
    send_hi = _Hosted([chip_hi, recv_in] + accs,
                      [recv_shape] + [jax.ShapeDtypeStruct((N_DEV,) + a.shape, F32) for a in accs],
                      [_Flow("chip_exchange", 0, 0, target_x=1)] + [_Flow("gather", 2 + i, 1 + i) for i in range(5)],
                      aliases={1: 0})
    dx, recv_in, *gathered = _input_gradient(d_ssd, d_att, w_ssd, w_att, dr, tm=256, comm=send_hi)
    n_recv_in = jnp.where(me < 4, N_DEV - 1, 3).reshape(1).astype(jnp.int32)
    n_recv_out = jnp.full((1,), N_DEV - 1, jnp.int32)

    g_in, d_in, nm_in, nv_in = _adamw_shard(n_recv_in, own_in, recv_in, w_in[0], m_w_in[0], v_w_in[0], rows=256,
                                            name="adamw_w_in")
    g_out, d_out, nm_out, nv_out = _adamw_shard(n_recv_out, own_out, recv_out, w_out[0], m_w_out[0], v_w_out[0],
                                                rows=256, name="adamw_w_out")
    loss, g_conv_w, small = _adamw_small(gathered, dict(
        conv_b=(conv_b, m_conv_b, v_conv_b), dt_bias=(dt_bias, m_dt_bias, v_dt_bias), a_log=(a_log, m_a_log, v_a_log),
        d_skip=(d_skip, m_d_skip, v_d_skip), ssd_norm_w=(ssd_norm_w, m_ssd_norm_w, v_ssd_norm_w),
        attn_sinks=(attn_sinks, m_attn_sinks, v_attn_sinks), ln_g=(ln_g, m_ln_g, v_ln_g), ln_b=(ln_b, m_ln_b, v_ln_b)))
    g_cw = lax.dynamic_slice_in_dim(g_conv_w, me * (D_XBC // N_DEV), D_XBC // N_DEV, axis=1)
    d_cw, nm_cw, nv_cw = _adamw_plain(g_cw, conv_w[0], m_conv_w[0], v_conv_w[0])

    def leaves(i, big_in, cw, big_out):
        mid = [small[k][i] for k in ("conv_b", "dt_bias", "a_log", "d_skip", "ssd_norm_w", "attn_sinks")]
        return [big_in[None], cw[None]] + mid + [big_out[None], small["ln_g"][i], small["ln_b"][i]]

    return (loss.reshape(()), dx[None], *leaves(0, g_in, g_cw, g_out), *leaves(1, d_in, d_cw, d_out),
            *leaves(2, nm_in, nm_cw, nm_out), *leaves(3, nv_in, nv_cw, nv_out))
```

```python
import jax
import jax.numpy as jnp
from jax import lax
from jax.experimental import pallas as pl
from jax.experimental.pallas import tpu as pltpu
import numpy as np

F32 = jnp.float32
BF16 = jnp.bfloat16
_MXU = jnp.bfloat16

N_DEV = 8
D_MODEL = 1024
D_SSD = 1024
D_ATT = 1024
HEAD_DIM = 64
N_HEADS = 16
SSD_GROUPS = 2
KV_HEADS = 4
CHUNK = 128
D_XBC = 1536
D_IN_PROJ = 5136
ROPE_DIM = 16
ROPE_THETA = 500000.0
ALPHA = (2.0 * 1) ** 0.25
LN_EPS = 1e-5
RMS_EPS = 1e-5
ATT_SCALE = HEAD_DIM ** -0.5
NEG = -1e30

S_Z, S_XS, S_B, S_C, S_DT, S_W = 0, 1024, 2048, 2304, 2560, 2816
N_SSD_REAL = 2576
A_Q, A_K, A_V, A_G, A_W = 0, 1024, 1280, 1536, 2560

ADAM_LR = 0.001
ADAM_B1 = 0.9
ADAM_B2 = 0.999
ADAM_EPS = 1e-08
ADAM_WD = 0.01
ADAM_STEP = 10

VMEM_LIMIT = 48 * 1024 * 1024
MESH = pl.DeviceIdType.MESH


def _params(sem=None):
    return pltpu.CompilerParams(dimension_semantics=sem, vmem_limit_bytes=VMEM_LIMIT)


def _mm(a, b):
    return jnp.dot(a.astype(_MXU), b.astype(_MXU), preferred_element_type=F32)


def _mm_nt(a, b):
    return lax.dot_general(a.astype(_MXU), b.astype(_MXU), (((1,), (1,)), ((), ())),
                           preferred_element_type=F32)


def _mm_tn(a, b):
    return lax.dot_general(a.astype(_MXU), b.astype(_MXU), (((0,), (0,)), ((), ())),
                           preferred_element_type=F32)


def _split3(v):
    hi = v.astype(BF16)
    r = v - hi.astype(F32)
    mid = r.astype(BF16)
    lo = (r - mid.astype(F32)).astype(BF16)
    return hi, mid, lo


def _mm_exact_r(v, p01):
    hi, mid, lo = _split3(v)
    d = lambda a: jnp.dot(a, p01, preferred_element_type=F32)
    return d(hi) + d(mid) + d(lo)


def _mm_exact_l(p01, v):
    hi, mid, lo = _split3(v)
    d = lambda a: jnp.dot(p01, a, preferred_element_type=F32)
    return d(hi) + d(mid) + d(lo)


def _sigmoid(x):
    return 1.0 / (1.0 + jnp.exp(-x))


def _softplus(x):
    e = jnp.exp(-jnp.abs(x))
    u = 1.0 + e
    log1p = jnp.where(u == 1.0, e, jnp.log(u) * (e / (u - 1.0)))
    return jnp.maximum(x, 0.0) + log1p


def _rows8(rows):
    n = rows[0].shape[1]
    rid = lax.broadcasted_iota(jnp.int32, (8, n), 0)
    out = jnp.zeros((8, n), F32)
    for k, r in enumerate(rows):
        out = out + jnp.where(rid == k, r, 0.0)
    return out


def _colsum(a):
    return jnp.sum(a, axis=0, keepdims=True)


def _matmul(a, b, *, tm, tn, name, emit_a=False):
    M, K = a.shape
    N = b.shape[1]

    def body(a_ref, b_ref, o_ref, *rest):
        am = a_ref[...].astype(_MXU)
        o_ref[...] = jnp.dot(am, b_ref[...].astype(_MXU), preferred_element_type=F32)
        if emit_a:
            rest[0][...] = am

    out_specs = [pl.BlockSpec((tm, tn), lambda i, j: (i, j))]
    out_shape = [jax.ShapeDtypeStruct((M, N), F32)]
    if emit_a:
        out_specs.append(pl.BlockSpec((tm, K), lambda i, j: (i, 0)))
        out_shape.append(jax.ShapeDtypeStruct((M, K), _MXU))
    res = pl.pallas_call(
        body, name=name, grid=(M // tm, N // tn),
        in_specs=[pl.BlockSpec((tm, K), lambda i, j: (i, 0)), pl.BlockSpec((K, tn), lambda i, j: (0, j))],
        out_specs=out_specs, out_shape=out_shape, compiler_params=_params(("arbitrary", "arbitrary")),
    )(a, b)
    return res if emit_a else res[0]


def _matmul_tn(a, g, *, tl, tn, name, emit_bf16=False):
    L, M = a.shape
    N = g.shape[1]
    last = L // tl - 1

    def body(a_ref, g_ref, o_ref, *rest):
        @pl.when(pl.program_id(1) == 0)
        def _():
            o_ref[...] = jnp.zeros_like(o_ref)

        o_ref[...] += _mm_tn(a_ref[...], g_ref[...])
        if emit_bf16:
            @pl.when(pl.program_id(1) == last)
            def _():
                rest[0][...] = o_ref[...].astype(BF16)

    spec = pl.BlockSpec((M, tn), lambda j, l: (0, j))
    res = pl.pallas_call(
        body, name=name, grid=(N // tn, L // tl),
        in_specs=[pl.BlockSpec((tl, M), lambda j, l: (l, 0)), pl.BlockSpec((tl, tn), lambda j, l: (l, j))],
        out_specs=[spec, spec] if emit_bf16 else [spec],
        out_shape=[jax.ShapeDtypeStruct((M, N), F32)] + ([jax.ShapeDtypeStruct((M, N), BF16)] if emit_bf16 else []),
        compiler_params=_params(("arbitrary", "arbitrary")),
    )(a, g)
    return res if emit_bf16 else res[0]


def _position():
    return lax.axis_index("x"), lax.axis_index("y"), lax.axis_index("c")


def _index(px, py, pc):
    return 4 * px + 2 * py + pc


def _flip(pos, k):
    x, y, c = pos
    return ((1 - x) if (k >> 2) & 1 else x, (1 - y) if (k >> 1) & 1 else y, (1 - c) if k & 1 else c)


def _when(cond, fn):
    if cond is True:
        fn()
    else:
        pl.when(cond)(fn)


def _remote(src, dst, send_sem, recv_sem, peer):
    return pltpu.make_async_remote_copy(src_ref=src, dst_ref=dst, send_sem=send_sem, recv_sem=recv_sem,
                                        device_id=peer, device_id_type=MESH)


class _Flow:
    def __init__(self, kind, operand, result, target_x=None):
        self.kind, self.operand, self.result, self.target_x = kind, operand, result, target_x


class _Hosted:
    def __init__(self, operands, out_shapes, flows, aliases=None):
        self.operands, self.out_shapes, self.flows = operands, out_shapes, flows
        self.aliases = aliases or {}

    def plan(self, ins, outs, send_sems, recv_sems, local_sems):
        me = _position()
        mi = _index(*me)
        sends, recvs, locals_ = [], [], []
        for row, f in enumerate(self.flows):
            src, dst = ins[f.operand], outs[f.result]
            for k in range(1, N_DEV):
                peer = _flip(me, k)
                sems = (send_sems.at[row, k - 1], recv_sems.at[row, k - 1])
                if f.kind == "exchange":
                    owner = _index(*peer) if f.target_x is None else 2 * peer[1] + peer[2]
                    cp = _remote(src.at[owner], dst.at[k - 1], *sems, peer)
                    to_peer = True if f.target_x is None else peer[0] == f.target_x
                    to_me = True if f.target_x is None else me[0] == f.target_x
                    sends.append((to_peer, cp))
                    recvs.append((to_me, cp))
                elif f.kind == "chip_exchange":
                    if k & 1:
                        continue
                    cp = _remote(src.at[peer[1]], dst.at[k // 2 - 1], *sems, peer)
                    sends.append((peer[0] == f.target_x, cp))
                    recvs.append((me[0] == f.target_x, cp))
                else:
                    sends.append((True, _remote(src, dst.at[mi], *sems, peer)))
                    recvs.append((True, _remote(src, dst.at[_index(*peer)], *sems, peer)))
            if f.kind == "gather":
                locals_.append(pltpu.make_async_copy(src, dst.at[mi], local_sems.at[row]))

        def start():
            for cp in locals_:
                cp.start()
            for cond, cp in sends:
                _when(cond, cp.start)

        def wait():
            for cond, cp in recvs:
                _when(cond, cp.wait_recv)
            for cond, cp in sends:
                _when(cond, cp.wait_send)
            for cp in locals_:
                cp.wait()

        return start, wait


def _call(body, comm, *, name, grid, in_specs, out_specs, out_shape, scratch_shapes, args, aliases=None):
    io_alias = dict(aliases or {})
    if comm is None:
        return pl.pallas_call(body, name=name, grid=grid, in_specs=in_specs, out_specs=out_specs, out_shape=out_shape,
                              scratch_shapes=scratch_shapes, input_output_aliases=io_alias,
                              compiler_params=_params(("arbitrary",)))(*args)
    n_in, n_out, n_scr = len(args), len(out_shape), len(scratch_shapes)
    c_in, c_out, rows = len(comm.operands), len(comm.out_shapes), len(comm.flows)

    def hosted(*refs):
        ins, refs = refs[:n_in], refs[n_in:]
        cins, refs = refs[:c_in], refs[c_in:]
        outs, refs = refs[:n_out], refs[n_out:]
        couts, refs = refs[:c_out], refs[c_out:]
        scr, (send_sems, recv_sems, local_sems) = refs[:n_scr], refs[n_scr:]
        start, wait = comm.plan(cins, couts, send_sems, recv_sems, local_sems)
        pl.when(pl.program_id(0) == 0)(start)
        body(*ins, *outs, *scr)
        pl.when(pl.program_id(0) == grid[0] - 1)(wait)

    for ci, co in comm.aliases.items():
        io_alias[n_in + ci] = n_out + co
    any_spec = pl.BlockSpec(memory_space=pl.ANY)
    sems = [pltpu.SemaphoreType.DMA((rows, N_DEV - 1)), pltpu.SemaphoreType.DMA((rows, N_DEV - 1)),
            pltpu.SemaphoreType.DMA((rows,))]
    return pl.pallas_call(
        hosted, name=name, grid=grid, in_specs=list(in_specs) + [any_spec] * c_in,
        out_specs=list(out_specs) + [any_spec] * c_out, out_shape=list(out_shape) + list(comm.out_shapes),
        scratch_shapes=list(scratch_shapes) + sems, input_output_aliases=io_alias,
        compiler_params=_params(("arbitrary",)))(*args, *comm.operands)


def _head_row(ref, width, rep):
    hid = lax.broadcasted_iota(jnp.int32, (1, width), 1) // rep
    row = jnp.zeros((1, width), F32)
    for h in range(N_HEADS):
        row = jnp.where(hid == h, ref[h], row)
    return row


def _ssd_recompute(first, p_ref, halo_ref, cw_ref, cb_ref, dtb_ref, alog_ref, e_ref, ext_scr, pre=None):
    ext_scr[0:8, :] = jnp.where(first, 0.0, halo_ref[:, S_XS:S_DT])
    if pre is not None:
        ext_scr[8:16, :] = p_ref[0:8, S_XS:S_DT]
    else:
        ext_scr[8:136, :] = p_ref[:, S_XS:S_DT]
        cw = cw_ref[...]
        pre = (cb_ref[0:1, :] + cw[3:4, :] * ext_scr[8:136, :] + cw[2:3, :] * ext_scr[7:135, :]
               + cw[1:2, :] * ext_scr[6:134, :] + cw[0:1, :] * ext_scr[5:133, :])
    sg = _sigmoid(pre)
    act = pre * sg
    lane = lax.broadcasted_iota(jnp.int32, (1, 128), 1)
    A = jnp.where(lane < N_HEADS, -jnp.exp(_head_row(alog_ref, 128, 1)), 0.0)
    raw = p_ref[:, S_DT:S_DT + 128] + _head_row(dtb_ref, 128, 1)
    dt = _softplus(raw)
    dA = dt * A
    row = lax.broadcasted_iota(jnp.int32, (128, 128), 0)
    col = lax.broadcasted_iota(jnp.int32, (128, 128), 1)
    tril = (row >= col).astype(BF16)
    acs = _mm_exact_l(tril, dA)
    last = acs[127:128, :]
    ds = jnp.exp(last - acs)
    eo = jnp.exp(acs)
    E = e_ref[...]
    ex = _mm_exact_r(jnp.concatenate([dt, ds, eo], axis=0), E)
    dt_e, ds_e, eo_e = ex[0:128], ex[128:256], ex[256:384]
    xs_c = act[:, 0:1024]
    X = xs_c * dt_e
    return dict(pre=pre, sg=sg, xs_c=xs_c, Bc=act[:, 1024:1280], Cc=act[:, 1280:1536], A=A, raw=raw, dt=dt,
                acs=acs, acsT=acs.T, eo_e=eo_e, ds_e=ds_e, dt_e=dt_e, cd_e=eo_e[127:128, :],
                X=X, Xd=X * ds_e, row=row, col=col)


def _split_halves(t):
    lo = _lo_half(CHUNK)
    return jnp.concatenate([jnp.where(lo, t, 0.0), jnp.where(lo, 0.0, t)], axis=0)


def _ssd_core(R, hprev):
    causal = R["row"] >= R["col"]
    acs, acsT, X = R["acs"], R["acsT"], R["X"]
    ydiag, yoff, snew = [], [], []
    for g in range(SSD_GROUPS):
        Bg = R["Bc"][:, g * 128:(g + 1) * 128]
        Cg = R["Cc"][:, g * 128:(g + 1) * 128]
        cols = slice(g * 512, (g + 1) * 512)
        CB = _mm_nt(Cg, Bg)
        snew.append(_mm_tn(Bg, R["Xd"][:, cols]))
        yoff.append(_mm(Cg, hprev[:, cols]))
        for j in range(4):
            h0 = g * 8 + 2 * j
            ms = [CB * jnp.exp(jnp.where(causal, acs[:, h:h + 1] - acsT[h:h + 1, :], NEG)) for h in (h0, h0 + 1)]
            ydiag.append(_mm(jnp.concatenate(ms, axis=1), _split_halves(X[:, h0 * HEAD_DIM:h0 * HEAD_DIM + 128])))
    Y = jnp.concatenate(ydiag, axis=1) + jnp.concatenate(yoff, axis=1) * R["eo_e"]
    return Y, jnp.concatenate(snew, axis=1)


def _ssd_forward(proj_ssd, conv_w8, conv_b8, dtb8, alog8, dskip_e, norm_w, E, comm=None):
    L = proj_ssd.shape[0]
    nc = L // CHUNK

    def body(p_ref, halo_ref, cw_ref, cb_ref, dtb_ref, alog_ref, dsk_ref, nw_ref, e_ref,
             y_ref, ypre_ref, hprev_ref, pre_ref, h_scr, ext_scr):
        c = pl.program_id(0)
        first = c == 0

        @pl.when(first)
        def _():
            h_scr[...] = jnp.zeros_like(h_scr)

        R = _ssd_recompute(first, p_ref, halo_ref, cw_ref, cb_ref, dtb_ref, alog_ref, e_ref, ext_scr)
        hprev = h_scr[...]
        hprev_ref[...] = hprev
        pre_ref[...] = R["pre"]
        Y, snew = _ssd_core(R, hprev)
        h_scr[...] = hprev * R["cd_e"] + snew
        Y = Y + _head_row(dsk_ref, D_SSD, HEAD_DIM) * R["xs_c"]
        ypre_ref[...] = Y
        z = p_ref[:, S_Z:S_Z + 1024]
        yf = Y * (z * _sigmoid(z))
        outs = []
        for g in range(SSD_GROUPS):
            yg = yf[:, g * 512:(g + 1) * 512]
            r = lax.rsqrt(jnp.mean(yg * yg, axis=-1, keepdims=True) + RMS_EPS)
            outs.append(yg * r)
        y_ref[...] = (jnp.concatenate(outs, axis=1) * nw_ref[0:1, :]).astype(y_ref.dtype)

    const = lambda shape: pl.BlockSpec(shape, lambda c: (0, 0))
    smem = pl.BlockSpec(memory_space=pltpu.SMEM)
    return _call(
        body, comm, name="ssd_fwd", grid=(nc,),
        in_specs=[pl.BlockSpec((CHUNK, S_W), lambda c: (c, 0)),
                  pl.BlockSpec((8, S_W), lambda c: (jnp.maximum(c * 16 - 1, 0), 0)),
                  const((4, D_XBC)), const((1, D_XBC)), smem, smem, smem, const((1, 1024)), const((128, 1024))],
        out_specs=[pl.BlockSpec((CHUNK, D_SSD), lambda c: (c, 0)), pl.BlockSpec((CHUNK, D_SSD), lambda c: (c, 0)),
                   pl.BlockSpec((128, 1024), lambda c: (c, 0)), pl.BlockSpec((CHUNK, D_XBC), lambda c: (c, 0))],
        out_shape=[jax.ShapeDtypeStruct((L, D_SSD + D_ATT), _MXU), jax.ShapeDtypeStruct((L, D_SSD), F32),
                   jax.ShapeDtypeStruct((nc * 128, 1024), F32), jax.ShapeDtypeStruct((L, D_XBC), F32)],
        scratch_shapes=[pltpu.VMEM((128, 1024), F32), pltpu.VMEM((136, D_XBC), F32)],
        args=(proj_ssd, proj_ssd, conv_w8, conv_b8, dtb8, alog8, dskip_e, norm_w, E))


def _ssd_backward(proj_ssd, hprev_all, ypre, pre, dy, conv_w8, conv_b8, dtb8, alog8, dskip_e, norm_w, E, ET, comm=None):
    L = proj_ssd.shape[0]
    nc = L // CHUNK

    def body(p_ref, halo_ref, hprev_ref, ypre_ref, pre_ref, dy_ref, cw_ref, cb_ref, dtb_ref, alog_ref, dsk_ref, nw_ref, e_ref,
             et_ref, dp_ref, acc_cw_ref, acc_w_ref, acc_s_ref, dh_scr, ext_scr, ext2_scr, nxt_scr):
        i = pl.program_id(0)
        c = nc - 1 - i
        first = c == 0

        @pl.when(i == 0)
        def _():
            dh_scr[...] = jnp.zeros_like(dh_scr)
            nxt_scr[...] = jnp.zeros_like(nxt_scr)
            acc_cw_ref[...] = jnp.zeros_like(acc_cw_ref)
            acc_w_ref[...] = jnp.zeros_like(acc_w_ref)
            acc_s_ref[...] = jnp.zeros_like(acc_s_ref)

        R = _ssd_recompute(first, p_ref, halo_ref, cw_ref, cb_ref, dtb_ref, alog_ref, e_ref, ext_scr, pre_ref[...])
        hprev = hprev_ref[...]
        xs_c, X, Xd = R["xs_c"], R["X"], R["Xd"]
        acs, acsT = R["acs"], R["acsT"]
        ET = et_ref[...]
        dsk = _head_row(dsk_ref, D_SSD, HEAD_DIM)
        Y = ypre_ref[...]

        z = p_ref[:, S_Z:S_Z + 1024]
        sz = _sigmoid(z)
        silz = z * sz
        yf = Y * silz
        dyv = dy_ref[...]
        nw = nw_ref[0:1, :]
        dyf_parts, dnw_parts = [], []
        for g in range(SSD_GROUPS):
            cols = slice(g * 512, (g + 1) * 512)
            yg = yf[:, cols]
            r = lax.rsqrt(jnp.mean(yg * yg, axis=-1, keepdims=True) + RMS_EPS)
            yn = yg * r
            dyn = dyv[:, cols] * nw[:, cols]
            dnw_parts.append(_colsum(dyv[:, cols] * yn))
            dyf_parts.append(r * (dyn - yn * jnp.mean(dyn * yn, axis=-1, keepdims=True)))
        dyf = jnp.concatenate(dyf_parts, axis=1)
        dY = dyf * silz
        dz = dyf * Y * (sz * (1.0 + z * (1.0 - sz)))

        dhn = dh_scr[...]
        dYo = dY * R["eo_e"]
        causal = R["row"] >= R["col"]
        dacs = jnp.zeros((128, 128), F32)
        dacs_t = jnp.zeros((128, 128), F32)
        dxdiag, dxd, dhprev, dBs, dCs, yoff = [], [], [], [], [], []
        for g in range(SSD_GROUPS):
            Bg = R["Bc"][:, g * 128:(g + 1) * 128]
            Cg = R["Cc"][:, g * 128:(g + 1) * 128]
            cols = slice(g * 512, (g + 1) * 512)
            CB = _mm_nt(Cg, Bg)
            dCB = jnp.zeros((128, 128), F32)
            for j in range(4):
                h0 = g * 8 + 2 * j
                pc = slice(h0 * HEAD_DIM, h0 * HEAD_DIM + 128)
                dYst = _split_halves(dY[:, pc])
                dMst = _mm_nt(dYst, X[:, pc])
                mts = []
                for a, h in enumerate((h0, h0 + 1)):
                    acol = acs[:, h:h + 1]
                    arow = acsT[h:h + 1, :]
                    Lm = jnp.exp(jnp.where(causal, acol - arow, NEG))
                    M = CB * Lm
                    dM = dMst[a * 128:(a + 1) * 128]
                    dCB = dCB + dM * Lm
                    G = dM * M
                    dacs = dacs + jnp.where(R["col"] == h, jnp.sum(G, axis=1, keepdims=True), 0.0)
                    dacs_t = dacs_t + jnp.where(R["row"] == h, jnp.sum(G, axis=0, keepdims=True), 0.0)
                    mts.append(M.T)
                dxdiag.append(_mm(jnp.concatenate(mts, axis=1), dYst))
            dS = dhn[:, cols]
            dxd.append(_mm(Bg, dS))
            yoff.append(_mm(Cg, hprev[:, cols]))
            dhprev.append(_mm_tn(Cg, dYo[:, cols]))
            dCs.append(_mm_nt(dYo[:, cols], hprev[:, cols]) + _mm(dCB, Bg))
            dBs.append(_mm_tn(dCB, Cg) + _mm_nt(Xd[:, cols], dS))
        Yoff = jnp.concatenate(yoff, axis=1) * R["eo_e"]
        dXd = jnp.concatenate(dxd, axis=1)
        dX = jnp.concatenate(dxdiag, axis=1) + dXd * R["ds_e"]
        t_state = dXd * Xd
        dacs = dacs + _mm_exact_r(dY * Yoff - t_state, ET) - dacs_t.T
        v_last = _colsum(t_state + dhn * hprev * R["cd_e"])
        dlast = _mm_exact_r(jnp.broadcast_to(v_last, (8, 1024)), ET)[0:1, :]
        dacs = dacs + jnp.where(R["row"] == 127, dlast, 0.0)
        triu = (R["col"] >= R["row"]).astype(BF16)
        da = _mm_exact_l(triu, dacs)
        ddt = da * R["A"] + _mm(dX * xs_c, ET)
        ddt_raw = ddt * _sigmoid(R["raw"])
        dxs_c = dX * R["dt_e"] + dY * dsk
        dh_scr[...] = jnp.concatenate(dhprev, axis=1) + dhn * R["cd_e"]

        dact = jnp.concatenate([dxs_c] + dBs + dCs, axis=1)
        pre, sg = R["pre"], R["sg"]
        dpre = dact * (sg * (1.0 + pre * (1.0 - sg)))
        ext2_scr[0:8, :] = dpre[120:128, :]
        ext2_scr[8:16, :] = nxt_scr[...]
        nxt_scr[...] = dpre[0:8, :]
        cw = cw_ref[...]
        u_b, dpre_b = p_ref[:, S_XS:S_DT].astype(_MXU), dpre.astype(_MXU)
        dxbc = cw[3:4, :] * dpre
        taps = [_colsum(dpre * p_ref[:, S_XS:S_DT])]
        for s in (1, 2, 3):
            down = (R["row"] - R["col"] == s).astype(_MXU)
            up = (R["col"] - R["row"] == s).astype(_MXU)
            u_s = jnp.concatenate([ext_scr[8 - s:16 - s, :],
                                   jnp.dot(down, u_b, preferred_element_type=F32)[8:128]], axis=0)
            d_s = jnp.concatenate([jnp.dot(up, dpre_b, preferred_element_type=F32)[0:120],
                                   ext2_scr[s:8 + s, :]], axis=0)
            dxbc = dxbc + cw[3 - s:4 - s, :] * d_s
            taps.append(_colsum(dpre * u_s))
        acc_cw_ref[...] += _rows8(taps[::-1] + [_colsum(dpre)])
        acc_w_ref[...] += _rows8([jnp.concatenate(dnw_parts, axis=1), _colsum(dY * xs_c)])
        acc_s_ref[...] += _rows8([_colsum(ddt_raw), _colsum(da * R["dt"])])

        lane = lax.broadcasted_iota(jnp.int32, (128, 128), 1)
        dp_ref[:, S_Z:S_Z + 1024] = dz
        dp_ref[:, S_XS:S_DT] = dxbc
        dp_ref[:, S_DT:S_DT + 128] = jnp.where(lane < N_HEADS, ddt_raw, 0.0)
        dp_ref[:, S_DT + 128:S_W] = jnp.zeros((128, 128), F32)

        @pl.when(i == nc - 1)
        def _():
            acc = acc_s_ref[...]
            dskip = _mm_exact_r(acc_w_ref[...], ET)[1:2, :]
            acc_s_ref[...] = _rows8([acc[0:1, :], acc[1:2, :] * R["A"], dskip])

    const = lambda shape: pl.BlockSpec(shape, lambda i: (0, 0))
    smem = pl.BlockSpec(memory_space=pltpu.SMEM)
    rev = lambda i: (nc - 1 - i, 0)
    return _call(
        body, comm, name="ssd_bwd", grid=(nc,),
        in_specs=[pl.BlockSpec((CHUNK, S_W), rev),
                  pl.BlockSpec((8, S_W), lambda i: (jnp.maximum((nc - 1 - i) * 16 - 1, 0), 0)),
                  pl.BlockSpec((128, 1024), rev),
                  pl.BlockSpec((CHUNK, D_SSD), rev),
                  pl.BlockSpec((CHUNK, D_XBC), rev),
                  pl.BlockSpec((CHUNK, D_SSD), rev),
                  const((4, D_XBC)), const((1, D_XBC)), smem, smem, smem, const((1, 1024)),
                  const((128, 1024)), const((1024, 128))],
        out_specs=[pl.BlockSpec((CHUNK, S_W), rev), const((8, D_XBC)), const((8, 1024)), const((8, 128))],
        out_shape=[jax.ShapeDtypeStruct((L, S_W), F32), jax.ShapeDtypeStruct((8, D_XBC), F32),
                   jax.ShapeDtypeStruct((8, 1024), F32), jax.ShapeDtypeStruct((8, 128), F32)],
        scratch_shapes=[pltpu.VMEM((128, 1024), F32), pltpu.VMEM((16, D_XBC), F32),
                        pltpu.VMEM((16, D_XBC), F32), pltpu.VMEM((8, D_XBC), F32)],
        args=(proj_ssd, proj_ssd, hprev_all, ypre, pre, dy, conv_w8, conv_b8, dtb8, alog8, dskip_e, norm_w, E, ET))


def _rope(t, tab):
    cos, sa, sb = tab[:, 0:128], tab[:, 128:256], tab[:, 256:384]
    outs = []
    for i in range(t.shape[1] // 128):
        tg = t[:, i * 128:(i + 1) * 128]
        outs.append(tg * cos + pltpu.roll(tg, 8, 1) * sa + pltpu.roll(tg, 120, 1) * sb)
    return jnp.concatenate(outs, axis=1)


def _rope_transposed(d, tab):
    cos, sa, sb = tab[:, 0:128], tab[:, 128:256], tab[:, 256:384]
    outs = []
    for i in range(d.shape[1] // 128):
        dg = d[:, i * 128:(i + 1) * 128]
        outs.append(dg * cos + pltpu.roll(dg * sa, 120, 1) + pltpu.roll(dg * sb, 8, 1))
    return jnp.concatenate(outs, axis=1)


def _lo_half(rows):
    return lax.broadcasted_iota(jnp.int32, (rows, 128), 1) < HEAD_DIM


def _kv_both(t, j):
    p, b = j // 2, j % 2
    lo = _lo_half(2 * CHUNK)
    nat = jnp.where(lo if b == 0 else jnp.logical_not(lo), t[:, p * 128:(p + 1) * 128], 0.0)
    return nat + pltpu.roll(nat, HEAD_DIM, 1)


def _stack_heads(t, j):
    lo = _lo_half(CHUNK)
    hi = jnp.logical_not(lo)
    a, b = t[:, 2 * j * 128:(2 * j + 1) * 128], t[:, (2 * j + 1) * 128:(2 * j + 2) * 128]
    return jnp.concatenate([jnp.where(lo, a, 0.0), jnp.where(hi, a, 0.0),
                            jnp.where(lo, b, 0.0), jnp.where(hi, b, 0.0)], axis=0)


def _unstack_heads(s):
    lo = _lo_half(CHUNK)
    return jnp.concatenate([jnp.where(lo, s[0:128], s[128:256]), jnp.where(lo, s[256:384], s[384:512])], axis=1)


def _fold_kv(r, j):
    lo = _lo_half(2 * CHUNK)
    return jnp.where(lo if j % 2 == 0 else jnp.logical_not(lo), r + pltpu.roll(r, HEAD_DIM, 1), 0.0)


def _sink_row(sink_ref, j):
    hid = lax.broadcasted_iota(jnp.int32, (1, 4 * CHUNK), 1) // CHUNK
    row = jnp.zeros((1, 4 * CHUNK), F32)
    for hh in range(4):
        row = jnp.where(hid == hh, sink_ref[4 * j + hh], row)
    return row


def _band_mask(blk):
    si = lax.broadcasted_iota(jnp.int32, (2 * CHUNK, 4 * CHUNK), 0)
    qi = lax.broadcasted_iota(jnp.int32, (2 * CHUNK, 4 * CHUNK), 1) % CHUNK
    return (si > qi) & (si <= qi + CHUNK) & ((blk > 0) | (si >= CHUNK))


def _softmax_sink(s, valid, sink):
    s = jnp.where(valid, s, NEG)
    mx = jnp.maximum(jnp.max(s, axis=0, keepdims=True), sink)
    p = jnp.exp(s - mx)
    esink = jnp.exp(sink - mx)
    inv = 1.0 / (jnp.sum(p, axis=0, keepdims=True) + esink)
    return p * inv, esink * inv


def _swa_forward(proj_att, tabs, sinks, y):
    L = proj_att.shape[0]
    nb = L // CHUNK

    def body(sink_ref, p_ref, prev_ref, tab_ref, ptab_ref, y_in_ref, y_ref):
        n = pl.program_id(0)
        tab = tab_ref[...]
        qr = _rope(p_ref[:, A_Q:A_Q + 1024], tab)
        k_cur = _rope(p_ref[:, A_K:A_K + 256], tab)
        k_prev = _rope(prev_ref[:, 0:256], ptab_ref[...])
        kk = jnp.concatenate([k_prev, k_cur], axis=0)
        vv = jnp.concatenate([prev_ref[:, 256:512], p_ref[:, A_V:A_V + 256]], axis=0)
        valid = _band_mask(n)
        outs = []
        for j in range(KV_HEADS):
            s = _mm_nt(_kv_both(kk, j), _stack_heads(qr, j)) * ATT_SCALE
            P, _ = _softmax_sink(s, valid, _sink_row(sink_ref, j))
            outs.append(_unstack_heads(_mm_tn(P, _kv_both(vv, j))))
        g = p_ref[:, A_G:A_G + 1024]
        y_ref[...] = (jnp.concatenate(outs, axis=1) * (g * _sigmoid(g))).astype(y_ref.dtype)

    return pl.pallas_call(
        body, name="swa_fwd", grid=(nb,),
        in_specs=[pl.BlockSpec(memory_space=pltpu.SMEM),
                  pl.BlockSpec((CHUNK, A_W), lambda n: (n, 0)),
                  pl.BlockSpec((CHUNK, 512), lambda n: (jnp.maximum(n - 1, 0), 2)),
                  pl.BlockSpec((CHUNK, 384), lambda n: (n, 0)),
                  pl.BlockSpec((CHUNK, 384), lambda n: (jnp.maximum(n - 1, 0), 0)),
                  pl.BlockSpec(memory_space=pl.ANY)],
        out_specs=pl.BlockSpec((CHUNK, D_ATT), lambda n: (n, 1)),
        out_shape=jax.ShapeDtypeStruct(y.shape, y.dtype),
        input_output_aliases={5: 0},
        compiler_params=_params(("arbitrary",)),
    )(sinks, proj_att, proj_att, tabs, tabs, y)


def _swa_backward(proj_att, tabs, sinks, dy, comm=None):
    L = proj_att.shape[0]
    nb = L // CHUNK

    def body(sink_ref, p_ref, prev_ref, tab_ref, ptab_ref, dy_ref, dp_ref, dsink_ref, carry_k, carry_v):
        i = pl.program_id(0)
        n = nb - 1 - i

        @pl.when(i == 0)
        def _():
            carry_k[...] = jnp.zeros_like(carry_k)
            carry_v[...] = jnp.zeros_like(carry_v)
            dsink_ref[...] = jnp.zeros_like(dsink_ref)

        tab = tab_ref[...]
        qr = _rope(p_ref[:, A_Q:A_Q + 1024], tab)
        k_cur = _rope(p_ref[:, A_K:A_K + 256], tab)
        k_prev = _rope(prev_ref[:, 0:256], ptab_ref[...])
        kk = jnp.concatenate([k_prev, k_cur], axis=0)
        vv = jnp.concatenate([prev_ref[:, 256:512], p_ref[:, A_V:A_V + 256]], axis=0)
        valid = _band_mask(n)
        g = p_ref[:, A_G:A_G + 1024]
        sgm = _sigmoid(g)
        dyv = dy_ref[...]
        do_all = dyv * (g * sgm)
        lane8 = lax.broadcasted_iota(jnp.int32, (8, 128), 1)
        hid = lax.broadcasted_iota(jnp.int32, (1, 4 * CHUNK), 1) // CHUNK
        o_parts, dq_parts = [], []
        dk_nat = [jnp.zeros((2 * CHUNK, 128), F32) for _ in range(2)]
        dv_nat = [jnp.zeros((2 * CHUNK, 128), F32) for _ in range(2)]
        dsink = jnp.zeros((8, 128), F32)
        for j in range(KV_HEADS):
            qs = _stack_heads(qr, j)
            kkb, vvb = _kv_both(kk, j), _kv_both(vv, j)
            s = _mm_nt(kkb, qs) * ATT_SCALE
            P, psink = _softmax_sink(s, valid, _sink_row(sink_ref, j))
            o_parts.append(_unstack_heads(_mm_tn(P, vvb)))
            do_s = _stack_heads(do_all, j)
            dP = _mm_nt(vvb, do_s)
            D = jnp.sum(P * dP, axis=0, keepdims=True)
            dS = P * (dP - D)
            sd = psink * D
            for hh in range(4):
                dsink = dsink + jnp.where(lane8 == 4 * j + hh, -jnp.sum(jnp.where(hid == hh, sd, 0.0)), 0.0)
            dq_parts.append(_unstack_heads(_mm_tn(dS, kkb)) * ATT_SCALE)
            dk_nat[j // 2] = dk_nat[j // 2] + _fold_kv(_mm(dS, qs), j) * ATT_SCALE
            dv_nat[j // 2] = dv_nat[j // 2] + _fold_kv(_mm(P, do_s), j)
        o = jnp.concatenate(o_parts, axis=1)
        dkk = jnp.concatenate(dk_nat, axis=1)
        dvv = jnp.concatenate(dv_nat, axis=1)
        dp_ref[:, A_Q:A_Q + 1024] = _rope_transposed(jnp.concatenate(dq_parts, axis=1), tab)
        dp_ref[:, A_K:A_K + 256] = _rope_transposed(dkk[CHUNK:2 * CHUNK] + carry_k[...], tab)
        dp_ref[:, A_V:A_V + 256] = dvv[CHUNK:2 * CHUNK] + carry_v[...]
        dp_ref[:, A_G:A_G + 1024] = dyv * o * (sgm * (1.0 + g * (1.0 - sgm)))
        carry_k[...] = dkk[0:CHUNK]
        carry_v[...] = dvv[0:CHUNK]
        dsink_ref[...] += dsink

    rev = lambda i: (nb - 1 - i, 0)
    prev = lambda i: jnp.maximum(nb - 2 - i, 0)
    return _call(
        body, comm, name="swa_bwd", grid=(nb,),
        in_specs=[pl.BlockSpec(memory_space=pltpu.SMEM),
                  pl.BlockSpec((CHUNK, A_W), rev),
                  pl.BlockSpec((CHUNK, 512), lambda i: (prev(i), 2)),
                  pl.BlockSpec((CHUNK, 384), rev),
                  pl.BlockSpec((CHUNK, 384), lambda i: (prev(i), 0)),
                  pl.BlockSpec((CHUNK, D_ATT), lambda i: (nb - 1 - i, 1))],
        out_specs=[pl.BlockSpec((CHUNK, A_W), rev), pl.BlockSpec((8, 128), lambda i: (0, 0))],
        out_shape=[jax.ShapeDtypeStruct((L, A_W), F32), jax.ShapeDtypeStruct((8, 128), F32)],
        scratch_shapes=[pltpu.VMEM((CHUNK, 256), F32), pltpu.VMEM((CHUNK, 256), F32)],
        args=(sinks, proj_att, proj_att, tabs, tabs, dy))


def _head(y, x, target, w_out, ln_g8, ln_b8, *, tm):
    L = x.shape[0]
    nsteps = L // tm

    def body(y_ref, x_ref, t_ref, wo_ref, g_ref, b_ref, dr_ref, dy_ref, acc_ref):
        i = pl.program_id(0)

        @pl.when(i == 0)
        def _():
            acc_ref[...] = jnp.zeros_like(acc_ref)

        r = ALPHA * x_ref[...] + _mm(y_ref[...], wo_ref[...])
        mu = jnp.mean(r, axis=-1, keepdims=True)
        d = r - mu
        rstd = lax.rsqrt(jnp.mean(d * d, axis=-1, keepdims=True) + LN_EPS)
        xh = d * rstd
        gam = g_ref[0:1, :]
        e = xh * gam + b_ref[0:1, :] - t_ref[...]
        dout = e * (1.0 / D_MODEL)
        dxh = dout * gam
        dr = rstd * (dxh - jnp.mean(dxh, axis=-1, keepdims=True)
                     - xh * jnp.mean(dxh * xh, axis=-1, keepdims=True))
        dr_ref[...] = dr
        dy_ref[...] = _mm_nt(dr, wo_ref[...])
        acc_ref[...] += _rows8([_colsum(dout * xh), _colsum(dout), _colsum(e * e) * (0.5 / D_MODEL)])

        @pl.when(i == nsteps - 1)
        def _():
            acc = acc_ref[...]
            tot = jnp.sum(acc[2:3, :])
            rid = lax.broadcasted_iota(jnp.int32, (8, 1024), 0)
            acc_ref[...] = jnp.where(rid == 3, tot, acc)

    const = lambda shape: pl.BlockSpec(shape, lambda i: (0, 0))
    row = lambda w: pl.BlockSpec((tm, w), lambda i: (i, 0))
    return pl.pallas_call(
        body, name="head", grid=(nsteps,),
        in_specs=[row(2048), row(1024), row(1024), const((2048, 1024)), const((1, 1024)), const((1, 1024))],
        out_specs=[row(1024), row(2048), const((8, 1024))],
        out_shape=[jax.ShapeDtypeStruct((L, D_MODEL), F32), jax.ShapeDtypeStruct((L, 2048), F32),
                   jax.ShapeDtypeStruct((8, 1024), F32)],
        compiler_params=_params(("arbitrary",)),
    )(y, x, target, w_out, ln_g8, ln_b8)


def _all_gather(shards):
    n = len(shards)
    any_spec = pl.BlockSpec(memory_space=pl.ANY)

    def body(*refs):
        ins, outs = refs[:n], refs[n:2 * n]
        send_sems, recv_sems, local_sems = refs[2 * n:]
        x, y, c = _position()
        me, sibling = (x, y, c), (x, y, 1 - c)
        chips = [(1 - x, y), (x, 1 - y), (1 - x, 1 - y)]

        def copy(a, k, block, to, src=None):
            slot = outs[a].at[_index(*block)]
            return pltpu.make_async_remote_copy(
                src_ref=slot if src is None else src, dst_ref=slot,
                send_sem=send_sems.at[a, k], recv_sem=recv_sems.at[a, k],
                device_id=to, device_id_type=MESH)

        mine = [pltpu.make_async_copy(ins[a], outs[a].at[_index(*me)], local_sems.at[a]) for a in range(n)]
        for cp in mine:
            cp.start()
        first = []
        for a in range(n):
            first.append(copy(a, 0, me, sibling, src=ins[a]))
            first += [copy(a, 1 + j, me, (*chip, c), src=ins[a]) for j, chip in enumerate(chips)]
        for cp in first:
            cp.start()
        passed = []
        for j, chip in enumerate(chips):
            for a in range(n):
                copy(a, 1 + j, (*chip, c), me).wait_recv()
                fwd = copy(a, 4 + j, (*chip, c), sibling)
                fwd.start()
                passed.append(fwd)
        for a in range(n):
            copy(a, 0, sibling, me).wait_recv()
            for j, chip in enumerate(chips):
                copy(a, 4 + j, (*chip, 1 - c), me).wait_recv()
        for cp in first + passed:
            cp.wait_send()
        for cp in mine:
            cp.wait()

    return pl.pallas_call(
        body, name="weight_all_gather",
        in_specs=[any_spec] * n, out_specs=[any_spec] * n,
        out_shape=[jax.ShapeDtypeStruct((N_DEV,) + s.shape, s.dtype) for s in shards],
        scratch_shapes=[pltpu.SemaphoreType.DMA((n, 7)), pltpu.SemaphoreType.DMA((n, 7)),
                        pltpu.SemaphoreType.DMA((n,))],
    )(*shards)


def _input_gradient(d_ssd, d_att, w_ssd, w_att, dr, *, tm, comm=None):
    L = dr.shape[0]

    def body(ds_ref, da_ref, ws_ref, wa_ref, dr_ref, o_ref):
        o_ref[...] = ALPHA * dr_ref[...] + _mm_nt(ds_ref[...], ws_ref[...]) + _mm_nt(da_ref[...], wa_ref[...])

    row = lambda w: pl.BlockSpec((tm, w), lambda i: (i, 0))
    const = lambda shape: pl.BlockSpec(shape, lambda i: (0, 0))
    return _call(body, comm, name="dx", grid=(L // tm,),
                 in_specs=[row(S_W), row(A_W), const((D_MODEL, S_W)), const((D_MODEL, A_W)), row(D_MODEL)],
                 out_specs=[row(D_MODEL)], out_shape=[jax.ShapeDtypeStruct((L, D_MODEL), F32)],
                 scratch_shapes=[], args=(d_ssd, d_att, w_ssd, w_att, dr))


SHARD_COLS = D_IN_PROJ // N_DEV
SPLIT = N_SSD_REAL - 4 * SHARD_COLS
RELAYOUT_ROWS = 256


def _unpack_w_in(w_all):
    def body(g_ref, ws_ref, wa_ref):
        for j in range(4):
            ws_ref[:, SHARD_COLS * j:SHARD_COLS * (j + 1)] = g_ref[j]
        ws_ref[:, 4 * SHARD_COLS:N_SSD_REAL] = g_ref[4, :, 0:SPLIT]
        ws_ref[:, N_SSD_REAL:S_W] = jnp.zeros((RELAYOUT_ROWS, S_W - N_SSD_REAL), ws_ref.dtype)
        wa_ref[:, 0:SHARD_COLS - SPLIT] = g_ref[4, :, SPLIT:SHARD_COLS]
        for j in range(5, N_DEV):
            lo = SHARD_COLS * (j - 4) - SPLIT
            wa_ref[:, lo:lo + SHARD_COLS] = g_ref[j]

    return pl.pallas_call(
        body, name="unpack_w_in", grid=(D_MODEL // RELAYOUT_ROWS,),
        in_specs=[pl.BlockSpec((N_DEV, RELAYOUT_ROWS, SHARD_COLS), lambda i: (0, i, 0))],
        out_specs=[pl.BlockSpec((RELAYOUT_ROWS, S_W), lambda i: (i, 0)), pl.BlockSpec((RELAYOUT_ROWS, A_W), lambda i: (i, 0))],
        out_shape=[jax.ShapeDtypeStruct((D_MODEL, S_W), w_all.dtype), jax.ShapeDtypeStruct((D_MODEL, A_W), w_all.dtype)],
        compiler_params=_params(("arbitrary",)),
    )(w_all)


def _pack_dw_in(me1, dw_ssd, dw_att, half):
    def body(me_ref, *refs):
        if half == 0:
            ds_ref, p_ref, own_ref = refs
            me = me_ref[0]

            @pl.when(me >= 4)
            def _():
                own_ref[...] = jnp.zeros_like(own_ref)
        else:
            ds_ref, da_ref, p_ref = refs

        for j in range(4):
            if half == 0:
                pieces = [(0, ds_ref[:, SHARD_COLS * j:SHARD_COLS * (j + 1)])]
            elif j == 0:
                pieces = [(0, ds_ref[:, 4 * SHARD_COLS:N_SSD_REAL]), (SPLIT, da_ref[:, 0:SHARD_COLS - SPLIT])]
            else:
                lo = SHARD_COLS * j - SPLIT
                pieces = [(0, da_ref[:, lo:lo + SHARD_COLS])]
            for off, blk in pieces:
                p_ref[j, :, off:off + blk.shape[1]] = blk.astype(p_ref.dtype)
                if half == 0:
                    @pl.when(me == j)
                    def _(off=off, blk=blk):
                        own_ref[:, off:off + blk.shape[1]] = blk

    ins = [dw_ssd] if half == 0 else [dw_ssd, dw_att]
    row = lambda a: pl.BlockSpec((RELAYOUT_ROWS, a.shape[1]), lambda i: (i, 0))
    out_specs = [pl.BlockSpec((4, RELAYOUT_ROWS, SHARD_COLS), lambda i: (0, i, 0))]
    out_shape = [jax.ShapeDtypeStruct((4, D_MODEL, SHARD_COLS), BF16 if half == 0 else F32)]
    if half == 0:
        out_specs.append(pl.BlockSpec((RELAYOUT_ROWS, SHARD_COLS), lambda i: (i, 0)))
        out_shape.append(jax.ShapeDtypeStruct((D_MODEL, SHARD_COLS), F32))
    return pl.pallas_call(
        body, name="pack_dw_in_%d" % half, grid=(D_MODEL // RELAYOUT_ROWS,),
        in_specs=[pl.BlockSpec(memory_space=pltpu.SMEM)] + [row(a) for a in ins],
        out_specs=out_specs, out_shape=out_shape, compiler_params=_params(("arbitrary",)),
    )(me1, *ins)


def _pair_swap(stack):
    def body(in_ref, out_ref, send_sems, recv_sems):
        x, y, c = _position()
        cps = [_remote(in_ref.at[2 * oy + (1 - c)], out_ref.at[oy], send_sems.at[oy], recv_sems.at[oy], (x, y, 1 - c))
               for oy in range(2)]
        for cp in cps:
            cp.start()
        for cp in cps:
            cp.wait_recv()
        for cp in cps:
            cp.wait_send()

    any_spec = pl.BlockSpec(memory_space=pl.ANY)
    return pl.pallas_call(
        body, name="pair_swap", in_specs=[any_spec], out_specs=any_spec,
        out_shape=jax.ShapeDtypeStruct((2,) + stack.shape[1:], stack.dtype),
        scratch_shapes=[pltpu.SemaphoreType.DMA((2,)), pltpu.SemaphoreType.DMA((2,))],
    )(stack)


def _pair_sum(pos3, stack, swapped, own_lo):
    def body(pos_ref, a_ref, b_ref, lo_ref, chip_ref, own_ref):
        oy = pl.program_id(1)
        t = a_ref[0] + b_ref[0]
        chip_ref[0] = t.astype(chip_ref.dtype)

        @pl.when((pos_ref[0] == 0) & (oy == 0))
        def _():
            own_ref[...] = lo_ref[...]

        @pl.when((pos_ref[0] == 1) & (oy == pos_ref[1]))
        def _():
            own_ref[...] = t

    blk = (1, RELAYOUT_ROWS, SHARD_COLS)
    flat = pl.BlockSpec((RELAYOUT_ROWS, SHARD_COLS), lambda i, oy, pos: (i, 0))
    return pl.pallas_call(
        body, name="pair_sum",
        grid_spec=pltpu.PrefetchScalarGridSpec(
            num_scalar_prefetch=1, grid=(D_MODEL // RELAYOUT_ROWS, 2),
            in_specs=[pl.BlockSpec(blk, lambda i, oy, pos: (2 * oy + pos[2], i, 0)),
                      pl.BlockSpec(blk, lambda i, oy, pos: (oy, i, 0)), flat],
            out_specs=[pl.BlockSpec(blk, lambda i, oy, pos: (oy, i, 0)), flat]),
        out_shape=[jax.ShapeDtypeStruct((2, D_MODEL, SHARD_COLS), BF16), jax.ShapeDtypeStruct((D_MODEL, SHARD_COLS), F32)],
        compiler_params=_params(("arbitrary", "arbitrary")),
    )(pos3, stack, swapped, own_lo)


def _adamw_math(w, g, m, v):
    m = ADAM_B1 * m + (1.0 - ADAM_B1) * g
    v = ADAM_B2 * v + (1.0 - ADAM_B2) * (g * g)
    m_hat = m / (1.0 - ADAM_B1 ** ADAM_STEP)
    v_hat = v / (1.0 - ADAM_B2 ** ADAM_STEP)
    delta = -ADAM_LR * (m_hat / (jnp.sqrt(v_hat) + ADAM_EPS) + ADAM_WD * w)
    return delta, m, v


def _adamw_shard(n_recv, g_own, recv, w, m, v, *, rows, name):
    R, C = g_own.shape

    def body(n_ref, g_ref, r_ref, w_ref, m_ref, v_ref, go_ref, d_ref, mo_ref, vo_ref):
        g = g_ref[...]
        for k in range(N_DEV - 1):
            g = g + jnp.where(k < n_ref[0], r_ref[k].astype(F32), 0.0)
        d, mn, vn = _adamw_math(w_ref[...], g, m_ref[...], v_ref[...])
        go_ref[...] = g
        d_ref[...] = d
        mo_ref[...] = mn
        vo_ref[...] = vn

    blk = pl.BlockSpec((rows, C), lambda i: (i, 0))
    return pl.pallas_call(
        body, name=name, grid=(R // rows,),
        in_specs=[pl.BlockSpec(memory_space=pltpu.SMEM), blk,
                  pl.BlockSpec((N_DEV - 1, rows, C), lambda i: (0, i, 0)), blk, blk, blk],
        out_specs=[blk] * 4, out_shape=[jax.ShapeDtypeStruct((R, C), F32)] * 4,
        compiler_params=_params(("arbitrary",)),
    )(n_recv, g_own, recv, w, m, v)


SMALL = ("conv_b", "dt_bias", "a_log", "d_skip", "ssd_norm_w", "attn_sinks", "ln_g", "ln_b")


def _adamw_small(gathered, params):
    n_p = len(SMALL)

    def body(*refs):
        acc = []
        for r in refs[:5]:
            t = r[0]
            for k in range(1, N_DEV):
                t = t + r[k]
            acc.append(t)
        head, conv, norm, scal, sink = acc
        grads = dict(conv_b=conv[4:5, :], dt_bias=scal[0:1, 0:N_HEADS], a_log=scal[1:2, 0:N_HEADS],
                     d_skip=scal[2:3, 0:N_HEADS], ssd_norm_w=norm[0:1, :], attn_sinks=sink[0:1, 0:N_HEADS],
                     ln_g=head[0:1, :], ln_b=head[1:2, :])
        wmv = refs[5:5 + 3 * n_p]
        outs = refs[5 + 3 * n_p:]
        outs[0][...] = head[3:4, 0:1]
        outs[1][...] = conv[0:4, :]
        for i, name in enumerate(SMALL):
            w_ref, m_ref, v_ref = wmv[3 * i:3 * i + 3]
            g = grads[name]
            d, mn, vn = _adamw_math(w_ref[...], g, m_ref[...], v_ref[...])
            for o_ref, val in zip(outs[2 + 4 * i:6 + 4 * i], (g, d, mn, vn)):
                o_ref[...] = val

    flat = [a for name in SMALL for a in params[name]]
    out_shape = [jax.ShapeDtypeStruct((1, 1), F32), jax.ShapeDtypeStruct((4, D_XBC), F32)]
    for name in SMALL:
        out_shape += [jax.ShapeDtypeStruct(params[name][0].shape, F32)] * 4
    res = pl.pallas_call(body, name="adamw_small", out_shape=out_shape, compiler_params=_params())(*gathered, *flat)
    return res[0], res[1], {name: res[2 + 4 * i:6 + 4 * i] for i, name in enumerate(SMALL)}


def _adamw_plain(g, w, m, v):
    def body(g_ref, w_ref, m_ref, v_ref, d_ref, mo_ref, vo_ref):
        d, mn, vn = _adamw_math(w_ref[...], g_ref[...], m_ref[...], v_ref[...])
        d_ref[...] = d
        mo_ref[...] = mn
        vo_ref[...] = vn

    return pl.pallas_call(
        body, name="adamw_conv_w", out_shape=[jax.ShapeDtypeStruct(w.shape, F32)] * 3,
        compiler_params=_params(),
    )(g, w, m, v)


def _lane_pattern(fn):
    return np.asarray([fn(l % HEAD_DIM) for l in range(128)], np.float32)


ROPE_INV = _lane_pattern(lambda r: ROPE_THETA ** (-2.0 * (r % 8) / ROPE_DIM) if r < ROPE_DIM else 0.0)
ROPE_SIN_A = _lane_pattern(lambda r: 1.0 if 8 <= r < ROPE_DIM else 0.0)
ROPE_SIN_B = _lane_pattern(lambda r: -1.0 if r < 8 else 0.0)


def _rope_tables(positions):
    ang = positions.astype(F32)[:, None] * ROPE_INV[None, :]
    sn = jnp.sin(ang)
    return jnp.concatenate([jnp.cos(ang), sn * ROPE_SIN_A[None, :], sn * ROPE_SIN_B[None, :]], axis=1)


def _expansion():
    E = np.arange(1024)[None, :] // HEAD_DIM == np.arange(128)[:, None]
    return jnp.asarray(E, BF16), jnp.asarray(E.T, BF16)


def _ssd_args(conv_w, conv_b, dt_bias, a_log, d_skip, norm_w, E):
    return (conv_w, conv_b, dt_bias.reshape(-1), a_log.reshape(-1), d_skip.reshape(-1), norm_w, E)


def kernel(x, positions, w_in, conv_w, conv_b, dt_bias, a_log, d_skip, ssd_norm_w, attn_sinks, w_out, ln_g, ln_b, loss_target, m_w_in, m_conv_w, m_conv_b, m_dt_bias, m_a_log, m_d_skip, m_ssd_norm_w, m_attn_sinks, m_w_out, m_ln_g, m_ln_b, v_w_in, v_conv_w, v_conv_b, v_dt_bias, v_a_log, v_d_skip, v_ssd_norm_w, v_attn_sinks, v_w_out, v_ln_g, v_ln_b):
    me = _index(*_position())
    me1 = me.reshape(1).astype(jnp.int32)
    x0, target = x[0], loss_target[0]
    bf16_shard = lambda shape: jax.ShapeDtypeStruct(shape, BF16)
    E, ET = _expansion()
    tabs = _rope_tables(positions[0])
    sinks = attn_sinks.reshape(-1)

    w_in_all, conv_w_all = _all_gather([w_in[0].astype(BF16), conv_w[0]])
    w_ssd, w_att = _unpack_w_in(w_in_all)
    conv_w_f = jnp.transpose(conv_w_all, (1, 0, 2)).reshape(4, D_XBC)
    ssd_args = _ssd_args(conv_w_f, conv_b, dt_bias, a_log, d_skip, ssd_norm_w, E)

    proj_ssd, xb = _matmul(x0, w_ssd, tm=1024, tn=S_W // 2, name="in_proj_ssd", emit_a=True)
    proj_att = _matmul(xb, w_att, tm=1024, tn=A_W // 2, name="in_proj_att")
    gather_w_out = _Hosted([w_out[0].astype(BF16)], [bf16_shard((N_DEV, 256, D_MODEL))], [_Flow("gather", 0, 0)])
    y, ypre, hprev, pre, w_out_all = _ssd_forward(proj_ssd, *ssd_args, comm=gather_w_out)
    w_out_f = w_out_all.reshape(2 * D_MODEL, D_MODEL)
    y = _swa_forward(proj_att, tabs, sinks, y)
    dr, dy, acc_head = _head(y, x0, target, w_out_f, ln_g, ln_b, tm=256)

    dw_out, dw_out_bf16 = _matmul_tn(y, dr, tl=512, tn=D_MODEL, name="dw_out", emit_bf16=True)
    own_out = lax.dynamic_index_in_dim(dw_out.reshape(N_DEV, 256, D_MODEL), me, axis=0, keepdims=False)
    send_out = _Hosted([dw_out_bf16.reshape(N_DEV, 256, D_MODEL)], [bf16_shard((N_DEV - 1, 256, D_MODEL))],
                       [_Flow("exchange", 0, 0)])
    d_ssd, acc_cw, acc_w, acc_s, recv_out = _ssd_backward(proj_ssd, hprev, ypre, pre, dy, *ssd_args, ET, comm=send_out)
    dw_ssd = _matmul_tn(xb, d_ssd, tl=512, tn=S_W // 2, name="dw_in_ssd")
    parts_lo, own_lo = _pack_dw_in(me1, dw_ssd, None, 0)
    recv_shape = bf16_shard((N_DEV - 1, D_MODEL, SHARD_COLS))
    send_lo = _Hosted([parts_lo], [recv_shape], [_Flow("exchange", 0, 0, target_x=0)])
    d_att, dsink, recv_in = _swa_backward(proj_att, tabs, sinks, dy, comm=send_lo)
    dw_att = _matmul_tn(xb, d_att, tl=512, tn=A_W // 2, name="dw_in_att")
    (stack_hi,) = _pack_dw_in(me1, dw_ssd, dw_att, 1)
    pos3 = jnp.stack(_position()).astype(jnp.int32)
    chip_hi, own_in = _pair_sum(pos3, stack_hi, _pair_swap(stack_hi), own_lo)
    accs = [acc_head, acc_cw, acc_w, acc_s, dsink]
    send_hi = _Hosted([chip_hi, recv_in] + accs,
                      [recv_shape] + [jax.ShapeDtypeStruct((N_DEV,) + a.shape, F32) for a in accs],
                      [_Flow("chip_exchange", 0, 0, target_x=1)] + [_Flow("gather", 2 + i, 1 + i) for i in range(5)],
                      aliases={1: 0})
    dx, recv_in, *gathered = _input_gradient(d_ssd, d_att, w_ssd, w_att, dr, tm=256, comm=send_hi)
    n_recv_in = jnp.where(me < 4, N_DEV - 1, 3).reshape(1).astype(jnp.int32)
    n_recv_out = jnp.full((1,), N_DEV - 1, jnp.int32)

    g_in, d_in, nm_in, nv_in = _adamw_shard(n_recv_in, own_in, recv_in, w_in[0], m_w_in[0], v_w_in[0], rows=256,
                                            name="adamw_w_in")
    g_out, d_out, nm_out, nv_out = _adamw_shard(n_recv_out, own_out, recv_out, w_out[0], m_w_out[0], v_w_out[0],
                                                rows=256, name="adamw_w_out")
    loss, g_conv_w, small = _adamw_small(gathered, dict(
        conv_b=(conv_b, m_conv_b, v_conv_b), dt_bias=(dt_bias, m_dt_bias, v_dt_bias), a_log=(a_log, m_a_log, v_a_log),
        d_skip=(d_skip, m_d_skip, v_d_skip), ssd_norm_w=(ssd_norm_w, m_ssd_norm_w, v_ssd_norm_w),
        attn_sinks=(attn_sinks, m_attn_sinks, v_attn_sinks), ln_g=(ln_g, m_ln_g, v_ln_g), ln_b=(ln_b, m_ln_b, v_ln_b)))
    g_cw = lax.dynamic_slice_in_dim(g_conv_w, me * (D_XBC // N_DEV), D_XBC // N_DEV, axis=1)
    d_cw, nm_cw, nv_cw = _adamw_plain(g_cw, conv_w[0], m_conv_w[0], v_conv_w[0])

    def leaves(i, big_in, cw, big_out):
        mid = [small[k][i] for k in ("conv_b", "dt_bias", "a_log", "d_skip", "ssd_norm_w", "attn_sinks")]
        return [big_in[None], cw[None]] + mid + [big_out[None], small["ln_g"][i], small["ln_b"][i]]

    return (loss.reshape(()), dx[None], *leaves(0, g_in, g_cw, g_out), *leaves(1, d_in, d_cw, d_out),
            *leaves(2, nm_in, nm_cw, nm_out), *leaves(3, nv_in, nv_cw, nv_out))
```

```python
import jax
import jax.numpy as jnp
from jax import lax
from jax.experimental import pallas as pl
from jax.experimental.pallas import tpu as pltpu
import numpy as np

F32 = jnp.float32
BF16 = jnp.bfloat16
_MXU = jnp.bfloat16

N_DEV = 8
D_MODEL = 1024
D_SSD = 1024
D_ATT = 1024
HEAD_DIM = 64
N_HEADS = 16
SSD_GROUPS = 2
KV_HEADS = 4
CHUNK = 128
D_XBC = 1536
D_IN_PROJ = 5136
ROPE_DIM = 16
ROPE_THETA = 500000.0
ALPHA = (2.0 * 1) ** 0.25
LN_EPS = 1e-5
RMS_EPS = 1e-5
ATT_SCALE = HEAD_DIM ** -0.5
NEG = -1e30

S_Z, S_XS, S_B, S_C, S_DT, S_W = 0, 1024, 2048, 2304, 2560, 2816
N_SSD_REAL = 2576
A_Q, A_K, A_V, A_G, A_W = 0, 1024, 1280, 1536, 2560

ADAM_LR = 0.001
ADAM_B1 = 0.9
ADAM_B2 = 0.999
ADAM_EPS = 1e-08
ADAM_WD = 0.01
ADAM_STEP = 10

VMEM_LIMIT = 48 * 1024 * 1024
MESH = pl.DeviceIdType.MESH


def _params(sem=None):
    return pltpu.CompilerParams(dimension_semantics=sem, vmem_limit_bytes=VMEM_LIMIT)


def _mm(a, b):
    return jnp.dot(a.astype(_MXU), b.astype(_MXU), preferred_element_type=F32)


def _mm_nt(a, b):
    return lax.dot_general(a.astype(_MXU), b.astype(_MXU), (((1,), (1,)), ((), ())),
                           preferred_element_type=F32)


def _mm_tn(a, b):
    return lax.dot_general(a.astype(_MXU), b.astype(_MXU), (((0,), (0,)), ((), ())),
                           preferred_element_type=F32)


def _split3(v):
    hi = v.astype(BF16)
    r = v - hi.astype(F32)
    mid = r.astype(BF16)
    lo = (r - mid.astype(F32)).astype(BF16)
    return hi, mid, lo


def _mm_exact_r(v, p01):
    hi, mid, lo = _split3(v)
    d = lambda a: jnp.dot(a, p01, preferred_element_type=F32)
    return d(hi) + d(mid) + d(lo)


def _mm_exact_l(p01, v):
    hi, mid, lo = _split3(v)
    d = lambda a: jnp.dot(p01, a, preferred_element_type=F32)
    return d(hi) + d(mid) + d(lo)


def _sigmoid(x):
    return 1.0 / (1.0 + jnp.exp(-x))


def _softplus(x):
    e = jnp.exp(-jnp.abs(x))
    u = 1.0 + e
    log1p = jnp.where(u == 1.0, e, jnp.log(u) * (e / (u - 1.0)))
    return jnp.maximum(x, 0.0) + log1p


def _rows8(rows):
    n = rows[0].shape[1]
    rid = lax.broadcasted_iota(jnp.int32, (8, n), 0)
    out = jnp.zeros((8, n), F32)
    for k, r in enumerate(rows):
        out = out + jnp.where(rid == k, r, 0.0)
    return out


def _colsum(a):
    return jnp.sum(a, axis=0, keepdims=True)


def _matmul(a, b, *, tm, tn, name, emit_a=False):
    M, K = a.shape
    N = b.shape[1]

    def body(a_ref, b_ref, o_ref, *rest):
        am = a_ref[...].astype(_MXU)
        o_ref[...] = jnp.dot(am, b_ref[...].astype(_MXU), preferred_element_type=F32)
        if emit_a:
            rest[0][...] = am

    out_specs = [pl.BlockSpec((tm, tn), lambda i, j: (i, j))]
    out_shape = [jax.ShapeDtypeStruct((M, N), F32)]
    if emit_a:
        out_specs.append(pl.BlockSpec((tm, K), lambda i, j: (i, 0)))
        out_shape.append(jax.ShapeDtypeStruct((M, K), _MXU))
    res = pl.pallas_call(
        body, name=name, grid=(M // tm, N // tn),
        in_specs=[pl.BlockSpec((tm, K), lambda i, j: (i, 0)), pl.BlockSpec((K, tn), lambda i, j: (0, j))],
        out_specs=out_specs, out_shape=out_shape, compiler_params=_params(("arbitrary", "arbitrary")),
    )(a, b)
    return res if emit_a else res[0]


def _matmul_tn(a, g, *, tl, tn, name, emit_bf16=False):
    L, M = a.shape
    N = g.shape[1]
    last = L // tl - 1

    def body(a_ref, g_ref, o_ref, *rest):
        @pl.when(pl.program_id(1) == 0)
        def _():
            o_ref[...] = jnp.zeros_like(o_ref)

        o_ref[...] += _mm_tn(a_ref[...], g_ref[...])
        if emit_bf16:
            @pl.when(pl.program_id(1) == last)
            def _():
                rest[0][...] = o_ref[...].astype(BF16)

    spec = pl.BlockSpec((M, tn), lambda j, l: (0, j))
    res = pl.pallas_call(
        body, name=name, grid=(N // tn, L // tl),
        in_specs=[pl.BlockSpec((tl, M), lambda j, l: (l, 0)), pl.BlockSpec((tl, tn), lambda j, l: (l, j))],
        out_specs=[spec, spec] if emit_bf16 else [spec],
        out_shape=[jax.ShapeDtypeStruct((M, N), F32)] + ([jax.ShapeDtypeStruct((M, N), BF16)] if emit_bf16 else []),
        compiler_params=_params(("arbitrary", "arbitrary")),
    )(a, g)
    return res if emit_bf16 else res[0]


def _position():
    return lax.axis_index("x"), lax.axis_index("y"), lax.axis_index("c")


def _index(px, py, pc):
    return 4 * px + 2 * py + pc


def _flip(pos, k):
    x, y, c = pos
    return ((1 - x) if (k >> 2) & 1 else x, (1 - y) if (k >> 1) & 1 else y, (1 - c) if k & 1 else c)


def _when(cond, fn):
    if cond is True:
        fn()
    else:
        pl.when(cond)(fn)


def _remote(src, dst, send_sem, recv_sem, peer):
    return pltpu.make_async_remote_copy(src_ref=src, dst_ref=dst, send_sem=send_sem, recv_sem=recv_sem,
                                        device_id=peer, device_id_type=MESH)


class _Flow:
    def __init__(self, kind, operand, result, target_x=None):
        self.kind, self.operand, self.result, self.target_x = kind, operand, result, target_x


class _Hosted:
    def __init__(self, operands, out_shapes, flows, aliases=None):
        self.operands, self.out_shapes, self.flows = operands, out_shapes, flows
        self.aliases = aliases or {}

    def plan(self, ins, outs, send_sems, recv_sems, local_sems):
        me = _position()
        mi = _index(*me)
        sends, recvs, locals_ = [], [], []
        for row, f in enumerate(self.flows):
            src, dst = ins[f.operand], outs[f.result]
            for k in range(1, N_DEV):
                peer = _flip(me, k)
                sems = (send_sems.at[row, k - 1], recv_sems.at[row, k - 1])
                if f.kind == "exchange":
                    owner = _index(*peer) if f.target_x is None else 2 * peer[1] + peer[2]
                    cp = _remote(src.at[owner], dst.at[k - 1], *sems, peer)
                    to_peer = True if f.target_x is None else peer[0] == f.target_x
                    to_me = True if f.target_x is None else me[0] == f.target_x
                    sends.append((to_peer, cp))
                    recvs.append((to_me, cp))
                elif f.kind == "chip_exchange":
                    if k & 1:
                        continue
                    cp = _remote(src.at[peer[1]], dst.at[k // 2 - 1], *sems, peer)
                    sends.append((peer[0] == f.target_x, cp))
                    recvs.append((me[0] == f.target_x, cp))
                else:
                    sends.append((True, _remote(src, dst.at[mi], *sems, peer)))
                    recvs.append((True, _remote(src, dst.at[_index(*peer)], *sems, peer)))
            if f.kind == "gather":
                locals_.append(pltpu.make_async_copy(src, dst.at[mi], local_sems.at[row]))

        def start():
            for cp in locals_:
                cp.start()
            for cond, cp in sends:
                _when(cond, cp.start)

        def wait():
            for cond, cp in recvs:
                _when(cond, cp.wait_recv)
            for cond, cp in sends:
                _when(cond, cp.wait_send)
            for cp in locals_:
                cp.wait()

        return start, wait


def _call(body, comm, *, name, grid, in_specs, out_specs, out_shape, scratch_shapes, args, aliases=None):
    io_alias = dict(aliases or {})
    if comm is None:
        return pl.pallas_call(body, name=name, grid=grid, in_specs=in_specs, out_specs=out_specs, out_shape=out_shape,
                              scratch_shapes=scratch_shapes, input_output_aliases=io_alias,
                              compiler_params=_params(("arbitrary",)))(*args)
    n_in, n_out, n_scr = len(args), len(out_shape), len(scratch_shapes)
    c_in, c_out, rows = len(comm.operands), len(comm.out_shapes), len(comm.flows)

    def hosted(*refs):
        ins, refs = refs[:n_in], refs[n_in:]
        cins, refs = refs[:c_in], refs[c_in:]
        outs, refs = refs[:n_out], refs[n_out:]
        couts, refs = refs[:c_out], refs[c_out:]
        scr, (send_sems, recv_sems, local_sems) = refs[:n_scr], refs[n_scr:]
        start, wait = comm.plan(cins, couts, send_sems, recv_sems, local_sems)
        pl.when(pl.program_id(0) == 0)(start)
        body(*ins, *outs, *scr)
        pl.when(pl.program_id(0) == grid[0] - 1)(wait)

    for ci, co in comm.aliases.items():
        io_alias[n_in + ci] = n_out + co
    any_spec = pl.BlockSpec(memory_space=pl.ANY)
    sems = [pltpu.SemaphoreType.DMA((rows, N_DEV - 1)), pltpu.SemaphoreType.DMA((rows, N_DEV - 1)),
            pltpu.SemaphoreType.DMA((rows,))]
    return pl.pallas_call(
        hosted, name=name, grid=grid, in_specs=list(in_specs) + [any_spec] * c_in,
        out_specs=list(out_specs) + [any_spec] * c_out, out_shape=list(out_shape) + list(comm.out_shapes),
        scratch_shapes=list(scratch_shapes) + sems, input_output_aliases=io_alias,
        compiler_params=_params(("arbitrary",)))(*args, *comm.operands)


def _head_row(ref, width, rep):
    hid = lax.broadcasted_iota(jnp.int32, (1, width), 1) // rep
    row = jnp.zeros((1, width), F32)
    for h in range(N_HEADS):
        row = jnp.where(hid == h, ref[h], row)
    return row


def _ssd_recompute(first, p_ref, halo_ref, cw_ref, cb_ref, dtb_ref, alog_ref, e_ref, ext_scr, pre=None):
    ext_scr[0:8, :] = jnp.where(first, 0.0, halo_ref[:, S_XS:S_DT])
    if pre is not None:
        ext_scr[8:16, :] = p_ref[0:8, S_XS:S_DT]
    else:
        ext_scr[8:136, :] = p_ref[:, S_XS:S_DT]
        cw = cw_ref[...]
        pre = (cb_ref[0:1, :] + cw[3:4, :] * ext_scr[8:136, :] + cw[2:3, :] * ext_scr[7:135, :]
               + cw[1:2, :] * ext_scr[6:134, :] + cw[0:1, :] * ext_scr[5:133, :])
    sg = _sigmoid(pre)
    act = pre * sg
    lane = lax.broadcasted_iota(jnp.int32, (1, 128), 1)
    A = jnp.where(lane < N_HEADS, -jnp.exp(_head_row(alog_ref, 128, 1)), 0.0)
    raw = p_ref[:, S_DT:S_DT + 128] + _head_row(dtb_ref, 128, 1)
    dt = _softplus(raw)
    dA = dt * A
    row = lax.broadcasted_iota(jnp.int32, (128, 128), 0)
    col = lax.broadcasted_iota(jnp.int32, (128, 128), 1)
    tril = (row >= col).astype(BF16)
    acs = _mm_exact_l(tril, dA)
    last = acs[127:128, :]
    ds = jnp.exp(last - acs)
    eo = jnp.exp(acs)
    E = e_ref[...]
    ex = _mm_exact_r(jnp.concatenate([dt, ds, eo], axis=0), E)
    dt_e, ds_e, eo_e = ex[0:128], ex[128:256], ex[256:384]
    xs_c = act[:, 0:1024]
    X = xs_c * dt_e
    return dict(pre=pre, sg=sg, xs_c=xs_c, Bc=act[:, 1024:1280], Cc=act[:, 1280:1536], A=A, raw=raw, dt=dt,
                acs=acs, acsT=acs.T, eo_e=eo_e, ds_e=ds_e, dt_e=dt_e, cd_e=eo_e[127:128, :],
                X=X, Xd=X * ds_e, row=row, col=col)


def _split_halves(t):
    lo = _lo_half(CHUNK)
    return jnp.concatenate([jnp.where(lo, t, 0.0), jnp.where(lo, 0.0, t)], axis=0)


def _ssd_core(R, hprev):
    causal = R["row"] >= R["col"]
    acs, acsT, X = R["acs"], R["acsT"], R["X"]
    ydiag, yoff, snew = [], [], []
    for g in range(SSD_GROUPS):
        Bg = R["Bc"][:, g * 128:(g + 1) * 128]
        Cg = R["Cc"][:, g * 128:(g + 1) * 128]
        cols = slice(g * 512, (g + 1) * 512)
        CB = _mm_nt(Cg, Bg)
        snew.append(_mm_tn(Bg, R["Xd"][:, cols]))
        yoff.append(_mm(Cg, hprev[:, cols]))
        for j in range(4):
            h0 = g * 8 + 2 * j
            ms = [CB * jnp.exp(jnp.where(causal, acs[:, h:h + 1] - acsT[h:h + 1, :], NEG)) for h in (h0, h0 + 1)]
            ydiag.append(_mm(jnp.concatenate(ms, axis=1), _split_halves(X[:, h0 * HEAD_DIM:h0 * HEAD_DIM + 128])))
    Y = jnp.concatenate(ydiag, axis=1) + jnp.concatenate(yoff, axis=1) * R["eo_e"]
    return Y, jnp.concatenate(snew, axis=1)


def _ssd_forward(proj_ssd, conv_w8, conv_b8, dtb8, alog8, dskip_e, norm_w, E, comm=None):
    L = proj_ssd.shape[0]
    nc = L // CHUNK

    def body(p_ref, halo_ref, cw_ref, cb_ref, dtb_ref, alog_ref, dsk_ref, nw_ref, e_ref,
             y_ref, ypre_ref, hprev_ref, pre_ref, h_scr, ext_scr):
        c = pl.program_id(0)
        first = c == 0

        @pl.when(first)
        def _():
            h_scr[...] = jnp.zeros_like(h_scr)

        R = _ssd_recompute(first, p_ref, halo_ref, cw_ref, cb_ref, dtb_ref, alog_ref, e_ref, ext_scr)
        hprev = h_scr[...]
        hprev_ref[...] = hprev
        pre_ref[...] = R["pre"]
        Y, snew = _ssd_core(R, hprev)
        h_scr[...] = hprev * R["cd_e"] + snew
        Y = Y + _head_row(dsk_ref, D_SSD, HEAD_DIM) * R["xs_c"]
        ypre_ref[...] = Y
        z = p_ref[:, S_Z:S_Z + 1024]
        yf = Y * (z * _sigmoid(z))
        outs = []
        for g in range(SSD_GROUPS):
            yg = yf[:, g * 512:(g + 1) * 512]
            r = lax.rsqrt(jnp.mean(yg * yg, axis=-1, keepdims=True) + RMS_EPS)
            outs.append(yg * r)
        y_ref[...] = (jnp.concatenate(outs, axis=1) * nw_ref[0:1, :]).astype(y_ref.dtype)

    const = lambda shape: pl.BlockSpec(shape, lambda c: (0, 0))
    smem = pl.BlockSpec(memory_space=pltpu.SMEM)
    return _call(
        body, comm, name="ssd_fwd", grid=(nc,),
        in_specs=[pl.BlockSpec((CHUNK, S_W), lambda c: (c, 0)),
                  pl.BlockSpec((8, S_W), lambda c: (jnp.maximum(c * 16 - 1, 0), 0)),
                  const((4, D_XBC)), const((1, D_XBC)), smem, smem, smem, const((1, 1024)), const((128, 1024))],
        out_specs=[pl.BlockSpec((CHUNK, D_SSD), lambda c: (c, 0)), pl.BlockSpec((CHUNK, D_SSD), lambda c: (c, 0)),
                   pl.BlockSpec((128, 1024), lambda c: (c, 0)), pl.BlockSpec((CHUNK, D_XBC), lambda c: (c, 0))],
        out_shape=[jax.ShapeDtypeStruct((L, D_SSD + D_ATT), _MXU), jax.ShapeDtypeStruct((L, D_SSD), F32),
                   jax.ShapeDtypeStruct((nc * 128, 1024), F32), jax.ShapeDtypeStruct((L, D_XBC), F32)],
        scratch_shapes=[pltpu.VMEM((128, 1024), F32), pltpu.VMEM((136, D_XBC), F32)],
        args=(proj_ssd, proj_ssd, conv_w8, conv_b8, dtb8, alog8, dskip_e, norm_w, E))


def _ssd_backward(proj_ssd, hprev_all, ypre, pre, dy, conv_w8, conv_b8, dtb8, alog8, dskip_e, norm_w, E, ET, comm=None):
    L = proj_ssd.shape[0]
    nc = L // CHUNK

    def body(p_ref, halo_ref, hprev_ref, ypre_ref, pre_ref, dy_ref, cw_ref, cb_ref, dtb_ref, alog_ref, dsk_ref, nw_ref, e_ref,
             et_ref, dp_ref, acc_cw_ref, acc_w_ref, acc_s_ref, dh_scr, ext_scr, ext2_scr, nxt_scr):
        i = pl.program_id(0)
        c = nc - 1 - i
        first = c == 0

        @pl.when(i == 0)
        def _():
            dh_scr[...] = jnp.zeros_like(dh_scr)
            nxt_scr[...] = jnp.zeros_like(nxt_scr)
            acc_cw_ref[...] = jnp.zeros_like(acc_cw_ref)
            acc_w_ref[...] = jnp.zeros_like(acc_w_ref)
            acc_s_ref[...] = jnp.zeros_like(acc_s_ref)

        R = _ssd_recompute(first, p_ref, halo_ref, cw_ref, cb_ref, dtb_ref, alog_ref, e_ref, ext_scr, pre_ref[...])
        hprev = hprev_ref[...]
        xs_c, X, Xd = R["xs_c"], R["X"], R["Xd"]
        acs, acsT = R["acs"], R["acsT"]
        ET = et_ref[...]
        dsk = _head_row(dsk_ref, D_SSD, HEAD_DIM)
        Y = ypre_ref[...]

        z = p_ref[:, S_Z:S_Z + 1024]
        sz = _sigmoid(z)
        silz = z * sz
        yf = Y * silz
        dyv = dy_ref[...]
        nw = nw_ref[0:1, :]
        dyf_parts, dnw_parts = [], []
        for g in range(SSD_GROUPS):
            cols = slice(g * 512, (g + 1) * 512)
            yg = yf[:, cols]
            r = lax.rsqrt(jnp.mean(yg * yg, axis=-1, keepdims=True) + RMS_EPS)
            yn = yg * r
            dyn = dyv[:, cols] * nw[:, cols]
            dnw_parts.append(_colsum(dyv[:, cols] * yn))
            dyf_parts.append(r * (dyn - yn * jnp.mean(dyn * yn, axis=-1, keepdims=True)))
        dyf = jnp.concatenate(dyf_parts, axis=1)
        dY = dyf * silz
        dz = dyf * Y * (sz * (1.0 + z * (1.0 - sz)))

        dhn = dh_scr[...]
        dYo = dY * R["eo_e"]
        causal = R["row"] >= R["col"]
        dacs = jnp.zeros((128, 128), F32)
        dacs_t = jnp.zeros((128, 128), F32)
        dxdiag, dxd, dhprev, dBs, dCs, yoff = [], [], [], [], [], []
        for g in range(SSD_GROUPS):
            Bg = R["Bc"][:, g * 128:(g + 1) * 128]
            Cg = R["Cc"][:, g * 128:(g + 1) * 128]
            cols = slice(g * 512, (g + 1) * 512)
            CB = _mm_nt(Cg, Bg)
            dCB = jnp.zeros((128, 128), F32)
            for j in range(4):
                h0 = g * 8 + 2 * j
                pc = slice(h0 * HEAD_DIM, h0 * HEAD_DIM + 128)
                dYst = _split_halves(dY[:, pc])
                dMst = _mm_nt(dYst, X[:, pc])
                mts = []
                for a, h in enumerate((h0, h0 + 1)):
                    acol = acs[:, h:h + 1]
                    arow = acsT[h:h + 1, :]
                    Lm = jnp.exp(jnp.where(causal, acol - arow, NEG))
                    M = CB * Lm
                    dM = dMst[a * 128:(a + 1) * 128]
                    dCB = dCB + dM * Lm
                    G = dM * M
                    dacs = dacs + jnp.where(R["col"] == h, jnp.sum(G, axis=1, keepdims=True), 0.0)
                    dacs_t = dacs_t + jnp.where(R["row"] == h, jnp.sum(G, axis=0, keepdims=True), 0.0)
                    mts.append(M.T)
                dxdiag.append(_mm(jnp.concatenate(mts, axis=1), dYst))
            dS = dhn[:, cols]
            dxd.append(_mm(Bg, dS))
            yoff.append(_mm(Cg, hprev[:, cols]))
            dhprev.append(_mm_tn(Cg, dYo[:, cols]))
            dCs.append(_mm_nt(dYo[:, cols], hprev[:, cols]) + _mm(dCB, Bg))
            dBs.append(_mm_tn(dCB, Cg) + _mm_nt(Xd[:, cols], dS))
        Yoff = jnp.concatenate(yoff, axis=1) * R["eo_e"]
        dXd = jnp.concatenate(dxd, axis=1)
        dX = jnp.concatenate(dxdiag, axis=1) + dXd * R["ds_e"]
        t_state = dXd * Xd
        dacs = dacs + _mm_exact_r(dY * Yoff - t_state, ET) - dacs_t.T
        v_last = _colsum(t_state + dhn * hprev * R["cd_e"])
        dlast = _mm_exact_r(jnp.broadcast_to(v_last, (8, 1024)), ET)[0:1, :]
        dacs = dacs + jnp.where(R["row"] == 127, dlast, 0.0)
        triu = (R["col"] >= R["row"]).astype(BF16)
        da = _mm_exact_l(triu, dacs)
        ddt = da * R["A"] + _mm(dX * xs_c, ET)
        ddt_raw = ddt * _sigmoid(R["raw"])
        dxs_c = dX * R["dt_e"] + dY * dsk
        dh_scr[...] = jnp.concatenate(dhprev, axis=1) + dhn * R["cd_e"]

        dact = jnp.concatenate([dxs_c] + dBs + dCs, axis=1)
        pre, sg = R["pre"], R["sg"]
        dpre = dact * (sg * (1.0 + pre * (1.0 - sg)))
        ext2_scr[0:8, :] = dpre[120:128, :]
        ext2_scr[8:16, :] = nxt_scr[...]
        nxt_scr[...] = dpre[0:8, :]
        cw = cw_ref[...]
        u_b, dpre_b = p_ref[:, S_XS:S_DT].astype(_MXU), dpre.astype(_MXU)
        dxbc = cw[3:4, :] * dpre
        taps = [_colsum(dpre * p_ref[:, S_XS:S_DT])]
        for s in (1, 2, 3):
            down = (R["row"] - R["col"] == s).astype(_MXU)
            up = (R["col"] - R["row"] == s).astype(_MXU)
            u_s = jnp.concatenate([ext_scr[8 - s:16 - s, :],
                                   jnp.dot(down, u_b, preferred_element_type=F32)[8:128]], axis=0)
            d_s = jnp.concatenate([jnp.dot(up, dpre_b, preferred_element_type=F32)[0:120],
                                   ext2_scr[s:8 + s, :]], axis=0)
            dxbc = dxbc + cw[3 - s:4 - s, :] * d_s
            taps.append(_colsum(dpre * u_s))
        acc_cw_ref[...] += _rows8(taps[::-1] + [_colsum(dpre)])
        acc_w_ref[...] += _rows8([jnp.concatenate(dnw_parts, axis=1), _colsum(dY * xs_c)])
        acc_s_ref[...] += _rows8([_colsum(ddt_raw), _colsum(da * R["dt"])])

        lane = lax.broadcasted_iota(jnp.int32, (128, 128), 1)
        dp_ref[:, S_Z:S_Z + 1024] = dz
        dp_ref[:, S_XS:S_DT] = dxbc
        dp_ref[:, S_DT:S_DT + 128] = jnp.where(lane < N_HEADS, ddt_raw, 0.0)
        dp_ref[:, S_DT + 128:S_W] = jnp.zeros((128, 128), F32)

        @pl.when(i == nc - 1)
        def _():
            acc = acc_s_ref[...]
            dskip = _mm_exact_r(acc_w_ref[...], ET)[1:2, :]
            acc_s_ref[...] = _rows8([acc[0:1, :], acc[1:2, :] * R["A"], dskip])

    const = lambda shape: pl.BlockSpec(shape, lambda i: (0, 0))
    smem = pl.BlockSpec(memory_space=pltpu.SMEM)
    rev = lambda i: (nc - 1 - i, 0)
    return _call(
        body, comm, name="ssd_bwd", grid=(nc,),
        in_specs=[pl.BlockSpec((CHUNK, S_W), rev),
                  pl.BlockSpec((8, S_W), lambda i: (jnp.maximum((nc - 1 - i) * 16 - 1, 0), 0)),
                  pl.BlockSpec((128, 1024), rev),
                  pl.BlockSpec((CHUNK, D_SSD), rev),
                  pl.BlockSpec((CHUNK, D_XBC), rev),
                  pl.BlockSpec((CHUNK, D_SSD), rev),
                  const((4, D_XBC)), const((1, D_XBC)), smem, smem, smem, const((1, 1024)),
                  const((128, 1024)), const((1024, 128))],
        out_specs=[pl.BlockSpec((CHUNK, S_W), rev), const((8, D_XBC)), const((8, 1024)), const((8, 128))],
        out_shape=[jax.ShapeDtypeStruct((L, S_W), F32), jax.ShapeDtypeStruct((8, D_XBC), F32),
                   jax.ShapeDtypeStruct((8, 1024), F32), jax.ShapeDtypeStruct((8, 128), F32)],
        scratch_shapes=[pltpu.VMEM((128, 1024), F32), pltpu.VMEM((16, D_XBC), F32),
                        pltpu.VMEM((16, D_XBC), F32), pltpu.VMEM((8, D_XBC), F32)],
        args=(proj_ssd, proj_ssd, hprev_all, ypre, pre, dy, conv_w8, conv_b8, dtb8, alog8, dskip_e, norm_w, E, ET))


def _rope(t, tab):
    cos, sa, sb = tab[:, 0:128], tab[:, 128:256], tab[:, 256:384]
    outs = []
    for i in range(t.shape[1] // 128):
        tg = t[:, i * 128:(i + 1) * 128]
        outs.append(tg * cos + pltpu.roll(tg, 8, 1) * sa + pltpu.roll(tg, 120, 1) * sb)
    return jnp.concatenate(outs, axis=1)


def _rope_transposed(d, tab):
    cos, sa, sb = tab[:, 0:128], tab[:, 128:256], tab[:, 256:384]
    outs = []
    for i in range(d.shape[1] // 128):
        dg = d[:, i * 128:(i + 1) * 128]
        outs.append(dg * cos + pltpu.roll(dg * sa, 120, 1) + pltpu.roll(dg * sb, 8, 1))
    return jnp.concatenate(outs, axis=1)


def _lo_half(rows):
    return lax.broadcasted_iota(jnp.int32, (rows, 128), 1) < HEAD_DIM


def _kv_both(t, j):
    p, b = j // 2, j % 2
    lo = _lo_half(2 * CHUNK)
    nat = jnp.where(lo if b == 0 else jnp.logical_not(lo), t[:, p * 128:(p + 1) * 128], 0.0)
    return nat + pltpu.roll(nat, HEAD_DIM, 1)


def _stack_heads(t, j):
    lo = _lo_half(CHUNK)
    hi = jnp.logical_not(lo)
    a, b = t[:, 2 * j * 128:(2 * j + 1) * 128], t[:, (2 * j + 1) * 128:(2 * j + 2) * 128]
    return jnp.concatenate([jnp.where(lo, a, 0.0), jnp.where(hi, a, 0.0),
                            jnp.where(lo, b, 0.0), jnp.where(hi, b, 0.0)], axis=0)


def _unstack_heads(s):
    lo = _lo_half(CHUNK)
    return jnp.concatenate([jnp.where(lo, s[0:128], s[128:256]), jnp.where(lo, s[256:384], s[384:512])], axis=1)


def _fold_kv(r, j):
    lo = _lo_half(2 * CHUNK)
    return jnp.where(lo if j % 2 == 0 else jnp.logical_not(lo), r + pltpu.roll(r, HEAD_DIM, 1), 0.0)


def _sink_row(sink_ref, j):
    hid = lax.broadcasted_iota(jnp.int32, (1, 4 * CHUNK), 1) // CHUNK
    row = jnp.zeros((1, 4 * CHUNK), F32)
    for hh in range(4):
        row = jnp.where(hid == hh, sink_ref[4 * j + hh], row)
    return row


def _band_mask(blk):
    si = lax.broadcasted_iota(jnp.int32, (2 * CHUNK, 4 * CHUNK), 0)
    qi = lax.broadcasted_iota(jnp.int32, (2 * CHUNK, 4 * CHUNK), 1) % CHUNK
    return (si > qi) & (si <= qi + CHUNK) & ((blk > 0) | (si >= CHUNK))


def _softmax_sink(s, valid, sink):
    s = jnp.where(valid, s, NEG)
    mx = jnp.maximum(jnp.max(s, axis=0, keepdims=True), sink)
    p = jnp.exp(s - mx)
    esink = jnp.exp(sink - mx)
    inv = 1.0 / (jnp.sum(p, axis=0, keepdims=True) + esink)
    return p * inv, esink * inv


def _swa_forward(proj_att, tabs, sinks, y):
    L = proj_att.shape[0]
    nb = L // CHUNK

    def body(sink_ref, p_ref, prev_ref, tab_ref, ptab_ref, y_in_ref, y_ref):
        n = pl.program_id(0)
        tab = tab_ref[...]
        qr = _rope(p_ref[:, A_Q:A_Q + 1024], tab)
        k_cur = _rope(p_ref[:, A_K:A_K + 256], tab)
        k_prev = _rope(prev_ref[:, 0:256], ptab_ref[...])
        kk = jnp.concatenate([k_prev, k_cur], axis=0)
        vv = jnp.concatenate([prev_ref[:, 256:512], p_ref[:, A_V:A_V + 256]], axis=0)
        valid = _band_mask(n)
        outs = []
        for j in range(KV_HEADS):
            s = _mm_nt(_kv_both(kk, j), _stack_heads(qr, j)) * ATT_SCALE
            P, _ = _softmax_sink(s, valid, _sink_row(sink_ref, j))
            outs.append(_unstack_heads(_mm_tn(P, _kv_both(vv, j))))
        g = p_ref[:, A_G:A_G + 1024]
        y_ref[...] = (jnp.concatenate(outs, axis=1) * (g * _sigmoid(g))).astype(y_ref.dtype)

    return pl.pallas_call(
        body, name="swa_fwd", grid=(nb,),
        in_specs=[pl.BlockSpec(memory_space=pltpu.SMEM),
                  pl.BlockSpec((CHUNK, A_W), lambda n: (n, 0)),
                  pl.BlockSpec((CHUNK, 512), lambda n: (jnp.maximum(n - 1, 0), 2)),
                  pl.BlockSpec((CHUNK, 384), lambda n: (n, 0)),
                  pl.BlockSpec((CHUNK, 384), lambda n: (jnp.maximum(n - 1, 0), 0)),
                  pl.BlockSpec(memory_space=pl.ANY)],
        out_specs=pl.BlockSpec((CHUNK, D_ATT), lambda n: (n, 1)),
        out_shape=jax.ShapeDtypeStruct(y.shape, y.dtype),
        input_output_aliases={5: 0},
        compiler_params=_params(("arbitrary",)),
    )(sinks, proj_att, proj_att, tabs, tabs, y)


def _swa_backward(proj_att, tabs, sinks, dy, comm=None):
    L = proj_att.shape[0]
    nb = L // CHUNK

    def body(sink_ref, p_ref, prev_ref, tab_ref, ptab_ref, dy_ref, dp_ref, dsink_ref, carry_k, carry_v):
        i = pl.program_id(0)
        n = nb - 1 - i

        @pl.when(i == 0)
        def _():
            carry_k[...] = jnp.zeros_like(carry_k)
            carry_v[...] = jnp.zeros_like(carry_v)
            dsink_ref[...] = jnp.zeros_like(dsink_ref)

        tab = tab_ref[...]
        qr = _rope(p_ref[:, A_Q:A_Q + 1024], tab)
        k_cur = _rope(p_ref[:, A_K:A_K + 256], tab)
        k_prev = _rope(prev_ref[:, 0:256], ptab_ref[...])
        kk = jnp.concatenate([k_prev, k_cur], axis=0)
        vv = jnp.concatenate([prev_ref[:, 256:512], p_ref[:, A_V:A_V + 256]], axis=0)
        valid = _band_mask(n)
        g = p_ref[:, A_G:A_G + 1024]
        sgm = _sigmoid(g)
        dyv = dy_ref[...]
        do_all = dyv * (g * sgm)
        lane8 = lax.broadcasted_iota(jnp.int32, (8, 128), 1)
        hid = lax.broadcasted_iota(jnp.int32, (1, 4 * CHUNK), 1) // CHUNK
        o_parts, dq_parts = [], []
        dk_nat = [jnp.zeros((2 * CHUNK, 128), F32) for _ in range(2)]
        dv_nat = [jnp.zeros((2 * CHUNK, 128), F32) for _ in range(2)]
        dsink = jnp.zeros((8, 128), F32)
        for j in range(KV_HEADS):
            qs = _stack_heads(qr, j)
            kkb, vvb = _kv_both(kk, j), _kv_both(vv, j)
            s = _mm_nt(kkb, qs) * ATT_SCALE
            P, psink = _softmax_sink(s, valid, _sink_row(sink_ref, j))
            o_parts.append(_unstack_heads(_mm_tn(P, vvb)))
            do_s = _stack_heads(do_all, j)
            dP = _mm_nt(vvb, do_s)
            D = jnp.sum(P * dP, axis=0, keepdims=True)
            dS = P * (dP - D)
            sd = psink * D
            for hh in range(4):
                dsink = dsink + jnp.where(lane8 == 4 * j + hh, -jnp.sum(jnp.where(hid == hh, sd, 0.0)), 0.0)
            dq_parts.append(_unstack_heads(_mm_tn(dS, kkb)) * ATT_SCALE)
            dk_nat[j // 2] = dk_nat[j // 2] + _fold_kv(_mm(dS, qs), j) * ATT_SCALE
            dv_nat[j // 2] = dv_nat[j // 2] + _fold_kv(_mm(P, do_s), j)
        o = jnp.concatenate(o_parts, axis=1)
        dkk = jnp.concatenate(dk_nat, axis=1)
        dvv = jnp.concatenate(dv_nat, axis=1)
        dp_ref[:, A_Q:A_Q + 1024] = _rope_transposed(jnp.concatenate(dq_parts, axis=1), tab)
        dp_ref[:, A_K:A_K + 256] = _rope_transposed(dkk[CHUNK:2 * CHUNK] + carry_k[...], tab)
        dp_ref[:, A_V:A_V + 256] = dvv[CHUNK:2 * CHUNK] + carry_v[...]
        dp_ref[:, A_G:A_G + 1024] = dyv * o * (sgm * (1.0 + g * (1.0 - sgm)))
        carry_k[...] = dkk[0:CHUNK]
        carry_v[...] = dvv[0:CHUNK]
        dsink_ref[...] += dsink

    rev = lambda i: (nb - 1 - i, 0)
    prev = lambda i: jnp.maximum(nb - 2 - i, 0)
    return _call(
        body, comm, name="swa_bwd", grid=(nb,),
        in_specs=[pl.BlockSpec(memory_space=pltpu.SMEM),
                  pl.BlockSpec((CHUNK, A_W), rev),
                  pl.BlockSpec((CHUNK, 512), lambda i: (prev(i), 2)),
                  pl.BlockSpec((CHUNK, 384), rev),
                  pl.BlockSpec((CHUNK, 384), lambda i: (prev(i), 0)),
                  pl.BlockSpec((CHUNK, D_ATT), lambda i: (nb - 1 - i, 1))],
        out_specs=[pl.BlockSpec((CHUNK, A_W), rev), pl.BlockSpec((8, 128), lambda i: (0, 0))],
        out_shape=[jax.ShapeDtypeStruct((L, A_W), F32), jax.ShapeDtypeStruct((8, 128), F32)],
        scratch_shapes=[pltpu.VMEM((CHUNK, 256), F32), pltpu.VMEM((CHUNK, 256), F32)],
        args=(sinks, proj_att, proj_att, tabs, tabs, dy))


def _head(y, x, target, w_out, ln_g8, ln_b8, *, tm):
    L = x.shape[0]
    nsteps = L // tm

    def body(y_ref, x_ref, t_ref, wo_ref, g_ref, b_ref, dr_ref, dy_ref, acc_ref):
        i = pl.program_id(0)

        @pl.when(i == 0)
        def _():
            acc_ref[...] = jnp.zeros_like(acc_ref)

        r = ALPHA * x_ref[...] + _mm(y_ref[...], wo_ref[...])
        mu = jnp.mean(r, axis=-1, keepdims=True)
        d = r - mu
        rstd = lax.rsqrt(jnp.mean(d * d, axis=-1, keepdims=True) + LN_EPS)
        xh = d * rstd
        gam = g_ref[0:1, :]
        e = xh * gam + b_ref[0:1, :] - t_ref[...]
        dout = e * (1.0 / D_MODEL)
        dxh = dout * gam
        dr = rstd * (dxh - jnp.mean(dxh, axis=-1, keepdims=True)
                     - xh * jnp.mean(dxh * xh, axis=-1, keepdims=True))
        dr_ref[...] = dr
        dy_ref[...] = _mm_nt(dr, wo_ref[...])
        acc_ref[...] += _rows8([_colsum(dout * xh), _colsum(dout), _colsum(e * e) * (0.5 / D_MODEL)])

        @pl.when(i == nsteps - 1)
        def _():
            acc = acc_ref[...]
            tot = jnp.sum(acc[2:3, :])
            rid = lax.broadcasted_iota(jnp.int32, (8, 1024), 0)
            acc_ref[...] = jnp.where(rid == 3, tot, acc)

    const = lambda shape: pl.BlockSpec(shape, lambda i: (0, 0))
    row = lambda w: pl.BlockSpec((tm, w), lambda i: (i, 0))
    return pl.pallas_call(
        body, name="head", grid=(nsteps,),
        in_specs=[row(2048), row(1024), row(1024), const((2048, 1024)), const((1, 1024)), const((1, 1024))],
        out_specs=[row(1024), row(2048), const((8, 1024))],
        out_shape=[jax.ShapeDtypeStruct((L, D_MODEL), F32), jax.ShapeDtypeStruct((L, 2048), F32),
                   jax.ShapeDtypeStruct((8, 1024), F32)],
        compiler_params=_params(("arbitrary",)),
    )(y, x, target, w_out, ln_g8, ln_b8)


def _all_gather(shards):
    n = len(shards)
    any_spec = pl.BlockSpec(memory_space=pl.ANY)

    def body(*refs):
        ins, outs = refs[:n], refs[n:2 * n]
        send_sems, recv_sems, local_sems = refs[2 * n:]
        x, y, c = _position()
        me, sibling = (x, y, c), (x, y, 1 - c)
        chips = [(1 - x, y), (x, 1 - y), (1 - x, 1 - y)]

        def copy(a, k, block, to, src=None):
            slot = outs[a].at[_index(*block)]
            return pltpu.make_async_remote_copy(
                src_ref=slot if src is None else src, dst_ref=slot,
                send_sem=send_sems.at[a, k], recv_sem=recv_sems.at[a, k],
                device_id=to, device_id_type=MESH)

        mine = [pltpu.make_async_copy(ins[a], outs[a].at[_index(*me)], local_sems.at[a]) for a in range(n)]
        for cp in mine:
            cp.start()
        first = []
        for a in range(n):
            first.append(copy(a, 0, me, sibling, src=ins[a]))
            first += [copy(a, 1 + j, me, (*chip, c), src=ins[a]) for j, chip in enumerate(chips)]
        for cp in first:
            cp.start()
        passed = []
        for j, chip in enumerate(chips):
            for a in range(n):
                copy(a, 1 + j, (*chip, c), me).wait_recv()
                fwd = copy(a, 4 + j, (*chip, c), sibling)
                fwd.start()
                passed.append(fwd)
        for a in range(n):
            copy(a, 0, sibling, me).wait_recv()
            for j, chip in enumerate(chips):
                copy(a, 4 + j, (*chip, 1 - c), me).wait_recv()
        for cp in first + passed:
            cp.wait_send()
        for cp in mine:
            cp.wait()

    return pl.pallas_call(
        body, name="weight_all_gather",
        in_specs=[any_spec] * n, out_specs=[any_spec] * n,
        out_shape=[jax.ShapeDtypeStruct((N_DEV,) + s.shape, s.dtype) for s in shards],
        scratch_shapes=[pltpu.SemaphoreType.DMA((n, 7)), pltpu.SemaphoreType.DMA((n, 7)),
                        pltpu.SemaphoreType.DMA((n,))],
    )(*shards)


def _input_gradient(d_ssd, d_att, w_ssd, w_att, dr, *, tm, comm=None):
    L = dr.shape[0]

    def body(ds_ref, da_ref, ws_ref, wa_ref, dr_ref, o_ref):
        o_ref[...] = ALPHA * dr_ref[...] + _mm_nt(ds_ref[...], ws_ref[...]) + _mm_nt(da_ref[...], wa_ref[...])

    row = lambda w: pl.BlockSpec((tm, w), lambda i: (i, 0))
    const = lambda shape: pl.BlockSpec(shape, lambda i: (0, 0))
    return _call(body, comm, name="dx", grid=(L // tm,),
                 in_specs=[row(S_W), row(A_W), const((D_MODEL, S_W)), const((D_MODEL, A_W)), row(D_MODEL)],
                 out_specs=[row(D_MODEL)], out_shape=[jax.ShapeDtypeStruct((L, D_MODEL), F32)],
                 scratch_shapes=[], args=(d_ssd, d_att, w_ssd, w_att, dr))


SHARD_COLS = D_IN_PROJ // N_DEV
SPLIT = N_SSD_REAL - 4 * SHARD_COLS
RELAYOUT_ROWS = 256


def _unpack_w_in(w_all):
    def body(g_ref, ws_ref, wa_ref):
        for j in range(4):
            ws_ref[:, SHARD_COLS * j:SHARD_COLS * (j + 1)] = g_ref[j]
        ws_ref[:, 4 * SHARD_COLS:N_SSD_REAL] = g_ref[4, :, 0:SPLIT]
        ws_ref[:, N_SSD_REAL:S_W] = jnp.zeros((RELAYOUT_ROWS, S_W - N_SSD_REAL), ws_ref.dtype)
        wa_ref[:, 0:SHARD_COLS - SPLIT] = g_ref[4, :, SPLIT:SHARD_COLS]
        for j in range(5, N_DEV):
            lo = SHARD_COLS * (j - 4) - SPLIT
            wa_ref[:, lo:lo + SHARD_COLS] = g_ref[j]

    return pl.pallas_call(
        body, name="unpack_w_in", grid=(D_MODEL // RELAYOUT_ROWS,),
        in_specs=[pl.BlockSpec((N_DEV, RELAYOUT_ROWS, SHARD_COLS), lambda i: (0, i, 0))],
        out_specs=[pl.BlockSpec((RELAYOUT_ROWS, S_W), lambda i: (i, 0)), pl.BlockSpec((RELAYOUT_ROWS, A_W), lambda i: (i, 0))],
        out_shape=[jax.ShapeDtypeStruct((D_MODEL, S_W), w_all.dtype), jax.ShapeDtypeStruct((D_MODEL, A_W), w_all.dtype)],
        compiler_params=_params(("arbitrary",)),
    )(w_all)


def _pack_dw_in(me1, dw_ssd, dw_att, half):
    def body(me_ref, *refs):
        if half == 0:
            ds_ref, p_ref, own_ref = refs
            me = me_ref[0]

            @pl.when(me >= 4)
            def _():
                own_ref[...] = jnp.zeros_like(own_ref)
        else:
            ds_ref, da_ref, p_ref = refs

        for j in range(4):
            if half == 0:
                pieces = [(0, ds_ref[:, SHARD_COLS * j:SHARD_COLS * (j + 1)])]
            elif j == 0:
                pieces = [(0, ds_ref[:, 4 * SHARD_COLS:N_SSD_REAL]), (SPLIT, da_ref[:, 0:SHARD_COLS - SPLIT])]
            else:
                lo = SHARD_COLS * j - SPLIT
                pieces = [(0, da_ref[:, lo:lo + SHARD_COLS])]
            for off, blk in pieces:
                p_ref[j, :, off:off + blk.shape[1]] = blk.astype(p_ref.dtype)
                if half == 0:
                    @pl.when(me == j)
                    def _(off=off, blk=blk):
                        own_ref[:, off:off + blk.shape[1]] = blk

    ins = [dw_ssd] if half == 0 else [dw_ssd, dw_att]
    row = lambda a: pl.BlockSpec((RELAYOUT_ROWS, a.shape[1]), lambda i: (i, 0))
    out_specs = [pl.BlockSpec((4, RELAYOUT_ROWS, SHARD_COLS), lambda i: (0, i, 0))]
    out_shape = [jax.ShapeDtypeStruct((4, D_MODEL, SHARD_COLS), BF16 if half == 0 else F32)]
    if half == 0:
        out_specs.append(pl.BlockSpec((RELAYOUT_ROWS, SHARD_COLS), lambda i: (i, 0)))
        out_shape.append(jax.ShapeDtypeStruct((D_MODEL, SHARD_COLS), F32))
    return pl.pallas_call(
        body, name="pack_dw_in_%d" % half, grid=(D_MODEL // RELAYOUT_ROWS,),
        in_specs=[pl.BlockSpec(memory_space=pltpu.SMEM)] + [row(a) for a in ins],
        out_specs=out_specs, out_shape=out_shape, compiler_params=_params(("arbitrary",)),
    )(me1, *ins)


def _pair_swap(stack):
    def body(in_ref, out_ref, send_sems, recv_sems):
        x, y, c = _position()
        cps = [_remote(in_ref.at[2 * oy + (1 - c)], out_ref.at[oy], send_sems.at[oy], recv_sems.at[oy], (x, y, 1 - c))
               for oy in range(2)]
        for cp in cps:
            cp.start()
        for cp in cps:
            cp.wait_recv()
        for cp in cps:
            cp.wait_send()

    any_spec = pl.BlockSpec(memory_space=pl.ANY)
    return pl.pallas_call(
        body, name="pair_swap", in_specs=[any_spec], out_specs=any_spec,
        out_shape=jax.ShapeDtypeStruct((2,) + stack.shape[1:], stack.dtype),
        scratch_shapes=[pltpu.SemaphoreType.DMA((2,)), pltpu.SemaphoreType.DMA((2,))],
    )(stack)


def _pair_sum(pos3, stack, swapped, own_lo):
    def body(pos_ref, a_ref, b_ref, lo_ref, chip_ref, own_ref):
        oy = pl.program_id(1)
        t = a_ref[0] + b_ref[0]
        chip_ref[0] = t.astype(chip_ref.dtype)

        @pl.when((pos_ref[0] == 0) & (oy == 0))
        def _():
            own_ref[...] = lo_ref[...]

        @pl.when((pos_ref[0] == 1) & (oy == pos_ref[1]))
        def _():
            own_ref[...] = t

    blk = (1, RELAYOUT_ROWS, SHARD_COLS)
    flat = pl.BlockSpec((RELAYOUT_ROWS, SHARD_COLS), lambda i, oy, pos: (i, 0))
    return pl.pallas_call(
        body, name="pair_sum",
        grid_spec=pltpu.PrefetchScalarGridSpec(
            num_scalar_prefetch=1, grid=(D_MODEL // RELAYOUT_ROWS, 2),
            in_specs=[pl.BlockSpec(blk, lambda i, oy, pos: (2 * oy + pos[2], i, 0)),
                      pl.BlockSpec(blk, lambda i, oy, pos: (oy, i, 0)), flat],
            out_specs=[pl.BlockSpec(blk, lambda i, oy, pos: (oy, i, 0)), flat]),
        out_shape=[jax.ShapeDtypeStruct((2, D_MODEL, SHARD_COLS), BF16), jax.ShapeDtypeStruct((D_MODEL, SHARD_COLS), F32)],
        compiler_params=_params(("arbitrary", "arbitrary")),
    )(pos3, stack, swapped, own_lo)


def _adamw_math(w, g, m, v):
    m = ADAM_B1 * m + (1.0 - ADAM_B1) * g
    v = ADAM_B2 * v + (1.0 - ADAM_B2) * (g * g)
    m_hat = m / (1.0 - ADAM_B1 ** ADAM_STEP)
    v_hat = v / (1.0 - ADAM_B2 ** ADAM_STEP)
    delta = -ADAM_LR * (m_hat / (jnp.sqrt(v_hat) + ADAM_EPS) + ADAM_WD * w)
    return delta, m, v


def _adamw_shard(n_recv, g_own, recv, w, m, v, *, rows, name):
    R, C = g_own.shape

    def body(n_ref, g_ref, r_ref, w_ref, m_ref, v_ref, go_ref, d_ref, mo_ref, vo_ref):
        g = g_ref[...]
        for k in range(N_DEV - 1):
            g = g + jnp.where(k < n_ref[0], r_ref[k].astype(F32), 0.0)
        d, mn, vn = _adamw_math(w_ref[...], g, m_ref[...], v_ref[...])
        go_ref[...] = g
        d_ref[...] = d
        mo_ref[...] = mn
        vo_ref[...] = vn

    blk = pl.BlockSpec((rows, C), lambda i: (i, 0))
    return pl.pallas_call(
        body, name=name, grid=(R // rows,),
        in_specs=[pl.BlockSpec(memory_space=pltpu.SMEM), blk,
                  pl.BlockSpec((N_DEV - 1, rows, C), lambda i: (0, i, 0)), blk, blk, blk],
        out_specs=[blk] * 4, out_shape=[jax.ShapeDtypeStruct((R, C), F32)] * 4,
        compiler_params=_params(("arbitrary",)),
    )(n_recv, g_own, recv, w, m, v)


def _minor_rows_view(a):
    return jnp.transpose(a, (2, 0, 1)).reshape(SHARD_COLS * 8, 128)


def _from_minor_rows_view(v):
    return jnp.transpose(v.reshape(SHARD_COLS, 8, 128), (1, 2, 0)).reshape(1, D_MODEL, SHARD_COLS)


def _adamw_w_in(n_recv, g_own, recv, w, m, v):
    C = SHARD_COLS
    pad = -C % 128

    def body(n_ref, g_ref, r_ref, w_ref, m_ref, v_ref, go_ref, d_ref, mo_ref, vo_ref):
        for q in range(D_MODEL // 128):
            band = pl.ds(q * 128, 128)
            g = g_ref[band, :]
            for k in range(N_DEV - 1):
                g = g + jnp.where(k < n_ref[0], r_ref[k, band, :].astype(F32), 0.0)
            g = jnp.pad(g, ((0, 0), (0, pad))).T[0:C]
            rows = pl.ds(q, C, stride=8)
            d, mn, vn = _adamw_math(w_ref[rows, :], g, m_ref[rows, :], v_ref[rows, :])
            go_ref[rows, :] = g
            d_ref[rows, :] = d
            mo_ref[rows, :] = mn
            vo_ref[rows, :] = vn

    return pl.pallas_call(
        body, name="adamw_w_in", out_shape=[jax.ShapeDtypeStruct(w.shape, F32)] * 4,
        in_specs=[pl.BlockSpec(memory_space=pltpu.SMEM)] + [pl.BlockSpec(memory_space=pltpu.VMEM)] * 5,
        out_specs=[pl.BlockSpec(memory_space=pltpu.VMEM)] * 4,
        compiler_params=_params(),
    )(n_recv, g_own, recv, w, m, v)


SMALL = ("conv_b", "dt_bias", "a_log", "d_skip", "ssd_norm_w", "attn_sinks", "ln_g", "ln_b")


def _adamw_small(gathered, params):
    n_p = len(SMALL)

    def body(*refs):
        acc = []
        for r in refs[:5]:
            t = r[0]
            for k in range(1, N_DEV):
                t = t + r[k]
            acc.append(t)
        head, conv, norm, scal, sink = acc
        grads = dict(conv_b=conv[4:5, :], dt_bias=scal[0:1, 0:N_HEADS], a_log=scal[1:2, 0:N_HEADS],
                     d_skip=scal[2:3, 0:N_HEADS], ssd_norm_w=norm[0:1, :], attn_sinks=sink[0:1, 0:N_HEADS],
                     ln_g=head[0:1, :], ln_b=head[1:2, :])
        wmv = refs[5:5 + 3 * n_p]
        outs = refs[5 + 3 * n_p:]
        outs[0][...] = head[3:4, 0:1]
        outs[1][...] = conv[0:4, :]
        for i, name in enumerate(SMALL):
            w_ref, m_ref, v_ref = wmv[3 * i:3 * i + 3]
            g = grads[name]
            d, mn, vn = _adamw_math(w_ref[...], g, m_ref[...], v_ref[...])
            for o_ref, val in zip(outs[2 + 4 * i:6 + 4 * i], (g, d, mn, vn)):
                o_ref[...] = val

    flat = [a for name in SMALL for a in params[name]]
    out_shape = [jax.ShapeDtypeStruct((1, 1), F32), jax.ShapeDtypeStruct((4, D_XBC), F32)]
    for name in SMALL:
        out_shape += [jax.ShapeDtypeStruct(params[name][0].shape, F32)] * 4
    res = pl.pallas_call(body, name="adamw_small", out_shape=out_shape, compiler_params=_params())(*gathered, *flat)
    return res[0], res[1], {name: res[2 + 4 * i:6 + 4 * i] for i, name in enumerate(SMALL)}


def _adamw_plain(g, w, m, v):
    def body(g_ref, w_ref, m_ref, v_ref, d_ref, mo_ref, vo_ref):
        d, mn, vn = _adamw_math(w_ref[...], g_ref[...], m_ref[...], v_ref[...])
        d_ref[...] = d
        mo_ref[...] = mn
        vo_ref[...] = vn

    return pl.pallas_call(
        body, name="adamw_conv_w", out_shape=[jax.ShapeDtypeStruct(w.shape, F32)] * 3,
        compiler_params=_params(),
    )(g, w, m, v)


def _lane_pattern(fn):
    return np.asarray([fn(l % HEAD_DIM) for l in range(128)], np.float32)


ROPE_INV = _lane_pattern(lambda r: ROPE_THETA ** (-2.0 * (r % 8) / ROPE_DIM) if r < ROPE_DIM else 0.0)
ROPE_SIN_A = _lane_pattern(lambda r: 1.0 if 8 <= r < ROPE_DIM else 0.0)
ROPE_SIN_B = _lane_pattern(lambda r: -1.0 if r < 8 else 0.0)


def _rope_tables(positions):
    ang = positions.astype(F32)[:, None] * ROPE_INV[None, :]
    sn = jnp.sin(ang)
    return jnp.concatenate([jnp.cos(ang), sn * ROPE_SIN_A[None, :], sn * ROPE_SIN_B[None, :]], axis=1)


def _expansion():
    E = np.arange(1024)[None, :] // HEAD_DIM == np.arange(128)[:, None]
    return jnp.asarray(E, BF16), jnp.asarray(E.T, BF16)


def _ssd_args(conv_w, conv_b, dt_bias, a_log, d_skip, norm_w, E):
    return (conv_w, conv_b, dt_bias.reshape(-1), a_log.reshape(-1), d_skip.reshape(-1), norm_w, E)


def kernel(x, positions, w_in, conv_w, conv_b, dt_bias, a_log, d_skip, ssd_norm_w, attn_sinks, w_out, ln_g, ln_b, loss_target, m_w_in, m_conv_w, m_conv_b, m_dt_bias, m_a_log, m_d_skip, m_ssd_norm_w, m_attn_sinks, m_w_out, m_ln_g, m_ln_b, v_w_in, v_conv_w, v_conv_b, v_dt_bias, v_a_log, v_d_skip, v_ssd_norm_w, v_attn_sinks, v_w_out, v_ln_g, v_ln_b):
    me = _index(*_position())
    me1 = me.reshape(1).astype(jnp.int32)
    x0, target = x[0], loss_target[0]
    bf16_shard = lambda shape: jax.ShapeDtypeStruct(shape, BF16)
    E, ET = _expansion()
    tabs = _rope_tables(positions[0])
    sinks = attn_sinks.reshape(-1)

    w_in_all, conv_w_all = _all_gather([w_in[0].astype(BF16), conv_w[0]])
    w_ssd, w_att = _unpack_w_in(w_in_all)
    conv_w_f = jnp.transpose(conv_w_all, (1, 0, 2)).reshape(4, D_XBC)
    ssd_args = _ssd_args(conv_w_f, conv_b, dt_bias, a_log, d_skip, ssd_norm_w, E)

    proj_ssd, xb = _matmul(x0, w_ssd, tm=1024, tn=S_W // 2, name="in_proj_ssd", emit_a=True)
    proj_att = _matmul(xb, w_att, tm=1024, tn=A_W // 2, name="in_proj_att")
    gather_w_out = _Hosted([w_out[0].astype(BF16)], [bf16_shard((N_DEV, 256, D_MODEL))], [_Flow("gather", 0, 0)])
    y, ypre, hprev, pre, w_out_all = _ssd_forward(proj_ssd, *ssd_args, comm=gather_w_out)
    w_out_f = w_out_all.reshape(2 * D_MODEL, D_MODEL)
    y = _swa_forward(proj_att, tabs, sinks, y)
    dr, dy, acc_head = _head(y, x0, target, w_out_f, ln_g, ln_b, tm=256)

    dw_out, dw_out_bf16 = _matmul_tn(y, dr, tl=512, tn=D_MODEL, name="dw_out", emit_bf16=True)
    own_out = lax.dynamic_index_in_dim(dw_out.reshape(N_DEV, 256, D_MODEL), me, axis=0, keepdims=False)
    send_out = _Hosted([dw_out_bf16.reshape(N_DEV, 256, D_MODEL)], [bf16_shard((N_DEV - 1, 256, D_MODEL))],
                       [_Flow("exchange", 0, 0)])
    d_ssd, acc_cw, acc_w, acc_s, recv_out = _ssd_backward(proj_ssd, hprev, ypre, pre, dy, *ssd_args, ET, comm=send_out)
    dw_ssd = _matmul_tn(xb, d_ssd, tl=512, tn=S_W // 2, name="dw_in_ssd")
    parts_lo, own_lo = _pack_dw_in(me1, dw_ssd, None, 0)
    recv_shape = bf16_shard((N_DEV - 1, D_MODEL, SHARD_COLS))
    send_lo = _Hosted([parts_lo], [recv_shape], [_Flow("exchange", 0, 0, target_x=0)])
    d_att, dsink, recv_in = _swa_backward(proj_att, tabs, sinks, dy, comm=send_lo)
    dw_att = _matmul_tn(xb, d_att, tl=512, tn=A_W // 2, name="dw_in_att")
    (stack_hi,) = _pack_dw_in(me1, dw_ssd, dw_att, 1)
    pos3 = jnp.stack(_position()).astype(jnp.int32)
    chip_hi, own_in = _pair_sum(pos3, stack_hi, _pair_swap(stack_hi), own_lo)
    accs = [acc_head, acc_cw, acc_w, acc_s, dsink]
    send_hi = _Hosted([chip_hi, recv_in] + accs,
                      [recv_shape] + [jax.ShapeDtypeStruct((N_DEV,) + a.shape, F32) for a in accs],
                      [_Flow("chip_exchange", 0, 0, target_x=1)] + [_Flow("gather", 2 + i, 1 + i) for i in range(5)],
                      aliases={1: 0})
    dx, recv_in, *gathered = _input_gradient(d_ssd, d_att, w_ssd, w_att, dr, tm=256, comm=send_hi)
    n_recv_in = jnp.where(me < 4, N_DEV - 1, 3).reshape(1).astype(jnp.int32)
    n_recv_out = jnp.full((1,), N_DEV - 1, jnp.int32)

    g_in, d_in, nm_in, nv_in = [_from_minor_rows_view(r) for r in _adamw_w_in(
        n_recv_in, own_in, recv_in, _minor_rows_view(w_in), _minor_rows_view(m_w_in), _minor_rows_view(v_w_in))]
    g_out, d_out, nm_out, nv_out = _adamw_shard(n_recv_out, own_out, recv_out, w_out[0], m_w_out[0], v_w_out[0],
                                                rows=256, name="adamw_w_out")
    loss, g_conv_w, small = _adamw_small(gathered, dict(
        conv_b=(conv_b, m_conv_b, v_conv_b), dt_bias=(dt_bias, m_dt_bias, v_dt_bias), a_log=(a_log, m_a_log, v_a_log),
        d_skip=(d_skip, m_d_skip, v_d_skip), ssd_norm_w=(ssd_norm_w, m_ssd_norm_w, v_ssd_norm_w),
        attn_sinks=(attn_sinks, m_attn_sinks, v_attn_sinks), ln_g=(ln_g, m_ln_g, v_ln_g), ln_b=(ln_b, m_ln_b, v_ln_b)))
    g_cw = lax.dynamic_slice_in_dim(g_conv_w, me * (D_XBC // N_DEV), D_XBC // N_DEV, axis=1)
    d_cw, nm_cw, nv_cw = _adamw_plain(g_cw, conv_w[0], m_conv_w[0], v_conv_w[0])

    def leaves(i, big_in, cw, big_out):
        mid = [small[k][i] for k in ("conv_b", "dt_bias", "a_log", "d_skip", "ssd_norm_w", "attn_sinks")]
        return [big_in, cw[None]] + mid + [big_out[None], small["ln_g"][i], small["ln_b"][i]]

    return (loss.reshape(()), dx[None], *leaves(0, g_in, g_cw, g_out), *leaves(1, d_in, d_cw, d_out),
            *leaves(2, nm_in, nm_cw, nm_out), *leaves(3, nv_in, nv_cw, nv_out))
```

```python
import jax
import jax.numpy as jnp
from jax import lax
from jax.experimental import pallas as pl
from jax.experimental.pallas import tpu as pltpu
import numpy as np

F32 = jnp.float32
BF16 = jnp.bfloat16
_MXU = jnp.bfloat16

N_DEV = 8
D_MODEL = 1024
D_SSD = 1024
D_ATT = 1024
HEAD_DIM = 64
N_HEADS = 16
SSD_GROUPS = 2
KV_HEADS = 4
CHUNK = 128
D_XBC = 1536
D_IN_PROJ = 5136
ROPE_DIM = 16
ROPE_THETA = 500000.0
ALPHA = (2.0 * 1) ** 0.25
LN_EPS = 1e-5
RMS_EPS = 1e-5
ATT_SCALE = HEAD_DIM ** -0.5
NEG = -1e30

S_Z, S_XS, S_B, S_C, S_DT, S_W = 0, 1024, 2048, 2304, 2560, 2816
N_SSD_REAL = 2576
A_Q, A_K, A_V, A_G, A_W = 0, 1024, 1280, 1536, 2560

ADAM_LR = 0.001
ADAM_B1 = 0.9
ADAM_B2 = 0.999
ADAM_EPS = 1e-08
ADAM_WD = 0.01
ADAM_STEP = 10

VMEM_LIMIT = 48 * 1024 * 1024
MESH = pl.DeviceIdType.MESH


def _params(sem=None):
    return pltpu.CompilerParams(dimension_semantics=sem, vmem_limit_bytes=VMEM_LIMIT)


def _mm(a, b):
    return jnp.dot(a.astype(_MXU), b.astype(_MXU), preferred_element_type=F32)


def _mm_nt(a, b):
    return lax.dot_general(a.astype(_MXU), b.astype(_MXU), (((1,), (1,)), ((), ())),
                           preferred_element_type=F32)


def _mm_tn(a, b):
    return lax.dot_general(a.astype(_MXU), b.astype(_MXU), (((0,), (0,)), ((), ())),
                           preferred_element_type=F32)


def _split3(v):
    hi = v.astype(BF16)
    r = v - hi.astype(F32)
    mid = r.astype(BF16)
    lo = (r - mid.astype(F32)).astype(BF16)
    return hi, mid, lo


def _mm_exact_r(v, p01):
    hi, mid, lo = _split3(v)
    d = lambda a: jnp.dot(a, p01, preferred_element_type=F32)
    return d(hi) + d(mid) + d(lo)


def _mm_exact_l(p01, v):
    hi, mid, lo = _split3(v)
    d = lambda a: jnp.dot(p01, a, preferred_element_type=F32)
    return d(hi) + d(mid) + d(lo)


def _sigmoid(x):
    return 1.0 / (1.0 + jnp.exp(-x))


def _softplus(x):
    e = jnp.exp(-jnp.abs(x))
    u = 1.0 + e
    log1p = jnp.where(u == 1.0, e, jnp.log(u) * (e / (u - 1.0)))
    return jnp.maximum(x, 0.0) + log1p


def _rows8(rows):
    n = rows[0].shape[1]
    rid = lax.broadcasted_iota(jnp.int32, (8, n), 0)
    out = jnp.zeros((8, n), F32)
    for k, r in enumerate(rows):
        out = out + jnp.where(rid == k, r, 0.0)
    return out


def _colsum(a):
    return jnp.sum(a, axis=0, keepdims=True)


def _matmul(a, b, *, tm, tn, name, emit_a=False):
    M, K = a.shape
    N = b.shape[1]

    def body(a_ref, b_ref, o_ref, *rest):
        am = a_ref[...].astype(_MXU)
        o_ref[...] = jnp.dot(am, b_ref[...].astype(_MXU), preferred_element_type=F32)
        if emit_a:
            rest[0][...] = am

    out_specs = [pl.BlockSpec((tm, tn), lambda i, j: (i, j))]
    out_shape = [jax.ShapeDtypeStruct((M, N), F32)]
    if emit_a:
        out_specs.append(pl.BlockSpec((tm, K), lambda i, j: (i, 0)))
        out_shape.append(jax.ShapeDtypeStruct((M, K), _MXU))
    res = pl.pallas_call(
        body, name=name, grid=(M // tm, N // tn),
        in_specs=[pl.BlockSpec((tm, K), lambda i, j: (i, 0)), pl.BlockSpec((K, tn), lambda i, j: (0, j))],
        out_specs=out_specs, out_shape=out_shape, compiler_params=_params(("arbitrary", "arbitrary")),
    )(a, b)
    return res if emit_a else res[0]


def _matmul_tn(a, g, *, tl, tn, name, emit_bf16=False):
    L, M = a.shape
    N = g.shape[1]
    last = L // tl - 1

    def body(a_ref, g_ref, o_ref, *rest):
        @pl.when(pl.program_id(1) == 0)
        def _():
            o_ref[...] = jnp.zeros_like(o_ref)

        o_ref[...] += _mm_tn(a_ref[...], g_ref[...])
        if emit_bf16:
            @pl.when(pl.program_id(1) == last)
            def _():
                rest[0][...] = o_ref[...].astype(BF16)

    spec = pl.BlockSpec((M, tn), lambda j, l: (0, j))
    res = pl.pallas_call(
        body, name=name, grid=(N // tn, L // tl),
        in_specs=[pl.BlockSpec((tl, M), lambda j, l: (l, 0)), pl.BlockSpec((tl, tn), lambda j, l: (l, j))],
        out_specs=[spec, spec] if emit_bf16 else [spec],
        out_shape=[jax.ShapeDtypeStruct((M, N), F32)] + ([jax.ShapeDtypeStruct((M, N), BF16)] if emit_bf16 else []),
        compiler_params=_params(("arbitrary", "arbitrary")),
    )(a, g)
    return res if emit_bf16 else res[0]


def _position():
    return lax.axis_index("x"), lax.axis_index("y"), lax.axis_index("c")


def _index(px, py, pc):
    return 4 * px + 2 * py + pc


def _flip(pos, k):
    x, y, c = pos
    return ((1 - x) if (k >> 2) & 1 else x, (1 - y) if (k >> 1) & 1 else y, (1 - c) if k & 1 else c)


def _when(cond, fn):
    if cond is True:
        fn()
    else:
        pl.when(cond)(fn)


def _remote(src, dst, send_sem, recv_sem, peer):
    return pltpu.make_async_remote_copy(src_ref=src, dst_ref=dst, send_sem=send_sem, recv_sem=recv_sem,
                                        device_id=peer, device_id_type=MESH)


class _Flow:
    def __init__(self, kind, operand, result, target_x=None):
        self.kind, self.operand, self.result, self.target_x = kind, operand, result, target_x


class _Hosted:
    def __init__(self, operands, out_shapes, flows, aliases=None):
        self.operands, self.out_shapes, self.flows = operands, out_shapes, flows
        self.aliases = aliases or {}

    def plan(self, ins, outs, send_sems, recv_sems, local_sems):
        me = _position()
        mi = _index(*me)
        sends, recvs, locals_ = [], [], []
        for row, f in enumerate(self.flows):
            src, dst = ins[f.operand], outs[f.result]
            for k in range(1, N_DEV):
                peer = _flip(me, k)
                sems = (send_sems.at[row, k - 1], recv_sems.at[row, k - 1])
                if f.kind == "exchange":
                    owner = _index(*peer) if f.target_x is None else 2 * peer[1] + peer[2]
                    cp = _remote(src.at[owner], dst.at[k - 1], *sems, peer)
                    to_peer = True if f.target_x is None else peer[0] == f.target_x
                    to_me = True if f.target_x is None else me[0] == f.target_x
                    sends.append((to_peer, cp))
                    recvs.append((to_me, cp))
                elif f.kind == "chip_exchange":
                    if k & 1:
                        continue
                    cp = _remote(src.at[peer[1]], dst.at[k // 2 - 1], *sems, peer)
                    sends.append((peer[0] == f.target_x, cp))
                    recvs.append((me[0] == f.target_x, cp))
                else:
                    sends.append((True, _remote(src, dst.at[mi], *sems, peer)))
                    recvs.append((True, _remote(src, dst.at[_index(*peer)], *sems, peer)))
            if f.kind == "gather":
                locals_.append(pltpu.make_async_copy(src, dst.at[mi], local_sems.at[row]))

        def start():
            for cp in locals_:
                cp.start()
            for cond, cp in sends:
                _when(cond, cp.start)

        def wait():
            for cond, cp in recvs:
                _when(cond, cp.wait_recv)
            for cond, cp in sends:
                _when(cond, cp.wait_send)
            for cp in locals_:
                cp.wait()

        return start, wait


def _call(body, comm, *, name, grid, in_specs, out_specs, out_shape, scratch_shapes, args, aliases=None):
    io_alias = dict(aliases or {})
    if comm is None:
        return pl.pallas_call(body, name=name, grid=grid, in_specs=in_specs, out_specs=out_specs, out_shape=out_shape,
                              scratch_shapes=scratch_shapes, input_output_aliases=io_alias,
                              compiler_params=_params(("arbitrary",)))(*args)
    n_in, n_out, n_scr = len(args), len(out_shape), len(scratch_shapes)
    c_in, c_out, rows = len(comm.operands), len(comm.out_shapes), len(comm.flows)

    def hosted(*refs):
        ins, refs = refs[:n_in], refs[n_in:]
        cins, refs = refs[:c_in], refs[c_in:]
        outs, refs = refs[:n_out], refs[n_out:]
        couts, refs = refs[:c_out], refs[c_out:]
        scr, (send_sems, recv_sems, local_sems) = refs[:n_scr], refs[n_scr:]
        start, wait = comm.plan(cins, couts, send_sems, recv_sems, local_sems)
        pl.when(pl.program_id(0) == 0)(start)
        body(*ins, *outs, *scr)
        pl.when(pl.program_id(0) == grid[0] - 1)(wait)

    for ci, co in comm.aliases.items():
        io_alias[n_in + ci] = n_out + co
    any_spec = pl.BlockSpec(memory_space=pl.ANY)
    sems = [pltpu.SemaphoreType.DMA((rows, N_DEV - 1)), pltpu.SemaphoreType.DMA((rows, N_DEV - 1)),
            pltpu.SemaphoreType.DMA((rows,))]
    return pl.pallas_call(
        hosted, name=name, grid=grid, in_specs=list(in_specs) + [any_spec] * c_in,
        out_specs=list(out_specs) + [any_spec] * c_out, out_shape=list(out_shape) + list(comm.out_shapes),
        scratch_shapes=list(scratch_shapes) + sems, input_output_aliases=io_alias,
        compiler_params=_params(("arbitrary",)))(*args, *comm.operands)


def _head_row(ref, width, rep):
    hid = lax.broadcasted_iota(jnp.int32, (1, width), 1) // rep
    row = jnp.zeros((1, width), F32)
    for h in range(N_HEADS):
        row = jnp.where(hid == h, ref[h], row)
    return row


def _ssd_recompute(first, p_ref, halo_ref, cw_ref, cb_ref, dtb_ref, alog_ref, e_ref, ext_scr, pre=None):
    ext_scr[0:8, :] = jnp.where(first, 0.0, halo_ref[:, S_XS:S_DT])
    if pre is not None:
        ext_scr[8:16, :] = p_ref[0:8, S_XS:S_DT]
    else:
        ext_scr[8:136, :] = p_ref[:, S_XS:S_DT]
        cw = cw_ref[...]
        pre = (cb_ref[0:1, :] + cw[3:4, :] * ext_scr[8:136, :] + cw[2:3, :] * ext_scr[7:135, :]
               + cw[1:2, :] * ext_scr[6:134, :] + cw[0:1, :] * ext_scr[5:133, :])
    sg = _sigmoid(pre)
    act = pre * sg
    lane = lax.broadcasted_iota(jnp.int32, (1, 128), 1)
    A = jnp.where(lane < N_HEADS, -jnp.exp(_head_row(alog_ref, 128, 1)), 0.0)
    raw = p_ref[:, S_DT:S_DT + 128] + _head_row(dtb_ref, 128, 1)
    dt = _softplus(raw)
    dA = dt * A
    row = lax.broadcasted_iota(jnp.int32, (128, 128), 0)
    col = lax.broadcasted_iota(jnp.int32, (128, 128), 1)
    tril = (row >= col).astype(BF16)
    acs = _mm_exact_l(tril, dA)
    last = acs[127:128, :]
    ds = jnp.exp(last - acs)
    eo = jnp.exp(acs)
    E = e_ref[...]
    ex = _mm_exact_r(jnp.concatenate([dt, ds, eo], axis=0), E)
    dt_e, ds_e, eo_e = ex[0:128], ex[128:256], ex[256:384]
    xs_c = act[:, 0:1024]
    X = xs_c * dt_e
    return dict(pre=pre, sg=sg, xs_c=xs_c, Bc=act[:, 1024:1280], Cc=act[:, 1280:1536], A=A, raw=raw, dt=dt,
                acs=acs, acsT=acs.T, eo_e=eo_e, ds_e=ds_e, dt_e=dt_e, cd_e=eo_e[127:128, :],
                X=X, Xd=X * ds_e, row=row, col=col)


def _split_halves(t):
    lo = _lo_half(CHUNK)
    return jnp.concatenate([jnp.where(lo, t, 0.0), jnp.where(lo, 0.0, t)], axis=0)


def _ssd_core(R, hprev):
    causal = R["row"] >= R["col"]
    acs, acsT, X = R["acs"], R["acsT"], R["X"]
    ydiag, yoff, snew = [], [], []
    for g in range(SSD_GROUPS):
        Bg = R["Bc"][:, g * 128:(g + 1) * 128]
        Cg = R["Cc"][:, g * 128:(g + 1) * 128]
        cols = slice(g * 512, (g + 1) * 512)
        CB = _mm_nt(Cg, Bg)
        snew.append(_mm_tn(Bg, R["Xd"][:, cols]))
        yoff.append(_mm(Cg, hprev[:, cols]))
        for j in range(4):
            h0 = g * 8 + 2 * j
            ms = [CB * jnp.exp(jnp.where(causal, acs[:, h:h + 1] - acsT[h:h + 1, :], NEG)) for h in (h0, h0 + 1)]
            ydiag.append(_mm(jnp.concatenate(ms, axis=1), _split_halves(X[:, h0 * HEAD_DIM:h0 * HEAD_DIM + 128])))
    Y = jnp.concatenate(ydiag, axis=1) + jnp.concatenate(yoff, axis=1) * R["eo_e"]
    return Y, jnp.concatenate(snew, axis=1)


def _ssd_forward(proj_ssd, conv_w8, conv_b8, dtb8, alog8, dskip_e, norm_w, E, comm=None):
    L = proj_ssd.shape[0]
    nc = L // CHUNK

    def body(p_ref, halo_ref, cw_ref, cb_ref, dtb_ref, alog_ref, dsk_ref, nw_ref, e_ref,
             y_ref, ypre_ref, hprev_ref, pre_ref, h_scr, ext_scr):
        c = pl.program_id(0)
        first = c == 0

        @pl.when(first)
        def _():
            h_scr[...] = jnp.zeros_like(h_scr)

        R = _ssd_recompute(first, p_ref, halo_ref, cw_ref, cb_ref, dtb_ref, alog_ref, e_ref, ext_scr)
        hprev = h_scr[...]
        hprev_ref[...] = hprev
        pre_ref[...] = R["pre"]
        Y, snew = _ssd_core(R, hprev)
        h_scr[...] = hprev * R["cd_e"] + snew
        Y = Y + _head_row(dsk_ref, D_SSD, HEAD_DIM) * R["xs_c"]
        ypre_ref[...] = Y
        z = p_ref[:, S_Z:S_Z + 1024]
        yf = Y * (z * _sigmoid(z))
        outs = []
        for g in range(SSD_GROUPS):
            yg = yf[:, g * 512:(g + 1) * 512]
            r = lax.rsqrt(jnp.mean(yg * yg, axis=-1, keepdims=True) + RMS_EPS)
            outs.append(yg * r)
        y_ref[...] = (jnp.concatenate(outs, axis=1) * nw_ref[0:1, :]).astype(y_ref.dtype)

    const = lambda shape: pl.BlockSpec(shape, lambda c: (0, 0))
    smem = pl.BlockSpec(memory_space=pltpu.SMEM)
    return _call(
        body, comm, name="ssd_fwd", grid=(nc,),
        in_specs=[pl.BlockSpec((CHUNK, S_W), lambda c: (c, 0)),
                  pl.BlockSpec((8, S_W), lambda c: (jnp.maximum(c * 16 - 1, 0), 0)),
                  const((4, D_XBC)), const((1, D_XBC)), smem, smem, smem, const((1, 1024)), const((128, 1024))],
        out_specs=[pl.BlockSpec((CHUNK, D_SSD), lambda c: (c, 0)), pl.BlockSpec((CHUNK, D_SSD), lambda c: (c, 0)),
                   pl.BlockSpec((128, 1024), lambda c: (c, 0)), pl.BlockSpec((CHUNK, D_XBC), lambda c: (c, 0))],
        out_shape=[jax.ShapeDtypeStruct((L, D_SSD + D_ATT), _MXU), jax.ShapeDtypeStruct((L, D_SSD), F32),
                   jax.ShapeDtypeStruct((nc * 128, 1024), F32), jax.ShapeDtypeStruct((L, D_XBC), F32)],
        scratch_shapes=[pltpu.VMEM((128, 1024), F32), pltpu.VMEM((136, D_XBC), F32)],
        args=(proj_ssd, proj_ssd, conv_w8, conv_b8, dtb8, alog8, dskip_e, norm_w, E))


def _ssd_backward(proj_ssd, hprev_all, ypre, pre, dy, conv_w8, conv_b8, dtb8, alog8, dskip_e, norm_w, E, ET, comm=None):
    L = proj_ssd.shape[0]
    nc = L // CHUNK

    def body(p_ref, halo_ref, hprev_ref, ypre_ref, pre_ref, dy_ref, cw_ref, cb_ref, dtb_ref, alog_ref, dsk_ref, nw_ref, e_ref,
             et_ref, dp_ref, acc_cw_ref, acc_w_ref, acc_s_ref, dh_scr, ext_scr, ext2_scr, nxt_scr):
        i = pl.program_id(0)
        c = nc - 1 - i
        first = c == 0

        @pl.when(i == 0)
        def _():
            dh_scr[...] = jnp.zeros_like(dh_scr)
            nxt_scr[...] = jnp.zeros_like(nxt_scr)
            acc_cw_ref[...] = jnp.zeros_like(acc_cw_ref)
            acc_w_ref[...] = jnp.zeros_like(acc_w_ref)
            acc_s_ref[...] = jnp.zeros_like(acc_s_ref)

        R = _ssd_recompute(first, p_ref, halo_ref, cw_ref, cb_ref, dtb_ref, alog_ref, e_ref, ext_scr, pre_ref[...])
        hprev = hprev_ref[...]
        xs_c, X, Xd = R["xs_c"], R["X"], R["Xd"]
        acs, acsT = R["acs"], R["acsT"]
        ET = et_ref[...]
        dsk = _head_row(dsk_ref, D_SSD, HEAD_DIM)
        Y = ypre_ref[...]

        z = p_ref[:, S_Z:S_Z + 1024]
        sz = _sigmoid(z)
        silz = z * sz
        yf = Y * silz
        dyv = dy_ref[...]
        nw = nw_ref[0:1, :]
        dyf_parts, dnw_parts = [], []
        for g in range(SSD_GROUPS):
            cols = slice(g * 512, (g + 1) * 512)
            yg = yf[:, cols]
            r = lax.rsqrt(jnp.mean(yg * yg, axis=-1, keepdims=True) + RMS_EPS)
            yn = yg * r
            dyn = dyv[:, cols] * nw[:, cols]
            dnw_parts.append(_colsum(dyv[:, cols] * yn))
            dyf_parts.append(r * (dyn - yn * jnp.mean(dyn * yn, axis=-1, keepdims=True)))
        dyf = jnp.concatenate(dyf_parts, axis=1)
        dY = dyf * silz
        dz = dyf * Y * (sz * (1.0 + z * (1.0 - sz)))

        dhn = dh_scr[...]
        dYo = dY * R["eo_e"]
        causal = R["row"] >= R["col"]
        dacs = jnp.zeros((128, 128), F32)
        dacs_t = jnp.zeros((128, 128), F32)
        dxdiag, dxd, dhprev, dBs, dCs, yoff = [], [], [], [], [], []
        for g in range(SSD_GROUPS):
            Bg = R["Bc"][:, g * 128:(g + 1) * 128]
            Cg = R["Cc"][:, g * 128:(g + 1) * 128]
            cols = slice(g * 512, (g + 1) * 512)
            CB = _mm_nt(Cg, Bg)
            dCB = jnp.zeros((128, 128), F32)
            for j in range(4):
                h0 = g * 8 + 2 * j
                pc = slice(h0 * HEAD_DIM, h0 * HEAD_DIM + 128)
                dYst = _split_halves(dY[:, pc])
                dMst = _mm_nt(dYst, X[:, pc])
                mts = []
                for a, h in enumerate((h0, h0 + 1)):
                    acol = acs[:, h:h + 1]
                    arow = acsT[h:h + 1, :]
                    Lm = jnp.exp(jnp.where(causal, acol - arow, NEG))
                    M = CB * Lm
                    dM = dMst[a * 128:(a + 1) * 128]
                    dCB = dCB + dM * Lm
                    G = dM * M
                    dacs = dacs + jnp.where(R["col"] == h, jnp.sum(G, axis=1, keepdims=True), 0.0)
                    dacs_t = dacs_t + jnp.where(R["row"] == h, jnp.sum(G, axis=0, keepdims=True), 0.0)
                    mts.append(M.T)
                dxdiag.append(_mm(jnp.concatenate(mts, axis=1), dYst))
            dS = dhn[:, cols]
            dxd.append(_mm(Bg, dS))
            yoff.append(_mm(Cg, hprev[:, cols]))
            dhprev.append(_mm_tn(Cg, dYo[:, cols]))
            dCs.append(_mm_nt(dYo[:, cols], hprev[:, cols]) + _mm(dCB, Bg))
            dBs.append(_mm_tn(dCB, Cg) + _mm_nt(Xd[:, cols], dS))
        Yoff = jnp.concatenate(yoff, axis=1) * R["eo_e"]
        dXd = jnp.concatenate(dxd, axis=1)
        dX = jnp.concatenate(dxdiag, axis=1) + dXd * R["ds_e"]
        t_state = dXd * Xd
        dacs = dacs + _mm_exact_r(dY * Yoff - t_state, ET) - dacs_t.T
        v_last = _colsum(t_state + dhn * hprev * R["cd_e"])
        dlast = _mm_exact_r(jnp.broadcast_to(v_last, (8, 1024)), ET)[0:1, :]
        dacs = dacs + jnp.where(R["row"] == 127, dlast, 0.0)
        triu = (R["col"] >= R["row"]).astype(BF16)
        da = _mm_exact_l(triu, dacs)
        ddt = da * R["A"] + _mm(dX * xs_c, ET)
        ddt_raw = ddt * _sigmoid(R["raw"])
        dxs_c = dX * R["dt_e"] + dY * dsk
        dh_scr[...] = jnp.concatenate(dhprev, axis=1) + dhn * R["cd_e"]

        dact = jnp.concatenate([dxs_c] + dBs + dCs, axis=1)
        pre, sg = R["pre"], R["sg"]
        dpre = dact * (sg * (1.0 + pre * (1.0 - sg)))
        ext2_scr[0:8, :] = dpre[120:128, :]
        ext2_scr[8:16, :] = nxt_scr[...]
        nxt_scr[...] = dpre[0:8, :]
        cw = cw_ref[...]
        u_b, dpre_b = p_ref[:, S_XS:S_DT].astype(_MXU), dpre.astype(_MXU)
        dxbc = cw[3:4, :] * dpre
        taps = [_colsum(dpre * p_ref[:, S_XS:S_DT])]
        for s in (1, 2, 3):
            down = (R["row"] - R["col"] == s).astype(_MXU)
            up = (R["col"] - R["row"] == s).astype(_MXU)
            u_s = jnp.concatenate([ext_scr[8 - s:16 - s, :],
                                   jnp.dot(down, u_b, preferred_element_type=F32)[8:128]], axis=0)
            d_s = jnp.concatenate([jnp.dot(up, dpre_b, preferred_element_type=F32)[0:120],
                                   ext2_scr[s:8 + s, :]], axis=0)
            dxbc = dxbc + cw[3 - s:4 - s, :] * d_s
            taps.append(_colsum(dpre * u_s))
        acc_cw_ref[...] += _rows8(taps[::-1] + [_colsum(dpre)])
        acc_w_ref[...] += _rows8([jnp.concatenate(dnw_parts, axis=1), _colsum(dY * xs_c)])
        acc_s_ref[...] += _rows8([_colsum(ddt_raw), _colsum(da * R["dt"])])

        lane = lax.broadcasted_iota(jnp.int32, (128, 128), 1)
        dp_ref[:, S_Z:S_Z + 1024] = dz
        dp_ref[:, S_XS:S_DT] = dxbc
        dp_ref[:, S_DT:S_DT + 128] = jnp.where(lane < N_HEADS, ddt_raw, 0.0)
        dp_ref[:, S_DT + 128:S_W] = jnp.zeros((128, 128), F32)

        @pl.when(i == nc - 1)
        def _():
            acc = acc_s_ref[...]
            dskip = _mm_exact_r(acc_w_ref[...], ET)[1:2, :]
            acc_s_ref[...] = _rows8([acc[0:1, :], acc[1:2, :] * R["A"], dskip])

    const = lambda shape: pl.BlockSpec(shape, lambda i: (0, 0))
    smem = pl.BlockSpec(memory_space=pltpu.SMEM)
    rev = lambda i: (nc - 1 - i, 0)
    return _call(
        body, comm, name="ssd_bwd", grid=(nc,),
        in_specs=[pl.BlockSpec((CHUNK, S_W), rev),
                  pl.BlockSpec((8, S_W), lambda i: (jnp.maximum((nc - 1 - i) * 16 - 1, 0), 0)),
                  pl.BlockSpec((128, 1024), rev),
                  pl.BlockSpec((CHUNK, D_SSD), rev),
                  pl.BlockSpec((CHUNK, D_XBC), rev),
                  pl.BlockSpec((CHUNK, D_SSD), rev),
                  const((4, D_XBC)), const((1, D_XBC)), smem, smem, smem, const((1, 1024)),
                  const((128, 1024)), const((1024, 128))],
        out_specs=[pl.BlockSpec((CHUNK, S_W), rev), const((8, D_XBC)), const((8, 1024)), const((8, 128))],
        out_shape=[jax.ShapeDtypeStruct((L, S_W), F32), jax.ShapeDtypeStruct((8, D_XBC), F32),
                   jax.ShapeDtypeStruct((8, 1024), F32), jax.ShapeDtypeStruct((8, 128), F32)],
        scratch_shapes=[pltpu.VMEM((128, 1024), F32), pltpu.VMEM((16, D_XBC), F32),
                        pltpu.VMEM((16, D_XBC), F32), pltpu.VMEM((8, D_XBC), F32)],
        args=(proj_ssd, proj_ssd, hprev_all, ypre, pre, dy, conv_w8, conv_b8, dtb8, alog8, dskip_e, norm_w, E, ET))


def _rope(t, tab):
    cos, sa, sb = tab[:, 0:128], tab[:, 128:256], tab[:, 256:384]
    outs = []
    for i in range(t.shape[1] // 128):
        tg = t[:, i * 128:(i + 1) * 128]
        outs.append(tg * cos + pltpu.roll(tg, 8, 1) * sa + pltpu.roll(tg, 120, 1) * sb)
    return jnp.concatenate(outs, axis=1)


def _rope_transposed(d, tab):
    cos, sa, sb = tab[:, 0:128], tab[:, 128:256], tab[:, 256:384]
    outs = []
    for i in range(d.shape[1] // 128):
        dg = d[:, i * 128:(i + 1) * 128]
        outs.append(dg * cos + pltpu.roll(dg * sa, 120, 1) + pltpu.roll(dg * sb, 8, 1))
    return jnp.concatenate(outs, axis=1)


def _lo_half(rows):
    return lax.broadcasted_iota(jnp.int32, (rows, 128), 1) < HEAD_DIM


def _kv_both(t, j):
    p, b = j // 2, j % 2
    lo = _lo_half(t.shape[0])
    nat = jnp.where(lo if b == 0 else jnp.logical_not(lo), t[:, p * 128:(p + 1) * 128], 0.0)
    return nat + pltpu.roll(nat, HEAD_DIM, 1)


def _stack_heads(t, j):
    lo = _lo_half(CHUNK)
    hi = jnp.logical_not(lo)
    a, b = t[:, 2 * j * 128:(2 * j + 1) * 128], t[:, (2 * j + 1) * 128:(2 * j + 2) * 128]
    return jnp.concatenate([jnp.where(lo, a, 0.0), jnp.where(hi, a, 0.0),
                            jnp.where(lo, b, 0.0), jnp.where(hi, b, 0.0)], axis=0)


def _unstack_heads(s):
    lo = _lo_half(CHUNK)
    return jnp.concatenate([jnp.where(lo, s[0:128], s[128:256]), jnp.where(lo, s[256:384], s[384:512])], axis=1)


def _fold_kv(r, j):
    lo = _lo_half(r.shape[0])
    return jnp.where(lo if j % 2 == 0 else jnp.logical_not(lo), r + pltpu.roll(r, HEAD_DIM, 1), 0.0)


def _sink_row(sink_ref, j):
    hid = lax.broadcasted_iota(jnp.int32, (1, 4 * CHUNK), 1) // CHUNK
    row = jnp.zeros((1, 4 * CHUNK), F32)
    for hh in range(4):
        row = jnp.where(hid == hh, sink_ref[4 * j + hh], row)
    return row


def _from_current(n_rows=CHUNK):
    si = lax.broadcasted_iota(jnp.int32, (n_rows, 4 * CHUNK), 0)
    qi = lax.broadcasted_iota(jnp.int32, (n_rows, 4 * CHUNK), 1) % CHUNK
    return si <= qi


def _fold(full, from_cur, pen=0.0):
    return jnp.where(from_cur, full[CHUNK:2 * CHUNK], full[0:CHUNK] + pen)


def _unfold(t, from_cur):
    c = jnp.where(from_cur, t, 0.0)
    return jnp.concatenate([t - c, c], axis=0)


def _softmax_sink(s, sink):
    mx = jnp.maximum(jnp.max(s, axis=0, keepdims=True), sink)
    p = jnp.exp(s - mx)
    esink = jnp.exp(sink - mx)
    inv = 1.0 / (jnp.sum(p, axis=0, keepdims=True) + esink)
    return p * inv, esink * inv


def _swa_inputs(blk, p_ref, prev_ref, tab_ref, ptab_ref):
    tab = tab_ref[...]
    qr = _rope(p_ref[:, A_Q:A_Q + 1024], tab) * ATT_SCALE
    kk = jnp.concatenate([_rope(prev_ref[:, 0:256], ptab_ref[...]), _rope(p_ref[:, A_K:A_K + 256], tab)], axis=0)
    vv = jnp.concatenate([prev_ref[:, 256:512], p_ref[:, A_V:A_V + 256]], axis=0)
    return tab, qr, kk, vv, jnp.where(blk > 0, 0.0, NEG)


def _swa_forward(proj_att, tabs, sinks, y):
    L = proj_att.shape[0]
    nb = L // CHUNK

    def body(sink_ref, p_ref, prev_ref, tab_ref, ptab_ref, y_in_ref, y_ref):
        n = pl.program_id(0)
        _, qr, kk, vv, pen = _swa_inputs(n, p_ref, prev_ref, tab_ref, ptab_ref)
        from_cur = _from_current()
        outs = []
        for j in range(KV_HEADS):
            s = _fold(_mm_nt(_kv_both(kk, j), _stack_heads(qr, j)), from_cur, pen)
            P, _ = _softmax_sink(s, _sink_row(sink_ref, j))
            outs.append(_unstack_heads(_mm_tn(_unfold(P, from_cur), _kv_both(vv, j))))
        g = p_ref[:, A_G:A_G + 1024]
        y_ref[...] = (jnp.concatenate(outs, axis=1) * (g * _sigmoid(g))).astype(y_ref.dtype)

    return pl.pallas_call(
        body, name="swa_fwd", grid=(nb,),
        in_specs=[pl.BlockSpec(memory_space=pltpu.SMEM),
                  pl.BlockSpec((CHUNK, A_W), lambda n: (n, 0)),
                  pl.BlockSpec((CHUNK, 512), lambda n: (jnp.maximum(n - 1, 0), 2)),
                  pl.BlockSpec((CHUNK, 384), lambda n: (n, 0)),
                  pl.BlockSpec((CHUNK, 384), lambda n: (jnp.maximum(n - 1, 0), 0)),
                  pl.BlockSpec(memory_space=pl.ANY)],
        out_specs=pl.BlockSpec((CHUNK, D_ATT), lambda n: (n, 1)),
        out_shape=jax.ShapeDtypeStruct(y.shape, y.dtype),
        input_output_aliases={5: 0},
        compiler_params=_params(("arbitrary",)),
    )(sinks, proj_att, proj_att, tabs, tabs, y)


def _swa_backward(proj_att, tabs, sinks, dy, comm=None):
    L = proj_att.shape[0]
    nb = L // CHUNK

    def body(sink_ref, p_ref, prev_ref, tab_ref, ptab_ref, dy_ref, dp_ref, dsink_ref, carry_k, carry_v):
        i = pl.program_id(0)
        n = nb - 1 - i

        @pl.when(i == 0)
        def _():
            carry_k[...] = jnp.zeros_like(carry_k)
            carry_v[...] = jnp.zeros_like(carry_v)
            dsink_ref[...] = jnp.zeros_like(dsink_ref)

        tab, qr, kk, vv, pen = _swa_inputs(n, p_ref, prev_ref, tab_ref, ptab_ref)
        from_cur = _from_current()
        g = p_ref[:, A_G:A_G + 1024]
        sgm = _sigmoid(g)
        dyv = dy_ref[...]
        do_all = dyv * (g * sgm)
        lane8 = lax.broadcasted_iota(jnp.int32, (8, 128), 1)
        hid = lax.broadcasted_iota(jnp.int32, (1, 4 * CHUNK), 1) // CHUNK
        o_parts, dq_parts = [], []
        dk_nat = [jnp.zeros((2 * CHUNK, 128), F32) for _ in range(2)]
        dv_nat = [jnp.zeros((2 * CHUNK, 128), F32) for _ in range(2)]
        dsink = jnp.zeros((8, 128), F32)
        for j in range(KV_HEADS):
            qs = _stack_heads(qr, j)
            kkb, vvb = _kv_both(kk, j), _kv_both(vv, j)
            P, psink = _softmax_sink(_fold(_mm_nt(kkb, qs), from_cur, pen), _sink_row(sink_ref, j))
            p_full = _unfold(P, from_cur)
            o_parts.append(_unstack_heads(_mm_tn(p_full, vvb)))
            do_s = _stack_heads(do_all, j)
            dP = _fold(_mm_nt(vvb, do_s), from_cur)
            D = jnp.sum(P * dP, axis=0, keepdims=True)
            ds_full = _unfold(P * (dP - D), from_cur)
            sd = psink * D
            for hh in range(4):
                dsink = dsink + jnp.where(lane8 == 4 * j + hh, -jnp.sum(jnp.where(hid == hh, sd, 0.0)), 0.0)
            dq_parts.append(_unstack_heads(_mm_tn(ds_full, kkb)) * ATT_SCALE)
            dk_nat[j // 2] = dk_nat[j // 2] + _fold_kv(_mm(ds_full, qs), j)
            dv_nat[j // 2] = dv_nat[j // 2] + _fold_kv(_mm(p_full, do_s), j)
        o = jnp.concatenate(o_parts, axis=1)
        dkk = jnp.concatenate(dk_nat, axis=1)
        dvv = jnp.concatenate(dv_nat, axis=1)
        dp_ref[:, A_Q:A_Q + 1024] = _rope_transposed(jnp.concatenate(dq_parts, axis=1), tab)
        dp_ref[:, A_K:A_K + 256] = _rope_transposed(dkk[CHUNK:2 * CHUNK] + carry_k[...], tab)
        dp_ref[:, A_V:A_V + 256] = dvv[CHUNK:2 * CHUNK] + carry_v[...]
        dp_ref[:, A_G:A_G + 1024] = dyv * o * (sgm * (1.0 + g * (1.0 - sgm)))
        carry_k[...] = dkk[0:CHUNK]
        carry_v[...] = dvv[0:CHUNK]
        dsink_ref[...] += dsink

    rev = lambda i: (nb - 1 - i, 0)
    prev = lambda i: jnp.maximum(nb - 2 - i, 0)
    return _call(
        body, comm, name="swa_bwd", grid=(nb,),
        in_specs=[pl.BlockSpec(memory_space=pltpu.SMEM),
                  pl.BlockSpec((CHUNK, A_W), rev),
                  pl.BlockSpec((CHUNK, 512), lambda i: (prev(i), 2)),
                  pl.BlockSpec((CHUNK, 384), rev),
                  pl.BlockSpec((CHUNK, 384), lambda i: (prev(i), 0)),
                  pl.BlockSpec((CHUNK, D_ATT), lambda i: (nb - 1 - i, 1))],
        out_specs=[pl.BlockSpec((CHUNK, A_W), rev), pl.BlockSpec((8, 128), lambda i: (0, 0))],
        out_shape=[jax.ShapeDtypeStruct((L, A_W), F32), jax.ShapeDtypeStruct((8, 128), F32)],
        scratch_shapes=[pltpu.VMEM((CHUNK, 256), F32), pltpu.VMEM((CHUNK, 256), F32)],
        args=(sinks, proj_att, proj_att, tabs, tabs, dy))


def _head(y, x, target, w_out, ln_g8, ln_b8, *, tm):
    L = x.shape[0]
    nsteps = L // tm

    def body(y_ref, x_ref, t_ref, wo_ref, g_ref, b_ref, dr_ref, dy_ref, acc_ref):
        i = pl.program_id(0)

        @pl.when(i == 0)
        def _():
            acc_ref[...] = jnp.zeros_like(acc_ref)

        r = ALPHA * x_ref[...] + _mm(y_ref[...], wo_ref[...])
        mu = jnp.mean(r, axis=-1, keepdims=True)
        d = r - mu
        rstd = lax.rsqrt(jnp.mean(d * d, axis=-1, keepdims=True) + LN_EPS)
        xh = d * rstd
        gam = g_ref[0:1, :]
        e = xh * gam + b_ref[0:1, :] - t_ref[...]
        dout = e * (1.0 / D_MODEL)
        dxh = dout * gam
        dr = rstd * (dxh - jnp.mean(dxh, axis=-1, keepdims=True)
                     - xh * jnp.mean(dxh * xh, axis=-1, keepdims=True))
        dr_ref[...] = dr
        dy_ref[...] = _mm_nt(dr, wo_ref[...])
        acc_ref[...] += _rows8([_colsum(dout * xh), _colsum(dout), _colsum(e * e) * (0.5 / D_MODEL)])

        @pl.when(i == nsteps - 1)
        def _():
            acc = acc_ref[...]
            tot = jnp.sum(acc[2:3, :])
            rid = lax.broadcasted_iota(jnp.int32, (8, 1024), 0)
            acc_ref[...] = jnp.where(rid == 3, tot, acc)

    const = lambda shape: pl.BlockSpec(shape, lambda i: (0, 0))
    row = lambda w: pl.BlockSpec((tm, w), lambda i: (i, 0))
    return pl.pallas_call(
        body, name="head", grid=(nsteps,),
        in_specs=[row(2048), row(1024), row(1024), const((2048, 1024)), const((1, 1024)), const((1, 1024))],
        out_specs=[row(1024), row(2048), const((8, 1024))],
        out_shape=[jax.ShapeDtypeStruct((L, D_MODEL), F32), jax.ShapeDtypeStruct((L, 2048), F32),
                   jax.ShapeDtypeStruct((8, 1024), F32)],
        compiler_params=_params(("arbitrary",)),
    )(y, x, target, w_out, ln_g8, ln_b8)


def _all_gather(shards):
    n = len(shards)
    any_spec = pl.BlockSpec(memory_space=pl.ANY)

    def body(*refs):
        ins, outs = refs[:n], refs[n:2 * n]
        send_sems, recv_sems, local_sems = refs[2 * n:]
        x, y, c = _position()
        me, sibling = (x, y, c), (x, y, 1 - c)
        chips = [(1 - x, y), (x, 1 - y), (1 - x, 1 - y)]

        def copy(a, k, block, to, src=None):
            slot = outs[a].at[_index(*block)]
            return pltpu.make_async_remote_copy(
                src_ref=slot if src is None else src, dst_ref=slot,
                send_sem=send_sems.at[a, k], recv_sem=recv_sems.at[a, k],
                device_id=to, device_id_type=MESH)

        mine = [pltpu.make_async_copy(ins[a], outs[a].at[_index(*me)], local_sems.at[a]) for a in range(n)]
        for cp in mine:
            cp.start()
        first = []
        for a in range(n):
            first.append(copy(a, 0, me, sibling, src=ins[a]))
            first += [copy(a, 1 + j, me, (*chip, c), src=ins[a]) for j, chip in enumerate(chips)]
        for cp in first:
            cp.start()
        passed = []
        for j, chip in enumerate(chips):
            for a in range(n):
                copy(a, 1 + j, (*chip, c), me).wait_recv()
                fwd = copy(a, 4 + j, (*chip, c), sibling)
                fwd.start()
                passed.append(fwd)
        for a in range(n):
            copy(a, 0, sibling, me).wait_recv()
            for j, chip in enumerate(chips):
                copy(a, 4 + j, (*chip, 1 - c), me).wait_recv()
        for cp in first + passed:
            cp.wait_send()
        for cp in mine:
            cp.wait()

    return pl.pallas_call(
        body, name="weight_all_gather",
        in_specs=[any_spec] * n, out_specs=[any_spec] * n,
        out_shape=[jax.ShapeDtypeStruct((N_DEV,) + s.shape, s.dtype) for s in shards],
        scratch_shapes=[pltpu.SemaphoreType.DMA((n, 7)), pltpu.SemaphoreType.DMA((n, 7)),
                        pltpu.SemaphoreType.DMA((n,))],
    )(*shards)


def _input_gradient(d_ssd, d_att, w_ssd, w_att, dr, *, tm, comm=None):
    L = dr.shape[0]

    def body(ds_ref, da_ref, ws_ref, wa_ref, dr_ref, o_ref):
        o_ref[...] = ALPHA * dr_ref[...] + _mm_nt(ds_ref[...], ws_ref[...]) + _mm_nt(da_ref[...], wa_ref[...])

    row = lambda w: pl.BlockSpec((tm, w), lambda i: (i, 0))
    const = lambda shape: pl.BlockSpec(shape, lambda i: (0, 0))
    return _call(body, comm, name="dx", grid=(L // tm,),
                 in_specs=[row(S_W), row(A_W), const((D_MODEL, S_W)), const((D_MODEL, A_W)), row(D_MODEL)],
                 out_specs=[row(D_MODEL)], out_shape=[jax.ShapeDtypeStruct((L, D_MODEL), F32)],
                 scratch_shapes=[], args=(d_ssd, d_att, w_ssd, w_att, dr))


SHARD_COLS = D_IN_PROJ // N_DEV
SPLIT = N_SSD_REAL - 4 * SHARD_COLS
RELAYOUT_ROWS = 256


def _unpack_w_in(w_all):
    def body(g_ref, ws_ref, wa_ref):
        for j in range(4):
            ws_ref[:, SHARD_COLS * j:SHARD_COLS * (j + 1)] = g_ref[j]
        ws_ref[:, 4 * SHARD_COLS:N_SSD_REAL] = g_ref[4, :, 0:SPLIT]
        ws_ref[:, N_SSD_REAL:S_W] = jnp.zeros((RELAYOUT_ROWS, S_W - N_SSD_REAL), ws_ref.dtype)
        wa_ref[:, 0:SHARD_COLS - SPLIT] = g_ref[4, :, SPLIT:SHARD_COLS]
        for j in range(5, N_DEV):
            lo = SHARD_COLS * (j - 4) - SPLIT
            wa_ref[:, lo:lo + SHARD_COLS] = g_ref[j]

    return pl.pallas_call(
        body, name="unpack_w_in", grid=(D_MODEL // RELAYOUT_ROWS,),
        in_specs=[pl.BlockSpec((N_DEV, RELAYOUT_ROWS, SHARD_COLS), lambda i: (0, i, 0))],
        out_specs=[pl.BlockSpec((RELAYOUT_ROWS, S_W), lambda i: (i, 0)), pl.BlockSpec((RELAYOUT_ROWS, A_W), lambda i: (i, 0))],
        out_shape=[jax.ShapeDtypeStruct((D_MODEL, S_W), w_all.dtype), jax.ShapeDtypeStruct((D_MODEL, A_W), w_all.dtype)],
        compiler_params=_params(("arbitrary",)),
    )(w_all)


def _pack_dw_in(me1, dw_ssd, dw_att, half):
    def body(me_ref, *refs):
        if half == 0:
            ds_ref, p_ref, own_ref = refs
            me = me_ref[0]

            @pl.when(me >= 4)
            def _():
                own_ref[...] = jnp.zeros_like(own_ref)
        else:
            ds_ref, da_ref, p_ref = refs

        for j in range(4):
            if half == 0:
                pieces = [(0, ds_ref[:, SHARD_COLS * j:SHARD_COLS * (j + 1)])]
            elif j == 0:
                pieces = [(0, ds_ref[:, 4 * SHARD_COLS:N_SSD_REAL]), (SPLIT, da_ref[:, 0:SHARD_COLS - SPLIT])]
            else:
                lo = SHARD_COLS * j - SPLIT
                pieces = [(0, da_ref[:, lo:lo + SHARD_COLS])]
            for off, blk in pieces:
                p_ref[j, :, off:off + blk.shape[1]] = blk.astype(p_ref.dtype)
                if half == 0:
                    @pl.when(me == j)
                    def _(off=off, blk=blk):
                        own_ref[:, off:off + blk.shape[1]] = blk

    ins = [dw_ssd] if half == 0 else [dw_ssd, dw_att]
    row = lambda a: pl.BlockSpec((RELAYOUT_ROWS, a.shape[1]), lambda i: (i, 0))
    out_specs = [pl.BlockSpec((4, RELAYOUT_ROWS, SHARD_COLS), lambda i: (0, i, 0))]
    out_shape = [jax.ShapeDtypeStruct((4, D_MODEL, SHARD_COLS), BF16 if half == 0 else F32)]
    if half == 0:
        out_specs.append(pl.BlockSpec((RELAYOUT_ROWS, SHARD_COLS), lambda i: (i, 0)))
        out_shape.append(jax.ShapeDtypeStruct((D_MODEL, SHARD_COLS), F32))
    return pl.pallas_call(
        body, name="pack_dw_in_%d" % half, grid=(D_MODEL // RELAYOUT_ROWS,),
        in_specs=[pl.BlockSpec(memory_space=pltpu.SMEM)] + [row(a) for a in ins],
        out_specs=out_specs, out_shape=out_shape, compiler_params=_params(("arbitrary",)),
    )(me1, *ins)


def _pair_swap(stack):
    def body(in_ref, out_ref, send_sems, recv_sems):
        x, y, c = _position()
        cps = [_remote(in_ref.at[2 * oy + (1 - c)], out_ref.at[oy], send_sems.at[oy], recv_sems.at[oy], (x, y, 1 - c))
               for oy in range(2)]
        for cp in cps:
            cp.start()
        for cp in cps:
            cp.wait_recv()
        for cp in cps:
            cp.wait_send()

    any_spec = pl.BlockSpec(memory_space=pl.ANY)
    return pl.pallas_call(
        body, name="pair_swap", in_specs=[any_spec], out_specs=any_spec,
        out_shape=jax.ShapeDtypeStruct((2,) + stack.shape[1:], stack.dtype),
        scratch_shapes=[pltpu.SemaphoreType.DMA((2,)), pltpu.SemaphoreType.DMA((2,))],
    )(stack)


def _pair_sum(pos3, stack, swapped, own_lo):
    def body(pos_ref, a_ref, b_ref, lo_ref, chip_ref, own_ref):
        oy = pl.program_id(1)
        t = a_ref[0] + b_ref[0]
        chip_ref[0] = t.astype(chip_ref.dtype)

        @pl.when((pos_ref[0] == 0) & (oy == 0))
        def _():
            own_ref[...] = lo_ref[...]

        @pl.when((pos_ref[0] == 1) & (oy == pos_ref[1]))
        def _():
            own_ref[...] = t

    blk = (1, RELAYOUT_ROWS, SHARD_COLS)
    flat = pl.BlockSpec((RELAYOUT_ROWS, SHARD_COLS), lambda i, oy, pos: (i, 0))
    return pl.pallas_call(
        body, name="pair_sum",
        grid_spec=pltpu.PrefetchScalarGridSpec(
            num_scalar_prefetch=1, grid=(D_MODEL // RELAYOUT_ROWS, 2),
            in_specs=[pl.BlockSpec(blk, lambda i, oy, pos: (2 * oy + pos[2], i, 0)),
                      pl.BlockSpec(blk, lambda i, oy, pos: (oy, i, 0)), flat],
            out_specs=[pl.BlockSpec(blk, lambda i, oy, pos: (oy, i, 0)), flat]),
        out_shape=[jax.ShapeDtypeStruct((2, D_MODEL, SHARD_COLS), BF16), jax.ShapeDtypeStruct((D_MODEL, SHARD_COLS), F32)],
        compiler_params=_params(("arbitrary", "arbitrary")),
    )(pos3, stack, swapped, own_lo)


def _adamw_math(w, g, m, v):
    m = ADAM_B1 * m + (1.0 - ADAM_B1) * g
    v = ADAM_B2 * v + (1.0 - ADAM_B2) * (g * g)
    m_hat = m / (1.0 - ADAM_B1 ** ADAM_STEP)
    v_hat = v / (1.0 - ADAM_B2 ** ADAM_STEP)
    delta = -ADAM_LR * (m_hat / (jnp.sqrt(v_hat) + ADAM_EPS) + ADAM_WD * w)
    return delta, m, v


def _adamw_shard(n_recv, g_own, recv, w, m, v, *, rows, name):
    R, C = g_own.shape

    def body(n_ref, g_ref, r_ref, w_ref, m_ref, v_ref, go_ref, d_ref, mo_ref, vo_ref):
        g = g_ref[...]
        for k in range(N_DEV - 1):
            g = g + jnp.where(k < n_ref[0], r_ref[k].astype(F32), 0.0)
        d, mn, vn = _adamw_math(w_ref[...], g, m_ref[...], v_ref[...])
        go_ref[...] = g
        d_ref[...] = d
        mo_ref[...] = mn
        vo_ref[...] = vn

    blk = pl.BlockSpec((rows, C), lambda i: (i, 0))
    return pl.pallas_call(
        body, name=name, grid=(R // rows,),
        in_specs=[pl.BlockSpec(memory_space=pltpu.SMEM), blk,
                  pl.BlockSpec((N_DEV - 1, rows, C), lambda i: (0, i, 0)), blk, blk, blk],
        out_specs=[blk] * 4, out_shape=[jax.ShapeDtypeStruct((R, C), F32)] * 4,
        compiler_params=_params(("arbitrary",)),
    )(n_recv, g_own, recv, w, m, v)


def _minor_rows_view(a):
    return jnp.transpose(a, (2, 0, 1)).reshape(SHARD_COLS * 8, 128)


def _from_minor_rows_view(v):
    return jnp.transpose(v.reshape(SHARD_COLS, 8, 128), (1, 2, 0)).reshape(1, D_MODEL, SHARD_COLS)


def _adamw_w_in(n_recv, g_own, recv, w, m, v):
    C = SHARD_COLS
    pad = -C % 128

    def body(n_ref, g_ref, r_ref, w_ref, m_ref, v_ref, go_ref, d_ref, mo_ref, vo_ref):
        for q in range(D_MODEL // 128):
            band = pl.ds(q * 128, 128)
            g = g_ref[band, :]
            for k in range(N_DEV - 1):
                g = g + jnp.where(k < n_ref[0], r_ref[k, band, :].astype(F32), 0.0)
            g = jnp.pad(g, ((0, 0), (0, pad))).T[0:C]
            rows = pl.ds(q, C, stride=8)
            d, mn, vn = _adamw_math(w_ref[rows, :], g, m_ref[rows, :], v_ref[rows, :])
            go_ref[rows, :] = g
            d_ref[rows, :] = d
            mo_ref[rows, :] = mn
            vo_ref[rows, :] = vn

    return pl.pallas_call(
        body, name="adamw_w_in", out_shape=[jax.ShapeDtypeStruct(w.shape, F32)] * 4,
        in_specs=[pl.BlockSpec(memory_space=pltpu.SMEM)] + [pl.BlockSpec(memory_space=pltpu.VMEM)] * 5,
        out_specs=[pl.BlockSpec(memory_space=pltpu.VMEM)] * 4,
        compiler_params=_params(),
    )(n_recv, g_own, recv, w, m, v)


SMALL = ("conv_b", "dt_bias", "a_log", "d_skip", "ssd_norm_w", "attn_sinks", "ln_g", "ln_b")


def _adamw_small(gathered, params):
    n_p = len(SMALL)

    def body(*refs):
        acc = []
        for r in refs[:5]:
            t = r[0]
            for k in range(1, N_DEV):
                t = t + r[k]
            acc.append(t)
        head, conv, norm, scal, sink = acc
        grads = dict(conv_b=conv[4:5, :], dt_bias=scal[0:1, 0:N_HEADS], a_log=scal[1:2, 0:N_HEADS],
                     d_skip=scal[2:3, 0:N_HEADS], ssd_norm_w=norm[0:1, :], attn_sinks=sink[0:1, 0:N_HEADS],
                     ln_g=head[0:1, :], ln_b=head[1:2, :])
        wmv = refs[5:5 + 3 * n_p]
        outs = refs[5 + 3 * n_p:]
        outs[0][...] = head[3:4, 0:1]
        outs[1][...] = conv[0:4, :]
        for i, name in enumerate(SMALL):
            w_ref, m_ref, v_ref = wmv[3 * i:3 * i + 3]
            g = grads[name]
            d, mn, vn = _adamw_math(w_ref[...], g, m_ref[...], v_ref[...])
            for o_ref, val in zip(outs[2 + 4 * i:6 + 4 * i], (g, d, mn, vn)):
                o_ref[...] = val

    flat = [a for name in SMALL for a in params[name]]
    out_shape = [jax.ShapeDtypeStruct((1, 1), F32), jax.ShapeDtypeStruct((4, D_XBC), F32)]
    for name in SMALL:
        out_shape += [jax.ShapeDtypeStruct(params[name][0].shape, F32)] * 4
    res = pl.pallas_call(body, name="adamw_small", out_shape=out_shape, compiler_params=_params())(*gathered, *flat)
    return res[0], res[1], {name: res[2 + 4 * i:6 + 4 * i] for i, name in enumerate(SMALL)}


def _adamw_plain(g, w, m, v):
    def body(g_ref, w_ref, m_ref, v_ref, d_ref, mo_ref, vo_ref):
        d, mn, vn = _adamw_math(w_ref[...], g_ref[...], m_ref[...], v_ref[...])
        d_ref[...] = d
        mo_ref[...] = mn
        vo_ref[...] = vn

    return pl.pallas_call(
        body, name="adamw_conv_w", out_shape=[jax.ShapeDtypeStruct(w.shape, F32)] * 3,
        compiler_params=_params(),
    )(g, w, m, v)


def _lane_pattern(fn):
    return np.asarray([fn(l % HEAD_DIM) for l in range(128)], np.float32)


ROPE_INV = _lane_pattern(lambda r: ROPE_THETA ** (-2.0 * (r % 8) / ROPE_DIM) if r < ROPE_DIM else 0.0)
ROPE_SIN_A = _lane_pattern(lambda r: 1.0 if 8 <= r < ROPE_DIM else 0.0)
ROPE_SIN_B = _lane_pattern(lambda r: -1.0 if r < 8 else 0.0)


def _rope_tables(positions):
    ang = positions.astype(F32)[:, None] * ROPE_INV[None, :]
    sn = jnp.sin(ang)
    return jnp.concatenate([jnp.cos(ang), sn * ROPE_SIN_A[None, :], sn * ROPE_SIN_B[None, :]], axis=1)


def _expansion():
    E = np.arange(1024)[None, :] // HEAD_DIM == np.arange(128)[:, None]
    return jnp.asarray(E, BF16), jnp.asarray(E.T, BF16)


def _ssd_args(conv_w, conv_b, dt_bias, a_log, d_skip, norm_w, E):
    return (conv_w, conv_b, dt_bias.reshape(-1), a_log.reshape(-1), d_skip.reshape(-1), norm_w, E)


def kernel(x, positions, w_in, conv_w, conv_b, dt_bias, a_log, d_skip, ssd_norm_w, attn_sinks, w_out, ln_g, ln_b, loss_target, m_w_in, m_conv_w, m_conv_b, m_dt_bias, m_a_log, m_d_skip, m_ssd_norm_w, m_attn_sinks, m_w_out, m_ln_g, m_ln_b, v_w_in, v_conv_w, v_conv_b, v_dt_bias, v_a_log, v_d_skip, v_ssd_norm_w, v_attn_sinks, v_w_out, v_ln_g, v_ln_b):
    me = _index(*_position())
    me1 = me.reshape(1).astype(jnp.int32)
    x0, target = x[0], loss_target[0]
    bf16_shard = lambda shape: jax.ShapeDtypeStruct(shape, BF16)
    E, ET = _expansion()
    tabs = _rope_tables(positions[0])
    sinks = attn_sinks.reshape(-1)

    w_in_all, conv_w_all = _all_gather([w_in[0].astype(BF16), conv_w[0]])
    w_ssd, w_att = _unpack_w_in(w_in_all)
    conv_w_f = jnp.transpose(conv_w_all, (1, 0, 2)).reshape(4, D_XBC)
    ssd_args = _ssd_args(conv_w_f, conv_b, dt_bias, a_log, d_skip, ssd_norm_w, E)

    proj_ssd, xb = _matmul(x0, w_ssd, tm=1024, tn=S_W // 2, name="in_proj_ssd", emit_a=True)
    proj_att = _matmul(xb, w_att, tm=1024, tn=A_W // 2, name="in_proj_att")
    gather_w_out = _Hosted([w_out[0].astype(BF16)], [bf16_shard((N_DEV, 256, D_MODEL))], [_Flow("gather", 0, 0)])
    y, ypre, hprev, pre, w_out_all = _ssd_forward(proj_ssd, *ssd_args, comm=gather_w_out)
    w_out_f = w_out_all.reshape(2 * D_MODEL, D_MODEL)
    y = _swa_forward(proj_att, tabs, sinks, y)
    dr, dy, acc_head = _head(y, x0, target, w_out_f, ln_g, ln_b, tm=512)

    dw_out, dw_out_bf16 = _matmul_tn(y, dr, tl=512, tn=D_MODEL, name="dw_out", emit_bf16=True)
    own_out = lax.dynamic_index_in_dim(dw_out.reshape(N_DEV, 256, D_MODEL), me, axis=0, keepdims=False)
    send_out = _Hosted([dw_out_bf16.reshape(N_DEV, 256, D_MODEL)], [bf16_shard((N_DEV - 1, 256, D_MODEL))],
                       [_Flow("exchange", 0, 0)])
    d_ssd, acc_cw, acc_w, acc_s, recv_out = _ssd_backward(proj_ssd, hprev, ypre, pre, dy, *ssd_args, ET, comm=send_out)
    dw_ssd = _matmul_tn(xb, d_ssd, tl=512, tn=S_W // 2, name="dw_in_ssd")
    parts_lo, own_lo = _pack_dw_in(me1, dw_ssd, None, 0)
    recv_shape = bf16_shard((N_DEV - 1, D_MODEL, SHARD_COLS))
    send_lo = _Hosted([parts_lo], [recv_shape], [_Flow("exchange", 0, 0, target_x=0)])
    d_att, dsink, recv_in = _swa_backward(proj_att, tabs, sinks, dy, comm=send_lo)
    dw_att = _matmul_tn(xb, d_att, tl=512, tn=A_W // 2, name="dw_in_att")
    (stack_hi,) = _pack_dw_in(me1, dw_ssd, dw_att, 1)
    pos3 = jnp.stack(_position()).astype(jnp.int32)
    chip_hi, own_in = _pair_sum(pos3, stack_hi, _pair_swap(stack_hi), own_lo)
    accs = [acc_head, acc_cw, acc_w, acc_s, dsink]
    send_hi = _Hosted([chip_hi, recv_in] + accs,
                      [recv_shape] + [jax.ShapeDtypeStruct((N_DEV,) + a.shape, F32) for a in accs],
                      [_Flow("chip_exchange", 0, 0, target_x=1)] + [_Flow("gather", 2 + i, 1 + i) for i in range(5)],
                      aliases={1: 0})
    dx, recv_in, *gathered = _input_gradient(d_ssd, d_att, w_ssd, w_att, dr, tm=256, comm=send_hi)
    n_recv_in = jnp.where(me < 4, N_DEV - 1, 3).reshape(1).astype(jnp.int32)
    n_recv_out = jnp.full((1,), N_DEV - 1, jnp.int32)

    g_in, d_in, nm_in, nv_in = [_from_minor_rows_view(r) for r in _adamw_w_in(
        n_recv_in, own_in, recv_in, _minor_rows_view(w_in), _minor_rows_view(m_w_in), _minor_rows_view(v_w_in))]
    g_out, d_out, nm_out, nv_out = _adamw_shard(n_recv_out, own_out, recv_out, w_out[0], m_w_out[0], v_w_out[0],
                                                rows=256, name="adamw_w_out")
    loss, g_conv_w, small = _adamw_small(gathered, dict(
        conv_b=(conv_b, m_conv_b, v_conv_b), dt_bias=(dt_bias, m_dt_bias, v_dt_bias), a_log=(a_log, m_a_log, v_a_log),
        d_skip=(d_skip, m_d_skip, v_d_skip), ssd_norm_w=(ssd_norm_w, m_ssd_norm_w, v_ssd_norm_w),
        attn_sinks=(attn_sinks, m_attn_sinks, v_attn_sinks), ln_g=(ln_g, m_ln_g, v_ln_g), ln_b=(ln_b, m_ln_b, v_ln_b)))
    g_cw = lax.dynamic_slice_in_dim(g_conv_w, me * (D_XBC // N_DEV), D_XBC // N_DEV, axis=1)
    d_cw, nm_cw, nv_cw = _adamw_plain(g_cw, conv_w[0], m_conv_w[0], v_conv_w[0])

    def leaves(i, big_in, cw, big_out):
        mid = [small[k][i] for k in ("conv_b", "dt_bias", "a_log", "d_skip", "ssd_norm_w", "attn_sinks")]
        return [big_in, cw[None]] + mid + [big_out[None], small["ln_g"][i], small["ln_b"][i]]

    return (loss.reshape(()), dx[None], *leaves(0, g_in, g_cw, g_out), *leaves(1, d_in, d_cw, d_out),
            *leaves(2, nm_in, nm_cw, nm_out), *leaves(3, nv_in, nv_cw, nv_out))
```

```python
import jax
import jax.numpy as jnp
from jax import lax
from jax.experimental import pallas as pl
from jax.experimental.pallas import tpu as pltpu
import numpy as np

F32 = jnp.float32
BF16 = jnp.bfloat16
_MXU = jnp.bfloat16

N_DEV = 8
D_MODEL = 1024
D_SSD = 1024
D_ATT = 1024
HEAD_DIM = 64
N_HEADS = 16
SSD_GROUPS = 2
KV_HEADS = 4
CHUNK = 128
D_XBC = 1536
D_IN_PROJ = 5136
ROPE_DIM = 16
ROPE_THETA = 500000.0
ALPHA = (2.0 * 1) ** 0.25
LN_EPS = 1e-5
RMS_EPS = 1e-5
ATT_SCALE = HEAD_DIM ** -0.5
NEG = -1e30

S_Z, S_XS, S_B, S_C, S_DT, S_W = 0, 1024, 2048, 2304, 2560, 2816
N_SSD_REAL = 2576
A_Q, A_K, A_V, A_G, A_W = 0, 1024, 1280, 1536, 2560

ADAM_LR = 0.001
ADAM_B1 = 0.9
ADAM_B2 = 0.999
ADAM_EPS = 1e-08
ADAM_WD = 0.01
ADAM_STEP = 10

VMEM_LIMIT = 48 * 1024 * 1024
MESH = pl.DeviceIdType.MESH


def _params(sem=None):
    return pltpu.CompilerParams(dimension_semantics=sem, vmem_limit_bytes=VMEM_LIMIT)


def _mm(a, b):
    return jnp.dot(a.astype(_MXU), b.astype(_MXU), preferred_element_type=F32)


def _mm_nt(a, b):
    return lax.dot_general(a.astype(_MXU), b.astype(_MXU), (((1,), (1,)), ((), ())),
                           preferred_element_type=F32)


def _mm_tn(a, b):
    return lax.dot_general(a.astype(_MXU), b.astype(_MXU), (((0,), (0,)), ((), ())),
                           preferred_element_type=F32)


def _split3(v):
    hi = v.astype(BF16)
    r = v - hi.astype(F32)
    mid = r.astype(BF16)
    lo = (r - mid.astype(F32)).astype(BF16)
    return hi, mid, lo


def _mm_exact_r(v, p01):
    hi, mid, lo = _split3(v)
    d = lambda a: jnp.dot(a, p01, preferred_element_type=F32)
    return d(hi) + d(mid) + d(lo)


def _mm_exact_l(p01, v):
    hi, mid, lo = _split3(v)
    d = lambda a: jnp.dot(p01, a, preferred_element_type=F32)
    return d(hi) + d(mid) + d(lo)


def _sigmoid(x):
    return 1.0 / (1.0 + jnp.exp(-x))


def _softplus(x):
    e = jnp.exp(-jnp.abs(x))
    u = 1.0 + e
    log1p = jnp.where(u == 1.0, e, jnp.log(u) * (e / (u - 1.0)))
    return jnp.maximum(x, 0.0) + log1p


def _rows8(rows):
    n = rows[0].shape[1]
    rid = lax.broadcasted_iota(jnp.int32, (8, n), 0)
    out = jnp.zeros((8, n), F32)
    for k, r in enumerate(rows):
        out = out + jnp.where(rid == k, r, 0.0)
    return out


def _colsum(a):
    return jnp.sum(a, axis=0, keepdims=True)


def _matmul(a, b, *, tm, tn, name, emit_a=False, comm=None):
    M, K = a.shape
    N = b.shape[1]

    def body(a_ref, b_ref, o_ref, *rest):
        am = a_ref[...].astype(_MXU)
        o_ref[...] = jnp.dot(am, b_ref[...].astype(_MXU), preferred_element_type=F32)
        if emit_a:
            rest[0][...] = am

    out_specs = [pl.BlockSpec((tm, tn), lambda i, j: (i, j))]
    out_shape = [jax.ShapeDtypeStruct((M, N), F32)]
    if emit_a:
        out_specs.append(pl.BlockSpec((tm, K), lambda i, j: (i, 0)))
        out_shape.append(jax.ShapeDtypeStruct((M, K), _MXU))
    res = _call(
        body, comm, name=name, grid=(M // tm, N // tn),
        in_specs=[pl.BlockSpec((tm, K), lambda i, j: (i, 0)), pl.BlockSpec((K, tn), lambda i, j: (0, j))],
        out_specs=out_specs, out_shape=out_shape, scratch_shapes=[], args=(a, b))
    return res if (emit_a or comm is not None) else res[0]


def _matmul_tn(a, g, *, tl, tn, name, emit_bf16=False):
    L, M = a.shape
    N = g.shape[1]
    last = L // tl - 1

    def body(a_ref, g_ref, o_ref, *rest):
        @pl.when(pl.program_id(1) == 0)
        def _():
            o_ref[...] = jnp.zeros_like(o_ref)

        o_ref[...] += _mm_tn(a_ref[...], g_ref[...])
        if emit_bf16:
            @pl.when(pl.program_id(1) == last)
            def _():
                rest[0][...] = o_ref[...].astype(BF16)

    spec = pl.BlockSpec((M, tn), lambda j, l: (0, j))
    res = pl.pallas_call(
        body, name=name, grid=(N // tn, L // tl),
        in_specs=[pl.BlockSpec((tl, M), lambda j, l: (l, 0)), pl.BlockSpec((tl, tn), lambda j, l: (l, j))],
        out_specs=[spec, spec] if emit_bf16 else [spec],
        out_shape=[jax.ShapeDtypeStruct((M, N), F32)] + ([jax.ShapeDtypeStruct((M, N), BF16)] if emit_bf16 else []),
        compiler_params=_params(("arbitrary", "arbitrary")),
    )(a, g)
    return res if emit_bf16 else res[0]


def _position():
    return lax.axis_index("x"), lax.axis_index("y"), lax.axis_index("c")


def _index(px, py, pc):
    return 4 * px + 2 * py + pc


def _flip(pos, k):
    x, y, c = pos
    return ((1 - x) if (k >> 2) & 1 else x, (1 - y) if (k >> 1) & 1 else y, (1 - c) if k & 1 else c)


def _when(cond, fn):
    if cond is True:
        fn()
    else:
        pl.when(cond)(fn)


def _remote(src, dst, send_sem, recv_sem, peer):
    return pltpu.make_async_remote_copy(src_ref=src, dst_ref=dst, send_sem=send_sem, recv_sem=recv_sem,
                                        device_id=peer, device_id_type=MESH)


class _Flow:
    def __init__(self, kind, operand, result, target_x=None):
        self.kind, self.operand, self.result, self.target_x = kind, operand, result, target_x


class _Hosted:
    def __init__(self, operands, out_shapes, flows, aliases=None):
        self.operands, self.out_shapes, self.flows = operands, out_shapes, flows
        self.aliases = aliases or {}

    def plan(self, ins, outs, send_sems, recv_sems, local_sems):
        me = _position()
        mi = _index(*me)
        sends, recvs, locals_ = [], [], []
        for row, f in enumerate(self.flows):
            src, dst = ins[f.operand], outs[f.result]
            for k in range(1, N_DEV):
                peer = _flip(me, k)
                sems = (send_sems.at[row, k - 1], recv_sems.at[row, k - 1])
                if f.kind == "exchange":
                    owner = _index(*peer) if f.target_x is None else 2 * peer[1] + peer[2]
                    cp = _remote(src.at[owner], dst.at[k - 1], *sems, peer)
                    to_peer = True if f.target_x is None else peer[0] == f.target_x
                    to_me = True if f.target_x is None else me[0] == f.target_x
                    sends.append((to_peer, cp))
                    recvs.append((to_me, cp))
                elif f.kind == "chip_exchange":
                    if k & 1:
                        continue
                    cp = _remote(src.at[peer[1]], dst.at[k // 2 - 1], *sems, peer)
                    sends.append((peer[0] == f.target_x, cp))
                    recvs.append((me[0] == f.target_x, cp))
                else:
                    sends.append((True, _remote(src, dst.at[mi], *sems, peer)))
                    recvs.append((True, _remote(src, dst.at[_index(*peer)], *sems, peer)))
            if f.kind == "gather":
                locals_.append(pltpu.make_async_copy(src, dst.at[mi], local_sems.at[row]))

        def start():
            for cp in locals_:
                cp.start()
            for cond, cp in sends:
                _when(cond, cp.start)

        def wait():
            for cond, cp in recvs:
                _when(cond, cp.wait_recv)
            for cond, cp in sends:
                _when(cond, cp.wait_send)
            for cp in locals_:
                cp.wait()

        return start, wait


def _call(body, comm, *, name, grid, in_specs, out_specs, out_shape, scratch_shapes, args, aliases=None):
    io_alias = dict(aliases or {})
    semantics = ("arbitrary",) * len(grid)
    if comm is None:
        return pl.pallas_call(body, name=name, grid=grid, in_specs=in_specs, out_specs=out_specs, out_shape=out_shape,
                              scratch_shapes=scratch_shapes, input_output_aliases=io_alias,
                              compiler_params=_params(semantics))(*args)
    n_in, n_out, n_scr = len(args), len(out_shape), len(scratch_shapes)
    c_in, c_out, rows = len(comm.operands), len(comm.out_shapes), len(comm.flows)

    def hosted(*refs):
        ins, refs = refs[:n_in], refs[n_in:]
        cins, refs = refs[:c_in], refs[c_in:]
        outs, refs = refs[:n_out], refs[n_out:]
        couts, refs = refs[:c_out], refs[c_out:]
        scr, (send_sems, recv_sems, local_sems) = refs[:n_scr], refs[n_scr:]
        start, wait = comm.plan(cins, couts, send_sems, recv_sems, local_sems)
        ids = [pl.program_id(d) for d in range(len(grid))]
        first, last = ids[0] == 0, ids[0] == grid[0] - 1
        for d in range(1, len(grid)):
            first, last = first & (ids[d] == 0), last & (ids[d] == grid[d] - 1)
        pl.when(first)(start)
        body(*ins, *outs, *scr)
        pl.when(last)(wait)

    for ci, co in comm.aliases.items():
        io_alias[n_in + ci] = n_out + co
    any_spec = pl.BlockSpec(memory_space=pl.ANY)
    sems = [pltpu.SemaphoreType.DMA((rows, N_DEV - 1)), pltpu.SemaphoreType.DMA((rows, N_DEV - 1)),
            pltpu.SemaphoreType.DMA((rows,))]
    return pl.pallas_call(
        hosted, name=name, grid=grid, in_specs=list(in_specs) + [any_spec] * c_in,
        out_specs=list(out_specs) + [any_spec] * c_out, out_shape=list(out_shape) + list(comm.out_shapes),
        scratch_shapes=list(scratch_shapes) + sems, input_output_aliases=io_alias,
        compiler_params=_params(semantics))(*args, *comm.operands)


def _head_row(ref, width, rep):
    hid = lax.broadcasted_iota(jnp.int32, (1, width), 1) // rep
    row = jnp.zeros((1, width), F32)
    for h in range(N_HEADS):
        row = jnp.where(hid == h, ref[h], row)
    return row


def _rows_from_above(u_b, s, ext_scr, row, col):
    down = (row - col == s).astype(_MXU)
    return jnp.concatenate([ext_scr[8 - s:16 - s, :], jnp.dot(down, u_b, preferred_element_type=F32)[8:128]], axis=0)


def _ssd_recompute(first, p_ref, halo_ref, cw_ref, cb_ref, dtb_ref, alog_ref, e_ref, ext_scr, pre=None):
    row = lax.broadcasted_iota(jnp.int32, (128, 128), 0)
    col = lax.broadcasted_iota(jnp.int32, (128, 128), 1)
    ext_scr[0:8, :] = jnp.where(first, 0.0, halo_ref[:, S_XS:S_DT])
    if pre is not None:
        ext_scr[8:16, :] = p_ref[0:8, S_XS:S_DT]
    else:
        ext_scr[8:136, :] = p_ref[:, S_XS:S_DT]
        cw = cw_ref[...]
        pre = (cb_ref[0:1, :] + cw[3:4, :] * ext_scr[8:136, :] + cw[2:3, :] * ext_scr[7:135, :]
               + cw[1:2, :] * ext_scr[6:134, :] + cw[0:1, :] * ext_scr[5:133, :])
    sg = _sigmoid(pre)
    act = pre * sg
    lane = lax.broadcasted_iota(jnp.int32, (1, 128), 1)
    A = jnp.where(lane < N_HEADS, -jnp.exp(_head_row(alog_ref, 128, 1)), 0.0)
    raw = p_ref[:, S_DT:S_DT + 128] + _head_row(dtb_ref, 128, 1)
    dt = _softplus(raw)
    dA = dt * A
    tril = (row >= col).astype(BF16)
    acs = _mm_exact_l(tril, dA)
    last = acs[127:128, :]
    ds = jnp.exp(last - acs)
    eo = jnp.exp(acs)
    E = e_ref[...]
    ex = _mm_exact_r(jnp.concatenate([dt, ds, eo], axis=0), E)
    dt_e, ds_e, eo_e = ex[0:128], ex[128:256], ex[256:384]
    xs_c = act[:, 0:1024]
    X = xs_c * dt_e
    return dict(pre=pre, sg=sg, xs_c=xs_c, Bc=act[:, 1024:1280], Cc=act[:, 1280:1536], A=A, raw=raw, dt=dt,
                acs=acs, acsT=acs.T, eo_e=eo_e, ds_e=ds_e, dt_e=dt_e, cd_e=eo_e[127:128, :],
                X=X, Xd=X * ds_e, row=row, col=col)


def _split_halves(t):
    lo = _lo_half(CHUNK)
    return jnp.concatenate([jnp.where(lo, t, 0.0), jnp.where(lo, 0.0, t)], axis=0)


def _ssd_core(R, hprev):
    causal = R["row"] >= R["col"]
    acs, acsT, X = R["acs"], R["acsT"], R["X"]
    ydiag, yoff, snew = [], [], []
    for g in range(SSD_GROUPS):
        Bg = R["Bc"][:, g * 128:(g + 1) * 128]
        Cg = R["Cc"][:, g * 128:(g + 1) * 128]
        cols = slice(g * 512, (g + 1) * 512)
        CB = _mm_nt(Cg, Bg)
        snew.append(_mm_tn(Bg, R["Xd"][:, cols]))
        yoff.append(_mm(Cg, hprev[:, cols]))
        for j in range(4):
            h0 = g * 8 + 2 * j
            ms = [CB * jnp.exp(jnp.where(causal, acs[:, h:h + 1] - acsT[h:h + 1, :], NEG)) for h in (h0, h0 + 1)]
            ydiag.append(_mm(jnp.concatenate(ms, axis=1), _split_halves(X[:, h0 * HEAD_DIM:h0 * HEAD_DIM + 128])))
    Y = jnp.concatenate(ydiag, axis=1) + jnp.concatenate(yoff, axis=1) * R["eo_e"]
    return Y, jnp.concatenate(snew, axis=1)


def _ssd_forward(proj_ssd, conv_w8, conv_b8, dtb8, alog8, dskip_e, norm_w, E, comm=None):
    L = proj_ssd.shape[0]
    nc = L // CHUNK

    def body(p_ref, halo_ref, cw_ref, cb_ref, dtb_ref, alog_ref, dsk_ref, nw_ref, e_ref,
             y_ref, ypre_ref, hprev_ref, pre_ref, h_scr, ext_scr):
        c = pl.program_id(0)
        first = c == 0

        @pl.when(first)
        def _():
            h_scr[...] = jnp.zeros_like(h_scr)

        R = _ssd_recompute(first, p_ref, halo_ref, cw_ref, cb_ref, dtb_ref, alog_ref, e_ref, ext_scr)
        hprev = h_scr[...]
        hprev_ref[...] = hprev
        pre_ref[...] = R["pre"]
        Y, snew = _ssd_core(R, hprev)
        h_scr[...] = hprev * R["cd_e"] + snew
        Y = Y + _head_row(dsk_ref, D_SSD, HEAD_DIM) * R["xs_c"]
        ypre_ref[...] = Y
        z = p_ref[:, S_Z:S_Z + 1024]
        yf = Y * (z * _sigmoid(z))
        outs = []
        for g in range(SSD_GROUPS):
            yg = yf[:, g * 512:(g + 1) * 512]
            r = lax.rsqrt(jnp.mean(yg * yg, axis=-1, keepdims=True) + RMS_EPS)
            outs.append(yg * r)
        y_ref[...] = (jnp.concatenate(outs, axis=1) * nw_ref[0:1, :]).astype(y_ref.dtype)

    const = lambda shape: pl.BlockSpec(shape, lambda c: (0, 0))
    smem = pl.BlockSpec(memory_space=pltpu.SMEM)
    return _call(
        body, comm, name="ssd_fwd", grid=(nc,),
        in_specs=[pl.BlockSpec((CHUNK, S_W), lambda c: (c, 0)),
                  pl.BlockSpec((8, S_W), lambda c: (jnp.maximum(c * 16 - 1, 0), 0)),
                  const((4, D_XBC)), const((1, D_XBC)), smem, smem, smem, const((1, 1024)), const((128, 1024))],
        out_specs=[pl.BlockSpec((CHUNK, D_SSD), lambda c: (c, 0)), pl.BlockSpec((CHUNK, D_SSD), lambda c: (c, 0)),
                   pl.BlockSpec((128, 1024), lambda c: (c, 0)), pl.BlockSpec((CHUNK, D_XBC), lambda c: (c, 0))],
        out_shape=[jax.ShapeDtypeStruct((L, D_SSD + D_ATT), _MXU), jax.ShapeDtypeStruct((L, D_SSD), F32),
                   jax.ShapeDtypeStruct((nc * 128, 1024), F32), jax.ShapeDtypeStruct((L, D_XBC), F32)],
        scratch_shapes=[pltpu.VMEM((128, 1024), F32), pltpu.VMEM((136, D_XBC), F32)],
        args=(proj_ssd, proj_ssd, conv_w8, conv_b8, dtb8, alog8, dskip_e, norm_w, E))


def _ssd_backward(proj_ssd, hprev_all, ypre, pre, dy, conv_w8, conv_b8, dtb8, alog8, dskip_e, norm_w, E, ET, comm=None):
    L = proj_ssd.shape[0]
    nc = L // CHUNK

    def body(p_ref, halo_ref, hprev_ref, ypre_ref, pre_ref, dy_ref, cw_ref, cb_ref, dtb_ref, alog_ref, dsk_ref, nw_ref, e_ref,
             et_ref, dp_ref, acc_cw_ref, acc_w_ref, acc_s_ref, dh_scr, ext_scr, ext2_scr, nxt_scr):
        i = pl.program_id(0)
        c = nc - 1 - i
        first = c == 0

        @pl.when(i == 0)
        def _():
            dh_scr[...] = jnp.zeros_like(dh_scr)
            nxt_scr[...] = jnp.zeros_like(nxt_scr)
            acc_cw_ref[...] = jnp.zeros_like(acc_cw_ref)
            acc_w_ref[...] = jnp.zeros_like(acc_w_ref)
            acc_s_ref[...] = jnp.zeros_like(acc_s_ref)

        R = _ssd_recompute(first, p_ref, halo_ref, cw_ref, cb_ref, dtb_ref, alog_ref, e_ref, ext_scr, pre_ref[...])
        hprev = hprev_ref[...]
        xs_c, X, Xd = R["xs_c"], R["X"], R["Xd"]
        acs, acsT = R["acs"], R["acsT"]
        ET = et_ref[...]
        dsk = _head_row(dsk_ref, D_SSD, HEAD_DIM)
        Y = ypre_ref[...]

        z = p_ref[:, S_Z:S_Z + 1024]
        sz = _sigmoid(z)
        silz = z * sz
        yf = Y * silz
        dyv = dy_ref[...]
        nw = nw_ref[0:1, :]
        dyf_parts, dnw_parts = [], []
        for g in range(SSD_GROUPS):
            cols = slice(g * 512, (g + 1) * 512)
            yg = yf[:, cols]
            r = lax.rsqrt(jnp.mean(yg * yg, axis=-1, keepdims=True) + RMS_EPS)
            yn = yg * r
            dyn = dyv[:, cols] * nw[:, cols]
            dnw_parts.append(_colsum(dyv[:, cols] * yn))
            dyf_parts.append(r * (dyn - yn * jnp.mean(dyn * yn, axis=-1, keepdims=True)))
        dyf = jnp.concatenate(dyf_parts, axis=1)
        dY = dyf * silz
        dz = dyf * Y * (sz * (1.0 + z * (1.0 - sz)))

        dhn = dh_scr[...]
        dYo = dY * R["eo_e"]
        causal = R["row"] >= R["col"]
        dacs = jnp.zeros((128, 128), F32)
        dacs_t = jnp.zeros((128, 128), F32)
        dxdiag, dxd, dhprev, dBs, dCs, yoff = [], [], [], [], [], []
        for g in range(SSD_GROUPS):
            Bg = R["Bc"][:, g * 128:(g + 1) * 128]
            Cg = R["Cc"][:, g * 128:(g + 1) * 128]
            cols = slice(g * 512, (g + 1) * 512)
            CB = _mm_nt(Cg, Bg)
            dCB = jnp.zeros((128, 128), F32)
            for j in range(4):
                h0 = g * 8 + 2 * j
                pc = slice(h0 * HEAD_DIM, h0 * HEAD_DIM + 128)
                dYst = _split_halves(dY[:, pc])
                dMst = _mm_nt(dYst, X[:, pc])
                mts = []
                for a, h in enumerate((h0, h0 + 1)):
                    acol = acs[:, h:h + 1]
                    arow = acsT[h:h + 1, :]
                    Lm = jnp.exp(jnp.where(causal, acol - arow, NEG))
                    M = CB * Lm
                    dM = dMst[a * 128:(a + 1) * 128]
                    dCB = dCB + dM * Lm
                    G = dM * M
                    dacs = dacs + jnp.where(R["col"] == h, jnp.sum(G, axis=1, keepdims=True), 0.0)
                    dacs_t = dacs_t + jnp.where(R["row"] == h, jnp.sum(G, axis=0, keepdims=True), 0.0)
                    mts.append(M.T)
                dxdiag.append(_mm(jnp.concatenate(mts, axis=1), dYst))
            dS = dhn[:, cols]
            dxd.append(_mm(Bg, dS))
            yoff.append(_mm(Cg, hprev[:, cols]))
            dhprev.append(_mm_tn(Cg, dYo[:, cols]))
            dCs.append(_mm_nt(dYo[:, cols], hprev[:, cols]) + _mm(dCB, Bg))
            dBs.append(_mm_tn(dCB, Cg) + _mm_nt(Xd[:, cols], dS))
        Yoff = jnp.concatenate(yoff, axis=1) * R["eo_e"]
        dXd = jnp.concatenate(dxd, axis=1)
        dX = jnp.concatenate(dxdiag, axis=1) + dXd * R["ds_e"]
        t_state = dXd * Xd
        dacs = dacs + _mm_exact_r(dY * Yoff - t_state, ET) - dacs_t.T
        v_last = _colsum(t_state + dhn * hprev * R["cd_e"])
        dlast = _mm_exact_r(jnp.broadcast_to(v_last, (8, 1024)), ET)[0:1, :]
        dacs = dacs + jnp.where(R["row"] == 127, dlast, 0.0)
        triu = (R["col"] >= R["row"]).astype(BF16)
        da = _mm_exact_l(triu, dacs)
        ddt = da * R["A"] + _mm(dX * xs_c, ET)
        ddt_raw = ddt * _sigmoid(R["raw"])
        dxs_c = dX * R["dt_e"] + dY * dsk
        dh_scr[...] = jnp.concatenate(dhprev, axis=1) + dhn * R["cd_e"]

        dact = jnp.concatenate([dxs_c] + dBs + dCs, axis=1)
        pre, sg = R["pre"], R["sg"]
        dpre = dact * (sg * (1.0 + pre * (1.0 - sg)))
        ext2_scr[0:8, :] = dpre[120:128, :]
        ext2_scr[8:16, :] = nxt_scr[...]
        nxt_scr[...] = dpre[0:8, :]
        cw = cw_ref[...]
        u_b, dpre_b = p_ref[:, S_XS:S_DT].astype(_MXU), dpre.astype(_MXU)
        dxbc = cw[3:4, :] * dpre
        taps = [_colsum(dpre * p_ref[:, S_XS:S_DT])]
        for s in (1, 2, 3):
            up = (R["col"] - R["row"] == s).astype(_MXU)
            d_s = jnp.concatenate([jnp.dot(up, dpre_b, preferred_element_type=F32)[0:120],
                                   ext2_scr[s:8 + s, :]], axis=0)
            dxbc = dxbc + cw[3 - s:4 - s, :] * d_s
            taps.append(_colsum(dpre * _rows_from_above(u_b, s, ext_scr, R["row"], R["col"])))
        acc_cw_ref[...] += _rows8(taps[::-1] + [_colsum(dpre)])
        acc_w_ref[...] += _rows8([jnp.concatenate(dnw_parts, axis=1), _colsum(dY * xs_c)])
        acc_s_ref[...] += _rows8([_colsum(ddt_raw), _colsum(da * R["dt"])])

        lane = lax.broadcasted_iota(jnp.int32, (128, 128), 1)
        dp_ref[:, S_Z:S_Z + 1024] = dz
        dp_ref[:, S_XS:S_DT] = dxbc
        dp_ref[:, S_DT:S_DT + 128] = jnp.where(lane < N_HEADS, ddt_raw, 0.0)
        dp_ref[:, S_DT + 128:S_W] = jnp.zeros((128, 128), F32)

        @pl.when(i == nc - 1)
        def _():
            acc = acc_s_ref[...]
            dskip = _mm_exact_r(acc_w_ref[...], ET)[1:2, :]
            acc_s_ref[...] = _rows8([acc[0:1, :], acc[1:2, :] * R["A"], dskip])

    const = lambda shape: pl.BlockSpec(shape, lambda i: (0, 0))
    smem = pl.BlockSpec(memory_space=pltpu.SMEM)
    rev = lambda i: (nc - 1 - i, 0)
    return _call(
        body, comm, name="ssd_bwd", grid=(nc,),
        in_specs=[pl.BlockSpec((CHUNK, S_W), rev),
                  pl.BlockSpec((8, S_W), lambda i: (jnp.maximum((nc - 1 - i) * 16 - 1, 0), 0)),
                  pl.BlockSpec((128, 1024), rev),
                  pl.BlockSpec((CHUNK, D_SSD), rev),
                  pl.BlockSpec((CHUNK, D_XBC), rev),
                  pl.BlockSpec((CHUNK, D_SSD), rev),
                  const((4, D_XBC)), const((1, D_XBC)), smem, smem, smem, const((1, 1024)),
                  const((128, 1024)), const((1024, 128))],
        out_specs=[pl.BlockSpec((CHUNK, S_W), rev), const((8, D_XBC)), const((8, 1024)), const((8, 128))],
        out_shape=[jax.ShapeDtypeStruct((L, S_W), F32), jax.ShapeDtypeStruct((8, D_XBC), F32),
                   jax.ShapeDtypeStruct((8, 1024), F32), jax.ShapeDtypeStruct((8, 128), F32)],
        scratch_shapes=[pltpu.VMEM((128, 1024), F32), pltpu.VMEM((16, D_XBC), F32),
                        pltpu.VMEM((16, D_XBC), F32), pltpu.VMEM((8, D_XBC), F32)],
        args=(proj_ssd, proj_ssd, hprev_all, ypre, pre, dy, conv_w8, conv_b8, dtb8, alog8, dskip_e, norm_w, E, ET))


def _rope(t, tab):
    cos, sa, sb = tab[:, 0:128], tab[:, 128:256], tab[:, 256:384]
    outs = []
    for i in range(t.shape[1] // 128):
        tg = t[:, i * 128:(i + 1) * 128]
        outs.append(tg * cos + pltpu.roll(tg, 8, 1) * sa + pltpu.roll(tg, 120, 1) * sb)
    return jnp.concatenate(outs, axis=1)


def _rope_transposed(d, tab):
    cos, sa, sb = tab[:, 0:128], tab[:, 128:256], tab[:, 256:384]
    outs = []
    for i in range(d.shape[1] // 128):
        dg = d[:, i * 128:(i + 1) * 128]
        outs.append(dg * cos + pltpu.roll(dg * sa, 120, 1) + pltpu.roll(dg * sb, 8, 1))
    return jnp.concatenate(outs, axis=1)


def _lo_half(rows):
    return lax.broadcasted_iota(jnp.int32, (rows, 128), 1) < HEAD_DIM


def _kv_both(t, j):
    p, b = j // 2, j % 2
    lo = _lo_half(t.shape[0])
    nat = jnp.where(lo if b == 0 else jnp.logical_not(lo), t[:, p * 128:(p + 1) * 128], 0.0)
    return nat + pltpu.roll(nat, HEAD_DIM, 1)


def _stack_heads(t, j):
    lo = _lo_half(CHUNK)
    hi = jnp.logical_not(lo)
    a, b = t[:, 2 * j * 128:(2 * j + 1) * 128], t[:, (2 * j + 1) * 128:(2 * j + 2) * 128]
    return jnp.concatenate([jnp.where(lo, a, 0.0), jnp.where(hi, a, 0.0),
                            jnp.where(lo, b, 0.0), jnp.where(hi, b, 0.0)], axis=0)


def _unstack_heads(s):
    lo = _lo_half(CHUNK)
    return jnp.concatenate([jnp.where(lo, s[0:128], s[128:256]), jnp.where(lo, s[256:384], s[384:512])], axis=1)


def _fold_kv(r, j):
    lo = _lo_half(r.shape[0])
    return jnp.where(lo if j % 2 == 0 else jnp.logical_not(lo), r + pltpu.roll(r, HEAD_DIM, 1), 0.0)


def _sink_row(sink_ref, j):
    hid = lax.broadcasted_iota(jnp.int32, (1, 4 * CHUNK), 1) // CHUNK
    row = jnp.zeros((1, 4 * CHUNK), F32)
    for hh in range(4):
        row = jnp.where(hid == hh, sink_ref[4 * j + hh], row)
    return row


def _from_current(n_rows=CHUNK):
    si = lax.broadcasted_iota(jnp.int32, (n_rows, 4 * CHUNK), 0)
    qi = lax.broadcasted_iota(jnp.int32, (n_rows, 4 * CHUNK), 1) % CHUNK
    return si <= qi


def _fold(full, from_cur, pen=0.0):
    return jnp.where(from_cur, full[CHUNK:2 * CHUNK], full[0:CHUNK] + pen)


def _unfold(t, from_cur):
    c = jnp.where(from_cur, t, 0.0)
    return jnp.concatenate([t - c, c], axis=0)


def _softmax_sink(s, sink):
    mx = jnp.maximum(jnp.max(s, axis=0, keepdims=True), sink)
    p = jnp.exp(s - mx)
    esink = jnp.exp(sink - mx)
    inv = 1.0 / (jnp.sum(p, axis=0, keepdims=True) + esink)
    return p * inv, esink * inv


def _swa_inputs(blk, p_ref, prev_ref, tab_ref, ptab_ref):
    tab = tab_ref[...]
    qr = _rope(p_ref[:, A_Q:A_Q + 1024], tab) * ATT_SCALE
    kk = jnp.concatenate([_rope(prev_ref[:, 0:256], ptab_ref[...]), _rope(p_ref[:, A_K:A_K + 256], tab)], axis=0)
    vv = jnp.concatenate([prev_ref[:, 256:512], p_ref[:, A_V:A_V + 256]], axis=0)
    return tab, qr, kk, vv, jnp.where(blk > 0, 0.0, NEG)


def _swa_forward(proj_att, tabs, sinks, y):
    L = proj_att.shape[0]
    nb = L // CHUNK

    def body(sink_ref, p_ref, prev_ref, tab_ref, ptab_ref, y_in_ref, y_ref):
        n = pl.program_id(0)
        _, qr, kk, vv, pen = _swa_inputs(n, p_ref, prev_ref, tab_ref, ptab_ref)
        from_cur = _from_current()
        outs = []
        for j in range(KV_HEADS):
            s = _fold(_mm_nt(_kv_both(kk, j), _stack_heads(qr, j)), from_cur, pen)
            P, _ = _softmax_sink(s, _sink_row(sink_ref, j))
            outs.append(_unstack_heads(_mm_tn(_unfold(P, from_cur), _kv_both(vv, j))))
        g = p_ref[:, A_G:A_G + 1024]
        y_ref[...] = (jnp.concatenate(outs, axis=1) * (g * _sigmoid(g))).astype(y_ref.dtype)

    return pl.pallas_call(
        body, name="swa_fwd", grid=(nb,),
        in_specs=[pl.BlockSpec(memory_space=pltpu.SMEM),
                  pl.BlockSpec((CHUNK, A_W), lambda n: (n, 0)),
                  pl.BlockSpec((CHUNK, 512), lambda n: (jnp.maximum(n - 1, 0), 2)),
                  pl.BlockSpec((CHUNK, 384), lambda n: (n, 0)),
                  pl.BlockSpec((CHUNK, 384), lambda n: (jnp.maximum(n - 1, 0), 0)),
                  pl.BlockSpec(memory_space=pl.ANY)],
        out_specs=pl.BlockSpec((CHUNK, D_ATT), lambda n: (n, 1)),
        out_shape=jax.ShapeDtypeStruct(y.shape, y.dtype),
        input_output_aliases={5: 0},
        compiler_params=_params(("arbitrary",)),
    )(sinks, proj_att, proj_att, tabs, tabs, y)


def _swa_backward(proj_att, tabs, sinks, dy, comm=None):
    L = proj_att.shape[0]
    nb = L // CHUNK

    def body(sink_ref, p_ref, prev_ref, tab_ref, ptab_ref, dy_ref, dp_ref, dsink_ref, carry_k, carry_v):
        i = pl.program_id(0)
        n = nb - 1 - i

        @pl.when(i == 0)
        def _():
            carry_k[...] = jnp.zeros_like(carry_k)
            carry_v[...] = jnp.zeros_like(carry_v)
            dsink_ref[...] = jnp.zeros_like(dsink_ref)

        tab, qr, kk, vv, pen = _swa_inputs(n, p_ref, prev_ref, tab_ref, ptab_ref)
        from_cur = _from_current()
        g = p_ref[:, A_G:A_G + 1024]
        sgm = _sigmoid(g)
        dyv = dy_ref[...]
        do_all = dyv * (g * sgm)
        lane8 = lax.broadcasted_iota(jnp.int32, (8, 128), 1)
        hid = lax.broadcasted_iota(jnp.int32, (1, 4 * CHUNK), 1) // CHUNK
        o_parts, dq_parts = [], []
        dk_nat = [jnp.zeros((2 * CHUNK, 128), F32) for _ in range(2)]
        dv_nat = [jnp.zeros((2 * CHUNK, 128), F32) for _ in range(2)]
        dsink = jnp.zeros((8, 128), F32)
        for j in range(KV_HEADS):
            qs = _stack_heads(qr, j)
            kkb, vvb = _kv_both(kk, j), _kv_both(vv, j)
            P, psink = _softmax_sink(_fold(_mm_nt(kkb, qs), from_cur, pen), _sink_row(sink_ref, j))
            p_full = _unfold(P, from_cur)
            o_parts.append(_unstack_heads(_mm_tn(p_full, vvb)))
            do_s = _stack_heads(do_all, j)
            dP = _fold(_mm_nt(vvb, do_s), from_cur)
            D = jnp.sum(P * dP, axis=0, keepdims=True)
            ds_full = _unfold(P * (dP - D), from_cur)
            sd = psink * D
            for hh in range(4):
                dsink = dsink + jnp.where(lane8 == 4 * j + hh, -jnp.sum(jnp.where(hid == hh, sd, 0.0)), 0.0)
            dq_parts.append(_unstack_heads(_mm_tn(ds_full, kkb)) * ATT_SCALE)
            dk_nat[j // 2] = dk_nat[j // 2] + _fold_kv(_mm(ds_full, qs), j)
            dv_nat[j // 2] = dv_nat[j // 2] + _fold_kv(_mm(p_full, do_s), j)
        o = jnp.concatenate(o_parts, axis=1)
        dkk = jnp.concatenate(dk_nat, axis=1)
        dvv = jnp.concatenate(dv_nat, axis=1)
        dp_ref[:, A_Q:A_Q + 1024] = _rope_transposed(jnp.concatenate(dq_parts, axis=1), tab)
        dp_ref[:, A_K:A_K + 256] = _rope_transposed(dkk[CHUNK:2 * CHUNK] + carry_k[...], tab)
        dp_ref[:, A_V:A_V + 256] = dvv[CHUNK:2 * CHUNK] + carry_v[...]
        dp_ref[:, A_G:A_G + 1024] = dyv * o * (sgm * (1.0 + g * (1.0 - sgm)))
        carry_k[...] = dkk[0:CHUNK]
        carry_v[...] = dvv[0:CHUNK]
        dsink_ref[...] += dsink

    rev = lambda i: (nb - 1 - i, 0)
    prev = lambda i: jnp.maximum(nb - 2 - i, 0)
    return _call(
        body, comm, name="swa_bwd", grid=(nb,),
        in_specs=[pl.BlockSpec(memory_space=pltpu.SMEM),
                  pl.BlockSpec((CHUNK, A_W), rev),
                  pl.BlockSpec((CHUNK, 512), lambda i: (prev(i), 2)),
                  pl.BlockSpec((CHUNK, 384), rev),
                  pl.BlockSpec((CHUNK, 384), lambda i: (prev(i), 0)),
                  pl.BlockSpec((CHUNK, D_ATT), lambda i: (nb - 1 - i, 1))],
        out_specs=[pl.BlockSpec((CHUNK, A_W), rev), pl.BlockSpec((8, 128), lambda i: (0, 0))],
        out_shape=[jax.ShapeDtypeStruct((L, A_W), F32), jax.ShapeDtypeStruct((8, 128), F32)],
        scratch_shapes=[pltpu.VMEM((CHUNK, 256), F32), pltpu.VMEM((CHUNK, 256), F32)],
        args=(sinks, proj_att, proj_att, tabs, tabs, dy))


def _head(y, x, target, w_out, ln_g8, ln_b8, *, tm):
    L = x.shape[0]
    nsteps = L // tm

    def body(y_ref, x_ref, t_ref, wo_ref, g_ref, b_ref, dr_ref, dy_ref, acc_ref):
        i = pl.program_id(0)

        @pl.when(i == 0)
        def _():
            acc_ref[...] = jnp.zeros_like(acc_ref)

        r = ALPHA * x_ref[...] + _mm(y_ref[...], wo_ref[...])
        mu = jnp.mean(r, axis=-1, keepdims=True)
        d = r - mu
        rstd = lax.rsqrt(jnp.mean(d * d, axis=-1, keepdims=True) + LN_EPS)
        xh = d * rstd
        gam = g_ref[0:1, :]
        e = xh * gam + b_ref[0:1, :] - t_ref[...]
        dout = e * (1.0 / D_MODEL)
        dxh = dout * gam
        dr = rstd * (dxh - jnp.mean(dxh, axis=-1, keepdims=True)
                     - xh * jnp.mean(dxh * xh, axis=-1, keepdims=True))
        dr_ref[...] = dr
        dy_ref[...] = _mm_nt(dr, wo_ref[...])
        acc_ref[...] += _rows8([_colsum(dout * xh), _colsum(dout), _colsum(e * e) * (0.5 / D_MODEL)])

        @pl.when(i == nsteps - 1)
        def _():
            acc = acc_ref[...]
            tot = jnp.sum(acc[2:3, :])
            rid = lax.broadcasted_iota(jnp.int32, (8, 1024), 0)
            acc_ref[...] = jnp.where(rid == 3, tot, acc)

    const = lambda shape: pl.BlockSpec(shape, lambda i: (0, 0))
    row = lambda w: pl.BlockSpec((tm, w), lambda i: (i, 0))
    return pl.pallas_call(
        body, name="head", grid=(nsteps,),
        in_specs=[row(2048), row(1024), row(1024), const((2048, 1024)), const((1, 1024)), const((1, 1024))],
        out_specs=[row(1024), row(2048), const((8, 1024))],
        out_shape=[jax.ShapeDtypeStruct((L, D_MODEL), F32), jax.ShapeDtypeStruct((L, 2048), F32),
                   jax.ShapeDtypeStruct((8, 1024), F32)],
        compiler_params=_params(("arbitrary",)),
    )(y, x, target, w_out, ln_g8, ln_b8)


def _gather_w_in(w_shard):
    R = w_shard.shape[0]
    halves = (pl.ds(0, R // 2), pl.ds(R // 2, R // 2))
    any_spec = pl.BlockSpec(memory_space=pl.ANY)

    def body(in_ref, out_ref, send_sems, recv_sems, local_sem):
        x, y, c = _position()

        def slot(p, half=None):
            s = out_ref.at[_index(*p)]
            return s if half is None else s.at[halves[half]]

        def same_core(p):
            return (p[0], p[1], c)

        def other_core(p):
            return (p[0], p[1], 1 - c)

        me, xn, yn, dg = (x, y), (1 - x, y), (x, 1 - y), (1 - x, 1 - y)

        def copy(k, dst, to, src=None):
            return _remote(dst if src is None else src, dst, send_sems.at[k], recv_sems.at[k], to)

        local = pltpu.make_async_copy(in_ref, slot(same_core(me)), local_sem)
        local.start()
        own = [copy(0, slot(same_core(me)), other_core(me), in_ref), copy(1, slot(same_core(me)), same_core(xn), in_ref),
               copy(2, slot(same_core(me)), same_core(yn), in_ref)]
        for cp in own:
            cp.start()
        copy(1, slot(same_core(xn)), same_core(xn)).wait_recv()
        passed = [copy(4, slot(same_core(xn), 1), same_core(yn)), copy(5, slot(same_core(xn)), other_core(me))]
        for cp in passed:
            cp.start()
        copy(2, slot(same_core(yn)), same_core(yn)).wait_recv()
        more = [copy(3, slot(same_core(yn), 0), same_core(xn)), copy(6, slot(same_core(yn)), other_core(me))]
        for cp in more:
            cp.start()
        passed += more
        for k, half in ((3, 0), (4, 1)):
            copy(k, slot(same_core(dg), half), same_core(xn)).wait_recv()
            fwd = copy(7 + half, slot(same_core(dg), half), other_core(me))
            fwd.start()
            passed.append(fwd)
        copy(0, slot(other_core(me)), other_core(me)).wait_recv()
        copy(5, slot(other_core(xn)), other_core(me)).wait_recv()
        copy(6, slot(other_core(yn)), other_core(me)).wait_recv()
        for half in (0, 1):
            copy(7 + half, slot(other_core(dg), half), other_core(me)).wait_recv()
        for cp in own + passed:
            cp.wait_send()
        local.wait()

    return pl.pallas_call(
        body, name="gather_w_in", in_specs=[any_spec], out_specs=any_spec,
        out_shape=jax.ShapeDtypeStruct((N_DEV,) + w_shard.shape, w_shard.dtype),
        scratch_shapes=[pltpu.SemaphoreType.DMA((9,)), pltpu.SemaphoreType.DMA((9,)), pltpu.SemaphoreType.DMA],
    )(w_shard)


def _input_gradient(d_ssd, d_att, w_ssd, w_att, dr, *, tm, comm=None):
    L = dr.shape[0]

    def body(ds_ref, da_ref, ws_ref, wa_ref, dr_ref, o_ref):
        o_ref[...] = ALPHA * dr_ref[...] + _mm_nt(ds_ref[...], ws_ref[...]) + _mm_nt(da_ref[...], wa_ref[...])

    row = lambda w: pl.BlockSpec((tm, w), lambda i: (i, 0))
    const = lambda shape: pl.BlockSpec(shape, lambda i: (0, 0))
    return _call(body, comm, name="dx", grid=(L // tm,),
                 in_specs=[row(S_W), row(A_W), const((D_MODEL, S_W)), const((D_MODEL, A_W)), row(D_MODEL)],
                 out_specs=[row(D_MODEL)], out_shape=[jax.ShapeDtypeStruct((L, D_MODEL), F32)],
                 scratch_shapes=[], args=(d_ssd, d_att, w_ssd, w_att, dr))


SHARD_COLS = D_IN_PROJ // N_DEV
SPLIT = N_SSD_REAL - 4 * SHARD_COLS
RELAYOUT_ROWS = 256


def _unpack_w_in(w_all):
    def body(g_ref, ws_ref, wa_ref):
        for j in range(4):
            ws_ref[:, SHARD_COLS * j:SHARD_COLS * (j + 1)] = g_ref[j]
        ws_ref[:, 4 * SHARD_COLS:N_SSD_REAL] = g_ref[4, :, 0:SPLIT]
        ws_ref[:, N_SSD_REAL:S_W] = jnp.zeros((RELAYOUT_ROWS, S_W - N_SSD_REAL), ws_ref.dtype)
        wa_ref[:, 0:SHARD_COLS - SPLIT] = g_ref[4, :, SPLIT:SHARD_COLS]
        for j in range(5, N_DEV):
            lo = SHARD_COLS * (j - 4) - SPLIT
            wa_ref[:, lo:lo + SHARD_COLS] = g_ref[j]

    return pl.pallas_call(
        body, name="unpack_w_in", grid=(D_MODEL // RELAYOUT_ROWS,),
        in_specs=[pl.BlockSpec((N_DEV, RELAYOUT_ROWS, SHARD_COLS), lambda i: (0, i, 0))],
        out_specs=[pl.BlockSpec((RELAYOUT_ROWS, S_W), lambda i: (i, 0)), pl.BlockSpec((RELAYOUT_ROWS, A_W), lambda i: (i, 0))],
        out_shape=[jax.ShapeDtypeStruct((D_MODEL, S_W), w_all.dtype), jax.ShapeDtypeStruct((D_MODEL, A_W), w_all.dtype)],
        compiler_params=_params(("arbitrary",)),
    )(w_all)


def _pack_dw_in(me1, dw_ssd, dw_att, half):
    def body(me_ref, *refs):
        if half == 0:
            ds_ref, p_ref, own_ref = refs
            me = me_ref[0]

            @pl.when(me >= 4)
            def _():
                own_ref[...] = jnp.zeros_like(own_ref)
        else:
            ds_ref, da_ref, p_ref = refs

        for j in range(4):
            if half == 0:
                pieces = [(0, ds_ref[:, SHARD_COLS * j:SHARD_COLS * (j + 1)])]
            elif j == 0:
                pieces = [(0, ds_ref[:, 4 * SHARD_COLS:N_SSD_REAL]), (SPLIT, da_ref[:, 0:SHARD_COLS - SPLIT])]
            else:
                lo = SHARD_COLS * j - SPLIT
                pieces = [(0, da_ref[:, lo:lo + SHARD_COLS])]
            for off, blk in pieces:
                p_ref[j, :, off:off + blk.shape[1]] = blk.astype(p_ref.dtype)
                if half == 0:
                    @pl.when(me == j)
                    def _(off=off, blk=blk):
                        own_ref[:, off:off + blk.shape[1]] = blk

    ins = [dw_ssd] if half == 0 else [dw_ssd, dw_att]
    row = lambda a: pl.BlockSpec((RELAYOUT_ROWS, a.shape[1]), lambda i: (i, 0))
    out_specs = [pl.BlockSpec((4, RELAYOUT_ROWS, SHARD_COLS), lambda i: (0, i, 0))]
    out_shape = [jax.ShapeDtypeStruct((4, D_MODEL, SHARD_COLS), BF16 if half == 0 else F32)]
    if half == 0:
        out_specs.append(pl.BlockSpec((RELAYOUT_ROWS, SHARD_COLS), lambda i: (i, 0)))
        out_shape.append(jax.ShapeDtypeStruct((D_MODEL, SHARD_COLS), F32))
    return pl.pallas_call(
        body, name="pack_dw_in_%d" % half, grid=(D_MODEL // RELAYOUT_ROWS,),
        in_specs=[pl.BlockSpec(memory_space=pltpu.SMEM)] + [row(a) for a in ins],
        out_specs=out_specs, out_shape=out_shape, compiler_params=_params(("arbitrary",)),
    )(me1, *ins)


def _pair_swap(stack):
    def body(in_ref, out_ref, send_sems, recv_sems):
        x, y, c = _position()
        cps = [_remote(in_ref.at[2 * oy + (1 - c)], out_ref.at[oy], send_sems.at[oy], recv_sems.at[oy], (x, y, 1 - c))
               for oy in range(2)]
        for cp in cps:
            cp.start()
        for cp in cps:
            cp.wait_recv()
        for cp in cps:
            cp.wait_send()

    any_spec = pl.BlockSpec(memory_space=pl.ANY)
    return pl.pallas_call(
        body, name="pair_swap", in_specs=[any_spec], out_specs=any_spec,
        out_shape=jax.ShapeDtypeStruct((2,) + stack.shape[1:], stack.dtype),
        scratch_shapes=[pltpu.SemaphoreType.DMA((2,)), pltpu.SemaphoreType.DMA((2,))],
    )(stack)


def _pair_sum(pos3, stack, swapped, own_lo):
    def body(pos_ref, a_ref, b_ref, lo_ref, chip_ref, own_ref):
        oy = pl.program_id(1)
        t = a_ref[0] + b_ref[0]
        chip_ref[0] = t.astype(chip_ref.dtype)

        @pl.when((pos_ref[0] == 0) & (oy == 0))
        def _():
            own_ref[...] = lo_ref[...]

        @pl.when((pos_ref[0] == 1) & (oy == pos_ref[1]))
        def _():
            own_ref[...] = t

    blk = (1, RELAYOUT_ROWS, SHARD_COLS)
    flat = pl.BlockSpec((RELAYOUT_ROWS, SHARD_COLS), lambda i, oy, pos: (i, 0))
    return pl.pallas_call(
        body, name="pair_sum",
        grid_spec=pltpu.PrefetchScalarGridSpec(
            num_scalar_prefetch=1, grid=(D_MODEL // RELAYOUT_ROWS, 2),
            in_specs=[pl.BlockSpec(blk, lambda i, oy, pos: (2 * oy + pos[2], i, 0)),
                      pl.BlockSpec(blk, lambda i, oy, pos: (oy, i, 0)), flat],
            out_specs=[pl.BlockSpec(blk, lambda i, oy, pos: (oy, i, 0)), flat]),
        out_shape=[jax.ShapeDtypeStruct((2, D_MODEL, SHARD_COLS), BF16), jax.ShapeDtypeStruct((D_MODEL, SHARD_COLS), F32)],
        compiler_params=_params(("arbitrary", "arbitrary")),
    )(pos3, stack, swapped, own_lo)


def _adamw_math(w, g, m, v):
    m = ADAM_B1 * m + (1.0 - ADAM_B1) * g
    v = ADAM_B2 * v + (1.0 - ADAM_B2) * (g * g)
    m_hat = m / (1.0 - ADAM_B1 ** ADAM_STEP)
    v_hat = v / (1.0 - ADAM_B2 ** ADAM_STEP)
    delta = -ADAM_LR * (m_hat / (jnp.sqrt(v_hat) + ADAM_EPS) + ADAM_WD * w)
    return delta, m, v


def _adamw_shard(n_recv, g_own, recv, w, m, v, *, rows, name):
    R, C = g_own.shape

    def body(n_ref, g_ref, r_ref, w_ref, m_ref, v_ref, go_ref, d_ref, mo_ref, vo_ref):
        g = g_ref[...]
        for k in range(N_DEV - 1):
            g = g + jnp.where(k < n_ref[0], r_ref[k].astype(F32), 0.0)
        d, mn, vn = _adamw_math(w_ref[...], g, m_ref[...], v_ref[...])
        go_ref[...] = g
        d_ref[...] = d
        mo_ref[...] = mn
        vo_ref[...] = vn

    blk = pl.BlockSpec((rows, C), lambda i: (i, 0))
    return pl.pallas_call(
        body, name=name, grid=(R // rows,),
        in_specs=[pl.BlockSpec(memory_space=pltpu.SMEM), blk,
                  pl.BlockSpec((N_DEV - 1, rows, C), lambda i: (0, i, 0)), blk, blk, blk],
        out_specs=[blk] * 4, out_shape=[jax.ShapeDtypeStruct((R, C), F32)] * 4,
        compiler_params=_params(("arbitrary",)),
    )(n_recv, g_own, recv, w, m, v)


def _minor_rows_view(a):
    return jnp.transpose(a, (2, 0, 1)).reshape(SHARD_COLS * 8, 128)


def _from_minor_rows_view(v):
    return jnp.transpose(v.reshape(SHARD_COLS, 8, 128), (1, 2, 0)).reshape(1, D_MODEL, SHARD_COLS)


def _adamw_w_in(n_recv, g_own, recv, w, m, v):
    C = SHARD_COLS
    pad = -C % 128

    def body(n_ref, g_ref, r_ref, w_ref, m_ref, v_ref, go_ref, d_ref, mo_ref, vo_ref):
        for q in range(D_MODEL // 128):
            band = pl.ds(q * 128, 128)
            g = g_ref[band, :]
            for k in range(N_DEV - 1):
                g = g + jnp.where(k < n_ref[0], r_ref[k, band, :].astype(F32), 0.0)
            g = jnp.pad(g, ((0, 0), (0, pad))).T[0:C]
            rows = pl.ds(q, C, stride=8)
            d, mn, vn = _adamw_math(w_ref[rows, :], g, m_ref[rows, :], v_ref[rows, :])
            go_ref[rows, :] = g
            d_ref[rows, :] = d
            mo_ref[rows, :] = mn
            vo_ref[rows, :] = vn

    return pl.pallas_call(
        body, name="adamw_w_in", out_shape=[jax.ShapeDtypeStruct(w.shape, F32)] * 4,
        in_specs=[pl.BlockSpec(memory_space=pltpu.SMEM)] + [pl.BlockSpec(memory_space=pltpu.VMEM)] * 5,
        out_specs=[pl.BlockSpec(memory_space=pltpu.VMEM)] * 4,
        compiler_params=_params(),
    )(n_recv, g_own, recv, w, m, v)


SMALL = ("conv_b", "dt_bias", "a_log", "d_skip", "ssd_norm_w", "attn_sinks", "ln_g", "ln_b")


def _adamw_small(gathered, params):
    n_p = len(SMALL)

    def body(*refs):
        acc = []
        for r in refs[:5]:
            t = r[0]
            for k in range(1, N_DEV):
                t = t + r[k]
            acc.append(t)
        head, conv, norm, scal, sink = acc
        grads = dict(conv_b=conv[4:5, :], dt_bias=scal[0:1, 0:N_HEADS], a_log=scal[1:2, 0:N_HEADS],
                     d_skip=scal[2:3, 0:N_HEADS], ssd_norm_w=norm[0:1, :], attn_sinks=sink[0:1, 0:N_HEADS],
                     ln_g=head[0:1, :], ln_b=head[1:2, :])
        wmv = refs[5:5 + 3 * n_p]
        outs = refs[5 + 3 * n_p:]
        outs[0][...] = head[3:4, 0:1]
        outs[1][...] = conv[0:4, :]
        for i, name in enumerate(SMALL):
            w_ref, m_ref, v_ref = wmv[3 * i:3 * i + 3]
            g = grads[name]
            d, mn, vn = _adamw_math(w_ref[...], g, m_ref[...], v_ref[...])
            for o_ref, val in zip(outs[2 + 4 * i:6 + 4 * i], (g, d, mn, vn)):
                o_ref[...] = val

    flat = [a for name in SMALL for a in params[name]]
    out_shape = [jax.ShapeDtypeStruct((1, 1), F32), jax.ShapeDtypeStruct((4, D_XBC), F32)]
    for name in SMALL:
        out_shape += [jax.ShapeDtypeStruct(params[name][0].shape, F32)] * 4
    res = pl.pallas_call(body, name="adamw_small", out_shape=out_shape, compiler_params=_params())(*gathered, *flat)
    return res[0], res[1], {name: res[2 + 4 * i:6 + 4 * i] for i, name in enumerate(SMALL)}


def _adamw_plain(g, w, m, v):
    def body(g_ref, w_ref, m_ref, v_ref, d_ref, mo_ref, vo_ref):
        d, mn, vn = _adamw_math(w_ref[...], g_ref[...], m_ref[...], v_ref[...])
        d_ref[...] = d
        mo_ref[...] = mn
        vo_ref[...] = vn

    return pl.pallas_call(
        body, name="adamw_conv_w", out_shape=[jax.ShapeDtypeStruct(w.shape, F32)] * 3,
        compiler_params=_params(),
    )(g, w, m, v)


def _lane_pattern(fn):
    return np.asarray([fn(l % HEAD_DIM) for l in range(128)], np.float32)


ROPE_INV = _lane_pattern(lambda r: ROPE_THETA ** (-2.0 * (r % 8) / ROPE_DIM) if r < ROPE_DIM else 0.0)
ROPE_SIN_A = _lane_pattern(lambda r: 1.0 if 8 <= r < ROPE_DIM else 0.0)
ROPE_SIN_B = _lane_pattern(lambda r: -1.0 if r < 8 else 0.0)


def _rope_tables(positions):
    ang = positions.astype(F32)[:, None] * ROPE_INV[None, :]
    sn = jnp.sin(ang)
    return jnp.concatenate([jnp.cos(ang), sn * ROPE_SIN_A[None, :], sn * ROPE_SIN_B[None, :]], axis=1)


def _expansion():
    E = np.arange(1024)[None, :] // HEAD_DIM == np.arange(128)[:, None]
    return jnp.asarray(E, BF16), jnp.asarray(E.T, BF16)


def _ssd_args(conv_w, conv_b, dt_bias, a_log, d_skip, norm_w, E):
    return (conv_w, conv_b, dt_bias.reshape(-1), a_log.reshape(-1), d_skip.reshape(-1), norm_w, E)


def kernel(x, positions, w_in, conv_w, conv_b, dt_bias, a_log, d_skip, ssd_norm_w, attn_sinks, w_out, ln_g, ln_b, loss_target, m_w_in, m_conv_w, m_conv_b, m_dt_bias, m_a_log, m_d_skip, m_ssd_norm_w, m_attn_sinks, m_w_out, m_ln_g, m_ln_b, v_w_in, v_conv_w, v_conv_b, v_dt_bias, v_a_log, v_d_skip, v_ssd_norm_w, v_attn_sinks, v_w_out, v_ln_g, v_ln_b):
    me = _index(*_position())
    me1 = me.reshape(1).astype(jnp.int32)
    x0, target = x[0], loss_target[0]
    bf16_shard = lambda shape: jax.ShapeDtypeStruct(shape, BF16)
    E, ET = _expansion()
    tabs = _rope_tables(positions[0])
    sinks = attn_sinks.reshape(-1)

    w_ssd, w_att = _unpack_w_in(_gather_w_in(w_in[0].astype(BF16)))
    gather_conv_w = _Hosted([conv_w[0]], [jax.ShapeDtypeStruct((N_DEV,) + conv_w.shape[1:], F32)],
                            [_Flow("gather", 0, 0)])

    proj_ssd, xb, conv_w_all = _matmul(x0, w_ssd, tm=1024, tn=S_W // 2, name="in_proj_ssd", emit_a=True,
                                       comm=gather_conv_w)
    conv_w_f = jnp.transpose(conv_w_all, (1, 0, 2)).reshape(4, D_XBC)
    ssd_args = _ssd_args(conv_w_f, conv_b, dt_bias, a_log, d_skip, ssd_norm_w, E)
    proj_att = _matmul(xb, w_att, tm=1024, tn=A_W // 2, name="in_proj_att")
    gather_w_out = _Hosted([w_out[0].astype(BF16)], [bf16_shard((N_DEV, 256, D_MODEL))], [_Flow("gather", 0, 0)])
    y, ypre, hprev, pre, w_out_all = _ssd_forward(proj_ssd, *ssd_args, comm=gather_w_out)
    w_out_f = w_out_all.reshape(2 * D_MODEL, D_MODEL)
    y = _swa_forward(proj_att, tabs, sinks, y)
    dr, dy, acc_head = _head(y, x0, target, w_out_f, ln_g, ln_b, tm=512)

    dw_out, dw_out_bf16 = _matmul_tn(y, dr, tl=512, tn=D_MODEL, name="dw_out", emit_bf16=True)
    own_out = lax.dynamic_index_in_dim(dw_out.reshape(N_DEV, 256, D_MODEL), me, axis=0, keepdims=False)
    send_out = _Hosted([dw_out_bf16.reshape(N_DEV, 256, D_MODEL)], [bf16_shard((N_DEV - 1, 256, D_MODEL))],
                       [_Flow("exchange", 0, 0)])
    d_ssd, acc_cw, acc_w, acc_s, recv_out = _ssd_backward(proj_ssd, hprev, ypre, pre, dy, *ssd_args, ET, comm=send_out)
    dw_ssd = _matmul_tn(xb, d_ssd, tl=512, tn=S_W // 2, name="dw_in_ssd")
    parts_lo, own_lo = _pack_dw_in(me1, dw_ssd, None, 0)
    recv_shape = bf16_shard((N_DEV - 1, D_MODEL, SHARD_COLS))
    send_lo = _Hosted([parts_lo], [recv_shape], [_Flow("exchange", 0, 0, target_x=0)])
    d_att, dsink, recv_in = _swa_backward(proj_att, tabs, sinks, dy, comm=send_lo)
    dw_att = _matmul_tn(xb, d_att, tl=512, tn=A_W // 2, name="dw_in_att")
    (stack_hi,) = _pack_dw_in(me1, dw_ssd, dw_att, 1)
    pos3 = jnp.stack(_position()).astype(jnp.int32)
    chip_hi, own_in = _pair_sum(pos3, stack_hi, _pair_swap(stack_hi), own_lo)
    accs = [acc_head, acc_cw, acc_w, acc_s, dsink]
    send_hi = _Hosted([chip_hi, recv_in] + accs,
                      [recv_shape] + [jax.ShapeDtypeStruct((N_DEV,) + a.shape, F32) for a in accs],
                      [_Flow("chip_exchange", 0, 0, target_x=1)] + [_Flow("gather", 2 + i, 1 + i) for i in range(5)],
                      aliases={1: 0})
    dx, recv_in, *gathered = _input_gradient(d_ssd, d_att, w_ssd, w_att, dr, tm=256, comm=send_hi)
    n_recv_in = jnp.where(me < 4, N_DEV - 1, 3).reshape(1).astype(jnp.int32)
    n_recv_out = jnp.full((1,), N_DEV - 1, jnp.int32)

    g_in, d_in, nm_in, nv_in = [_from_minor_rows_view(r) for r in _adamw_w_in(
        n_recv_in, own_in, recv_in, _minor_rows_view(w_in), _minor_rows_view(m_w_in), _minor_rows_view(v_w_in))]
    g_out, d_out, nm_out, nv_out = _adamw_shard(n_recv_out, own_out, recv_out, w_out[0], m_w_out[0], v_w_out[0],
                                                rows=256, name="adamw_w_out")
    loss, g_conv_w, small = _adamw_small(gathered, dict(
        conv_b=(conv_b, m_conv_b, v_conv_b), dt_bias=(dt_bias, m_dt_bias, v_dt_bias), a_log=(a_log, m_a_log, v_a_log),
        d_skip=(d_skip, m_d_skip, v_d_skip), ssd_norm_w=(ssd_norm_w, m_ssd_norm_w, v_ssd_norm_w),
        attn_sinks=(attn_sinks, m_attn_sinks, v_attn_sinks), ln_g=(ln_g, m_ln_g, v_ln_g), ln_b=(ln_b, m_ln_b, v_ln_b)))
    g_cw = lax.dynamic_slice_in_dim(g_conv_w, me * (D_XBC // N_DEV), D_XBC // N_DEV, axis=1)
    d_cw, nm_cw, nv_cw = _adamw_plain(g_cw, conv_w[0], m_conv_w[0], v_conv_w[0])

    def leaves(i, big_in, cw, big_out):
        mid = [small[k][i] for k in ("conv_b", "dt_bias", "a_log", "d_skip", "ssd_norm_w", "attn_sinks")]
        return [big_in, cw[None]] + mid + [big_out[None], small["ln_g"][i], small["ln_b"][i]]

    return (loss.reshape(()), dx[None], *leaves(0, g_in, g_cw, g_out), *leaves(1, d_in, d_cw, d_out),
            *leaves(2, nm_in, nm_cw, nm_out), *leaves(3, nv_in, nv_cw, nv_out))
```

```python
import jax
import jax.numpy as jnp
from jax import lax
from jax.experimental import pallas as pl
from jax.experimental.pallas import tpu as pltpu
import numpy as np

F32 = jnp.float32
BF16 = jnp.bfloat16
_MXU = jnp.bfloat16

N_DEV = 8
D_MODEL = 1024
D_SSD = 1024
D_ATT = 1024
HEAD_DIM = 64
N_HEADS = 16
SSD_GROUPS = 2
KV_HEADS = 4
CHUNK = 128
D_XBC = 1536
D_IN_PROJ = 5136
ROPE_DIM = 16
ROPE_THETA = 500000.0
ALPHA = (2.0 * 1) ** 0.25
LN_EPS = 1e-5
RMS_EPS = 1e-5
ATT_SCALE = HEAD_DIM ** -0.5
NEG = -1e30

S_Z, S_XS, S_B, S_C, S_DT, S_W = 0, 1024, 2048, 2304, 2560, 2816
N_SSD_REAL = 2576
A_Q, A_K, A_V, A_G, A_W = 0, 1024, 1280, 1536, 2560

ADAM_LR = 0.001
ADAM_B1 = 0.9
ADAM_B2 = 0.999
ADAM_EPS = 1e-08
ADAM_WD = 0.01
ADAM_STEP = 10

VMEM_LIMIT = 48 * 1024 * 1024
MESH = pl.DeviceIdType.MESH


def _params(sem=None):
    return pltpu.CompilerParams(dimension_semantics=sem, vmem_limit_bytes=VMEM_LIMIT)


def _mm(a, b):
    return jnp.dot(a.astype(_MXU), b.astype(_MXU), preferred_element_type=F32)


def _mm_nt(a, b):
    return lax.dot_general(a.astype(_MXU), b.astype(_MXU), (((1,), (1,)), ((), ())),
                           preferred_element_type=F32)


def _mm_tn(a, b):
    return lax.dot_general(a.astype(_MXU), b.astype(_MXU), (((0,), (0,)), ((), ())),
                           preferred_element_type=F32)


def _split3(v):
    hi = v.astype(BF16)
    r = v - hi.astype(F32)
    mid = r.astype(BF16)
    lo = (r - mid.astype(F32)).astype(BF16)
    return hi, mid, lo


def _mm_exact_r(v, p01):
    hi, mid, lo = _split3(v)
    d = lambda a: jnp.dot(a, p01, preferred_element_type=F32)
    return d(hi) + d(mid) + d(lo)


def _mm_exact_l(p01, v):
    hi, mid, lo = _split3(v)
    d = lambda a: jnp.dot(p01, a, preferred_element_type=F32)
    return d(hi) + d(mid) + d(lo)


def _sigmoid(x):
    return 1.0 / (1.0 + jnp.exp(-x))


def _softplus(x):
    e = jnp.exp(-jnp.abs(x))
    u = 1.0 + e
    log1p = jnp.where(u == 1.0, e, jnp.log(u) * (e / (u - 1.0)))
    return jnp.maximum(x, 0.0) + log1p


def _rows8(rows):
    n = rows[0].shape[1]
    rid = lax.broadcasted_iota(jnp.int32, (8, n), 0)
    out = jnp.zeros((8, n), F32)
    for k, r in enumerate(rows):
        out = out + jnp.where(rid == k, r, 0.0)
    return out


def _colsum(a):
    return jnp.sum(a, axis=0, keepdims=True)


def _matmul(a, b, *, tm, tn, name, emit_a=False, comm=None):
    M, K = a.shape
    N = b.shape[1]

    def body(a_ref, b_ref, o_ref, *rest):
        am = a_ref[...].astype(_MXU)
        o_ref[...] = jnp.dot(am, b_ref[...].astype(_MXU), preferred_element_type=F32)
        if emit_a:
            rest[0][...] = am

    out_specs = [pl.BlockSpec((tm, tn), lambda i, j: (i, j))]
    out_shape = [jax.ShapeDtypeStruct((M, N), F32)]
    if emit_a:
        out_specs.append(pl.BlockSpec((tm, K), lambda i, j: (i, 0)))
        out_shape.append(jax.ShapeDtypeStruct((M, K), _MXU))
    res = _call(
        body, comm, name=name, grid=(M // tm, N // tn),
        in_specs=[pl.BlockSpec((tm, K), lambda i, j: (i, 0)), pl.BlockSpec((K, tn), lambda i, j: (0, j))],
        out_specs=out_specs, out_shape=out_shape, scratch_shapes=[], args=(a, b))
    return res if (emit_a or comm is not None) else res[0]


def _matmul_tn(a, g, *, tl, tn, name, emit_bf16=False):
    L, M = a.shape
    N = g.shape[1]
    last = L // tl - 1

    def body(a_ref, g_ref, o_ref, *rest):
        @pl.when(pl.program_id(1) == 0)
        def _():
            o_ref[...] = jnp.zeros_like(o_ref)

        o_ref[...] += _mm_tn(a_ref[...], g_ref[...])
        if emit_bf16:
            @pl.when(pl.program_id(1) == last)
            def _():
                rest[0][...] = o_ref[...].astype(BF16)

    spec = pl.BlockSpec((M, tn), lambda j, l: (0, j))
    res = pl.pallas_call(
        body, name=name, grid=(N // tn, L // tl),
        in_specs=[pl.BlockSpec((tl, M), lambda j, l: (l, 0)), pl.BlockSpec((tl, tn), lambda j, l: (l, j))],
        out_specs=[spec, spec] if emit_bf16 else [spec],
        out_shape=[jax.ShapeDtypeStruct((M, N), F32)] + ([jax.ShapeDtypeStruct((M, N), BF16)] if emit_bf16 else []),
        compiler_params=_params(("arbitrary", "arbitrary")),
    )(a, g)
    return res if emit_bf16 else res[0]


def _position():
    return lax.axis_index("x"), lax.axis_index("y"), lax.axis_index("c")


def _index(px, py, pc):
    return 4 * px + 2 * py + pc


def _flip(pos, k):
    x, y, c = pos
    return ((1 - x) if (k >> 2) & 1 else x, (1 - y) if (k >> 1) & 1 else y, (1 - c) if k & 1 else c)


def _when(cond, fn):
    if cond is True:
        fn()
    else:
        pl.when(cond)(fn)


def _remote(src, dst, send_sem, recv_sem, peer):
    return pltpu.make_async_remote_copy(src_ref=src, dst_ref=dst, send_sem=send_sem, recv_sem=recv_sem,
                                        device_id=peer, device_id_type=MESH)


class _Flow:
    def __init__(self, kind, operand, result, target_x=None, target_c=None):
        self.kind, self.operand, self.result, self.target_x, self.target_c = kind, operand, result, target_x, target_c

    def owns(self, pos):
        if self.target_x is None:
            return True
        cond = pos[0] == self.target_x
        return cond if self.target_c is None else cond & (pos[2] == self.target_c)


class _Hosted:
    def __init__(self, operands, out_shapes, flows, aliases=None):
        self.operands, self.out_shapes, self.flows = operands, out_shapes, flows
        self.aliases = aliases or {}

    def plan(self, ins, outs, send_sems, recv_sems, local_sems):
        me = _position()
        mi = _index(*me)
        sends, recvs, locals_ = [], [], []
        for row, f in enumerate(self.flows):
            src, dst = ins[f.operand], outs[f.result]
            for k in range(1, N_DEV):
                peer = _flip(me, k)
                sems = (send_sems.at[row, k - 1], recv_sems.at[row, k - 1])
                if f.kind == "exchange":
                    owner = _index(*peer) if f.target_x is None else 2 * peer[1] + peer[2]
                    cp = _remote(src.at[owner], dst.at[k - 1], *sems, peer)
                    sends.append((f.owns(peer), cp))
                    recvs.append((f.owns(me), cp))
                elif f.kind == "chip_exchange":
                    if k & 1:
                        continue
                    cp = _remote(src.at[peer[1]], dst.at[k // 2 - 1], *sems, peer)
                    sends.append((peer[0] == f.target_x, cp))
                    recvs.append((me[0] == f.target_x, cp))
                else:
                    sends.append((True, _remote(src, dst.at[mi], *sems, peer)))
                    recvs.append((True, _remote(src, dst.at[_index(*peer)], *sems, peer)))
            if f.kind == "gather":
                locals_.append(pltpu.make_async_copy(src, dst.at[mi], local_sems.at[row]))

        def start():
            for cp in locals_:
                cp.start()
            for cond, cp in sends:
                _when(cond, cp.start)

        def wait():
            for cond, cp in recvs:
                _when(cond, cp.wait_recv)
            for cond, cp in sends:
                _when(cond, cp.wait_send)
            for cp in locals_:
                cp.wait()

        return start, wait


def _call(body, comm, *, name, grid, in_specs, out_specs, out_shape, scratch_shapes, args, aliases=None):
    io_alias = dict(aliases or {})
    semantics = ("arbitrary",) * len(grid)
    if comm is None:
        return pl.pallas_call(body, name=name, grid=grid, in_specs=in_specs, out_specs=out_specs, out_shape=out_shape,
                              scratch_shapes=scratch_shapes, input_output_aliases=io_alias,
                              compiler_params=_params(semantics))(*args)
    n_in, n_out, n_scr = len(args), len(out_shape), len(scratch_shapes)
    c_in, c_out, rows = len(comm.operands), len(comm.out_shapes), len(comm.flows)

    def hosted(*refs):
        ins, refs = refs[:n_in], refs[n_in:]
        cins, refs = refs[:c_in], refs[c_in:]
        outs, refs = refs[:n_out], refs[n_out:]
        couts, refs = refs[:c_out], refs[c_out:]
        scr, (send_sems, recv_sems, local_sems) = refs[:n_scr], refs[n_scr:]
        start, wait = comm.plan(cins, couts, send_sems, recv_sems, local_sems)
        ids = [pl.program_id(d) for d in range(len(grid))]
        first, last = ids[0] == 0, ids[0] == grid[0] - 1
        for d in range(1, len(grid)):
            first, last = first & (ids[d] == 0), last & (ids[d] == grid[d] - 1)
        pl.when(first)(start)
        body(*ins, *outs, *scr)
        pl.when(last)(wait)

    for ci, co in comm.aliases.items():
        io_alias[n_in + ci] = n_out + co
    any_spec = pl.BlockSpec(memory_space=pl.ANY)
    sems = [pltpu.SemaphoreType.DMA((rows, N_DEV - 1)), pltpu.SemaphoreType.DMA((rows, N_DEV - 1)),
            pltpu.SemaphoreType.DMA((rows,))]
    return pl.pallas_call(
        hosted, name=name, grid=grid, in_specs=list(in_specs) + [any_spec] * c_in,
        out_specs=list(out_specs) + [any_spec] * c_out, out_shape=list(out_shape) + list(comm.out_shapes),
        scratch_shapes=list(scratch_shapes) + sems, input_output_aliases=io_alias,
        compiler_params=_params(semantics))(*args, *comm.operands)


def _head_row(ref, width, rep):
    hid = lax.broadcasted_iota(jnp.int32, (1, width), 1) // rep
    row = jnp.zeros((1, width), F32)
    for h in range(N_HEADS):
        row = jnp.where(hid == h, ref[h], row)
    return row


def _rows_from_above(u_b, s, ext_scr, row, col):
    down = (row - col == s).astype(_MXU)
    return jnp.concatenate([ext_scr[8 - s:16 - s, :], jnp.dot(down, u_b, preferred_element_type=F32)[8:128]], axis=0)


def _ssd_recompute(first, p_ref, halo_ref, cw_ref, cb_ref, dtb_ref, alog_ref, e_ref, ext_scr, pre=None):
    row = lax.broadcasted_iota(jnp.int32, (128, 128), 0)
    col = lax.broadcasted_iota(jnp.int32, (128, 128), 1)
    ext_scr[0:8, :] = jnp.where(first, 0.0, halo_ref[:, S_XS:S_DT])
    if pre is not None:
        ext_scr[8:16, :] = p_ref[0:8, S_XS:S_DT]
    else:
        ext_scr[8:136, :] = p_ref[:, S_XS:S_DT]
        cw = cw_ref[...]
        pre = (cb_ref[0:1, :] + cw[3:4, :] * ext_scr[8:136, :] + cw[2:3, :] * ext_scr[7:135, :]
               + cw[1:2, :] * ext_scr[6:134, :] + cw[0:1, :] * ext_scr[5:133, :])
    sg = _sigmoid(pre)
    act = pre * sg
    lane = lax.broadcasted_iota(jnp.int32, (1, 128), 1)
    A = jnp.where(lane < N_HEADS, -jnp.exp(_head_row(alog_ref, 128, 1)), 0.0)
    raw = p_ref[:, S_DT:S_DT + 128] + _head_row(dtb_ref, 128, 1)
    dt = _softplus(raw)
    dA = dt * A
    tril = (row >= col).astype(BF16)
    acs = _mm_exact_l(tril, dA)
    last = acs[127:128, :]
    ds = jnp.exp(last - acs)
    eo = jnp.exp(acs)
    E = e_ref[...]
    ex = _mm_exact_r(jnp.concatenate([dt, ds, eo], axis=0), E)
    dt_e, ds_e, eo_e = ex[0:128], ex[128:256], ex[256:384]
    xs_c = act[:, 0:1024]
    X = xs_c * dt_e
    return dict(pre=pre, sg=sg, xs_c=xs_c, Bc=act[:, 1024:1280], Cc=act[:, 1280:1536], A=A, raw=raw, dt=dt,
                acs=acs, acsT=acs.T, eo_e=eo_e, ds_e=ds_e, dt_e=dt_e, cd_e=eo_e[127:128, :],
                X=X, Xd=X * ds_e, row=row, col=col)


def _split_halves(t):
    lo = _lo_half(CHUNK)
    return jnp.concatenate([jnp.where(lo, t, 0.0), jnp.where(lo, 0.0, t)], axis=0)


def _ssd_core(R, hprev):
    causal = R["row"] >= R["col"]
    acs, acsT, X = R["acs"], R["acsT"], R["X"]
    ydiag, yoff, snew = [], [], []
    for g in range(SSD_GROUPS):
        Bg = R["Bc"][:, g * 128:(g + 1) * 128]
        Cg = R["Cc"][:, g * 128:(g + 1) * 128]
        cols = slice(g * 512, (g + 1) * 512)
        CB = _mm_nt(Cg, Bg)
        snew.append(_mm_tn(Bg, R["Xd"][:, cols]))
        yoff.append(_mm(Cg, hprev[:, cols]))
        for j in range(4):
            h0 = g * 8 + 2 * j
            ms = [CB * jnp.exp(jnp.where(causal, acs[:, h:h + 1] - acsT[h:h + 1, :], NEG)) for h in (h0, h0 + 1)]
            ydiag.append(_mm(jnp.concatenate(ms, axis=1), _split_halves(X[:, h0 * HEAD_DIM:h0 * HEAD_DIM + 128])))
    Y = jnp.concatenate(ydiag, axis=1) + jnp.concatenate(yoff, axis=1) * R["eo_e"]
    return Y, jnp.concatenate(snew, axis=1)


def _ssd_forward(proj_ssd, conv_w8, conv_b8, dtb8, alog8, dskip_e, norm_w, E, comm=None):
    L = proj_ssd.shape[0]
    nc = L // CHUNK

    def body(p_ref, halo_ref, cw_ref, cb_ref, dtb_ref, alog_ref, dsk_ref, nw_ref, e_ref,
             y_ref, ypre_ref, hprev_ref, pre_ref, h_scr, ext_scr):
        c = pl.program_id(0)
        first = c == 0

        @pl.when(first)
        def _():
            h_scr[...] = jnp.zeros_like(h_scr)

        R = _ssd_recompute(first, p_ref, halo_ref, cw_ref, cb_ref, dtb_ref, alog_ref, e_ref, ext_scr)
        hprev = h_scr[...]
        hprev_ref[...] = hprev
        pre_ref[...] = R["pre"]
        Y, snew = _ssd_core(R, hprev)
        h_scr[...] = hprev * R["cd_e"] + snew
        Y = Y + _head_row(dsk_ref, D_SSD, HEAD_DIM) * R["xs_c"]
        ypre_ref[...] = Y
        z = p_ref[:, S_Z:S_Z + 1024]
        yf = Y * (z * _sigmoid(z))
        outs = []
        for g in range(SSD_GROUPS):
            yg = yf[:, g * 512:(g + 1) * 512]
            r = lax.rsqrt(jnp.mean(yg * yg, axis=-1, keepdims=True) + RMS_EPS)
            outs.append(yg * r)
        y_ref[...] = (jnp.concatenate(outs, axis=1) * nw_ref[0:1, :]).astype(y_ref.dtype)

    const = lambda shape: pl.BlockSpec(shape, lambda c: (0, 0))
    smem = pl.BlockSpec(memory_space=pltpu.SMEM)
    return _call(
        body, comm, name="ssd_fwd", grid=(nc,),
        in_specs=[pl.BlockSpec((CHUNK, S_W), lambda c: (c, 0)),
                  pl.BlockSpec((8, S_W), lambda c: (jnp.maximum(c * 16 - 1, 0), 0)),
                  const((4, D_XBC)), const((1, D_XBC)), smem, smem, smem, const((1, 1024)), const((128, 1024))],
        out_specs=[pl.BlockSpec((CHUNK, D_SSD), lambda c: (c, 0)), pl.BlockSpec((CHUNK, D_SSD), lambda c: (c, 0)),
                   pl.BlockSpec((128, 1024), lambda c: (c, 0)), pl.BlockSpec((CHUNK, D_XBC), lambda c: (c, 0))],
        out_shape=[jax.ShapeDtypeStruct((L, D_SSD + D_ATT), _MXU), jax.ShapeDtypeStruct((L, D_SSD), F32),
                   jax.ShapeDtypeStruct((nc * 128, 1024), F32), jax.ShapeDtypeStruct((L, D_XBC), F32)],
        scratch_shapes=[pltpu.VMEM((128, 1024), F32), pltpu.VMEM((136, D_XBC), F32)],
        args=(proj_ssd, proj_ssd, conv_w8, conv_b8, dtb8, alog8, dskip_e, norm_w, E))


def _ssd_backward(proj_ssd, hprev_all, ypre, pre, dy, conv_w8, conv_b8, dtb8, alog8, dskip_e, norm_w, E, ET, comm=None):
    L = proj_ssd.shape[0]
    nc = L // CHUNK

    def body(p_ref, halo_ref, hprev_ref, ypre_ref, pre_ref, dy_ref, cw_ref, cb_ref, dtb_ref, alog_ref, dsk_ref, nw_ref, e_ref,
             et_ref, dp_ref, acc_cw_ref, acc_w_ref, acc_s_ref, dh_scr, ext_scr, ext2_scr, nxt_scr):
        i = pl.program_id(0)
        c = nc - 1 - i
        first = c == 0

        @pl.when(i == 0)
        def _():
            dh_scr[...] = jnp.zeros_like(dh_scr)
            nxt_scr[...] = jnp.zeros_like(nxt_scr)
            acc_cw_ref[...] = jnp.zeros_like(acc_cw_ref)
            acc_w_ref[...] = jnp.zeros_like(acc_w_ref)
            acc_s_ref[...] = jnp.zeros_like(acc_s_ref)

        R = _ssd_recompute(first, p_ref, halo_ref, cw_ref, cb_ref, dtb_ref, alog_ref, e_ref, ext_scr, pre_ref[...])
        hprev = hprev_ref[...]
        xs_c, X, Xd = R["xs_c"], R["X"], R["Xd"]
        acs, acsT = R["acs"], R["acsT"]
        ET = et_ref[...]
        dsk = _head_row(dsk_ref, D_SSD, HEAD_DIM)
        Y = ypre_ref[...]

        z = p_ref[:, S_Z:S_Z + 1024]
        sz = _sigmoid(z)
        silz = z * sz
        yf = Y * silz
        dyv = dy_ref[...]
        nw = nw_ref[0:1, :]
        dyf_parts, dnw_parts = [], []
        for g in range(SSD_GROUPS):
            cols = slice(g * 512, (g + 1) * 512)
            yg = yf[:, cols]
            r = lax.rsqrt(jnp.mean(yg * yg, axis=-1, keepdims=True) + RMS_EPS)
            yn = yg * r
            dyn = dyv[:, cols] * nw[:, cols]
            dnw_parts.append(_colsum(dyv[:, cols] * yn))
            dyf_parts.append(r * (dyn - yn * jnp.mean(dyn * yn, axis=-1, keepdims=True)))
        dyf = jnp.concatenate(dyf_parts, axis=1)
        dY = dyf * silz
        dz = dyf * Y * (sz * (1.0 + z * (1.0 - sz)))

        dhn = dh_scr[...]
        dYo = dY * R["eo_e"]
        causal = R["row"] >= R["col"]
        dacs = jnp.zeros((128, 128), F32)
        dacs_t = jnp.zeros((128, 128), F32)
        dxdiag, dxd, dhprev, dBs, dCs, yoff = [], [], [], [], [], []
        for g in range(SSD_GROUPS):
            Bg = R["Bc"][:, g * 128:(g + 1) * 128]
            Cg = R["Cc"][:, g * 128:(g + 1) * 128]
            cols = slice(g * 512, (g + 1) * 512)
            CB = _mm_nt(Cg, Bg)
            dCB = jnp.zeros((128, 128), F32)
            for j in range(4):
                h0 = g * 8 + 2 * j
                pc = slice(h0 * HEAD_DIM, h0 * HEAD_DIM + 128)
                dYst = _split_halves(dY[:, pc])
                dMst = _mm_nt(dYst, X[:, pc])
                mts = []
                for a, h in enumerate((h0, h0 + 1)):
                    acol = acs[:, h:h + 1]
                    arow = acsT[h:h + 1, :]
                    Lm = jnp.exp(jnp.where(causal, acol - arow, NEG))
                    M = CB * Lm
                    dM = dMst[a * 128:(a + 1) * 128]
                    dCB = dCB + dM * Lm
                    G = dM * M
                    dacs = dacs + jnp.where(R["col"] == h, jnp.sum(G, axis=1, keepdims=True), 0.0)
                    dacs_t = dacs_t + jnp.where(R["row"] == h, jnp.sum(G, axis=0, keepdims=True), 0.0)
                    mts.append(M.T)
                dxdiag.append(_mm(jnp.concatenate(mts, axis=1), dYst))
            dS = dhn[:, cols]
            dxd.append(_mm(Bg, dS))
            yoff.append(_mm(Cg, hprev[:, cols]))
            dhprev.append(_mm_tn(Cg, dYo[:, cols]))
            dCs.append(_mm_nt(dYo[:, cols], hprev[:, cols]) + _mm(dCB, Bg))
            dBs.append(_mm_tn(dCB, Cg) + _mm_nt(Xd[:, cols], dS))
        Yoff = jnp.concatenate(yoff, axis=1) * R["eo_e"]
        dXd = jnp.concatenate(dxd, axis=1)
        dX = jnp.concatenate(dxdiag, axis=1) + dXd * R["ds_e"]
        t_state = dXd * Xd
        dacs = dacs + _mm_exact_r(dY * Yoff - t_state, ET) - dacs_t.T
        v_last = _colsum(t_state + dhn * hprev * R["cd_e"])
        dlast = _mm_exact_r(jnp.broadcast_to(v_last, (8, 1024)), ET)[0:1, :]
        dacs = dacs + jnp.where(R["row"] == 127, dlast, 0.0)
        triu = (R["col"] >= R["row"]).astype(BF16)
        da = _mm_exact_l(triu, dacs)
        ddt = da * R["A"] + _mm(dX * xs_c, ET)
        ddt_raw = ddt * _sigmoid(R["raw"])
        dxs_c = dX * R["dt_e"] + dY * dsk
        dh_scr[...] = jnp.concatenate(dhprev, axis=1) + dhn * R["cd_e"]

        dact = jnp.concatenate([dxs_c] + dBs + dCs, axis=1)
        pre, sg = R["pre"], R["sg"]
        dpre = dact * (sg * (1.0 + pre * (1.0 - sg)))
        ext2_scr[0:8, :] = dpre[120:128, :]
        ext2_scr[8:16, :] = nxt_scr[...]
        nxt_scr[...] = dpre[0:8, :]
        cw = cw_ref[...]
        u_b, dpre_b = p_ref[:, S_XS:S_DT].astype(_MXU), dpre.astype(_MXU)
        dxbc = cw[3:4, :] * dpre
        taps = [_colsum(dpre * p_ref[:, S_XS:S_DT])]
        for s in (1, 2, 3):
            up = (R["col"] - R["row"] == s).astype(_MXU)
            d_s = jnp.concatenate([jnp.dot(up, dpre_b, preferred_element_type=F32)[0:120],
                                   ext2_scr[s:8 + s, :]], axis=0)
            dxbc = dxbc + cw[3 - s:4 - s, :] * d_s
            taps.append(_colsum(dpre * _rows_from_above(u_b, s, ext_scr, R["row"], R["col"])))
        acc_cw_ref[...] += _rows8(taps[::-1] + [_colsum(dpre)])
        acc_w_ref[...] += _rows8([jnp.concatenate(dnw_parts, axis=1), _colsum(dY * xs_c)])
        acc_s_ref[...] += _rows8([_colsum(ddt_raw), _colsum(da * R["dt"])])

        lane = lax.broadcasted_iota(jnp.int32, (128, 128), 1)
        dp_ref[:, S_Z:S_Z + 1024] = dz
        dp_ref[:, S_XS:S_DT] = dxbc
        dp_ref[:, S_DT:S_DT + 128] = jnp.where(lane < N_HEADS, ddt_raw, 0.0)
        dp_ref[:, S_DT + 128:S_W] = jnp.zeros((128, 128), F32)

        @pl.when(i == nc - 1)
        def _():
            acc = acc_s_ref[...]
            dskip = _mm_exact_r(acc_w_ref[...], ET)[1:2, :]
            acc_s_ref[...] = _rows8([acc[0:1, :], acc[1:2, :] * R["A"], dskip])

    const = lambda shape: pl.BlockSpec(shape, lambda i: (0, 0))
    smem = pl.BlockSpec(memory_space=pltpu.SMEM)
    rev = lambda i: (nc - 1 - i, 0)
    return _call(
        body, comm, name="ssd_bwd", grid=(nc,),
        in_specs=[pl.BlockSpec((CHUNK, S_W), rev),
                  pl.BlockSpec((8, S_W), lambda i: (jnp.maximum((nc - 1 - i) * 16 - 1, 0), 0)),
                  pl.BlockSpec((128, 1024), rev),
                  pl.BlockSpec((CHUNK, D_SSD), rev),
                  pl.BlockSpec((CHUNK, D_XBC), rev),
                  pl.BlockSpec((CHUNK, D_SSD), rev),
                  const((4, D_XBC)), const((1, D_XBC)), smem, smem, smem, const((1, 1024)),
                  const((128, 1024)), const((1024, 128))],
        out_specs=[pl.BlockSpec((CHUNK, S_W), rev), const((8, D_XBC)), const((8, 1024)), const((8, 128))],
        out_shape=[jax.ShapeDtypeStruct((L, S_W), F32), jax.ShapeDtypeStruct((8, D_XBC), F32),
                   jax.ShapeDtypeStruct((8, 1024), F32), jax.ShapeDtypeStruct((8, 128), F32)],
        scratch_shapes=[pltpu.VMEM((128, 1024), F32), pltpu.VMEM((16, D_XBC), F32),
                        pltpu.VMEM((16, D_XBC), F32), pltpu.VMEM((8, D_XBC), F32)],
        args=(proj_ssd, proj_ssd, hprev_all, ypre, pre, dy, conv_w8, conv_b8, dtb8, alog8, dskip_e, norm_w, E, ET))


def _rope(t, tab):
    cos, sa, sb = tab[:, 0:128], tab[:, 128:256], tab[:, 256:384]
    outs = []
    for i in range(t.shape[1] // 128):
        tg = t[:, i * 128:(i + 1) * 128]
        outs.append(tg * cos + pltpu.roll(tg, 8, 1) * sa + pltpu.roll(tg, 120, 1) * sb)
    return jnp.concatenate(outs, axis=1)


def _rope_transposed(d, tab):
    cos, sa, sb = tab[:, 0:128], tab[:, 128:256], tab[:, 256:384]
    outs = []
    for i in range(d.shape[1] // 128):
        dg = d[:, i * 128:(i + 1) * 128]
        outs.append(dg * cos + pltpu.roll(dg * sa, 120, 1) + pltpu.roll(dg * sb, 8, 1))
    return jnp.concatenate(outs, axis=1)


def _lo_half(rows):
    return lax.broadcasted_iota(jnp.int32, (rows, 128), 1) < HEAD_DIM


def _kv_both(t, j):
    p, b = j // 2, j % 2
    lo = _lo_half(t.shape[0])
    nat = jnp.where(lo if b == 0 else jnp.logical_not(lo), t[:, p * 128:(p + 1) * 128], 0.0)
    return nat + pltpu.roll(nat, HEAD_DIM, 1)


def _stack_heads(t, j):
    lo = _lo_half(CHUNK)
    hi = jnp.logical_not(lo)
    a, b = t[:, 2 * j * 128:(2 * j + 1) * 128], t[:, (2 * j + 1) * 128:(2 * j + 2) * 128]
    return jnp.concatenate([jnp.where(lo, a, 0.0), jnp.where(hi, a, 0.0),
                            jnp.where(lo, b, 0.0), jnp.where(hi, b, 0.0)], axis=0)


def _unstack_heads(s):
    lo = _lo_half(CHUNK)
    return jnp.concatenate([jnp.where(lo, s[0:128], s[128:256]), jnp.where(lo, s[256:384], s[384:512])], axis=1)


def _fold_kv(r, j):
    lo = _lo_half(r.shape[0])
    return jnp.where(lo if j % 2 == 0 else jnp.logical_not(lo), r + pltpu.roll(r, HEAD_DIM, 1), 0.0)


def _sink_row(sink_ref, j):
    hid = lax.broadcasted_iota(jnp.int32, (1, 4 * CHUNK), 1) // CHUNK
    row = jnp.zeros((1, 4 * CHUNK), F32)
    for hh in range(4):
        row = jnp.where(hid == hh, sink_ref[4 * j + hh], row)
    return row


def _from_current(n_rows=CHUNK):
    si = lax.broadcasted_iota(jnp.int32, (n_rows, 4 * CHUNK), 0)
    qi = lax.broadcasted_iota(jnp.int32, (n_rows, 4 * CHUNK), 1) % CHUNK
    return si <= qi


def _fold(full, from_cur, pen=0.0):
    return jnp.where(from_cur, full[CHUNK:2 * CHUNK], full[0:CHUNK] + pen)


def _unfold(t, from_cur):
    c = jnp.where(from_cur, t, 0.0)
    return jnp.concatenate([t - c, c], axis=0)


def _softmax_sink(s, sink):
    mx = jnp.maximum(jnp.max(s, axis=0, keepdims=True), sink)
    p = jnp.exp(s - mx)
    esink = jnp.exp(sink - mx)
    inv = 1.0 / (jnp.sum(p, axis=0, keepdims=True) + esink)
    return p * inv, esink * inv


def _swa_inputs(blk, p_ref, prev_ref, tab_ref, ptab_ref):
    tab = tab_ref[...]
    qr = _rope(p_ref[:, A_Q:A_Q + 1024], tab) * ATT_SCALE
    kk = jnp.concatenate([_rope(prev_ref[:, 0:256], ptab_ref[...]), _rope(p_ref[:, A_K:A_K + 256], tab)], axis=0)
    vv = jnp.concatenate([prev_ref[:, 256:512], p_ref[:, A_V:A_V + 256]], axis=0)
    return tab, qr, kk, vv, jnp.where(blk > 0, 0.0, NEG)


def _swa_forward(proj_att, tabs, sinks, y):
    L = proj_att.shape[0]
    nb = L // CHUNK

    def body(sink_ref, p_ref, prev_ref, tab_ref, ptab_ref, y_in_ref, y_ref):
        n = pl.program_id(0)
        _, qr, kk, vv, pen = _swa_inputs(n, p_ref, prev_ref, tab_ref, ptab_ref)
        from_cur = _from_current()
        outs = []
        for j in range(KV_HEADS):
            s = _fold(_mm_nt(_kv_both(kk, j), _stack_heads(qr, j)), from_cur, pen)
            P, _ = _softmax_sink(s, _sink_row(sink_ref, j))
            outs.append(_unstack_heads(_mm_tn(_unfold(P, from_cur), _kv_both(vv, j))))
        g = p_ref[:, A_G:A_G + 1024]
        y_ref[...] = (jnp.concatenate(outs, axis=1) * (g * _sigmoid(g))).astype(y_ref.dtype)

    return pl.pallas_call(
        body, name="swa_fwd", grid=(nb,),
        in_specs=[pl.BlockSpec(memory_space=pltpu.SMEM),
                  pl.BlockSpec((CHUNK, A_W), lambda n: (n, 0)),
                  pl.BlockSpec((CHUNK, 512), lambda n: (jnp.maximum(n - 1, 0), 2)),
                  pl.BlockSpec((CHUNK, 384), lambda n: (n, 0)),
                  pl.BlockSpec((CHUNK, 384), lambda n: (jnp.maximum(n - 1, 0), 0)),
                  pl.BlockSpec(memory_space=pl.ANY)],
        out_specs=pl.BlockSpec((CHUNK, D_ATT), lambda n: (n, 1)),
        out_shape=jax.ShapeDtypeStruct(y.shape, y.dtype),
        input_output_aliases={5: 0},
        compiler_params=_params(("arbitrary",)),
    )(sinks, proj_att, proj_att, tabs, tabs, y)


def _swa_backward(proj_att, tabs, sinks, dy, comm=None):
    L = proj_att.shape[0]
    nb = L // CHUNK

    def body(sink_ref, p_ref, prev_ref, tab_ref, ptab_ref, dy_ref, dp_ref, dsink_ref, carry_k, carry_v):
        i = pl.program_id(0)
        n = nb - 1 - i

        @pl.when(i == 0)
        def _():
            carry_k[...] = jnp.zeros_like(carry_k)
            carry_v[...] = jnp.zeros_like(carry_v)
            dsink_ref[...] = jnp.zeros_like(dsink_ref)

        tab, qr, kk, vv, pen = _swa_inputs(n, p_ref, prev_ref, tab_ref, ptab_ref)
        from_cur = _from_current()
        g = p_ref[:, A_G:A_G + 1024]
        sgm = _sigmoid(g)
        dyv = dy_ref[...]
        do_all = dyv * (g * sgm)
        lane8 = lax.broadcasted_iota(jnp.int32, (8, 128), 1)
        hid = lax.broadcasted_iota(jnp.int32, (1, 4 * CHUNK), 1) // CHUNK
        o_parts, dq_parts = [], []
        dk_nat = [jnp.zeros((2 * CHUNK, 128), F32) for _ in range(2)]
        dv_nat = [jnp.zeros((2 * CHUNK, 128), F32) for _ in range(2)]
        dsink = jnp.zeros((8, 128), F32)
        for j in range(KV_HEADS):
            qs = _stack_heads(qr, j)
            kkb, vvb = _kv_both(kk, j), _kv_both(vv, j)
            P, psink = _softmax_sink(_fold(_mm_nt(kkb, qs), from_cur, pen), _sink_row(sink_ref, j))
            p_full = _unfold(P, from_cur)
            o_parts.append(_unstack_heads(_mm_tn(p_full, vvb)))
            do_s = _stack_heads(do_all, j)
            dP = _fold(_mm_nt(vvb, do_s), from_cur)
            D = jnp.sum(P * dP, axis=0, keepdims=True)
            ds_full = _unfold(P * (dP - D), from_cur)
            sd = psink * D
            for hh in range(4):
                dsink = dsink + jnp.where(lane8 == 4 * j + hh, -jnp.sum(jnp.where(hid == hh, sd, 0.0)), 0.0)
            dq_parts.append(_unstack_heads(_mm_tn(ds_full, kkb)) * ATT_SCALE)
            dk_nat[j // 2] = dk_nat[j // 2] + _fold_kv(_mm(ds_full, qs), j)
            dv_nat[j // 2] = dv_nat[j // 2] + _fold_kv(_mm(p_full, do_s), j)
        o = jnp.concatenate(o_parts, axis=1)
        dkk = jnp.concatenate(dk_nat, axis=1)
        dvv = jnp.concatenate(dv_nat, axis=1)
        dp_ref[:, A_Q:A_Q + 1024] = _rope_transposed(jnp.concatenate(dq_parts, axis=1), tab)
        dp_ref[:, A_K:A_K + 256] = _rope_transposed(dkk[CHUNK:2 * CHUNK] + carry_k[...], tab)
        dp_ref[:, A_V:A_V + 256] = dvv[CHUNK:2 * CHUNK] + carry_v[...]
        dp_ref[:, A_G:A_G + 1024] = dyv * o * (sgm * (1.0 + g * (1.0 - sgm)))
        carry_k[...] = dkk[0:CHUNK]
        carry_v[...] = dvv[0:CHUNK]
        dsink_ref[...] += dsink

    rev = lambda i: (nb - 1 - i, 0)
    prev = lambda i: jnp.maximum(nb - 2 - i, 0)
    return _call(
        body, comm, name="swa_bwd", grid=(nb,),
        in_specs=[pl.BlockSpec(memory_space=pltpu.SMEM),
                  pl.BlockSpec((CHUNK, A_W), rev),
                  pl.BlockSpec((CHUNK, 512), lambda i: (prev(i), 2)),
                  pl.BlockSpec((CHUNK, 384), rev),
                  pl.BlockSpec((CHUNK, 384), lambda i: (prev(i), 0)),
                  pl.BlockSpec((CHUNK, D_ATT), lambda i: (nb - 1 - i, 1))],
        out_specs=[pl.BlockSpec((CHUNK, A_W), rev), pl.BlockSpec((8, 128), lambda i: (0, 0))],
        out_shape=[jax.ShapeDtypeStruct((L, A_W), F32), jax.ShapeDtypeStruct((8, 128), F32)],
        scratch_shapes=[pltpu.VMEM((CHUNK, 256), F32), pltpu.VMEM((CHUNK, 256), F32)],
        args=(sinks, proj_att, proj_att, tabs, tabs, dy))


def _head(y, x, target, w_out, ln_g8, ln_b8, *, tm):
    L = x.shape[0]
    nsteps = L // tm

    def body(y_ref, x_ref, t_ref, wo_ref, g_ref, b_ref, dr_ref, dy_ref, acc_ref):
        i = pl.program_id(0)

        @pl.when(i == 0)
        def _():
            acc_ref[...] = jnp.zeros_like(acc_ref)

        r = ALPHA * x_ref[...] + _mm(y_ref[...], wo_ref[...])
        mu = jnp.mean(r, axis=-1, keepdims=True)
        d = r - mu
        rstd = lax.rsqrt(jnp.mean(d * d, axis=-1, keepdims=True) + LN_EPS)
        xh = d * rstd
        gam = g_ref[0:1, :]
        e = xh * gam + b_ref[0:1, :] - t_ref[...]
        dout = e * (1.0 / D_MODEL)
        dxh = dout * gam
        dr = rstd * (dxh - jnp.mean(dxh, axis=-1, keepdims=True)
                     - xh * jnp.mean(dxh * xh, axis=-1, keepdims=True))
        dr_ref[...] = dr
        dy_ref[...] = _mm_nt(dr, wo_ref[...])
        acc_ref[...] += _rows8([_colsum(dout * xh), _colsum(dout), _colsum(e * e) * (0.5 / D_MODEL)])

        @pl.when(i == nsteps - 1)
        def _():
            acc = acc_ref[...]
            tot = jnp.sum(acc[2:3, :])
            rid = lax.broadcasted_iota(jnp.int32, (8, 1024), 0)
            acc_ref[...] = jnp.where(rid == 3, tot, acc)

    const = lambda shape: pl.BlockSpec(shape, lambda i: (0, 0))
    row = lambda w: pl.BlockSpec((tm, w), lambda i: (i, 0))
    return pl.pallas_call(
        body, name="head", grid=(nsteps,),
        in_specs=[row(2048), row(1024), row(1024), const((2048, 1024)), const((1, 1024)), const((1, 1024))],
        out_specs=[row(1024), row(2048), const((8, 1024))],
        out_shape=[jax.ShapeDtypeStruct((L, D_MODEL), F32), jax.ShapeDtypeStruct((L, 2048), F32),
                   jax.ShapeDtypeStruct((8, 1024), F32)],
        compiler_params=_params(("arbitrary",)),
    )(y, x, target, w_out, ln_g8, ln_b8)


def _gather_w_in(w_shard):
    R = w_shard.shape[0]
    halves = (pl.ds(0, R // 2), pl.ds(R // 2, R // 2))
    any_spec = pl.BlockSpec(memory_space=pl.ANY)

    def body(in_ref, out_ref, send_sems, recv_sems, local_sem):
        x, y, c = _position()

        def slot(p, half=None):
            s = out_ref.at[_index(*p)]
            return s if half is None else s.at[halves[half]]

        def same_core(p):
            return (p[0], p[1], c)

        def other_core(p):
            return (p[0], p[1], 1 - c)

        me, xn, yn, dg = (x, y), (1 - x, y), (x, 1 - y), (1 - x, 1 - y)

        def copy(k, dst, to, src=None):
            return _remote(dst if src is None else src, dst, send_sems.at[k], recv_sems.at[k], to)

        local = pltpu.make_async_copy(in_ref, slot(same_core(me)), local_sem)
        local.start()
        own = [copy(0, slot(same_core(me)), other_core(me), in_ref), copy(1, slot(same_core(me)), same_core(xn), in_ref),
               copy(2, slot(same_core(me)), same_core(yn), in_ref)]
        for cp in own:
            cp.start()
        copy(1, slot(same_core(xn)), same_core(xn)).wait_recv()
        passed = [copy(4, slot(same_core(xn), 1), same_core(yn)), copy(5, slot(same_core(xn)), other_core(me))]
        for cp in passed:
            cp.start()
        copy(2, slot(same_core(yn)), same_core(yn)).wait_recv()
        more = [copy(3, slot(same_core(yn), 0), same_core(xn)), copy(6, slot(same_core(yn)), other_core(me))]
        for cp in more:
            cp.start()
        passed += more
        for k, half in ((3, 0), (4, 1)):
            copy(k, slot(same_core(dg), half), same_core(xn)).wait_recv()
            fwd = copy(7 + half, slot(same_core(dg), half), other_core(me))
            fwd.start()
            passed.append(fwd)
        copy(0, slot(other_core(me)), other_core(me)).wait_recv()
        copy(5, slot(other_core(xn)), other_core(me)).wait_recv()
        copy(6, slot(other_core(yn)), other_core(me)).wait_recv()
        for half in (0, 1):
            copy(7 + half, slot(other_core(dg), half), other_core(me)).wait_recv()
        for cp in own + passed:
            cp.wait_send()
        local.wait()

    return pl.pallas_call(
        body, name="gather_w_in", in_specs=[any_spec], out_specs=any_spec,
        out_shape=jax.ShapeDtypeStruct((N_DEV,) + w_shard.shape, w_shard.dtype),
        scratch_shapes=[pltpu.SemaphoreType.DMA((9,)), pltpu.SemaphoreType.DMA((9,)), pltpu.SemaphoreType.DMA],
    )(w_shard)


def _input_gradient(d_ssd, d_att, w_ssd, w_att, dr, *, tm, comm=None):
    L = dr.shape[0]

    def body(ds_ref, da_ref, ws_ref, wa_ref, dr_ref, o_ref):
        o_ref[...] = ALPHA * dr_ref[...] + _mm_nt(ds_ref[...], ws_ref[...]) + _mm_nt(da_ref[...], wa_ref[...])

    row = lambda w: pl.BlockSpec((tm, w), lambda i: (i, 0))
    const = lambda shape: pl.BlockSpec(shape, lambda i: (0, 0))
    return _call(body, comm, name="dx", grid=(L // tm,),
                 in_specs=[row(S_W), row(A_W), const((D_MODEL, S_W)), const((D_MODEL, A_W)), row(D_MODEL)],
                 out_specs=[row(D_MODEL)], out_shape=[jax.ShapeDtypeStruct((L, D_MODEL), F32)],
                 scratch_shapes=[], args=(d_ssd, d_att, w_ssd, w_att, dr))


SHARD_COLS = D_IN_PROJ // N_DEV
SPLIT = N_SSD_REAL - 4 * SHARD_COLS
RELAYOUT_ROWS = 256


def _unpack_w_in(w_all):
    def body(g_ref, ws_ref, wa_ref):
        for j in range(4):
            ws_ref[:, SHARD_COLS * j:SHARD_COLS * (j + 1)] = g_ref[j]
        ws_ref[:, 4 * SHARD_COLS:N_SSD_REAL] = g_ref[4, :, 0:SPLIT]
        ws_ref[:, N_SSD_REAL:S_W] = jnp.zeros((RELAYOUT_ROWS, S_W - N_SSD_REAL), ws_ref.dtype)
        wa_ref[:, 0:SHARD_COLS - SPLIT] = g_ref[4, :, SPLIT:SHARD_COLS]
        for j in range(5, N_DEV):
            lo = SHARD_COLS * (j - 4) - SPLIT
            wa_ref[:, lo:lo + SHARD_COLS] = g_ref[j]

    return pl.pallas_call(
        body, name="unpack_w_in", grid=(D_MODEL // RELAYOUT_ROWS,),
        in_specs=[pl.BlockSpec((N_DEV, RELAYOUT_ROWS, SHARD_COLS), lambda i: (0, i, 0))],
        out_specs=[pl.BlockSpec((RELAYOUT_ROWS, S_W), lambda i: (i, 0)), pl.BlockSpec((RELAYOUT_ROWS, A_W), lambda i: (i, 0))],
        out_shape=[jax.ShapeDtypeStruct((D_MODEL, S_W), w_all.dtype), jax.ShapeDtypeStruct((D_MODEL, A_W), w_all.dtype)],
        compiler_params=_params(("arbitrary",)),
    )(w_all)


def _pack_dw_in(me1, dw_ssd, dw_att, half):
    def body(me_ref, *refs):
        if half == 0:
            ds_ref, p_ref, own_ref = refs
            me = me_ref[0]

            @pl.when(me >= 4)
            def _():
                own_ref[...] = jnp.zeros_like(own_ref)
        else:
            ds_ref, da_ref, p_ref = refs

        for j in range(4):
            if half == 0:
                pieces = [(0, ds_ref[:, SHARD_COLS * j:SHARD_COLS * (j + 1)])]
            elif j == 0:
                pieces = [(0, ds_ref[:, 4 * SHARD_COLS:N_SSD_REAL]), (SPLIT, da_ref[:, 0:SHARD_COLS - SPLIT])]
            else:
                lo = SHARD_COLS * j - SPLIT
                pieces = [(0, da_ref[:, lo:lo + SHARD_COLS])]
            for off, blk in pieces:
                p_ref[j, :, off:off + blk.shape[1]] = blk.astype(p_ref.dtype)
                if half == 0:
                    @pl.when(me == j)
                    def _(off=off, blk=blk):
                        own_ref[:, off:off + blk.shape[1]] = blk

    ins = [dw_ssd] if half == 0 else [dw_ssd, dw_att]
    row = lambda a: pl.BlockSpec((RELAYOUT_ROWS, a.shape[1]), lambda i: (i, 0))
    out_specs = [pl.BlockSpec((4, RELAYOUT_ROWS, SHARD_COLS), lambda i: (0, i, 0))]
    out_shape = [jax.ShapeDtypeStruct((4, D_MODEL, SHARD_COLS), BF16 if half == 0 else F32)]
    if half == 0:
        out_specs.append(pl.BlockSpec((RELAYOUT_ROWS, SHARD_COLS), lambda i: (i, 0)))
        out_shape.append(jax.ShapeDtypeStruct((D_MODEL, SHARD_COLS), F32))
    return pl.pallas_call(
        body, name="pack_dw_in_%d" % half, grid=(D_MODEL // RELAYOUT_ROWS,),
        in_specs=[pl.BlockSpec(memory_space=pltpu.SMEM)] + [row(a) for a in ins],
        out_specs=out_specs, out_shape=out_shape, compiler_params=_params(("arbitrary",)),
    )(me1, *ins)


def _pair_swap(stack):
    def body(in_ref, out_ref, send_sems, recv_sems):
        x, y, c = _position()
        cps = [_remote(in_ref.at[2 * oy + (1 - c)], out_ref.at[oy], send_sems.at[oy], recv_sems.at[oy], (x, y, 1 - c))
               for oy in range(2)]
        for cp in cps:
            cp.start()
        for cp in cps:
            cp.wait_recv()
        for cp in cps:
            cp.wait_send()

    any_spec = pl.BlockSpec(memory_space=pl.ANY)
    return pl.pallas_call(
        body, name="pair_swap", in_specs=[any_spec], out_specs=any_spec,
        out_shape=jax.ShapeDtypeStruct((2,) + stack.shape[1:], stack.dtype),
        scratch_shapes=[pltpu.SemaphoreType.DMA((2,)), pltpu.SemaphoreType.DMA((2,))],
    )(stack)


def _pair_sum(pos3, stack, swapped, own_lo):
    def body(pos_ref, a_ref, b_ref, lo_ref, chip_ref, own_ref):
        oy = pl.program_id(1)
        t = a_ref[0] + b_ref[0]
        chip_ref[0] = t.astype(chip_ref.dtype)

        @pl.when((pos_ref[0] == 0) & (oy == 0))
        def _():
            own_ref[...] = lo_ref[...]

        @pl.when((pos_ref[0] == 1) & (oy == pos_ref[1]))
        def _():
            own_ref[...] = t

    blk = (1, RELAYOUT_ROWS, SHARD_COLS)
    flat = pl.BlockSpec((RELAYOUT_ROWS, SHARD_COLS), lambda i, oy, pos: (i, 0))
    return pl.pallas_call(
        body, name="pair_sum",
        grid_spec=pltpu.PrefetchScalarGridSpec(
            num_scalar_prefetch=1, grid=(D_MODEL // RELAYOUT_ROWS, 2),
            in_specs=[pl.BlockSpec(blk, lambda i, oy, pos: (2 * oy + pos[2], i, 0)),
                      pl.BlockSpec(blk, lambda i, oy, pos: (oy, i, 0)), flat],
            out_specs=[pl.BlockSpec(blk, lambda i, oy, pos: (oy, i, 0)), flat]),
        out_shape=[jax.ShapeDtypeStruct((2, D_MODEL, SHARD_COLS), BF16), jax.ShapeDtypeStruct((D_MODEL, SHARD_COLS), F32)],
        compiler_params=_params(("arbitrary", "arbitrary")),
    )(pos3, stack, swapped, own_lo)


def _adamw_math(w, g, m, v):
    m = ADAM_B1 * m + (1.0 - ADAM_B1) * g
    v = ADAM_B2 * v + (1.0 - ADAM_B2) * (g * g)
    m_hat = m / (1.0 - ADAM_B1 ** ADAM_STEP)
    v_hat = v / (1.0 - ADAM_B2 ** ADAM_STEP)
    delta = -ADAM_LR * (m_hat / (jnp.sqrt(v_hat) + ADAM_EPS) + ADAM_WD * w)
    return delta, m, v


def _adamw_shard(n_recv, g_own, recv, w, m, v, *, rows, name):
    R, C = g_own.shape

    def body(n_ref, g_ref, r_ref, w_ref, m_ref, v_ref, go_ref, d_ref, mo_ref, vo_ref):
        g = g_ref[...]
        for k in range(N_DEV - 1):
            g = g + jnp.where(k < n_ref[0], r_ref[k].astype(F32), 0.0)
        d, mn, vn = _adamw_math(w_ref[...], g, m_ref[...], v_ref[...])
        go_ref[...] = g
        d_ref[...] = d
        mo_ref[...] = mn
        vo_ref[...] = vn

    blk = pl.BlockSpec((rows, C), lambda i: (i, 0))
    return pl.pallas_call(
        body, name=name, grid=(R // rows,),
        in_specs=[pl.BlockSpec(memory_space=pltpu.SMEM), blk,
                  pl.BlockSpec((N_DEV - 1, rows, C), lambda i: (0, i, 0)), blk, blk, blk],
        out_specs=[blk] * 4, out_shape=[jax.ShapeDtypeStruct((R, C), F32)] * 4,
        compiler_params=_params(("arbitrary",)),
    )(n_recv, g_own, recv, w, m, v)


def _minor_rows_view(a):
    return jnp.transpose(a, (2, 0, 1)).reshape(SHARD_COLS * 8, 128)


def _from_minor_rows_view(v):
    return jnp.transpose(v.reshape(SHARD_COLS, 8, 128), (1, 2, 0)).reshape(1, D_MODEL, SHARD_COLS)


def _adamw_w_in(n_recv, g_own, recv, w, m, v):
    C = SHARD_COLS
    pad = -C % 128

    def body(n_ref, g_ref, r_ref, w_ref, m_ref, v_ref, go_ref, d_ref, mo_ref, vo_ref):
        for q in range(D_MODEL // 128):
            band = pl.ds(q * 128, 128)
            g = g_ref[band, :]
            for k in range(N_DEV - 1):
                g = g + jnp.where(k < n_ref[0], r_ref[k, band, :].astype(F32), 0.0)
            g = jnp.pad(g, ((0, 0), (0, pad))).T[0:C]
            rows = pl.ds(q, C, stride=8)
            d, mn, vn = _adamw_math(w_ref[rows, :], g, m_ref[rows, :], v_ref[rows, :])
            go_ref[rows, :] = g
            d_ref[rows, :] = d
            mo_ref[rows, :] = mn
            vo_ref[rows, :] = vn

    return pl.pallas_call(
        body, name="adamw_w_in", out_shape=[jax.ShapeDtypeStruct(w.shape, F32)] * 4,
        in_specs=[pl.BlockSpec(memory_space=pltpu.SMEM)] + [pl.BlockSpec(memory_space=pltpu.VMEM)] * 5,
        out_specs=[pl.BlockSpec(memory_space=pltpu.VMEM)] * 4,
        compiler_params=_params(),
    )(n_recv, g_own, recv, w, m, v)


SMALL = ("conv_b", "dt_bias", "a_log", "d_skip", "ssd_norm_w", "attn_sinks", "ln_g", "ln_b")


def _adamw_small(gathered, params):
    n_p = len(SMALL)

    def body(*refs):
        acc = []
        for r in refs[:5]:
            t = r[0]
            for k in range(1, N_DEV):
                t = t + r[k]
            acc.append(t)
        head, conv, norm, scal, sink = acc
        grads = dict(conv_b=conv[4:5, :], dt_bias=scal[0:1, 0:N_HEADS], a_log=scal[1:2, 0:N_HEADS],
                     d_skip=scal[2:3, 0:N_HEADS], ssd_norm_w=norm[0:1, :], attn_sinks=sink[0:1, 0:N_HEADS],
                     ln_g=head[0:1, :], ln_b=head[1:2, :])
        wmv = refs[5:5 + 3 * n_p]
        outs = refs[5 + 3 * n_p:]
        outs[0][...] = head[3:4, 0:1]
        outs[1][...] = conv[0:4, :]
        for i, name in enumerate(SMALL):
            w_ref, m_ref, v_ref = wmv[3 * i:3 * i + 3]
            g = grads[name]
            d, mn, vn = _adamw_math(w_ref[...], g, m_ref[...], v_ref[...])
            for o_ref, val in zip(outs[2 + 4 * i:6 + 4 * i], (g, d, mn, vn)):
                o_ref[...] = val

    flat = [a for name in SMALL for a in params[name]]
    out_shape = [jax.ShapeDtypeStruct((1, 1), F32), jax.ShapeDtypeStruct((4, D_XBC), F32)]
    for name in SMALL:
        out_shape += [jax.ShapeDtypeStruct(params[name][0].shape, F32)] * 4
    res = pl.pallas_call(body, name="adamw_small", out_shape=out_shape, compiler_params=_params())(*gathered, *flat)
    return res[0], res[1], {name: res[2 + 4 * i:6 + 4 * i] for i, name in enumerate(SMALL)}


def _adamw_plain(g, w, m, v):
    def body(g_ref, w_ref, m_ref, v_ref, d_ref, mo_ref, vo_ref):
        d, mn, vn = _adamw_math(w_ref[...], g_ref[...], m_ref[...], v_ref[...])
        d_ref[...] = d
        mo_ref[...] = mn
        vo_ref[...] = vn

    return pl.pallas_call(
        body, name="adamw_conv_w", out_shape=[jax.ShapeDtypeStruct(w.shape, F32)] * 3,
        compiler_params=_params(),
    )(g, w, m, v)


def _lane_pattern(fn):
    return np.asarray([fn(l % HEAD_DIM) for l in range(128)], np.float32)


ROPE_INV = _lane_pattern(lambda r: ROPE_THETA ** (-2.0 * (r % 8) / ROPE_DIM) if r < ROPE_DIM else 0.0)
ROPE_SIN_A = _lane_pattern(lambda r: 1.0 if 8 <= r < ROPE_DIM else 0.0)
ROPE_SIN_B = _lane_pattern(lambda r: -1.0 if r < 8 else 0.0)


def _rope_tables(positions):
    ang = positions.astype(F32)[:, None] * ROPE_INV[None, :]
    sn = jnp.sin(ang)
    return jnp.concatenate([jnp.cos(ang), sn * ROPE_SIN_A[None, :], sn * ROPE_SIN_B[None, :]], axis=1)


def _expansion():
    E = np.arange(1024)[None, :] // HEAD_DIM == np.arange(128)[:, None]
    return jnp.asarray(E, BF16), jnp.asarray(E.T, BF16)


def _ssd_args(conv_w, conv_b, dt_bias, a_log, d_skip, norm_w, E):
    return (conv_w, conv_b, dt_bias.reshape(-1), a_log.reshape(-1), d_skip.reshape(-1), norm_w, E)


def kernel(x, positions, w_in, conv_w, conv_b, dt_bias, a_log, d_skip, ssd_norm_w, attn_sinks, w_out, ln_g, ln_b, loss_target, m_w_in, m_conv_w, m_conv_b, m_dt_bias, m_a_log, m_d_skip, m_ssd_norm_w, m_attn_sinks, m_w_out, m_ln_g, m_ln_b, v_w_in, v_conv_w, v_conv_b, v_dt_bias, v_a_log, v_d_skip, v_ssd_norm_w, v_attn_sinks, v_w_out, v_ln_g, v_ln_b):
    me = _index(*_position())
    me1 = me.reshape(1).astype(jnp.int32)
    x0, target = x[0], loss_target[0]
    bf16_shard = lambda shape: jax.ShapeDtypeStruct(shape, BF16)
    E, ET = _expansion()
    tabs = _rope_tables(positions[0])
    sinks = attn_sinks.reshape(-1)

    w_ssd, w_att = _unpack_w_in(_gather_w_in(w_in[0].astype(BF16)))
    gather_conv_w = _Hosted([conv_w[0]], [jax.ShapeDtypeStruct((N_DEV,) + conv_w.shape[1:], F32)],
                            [_Flow("gather", 0, 0)])

    proj_ssd, xb, conv_w_all = _matmul(x0, w_ssd, tm=1024, tn=S_W // 2, name="in_proj_ssd", emit_a=True,
                                       comm=gather_conv_w)
    conv_w_f = jnp.transpose(conv_w_all, (1, 0, 2)).reshape(4, D_XBC)
    ssd_args = _ssd_args(conv_w_f, conv_b, dt_bias, a_log, d_skip, ssd_norm_w, E)
    proj_att = _matmul(xb, w_att, tm=1024, tn=A_W // 2, name="in_proj_att")
    gather_w_out = _Hosted([w_out[0].astype(BF16)], [bf16_shard((N_DEV, 256, D_MODEL))], [_Flow("gather", 0, 0)])
    y, ypre, hprev, pre, w_out_all = _ssd_forward(proj_ssd, *ssd_args, comm=gather_w_out)
    w_out_f = w_out_all.reshape(2 * D_MODEL, D_MODEL)
    y = _swa_forward(proj_att, tabs, sinks, y)
    dr, dy, acc_head = _head(y, x0, target, w_out_f, ln_g, ln_b, tm=512)

    dw_out, dw_out_bf16 = _matmul_tn(y, dr, tl=512, tn=D_MODEL, name="dw_out", emit_bf16=True)
    own_out = lax.dynamic_index_in_dim(dw_out.reshape(N_DEV, 256, D_MODEL), me, axis=0, keepdims=False)
    send_out = _Hosted([dw_out_bf16.reshape(N_DEV, 256, D_MODEL)], [bf16_shard((N_DEV - 1, 256, D_MODEL))],
                       [_Flow("exchange", 0, 0)])
    d_ssd, acc_cw, acc_w, acc_s, recv_out = _ssd_backward(proj_ssd, hprev, ypre, pre, dy, *ssd_args, ET, comm=send_out)
    dw_ssd = _matmul_tn(xb, d_ssd, tl=512, tn=S_W // 2, name="dw_in_ssd")
    parts_lo, own_lo = _pack_dw_in(me1, dw_ssd, None, 0)
    recv_shape = bf16_shard((N_DEV - 1, D_MODEL, SHARD_COLS))
    send_lo = _Hosted([parts_lo], [recv_shape], [_Flow("exchange", 0, 0, target_x=0, target_c=0)])
    d_att, dsink, recv_in = _swa_backward(proj_att, tabs, sinks, dy, comm=send_lo)
    dw_att = _matmul_tn(xb, d_att, tl=512, tn=A_W // 2, name="dw_in_att")
    (stack_hi,) = _pack_dw_in(me1, dw_ssd, dw_att, 1)
    pos3 = jnp.stack(_position()).astype(jnp.int32)
    chip_hi, own_in = _pair_sum(pos3, stack_hi, _pair_swap(stack_hi), own_lo)
    accs = [acc_head, acc_cw, acc_w, acc_s, dsink]
    send_hi = _Hosted([chip_hi, recv_in, parts_lo] + accs,
                      [recv_shape] + [jax.ShapeDtypeStruct((N_DEV,) + a.shape, F32) for a in accs],
                      [_Flow("chip_exchange", 0, 0, target_x=1), _Flow("exchange", 2, 0, target_x=0, target_c=1)]
                      + [_Flow("gather", 3 + i, 1 + i) for i in range(5)], aliases={1: 0})
    dx, recv_in, *gathered = _input_gradient(d_ssd, d_att, w_ssd, w_att, dr, tm=256, comm=send_hi)
    n_recv_in = jnp.where(me < 4, N_DEV - 1, 3).reshape(1).astype(jnp.int32)
    n_recv_out = jnp.full((1,), N_DEV - 1, jnp.int32)

    g_in, d_in, nm_in, nv_in = [_from_minor_rows_view(r) for r in _adamw_w_in(
        n_recv_in, own_in, recv_in, _minor_rows_view(w_in), _minor_rows_view(m_w_in), _minor_rows_view(v_w_in))]
    g_out, d_out, nm_out, nv_out = _adamw_shard(n_recv_out, own_out, recv_out, w_out[0], m_w_out[0], v_w_out[0],
                                                rows=256, name="adamw_w_out")
    loss, g_conv_w, small = _adamw_small(gathered, dict(
        conv_b=(conv_b, m_conv_b, v_conv_b), dt_bias=(dt_bias, m_dt_bias, v_dt_bias), a_log=(a_log, m_a_log, v_a_log),
        d_skip=(d_skip, m_d_skip, v_d_skip), ssd_norm_w=(ssd_norm_w, m_ssd_norm_w, v_ssd_norm_w),
        attn_sinks=(attn_sinks, m_attn_sinks, v_attn_sinks), ln_g=(ln_g, m_ln_g, v_ln_g), ln_b=(ln_b, m_ln_b, v_ln_b)))
    g_cw = lax.dynamic_slice_in_dim(g_conv_w, me * (D_XBC // N_DEV), D_XBC // N_DEV, axis=1)
    d_cw, nm_cw, nv_cw = _adamw_plain(g_cw, conv_w[0], m_conv_w[0], v_conv_w[0])

    def leaves(i, big_in, cw, big_out):
        mid = [small[k][i] for k in ("conv_b", "dt_bias", "a_log", "d_skip", "ssd_norm_w", "attn_sinks")]
        return [big_in, cw[None]] + mid + [big_out[None], small["ln_g"][i], small["ln_b"][i]]

    return (loss.reshape(()), dx[None], *leaves(0, g_in, g_cw, g_out), *leaves(1, d_in, d_cw, d_out),
            *leaves(2, nm_in, nm_cw, nm_out), *leaves(3, nv_in, nv_cw, nv_out))
```

```python
import jax
import jax.numpy as jnp
from jax import lax
from jax.experimental import pallas as pl
from jax.experimental.pallas import tpu as pltpu
import numpy as np

F32 = jnp.float32
BF16 = jnp.bfloat16
_MXU = jnp.bfloat16

N_DEV = 8
D_MODEL = 1024
D_SSD = 1024
D_ATT = 1024
HEAD_DIM = 64
N_HEADS = 16
SSD_GROUPS = 2
KV_HEADS = 4
CHUNK = 128
D_XBC = 1536
D_IN_PROJ = 5136
ROPE_DIM = 16
ROPE_THETA = 500000.0
ALPHA = (2.0 * 1) ** 0.25
LN_EPS = 1e-5
RMS_EPS = 1e-5
ATT_SCALE = HEAD_DIM ** -0.5
NEG = -1e30

S_Z, S_XS, S_B, S_C, S_DT, S_W = 0, 1024, 2048, 2304, 2560, 2816
N_SSD_REAL = 2576
A_Q, A_K, A_V, A_G, A_W = 0, 1024, 1280, 1536, 2560

ADAM_LR = 0.001
ADAM_B1 = 0.9
ADAM_B2 = 0.999
ADAM_EPS = 1e-08
ADAM_WD = 0.01
ADAM_STEP = 10

VMEM_LIMIT = 48 * 1024 * 1024
MESH = pl.DeviceIdType.MESH


def _params(sem=None):
    return pltpu.CompilerParams(dimension_semantics=sem, vmem_limit_bytes=VMEM_LIMIT)


def _mm(a, b):
    return jnp.dot(a.astype(_MXU), b.astype(_MXU), preferred_element_type=F32)


def _mm_nt(a, b):
    return lax.dot_general(a.astype(_MXU), b.astype(_MXU), (((1,), (1,)), ((), ())),
                           preferred_element_type=F32)


def _mm_tn(a, b):
    return lax.dot_general(a.astype(_MXU), b.astype(_MXU), (((0,), (0,)), ((), ())),
                           preferred_element_type=F32)


def _split3(v):
    hi = v.astype(BF16)
    r = v - hi.astype(F32)
    mid = r.astype(BF16)
    lo = (r - mid.astype(F32)).astype(BF16)
    return hi, mid, lo


def _mm_exact_r(v, p01):
    hi, mid, lo = _split3(v)
    d = lambda a: jnp.dot(a, p01, preferred_element_type=F32)
    return d(hi) + d(mid) + d(lo)


def _mm_exact_l(p01, v):
    hi, mid, lo = _split3(v)
    d = lambda a: jnp.dot(p01, a, preferred_element_type=F32)
    return d(hi) + d(mid) + d(lo)


def _sigmoid(x):
    return 1.0 / (1.0 + jnp.exp(-x))


def _softplus(x):
    e = jnp.exp(-jnp.abs(x))
    u = 1.0 + e
    log1p = jnp.where(u == 1.0, e, jnp.log(u) * (e / (u - 1.0)))
    return jnp.maximum(x, 0.0) + log1p


def _rows8(rows):
    n = rows[0].shape[1]
    rid = lax.broadcasted_iota(jnp.int32, (8, n), 0)
    out = jnp.zeros((8, n), F32)
    for k, r in enumerate(rows):
        out = out + jnp.where(rid == k, r, 0.0)
    return out


def _colsum(a):
    return jnp.sum(a, axis=0, keepdims=True)


def _matmul(a, b, *, tm, tn, name, emit_a=False, comm=None):
    M, K = a.shape
    N = b.shape[1]

    def body(a_ref, b_ref, o_ref, *rest):
        am = a_ref[...].astype(_MXU)
        o_ref[...] = jnp.dot(am, b_ref[...].astype(_MXU), preferred_element_type=F32)
        if emit_a:
            rest[0][...] = am

    out_specs = [pl.BlockSpec((tm, tn), lambda i, j: (i, j))]
    out_shape = [jax.ShapeDtypeStruct((M, N), F32)]
    if emit_a:
        out_specs.append(pl.BlockSpec((tm, K), lambda i, j: (i, 0)))
        out_shape.append(jax.ShapeDtypeStruct((M, K), _MXU))
    res = _call(
        body, comm, name=name, grid=(M // tm, N // tn),
        in_specs=[pl.BlockSpec((tm, K), lambda i, j: (i, 0)), pl.BlockSpec((K, tn), lambda i, j: (0, j))],
        out_specs=out_specs, out_shape=out_shape, scratch_shapes=[], args=(a, b))
    return res if (emit_a or comm is not None) else res[0]


def _matmul_tn(a, g, *, tl, tn, name, emit_bf16=False):
    L, M = a.shape
    N = g.shape[1]
    last = L // tl - 1

    def body(a_ref, g_ref, o_ref, *rest):
        @pl.when(pl.program_id(1) == 0)
        def _():
            o_ref[...] = jnp.zeros_like(o_ref)

        o_ref[...] += _mm_tn(a_ref[...], g_ref[...])
        if emit_bf16:
            @pl.when(pl.program_id(1) == last)
            def _():
                rest[0][...] = o_ref[...].astype(BF16)

    spec = pl.BlockSpec((M, tn), lambda j, l: (0, j))
    res = pl.pallas_call(
        body, name=name, grid=(N // tn, L // tl),
        in_specs=[pl.BlockSpec((tl, M), lambda j, l: (l, 0)), pl.BlockSpec((tl, tn), lambda j, l: (l, j))],
        out_specs=[spec, spec] if emit_bf16 else [spec],
        out_shape=[jax.ShapeDtypeStruct((M, N), F32)] + ([jax.ShapeDtypeStruct((M, N), BF16)] if emit_bf16 else []),
        compiler_params=_params(("arbitrary", "arbitrary")),
    )(a, g)
    return res if emit_bf16 else res[0]


def _position():
    return lax.axis_index("x"), lax.axis_index("y"), lax.axis_index("c")


def _index(px, py, pc):
    return 4 * px + 2 * py + pc


def _flip(pos, k):
    x, y, c = pos
    return ((1 - x) if (k >> 2) & 1 else x, (1 - y) if (k >> 1) & 1 else y, (1 - c) if k & 1 else c)


def _when(cond, fn):
    if cond is True:
        fn()
    else:
        pl.when(cond)(fn)


def _remote(src, dst, send_sem, recv_sem, peer):
    return pltpu.make_async_remote_copy(src_ref=src, dst_ref=dst, send_sem=send_sem, recv_sem=recv_sem,
                                        device_id=peer, device_id_type=MESH)


class _Flow:
    def __init__(self, kind, operand, result, target_x=None, target_c=None):
        self.kind, self.operand, self.result, self.target_x, self.target_c = kind, operand, result, target_x, target_c

    def owns(self, pos):
        if self.target_x is None:
            return True
        cond = pos[0] == self.target_x
        return cond if self.target_c is None else cond & (pos[2] == self.target_c)


class _Hosted:
    def __init__(self, operands, out_shapes, flows, aliases=None):
        self.operands, self.out_shapes, self.flows = operands, out_shapes, flows
        self.aliases = aliases or {}

    def plan(self, ins, outs, send_sems, recv_sems, local_sems):
        me = _position()
        mi = _index(*me)
        sends, recvs, locals_ = [], [], []
        for row, f in enumerate(self.flows):
            src, dst = ins[f.operand], outs[f.result]
            for k in range(1, N_DEV):
                peer = _flip(me, k)
                sems = (send_sems.at[row, k - 1], recv_sems.at[row, k - 1])
                if f.kind == "exchange":
                    owner = _index(*peer) if f.target_x is None else 2 * peer[1] + peer[2]
                    cp = _remote(src.at[owner], dst.at[k - 1], *sems, peer)
                    sends.append((f.owns(peer), cp))
                    recvs.append((f.owns(me), cp))
                elif f.kind == "chip_exchange":
                    if k & 1:
                        continue
                    cp = _remote(src.at[peer[1]], dst.at[k // 2 - 1], *sems, peer)
                    sends.append((peer[0] == f.target_x, cp))
                    recvs.append((me[0] == f.target_x, cp))
                else:
                    sends.append((True, _remote(src, dst.at[mi], *sems, peer)))
                    recvs.append((True, _remote(src, dst.at[_index(*peer)], *sems, peer)))
            if f.kind == "gather":
                locals_.append(pltpu.make_async_copy(src, dst.at[mi], local_sems.at[row]))

        def start():
            for cp in locals_:
                cp.start()
            for cond, cp in sends:
                _when(cond, cp.start)

        def wait():
            for cond, cp in recvs:
                _when(cond, cp.wait_recv)
            for cond, cp in sends:
                _when(cond, cp.wait_send)
            for cp in locals_:
                cp.wait()

        return start, wait


def _call(body, comm, *, name, grid, in_specs, out_specs, out_shape, scratch_shapes, args, aliases=None):
    io_alias = dict(aliases or {})
    semantics = ("arbitrary",) * len(grid)
    if comm is None:
        return pl.pallas_call(body, name=name, grid=grid, in_specs=in_specs, out_specs=out_specs, out_shape=out_shape,
                              scratch_shapes=scratch_shapes, input_output_aliases=io_alias,
                              compiler_params=_params(semantics))(*args)
    n_in, n_out, n_scr = len(args), len(out_shape), len(scratch_shapes)
    c_in, c_out, rows = len(comm.operands), len(comm.out_shapes), len(comm.flows)

    def hosted(*refs):
        ins, refs = refs[:n_in], refs[n_in:]
        cins, refs = refs[:c_in], refs[c_in:]
        outs, refs = refs[:n_out], refs[n_out:]
        couts, refs = refs[:c_out], refs[c_out:]
        scr, (send_sems, recv_sems, local_sems) = refs[:n_scr], refs[n_scr:]
        start, wait = comm.plan(cins, couts, send_sems, recv_sems, local_sems)
        ids = [pl.program_id(d) for d in range(len(grid))]
        first, last = ids[0] == 0, ids[0] == grid[0] - 1
        for d in range(1, len(grid)):
            first, last = first & (ids[d] == 0), last & (ids[d] == grid[d] - 1)
        pl.when(first)(start)
        body(*ins, *outs, *scr)
        pl.when(last)(wait)

    for ci, co in comm.aliases.items():
        io_alias[n_in + ci] = n_out + co
    any_spec = pl.BlockSpec(memory_space=pl.ANY)
    sems = [pltpu.SemaphoreType.DMA((rows, N_DEV - 1)), pltpu.SemaphoreType.DMA((rows, N_DEV - 1)),
            pltpu.SemaphoreType.DMA((rows,))]
    return pl.pallas_call(
        hosted, name=name, grid=grid, in_specs=list(in_specs) + [any_spec] * c_in,
        out_specs=list(out_specs) + [any_spec] * c_out, out_shape=list(out_shape) + list(comm.out_shapes),
        scratch_shapes=list(scratch_shapes) + sems, input_output_aliases=io_alias,
        compiler_params=_params(semantics))(*args, *comm.operands)


def _head_row(ref, width, rep):
    hid = lax.broadcasted_iota(jnp.int32, (1, width), 1) // rep
    row = jnp.zeros((1, width), F32)
    for h in range(N_HEADS):
        row = jnp.where(hid == h, ref[h], row)
    return row


def _rows_from_above(u_b, s, ext_scr, row, col):
    down = (row - col == s).astype(_MXU)
    return jnp.concatenate([ext_scr[8 - s:16 - s, :], jnp.dot(down, u_b, preferred_element_type=F32)[8:128]], axis=0)


def _ssd_recompute(first, p_ref, halo_ref, cw_ref, cb_ref, dtb_ref, alog_ref, e_ref, ext_scr, pre=None):
    row = lax.broadcasted_iota(jnp.int32, (128, 128), 0)
    col = lax.broadcasted_iota(jnp.int32, (128, 128), 1)
    ext_scr[0:8, :] = jnp.where(first, 0.0, halo_ref[:, S_XS:S_DT])
    if pre is not None:
        ext_scr[8:16, :] = p_ref[0:8, S_XS:S_DT]
    else:
        ext_scr[8:136, :] = p_ref[:, S_XS:S_DT]
        cw = cw_ref[...]
        pre = (cb_ref[0:1, :] + cw[3:4, :] * ext_scr[8:136, :] + cw[2:3, :] * ext_scr[7:135, :]
               + cw[1:2, :] * ext_scr[6:134, :] + cw[0:1, :] * ext_scr[5:133, :])
    sg = _sigmoid(pre)
    act = pre * sg
    lane = lax.broadcasted_iota(jnp.int32, (1, 128), 1)
    A = jnp.where(lane < N_HEADS, -jnp.exp(_head_row(alog_ref, 128, 1)), 0.0)
    raw = p_ref[:, S_DT:S_DT + 128] + _head_row(dtb_ref, 128, 1)
    dt = _softplus(raw)
    dA = dt * A
    tril = (row >= col).astype(BF16)
    acs = _mm_exact_l(tril, dA)
    last = acs[127:128, :]
    ds = jnp.exp(last - acs)
    eo = jnp.exp(acs)
    E = e_ref[...]
    ex = _mm_exact_r(jnp.concatenate([dt, ds, eo], axis=0), E)
    dt_e, ds_e, eo_e = ex[0:128], ex[128:256], ex[256:384]
    xs_c = act[:, 0:1024]
    X = xs_c * dt_e
    return dict(pre=pre, sg=sg, xs_c=xs_c, Bc=act[:, 1024:1280], Cc=act[:, 1280:1536], A=A, raw=raw, dt=dt,
                acs=acs, acsT=acs.T, eo_e=eo_e, ds_e=ds_e, dt_e=dt_e, cd_e=eo_e[127:128, :],
                X=X, Xd=X * ds_e, row=row, col=col)


def _split_halves(t):
    lo = _lo_half(CHUNK)
    return jnp.concatenate([jnp.where(lo, t, 0.0), jnp.where(lo, 0.0, t)], axis=0)


def _ssd_core(R, hprev):
    causal = R["row"] >= R["col"]
    acs, acsT, X = R["acs"], R["acsT"], R["X"]
    ydiag, yoff, snew = [], [], []
    for g in range(SSD_GROUPS):
        Bg = R["Bc"][:, g * 128:(g + 1) * 128]
        Cg = R["Cc"][:, g * 128:(g + 1) * 128]
        cols = slice(g * 512, (g + 1) * 512)
        CB = _mm_nt(Cg, Bg)
        snew.append(_mm_tn(Bg, R["Xd"][:, cols]))
        yoff.append(_mm(Cg, hprev[:, cols]))
        for j in range(4):
            h0 = g * 8 + 2 * j
            ms = [CB * jnp.exp(jnp.where(causal, acs[:, h:h + 1] - acsT[h:h + 1, :], NEG)) for h in (h0, h0 + 1)]
            ydiag.append(_mm(jnp.concatenate(ms, axis=1), _split_halves(X[:, h0 * HEAD_DIM:h0 * HEAD_DIM + 128])))
    Y = jnp.concatenate(ydiag, axis=1) + jnp.concatenate(yoff, axis=1) * R["eo_e"]
    return Y, jnp.concatenate(snew, axis=1)


def _ssd_forward_step(p_ref, halo_ref, cw_ref, cb_ref, dtb_ref, alog_ref, dsk_ref, nw_ref, e_ref,
                      y_ref, ypre_ref, hprev_ref, pre_ref, h_scr, ext_scr):
    c = pl.program_id(0)
    first = c == 0

    @pl.when(first)
    def _():
        h_scr[...] = jnp.zeros_like(h_scr)

    R = _ssd_recompute(first, p_ref, halo_ref, cw_ref, cb_ref, dtb_ref, alog_ref, e_ref, ext_scr)
    hprev = h_scr[...]
    hprev_ref[...] = hprev
    pre_ref[...] = R["pre"]
    Y, snew = _ssd_core(R, hprev)
    h_scr[...] = hprev * R["cd_e"] + snew
    Y = Y + _head_row(dsk_ref, D_SSD, HEAD_DIM) * R["xs_c"]
    ypre_ref[...] = Y
    z = p_ref[:, S_Z:S_Z + 1024]
    yf = Y * (z * _sigmoid(z))
    outs = []
    for g in range(SSD_GROUPS):
        yg = yf[:, g * 512:(g + 1) * 512]
        r = lax.rsqrt(jnp.mean(yg * yg, axis=-1, keepdims=True) + RMS_EPS)
        outs.append(yg * r)
    y_ref[:, 0:D_SSD] = (jnp.concatenate(outs, axis=1) * nw_ref[0:1, :]).astype(y_ref.dtype)


def _ssd_backward(proj_ssd, hprev_all, ypre, pre, dy, conv_w8, conv_b8, dtb8, alog8, dskip_e, norm_w, E, ET, comm=None):
    L = proj_ssd.shape[0]
    nc = L // CHUNK

    def body(p_ref, halo_ref, hprev_ref, ypre_ref, pre_ref, dy_ref, cw_ref, cb_ref, dtb_ref, alog_ref, dsk_ref, nw_ref, e_ref,
             et_ref, dp_ref, acc_cw_ref, acc_w_ref, acc_s_ref, dh_scr, ext_scr, ext2_scr, nxt_scr):
        i = pl.program_id(0)
        c = nc - 1 - i
        first = c == 0

        @pl.when(i == 0)
        def _():
            dh_scr[...] = jnp.zeros_like(dh_scr)
            nxt_scr[...] = jnp.zeros_like(nxt_scr)
            acc_cw_ref[...] = jnp.zeros_like(acc_cw_ref)
            acc_w_ref[...] = jnp.zeros_like(acc_w_ref)
            acc_s_ref[...] = jnp.zeros_like(acc_s_ref)

        R = _ssd_recompute(first, p_ref, halo_ref, cw_ref, cb_ref, dtb_ref, alog_ref, e_ref, ext_scr, pre_ref[...])
        hprev = hprev_ref[...]
        xs_c, X, Xd = R["xs_c"], R["X"], R["Xd"]
        acs, acsT = R["acs"], R["acsT"]
        ET = et_ref[...]
        dsk = _head_row(dsk_ref, D_SSD, HEAD_DIM)
        Y = ypre_ref[...]

        z = p_ref[:, S_Z:S_Z + 1024]
        sz = _sigmoid(z)
        silz = z * sz
        yf = Y * silz
        dyv = dy_ref[...]
        nw = nw_ref[0:1, :]
        dyf_parts, dnw_parts = [], []
        for g in range(SSD_GROUPS):
            cols = slice(g * 512, (g + 1) * 512)
            yg = yf[:, cols]
            r = lax.rsqrt(jnp.mean(yg * yg, axis=-1, keepdims=True) + RMS_EPS)
            yn = yg * r
            dyn = dyv[:, cols] * nw[:, cols]
            dnw_parts.append(_colsum(dyv[:, cols] * yn))
            dyf_parts.append(r * (dyn - yn * jnp.mean(dyn * yn, axis=-1, keepdims=True)))
        dyf = jnp.concatenate(dyf_parts, axis=1)
        dY = dyf * silz
        dz = dyf * Y * (sz * (1.0 + z * (1.0 - sz)))

        dhn = dh_scr[...]
        dYo = dY * R["eo_e"]
        causal = R["row"] >= R["col"]
        dacs = jnp.zeros((128, 128), F32)
        dacs_t = jnp.zeros((128, 128), F32)
        dxdiag, dxd, dhprev, dBs, dCs, yoff = [], [], [], [], [], []
        for g in range(SSD_GROUPS):
            Bg = R["Bc"][:, g * 128:(g + 1) * 128]
            Cg = R["Cc"][:, g * 128:(g + 1) * 128]
            cols = slice(g * 512, (g + 1) * 512)
            CB = _mm_nt(Cg, Bg)
            dCB = jnp.zeros((128, 128), F32)
            for j in range(4):
                h0 = g * 8 + 2 * j
                pc = slice(h0 * HEAD_DIM, h0 * HEAD_DIM + 128)
                dYst = _split_halves(dY[:, pc])
                dMst = _mm_nt(dYst, X[:, pc])
                mts = []
                for a, h in enumerate((h0, h0 + 1)):
                    acol = acs[:, h:h + 1]
                    arow = acsT[h:h + 1, :]
                    Lm = jnp.exp(jnp.where(causal, acol - arow, NEG))
                    M = CB * Lm
                    dM = dMst[a * 128:(a + 1) * 128]
                    dCB = dCB + dM * Lm
                    G = dM * M
                    dacs = dacs + jnp.where(R["col"] == h, jnp.sum(G, axis=1, keepdims=True), 0.0)
                    dacs_t = dacs_t + jnp.where(R["row"] == h, jnp.sum(G, axis=0, keepdims=True), 0.0)
                    mts.append(M.T)
                dxdiag.append(_mm(jnp.concatenate(mts, axis=1), dYst))
            dS = dhn[:, cols]
            dxd.append(_mm(Bg, dS))
            yoff.append(_mm(Cg, hprev[:, cols]))
            dhprev.append(_mm_tn(Cg, dYo[:, cols]))
            dCs.append(_mm_nt(dYo[:, cols], hprev[:, cols]) + _mm(dCB, Bg))
            dBs.append(_mm_tn(dCB, Cg) + _mm_nt(Xd[:, cols], dS))
        Yoff = jnp.concatenate(yoff, axis=1) * R["eo_e"]
        dXd = jnp.concatenate(dxd, axis=1)
        dX = jnp.concatenate(dxdiag, axis=1) + dXd * R["ds_e"]
        t_state = dXd * Xd
        dacs = dacs + _mm_exact_r(dY * Yoff - t_state, ET) - dacs_t.T
        v_last = _colsum(t_state + dhn * hprev * R["cd_e"])
        dlast = _mm_exact_r(jnp.broadcast_to(v_last, (8, 1024)), ET)[0:1, :]
        dacs = dacs + jnp.where(R["row"] == 127, dlast, 0.0)
        triu = (R["col"] >= R["row"]).astype(BF16)
        da = _mm_exact_l(triu, dacs)
        ddt = da * R["A"] + _mm(dX * xs_c, ET)
        ddt_raw = ddt * _sigmoid(R["raw"])
        dxs_c = dX * R["dt_e"] + dY * dsk
        dh_scr[...] = jnp.concatenate(dhprev, axis=1) + dhn * R["cd_e"]

        dact = jnp.concatenate([dxs_c] + dBs + dCs, axis=1)
        pre, sg = R["pre"], R["sg"]
        dpre = dact * (sg * (1.0 + pre * (1.0 - sg)))
        ext2_scr[0:8, :] = dpre[120:128, :]
        ext2_scr[8:16, :] = nxt_scr[...]
        nxt_scr[...] = dpre[0:8, :]
        cw = cw_ref[...]
        u_b, dpre_b = p_ref[:, S_XS:S_DT].astype(_MXU), dpre.astype(_MXU)
        dxbc = cw[3:4, :] * dpre
        taps = [_colsum(dpre * p_ref[:, S_XS:S_DT])]
        for s in (1, 2, 3):
            up = (R["col"] - R["row"] == s).astype(_MXU)
            d_s = jnp.concatenate([jnp.dot(up, dpre_b, preferred_element_type=F32)[0:120],
                                   ext2_scr[s:8 + s, :]], axis=0)
            dxbc = dxbc + cw[3 - s:4 - s, :] * d_s
            taps.append(_colsum(dpre * _rows_from_above(u_b, s, ext_scr, R["row"], R["col"])))
        acc_cw_ref[...] += _rows8(taps[::-1] + [_colsum(dpre)])
        acc_w_ref[...] += _rows8([jnp.concatenate(dnw_parts, axis=1), _colsum(dY * xs_c)])
        acc_s_ref[...] += _rows8([_colsum(ddt_raw), _colsum(da * R["dt"])])

        lane = lax.broadcasted_iota(jnp.int32, (128, 128), 1)
        dp_ref[:, S_Z:S_Z + 1024] = dz.astype(dp_ref.dtype)
        dp_ref[:, S_XS:S_DT] = dxbc.astype(dp_ref.dtype)
        dp_ref[:, S_DT:S_DT + 128] = jnp.where(lane < N_HEADS, ddt_raw, 0.0).astype(dp_ref.dtype)
        dp_ref[:, S_DT + 128:S_W] = jnp.zeros((128, 128), dp_ref.dtype)

        @pl.when(i == nc - 1)
        def _():
            acc = acc_s_ref[...]
            dskip = _mm_exact_r(acc_w_ref[...], ET)[1:2, :]
            acc_s_ref[...] = _rows8([acc[0:1, :], acc[1:2, :] * R["A"], dskip])

    const = lambda shape: pl.BlockSpec(shape, lambda i: (0, 0))
    smem = pl.BlockSpec(memory_space=pltpu.SMEM)
    rev = lambda i: (nc - 1 - i, 0)
    return _call(
        body, comm, name="ssd_bwd", grid=(nc,),
        in_specs=[pl.BlockSpec((CHUNK, S_W), rev),
                  pl.BlockSpec((8, S_W), lambda i: (jnp.maximum((nc - 1 - i) * 16 - 1, 0), 0)),
                  pl.BlockSpec((128, 1024), rev),
                  pl.BlockSpec((CHUNK, D_SSD), rev),
                  pl.BlockSpec((CHUNK, D_XBC), rev),
                  pl.BlockSpec((CHUNK, D_SSD), rev),
                  const((4, D_XBC)), const((1, D_XBC)), smem, smem, smem, const((1, 1024)),
                  const((128, 1024)), const((1024, 128))],
        out_specs=[pl.BlockSpec((CHUNK, S_W), rev), const((8, D_XBC)), const((8, 1024)), const((8, 128))],
        out_shape=[jax.ShapeDtypeStruct((L, S_W), _MXU), jax.ShapeDtypeStruct((8, D_XBC), F32),
                   jax.ShapeDtypeStruct((8, 1024), F32), jax.ShapeDtypeStruct((8, 128), F32)],
        scratch_shapes=[pltpu.VMEM((128, 1024), F32), pltpu.VMEM((16, D_XBC), F32),
                        pltpu.VMEM((16, D_XBC), F32), pltpu.VMEM((8, D_XBC), F32)],
        args=(proj_ssd, proj_ssd, hprev_all, ypre, pre, dy, conv_w8, conv_b8, dtb8, alog8, dskip_e, norm_w, E, ET))


def _rope(t, tab):
    cos, sa, sb = tab[:, 0:128], tab[:, 128:256], tab[:, 256:384]
    outs = []
    for i in range(t.shape[1] // 128):
        tg = t[:, i * 128:(i + 1) * 128]
        outs.append(tg * cos + pltpu.roll(tg, 8, 1) * sa + pltpu.roll(tg, 120, 1) * sb)
    return jnp.concatenate(outs, axis=1)


def _rope_transposed(d, tab):
    cos, sa, sb = tab[:, 0:128], tab[:, 128:256], tab[:, 256:384]
    outs = []
    for i in range(d.shape[1] // 128):
        dg = d[:, i * 128:(i + 1) * 128]
        outs.append(dg * cos + pltpu.roll(dg * sa, 120, 1) + pltpu.roll(dg * sb, 8, 1))
    return jnp.concatenate(outs, axis=1)


def _lo_half(rows):
    return lax.broadcasted_iota(jnp.int32, (rows, 128), 1) < HEAD_DIM


def _kv_both(t, j):
    p, b = j // 2, j % 2
    lo = _lo_half(t.shape[0])
    nat = jnp.where(lo if b == 0 else jnp.logical_not(lo), t[:, p * 128:(p + 1) * 128], 0.0)
    return nat + pltpu.roll(nat, HEAD_DIM, 1)


def _stack_heads(t, j):
    lo = _lo_half(CHUNK)
    hi = jnp.logical_not(lo)
    a, b = t[:, 2 * j * 128:(2 * j + 1) * 128], t[:, (2 * j + 1) * 128:(2 * j + 2) * 128]
    return jnp.concatenate([jnp.where(lo, a, 0.0), jnp.where(hi, a, 0.0),
                            jnp.where(lo, b, 0.0), jnp.where(hi, b, 0.0)], axis=0)


def _unstack_heads(s):
    lo = _lo_half(CHUNK)
    return jnp.concatenate([jnp.where(lo, s[0:128], s[128:256]), jnp.where(lo, s[256:384], s[384:512])], axis=1)


def _fold_kv(r, j):
    lo = _lo_half(r.shape[0])
    return jnp.where(lo if j % 2 == 0 else jnp.logical_not(lo), r + pltpu.roll(r, HEAD_DIM, 1), 0.0)


def _sink_row(sink_ref, j):
    hid = lax.broadcasted_iota(jnp.int32, (1, 4 * CHUNK), 1) // CHUNK
    row = jnp.zeros((1, 4 * CHUNK), F32)
    for hh in range(4):
        row = jnp.where(hid == hh, sink_ref[4 * j + hh], row)
    return row


def _from_current(n_rows=CHUNK):
    si = lax.broadcasted_iota(jnp.int32, (n_rows, 4 * CHUNK), 0)
    qi = lax.broadcasted_iota(jnp.int32, (n_rows, 4 * CHUNK), 1) % CHUNK
    return si <= qi


def _fold(full, from_cur, pen=0.0):
    return jnp.where(from_cur, full[CHUNK:2 * CHUNK], full[0:CHUNK] + pen)


def _unfold(t, from_cur):
    c = jnp.where(from_cur, t, 0.0)
    return jnp.concatenate([t - c, c], axis=0)


def _softmax_sink(s, sink):
    mx = jnp.maximum(jnp.max(s, axis=0, keepdims=True), sink)
    p = jnp.exp(s - mx)
    esink = jnp.exp(sink - mx)
    inv = 1.0 / (jnp.sum(p, axis=0, keepdims=True) + esink)
    return p * inv, esink * inv


def _swa_inputs(blk, p_ref, prev_ref, tab_ref, ptab_ref):
    tab = tab_ref[...]
    qr = _rope(p_ref[:, A_Q:A_Q + 1024], tab) * ATT_SCALE
    kk = jnp.concatenate([_rope(prev_ref[:, 0:256], ptab_ref[...]), _rope(p_ref[:, A_K:A_K + 256], tab)], axis=0)
    vv = jnp.concatenate([prev_ref[:, 256:512], p_ref[:, A_V:A_V + 256]], axis=0)
    return tab, qr, kk, vv, jnp.where(blk > 0, 0.0, NEG)


def _swa_forward_step(sink_ref, p_ref, prev_ref, tab_ref, ptab_ref, y_ref):
    n = pl.program_id(0)
    _, qr, kk, vv, pen = _swa_inputs(n, p_ref, prev_ref, tab_ref, ptab_ref)
    from_cur = _from_current()
    outs = []
    for j in range(KV_HEADS):
        s = _fold(_mm_nt(_kv_both(kk, j), _stack_heads(qr, j)), from_cur, pen)
        P, _ = _softmax_sink(s, _sink_row(sink_ref, j))
        outs.append(_unstack_heads(_mm_tn(_unfold(P, from_cur), _kv_both(vv, j))))
    g = p_ref[:, A_G:A_G + 1024]
    y_ref[:, D_SSD:D_SSD + D_ATT] = (jnp.concatenate(outs, axis=1) * (g * _sigmoid(g))).astype(y_ref.dtype)


def _mixer_forward(proj_ssd, proj_att, tabs, sinks, conv_w, conv_b, dt_bias, a_log, d_skip, norm_w, E, comm=None):
    L = proj_ssd.shape[0]
    nc = L // CHUNK

    def body(p_ref, halo_ref, cw_ref, cb_ref, dtb_ref, alog_ref, dsk_ref, nw_ref, e_ref,
             sink_ref, pa_ref, prev_ref, tab_ref, ptab_ref, y_ref, ypre_ref, hprev_ref, pre_ref, h_scr, ext_scr):
        _ssd_forward_step(p_ref, halo_ref, cw_ref, cb_ref, dtb_ref, alog_ref, dsk_ref, nw_ref, e_ref,
                          y_ref, ypre_ref, hprev_ref, pre_ref, h_scr, ext_scr)
        _swa_forward_step(sink_ref, pa_ref, prev_ref, tab_ref, ptab_ref, y_ref)

    const = lambda shape: pl.BlockSpec(shape, lambda c: (0, 0))
    smem = pl.BlockSpec(memory_space=pltpu.SMEM)
    rows = lambda w: pl.BlockSpec((CHUNK, w), lambda c: (c, 0))
    return _call(
        body, comm, name="mixer_fwd", grid=(nc,),
        in_specs=[rows(S_W), pl.BlockSpec((8, S_W), lambda c: (jnp.maximum(c * 16 - 1, 0), 0)),
                  const((4, D_XBC)), const((1, D_XBC)), smem, smem, smem, const((1, 1024)), const((128, 1024)),
                  smem, rows(A_W), pl.BlockSpec((CHUNK, 512), lambda c: (jnp.maximum(c - 1, 0), 2)),
                  rows(384), pl.BlockSpec((CHUNK, 384), lambda c: (jnp.maximum(c - 1, 0), 0))],
        out_specs=[rows(D_SSD + D_ATT), rows(D_SSD), pl.BlockSpec((128, 1024), lambda c: (c, 0)), rows(D_XBC)],
        out_shape=[jax.ShapeDtypeStruct((L, D_SSD + D_ATT), _MXU), jax.ShapeDtypeStruct((L, D_SSD), F32),
                   jax.ShapeDtypeStruct((nc * 128, 1024), F32), jax.ShapeDtypeStruct((L, D_XBC), F32)],
        scratch_shapes=[pltpu.VMEM((128, 1024), F32), pltpu.VMEM((136, D_XBC), F32)],
        args=(proj_ssd, proj_ssd, conv_w, conv_b, dt_bias, a_log, d_skip, norm_w, E,
              sinks, proj_att, proj_att, tabs, tabs))


def _swa_backward(proj_att, tabs, sinks, dy, comm=None):
    L = proj_att.shape[0]
    nb = L // CHUNK

    def body(sink_ref, p_ref, prev_ref, tab_ref, ptab_ref, dy_ref, dp_ref, dsink_ref, carry_k, carry_v):
        i = pl.program_id(0)
        n = nb - 1 - i

        @pl.when(i == 0)
        def _():
            carry_k[...] = jnp.zeros_like(carry_k)
            carry_v[...] = jnp.zeros_like(carry_v)
            dsink_ref[...] = jnp.zeros_like(dsink_ref)

        tab, qr, kk, vv, pen = _swa_inputs(n, p_ref, prev_ref, tab_ref, ptab_ref)
        from_cur = _from_current()
        g = p_ref[:, A_G:A_G + 1024]
        sgm = _sigmoid(g)
        dyv = dy_ref[...]
        do_all = dyv * (g * sgm)
        lane8 = lax.broadcasted_iota(jnp.int32, (8, 128), 1)
        hid = lax.broadcasted_iota(jnp.int32, (1, 4 * CHUNK), 1) // CHUNK
        o_parts, dq_parts = [], []
        dk_nat = [jnp.zeros((2 * CHUNK, 128), F32) for _ in range(2)]
        dv_nat = [jnp.zeros((2 * CHUNK, 128), F32) for _ in range(2)]
        dsink = jnp.zeros((8, 128), F32)
        for j in range(KV_HEADS):
            qs = _stack_heads(qr, j)
            kkb, vvb = _kv_both(kk, j), _kv_both(vv, j)
            P, psink = _softmax_sink(_fold(_mm_nt(kkb, qs), from_cur, pen), _sink_row(sink_ref, j))
            p_full = _unfold(P, from_cur)
            o_parts.append(_unstack_heads(_mm_tn(p_full, vvb)))
            do_s = _stack_heads(do_all, j)
            dP = _fold(_mm_nt(vvb, do_s), from_cur)
            D = jnp.sum(P * dP, axis=0, keepdims=True)
            ds_full = _unfold(P * (dP - D), from_cur)
            sd = psink * D
            for hh in range(4):
                dsink = dsink + jnp.where(lane8 == 4 * j + hh, -jnp.sum(jnp.where(hid == hh, sd, 0.0)), 0.0)
            dq_parts.append(_unstack_heads(_mm_tn(ds_full, kkb)) * ATT_SCALE)
            dk_nat[j // 2] = dk_nat[j // 2] + _fold_kv(_mm(ds_full, qs), j)
            dv_nat[j // 2] = dv_nat[j // 2] + _fold_kv(_mm(p_full, do_s), j)
        o = jnp.concatenate(o_parts, axis=1)
        dkk = jnp.concatenate(dk_nat, axis=1)
        dvv = jnp.concatenate(dv_nat, axis=1)
        out = dp_ref.dtype
        dp_ref[:, A_Q:A_Q + 1024] = _rope_transposed(jnp.concatenate(dq_parts, axis=1), tab).astype(out)
        dp_ref[:, A_K:A_K + 256] = _rope_transposed(dkk[CHUNK:2 * CHUNK] + carry_k[...], tab).astype(out)
        dp_ref[:, A_V:A_V + 256] = (dvv[CHUNK:2 * CHUNK] + carry_v[...]).astype(out)
        dp_ref[:, A_G:A_G + 1024] = (dyv * o * (sgm * (1.0 + g * (1.0 - sgm)))).astype(out)
        carry_k[...] = dkk[0:CHUNK]
        carry_v[...] = dvv[0:CHUNK]
        dsink_ref[...] += dsink

    rev = lambda i: (nb - 1 - i, 0)
    prev = lambda i: jnp.maximum(nb - 2 - i, 0)
    return _call(
        body, comm, name="swa_bwd", grid=(nb,),
        in_specs=[pl.BlockSpec(memory_space=pltpu.SMEM),
                  pl.BlockSpec((CHUNK, A_W), rev),
                  pl.BlockSpec((CHUNK, 512), lambda i: (prev(i), 2)),
                  pl.BlockSpec((CHUNK, 384), rev),
                  pl.BlockSpec((CHUNK, 384), lambda i: (prev(i), 0)),
                  pl.BlockSpec((CHUNK, D_ATT), lambda i: (nb - 1 - i, 1))],
        out_specs=[pl.BlockSpec((CHUNK, A_W), rev), pl.BlockSpec((8, 128), lambda i: (0, 0))],
        out_shape=[jax.ShapeDtypeStruct((L, A_W), _MXU), jax.ShapeDtypeStruct((8, 128), F32)],
        scratch_shapes=[pltpu.VMEM((CHUNK, 256), F32), pltpu.VMEM((CHUNK, 256), F32)],
        args=(sinks, proj_att, proj_att, tabs, tabs, dy))


def _head(y, x, target, w_out, ln_g8, ln_b8, *, tm):
    L = x.shape[0]
    nsteps = L // tm

    def body(y_ref, x_ref, t_ref, wo_ref, g_ref, b_ref, dr_ref, dy_ref, acc_ref):
        i = pl.program_id(0)

        @pl.when(i == 0)
        def _():
            acc_ref[...] = jnp.zeros_like(acc_ref)

        r = ALPHA * x_ref[...] + _mm(y_ref[...], wo_ref[...])
        mu = jnp.mean(r, axis=-1, keepdims=True)
        d = r - mu
        rstd = lax.rsqrt(jnp.mean(d * d, axis=-1, keepdims=True) + LN_EPS)
        xh = d * rstd
        gam = g_ref[0:1, :]
        e = xh * gam + b_ref[0:1, :] - t_ref[...]
        dout = e * (1.0 / D_MODEL)
        dxh = dout * gam
        dr = rstd * (dxh - jnp.mean(dxh, axis=-1, keepdims=True)
                     - xh * jnp.mean(dxh * xh, axis=-1, keepdims=True))
        dr_ref[...] = dr
        dy_ref[...] = _mm_nt(dr, wo_ref[...])
        acc_ref[...] += _rows8([_colsum(dout * xh), _colsum(dout), _colsum(e * e) * (0.5 / D_MODEL)])

        @pl.when(i == nsteps - 1)
        def _():
            acc = acc_ref[...]
            tot = jnp.sum(acc[2:3, :])
            rid = lax.broadcasted_iota(jnp.int32, (8, 1024), 0)
            acc_ref[...] = jnp.where(rid == 3, tot, acc)

    const = lambda shape: pl.BlockSpec(shape, lambda i: (0, 0))
    row = lambda w: pl.BlockSpec((tm, w), lambda i: (i, 0))
    return pl.pallas_call(
        body, name="head", grid=(nsteps,),
        in_specs=[row(2048), row(1024), row(1024), const((2048, 1024)), const((1, 1024)), const((1, 1024))],
        out_specs=[row(1024), row(2048), const((8, 1024))],
        out_shape=[jax.ShapeDtypeStruct((L, D_MODEL), F32), jax.ShapeDtypeStruct((L, 2048), F32),
                   jax.ShapeDtypeStruct((8, 1024), F32)],
        compiler_params=_params(("arbitrary",)),
    )(y, x, target, w_out, ln_g8, ln_b8)


def _gather_w_in(w_shard):
    R = w_shard.shape[0]
    halves = (pl.ds(0, R // 2), pl.ds(R // 2, R // 2))
    any_spec = pl.BlockSpec(memory_space=pl.ANY)

    def body(in_ref, out_ref, send_sems, recv_sems, local_sem):
        x, y, c = _position()

        def slot(p, half=None):
            s = out_ref.at[_index(*p)]
            return s if half is None else s.at[halves[half]]

        def same_core(p):
            return (p[0], p[1], c)

        def other_core(p):
            return (p[0], p[1], 1 - c)

        me, xn, yn, dg = (x, y), (1 - x, y), (x, 1 - y), (1 - x, 1 - y)

        def copy(k, dst, to, src=None):
            return _remote(dst if src is None else src, dst, send_sems.at[k], recv_sems.at[k], to)

        local = pltpu.make_async_copy(in_ref, slot(same_core(me)), local_sem)
        local.start()
        own = [copy(0, slot(same_core(me)), other_core(me), in_ref), copy(1, slot(same_core(me)), same_core(xn), in_ref),
               copy(2, slot(same_core(me)), same_core(yn), in_ref)]
        for cp in own:
            cp.start()
        copy(1, slot(same_core(xn)), same_core(xn)).wait_recv()
        passed = [copy(4, slot(same_core(xn), 1), same_core(yn)), copy(5, slot(same_core(xn)), other_core(me))]
        for cp in passed:
            cp.start()
        copy(2, slot(same_core(yn)), same_core(yn)).wait_recv()
        more = [copy(3, slot(same_core(yn), 0), same_core(xn)), copy(6, slot(same_core(yn)), other_core(me))]
        for cp in more:
            cp.start()
        passed += more
        for k, half in ((3, 0), (4, 1)):
            copy(k, slot(same_core(dg), half), same_core(xn)).wait_recv()
            fwd = copy(7 + half, slot(same_core(dg), half), other_core(me))
            fwd.start()
            passed.append(fwd)
        copy(0, slot(other_core(me)), other_core(me)).wait_recv()
        copy(5, slot(other_core(xn)), other_core(me)).wait_recv()
        copy(6, slot(other_core(yn)), other_core(me)).wait_recv()
        for half in (0, 1):
            copy(7 + half, slot(other_core(dg), half), other_core(me)).wait_recv()
        for cp in own + passed:
            cp.wait_send()
        local.wait()

    return pl.pallas_call(
        body, name="gather_w_in", in_specs=[any_spec], out_specs=any_spec,
        out_shape=jax.ShapeDtypeStruct((N_DEV,) + w_shard.shape, w_shard.dtype),
        scratch_shapes=[pltpu.SemaphoreType.DMA((9,)), pltpu.SemaphoreType.DMA((9,)), pltpu.SemaphoreType.DMA],
    )(w_shard)


def _input_gradient(d_ssd, d_att, w_ssd, w_att, dr, *, tm, comm=None):
    L = dr.shape[0]

    def body(ds_ref, da_ref, ws_ref, wa_ref, dr_ref, o_ref):
        o_ref[...] = ALPHA * dr_ref[...] + _mm_nt(ds_ref[...], ws_ref[...]) + _mm_nt(da_ref[...], wa_ref[...])

    row = lambda w: pl.BlockSpec((tm, w), lambda i: (i, 0))
    const = lambda shape: pl.BlockSpec(shape, lambda i: (0, 0))
    return _call(body, comm, name="dx", grid=(L // tm,),
                 in_specs=[row(S_W), row(A_W), const((D_MODEL, S_W)), const((D_MODEL, A_W)), row(D_MODEL)],
                 out_specs=[row(D_MODEL)], out_shape=[jax.ShapeDtypeStruct((L, D_MODEL), F32)],
                 scratch_shapes=[], args=(d_ssd, d_att, w_ssd, w_att, dr))


SHARD_COLS = D_IN_PROJ // N_DEV
SPLIT = N_SSD_REAL - 4 * SHARD_COLS
RELAYOUT_ROWS = 256


def _unpack_w_in(w_all):
    def body(g_ref, ws_ref, wa_ref):
        for j in range(4):
            ws_ref[:, SHARD_COLS * j:SHARD_COLS * (j + 1)] = g_ref[j]
        ws_ref[:, 4 * SHARD_COLS:N_SSD_REAL] = g_ref[4, :, 0:SPLIT]
        ws_ref[:, N_SSD_REAL:S_W] = jnp.zeros((RELAYOUT_ROWS, S_W - N_SSD_REAL), ws_ref.dtype)
        wa_ref[:, 0:SHARD_COLS - SPLIT] = g_ref[4, :, SPLIT:SHARD_COLS]
        for j in range(5, N_DEV):
            lo = SHARD_COLS * (j - 4) - SPLIT
            wa_ref[:, lo:lo + SHARD_COLS] = g_ref[j]

    return pl.pallas_call(
        body, name="unpack_w_in", grid=(D_MODEL // RELAYOUT_ROWS,),
        in_specs=[pl.BlockSpec((N_DEV, RELAYOUT_ROWS, SHARD_COLS), lambda i: (0, i, 0))],
        out_specs=[pl.BlockSpec((RELAYOUT_ROWS, S_W), lambda i: (i, 0)), pl.BlockSpec((RELAYOUT_ROWS, A_W), lambda i: (i, 0))],
        out_shape=[jax.ShapeDtypeStruct((D_MODEL, S_W), w_all.dtype), jax.ShapeDtypeStruct((D_MODEL, A_W), w_all.dtype)],
        compiler_params=_params(("arbitrary",)),
    )(w_all)


def _pack_dw_in(me1, dw_ssd, dw_att, half):
    def body(me_ref, *refs):
        if half == 0:
            ds_ref, p_ref, own_ref = refs
            me = me_ref[0]

            @pl.when(me >= 4)
            def _():
                own_ref[...] = jnp.zeros_like(own_ref)
        else:
            ds_ref, da_ref, p_ref = refs

        for j in range(4):
            if half == 0:
                pieces = [(0, ds_ref[:, SHARD_COLS * j:SHARD_COLS * (j + 1)])]
            elif j == 0:
                pieces = [(0, ds_ref[:, 4 * SHARD_COLS:N_SSD_REAL]), (SPLIT, da_ref[:, 0:SHARD_COLS - SPLIT])]
            else:
                lo = SHARD_COLS * j - SPLIT
                pieces = [(0, da_ref[:, lo:lo + SHARD_COLS])]
            for off, blk in pieces:
                p_ref[j, :, off:off + blk.shape[1]] = blk.astype(p_ref.dtype)
                if half == 0:
                    @pl.when(me == j)
                    def _(off=off, blk=blk):
                        own_ref[:, off:off + blk.shape[1]] = blk

    ins = [dw_ssd] if half == 0 else [dw_ssd, dw_att]
    row = lambda a: pl.BlockSpec((RELAYOUT_ROWS, a.shape[1]), lambda i: (i, 0))
    out_specs = [pl.BlockSpec((4, RELAYOUT_ROWS, SHARD_COLS), lambda i: (0, i, 0))]
    out_shape = [jax.ShapeDtypeStruct((4, D_MODEL, SHARD_COLS), BF16 if half == 0 else F32)]
    if half == 0:
        out_specs.append(pl.BlockSpec((RELAYOUT_ROWS, SHARD_COLS), lambda i: (i, 0)))
        out_shape.append(jax.ShapeDtypeStruct((D_MODEL, SHARD_COLS), F32))
    return pl.pallas_call(
        body, name="pack_dw_in_%d" % half, grid=(D_MODEL // RELAYOUT_ROWS,),
        in_specs=[pl.BlockSpec(memory_space=pltpu.SMEM)] + [row(a) for a in ins],
        out_specs=out_specs, out_shape=out_shape, compiler_params=_params(("arbitrary",)),
    )(me1, *ins)


def _pair_swap(stack):
    def body(in_ref, out_ref, send_sems, recv_sems):
        x, y, c = _position()
        cps = [_remote(in_ref.at[2 * oy + (1 - c)], out_ref.at[oy], send_sems.at[oy], recv_sems.at[oy], (x, y, 1 - c))
               for oy in range(2)]
        for cp in cps:
            cp.start()
        for cp in cps:
            cp.wait_recv()
        for cp in cps:
            cp.wait_send()

    any_spec = pl.BlockSpec(memory_space=pl.ANY)
    return pl.pallas_call(
        body, name="pair_swap", in_specs=[any_spec], out_specs=any_spec,
        out_shape=jax.ShapeDtypeStruct((2,) + stack.shape[1:], stack.dtype),
        scratch_shapes=[pltpu.SemaphoreType.DMA((2,)), pltpu.SemaphoreType.DMA((2,))],
    )(stack)


def _pair_sum(pos3, stack, swapped, own_lo):
    def body(pos_ref, a_ref, b_ref, lo_ref, chip_ref, own_ref):
        oy = pl.program_id(1)
        t = a_ref[0] + b_ref[0]
        chip_ref[0] = t.astype(chip_ref.dtype)

        @pl.when((pos_ref[0] == 0) & (oy == 0))
        def _():
            own_ref[...] = lo_ref[...]

        @pl.when((pos_ref[0] == 1) & (oy == pos_ref[1]))
        def _():
            own_ref[...] = t

    blk = (1, RELAYOUT_ROWS, SHARD_COLS)
    flat = pl.BlockSpec((RELAYOUT_ROWS, SHARD_COLS), lambda i, oy, pos: (i, 0))
    return pl.pallas_call(
        body, name="pair_sum",
        grid_spec=pltpu.PrefetchScalarGridSpec(
            num_scalar_prefetch=1, grid=(D_MODEL // RELAYOUT_ROWS, 2),
            in_specs=[pl.BlockSpec(blk, lambda i, oy, pos: (2 * oy + pos[2], i, 0)),
                      pl.BlockSpec(blk, lambda i, oy, pos: (oy, i, 0)), flat],
            out_specs=[pl.BlockSpec(blk, lambda i, oy, pos: (oy, i, 0)), flat]),
        out_shape=[jax.ShapeDtypeStruct((2, D_MODEL, SHARD_COLS), BF16), jax.ShapeDtypeStruct((D_MODEL, SHARD_COLS), F32)],
        compiler_params=_params(("arbitrary", "arbitrary")),
    )(pos3, stack, swapped, own_lo)


def _adamw_math(w, g, m, v):
    m = ADAM_B1 * m + (1.0 - ADAM_B1) * g
    v = ADAM_B2 * v + (1.0 - ADAM_B2) * (g * g)
    m_hat = m / (1.0 - ADAM_B1 ** ADAM_STEP)
    v_hat = v / (1.0 - ADAM_B2 ** ADAM_STEP)
    delta = -ADAM_LR * (m_hat / (jnp.sqrt(v_hat) + ADAM_EPS) + ADAM_WD * w)
    return delta, m, v


def _adamw_shard(n_recv, g_own, recv, w, m, v, *, rows, name):
    R, C = g_own.shape

    def body(n_ref, g_ref, r_ref, w_ref, m_ref, v_ref, go_ref, d_ref, mo_ref, vo_ref):
        g = g_ref[...]
        for k in range(N_DEV - 1):
            g = g + jnp.where(k < n_ref[0], r_ref[k].astype(F32), 0.0)
        d, mn, vn = _adamw_math(w_ref[...], g, m_ref[...], v_ref[...])
        go_ref[...] = g
        d_ref[...] = d
        mo_ref[...] = mn
        vo_ref[...] = vn

    blk = pl.BlockSpec((rows, C), lambda i: (i, 0))
    return pl.pallas_call(
        body, name=name, grid=(R // rows,),
        in_specs=[pl.BlockSpec(memory_space=pltpu.SMEM), blk,
                  pl.BlockSpec((N_DEV - 1, rows, C), lambda i: (0, i, 0)), blk, blk, blk],
        out_specs=[blk] * 4, out_shape=[jax.ShapeDtypeStruct((R, C), F32)] * 4,
        compiler_params=_params(("arbitrary",)),
    )(n_recv, g_own, recv, w, m, v)


def _minor_rows_view(a):
    return jnp.transpose(a, (2, 0, 1)).reshape(SHARD_COLS * 8, 128)


def _from_minor_rows_view(v):
    return jnp.transpose(v.reshape(SHARD_COLS, 8, 128), (1, 2, 0)).reshape(1, D_MODEL, SHARD_COLS)


def _adamw_w_in(n_recv, g_own, recv, w, m, v):
    C = SHARD_COLS
    pad = -C % 128

    def body(n_ref, g_ref, r_ref, w_ref, m_ref, v_ref, go_ref, d_ref, mo_ref, vo_ref):
        for q in range(D_MODEL // 128):
            band = pl.ds(q * 128, 128)
            g = g_ref[band, :]
            for k in range(N_DEV - 1):
                g = g + jnp.where(k < n_ref[0], r_ref[k, band, :].astype(F32), 0.0)
            g = jnp.pad(g, ((0, 0), (0, pad))).T[0:C]
            rows = pl.ds(q, C, stride=8)
            d, mn, vn = _adamw_math(w_ref[rows, :], g, m_ref[rows, :], v_ref[rows, :])
            go_ref[rows, :] = g
            d_ref[rows, :] = d
            mo_ref[rows, :] = mn
            vo_ref[rows, :] = vn

    return pl.pallas_call(
        body, name="adamw_w_in", out_shape=[jax.ShapeDtypeStruct(w.shape, F32)] * 4,
        in_specs=[pl.BlockSpec(memory_space=pltpu.SMEM)] + [pl.BlockSpec(memory_space=pltpu.VMEM)] * 5,
        out_specs=[pl.BlockSpec(memory_space=pltpu.VMEM)] * 4,
        compiler_params=_params(),
    )(n_recv, g_own, recv, w, m, v)


SMALL = ("conv_b", "dt_bias", "a_log", "d_skip", "ssd_norm_w", "attn_sinks", "ln_g", "ln_b")


def _adamw_small(gathered, params):
    n_p = len(SMALL)

    def body(*refs):
        acc = []
        for r in refs[:5]:
            t = r[0]
            for k in range(1, N_DEV):
                t = t + r[k]
            acc.append(t)
        head, conv, norm, scal, sink = acc
        grads = dict(conv_b=conv[4:5, :], dt_bias=scal[0:1, 0:N_HEADS], a_log=scal[1:2, 0:N_HEADS],
                     d_skip=scal[2:3, 0:N_HEADS], ssd_norm_w=norm[0:1, :], attn_sinks=sink[0:1, 0:N_HEADS],
                     ln_g=head[0:1, :], ln_b=head[1:2, :])
        wmv = refs[5:5 + 3 * n_p]
        outs = refs[5 + 3 * n_p:]
        outs[0][...] = head[3:4, 0:1]
        outs[1][...] = conv[0:4, :]
        for i, name in enumerate(SMALL):
            w_ref, m_ref, v_ref = wmv[3 * i:3 * i + 3]
            g = grads[name]
            d, mn, vn = _adamw_math(w_ref[...], g, m_ref[...], v_ref[...])
            for o_ref, val in zip(outs[2 + 4 * i:6 + 4 * i], (g, d, mn, vn)):
                o_ref[...] = val

    flat = [a for name in SMALL for a in params[name]]
    out_shape = [jax.ShapeDtypeStruct((1, 1), F32), jax.ShapeDtypeStruct((4, D_XBC), F32)]
    for name in SMALL:
        out_shape += [jax.ShapeDtypeStruct(params[name][0].shape, F32)] * 4
    res = pl.pallas_call(body, name="adamw_small", out_shape=out_shape, compiler_params=_params())(*gathered, *flat)
    return res[0], res[1], {name: res[2 + 4 * i:6 + 4 * i] for i, name in enumerate(SMALL)}


def _adamw_plain(g, w, m, v):
    def body(g_ref, w_ref, m_ref, v_ref, d_ref, mo_ref, vo_ref):
        d, mn, vn = _adamw_math(w_ref[...], g_ref[...], m_ref[...], v_ref[...])
        d_ref[...] = d
        mo_ref[...] = mn
        vo_ref[...] = vn

    return pl.pallas_call(
        body, name="adamw_conv_w", out_shape=[jax.ShapeDtypeStruct(w.shape, F32)] * 3,
        compiler_params=_params(),
    )(g, w, m, v)


def _lane_pattern(fn):
    return np.asarray([fn(l % HEAD_DIM) for l in range(128)], np.float32)


ROPE_INV = _lane_pattern(lambda r: ROPE_THETA ** (-2.0 * (r % 8) / ROPE_DIM) if r < ROPE_DIM else 0.0)
ROPE_SIN_A = _lane_pattern(lambda r: 1.0 if 8 <= r < ROPE_DIM else 0.0)
ROPE_SIN_B = _lane_pattern(lambda r: -1.0 if r < 8 else 0.0)


def _rope_tables(positions):
    ang = positions.astype(F32)[:, None] * ROPE_INV[None, :]
    sn = jnp.sin(ang)
    return jnp.concatenate([jnp.cos(ang), sn * ROPE_SIN_A[None, :], sn * ROPE_SIN_B[None, :]], axis=1)


def _expansion():
    E = np.arange(1024)[None, :] // HEAD_DIM == np.arange(128)[:, None]
    return jnp.asarray(E, BF16), jnp.asarray(E.T, BF16)


def _ssd_args(conv_w, conv_b, dt_bias, a_log, d_skip, norm_w, E):
    return (conv_w, conv_b, dt_bias.reshape(-1), a_log.reshape(-1), d_skip.reshape(-1), norm_w, E)


def kernel(x, positions, w_in, conv_w, conv_b, dt_bias, a_log, d_skip, ssd_norm_w, attn_sinks, w_out, ln_g, ln_b, loss_target, m_w_in, m_conv_w, m_conv_b, m_dt_bias, m_a_log, m_d_skip, m_ssd_norm_w, m_attn_sinks, m_w_out, m_ln_g, m_ln_b, v_w_in, v_conv_w, v_conv_b, v_dt_bias, v_a_log, v_d_skip, v_ssd_norm_w, v_attn_sinks, v_w_out, v_ln_g, v_ln_b):
    me = _index(*_position())
    me1 = me.reshape(1).astype(jnp.int32)
    x0, target = x[0], loss_target[0]
    bf16_shard = lambda shape: jax.ShapeDtypeStruct(shape, BF16)
    E, ET = _expansion()
    tabs = _rope_tables(positions[0])
    sinks = attn_sinks.reshape(-1)

    w_ssd, w_att = _unpack_w_in(_gather_w_in(w_in[0].astype(BF16)))
    gather_conv_w = _Hosted([conv_w[0]], [jax.ShapeDtypeStruct((N_DEV,) + conv_w.shape[1:], F32)],
                            [_Flow("gather", 0, 0)])

    proj_ssd, xb, conv_w_all = _matmul(x0, w_ssd, tm=1024, tn=S_W // 2, name="in_proj_ssd", emit_a=True,
                                       comm=gather_conv_w)
    conv_w_f = jnp.transpose(conv_w_all, (1, 0, 2)).reshape(4, D_XBC)
    ssd_args = _ssd_args(conv_w_f, conv_b, dt_bias, a_log, d_skip, ssd_norm_w, E)
    proj_att = _matmul(xb, w_att, tm=1024, tn=A_W // 2, name="in_proj_att")
    gather_w_out = _Hosted([w_out[0].astype(BF16)], [bf16_shard((N_DEV, 256, D_MODEL))], [_Flow("gather", 0, 0)])
    y, ypre, hprev, pre, w_out_all = _mixer_forward(proj_ssd, proj_att, tabs, sinks, *ssd_args, comm=gather_w_out)
    w_out_f = w_out_all.reshape(2 * D_MODEL, D_MODEL)
    dr, dy, acc_head = _head(y, x0, target, w_out_f, ln_g, ln_b, tm=512)

    dw_out, dw_out_bf16 = _matmul_tn(y, dr, tl=512, tn=D_MODEL, name="dw_out", emit_bf16=True)
    own_out = lax.dynamic_index_in_dim(dw_out.reshape(N_DEV, 256, D_MODEL), me, axis=0, keepdims=False)
    send_out = _Hosted([dw_out_bf16.reshape(N_DEV, 256, D_MODEL)], [bf16_shard((N_DEV - 1, 256, D_MODEL))],
                       [_Flow("exchange", 0, 0)])
    d_ssd, acc_cw, acc_w, acc_s, recv_out = _ssd_backward(proj_ssd, hprev, ypre, pre, dy, *ssd_args, ET, comm=send_out)
    dw_ssd = _matmul_tn(xb, d_ssd, tl=512, tn=S_W // 2, name="dw_in_ssd")
    parts_lo, own_lo = _pack_dw_in(me1, dw_ssd, None, 0)
    recv_shape = bf16_shard((N_DEV - 1, D_MODEL, SHARD_COLS))
    send_lo = _Hosted([parts_lo], [recv_shape], [_Flow("exchange", 0, 0, target_x=0, target_c=0)])
    d_att, dsink, recv_in = _swa_backward(proj_att, tabs, sinks, dy, comm=send_lo)
    dw_att = _matmul_tn(xb, d_att, tl=512, tn=A_W // 2, name="dw_in_att")
    (stack_hi,) = _pack_dw_in(me1, dw_ssd, dw_att, 1)
    pos3 = jnp.stack(_position()).astype(jnp.int32)
    chip_hi, own_in = _pair_sum(pos3, stack_hi, _pair_swap(stack_hi), own_lo)
    accs = [acc_head, acc_cw, acc_w, acc_s, dsink]
    send_hi = _Hosted([chip_hi, recv_in, parts_lo] + accs,
                      [recv_shape] + [jax.ShapeDtypeStruct((N_DEV,) + a.shape, F32) for a in accs],
                      [_Flow("chip_exchange", 0, 0, target_x=1), _Flow("exchange", 2, 0, target_x=0, target_c=1)]
                      + [_Flow("gather", 3 + i, 1 + i) for i in range(5)], aliases={1: 0})
    dx, recv_in, *gathered = _input_gradient(d_ssd, d_att, w_ssd, w_att, dr, tm=256, comm=send_hi)
    n_recv_in = jnp.where(me < 4, N_DEV - 1, 3).reshape(1).astype(jnp.int32)
    n_recv_out = jnp.full((1,), N_DEV - 1, jnp.int32)

    g_in, d_in, nm_in, nv_in = [_from_minor_rows_view(r) for r in _adamw_w_in(
        n_recv_in, own_in, recv_in, _minor_rows_view(w_in), _minor_rows_view(m_w_in), _minor_rows_view(v_w_in))]
    g_out, d_out, nm_out, nv_out = _adamw_shard(n_recv_out, own_out, recv_out, w_out[0], m_w_out[0], v_w_out[0],
                                                rows=256, name="adamw_w_out")
    loss, g_conv_w, small = _adamw_small(gathered, dict(
        conv_b=(conv_b, m_conv_b, v_conv_b), dt_bias=(dt_bias, m_dt_bias, v_dt_bias), a_log=(a_log, m_a_log, v_a_log),
        d_skip=(d_skip, m_d_skip, v_d_skip), ssd_norm_w=(ssd_norm_w, m_ssd_norm_w, v_ssd_norm_w),
        attn_sinks=(attn_sinks, m_attn_sinks, v_attn_sinks), ln_g=(ln_g, m_ln_g, v_ln_g), ln_b=(ln_b, m_ln_b, v_ln_b)))
    g_cw = lax.dynamic_slice_in_dim(g_conv_w, me * (D_XBC // N_DEV), D_XBC // N_DEV, axis=1)
    d_cw, nm_cw, nv_cw = _adamw_plain(g_cw, conv_w[0], m_conv_w[0], v_conv_w[0])

    def leaves(i, big_in, cw, big_out):
        mid = [small[k][i] for k in ("conv_b", "dt_bias", "a_log", "d_skip", "ssd_norm_w", "attn_sinks")]
        return [big_in, cw[None]] + mid + [big_out[None], small["ln_g"][i], small["ln_b"][i]]

    return (loss.reshape(()), dx[None], *leaves(0, g_in, g_cw, g_out), *leaves(1, d_in, d_cw, d_out),
            *leaves(2, nm_in, nm_cw, nm_out), *leaves(3, nv_in, nv_cw, nv_out))
```

```python
import jax
import jax.numpy as jnp
from jax import lax
from jax.experimental import pallas as pl
from jax.experimental.pallas import tpu as pltpu
import numpy as np

F32 = jnp.float32
BF16 = jnp.bfloat16
_MXU = jnp.bfloat16

N_DEV = 8
D_MODEL = 1024
D_SSD = 1024
D_ATT = 1024
HEAD_DIM = 64
N_HEADS = 16
SSD_GROUPS = 2
KV_HEADS = 4
CHUNK = 128
D_XBC = 1536
D_IN_PROJ = 5136
ROPE_DIM = 16
ROPE_THETA = 500000.0
ALPHA = (2.0 * 1) ** 0.25
LN_EPS = 1e-5
RMS_EPS = 1e-5
ATT_SCALE = HEAD_DIM ** -0.5
NEG = -1e30

S_Z, S_XS, S_B, S_C, S_DT, S_W = 0, 1024, 2048, 2304, 2560, 2816
N_SSD_REAL = 2576
A_Q, A_K, A_V, A_G, A_W = 0, 1024, 1280, 1536, 2560

ADAM_LR = 0.001
ADAM_B1 = 0.9
ADAM_B2 = 0.999
ADAM_EPS = 1e-08
ADAM_WD = 0.01
ADAM_STEP = 10

VMEM_LIMIT = 48 * 1024 * 1024
MESH = pl.DeviceIdType.MESH


def _params(sem=None):
    return pltpu.CompilerParams(dimension_semantics=sem, vmem_limit_bytes=VMEM_LIMIT)


def _mm(a, b):
    return jnp.dot(a.astype(_MXU), b.astype(_MXU), preferred_element_type=F32)


def _mm_nt(a, b):
    return lax.dot_general(a.astype(_MXU), b.astype(_MXU), (((1,), (1,)), ((), ())),
                           preferred_element_type=F32)


def _mm_tn(a, b):
    return lax.dot_general(a.astype(_MXU), b.astype(_MXU), (((0,), (0,)), ((), ())),
                           preferred_element_type=F32)


def _split3(v):
    hi = v.astype(BF16)
    r = v - hi.astype(F32)
    mid = r.astype(BF16)
    lo = (r - mid.astype(F32)).astype(BF16)
    return hi, mid, lo


def _mm_exact_r(v, p01):
    hi, mid, lo = _split3(v)
    d = lambda a: jnp.dot(a, p01, preferred_element_type=F32)
    return d(hi) + d(mid) + d(lo)


def _mm_exact_l(p01, v):
    hi, mid, lo = _split3(v)
    d = lambda a: jnp.dot(p01, a, preferred_element_type=F32)
    return d(hi) + d(mid) + d(lo)


def _mm_2pass_r(v, p01):
    hi = v.astype(BF16)
    lo = (v - hi.astype(F32)).astype(BF16)
    return jnp.dot(hi, p01, preferred_element_type=F32) + jnp.dot(lo, p01, preferred_element_type=F32)


def _sigmoid(x):
    return 1.0 / (1.0 + jnp.exp(-x))


def _softplus(x):
    e = jnp.exp(-jnp.abs(x))
    u = 1.0 + e
    log1p = jnp.where(u == 1.0, e, jnp.log(u) * (e / (u - 1.0)))
    return jnp.maximum(x, 0.0) + log1p


def _rows8(rows):
    n = rows[0].shape[1]
    rid = lax.broadcasted_iota(jnp.int32, (8, n), 0)
    out = jnp.zeros((8, n), F32)
    for k, r in enumerate(rows):
        out = out + jnp.where(rid == k, r, 0.0)
    return out


def _colsum(a):
    return jnp.sum(a, axis=0, keepdims=True)


def _matmul(a, b, *, tm, tn, name, emit_a=False, comm=None):
    M, K = a.shape
    N = b.shape[1]

    def body(a_ref, b_ref, o_ref, *rest):
        am = a_ref[...].astype(_MXU)
        o_ref[...] = jnp.dot(am, b_ref[...].astype(_MXU), preferred_element_type=F32)
        if emit_a:
            rest[0][...] = am

    out_specs = [pl.BlockSpec((tm, tn), lambda i, j: (i, j))]
    out_shape = [jax.ShapeDtypeStruct((M, N), F32)]
    if emit_a:
        out_specs.append(pl.BlockSpec((tm, K), lambda i, j: (i, 0)))
        out_shape.append(jax.ShapeDtypeStruct((M, K), _MXU))
    res = _call(
        body, comm, name=name, grid=(M // tm, N // tn),
        in_specs=[pl.BlockSpec((tm, K), lambda i, j: (i, 0)), pl.BlockSpec((K, tn), lambda i, j: (0, j))],
        out_specs=out_specs, out_shape=out_shape, scratch_shapes=[], args=(a, b))
    return res if (emit_a or comm is not None) else res[0]


def _matmul_tn(a, g, *, tl, tn, name, emit_bf16=False):
    L, M = a.shape
    N = g.shape[1]
    last = L // tl - 1

    def body(a_ref, g_ref, o_ref, *rest):
        @pl.when(pl.program_id(1) == 0)
        def _():
            o_ref[...] = jnp.zeros_like(o_ref)

        o_ref[...] += _mm_tn(a_ref[...], g_ref[...])
        if emit_bf16:
            @pl.when(pl.program_id(1) == last)
            def _():
                rest[0][...] = o_ref[...].astype(BF16)

    spec = pl.BlockSpec((M, tn), lambda j, l: (0, j))
    res = pl.pallas_call(
        body, name=name, grid=(N // tn, L // tl),
        in_specs=[pl.BlockSpec((tl, M), lambda j, l: (l, 0)), pl.BlockSpec((tl, tn), lambda j, l: (l, j))],
        out_specs=[spec, spec] if emit_bf16 else [spec],
        out_shape=[jax.ShapeDtypeStruct((M, N), F32)] + ([jax.ShapeDtypeStruct((M, N), BF16)] if emit_bf16 else []),
        compiler_params=_params(("arbitrary", "arbitrary")),
    )(a, g)
    return res if emit_bf16 else res[0]


def _position():
    return lax.axis_index("x"), lax.axis_index("y"), lax.axis_index("c")


def _index(px, py, pc):
    return 4 * px + 2 * py + pc


def _flip(pos, k):
    x, y, c = pos
    return ((1 - x) if (k >> 2) & 1 else x, (1 - y) if (k >> 1) & 1 else y, (1 - c) if k & 1 else c)


def _when(cond, fn):
    if cond is True:
        fn()
    else:
        pl.when(cond)(fn)


def _remote(src, dst, send_sem, recv_sem, peer):
    return pltpu.make_async_remote_copy(src_ref=src, dst_ref=dst, send_sem=send_sem, recv_sem=recv_sem,
                                        device_id=peer, device_id_type=MESH)


class _Flow:
    def __init__(self, kind, operand, result, target_x=None, target_c=None):
        self.kind, self.operand, self.result, self.target_x, self.target_c = kind, operand, result, target_x, target_c

    def owns(self, pos):
        if self.target_x is None:
            return True
        cond = pos[0] == self.target_x
        return cond if self.target_c is None else cond & (pos[2] == self.target_c)


class _Hosted:
    def __init__(self, operands, out_shapes, flows, aliases=None):
        self.operands, self.out_shapes, self.flows = operands, out_shapes, flows
        self.aliases = aliases or {}

    def plan(self, ins, outs, send_sems, recv_sems, local_sems):
        me = _position()
        mi = _index(*me)
        sends, recvs, locals_ = [], [], []
        for row, f in enumerate(self.flows):
            src, dst = ins[f.operand], outs[f.result]
            for k in range(1, N_DEV):
                peer = _flip(me, k)
                sems = (send_sems.at[row, k - 1], recv_sems.at[row, k - 1])
                if f.kind == "exchange":
                    owner = _index(*peer) if f.target_x is None else 2 * peer[1] + peer[2]
                    cp = _remote(src.at[owner], dst.at[k - 1], *sems, peer)
                    sends.append((f.owns(peer), cp))
                    recvs.append((f.owns(me), cp))
                elif f.kind == "chip_exchange":
                    if k & 1:
                        continue
                    cp = _remote(src.at[peer[1]], dst.at[k // 2 - 1], *sems, peer)
                    sends.append((peer[0] == f.target_x, cp))
                    recvs.append((me[0] == f.target_x, cp))
                else:
                    sends.append((True, _remote(src, dst.at[mi], *sems, peer)))
                    recvs.append((True, _remote(src, dst.at[_index(*peer)], *sems, peer)))
            if f.kind == "gather":
                locals_.append(pltpu.make_async_copy(src, dst.at[mi], local_sems.at[row]))

        def start():
            for cp in locals_:
                cp.start()
            for cond, cp in sends:
                _when(cond, cp.start)

        def wait():
            for cond, cp in recvs:
                _when(cond, cp.wait_recv)
            for cond, cp in sends:
                _when(cond, cp.wait_send)
            for cp in locals_:
                cp.wait()

        return start, wait


def _call(body, comm, *, name, grid, in_specs, out_specs, out_shape, scratch_shapes, args, aliases=None):
    io_alias = dict(aliases or {})
    semantics = ("arbitrary",) * len(grid)
    if comm is None:
        return pl.pallas_call(body, name=name, grid=grid, in_specs=in_specs, out_specs=out_specs, out_shape=out_shape,
                              scratch_shapes=scratch_shapes, input_output_aliases=io_alias,
                              compiler_params=_params(semantics))(*args)
    n_in, n_out, n_scr = len(args), len(out_shape), len(scratch_shapes)
    c_in, c_out, rows = len(comm.operands), len(comm.out_shapes), len(comm.flows)

    def hosted(*refs):
        ins, refs = refs[:n_in], refs[n_in:]
        cins, refs = refs[:c_in], refs[c_in:]
        outs, refs = refs[:n_out], refs[n_out:]
        couts, refs = refs[:c_out], refs[c_out:]
        scr, (send_sems, recv_sems, local_sems) = refs[:n_scr], refs[n_scr:]
        start, wait = comm.plan(cins, couts, send_sems, recv_sems, local_sems)
        ids = [pl.program_id(d) for d in range(len(grid))]
        first, last = ids[0] == 0, ids[0] == grid[0] - 1
        for d in range(1, len(grid)):
            first, last = first & (ids[d] == 0), last & (ids[d] == grid[d] - 1)
        pl.when(first)(start)
        body(*ins, *outs, *scr)
        pl.when(last)(wait)

    for ci, co in comm.aliases.items():
        io_alias[n_in + ci] = n_out + co
    any_spec = pl.BlockSpec(memory_space=pl.ANY)
    sems = [pltpu.SemaphoreType.DMA((rows, N_DEV - 1)), pltpu.SemaphoreType.DMA((rows, N_DEV - 1)),
            pltpu.SemaphoreType.DMA((rows,))]
    return pl.pallas_call(
        hosted, name=name, grid=grid, in_specs=list(in_specs) + [any_spec] * c_in,
        out_specs=list(out_specs) + [any_spec] * c_out, out_shape=list(out_shape) + list(comm.out_shapes),
        scratch_shapes=list(scratch_shapes) + sems, input_output_aliases=io_alias,
        compiler_params=_params(semantics))(*args, *comm.operands)


def _head_row(ref, width, rep):
    hid = lax.broadcasted_iota(jnp.int32, (1, width), 1) // rep
    row = jnp.zeros((1, width), F32)
    for h in range(N_HEADS):
        row = jnp.where(hid == h, ref[h], row)
    return row


def _rows_from_above(u_b, s, ext_scr, row, col):
    down = (row - col == s).astype(_MXU)
    return jnp.concatenate([ext_scr[8 - s:16 - s, :], jnp.dot(down, u_b, preferred_element_type=F32)[8:128]], axis=0)


def _ssd_recompute(first, p_ref, halo_ref, cw_ref, cb_ref, dtb_ref, alog_ref, e_ref, ext_scr, pre=None):
    row = lax.broadcasted_iota(jnp.int32, (128, 128), 0)
    col = lax.broadcasted_iota(jnp.int32, (128, 128), 1)
    ext_scr[0:8, :] = jnp.where(first, 0.0, halo_ref[:, S_XS:S_DT])
    if pre is not None:
        ext_scr[8:16, :] = p_ref[0:8, S_XS:S_DT]
    else:
        ext_scr[8:136, :] = p_ref[:, S_XS:S_DT]
        cw = cw_ref[...]
        pre = (cb_ref[0:1, :] + cw[3:4, :] * ext_scr[8:136, :] + cw[2:3, :] * ext_scr[7:135, :]
               + cw[1:2, :] * ext_scr[6:134, :] + cw[0:1, :] * ext_scr[5:133, :])
    sg = _sigmoid(pre)
    act = pre * sg
    lane = lax.broadcasted_iota(jnp.int32, (1, 128), 1)
    A = jnp.where(lane < N_HEADS, -jnp.exp(_head_row(alog_ref, 128, 1)), 0.0)
    raw = p_ref[:, S_DT:S_DT + 128] + _head_row(dtb_ref, 128, 1)
    dt = _softplus(raw)
    dA = dt * A
    tril = (row >= col).astype(BF16)
    acs = _mm_exact_l(tril, dA)
    last = acs[127:128, :]
    ds = jnp.exp(last - acs)
    eo = jnp.exp(acs)
    E = e_ref[...]
    ex = _mm_2pass_r(jnp.concatenate([dt, ds, eo], axis=0), E)
    dt_e, ds_e, eo_e = ex[0:128], ex[128:256], ex[256:384]
    xs_c = act[:, 0:1024]
    X = xs_c * dt_e
    return dict(pre=pre, sg=sg, xs_c=xs_c, Bc=act[:, 1024:1280], Cc=act[:, 1280:1536], A=A, raw=raw, dt=dt,
                acs=acs, acsT=acs.T, eo_e=eo_e, ds_e=ds_e, dt_e=dt_e, cd_e=eo_e[127:128, :],
                X=X, Xd=X * ds_e, row=row, col=col)


def _split_halves(t):
    lo = _lo_half(CHUNK)
    return jnp.concatenate([jnp.where(lo, t, 0.0), jnp.where(lo, 0.0, t)], axis=0)


def _ssd_core(R, hprev):
    causal = R["row"] >= R["col"]
    acs, acsT, X = R["acs"], R["acsT"], R["X"]
    ydiag, yoff, snew = [], [], []
    for g in range(SSD_GROUPS):
        Bg = R["Bc"][:, g * 128:(g + 1) * 128]
        Cg = R["Cc"][:, g * 128:(g + 1) * 128]
        cols = slice(g * 512, (g + 1) * 512)
        CB = _mm_nt(Cg, Bg)
        snew.append(_mm_tn(Bg, R["Xd"][:, cols]))
        yoff.append(_mm(Cg, hprev[:, cols]))
        for j in range(4):
            h0 = g * 8 + 2 * j
            ms = [CB * jnp.exp(jnp.where(causal, acs[:, h:h + 1] - acsT[h:h + 1, :], NEG)) for h in (h0, h0 + 1)]
            ydiag.append(_mm(jnp.concatenate(ms, axis=1), _split_halves(X[:, h0 * HEAD_DIM:h0 * HEAD_DIM + 128])))
    Y = jnp.concatenate(ydiag, axis=1) + jnp.concatenate(yoff, axis=1) * R["eo_e"]
    return Y, jnp.concatenate(snew, axis=1)


def _ssd_forward_step(p_ref, halo_ref, cw_ref, cb_ref, dtb_ref, alog_ref, dsk_ref, nw_ref, e_ref,
                      y_ref, ypre_ref, hprev_ref, pre_ref, h_scr, ext_scr):
    c = pl.program_id(0)
    first = c == 0

    @pl.when(first)
    def _():
        h_scr[...] = jnp.zeros_like(h_scr)

    R = _ssd_recompute(first, p_ref, halo_ref, cw_ref, cb_ref, dtb_ref, alog_ref, e_ref, ext_scr)
    hprev = h_scr[...]
    hprev_ref[...] = hprev
    pre_ref[...] = R["pre"]
    Y, snew = _ssd_core(R, hprev)
    h_scr[...] = hprev * R["cd_e"] + snew
    Y = Y + _head_row(dsk_ref, D_SSD, HEAD_DIM) * R["xs_c"]
    ypre_ref[...] = Y
    z = p_ref[:, S_Z:S_Z + 1024]
    yf = Y * (z * _sigmoid(z))
    outs = []
    for g in range(SSD_GROUPS):
        yg = yf[:, g * 512:(g + 1) * 512]
        r = lax.rsqrt(jnp.mean(yg * yg, axis=-1, keepdims=True) + RMS_EPS)
        outs.append(yg * r)
    y_ref[:, 0:D_SSD] = (jnp.concatenate(outs, axis=1) * nw_ref[0:1, :]).astype(y_ref.dtype)


def _ssd_backward(proj_ssd, hprev_all, ypre, pre, dy, conv_w8, conv_b8, dtb8, alog8, dskip_e, norm_w, E, ET, comm=None):
    L = proj_ssd.shape[0]
    nc = L // CHUNK

    def body(p_ref, halo_ref, hprev_ref, ypre_ref, pre_ref, dy_ref, cw_ref, cb_ref, dtb_ref, alog_ref, dsk_ref, nw_ref, e_ref,
             et_ref, dp_ref, acc_cw_ref, acc_w_ref, acc_s_ref, dh_scr, ext_scr, ext2_scr, nxt_scr):
        i = pl.program_id(0)
        c = nc - 1 - i
        first = c == 0

        @pl.when(i == 0)
        def _():
            dh_scr[...] = jnp.zeros_like(dh_scr)
            nxt_scr[...] = jnp.zeros_like(nxt_scr)
            acc_cw_ref[...] = jnp.zeros_like(acc_cw_ref)
            acc_w_ref[...] = jnp.zeros_like(acc_w_ref)
            acc_s_ref[...] = jnp.zeros_like(acc_s_ref)

        R = _ssd_recompute(first, p_ref, halo_ref, cw_ref, cb_ref, dtb_ref, alog_ref, e_ref, ext_scr, pre_ref[...])
        hprev = hprev_ref[...]
        xs_c, X, Xd = R["xs_c"], R["X"], R["Xd"]
        acs, acsT = R["acs"], R["acsT"]
        ET = et_ref[...]
        dsk = _head_row(dsk_ref, D_SSD, HEAD_DIM)
        Y = ypre_ref[...]

        z = p_ref[:, S_Z:S_Z + 1024]
        sz = _sigmoid(z)
        silz = z * sz
        yf = Y * silz
        dyv = dy_ref[...]
        nw = nw_ref[0:1, :]
        dyf_parts, dnw_parts = [], []
        for g in range(SSD_GROUPS):
            cols = slice(g * 512, (g + 1) * 512)
            yg = yf[:, cols]
            r = lax.rsqrt(jnp.mean(yg * yg, axis=-1, keepdims=True) + RMS_EPS)
            yn = yg * r
            dyn = dyv[:, cols] * nw[:, cols]
            dnw_parts.append(_colsum(dyv[:, cols] * yn))
            dyf_parts.append(r * (dyn - yn * jnp.mean(dyn * yn, axis=-1, keepdims=True)))
        dyf = jnp.concatenate(dyf_parts, axis=1)
        dY = dyf * silz
        dz = dyf * Y * (sz * (1.0 + z * (1.0 - sz)))

        dhn = dh_scr[...]
        dYo = dY * R["eo_e"]
        causal = R["row"] >= R["col"]
        dacs = jnp.zeros((128, 128), F32)
        dacs_t = jnp.zeros((128, 128), F32)
        dxdiag, dxd, dhprev, dBs, dCs, yoff = [], [], [], [], [], []
        for g in range(SSD_GROUPS):
            Bg = R["Bc"][:, g * 128:(g + 1) * 128]
            Cg = R["Cc"][:, g * 128:(g + 1) * 128]
            cols = slice(g * 512, (g + 1) * 512)
            CB = _mm_nt(Cg, Bg)
            dCB = jnp.zeros((128, 128), F32)
            for j in range(4):
                h0 = g * 8 + 2 * j
                pc = slice(h0 * HEAD_DIM, h0 * HEAD_DIM + 128)
                dYst = _split_halves(dY[:, pc])
                dMst = _mm_nt(dYst, X[:, pc])
                mts = []
                for a, h in enumerate((h0, h0 + 1)):
                    acol = acs[:, h:h + 1]
                    arow = acsT[h:h + 1, :]
                    Lm = jnp.exp(jnp.where(causal, acol - arow, NEG))
                    M = CB * Lm
                    dM = dMst[a * 128:(a + 1) * 128]
                    dCB = dCB + dM * Lm
                    G = dM * M
                    dacs = dacs + jnp.where(R["col"] == h, jnp.sum(G, axis=1, keepdims=True), 0.0)
                    dacs_t = dacs_t + jnp.where(R["row"] == h, jnp.sum(G, axis=0, keepdims=True), 0.0)
                    mts.append(M.T)
                dxdiag.append(_mm(jnp.concatenate(mts, axis=1), dYst))
            dS = dhn[:, cols]
            dxd.append(_mm(Bg, dS))
            yoff.append(_mm(Cg, hprev[:, cols]))
            dhprev.append(_mm_tn(Cg, dYo[:, cols]))
            dCs.append(_mm_nt(dYo[:, cols], hprev[:, cols]) + _mm(dCB, Bg))
            dBs.append(_mm_tn(dCB, Cg) + _mm_nt(Xd[:, cols], dS))
        Yoff = jnp.concatenate(yoff, axis=1) * R["eo_e"]
        dXd = jnp.concatenate(dxd, axis=1)
        dX = jnp.concatenate(dxdiag, axis=1) + dXd * R["ds_e"]
        t_state = dXd * Xd
        dacs = dacs + _mm_2pass_r(dY * Yoff - t_state, ET) - dacs_t.T
        v_last = _colsum(t_state + dhn * hprev * R["cd_e"])
        dlast = _mm_exact_r(jnp.broadcast_to(v_last, (8, 1024)), ET)[0:1, :]
        dacs = dacs + jnp.where(R["row"] == 127, dlast, 0.0)
        triu = (R["col"] >= R["row"]).astype(BF16)
        da = _mm_exact_l(triu, dacs)
        ddt = da * R["A"] + _mm(dX * xs_c, ET)
        ddt_raw = ddt * _sigmoid(R["raw"])
        dxs_c = dX * R["dt_e"] + dY * dsk
        dh_scr[...] = jnp.concatenate(dhprev, axis=1) + dhn * R["cd_e"]

        dact = jnp.concatenate([dxs_c] + dBs + dCs, axis=1)
        pre, sg = R["pre"], R["sg"]
        dpre = dact * (sg * (1.0 + pre * (1.0 - sg)))
        ext2_scr[0:8, :] = dpre[120:128, :]
        ext2_scr[8:16, :] = nxt_scr[...]
        nxt_scr[...] = dpre[0:8, :]
        cw = cw_ref[...]
        u_b, dpre_b = p_ref[:, S_XS:S_DT].astype(_MXU), dpre.astype(_MXU)
        dxbc = cw[3:4, :] * dpre
        taps = [_colsum(dpre * p_ref[:, S_XS:S_DT])]
        for s in (1, 2, 3):
            up = (R["col"] - R["row"] == s).astype(_MXU)
            d_s = jnp.concatenate([jnp.dot(up, dpre_b, preferred_element_type=F32)[0:120],
                                   ext2_scr[s:8 + s, :]], axis=0)
            dxbc = dxbc + cw[3 - s:4 - s, :] * d_s
            taps.append(_colsum(dpre * _rows_from_above(u_b, s, ext_scr, R["row"], R["col"])))
        acc_cw_ref[...] += _rows8(taps[::-1] + [_colsum(dpre)])
        acc_w_ref[...] += _rows8([jnp.concatenate(dnw_parts, axis=1), _colsum(dY * xs_c)])
        acc_s_ref[...] += _rows8([_colsum(ddt_raw), _colsum(da * R["dt"])])

        lane = lax.broadcasted_iota(jnp.int32, (128, 128), 1)
        dp_ref[:, S_Z:S_Z + 1024] = dz.astype(dp_ref.dtype)
        dp_ref[:, S_XS:S_DT] = dxbc.astype(dp_ref.dtype)
        dp_ref[:, S_DT:S_DT + 128] = jnp.where(lane < N_HEADS, ddt_raw, 0.0).astype(dp_ref.dtype)
        dp_ref[:, S_DT + 128:S_W] = jnp.zeros((128, 128), dp_ref.dtype)

        @pl.when(i == nc - 1)
        def _():
            acc = acc_s_ref[...]
            dskip = _mm_exact_r(acc_w_ref[...], ET)[1:2, :]
            acc_s_ref[...] = _rows8([acc[0:1, :], acc[1:2, :] * R["A"], dskip])

    const = lambda shape: pl.BlockSpec(shape, lambda i: (0, 0))
    smem = pl.BlockSpec(memory_space=pltpu.SMEM)
    rev = lambda i: (nc - 1 - i, 0)
    return _call(
        body, comm, name="ssd_bwd", grid=(nc,),
        in_specs=[pl.BlockSpec((CHUNK, S_W), rev),
                  pl.BlockSpec((8, S_W), lambda i: (jnp.maximum((nc - 1 - i) * 16 - 1, 0), 0)),
                  pl.BlockSpec((128, 1024), rev),
                  pl.BlockSpec((CHUNK, D_SSD), rev),
                  pl.BlockSpec((CHUNK, D_XBC), rev),
                  pl.BlockSpec((CHUNK, D_SSD), rev),
                  const((4, D_XBC)), const((1, D_XBC)), smem, smem, smem, const((1, 1024)),
                  const((128, 1024)), const((1024, 128))],
        out_specs=[pl.BlockSpec((CHUNK, S_W), rev), const((8, D_XBC)), const((8, 1024)), const((8, 128))],
        out_shape=[jax.ShapeDtypeStruct((L, S_W), _MXU), jax.ShapeDtypeStruct((8, D_XBC), F32),
                   jax.ShapeDtypeStruct((8, 1024), F32), jax.ShapeDtypeStruct((8, 128), F32)],
        scratch_shapes=[pltpu.VMEM((128, 1024), F32), pltpu.VMEM((16, D_XBC), F32),
                        pltpu.VMEM((16, D_XBC), F32), pltpu.VMEM((8, D_XBC), F32)],
        args=(proj_ssd, proj_ssd, hprev_all, ypre, pre, dy, conv_w8, conv_b8, dtb8, alog8, dskip_e, norm_w, E, ET))


def _rope(t, tab):
    cos, sa, sb = tab[:, 0:128], tab[:, 128:256], tab[:, 256:384]
    outs = []
    for i in range(t.shape[1] // 128):
        tg = t[:, i * 128:(i + 1) * 128]
        outs.append(tg * cos + pltpu.roll(tg, 8, 1) * sa + pltpu.roll(tg, 120, 1) * sb)
    return jnp.concatenate(outs, axis=1)


def _rope_transposed(d, tab):
    cos, sa, sb = tab[:, 0:128], tab[:, 128:256], tab[:, 256:384]
    outs = []
    for i in range(d.shape[1] // 128):
        dg = d[:, i * 128:(i + 1) * 128]
        outs.append(dg * cos + pltpu.roll(dg * sa, 120, 1) + pltpu.roll(dg * sb, 8, 1))
    return jnp.concatenate(outs, axis=1)


def _lo_half(rows):
    return lax.broadcasted_iota(jnp.int32, (rows, 128), 1) < HEAD_DIM


def _kv_both(t, j):
    p, b = j // 2, j % 2
    lo = _lo_half(t.shape[0])
    nat = jnp.where(lo if b == 0 else jnp.logical_not(lo), t[:, p * 128:(p + 1) * 128], 0.0)
    return nat + pltpu.roll(nat, HEAD_DIM, 1)


def _stack_heads(t, j):
    lo = _lo_half(CHUNK)
    hi = jnp.logical_not(lo)
    a, b = t[:, 2 * j * 128:(2 * j + 1) * 128], t[:, (2 * j + 1) * 128:(2 * j + 2) * 128]
    return jnp.concatenate([jnp.where(lo, a, 0.0), jnp.where(hi, a, 0.0),
                            jnp.where(lo, b, 0.0), jnp.where(hi, b, 0.0)], axis=0)


def _unstack_heads(s):
    lo = _lo_half(CHUNK)
    return jnp.concatenate([jnp.where(lo, s[0:128], s[128:256]), jnp.where(lo, s[256:384], s[384:512])], axis=1)


def _fold_kv(r, j):
    lo = _lo_half(r.shape[0])
    return jnp.where(lo if j % 2 == 0 else jnp.logical_not(lo), r + pltpu.roll(r, HEAD_DIM, 1), 0.0)


def _sink_row(sink_ref, j):
    hid = lax.broadcasted_iota(jnp.int32, (1, 4 * CHUNK), 1) // CHUNK
    row = jnp.zeros((1, 4 * CHUNK), F32)
    for hh in range(4):
        row = jnp.where(hid == hh, sink_ref[4 * j + hh], row)
    return row


def _from_current(n_rows=CHUNK):
    si = lax.broadcasted_iota(jnp.int32, (n_rows, 4 * CHUNK), 0)
    qi = lax.broadcasted_iota(jnp.int32, (n_rows, 4 * CHUNK), 1) % CHUNK
    return si <= qi


def _fold(full, from_cur, pen=0.0):
    return jnp.where(from_cur, full[CHUNK:2 * CHUNK], full[0:CHUNK] + pen)


def _unfold(t, from_cur):
    c = jnp.where(from_cur, t, 0.0)
    return jnp.concatenate([t - c, c], axis=0)


def _softmax_sink(s, sink):
    mx = jnp.maximum(jnp.max(s, axis=0, keepdims=True), sink)
    p = jnp.exp(s - mx)
    esink = jnp.exp(sink - mx)
    inv = 1.0 / (jnp.sum(p, axis=0, keepdims=True) + esink)
    return p * inv, esink * inv


def _swa_inputs(blk, p_ref, prev_ref, tab_ref, ptab_ref):
    tab = tab_ref[...]
    qr = _rope(p_ref[:, A_Q:A_Q + 1024], tab) * ATT_SCALE
    kk = jnp.concatenate([_rope(prev_ref[:, 0:256], ptab_ref[...]), _rope(p_ref[:, A_K:A_K + 256], tab)], axis=0)
    vv = jnp.concatenate([prev_ref[:, 256:512], p_ref[:, A_V:A_V + 256]], axis=0)
    return tab, qr, kk, vv, jnp.where(blk > 0, 0.0, NEG)


def _swa_forward_step(sink_ref, p_ref, prev_ref, tab_ref, ptab_ref, y_ref):
    n = pl.program_id(0)
    _, qr, kk, vv, pen = _swa_inputs(n, p_ref, prev_ref, tab_ref, ptab_ref)
    from_cur = _from_current()
    outs = []
    for j in range(KV_HEADS):
        s = _fold(_mm_nt(_kv_both(kk, j), _stack_heads(qr, j)), from_cur, pen)
        P, _ = _softmax_sink(s, _sink_row(sink_ref, j))
        outs.append(_unstack_heads(_mm_tn(_unfold(P, from_cur), _kv_both(vv, j))))
    g = p_ref[:, A_G:A_G + 1024]
    y_ref[:, D_SSD:D_SSD + D_ATT] = (jnp.concatenate(outs, axis=1) * (g * _sigmoid(g))).astype(y_ref.dtype)


def _mixer_forward(proj_ssd, proj_att, tabs, sinks, conv_w, conv_b, dt_bias, a_log, d_skip, norm_w, E, comm=None):
    L = proj_ssd.shape[0]
    nc = L // CHUNK

    def body(p_ref, halo_ref, cw_ref, cb_ref, dtb_ref, alog_ref, dsk_ref, nw_ref, e_ref,
             sink_ref, pa_ref, prev_ref, tab_ref, ptab_ref, y_ref, ypre_ref, hprev_ref, pre_ref, h_scr, ext_scr):
        _ssd_forward_step(p_ref, halo_ref, cw_ref, cb_ref, dtb_ref, alog_ref, dsk_ref, nw_ref, e_ref,
                          y_ref, ypre_ref, hprev_ref, pre_ref, h_scr, ext_scr)
        _swa_forward_step(sink_ref, pa_ref, prev_ref, tab_ref, ptab_ref, y_ref)

    const = lambda shape: pl.BlockSpec(shape, lambda c: (0, 0))
    smem = pl.BlockSpec(memory_space=pltpu.SMEM)
    rows = lambda w: pl.BlockSpec((CHUNK, w), lambda c: (c, 0))
    return _call(
        body, comm, name="mixer_fwd", grid=(nc,),
        in_specs=[rows(S_W), pl.BlockSpec((8, S_W), lambda c: (jnp.maximum(c * 16 - 1, 0), 0)),
                  const((4, D_XBC)), const((1, D_XBC)), smem, smem, smem, const((1, 1024)), const((128, 1024)),
                  smem, rows(A_W), pl.BlockSpec((CHUNK, 512), lambda c: (jnp.maximum(c - 1, 0), 2)),
                  rows(384), pl.BlockSpec((CHUNK, 384), lambda c: (jnp.maximum(c - 1, 0), 0))],
        out_specs=[rows(D_SSD + D_ATT), rows(D_SSD), pl.BlockSpec((128, 1024), lambda c: (c, 0)), rows(D_XBC)],
        out_shape=[jax.ShapeDtypeStruct((L, D_SSD + D_ATT), _MXU), jax.ShapeDtypeStruct((L, D_SSD), F32),
                   jax.ShapeDtypeStruct((nc * 128, 1024), F32), jax.ShapeDtypeStruct((L, D_XBC), F32)],
        scratch_shapes=[pltpu.VMEM((128, 1024), F32), pltpu.VMEM((136, D_XBC), F32)],
        args=(proj_ssd, proj_ssd, conv_w, conv_b, dt_bias, a_log, d_skip, norm_w, E,
              sinks, proj_att, proj_att, tabs, tabs))


def _swa_backward(proj_att, tabs, sinks, dy, comm=None):
    L = proj_att.shape[0]
    nb = L // CHUNK

    def body(sink_ref, p_ref, prev_ref, tab_ref, ptab_ref, dy_ref, dp_ref, dsink_ref, carry_k, carry_v):
        i = pl.program_id(0)
        n = nb - 1 - i

        @pl.when(i == 0)
        def _():
            carry_k[...] = jnp.zeros_like(carry_k)
            carry_v[...] = jnp.zeros_like(carry_v)
            dsink_ref[...] = jnp.zeros_like(dsink_ref)

        tab, qr, kk, vv, pen = _swa_inputs(n, p_ref, prev_ref, tab_ref, ptab_ref)
        from_cur = _from_current()
        g = p_ref[:, A_G:A_G + 1024]
        sgm = _sigmoid(g)
        dyv = dy_ref[...]
        do_all = dyv * (g * sgm)
        lane8 = lax.broadcasted_iota(jnp.int32, (8, 128), 1)
        hid = lax.broadcasted_iota(jnp.int32, (1, 4 * CHUNK), 1) // CHUNK
        o_parts, dq_parts = [], []
        dk_nat = [jnp.zeros((2 * CHUNK, 128), F32) for _ in range(2)]
        dv_nat = [jnp.zeros((2 * CHUNK, 128), F32) for _ in range(2)]
        dsink = jnp.zeros((8, 128), F32)
        for j in range(KV_HEADS):
            qs = _stack_heads(qr, j)
            kkb, vvb = _kv_both(kk, j), _kv_both(vv, j)
            P, psink = _softmax_sink(_fold(_mm_nt(kkb, qs), from_cur, pen), _sink_row(sink_ref, j))
            p_full = _unfold(P, from_cur)
            o_parts.append(_unstack_heads(_mm_tn(p_full, vvb)))
            do_s = _stack_heads(do_all, j)
            dP = _fold(_mm_nt(vvb, do_s), from_cur)
            D = jnp.sum(P * dP, axis=0, keepdims=True)
            ds_full = _unfold(P * (dP - D), from_cur)
            sd = psink * D
            for hh in range(4):
                dsink = dsink + jnp.where(lane8 == 4 * j + hh, -jnp.sum(jnp.where(hid == hh, sd, 0.0)), 0.0)
            dq_parts.append(_unstack_heads(_mm_tn(ds_full, kkb)) * ATT_SCALE)
            dk_nat[j // 2] = dk_nat[j // 2] + _fold_kv(_mm(ds_full, qs), j)
            dv_nat[j // 2] = dv_nat[j // 2] + _fold_kv(_mm(p_full, do_s), j)
        o = jnp.concatenate(o_parts, axis=1)
        dkk = jnp.concatenate(dk_nat, axis=1)
        dvv = jnp.concatenate(dv_nat, axis=1)
        out = dp_ref.dtype
        dp_ref[:, A_Q:A_Q + 1024] = _rope_transposed(jnp.concatenate(dq_parts, axis=1), tab).astype(out)
        dp_ref[:, A_K:A_K + 256] = _rope_transposed(dkk[CHUNK:2 * CHUNK] + carry_k[...], tab).astype(out)
        dp_ref[:, A_V:A_V + 256] = (dvv[CHUNK:2 * CHUNK] + carry_v[...]).astype(out)
        dp_ref[:, A_G:A_G + 1024] = (dyv * o * (sgm * (1.0 + g * (1.0 - sgm)))).astype(out)
        carry_k[...] = dkk[0:CHUNK]
        carry_v[...] = dvv[0:CHUNK]
        dsink_ref[...] += dsink

    rev = lambda i: (nb - 1 - i, 0)
    prev = lambda i: jnp.maximum(nb - 2 - i, 0)
    return _call(
        body, comm, name="swa_bwd", grid=(nb,),
        in_specs=[pl.BlockSpec(memory_space=pltpu.SMEM),
                  pl.BlockSpec((CHUNK, A_W), rev),
                  pl.BlockSpec((CHUNK, 512), lambda i: (prev(i), 2)),
                  pl.BlockSpec((CHUNK, 384), rev),
                  pl.BlockSpec((CHUNK, 384), lambda i: (prev(i), 0)),
                  pl.BlockSpec((CHUNK, D_ATT), lambda i: (nb - 1 - i, 1))],
        out_specs=[pl.BlockSpec((CHUNK, A_W), rev), pl.BlockSpec((8, 128), lambda i: (0, 0))],
        out_shape=[jax.ShapeDtypeStruct((L, A_W), _MXU), jax.ShapeDtypeStruct((8, 128), F32)],
        scratch_shapes=[pltpu.VMEM((CHUNK, 256), F32), pltpu.VMEM((CHUNK, 256), F32)],
        args=(sinks, proj_att, proj_att, tabs, tabs, dy))


def _head(y, x, target, w_out, ln_g8, ln_b8, *, tm):
    L = x.shape[0]
    nsteps = L // tm

    def body(y_ref, x_ref, t_ref, wo_ref, g_ref, b_ref, dr_ref, dy_ref, acc_ref):
        i = pl.program_id(0)

        @pl.when(i == 0)
        def _():
            acc_ref[...] = jnp.zeros_like(acc_ref)

        r = ALPHA * x_ref[...] + _mm(y_ref[...], wo_ref[...])
        mu = jnp.mean(r, axis=-1, keepdims=True)
        d = r - mu
        rstd = lax.rsqrt(jnp.mean(d * d, axis=-1, keepdims=True) + LN_EPS)
        xh = d * rstd
        gam = g_ref[0:1, :]
        e = xh * gam + b_ref[0:1, :] - t_ref[...]
        dout = e * (1.0 / D_MODEL)
        dxh = dout * gam
        dr = rstd * (dxh - jnp.mean(dxh, axis=-1, keepdims=True)
                     - xh * jnp.mean(dxh * xh, axis=-1, keepdims=True))
        dr_ref[...] = dr
        dy_ref[...] = _mm_nt(dr, wo_ref[...])
        acc_ref[...] += _rows8([_colsum(dout * xh), _colsum(dout), _colsum(e * e) * (0.5 / D_MODEL)])

        @pl.when(i == nsteps - 1)
        def _():
            acc = acc_ref[...]
            tot = jnp.sum(acc[2:3, :])
            rid = lax.broadcasted_iota(jnp.int32, (8, 1024), 0)
            acc_ref[...] = jnp.where(rid == 3, tot, acc)

    const = lambda shape: pl.BlockSpec(shape, lambda i: (0, 0))
    row = lambda w: pl.BlockSpec((tm, w), lambda i: (i, 0))
    return pl.pallas_call(
        body, name="head", grid=(nsteps,),
        in_specs=[row(2048), row(1024), row(1024), const((2048, 1024)), const((1, 1024)), const((1, 1024))],
        out_specs=[row(1024), row(2048), const((8, 1024))],
        out_shape=[jax.ShapeDtypeStruct((L, D_MODEL), F32), jax.ShapeDtypeStruct((L, 2048), F32),
                   jax.ShapeDtypeStruct((8, 1024), F32)],
        compiler_params=_params(("arbitrary",)),
    )(y, x, target, w_out, ln_g8, ln_b8)


def _gather_w_in(w_shard):
    R = w_shard.shape[0]
    halves = (pl.ds(0, R // 2), pl.ds(R // 2, R // 2))
    any_spec = pl.BlockSpec(memory_space=pl.ANY)

    def body(in_ref, out_ref, send_sems, recv_sems, local_sem):
        x, y, c = _position()

        def slot(p, half=None):
            s = out_ref.at[_index(*p)]
            return s if half is None else s.at[halves[half]]

        def same_core(p):
            return (p[0], p[1], c)

        def other_core(p):
            return (p[0], p[1], 1 - c)

        me, xn, yn, dg = (x, y), (1 - x, y), (x, 1 - y), (1 - x, 1 - y)

        def copy(k, dst, to, src=None):
            return _remote(dst if src is None else src, dst, send_sems.at[k], recv_sems.at[k], to)

        local = pltpu.make_async_copy(in_ref, slot(same_core(me)), local_sem)
        local.start()
        own = [copy(0, slot(same_core(me)), other_core(me), in_ref), copy(1, slot(same_core(me)), same_core(xn), in_ref),
               copy(2, slot(same_core(me)), same_core(yn), in_ref)]
        for cp in own:
            cp.start()
        copy(1, slot(same_core(xn)), same_core(xn)).wait_recv()
        passed = [copy(4, slot(same_core(xn), 1), same_core(yn)), copy(5, slot(same_core(xn)), other_core(me))]
        for cp in passed:
            cp.start()
        copy(2, slot(same_core(yn)), same_core(yn)).wait_recv()
        more = [copy(3, slot(same_core(yn), 0), same_core(xn)), copy(6, slot(same_core(yn)), other_core(me))]
        for cp in more:
            cp.start()
        passed += more
        for k, half in ((3, 0), (4, 1)):
            copy(k, slot(same_core(dg), half), same_core(xn)).wait_recv()
            fwd = copy(7 + half, slot(same_core(dg), half), other_core(me))
            fwd.start()
            passed.append(fwd)
        copy(0, slot(other_core(me)), other_core(me)).wait_recv()
        copy(5, slot(other_core(xn)), other_core(me)).wait_recv()
        copy(6, slot(other_core(yn)), other_core(me)).wait_recv()
        for half in (0, 1):
            copy(7 + half, slot(other_core(dg), half), other_core(me)).wait_recv()
        for cp in own + passed:
            cp.wait_send()
        local.wait()

    return pl.pallas_call(
        body, name="gather_w_in", in_specs=[any_spec], out_specs=any_spec,
        out_shape=jax.ShapeDtypeStruct((N_DEV,) + w_shard.shape, w_shard.dtype),
        scratch_shapes=[pltpu.SemaphoreType.DMA((9,)), pltpu.SemaphoreType.DMA((9,)), pltpu.SemaphoreType.DMA],
    )(w_shard)


def _input_gradient(d_ssd, d_att, w_ssd, w_att, dr, *, tm, comm=None):
    L = dr.shape[0]

    def body(ds_ref, da_ref, ws_ref, wa_ref, dr_ref, o_ref):
        o_ref[...] = ALPHA * dr_ref[...] + _mm_nt(ds_ref[...], ws_ref[...]) + _mm_nt(da_ref[...], wa_ref[...])

    row = lambda w: pl.BlockSpec((tm, w), lambda i: (i, 0))
    const = lambda shape: pl.BlockSpec(shape, lambda i: (0, 0))
    return _call(body, comm, name="dx", grid=(L // tm,),
                 in_specs=[row(S_W), row(A_W), const((D_MODEL, S_W)), const((D_MODEL, A_W)), row(D_MODEL)],
                 out_specs=[row(D_MODEL)], out_shape=[jax.ShapeDtypeStruct((L, D_MODEL), F32)],
                 scratch_shapes=[], args=(d_ssd, d_att, w_ssd, w_att, dr))


SHARD_COLS = D_IN_PROJ // N_DEV
SPLIT = N_SSD_REAL - 4 * SHARD_COLS
RELAYOUT_ROWS = 256


def _unpack_w_in(w_all):
    def body(g_ref, ws_ref, wa_ref):
        for j in range(4):
            ws_ref[:, SHARD_COLS * j:SHARD_COLS * (j + 1)] = g_ref[j]
        ws_ref[:, 4 * SHARD_COLS:N_SSD_REAL] = g_ref[4, :, 0:SPLIT]
        ws_ref[:, N_SSD_REAL:S_W] = jnp.zeros((RELAYOUT_ROWS, S_W - N_SSD_REAL), ws_ref.dtype)
        wa_ref[:, 0:SHARD_COLS - SPLIT] = g_ref[4, :, SPLIT:SHARD_COLS]
        for j in range(5, N_DEV):
            lo = SHARD_COLS * (j - 4) - SPLIT
            wa_ref[:, lo:lo + SHARD_COLS] = g_ref[j]

    return pl.pallas_call(
        body, name="unpack_w_in", grid=(D_MODEL // RELAYOUT_ROWS,),
        in_specs=[pl.BlockSpec((N_DEV, RELAYOUT_ROWS, SHARD_COLS), lambda i: (0, i, 0))],
        out_specs=[pl.BlockSpec((RELAYOUT_ROWS, S_W), lambda i: (i, 0)), pl.BlockSpec((RELAYOUT_ROWS, A_W), lambda i: (i, 0))],
        out_shape=[jax.ShapeDtypeStruct((D_MODEL, S_W), w_all.dtype), jax.ShapeDtypeStruct((D_MODEL, A_W), w_all.dtype)],
        compiler_params=_params(("arbitrary",)),
    )(w_all)


def _pack_dw_in(me1, dw_ssd, dw_att, half):
    def body(me_ref, *refs):
        if half == 0:
            ds_ref, p_ref, own_ref = refs
            me = me_ref[0]

            @pl.when(me >= 4)
            def _():
                own_ref[...] = jnp.zeros_like(own_ref)
        else:
            ds_ref, da_ref, p_ref = refs

        for j in range(4):
            if half == 0:
                pieces = [(0, ds_ref[:, SHARD_COLS * j:SHARD_COLS * (j + 1)])]
            elif j == 0:
                pieces = [(0, ds_ref[:, 4 * SHARD_COLS - S_DT:N_SSD_REAL - S_DT]), (SPLIT, da_ref[:, 0:SHARD_COLS - SPLIT])]
            else:
                lo = SHARD_COLS * j - SPLIT
                pieces = [(0, da_ref[:, lo:lo + SHARD_COLS])]
            for off, blk in pieces:
                p_ref[j, :, off:off + blk.shape[1]] = blk.astype(p_ref.dtype)
                if half == 0:
                    @pl.when(me == j)
                    def _(off=off, blk=blk):
                        own_ref[:, off:off + blk.shape[1]] = blk

    ins = [dw_ssd] if half == 0 else [dw_ssd, dw_att]
    row = lambda a: pl.BlockSpec((RELAYOUT_ROWS, a.shape[1]), lambda i: (i, 0))
    in_specs = [row(a) for a in ins]
    if half == 1:
        in_specs[0] = pl.BlockSpec((RELAYOUT_ROWS, S_W - S_DT), lambda i: (i, S_DT // (S_W - S_DT)))
    out_specs = [pl.BlockSpec((4, RELAYOUT_ROWS, SHARD_COLS), lambda i: (0, i, 0))]
    out_shape = [jax.ShapeDtypeStruct((4, D_MODEL, SHARD_COLS), BF16 if half == 0 else F32)]
    if half == 0:
        out_specs.append(pl.BlockSpec((RELAYOUT_ROWS, SHARD_COLS), lambda i: (i, 0)))
        out_shape.append(jax.ShapeDtypeStruct((D_MODEL, SHARD_COLS), F32))
    return pl.pallas_call(
        body, name="pack_dw_in_%d" % half, grid=(D_MODEL // RELAYOUT_ROWS,),
        in_specs=[pl.BlockSpec(memory_space=pltpu.SMEM)] + in_specs,
        out_specs=out_specs, out_shape=out_shape, compiler_params=_params(("arbitrary",)),
    )(me1, *ins)


def _pair_swap(stack):
    def body(in_ref, out_ref, send_sems, recv_sems):
        x, y, c = _position()
        cps = [_remote(in_ref.at[2 * oy + (1 - c)], out_ref.at[oy], send_sems.at[oy], recv_sems.at[oy], (x, y, 1 - c))
               for oy in range(2)]
        for cp in cps:
            cp.start()
        for cp in cps:
            cp.wait_recv()
        for cp in cps:
            cp.wait_send()

    any_spec = pl.BlockSpec(memory_space=pl.ANY)
    return pl.pallas_call(
        body, name="pair_swap", in_specs=[any_spec], out_specs=any_spec,
        out_shape=jax.ShapeDtypeStruct((2,) + stack.shape[1:], stack.dtype),
        scratch_shapes=[pltpu.SemaphoreType.DMA((2,)), pltpu.SemaphoreType.DMA((2,))],
    )(stack)


def _pair_sum(pos3, stack, swapped, own_lo):
    def body(pos_ref, a_ref, b_ref, lo_ref, chip_ref, own_ref):
        oy = pl.program_id(1)
        t = a_ref[0] + b_ref[0]
        chip_ref[0] = t.astype(chip_ref.dtype)

        @pl.when((pos_ref[0] == 0) & (oy == 0))
        def _():
            own_ref[...] = lo_ref[...]

        @pl.when((pos_ref[0] == 1) & (oy == pos_ref[1]))
        def _():
            own_ref[...] = t

    blk = (1, RELAYOUT_ROWS, SHARD_COLS)
    flat = pl.BlockSpec((RELAYOUT_ROWS, SHARD_COLS), lambda i, oy, pos: (i, 0))
    return pl.pallas_call(
        body, name="pair_sum",
        grid_spec=pltpu.PrefetchScalarGridSpec(
            num_scalar_prefetch=1, grid=(D_MODEL // RELAYOUT_ROWS, 2),
            in_specs=[pl.BlockSpec(blk, lambda i, oy, pos: (2 * oy + pos[2], i, 0)),
                      pl.BlockSpec(blk, lambda i, oy, pos: (oy, i, 0)), flat],
            out_specs=[pl.BlockSpec(blk, lambda i, oy, pos: (oy, i, 0)), flat]),
        out_shape=[jax.ShapeDtypeStruct((2, D_MODEL, SHARD_COLS), BF16), jax.ShapeDtypeStruct((D_MODEL, SHARD_COLS), F32)],
        compiler_params=_params(("arbitrary", "arbitrary")),
    )(pos3, stack, swapped, own_lo)


def _adamw_math(w, g, m, v):
    m = ADAM_B1 * m + (1.0 - ADAM_B1) * g
    v = ADAM_B2 * v + (1.0 - ADAM_B2) * (g * g)
    m_hat = m / (1.0 - ADAM_B1 ** ADAM_STEP)
    v_hat = v / (1.0 - ADAM_B2 ** ADAM_STEP)
    delta = -ADAM_LR * (m_hat / (jnp.sqrt(v_hat) + ADAM_EPS) + ADAM_WD * w)
    return delta, m, v


def _adamw_shard(n_recv, g_own, recv, w, m, v, *, rows, name):
    R, C = g_own.shape

    def body(n_ref, g_ref, r_ref, w_ref, m_ref, v_ref, go_ref, d_ref, mo_ref, vo_ref):
        g = g_ref[...]
        for k in range(N_DEV - 1):
            g = g + jnp.where(k < n_ref[0], r_ref[k].astype(F32), 0.0)
        d, mn, vn = _adamw_math(w_ref[...], g, m_ref[...], v_ref[...])
        go_ref[...] = g
        d_ref[...] = d
        mo_ref[...] = mn
        vo_ref[...] = vn

    blk = pl.BlockSpec((rows, C), lambda i: (i, 0))
    return pl.pallas_call(
        body, name=name, grid=(R // rows,),
        in_specs=[pl.BlockSpec(memory_space=pltpu.SMEM), blk,
                  pl.BlockSpec((N_DEV - 1, rows, C), lambda i: (0, i, 0)), blk, blk, blk],
        out_specs=[blk] * 4, out_shape=[jax.ShapeDtypeStruct((R, C), F32)] * 4,
        compiler_params=_params(("arbitrary",)),
    )(n_recv, g_own, recv, w, m, v)


def _minor_rows_view(a):
    return jnp.transpose(a, (2, 0, 1)).reshape(SHARD_COLS * 8, 128)


def _from_minor_rows_view(v):
    return jnp.transpose(v.reshape(SHARD_COLS, 8, 128), (1, 2, 0)).reshape(1, D_MODEL, SHARD_COLS)


def _adamw_w_in(n_recv, g_own, recv, w, m, v):
    C = SHARD_COLS
    pad = -C % 128

    def body(n_ref, g_ref, r_ref, w_ref, m_ref, v_ref, go_ref, d_ref, mo_ref, vo_ref):
        for q in range(D_MODEL // 128):
            band = pl.ds(q * 128, 128)
            g = g_ref[band, :]
            for k in range(N_DEV - 1):
                g = g + jnp.where(k < n_ref[0], r_ref[k, band, :].astype(F32), 0.0)
            g = jnp.pad(g, ((0, 0), (0, pad))).T[0:C]
            rows = pl.ds(q, C, stride=8)
            d, mn, vn = _adamw_math(w_ref[rows, :], g, m_ref[rows, :], v_ref[rows, :])
            go_ref[rows, :] = g
            d_ref[rows, :] = d
            mo_ref[rows, :] = mn
            vo_ref[rows, :] = vn

    return pl.pallas_call(
        body, name="adamw_w_in", out_shape=[jax.ShapeDtypeStruct(w.shape, F32)] * 4,
        in_specs=[pl.BlockSpec(memory_space=pltpu.SMEM)] + [pl.BlockSpec(memory_space=pltpu.VMEM)] * 5,
        out_specs=[pl.BlockSpec(memory_space=pltpu.VMEM)] * 4,
        compiler_params=_params(),
    )(n_recv, g_own, recv, w, m, v)


SMALL = ("conv_b", "dt_bias", "a_log", "d_skip", "ssd_norm_w", "attn_sinks", "ln_g", "ln_b")


def _adamw_small(gathered, params):
    n_p = len(SMALL)

    def body(*refs):
        acc = []
        for r in refs[:5]:
            t = r[0]
            for k in range(1, N_DEV):
                t = t + r[k]
            acc.append(t)
        head, conv, norm, scal, sink = acc
        grads = dict(conv_b=conv[4:5, :], dt_bias=scal[0:1, 0:N_HEADS], a_log=scal[1:2, 0:N_HEADS],
                     d_skip=scal[2:3, 0:N_HEADS], ssd_norm_w=norm[0:1, :], attn_sinks=sink[0:1, 0:N_HEADS],
                     ln_g=head[0:1, :], ln_b=head[1:2, :])
        wmv = refs[5:5 + 3 * n_p]
        outs = refs[5 + 3 * n_p:]
        outs[0][...] = head[3:4, 0:1]
        outs[1][...] = conv[0:4, :]
        for i, name in enumerate(SMALL):
            w_ref, m_ref, v_ref = wmv[3 * i:3 * i + 3]
            g = grads[name]
            d, mn, vn = _adamw_math(w_ref[...], g, m_ref[...], v_ref[...])
            for o_ref, val in zip(outs[2 + 4 * i:6 + 4 * i], (g, d, mn, vn)):
                o_ref[...] = val

    flat = [a for name in SMALL for a in params[name]]
    out_shape = [jax.ShapeDtypeStruct((1, 1), F32), jax.ShapeDtypeStruct((4, D_XBC), F32)]
    for name in SMALL:
        out_shape += [jax.ShapeDtypeStruct(params[name][0].shape, F32)] * 4
    res = pl.pallas_call(body, name="adamw_small", out_shape=out_shape, compiler_params=_params())(*gathered, *flat)
    return res[0], res[1], {name: res[2 + 4 * i:6 + 4 * i] for i, name in enumerate(SMALL)}


def _adamw_plain(g, w, m, v):
    def body(g_ref, w_ref, m_ref, v_ref, d_ref, mo_ref, vo_ref):
        d, mn, vn = _adamw_math(w_ref[...], g_ref[...], m_ref[...], v_ref[...])
        d_ref[...] = d
        mo_ref[...] = mn
        vo_ref[...] = vn

    return pl.pallas_call(
        body, name="adamw_conv_w", out_shape=[jax.ShapeDtypeStruct(w.shape, F32)] * 3,
        compiler_params=_params(),
    )(g, w, m, v)


def _lane_pattern(fn):
    return np.asarray([fn(l % HEAD_DIM) for l in range(128)], np.float32)


ROPE_INV = _lane_pattern(lambda r: ROPE_THETA ** (-2.0 * (r % 8) / ROPE_DIM) if r < ROPE_DIM else 0.0)
ROPE_SIN_A = _lane_pattern(lambda r: 1.0 if 8 <= r < ROPE_DIM else 0.0)
ROPE_SIN_B = _lane_pattern(lambda r: -1.0 if r < 8 else 0.0)


def _rope_tables(positions):
    ang = positions.astype(F32)[:, None] * ROPE_INV[None, :]
    sn = jnp.sin(ang)
    return jnp.concatenate([jnp.cos(ang), sn * ROPE_SIN_A[None, :], sn * ROPE_SIN_B[None, :]], axis=1)


def _expansion():
    E = np.arange(1024)[None, :] // HEAD_DIM == np.arange(128)[:, None]
    return jnp.asarray(E, BF16), jnp.asarray(E.T, BF16)


def _ssd_args(conv_w, conv_b, dt_bias, a_log, d_skip, norm_w, E):
    return (conv_w, conv_b, dt_bias.reshape(-1), a_log.reshape(-1), d_skip.reshape(-1), norm_w, E)


def kernel(x, positions, w_in, conv_w, conv_b, dt_bias, a_log, d_skip, ssd_norm_w, attn_sinks, w_out, ln_g, ln_b, loss_target, m_w_in, m_conv_w, m_conv_b, m_dt_bias, m_a_log, m_d_skip, m_ssd_norm_w, m_attn_sinks, m_w_out, m_ln_g, m_ln_b, v_w_in, v_conv_w, v_conv_b, v_dt_bias, v_a_log, v_d_skip, v_ssd_norm_w, v_attn_sinks, v_w_out, v_ln_g, v_ln_b):
    me = _index(*_position())
    me1 = me.reshape(1).astype(jnp.int32)
    x0, target = x[0], loss_target[0]
    bf16_shard = lambda shape: jax.ShapeDtypeStruct(shape, BF16)
    E, ET = _expansion()
    tabs = _rope_tables(positions[0])
    sinks = attn_sinks.reshape(-1)

    w_ssd, w_att = _unpack_w_in(_gather_w_in(w_in[0].astype(BF16)))
    gather_conv_w = _Hosted([conv_w[0]], [jax.ShapeDtypeStruct((N_DEV,) + conv_w.shape[1:], F32)],
                            [_Flow("gather", 0, 0)])

    proj_ssd, xb, conv_w_all = _matmul(x0, w_ssd, tm=1024, tn=S_W // 2, name="in_proj_ssd", emit_a=True,
                                       comm=gather_conv_w)
    conv_w_f = jnp.transpose(conv_w_all, (1, 0, 2)).reshape(4, D_XBC)
    ssd_args = _ssd_args(conv_w_f, conv_b, dt_bias, a_log, d_skip, ssd_norm_w, E)
    proj_att = _matmul(xb, w_att, tm=1024, tn=A_W // 2, name="in_proj_att")
    gather_w_out = _Hosted([w_out[0].astype(BF16)], [bf16_shard((N_DEV, 256, D_MODEL))], [_Flow("gather", 0, 0)])
    y, ypre, hprev, pre, w_out_all = _mixer_forward(proj_ssd, proj_att, tabs, sinks, *ssd_args, comm=gather_w_out)
    w_out_f = w_out_all.reshape(2 * D_MODEL, D_MODEL)
    dr, dy, acc_head = _head(y, x0, target, w_out_f, ln_g, ln_b, tm=512)

    dw_out, dw_out_bf16 = _matmul_tn(y, dr, tl=512, tn=D_MODEL, name="dw_out", emit_bf16=True)
    own_out = lax.dynamic_index_in_dim(dw_out.reshape(N_DEV, 256, D_MODEL), me, axis=0, keepdims=False)
    send_out = _Hosted([dw_out_bf16.reshape(N_DEV, 256, D_MODEL)], [bf16_shard((N_DEV - 1, 256, D_MODEL))],
                       [_Flow("exchange", 0, 0)])
    d_ssd, acc_cw, acc_w, acc_s, recv_out = _ssd_backward(proj_ssd, hprev, ypre, pre, dy, *ssd_args, ET, comm=send_out)
    dw_ssd = _matmul_tn(xb, d_ssd, tl=512, tn=S_W // 2, name="dw_in_ssd")
    parts_lo, own_lo = _pack_dw_in(me1, dw_ssd, None, 0)
    recv_shape = bf16_shard((N_DEV - 1, D_MODEL, SHARD_COLS))
    send_lo = _Hosted([parts_lo], [recv_shape], [_Flow("exchange", 0, 0, target_x=0, target_c=0)])
    d_att, dsink, recv_in = _swa_backward(proj_att, tabs, sinks, dy, comm=send_lo)
    dw_att = _matmul_tn(xb, d_att, tl=512, tn=A_W // 2, name="dw_in_att")
    (stack_hi,) = _pack_dw_in(me1, dw_ssd, dw_att, 1)
    pos3 = jnp.stack(_position()).astype(jnp.int32)
    chip_hi, own_in = _pair_sum(pos3, stack_hi, _pair_swap(stack_hi), own_lo)
    accs = [acc_head, acc_cw, acc_w, acc_s, dsink]
    send_hi = _Hosted([chip_hi, recv_in, parts_lo] + accs,
                      [recv_shape] + [jax.ShapeDtypeStruct((N_DEV,) + a.shape, F32) for a in accs],
                      [_Flow("chip_exchange", 0, 0, target_x=1), _Flow("exchange", 2, 0, target_x=0, target_c=1)]
                      + [_Flow("gather", 3 + i, 1 + i) for i in range(5)], aliases={1: 0})
    dx, recv_in, *gathered = _input_gradient(d_ssd, d_att, w_ssd, w_att, dr, tm=256, comm=send_hi)
    n_recv_in = jnp.where(me < 4, N_DEV - 1, 3).reshape(1).astype(jnp.int32)
    n_recv_out = jnp.full((1,), N_DEV - 1, jnp.int32)

    g_in, d_in, nm_in, nv_in = [_from_minor_rows_view(r) for r in _adamw_w_in(
        n_recv_in, own_in, recv_in, _minor_rows_view(w_in), _minor_rows_view(m_w_in), _minor_rows_view(v_w_in))]
    g_out, d_out, nm_out, nv_out = _adamw_shard(n_recv_out, own_out, recv_out, w_out[0], m_w_out[0], v_w_out[0],
                                                rows=256, name="adamw_w_out")
    loss, g_conv_w, small = _adamw_small(gathered, dict(
        conv_b=(conv_b, m_conv_b, v_conv_b), dt_bias=(dt_bias, m_dt_bias, v_dt_bias), a_log=(a_log, m_a_log, v_a_log),
        d_skip=(d_skip, m_d_skip, v_d_skip), ssd_norm_w=(ssd_norm_w, m_ssd_norm_w, v_ssd_norm_w),
        attn_sinks=(attn_sinks, m_attn_sinks, v_attn_sinks), ln_g=(ln_g, m_ln_g, v_ln_g), ln_b=(ln_b, m_ln_b, v_ln_b)))
    g_cw = lax.dynamic_slice_in_dim(g_conv_w, me * (D_XBC // N_DEV), D_XBC // N_DEV, axis=1)
    d_cw, nm_cw, nv_cw = _adamw_plain(g_cw, conv_w[0], m_conv_w[0], v_conv_w[0])

    def leaves(i, big_in, cw, big_out):
        mid = [small[k][i] for k in ("conv_b", "dt_bias", "a_log", "d_skip", "ssd_norm_w", "attn_sinks")]
        return [big_in, cw[None]] + mid + [big_out[None], small["ln_g"][i], small["ln_b"][i]]

    return (loss.reshape(()), dx[None], *leaves(0, g_in, g_cw, g_out), *leaves(1, d_in, d_cw, d_out),
            *leaves(2, nm_in, nm_cw, nm_out), *leaves(3, nv_in, nv_cw, nv_out))
```

```python
import jax
import jax.numpy as jnp
from jax import lax
from jax.experimental import pallas as pl
from jax.experimental.pallas import tpu as pltpu
import numpy as np

F32 = jnp.float32
BF16 = jnp.bfloat16
_MXU = jnp.bfloat16

N_DEV = 8
D_MODEL = 1024
D_SSD = 1024
D_ATT = 1024
HEAD_DIM = 64
N_HEADS = 16
SSD_GROUPS = 2
KV_HEADS = 4
CHUNK = 128
D_XBC = 1536
D_IN_PROJ = 5136
ROPE_DIM = 16
ROPE_THETA = 500000.0
ALPHA = (2.0 * 1) ** 0.25
LN_EPS = 1e-5
RMS_EPS = 1e-5
ATT_SCALE = HEAD_DIM ** -0.5
NEG = -1e30

S_Z, S_XS, S_B, S_C, S_DT, S_W = 0, 1024, 2048, 2304, 2560, 2816
N_SSD_REAL = 2576
A_Q, A_K, A_V, A_G, A_W = 0, 1024, 1280, 1536, 2560

ADAM_LR = 0.001
ADAM_B1 = 0.9
ADAM_B2 = 0.999
ADAM_EPS = 1e-08
ADAM_WD = 0.01
ADAM_STEP = 10

VMEM_LIMIT = 48 * 1024 * 1024
MESH = pl.DeviceIdType.MESH


def _params(sem=None):
    return pltpu.CompilerParams(dimension_semantics=sem, vmem_limit_bytes=VMEM_LIMIT)


def _mm(a, b):
    return jnp.dot(a.astype(_MXU), b.astype(_MXU), preferred_element_type=F32)


def _mm_nt(a, b):
    return lax.dot_general(a.astype(_MXU), b.astype(_MXU), (((1,), (1,)), ((), ())),
                           preferred_element_type=F32)


def _mm_tn(a, b):
    return lax.dot_general(a.astype(_MXU), b.astype(_MXU), (((0,), (0,)), ((), ())),
                           preferred_element_type=F32)


def _split3(v):
    hi = v.astype(BF16)
    r = v - hi.astype(F32)
    mid = r.astype(BF16)
    lo = (r - mid.astype(F32)).astype(BF16)
    return hi, mid, lo


def _mm_exact_r(v, p01):
    hi, mid, lo = _split3(v)
    d = lambda a: jnp.dot(a, p01, preferred_element_type=F32)
    return d(hi) + d(mid) + d(lo)


def _mm_exact_l(p01, v):
    hi, mid, lo = _split3(v)
    d = lambda a: jnp.dot(p01, a, preferred_element_type=F32)
    return d(hi) + d(mid) + d(lo)


def _mm_2pass_r(v, p01):
    hi = v.astype(BF16)
    lo = (v - hi.astype(F32)).astype(BF16)
    return jnp.dot(hi, p01, preferred_element_type=F32) + jnp.dot(lo, p01, preferred_element_type=F32)


def _sigmoid(x):
    return 1.0 / (1.0 + jnp.exp(-x))


def _softplus(x):
    e = jnp.exp(-jnp.abs(x))
    u = 1.0 + e
    log1p = jnp.where(u == 1.0, e, jnp.log(u) * (e / (u - 1.0)))
    return jnp.maximum(x, 0.0) + log1p


def _rows8(rows):
    n = rows[0].shape[1]
    rid = lax.broadcasted_iota(jnp.int32, (8, n), 0)
    out = jnp.zeros((8, n), F32)
    for k, r in enumerate(rows):
        out = out + jnp.where(rid == k, r, 0.0)
    return out


def _colsum(a):
    return jnp.sum(a, axis=0, keepdims=True)


def _matmul(a, b, *, tm, tn, name, emit_a=False, comm=None):
    M, K = a.shape
    N = b.shape[1]

    def body(a_ref, b_ref, o_ref, *rest):
        am = a_ref[...].astype(_MXU)
        o_ref[...] = jnp.dot(am, b_ref[...].astype(_MXU), preferred_element_type=F32)
        if emit_a:
            rest[0][...] = am

    out_specs = [pl.BlockSpec((tm, tn), lambda i, j: (i, j))]
    out_shape = [jax.ShapeDtypeStruct((M, N), F32)]
    if emit_a:
        out_specs.append(pl.BlockSpec((tm, K), lambda i, j: (i, 0)))
        out_shape.append(jax.ShapeDtypeStruct((M, K), _MXU))
    res = _call(
        body, comm, name=name, grid=(M // tm, N // tn),
        in_specs=[pl.BlockSpec((tm, K), lambda i, j: (i, 0)), pl.BlockSpec((K, tn), lambda i, j: (0, j))],
        out_specs=out_specs, out_shape=out_shape, scratch_shapes=[], args=(a, b))
    return res if (emit_a or comm is not None) else res[0]


def _matmul_tn(a, g, *, tl, tn, name, emit_bf16=False):
    L, M = a.shape
    N = g.shape[1]
    last = L // tl - 1

    def body(a_ref, g_ref, o_ref, *rest):
        @pl.when(pl.program_id(1) == 0)
        def _():
            o_ref[...] = jnp.zeros_like(o_ref)

        o_ref[...] += _mm_tn(a_ref[...], g_ref[...])
        if emit_bf16:
            @pl.when(pl.program_id(1) == last)
            def _():
                rest[0][...] = o_ref[...].astype(BF16)

    spec = pl.BlockSpec((M, tn), lambda j, l: (0, j))
    res = pl.pallas_call(
        body, name=name, grid=(N // tn, L // tl),
        in_specs=[pl.BlockSpec((tl, M), lambda j, l: (l, 0)), pl.BlockSpec((tl, tn), lambda j, l: (l, j))],
        out_specs=[spec, spec] if emit_bf16 else [spec],
        out_shape=[jax.ShapeDtypeStruct((M, N), F32)] + ([jax.ShapeDtypeStruct((M, N), BF16)] if emit_bf16 else []),
        compiler_params=_params(("arbitrary", "arbitrary")),
    )(a, g)
    return res if emit_bf16 else res[0]


def _position():
    return lax.axis_index("x"), lax.axis_index("y"), lax.axis_index("c")


def _index(px, py, pc):
    return 4 * px + 2 * py + pc


def _flip(pos, k):
    x, y, c = pos
    return ((1 - x) if (k >> 2) & 1 else x, (1 - y) if (k >> 1) & 1 else y, (1 - c) if k & 1 else c)


def _when(cond, fn):
    if cond is True:
        fn()
    else:
        pl.when(cond)(fn)


def _remote(src, dst, send_sem, recv_sem, peer):
    return pltpu.make_async_remote_copy(src_ref=src, dst_ref=dst, send_sem=send_sem, recv_sem=recv_sem,
                                        device_id=peer, device_id_type=MESH)


class _Flow:
    def __init__(self, kind, operand, result, target_x=None, target_c=None):
        self.kind, self.operand, self.result, self.target_x, self.target_c = kind, operand, result, target_x, target_c

    def owns(self, pos):
        if self.target_x is None:
            return True
        cond = pos[0] == self.target_x
        return cond if self.target_c is None else cond & (pos[2] == self.target_c)


class _Hosted:
    def __init__(self, operands, out_shapes, flows, aliases=None):
        self.operands, self.out_shapes, self.flows = operands, out_shapes, flows
        self.aliases = aliases or {}

    def plan(self, ins, outs, send_sems, recv_sems, local_sems):
        me = _position()
        mi = _index(*me)
        sends, recvs, locals_ = [], [], []
        for row, f in enumerate(self.flows):
            src, dst = ins[f.operand], outs[f.result]
            for k in range(1, N_DEV):
                peer = _flip(me, k)
                sems = (send_sems.at[row, k - 1], recv_sems.at[row, k - 1])
                if f.kind == "exchange":
                    owner = _index(*peer) if f.target_x is None else 2 * peer[1] + peer[2]
                    cp = _remote(src.at[owner], dst.at[k - 1], *sems, peer)
                    sends.append((f.owns(peer), cp))
                    recvs.append((f.owns(me), cp))
                elif f.kind == "chip_exchange":
                    if k & 1:
                        continue
                    cp = _remote(src.at[peer[1]], dst.at[k // 2 - 1], *sems, peer)
                    sends.append((peer[0] == f.target_x, cp))
                    recvs.append((me[0] == f.target_x, cp))
                else:
                    sends.append((True, _remote(src, dst.at[mi], *sems, peer)))
                    recvs.append((True, _remote(src, dst.at[_index(*peer)], *sems, peer)))
            if f.kind == "gather":
                locals_.append(pltpu.make_async_copy(src, dst.at[mi], local_sems.at[row]))

        def start():
            for cp in locals_:
                cp.start()
            for cond, cp in sends:
                _when(cond, cp.start)

        def wait():
            for cond, cp in recvs:
                _when(cond, cp.wait_recv)
            for cond, cp in sends:
                _when(cond, cp.wait_send)
            for cp in locals_:
                cp.wait()

        return start, wait


def _call(body, comm, *, name, grid, in_specs, out_specs, out_shape, scratch_shapes, args, aliases=None):
    io_alias = dict(aliases or {})
    semantics = ("arbitrary",) * len(grid)
    if comm is None:
        return pl.pallas_call(body, name=name, grid=grid, in_specs=in_specs, out_specs=out_specs, out_shape=out_shape,
                              scratch_shapes=scratch_shapes, input_output_aliases=io_alias,
                              compiler_params=_params(semantics))(*args)
    n_in, n_out, n_scr = len(args), len(out_shape), len(scratch_shapes)
    c_in, c_out, rows = len(comm.operands), len(comm.out_shapes), len(comm.flows)

    def hosted(*refs):
        ins, refs = refs[:n_in], refs[n_in:]
        cins, refs = refs[:c_in], refs[c_in:]
        outs, refs = refs[:n_out], refs[n_out:]
        couts, refs = refs[:c_out], refs[c_out:]
        scr, (send_sems, recv_sems, local_sems) = refs[:n_scr], refs[n_scr:]
        start, wait = comm.plan(cins, couts, send_sems, recv_sems, local_sems)
        ids = [pl.program_id(d) for d in range(len(grid))]
        first, last = ids[0] == 0, ids[0] == grid[0] - 1
        for d in range(1, len(grid)):
            first, last = first & (ids[d] == 0), last & (ids[d] == grid[d] - 1)
        pl.when(first)(start)
        body(*ins, *outs, *scr)
        pl.when(last)(wait)

    for ci, co in comm.aliases.items():
        io_alias[n_in + ci] = n_out + co
    any_spec = pl.BlockSpec(memory_space=pl.ANY)
    sems = [pltpu.SemaphoreType.DMA((rows, N_DEV - 1)), pltpu.SemaphoreType.DMA((rows, N_DEV - 1)),
            pltpu.SemaphoreType.DMA((rows,))]
    return pl.pallas_call(
        hosted, name=name, grid=grid, in_specs=list(in_specs) + [any_spec] * c_in,
        out_specs=list(out_specs) + [any_spec] * c_out, out_shape=list(out_shape) + list(comm.out_shapes),
        scratch_shapes=list(scratch_shapes) + sems, input_output_aliases=io_alias,
        compiler_params=_params(semantics))(*args, *comm.operands)


def _head_row(ref, width, rep):
    hid = lax.broadcasted_iota(jnp.int32, (1, width), 1) // rep
    row = jnp.zeros((1, width), F32)
    for h in range(N_HEADS):
        row = jnp.where(hid == h, ref[h], row)
    return row


def _rows_from_above(u_b, s, ext_scr, row, col):
    down = (row - col == s).astype(_MXU)
    return jnp.concatenate([ext_scr[8 - s:16 - s, :], jnp.dot(down, u_b, preferred_element_type=F32)[8:128]], axis=0)


def _ssd_recompute(first, p_ref, halo_ref, cw_ref, cb_ref, dtb_ref, alog_ref, e_ref, ext_scr, pre=None):
    row = lax.broadcasted_iota(jnp.int32, (128, 128), 0)
    col = lax.broadcasted_iota(jnp.int32, (128, 128), 1)
    ext_scr[0:8, :] = jnp.where(first, 0.0, halo_ref[:, S_XS:S_DT])
    if pre is not None:
        ext_scr[8:16, :] = p_ref[0:8, S_XS:S_DT]
    else:
        ext_scr[8:136, :] = p_ref[:, S_XS:S_DT]
        cw = cw_ref[...]
        pre = (cb_ref[0:1, :] + cw[3:4, :] * ext_scr[8:136, :] + cw[2:3, :] * ext_scr[7:135, :]
               + cw[1:2, :] * ext_scr[6:134, :] + cw[0:1, :] * ext_scr[5:133, :])
    sg = _sigmoid(pre)
    act = pre * sg
    lane = lax.broadcasted_iota(jnp.int32, (1, 128), 1)
    A = jnp.where(lane < N_HEADS, -jnp.exp(_head_row(alog_ref, 128, 1)), 0.0)
    raw = p_ref[:, S_DT:S_DT + 128] + _head_row(dtb_ref, 128, 1)
    dt = _softplus(raw)
    dA = dt * A
    tril = (row >= col).astype(BF16)
    acs = _mm_exact_l(tril, dA)
    last = acs[127:128, :]
    ds = jnp.exp(last - acs)
    eo = jnp.exp(acs)
    E = e_ref[...]
    ex = _mm_2pass_r(jnp.concatenate([dt, ds, eo], axis=0), E)
    dt_e, ds_e, eo_e = ex[0:128], ex[128:256], ex[256:384]
    xs_c = act[:, 0:1024]
    X = xs_c * dt_e
    return dict(pre=pre, sg=sg, xs_c=xs_c, Bc=act[:, 1024:1280], Cc=act[:, 1280:1536], A=A, raw=raw, dt=dt,
                acs=acs, acsT=acs.T, eo_e=eo_e, ds_e=ds_e, dt_e=dt_e, cd_e=eo_e[127:128, :],
                X=X, Xd=X * ds_e, row=row, col=col)


def _split_halves(t):
    lo = _lo_half(CHUNK)
    return jnp.concatenate([jnp.where(lo, t, 0.0), jnp.where(lo, 0.0, t)], axis=0)


def _ssd_core(R, hprev):
    causal = R["row"] >= R["col"]
    acs, acsT, X = R["acs"], R["acsT"], R["X"]
    ydiag, yoff, snew = [], [], []
    for g in range(SSD_GROUPS):
        Bg = R["Bc"][:, g * 128:(g + 1) * 128]
        Cg = R["Cc"][:, g * 128:(g + 1) * 128]
        cols = slice(g * 512, (g + 1) * 512)
        CB = _mm_nt(Cg, Bg)
        snew.append(_mm_tn(Bg, R["Xd"][:, cols]))
        yoff.append(_mm(Cg, hprev[:, cols]))
        for j in range(4):
            h0 = g * 8 + 2 * j
            ms = [CB * jnp.exp(jnp.where(causal, acs[:, h:h + 1] - acsT[h:h + 1, :], NEG)) for h in (h0, h0 + 1)]
            ydiag.append(_mm(jnp.concatenate(ms, axis=1), _split_halves(X[:, h0 * HEAD_DIM:h0 * HEAD_DIM + 128])))
    Y = jnp.concatenate(ydiag, axis=1) + jnp.concatenate(yoff, axis=1) * R["eo_e"]
    return Y, jnp.concatenate(snew, axis=1)


def _ssd_forward_step(p_ref, halo_ref, cw_ref, cb_ref, dtb_ref, alog_ref, dsk_ref, nw_ref, e_ref,
                      y_ref, ypre_ref, hprev_ref, pre_ref, h_scr, ext_scr):
    c = pl.program_id(0)
    first = c == 0

    @pl.when(first)
    def _():
        h_scr[...] = jnp.zeros_like(h_scr)

    R = _ssd_recompute(first, p_ref, halo_ref, cw_ref, cb_ref, dtb_ref, alog_ref, e_ref, ext_scr)
    hprev = h_scr[...]
    hprev_ref[...] = hprev
    pre_ref[...] = R["pre"]
    Y, snew = _ssd_core(R, hprev)
    h_scr[...] = hprev * R["cd_e"] + snew
    Y = Y + _head_row(dsk_ref, D_SSD, HEAD_DIM) * R["xs_c"]
    ypre_ref[...] = Y
    z = p_ref[:, S_Z:S_Z + 1024]
    yf = Y * (z * _sigmoid(z))
    outs = []
    for g in range(SSD_GROUPS):
        yg = yf[:, g * 512:(g + 1) * 512]
        r = lax.rsqrt(jnp.mean(yg * yg, axis=-1, keepdims=True) + RMS_EPS)
        outs.append(yg * r)
    y_ref[:, 0:D_SSD] = (jnp.concatenate(outs, axis=1) * nw_ref[0:1, :]).astype(y_ref.dtype)


def _ssd_backward(proj_ssd, hprev_all, ypre, pre, dy, conv_w, conv_b, dt_bias, a_log, d_skip, norm_w, E, ET, comm=None):
    L = proj_ssd.shape[0]
    nc = L // CHUNK

    def body(p_ref, halo_ref, hprev_ref, ypre_ref, pre_ref, dy_ref, cw_ref, cb_ref, dtb_ref, alog_ref, dsk_ref, nw_ref, e_ref,
             et_ref, dp_ref, acc_cw_ref, acc_w_ref, acc_s_ref, dh_scr, ext_scr, ext2_scr, nxt_scr):
        i = pl.program_id(0)
        c = nc - 1 - i
        first = c == 0

        @pl.when(i == 0)
        def _():
            dh_scr[...] = jnp.zeros_like(dh_scr)
            nxt_scr[...] = jnp.zeros_like(nxt_scr)
            acc_cw_ref[...] = jnp.zeros_like(acc_cw_ref)
            acc_w_ref[...] = jnp.zeros_like(acc_w_ref)
            acc_s_ref[...] = jnp.zeros_like(acc_s_ref)

        R = _ssd_recompute(first, p_ref, halo_ref, cw_ref, cb_ref, dtb_ref, alog_ref, e_ref, ext_scr, pre_ref[...])
        hprev = hprev_ref[...]
        xs_c, X, Xd = R["xs_c"], R["X"], R["Xd"]
        acs, acsT = R["acs"], R["acsT"]
        ET = et_ref[...]
        dsk = _head_row(dsk_ref, D_SSD, HEAD_DIM)
        Y = ypre_ref[...]

        z = p_ref[:, S_Z:S_Z + 1024]
        sz = _sigmoid(z)
        silz = z * sz
        yf = Y * silz
        dyv = dy_ref[...]
        nw = nw_ref[0:1, :]
        dyf_parts, dnw_parts = [], []
        for g in range(SSD_GROUPS):
            cols = slice(g * 512, (g + 1) * 512)
            yg = yf[:, cols]
            r = lax.rsqrt(jnp.mean(yg * yg, axis=-1, keepdims=True) + RMS_EPS)
            yn = yg * r
            dyn = dyv[:, cols] * nw[:, cols]
            dnw_parts.append(_colsum(dyv[:, cols] * yn))
            dyf_parts.append(r * (dyn - yn * jnp.mean(dyn * yn, axis=-1, keepdims=True)))
        dyf = jnp.concatenate(dyf_parts, axis=1)
        dY = dyf * silz
        dz = dyf * Y * (sz * (1.0 + z * (1.0 - sz)))

        dhn = dh_scr[...]
        dYo = dY * R["eo_e"]
        causal = R["row"] >= R["col"]
        dacs = jnp.zeros((128, 128), F32)
        dacs_t = jnp.zeros((128, 128), F32)
        dxdiag, dxd, dhprev, dBs, dCs, yoff = [], [], [], [], [], []
        for g in range(SSD_GROUPS):
            Bg = R["Bc"][:, g * 128:(g + 1) * 128]
            Cg = R["Cc"][:, g * 128:(g + 1) * 128]
            cols = slice(g * 512, (g + 1) * 512)
            CB = _mm_nt(Cg, Bg)
            dCB = jnp.zeros((128, 128), F32)
            for j in range(4):
                h0 = g * 8 + 2 * j
                pc = slice(h0 * HEAD_DIM, h0 * HEAD_DIM + 128)
                dYst = _split_halves(dY[:, pc])
                dMst = _mm_nt(dYst, X[:, pc])
                mts = []
                for a, h in enumerate((h0, h0 + 1)):
                    acol = acs[:, h:h + 1]
                    arow = acsT[h:h + 1, :]
                    Lm = jnp.exp(jnp.where(causal, acol - arow, NEG))
                    M = CB * Lm
                    dM = dMst[a * 128:(a + 1) * 128]
                    dCB = dCB + dM * Lm
                    G = dM * M
                    dacs = dacs + jnp.where(R["col"] == h, jnp.sum(G, axis=1, keepdims=True), 0.0)
                    dacs_t = dacs_t + jnp.where(R["row"] == h, jnp.sum(G, axis=0, keepdims=True), 0.0)
                    mts.append(M.T)
                dxdiag.append(_mm(jnp.concatenate(mts, axis=1), dYst))
            dS = dhn[:, cols]
            dxd.append(_mm(Bg, dS))
            yoff.append(_mm(Cg, hprev[:, cols]))
            dhprev.append(_mm_tn(Cg, dYo[:, cols]))
            dCs.append(_mm_nt(dYo[:, cols], hprev[:, cols]) + _mm(dCB, Bg))
            dBs.append(_mm_tn(dCB, Cg) + _mm_nt(Xd[:, cols], dS))
        Yoff = jnp.concatenate(yoff, axis=1) * R["eo_e"]
        dXd = jnp.concatenate(dxd, axis=1)
        dX = jnp.concatenate(dxdiag, axis=1) + dXd * R["ds_e"]
        t_state = dXd * Xd
        dacs = dacs + _mm_2pass_r(dY * Yoff - t_state, ET) - dacs_t.T
        v_last = _colsum(t_state + dhn * hprev * R["cd_e"])
        dlast = _mm_exact_r(jnp.broadcast_to(v_last, (8, 1024)), ET)[0:1, :]
        dacs = dacs + jnp.where(R["row"] == 127, dlast, 0.0)
        triu = (R["col"] >= R["row"]).astype(BF16)
        da = _mm_exact_l(triu, dacs)
        ddt = da * R["A"] + _mm(dX * xs_c, ET)
        ddt_raw = ddt * _sigmoid(R["raw"])
        dxs_c = dX * R["dt_e"] + dY * dsk
        dh_scr[...] = jnp.concatenate(dhprev, axis=1) + dhn * R["cd_e"]

        dact = jnp.concatenate([dxs_c] + dBs + dCs, axis=1)
        pre, sg = R["pre"], R["sg"]
        dpre = dact * (sg * (1.0 + pre * (1.0 - sg)))
        ext2_scr[0:8, :] = dpre[120:128, :]
        ext2_scr[8:16, :] = nxt_scr[...]
        nxt_scr[...] = dpre[0:8, :]
        cw = cw_ref[...]
        u_b, dpre_b = p_ref[:, S_XS:S_DT].astype(_MXU), dpre.astype(_MXU)
        dxbc = cw[3:4, :] * dpre
        taps = [_colsum(dpre * p_ref[:, S_XS:S_DT])]
        for s in (1, 2, 3):
            up = (R["col"] - R["row"] == s).astype(_MXU)
            d_s = jnp.concatenate([jnp.dot(up, dpre_b, preferred_element_type=F32)[0:120],
                                   ext2_scr[s:8 + s, :]], axis=0)
            dxbc = dxbc + cw[3 - s:4 - s, :] * d_s
            taps.append(_colsum(dpre * _rows_from_above(u_b, s, ext_scr, R["row"], R["col"])))
        acc_cw_ref[...] += _rows8(taps[::-1] + [_colsum(dpre)])
        acc_w_ref[...] += _rows8([jnp.concatenate(dnw_parts, axis=1), _colsum(dY * xs_c)])
        acc_s_ref[...] += _rows8([_colsum(ddt_raw), _colsum(da * R["dt"])])

        lane = lax.broadcasted_iota(jnp.int32, (128, 128), 1)
        dp_ref[:, S_Z:S_Z + 1024] = dz.astype(dp_ref.dtype)
        dp_ref[:, S_XS:S_DT] = dxbc.astype(dp_ref.dtype)
        dp_ref[:, S_DT:S_DT + 128] = jnp.where(lane < N_HEADS, ddt_raw, 0.0).astype(dp_ref.dtype)
        dp_ref[:, S_DT + 128:S_W] = jnp.zeros((128, 128), dp_ref.dtype)

        @pl.when(i == nc - 1)
        def _():
            acc = acc_s_ref[...]
            dskip = _mm_exact_r(acc_w_ref[...], ET)[1:2, :]
            acc_s_ref[...] = _rows8([acc[0:1, :], acc[1:2, :] * R["A"], dskip])

    const = lambda shape: pl.BlockSpec(shape, lambda i: (0, 0))
    smem = pl.BlockSpec(memory_space=pltpu.SMEM)
    rev = lambda i: (nc - 1 - i, 0)
    return _call(
        body, comm, name="ssd_bwd", grid=(nc,),
        in_specs=[pl.BlockSpec((CHUNK, S_W), rev),
                  pl.BlockSpec((8, S_W), lambda i: (jnp.maximum((nc - 1 - i) * 16 - 1, 0), 0)),
                  pl.BlockSpec((128, 1024), rev),
                  pl.BlockSpec((CHUNK, D_SSD), rev),
                  pl.BlockSpec((CHUNK, D_XBC), rev),
                  pl.BlockSpec((CHUNK, D_SSD), rev),
                  const((4, D_XBC)), const((1, D_XBC)), smem, smem, smem, const((1, 1024)),
                  const((128, 1024)), const((1024, 128))],
        out_specs=[pl.BlockSpec((CHUNK, S_W), rev), const((8, D_XBC)), const((8, 1024)), const((8, 128))],
        out_shape=[jax.ShapeDtypeStruct((L, S_W), _MXU), jax.ShapeDtypeStruct((8, D_XBC), F32),
                   jax.ShapeDtypeStruct((8, 1024), F32), jax.ShapeDtypeStruct((8, 128), F32)],
        scratch_shapes=[pltpu.VMEM((128, 1024), F32), pltpu.VMEM((16, D_XBC), F32),
                        pltpu.VMEM((16, D_XBC), F32), pltpu.VMEM((8, D_XBC), F32)],
        args=(proj_ssd, proj_ssd, hprev_all, ypre, pre, dy, conv_w, conv_b, dt_bias, a_log, d_skip, norm_w, E, ET))


def _rope(t, tab):
    cos, sa, sb = tab[:, 0:128], tab[:, 128:256], tab[:, 256:384]
    outs = []
    for i in range(t.shape[1] // 128):
        tg = t[:, i * 128:(i + 1) * 128]
        outs.append(tg * cos + pltpu.roll(tg, 8, 1) * sa + pltpu.roll(tg, 120, 1) * sb)
    return jnp.concatenate(outs, axis=1)


def _rope_transposed(d, tab):
    cos, sa, sb = tab[:, 0:128], tab[:, 128:256], tab[:, 256:384]
    outs = []
    for i in range(d.shape[1] // 128):
        dg = d[:, i * 128:(i + 1) * 128]
        outs.append(dg * cos + pltpu.roll(dg * sa, 120, 1) + pltpu.roll(dg * sb, 8, 1))
    return jnp.concatenate(outs, axis=1)


def _lo_half(rows):
    return lax.broadcasted_iota(jnp.int32, (rows, 128), 1) < HEAD_DIM


def _kv_both(t, j):
    p, b = j // 2, j % 2
    lo = _lo_half(t.shape[0])
    nat = jnp.where(lo if b == 0 else jnp.logical_not(lo), t[:, p * 128:(p + 1) * 128], 0.0)
    return nat + pltpu.roll(nat, HEAD_DIM, 1)


def _stack_heads(t, j):
    lo = _lo_half(CHUNK)
    hi = jnp.logical_not(lo)
    a, b = t[:, 2 * j * 128:(2 * j + 1) * 128], t[:, (2 * j + 1) * 128:(2 * j + 2) * 128]
    return jnp.concatenate([jnp.where(lo, a, 0.0), jnp.where(hi, a, 0.0),
                            jnp.where(lo, b, 0.0), jnp.where(hi, b, 0.0)], axis=0)


def _unstack_heads(s):
    lo = _lo_half(CHUNK)
    return jnp.concatenate([jnp.where(lo, s[0:128], s[128:256]), jnp.where(lo, s[256:384], s[384:512])], axis=1)


def _fold_kv(r, j):
    lo = _lo_half(r.shape[0])
    return jnp.where(lo if j % 2 == 0 else jnp.logical_not(lo), r + pltpu.roll(r, HEAD_DIM, 1), 0.0)


def _sink_row(sink_ref, j):
    hid = lax.broadcasted_iota(jnp.int32, (1, 4 * CHUNK), 1) // CHUNK
    row = jnp.zeros((1, 4 * CHUNK), F32)
    for hh in range(4):
        row = jnp.where(hid == hh, sink_ref[4 * j + hh], row)
    return row


def _from_current():
    si = lax.broadcasted_iota(jnp.int32, (CHUNK, 4 * CHUNK), 0)
    qi = lax.broadcasted_iota(jnp.int32, (CHUNK, 4 * CHUNK), 1) % CHUNK
    return si <= qi


def _fold(full, from_cur, pen=0.0):
    return jnp.where(from_cur, full[CHUNK:2 * CHUNK], full[0:CHUNK] + pen)


def _unfold(t, from_cur):
    c = jnp.where(from_cur, t, 0.0)
    return jnp.concatenate([t - c, c], axis=0)


def _softmax_sink(s, sink):
    mx = jnp.maximum(jnp.max(s, axis=0, keepdims=True), sink)
    p = jnp.exp(s - mx)
    esink = jnp.exp(sink - mx)
    inv = 1.0 / (jnp.sum(p, axis=0, keepdims=True) + esink)
    return p * inv, esink * inv


def _swa_inputs(blk, p_ref, prev_ref, tab_ref, ptab_ref):
    tab = tab_ref[...]
    qr = _rope(p_ref[:, A_Q:A_Q + 1024], tab) * ATT_SCALE
    kk = jnp.concatenate([_rope(prev_ref[:, 0:256], ptab_ref[...]), _rope(p_ref[:, A_K:A_K + 256], tab)], axis=0)
    vv = jnp.concatenate([prev_ref[:, 256:512], p_ref[:, A_V:A_V + 256]], axis=0)
    return tab, qr, kk, vv, jnp.where(blk > 0, 0.0, NEG)


def _swa_forward_step(sink_ref, p_ref, prev_ref, tab_ref, ptab_ref, y_ref):
    n = pl.program_id(0)
    _, qr, kk, vv, pen = _swa_inputs(n, p_ref, prev_ref, tab_ref, ptab_ref)
    from_cur = _from_current()
    outs = []
    for j in range(KV_HEADS):
        s = _fold(_mm_nt(_kv_both(kk, j), _stack_heads(qr, j)), from_cur, pen)
        P, _ = _softmax_sink(s, _sink_row(sink_ref, j))
        outs.append(_unstack_heads(_mm_tn(_unfold(P, from_cur), _kv_both(vv, j))))
    g = p_ref[:, A_G:A_G + 1024]
    y_ref[:, D_SSD:D_SSD + D_ATT] = (jnp.concatenate(outs, axis=1) * (g * _sigmoid(g))).astype(y_ref.dtype)


def _mixer_forward(proj_ssd, proj_att, tabs, sinks, conv_w, conv_b, dt_bias, a_log, d_skip, norm_w, E, comm=None):
    L = proj_ssd.shape[0]
    nc = L // CHUNK

    def body(p_ref, halo_ref, cw_ref, cb_ref, dtb_ref, alog_ref, dsk_ref, nw_ref, e_ref,
             sink_ref, pa_ref, prev_ref, tab_ref, ptab_ref, y_ref, ypre_ref, hprev_ref, pre_ref, h_scr, ext_scr):
        _ssd_forward_step(p_ref, halo_ref, cw_ref, cb_ref, dtb_ref, alog_ref, dsk_ref, nw_ref, e_ref,
                          y_ref, ypre_ref, hprev_ref, pre_ref, h_scr, ext_scr)
        _swa_forward_step(sink_ref, pa_ref, prev_ref, tab_ref, ptab_ref, y_ref)

    const = lambda shape: pl.BlockSpec(shape, lambda c: (0, 0))
    smem = pl.BlockSpec(memory_space=pltpu.SMEM)
    rows = lambda w: pl.BlockSpec((CHUNK, w), lambda c: (c, 0))
    return _call(
        body, comm, name="mixer_fwd", grid=(nc,),
        in_specs=[rows(S_W), pl.BlockSpec((8, S_W), lambda c: (jnp.maximum(c * 16 - 1, 0), 0)),
                  const((4, D_XBC)), const((1, D_XBC)), smem, smem, smem, const((1, 1024)), const((128, 1024)),
                  smem, rows(A_W), pl.BlockSpec((CHUNK, 512), lambda c: (jnp.maximum(c - 1, 0), 2)),
                  rows(384), pl.BlockSpec((CHUNK, 384), lambda c: (jnp.maximum(c - 1, 0), 0))],
        out_specs=[rows(D_SSD + D_ATT), rows(D_SSD), pl.BlockSpec((128, 1024), lambda c: (c, 0)), rows(D_XBC)],
        out_shape=[jax.ShapeDtypeStruct((L, D_SSD + D_ATT), _MXU), jax.ShapeDtypeStruct((L, D_SSD), F32),
                   jax.ShapeDtypeStruct((nc * 128, 1024), F32), jax.ShapeDtypeStruct((L, D_XBC), F32)],
        scratch_shapes=[pltpu.VMEM((128, 1024), F32), pltpu.VMEM((136, D_XBC), F32)],
        args=(proj_ssd, proj_ssd, conv_w, conv_b, dt_bias, a_log, d_skip, norm_w, E,
              sinks, proj_att, proj_att, tabs, tabs))


def _swa_backward(proj_att, tabs, sinks, dy, comm=None):
    L = proj_att.shape[0]
    nb = L // CHUNK

    def body(sink_ref, p_ref, prev_ref, tab_ref, ptab_ref, dy_ref, dp_ref, dsink_ref, carry_k, carry_v):
        i = pl.program_id(0)
        n = nb - 1 - i

        @pl.when(i == 0)
        def _():
            carry_k[...] = jnp.zeros_like(carry_k)
            carry_v[...] = jnp.zeros_like(carry_v)
            dsink_ref[...] = jnp.zeros_like(dsink_ref)

        tab, qr, kk, vv, pen = _swa_inputs(n, p_ref, prev_ref, tab_ref, ptab_ref)
        from_cur = _from_current()
        g = p_ref[:, A_G:A_G + 1024]
        sgm = _sigmoid(g)
        dyv = dy_ref[...]
        do_all = dyv * (g * sgm)
        lane8 = lax.broadcasted_iota(jnp.int32, (8, 128), 1)
        hid = lax.broadcasted_iota(jnp.int32, (1, 4 * CHUNK), 1) // CHUNK
        o_parts, dq_parts = [], []
        dk_nat = [jnp.zeros((2 * CHUNK, 128), F32) for _ in range(2)]
        dv_nat = [jnp.zeros((2 * CHUNK, 128), F32) for _ in range(2)]
        dsink = jnp.zeros((8, 128), F32)
        for j in range(KV_HEADS):
            qs = _stack_heads(qr, j)
            kkb, vvb = _kv_both(kk, j), _kv_both(vv, j)
            P, psink = _softmax_sink(_fold(_mm_nt(kkb, qs), from_cur, pen), _sink_row(sink_ref, j))
            p_full = _unfold(P, from_cur)
            o_parts.append(_unstack_heads(_mm_tn(p_full, vvb)))
            do_s = _stack_heads(do_all, j)
            dP = _fold(_mm_nt(vvb, do_s), from_cur)
            D = jnp.sum(P * dP, axis=0, keepdims=True)
            ds_full = _unfold(P * (dP - D), from_cur)
            sd = psink * D
            for hh in range(4):
                dsink = dsink + jnp.where(lane8 == 4 * j + hh, -jnp.sum(jnp.where(hid == hh, sd, 0.0)), 0.0)
            dq_parts.append(_unstack_heads(_mm_tn(ds_full, kkb)) * ATT_SCALE)
            dk_nat[j // 2] = dk_nat[j // 2] + _fold_kv(_mm(ds_full, qs), j)
            dv_nat[j // 2] = dv_nat[j // 2] + _fold_kv(_mm(p_full, do_s), j)
        o = jnp.concatenate(o_parts, axis=1)
        dkk = jnp.concatenate(dk_nat, axis=1)
        dvv = jnp.concatenate(dv_nat, axis=1)
        out = dp_ref.dtype
        dp_ref[:, A_Q:A_Q + 1024] = _rope_transposed(jnp.concatenate(dq_parts, axis=1), tab).astype(out)
        dp_ref[:, A_K:A_K + 256] = _rope_transposed(dkk[CHUNK:2 * CHUNK] + carry_k[...], tab).astype(out)
        dp_ref[:, A_V:A_V + 256] = (dvv[CHUNK:2 * CHUNK] + carry_v[...]).astype(out)
        dp_ref[:, A_G:A_G + 1024] = (dyv * o * (sgm * (1.0 + g * (1.0 - sgm)))).astype(out)
        carry_k[...] = dkk[0:CHUNK]
        carry_v[...] = dvv[0:CHUNK]
        dsink_ref[...] += dsink

    rev = lambda i: (nb - 1 - i, 0)
    prev = lambda i: jnp.maximum(nb - 2 - i, 0)
    return _call(
        body, comm, name="swa_bwd", grid=(nb,),
        in_specs=[pl.BlockSpec(memory_space=pltpu.SMEM),
                  pl.BlockSpec((CHUNK, A_W), rev),
                  pl.BlockSpec((CHUNK, 512), lambda i: (prev(i), 2)),
                  pl.BlockSpec((CHUNK, 384), rev),
                  pl.BlockSpec((CHUNK, 384), lambda i: (prev(i), 0)),
                  pl.BlockSpec((CHUNK, D_ATT), lambda i: (nb - 1 - i, 1))],
        out_specs=[pl.BlockSpec((CHUNK, A_W), rev), pl.BlockSpec((8, 128), lambda i: (0, 0))],
        out_shape=[jax.ShapeDtypeStruct((L, A_W), _MXU), jax.ShapeDtypeStruct((8, 128), F32)],
        scratch_shapes=[pltpu.VMEM((CHUNK, 256), F32), pltpu.VMEM((CHUNK, 256), F32)],
        args=(sinks, proj_att, proj_att, tabs, tabs, dy))


def _head(y, x, target, w_out, ln_g, ln_b, *, tm):
    L = x.shape[0]
    nsteps = L // tm

    def body(y_ref, x_ref, t_ref, wo_ref, g_ref, b_ref, dr_ref, dy_ref, acc_ref):
        i = pl.program_id(0)

        @pl.when(i == 0)
        def _():
            acc_ref[...] = jnp.zeros_like(acc_ref)

        r = ALPHA * x_ref[...] + _mm(y_ref[...], wo_ref[...])
        mu = jnp.mean(r, axis=-1, keepdims=True)
        d = r - mu
        rstd = lax.rsqrt(jnp.mean(d * d, axis=-1, keepdims=True) + LN_EPS)
        xh = d * rstd
        gam = g_ref[0:1, :]
        e = xh * gam + b_ref[0:1, :] - t_ref[...]
        dout = e * (1.0 / D_MODEL)
        dxh = dout * gam
        dr = rstd * (dxh - jnp.mean(dxh, axis=-1, keepdims=True)
                     - xh * jnp.mean(dxh * xh, axis=-1, keepdims=True))
        dr_ref[...] = dr
        dy_ref[...] = _mm_nt(dr, wo_ref[...])
        acc_ref[...] += _rows8([_colsum(dout * xh), _colsum(dout), _colsum(e * e) * (0.5 / D_MODEL)])

        @pl.when(i == nsteps - 1)
        def _():
            acc = acc_ref[...]
            tot = jnp.sum(acc[2:3, :])
            rid = lax.broadcasted_iota(jnp.int32, (8, 1024), 0)
            acc_ref[...] = jnp.where(rid == 3, tot, acc)

    const = lambda shape: pl.BlockSpec(shape, lambda i: (0, 0))
    row = lambda w: pl.BlockSpec((tm, w), lambda i: (i, 0))
    return pl.pallas_call(
        body, name="head", grid=(nsteps,),
        in_specs=[row(2048), row(1024), row(1024), const((2048, 1024)), const((1, 1024)), const((1, 1024))],
        out_specs=[row(1024), row(2048), const((8, 1024))],
        out_shape=[jax.ShapeDtypeStruct((L, D_MODEL), F32), jax.ShapeDtypeStruct((L, 2048), F32),
                   jax.ShapeDtypeStruct((8, 1024), F32)],
        compiler_params=_params(("arbitrary",)),
    )(y, x, target, w_out, ln_g, ln_b)


def _gather_w_in(w_shard):
    R = w_shard.shape[0]
    halves = (pl.ds(0, R // 2), pl.ds(R // 2, R // 2))
    any_spec = pl.BlockSpec(memory_space=pl.ANY)

    def body(in_ref, out_ref, send_sems, recv_sems, local_sem):
        x, y, c = _position()

        def slot(p, half=None):
            s = out_ref.at[_index(*p)]
            return s if half is None else s.at[halves[half]]

        def same_core(p):
            return (p[0], p[1], c)

        def other_core(p):
            return (p[0], p[1], 1 - c)

        me, xn, yn, dg = (x, y), (1 - x, y), (x, 1 - y), (1 - x, 1 - y)

        def copy(k, dst, to, src=None):
            return _remote(dst if src is None else src, dst, send_sems.at[k], recv_sems.at[k], to)

        local = pltpu.make_async_copy(in_ref, slot(same_core(me)), local_sem)
        local.start()
        own = [copy(0, slot(same_core(me)), other_core(me), in_ref), copy(1, slot(same_core(me)), same_core(xn), in_ref),
               copy(2, slot(same_core(me)), same_core(yn), in_ref)]
        for cp in own:
            cp.start()
        copy(1, slot(same_core(xn)), same_core(xn)).wait_recv()
        passed = [copy(4, slot(same_core(xn), 1), same_core(yn)), copy(5, slot(same_core(xn)), other_core(me))]
        for cp in passed:
            cp.start()
        copy(2, slot(same_core(yn)), same_core(yn)).wait_recv()
        more = [copy(3, slot(same_core(yn), 0), same_core(xn)), copy(6, slot(same_core(yn)), other_core(me))]
        for cp in more:
            cp.start()
        passed += more
        for k, half in ((3, 0), (4, 1)):
            copy(k, slot(same_core(dg), half), same_core(xn)).wait_recv()
            fwd = copy(7 + half, slot(same_core(dg), half), other_core(me))
            fwd.start()
            passed.append(fwd)
        copy(0, slot(other_core(me)), other_core(me)).wait_recv()
        copy(5, slot(other_core(xn)), other_core(me)).wait_recv()
        copy(6, slot(other_core(yn)), other_core(me)).wait_recv()
        for half in (0, 1):
            copy(7 + half, slot(other_core(dg), half), other_core(me)).wait_recv()
        for cp in own + passed:
            cp.wait_send()
        local.wait()

    return pl.pallas_call(
        body, name="gather_w_in", in_specs=[any_spec], out_specs=any_spec,
        out_shape=jax.ShapeDtypeStruct((N_DEV,) + w_shard.shape, w_shard.dtype),
        scratch_shapes=[pltpu.SemaphoreType.DMA((9,)), pltpu.SemaphoreType.DMA((9,)), pltpu.SemaphoreType.DMA],
    )(w_shard)


def _input_gradient(d_ssd, d_att, w_ssd, w_att, dr, *, tm, comm=None):
    L = dr.shape[0]

    def body(ds_ref, da_ref, ws_ref, wa_ref, dr_ref, o_ref):
        o_ref[...] = ALPHA * dr_ref[...] + _mm_nt(ds_ref[...], ws_ref[...]) + _mm_nt(da_ref[...], wa_ref[...])

    row = lambda w: pl.BlockSpec((tm, w), lambda i: (i, 0))
    const = lambda shape: pl.BlockSpec(shape, lambda i: (0, 0))
    return _call(body, comm, name="dx", grid=(L // tm,),
                 in_specs=[row(S_W), row(A_W), const((D_MODEL, S_W)), const((D_MODEL, A_W)), row(D_MODEL)],
                 out_specs=[row(D_MODEL)], out_shape=[jax.ShapeDtypeStruct((L, D_MODEL), F32)],
                 scratch_shapes=[], args=(d_ssd, d_att, w_ssd, w_att, dr))


SHARD_COLS = D_IN_PROJ // N_DEV
SPLIT = N_SSD_REAL - 4 * SHARD_COLS
RELAYOUT_ROWS = 256


def _unpack_w_in(w_all):
    def body(g_ref, ws_ref, wa_ref):
        for j in range(4):
            ws_ref[:, SHARD_COLS * j:SHARD_COLS * (j + 1)] = g_ref[j]
        ws_ref[:, 4 * SHARD_COLS:N_SSD_REAL] = g_ref[4, :, 0:SPLIT]
        ws_ref[:, N_SSD_REAL:S_W] = jnp.zeros((RELAYOUT_ROWS, S_W - N_SSD_REAL), ws_ref.dtype)
        wa_ref[:, 0:SHARD_COLS - SPLIT] = g_ref[4, :, SPLIT:SHARD_COLS]
        for j in range(5, N_DEV):
            lo = SHARD_COLS * (j - 4) - SPLIT
            wa_ref[:, lo:lo + SHARD_COLS] = g_ref[j]

    return pl.pallas_call(
        body, name="unpack_w_in", grid=(D_MODEL // RELAYOUT_ROWS,),
        in_specs=[pl.BlockSpec((N_DEV, RELAYOUT_ROWS, SHARD_COLS), lambda i: (0, i, 0))],
        out_specs=[pl.BlockSpec((RELAYOUT_ROWS, S_W), lambda i: (i, 0)), pl.BlockSpec((RELAYOUT_ROWS, A_W), lambda i: (i, 0))],
        out_shape=[jax.ShapeDtypeStruct((D_MODEL, S_W), w_all.dtype), jax.ShapeDtypeStruct((D_MODEL, A_W), w_all.dtype)],
        compiler_params=_params(("arbitrary",)),
    )(w_all)


def _pack_dw_in(me1, dw_ssd, dw_att, half):
    def body(me_ref, *refs):
        if half == 0:
            ds_ref, p_ref, own_ref = refs
            me = me_ref[0]

            @pl.when(me >= 4)
            def _():
                own_ref[...] = jnp.zeros_like(own_ref)
        else:
            ds_ref, da_ref, p_ref = refs

        for j in range(4):
            if half == 0:
                pieces = [(0, ds_ref[:, SHARD_COLS * j:SHARD_COLS * (j + 1)])]
            elif j == 0:
                pieces = [(0, ds_ref[:, 4 * SHARD_COLS - S_DT:N_SSD_REAL - S_DT]), (SPLIT, da_ref[:, 0:SHARD_COLS - SPLIT])]
            else:
                lo = SHARD_COLS * j - SPLIT
                pieces = [(0, da_ref[:, lo:lo + SHARD_COLS])]
            for off, blk in pieces:
                p_ref[j, :, off:off + blk.shape[1]] = blk.astype(p_ref.dtype)
                if half == 0:
                    @pl.when(me == j)
                    def _(off=off, blk=blk):
                        own_ref[:, off:off + blk.shape[1]] = blk

    ins = [dw_ssd] if half == 0 else [dw_ssd, dw_att]
    row = lambda a: pl.BlockSpec((RELAYOUT_ROWS, a.shape[1]), lambda i: (i, 0))
    in_specs = [row(a) for a in ins]
    if half == 1:
        in_specs[0] = pl.BlockSpec((RELAYOUT_ROWS, S_W - S_DT), lambda i: (i, S_DT // (S_W - S_DT)))
    out_specs = [pl.BlockSpec((4, RELAYOUT_ROWS, SHARD_COLS), lambda i: (0, i, 0))]
    out_shape = [jax.ShapeDtypeStruct((4, D_MODEL, SHARD_COLS), BF16 if half == 0 else F32)]
    if half == 0:
        out_specs.append(pl.BlockSpec((RELAYOUT_ROWS, SHARD_COLS), lambda i: (i, 0)))
        out_shape.append(jax.ShapeDtypeStruct((D_MODEL, SHARD_COLS), F32))
    return pl.pallas_call(
        body, name="pack_dw_in_%d" % half, grid=(D_MODEL // RELAYOUT_ROWS,),
        in_specs=[pl.BlockSpec(memory_space=pltpu.SMEM)] + in_specs,
        out_specs=out_specs, out_shape=out_shape, compiler_params=_params(("arbitrary",)),
    )(me1, *ins)


def _pair_swap(stack):
    def body(in_ref, out_ref, send_sems, recv_sems):
        x, y, c = _position()
        cps = [_remote(in_ref.at[2 * oy + (1 - c)], out_ref.at[oy], send_sems.at[oy], recv_sems.at[oy], (x, y, 1 - c))
               for oy in range(2)]
        for cp in cps:
            cp.start()
        for cp in cps:
            cp.wait_recv()
        for cp in cps:
            cp.wait_send()

    any_spec = pl.BlockSpec(memory_space=pl.ANY)
    return pl.pallas_call(
        body, name="pair_swap", in_specs=[any_spec], out_specs=any_spec,
        out_shape=jax.ShapeDtypeStruct((2,) + stack.shape[1:], stack.dtype),
        scratch_shapes=[pltpu.SemaphoreType.DMA((2,)), pltpu.SemaphoreType.DMA((2,))],
    )(stack)


def _pair_sum(pos3, stack, swapped, own_lo):
    def body(pos_ref, a_ref, b_ref, lo_ref, chip_ref, own_ref):
        oy = pl.program_id(1)
        t = a_ref[0] + b_ref[0]
        chip_ref[0] = t.astype(chip_ref.dtype)

        @pl.when((pos_ref[0] == 0) & (oy == 0))
        def _():
            own_ref[...] = lo_ref[...]

        @pl.when((pos_ref[0] == 1) & (oy == pos_ref[1]))
        def _():
            own_ref[...] = t

    blk = (1, RELAYOUT_ROWS, SHARD_COLS)
    flat = pl.BlockSpec((RELAYOUT_ROWS, SHARD_COLS), lambda i, oy, pos: (i, 0))
    return pl.pallas_call(
        body, name="pair_sum",
        grid_spec=pltpu.PrefetchScalarGridSpec(
            num_scalar_prefetch=1, grid=(D_MODEL // RELAYOUT_ROWS, 2),
            in_specs=[pl.BlockSpec(blk, lambda i, oy, pos: (2 * oy + pos[2], i, 0)),
                      pl.BlockSpec(blk, lambda i, oy, pos: (oy, i, 0)), flat],
            out_specs=[pl.BlockSpec(blk, lambda i, oy, pos: (oy, i, 0)), flat]),
        out_shape=[jax.ShapeDtypeStruct((2, D_MODEL, SHARD_COLS), BF16), jax.ShapeDtypeStruct((D_MODEL, SHARD_COLS), F32)],
        compiler_params=_params(("arbitrary", "arbitrary")),
    )(pos3, stack, swapped, own_lo)


def _adamw_math(w, g, m, v):
    m = ADAM_B1 * m + (1.0 - ADAM_B1) * g
    v = ADAM_B2 * v + (1.0 - ADAM_B2) * (g * g)
    m_hat = m / (1.0 - ADAM_B1 ** ADAM_STEP)
    v_hat = v / (1.0 - ADAM_B2 ** ADAM_STEP)
    delta = -ADAM_LR * (m_hat / (jnp.sqrt(v_hat) + ADAM_EPS) + ADAM_WD * w)
    return delta, m, v


def _adamw_shard(n_recv, g_own, recv, w, m, v, *, rows, name):
    R, C = g_own.shape

    def body(n_ref, g_ref, r_ref, w_ref, m_ref, v_ref, go_ref, d_ref, mo_ref, vo_ref):
        g = g_ref[...]
        for k in range(N_DEV - 1):
            g = g + jnp.where(k < n_ref[0], r_ref[k].astype(F32), 0.0)
        d, mn, vn = _adamw_math(w_ref[...], g, m_ref[...], v_ref[...])
        go_ref[...] = g
        d_ref[...] = d
        mo_ref[...] = mn
        vo_ref[...] = vn

    blk = pl.BlockSpec((rows, C), lambda i: (i, 0))
    return pl.pallas_call(
        body, name=name, grid=(R // rows,),
        in_specs=[pl.BlockSpec(memory_space=pltpu.SMEM), blk,
                  pl.BlockSpec((N_DEV - 1, rows, C), lambda i: (0, i, 0)), blk, blk, blk],
        out_specs=[blk] * 4, out_shape=[jax.ShapeDtypeStruct((R, C), F32)] * 4,
        compiler_params=_params(("arbitrary",)),
    )(n_recv, g_own, recv, w, m, v)


def _minor_rows_view(a):
    return jnp.transpose(a, (2, 0, 1)).reshape(SHARD_COLS * 8, 128)


def _from_minor_rows_view(v):
    return jnp.transpose(v.reshape(SHARD_COLS, 8, 128), (1, 2, 0)).reshape(1, D_MODEL, SHARD_COLS)


def _adamw_w_in(n_recv, g_own, recv, w, m, v):
    C = SHARD_COLS
    pad = -C % 128

    def body(n_ref, g_ref, r_ref, w_ref, m_ref, v_ref, go_ref, d_ref, mo_ref, vo_ref):
        for q in range(D_MODEL // 128):
            band = pl.ds(q * 128, 128)
            g = g_ref[band, :]
            for k in range(N_DEV - 1):
                g = g + jnp.where(k < n_ref[0], r_ref[k, band, :].astype(F32), 0.0)
            g = jnp.pad(g, ((0, 0), (0, pad))).T[0:C]
            rows = pl.ds(q, C, stride=8)
            d, mn, vn = _adamw_math(w_ref[rows, :], g, m_ref[rows, :], v_ref[rows, :])
            go_ref[rows, :] = g
            d_ref[rows, :] = d
            mo_ref[rows, :] = mn
            vo_ref[rows, :] = vn

    return pl.pallas_call(
        body, name="adamw_w_in", out_shape=[jax.ShapeDtypeStruct(w.shape, F32)] * 4,
        in_specs=[pl.BlockSpec(memory_space=pltpu.SMEM)] + [pl.BlockSpec(memory_space=pltpu.VMEM)] * 5,
        out_specs=[pl.BlockSpec(memory_space=pltpu.VMEM)] * 4,
        compiler_params=_params(),
    )(n_recv, g_own, recv, w, m, v)


SMALL = ("conv_b", "dt_bias", "a_log", "d_skip", "ssd_norm_w", "attn_sinks", "ln_g", "ln_b")


def _adamw_small(gathered, params):
    n_p = len(SMALL)

    def body(*refs):
        acc = []
        for r in refs[:5]:
            t = r[0]
            for k in range(1, N_DEV):
                t = t + r[k]
            acc.append(t)
        head, conv, norm, scal, sink = acc
        grads = dict(conv_b=conv[4:5, :], dt_bias=scal[0:1, 0:N_HEADS], a_log=scal[1:2, 0:N_HEADS],
                     d_skip=scal[2:3, 0:N_HEADS], ssd_norm_w=norm[0:1, :], attn_sinks=sink[0:1, 0:N_HEADS],
                     ln_g=head[0:1, :], ln_b=head[1:2, :])
        wmv = refs[5:5 + 3 * n_p]
        outs = refs[5 + 3 * n_p:]
        outs[0][...] = head[3:4, 0:1]
        outs[1][...] = conv[0:4, :]
        for i, name in enumerate(SMALL):
            w_ref, m_ref, v_ref = wmv[3 * i:3 * i + 3]
            g = grads[name]
            d, mn, vn = _adamw_math(w_ref[...], g, m_ref[...], v_ref[...])
            for o_ref, val in zip(outs[2 + 4 * i:6 + 4 * i], (g, d, mn, vn)):
                o_ref[...] = val

    flat = [a for name in SMALL for a in params[name]]
    out_shape = [jax.ShapeDtypeStruct((1, 1), F32), jax.ShapeDtypeStruct((4, D_XBC), F32)]
    for name in SMALL:
        out_shape += [jax.ShapeDtypeStruct(params[name][0].shape, F32)] * 4
    res = pl.pallas_call(body, name="adamw_small", out_shape=out_shape, compiler_params=_params())(*gathered, *flat)
    return res[0], res[1], {name: res[2 + 4 * i:6 + 4 * i] for i, name in enumerate(SMALL)}


def _adamw_plain(g, w, m, v):
    def body(g_ref, w_ref, m_ref, v_ref, d_ref, mo_ref, vo_ref):
        d, mn, vn = _adamw_math(w_ref[...], g_ref[...], m_ref[...], v_ref[...])
        d_ref[...] = d
        mo_ref[...] = mn
        vo_ref[...] = vn

    return pl.pallas_call(
        body, name="adamw_conv_w", out_shape=[jax.ShapeDtypeStruct(w.shape, F32)] * 3,
        compiler_params=_params(),
    )(g, w, m, v)


def _lane_pattern(fn):
    return np.asarray([fn(l % HEAD_DIM) for l in range(128)], np.float32)


ROPE_INV = _lane_pattern(lambda r: ROPE_THETA ** (-2.0 * (r % 8) / ROPE_DIM) if r < ROPE_DIM else 0.0)
ROPE_SIN_A = _lane_pattern(lambda r: 1.0 if 8 <= r < ROPE_DIM else 0.0)
ROPE_SIN_B = _lane_pattern(lambda r: -1.0 if r < 8 else 0.0)


def _rope_tables(positions):
    ang = positions.astype(F32)[:, None] * ROPE_INV[None, :]
    sn = jnp.sin(ang)
    return jnp.concatenate([jnp.cos(ang), sn * ROPE_SIN_A[None, :], sn * ROPE_SIN_B[None, :]], axis=1)


def _expansion():
    E = np.arange(1024)[None, :] // HEAD_DIM == np.arange(128)[:, None]
    return jnp.asarray(E, BF16), jnp.asarray(E.T, BF16)


def _ssd_args(conv_w, conv_b, dt_bias, a_log, d_skip, norm_w, E):
    return (conv_w, conv_b, dt_bias.reshape(-1), a_log.reshape(-1), d_skip.reshape(-1), norm_w, E)


def kernel(x, positions, w_in, conv_w, conv_b, dt_bias, a_log, d_skip, ssd_norm_w, attn_sinks, w_out, ln_g, ln_b, loss_target, m_w_in, m_conv_w, m_conv_b, m_dt_bias, m_a_log, m_d_skip, m_ssd_norm_w, m_attn_sinks, m_w_out, m_ln_g, m_ln_b, v_w_in, v_conv_w, v_conv_b, v_dt_bias, v_a_log, v_d_skip, v_ssd_norm_w, v_attn_sinks, v_w_out, v_ln_g, v_ln_b):
    me = _index(*_position())
    me1 = me.reshape(1).astype(jnp.int32)
    x0, target = x[0], loss_target[0]
    bf16_shard = lambda shape: jax.ShapeDtypeStruct(shape, BF16)
    E, ET = _expansion()
    tabs = _rope_tables(positions[0])
    sinks = attn_sinks.reshape(-1)

    w_ssd, w_att = _unpack_w_in(_gather_w_in(w_in[0].astype(BF16)))
    gather_conv_w = _Hosted([conv_w[0]], [jax.ShapeDtypeStruct((N_DEV,) + conv_w.shape[1:], F32)],
                            [_Flow("gather", 0, 0)])

    proj_ssd, xb, conv_w_all = _matmul(x0, w_ssd, tm=512, tn=S_W, name="in_proj_ssd", emit_a=True,
                                       comm=gather_conv_w)
    conv_w_f = jnp.transpose(conv_w_all, (1, 0, 2)).reshape(4, D_XBC)
    ssd_args = _ssd_args(conv_w_f, conv_b, dt_bias, a_log, d_skip, ssd_norm_w, E)
    proj_att = _matmul(xb, w_att, tm=512, tn=A_W, name="in_proj_att")
    gather_w_out = _Hosted([w_out[0].astype(BF16)], [bf16_shard((N_DEV, 256, D_MODEL))], [_Flow("gather", 0, 0)])
    y, ypre, hprev, pre, w_out_all = _mixer_forward(proj_ssd, proj_att, tabs, sinks, *ssd_args, comm=gather_w_out)
    w_out_f = w_out_all.reshape(2 * D_MODEL, D_MODEL)
    dr, dy, acc_head = _head(y, x0, target, w_out_f, ln_g, ln_b, tm=512)

    dw_out, dw_out_bf16 = _matmul_tn(y, dr, tl=1024, tn=D_MODEL, name="dw_out", emit_bf16=True)
    own_out = lax.dynamic_index_in_dim(dw_out.reshape(N_DEV, 256, D_MODEL), me, axis=0, keepdims=False)
    send_out = _Hosted([dw_out_bf16.reshape(N_DEV, 256, D_MODEL)], [bf16_shard((N_DEV - 1, 256, D_MODEL))],
                       [_Flow("exchange", 0, 0)])
    d_ssd, acc_cw, acc_w, acc_s, recv_out = _ssd_backward(proj_ssd, hprev, ypre, pre, dy, *ssd_args, ET, comm=send_out)
    dw_ssd = _matmul_tn(xb, d_ssd, tl=2048, tn=S_W // 2, name="dw_in_ssd")
    parts_lo, own_lo = _pack_dw_in(me1, dw_ssd, None, 0)
    recv_shape = bf16_shard((N_DEV - 1, D_MODEL, SHARD_COLS))
    send_lo = _Hosted([parts_lo], [recv_shape], [_Flow("exchange", 0, 0, target_x=0, target_c=0)])
    d_att, dsink, recv_in = _swa_backward(proj_att, tabs, sinks, dy, comm=send_lo)
    dw_att = _matmul_tn(xb, d_att, tl=2048, tn=A_W // 2, name="dw_in_att")
    (stack_hi,) = _pack_dw_in(me1, dw_ssd, dw_att, 1)
    pos3 = jnp.stack(_position()).astype(jnp.int32)
    chip_hi, own_in = _pair_sum(pos3, stack_hi, _pair_swap(stack_hi), own_lo)
    accs = [acc_head, acc_cw, acc_w, acc_s, dsink]
    send_hi = _Hosted([chip_hi, recv_in, parts_lo] + accs,
                      [recv_shape] + [jax.ShapeDtypeStruct((N_DEV,) + a.shape, F32) for a in accs],
                      [_Flow("chip_exchange", 0, 0, target_x=1), _Flow("exchange", 2, 0, target_x=0, target_c=1)]
                      + [_Flow("gather", 3 + i, 1 + i) for i in range(5)], aliases={1: 0})
    dx, recv_in, *gathered = _input_gradient(d_ssd, d_att, w_ssd, w_att, dr, tm=256, comm=send_hi)
    n_recv_in = jnp.where(me < 4, N_DEV - 1, 3).reshape(1).astype(jnp.int32)
    n_recv_out = jnp.full((1,), N_DEV - 1, jnp.int32)

    g_in, d_in, nm_in, nv_in = [_from_minor_rows_view(r) for r in _adamw_w_in(
        n_recv_in, own_in, recv_in, _minor_rows_view(w_in), _minor_rows_view(m_w_in), _minor_rows_view(v_w_in))]
    g_out, d_out, nm_out, nv_out = _adamw_shard(n_recv_out, own_out, recv_out, w_out[0], m_w_out[0], v_w_out[0],
                                                rows=256, name="adamw_w_out")
    loss, g_conv_w, small = _adamw_small(gathered, dict(
        conv_b=(conv_b, m_conv_b, v_conv_b), dt_bias=(dt_bias, m_dt_bias, v_dt_bias), a_log=(a_log, m_a_log, v_a_log),
        d_skip=(d_skip, m_d_skip, v_d_skip), ssd_norm_w=(ssd_norm_w, m_ssd_norm_w, v_ssd_norm_w),
        attn_sinks=(attn_sinks, m_attn_sinks, v_attn_sinks), ln_g=(ln_g, m_ln_g, v_ln_g), ln_b=(ln_b, m_ln_b, v_ln_b)))
    g_cw = lax.dynamic_slice_in_dim(g_conv_w, me * (D_XBC // N_DEV), D_XBC // N_DEV, axis=1)
    d_cw, nm_cw, nv_cw = _adamw_plain(g_cw, conv_w[0], m_conv_w[0], v_conv_w[0])

    def leaves(i, big_in, cw, big_out):
        mid = [small[k][i] for k in ("conv_b", "dt_bias", "a_log", "d_skip", "ssd_norm_w", "attn_sinks")]
        return [big_in, cw[None]] + mid + [big_out[None], small["ln_g"][i], small["ln_b"][i]]

    return (loss.reshape(()), dx[None], *leaves(0, g_in, g_cw, g_out), *leaves(1, d_in, d_cw, d_out),
            *leaves(2, nm_in, nm_cw, nm_out), *leaves(3, nv_in, nv_cw, nv_out))
```

```python
import jax
import jax.numpy as jnp
from jax import lax
from jax.experimental import pallas as pl
from jax.experimental.pallas import tpu as pltpu
import numpy as np

F32 = jnp.float32
BF16 = jnp.bfloat16
_MXU = jnp.bfloat16

N_DEV = 8
D_MODEL = 1024
D_SSD = 1024
D_ATT = 1024
HEAD_DIM = 64
N_HEADS = 16
SSD_GROUPS = 2
KV_HEADS = 4
CHUNK = 128
D_XBC = 1536
D_IN_PROJ = 5136
ROPE_DIM = 16
ROPE_THETA = 500000.0
ALPHA = (2.0 * 1) ** 0.25
LN_EPS = 1e-5
RMS_EPS = 1e-5
ATT_SCALE = HEAD_DIM ** -0.5
NEG = -1e30

S_Z, S_XS, S_B, S_C, S_DT, S_W = 0, 1024, 2048, 2304, 2560, 2816
N_SSD_REAL = 2576
A_Q, A_K, A_V, A_G, A_W = 0, 1024, 1280, 1536, 2560

ADAM_LR = 0.001
ADAM_B1 = 0.9
ADAM_B2 = 0.999
ADAM_EPS = 1e-08
ADAM_WD = 0.01
ADAM_STEP = 10

VMEM_LIMIT = 48 * 1024 * 1024
MESH = pl.DeviceIdType.MESH


def _params(sem=None):
    return pltpu.CompilerParams(dimension_semantics=sem, vmem_limit_bytes=VMEM_LIMIT)


def _mm(a, b):
    return jnp.dot(a.astype(_MXU), b.astype(_MXU), preferred_element_type=F32)


def _mm_nt(a, b):
    return lax.dot_general(a.astype(_MXU), b.astype(_MXU), (((1,), (1,)), ((), ())),
                           preferred_element_type=F32)


def _mm_tn(a, b):
    return lax.dot_general(a.astype(_MXU), b.astype(_MXU), (((0,), (0,)), ((), ())),
                           preferred_element_type=F32)


def _split3(v):
    hi = v.astype(BF16)
    r = v - hi.astype(F32)
    mid = r.astype(BF16)
    lo = (r - mid.astype(F32)).astype(BF16)
    return hi, mid, lo


def _mm_exact_r(v, p01):
    hi, mid, lo = _split3(v)
    d = lambda a: jnp.dot(a, p01, preferred_element_type=F32)
    return d(hi) + d(mid) + d(lo)


def _mm_exact_l(p01, v):
    hi, mid, lo = _split3(v)
    d = lambda a: jnp.dot(p01, a, preferred_element_type=F32)
    return d(hi) + d(mid) + d(lo)


def _mm_2pass_r(v, p01):
    hi = v.astype(BF16)
    lo = (v - hi.astype(F32)).astype(BF16)
    return jnp.dot(hi, p01, preferred_element_type=F32) + jnp.dot(lo, p01, preferred_element_type=F32)


def _sigmoid(x):
    return 1.0 / (1.0 + jnp.exp(-x))


def _softplus(x):
    e = jnp.exp(-jnp.abs(x))
    u = 1.0 + e
    log1p = jnp.where(u == 1.0, e, jnp.log(u) * (e / (u - 1.0)))
    return jnp.maximum(x, 0.0) + log1p


def _rows8(rows):
    n = rows[0].shape[1]
    rid = lax.broadcasted_iota(jnp.int32, (8, n), 0)
    out = jnp.zeros((8, n), F32)
    for k, r in enumerate(rows):
        out = out + jnp.where(rid == k, r, 0.0)
    return out


def _colsum(a):
    return jnp.sum(a, axis=0, keepdims=True)


def _matmul(a, b, *, tm, tn, name, emit_a=False, comm=None):
    M, K = a.shape
    N = b.shape[1]

    def body(a_ref, b_ref, o_ref, *rest):
        am = a_ref[...].astype(_MXU)
        o_ref[...] = jnp.dot(am, b_ref[...].astype(_MXU), preferred_element_type=F32)
        if emit_a:
            rest[0][...] = am

    out_specs = [pl.BlockSpec((tm, tn), lambda i, j: (i, j))]
    out_shape = [jax.ShapeDtypeStruct((M, N), F32)]
    if emit_a:
        out_specs.append(pl.BlockSpec((tm, K), lambda i, j: (i, 0)))
        out_shape.append(jax.ShapeDtypeStruct((M, K), _MXU))
    res = _call(
        body, comm, name=name, grid=(M // tm, N // tn),
        in_specs=[pl.BlockSpec((tm, K), lambda i, j: (i, 0)), pl.BlockSpec((K, tn), lambda i, j: (0, j))],
        out_specs=out_specs, out_shape=out_shape, scratch_shapes=[], args=(a, b))
    return res if (emit_a or comm is not None) else res[0]


def _matmul_tn(a, g, *, tl, tn, name, emit_bf16=False):
    L, M = a.shape
    N = g.shape[1]
    last = L // tl - 1

    def body(a_ref, g_ref, o_ref, *rest):
        @pl.when(pl.program_id(1) == 0)
        def _():
            o_ref[...] = jnp.zeros_like(o_ref)

        o_ref[...] += _mm_tn(a_ref[...], g_ref[...])
        if emit_bf16:
            @pl.when(pl.program_id(1) == last)
            def _():
                rest[0][...] = o_ref[...].astype(BF16)

    spec = pl.BlockSpec((M, tn), lambda j, l: (0, j))
    res = pl.pallas_call(
        body, name=name, grid=(N // tn, L // tl),
        in_specs=[pl.BlockSpec((tl, M), lambda j, l: (l, 0)), pl.BlockSpec((tl, tn), lambda j, l: (l, j))],
        out_specs=[spec, spec] if emit_bf16 else [spec],
        out_shape=[jax.ShapeDtypeStruct((M, N), F32)] + ([jax.ShapeDtypeStruct((M, N), BF16)] if emit_bf16 else []),
        compiler_params=_params(("arbitrary", "arbitrary")),
    )(a, g)
    return res if emit_bf16 else res[0]


def _position():
    return lax.axis_index("x"), lax.axis_index("y"), lax.axis_index("c")


def _index(px, py, pc):
    return 4 * px + 2 * py + pc


def _flip(pos, k):
    x, y, c = pos
    return ((1 - x) if (k >> 2) & 1 else x, (1 - y) if (k >> 1) & 1 else y, (1 - c) if k & 1 else c)


def _when(cond, fn):
    if cond is True:
        fn()
    else:
        pl.when(cond)(fn)


def _remote(src, dst, send_sem, recv_sem, peer):
    return pltpu.make_async_remote_copy(src_ref=src, dst_ref=dst, send_sem=send_sem, recv_sem=recv_sem,
                                        device_id=peer, device_id_type=MESH)


class _Flow:
    def __init__(self, kind, operand, result, target_x=None, target_c=None):
        self.kind, self.operand, self.result, self.target_x, self.target_c = kind, operand, result, target_x, target_c

    def owns(self, pos):
        if self.target_x is None:
            return True
        cond = pos[0] == self.target_x
        return cond if self.target_c is None else cond & (pos[2] == self.target_c)


class _Hosted:
    def __init__(self, operands, out_shapes, flows, aliases=None):
        self.operands, self.out_shapes, self.flows = operands, out_shapes, flows
        self.aliases = aliases or {}

    def plan(self, ins, outs, send_sems, recv_sems, local_sems):
        me = _position()
        mi = _index(*me)
        sends, recvs, locals_ = [], [], []
        for row, f in enumerate(self.flows):
            src, dst = ins[f.operand], outs[f.result]
            for k in range(1, N_DEV):
                peer = _flip(me, k)
                sems = (send_sems.at[row, k - 1], recv_sems.at[row, k - 1])
                if f.kind == "exchange":
                    owner = _index(*peer) if f.target_x is None else 2 * peer[1] + peer[2]
                    cp = _remote(src.at[owner], dst.at[k - 1], *sems, peer)
                    sends.append((f.owns(peer), cp))
                    recvs.append((f.owns(me), cp))
                elif f.kind == "chip_exchange":
                    if k & 1:
                        continue
                    cp = _remote(src.at[peer[1]], dst.at[k // 2 - 1], *sems, peer)
                    sends.append((peer[0] == f.target_x, cp))
                    recvs.append((me[0] == f.target_x, cp))
                else:
                    sends.append((True, _remote(src, dst.at[mi], *sems, peer)))
                    recvs.append((True, _remote(src, dst.at[_index(*peer)], *sems, peer)))
            if f.kind == "gather":
                locals_.append(pltpu.make_async_copy(src, dst.at[mi], local_sems.at[row]))

        def start():
            for cp in locals_:
                cp.start()
            for cond, cp in sends:
                _when(cond, cp.start)

        def wait():
            for cond, cp in recvs:
                _when(cond, cp.wait_recv)
            for cond, cp in sends:
                _when(cond, cp.wait_send)
            for cp in locals_:
                cp.wait()

        return start, wait


def _call(body, comm, *, name, grid, in_specs, out_specs, out_shape, scratch_shapes, args, aliases=None):
    io_alias = dict(aliases or {})
    semantics = ("arbitrary",) * len(grid)
    if comm is None:
        return pl.pallas_call(body, name=name, grid=grid, in_specs=in_specs, out_specs=out_specs, out_shape=out_shape,
                              scratch_shapes=scratch_shapes, input_output_aliases=io_alias,
                              compiler_params=_params(semantics))(*args)
    n_in, n_out, n_scr = len(args), len(out_shape), len(scratch_shapes)
    c_in, c_out, rows = len(comm.operands), len(comm.out_shapes), len(comm.flows)

    def hosted(*refs):
        ins, refs = refs[:n_in], refs[n_in:]
        cins, refs = refs[:c_in], refs[c_in:]
        outs, refs = refs[:n_out], refs[n_out:]
        couts, refs = refs[:c_out], refs[c_out:]
        scr, (send_sems, recv_sems, local_sems) = refs[:n_scr], refs[n_scr:]
        start, wait = comm.plan(cins, couts, send_sems, recv_sems, local_sems)
        ids = [pl.program_id(d) for d in range(len(grid))]
        first, last = ids[0] == 0, ids[0] == grid[0] - 1
        for d in range(1, len(grid)):
            first, last = first & (ids[d] == 0), last & (ids[d] == grid[d] - 1)
        pl.when(first)(start)
        body(*ins, *outs, *scr)
        pl.when(last)(wait)

    for ci, co in comm.aliases.items():
        io_alias[n_in + ci] = n_out + co
    any_spec = pl.BlockSpec(memory_space=pl.ANY)
    sems = [pltpu.SemaphoreType.DMA((rows, N_DEV - 1)), pltpu.SemaphoreType.DMA((rows, N_DEV - 1)),
            pltpu.SemaphoreType.DMA((rows,))]
    return pl.pallas_call(
        hosted, name=name, grid=grid, in_specs=list(in_specs) + [any_spec] * c_in,
        out_specs=list(out_specs) + [any_spec] * c_out, out_shape=list(out_shape) + list(comm.out_shapes),
        scratch_shapes=list(scratch_shapes) + sems, input_output_aliases=io_alias,
        compiler_params=_params(semantics))(*args, *comm.operands)


def _head_row(ref, width, rep):
    hid = lax.broadcasted_iota(jnp.int32, (1, width), 1) // rep
    row = jnp.zeros((1, width), F32)
    for h in range(N_HEADS):
        row = jnp.where(hid == h, ref[h], row)
    return row


def _rows_from_above(u_b, s, ext_scr, row, col):
    down = (row - col == s).astype(_MXU)
    return jnp.concatenate([ext_scr[8 - s:16 - s, :], jnp.dot(down, u_b, preferred_element_type=F32)[8:128]], axis=0)


def _ssd_recompute(first, p_ref, halo_ref, cw_ref, cb_ref, dtb_ref, alog_ref, e_ref, ext_scr, pre=None):
    row = lax.broadcasted_iota(jnp.int32, (128, 128), 0)
    col = lax.broadcasted_iota(jnp.int32, (128, 128), 1)
    ext_scr[0:8, :] = jnp.where(first, 0.0, halo_ref[:, S_XS:S_DT])
    if pre is not None:
        ext_scr[8:16, :] = p_ref[0:8, S_XS:S_DT]
    else:
        ext_scr[8:136, :] = p_ref[:, S_XS:S_DT]
        cw = cw_ref[...]
        pre = (cb_ref[0:1, :] + cw[3:4, :] * ext_scr[8:136, :] + cw[2:3, :] * ext_scr[7:135, :]
               + cw[1:2, :] * ext_scr[6:134, :] + cw[0:1, :] * ext_scr[5:133, :])
    sg = _sigmoid(pre)
    act = pre * sg
    lane = lax.broadcasted_iota(jnp.int32, (1, 128), 1)
    A = jnp.where(lane < N_HEADS, -jnp.exp(_head_row(alog_ref, 128, 1)), 0.0)
    raw = p_ref[:, S_DT:S_DT + 128] + _head_row(dtb_ref, 128, 1)
    dt = _softplus(raw)
    dA = dt * A
    tril = (row >= col).astype(BF16)
    acs = _mm_exact_l(tril, dA)
    last = acs[127:128, :]
    ds = jnp.exp(last - acs)
    eo = jnp.exp(acs)
    E = e_ref[...]
    ex = _mm_2pass_r(jnp.concatenate([dt, ds, eo], axis=0), E)
    dt_e, ds_e, eo_e = ex[0:128], ex[128:256], ex[256:384]
    xs_c = act[:, 0:1024]
    X = xs_c * dt_e
    return dict(pre=pre, sg=sg, xs_c=xs_c, Bc=act[:, 1024:1280], Cc=act[:, 1280:1536], A=A, raw=raw, dt=dt,
                acs=acs, acsT=acs.T, eo_e=eo_e, ds_e=ds_e, dt_e=dt_e, cd_e=eo_e[127:128, :],
                X=X, Xd=X * ds_e, row=row, col=col)


def _split_halves(t):
    lo = _lo_half(CHUNK)
    return jnp.concatenate([jnp.where(lo, t, 0.0), jnp.where(lo, 0.0, t)], axis=0)


def _ssd_core(R, hprev):
    causal = R["row"] >= R["col"]
    acs, acsT, X = R["acs"], R["acsT"], R["X"]
    ydiag, yoff, snew = [], [], []
    for g in range(SSD_GROUPS):
        Bg = R["Bc"][:, g * 128:(g + 1) * 128]
        Cg = R["Cc"][:, g * 128:(g + 1) * 128]
        cols = slice(g * 512, (g + 1) * 512)
        CB = _mm_nt(Cg, Bg)
        snew.append(_mm_tn(Bg, R["Xd"][:, cols]))
        yoff.append(_mm(Cg, hprev[:, cols]))
        for j in range(4):
            h0 = g * 8 + 2 * j
            ms = [CB * jnp.exp(jnp.where(causal, acs[:, h:h + 1] - acsT[h:h + 1, :], NEG)) for h in (h0, h0 + 1)]
            ydiag.append(_mm(jnp.concatenate(ms, axis=1), _split_halves(X[:, h0 * HEAD_DIM:h0 * HEAD_DIM + 128])))
    Y = jnp.concatenate(ydiag, axis=1) + jnp.concatenate(yoff, axis=1) * R["eo_e"]
    return Y, jnp.concatenate(snew, axis=1)


def _ssd_forward_step(p_ref, halo_ref, cw_ref, cb_ref, dtb_ref, alog_ref, dsk_ref, nw_ref, e_ref,
                      y_ref, ypre_ref, hprev_ref, pre_ref, h_scr, ext_scr):
    c = pl.program_id(0)
    first = c == 0

    @pl.when(first)
    def _():
        h_scr[...] = jnp.zeros_like(h_scr)

    R = _ssd_recompute(first, p_ref, halo_ref, cw_ref, cb_ref, dtb_ref, alog_ref, e_ref, ext_scr)
    hprev = h_scr[...]
    hprev_ref[...] = hprev
    pre_ref[...] = R["pre"]
    Y, snew = _ssd_core(R, hprev)
    h_scr[...] = hprev * R["cd_e"] + snew
    Y = Y + _head_row(dsk_ref, D_SSD, HEAD_DIM) * R["xs_c"]
    ypre_ref[...] = Y
    z = p_ref[:, S_Z:S_Z + 1024]
    yf = Y * (z * _sigmoid(z))
    outs = []
    for g in range(SSD_GROUPS):
        yg = yf[:, g * 512:(g + 1) * 512]
        r = lax.rsqrt(jnp.mean(yg * yg, axis=-1, keepdims=True) + RMS_EPS)
        outs.append(yg * r)
    y_ref[:, 0:D_SSD] = (jnp.concatenate(outs, axis=1) * nw_ref[0:1, :]).astype(y_ref.dtype)


def _ssd_backward(proj_ssd, hprev_all, ypre, pre, dy, conv_w, conv_b, dt_bias, a_log, d_skip, norm_w, E, ET, comm=None):
    L = proj_ssd.shape[0]
    nc = L // CHUNK

    def body(p_ref, halo_ref, hprev_ref, ypre_ref, pre_ref, dy_ref, cw_ref, cb_ref, dtb_ref, alog_ref, dsk_ref, nw_ref, e_ref,
             et_ref, dp_ref, acc_cw_ref, acc_w_ref, acc_s_ref, dh_scr, ext_scr, ext2_scr, nxt_scr):
        i = pl.program_id(0)
        c = nc - 1 - i
        first = c == 0

        @pl.when(i == 0)
        def _():
            dh_scr[...] = jnp.zeros_like(dh_scr)
            nxt_scr[...] = jnp.zeros_like(nxt_scr)
            acc_cw_ref[...] = jnp.zeros_like(acc_cw_ref)
            acc_w_ref[...] = jnp.zeros_like(acc_w_ref)
            acc_s_ref[...] = jnp.zeros_like(acc_s_ref)

        R = _ssd_recompute(first, p_ref, halo_ref, cw_ref, cb_ref, dtb_ref, alog_ref, e_ref, ext_scr, pre_ref[...])
        hprev = hprev_ref[...]
        xs_c, X, Xd = R["xs_c"], R["X"], R["Xd"]
        acs, acsT = R["acs"], R["acsT"]
        ET = et_ref[...]
        dsk = _head_row(dsk_ref, D_SSD, HEAD_DIM)
        Y = ypre_ref[...]

        z = p_ref[:, S_Z:S_Z + 1024]
        sz = _sigmoid(z)
        silz = z * sz
        yf = Y * silz
        dyv = dy_ref[...]
        nw = nw_ref[0:1, :]
        dyf_parts, dnw_parts = [], []
        for g in range(SSD_GROUPS):
            cols = slice(g * 512, (g + 1) * 512)
            yg = yf[:, cols]
            r = lax.rsqrt(jnp.mean(yg * yg, axis=-1, keepdims=True) + RMS_EPS)
            yn = yg * r
            dyn = dyv[:, cols] * nw[:, cols]
            dnw_parts.append(_colsum(dyv[:, cols] * yn))
            dyf_parts.append(r * (dyn - yn * jnp.mean(dyn * yn, axis=-1, keepdims=True)))
        dyf = jnp.concatenate(dyf_parts, axis=1)
        dY = dyf * silz
        dz = dyf * Y * (sz * (1.0 + z * (1.0 - sz)))

        dhn = dh_scr[...]
        dYo = dY * R["eo_e"]
        causal = R["row"] >= R["col"]
        dacs = jnp.zeros((128, 128), F32)
        dacs_t = jnp.zeros((128, 128), F32)
        dxdiag, dxd, dhprev, dBs, dCs, yoff = [], [], [], [], [], []
        for g in range(SSD_GROUPS):
            Bg = R["Bc"][:, g * 128:(g + 1) * 128]
            Cg = R["Cc"][:, g * 128:(g + 1) * 128]
            cols = slice(g * 512, (g + 1) * 512)
            CB = _mm_nt(Cg, Bg)
            dCB = jnp.zeros((128, 128), F32)
            for j in range(4):
                h0 = g * 8 + 2 * j
                pc = slice(h0 * HEAD_DIM, h0 * HEAD_DIM + 128)
                dYst = _split_halves(dY[:, pc])
                dMst = _mm_nt(dYst, X[:, pc])
                mts = []
                for a, h in enumerate((h0, h0 + 1)):
                    acol = acs[:, h:h + 1]
                    arow = acsT[h:h + 1, :]
                    Lm = jnp.exp(jnp.where(causal, acol - arow, NEG))
                    M = CB * Lm
                    dM = dMst[a * 128:(a + 1) * 128]
                    dCB = dCB + dM * Lm
                    G = dM * M
                    dacs = dacs + jnp.where(R["col"] == h, jnp.sum(G, axis=1, keepdims=True), 0.0)
                    dacs_t = dacs_t + jnp.where(R["row"] == h, jnp.sum(G, axis=0, keepdims=True), 0.0)
                    mts.append(M.T)
                dxdiag.append(_mm(jnp.concatenate(mts, axis=1), dYst))
            dS = dhn[:, cols]
            dxd.append(_mm(Bg, dS))
            yoff.append(_mm(Cg, hprev[:, cols]))
            dhprev.append(_mm_tn(Cg, dYo[:, cols]))
            dCs.append(_mm_nt(dYo[:, cols], hprev[:, cols]) + _mm(dCB, Bg))
            dBs.append(_mm_tn(dCB, Cg) + _mm_nt(Xd[:, cols], dS))
        Yoff = jnp.concatenate(yoff, axis=1) * R["eo_e"]
        dXd = jnp.concatenate(dxd, axis=1)
        dX = jnp.concatenate(dxdiag, axis=1) + dXd * R["ds_e"]
        t_state = dXd * Xd
        dacs = dacs + _mm_2pass_r(dY * Yoff - t_state, ET) - dacs_t.T
        v_last = _colsum(t_state + dhn * hprev * R["cd_e"])
        dlast = _mm_exact_r(jnp.broadcast_to(v_last, (8, 1024)), ET)[0:1, :]
        dacs = dacs + jnp.where(R["row"] == 127, dlast, 0.0)
        triu = (R["col"] >= R["row"]).astype(BF16)
        da = _mm_exact_l(triu, dacs)
        ddt = da * R["A"] + _mm(dX * xs_c, ET)
        ddt_raw = ddt * _sigmoid(R["raw"])
        dxs_c = dX * R["dt_e"] + dY * dsk
        dh_scr[...] = jnp.concatenate(dhprev, axis=1) + dhn * R["cd_e"]

        dact = jnp.concatenate([dxs_c] + dBs + dCs, axis=1)
        pre, sg = R["pre"], R["sg"]
        dpre = dact * (sg * (1.0 + pre * (1.0 - sg)))
        ext2_scr[0:8, :] = dpre[120:128, :]
        ext2_scr[8:16, :] = nxt_scr[...]
        nxt_scr[...] = dpre[0:8, :]
        cw = cw_ref[...]
        u_b, dpre_b = p_ref[:, S_XS:S_DT].astype(_MXU), dpre.astype(_MXU)
        dxbc = cw[3:4, :] * dpre
        taps = [_colsum(dpre * p_ref[:, S_XS:S_DT])]
        for s in (1, 2, 3):
            up = (R["col"] - R["row"] == s).astype(_MXU)
            d_s = jnp.concatenate([jnp.dot(up, dpre_b, preferred_element_type=F32)[0:120],
                                   ext2_scr[s:8 + s, :]], axis=0)
            dxbc = dxbc + cw[3 - s:4 - s, :] * d_s
            taps.append(_colsum(dpre * _rows_from_above(u_b, s, ext_scr, R["row"], R["col"])))
        acc_cw_ref[...] += _rows8(taps[::-1] + [_colsum(dpre)])
        acc_w_ref[...] += _rows8([jnp.concatenate(dnw_parts, axis=1), _colsum(dY * xs_c)])
        acc_s_ref[...] += _rows8([_colsum(ddt_raw), _colsum(da * R["dt"])])

        lane = lax.broadcasted_iota(jnp.int32, (128, 128), 1)
        dp_ref[:, S_Z:S_Z + 1024] = dz.astype(dp_ref.dtype)
        dp_ref[:, S_XS:S_DT] = dxbc.astype(dp_ref.dtype)
        dp_ref[:, S_DT:S_DT + 128] = jnp.where(lane < N_HEADS, ddt_raw, 0.0).astype(dp_ref.dtype)
        dp_ref[:, S_DT + 128:S_W] = jnp.zeros((128, 128), dp_ref.dtype)

        @pl.when(i == nc - 1)
        def _():
            acc = acc_s_ref[...]
            dskip = _mm_exact_r(acc_w_ref[...], ET)[1:2, :]
            acc_s_ref[...] = _rows8([acc[0:1, :], acc[1:2, :] * R["A"], dskip])

    const = lambda shape: pl.BlockSpec(shape, lambda i: (0, 0))
    smem = pl.BlockSpec(memory_space=pltpu.SMEM)
    rev = lambda i: (nc - 1 - i, 0)
    return _call(
        body, comm, name="ssd_bwd", grid=(nc,),
        in_specs=[pl.BlockSpec((CHUNK, S_W), rev),
                  pl.BlockSpec((8, S_W), lambda i: (jnp.maximum((nc - 1 - i) * 16 - 1, 0), 0)),
                  pl.BlockSpec((128, 1024), rev),
                  pl.BlockSpec((CHUNK, D_SSD), rev),
                  pl.BlockSpec((CHUNK, D_XBC), rev),
                  pl.BlockSpec((CHUNK, D_SSD), rev),
                  const((4, D_XBC)), const((1, D_XBC)), smem, smem, smem, const((1, 1024)),
                  const((128, 1024)), const((1024, 128))],
        out_specs=[pl.BlockSpec((CHUNK, S_W), rev), const((8, D_XBC)), const((8, 1024)), const((8, 128))],
        out_shape=[jax.ShapeDtypeStruct((L, S_W), _MXU), jax.ShapeDtypeStruct((8, D_XBC), F32),
                   jax.ShapeDtypeStruct((8, 1024), F32), jax.ShapeDtypeStruct((8, 128), F32)],
        scratch_shapes=[pltpu.VMEM((128, 1024), F32), pltpu.VMEM((16, D_XBC), F32),
                        pltpu.VMEM((16, D_XBC), F32), pltpu.VMEM((8, D_XBC), F32)],
        args=(proj_ssd, proj_ssd, hprev_all, ypre, pre, dy, conv_w, conv_b, dt_bias, a_log, d_skip, norm_w, E, ET))


def _rope(t, tab):
    cos, sa, sb = tab[:, 0:128], tab[:, 128:256], tab[:, 256:384]
    outs = []
    for i in range(t.shape[1] // 128):
        tg = t[:, i * 128:(i + 1) * 128]
        outs.append(tg * cos + pltpu.roll(tg, 8, 1) * sa + pltpu.roll(tg, 120, 1) * sb)
    return jnp.concatenate(outs, axis=1)


def _rope_transposed(d, tab):
    cos, sa, sb = tab[:, 0:128], tab[:, 128:256], tab[:, 256:384]
    outs = []
    for i in range(d.shape[1] // 128):
        dg = d[:, i * 128:(i + 1) * 128]
        outs.append(dg * cos + pltpu.roll(dg * sa, 120, 1) + pltpu.roll(dg * sb, 8, 1))
    return jnp.concatenate(outs, axis=1)


def _lo_half(rows):
    return lax.broadcasted_iota(jnp.int32, (rows, 128), 1) < HEAD_DIM


def _native_half(rows, j):
    lo = _lo_half(rows)
    return lo if j % 2 == 0 else jnp.logical_not(lo)


def _kv_native(t, j):
    p = j // 2
    return jnp.where(_native_half(t.shape[0], j), t[:, p * 128:(p + 1) * 128], 0.0)


def _stack_heads(t, j):
    out = []
    for m in (2 * j, 2 * j + 1):
        pair = t[:, m * 128:(m + 1) * 128]
        swapped = pltpu.roll(pair, HEAD_DIM, 1)
        out += [pair, swapped] if j % 2 == 0 else [swapped, pair]
    return jnp.concatenate(out, axis=0)


def _unstack_heads(s, j):
    out = []
    for m in range(2):
        first, second = s[256 * m:256 * m + 128], s[256 * m + 128:256 * m + 256]
        if j % 2 == 0:
            out.append(first + pltpu.roll(second, HEAD_DIM, 1))
        else:
            out.append(pltpu.roll(first, HEAD_DIM, 1) + second)
    return jnp.concatenate(out, axis=1)


def _keep_native(r, j):
    return jnp.where(_native_half(r.shape[0], j), r, 0.0)


def _sink_row(sink_ref, j):
    hid = lax.broadcasted_iota(jnp.int32, (1, 4 * CHUNK), 1) // CHUNK
    row = jnp.zeros((1, 4 * CHUNK), F32)
    for hh in range(4):
        row = jnp.where(hid == hh, sink_ref[4 * j + hh], row)
    return row


def _from_current():
    si = lax.broadcasted_iota(jnp.int32, (CHUNK, 4 * CHUNK), 0)
    qi = lax.broadcasted_iota(jnp.int32, (CHUNK, 4 * CHUNK), 1) % CHUNK
    return si <= qi


def _fold(full, from_cur, pen=0.0):
    return jnp.where(from_cur, full[CHUNK:2 * CHUNK], full[0:CHUNK] + pen)


def _unfold(t, from_cur):
    c = jnp.where(from_cur, t, 0.0)
    return jnp.concatenate([t - c, c], axis=0)


def _softmax_sink(s, sink):
    mx = jnp.maximum(jnp.max(s, axis=0, keepdims=True), sink)
    p = jnp.exp(s - mx)
    esink = jnp.exp(sink - mx)
    inv = 1.0 / (jnp.sum(p, axis=0, keepdims=True) + esink)
    return p * inv, esink * inv


def _swa_inputs(blk, p_ref, prev_ref, tab_ref, ptab_ref):
    tab = tab_ref[...]
    qr = _rope(p_ref[:, A_Q:A_Q + 1024], tab) * ATT_SCALE
    kk = jnp.concatenate([_rope(prev_ref[:, 0:256], ptab_ref[...]), _rope(p_ref[:, A_K:A_K + 256], tab)], axis=0)
    vv = jnp.concatenate([prev_ref[:, 256:512], p_ref[:, A_V:A_V + 256]], axis=0)
    return tab, qr, kk, vv, jnp.where(blk > 0, 0.0, NEG)


def _swa_forward_step(sink_ref, p_ref, prev_ref, tab_ref, ptab_ref, y_ref):
    n = pl.program_id(0)
    _, qr, kk, vv, pen = _swa_inputs(n, p_ref, prev_ref, tab_ref, ptab_ref)
    from_cur = _from_current()
    outs = []
    for j in range(KV_HEADS):
        s = _fold(_mm_nt(_kv_native(kk, j), _stack_heads(qr, j)), from_cur, pen)
        P, _ = _softmax_sink(s, _sink_row(sink_ref, j))
        outs.append(_unstack_heads(_mm_tn(_unfold(P, from_cur), _kv_native(vv, j)), j))
    g = p_ref[:, A_G:A_G + 1024]
    y_ref[:, D_SSD:D_SSD + D_ATT] = (jnp.concatenate(outs, axis=1) * (g * _sigmoid(g))).astype(y_ref.dtype)


def _mixer_forward(proj_ssd, proj_att, tabs, sinks, conv_w, conv_b, dt_bias, a_log, d_skip, norm_w, E, comm=None):
    L = proj_ssd.shape[0]
    nc = L // CHUNK

    def body(p_ref, halo_ref, cw_ref, cb_ref, dtb_ref, alog_ref, dsk_ref, nw_ref, e_ref,
             sink_ref, pa_ref, prev_ref, tab_ref, ptab_ref, y_ref, ypre_ref, hprev_ref, pre_ref, h_scr, ext_scr):
        _ssd_forward_step(p_ref, halo_ref, cw_ref, cb_ref, dtb_ref, alog_ref, dsk_ref, nw_ref, e_ref,
                          y_ref, ypre_ref, hprev_ref, pre_ref, h_scr, ext_scr)
        _swa_forward_step(sink_ref, pa_ref, prev_ref, tab_ref, ptab_ref, y_ref)

    const = lambda shape: pl.BlockSpec(shape, lambda c: (0, 0))
    smem = pl.BlockSpec(memory_space=pltpu.SMEM)
    rows = lambda w: pl.BlockSpec((CHUNK, w), lambda c: (c, 0))
    return _call(
        body, comm, name="mixer_fwd", grid=(nc,),
        in_specs=[rows(S_W), pl.BlockSpec((8, S_W), lambda c: (jnp.maximum(c * 16 - 1, 0), 0)),
                  const((4, D_XBC)), const((1, D_XBC)), smem, smem, smem, const((1, 1024)), const((128, 1024)),
                  smem, rows(A_W), pl.BlockSpec((CHUNK, 512), lambda c: (jnp.maximum(c - 1, 0), 2)),
                  rows(384), pl.BlockSpec((CHUNK, 384), lambda c: (jnp.maximum(c - 1, 0), 0))],
        out_specs=[rows(D_SSD + D_ATT), rows(D_SSD), pl.BlockSpec((128, 1024), lambda c: (c, 0)), rows(D_XBC)],
        out_shape=[jax.ShapeDtypeStruct((L, D_SSD + D_ATT), _MXU), jax.ShapeDtypeStruct((L, D_SSD), F32),
                   jax.ShapeDtypeStruct((nc * 128, 1024), F32), jax.ShapeDtypeStruct((L, D_XBC), F32)],
        scratch_shapes=[pltpu.VMEM((128, 1024), F32), pltpu.VMEM((136, D_XBC), F32)],
        args=(proj_ssd, proj_ssd, conv_w, conv_b, dt_bias, a_log, d_skip, norm_w, E,
              sinks, proj_att, proj_att, tabs, tabs))


def _swa_backward(proj_att, tabs, sinks, dy, comm=None):
    L = proj_att.shape[0]
    nb = L // CHUNK

    def body(sink_ref, p_ref, prev_ref, tab_ref, ptab_ref, dy_ref, dp_ref, dsink_ref, carry_k, carry_v):
        i = pl.program_id(0)
        n = nb - 1 - i

        @pl.when(i == 0)
        def _():
            carry_k[...] = jnp.zeros_like(carry_k)
            carry_v[...] = jnp.zeros_like(carry_v)
            dsink_ref[...] = jnp.zeros_like(dsink_ref)

        tab, qr, kk, vv, pen = _swa_inputs(n, p_ref, prev_ref, tab_ref, ptab_ref)
        from_cur = _from_current()
        g = p_ref[:, A_G:A_G + 1024]
        sgm = _sigmoid(g)
        dyv = dy_ref[...]
        do_all = dyv * (g * sgm)
        lane8 = lax.broadcasted_iota(jnp.int32, (8, 128), 1)
        hid = lax.broadcasted_iota(jnp.int32, (1, 4 * CHUNK), 1) // CHUNK
        o_parts, dq_parts = [], []
        dk_nat = [jnp.zeros((2 * CHUNK, 128), F32) for _ in range(2)]
        dv_nat = [jnp.zeros((2 * CHUNK, 128), F32) for _ in range(2)]
        dsink = jnp.zeros((8, 128), F32)
        for j in range(KV_HEADS):
            qs = _stack_heads(qr, j)
            kkb, vvb = _kv_native(kk, j), _kv_native(vv, j)
            P, psink = _softmax_sink(_fold(_mm_nt(kkb, qs), from_cur, pen), _sink_row(sink_ref, j))
            p_full = _unfold(P, from_cur)
            o_parts.append(_unstack_heads(_mm_tn(p_full, vvb), j))
            do_s = _stack_heads(do_all, j)
            dP = _fold(_mm_nt(vvb, do_s), from_cur)
            D = jnp.sum(P * dP, axis=0, keepdims=True)
            ds_full = _unfold(P * (dP - D), from_cur)
            sd = psink * D
            for hh in range(4):
                dsink = dsink + jnp.where(lane8 == 4 * j + hh, -jnp.sum(jnp.where(hid == hh, sd, 0.0)), 0.0)
            dq_parts.append(_unstack_heads(_mm_tn(ds_full, kkb), j) * ATT_SCALE)
            dk_nat[j // 2] = dk_nat[j // 2] + _keep_native(_mm(ds_full, qs), j)
            dv_nat[j // 2] = dv_nat[j // 2] + _keep_native(_mm(p_full, do_s), j)
        o = jnp.concatenate(o_parts, axis=1)
        dkk = jnp.concatenate(dk_nat, axis=1)
        dvv = jnp.concatenate(dv_nat, axis=1)
        out = dp_ref.dtype
        dp_ref[:, A_Q:A_Q + 1024] = _rope_transposed(jnp.concatenate(dq_parts, axis=1), tab).astype(out)
        dp_ref[:, A_K:A_K + 256] = _rope_transposed(dkk[CHUNK:2 * CHUNK] + carry_k[...], tab).astype(out)
        dp_ref[:, A_V:A_V + 256] = (dvv[CHUNK:2 * CHUNK] + carry_v[...]).astype(out)
        dp_ref[:, A_G:A_G + 1024] = (dyv * o * (sgm * (1.0 + g * (1.0 - sgm)))).astype(out)
        carry_k[...] = dkk[0:CHUNK]
        carry_v[...] = dvv[0:CHUNK]
        dsink_ref[...] += dsink

    rev = lambda i: (nb - 1 - i, 0)
    prev = lambda i: jnp.maximum(nb - 2 - i, 0)
    return _call(
        body, comm, name="swa_bwd", grid=(nb,),
        in_specs=[pl.BlockSpec(memory_space=pltpu.SMEM),
                  pl.BlockSpec((CHUNK, A_W), rev),
                  pl.BlockSpec((CHUNK, 512), lambda i: (prev(i), 2)),
                  pl.BlockSpec((CHUNK, 384), rev),
                  pl.BlockSpec((CHUNK, 384), lambda i: (prev(i), 0)),
                  pl.BlockSpec((CHUNK, D_ATT), lambda i: (nb - 1 - i, 1))],
        out_specs=[pl.BlockSpec((CHUNK, A_W), rev), pl.BlockSpec((8, 128), lambda i: (0, 0))],
        out_shape=[jax.ShapeDtypeStruct((L, A_W), _MXU), jax.ShapeDtypeStruct((8, 128), F32)],
        scratch_shapes=[pltpu.VMEM((CHUNK, 256), F32), pltpu.VMEM((CHUNK, 256), F32)],
        args=(sinks, proj_att, proj_att, tabs, tabs, dy))


def _head(y, x, target, w_out, ln_g, ln_b, *, tm):
    L = x.shape[0]
    nsteps = L // tm

    def body(y_ref, x_ref, t_ref, wo_ref, g_ref, b_ref, dr_ref, dy_ref, acc_ref):
        i = pl.program_id(0)

        @pl.when(i == 0)
        def _():
            acc_ref[...] = jnp.zeros_like(acc_ref)

        r = ALPHA * x_ref[...] + _mm(y_ref[...], wo_ref[...])
        mu = jnp.mean(r, axis=-1, keepdims=True)
        d = r - mu
        rstd = lax.rsqrt(jnp.mean(d * d, axis=-1, keepdims=True) + LN_EPS)
        xh = d * rstd
        gam = g_ref[0:1, :]
        e = xh * gam + b_ref[0:1, :] - t_ref[...]
        dout = e * (1.0 / D_MODEL)
        dxh = dout * gam
        dr = rstd * (dxh - jnp.mean(dxh, axis=-1, keepdims=True)
                     - xh * jnp.mean(dxh * xh, axis=-1, keepdims=True))
        dr_ref[...] = dr
        dy_ref[...] = _mm_nt(dr, wo_ref[...])
        acc_ref[...] += _rows8([_colsum(dout * xh), _colsum(dout), _colsum(e * e) * (0.5 / D_MODEL)])

        @pl.when(i == nsteps - 1)
        def _():
            acc = acc_ref[...]
            tot = jnp.sum(acc[2:3, :])
            rid = lax.broadcasted_iota(jnp.int32, (8, 1024), 0)
            acc_ref[...] = jnp.where(rid == 3, tot, acc)

    const = lambda shape: pl.BlockSpec(shape, lambda i: (0, 0))
    row = lambda w: pl.BlockSpec((tm, w), lambda i: (i, 0))
    return pl.pallas_call(
        body, name="head", grid=(nsteps,),
        in_specs=[row(2048), row(1024), row(1024), const((2048, 1024)), const((1, 1024)), const((1, 1024))],
        out_specs=[row(1024), row(2048), const((8, 1024))],
        out_shape=[jax.ShapeDtypeStruct((L, D_MODEL), F32), jax.ShapeDtypeStruct((L, 2048), F32),
                   jax.ShapeDtypeStruct((8, 1024), F32)],
        compiler_params=_params(("arbitrary",)),
    )(y, x, target, w_out, ln_g, ln_b)


def _gather_w_in(w_shard):
    R = w_shard.shape[0]
    halves = (pl.ds(0, R // 2), pl.ds(R // 2, R // 2))
    any_spec = pl.BlockSpec(memory_space=pl.ANY)

    def body(in_ref, out_ref, send_sems, recv_sems, local_sem):
        x, y, c = _position()

        def slot(p, half=None):
            s = out_ref.at[_index(*p)]
            return s if half is None else s.at[halves[half]]

        def same_core(p):
            return (p[0], p[1], c)

        def other_core(p):
            return (p[0], p[1], 1 - c)

        me, xn, yn, dg = (x, y), (1 - x, y), (x, 1 - y), (1 - x, 1 - y)

        def copy(k, dst, to, src=None):
            return _remote(dst if src is None else src, dst, send_sems.at[k], recv_sems.at[k], to)

        local = pltpu.make_async_copy(in_ref, slot(same_core(me)), local_sem)
        local.start()
        own = [copy(0, slot(same_core(me)), other_core(me), in_ref), copy(1, slot(same_core(me)), same_core(xn), in_ref),
               copy(2, slot(same_core(me)), same_core(yn), in_ref)]
        for cp in own:
            cp.start()
        copy(1, slot(same_core(xn)), same_core(xn)).wait_recv()
        passed = [copy(4, slot(same_core(xn), 1), same_core(yn)), copy(5, slot(same_core(xn)), other_core(me))]
        for cp in passed:
            cp.start()
        copy(2, slot(same_core(yn)), same_core(yn)).wait_recv()
        more = [copy(3, slot(same_core(yn), 0), same_core(xn)), copy(6, slot(same_core(yn)), other_core(me))]
        for cp in more:
            cp.start()
        passed += more
        for k, half in ((3, 0), (4, 1)):
            copy(k, slot(same_core(dg), half), same_core(xn)).wait_recv()
            fwd = copy(7 + half, slot(same_core(dg), half), other_core(me))
            fwd.start()
            passed.append(fwd)
        copy(0, slot(other_core(me)), other_core(me)).wait_recv()
        copy(5, slot(other_core(xn)), other_core(me)).wait_recv()
        copy(6, slot(other_core(yn)), other_core(me)).wait_recv()
        for half in (0, 1):
            copy(7 + half, slot(other_core(dg), half), other_core(me)).wait_recv()
        for cp in own + passed:
            cp.wait_send()
        local.wait()

    return pl.pallas_call(
        body, name="gather_w_in", in_specs=[any_spec], out_specs=any_spec,
        out_shape=jax.ShapeDtypeStruct((N_DEV,) + w_shard.shape, w_shard.dtype),
        scratch_shapes=[pltpu.SemaphoreType.DMA((9,)), pltpu.SemaphoreType.DMA((9,)), pltpu.SemaphoreType.DMA],
    )(w_shard)


def _input_gradient(d_ssd, d_att, w_ssd, w_att, dr, *, tm, comm=None):
    L = dr.shape[0]

    def body(ds_ref, da_ref, ws_ref, wa_ref, dr_ref, o_ref):
        o_ref[...] = ALPHA * dr_ref[...] + _mm_nt(ds_ref[...], ws_ref[...]) + _mm_nt(da_ref[...], wa_ref[...])

    row = lambda w: pl.BlockSpec((tm, w), lambda i: (i, 0))
    const = lambda shape: pl.BlockSpec(shape, lambda i: (0, 0))
    return _call(body, comm, name="dx", grid=(L // tm,),
                 in_specs=[row(S_W), row(A_W), const((D_MODEL, S_W)), const((D_MODEL, A_W)), row(D_MODEL)],
                 out_specs=[row(D_MODEL)], out_shape=[jax.ShapeDtypeStruct((L, D_MODEL), F32)],
                 scratch_shapes=[], args=(d_ssd, d_att, w_ssd, w_att, dr))


SHARD_COLS = D_IN_PROJ // N_DEV
SPLIT = N_SSD_REAL - 4 * SHARD_COLS
RELAYOUT_ROWS = 256


def _unpack_w_in(w_all):
    def body(g_ref, ws_ref, wa_ref):
        for j in range(4):
            ws_ref[:, SHARD_COLS * j:SHARD_COLS * (j + 1)] = g_ref[j]
        ws_ref[:, 4 * SHARD_COLS:N_SSD_REAL] = g_ref[4, :, 0:SPLIT]
        ws_ref[:, N_SSD_REAL:S_W] = jnp.zeros((RELAYOUT_ROWS, S_W - N_SSD_REAL), ws_ref.dtype)
        wa_ref[:, 0:SHARD_COLS - SPLIT] = g_ref[4, :, SPLIT:SHARD_COLS]
        for j in range(5, N_DEV):
            lo = SHARD_COLS * (j - 4) - SPLIT
            wa_ref[:, lo:lo + SHARD_COLS] = g_ref[j]

    return pl.pallas_call(
        body, name="unpack_w_in", grid=(D_MODEL // RELAYOUT_ROWS,),
        in_specs=[pl.BlockSpec((N_DEV, RELAYOUT_ROWS, SHARD_COLS), lambda i: (0, i, 0))],
        out_specs=[pl.BlockSpec((RELAYOUT_ROWS, S_W), lambda i: (i, 0)), pl.BlockSpec((RELAYOUT_ROWS, A_W), lambda i: (i, 0))],
        out_shape=[jax.ShapeDtypeStruct((D_MODEL, S_W), w_all.dtype), jax.ShapeDtypeStruct((D_MODEL, A_W), w_all.dtype)],
        compiler_params=_params(("arbitrary",)),
    )(w_all)


def _pack_dw_in(me1, dw_ssd, dw_att, half):
    def body(me_ref, *refs):
        if half == 0:
            ds_ref, p_ref, own_ref = refs
            me = me_ref[0]

            @pl.when(me >= 4)
            def _():
                own_ref[...] = jnp.zeros_like(own_ref)
        else:
            ds_ref, da_ref, p_ref = refs

        for j in range(4):
            if half == 0:
                pieces = [(0, ds_ref[:, SHARD_COLS * j:SHARD_COLS * (j + 1)])]
            elif j == 0:
                pieces = [(0, ds_ref[:, 4 * SHARD_COLS - S_DT:N_SSD_REAL - S_DT]), (SPLIT, da_ref[:, 0:SHARD_COLS - SPLIT])]
            else:
                lo = SHARD_COLS * j - SPLIT
                pieces = [(0, da_ref[:, lo:lo + SHARD_COLS])]
            for off, blk in pieces:
                p_ref[j, :, off:off + blk.shape[1]] = blk.astype(p_ref.dtype)
                if half == 0:
                    @pl.when(me == j)
                    def _(off=off, blk=blk):
                        own_ref[:, off:off + blk.shape[1]] = blk

    ins = [dw_ssd] if half == 0 else [dw_ssd, dw_att]
    row = lambda a: pl.BlockSpec((RELAYOUT_ROWS, a.shape[1]), lambda i: (i, 0))
    in_specs = [row(a) for a in ins]
    if half == 1:
        in_specs[0] = pl.BlockSpec((RELAYOUT_ROWS, S_W - S_DT), lambda i: (i, S_DT // (S_W - S_DT)))
    out_specs = [pl.BlockSpec((4, RELAYOUT_ROWS, SHARD_COLS), lambda i: (0, i, 0))]
    out_shape = [jax.ShapeDtypeStruct((4, D_MODEL, SHARD_COLS), BF16 if half == 0 else F32)]
    if half == 0:
        out_specs.append(pl.BlockSpec((RELAYOUT_ROWS, SHARD_COLS), lambda i: (i, 0)))
        out_shape.append(jax.ShapeDtypeStruct((D_MODEL, SHARD_COLS), F32))
    return pl.pallas_call(
        body, name="pack_dw_in_%d" % half, grid=(D_MODEL // RELAYOUT_ROWS,),
        in_specs=[pl.BlockSpec(memory_space=pltpu.SMEM)] + in_specs,
        out_specs=out_specs, out_shape=out_shape, compiler_params=_params(("arbitrary",)),
    )(me1, *ins)


def _pair_swap(stack):
    def body(in_ref, out_ref, send_sems, recv_sems):
        x, y, c = _position()
        cps = [_remote(in_ref.at[2 * oy + (1 - c)], out_ref.at[oy], send_sems.at[oy], recv_sems.at[oy], (x, y, 1 - c))
               for oy in range(2)]
        for cp in cps:
            cp.start()
        for cp in cps:
            cp.wait_recv()
        for cp in cps:
            cp.wait_send()

    any_spec = pl.BlockSpec(memory_space=pl.ANY)
    return pl.pallas_call(
        body, name="pair_swap", in_specs=[any_spec], out_specs=any_spec,
        out_shape=jax.ShapeDtypeStruct((2,) + stack.shape[1:], stack.dtype),
        scratch_shapes=[pltpu.SemaphoreType.DMA((2,)), pltpu.SemaphoreType.DMA((2,))],
    )(stack)


def _pair_sum(pos3, stack, swapped, own_lo):
    def body(pos_ref, a_ref, b_ref, lo_ref, chip_ref, own_ref):
        oy = pl.program_id(1)
        t = a_ref[0] + b_ref[0]
        chip_ref[0] = t.astype(chip_ref.dtype)

        @pl.when((pos_ref[0] == 0) & (oy == 0))
        def _():
            own_ref[...] = lo_ref[...]

        @pl.when((pos_ref[0] == 1) & (oy == pos_ref[1]))
        def _():
            own_ref[...] = t

    blk = (1, RELAYOUT_ROWS, SHARD_COLS)
    flat = pl.BlockSpec((RELAYOUT_ROWS, SHARD_COLS), lambda i, oy, pos: (i, 0))
    return pl.pallas_call(
        body, name="pair_sum",
        grid_spec=pltpu.PrefetchScalarGridSpec(
            num_scalar_prefetch=1, grid=(D_MODEL // RELAYOUT_ROWS, 2),
            in_specs=[pl.BlockSpec(blk, lambda i, oy, pos: (2 * oy + pos[2], i, 0)),
                      pl.BlockSpec(blk, lambda i, oy, pos: (oy, i, 0)), flat],
            out_specs=[pl.BlockSpec(blk, lambda i, oy, pos: (oy, i, 0)), flat]),
        out_shape=[jax.ShapeDtypeStruct((2, D_MODEL, SHARD_COLS), BF16), jax.ShapeDtypeStruct((D_MODEL, SHARD_COLS), F32)],
        compiler_params=_params(("arbitrary", "arbitrary")),
    )(pos3, stack, swapped, own_lo)


def _adamw_math(w, g, m, v):
    m = ADAM_B1 * m + (1.0 - ADAM_B1) * g
    v = ADAM_B2 * v + (1.0 - ADAM_B2) * (g * g)
    m_hat = m / (1.0 - ADAM_B1 ** ADAM_STEP)
    v_hat = v / (1.0 - ADAM_B2 ** ADAM_STEP)
    delta = -ADAM_LR * (m_hat / (jnp.sqrt(v_hat) + ADAM_EPS) + ADAM_WD * w)
    return delta, m, v


def _adamw_shard(n_recv, g_own, recv, w, m, v, *, rows, name):
    R, C = g_own.shape

    def body(n_ref, g_ref, r_ref, w_ref, m_ref, v_ref, go_ref, d_ref, mo_ref, vo_ref):
        g = g_ref[...]
        for k in range(N_DEV - 1):
            g = g + jnp.where(k < n_ref[0], r_ref[k].astype(F32), 0.0)
        d, mn, vn = _adamw_math(w_ref[...], g, m_ref[...], v_ref[...])
        go_ref[...] = g
        d_ref[...] = d
        mo_ref[...] = mn
        vo_ref[...] = vn

    blk = pl.BlockSpec((rows, C), lambda i: (i, 0))
    return pl.pallas_call(
        body, name=name, grid=(R // rows,),
        in_specs=[pl.BlockSpec(memory_space=pltpu.SMEM), blk,
                  pl.BlockSpec((N_DEV - 1, rows, C), lambda i: (0, i, 0)), blk, blk, blk],
        out_specs=[blk] * 4, out_shape=[jax.ShapeDtypeStruct((R, C), F32)] * 4,
        compiler_params=_params(("arbitrary",)),
    )(n_recv, g_own, recv, w, m, v)


def _minor_rows_view(a):
    return jnp.transpose(a, (2, 0, 1)).reshape(SHARD_COLS * 8, 128)


def _from_minor_rows_view(v):
    return jnp.transpose(v.reshape(SHARD_COLS, 8, 128), (1, 2, 0)).reshape(1, D_MODEL, SHARD_COLS)


def _adamw_w_in(n_recv, g_own, recv, w, m, v):
    C = SHARD_COLS
    pad = -C % 128

    def body(n_ref, g_ref, r_ref, w_ref, m_ref, v_ref, go_ref, d_ref, mo_ref, vo_ref):
        for q in range(D_MODEL // 128):
            band = pl.ds(q * 128, 128)
            g = g_ref[band, :]
            for k in range(N_DEV - 1):
                g = g + jnp.where(k < n_ref[0], r_ref[k, band, :].astype(F32), 0.0)
            g = jnp.pad(g, ((0, 0), (0, pad))).T[0:C]
            rows = pl.ds(q, C, stride=8)
            d, mn, vn = _adamw_math(w_ref[rows, :], g, m_ref[rows, :], v_ref[rows, :])
            go_ref[rows, :] = g
            d_ref[rows, :] = d
            mo_ref[rows, :] = mn
            vo_ref[rows, :] = vn

    return pl.pallas_call(
        body, name="adamw_w_in", out_shape=[jax.ShapeDtypeStruct(w.shape, F32)] * 4,
        in_specs=[pl.BlockSpec(memory_space=pltpu.SMEM)] + [pl.BlockSpec(memory_space=pltpu.VMEM)] * 5,
        out_specs=[pl.BlockSpec(memory_space=pltpu.VMEM)] * 4,
        compiler_params=_params(),
    )(n_recv, g_own, recv, w, m, v)


SMALL = ("conv_b", "dt_bias", "a_log", "d_skip", "ssd_norm_w", "attn_sinks", "ln_g", "ln_b")


def _adamw_small(gathered, params):
    n_p = len(SMALL)

    def body(*refs):
        acc = []
        for r in refs[:5]:
            t = r[0]
            for k in range(1, N_DEV):
                t = t + r[k]
            acc.append(t)
        head, conv, norm, scal, sink = acc
        grads = dict(conv_b=conv[4:5, :], dt_bias=scal[0:1, 0:N_HEADS], a_log=scal[1:2, 0:N_HEADS],
                     d_skip=scal[2:3, 0:N_HEADS], ssd_norm_w=norm[0:1, :], attn_sinks=sink[0:1, 0:N_HEADS],
                     ln_g=head[0:1, :], ln_b=head[1:2, :])
        wmv = refs[5:5 + 3 * n_p]
        outs = refs[5 + 3 * n_p:]
        outs[0][...] = head[3:4, 0:1]
        outs[1][...] = conv[0:4, :]
        for i, name in enumerate(SMALL):
            w_ref, m_ref, v_ref = wmv[3 * i:3 * i + 3]
            g = grads[name]
            d, mn, vn = _adamw_math(w_ref[...], g, m_ref[...], v_ref[...])
            for o_ref, val in zip(outs[2 + 4 * i:6 + 4 * i], (g, d, mn, vn)):
                o_ref[...] = val

    flat = [a for name in SMALL for a in params[name]]
    out_shape = [jax.ShapeDtypeStruct((1, 1), F32), jax.ShapeDtypeStruct((4, D_XBC), F32)]
    for name in SMALL:
        out_shape += [jax.ShapeDtypeStruct(params[name][0].shape, F32)] * 4
    res = pl.pallas_call(body, name="adamw_small", out_shape=out_shape, compiler_params=_params())(*gathered, *flat)
    return res[0], res[1], {name: res[2 + 4 * i:6 + 4 * i] for i, name in enumerate(SMALL)}


def _adamw_plain(g, w, m, v):
    def body(g_ref, w_ref, m_ref, v_ref, d_ref, mo_ref, vo_ref):
        d, mn, vn = _adamw_math(w_ref[...], g_ref[...], m_ref[...], v_ref[...])
        d_ref[...] = d
        mo_ref[...] = mn
        vo_ref[...] = vn

    return pl.pallas_call(
        body, name="adamw_conv_w", out_shape=[jax.ShapeDtypeStruct(w.shape, F32)] * 3,
        compiler_params=_params(),
    )(g, w, m, v)


def _lane_pattern(fn):
    return np.asarray([fn(l % HEAD_DIM) for l in range(128)], np.float32)


ROPE_INV = _lane_pattern(lambda r: ROPE_THETA ** (-2.0 * (r % 8) / ROPE_DIM) if r < ROPE_DIM else 0.0)
ROPE_SIN_A = _lane_pattern(lambda r: 1.0 if 8 <= r < ROPE_DIM else 0.0)
ROPE_SIN_B = _lane_pattern(lambda r: -1.0 if r < 8 else 0.0)


def _rope_tables(positions):
    ang = positions.astype(F32)[:, None] * ROPE_INV[None, :]
    sn = jnp.sin(ang)
    return jnp.concatenate([jnp.cos(ang), sn * ROPE_SIN_A[None, :], sn * ROPE_SIN_B[None, :]], axis=1)


def _expansion():
    E = np.arange(1024)[None, :] // HEAD_DIM == np.arange(128)[:, None]
    return jnp.asarray(E, BF16), jnp.asarray(E.T, BF16)


def _ssd_args(conv_w, conv_b, dt_bias, a_log, d_skip, norm_w, E):
    return (conv_w, conv_b, dt_bias.reshape(-1), a_log.reshape(-1), d_skip.reshape(-1), norm_w, E)


def kernel(x, positions, w_in, conv_w, conv_b, dt_bias, a_log, d_skip, ssd_norm_w, attn_sinks, w_out, ln_g, ln_b, loss_target, m_w_in, m_conv_w, m_conv_b, m_dt_bias, m_a_log, m_d_skip, m_ssd_norm_w, m_attn_sinks, m_w_out, m_ln_g, m_ln_b, v_w_in, v_conv_w, v_conv_b, v_dt_bias, v_a_log, v_d_skip, v_ssd_norm_w, v_attn_sinks, v_w_out, v_ln_g, v_ln_b):
    me = _index(*_position())
    me1 = me.reshape(1).astype(jnp.int32)
    x0, target = x[0], loss_target[0]
    bf16_shard = lambda shape: jax.ShapeDtypeStruct(shape, BF16)
    E, ET = _expansion()
    tabs = _rope_tables(positions[0])
    sinks = attn_sinks.reshape(-1)

    w_ssd, w_att = _unpack_w_in(_gather_w_in(w_in[0].astype(BF16)))
    gather_conv_w = _Hosted([conv_w[0]], [jax.ShapeDtypeStruct((N_DEV,) + conv_w.shape[1:], F32)],
                            [_Flow("gather", 0, 0)])

    proj_ssd, xb, conv_w_all = _matmul(x0, w_ssd, tm=512, tn=S_W, name="in_proj_ssd", emit_a=True,
                                       comm=gather_conv_w)
    conv_w_f = jnp.transpose(conv_w_all, (1, 0, 2)).reshape(4, D_XBC)
    ssd_args = _ssd_args(conv_w_f, conv_b, dt_bias, a_log, d_skip, ssd_norm_w, E)
    proj_att = _matmul(xb, w_att, tm=512, tn=A_W, name="in_proj_att")
    gather_w_out = _Hosted([w_out[0].astype(BF16)], [bf16_shard((N_DEV, 256, D_MODEL))], [_Flow("gather", 0, 0)])
    y, ypre, hprev, pre, w_out_all = _mixer_forward(proj_ssd, proj_att, tabs, sinks, *ssd_args, comm=gather_w_out)
    w_out_f = w_out_all.reshape(2 * D_MODEL, D_MODEL)
    dr, dy, acc_head = _head(y, x0, target, w_out_f, ln_g, ln_b, tm=512)

    dw_out, dw_out_bf16 = _matmul_tn(y, dr, tl=1024, tn=D_MODEL, name="dw_out", emit_bf16=True)
    own_out = lax.dynamic_index_in_dim(dw_out.reshape(N_DEV, 256, D_MODEL), me, axis=0, keepdims=False)
    send_out = _Hosted([dw_out_bf16.reshape(N_DEV, 256, D_MODEL)], [bf16_shard((N_DEV - 1, 256, D_MODEL))],
                       [_Flow("exchange", 0, 0)])
    d_ssd, acc_cw, acc_w, acc_s, recv_out = _ssd_backward(proj_ssd, hprev, ypre, pre, dy, *ssd_args, ET, comm=send_out)
    dw_ssd = _matmul_tn(xb, d_ssd, tl=2048, tn=S_W // 2, name="dw_in_ssd")
    parts_lo, own_lo = _pack_dw_in(me1, dw_ssd, None, 0)
    recv_shape = bf16_shard((N_DEV - 1, D_MODEL, SHARD_COLS))
    send_lo = _Hosted([parts_lo], [recv_shape], [_Flow("exchange", 0, 0, target_x=0, target_c=0)])
    d_att, dsink, recv_in = _swa_backward(proj_att, tabs, sinks, dy, comm=send_lo)
    dw_att = _matmul_tn(xb, d_att, tl=2048, tn=A_W // 2, name="dw_in_att")
    (stack_hi,) = _pack_dw_in(me1, dw_ssd, dw_att, 1)
    pos3 = jnp.stack(_position()).astype(jnp.int32)
    chip_hi, own_in = _pair_sum(pos3, stack_hi, _pair_swap(stack_hi), own_lo)
    accs = [acc_head, acc_cw, acc_w, acc_s, dsink]
    send_hi = _Hosted([chip_hi, recv_in, parts_lo] + accs,
                      [recv_shape] + [jax.ShapeDtypeStruct((N_DEV,) + a.shape, F32) for a in accs],
                      [_Flow("chip_exchange", 0, 0, target_x=1), _Flow("exchange", 2, 0, target_x=0, target_c=1)]
                      + [_Flow("gather", 3 + i, 1 + i) for i in range(5)], aliases={1: 0})
    dx, recv_in, *gathered = _input_gradient(d_ssd, d_att, w_ssd, w_att, dr, tm=256, comm=send_hi)
    n_recv_in = jnp.where(me < 4, N_DEV - 1, 3).reshape(1).astype(jnp.int32)
    n_recv_out = jnp.full((1,), N_DEV - 1, jnp.int32)

    g_in, d_in, nm_in, nv_in = [_from_minor_rows_view(r) for r in _adamw_w_in(
        n_recv_in, own_in, recv_in, _minor_rows_view(w_in), _minor_rows_view(m_w_in), _minor_rows_view(v_w_in))]
    g_out, d_out, nm_out, nv_out = _adamw_shard(n_recv_out, own_out, recv_out, w_out[0], m_w_out[0], v_w_out[0],
                                                rows=256, name="adamw_w_out")
    loss, g_conv_w, small = _adamw_small(gathered, dict(
        conv_b=(conv_b, m_conv_b, v_conv_b), dt_bias=(dt_bias, m_dt_bias, v_dt_bias), a_log=(a_log, m_a_log, v_a_log),
        d_skip=(d_skip, m_d_skip, v_d_skip), ssd_norm_w=(ssd_norm_w, m_ssd_norm_w, v_ssd_norm_w),
        attn_sinks=(attn_sinks, m_attn_sinks, v_attn_sinks), ln_g=(ln_g, m_ln_g, v_ln_g), ln_b=(ln_b, m_ln_b, v_ln_b)))
    g_cw = lax.dynamic_slice_in_dim(g_conv_w, me * (D_XBC // N_DEV), D_XBC // N_DEV, axis=1)
    d_cw, nm_cw, nv_cw = _adamw_plain(g_cw, conv_w[0], m_conv_w[0], v_conv_w[0])

    def leaves(i, big_in, cw, big_out):
        mid = [small[k][i] for k in ("conv_b", "dt_bias", "a_log", "d_skip", "ssd_norm_w", "attn_sinks")]
        return [big_in, cw[None]] + mid + [big_out[None], small["ln_g"][i], small["ln_b"][i]]

    return (loss.reshape(()), dx[None], *leaves(0, g_in, g_cw, g_out), *leaves(1, d_in, d_cw, d_out),
            *leaves(2, nm_in, nm_cw, nm_out), *leaves(3, nv_in, nv_cw, nv_out))
```

```python
import jax
import jax.numpy as jnp
from jax import lax
from jax.experimental import pallas as pl
from jax.experimental.pallas import tpu as pltpu
import numpy as np

F32 = jnp.float32
BF16 = jnp.bfloat16
_MXU = jnp.bfloat16

N_DEV = 8
D_MODEL = 1024
D_SSD = 1024
D_ATT = 1024
HEAD_DIM = 64
N_HEADS = 16
SSD_GROUPS = 2
KV_HEADS = 4
CHUNK = 128
D_XBC = 1536
D_IN_PROJ = 5136
ROPE_DIM = 16
ROPE_THETA = 500000.0
ALPHA = (2.0 * 1) ** 0.25
LN_EPS = 1e-5
RMS_EPS = 1e-5
ATT_SCALE = HEAD_DIM ** -0.5
NEG = -1e30

S_Z, S_XS, S_B, S_C, S_DT, S_W = 0, 1024, 2048, 2304, 2560, 2816
N_SSD_REAL = 2576
A_Q, A_K, A_V, A_G, A_W = 0, 1024, 1280, 1536, 2560

ADAM_LR = 0.001
ADAM_B1 = 0.9
ADAM_B2 = 0.999
ADAM_EPS = 1e-08
ADAM_WD = 0.01
ADAM_STEP = 10

VMEM_LIMIT = 48 * 1024 * 1024
MESH = pl.DeviceIdType.MESH


def _params(sem=None):
    return pltpu.CompilerParams(dimension_semantics=sem, vmem_limit_bytes=VMEM_LIMIT)


def _mm(a, b):
    return jnp.dot(a.astype(_MXU), b.astype(_MXU), preferred_element_type=F32)


def _mm_nt(a, b):
    return lax.dot_general(a.astype(_MXU), b.astype(_MXU), (((1,), (1,)), ((), ())),
                           preferred_element_type=F32)


def _mm_tn(a, b):
    return lax.dot_general(a.astype(_MXU), b.astype(_MXU), (((0,), (0,)), ((), ())),
                           preferred_element_type=F32)


def _split3(v):
    hi = v.astype(BF16)
    r = v - hi.astype(F32)
    mid = r.astype(BF16)
    lo = (r - mid.astype(F32)).astype(BF16)
    return hi, mid, lo


def _mm_exact_r(v, p01):
    hi, mid, lo = _split3(v)
    d = lambda a: jnp.dot(a, p01, preferred_element_type=F32)
    return d(hi) + d(mid) + d(lo)


def _mm_exact_l(p01, v):
    hi, mid, lo = _split3(v)
    d = lambda a: jnp.dot(p01, a, preferred_element_type=F32)
    return d(hi) + d(mid) + d(lo)


def _mm_2pass_r(v, p01):
    hi = v.astype(BF16)
    lo = (v - hi.astype(F32)).astype(BF16)
    return jnp.dot(hi, p01, preferred_element_type=F32) + jnp.dot(lo, p01, preferred_element_type=F32)


def _sigmoid(x):
    return 1.0 / (1.0 + jnp.exp(-x))


def _softplus(x):
    e = jnp.exp(-jnp.abs(x))
    u = 1.0 + e
    log1p = jnp.where(u == 1.0, e, jnp.log(u) * (e / (u - 1.0)))
    return jnp.maximum(x, 0.0) + log1p


def _rows8(rows):
    n = rows[0].shape[1]
    rid = lax.broadcasted_iota(jnp.int32, (8, n), 0)
    out = jnp.zeros((8, n), F32)
    for k, r in enumerate(rows):
        out = out + jnp.where(rid == k, r, 0.0)
    return out


def _colsum(a):
    return jnp.sum(a, axis=0, keepdims=True)


def _matmul_tn(a, g, *, tl, tn, name, emit_bf16=False):
    L, M = a.shape
    N = g.shape[1]
    last = L // tl - 1

    def body(a_ref, g_ref, o_ref, *rest):
        @pl.when(pl.program_id(1) == 0)
        def _():
            o_ref[...] = jnp.zeros_like(o_ref)

        o_ref[...] += _mm_tn(a_ref[...], g_ref[...])
        if emit_bf16:
            @pl.when(pl.program_id(1) == last)
            def _():
                rest[0][...] = o_ref[...].astype(BF16)

    spec = pl.BlockSpec((M, tn), lambda j, l: (0, j))
    res = pl.pallas_call(
        body, name=name, grid=(N // tn, L // tl),
        in_specs=[pl.BlockSpec((tl, M), lambda j, l: (l, 0)), pl.BlockSpec((tl, tn), lambda j, l: (l, j))],
        out_specs=[spec, spec] if emit_bf16 else [spec],
        out_shape=[jax.ShapeDtypeStruct((M, N), F32)] + ([jax.ShapeDtypeStruct((M, N), BF16)] if emit_bf16 else []),
        compiler_params=_params(("arbitrary", "arbitrary")),
    )(a, g)
    return res if emit_bf16 else res[0]


def _position():
    return lax.axis_index("x"), lax.axis_index("y"), lax.axis_index("c")


def _index(px, py, pc):
    return 4 * px + 2 * py + pc


def _flip(pos, k):
    x, y, c = pos
    return ((1 - x) if (k >> 2) & 1 else x, (1 - y) if (k >> 1) & 1 else y, (1 - c) if k & 1 else c)


def _when(cond, fn):
    if cond is True:
        fn()
    else:
        pl.when(cond)(fn)


def _remote(src, dst, send_sem, recv_sem, peer):
    return pltpu.make_async_remote_copy(src_ref=src, dst_ref=dst, send_sem=send_sem, recv_sem=recv_sem,
                                        device_id=peer, device_id_type=MESH)


class _Flow:
    def __init__(self, kind, operand, result, target_x=None, target_c=None):
        self.kind, self.operand, self.result, self.target_x, self.target_c = kind, operand, result, target_x, target_c

    def owns(self, pos):
        if self.target_x is None:
            return True
        cond = pos[0] == self.target_x
        return cond if self.target_c is None else cond & (pos[2] == self.target_c)


class _Hosted:
    def __init__(self, operands, out_shapes, flows, aliases=None):
        self.operands, self.out_shapes, self.flows = operands, out_shapes, flows
        self.aliases = aliases or {}

    def plan(self, ins, outs, send_sems, recv_sems, local_sems):
        me = _position()
        mi = _index(*me)
        sends, recvs, locals_ = [], [], []
        for row, f in enumerate(self.flows):
            src, dst = ins[f.operand], outs[f.result]
            for k in range(1, N_DEV):
                peer = _flip(me, k)
                sems = (send_sems.at[row, k - 1], recv_sems.at[row, k - 1])
                if f.kind == "exchange":
                    owner = _index(*peer) if f.target_x is None else 2 * peer[1] + peer[2]
                    cp = _remote(src.at[owner], dst.at[k - 1], *sems, peer)
                    sends.append((f.owns(peer), cp))
                    recvs.append((f.owns(me), cp))
                elif f.kind == "chip_exchange":
                    if k & 1:
                        continue
                    cp = _remote(src.at[peer[1]], dst.at[k // 2 - 1], *sems, peer)
                    sends.append((peer[0] == f.target_x, cp))
                    recvs.append((me[0] == f.target_x, cp))
                else:
                    sends.append((True, _remote(src, dst.at[mi], *sems, peer)))
                    recvs.append((True, _remote(src, dst.at[_index(*peer)], *sems, peer)))
            if f.kind == "gather":
                locals_.append(pltpu.make_async_copy(src, dst.at[mi], local_sems.at[row]))

        def start():
            for cp in locals_:
                cp.start()
            for cond, cp in sends:
                _when(cond, cp.start)

        def wait():
            for cond, cp in recvs:
                _when(cond, cp.wait_recv)
            for cond, cp in sends:
                _when(cond, cp.wait_send)
            for cp in locals_:
                cp.wait()

        return start, wait


def _call(body, comm, *, name, grid, in_specs, out_specs, out_shape, scratch_shapes, args, aliases=None):
    io_alias = dict(aliases or {})
    semantics = ("arbitrary",) * len(grid)
    if comm is None:
        return pl.pallas_call(body, name=name, grid=grid, in_specs=in_specs, out_specs=out_specs, out_shape=out_shape,
                              scratch_shapes=scratch_shapes, input_output_aliases=io_alias,
                              compiler_params=_params(semantics))(*args)
    n_in, n_out, n_scr = len(args), len(out_shape), len(scratch_shapes)
    c_in, c_out, rows = len(comm.operands), len(comm.out_shapes), len(comm.flows)

    def hosted(*refs):
        ins, refs = refs[:n_in], refs[n_in:]
        cins, refs = refs[:c_in], refs[c_in:]
        outs, refs = refs[:n_out], refs[n_out:]
        couts, refs = refs[:c_out], refs[c_out:]
        scr, (send_sems, recv_sems, local_sems) = refs[:n_scr], refs[n_scr:]
        start, wait = comm.plan(cins, couts, send_sems, recv_sems, local_sems)
        ids = [pl.program_id(d) for d in range(len(grid))]
        first, last = ids[0] == 0, ids[0] == grid[0] - 1
        for d in range(1, len(grid)):
            first, last = first & (ids[d] == 0), last & (ids[d] == grid[d] - 1)
        pl.when(first)(start)
        body(*ins, *outs, *scr)
        pl.when(last)(wait)

    for ci, co in comm.aliases.items():
        io_alias[n_in + ci] = n_out + co
    any_spec = pl.BlockSpec(memory_space=pl.ANY)
    sems = [pltpu.SemaphoreType.DMA((rows, N_DEV - 1)), pltpu.SemaphoreType.DMA((rows, N_DEV - 1)),
            pltpu.SemaphoreType.DMA((rows,))]
    return pl.pallas_call(
        hosted, name=name, grid=grid, in_specs=list(in_specs) + [any_spec] * c_in,
        out_specs=list(out_specs) + [any_spec] * c_out, out_shape=list(out_shape) + list(comm.out_shapes),
        scratch_shapes=list(scratch_shapes) + sems, input_output_aliases=io_alias,
        compiler_params=_params(semantics))(*args, *comm.operands)


def _head_row(ref, width, rep):
    hid = lax.broadcasted_iota(jnp.int32, (1, width), 1) // rep
    row = jnp.zeros((1, width), F32)
    for h in range(N_HEADS):
        row = jnp.where(hid == h, ref[h], row)
    return row


def _rows_from_above(u_b, s, ext_scr, row, col):
    down = (row - col == s).astype(_MXU)
    return jnp.concatenate([ext_scr[8 - s:16 - s, :], jnp.dot(down, u_b, preferred_element_type=F32)[8:128]], axis=0)


def _ssd_recompute(first, p_ref, halo_ref, cw_ref, cb_ref, dtb_ref, alog_ref, e_ref, ext_scr, pre=None):
    row = lax.broadcasted_iota(jnp.int32, (128, 128), 0)
    col = lax.broadcasted_iota(jnp.int32, (128, 128), 1)
    ext_scr[0:8, :] = jnp.where(first, 0.0, halo_ref[:, S_XS:S_DT])
    if pre is not None:
        ext_scr[8:16, :] = p_ref[0:8, S_XS:S_DT]
    else:
        ext_scr[8:136, :] = p_ref[:, S_XS:S_DT]
        cw = cw_ref[...]
        pre = (cb_ref[0:1, :] + cw[3:4, :] * ext_scr[8:136, :] + cw[2:3, :] * ext_scr[7:135, :]
               + cw[1:2, :] * ext_scr[6:134, :] + cw[0:1, :] * ext_scr[5:133, :])
    sg = _sigmoid(pre)
    act = pre * sg
    lane = lax.broadcasted_iota(jnp.int32, (1, 128), 1)
    A = jnp.where(lane < N_HEADS, -jnp.exp(_head_row(alog_ref, 128, 1)), 0.0)
    raw = p_ref[:, S_DT:S_DT + 128] + _head_row(dtb_ref, 128, 1)
    dt = _softplus(raw)
    dA = dt * A
    tril = (row >= col).astype(BF16)
    acs = _mm_exact_l(tril, dA)
    last = acs[127:128, :]
    ds = jnp.exp(last - acs)
    eo = jnp.exp(acs)
    E = e_ref[...]
    ex = _mm_2pass_r(jnp.concatenate([dt, ds, eo], axis=0), E)
    dt_e, ds_e, eo_e = ex[0:128], ex[128:256], ex[256:384]
    xs_c = act[:, 0:1024]
    X = xs_c * dt_e
    return dict(pre=pre, sg=sg, xs_c=xs_c, Bc=act[:, 1024:1280], Cc=act[:, 1280:1536], A=A, raw=raw, dt=dt,
                acs=acs, acsT=acs.T, eo_e=eo_e, ds_e=ds_e, dt_e=dt_e, cd_e=eo_e[127:128, :],
                X=X, Xd=X * ds_e, row=row, col=col)


def _split_halves(t):
    lo = _lo_half(CHUNK)
    return jnp.concatenate([jnp.where(lo, t, 0.0), jnp.where(lo, 0.0, t)], axis=0)


def _ssd_core(R, hprev):
    causal = R["row"] >= R["col"]
    acs, acsT, X = R["acs"], R["acsT"], R["X"]
    ydiag, yoff, snew = [], [], []
    for g in range(SSD_GROUPS):
        Bg = R["Bc"][:, g * 128:(g + 1) * 128]
        Cg = R["Cc"][:, g * 128:(g + 1) * 128]
        cols = slice(g * 512, (g + 1) * 512)
        CB = _mm_nt(Cg, Bg)
        snew.append(_mm_tn(Bg, R["Xd"][:, cols]))
        yoff.append(_mm(Cg, hprev[:, cols]))
        for j in range(4):
            h0 = g * 8 + 2 * j
            ms = [CB * jnp.exp(jnp.where(causal, acs[:, h:h + 1] - acsT[h:h + 1, :], NEG)) for h in (h0, h0 + 1)]
            ydiag.append(_mm(jnp.concatenate(ms, axis=1), _split_halves(X[:, h0 * HEAD_DIM:h0 * HEAD_DIM + 128])))
    Y = jnp.concatenate(ydiag, axis=1) + jnp.concatenate(yoff, axis=1) * R["eo_e"]
    return Y, jnp.concatenate(snew, axis=1)


def _ssd_forward_step(p_ref, halo_ref, cw_ref, cb_ref, dtb_ref, alog_ref, dsk_ref, nw_ref, e_ref,
                      y_ref, ypre_ref, hprev_ref, pre_ref, h_scr, ext_scr):
    c = pl.program_id(0)
    first = c == 0

    @pl.when(first)
    def _():
        h_scr[...] = jnp.zeros_like(h_scr)

    R = _ssd_recompute(first, p_ref, halo_ref, cw_ref, cb_ref, dtb_ref, alog_ref, e_ref, ext_scr)
    hprev = h_scr[...]
    hprev_ref[...] = hprev
    pre_ref[...] = R["pre"]
    Y, snew = _ssd_core(R, hprev)
    h_scr[...] = hprev * R["cd_e"] + snew
    Y = Y + _head_row(dsk_ref, D_SSD, HEAD_DIM) * R["xs_c"]
    ypre_ref[...] = Y
    z = p_ref[:, S_Z:S_Z + 1024]
    yf = Y * (z * _sigmoid(z))
    outs = []
    for g in range(SSD_GROUPS):
        yg = yf[:, g * 512:(g + 1) * 512]
        r = lax.rsqrt(jnp.mean(yg * yg, axis=-1, keepdims=True) + RMS_EPS)
        outs.append(yg * r)
    y_ref[:, 0:D_SSD] = (jnp.concatenate(outs, axis=1) * nw_ref[0:1, :]).astype(y_ref.dtype)


def _ssd_backward(proj_ssd, hprev_all, ypre, pre, dy, conv_w, conv_b, dt_bias, a_log, d_skip, norm_w, E, ET, comm=None):
    L = proj_ssd.shape[0]
    nc = L // CHUNK

    def body(p_ref, halo_ref, hprev_ref, ypre_ref, pre_ref, dy_ref, cw_ref, cb_ref, dtb_ref, alog_ref, dsk_ref, nw_ref, e_ref,
             et_ref, dp_ref, acc_cw_ref, acc_w_ref, acc_s_ref, dh_scr, ext_scr, ext2_scr, nxt_scr):
        i = pl.program_id(0)
        c = nc - 1 - i
        first = c == 0

        @pl.when(i == 0)
        def _():
            dh_scr[...] = jnp.zeros_like(dh_scr)
            nxt_scr[...] = jnp.zeros_like(nxt_scr)
            acc_cw_ref[...] = jnp.zeros_like(acc_cw_ref)
            acc_w_ref[...] = jnp.zeros_like(acc_w_ref)
            acc_s_ref[...] = jnp.zeros_like(acc_s_ref)

        R = _ssd_recompute(first, p_ref, halo_ref, cw_ref, cb_ref, dtb_ref, alog_ref, e_ref, ext_scr, pre_ref[...])
        hprev = hprev_ref[...]
        xs_c, X, Xd = R["xs_c"], R["X"], R["Xd"]
        acs, acsT = R["acs"], R["acsT"]
        ET = et_ref[...]
        dsk = _head_row(dsk_ref, D_SSD, HEAD_DIM)
        Y = ypre_ref[...]

        z = p_ref[:, S_Z:S_Z + 1024]
        sz = _sigmoid(z)
        silz = z * sz
        yf = Y * silz
        dyv = dy_ref[...]
        nw = nw_ref[0:1, :]
        dyf_parts, dnw_parts = [], []
        for g in range(SSD_GROUPS):
            cols = slice(g * 512, (g + 1) * 512)
            yg = yf[:, cols]
            r = lax.rsqrt(jnp.mean(yg * yg, axis=-1, keepdims=True) + RMS_EPS)
            yn = yg * r
            dyn = dyv[:, cols] * nw[:, cols]
            dnw_parts.append(_colsum(dyv[:, cols] * yn))
            dyf_parts.append(r * (dyn - yn * jnp.mean(dyn * yn, axis=-1, keepdims=True)))
        dyf = jnp.concatenate(dyf_parts, axis=1)
        dY = dyf * silz
        dz = dyf * Y * (sz * (1.0 + z * (1.0 - sz)))

        dhn = dh_scr[...]
        dYo = dY * R["eo_e"]
        causal = R["row"] >= R["col"]
        dacs = jnp.zeros((128, 128), F32)
        dacs_t = jnp.zeros((128, 128), F32)
        dxdiag, dxd, dhprev, dBs, dCs, yoff = [], [], [], [], [], []
        for g in range(SSD_GROUPS):
            Bg = R["Bc"][:, g * 128:(g + 1) * 128]
            Cg = R["Cc"][:, g * 128:(g + 1) * 128]
            cols = slice(g * 512, (g + 1) * 512)
            CB = _mm_nt(Cg, Bg)
            dCB = jnp.zeros((128, 128), F32)
            for j in range(4):
                h0 = g * 8 + 2 * j
                pc = slice(h0 * HEAD_DIM, h0 * HEAD_DIM + 128)
                dYst = _split_halves(dY[:, pc])
                dMst = _mm_nt(dYst, X[:, pc])
                mts = []
                for a, h in enumerate((h0, h0 + 1)):
                    acol = acs[:, h:h + 1]
                    arow = acsT[h:h + 1, :]
                    Lm = jnp.exp(jnp.where(causal, acol - arow, NEG))
                    M = CB * Lm
                    dM = dMst[a * 128:(a + 1) * 128]
                    dCB = dCB + dM * Lm
                    G = dM * M
                    dacs = dacs + jnp.where(R["col"] == h, jnp.sum(G, axis=1, keepdims=True), 0.0)
                    dacs_t = dacs_t + jnp.where(R["row"] == h, jnp.sum(G, axis=0, keepdims=True), 0.0)
                    mts.append(M.T)
                dxdiag.append(_mm(jnp.concatenate(mts, axis=1), dYst))
            dS = dhn[:, cols]
            dxd.append(_mm(Bg, dS))
            yoff.append(_mm(Cg, hprev[:, cols]))
            dhprev.append(_mm_tn(Cg, dYo[:, cols]))
            dCs.append(_mm_nt(dYo[:, cols], hprev[:, cols]) + _mm(dCB, Bg))
            dBs.append(_mm_tn(dCB, Cg) + _mm_nt(Xd[:, cols], dS))
        Yoff = jnp.concatenate(yoff, axis=1) * R["eo_e"]
        dXd = jnp.concatenate(dxd, axis=1)
        dX = jnp.concatenate(dxdiag, axis=1) + dXd * R["ds_e"]
        t_state = dXd * Xd
        dacs = dacs + _mm_2pass_r(dY * Yoff - t_state, ET) - dacs_t.T
        v_last = _colsum(t_state + dhn * hprev * R["cd_e"])
        dlast = _mm_exact_r(jnp.broadcast_to(v_last, (8, 1024)), ET)[0:1, :]
        dacs = dacs + jnp.where(R["row"] == 127, dlast, 0.0)
        triu = (R["col"] >= R["row"]).astype(BF16)
        da = _mm_exact_l(triu, dacs)
        ddt = da * R["A"] + _mm(dX * xs_c, ET)
        ddt_raw = ddt * _sigmoid(R["raw"])
        dxs_c = dX * R["dt_e"] + dY * dsk
        dh_scr[...] = jnp.concatenate(dhprev, axis=1) + dhn * R["cd_e"]

        dact = jnp.concatenate([dxs_c] + dBs + dCs, axis=1)
        pre, sg = R["pre"], R["sg"]
        dpre = dact * (sg * (1.0 + pre * (1.0 - sg)))
        ext2_scr[0:8, :] = dpre[120:128, :]
        ext2_scr[8:16, :] = nxt_scr[...]
        nxt_scr[...] = dpre[0:8, :]
        cw = cw_ref[...]
        u_b, dpre_b = p_ref[:, S_XS:S_DT].astype(_MXU), dpre.astype(_MXU)
        dxbc = cw[3:4, :] * dpre
        taps = [_colsum(dpre * p_ref[:, S_XS:S_DT])]
        for s in (1, 2, 3):
            up = (R["col"] - R["row"] == s).astype(_MXU)
            d_s = jnp.concatenate([jnp.dot(up, dpre_b, preferred_element_type=F32)[0:120],
                                   ext2_scr[s:8 + s, :]], axis=0)
            dxbc = dxbc + cw[3 - s:4 - s, :] * d_s
            taps.append(_colsum(dpre * _rows_from_above(u_b, s, ext_scr, R["row"], R["col"])))
        acc_cw_ref[...] += _rows8(taps[::-1] + [_colsum(dpre)])
        acc_w_ref[...] += _rows8([jnp.concatenate(dnw_parts, axis=1), _colsum(dY * xs_c)])
        acc_s_ref[...] += _rows8([_colsum(ddt_raw), _colsum(da * R["dt"])])

        lane = lax.broadcasted_iota(jnp.int32, (128, 128), 1)
        dp_ref[:, S_Z:S_Z + 1024] = dz.astype(dp_ref.dtype)
        dp_ref[:, S_XS:S_DT] = dxbc.astype(dp_ref.dtype)
        dp_ref[:, S_DT:S_DT + 128] = jnp.where(lane < N_HEADS, ddt_raw, 0.0).astype(dp_ref.dtype)
        dp_ref[:, S_DT + 128:S_W] = jnp.zeros((128, 128), dp_ref.dtype)

        @pl.when(i == nc - 1)
        def _():
            acc = acc_s_ref[...]
            dskip = _mm_exact_r(acc_w_ref[...], ET)[1:2, :]
            acc_s_ref[...] = _rows8([acc[0:1, :], acc[1:2, :] * R["A"], dskip])

    const = lambda shape: pl.BlockSpec(shape, lambda i: (0, 0))
    smem = pl.BlockSpec(memory_space=pltpu.SMEM)
    rev = lambda i: (nc - 1 - i, 0)
    return _call(
        body, comm, name="ssd_bwd", grid=(nc,),
        in_specs=[pl.BlockSpec((CHUNK, S_W), rev),
                  pl.BlockSpec((8, S_W), lambda i: (jnp.maximum((nc - 1 - i) * 16 - 1, 0), 0)),
                  pl.BlockSpec((128, 1024), rev),
                  pl.BlockSpec((CHUNK, D_SSD), rev),
                  pl.BlockSpec((CHUNK, D_XBC), rev),
                  pl.BlockSpec((CHUNK, D_SSD), rev),
                  const((4, D_XBC)), const((1, D_XBC)), smem, smem, smem, const((1, 1024)),
                  const((128, 1024)), const((1024, 128))],
        out_specs=[pl.BlockSpec((CHUNK, S_W), rev), const((8, D_XBC)), const((8, 1024)), const((8, 128))],
        out_shape=[jax.ShapeDtypeStruct((L, S_W), _MXU), jax.ShapeDtypeStruct((8, D_XBC), F32),
                   jax.ShapeDtypeStruct((8, 1024), F32), jax.ShapeDtypeStruct((8, 128), F32)],
        scratch_shapes=[pltpu.VMEM((128, 1024), F32), pltpu.VMEM((16, D_XBC), F32),
                        pltpu.VMEM((16, D_XBC), F32), pltpu.VMEM((8, D_XBC), F32)],
        args=(proj_ssd, proj_ssd, hprev_all, ypre, pre, dy, conv_w, conv_b, dt_bias, a_log, d_skip, norm_w, E, ET))


def _rope(t, tab):
    cos, sa, sb = tab[:, 0:128], tab[:, 128:256], tab[:, 256:384]
    outs = []
    for i in range(t.shape[1] // 128):
        tg = t[:, i * 128:(i + 1) * 128]
        outs.append(tg * cos + pltpu.roll(tg, 8, 1) * sa + pltpu.roll(tg, 120, 1) * sb)
    return jnp.concatenate(outs, axis=1)


def _rope_transposed(d, tab):
    cos, sa, sb = tab[:, 0:128], tab[:, 128:256], tab[:, 256:384]
    outs = []
    for i in range(d.shape[1] // 128):
        dg = d[:, i * 128:(i + 1) * 128]
        outs.append(dg * cos + pltpu.roll(dg * sa, 120, 1) + pltpu.roll(dg * sb, 8, 1))
    return jnp.concatenate(outs, axis=1)


def _lo_half(rows):
    return lax.broadcasted_iota(jnp.int32, (rows, 128), 1) < HEAD_DIM


def _native_half(rows, j):
    lo = _lo_half(rows)
    return lo if j % 2 == 0 else jnp.logical_not(lo)


def _kv_native(t, j):
    p = j // 2
    return jnp.where(_native_half(t.shape[0], j), t[:, p * 128:(p + 1) * 128], 0.0)


def _stack_heads(t, j):
    out = []
    for m in (2 * j, 2 * j + 1):
        pair = t[:, m * 128:(m + 1) * 128]
        swapped = pltpu.roll(pair, HEAD_DIM, 1)
        out += [pair, swapped] if j % 2 == 0 else [swapped, pair]
    return jnp.concatenate(out, axis=0)


def _unstack_heads(s, j):
    out = []
    for m in range(2):
        first, second = s[256 * m:256 * m + 128], s[256 * m + 128:256 * m + 256]
        if j % 2 == 0:
            out.append(first + pltpu.roll(second, HEAD_DIM, 1))
        else:
            out.append(pltpu.roll(first, HEAD_DIM, 1) + second)
    return jnp.concatenate(out, axis=1)


def _keep_native(r, j):
    return jnp.where(_native_half(r.shape[0], j), r, 0.0)


def _sink_row(sink_ref, j):
    hid = lax.broadcasted_iota(jnp.int32, (1, 4 * CHUNK), 1) // CHUNK
    row = jnp.zeros((1, 4 * CHUNK), F32)
    for hh in range(4):
        row = jnp.where(hid == hh, sink_ref[4 * j + hh], row)
    return row


def _from_current():
    si = lax.broadcasted_iota(jnp.int32, (CHUNK, 4 * CHUNK), 0)
    qi = lax.broadcasted_iota(jnp.int32, (CHUNK, 4 * CHUNK), 1) % CHUNK
    return si <= qi


def _fold(full, from_cur, pen=0.0):
    return jnp.where(from_cur, full[CHUNK:2 * CHUNK], full[0:CHUNK] + pen)


def _unfold(t, from_cur):
    c = jnp.where(from_cur, t, 0.0)
    return jnp.concatenate([t - c, c], axis=0)


def _softmax_sink(s, sink):
    mx = jnp.maximum(jnp.max(s, axis=0, keepdims=True), sink)
    p = jnp.exp(s - mx)
    esink = jnp.exp(sink - mx)
    inv = 1.0 / (jnp.sum(p, axis=0, keepdims=True) + esink)
    return p * inv, esink * inv


def _swa_inputs(blk, p_ref, prev_ref, tab_ref, ptab_ref):
    tab = tab_ref[...]
    qr = _rope(p_ref[:, A_Q:A_Q + 1024], tab) * ATT_SCALE
    kk = jnp.concatenate([_rope(prev_ref[:, 0:256], ptab_ref[...]), _rope(p_ref[:, A_K:A_K + 256], tab)], axis=0)
    vv = jnp.concatenate([prev_ref[:, 256:512], p_ref[:, A_V:A_V + 256]], axis=0)
    return tab, qr, kk, vv, jnp.where(blk > 0, 0.0, NEG)


def _swa_forward_step(sink_ref, p_ref, prev_ref, tab_ref, ptab_ref, y_ref):
    n = pl.program_id(0)
    _, qr, kk, vv, pen = _swa_inputs(n, p_ref, prev_ref, tab_ref, ptab_ref)
    from_cur = _from_current()
    outs = []
    for j in range(KV_HEADS):
        s = _fold(_mm_nt(_kv_native(kk, j), _stack_heads(qr, j)), from_cur, pen)
        P, _ = _softmax_sink(s, _sink_row(sink_ref, j))
        outs.append(_unstack_heads(_mm_tn(_unfold(P, from_cur), _kv_native(vv, j)), j))
    g = p_ref[:, A_G:A_G + 1024]
    y_ref[:, D_SSD:D_SSD + D_ATT] = (jnp.concatenate(outs, axis=1) * (g * _sigmoid(g))).astype(y_ref.dtype)


def _mixer_forward(x, w_ssd, w_att, tabs, sinks, conv_w, conv_b, dt_bias, a_log, d_skip, norm_w, E, comm=None):
    L = x.shape[0]
    nc = L // CHUNK

    def body(x_ref, ws_ref, wa_ref, cw_ref, cb_ref, dtb_ref, alog_ref, dsk_ref, nw_ref, e_ref, sink_ref, tab_ref,
             ptab_ref, ps_ref, pa_ref, xb_ref, y_ref, ypre_ref, hprev_ref, pre_ref, h_scr, ext_scr, halo_scr, kv_scr):
        @pl.when(pl.program_id(0) == 0)
        def _():
            halo_scr[...] = jnp.zeros_like(halo_scr)
            kv_scr[...] = jnp.zeros_like(kv_scr)

        xb = x_ref[...].astype(_MXU)
        xb_ref[...] = xb
        ps_ref[...] = jnp.dot(xb, ws_ref[...], preferred_element_type=F32)
        pa_ref[...] = jnp.dot(xb, wa_ref[...], preferred_element_type=F32)
        _ssd_forward_step(ps_ref, halo_scr, cw_ref, cb_ref, dtb_ref, alog_ref, dsk_ref, nw_ref, e_ref,
                          y_ref, ypre_ref, hprev_ref, pre_ref, h_scr, ext_scr)
        _swa_forward_step(sink_ref, pa_ref, kv_scr, tab_ref, ptab_ref, y_ref)
        halo_scr[...] = ps_ref[CHUNK - 8:CHUNK, :]
        kv_scr[...] = pa_ref[:, A_K:A_K + 512]

    const = lambda shape: pl.BlockSpec(shape, lambda c: (0, 0))
    smem = pl.BlockSpec(memory_space=pltpu.SMEM)
    rows = lambda w: pl.BlockSpec((CHUNK, w), lambda c: (c, 0))
    return _call(
        body, comm, name="mixer_fwd", grid=(nc,),
        in_specs=[rows(D_MODEL), const((D_MODEL, S_W)), const((D_MODEL, A_W)),
                  const((4, D_XBC)), const((1, D_XBC)), smem, smem, smem, const((1, 1024)), const((128, 1024)),
                  smem, rows(384), pl.BlockSpec((CHUNK, 384), lambda c: (jnp.maximum(c - 1, 0), 0))],
        out_specs=[rows(S_W), rows(A_W), rows(D_MODEL), rows(D_SSD + D_ATT), rows(D_SSD),
                   pl.BlockSpec((128, 1024), lambda c: (c, 0)), rows(D_XBC)],
        out_shape=[jax.ShapeDtypeStruct((L, S_W), F32), jax.ShapeDtypeStruct((L, A_W), F32),
                   jax.ShapeDtypeStruct((L, D_MODEL), _MXU), jax.ShapeDtypeStruct((L, D_SSD + D_ATT), _MXU),
                   jax.ShapeDtypeStruct((L, D_SSD), F32), jax.ShapeDtypeStruct((nc * 128, 1024), F32),
                   jax.ShapeDtypeStruct((L, D_XBC), F32)],
        scratch_shapes=[pltpu.VMEM((128, 1024), F32), pltpu.VMEM((136, D_XBC), F32),
                        pltpu.VMEM((8, S_W), F32), pltpu.VMEM((CHUNK, 512), F32)],
        args=(x, w_ssd, w_att, conv_w, conv_b, dt_bias, a_log, d_skip, norm_w, E, sinks, tabs, tabs))


def _swa_backward(proj_att, tabs, sinks, dy, comm=None):
    L = proj_att.shape[0]
    nb = L // CHUNK

    def body(sink_ref, p_ref, prev_ref, tab_ref, ptab_ref, dy_ref, dp_ref, dsink_ref, carry_k, carry_v):
        i = pl.program_id(0)
        n = nb - 1 - i

        @pl.when(i == 0)
        def _():
            carry_k[...] = jnp.zeros_like(carry_k)
            carry_v[...] = jnp.zeros_like(carry_v)
            dsink_ref[...] = jnp.zeros_like(dsink_ref)

        tab, qr, kk, vv, pen = _swa_inputs(n, p_ref, prev_ref, tab_ref, ptab_ref)
        from_cur = _from_current()
        g = p_ref[:, A_G:A_G + 1024]
        sgm = _sigmoid(g)
        dyv = dy_ref[...]
        do_all = dyv * (g * sgm)
        lane8 = lax.broadcasted_iota(jnp.int32, (8, 128), 1)
        hid = lax.broadcasted_iota(jnp.int32, (1, 4 * CHUNK), 1) // CHUNK
        o_parts, dq_parts = [], []
        dk_nat = [jnp.zeros((2 * CHUNK, 128), F32) for _ in range(2)]
        dv_nat = [jnp.zeros((2 * CHUNK, 128), F32) for _ in range(2)]
        dsink = jnp.zeros((8, 128), F32)
        for j in range(KV_HEADS):
            qs = _stack_heads(qr, j)
            kkb, vvb = _kv_native(kk, j), _kv_native(vv, j)
            P, psink = _softmax_sink(_fold(_mm_nt(kkb, qs), from_cur, pen), _sink_row(sink_ref, j))
            p_full = _unfold(P, from_cur)
            o_parts.append(_unstack_heads(_mm_tn(p_full, vvb), j))
            do_s = _stack_heads(do_all, j)
            dP = _fold(_mm_nt(vvb, do_s), from_cur)
            D = jnp.sum(P * dP, axis=0, keepdims=True)
            ds_full = _unfold(P * (dP - D), from_cur)
            sd = psink * D
            for hh in range(4):
                dsink = dsink + jnp.where(lane8 == 4 * j + hh, -jnp.sum(jnp.where(hid == hh, sd, 0.0)), 0.0)
            dq_parts.append(_unstack_heads(_mm_tn(ds_full, kkb), j) * ATT_SCALE)
            dk_nat[j // 2] = dk_nat[j // 2] + _keep_native(_mm(ds_full, qs), j)
            dv_nat[j // 2] = dv_nat[j // 2] + _keep_native(_mm(p_full, do_s), j)
        o = jnp.concatenate(o_parts, axis=1)
        dkk = jnp.concatenate(dk_nat, axis=1)
        dvv = jnp.concatenate(dv_nat, axis=1)
        out = dp_ref.dtype
        dp_ref[:, A_Q:A_Q + 1024] = _rope_transposed(jnp.concatenate(dq_parts, axis=1), tab).astype(out)
        dp_ref[:, A_K:A_K + 256] = _rope_transposed(dkk[CHUNK:2 * CHUNK] + carry_k[...], tab).astype(out)
        dp_ref[:, A_V:A_V + 256] = (dvv[CHUNK:2 * CHUNK] + carry_v[...]).astype(out)
        dp_ref[:, A_G:A_G + 1024] = (dyv * o * (sgm * (1.0 + g * (1.0 - sgm)))).astype(out)
        carry_k[...] = dkk[0:CHUNK]
        carry_v[...] = dvv[0:CHUNK]
        dsink_ref[...] += dsink

    rev = lambda i: (nb - 1 - i, 0)
    prev = lambda i: jnp.maximum(nb - 2 - i, 0)
    return _call(
        body, comm, name="swa_bwd", grid=(nb,),
        in_specs=[pl.BlockSpec(memory_space=pltpu.SMEM),
                  pl.BlockSpec((CHUNK, A_W), rev),
                  pl.BlockSpec((CHUNK, 512), lambda i: (prev(i), 2)),
                  pl.BlockSpec((CHUNK, 384), rev),
                  pl.BlockSpec((CHUNK, 384), lambda i: (prev(i), 0)),
                  pl.BlockSpec((CHUNK, D_ATT), lambda i: (nb - 1 - i, 1))],
        out_specs=[pl.BlockSpec((CHUNK, A_W), rev), pl.BlockSpec((8, 128), lambda i: (0, 0))],
        out_shape=[jax.ShapeDtypeStruct((L, A_W), _MXU), jax.ShapeDtypeStruct((8, 128), F32)],
        scratch_shapes=[pltpu.VMEM((CHUNK, 256), F32), pltpu.VMEM((CHUNK, 256), F32)],
        args=(sinks, proj_att, proj_att, tabs, tabs, dy))


def _head(y, x, target, w_out, ln_g, ln_b, *, tm):
    L = x.shape[0]
    nsteps = L // tm

    def body(y_ref, x_ref, t_ref, wo_ref, g_ref, b_ref, dr_ref, dy_ref, acc_ref):
        i = pl.program_id(0)

        @pl.when(i == 0)
        def _():
            acc_ref[...] = jnp.zeros_like(acc_ref)

        r = ALPHA * x_ref[...] + _mm(y_ref[...], wo_ref[...])
        mu = jnp.mean(r, axis=-1, keepdims=True)
        d = r - mu
        rstd = lax.rsqrt(jnp.mean(d * d, axis=-1, keepdims=True) + LN_EPS)
        xh = d * rstd
        gam = g_ref[0:1, :]
        e = xh * gam + b_ref[0:1, :] - t_ref[...]
        dout = e * (1.0 / D_MODEL)
        dxh = dout * gam
        dr = rstd * (dxh - jnp.mean(dxh, axis=-1, keepdims=True)
                     - xh * jnp.mean(dxh * xh, axis=-1, keepdims=True))
        dr_ref[...] = dr
        dy_ref[...] = _mm_nt(dr, wo_ref[...])
        acc_ref[...] += _rows8([_colsum(dout * xh), _colsum(dout), _colsum(e * e) * (0.5 / D_MODEL)])

        @pl.when(i == nsteps - 1)
        def _():
            acc = acc_ref[...]
            tot = jnp.sum(acc[2:3, :])
            rid = lax.broadcasted_iota(jnp.int32, (8, 1024), 0)
            acc_ref[...] = jnp.where(rid == 3, tot, acc)

    const = lambda shape: pl.BlockSpec(shape, lambda i: (0, 0))
    row = lambda w: pl.BlockSpec((tm, w), lambda i: (i, 0))
    return pl.pallas_call(
        body, name="head", grid=(nsteps,),
        in_specs=[row(2048), row(1024), row(1024), const((2048, 1024)), const((1, 1024)), const((1, 1024))],
        out_specs=[row(1024), row(2048), const((8, 1024))],
        out_shape=[jax.ShapeDtypeStruct((L, D_MODEL), F32), jax.ShapeDtypeStruct((L, 2048), F32),
                   jax.ShapeDtypeStruct((8, 1024), F32)],
        compiler_params=_params(("arbitrary",)),
    )(y, x, target, w_out, ln_g, ln_b)


def _gather_w_in(w_shard):
    R = w_shard.shape[0]
    halves = (pl.ds(0, R // 2), pl.ds(R // 2, R // 2))
    any_spec = pl.BlockSpec(memory_space=pl.ANY)

    def body(in_ref, out_ref, send_sems, recv_sems, local_sem):
        x, y, c = _position()

        def slot(p, half=None):
            s = out_ref.at[_index(*p)]
            return s if half is None else s.at[halves[half]]

        def same_core(p):
            return (p[0], p[1], c)

        def other_core(p):
            return (p[0], p[1], 1 - c)

        me, xn, yn, dg = (x, y), (1 - x, y), (x, 1 - y), (1 - x, 1 - y)

        def copy(k, dst, to, src=None):
            return _remote(dst if src is None else src, dst, send_sems.at[k], recv_sems.at[k], to)

        local = pltpu.make_async_copy(in_ref, slot(same_core(me)), local_sem)
        local.start()
        own = [copy(0, slot(same_core(me)), other_core(me), in_ref), copy(1, slot(same_core(me)), same_core(xn), in_ref),
               copy(2, slot(same_core(me)), same_core(yn), in_ref)]
        for cp in own:
            cp.start()
        copy(1, slot(same_core(xn)), same_core(xn)).wait_recv()
        passed = [copy(4, slot(same_core(xn), 1), same_core(yn)), copy(5, slot(same_core(xn)), other_core(me))]
        for cp in passed:
            cp.start()
        copy(2, slot(same_core(yn)), same_core(yn)).wait_recv()
        more = [copy(3, slot(same_core(yn), 0), same_core(xn)), copy(6, slot(same_core(yn)), other_core(me))]
        for cp in more:
            cp.start()
        passed += more
        for k, half in ((3, 0), (4, 1)):
            copy(k, slot(same_core(dg), half), same_core(xn)).wait_recv()
            fwd = copy(7 + half, slot(same_core(dg), half), other_core(me))
            fwd.start()
            passed.append(fwd)
        copy(0, slot(other_core(me)), other_core(me)).wait_recv()
        copy(5, slot(other_core(xn)), other_core(me)).wait_recv()
        copy(6, slot(other_core(yn)), other_core(me)).wait_recv()
        for half in (0, 1):
            copy(7 + half, slot(other_core(dg), half), other_core(me)).wait_recv()
        for cp in own + passed:
            cp.wait_send()
        local.wait()

    return pl.pallas_call(
        body, name="gather_w_in", in_specs=[any_spec], out_specs=any_spec,
        out_shape=jax.ShapeDtypeStruct((N_DEV,) + w_shard.shape, w_shard.dtype),
        scratch_shapes=[pltpu.SemaphoreType.DMA((9,)), pltpu.SemaphoreType.DMA((9,)), pltpu.SemaphoreType.DMA],
    )(w_shard)


def _input_gradient(d_ssd, d_att, w_ssd, w_att, dr, *, tm, comm=None):
    L = dr.shape[0]

    def body(ds_ref, da_ref, ws_ref, wa_ref, dr_ref, o_ref):
        o_ref[...] = ALPHA * dr_ref[...] + _mm_nt(ds_ref[...], ws_ref[...]) + _mm_nt(da_ref[...], wa_ref[...])

    row = lambda w: pl.BlockSpec((tm, w), lambda i: (i, 0))
    const = lambda shape: pl.BlockSpec(shape, lambda i: (0, 0))
    return _call(body, comm, name="dx", grid=(L // tm,),
                 in_specs=[row(S_W), row(A_W), const((D_MODEL, S_W)), const((D_MODEL, A_W)), row(D_MODEL)],
                 out_specs=[row(D_MODEL)], out_shape=[jax.ShapeDtypeStruct((L, D_MODEL), F32)],
                 scratch_shapes=[], args=(d_ssd, d_att, w_ssd, w_att, dr))


SHARD_COLS = D_IN_PROJ // N_DEV
SPLIT = N_SSD_REAL - 4 * SHARD_COLS
RELAYOUT_ROWS = 256


def _unpack_w_in(w_all, comm=None):
    def body(g_ref, ws_ref, wa_ref):
        for j in range(4):
            ws_ref[:, SHARD_COLS * j:SHARD_COLS * (j + 1)] = g_ref[j]
        ws_ref[:, 4 * SHARD_COLS:N_SSD_REAL] = g_ref[4, :, 0:SPLIT]
        ws_ref[:, N_SSD_REAL:S_W] = jnp.zeros((RELAYOUT_ROWS, S_W - N_SSD_REAL), ws_ref.dtype)
        wa_ref[:, 0:SHARD_COLS - SPLIT] = g_ref[4, :, SPLIT:SHARD_COLS]
        for j in range(5, N_DEV):
            lo = SHARD_COLS * (j - 4) - SPLIT
            wa_ref[:, lo:lo + SHARD_COLS] = g_ref[j]

    return _call(
        body, comm, name="unpack_w_in", grid=(D_MODEL // RELAYOUT_ROWS,),
        in_specs=[pl.BlockSpec((N_DEV, RELAYOUT_ROWS, SHARD_COLS), lambda i: (0, i, 0))],
        out_specs=[pl.BlockSpec((RELAYOUT_ROWS, S_W), lambda i: (i, 0)), pl.BlockSpec((RELAYOUT_ROWS, A_W), lambda i: (i, 0))],
        out_shape=[jax.ShapeDtypeStruct((D_MODEL, S_W), w_all.dtype), jax.ShapeDtypeStruct((D_MODEL, A_W), w_all.dtype)],
        scratch_shapes=[], args=(w_all,))


def _pack_dw_in(me1, dw_ssd, dw_att, half):
    def body(me_ref, *refs):
        if half == 0:
            ds_ref, p_ref, own_ref = refs
            me = me_ref[0]

            @pl.when(me >= 4)
            def _():
                own_ref[...] = jnp.zeros_like(own_ref)
        else:
            ds_ref, da_ref, p_ref = refs

        for j in range(4):
            if half == 0:
                pieces = [(0, ds_ref[:, SHARD_COLS * j:SHARD_COLS * (j + 1)])]
            elif j == 0:
                pieces = [(0, ds_ref[:, 4 * SHARD_COLS - S_DT:N_SSD_REAL - S_DT]), (SPLIT, da_ref[:, 0:SHARD_COLS - SPLIT])]
            else:
                lo = SHARD_COLS * j - SPLIT
                pieces = [(0, da_ref[:, lo:lo + SHARD_COLS])]
            for off, blk in pieces:
                p_ref[j, :, off:off + blk.shape[1]] = blk.astype(p_ref.dtype)
                if half == 0:
                    @pl.when(me == j)
                    def _(off=off, blk=blk):
                        own_ref[:, off:off + blk.shape[1]] = blk

    ins = [dw_ssd] if half == 0 else [dw_ssd, dw_att]
    row = lambda a: pl.BlockSpec((RELAYOUT_ROWS, a.shape[1]), lambda i: (i, 0))
    in_specs = [row(a) for a in ins]
    if half == 1:
        in_specs[0] = pl.BlockSpec((RELAYOUT_ROWS, S_W - S_DT), lambda i: (i, S_DT // (S_W - S_DT)))
    out_specs = [pl.BlockSpec((4, RELAYOUT_ROWS, SHARD_COLS), lambda i: (0, i, 0))]
    out_shape = [jax.ShapeDtypeStruct((4, D_MODEL, SHARD_COLS), BF16 if half == 0 else F32)]
    if half == 0:
        out_specs.append(pl.BlockSpec((RELAYOUT_ROWS, SHARD_COLS), lambda i: (i, 0)))
        out_shape.append(jax.ShapeDtypeStruct((D_MODEL, SHARD_COLS), F32))
    return pl.pallas_call(
        body, name="pack_dw_in_%d" % half, grid=(D_MODEL // RELAYOUT_ROWS,),
        in_specs=[pl.BlockSpec(memory_space=pltpu.SMEM)] + in_specs,
        out_specs=out_specs, out_shape=out_shape, compiler_params=_params(("arbitrary",)),
    )(me1, *ins)


def _pair_swap(stack):
    def body(in_ref, out_ref, send_sems, recv_sems):
        x, y, c = _position()
        cps = [_remote(in_ref.at[2 * oy + (1 - c)], out_ref.at[oy], send_sems.at[oy], recv_sems.at[oy], (x, y, 1 - c))
               for oy in range(2)]
        for cp in cps:
            cp.start()
        for cp in cps:
            cp.wait_recv()
        for cp in cps:
            cp.wait_send()

    any_spec = pl.BlockSpec(memory_space=pl.ANY)
    return pl.pallas_call(
        body, name="pair_swap", in_specs=[any_spec], out_specs=any_spec,
        out_shape=jax.ShapeDtypeStruct((2,) + stack.shape[1:], stack.dtype),
        scratch_shapes=[pltpu.SemaphoreType.DMA((2,)), pltpu.SemaphoreType.DMA((2,))],
    )(stack)


def _pair_sum(pos3, stack, swapped, own_lo):
    def body(pos_ref, a_ref, b_ref, lo_ref, chip_ref, own_ref):
        oy = pl.program_id(1)
        t = a_ref[0] + b_ref[0]
        chip_ref[0] = t.astype(chip_ref.dtype)

        @pl.when((pos_ref[0] == 0) & (oy == 0))
        def _():
            own_ref[...] = lo_ref[...]

        @pl.when((pos_ref[0] == 1) & (oy == pos_ref[1]))
        def _():
            own_ref[...] = t

    blk = (1, RELAYOUT_ROWS, SHARD_COLS)
    flat = pl.BlockSpec((RELAYOUT_ROWS, SHARD_COLS), lambda i, oy, pos: (i, 0))
    return pl.pallas_call(
        body, name="pair_sum",
        grid_spec=pltpu.PrefetchScalarGridSpec(
            num_scalar_prefetch=1, grid=(D_MODEL // RELAYOUT_ROWS, 2),
            in_specs=[pl.BlockSpec(blk, lambda i, oy, pos: (2 * oy + pos[2], i, 0)),
                      pl.BlockSpec(blk, lambda i, oy, pos: (oy, i, 0)), flat],
            out_specs=[pl.BlockSpec(blk, lambda i, oy, pos: (oy, i, 0)), flat]),
        out_shape=[jax.ShapeDtypeStruct((2, D_MODEL, SHARD_COLS), BF16), jax.ShapeDtypeStruct((D_MODEL, SHARD_COLS), F32)],
        compiler_params=_params(("arbitrary", "arbitrary")),
    )(pos3, stack, swapped, own_lo)


def _adamw_math(w, g, m, v):
    m = ADAM_B1 * m + (1.0 - ADAM_B1) * g
    v = ADAM_B2 * v + (1.0 - ADAM_B2) * (g * g)
    m_hat = m / (1.0 - ADAM_B1 ** ADAM_STEP)
    v_hat = v / (1.0 - ADAM_B2 ** ADAM_STEP)
    delta = -ADAM_LR * (m_hat / (jnp.sqrt(v_hat) + ADAM_EPS) + ADAM_WD * w)
    return delta, m, v


def _adamw_shard(n_recv, g_own, recv, w, m, v, *, rows, name):
    R, C = g_own.shape

    def body(n_ref, g_ref, r_ref, w_ref, m_ref, v_ref, go_ref, d_ref, mo_ref, vo_ref):
        g = g_ref[...]
        for k in range(N_DEV - 1):
            g = g + jnp.where(k < n_ref[0], r_ref[k].astype(F32), 0.0)
        d, mn, vn = _adamw_math(w_ref[...], g, m_ref[...], v_ref[...])
        go_ref[...] = g
        d_ref[...] = d
        mo_ref[...] = mn
        vo_ref[...] = vn

    blk = pl.BlockSpec((rows, C), lambda i: (i, 0))
    return pl.pallas_call(
        body, name=name, grid=(R // rows,),
        in_specs=[pl.BlockSpec(memory_space=pltpu.SMEM), blk,
                  pl.BlockSpec((N_DEV - 1, rows, C), lambda i: (0, i, 0)), blk, blk, blk],
        out_specs=[blk] * 4, out_shape=[jax.ShapeDtypeStruct((R, C), F32)] * 4,
        compiler_params=_params(("arbitrary",)),
    )(n_recv, g_own, recv, w, m, v)


def _minor_rows_view(a):
    return jnp.transpose(a, (2, 0, 1)).reshape(SHARD_COLS * 8, 128)


def _from_minor_rows_view(v):
    return jnp.transpose(v.reshape(SHARD_COLS, 8, 128), (1, 2, 0)).reshape(1, D_MODEL, SHARD_COLS)


def _adamw_w_in(n_recv, g_own, recv, w, m, v):
    C = SHARD_COLS
    pad = -C % 128

    def body(n_ref, g_ref, r_ref, w_ref, m_ref, v_ref, go_ref, d_ref, mo_ref, vo_ref):
        for q in range(D_MODEL // 128):
            band = pl.ds(q * 128, 128)
            g = g_ref[band, :]
            for k in range(N_DEV - 1):
                g = g + jnp.where(k < n_ref[0], r_ref[k, band, :].astype(F32), 0.0)
            g = jnp.pad(g, ((0, 0), (0, pad))).T[0:C]
            rows = pl.ds(q, C, stride=8)
            d, mn, vn = _adamw_math(w_ref[rows, :], g, m_ref[rows, :], v_ref[rows, :])
            go_ref[rows, :] = g
            d_ref[rows, :] = d
            mo_ref[rows, :] = mn
            vo_ref[rows, :] = vn

    return pl.pallas_call(
        body, name="adamw_w_in", out_shape=[jax.ShapeDtypeStruct(w.shape, F32)] * 4,
        in_specs=[pl.BlockSpec(memory_space=pltpu.SMEM)] + [pl.BlockSpec(memory_space=pltpu.VMEM)] * 5,
        out_specs=[pl.BlockSpec(memory_space=pltpu.VMEM)] * 4,
        compiler_params=_params(),
    )(n_recv, g_own, recv, w, m, v)


SMALL = ("conv_b", "dt_bias", "a_log", "d_skip", "ssd_norm_w", "attn_sinks", "ln_g", "ln_b")


def _adamw_small(gathered, params):
    n_p = len(SMALL)

    def body(*refs):
        acc = []
        for r in refs[:5]:
            t = r[0]
            for k in range(1, N_DEV):
                t = t + r[k]
            acc.append(t)
        head, conv, norm, scal, sink = acc
        grads = dict(conv_b=conv[4:5, :], dt_bias=scal[0:1, 0:N_HEADS], a_log=scal[1:2, 0:N_HEADS],
                     d_skip=scal[2:3, 0:N_HEADS], ssd_norm_w=norm[0:1, :], attn_sinks=sink[0:1, 0:N_HEADS],
                     ln_g=head[0:1, :], ln_b=head[1:2, :])
        wmv = refs[5:5 + 3 * n_p]
        outs = refs[5 + 3 * n_p:]
        outs[0][...] = head[3:4, 0:1]
        outs[1][...] = conv[0:4, :]
        for i, name in enumerate(SMALL):
            w_ref, m_ref, v_ref = wmv[3 * i:3 * i + 3]
            g = grads[name]
            d, mn, vn = _adamw_math(w_ref[...], g, m_ref[...], v_ref[...])
            for o_ref, val in zip(outs[2 + 4 * i:6 + 4 * i], (g, d, mn, vn)):
                o_ref[...] = val

    flat = [a for name in SMALL for a in params[name]]
    out_shape = [jax.ShapeDtypeStruct((1, 1), F32), jax.ShapeDtypeStruct((4, D_XBC), F32)]
    for name in SMALL:
        out_shape += [jax.ShapeDtypeStruct(params[name][0].shape, F32)] * 4
    res = pl.pallas_call(body, name="adamw_small", out_shape=out_shape, compiler_params=_params())(*gathered, *flat)
    return res[0], res[1], {name: res[2 + 4 * i:6 + 4 * i] for i, name in enumerate(SMALL)}


def _adamw_plain(g, w, m, v):
    def body(g_ref, w_ref, m_ref, v_ref, d_ref, mo_ref, vo_ref):
        d, mn, vn = _adamw_math(w_ref[...], g_ref[...], m_ref[...], v_ref[...])
        d_ref[...] = d
        mo_ref[...] = mn
        vo_ref[...] = vn

    return pl.pallas_call(
        body, name="adamw_conv_w", out_shape=[jax.ShapeDtypeStruct(w.shape, F32)] * 3,
        compiler_params=_params(),
    )(g, w, m, v)


def _lane_pattern(fn):
    return np.asarray([fn(l % HEAD_DIM) for l in range(128)], np.float32)


ROPE_INV = _lane_pattern(lambda r: ROPE_THETA ** (-2.0 * (r % 8) / ROPE_DIM) if r < ROPE_DIM else 0.0)
ROPE_SIN_A = _lane_pattern(lambda r: 1.0 if 8 <= r < ROPE_DIM else 0.0)
ROPE_SIN_B = _lane_pattern(lambda r: -1.0 if r < 8 else 0.0)


def _rope_tables(positions):
    ang = positions.astype(F32)[:, None] * ROPE_INV[None, :]
    sn = jnp.sin(ang)
    return jnp.concatenate([jnp.cos(ang), sn * ROPE_SIN_A[None, :], sn * ROPE_SIN_B[None, :]], axis=1)


def _expansion():
    E = np.arange(1024)[None, :] // HEAD_DIM == np.arange(128)[:, None]
    return jnp.asarray(E, BF16), jnp.asarray(E.T, BF16)


def _ssd_args(conv_w, conv_b, dt_bias, a_log, d_skip, norm_w, E):
    return (conv_w, conv_b, dt_bias.reshape(-1), a_log.reshape(-1), d_skip.reshape(-1), norm_w, E)


def kernel(x, positions, w_in, conv_w, conv_b, dt_bias, a_log, d_skip, ssd_norm_w, attn_sinks, w_out, ln_g, ln_b, loss_target, m_w_in, m_conv_w, m_conv_b, m_dt_bias, m_a_log, m_d_skip, m_ssd_norm_w, m_attn_sinks, m_w_out, m_ln_g, m_ln_b, v_w_in, v_conv_w, v_conv_b, v_dt_bias, v_a_log, v_d_skip, v_ssd_norm_w, v_attn_sinks, v_w_out, v_ln_g, v_ln_b):
    me = _index(*_position())
    me1 = me.reshape(1).astype(jnp.int32)
    x0, target = x[0], loss_target[0]
    bf16_shard = lambda shape: jax.ShapeDtypeStruct(shape, BF16)
    E, ET = _expansion()
    tabs = _rope_tables(positions[0])
    sinks = attn_sinks.reshape(-1)

    gather_conv_w = _Hosted([conv_w[0]], [jax.ShapeDtypeStruct((N_DEV,) + conv_w.shape[1:], F32)],
                            [_Flow("gather", 0, 0)])
    w_ssd, w_att, conv_w_all = _unpack_w_in(_gather_w_in(w_in[0].astype(BF16)), comm=gather_conv_w)
    conv_w_f = jnp.transpose(conv_w_all, (1, 0, 2)).reshape(4, D_XBC)
    ssd_args = _ssd_args(conv_w_f, conv_b, dt_bias, a_log, d_skip, ssd_norm_w, E)

    gather_w_out = _Hosted([w_out[0].astype(BF16)], [bf16_shard((N_DEV, 256, D_MODEL))], [_Flow("gather", 0, 0)])
    proj_ssd, proj_att, xb, y, ypre, hprev, pre, w_out_all = _mixer_forward(
        x0, w_ssd, w_att, tabs, sinks, *ssd_args, comm=gather_w_out)
    w_out_f = w_out_all.reshape(2 * D_MODEL, D_MODEL)
    dr, dy, acc_head = _head(y, x0, target, w_out_f, ln_g, ln_b, tm=512)

    dw_out, dw_out_bf16 = _matmul_tn(y, dr, tl=1024, tn=D_MODEL, name="dw_out", emit_bf16=True)
    own_out = lax.dynamic_index_in_dim(dw_out.reshape(N_DEV, 256, D_MODEL), me, axis=0, keepdims=False)
    send_out = _Hosted([dw_out_bf16.reshape(N_DEV, 256, D_MODEL)], [bf16_shard((N_DEV - 1, 256, D_MODEL))],
                       [_Flow("exchange", 0, 0)])
    d_ssd, acc_cw, acc_w, acc_s, recv_out = _ssd_backward(proj_ssd, hprev, ypre, pre, dy, *ssd_args, ET, comm=send_out)
    dw_ssd = _matmul_tn(xb, d_ssd, tl=2048, tn=S_W // 2, name="dw_in_ssd")
    parts_lo, own_lo = _pack_dw_in(me1, dw_ssd, None, 0)
    recv_shape = bf16_shard((N_DEV - 1, D_MODEL, SHARD_COLS))
    send_lo = _Hosted([parts_lo], [recv_shape], [_Flow("exchange", 0, 0, target_x=0, target_c=0)])
    d_att, dsink, recv_in = _swa_backward(proj_att, tabs, sinks, dy, comm=send_lo)
    dw_att = _matmul_tn(xb, d_att, tl=2048, tn=A_W // 2, name="dw_in_att")
    (stack_hi,) = _pack_dw_in(me1, dw_ssd, dw_att, 1)
    pos3 = jnp.stack(_position()).astype(jnp.int32)
    chip_hi, own_in = _pair_sum(pos3, stack_hi, _pair_swap(stack_hi), own_lo)
    accs = [acc_head, acc_cw, acc_w, acc_s, dsink]
    send_hi = _Hosted([chip_hi, recv_in, parts_lo] + accs,
                      [recv_shape] + [jax.ShapeDtypeStruct((N_DEV,) + a.shape, F32) for a in accs],
                      [_Flow("chip_exchange", 0, 0, target_x=1), _Flow("exchange", 2, 0, target_x=0, target_c=1)]
                      + [_Flow("gather", 3 + i, 1 + i) for i in range(5)], aliases={1: 0})
    dx, recv_in, *gathered = _input_gradient(d_ssd, d_att, w_ssd, w_att, dr, tm=256, comm=send_hi)
    n_recv_in = jnp.where(me < 4, N_DEV - 1, 3).reshape(1).astype(jnp.int32)
    n_recv_out = jnp.full((1,), N_DEV - 1, jnp.int32)

    g_in, d_in, nm_in, nv_in = [_from_minor_rows_view(r) for r in _adamw_w_in(
        n_recv_in, own_in, recv_in, _minor_rows_view(w_in), _minor_rows_view(m_w_in), _minor_rows_view(v_w_in))]
    g_out, d_out, nm_out, nv_out = _adamw_shard(n_recv_out, own_out, recv_out, w_out[0], m_w_out[0], v_w_out[0],
                                                rows=256, name="adamw_w_out")
    loss, g_conv_w, small = _adamw_small(gathered, dict(
        conv_b=(conv_b, m_conv_b, v_conv_b), dt_bias=(dt_bias, m_dt_bias, v_dt_bias), a_log=(a_log, m_a_log, v_a_log),
        d_skip=(d_skip, m_d_skip, v_d_skip), ssd_norm_w=(ssd_norm_w, m_ssd_norm_w, v_ssd_norm_w),
        attn_sinks=(attn_sinks, m_attn_sinks, v_attn_sinks), ln_g=(ln_g, m_ln_g, v_ln_g), ln_b=(ln_b, m_ln_b, v_ln_b)))
    g_cw = lax.dynamic_slice_in_dim(g_conv_w, me * (D_XBC // N_DEV), D_XBC // N_DEV, axis=1)
    d_cw, nm_cw, nv_cw = _adamw_plain(g_cw, conv_w[0], m_conv_w[0], v_conv_w[0])

    def leaves(i, big_in, cw, big_out):
        mid = [small[k][i] for k in ("conv_b", "dt_bias", "a_log", "d_skip", "ssd_norm_w", "attn_sinks")]
        return [big_in, cw[None]] + mid + [big_out[None], small["ln_g"][i], small["ln_b"][i]]

    return (loss.reshape(()), dx[None], *leaves(0, g_in, g_cw, g_out), *leaves(1, d_in, d_cw, d_out),
            *leaves(2, nm_in, nm_cw, nm_out), *leaves(3, nv_in, nv_cw, nv_out))
```

```python
import jax
import jax.numpy as jnp
from jax import lax
from jax.experimental import pallas as pl
from jax.experimental.pallas import tpu as pltpu
import numpy as np

F32 = jnp.float32
BF16 = jnp.bfloat16
_MXU = jnp.bfloat16

N_DEV = 8
D_MODEL = 1024
D_SSD = 1024
D_ATT = 1024
HEAD_DIM = 64
N_HEADS = 16
SSD_GROUPS = 2
KV_HEADS = 4
CHUNK = 128
D_XBC = 1536
D_IN_PROJ = 5136
ROPE_DIM = 16
ROPE_THETA = 500000.0
ALPHA = (2.0 * 1) ** 0.25
LN_EPS = 1e-5
RMS_EPS = 1e-5
ATT_SCALE = HEAD_DIM ** -0.5
NEG = -1e30

S_Z, S_XS, S_B, S_C, S_DT, S_W = 0, 1024, 2048, 2304, 2560, 2816
N_SSD_REAL = 2576
A_Q, A_K, A_V, A_G, A_W = 0, 1024, 1280, 1536, 2560

ADAM_LR = 0.001
ADAM_B1 = 0.9
ADAM_B2 = 0.999
ADAM_EPS = 1e-08
ADAM_WD = 0.01
ADAM_STEP = 10

VMEM_LIMIT = 48 * 1024 * 1024
MESH = pl.DeviceIdType.MESH


def _params(sem=None):
    return pltpu.CompilerParams(dimension_semantics=sem, vmem_limit_bytes=VMEM_LIMIT)


def _mm(a, b):
    return jnp.dot(a.astype(_MXU), b.astype(_MXU), preferred_element_type=F32)


def _mm_nt(a, b):
    return lax.dot_general(a.astype(_MXU), b.astype(_MXU), (((1,), (1,)), ((), ())),
                           preferred_element_type=F32)


def _mm_tn(a, b):
    return lax.dot_general(a.astype(_MXU), b.astype(_MXU), (((0,), (0,)), ((), ())),
                           preferred_element_type=F32)


def _split3(v):
    hi = v.astype(BF16)
    r = v - hi.astype(F32)
    mid = r.astype(BF16)
    lo = (r - mid.astype(F32)).astype(BF16)
    return hi, mid, lo


def _mm_exact_r(v, p01):
    hi, mid, lo = _split3(v)
    d = lambda a: jnp.dot(a, p01, preferred_element_type=F32)
    return d(hi) + d(mid) + d(lo)


def _mm_exact_l(p01, v):
    hi, mid, lo = _split3(v)
    d = lambda a: jnp.dot(p01, a, preferred_element_type=F32)
    return d(hi) + d(mid) + d(lo)


def _mm_2pass_r(v, p01):
    hi = v.astype(BF16)
    lo = (v - hi.astype(F32)).astype(BF16)
    return jnp.dot(hi, p01, preferred_element_type=F32) + jnp.dot(lo, p01, preferred_element_type=F32)


def _sigmoid(x):
    return 1.0 / (1.0 + jnp.exp(-x))


def _softplus(x):
    e = jnp.exp(-jnp.abs(x))
    u = 1.0 + e
    log1p = jnp.where(u == 1.0, e, jnp.log(u) * (e / (u - 1.0)))
    return jnp.maximum(x, 0.0) + log1p


def _rows8(rows):
    n = rows[0].shape[1]
    rid = lax.broadcasted_iota(jnp.int32, (8, n), 0)
    out = jnp.zeros((8, n), F32)
    for k, r in enumerate(rows):
        out = out + jnp.where(rid == k, r, 0.0)
    return out


def _colsum(a):
    return jnp.sum(a, axis=0, keepdims=True)


def _in_proj(x, w_ssd, w_att, *, tm, comm=None):
    L, K = x.shape

    def body(x_ref, ws_ref, wa_ref, ps_ref, pa_ref, xb_ref):
        xb = x_ref[...].astype(_MXU)
        xb_ref[...] = xb
        ps_ref[...] = jnp.dot(xb, ws_ref[...], preferred_element_type=F32)
        pa_ref[...] = jnp.dot(xb, wa_ref[...], preferred_element_type=F32)

    row = lambda w: pl.BlockSpec((tm, w), lambda i: (i, 0))
    resident = lambda a: pl.BlockSpec(a.shape, lambda i: (0, 0), pipeline_mode=pl.Buffered(1))
    return _call(
        body, comm, name="in_proj", grid=(L // tm,),
        in_specs=[row(K), resident(w_ssd), resident(w_att)], out_specs=[row(S_W), row(A_W), row(K)],
        out_shape=[jax.ShapeDtypeStruct((L, S_W), F32), jax.ShapeDtypeStruct((L, A_W), F32),
                   jax.ShapeDtypeStruct((L, K), _MXU)],
        scratch_shapes=[], args=(x, w_ssd, w_att))


def _matmul_tn(a, g, *, tl, tn, name, emit_bf16=False):
    L, M = a.shape
    N = g.shape[1]
    last = L // tl - 1

    def body(a_ref, g_ref, o_ref, *rest):
        @pl.when(pl.program_id(1) == 0)
        def _():
            o_ref[...] = jnp.zeros_like(o_ref)

        o_ref[...] += _mm_tn(a_ref[...], g_ref[...])
        if emit_bf16:
            @pl.when(pl.program_id(1) == last)
            def _():
                rest[0][...] = o_ref[...].astype(BF16)

    spec = pl.BlockSpec((M, tn), lambda j, l: (0, j))
    res = pl.pallas_call(
        body, name=name, grid=(N // tn, L // tl),
        in_specs=[pl.BlockSpec((tl, M), lambda j, l: (l, 0)), pl.BlockSpec((tl, tn), lambda j, l: (l, j))],
        out_specs=[spec, spec] if emit_bf16 else [spec],
        out_shape=[jax.ShapeDtypeStruct((M, N), F32)] + ([jax.ShapeDtypeStruct((M, N), BF16)] if emit_bf16 else []),
        compiler_params=_params(("arbitrary", "arbitrary")),
    )(a, g)
    return res if emit_bf16 else res[0]


def _position():
    return lax.axis_index("x"), lax.axis_index("y"), lax.axis_index("c")


def _index(px, py, pc):
    return 4 * px + 2 * py + pc


def _flip(pos, k):
    x, y, c = pos
    return ((1 - x) if (k >> 2) & 1 else x, (1 - y) if (k >> 1) & 1 else y, (1 - c) if k & 1 else c)


def _when(cond, fn):
    if cond is True:
        fn()
    else:
        pl.when(cond)(fn)


def _remote(src, dst, send_sem, recv_sem, peer):
    return pltpu.make_async_remote_copy(src_ref=src, dst_ref=dst, send_sem=send_sem, recv_sem=recv_sem,
                                        device_id=peer, device_id_type=MESH)


class _Flow:
    def __init__(self, kind, operand, result, target_x=None, target_c=None):
        self.kind, self.operand, self.result, self.target_x, self.target_c = kind, operand, result, target_x, target_c

    def owns(self, pos):
        if self.target_x is None:
            return True
        cond = pos[0] == self.target_x
        return cond if self.target_c is None else cond & (pos[2] == self.target_c)


class _Hosted:
    def __init__(self, operands, out_shapes, flows, aliases=None):
        self.operands, self.out_shapes, self.flows = operands, out_shapes, flows
        self.aliases = aliases or {}

    def plan(self, ins, outs, send_sems, recv_sems, local_sems):
        me = _position()
        mi = _index(*me)
        sends, recvs, locals_ = [], [], []
        for row, f in enumerate(self.flows):
            src, dst = ins[f.operand], outs[f.result]
            for k in range(1, N_DEV):
                peer = _flip(me, k)
                sems = (send_sems.at[row, k - 1], recv_sems.at[row, k - 1])
                if f.kind == "exchange":
                    owner = _index(*peer) if f.target_x is None else 2 * peer[1] + peer[2]
                    cp = _remote(src.at[owner], dst.at[k - 1], *sems, peer)
                    sends.append((f.owns(peer), cp))
                    recvs.append((f.owns(me), cp))
                elif f.kind == "chip_exchange":
                    if k & 1:
                        continue
                    cp = _remote(src.at[peer[1]], dst.at[k // 2 - 1], *sems, peer)
                    sends.append((peer[0] == f.target_x, cp))
                    recvs.append((me[0] == f.target_x, cp))
                else:
                    sends.append((True, _remote(src, dst.at[mi], *sems, peer)))
                    recvs.append((True, _remote(src, dst.at[_index(*peer)], *sems, peer)))
            if f.kind == "gather":
                locals_.append(pltpu.make_async_copy(src, dst.at[mi], local_sems.at[row]))

        def start():
            for cp in locals_:
                cp.start()
            for cond, cp in sends:
                _when(cond, cp.start)

        def wait():
            for cond, cp in recvs:
                _when(cond, cp.wait_recv)
            for cond, cp in sends:
                _when(cond, cp.wait_send)
            for cp in locals_:
                cp.wait()

        return start, wait


def _call(body, comm, *, name, grid, in_specs, out_specs, out_shape, scratch_shapes, args, aliases=None):
    io_alias = dict(aliases or {})
    semantics = ("arbitrary",) * len(grid)
    if comm is None:
        return pl.pallas_call(body, name=name, grid=grid, in_specs=in_specs, out_specs=out_specs, out_shape=out_shape,
                              scratch_shapes=scratch_shapes, input_output_aliases=io_alias,
                              compiler_params=_params(semantics))(*args)
    n_in, n_out, n_scr = len(args), len(out_shape), len(scratch_shapes)
    c_in, c_out, rows = len(comm.operands), len(comm.out_shapes), len(comm.flows)

    def hosted(*refs):
        ins, refs = refs[:n_in], refs[n_in:]
        cins, refs = refs[:c_in], refs[c_in:]
        outs, refs = refs[:n_out], refs[n_out:]
        couts, refs = refs[:c_out], refs[c_out:]
        scr, (send_sems, recv_sems, local_sems) = refs[:n_scr], refs[n_scr:]
        start, wait = comm.plan(cins, couts, send_sems, recv_sems, local_sems)
        ids = [pl.program_id(d) for d in range(len(grid))]
        first, last = ids[0] == 0, ids[0] == grid[0] - 1
        for d in range(1, len(grid)):
            first, last = first & (ids[d] == 0), last & (ids[d] == grid[d] - 1)
        pl.when(first)(start)
        body(*ins, *outs, *scr)
        pl.when(last)(wait)

    for ci, co in comm.aliases.items():
        io_alias[n_in + ci] = n_out + co
    any_spec = pl.BlockSpec(memory_space=pl.ANY)
    sems = [pltpu.SemaphoreType.DMA((rows, N_DEV - 1)), pltpu.SemaphoreType.DMA((rows, N_DEV - 1)),
            pltpu.SemaphoreType.DMA((rows,))]
    return pl.pallas_call(
        hosted, name=name, grid=grid, in_specs=list(in_specs) + [any_spec] * c_in,
        out_specs=list(out_specs) + [any_spec] * c_out, out_shape=list(out_shape) + list(comm.out_shapes),
        scratch_shapes=list(scratch_shapes) + sems, input_output_aliases=io_alias,
        compiler_params=_params(semantics))(*args, *comm.operands)


def _head_row(ref, width, rep):
    hid = lax.broadcasted_iota(jnp.int32, (1, width), 1) // rep
    row = jnp.zeros((1, width), F32)
    for h in range(N_HEADS):
        row = jnp.where(hid == h, ref[h], row)
    return row


def _rows_from_above(u_b, s, ext_scr, row, col):
    down = (row - col == s).astype(_MXU)
    return jnp.concatenate([ext_scr[8 - s:16 - s, :], jnp.dot(down, u_b, preferred_element_type=F32)[8:128]], axis=0)


def _ssd_recompute(first, p_ref, halo_ref, cw_ref, cb_ref, dtb_ref, alog_ref, e_ref, ext_scr, pre=None):
    row = lax.broadcasted_iota(jnp.int32, (128, 128), 0)
    col = lax.broadcasted_iota(jnp.int32, (128, 128), 1)
    ext_scr[0:8, :] = jnp.where(first, 0.0, halo_ref[:, S_XS:S_DT])
    if pre is not None:
        ext_scr[8:16, :] = p_ref[0:8, S_XS:S_DT]
    else:
        ext_scr[8:136, :] = p_ref[:, S_XS:S_DT]
        cw = cw_ref[...]
        pre = (cb_ref[0:1, :] + cw[3:4, :] * ext_scr[8:136, :] + cw[2:3, :] * ext_scr[7:135, :]
               + cw[1:2, :] * ext_scr[6:134, :] + cw[0:1, :] * ext_scr[5:133, :])
    sg = _sigmoid(pre)
    act = pre * sg
    lane = lax.broadcasted_iota(jnp.int32, (1, 128), 1)
    A = jnp.where(lane < N_HEADS, -jnp.exp(_head_row(alog_ref, 128, 1)), 0.0)
    raw = p_ref[:, S_DT:S_DT + 128] + _head_row(dtb_ref, 128, 1)
    dt = _softplus(raw)
    dA = dt * A
    tril = (row >= col).astype(BF16)
    acs = _mm_exact_l(tril, dA)
    last = acs[127:128, :]
    ds = jnp.exp(last - acs)
    eo = jnp.exp(acs)
    E = e_ref[...]
    ex = _mm_2pass_r(jnp.concatenate([dt, ds, eo], axis=0), E)
    dt_e, ds_e, eo_e = ex[0:128], ex[128:256], ex[256:384]
    xs_c = act[:, 0:1024]
    X = xs_c * dt_e
    return dict(pre=pre, sg=sg, xs_c=xs_c, Bc=act[:, 1024:1280], Cc=act[:, 1280:1536], A=A, raw=raw, dt=dt,
                acs=acs, acsT=acs.T, eo_e=eo_e, ds_e=ds_e, dt_e=dt_e, cd_e=eo_e[127:128, :],
                X=X, Xd=X * ds_e, row=row, col=col)


def _split_halves(t):
    lo = _lo_half(CHUNK)
    return jnp.concatenate([jnp.where(lo, t, 0.0), jnp.where(lo, 0.0, t)], axis=0)


def _ssd_core(R, hprev):
    causal = R["row"] >= R["col"]
    acs, acsT, X = R["acs"], R["acsT"], R["X"]
    ydiag, yoff, snew = [], [], []
    for g in range(SSD_GROUPS):
        Bg = R["Bc"][:, g * 128:(g + 1) * 128]
        Cg = R["Cc"][:, g * 128:(g + 1) * 128]
        cols = slice(g * 512, (g + 1) * 512)
        CB = _mm_nt(Cg, Bg)
        snew.append(_mm_tn(Bg, R["Xd"][:, cols]))
        yoff.append(_mm(Cg, hprev[:, cols]))
        for j in range(4):
            h0 = g * 8 + 2 * j
            ms = [CB * jnp.exp(jnp.where(causal, acs[:, h:h + 1] - acsT[h:h + 1, :], NEG)) for h in (h0, h0 + 1)]
            ydiag.append(_mm(jnp.concatenate(ms, axis=1), _split_halves(X[:, h0 * HEAD_DIM:h0 * HEAD_DIM + 128])))
    Y = jnp.concatenate(ydiag, axis=1) + jnp.concatenate(yoff, axis=1) * R["eo_e"]
    return Y, jnp.concatenate(snew, axis=1)


def _ssd_forward_step(p_ref, halo_ref, cw_ref, cb_ref, dtb_ref, alog_ref, dsk_ref, nw_ref, e_ref,
                      y_ref, ypre_ref, hprev_ref, pre_ref, h_scr, ext_scr):
    c = pl.program_id(0)
    first = c == 0

    @pl.when(first)
    def _():
        h_scr[...] = jnp.zeros_like(h_scr)

    R = _ssd_recompute(first, p_ref, halo_ref, cw_ref, cb_ref, dtb_ref, alog_ref, e_ref, ext_scr)
    hprev = h_scr[...]
    hprev_ref[...] = hprev
    pre_ref[...] = R["pre"]
    Y, snew = _ssd_core(R, hprev)
    h_scr[...] = hprev * R["cd_e"] + snew
    Y = Y + _head_row(dsk_ref, D_SSD, HEAD_DIM) * R["xs_c"]
    ypre_ref[...] = Y
    z = p_ref[:, S_Z:S_Z + 1024]
    yf = Y * (z * _sigmoid(z))
    outs = []
    for g in range(SSD_GROUPS):
        yg = yf[:, g * 512:(g + 1) * 512]
        r = lax.rsqrt(jnp.mean(yg * yg, axis=-1, keepdims=True) + RMS_EPS)
        outs.append(yg * r)
    y_ref[:, 0:D_SSD] = (jnp.concatenate(outs, axis=1) * nw_ref[0:1, :]).astype(y_ref.dtype)


def _ssd_backward(proj_ssd, hprev_all, ypre, pre, dy, conv_w, conv_b, dt_bias, a_log, d_skip, norm_w, E, ET, comm=None):
    L = proj_ssd.shape[0]
    nc = L // CHUNK

    def body(p_ref, halo_ref, hprev_ref, ypre_ref, pre_ref, dy_ref, cw_ref, cb_ref, dtb_ref, alog_ref, dsk_ref, nw_ref, e_ref,
             et_ref, dp_ref, acc_cw_ref, acc_w_ref, acc_s_ref, dh_scr, ext_scr, ext2_scr, nxt_scr):
        i = pl.program_id(0)
        c = nc - 1 - i
        first = c == 0

        @pl.when(i == 0)
        def _():
            dh_scr[...] = jnp.zeros_like(dh_scr)
            nxt_scr[...] = jnp.zeros_like(nxt_scr)
            acc_cw_ref[...] = jnp.zeros_like(acc_cw_ref)
            acc_w_ref[...] = jnp.zeros_like(acc_w_ref)
            acc_s_ref[...] = jnp.zeros_like(acc_s_ref)

        R = _ssd_recompute(first, p_ref, halo_ref, cw_ref, cb_ref, dtb_ref, alog_ref, e_ref, ext_scr, pre_ref[...])
        hprev = hprev_ref[...]
        xs_c, X, Xd = R["xs_c"], R["X"], R["Xd"]
        acs, acsT = R["acs"], R["acsT"]
        ET = et_ref[...]
        dsk = _head_row(dsk_ref, D_SSD, HEAD_DIM)
        Y = ypre_ref[...]

        z = p_ref[:, S_Z:S_Z + 1024]
        sz = _sigmoid(z)
        silz = z * sz
        yf = Y * silz
        dyv = dy_ref[...]
        nw = nw_ref[0:1, :]
        dyf_parts, dnw_parts = [], []
        for g in range(SSD_GROUPS):
            cols = slice(g * 512, (g + 1) * 512)
            yg = yf[:, cols]
            r = lax.rsqrt(jnp.mean(yg * yg, axis=-1, keepdims=True) + RMS_EPS)
            yn = yg * r
            dyn = dyv[:, cols] * nw[:, cols]
            dnw_parts.append(_colsum(dyv[:, cols] * yn))
            dyf_parts.append(r * (dyn - yn * jnp.mean(dyn * yn, axis=-1, keepdims=True)))
        dyf = jnp.concatenate(dyf_parts, axis=1)
        dY = dyf * silz
        dz = dyf * Y * (sz * (1.0 + z * (1.0 - sz)))

        dhn = dh_scr[...]
        dYo = dY * R["eo_e"]
        causal = R["row"] >= R["col"]
        dacs = jnp.zeros((128, 128), F32)
        dacs_t = jnp.zeros((128, 128), F32)
        dxdiag, dxd, dhprev, dBs, dCs, yoff = [], [], [], [], [], []
        for g in range(SSD_GROUPS):
            Bg = R["Bc"][:, g * 128:(g + 1) * 128]
            Cg = R["Cc"][:, g * 128:(g + 1) * 128]
            cols = slice(g * 512, (g + 1) * 512)
            CB = _mm_nt(Cg, Bg)
            dCB = jnp.zeros((128, 128), F32)
            for j in range(4):
                h0 = g * 8 + 2 * j
                pc = slice(h0 * HEAD_DIM, h0 * HEAD_DIM + 128)
                dYst = _split_halves(dY[:, pc])
                dMst = _mm_nt(dYst, X[:, pc])
                mts = []
                for a, h in enumerate((h0, h0 + 1)):
                    acol = acs[:, h:h + 1]
                    arow = acsT[h:h + 1, :]
                    Lm = jnp.exp(jnp.where(causal, acol - arow, NEG))
                    M = CB * Lm
                    dM = dMst[a * 128:(a + 1) * 128]
                    dCB = dCB + dM * Lm
                    G = dM * M
                    dacs = dacs + jnp.where(R["col"] == h, jnp.sum(G, axis=1, keepdims=True), 0.0)
                    dacs_t = dacs_t + jnp.where(R["row"] == h, jnp.sum(G, axis=0, keepdims=True), 0.0)
                    mts.append(M.T)
                dxdiag.append(_mm(jnp.concatenate(mts, axis=1), dYst))
            dS = dhn[:, cols]
            dxd.append(_mm(Bg, dS))
            yoff.append(_mm(Cg, hprev[:, cols]))
            dhprev.append(_mm_tn(Cg, dYo[:, cols]))
            dCs.append(_mm_nt(dYo[:, cols], hprev[:, cols]) + _mm(dCB, Bg))
            dBs.append(_mm_tn(dCB, Cg) + _mm_nt(Xd[:, cols], dS))
        Yoff = jnp.concatenate(yoff, axis=1) * R["eo_e"]
        dXd = jnp.concatenate(dxd, axis=1)
        dX = jnp.concatenate(dxdiag, axis=1) + dXd * R["ds_e"]
        t_state = dXd * Xd
        dacs = dacs + _mm_2pass_r(dY * Yoff - t_state, ET) - dacs_t.T
        v_last = _colsum(t_state + dhn * hprev * R["cd_e"])
        dlast = _mm_exact_r(jnp.broadcast_to(v_last, (8, 1024)), ET)[0:1, :]
        dacs = dacs + jnp.where(R["row"] == 127, dlast, 0.0)
        triu = (R["col"] >= R["row"]).astype(BF16)
        da = _mm_exact_l(triu, dacs)
        ddt = da * R["A"] + _mm(dX * xs_c, ET)
        ddt_raw = ddt * _sigmoid(R["raw"])
        dxs_c = dX * R["dt_e"] + dY * dsk
        dh_scr[...] = jnp.concatenate(dhprev, axis=1) + dhn * R["cd_e"]

        dact = jnp.concatenate([dxs_c] + dBs + dCs, axis=1)
        pre, sg = R["pre"], R["sg"]
        dpre = dact * (sg * (1.0 + pre * (1.0 - sg)))
        ext2_scr[0:8, :] = dpre[120:128, :]
        ext2_scr[8:16, :] = nxt_scr[...]
        nxt_scr[...] = dpre[0:8, :]
        cw = cw_ref[...]
        u_b, dpre_b = p_ref[:, S_XS:S_DT].astype(_MXU), dpre.astype(_MXU)
        dxbc = cw[3:4, :] * dpre
        taps = [_colsum(dpre * p_ref[:, S_XS:S_DT])]
        for s in (1, 2, 3):
            up = (R["col"] - R["row"] == s).astype(_MXU)
            d_s = jnp.concatenate([jnp.dot(up, dpre_b, preferred_element_type=F32)[0:120],
                                   ext2_scr[s:8 + s, :]], axis=0)
            dxbc = dxbc + cw[3 - s:4 - s, :] * d_s
            taps.append(_colsum(dpre * _rows_from_above(u_b, s, ext_scr, R["row"], R["col"])))
        acc_cw_ref[...] += _rows8(taps[::-1] + [_colsum(dpre)])
        acc_w_ref[...] += _rows8([jnp.concatenate(dnw_parts, axis=1), _colsum(dY * xs_c)])
        acc_s_ref[...] += _rows8([_colsum(ddt_raw), _colsum(da * R["dt"])])

        lane = lax.broadcasted_iota(jnp.int32, (128, 128), 1)
        dp_ref[:, S_Z:S_Z + 1024] = dz.astype(dp_ref.dtype)
        dp_ref[:, S_XS:S_DT] = dxbc.astype(dp_ref.dtype)
        dp_ref[:, S_DT:S_DT + 128] = jnp.where(lane < N_HEADS, ddt_raw, 0.0).astype(dp_ref.dtype)
        dp_ref[:, S_DT + 128:S_W] = jnp.zeros((128, 128), dp_ref.dtype)

        @pl.when(i == nc - 1)
        def _():
            acc = acc_s_ref[...]
            dskip = _mm_exact_r(acc_w_ref[...], ET)[1:2, :]
            acc_s_ref[...] = _rows8([acc[0:1, :], acc[1:2, :] * R["A"], dskip])

    const = lambda shape: pl.BlockSpec(shape, lambda i: (0, 0))
    smem = pl.BlockSpec(memory_space=pltpu.SMEM)
    rev = lambda i: (nc - 1 - i, 0)
    return _call(
        body, comm, name="ssd_bwd", grid=(nc,),
        in_specs=[pl.BlockSpec((CHUNK, S_W), rev),
                  pl.BlockSpec((8, S_W), lambda i: (jnp.maximum((nc - 1 - i) * 16 - 1, 0), 0)),
                  pl.BlockSpec((128, 1024), rev),
                  pl.BlockSpec((CHUNK, D_SSD), rev),
                  pl.BlockSpec((CHUNK, D_XBC), rev),
                  pl.BlockSpec((CHUNK, D_SSD), rev),
                  const((4, D_XBC)), const((1, D_XBC)), smem, smem, smem, const((1, 1024)),
                  const((128, 1024)), const((1024, 128))],
        out_specs=[pl.BlockSpec((CHUNK, S_W), rev), const((8, D_XBC)), const((8, 1024)), const((8, 128))],
        out_shape=[jax.ShapeDtypeStruct((L, S_W), _MXU), jax.ShapeDtypeStruct((8, D_XBC), F32),
                   jax.ShapeDtypeStruct((8, 1024), F32), jax.ShapeDtypeStruct((8, 128), F32)],
        scratch_shapes=[pltpu.VMEM((128, 1024), F32), pltpu.VMEM((16, D_XBC), F32),
                        pltpu.VMEM((16, D_XBC), F32), pltpu.VMEM((8, D_XBC), F32)],
        args=(proj_ssd, proj_ssd, hprev_all, ypre, pre, dy, conv_w, conv_b, dt_bias, a_log, d_skip, norm_w, E, ET))


def _rope(t, tab):
    cos, sa, sb = tab[:, 0:128], tab[:, 128:256], tab[:, 256:384]
    outs = []
    for i in range(t.shape[1] // 128):
        tg = t[:, i * 128:(i + 1) * 128]
        outs.append(tg * cos + pltpu.roll(tg, 8, 1) * sa + pltpu.roll(tg, 120, 1) * sb)
    return jnp.concatenate(outs, axis=1)


def _rope_transposed(d, tab):
    cos, sa, sb = tab[:, 0:128], tab[:, 128:256], tab[:, 256:384]
    outs = []
    for i in range(d.shape[1] // 128):
        dg = d[:, i * 128:(i + 1) * 128]
        outs.append(dg * cos + pltpu.roll(dg * sa, 120, 1) + pltpu.roll(dg * sb, 8, 1))
    return jnp.concatenate(outs, axis=1)


def _lo_half(rows):
    return lax.broadcasted_iota(jnp.int32, (rows, 128), 1) < HEAD_DIM


def _native_half(rows, j):
    lo = _lo_half(rows)
    return lo if j % 2 == 0 else jnp.logical_not(lo)


def _kv_native(t, j):
    p = j // 2
    return jnp.where(_native_half(t.shape[0], j), t[:, p * 128:(p + 1) * 128], 0.0)


def _stack_heads(t, j):
    out = []
    for m in (2 * j, 2 * j + 1):
        pair = t[:, m * 128:(m + 1) * 128]
        swapped = pltpu.roll(pair, HEAD_DIM, 1)
        out += [pair, swapped] if j % 2 == 0 else [swapped, pair]
    return jnp.concatenate(out, axis=0)


def _unstack_heads(s, j):
    out = []
    for m in range(2):
        first, second = s[256 * m:256 * m + 128], s[256 * m + 128:256 * m + 256]
        if j % 2 == 0:
            out.append(first + pltpu.roll(second, HEAD_DIM, 1))
        else:
            out.append(pltpu.roll(first, HEAD_DIM, 1) + second)
    return jnp.concatenate(out, axis=1)


def _keep_native(r, j):
    return jnp.where(_native_half(r.shape[0], j), r, 0.0)


def _sink_row(sink_ref, j):
    hid = lax.broadcasted_iota(jnp.int32, (1, 4 * CHUNK), 1) // CHUNK
    row = jnp.zeros((1, 4 * CHUNK), F32)
    for hh in range(4):
        row = jnp.where(hid == hh, sink_ref[4 * j + hh], row)
    return row


def _from_current():
    si = lax.broadcasted_iota(jnp.int32, (CHUNK, 4 * CHUNK), 0)
    qi = lax.broadcasted_iota(jnp.int32, (CHUNK, 4 * CHUNK), 1) % CHUNK
    return si <= qi


def _fold(full, from_cur, pen=0.0):
    return jnp.where(from_cur, full[CHUNK:2 * CHUNK], full[0:CHUNK] + pen)


def _unfold(t, from_cur):
    c = jnp.where(from_cur, t, 0.0)
    return jnp.concatenate([t - c, c], axis=0)


def _softmax_sink(s, sink):
    mx = jnp.maximum(jnp.max(s, axis=0, keepdims=True), sink)
    p = jnp.exp(s - mx)
    esink = jnp.exp(sink - mx)
    inv = 1.0 / (jnp.sum(p, axis=0, keepdims=True) + esink)
    return p * inv, esink * inv


def _swa_inputs(blk, p_ref, prev_ref, tab_ref, ptab_ref):
    tab = tab_ref[...]
    qr = _rope(p_ref[:, A_Q:A_Q + 1024], tab) * ATT_SCALE
    kk = jnp.concatenate([_rope(prev_ref[:, 0:256], ptab_ref[...]), _rope(p_ref[:, A_K:A_K + 256], tab)], axis=0)
    vv = jnp.concatenate([prev_ref[:, 256:512], p_ref[:, A_V:A_V + 256]], axis=0)
    return tab, qr, kk, vv, jnp.where(blk > 0, 0.0, NEG)


def _swa_forward_step(sink_ref, p_ref, prev_ref, tab_ref, ptab_ref, y_ref):
    n = pl.program_id(0)
    _, qr, kk, vv, pen = _swa_inputs(n, p_ref, prev_ref, tab_ref, ptab_ref)
    from_cur = _from_current()
    outs = []
    for j in range(KV_HEADS):
        s = _fold(_mm_nt(_kv_native(kk, j), _stack_heads(qr, j)), from_cur, pen)
        P, _ = _softmax_sink(s, _sink_row(sink_ref, j))
        outs.append(_unstack_heads(_mm_tn(_unfold(P, from_cur), _kv_native(vv, j)), j))
    g = p_ref[:, A_G:A_G + 1024]
    y_ref[:, D_SSD:D_SSD + D_ATT] = (jnp.concatenate(outs, axis=1) * (g * _sigmoid(g))).astype(y_ref.dtype)


def _mixer_forward(proj_ssd, proj_att, tabs, sinks, conv_w, conv_b, dt_bias, a_log, d_skip, norm_w, E, comm=None):
    L = proj_ssd.shape[0]
    nc = L // CHUNK

    def body(p_ref, halo_ref, cw_ref, cb_ref, dtb_ref, alog_ref, dsk_ref, nw_ref, e_ref,
             sink_ref, pa_ref, prev_ref, tab_ref, ptab_ref, y_ref, ypre_ref, hprev_ref, pre_ref, h_scr, ext_scr):
        _ssd_forward_step(p_ref, halo_ref, cw_ref, cb_ref, dtb_ref, alog_ref, dsk_ref, nw_ref, e_ref,
                          y_ref, ypre_ref, hprev_ref, pre_ref, h_scr, ext_scr)
        _swa_forward_step(sink_ref, pa_ref, prev_ref, tab_ref, ptab_ref, y_ref)

    const = lambda shape: pl.BlockSpec(shape, lambda c: (0, 0))
    smem = pl.BlockSpec(memory_space=pltpu.SMEM)
    rows = lambda w: pl.BlockSpec((CHUNK, w), lambda c: (c, 0))
    return _call(
        body, comm, name="mixer_fwd", grid=(nc,),
        in_specs=[rows(S_W), pl.BlockSpec((8, S_W), lambda c: (jnp.maximum(c * 16 - 1, 0), 0)),
                  const((4, D_XBC)), const((1, D_XBC)), smem, smem, smem, const((1, 1024)), const((128, 1024)),
                  smem, rows(A_W), pl.BlockSpec((CHUNK, 512), lambda c: (jnp.maximum(c - 1, 0), 2)),
                  rows(384), pl.BlockSpec((CHUNK, 384), lambda c: (jnp.maximum(c - 1, 0), 0))],
        out_specs=[rows(D_SSD + D_ATT), rows(D_SSD), pl.BlockSpec((128, 1024), lambda c: (c, 0)), rows(D_XBC)],
        out_shape=[jax.ShapeDtypeStruct((L, D_SSD + D_ATT), _MXU), jax.ShapeDtypeStruct((L, D_SSD), F32),
                   jax.ShapeDtypeStruct((nc * 128, 1024), F32), jax.ShapeDtypeStruct((L, D_XBC), F32)],
        scratch_shapes=[pltpu.VMEM((128, 1024), F32), pltpu.VMEM((136, D_XBC), F32)],
        args=(proj_ssd, proj_ssd, conv_w, conv_b, dt_bias, a_log, d_skip, norm_w, E,
              sinks, proj_att, proj_att, tabs, tabs))


def _swa_backward(proj_att, tabs, sinks, dy, comm=None):
    L = proj_att.shape[0]
    nb = L // CHUNK

    def body(sink_ref, p_ref, prev_ref, tab_ref, ptab_ref, dy_ref, dp_ref, dsink_ref, carry_k, carry_v):
        i = pl.program_id(0)
        n = nb - 1 - i

        @pl.when(i == 0)
        def _():
            carry_k[...] = jnp.zeros_like(carry_k)
            carry_v[...] = jnp.zeros_like(carry_v)
            dsink_ref[...] = jnp.zeros_like(dsink_ref)

        tab, qr, kk, vv, pen = _swa_inputs(n, p_ref, prev_ref, tab_ref, ptab_ref)
        from_cur = _from_current()
        g = p_ref[:, A_G:A_G + 1024]
        sgm = _sigmoid(g)
        dyv = dy_ref[...]
        do_all = dyv * (g * sgm)
        lane8 = lax.broadcasted_iota(jnp.int32, (8, 128), 1)
        hid = lax.broadcasted_iota(jnp.int32, (1, 4 * CHUNK), 1) // CHUNK
        o_parts, dq_parts = [], []
        dk_nat = [jnp.zeros((2 * CHUNK, 128), F32) for _ in range(2)]
        dv_nat = [jnp.zeros((2 * CHUNK, 128), F32) for _ in range(2)]
        dsink = jnp.zeros((8, 128), F32)
        for j in range(KV_HEADS):
            qs = _stack_heads(qr, j)
            kkb, vvb = _kv_native(kk, j), _kv_native(vv, j)
            P, psink = _softmax_sink(_fold(_mm_nt(kkb, qs), from_cur, pen), _sink_row(sink_ref, j))
            p_full = _unfold(P, from_cur)
            o_parts.append(_unstack_heads(_mm_tn(p_full, vvb), j))
            do_s = _stack_heads(do_all, j)
            dP = _fold(_mm_nt(vvb, do_s), from_cur)
            D = jnp.sum(P * dP, axis=0, keepdims=True)
            ds_full = _unfold(P * (dP - D), from_cur)
            sd = psink * D
            for hh in range(4):
                dsink = dsink + jnp.where(lane8 == 4 * j + hh, -jnp.sum(jnp.where(hid == hh, sd, 0.0)), 0.0)
            dq_parts.append(_unstack_heads(_mm_tn(ds_full, kkb), j) * ATT_SCALE)
            dk_nat[j // 2] = dk_nat[j // 2] + _keep_native(_mm(ds_full, qs), j)
            dv_nat[j // 2] = dv_nat[j // 2] + _keep_native(_mm(p_full, do_s), j)
        o = jnp.concatenate(o_parts, axis=1)
        dkk = jnp.concatenate(dk_nat, axis=1)
        dvv = jnp.concatenate(dv_nat, axis=1)
        out = dp_ref.dtype
        dp_ref[:, A_Q:A_Q + 1024] = _rope_transposed(jnp.concatenate(dq_parts, axis=1), tab).astype(out)
        dp_ref[:, A_K:A_K + 256] = _rope_transposed(dkk[CHUNK:2 * CHUNK] + carry_k[...], tab).astype(out)
        dp_ref[:, A_V:A_V + 256] = (dvv[CHUNK:2 * CHUNK] + carry_v[...]).astype(out)
        dp_ref[:, A_G:A_G + 1024] = (dyv * o * (sgm * (1.0 + g * (1.0 - sgm)))).astype(out)
        carry_k[...] = dkk[0:CHUNK]
        carry_v[...] = dvv[0:CHUNK]
        dsink_ref[...] += dsink

    rev = lambda i: (nb - 1 - i, 0)
    prev = lambda i: jnp.maximum(nb - 2 - i, 0)
    return _call(
        body, comm, name="swa_bwd", grid=(nb,),
        in_specs=[pl.BlockSpec(memory_space=pltpu.SMEM),
                  pl.BlockSpec((CHUNK, A_W), rev),
                  pl.BlockSpec((CHUNK, 512), lambda i: (prev(i), 2)),
                  pl.BlockSpec((CHUNK, 384), rev),
                  pl.BlockSpec((CHUNK, 384), lambda i: (prev(i), 0)),
                  pl.BlockSpec((CHUNK, D_ATT), lambda i: (nb - 1 - i, 1))],
        out_specs=[pl.BlockSpec((CHUNK, A_W), rev), pl.BlockSpec((8, 128), lambda i: (0, 0))],
        out_shape=[jax.ShapeDtypeStruct((L, A_W), _MXU), jax.ShapeDtypeStruct((8, 128), F32)],
        scratch_shapes=[pltpu.VMEM((CHUNK, 256), F32), pltpu.VMEM((CHUNK, 256), F32)],
        args=(sinks, proj_att, proj_att, tabs, tabs, dy))


def _head(y, x, target, w_out, ln_g, ln_b, *, tm):
    L = x.shape[0]
    nsteps = L // tm

    def body(y_ref, x_ref, t_ref, wo_ref, g_ref, b_ref, dr_ref, dy_ref, acc_ref):
        i = pl.program_id(0)

        @pl.when(i == 0)
        def _():
            acc_ref[...] = jnp.zeros_like(acc_ref)

        r = ALPHA * x_ref[...] + _mm(y_ref[...], wo_ref[...])
        mu = jnp.mean(r, axis=-1, keepdims=True)
        d = r - mu
        rstd = lax.rsqrt(jnp.mean(d * d, axis=-1, keepdims=True) + LN_EPS)
        xh = d * rstd
        gam = g_ref[0:1, :]
        e = xh * gam + b_ref[0:1, :] - t_ref[...]
        dout = e * (1.0 / D_MODEL)
        dxh = dout * gam
        dr = rstd * (dxh - jnp.mean(dxh, axis=-1, keepdims=True)
                     - xh * jnp.mean(dxh * xh, axis=-1, keepdims=True))
        dr_ref[...] = dr
        dy_ref[...] = _mm_nt(dr, wo_ref[...])
        acc_ref[...] += _rows8([_colsum(dout * xh), _colsum(dout), _colsum(e * e) * (0.5 / D_MODEL)])

        @pl.when(i == nsteps - 1)
        def _():
            acc = acc_ref[...]
            tot = jnp.sum(acc[2:3, :])
            rid = lax.broadcasted_iota(jnp.int32, (8, 1024), 0)
            acc_ref[...] = jnp.where(rid == 3, tot, acc)

    const = lambda shape: pl.BlockSpec(shape, lambda i: (0, 0))
    row = lambda w: pl.BlockSpec((tm, w), lambda i: (i, 0))
    return pl.pallas_call(
        body, name="head", grid=(nsteps,),
        in_specs=[row(2048), row(1024), row(1024), const((2048, 1024)), const((1, 1024)), const((1, 1024))],
        out_specs=[row(1024), row(2048), const((8, 1024))],
        out_shape=[jax.ShapeDtypeStruct((L, D_MODEL), F32), jax.ShapeDtypeStruct((L, 2048), F32),
                   jax.ShapeDtypeStruct((8, 1024), F32)],
        compiler_params=_params(("arbitrary",)),
    )(y, x, target, w_out, ln_g, ln_b)


def _gather_w_in(w_shard):
    R = w_shard.shape[0]
    halves = (pl.ds(0, R // 2), pl.ds(R // 2, R // 2))
    any_spec = pl.BlockSpec(memory_space=pl.ANY)

    def body(in_ref, out_ref, send_sems, recv_sems, local_sem):
        x, y, c = _position()

        def slot(p, half=None):
            s = out_ref.at[_index(*p)]
            return s if half is None else s.at[halves[half]]

        def same_core(p):
            return (p[0], p[1], c)

        def other_core(p):
            return (p[0], p[1], 1 - c)

        me, xn, yn, dg = (x, y), (1 - x, y), (x, 1 - y), (1 - x, 1 - y)

        def copy(k, dst, to, src=None):
            return _remote(dst if src is None else src, dst, send_sems.at[k], recv_sems.at[k], to)

        local = pltpu.make_async_copy(in_ref, slot(same_core(me)), local_sem)
        local.start()
        own = [copy(0, slot(same_core(me)), other_core(me), in_ref), copy(1, slot(same_core(me)), same_core(xn), in_ref),
               copy(2, slot(same_core(me)), same_core(yn), in_ref)]
        for cp in own:
            cp.start()
        copy(1, slot(same_core(xn)), same_core(xn)).wait_recv()
        passed = [copy(4, slot(same_core(xn), 1), same_core(yn)), copy(5, slot(same_core(xn)), other_core(me))]
        for cp in passed:
            cp.start()
        copy(2, slot(same_core(yn)), same_core(yn)).wait_recv()
        more = [copy(3, slot(same_core(yn), 0), same_core(xn)), copy(6, slot(same_core(yn)), other_core(me))]
        for cp in more:
            cp.start()
        passed += more
        for k, half in ((3, 0), (4, 1)):
            copy(k, slot(same_core(dg), half), same_core(xn)).wait_recv()
            fwd = copy(7 + half, slot(same_core(dg), half), other_core(me))
            fwd.start()
            passed.append(fwd)
        copy(0, slot(other_core(me)), other_core(me)).wait_recv()
        copy(5, slot(other_core(xn)), other_core(me)).wait_recv()
        copy(6, slot(other_core(yn)), other_core(me)).wait_recv()
        for half in (0, 1):
            copy(7 + half, slot(other_core(dg), half), other_core(me)).wait_recv()
        for cp in own + passed:
            cp.wait_send()
        local.wait()

    return pl.pallas_call(
        body, name="gather_w_in", in_specs=[any_spec], out_specs=any_spec,
        out_shape=jax.ShapeDtypeStruct((N_DEV,) + w_shard.shape, w_shard.dtype),
        scratch_shapes=[pltpu.SemaphoreType.DMA((9,)), pltpu.SemaphoreType.DMA((9,)), pltpu.SemaphoreType.DMA],
    )(w_shard)


def _input_gradient(d_ssd, d_att, w_ssd, w_att, dr, *, tm, comm=None):
    L = dr.shape[0]

    def body(ds_ref, da_ref, ws_ref, wa_ref, dr_ref, o_ref):
        o_ref[...] = ALPHA * dr_ref[...] + _mm_nt(ds_ref[...], ws_ref[...]) + _mm_nt(da_ref[...], wa_ref[...])

    row = lambda w: pl.BlockSpec((tm, w), lambda i: (i, 0))
    const = lambda shape: pl.BlockSpec(shape, lambda i: (0, 0))
    return _call(body, comm, name="dx", grid=(L // tm,),
                 in_specs=[row(S_W), row(A_W), const((D_MODEL, S_W)), const((D_MODEL, A_W)), row(D_MODEL)],
                 out_specs=[row(D_MODEL)], out_shape=[jax.ShapeDtypeStruct((L, D_MODEL), F32)],
                 scratch_shapes=[], args=(d_ssd, d_att, w_ssd, w_att, dr))


SHARD_COLS = D_IN_PROJ // N_DEV
SPLIT = N_SSD_REAL - 4 * SHARD_COLS
RELAYOUT_ROWS = 256


def _unpack_w_in(w_all):
    def body(g_ref, ws_ref, wa_ref):
        for j in range(4):
            ws_ref[:, SHARD_COLS * j:SHARD_COLS * (j + 1)] = g_ref[j]
        ws_ref[:, 4 * SHARD_COLS:N_SSD_REAL] = g_ref[4, :, 0:SPLIT]
        ws_ref[:, N_SSD_REAL:S_W] = jnp.zeros((RELAYOUT_ROWS, S_W - N_SSD_REAL), ws_ref.dtype)
        wa_ref[:, 0:SHARD_COLS - SPLIT] = g_ref[4, :, SPLIT:SHARD_COLS]
        for j in range(5, N_DEV):
            lo = SHARD_COLS * (j - 4) - SPLIT
            wa_ref[:, lo:lo + SHARD_COLS] = g_ref[j]

    return pl.pallas_call(
        body, name="unpack_w_in", grid=(D_MODEL // RELAYOUT_ROWS,),
        in_specs=[pl.BlockSpec((N_DEV, RELAYOUT_ROWS, SHARD_COLS), lambda i: (0, i, 0))],
        out_specs=[pl.BlockSpec((RELAYOUT_ROWS, S_W), lambda i: (i, 0)), pl.BlockSpec((RELAYOUT_ROWS, A_W), lambda i: (i, 0))],
        out_shape=[jax.ShapeDtypeStruct((D_MODEL, S_W), w_all.dtype), jax.ShapeDtypeStruct((D_MODEL, A_W), w_all.dtype)],
        compiler_params=_params(("arbitrary",)),
    )(w_all)


def _pack_dw_in(me1, dw_ssd, dw_att, half):
    def body(me_ref, *refs):
        if half == 0:
            ds_ref, p_ref, own_ref = refs
            me = me_ref[0]

            @pl.when(me >= 4)
            def _():
                own_ref[...] = jnp.zeros_like(own_ref)
        else:
            ds_ref, da_ref, p_ref = refs

        for j in range(4):
            if half == 0:
                pieces = [(0, ds_ref[:, SHARD_COLS * j:SHARD_COLS * (j + 1)])]
            elif j == 0:
                pieces = [(0, ds_ref[:, 4 * SHARD_COLS - S_DT:N_SSD_REAL - S_DT]), (SPLIT, da_ref[:, 0:SHARD_COLS - SPLIT])]
            else:
                lo = SHARD_COLS * j - SPLIT
                pieces = [(0, da_ref[:, lo:lo + SHARD_COLS])]
            for off, blk in pieces:
                p_ref[j, :, off:off + blk.shape[1]] = blk.astype(p_ref.dtype)
                if half == 0:
                    @pl.when(me == j)
                    def _(off=off, blk=blk):
                        own_ref[:, off:off + blk.shape[1]] = blk

    ins = [dw_ssd] if half == 0 else [dw_ssd, dw_att]
    row = lambda a: pl.BlockSpec((RELAYOUT_ROWS, a.shape[1]), lambda i: (i, 0))
    in_specs = [row(a) for a in ins]
    if half == 1:
        in_specs[0] = pl.BlockSpec((RELAYOUT_ROWS, S_W - S_DT), lambda i: (i, S_DT // (S_W - S_DT)))
    out_specs = [pl.BlockSpec((4, RELAYOUT_ROWS, SHARD_COLS), lambda i: (0, i, 0))]
    out_shape = [jax.ShapeDtypeStruct((4, D_MODEL, SHARD_COLS), BF16 if half == 0 else F32)]
    if half == 0:
        out_specs.append(pl.BlockSpec((RELAYOUT_ROWS, SHARD_COLS), lambda i: (i, 0)))
        out_shape.append(jax.ShapeDtypeStruct((D_MODEL, SHARD_COLS), F32))
    return pl.pallas_call(
        body, name="pack_dw_in_%d" % half, grid=(D_MODEL // RELAYOUT_ROWS,),
        in_specs=[pl.BlockSpec(memory_space=pltpu.SMEM)] + in_specs,
        out_specs=out_specs, out_shape=out_shape, compiler_params=_params(("arbitrary",)),
    )(me1, *ins)


def _pair_swap(stack):
    def body(in_ref, out_ref, send_sems, recv_sems):
        x, y, c = _position()
        cps = [_remote(in_ref.at[2 * oy + (1 - c)], out_ref.at[oy], send_sems.at[oy], recv_sems.at[oy], (x, y, 1 - c))
               for oy in range(2)]
        for cp in cps:
            cp.start()
        for cp in cps:
            cp.wait_recv()
        for cp in cps:
            cp.wait_send()

    any_spec = pl.BlockSpec(memory_space=pl.ANY)
    return pl.pallas_call(
        body, name="pair_swap", in_specs=[any_spec], out_specs=any_spec,
        out_shape=jax.ShapeDtypeStruct((2,) + stack.shape[1:], stack.dtype),
        scratch_shapes=[pltpu.SemaphoreType.DMA((2,)), pltpu.SemaphoreType.DMA((2,))],
    )(stack)


def _pair_sum(pos3, stack, swapped, own_lo):
    def body(pos_ref, a_ref, b_ref, lo_ref, chip_ref, own_ref):
        oy = pl.program_id(1)
        t = a_ref[0] + b_ref[0]
        chip_ref[0] = t.astype(chip_ref.dtype)

        @pl.when((pos_ref[0] == 0) & (oy == 0))
        def _():
            own_ref[...] = lo_ref[...]

        @pl.when((pos_ref[0] == 1) & (oy == pos_ref[1]))
        def _():
            own_ref[...] = t

    blk = (1, RELAYOUT_ROWS, SHARD_COLS)
    flat = pl.BlockSpec((RELAYOUT_ROWS, SHARD_COLS), lambda i, oy, pos: (i, 0))
    return pl.pallas_call(
        body, name="pair_sum",
        grid_spec=pltpu.PrefetchScalarGridSpec(
            num_scalar_prefetch=1, grid=(D_MODEL // RELAYOUT_ROWS, 2),
            in_specs=[pl.BlockSpec(blk, lambda i, oy, pos: (2 * oy + pos[2], i, 0)),
                      pl.BlockSpec(blk, lambda i, oy, pos: (oy, i, 0)), flat],
            out_specs=[pl.BlockSpec(blk, lambda i, oy, pos: (oy, i, 0)), flat]),
        out_shape=[jax.ShapeDtypeStruct((2, D_MODEL, SHARD_COLS), BF16), jax.ShapeDtypeStruct((D_MODEL, SHARD_COLS), F32)],
        compiler_params=_params(("arbitrary", "arbitrary")),
    )(pos3, stack, swapped, own_lo)


def _adamw_math(w, g, m, v):
    m = ADAM_B1 * m + (1.0 - ADAM_B1) * g
    v = ADAM_B2 * v + (1.0 - ADAM_B2) * (g * g)
    m_hat = m / (1.0 - ADAM_B1 ** ADAM_STEP)
    v_hat = v / (1.0 - ADAM_B2 ** ADAM_STEP)
    delta = -ADAM_LR * (m_hat / (jnp.sqrt(v_hat) + ADAM_EPS) + ADAM_WD * w)
    return delta, m, v


def _adamw_shard(n_recv, g_own, recv, w, m, v, *, rows, name):
    R, C = g_own.shape

    def body(n_ref, g_ref, r_ref, w_ref, m_ref, v_ref, go_ref, d_ref, mo_ref, vo_ref):
        g = g_ref[...]
        for k in range(N_DEV - 1):
            g = g + jnp.where(k < n_ref[0], r_ref[k].astype(F32), 0.0)
        d, mn, vn = _adamw_math(w_ref[...], g, m_ref[...], v_ref[...])
        go_ref[...] = g
        d_ref[...] = d
        mo_ref[...] = mn
        vo_ref[...] = vn

    blk = pl.BlockSpec((rows, C), lambda i: (i, 0))
    return pl.pallas_call(
        body, name=name, grid=(R // rows,),
        in_specs=[pl.BlockSpec(memory_space=pltpu.SMEM), blk,
                  pl.BlockSpec((N_DEV - 1, rows, C), lambda i: (0, i, 0)), blk, blk, blk],
        out_specs=[blk] * 4, out_shape=[jax.ShapeDtypeStruct((R, C), F32)] * 4,
        compiler_params=_params(("arbitrary",)),
    )(n_recv, g_own, recv, w, m, v)


def _minor_rows_view(a):
    return jnp.transpose(a, (2, 0, 1)).reshape(SHARD_COLS * 8, 128)


def _from_minor_rows_view(v):
    return jnp.transpose(v.reshape(SHARD_COLS, 8, 128), (1, 2, 0)).reshape(1, D_MODEL, SHARD_COLS)


def _adamw_w_in(n_recv, g_own, recv, w, m, v):
    C = SHARD_COLS
    pad = -C % 128

    def body(n_ref, g_ref, r_ref, w_ref, m_ref, v_ref, go_ref, d_ref, mo_ref, vo_ref):
        for q in range(D_MODEL // 128):
            band = pl.ds(q * 128, 128)
            g = g_ref[band, :]
            for k in range(N_DEV - 1):
                g = g + jnp.where(k < n_ref[0], r_ref[k, band, :].astype(F32), 0.0)
            g = jnp.pad(g, ((0, 0), (0, pad))).T[0:C]
            rows = pl.ds(q, C, stride=8)
            d, mn, vn = _adamw_math(w_ref[rows, :], g, m_ref[rows, :], v_ref[rows, :])
            go_ref[rows, :] = g
            d_ref[rows, :] = d
            mo_ref[rows, :] = mn
            vo_ref[rows, :] = vn

    return pl.pallas_call(
        body, name="adamw_w_in", out_shape=[jax.ShapeDtypeStruct(w.shape, F32)] * 4,
        in_specs=[pl.BlockSpec(memory_space=pltpu.SMEM)] + [pl.BlockSpec(memory_space=pltpu.VMEM)] * 5,
        out_specs=[pl.BlockSpec(memory_space=pltpu.VMEM)] * 4,
        compiler_params=_params(),
    )(n_recv, g_own, recv, w, m, v)


SMALL = ("conv_b", "dt_bias", "a_log", "d_skip", "ssd_norm_w", "attn_sinks", "ln_g", "ln_b")


def _adamw_small(gathered, params):
    n_p = len(SMALL)

    def body(*refs):
        acc = []
        for r in refs[:5]:
            t = r[0]
            for k in range(1, N_DEV):
                t = t + r[k]
            acc.append(t)
        head, conv, norm, scal, sink = acc
        grads = dict(conv_b=conv[4:5, :], dt_bias=scal[0:1, 0:N_HEADS], a_log=scal[1:2, 0:N_HEADS],
                     d_skip=scal[2:3, 0:N_HEADS], ssd_norm_w=norm[0:1, :], attn_sinks=sink[0:1, 0:N_HEADS],
                     ln_g=head[0:1, :], ln_b=head[1:2, :])
        wmv = refs[5:5 + 3 * n_p]
        outs = refs[5 + 3 * n_p:]
        outs[0][...] = head[3:4, 0:1]
        outs[1][...] = conv[0:4, :]
        for i, name in enumerate(SMALL):
            w_ref, m_ref, v_ref = wmv[3 * i:3 * i + 3]
            g = grads[name]
            d, mn, vn = _adamw_math(w_ref[...], g, m_ref[...], v_ref[...])
            for o_ref, val in zip(outs[2 + 4 * i:6 + 4 * i], (g, d, mn, vn)):
                o_ref[...] = val

    flat = [a for name in SMALL for a in params[name]]
    out_shape = [jax.ShapeDtypeStruct((1, 1), F32), jax.ShapeDtypeStruct((4, D_XBC), F32)]
    for name in SMALL:
        out_shape += [jax.ShapeDtypeStruct(params[name][0].shape, F32)] * 4
    res = pl.pallas_call(body, name="adamw_small", out_shape=out_shape, compiler_params=_params())(*gathered, *flat)
    return res[0], res[1], {name: res[2 + 4 * i:6 + 4 * i] for i, name in enumerate(SMALL)}


def _adamw_plain(g, w, m, v):
    def body(g_ref, w_ref, m_ref, v_ref, d_ref, mo_ref, vo_ref):
        d, mn, vn = _adamw_math(w_ref[...], g_ref[...], m_ref[...], v_ref[...])
        d_ref[...] = d
        mo_ref[...] = mn
        vo_ref[...] = vn

    return pl.pallas_call(
        body, name="adamw_conv_w", out_shape=[jax.ShapeDtypeStruct(w.shape, F32)] * 3,
        compiler_params=_params(),
    )(g, w, m, v)


def _lane_pattern(fn):
    return np.asarray([fn(l % HEAD_DIM) for l in range(128)], np.float32)


ROPE_INV = _lane_pattern(lambda r: ROPE_THETA ** (-2.0 * (r % 8) / ROPE_DIM) if r < ROPE_DIM else 0.0)
ROPE_SIN_A = _lane_pattern(lambda r: 1.0 if 8 <= r < ROPE_DIM else 0.0)
ROPE_SIN_B = _lane_pattern(lambda r: -1.0 if r < 8 else 0.0)


def _rope_tables(positions):
    ang = positions.astype(F32)[:, None] * ROPE_INV[None, :]
    sn = jnp.sin(ang)
    return jnp.concatenate([jnp.cos(ang), sn * ROPE_SIN_A[None, :], sn * ROPE_SIN_B[None, :]], axis=1)


def _expansion():
    E = np.arange(1024)[None, :] // HEAD_DIM == np.arange(128)[:, None]
    return jnp.asarray(E, BF16), jnp.asarray(E.T, BF16)


def _ssd_args(conv_w, conv_b, dt_bias, a_log, d_skip, norm_w, E):
    return (conv_w, conv_b, dt_bias.reshape(-1), a_log.reshape(-1), d_skip.reshape(-1), norm_w, E)


def kernel(x, positions, w_in, conv_w, conv_b, dt_bias, a_log, d_skip, ssd_norm_w, attn_sinks, w_out, ln_g, ln_b, loss_target, m_w_in, m_conv_w, m_conv_b, m_dt_bias, m_a_log, m_d_skip, m_ssd_norm_w, m_attn_sinks, m_w_out, m_ln_g, m_ln_b, v_w_in, v_conv_w, v_conv_b, v_dt_bias, v_a_log, v_d_skip, v_ssd_norm_w, v_attn_sinks, v_w_out, v_ln_g, v_ln_b):
    me = _index(*_position())
    me1 = me.reshape(1).astype(jnp.int32)
    x0, target = x[0], loss_target[0]
    bf16_shard = lambda shape: jax.ShapeDtypeStruct(shape, BF16)
    E, ET = _expansion()
    tabs = _rope_tables(positions[0])
    sinks = attn_sinks.reshape(-1)

    w_ssd, w_att = _unpack_w_in(_gather_w_in(w_in[0].astype(BF16)))
    gather_conv_w = _Hosted([conv_w[0]], [jax.ShapeDtypeStruct((N_DEV,) + conv_w.shape[1:], F32)],
                            [_Flow("gather", 0, 0)])

    proj_ssd, proj_att, xb, conv_w_all = _in_proj(x0, w_ssd, w_att, tm=512, comm=gather_conv_w)
    conv_w_f = jnp.transpose(conv_w_all, (1, 0, 2)).reshape(4, D_XBC)
    ssd_args = _ssd_args(conv_w_f, conv_b, dt_bias, a_log, d_skip, ssd_norm_w, E)
    gather_w_out = _Hosted([w_out[0].astype(BF16)], [bf16_shard((N_DEV, 256, D_MODEL))], [_Flow("gather", 0, 0)])
    y, ypre, hprev, pre, w_out_all = _mixer_forward(proj_ssd, proj_att, tabs, sinks, *ssd_args, comm=gather_w_out)
    w_out_f = w_out_all.reshape(2 * D_MODEL, D_MODEL)
    dr, dy, acc_head = _head(y, x0, target, w_out_f, ln_g, ln_b, tm=512)

    dw_out, dw_out_bf16 = _matmul_tn(y, dr, tl=1024, tn=D_MODEL, name="dw_out", emit_bf16=True)
    own_out = lax.dynamic_index_in_dim(dw_out.reshape(N_DEV, 256, D_MODEL), me, axis=0, keepdims=False)
    send_out = _Hosted([dw_out_bf16.reshape(N_DEV, 256, D_MODEL)], [bf16_shard((N_DEV - 1, 256, D_MODEL))],
                       [_Flow("exchange", 0, 0)])
    d_ssd, acc_cw, acc_w, acc_s, recv_out = _ssd_backward(proj_ssd, hprev, ypre, pre, dy, *ssd_args, ET, comm=send_out)
    dw_ssd = _matmul_tn(xb, d_ssd, tl=2048, tn=S_W // 2, name="dw_in_ssd")
    parts_lo, own_lo = _pack_dw_in(me1, dw_ssd, None, 0)
    recv_shape = bf16_shard((N_DEV - 1, D_MODEL, SHARD_COLS))
    send_lo = _Hosted([parts_lo], [recv_shape], [_Flow("exchange", 0, 0, target_x=0, target_c=0)])
    d_att, dsink, recv_in = _swa_backward(proj_att, tabs, sinks, dy, comm=send_lo)
    dw_att = _matmul_tn(xb, d_att, tl=2048, tn=A_W // 2, name="dw_in_att")
    (stack_hi,) = _pack_dw_in(me1, dw_ssd, dw_att, 1)
    pos3 = jnp.stack(_position()).astype(jnp.int32)
    chip_hi, own_in = _pair_sum(pos3, stack_hi, _pair_swap(stack_hi), own_lo)
    accs = [acc_head, acc_cw, acc_w, acc_s, dsink]
    send_hi = _Hosted([chip_hi, recv_in, parts_lo] + accs,
                      [recv_shape] + [jax.ShapeDtypeStruct((N_DEV,) + a.shape, F32) for a in accs],
                      [_Flow("chip_exchange", 0, 0, target_x=1), _Flow("exchange", 2, 0, target_x=0, target_c=1)]
                      + [_Flow("gather", 3 + i, 1 + i) for i in range(5)], aliases={1: 0})
    dx, recv_in, *gathered = _input_gradient(d_ssd, d_att, w_ssd, w_att, dr, tm=256, comm=send_hi)
    n_recv_in = jnp.where(me < 4, N_DEV - 1, 3).reshape(1).astype(jnp.int32)
    n_recv_out = jnp.full((1,), N_DEV - 1, jnp.int32)

    g_in, d_in, nm_in, nv_in = [_from_minor_rows_view(r) for r in _adamw_w_in(
        n_recv_in, own_in, recv_in, _minor_rows_view(w_in), _minor_rows_view(m_w_in), _minor_rows_view(v_w_in))]
    g_out, d_out, nm_out, nv_out = _adamw_shard(n_recv_out, own_out, recv_out, w_out[0], m_w_out[0], v_w_out[0],
                                                rows=256, name="adamw_w_out")
    loss, g_conv_w, small = _adamw_small(gathered, dict(
        conv_b=(conv_b, m_conv_b, v_conv_b), dt_bias=(dt_bias, m_dt_bias, v_dt_bias), a_log=(a_log, m_a_log, v_a_log),
        d_skip=(d_skip, m_d_skip, v_d_skip), ssd_norm_w=(ssd_norm_w, m_ssd_norm_w, v_ssd_norm_w),
        attn_sinks=(attn_sinks, m_attn_sinks, v_attn_sinks), ln_g=(ln_g, m_ln_g, v_ln_g), ln_b=(ln_b, m_ln_b, v_ln_b)))
    g_cw = lax.dynamic_slice_in_dim(g_conv_w, me * (D_XBC // N_DEV), D_XBC // N_DEV, axis=1)
    d_cw, nm_cw, nv_cw = _adamw_plain(g_cw, conv_w[0], m_conv_w[0], v_conv_w[0])

    def leaves(i, big_in, cw, big_out):
        mid = [small[k][i] for k in ("conv_b", "dt_bias", "a_log", "d_skip", "ssd_norm_w", "attn_sinks")]
        return [big_in, cw[None]] + mid + [big_out[None], small["ln_g"][i], small["ln_b"][i]]

    return (loss.reshape(()), dx[None], *leaves(0, g_in, g_cw, g_out), *leaves(1, d_in, d_cw, d_out),
            *leaves(2, nm_in, nm_cw, nm_out), *leaves(3, nv_in, nv_cw, nv_out))
```

```python
import jax
import jax.numpy as jnp
from jax import lax
from jax.experimental import pallas as pl
from jax.experimental.pallas import tpu as pltpu
import numpy as np

F32 = jnp.float32
BF16 = jnp.bfloat16
_MXU = jnp.bfloat16

N_DEV = 8
D_MODEL = 1024
D_SSD = 1024
D_ATT = 1024
HEAD_DIM = 64
N_HEADS = 16
SSD_GROUPS = 2
KV_HEADS = 4
CHUNK = 128
D_XBC = 1536
D_IN_PROJ = 5136
ROPE_DIM = 16
ROPE_THETA = 500000.0
ALPHA = (2.0 * 1) ** 0.25
LN_EPS = 1e-5
RMS_EPS = 1e-5
ATT_SCALE = HEAD_DIM ** -0.5
NEG = -1e30

S_Z, S_XS, S_B, S_C, S_DT, S_W = 0, 1024, 2048, 2304, 2560, 2816
N_SSD_REAL = 2576
A_Q, A_K, A_V, A_G, A_W = 0, 1024, 1280, 1536, 2560

ADAM_LR = 0.001
ADAM_B1 = 0.9
ADAM_B2 = 0.999
ADAM_EPS = 1e-08
ADAM_WD = 0.01
ADAM_STEP = 10

VMEM_LIMIT = 48 * 1024 * 1024
MESH = pl.DeviceIdType.MESH


def _params(sem=None):
    return pltpu.CompilerParams(dimension_semantics=sem, vmem_limit_bytes=VMEM_LIMIT)


def _mm(a, b):
    return jnp.dot(a.astype(_MXU), b.astype(_MXU), preferred_element_type=F32)


def _mm_nt(a, b):
    return lax.dot_general(a.astype(_MXU), b.astype(_MXU), (((1,), (1,)), ((), ())),
                           preferred_element_type=F32)


def _mm_tn(a, b):
    return lax.dot_general(a.astype(_MXU), b.astype(_MXU), (((0,), (0,)), ((), ())),
                           preferred_element_type=F32)


def _split3(v):
    hi = v.astype(BF16)
    r = v - hi.astype(F32)
    mid = r.astype(BF16)
    lo = (r - mid.astype(F32)).astype(BF16)
    return hi, mid, lo


def _mm_exact_r(v, p01):
    hi, mid, lo = _split3(v)
    d = lambda a: jnp.dot(a, p01, preferred_element_type=F32)
    return d(hi) + d(mid) + d(lo)


def _mm_exact_l(p01, v):
    hi, mid, lo = _split3(v)
    d = lambda a: jnp.dot(p01, a, preferred_element_type=F32)
    return d(hi) + d(mid) + d(lo)


def _mm_2pass_r(v, p01):
    hi = v.astype(BF16)
    lo = (v - hi.astype(F32)).astype(BF16)
    return jnp.dot(hi, p01, preferred_element_type=F32) + jnp.dot(lo, p01, preferred_element_type=F32)


def _sigmoid(x):
    return 1.0 / (1.0 + jnp.exp(-x))


def _softplus(x):
    e = jnp.exp(-jnp.abs(x))
    u = 1.0 + e
    log1p = jnp.where(u == 1.0, e, jnp.log(u) * (e / (u - 1.0)))
    return jnp.maximum(x, 0.0) + log1p


def _rows8(rows):
    n = rows[0].shape[1]
    rid = lax.broadcasted_iota(jnp.int32, (8, n), 0)
    out = jnp.zeros((8, n), F32)
    for k, r in enumerate(rows):
        out = out + jnp.where(rid == k, r, 0.0)
    return out


def _colsum(a):
    return jnp.sum(a, axis=0, keepdims=True)


def _in_proj(x, w_ssd, w_att, *, tm, comm=None):
    L, K = x.shape

    def body(x_ref, ws_ref, wa_ref, ps_ref, pa_ref, xb_ref):
        xb = x_ref[...].astype(_MXU)
        xb_ref[...] = xb
        ps_ref[...] = jnp.dot(xb, ws_ref[...], preferred_element_type=F32)
        pa_ref[...] = jnp.dot(xb, wa_ref[...], preferred_element_type=F32)

    row = lambda w: pl.BlockSpec((tm, w), lambda i: (i, 0))
    resident = lambda a: pl.BlockSpec(a.shape, lambda i: (0, 0), pipeline_mode=pl.Buffered(1))
    return _call(
        body, comm, name="in_proj", grid=(L // tm,),
        in_specs=[row(K), resident(w_ssd), resident(w_att)], out_specs=[row(S_W), row(A_W), row(K)],
        out_shape=[jax.ShapeDtypeStruct((L, S_W), F32), jax.ShapeDtypeStruct((L, A_W), F32),
                   jax.ShapeDtypeStruct((L, K), _MXU)],
        scratch_shapes=[], args=(x, w_ssd, w_att))


def _matmul_tn(a, g, *, tl, tn, name, emit_bf16=False):
    L, M = a.shape
    N = g.shape[1]
    last = L // tl - 1

    def body(a_ref, g_ref, o_ref, *rest):
        @pl.when(pl.program_id(1) == 0)
        def _():
            o_ref[...] = jnp.zeros_like(o_ref)

        o_ref[...] += _mm_tn(a_ref[...], g_ref[...])
        if emit_bf16:
            @pl.when(pl.program_id(1) == last)
            def _():
                rest[0][...] = o_ref[...].astype(BF16)

    spec = pl.BlockSpec((M, tn), lambda j, l: (0, j))
    res = pl.pallas_call(
        body, name=name, grid=(N // tn, L // tl),
        in_specs=[pl.BlockSpec((tl, M), lambda j, l: (l, 0)), pl.BlockSpec((tl, tn), lambda j, l: (l, j))],
        out_specs=[spec, spec] if emit_bf16 else [spec],
        out_shape=[jax.ShapeDtypeStruct((M, N), F32)] + ([jax.ShapeDtypeStruct((M, N), BF16)] if emit_bf16 else []),
        compiler_params=_params(("arbitrary", "arbitrary")),
    )(a, g)
    return res if emit_bf16 else res[0]


def _position():
    return lax.axis_index("x"), lax.axis_index("y"), lax.axis_index("c")


def _index(px, py, pc):
    return 4 * px + 2 * py + pc


def _flip(pos, k):
    x, y, c = pos
    return ((1 - x) if (k >> 2) & 1 else x, (1 - y) if (k >> 1) & 1 else y, (1 - c) if k & 1 else c)


def _when(cond, fn):
    if cond is True:
        fn()
    else:
        pl.when(cond)(fn)


def _remote(src, dst, send_sem, recv_sem, peer):
    return pltpu.make_async_remote_copy(src_ref=src, dst_ref=dst, send_sem=send_sem, recv_sem=recv_sem,
                                        device_id=peer, device_id_type=MESH)


class _Flow:
    def __init__(self, kind, operand, result, target_x=None, target_c=None):
        self.kind, self.operand, self.result, self.target_x, self.target_c = kind, operand, result, target_x, target_c

    def owns(self, pos):
        if self.target_x is None:
            return True
        cond = pos[0] == self.target_x
        return cond if self.target_c is None else cond & (pos[2] == self.target_c)


class _Hosted:
    def __init__(self, operands, out_shapes, flows, aliases=None):
        self.operands, self.out_shapes, self.flows = operands, out_shapes, flows
        self.aliases = aliases or {}

    def plan(self, ins, outs, send_sems, recv_sems, local_sems):
        me = _position()
        mi = _index(*me)
        sends, recvs, locals_ = [], [], []
        for row, f in enumerate(self.flows):
            src, dst = ins[f.operand], outs[f.result]
            for k in range(1, N_DEV):
                peer = _flip(me, k)
                sems = (send_sems.at[row, k - 1], recv_sems.at[row, k - 1])
                if f.kind == "exchange":
                    owner = _index(*peer) if f.target_x is None else 2 * peer[1] + peer[2]
                    cp = _remote(src.at[owner], dst.at[k - 1], *sems, peer)
                    sends.append((f.owns(peer), cp))
                    recvs.append((f.owns(me), cp))
                elif f.kind == "chip_exchange":
                    if k & 1:
                        continue
                    cp = _remote(src.at[peer[1]], dst.at[k // 2 - 1], *sems, peer)
                    sends.append((peer[0] == f.target_x, cp))
                    recvs.append((me[0] == f.target_x, cp))
                else:
                    sends.append((True, _remote(src, dst.at[mi], *sems, peer)))
                    recvs.append((True, _remote(src, dst.at[_index(*peer)], *sems, peer)))
            if f.kind == "gather":
                locals_.append(pltpu.make_async_copy(src, dst.at[mi], local_sems.at[row]))

        def start():
            for cp in locals_:
                cp.start()
            for cond, cp in sends:
                _when(cond, cp.start)

        def wait():
            for cond, cp in recvs:
                _when(cond, cp.wait_recv)
            for cond, cp in sends:
                _when(cond, cp.wait_send)
            for cp in locals_:
                cp.wait()

        return start, wait


def _call(body, comm, *, name, grid, in_specs, out_specs, out_shape, scratch_shapes, args, aliases=None):
    io_alias = dict(aliases or {})
    semantics = ("arbitrary",) * len(grid)
    if comm is None:
        return pl.pallas_call(body, name=name, grid=grid, in_specs=in_specs, out_specs=out_specs, out_shape=out_shape,
                              scratch_shapes=scratch_shapes, input_output_aliases=io_alias,
                              compiler_params=_params(semantics))(*args)
    n_in, n_out, n_scr = len(args), len(out_shape), len(scratch_shapes)
    c_in, c_out, rows = len(comm.operands), len(comm.out_shapes), len(comm.flows)

    def hosted(*refs):
        ins, refs = refs[:n_in], refs[n_in:]
        cins, refs = refs[:c_in], refs[c_in:]
        outs, refs = refs[:n_out], refs[n_out:]
        couts, refs = refs[:c_out], refs[c_out:]
        scr, (send_sems, recv_sems, local_sems) = refs[:n_scr], refs[n_scr:]
        start, wait = comm.plan(cins, couts, send_sems, recv_sems, local_sems)
        ids = [pl.program_id(d) for d in range(len(grid))]
        first, last = ids[0] == 0, ids[0] == grid[0] - 1
        for d in range(1, len(grid)):
            first, last = first & (ids[d] == 0), last & (ids[d] == grid[d] - 1)
        pl.when(first)(start)
        body(*ins, *outs, *scr)
        pl.when(last)(wait)

    for ci, co in comm.aliases.items():
        io_alias[n_in + ci] = n_out + co
    any_spec = pl.BlockSpec(memory_space=pl.ANY)
    sems = [pltpu.SemaphoreType.DMA((rows, N_DEV - 1)), pltpu.SemaphoreType.DMA((rows, N_DEV - 1)),
            pltpu.SemaphoreType.DMA((rows,))]
    return pl.pallas_call(
        hosted, name=name, grid=grid, in_specs=list(in_specs) + [any_spec] * c_in,
        out_specs=list(out_specs) + [any_spec] * c_out, out_shape=list(out_shape) + list(comm.out_shapes),
        scratch_shapes=list(scratch_shapes) + sems, input_output_aliases=io_alias,
        compiler_params=_params(semantics))(*args, *comm.operands)


def _head_row(ref, width, rep):
    hid = lax.broadcasted_iota(jnp.int32, (1, width), 1) // rep
    row = jnp.zeros((1, width), F32)
    for h in range(N_HEADS):
        row = jnp.where(hid == h, ref[h], row)
    return row


def _rows_from_above(u_b, s, ext_scr, row, col):
    down = (row - col == s).astype(_MXU)
    return jnp.concatenate([ext_scr[8 - s:16 - s, :], jnp.dot(down, u_b, preferred_element_type=F32)[8:128]], axis=0)


def _ssd_recompute(first, p_ref, halo_ref, cw_ref, cb_ref, dtb_ref, alog_ref, e_ref, ext_scr, pre=None):
    row = lax.broadcasted_iota(jnp.int32, (128, 128), 0)
    col = lax.broadcasted_iota(jnp.int32, (128, 128), 1)
    ext_scr[0:8, :] = jnp.where(first, 0.0, halo_ref[:, S_XS:S_DT])
    if pre is not None:
        ext_scr[8:16, :] = p_ref[0:8, S_XS:S_DT]
    else:
        ext_scr[8:136, :] = p_ref[:, S_XS:S_DT]
        cw = cw_ref[...]
        pre = (cb_ref[0:1, :] + cw[3:4, :] * ext_scr[8:136, :] + cw[2:3, :] * ext_scr[7:135, :]
               + cw[1:2, :] * ext_scr[6:134, :] + cw[0:1, :] * ext_scr[5:133, :])
    sg = _sigmoid(pre)
    act = pre * sg
    lane = lax.broadcasted_iota(jnp.int32, (1, 128), 1)
    A = jnp.where(lane < N_HEADS, -jnp.exp(_head_row(alog_ref, 128, 1)), 0.0)
    raw = p_ref[:, S_DT:S_DT + 128] + _head_row(dtb_ref, 128, 1)
    dt = _softplus(raw)
    dA = dt * A
    tril = (row >= col).astype(BF16)
    acs = _mm_exact_l(tril, dA)
    last = acs[127:128, :]
    ds = jnp.exp(last - acs)
    eo = jnp.exp(acs)
    E = e_ref[...]
    ex = _mm_2pass_r(jnp.concatenate([dt, ds, eo], axis=0), E)
    dt_e, ds_e, eo_e = ex[0:128], ex[128:256], ex[256:384]
    xs_c = act[:, 0:1024]
    X = xs_c * dt_e
    return dict(pre=pre, sg=sg, xs_c=xs_c, Bc=act[:, 1024:1280], Cc=act[:, 1280:1536], A=A, raw=raw, dt=dt,
                acs=acs, acsT=acs.T, eo_e=eo_e, ds_e=ds_e, dt_e=dt_e, cd_e=eo_e[127:128, :],
                X=X, Xd=X * ds_e, row=row, col=col)


def _split_halves(t):
    lo = _lo_half(CHUNK)
    return jnp.concatenate([jnp.where(lo, t, 0.0), jnp.where(lo, 0.0, t)], axis=0)


def _ssd_core(R, hprev):
    causal = R["row"] >= R["col"]
    acs, acsT, X = R["acs"], R["acsT"], R["X"]
    ydiag, yoff, snew = [], [], []
    for g in range(SSD_GROUPS):
        Bg = R["Bc"][:, g * 128:(g + 1) * 128]
        Cg = R["Cc"][:, g * 128:(g + 1) * 128]
        cols = slice(g * 512, (g + 1) * 512)
        CB = _mm_nt(Cg, Bg)
        snew.append(_mm_tn(Bg, R["Xd"][:, cols]))
        yoff.append(_mm(Cg, hprev[:, cols]))
        for j in range(4):
            h0 = g * 8 + 2 * j
            ms = [CB * jnp.exp(jnp.where(causal, acs[:, h:h + 1] - acsT[h:h + 1, :], NEG)) for h in (h0, h0 + 1)]
            ydiag.append(_mm(jnp.concatenate(ms, axis=1), _split_halves(X[:, h0 * HEAD_DIM:h0 * HEAD_DIM + 128])))
    Y = jnp.concatenate(ydiag, axis=1) + jnp.concatenate(yoff, axis=1) * R["eo_e"]
    return Y, jnp.concatenate(snew, axis=1)


def _ssd_forward_step(p_ref, halo_ref, cw_ref, cb_ref, dtb_ref, alog_ref, dsk_ref, nw_ref, e_ref,
                      y_ref, ypre_ref, hprev_ref, pre_ref, h_scr, ext_scr):
    c = pl.program_id(0)
    first = c == 0

    @pl.when(first)
    def _():
        h_scr[...] = jnp.zeros_like(h_scr)

    R = _ssd_recompute(first, p_ref, halo_ref, cw_ref, cb_ref, dtb_ref, alog_ref, e_ref, ext_scr)
    hprev = h_scr[...]
    hprev_ref[...] = hprev
    pre_ref[...] = R["pre"]
    Y, snew = _ssd_core(R, hprev)
    h_scr[...] = hprev * R["cd_e"] + snew
    Y = Y + _head_row(dsk_ref, D_SSD, HEAD_DIM) * R["xs_c"]
    ypre_ref[...] = Y
    z = p_ref[:, S_Z:S_Z + 1024]
    yf = Y * (z * _sigmoid(z))
    outs = []
    for g in range(SSD_GROUPS):
        yg = yf[:, g * 512:(g + 1) * 512]
        r = lax.rsqrt(jnp.mean(yg * yg, axis=-1, keepdims=True) + RMS_EPS)
        outs.append(yg * r)
    y_ref[:, 0:D_SSD] = (jnp.concatenate(outs, axis=1) * nw_ref[0:1, :]).astype(y_ref.dtype)


def _ssd_backward(proj_ssd, hprev_all, ypre, pre, dy, conv_w, conv_b, dt_bias, a_log, d_skip, norm_w, E, ET, comm=None):
    L = proj_ssd.shape[0]
    nc = L // CHUNK

    def body(p_ref, halo_ref, hprev_ref, ypre_ref, pre_ref, dy_ref, cw_ref, cb_ref, dtb_ref, alog_ref, dsk_ref, nw_ref, e_ref,
             et_ref, dp_ref, acc_cw_ref, acc_w_ref, acc_s_ref, dh_scr, ext_scr, ext2_scr, nxt_scr):
        i = pl.program_id(0)
        c = nc - 1 - i
        first = c == 0

        @pl.when(i == 0)
        def _():
            dh_scr[...] = jnp.zeros_like(dh_scr)
            nxt_scr[...] = jnp.zeros_like(nxt_scr)
            acc_cw_ref[...] = jnp.zeros_like(acc_cw_ref)
            acc_w_ref[...] = jnp.zeros_like(acc_w_ref)
            acc_s_ref[...] = jnp.zeros_like(acc_s_ref)

        R = _ssd_recompute(first, p_ref, halo_ref, cw_ref, cb_ref, dtb_ref, alog_ref, e_ref, ext_scr, pre_ref[...])
        hprev = hprev_ref[...]
        xs_c, X, Xd = R["xs_c"], R["X"], R["Xd"]
        acs, acsT = R["acs"], R["acsT"]
        ET = et_ref[...]
        dsk = _head_row(dsk_ref, D_SSD, HEAD_DIM)
        Y = ypre_ref[...]

        z = p_ref[:, S_Z:S_Z + 1024]
        sz = _sigmoid(z)
        silz = z * sz
        yf = Y * silz
        dyv = dy_ref[...]
        nw = nw_ref[0:1, :]
        dyf_parts, dnw_parts = [], []
        for g in range(SSD_GROUPS):
            cols = slice(g * 512, (g + 1) * 512)
            yg = yf[:, cols]
            r = lax.rsqrt(jnp.mean(yg * yg, axis=-1, keepdims=True) + RMS_EPS)
            yn = yg * r
            dyn = dyv[:, cols] * nw[:, cols]
            dnw_parts.append(_colsum(dyv[:, cols] * yn))
            dyf_parts.append(r * (dyn - yn * jnp.mean(dyn * yn, axis=-1, keepdims=True)))
        dyf = jnp.concatenate(dyf_parts, axis=1)
        dY = dyf * silz
        dz = dyf * Y * (sz * (1.0 + z * (1.0 - sz)))

        dhn = dh_scr[...]
        dYo = dY * R["eo_e"]
        causal = R["row"] >= R["col"]
        dacs = jnp.zeros((128, 128), F32)
        dacs_t = jnp.zeros((128, 128), F32)
        dxdiag, dxd, dhprev, dBs, dCs, yoff = [], [], [], [], [], []
        for g in range(SSD_GROUPS):
            Bg = R["Bc"][:, g * 128:(g + 1) * 128]
            Cg = R["Cc"][:, g * 128:(g + 1) * 128]
            cols = slice(g * 512, (g + 1) * 512)
            CB = _mm_nt(Cg, Bg)
            dCB = jnp.zeros((128, 128), F32)
            for j in range(4):
                h0 = g * 8 + 2 * j
                pc = slice(h0 * HEAD_DIM, h0 * HEAD_DIM + 128)
                dYst = _split_halves(dY[:, pc])
                dMst = _mm_nt(dYst, X[:, pc])
                mts = []
                for a, h in enumerate((h0, h0 + 1)):
                    acol = acs[:, h:h + 1]
                    arow = acsT[h:h + 1, :]
                    Lm = jnp.exp(jnp.where(causal, acol - arow, NEG))
                    M = CB * Lm
                    dM = dMst[a * 128:(a + 1) * 128]
                    dCB = dCB + dM * Lm
                    G = dM * M
                    dacs = dacs + jnp.where(R["col"] == h, jnp.sum(G, axis=1, keepdims=True), 0.0)
                    dacs_t = dacs_t + jnp.where(R["row"] == h, jnp.sum(G, axis=0, keepdims=True), 0.0)
                    mts.append(M.T)
                dxdiag.append(_mm(jnp.concatenate(mts, axis=1), dYst))
            dS = dhn[:, cols]
            dxd.append(_mm(Bg, dS))
            yoff.append(_mm(Cg, hprev[:, cols]))
            dhprev.append(_mm_tn(Cg, dYo[:, cols]))
            dCs.append(_mm_nt(dYo[:, cols], hprev[:, cols]) + _mm(dCB, Bg))
            dBs.append(_mm_tn(dCB, Cg) + _mm_nt(Xd[:, cols], dS))
        Yoff = jnp.concatenate(yoff, axis=1) * R["eo_e"]
        dXd = jnp.concatenate(dxd, axis=1)
        dX = jnp.concatenate(dxdiag, axis=1) + dXd * R["ds_e"]
        t_state = dXd * Xd
        dacs = dacs + _mm_2pass_r(dY * Yoff - t_state, ET) - dacs_t.T
        v_last = _colsum(t_state + dhn * hprev * R["cd_e"])
        dlast = _mm_exact_r(jnp.broadcast_to(v_last, (8, 1024)), ET)[0:1, :]
        dacs = dacs + jnp.where(R["row"] == 127, dlast, 0.0)
        triu = (R["col"] >= R["row"]).astype(BF16)
        da = _mm_exact_l(triu, dacs)
        ddt = da * R["A"] + _mm(dX * xs_c, ET)
        ddt_raw = ddt * _sigmoid(R["raw"])
        dxs_c = dX * R["dt_e"] + dY * dsk
        dh_scr[...] = jnp.concatenate(dhprev, axis=1) + dhn * R["cd_e"]

        dact = jnp.concatenate([dxs_c] + dBs + dCs, axis=1)
        pre, sg = R["pre"], R["sg"]
        dpre = dact * (sg * (1.0 + pre * (1.0 - sg)))
        ext2_scr[0:8, :] = dpre[120:128, :]
        ext2_scr[8:16, :] = nxt_scr[...]
        nxt_scr[...] = dpre[0:8, :]
        cw = cw_ref[...]
        u_b, dpre_b = p_ref[:, S_XS:S_DT].astype(_MXU), dpre.astype(_MXU)
        dxbc = cw[3:4, :] * dpre
        taps = [_colsum(dpre * p_ref[:, S_XS:S_DT])]
        for s in (1, 2, 3):
            up = (R["col"] - R["row"] == s).astype(_MXU)
            d_s = jnp.concatenate([jnp.dot(up, dpre_b, preferred_element_type=F32)[0:120],
                                   ext2_scr[s:8 + s, :]], axis=0)
            dxbc = dxbc + cw[3 - s:4 - s, :] * d_s
            taps.append(_colsum(dpre * _rows_from_above(u_b, s, ext_scr, R["row"], R["col"])))
        acc_cw_ref[...] += _rows8(taps[::-1] + [_colsum(dpre)])
        acc_w_ref[...] += _rows8([jnp.concatenate(dnw_parts, axis=1), _colsum(dY * xs_c)])
        acc_s_ref[...] += _rows8([_colsum(ddt_raw), _colsum(da * R["dt"])])

        lane = lax.broadcasted_iota(jnp.int32, (128, 128), 1)
        dp_ref[:, S_Z:S_Z + 1024] = dz.astype(dp_ref.dtype)
        dp_ref[:, S_XS:S_DT] = dxbc.astype(dp_ref.dtype)
        dp_ref[:, S_DT:S_DT + 128] = jnp.where(lane < N_HEADS, ddt_raw, 0.0).astype(dp_ref.dtype)
        dp_ref[:, S_DT + 128:S_W] = jnp.zeros((128, 128), dp_ref.dtype)

        @pl.when(i == nc - 1)
        def _():
            acc = acc_s_ref[...]
            dskip = _mm_exact_r(acc_w_ref[...], ET)[1:2, :]
            acc_s_ref[...] = _rows8([acc[0:1, :], acc[1:2, :] * R["A"], dskip])

    const = lambda shape: pl.BlockSpec(shape, lambda i: (0, 0))
    smem = pl.BlockSpec(memory_space=pltpu.SMEM)
    rev = lambda i: (nc - 1 - i, 0)
    return _call(
        body, comm, name="ssd_bwd", grid=(nc,),
        in_specs=[pl.BlockSpec((CHUNK, S_W), rev),
                  pl.BlockSpec((8, S_W), lambda i: (jnp.maximum((nc - 1 - i) * 16 - 1, 0), 0)),
                  pl.BlockSpec((128, 1024), rev),
                  pl.BlockSpec((CHUNK, D_SSD), rev),
                  pl.BlockSpec((CHUNK, D_XBC), rev),
                  pl.BlockSpec((CHUNK, D_SSD), rev),
                  const((4, D_XBC)), const((1, D_XBC)), smem, smem, smem, const((1, 1024)),
                  const((128, 1024)), const((1024, 128))],
        out_specs=[pl.BlockSpec((CHUNK, S_W), rev), const((8, D_XBC)), const((8, 1024)), const((8, 128))],
        out_shape=[jax.ShapeDtypeStruct((L, S_W), _MXU), jax.ShapeDtypeStruct((8, D_XBC), F32),
                   jax.ShapeDtypeStruct((8, 1024), F32), jax.ShapeDtypeStruct((8, 128), F32)],
        scratch_shapes=[pltpu.VMEM((128, 1024), F32), pltpu.VMEM((16, D_XBC), F32),
                        pltpu.VMEM((16, D_XBC), F32), pltpu.VMEM((8, D_XBC), F32)],
        args=(proj_ssd, proj_ssd, hprev_all, ypre, pre, dy, conv_w, conv_b, dt_bias, a_log, d_skip, norm_w, E, ET))


def _rope(t, tab):
    cos, sa, sb = tab[:, 0:128], tab[:, 128:256], tab[:, 256:384]
    outs = []
    for i in range(t.shape[1] // 128):
        tg = t[:, i * 128:(i + 1) * 128]
        outs.append(tg * cos + pltpu.roll(tg, 8, 1) * sa + pltpu.roll(tg, 120, 1) * sb)
    return jnp.concatenate(outs, axis=1)


def _rope_transposed(d, tab):
    cos, sa, sb = tab[:, 0:128], tab[:, 128:256], tab[:, 256:384]
    outs = []
    for i in range(d.shape[1] // 128):
        dg = d[:, i * 128:(i + 1) * 128]
        outs.append(dg * cos + pltpu.roll(dg * sa, 120, 1) + pltpu.roll(dg * sb, 8, 1))
    return jnp.concatenate(outs, axis=1)


def _lo_half(rows):
    return lax.broadcasted_iota(jnp.int32, (rows, 128), 1) < HEAD_DIM


def _native_half(rows, j):
    lo = _lo_half(rows)
    return lo if j % 2 == 0 else jnp.logical_not(lo)


def _kv_native(t, j):
    p = j // 2
    return jnp.where(_native_half(t.shape[0], j), t[:, p * 128:(p + 1) * 128], 0.0)


def _stack_heads(t, j):
    out = []
    for m in (2 * j, 2 * j + 1):
        pair = t[:, m * 128:(m + 1) * 128]
        swapped = pltpu.roll(pair, HEAD_DIM, 1)
        out += [pair, swapped] if j % 2 == 0 else [swapped, pair]
    return jnp.concatenate(out, axis=0)


def _unstack_heads(s, j):
    out = []
    for m in range(2):
        first, second = s[256 * m:256 * m + 128], s[256 * m + 128:256 * m + 256]
        if j % 2 == 0:
            out.append(first + pltpu.roll(second, HEAD_DIM, 1))
        else:
            out.append(pltpu.roll(first, HEAD_DIM, 1) + second)
    return jnp.concatenate(out, axis=1)


def _keep_native(r, j):
    return jnp.where(_native_half(r.shape[0], j), r, 0.0)


def _sink_row(sink_ref, j):
    hid = lax.broadcasted_iota(jnp.int32, (1, 4 * CHUNK), 1) // CHUNK
    row = jnp.zeros((1, 4 * CHUNK), F32)
    for hh in range(4):
        row = jnp.where(hid == hh, sink_ref[4 * j + hh], row)
    return row


def _from_current():
    si = lax.broadcasted_iota(jnp.int32, (CHUNK, 4 * CHUNK), 0)
    qi = lax.broadcasted_iota(jnp.int32, (CHUNK, 4 * CHUNK), 1) % CHUNK
    return si <= qi


def _fold(full, from_cur, pen=0.0):
    return jnp.where(from_cur, full[CHUNK:2 * CHUNK], full[0:CHUNK] + pen)


def _unfold(t, from_cur):
    c = jnp.where(from_cur, t, 0.0)
    return jnp.concatenate([t - c, c], axis=0)


def _softmax_sink(s, sink):
    mx = jnp.maximum(jnp.max(s, axis=0, keepdims=True), sink)
    p = jnp.exp(s - mx)
    esink = jnp.exp(sink - mx)
    inv = 1.0 / (jnp.sum(p, axis=0, keepdims=True) + esink)
    return p * inv, esink * inv


def _swa_inputs(blk, p_ref, prev_ref, tab_ref, ptab_ref):
    tab = tab_ref[...]
    qr = _rope(p_ref[:, A_Q:A_Q + 1024], tab) * ATT_SCALE
    kk = jnp.concatenate([_rope(prev_ref[:, 0:256], ptab_ref[...]), _rope(p_ref[:, A_K:A_K + 256], tab)], axis=0)
    vv = jnp.concatenate([prev_ref[:, 256:512], p_ref[:, A_V:A_V + 256]], axis=0)
    return tab, qr, kk, vv, jnp.where(blk > 0, 0.0, NEG)


def _swa_forward_step(sink_ref, p_ref, prev_ref, tab_ref, ptab_ref, y_ref):
    n = pl.program_id(0)
    _, qr, kk, vv, pen = _swa_inputs(n, p_ref, prev_ref, tab_ref, ptab_ref)
    from_cur = _from_current()
    outs = []
    for j in range(KV_HEADS):
        s = _fold(_mm_nt(_kv_native(kk, j), _stack_heads(qr, j)), from_cur, pen)
        P, _ = _softmax_sink(s, _sink_row(sink_ref, j))
        outs.append(_unstack_heads(_mm_tn(_unfold(P, from_cur), _kv_native(vv, j)), j))
    g = p_ref[:, A_G:A_G + 1024]
    y_ref[:, D_SSD:D_SSD + D_ATT] = (jnp.concatenate(outs, axis=1) * (g * _sigmoid(g))).astype(y_ref.dtype)


def _mixer_forward(proj_ssd, proj_att, tabs, sinks, conv_w, conv_b, dt_bias, a_log, d_skip, norm_w, E, comm=None):
    L = proj_ssd.shape[0]
    nc = L // CHUNK

    def body(p_ref, halo_ref, cw_ref, cb_ref, dtb_ref, alog_ref, dsk_ref, nw_ref, e_ref,
             sink_ref, pa_ref, prev_ref, tab_ref, ptab_ref, y_ref, ypre_ref, hprev_ref, pre_ref, h_scr, ext_scr):
        _ssd_forward_step(p_ref, halo_ref, cw_ref, cb_ref, dtb_ref, alog_ref, dsk_ref, nw_ref, e_ref,
                          y_ref, ypre_ref, hprev_ref, pre_ref, h_scr, ext_scr)
        _swa_forward_step(sink_ref, pa_ref, prev_ref, tab_ref, ptab_ref, y_ref)

    const = lambda shape: pl.BlockSpec(shape, lambda c: (0, 0))
    smem = pl.BlockSpec(memory_space=pltpu.SMEM)
    rows = lambda w: pl.BlockSpec((CHUNK, w), lambda c: (c, 0))
    return _call(
        body, comm, name="mixer_fwd", grid=(nc,),
        in_specs=[rows(S_W), pl.BlockSpec((8, S_W), lambda c: (jnp.maximum(c * 16 - 1, 0), 0)),
                  const((4, D_XBC)), const((1, D_XBC)), smem, smem, smem, const((1, 1024)), const((128, 1024)),
                  smem, rows(A_W), pl.BlockSpec((CHUNK, 512), lambda c: (jnp.maximum(c - 1, 0), 2)),
                  rows(384), pl.BlockSpec((CHUNK, 384), lambda c: (jnp.maximum(c - 1, 0), 0))],
        out_specs=[rows(D_SSD + D_ATT), rows(D_SSD), pl.BlockSpec((128, 1024), lambda c: (c, 0)), rows(D_XBC)],
        out_shape=[jax.ShapeDtypeStruct((L, D_SSD + D_ATT), _MXU), jax.ShapeDtypeStruct((L, D_SSD), F32),
                   jax.ShapeDtypeStruct((nc * 128, 1024), F32), jax.ShapeDtypeStruct((L, D_XBC), F32)],
        scratch_shapes=[pltpu.VMEM((128, 1024), F32), pltpu.VMEM((136, D_XBC), F32)],
        args=(proj_ssd, proj_ssd, conv_w, conv_b, dt_bias, a_log, d_skip, norm_w, E,
              sinks, proj_att, proj_att, tabs, tabs))


def _swa_backward(proj_att, tabs, sinks, dy, comm=None):
    L = proj_att.shape[0]
    nb = L // CHUNK

    def body(sink_ref, p_ref, prev_ref, tab_ref, ptab_ref, dy_ref, dp_ref, dsink_ref, carry_k, carry_v):
        i = pl.program_id(0)
        n = nb - 1 - i

        @pl.when(i == 0)
        def _():
            carry_k[...] = jnp.zeros_like(carry_k)
            carry_v[...] = jnp.zeros_like(carry_v)
            dsink_ref[...] = jnp.zeros_like(dsink_ref)

        tab, qr, kk, vv, pen = _swa_inputs(n, p_ref, prev_ref, tab_ref, ptab_ref)
        from_cur = _from_current()
        g = p_ref[:, A_G:A_G + 1024]
        sgm = _sigmoid(g)
        dyv = dy_ref[...]
        do_all = dyv * (g * sgm)
        lane8 = lax.broadcasted_iota(jnp.int32, (8, 128), 1)
        hid = lax.broadcasted_iota(jnp.int32, (1, 4 * CHUNK), 1) // CHUNK
        o_parts, dq_parts = [], []
        dk_nat = [jnp.zeros((2 * CHUNK, 128), F32) for _ in range(2)]
        dv_nat = [jnp.zeros((2 * CHUNK, 128), F32) for _ in range(2)]
        dsink = jnp.zeros((8, 128), F32)
        for j in range(KV_HEADS):
            qs = _stack_heads(qr, j)
            kkb, vvb = _kv_native(kk, j), _kv_native(vv, j)
            P, psink = _softmax_sink(_fold(_mm_nt(kkb, qs), from_cur, pen), _sink_row(sink_ref, j))
            p_full = _unfold(P, from_cur)
            o_parts.append(_unstack_heads(_mm_tn(p_full, vvb), j))
            do_s = _stack_heads(do_all, j)
            dP = _fold(_mm_nt(vvb, do_s), from_cur)
            D = jnp.sum(P * dP, axis=0, keepdims=True)
            ds_full = _unfold(P * (dP - D), from_cur)
            sd = psink * D
            for hh in range(4):
                dsink = dsink + jnp.where(lane8 == 4 * j + hh, -jnp.sum(jnp.where(hid == hh, sd, 0.0)), 0.0)
            dq_parts.append(_unstack_heads(_mm_tn(ds_full, kkb), j) * ATT_SCALE)
            dk_nat[j // 2] = dk_nat[j // 2] + _keep_native(_mm(ds_full, qs), j)
            dv_nat[j // 2] = dv_nat[j // 2] + _keep_native(_mm(p_full, do_s), j)
        o = jnp.concatenate(o_parts, axis=1)
        dkk = jnp.concatenate(dk_nat, axis=1)
        dvv = jnp.concatenate(dv_nat, axis=1)
        out = dp_ref.dtype
        dp_ref[:, A_Q:A_Q + 1024] = _rope_transposed(jnp.concatenate(dq_parts, axis=1), tab).astype(out)
        dp_ref[:, A_K:A_K + 256] = _rope_transposed(dkk[CHUNK:2 * CHUNK] + carry_k[...], tab).astype(out)
        dp_ref[:, A_V:A_V + 256] = (dvv[CHUNK:2 * CHUNK] + carry_v[...]).astype(out)
        dp_ref[:, A_G:A_G + 1024] = (dyv * o * (sgm * (1.0 + g * (1.0 - sgm)))).astype(out)
        carry_k[...] = dkk[0:CHUNK]
        carry_v[...] = dvv[0:CHUNK]
        dsink_ref[...] += dsink

    rev = lambda i: (nb - 1 - i, 0)
    prev = lambda i: jnp.maximum(nb - 2 - i, 0)
    return _call(
        body, comm, name="swa_bwd", grid=(nb,),
        in_specs=[pl.BlockSpec(memory_space=pltpu.SMEM),
                  pl.BlockSpec((CHUNK, A_W), rev),
                  pl.BlockSpec((CHUNK, 512), lambda i: (prev(i), 2)),
                  pl.BlockSpec((CHUNK, 384), rev),
                  pl.BlockSpec((CHUNK, 384), lambda i: (prev(i), 0)),
                  pl.BlockSpec((CHUNK, D_ATT), lambda i: (nb - 1 - i, 1))],
        out_specs=[pl.BlockSpec((CHUNK, A_W), rev), pl.BlockSpec((8, 128), lambda i: (0, 0))],
        out_shape=[jax.ShapeDtypeStruct((L, A_W), _MXU), jax.ShapeDtypeStruct((8, 128), F32)],
        scratch_shapes=[pltpu.VMEM((CHUNK, 256), F32), pltpu.VMEM((CHUNK, 256), F32)],
        args=(sinks, proj_att, proj_att, tabs, tabs, dy))


def _head(y, x, target, w_out, ln_g, ln_b, *, tm):
    L = x.shape[0]
    nsteps = L // tm

    def body(y_ref, x_ref, t_ref, wo_ref, g_ref, b_ref, dr_ref, dy_ref, acc_ref):
        i = pl.program_id(0)

        @pl.when(i == 0)
        def _():
            acc_ref[...] = jnp.zeros_like(acc_ref)

        r = ALPHA * x_ref[...] + _mm(y_ref[...], wo_ref[...])
        mu = jnp.mean(r, axis=-1, keepdims=True)
        d = r - mu
        rstd = lax.rsqrt(jnp.mean(d * d, axis=-1, keepdims=True) + LN_EPS)
        xh = d * rstd
        gam = g_ref[0:1, :]
        e = xh * gam + b_ref[0:1, :] - t_ref[...]
        dout = e * (1.0 / D_MODEL)
        dxh = dout * gam
        dr = rstd * (dxh - jnp.mean(dxh, axis=-1, keepdims=True)
                     - xh * jnp.mean(dxh * xh, axis=-1, keepdims=True))
        dr_ref[...] = dr
        dy_ref[...] = _mm_nt(dr, wo_ref[...])
        acc_ref[...] += _rows8([_colsum(dout * xh), _colsum(dout), _colsum(e * e) * (0.5 / D_MODEL)])

        @pl.when(i == nsteps - 1)
        def _():
            acc = acc_ref[...]
            tot = jnp.sum(acc[2:3, :])
            rid = lax.broadcasted_iota(jnp.int32, (8, 1024), 0)
            acc_ref[...] = jnp.where(rid == 3, tot, acc)

    const = lambda shape: pl.BlockSpec(shape, lambda i: (0, 0))
    row = lambda w: pl.BlockSpec((tm, w), lambda i: (i, 0))
    return pl.pallas_call(
        body, name="head", grid=(nsteps,),
        in_specs=[row(2048), row(1024), row(1024), const((2048, 1024)), const((1, 1024)), const((1, 1024))],
        out_specs=[row(1024), row(2048), const((8, 1024))],
        out_shape=[jax.ShapeDtypeStruct((L, D_MODEL), F32), jax.ShapeDtypeStruct((L, 2048), F32),
                   jax.ShapeDtypeStruct((8, 1024), F32)],
        compiler_params=_params(("arbitrary",)),
    )(y, x, target, w_out, ln_g, ln_b)


def _gather_w_in(w_shard):
    R = w_shard.shape[0]
    halves = (pl.ds(0, R // 2), pl.ds(R // 2, R // 2))
    any_spec = pl.BlockSpec(memory_space=pl.ANY)

    def body(in_ref, out_ref, send_sems, recv_sems, local_sem):
        x, y, c = _position()

        def slot(p, half=None):
            s = out_ref.at[_index(*p)]
            return s if half is None else s.at[halves[half]]

        def same_core(p):
            return (p[0], p[1], c)

        def other_core(p):
            return (p[0], p[1], 1 - c)

        me, xn, yn, dg = (x, y), (1 - x, y), (x, 1 - y), (1 - x, 1 - y)

        def copy(k, dst, to, src=None):
            return _remote(dst if src is None else src, dst, send_sems.at[k], recv_sems.at[k], to)

        local = pltpu.make_async_copy(in_ref, slot(same_core(me)), local_sem)
        local.start()
        own = [copy(0, slot(same_core(me)), other_core(me), in_ref), copy(1, slot(same_core(me)), same_core(xn), in_ref),
               copy(2, slot(same_core(me)), same_core(yn), in_ref)]
        for cp in own:
            cp.start()
        copy(1, slot(same_core(xn)), same_core(xn)).wait_recv()
        passed = [copy(4, slot(same_core(xn), 1), same_core(yn)), copy(5, slot(same_core(xn)), other_core(me))]
        for cp in passed:
            cp.start()
        copy(2, slot(same_core(yn)), same_core(yn)).wait_recv()
        more = [copy(3, slot(same_core(yn), 0), same_core(xn)), copy(6, slot(same_core(yn)), other_core(me))]
        for cp in more:
            cp.start()
        passed += more
        for k, half in ((3, 0), (4, 1)):
            copy(k, slot(same_core(dg), half), same_core(xn)).wait_recv()
            fwd = copy(7 + half, slot(same_core(dg), half), other_core(me))
            fwd.start()
            passed.append(fwd)
        copy(0, slot(other_core(me)), other_core(me)).wait_recv()
        copy(5, slot(other_core(xn)), other_core(me)).wait_recv()
        copy(6, slot(other_core(yn)), other_core(me)).wait_recv()
        for half in (0, 1):
            copy(7 + half, slot(other_core(dg), half), other_core(me)).wait_recv()
        for cp in own + passed:
            cp.wait_send()
        local.wait()

    return pl.pallas_call(
        body, name="gather_w_in", in_specs=[any_spec], out_specs=any_spec,
        out_shape=jax.ShapeDtypeStruct((N_DEV,) + w_shard.shape, w_shard.dtype),
        scratch_shapes=[pltpu.SemaphoreType.DMA((9,)), pltpu.SemaphoreType.DMA((9,)), pltpu.SemaphoreType.DMA],
    )(w_shard)


def _input_gradient(d_ssd, d_att, w_ssd, w_att, dr, *, tm, comm=None):
    L = dr.shape[0]

    def body(ds_ref, da_ref, ws_ref, wa_ref, dr_ref, o_ref):
        o_ref[...] = ALPHA * dr_ref[...] + _mm_nt(ds_ref[...], ws_ref[...]) + _mm_nt(da_ref[...], wa_ref[...])

    row = lambda w: pl.BlockSpec((tm, w), lambda i: (i, 0))
    const = lambda shape: pl.BlockSpec(shape, lambda i: (0, 0))
    return _call(body, comm, name="dx", grid=(L // tm,),
                 in_specs=[row(S_W), row(A_W), const((D_MODEL, S_W)), const((D_MODEL, A_W)), row(D_MODEL)],
                 out_specs=[row(D_MODEL)], out_shape=[jax.ShapeDtypeStruct((L, D_MODEL), F32)],
                 scratch_shapes=[], args=(d_ssd, d_att, w_ssd, w_att, dr))


SHARD_COLS = D_IN_PROJ // N_DEV
SPLIT = N_SSD_REAL - 4 * SHARD_COLS
RELAYOUT_ROWS = 256


def _unpack_w_in(w_all):
    def body(g_ref, ws_ref, wa_ref):
        for j in range(4):
            ws_ref[:, SHARD_COLS * j:SHARD_COLS * (j + 1)] = g_ref[j]
        ws_ref[:, 4 * SHARD_COLS:N_SSD_REAL] = g_ref[4, :, 0:SPLIT]
        ws_ref[:, N_SSD_REAL:S_W] = jnp.zeros((RELAYOUT_ROWS, S_W - N_SSD_REAL), ws_ref.dtype)
        wa_ref[:, 0:SHARD_COLS - SPLIT] = g_ref[4, :, SPLIT:SHARD_COLS]
        for j in range(5, N_DEV):
            lo = SHARD_COLS * (j - 4) - SPLIT
            wa_ref[:, lo:lo + SHARD_COLS] = g_ref[j]

    return pl.pallas_call(
        body, name="unpack_w_in", grid=(D_MODEL // RELAYOUT_ROWS,),
        in_specs=[pl.BlockSpec((N_DEV, RELAYOUT_ROWS, SHARD_COLS), lambda i: (0, i, 0))],
        out_specs=[pl.BlockSpec((RELAYOUT_ROWS, S_W), lambda i: (i, 0)), pl.BlockSpec((RELAYOUT_ROWS, A_W), lambda i: (i, 0))],
        out_shape=[jax.ShapeDtypeStruct((D_MODEL, S_W), w_all.dtype), jax.ShapeDtypeStruct((D_MODEL, A_W), w_all.dtype)],
        compiler_params=_params(("arbitrary",)),
    )(w_all)


def _dw_in(me1, xb, d, half, tail=None, *, tl=1024):
    L, N = d.shape
    steps = L // tl

    def body(me_ref, x_ref, d_ref, *refs):
        if half == 0:
            p_ref, own_ref, tail_ref, acc = refs
        else:
            t_ref, p_ref, acc = refs
        l = pl.program_id(0)

        @pl.when(l == 0)
        def _():
            acc[...] = jnp.zeros_like(acc)

        acc[...] += _mm_tn(x_ref[...], d_ref[...])

        @pl.when(l == steps - 1)
        def _():
            if half == 0:
                me = me_ref[0]

                @pl.when(me >= 4)
                def _():
                    own_ref[...] = jnp.zeros_like(own_ref)

                tail_ref[...] = acc[:, S_DT:S_W]
            for j in range(4):
                if half == 0:
                    pieces = [(0, acc[:, SHARD_COLS * j:SHARD_COLS * (j + 1)])]
                elif j == 0:
                    pieces = [(0, t_ref[:, 4 * SHARD_COLS - S_DT:N_SSD_REAL - S_DT]), (SPLIT, acc[:, 0:SHARD_COLS - SPLIT])]
                else:
                    lo = SHARD_COLS * j - SPLIT
                    pieces = [(0, acc[:, lo:lo + SHARD_COLS])]
                for off, blk in pieces:
                    p_ref[j, :, off:off + blk.shape[1]] = blk.astype(p_ref.dtype)
                    if half == 0:
                        @pl.when(me == j)
                        def _(off=off, blk=blk):
                            own_ref[:, off:off + blk.shape[1]] = blk

    once = pl.Buffered(1)
    whole = lambda shape: pl.BlockSpec(shape, lambda l: (0,) * len(shape), pipeline_mode=once)
    in_specs = [pl.BlockSpec(memory_space=pltpu.SMEM), pl.BlockSpec((tl, D_MODEL), lambda l: (l, 0)),
                pl.BlockSpec((tl, N), lambda l: (l, 0))]
    args = [me1, xb, d]
    if half == 0:
        out_shape = [jax.ShapeDtypeStruct((4, D_MODEL, SHARD_COLS), BF16), jax.ShapeDtypeStruct((D_MODEL, SHARD_COLS), F32),
                     jax.ShapeDtypeStruct((D_MODEL, S_W - S_DT), F32)]
    else:
        in_specs.append(whole(tail.shape))
        args.append(tail)
        out_shape = [jax.ShapeDtypeStruct((4, D_MODEL, SHARD_COLS), F32)]
    return pl.pallas_call(
        body, name="dw_in_%d" % half, grid=(steps,), in_specs=in_specs,
        out_specs=[whole(o.shape) for o in out_shape], out_shape=out_shape,
        scratch_shapes=[pltpu.VMEM((D_MODEL, N), F32)], compiler_params=_params(("arbitrary",)),
    )(*args)


def _pair_swap(stack):
    def body(in_ref, out_ref, send_sems, recv_sems):
        x, y, c = _position()
        cps = [_remote(in_ref.at[2 * oy + (1 - c)], out_ref.at[oy], send_sems.at[oy], recv_sems.at[oy], (x, y, 1 - c))
               for oy in range(2)]
        for cp in cps:
            cp.start()
        for cp in cps:
            cp.wait_recv()
        for cp in cps:
            cp.wait_send()

    any_spec = pl.BlockSpec(memory_space=pl.ANY)
    return pl.pallas_call(
        body, name="pair_swap", in_specs=[any_spec], out_specs=any_spec,
        out_shape=jax.ShapeDtypeStruct((2,) + stack.shape[1:], stack.dtype),
        scratch_shapes=[pltpu.SemaphoreType.DMA((2,)), pltpu.SemaphoreType.DMA((2,))],
    )(stack)


def _pair_sum(pos3, stack, swapped, own_lo):
    def body(pos_ref, a_ref, b_ref, lo_ref, chip_ref, own_ref):
        oy = pl.program_id(1)
        t = a_ref[0] + b_ref[0]
        chip_ref[0] = t.astype(chip_ref.dtype)

        @pl.when((pos_ref[0] == 0) & (oy == 0))
        def _():
            own_ref[...] = lo_ref[...]

        @pl.when((pos_ref[0] == 1) & (oy == pos_ref[1]))
        def _():
            own_ref[...] = t

    blk = (1, RELAYOUT_ROWS, SHARD_COLS)
    flat = pl.BlockSpec((RELAYOUT_ROWS, SHARD_COLS), lambda i, oy, pos: (i, 0))
    return pl.pallas_call(
        body, name="pair_sum",
        grid_spec=pltpu.PrefetchScalarGridSpec(
            num_scalar_prefetch=1, grid=(D_MODEL // RELAYOUT_ROWS, 2),
            in_specs=[pl.BlockSpec(blk, lambda i, oy, pos: (2 * oy + pos[2], i, 0)),
                      pl.BlockSpec(blk, lambda i, oy, pos: (oy, i, 0)), flat],
            out_specs=[pl.BlockSpec(blk, lambda i, oy, pos: (oy, i, 0)), flat]),
        out_shape=[jax.ShapeDtypeStruct((2, D_MODEL, SHARD_COLS), BF16), jax.ShapeDtypeStruct((D_MODEL, SHARD_COLS), F32)],
        compiler_params=_params(("arbitrary", "arbitrary")),
    )(pos3, stack, swapped, own_lo)


def _adamw_math(w, g, m, v):
    m = ADAM_B1 * m + (1.0 - ADAM_B1) * g
    v = ADAM_B2 * v + (1.0 - ADAM_B2) * (g * g)
    m_hat = m / (1.0 - ADAM_B1 ** ADAM_STEP)
    v_hat = v / (1.0 - ADAM_B2 ** ADAM_STEP)
    delta = -ADAM_LR * (m_hat / (jnp.sqrt(v_hat) + ADAM_EPS) + ADAM_WD * w)
    return delta, m, v


def _adamw_shard(n_recv, g_own, recv, w, m, v, *, rows, name):
    R, C = g_own.shape

    def body(n_ref, g_ref, r_ref, w_ref, m_ref, v_ref, go_ref, d_ref, mo_ref, vo_ref):
        g = g_ref[...]
        for k in range(N_DEV - 1):
            g = g + jnp.where(k < n_ref[0], r_ref[k].astype(F32), 0.0)
        d, mn, vn = _adamw_math(w_ref[...], g, m_ref[...], v_ref[...])
        go_ref[...] = g
        d_ref[...] = d
        mo_ref[...] = mn
        vo_ref[...] = vn

    blk = pl.BlockSpec((rows, C), lambda i: (i, 0))
    return pl.pallas_call(
        body, name=name, grid=(R // rows,),
        in_specs=[pl.BlockSpec(memory_space=pltpu.SMEM), blk,
                  pl.BlockSpec((N_DEV - 1, rows, C), lambda i: (0, i, 0)), blk, blk, blk],
        out_specs=[blk] * 4, out_shape=[jax.ShapeDtypeStruct((R, C), F32)] * 4,
        compiler_params=_params(("arbitrary",)),
    )(n_recv, g_own, recv, w, m, v)


def _minor_rows_view(a):
    return jnp.transpose(a, (2, 0, 1)).reshape(SHARD_COLS * 8, 128)


def _from_minor_rows_view(v):
    return jnp.transpose(v.reshape(SHARD_COLS, 8, 128), (1, 2, 0)).reshape(1, D_MODEL, SHARD_COLS)


def _adamw_w_in(n_recv, g_own, recv, w, m, v):
    C = SHARD_COLS
    pad = -C % 128

    def body(n_ref, g_ref, r_ref, w_ref, m_ref, v_ref, go_ref, d_ref, mo_ref, vo_ref):
        for q in range(D_MODEL // 128):
            band = pl.ds(q * 128, 128)
            g = g_ref[band, :]
            for k in range(N_DEV - 1):
                g = g + jnp.where(k < n_ref[0], r_ref[k, band, :].astype(F32), 0.0)
            g = jnp.pad(g, ((0, 0), (0, pad))).T[0:C]
            rows = pl.ds(q, C, stride=8)
            d, mn, vn = _adamw_math(w_ref[rows, :], g, m_ref[rows, :], v_ref[rows, :])
            go_ref[rows, :] = g
            d_ref[rows, :] = d
            mo_ref[rows, :] = mn
            vo_ref[rows, :] = vn

    return pl.pallas_call(
        body, name="adamw_w_in", out_shape=[jax.ShapeDtypeStruct(w.shape, F32)] * 4,
        in_specs=[pl.BlockSpec(memory_space=pltpu.SMEM)] + [pl.BlockSpec(memory_space=pltpu.VMEM)] * 5,
        out_specs=[pl.BlockSpec(memory_space=pltpu.VMEM)] * 4,
        compiler_params=_params(),
    )(n_recv, g_own, recv, w, m, v)


SMALL = ("conv_b", "dt_bias", "a_log", "d_skip", "ssd_norm_w", "attn_sinks", "ln_g", "ln_b")


def _adamw_small(gathered, params):
    n_p = len(SMALL)

    def body(*refs):
        acc = []
        for r in refs[:5]:
            t = r[0]
            for k in range(1, N_DEV):
                t = t + r[k]
            acc.append(t)
        head, conv, norm, scal, sink = acc
        grads = dict(conv_b=conv[4:5, :], dt_bias=scal[0:1, 0:N_HEADS], a_log=scal[1:2, 0:N_HEADS],
                     d_skip=scal[2:3, 0:N_HEADS], ssd_norm_w=norm[0:1, :], attn_sinks=sink[0:1, 0:N_HEADS],
                     ln_g=head[0:1, :], ln_b=head[1:2, :])
        wmv = refs[5:5 + 3 * n_p]
        outs = refs[5 + 3 * n_p:]
        outs[0][...] = head[3:4, 0:1]
        outs[1][...] = conv[0:4, :]
        for i, name in enumerate(SMALL):
            w_ref, m_ref, v_ref = wmv[3 * i:3 * i + 3]
            g = grads[name]
            d, mn, vn = _adamw_math(w_ref[...], g, m_ref[...], v_ref[...])
            for o_ref, val in zip(outs[2 + 4 * i:6 + 4 * i], (g, d, mn, vn)):
                o_ref[...] = val

    flat = [a for name in SMALL for a in params[name]]
    out_shape = [jax.ShapeDtypeStruct((1, 1), F32), jax.ShapeDtypeStruct((4, D_XBC), F32)]
    for name in SMALL:
        out_shape += [jax.ShapeDtypeStruct(params[name][0].shape, F32)] * 4
    res = pl.pallas_call(body, name="adamw_small", out_shape=out_shape, compiler_params=_params())(*gathered, *flat)
    return res[0], res[1], {name: res[2 + 4 * i:6 + 4 * i] for i, name in enumerate(SMALL)}


def _adamw_plain(g, w, m, v):
    def body(g_ref, w_ref, m_ref, v_ref, d_ref, mo_ref, vo_ref):
        d, mn, vn = _adamw_math(w_ref[...], g_ref[...], m_ref[...], v_ref[...])
        d_ref[...] = d
        mo_ref[...] = mn
        vo_ref[...] = vn

    return pl.pallas_call(
        body, name="adamw_conv_w", out_shape=[jax.ShapeDtypeStruct(w.shape, F32)] * 3,
        compiler_params=_params(),
    )(g, w, m, v)


def _lane_pattern(fn):
    return np.asarray([fn(l % HEAD_DIM) for l in range(128)], np.float32)


ROPE_INV = _lane_pattern(lambda r: ROPE_THETA ** (-2.0 * (r % 8) / ROPE_DIM) if r < ROPE_DIM else 0.0)
ROPE_SIN_A = _lane_pattern(lambda r: 1.0 if 8 <= r < ROPE_DIM else 0.0)
ROPE_SIN_B = _lane_pattern(lambda r: -1.0 if r < 8 else 0.0)


def _rope_tables(positions):
    ang = positions.astype(F32)[:, None] * ROPE_INV[None, :]
    sn = jnp.sin(ang)
    return jnp.concatenate([jnp.cos(ang), sn * ROPE_SIN_A[None, :], sn * ROPE_SIN_B[None, :]], axis=1)


def _expansion():
    E = np.arange(1024)[None, :] // HEAD_DIM == np.arange(128)[:, None]
    return jnp.asarray(E, BF16), jnp.asarray(E.T, BF16)


def _ssd_args(conv_w, conv_b, dt_bias, a_log, d_skip, norm_w, E):
    return (conv_w, conv_b, dt_bias.reshape(-1), a_log.reshape(-1), d_skip.reshape(-1), norm_w, E)


def kernel(x, positions, w_in, conv_w, conv_b, dt_bias, a_log, d_skip, ssd_norm_w, attn_sinks, w_out, ln_g, ln_b, loss_target, m_w_in, m_conv_w, m_conv_b, m_dt_bias, m_a_log, m_d_skip, m_ssd_norm_w, m_attn_sinks, m_w_out, m_ln_g, m_ln_b, v_w_in, v_conv_w, v_conv_b, v_dt_bias, v_a_log, v_d_skip, v_ssd_norm_w, v_attn_sinks, v_w_out, v_ln_g, v_ln_b):
    me = _index(*_position())
    me1 = me.reshape(1).astype(jnp.int32)
    x0, target = x[0], loss_target[0]
    bf16_shard = lambda shape: jax.ShapeDtypeStruct(shape, BF16)
    E, ET = _expansion()
    tabs = _rope_tables(positions[0])
    sinks = attn_sinks.reshape(-1)

    w_ssd, w_att = _unpack_w_in(_gather_w_in(w_in[0].astype(BF16)))
    gather_conv_w = _Hosted([conv_w[0]], [jax.ShapeDtypeStruct((N_DEV,) + conv_w.shape[1:], F32)],
                            [_Flow("gather", 0, 0)])

    proj_ssd, proj_att, xb, conv_w_all = _in_proj(x0, w_ssd, w_att, tm=512, comm=gather_conv_w)
    conv_w_f = jnp.transpose(conv_w_all, (1, 0, 2)).reshape(4, D_XBC)
    ssd_args = _ssd_args(conv_w_f, conv_b, dt_bias, a_log, d_skip, ssd_norm_w, E)
    gather_w_out = _Hosted([w_out[0].astype(BF16)], [bf16_shard((N_DEV, 256, D_MODEL))], [_Flow("gather", 0, 0)])
    y, ypre, hprev, pre, w_out_all = _mixer_forward(proj_ssd, proj_att, tabs, sinks, *ssd_args, comm=gather_w_out)
    w_out_f = w_out_all.reshape(2 * D_MODEL, D_MODEL)
    dr, dy, acc_head = _head(y, x0, target, w_out_f, ln_g, ln_b, tm=512)

    dw_out, dw_out_bf16 = _matmul_tn(y, dr, tl=1024, tn=D_MODEL, name="dw_out", emit_bf16=True)
    own_out = lax.dynamic_index_in_dim(dw_out.reshape(N_DEV, 256, D_MODEL), me, axis=0, keepdims=False)
    send_out = _Hosted([dw_out_bf16.reshape(N_DEV, 256, D_MODEL)], [bf16_shard((N_DEV - 1, 256, D_MODEL))],
                       [_Flow("exchange", 0, 0)])
    d_ssd, acc_cw, acc_w, acc_s, recv_out = _ssd_backward(proj_ssd, hprev, ypre, pre, dy, *ssd_args, ET, comm=send_out)
    parts_lo, own_lo, dw_dt_block = _dw_in(me1, xb, d_ssd, 0)
    recv_shape = bf16_shard((N_DEV - 1, D_MODEL, SHARD_COLS))
    send_lo = _Hosted([parts_lo], [recv_shape], [_Flow("exchange", 0, 0, target_x=0, target_c=0)])
    d_att, dsink, recv_in = _swa_backward(proj_att, tabs, sinks, dy, comm=send_lo)
    (stack_hi,) = _dw_in(me1, xb, d_att, 1, dw_dt_block)
    pos3 = jnp.stack(_position()).astype(jnp.int32)
    chip_hi, own_in = _pair_sum(pos3, stack_hi, _pair_swap(stack_hi), own_lo)
    accs = [acc_head, acc_cw, acc_w, acc_s, dsink]
    send_hi = _Hosted([chip_hi, recv_in, parts_lo] + accs,
                      [recv_shape] + [jax.ShapeDtypeStruct((N_DEV,) + a.shape, F32) for a in accs],
                      [_Flow("chip_exchange", 0, 0, target_x=1), _Flow("exchange", 2, 0, target_x=0, target_c=1)]
                      + [_Flow("gather", 3 + i, 1 + i) for i in range(5)], aliases={1: 0})
    dx, recv_in, *gathered = _input_gradient(d_ssd, d_att, w_ssd, w_att, dr, tm=256, comm=send_hi)
    n_recv_in = jnp.where(me < 4, N_DEV - 1, 3).reshape(1).astype(jnp.int32)
    n_recv_out = jnp.full((1,), N_DEV - 1, jnp.int32)

    g_in, d_in, nm_in, nv_in = [_from_minor_rows_view(r) for r in _adamw_w_in(
        n_recv_in, own_in, recv_in, _minor_rows_view(w_in), _minor_rows_view(m_w_in), _minor_rows_view(v_w_in))]
    g_out, d_out, nm_out, nv_out = _adamw_shard(n_recv_out, own_out, recv_out, w_out[0], m_w_out[0], v_w_out[0],
                                                rows=256, name="adamw_w_out")
    loss, g_conv_w, small = _adamw_small(gathered, dict(
        conv_b=(conv_b, m_conv_b, v_conv_b), dt_bias=(dt_bias, m_dt_bias, v_dt_bias), a_log=(a_log, m_a_log, v_a_log),
        d_skip=(d_skip, m_d_skip, v_d_skip), ssd_norm_w=(ssd_norm_w, m_ssd_norm_w, v_ssd_norm_w),
        attn_sinks=(attn_sinks, m_attn_sinks, v_attn_sinks), ln_g=(ln_g, m_ln_g, v_ln_g), ln_b=(ln_b, m_ln_b, v_ln_b)))
    g_cw = lax.dynamic_slice_in_dim(g_conv_w, me * (D_XBC // N_DEV), D_XBC // N_DEV, axis=1)
    d_cw, nm_cw, nv_cw = _adamw_plain(g_cw, conv_w[0], m_conv_w[0], v_conv_w[0])

    def leaves(i, big_in, cw, big_out):
        mid = [small[k][i] for k in ("conv_b", "dt_bias", "a_log", "d_skip", "ssd_norm_w", "attn_sinks")]
        return [big_in, cw[None]] + mid + [big_out[None], small["ln_g"][i], small["ln_b"][i]]

    return (loss.reshape(()), dx[None], *leaves(0, g_in, g_cw, g_out), *leaves(1, d_in, d_cw, d_out),
            *leaves(2, nm_in, nm_cw, nm_out), *leaves(3, nv_in, nv_cw, nv_out))
```

```python
import jax
import jax.numpy as jnp
from jax import lax
from jax.experimental import pallas as pl
from jax.experimental.pallas import tpu as pltpu
import numpy as np

F32 = jnp.float32
BF16 = jnp.bfloat16
_MXU = jnp.bfloat16

N_DEV = 8
D_MODEL = 1024
D_SSD = 1024
D_ATT = 1024
HEAD_DIM = 64
N_HEADS = 16
SSD_GROUPS = 2
KV_HEADS = 4
CHUNK = 128
D_XBC = 1536
D_IN_PROJ = 5136
ROPE_DIM = 16
ROPE_THETA = 500000.0
ALPHA = (2.0 * 1) ** 0.25
LN_EPS = 1e-5
RMS_EPS = 1e-5
ATT_SCALE = HEAD_DIM ** -0.5
NEG = -1e30

S_Z, S_XS, S_B, S_C, S_DT, S_W = 0, 1024, 2048, 2304, 2560, 2816
N_SSD_REAL = 2576
A_Q, A_K, A_V, A_G, A_W = 0, 1024, 1280, 1536, 2560

ADAM_LR = 0.001
ADAM_B1 = 0.9
ADAM_B2 = 0.999
ADAM_EPS = 1e-08
ADAM_WD = 0.01
ADAM_STEP = 10

VMEM_LIMIT = 48 * 1024 * 1024
MESH = pl.DeviceIdType.MESH


def _params(sem=None):
    return pltpu.CompilerParams(dimension_semantics=sem, vmem_limit_bytes=VMEM_LIMIT)


def _mm(a, b):
    return jnp.dot(a.astype(_MXU), b.astype(_MXU), preferred_element_type=F32)


def _mm_nt(a, b):
    return lax.dot_general(a.astype(_MXU), b.astype(_MXU), (((1,), (1,)), ((), ())),
                           preferred_element_type=F32)


def _mm_tn(a, b):
    return lax.dot_general(a.astype(_MXU), b.astype(_MXU), (((0,), (0,)), ((), ())),
                           preferred_element_type=F32)


def _split3(v):
    hi = v.astype(BF16)
    r = v - hi.astype(F32)
    mid = r.astype(BF16)
    lo = (r - mid.astype(F32)).astype(BF16)
    return hi, mid, lo


def _mm_exact_r(v, p01):
    hi, mid, lo = _split3(v)
    d = lambda a: jnp.dot(a, p01, preferred_element_type=F32)
    return d(hi) + d(mid) + d(lo)


def _mm_exact_l(p01, v):
    hi, mid, lo = _split3(v)
    d = lambda a: jnp.dot(p01, a, preferred_element_type=F32)
    return d(hi) + d(mid) + d(lo)


def _mm_2pass_r(v, p01):
    hi = v.astype(BF16)
    lo = (v - hi.astype(F32)).astype(BF16)
    return jnp.dot(hi, p01, preferred_element_type=F32) + jnp.dot(lo, p01, preferred_element_type=F32)


def _sigmoid(x):
    return 1.0 / (1.0 + jnp.exp(-x))


def _softplus(x):
    e = jnp.exp(-jnp.abs(x))
    u = 1.0 + e
    log1p = jnp.where(u == 1.0, e, jnp.log(u) * (e / (u - 1.0)))
    return jnp.maximum(x, 0.0) + log1p


def _rows8(rows):
    n = rows[0].shape[1]
    rid = lax.broadcasted_iota(jnp.int32, (8, n), 0)
    out = jnp.zeros((8, n), F32)
    for k, r in enumerate(rows):
        out = out + jnp.where(rid == k, r, 0.0)
    return out


def _colsum(a):
    return jnp.sum(a, axis=0, keepdims=True)


def _in_proj(x, w_ssd, w_att, *, tm, comm=None):
    L, K = x.shape

    def body(x_ref, ws_ref, wa_ref, ps_ref, pa_ref, xb_ref):
        xb = x_ref[...].astype(_MXU)
        xb_ref[...] = xb
        ps_ref[...] = jnp.dot(xb, ws_ref[...], preferred_element_type=F32)
        pa_ref[...] = jnp.dot(xb, wa_ref[...], preferred_element_type=F32)

    row = lambda w: pl.BlockSpec((tm, w), lambda i: (i, 0))
    resident = lambda a: pl.BlockSpec(a.shape, lambda i: (0, 0), pipeline_mode=pl.Buffered(1))
    return _call(
        body, comm, name="in_proj", grid=(L // tm,),
        in_specs=[row(K), resident(w_ssd), resident(w_att)], out_specs=[row(S_W), row(A_W), row(K)],
        out_shape=[jax.ShapeDtypeStruct((L, S_W), F32), jax.ShapeDtypeStruct((L, A_W), F32),
                   jax.ShapeDtypeStruct((L, K), _MXU)],
        scratch_shapes=[], args=(x, w_ssd, w_att))


def _matmul_tn(a, g, *, tl, tn, name, emit_bf16=False):
    L, M = a.shape
    N = g.shape[1]
    last = L // tl - 1

    def body(a_ref, g_ref, o_ref, *rest):
        @pl.when(pl.program_id(1) == 0)
        def _():
            o_ref[...] = jnp.zeros_like(o_ref)

        o_ref[...] += _mm_tn(a_ref[...], g_ref[...])
        if emit_bf16:
            @pl.when(pl.program_id(1) == last)
            def _():
                rest[0][...] = o_ref[...].astype(BF16)

    spec = pl.BlockSpec((M, tn), lambda j, l: (0, j))
    res = pl.pallas_call(
        body, name=name, grid=(N // tn, L // tl),
        in_specs=[pl.BlockSpec((tl, M), lambda j, l: (l, 0)), pl.BlockSpec((tl, tn), lambda j, l: (l, j))],
        out_specs=[spec, spec] if emit_bf16 else [spec],
        out_shape=[jax.ShapeDtypeStruct((M, N), F32)] + ([jax.ShapeDtypeStruct((M, N), BF16)] if emit_bf16 else []),
        compiler_params=_params(("arbitrary", "arbitrary")),
    )(a, g)
    return res if emit_bf16 else res[0]


def _position():
    return lax.axis_index("x"), lax.axis_index("y"), lax.axis_index("c")


def _index(px, py, pc):
    return 4 * px + 2 * py + pc


def _flip(pos, k):
    x, y, c = pos
    return ((1 - x) if (k >> 2) & 1 else x, (1 - y) if (k >> 1) & 1 else y, (1 - c) if k & 1 else c)


def _when(cond, fn):
    if cond is True:
        fn()
    else:
        pl.when(cond)(fn)


def _remote(src, dst, send_sem, recv_sem, peer):
    return pltpu.make_async_remote_copy(src_ref=src, dst_ref=dst, send_sem=send_sem, recv_sem=recv_sem,
                                        device_id=peer, device_id_type=MESH)


class _Flow:
    def __init__(self, kind, operand, result, target_x=None, target_c=None):
        self.kind, self.operand, self.result, self.target_x, self.target_c = kind, operand, result, target_x, target_c

    def owns(self, pos):
        if self.target_x is None:
            return True
        cond = pos[0] == self.target_x
        return cond if self.target_c is None else cond & (pos[2] == self.target_c)


class _Hosted:
    def __init__(self, operands, out_shapes, flows, aliases=None):
        self.operands, self.out_shapes, self.flows = operands, out_shapes, flows
        self.aliases = aliases or {}

    def plan(self, ins, outs, send_sems, recv_sems, local_sems):
        me = _position()
        mi = _index(*me)
        sends, recvs, locals_ = [], [], []
        for row, f in enumerate(self.flows):
            src, dst = ins[f.operand], outs[f.result]
            for k in range(1, N_DEV):
                peer = _flip(me, k)
                sems = (send_sems.at[row, k - 1], recv_sems.at[row, k - 1])
                if f.kind == "exchange":
                    owner = _index(*peer) if f.target_x is None else 2 * peer[1] + peer[2]
                    cp = _remote(src.at[owner], dst.at[k - 1], *sems, peer)
                    sends.append((f.owns(peer), cp))
                    recvs.append((f.owns(me), cp))
                elif f.kind == "chip_exchange":
                    if k & 1:
                        continue
                    cp = _remote(src.at[peer[1]], dst.at[k // 2 - 1], *sems, peer)
                    sends.append((peer[0] == f.target_x, cp))
                    recvs.append((me[0] == f.target_x, cp))
                else:
                    sends.append((True, _remote(src, dst.at[mi], *sems, peer)))
                    recvs.append((True, _remote(src, dst.at[_index(*peer)], *sems, peer)))
            if f.kind == "gather":
                locals_.append(pltpu.make_async_copy(src, dst.at[mi], local_sems.at[row]))

        def start():
            for cp in locals_:
                cp.start()
            for cond, cp in sends:
                _when(cond, cp.start)

        def wait():
            for cond, cp in recvs:
                _when(cond, cp.wait_recv)
            for cond, cp in sends:
                _when(cond, cp.wait_send)
            for cp in locals_:
                cp.wait()

        return start, wait


def _call(body, comm, *, name, grid, in_specs, out_specs, out_shape, scratch_shapes, args, aliases=None):
    io_alias = dict(aliases or {})
    semantics = ("arbitrary",) * len(grid)
    if comm is None:
        return pl.pallas_call(body, name=name, grid=grid, in_specs=in_specs, out_specs=out_specs, out_shape=out_shape,
                              scratch_shapes=scratch_shapes, input_output_aliases=io_alias,
                              compiler_params=_params(semantics))(*args)
    n_in, n_out, n_scr = len(args), len(out_shape), len(scratch_shapes)
    c_in, c_out, rows = len(comm.operands), len(comm.out_shapes), len(comm.flows)

    def hosted(*refs):
        ins, refs = refs[:n_in], refs[n_in:]
        cins, refs = refs[:c_in], refs[c_in:]
        outs, refs = refs[:n_out], refs[n_out:]
        couts, refs = refs[:c_out], refs[c_out:]
        scr, (send_sems, recv_sems, local_sems) = refs[:n_scr], refs[n_scr:]
        start, wait = comm.plan(cins, couts, send_sems, recv_sems, local_sems)
        ids = [pl.program_id(d) for d in range(len(grid))]
        first, last = ids[0] == 0, ids[0] == grid[0] - 1
        for d in range(1, len(grid)):
            first, last = first & (ids[d] == 0), last & (ids[d] == grid[d] - 1)
        pl.when(first)(start)
        body(*ins, *outs, *scr)
        pl.when(last)(wait)

    for ci, co in comm.aliases.items():
        io_alias[n_in + ci] = n_out + co
    any_spec = pl.BlockSpec(memory_space=pl.ANY)
    sems = [pltpu.SemaphoreType.DMA((rows, N_DEV - 1)), pltpu.SemaphoreType.DMA((rows, N_DEV - 1)),
            pltpu.SemaphoreType.DMA((rows,))]
    return pl.pallas_call(
        hosted, name=name, grid=grid, in_specs=list(in_specs) + [any_spec] * c_in,
        out_specs=list(out_specs) + [any_spec] * c_out, out_shape=list(out_shape) + list(comm.out_shapes),
        scratch_shapes=list(scratch_shapes) + sems, input_output_aliases=io_alias,
        compiler_params=_params(semantics))(*args, *comm.operands)


def _head_row(ref, width, rep):
    hid = lax.broadcasted_iota(jnp.int32, (1, width), 1) // rep
    row = jnp.zeros((1, width), F32)
    for h in range(N_HEADS):
        row = jnp.where(hid == h, ref[h], row)
    return row


def _rows_from_above(u_b, s, ext_scr, row, col):
    down = (row - col == s).astype(_MXU)
    return jnp.concatenate([ext_scr[8 - s:16 - s, :], jnp.dot(down, u_b, preferred_element_type=F32)[8:128]], axis=0)


def _ssd_recompute(first, p_ref, halo_ref, cw_ref, cb_ref, dtb_ref, alog_ref, e_ref, ext_scr, pre=None):
    row = lax.broadcasted_iota(jnp.int32, (128, 128), 0)
    col = lax.broadcasted_iota(jnp.int32, (128, 128), 1)
    ext_scr[0:8, :] = jnp.where(first, 0.0, halo_ref[:, S_XS:S_DT])
    if pre is not None:
        ext_scr[8:16, :] = p_ref[0:8, S_XS:S_DT]
    else:
        ext_scr[8:136, :] = p_ref[:, S_XS:S_DT]
        cw = cw_ref[...]
        pre = (cb_ref[0:1, :] + cw[3:4, :] * ext_scr[8:136, :] + cw[2:3, :] * ext_scr[7:135, :]
               + cw[1:2, :] * ext_scr[6:134, :] + cw[0:1, :] * ext_scr[5:133, :])
    sg = _sigmoid(pre)
    act = pre * sg
    lane = lax.broadcasted_iota(jnp.int32, (1, 128), 1)
    A = jnp.where(lane < N_HEADS, -jnp.exp(_head_row(alog_ref, 128, 1)), 0.0)
    raw = p_ref[:, S_DT:S_DT + 128] + _head_row(dtb_ref, 128, 1)
    dt = _softplus(raw)
    dA = dt * A
    tril = (row >= col).astype(BF16)
    acs = _mm_exact_l(tril, dA)
    last = acs[127:128, :]
    ds = jnp.exp(last - acs)
    eo = jnp.exp(acs)
    E = e_ref[...]
    ex = _mm_2pass_r(jnp.concatenate([dt, ds, eo], axis=0), E)
    dt_e, ds_e, eo_e = ex[0:128], ex[128:256], ex[256:384]
    xs_c = act[:, 0:1024]
    X = xs_c * dt_e
    return dict(pre=pre, sg=sg, xs_c=xs_c, Bc=act[:, 1024:1280], Cc=act[:, 1280:1536], A=A, raw=raw, dt=dt,
                acs=acs, acsT=acs.T, eo_e=eo_e, ds_e=ds_e, dt_e=dt_e, cd_e=eo_e[127:128, :],
                X=X, Xd=X * ds_e, row=row, col=col)


def _split_halves(t):
    lo = _lo_half(CHUNK)
    return jnp.concatenate([jnp.where(lo, t, 0.0), jnp.where(lo, 0.0, t)], axis=0)


def _ssd_core(R, hprev):
    causal = R["row"] >= R["col"]
    acs, acsT, X = R["acs"], R["acsT"], R["X"]
    ydiag, yoff, snew = [], [], []
    for g in range(SSD_GROUPS):
        Bg = R["Bc"][:, g * 128:(g + 1) * 128]
        Cg = R["Cc"][:, g * 128:(g + 1) * 128]
        cols = slice(g * 512, (g + 1) * 512)
        CB = _mm_nt(Cg, Bg)
        snew.append(_mm_tn(Bg, R["Xd"][:, cols]))
        yoff.append(_mm(Cg, hprev[:, cols]))
        for j in range(4):
            h0 = g * 8 + 2 * j
            ms = [CB * jnp.exp(jnp.where(causal, acs[:, h:h + 1] - acsT[h:h + 1, :], NEG)) for h in (h0, h0 + 1)]
            ydiag.append(_mm(jnp.concatenate(ms, axis=1), _split_halves(X[:, h0 * HEAD_DIM:h0 * HEAD_DIM + 128])))
    Y = jnp.concatenate(ydiag, axis=1) + jnp.concatenate(yoff, axis=1) * R["eo_e"]
    return Y, jnp.concatenate(snew, axis=1)


def _ssd_forward_step(p_ref, halo_ref, cw_ref, cb_ref, dtb_ref, alog_ref, dsk_ref, nw_ref, e_ref,
                      y_ref, ypre_ref, hprev_ref, pre_ref, h_scr, ext_scr):
    c = pl.program_id(0)
    first = c == 0

    @pl.when(first)
    def _():
        h_scr[...] = jnp.zeros_like(h_scr)

    R = _ssd_recompute(first, p_ref, halo_ref, cw_ref, cb_ref, dtb_ref, alog_ref, e_ref, ext_scr)
    hprev = h_scr[...]
    hprev_ref[...] = hprev
    pre_ref[...] = R["pre"]
    Y, snew = _ssd_core(R, hprev)
    h_scr[...] = hprev * R["cd_e"] + snew
    Y = Y + _head_row(dsk_ref, D_SSD, HEAD_DIM) * R["xs_c"]
    ypre_ref[...] = Y
    z = p_ref[:, S_Z:S_Z + 1024]
    yf = Y * (z * _sigmoid(z))
    outs = []
    for g in range(SSD_GROUPS):
        yg = yf[:, g * 512:(g + 1) * 512]
        r = lax.rsqrt(jnp.mean(yg * yg, axis=-1, keepdims=True) + RMS_EPS)
        outs.append(yg * r)
    y_ref[:, 0:D_SSD] = (jnp.concatenate(outs, axis=1) * nw_ref[0:1, :]).astype(y_ref.dtype)


def _ssd_backward(proj_ssd, hprev_all, ypre, pre, dy, conv_w, conv_b, dt_bias, a_log, d_skip, norm_w, E, ET, comm=None):
    L = proj_ssd.shape[0]
    nc = L // CHUNK

    def body(p_ref, halo_ref, hprev_ref, ypre_ref, pre_ref, dy_ref, cw_ref, cb_ref, dtb_ref, alog_ref, dsk_ref, nw_ref, e_ref,
             et_ref, dp_ref, acc_cw_ref, acc_w_ref, acc_s_ref, dh_scr, ext_scr, ext2_scr, nxt_scr):
        i = pl.program_id(0)
        c = nc - 1 - i
        first = c == 0

        @pl.when(i == 0)
        def _():
            dh_scr[...] = jnp.zeros_like(dh_scr)
            nxt_scr[...] = jnp.zeros_like(nxt_scr)
            acc_cw_ref[...] = jnp.zeros_like(acc_cw_ref)
            acc_w_ref[...] = jnp.zeros_like(acc_w_ref)
            acc_s_ref[...] = jnp.zeros_like(acc_s_ref)

        R = _ssd_recompute(first, p_ref, halo_ref, cw_ref, cb_ref, dtb_ref, alog_ref, e_ref, ext_scr, pre_ref[...])
        hprev = hprev_ref[...]
        xs_c, X, Xd = R["xs_c"], R["X"], R["Xd"]
        acs, acsT = R["acs"], R["acsT"]
        ET = et_ref[...]
        dsk = _head_row(dsk_ref, D_SSD, HEAD_DIM)
        Y = ypre_ref[...]

        z = p_ref[:, S_Z:S_Z + 1024]
        sz = _sigmoid(z)
        silz = z * sz
        yf = Y * silz
        dyv = dy_ref[...]
        nw = nw_ref[0:1, :]
        dyf_parts, dnw_parts = [], []
        for g in range(SSD_GROUPS):
            cols = slice(g * 512, (g + 1) * 512)
            yg = yf[:, cols]
            r = lax.rsqrt(jnp.mean(yg * yg, axis=-1, keepdims=True) + RMS_EPS)
            yn = yg * r
            dyn = dyv[:, cols] * nw[:, cols]
            dnw_parts.append(_colsum(dyv[:, cols] * yn))
            dyf_parts.append(r * (dyn - yn * jnp.mean(dyn * yn, axis=-1, keepdims=True)))
        dyf = jnp.concatenate(dyf_parts, axis=1)
        dY = dyf * silz
        dz = dyf * Y * (sz * (1.0 + z * (1.0 - sz)))

        dhn = dh_scr[...]
        dYo = dY * R["eo_e"]
        causal = R["row"] >= R["col"]
        dacs = jnp.zeros((128, 128), F32)
        dacs_t = jnp.zeros((128, 128), F32)
        dxdiag, dxd, dhprev, dBs, dCs, yoff = [], [], [], [], [], []
        for g in range(SSD_GROUPS):
            Bg = R["Bc"][:, g * 128:(g + 1) * 128]
            Cg = R["Cc"][:, g * 128:(g + 1) * 128]
            cols = slice(g * 512, (g + 1) * 512)
            CB = _mm_nt(Cg, Bg)
            dCB = jnp.zeros((128, 128), F32)
            for j in range(4):
                h0 = g * 8 + 2 * j
                pc = slice(h0 * HEAD_DIM, h0 * HEAD_DIM + 128)
                dYst = _split_halves(dY[:, pc])
                dMst = _mm_nt(dYst, X[:, pc])
                mts = []
                for a, h in enumerate((h0, h0 + 1)):
                    acol = acs[:, h:h + 1]
                    arow = acsT[h:h + 1, :]
                    Lm = jnp.exp(jnp.where(causal, acol - arow, NEG))
                    M = CB * Lm
                    dM = dMst[a * 128:(a + 1) * 128]
                    dCB = dCB + dM * Lm
                    G = dM * M
                    dacs = dacs + jnp.where(R["col"] == h, jnp.sum(G, axis=1, keepdims=True), 0.0)
                    dacs_t = dacs_t + jnp.where(R["row"] == h, jnp.sum(G, axis=0, keepdims=True), 0.0)
                    mts.append(M.T)
                dxdiag.append(_mm(jnp.concatenate(mts, axis=1), dYst))
            dS = dhn[:, cols]
            dxd.append(_mm(Bg, dS))
            yoff.append(_mm(Cg, hprev[:, cols]))
            dhprev.append(_mm_tn(Cg, dYo[:, cols]))
            dCs.append(_mm_nt(dYo[:, cols], hprev[:, cols]) + _mm(dCB, Bg))
            dBs.append(_mm_tn(dCB, Cg) + _mm_nt(Xd[:, cols], dS))
        Yoff = jnp.concatenate(yoff, axis=1) * R["eo_e"]
        dXd = jnp.concatenate(dxd, axis=1)
        dX = jnp.concatenate(dxdiag, axis=1) + dXd * R["ds_e"]
        t_state = dXd * Xd
        dacs = dacs + _mm_2pass_r(dY * Yoff - t_state, ET) - dacs_t.T
        v_last = _colsum(t_state + dhn * hprev * R["cd_e"])
        dlast = _mm_exact_r(jnp.broadcast_to(v_last, (8, 1024)), ET)[0:1, :]
        dacs = dacs + jnp.where(R["row"] == 127, dlast, 0.0)
        triu = (R["col"] >= R["row"]).astype(BF16)
        da = _mm_exact_l(triu, dacs)
        ddt = da * R["A"] + _mm(dX * xs_c, ET)
        ddt_raw = ddt * _sigmoid(R["raw"])
        dxs_c = dX * R["dt_e"] + dY * dsk
        dh_scr[...] = jnp.concatenate(dhprev, axis=1) + dhn * R["cd_e"]

        dact = jnp.concatenate([dxs_c] + dBs + dCs, axis=1)
        pre, sg = R["pre"], R["sg"]
        dpre = dact * (sg * (1.0 + pre * (1.0 - sg)))
        ext2_scr[0:8, :] = dpre[120:128, :]
        ext2_scr[8:16, :] = nxt_scr[...]
        nxt_scr[...] = dpre[0:8, :]
        cw = cw_ref[...]
        u_b, dpre_b = p_ref[:, S_XS:S_DT].astype(_MXU), dpre.astype(_MXU)
        dxbc = cw[3:4, :] * dpre
        taps = [_colsum(dpre * p_ref[:, S_XS:S_DT])]
        for s in (1, 2, 3):
            up = (R["col"] - R["row"] == s).astype(_MXU)
            d_s = jnp.concatenate([jnp.dot(up, dpre_b, preferred_element_type=F32)[0:120],
                                   ext2_scr[s:8 + s, :]], axis=0)
            dxbc = dxbc + cw[3 - s:4 - s, :] * d_s
            taps.append(_colsum(dpre * _rows_from_above(u_b, s, ext_scr, R["row"], R["col"])))
        acc_cw_ref[...] += _rows8(taps[::-1] + [_colsum(dpre)])
        acc_w_ref[...] += _rows8([jnp.concatenate(dnw_parts, axis=1), _colsum(dY * xs_c)])
        acc_s_ref[...] += _rows8([_colsum(ddt_raw), _colsum(da * R["dt"])])

        lane = lax.broadcasted_iota(jnp.int32, (128, 128), 1)
        dp_ref[:, S_Z:S_Z + 1024] = dz.astype(dp_ref.dtype)
        dp_ref[:, S_XS:S_DT] = dxbc.astype(dp_ref.dtype)
        dp_ref[:, S_DT:S_DT + 128] = jnp.where(lane < N_HEADS, ddt_raw, 0.0).astype(dp_ref.dtype)
        dp_ref[:, S_DT + 128:S_W] = jnp.zeros((128, 128), dp_ref.dtype)

        @pl.when(i == nc - 1)
        def _():
            acc = acc_s_ref[...]
            dskip = _mm_exact_r(acc_w_ref[...], ET)[1:2, :]
            acc_s_ref[...] = _rows8([acc[0:1, :], acc[1:2, :] * R["A"], dskip])

    const = lambda shape: pl.BlockSpec(shape, lambda i: (0, 0))
    smem = pl.BlockSpec(memory_space=pltpu.SMEM)
    rev = lambda i: (nc - 1 - i, 0)
    return _call(
        body, comm, name="ssd_bwd", grid=(nc,),
        in_specs=[pl.BlockSpec((CHUNK, S_W), rev),
                  pl.BlockSpec((8, S_W), lambda i: (jnp.maximum((nc - 1 - i) * 16 - 1, 0), 0)),
                  pl.BlockSpec((128, 1024), rev),
                  pl.BlockSpec((CHUNK, D_SSD), rev),
                  pl.BlockSpec((CHUNK, D_XBC), rev),
                  pl.BlockSpec((CHUNK, D_SSD), rev),
                  const((4, D_XBC)), const((1, D_XBC)), smem, smem, smem, const((1, 1024)),
                  const((128, 1024)), const((1024, 128))],
        out_specs=[pl.BlockSpec((CHUNK, S_W), rev), const((8, D_XBC)), const((8, 1024)), const((8, 128))],
        out_shape=[jax.ShapeDtypeStruct((L, S_W), _MXU), jax.ShapeDtypeStruct((8, D_XBC), F32),
                   jax.ShapeDtypeStruct((8, 1024), F32), jax.ShapeDtypeStruct((8, 128), F32)],
        scratch_shapes=[pltpu.VMEM((128, 1024), F32), pltpu.VMEM((16, D_XBC), F32),
                        pltpu.VMEM((16, D_XBC), F32), pltpu.VMEM((8, D_XBC), F32)],
        args=(proj_ssd, proj_ssd, hprev_all, ypre, pre, dy, conv_w, conv_b, dt_bias, a_log, d_skip, norm_w, E, ET))


def _rope(t, tab):
    cos, sa, sb = tab[:, 0:128], tab[:, 128:256], tab[:, 256:384]
    outs = []
    for i in range(t.shape[1] // 128):
        tg = t[:, i * 128:(i + 1) * 128]
        outs.append(tg * cos + pltpu.roll(tg, 8, 1) * sa + pltpu.roll(tg, 120, 1) * sb)
    return jnp.concatenate(outs, axis=1)


def _rope_transposed(d, tab):
    cos, sa, sb = tab[:, 0:128], tab[:, 128:256], tab[:, 256:384]
    outs = []
    for i in range(d.shape[1] // 128):
        dg = d[:, i * 128:(i + 1) * 128]
        outs.append(dg * cos + pltpu.roll(dg * sa, 120, 1) + pltpu.roll(dg * sb, 8, 1))
    return jnp.concatenate(outs, axis=1)


def _lo_half(rows):
    return lax.broadcasted_iota(jnp.int32, (rows, 128), 1) < HEAD_DIM


def _native_half(rows, j):
    lo = _lo_half(rows)
    return lo if j % 2 == 0 else jnp.logical_not(lo)


def _kv_native(t, j):
    p = j // 2
    return jnp.where(_native_half(t.shape[0], j), t[:, p * 128:(p + 1) * 128], 0.0)


def _stack_heads(t, j):
    out = []
    for m in (2 * j, 2 * j + 1):
        pair = t[:, m * 128:(m + 1) * 128]
        swapped = pltpu.roll(pair, HEAD_DIM, 1)
        out += [pair, swapped] if j % 2 == 0 else [swapped, pair]
    return jnp.concatenate(out, axis=0)


def _unstack_heads(s, j):
    out = []
    for m in range(2):
        first, second = s[256 * m:256 * m + 128], s[256 * m + 128:256 * m + 256]
        if j % 2 == 0:
            out.append(first + pltpu.roll(second, HEAD_DIM, 1))
        else:
            out.append(pltpu.roll(first, HEAD_DIM, 1) + second)
    return jnp.concatenate(out, axis=1)


def _keep_native(r, j):
    return jnp.where(_native_half(r.shape[0], j), r, 0.0)


def _sink_row(sink_ref, j):
    hid = lax.broadcasted_iota(jnp.int32, (1, 4 * CHUNK), 1) // CHUNK
    row = jnp.zeros((1, 4 * CHUNK), F32)
    for hh in range(4):
        row = jnp.where(hid == hh, sink_ref[4 * j + hh], row)
    return row


def _from_current():
    si = lax.broadcasted_iota(jnp.int32, (CHUNK, 4 * CHUNK), 0)
    qi = lax.broadcasted_iota(jnp.int32, (CHUNK, 4 * CHUNK), 1) % CHUNK
    return si <= qi


def _fold(full, from_cur, pen=0.0):
    return jnp.where(from_cur, full[CHUNK:2 * CHUNK], full[0:CHUNK] + pen)


def _unfold(t, from_cur):
    c = jnp.where(from_cur, t, 0.0)
    return jnp.concatenate([t - c, c], axis=0)


def _softmax_sink(s, sink):
    mx = jnp.maximum(jnp.max(s, axis=0, keepdims=True), sink)
    p = jnp.exp(s - mx)
    esink = jnp.exp(sink - mx)
    inv = 1.0 / (jnp.sum(p, axis=0, keepdims=True) + esink)
    return p * inv, esink * inv


def _swa_inputs(blk, p_ref, prev_ref, tab_ref, ptab_ref):
    tab = tab_ref[...]
    qr = _rope(p_ref[:, A_Q:A_Q + 1024], tab) * ATT_SCALE
    kk = jnp.concatenate([_rope(prev_ref[:, 0:256], ptab_ref[...]), _rope(p_ref[:, A_K:A_K + 256], tab)], axis=0)
    vv = jnp.concatenate([prev_ref[:, 256:512], p_ref[:, A_V:A_V + 256]], axis=0)
    return tab, qr, kk, vv, jnp.where(blk > 0, 0.0, NEG)


def _swa_forward_step(sink_ref, p_ref, prev_ref, tab_ref, ptab_ref, y_ref):
    n = pl.program_id(0)
    _, qr, kk, vv, pen = _swa_inputs(n, p_ref, prev_ref, tab_ref, ptab_ref)
    from_cur = _from_current()
    outs = []
    for j in range(KV_HEADS):
        s = _fold(_mm_nt(_kv_native(kk, j), _stack_heads(qr, j)), from_cur, pen)
        P, _ = _softmax_sink(s, _sink_row(sink_ref, j))
        outs.append(_unstack_heads(_mm_tn(_unfold(P, from_cur), _kv_native(vv, j)), j))
    g = p_ref[:, A_G:A_G + 1024]
    y_ref[:, D_SSD:D_SSD + D_ATT] = (jnp.concatenate(outs, axis=1) * (g * _sigmoid(g))).astype(y_ref.dtype)


def _mixer_forward(proj_ssd, proj_att, tabs, sinks, conv_w, conv_b, dt_bias, a_log, d_skip, norm_w, E, comm=None):
    L = proj_ssd.shape[0]
    nc = L // CHUNK

    def body(p_ref, halo_ref, cw_ref, cb_ref, dtb_ref, alog_ref, dsk_ref, nw_ref, e_ref,
             sink_ref, pa_ref, prev_ref, tab_ref, ptab_ref, y_ref, ypre_ref, hprev_ref, pre_ref, h_scr, ext_scr):
        _ssd_forward_step(p_ref, halo_ref, cw_ref, cb_ref, dtb_ref, alog_ref, dsk_ref, nw_ref, e_ref,
                          y_ref, ypre_ref, hprev_ref, pre_ref, h_scr, ext_scr)
        _swa_forward_step(sink_ref, pa_ref, prev_ref, tab_ref, ptab_ref, y_ref)

    const = lambda shape: pl.BlockSpec(shape, lambda c: (0, 0))
    smem = pl.BlockSpec(memory_space=pltpu.SMEM)
    rows = lambda w: pl.BlockSpec((CHUNK, w), lambda c: (c, 0))
    return _call(
        body, comm, name="mixer_fwd", grid=(nc,),
        in_specs=[rows(S_W), pl.BlockSpec((8, S_W), lambda c: (jnp.maximum(c * 16 - 1, 0), 0)),
                  const((4, D_XBC)), const((1, D_XBC)), smem, smem, smem, const((1, 1024)), const((128, 1024)),
                  smem, rows(A_W), pl.BlockSpec((CHUNK, 512), lambda c: (jnp.maximum(c - 1, 0), 2)),
                  rows(384), pl.BlockSpec((CHUNK, 384), lambda c: (jnp.maximum(c - 1, 0), 0))],
        out_specs=[rows(D_SSD + D_ATT), rows(D_SSD), pl.BlockSpec((128, 1024), lambda c: (c, 0)), rows(D_XBC)],
        out_shape=[jax.ShapeDtypeStruct((L, D_SSD + D_ATT), _MXU), jax.ShapeDtypeStruct((L, D_SSD), F32),
                   jax.ShapeDtypeStruct((nc * 128, 1024), F32), jax.ShapeDtypeStruct((L, D_XBC), F32)],
        scratch_shapes=[pltpu.VMEM((128, 1024), F32), pltpu.VMEM((136, D_XBC), F32)],
        args=(proj_ssd, proj_ssd, conv_w, conv_b, dt_bias, a_log, d_skip, norm_w, E,
              sinks, proj_att, proj_att, tabs, tabs))


def _swa_backward(proj_att, tabs, sinks, dy, comm=None):
    L = proj_att.shape[0]
    nb = L // CHUNK

    def body(sink_ref, p_ref, prev_ref, tab_ref, ptab_ref, dy_ref, dp_ref, dsink_ref, carry_k, carry_v):
        i = pl.program_id(0)
        n = nb - 1 - i

        @pl.when(i == 0)
        def _():
            carry_k[...] = jnp.zeros_like(carry_k)
            carry_v[...] = jnp.zeros_like(carry_v)
            dsink_ref[...] = jnp.zeros_like(dsink_ref)

        tab, qr, kk, vv, pen = _swa_inputs(n, p_ref, prev_ref, tab_ref, ptab_ref)
        from_cur = _from_current()
        g = p_ref[:, A_G:A_G + 1024]
        sgm = _sigmoid(g)
        dyv = dy_ref[...]
        do_all = dyv * (g * sgm)
        lane8 = lax.broadcasted_iota(jnp.int32, (8, 128), 1)
        hid = lax.broadcasted_iota(jnp.int32, (1, 4 * CHUNK), 1) // CHUNK
        o_parts, dq_parts = [], []
        dk_nat = [jnp.zeros((2 * CHUNK, 128), F32) for _ in range(2)]
        dv_nat = [jnp.zeros((2 * CHUNK, 128), F32) for _ in range(2)]
        dsink = jnp.zeros((8, 128), F32)
        for j in range(KV_HEADS):
            qs = _stack_heads(qr, j)
            kkb, vvb = _kv_native(kk, j), _kv_native(vv, j)
            P, psink = _softmax_sink(_fold(_mm_nt(kkb, qs), from_cur, pen), _sink_row(sink_ref, j))
            p_full = _unfold(P, from_cur)
            o_parts.append(_unstack_heads(_mm_tn(p_full, vvb), j))
            do_s = _stack_heads(do_all, j)
            dP = _fold(_mm_nt(vvb, do_s), from_cur)
            D = jnp.sum(P * dP, axis=0, keepdims=True)
            ds_full = _unfold(P * (dP - D), from_cur)
            sd = psink * D
            for hh in range(4):
                dsink = dsink + jnp.where(lane8 == 4 * j + hh, -jnp.sum(jnp.where(hid == hh, sd, 0.0)), 0.0)
            dq_parts.append(_unstack_heads(_mm_tn(ds_full, kkb), j) * ATT_SCALE)
            dk_nat[j // 2] = dk_nat[j // 2] + _keep_native(_mm(ds_full, qs), j)
            dv_nat[j // 2] = dv_nat[j // 2] + _keep_native(_mm(p_full, do_s), j)
        o = jnp.concatenate(o_parts, axis=1)
        dkk = jnp.concatenate(dk_nat, axis=1)
        dvv = jnp.concatenate(dv_nat, axis=1)
        out = dp_ref.dtype
        dp_ref[:, A_Q:A_Q + 1024] = _rope_transposed(jnp.concatenate(dq_parts, axis=1), tab).astype(out)
        dp_ref[:, A_K:A_K + 256] = _rope_transposed(dkk[CHUNK:2 * CHUNK] + carry_k[...], tab).astype(out)
        dp_ref[:, A_V:A_V + 256] = (dvv[CHUNK:2 * CHUNK] + carry_v[...]).astype(out)
        dp_ref[:, A_G:A_G + 1024] = (dyv * o * (sgm * (1.0 + g * (1.0 - sgm)))).astype(out)
        carry_k[...] = dkk[0:CHUNK]
        carry_v[...] = dvv[0:CHUNK]
        dsink_ref[...] += dsink

    rev = lambda i: (nb - 1 - i, 0)
    prev = lambda i: jnp.maximum(nb - 2 - i, 0)
    return _call(
        body, comm, name="swa_bwd", grid=(nb,),
        in_specs=[pl.BlockSpec(memory_space=pltpu.SMEM),
                  pl.BlockSpec((CHUNK, A_W), rev),
                  pl.BlockSpec((CHUNK, 512), lambda i: (prev(i), 2)),
                  pl.BlockSpec((CHUNK, 384), rev),
                  pl.BlockSpec((CHUNK, 384), lambda i: (prev(i), 0)),
                  pl.BlockSpec((CHUNK, D_ATT), lambda i: (nb - 1 - i, 1))],
        out_specs=[pl.BlockSpec((CHUNK, A_W), rev), pl.BlockSpec((8, 128), lambda i: (0, 0))],
        out_shape=[jax.ShapeDtypeStruct((L, A_W), _MXU), jax.ShapeDtypeStruct((8, 128), F32)],
        scratch_shapes=[pltpu.VMEM((CHUNK, 256), F32), pltpu.VMEM((CHUNK, 256), F32)],
        args=(sinks, proj_att, proj_att, tabs, tabs, dy))


def _head(y, x, target, w_out, ln_g, ln_b, *, tm):
    L = x.shape[0]
    nsteps = L // tm

    def body(y_ref, x_ref, t_ref, wo_ref, g_ref, b_ref, dr_ref, dy_ref, acc_ref):
        i = pl.program_id(0)

        @pl.when(i == 0)
        def _():
            acc_ref[...] = jnp.zeros_like(acc_ref)

        r = ALPHA * x_ref[...] + _mm(y_ref[...], wo_ref[...])
        mu = jnp.mean(r, axis=-1, keepdims=True)
        d = r - mu
        rstd = lax.rsqrt(jnp.mean(d * d, axis=-1, keepdims=True) + LN_EPS)
        xh = d * rstd
        gam = g_ref[0:1, :]
        e = xh * gam + b_ref[0:1, :] - t_ref[...]
        dout = e * (1.0 / D_MODEL)
        dxh = dout * gam
        dr = rstd * (dxh - jnp.mean(dxh, axis=-1, keepdims=True)
                     - xh * jnp.mean(dxh * xh, axis=-1, keepdims=True))
        dr_ref[...] = dr
        dy_ref[...] = _mm_nt(dr, wo_ref[...])
        acc_ref[...] += _rows8([_colsum(dout * xh), _colsum(dout), _colsum(e * e) * (0.5 / D_MODEL)])

        @pl.when(i == nsteps - 1)
        def _():
            acc = acc_ref[...]
            tot = jnp.sum(acc[2:3, :])
            rid = lax.broadcasted_iota(jnp.int32, (8, 1024), 0)
            acc_ref[...] = jnp.where(rid == 3, tot, acc)

    const = lambda shape: pl.BlockSpec(shape, lambda i: (0, 0))
    row = lambda w: pl.BlockSpec((tm, w), lambda i: (i, 0))
    return pl.pallas_call(
        body, name="head", grid=(nsteps,),
        in_specs=[row(2048), row(1024), row(1024), const((2048, 1024)), const((1, 1024)), const((1, 1024))],
        out_specs=[row(1024), row(2048), const((8, 1024))],
        out_shape=[jax.ShapeDtypeStruct((L, D_MODEL), F32), jax.ShapeDtypeStruct((L, 2048), F32),
                   jax.ShapeDtypeStruct((8, 1024), F32)],
        compiler_params=_params(("arbitrary",)),
    )(y, x, target, w_out, ln_g, ln_b)


def _gather_w_in(w_shard):
    R = w_shard.shape[0]
    halves = (pl.ds(0, R // 2), pl.ds(R // 2, R // 2))
    any_spec = pl.BlockSpec(memory_space=pl.ANY)

    def body(in_ref, out_ref, send_sems, recv_sems, local_sem):
        x, y, c = _position()

        def slot(p, half=None):
            s = out_ref.at[_index(*p)]
            return s if half is None else s.at[halves[half]]

        def same_core(p):
            return (p[0], p[1], c)

        def other_core(p):
            return (p[0], p[1], 1 - c)

        me, xn, yn, dg = (x, y), (1 - x, y), (x, 1 - y), (1 - x, 1 - y)

        def copy(k, dst, to, src=None):
            return _remote(dst if src is None else src, dst, send_sems.at[k], recv_sems.at[k], to)

        local = pltpu.make_async_copy(in_ref, slot(same_core(me)), local_sem)
        local.start()
        own = [copy(0, slot(same_core(me)), other_core(me), in_ref), copy(1, slot(same_core(me)), same_core(xn), in_ref),
               copy(2, slot(same_core(me)), same_core(yn), in_ref)]
        for cp in own:
            cp.start()
        copy(1, slot(same_core(xn)), same_core(xn)).wait_recv()
        passed = [copy(4, slot(same_core(xn), 1), same_core(yn)), copy(5, slot(same_core(xn)), other_core(me))]
        for cp in passed:
            cp.start()
        copy(2, slot(same_core(yn)), same_core(yn)).wait_recv()
        more = [copy(3, slot(same_core(yn), 0), same_core(xn)), copy(6, slot(same_core(yn)), other_core(me))]
        for cp in more:
            cp.start()
        passed += more
        for k, half in ((3, 0), (4, 1)):
            copy(k, slot(same_core(dg), half), same_core(xn)).wait_recv()
            fwd = copy(7 + half, slot(same_core(dg), half), other_core(me))
            fwd.start()
            passed.append(fwd)
        copy(0, slot(other_core(me)), other_core(me)).wait_recv()
        copy(5, slot(other_core(xn)), other_core(me)).wait_recv()
        copy(6, slot(other_core(yn)), other_core(me)).wait_recv()
        for half in (0, 1):
            copy(7 + half, slot(other_core(dg), half), other_core(me)).wait_recv()
        for cp in own + passed:
            cp.wait_send()
        local.wait()

    return pl.pallas_call(
        body, name="gather_w_in", in_specs=[any_spec], out_specs=any_spec,
        out_shape=jax.ShapeDtypeStruct((N_DEV,) + w_shard.shape, w_shard.dtype),
        scratch_shapes=[pltpu.SemaphoreType.DMA((9,)), pltpu.SemaphoreType.DMA((9,)), pltpu.SemaphoreType.DMA],
    )(w_shard)


def _input_gradient(d_ssd, d_att, w_ssd, w_att, dr, *, tm, comm=None):
    L = dr.shape[0]

    def body(ds_ref, da_ref, ws_ref, wa_ref, dr_ref, o_ref):
        o_ref[...] = ALPHA * dr_ref[...] + _mm_nt(ds_ref[...], ws_ref[...]) + _mm_nt(da_ref[...], wa_ref[...])

    row = lambda w: pl.BlockSpec((tm, w), lambda i: (i, 0))
    const = lambda shape: pl.BlockSpec(shape, lambda i: (0, 0))
    return _call(body, comm, name="dx", grid=(L // tm,),
                 in_specs=[row(S_W), row(A_W), const((D_MODEL, S_W)), const((D_MODEL, A_W)), row(D_MODEL)],
                 out_specs=[row(D_MODEL)], out_shape=[jax.ShapeDtypeStruct((L, D_MODEL), F32)],
                 scratch_shapes=[], args=(d_ssd, d_att, w_ssd, w_att, dr))


SHARD_COLS = D_IN_PROJ // N_DEV
SPLIT = N_SSD_REAL - 4 * SHARD_COLS
RELAYOUT_ROWS = 256


def _unpack_w_in(w_all):
    def body(g_ref, ws_ref, wa_ref):
        for j in range(4):
            ws_ref[:, SHARD_COLS * j:SHARD_COLS * (j + 1)] = g_ref[j]
        ws_ref[:, 4 * SHARD_COLS:N_SSD_REAL] = g_ref[4, :, 0:SPLIT]
        ws_ref[:, N_SSD_REAL:S_W] = jnp.zeros((RELAYOUT_ROWS, S_W - N_SSD_REAL), ws_ref.dtype)
        wa_ref[:, 0:SHARD_COLS - SPLIT] = g_ref[4, :, SPLIT:SHARD_COLS]
        for j in range(5, N_DEV):
            lo = SHARD_COLS * (j - 4) - SPLIT
            wa_ref[:, lo:lo + SHARD_COLS] = g_ref[j]

    return pl.pallas_call(
        body, name="unpack_w_in", grid=(D_MODEL // RELAYOUT_ROWS,),
        in_specs=[pl.BlockSpec((N_DEV, RELAYOUT_ROWS, SHARD_COLS), lambda i: (0, i, 0))],
        out_specs=[pl.BlockSpec((RELAYOUT_ROWS, S_W), lambda i: (i, 0)), pl.BlockSpec((RELAYOUT_ROWS, A_W), lambda i: (i, 0))],
        out_shape=[jax.ShapeDtypeStruct((D_MODEL, S_W), w_all.dtype), jax.ShapeDtypeStruct((D_MODEL, A_W), w_all.dtype)],
        compiler_params=_params(("arbitrary",)),
    )(w_all)


def _dw_in(me1, xb, d, half, tail=None, *, tl=1024):
    L, N = d.shape
    steps = L // tl

    def body(me_ref, x_ref, d_ref, *refs):
        if half == 0:
            p_ref, own_ref, tail_ref, acc = refs
        else:
            t_ref, p_ref, acc = refs
        l = pl.program_id(0)

        @pl.when(l == 0)
        def _():
            acc[...] = jnp.zeros_like(acc)

        acc[...] += _mm_tn(x_ref[...], d_ref[...])

        @pl.when(l == steps - 1)
        def _():
            if half == 0:
                me = me_ref[0]

                @pl.when(me >= 4)
                def _():
                    own_ref[...] = jnp.zeros_like(own_ref)

                tail_ref[...] = acc[:, S_DT:S_W]
            for j in range(4):
                if half == 0:
                    pieces = [(0, acc[:, SHARD_COLS * j:SHARD_COLS * (j + 1)])]
                elif j == 0:
                    pieces = [(0, t_ref[:, 4 * SHARD_COLS - S_DT:N_SSD_REAL - S_DT]), (SPLIT, acc[:, 0:SHARD_COLS - SPLIT])]
                else:
                    lo = SHARD_COLS * j - SPLIT
                    pieces = [(0, acc[:, lo:lo + SHARD_COLS])]
                for off, blk in pieces:
                    p_ref[j, :, off:off + blk.shape[1]] = blk.astype(p_ref.dtype)
                    if half == 0:
                        @pl.when(me == j)
                        def _(off=off, blk=blk):
                            own_ref[:, off:off + blk.shape[1]] = blk

    once = pl.Buffered(1)
    whole = lambda shape: pl.BlockSpec(shape, lambda l: (0,) * len(shape), pipeline_mode=once)
    in_specs = [pl.BlockSpec(memory_space=pltpu.SMEM), pl.BlockSpec((tl, D_MODEL), lambda l: (l, 0)),
                pl.BlockSpec((tl, N), lambda l: (l, 0))]
    args = [me1, xb, d]
    if half == 0:
        out_shape = [jax.ShapeDtypeStruct((4, D_MODEL, SHARD_COLS), BF16), jax.ShapeDtypeStruct((D_MODEL, SHARD_COLS), F32),
                     jax.ShapeDtypeStruct((D_MODEL, S_W - S_DT), F32)]
    else:
        in_specs.append(whole(tail.shape))
        args.append(tail)
        out_shape = [jax.ShapeDtypeStruct((4, D_MODEL, SHARD_COLS), F32)]
    return pl.pallas_call(
        body, name="dw_in_%d" % half, grid=(steps,), in_specs=in_specs,
        out_specs=[whole(o.shape) for o in out_shape], out_shape=out_shape,
        scratch_shapes=[pltpu.VMEM((D_MODEL, N), F32)], compiler_params=_params(("arbitrary",)),
    )(*args)


def _pair_swap(stack):
    def body(in_ref, out_ref, send_sems, recv_sems):
        x, y, c = _position()
        cps = [_remote(in_ref.at[2 * oy + (1 - c)], out_ref.at[oy], send_sems.at[oy], recv_sems.at[oy], (x, y, 1 - c))
               for oy in range(2)]
        for cp in cps:
            cp.start()
        for cp in cps:
            cp.wait_recv()
        for cp in cps:
            cp.wait_send()

    any_spec = pl.BlockSpec(memory_space=pl.ANY)
    return pl.pallas_call(
        body, name="pair_swap", in_specs=[any_spec], out_specs=any_spec,
        out_shape=jax.ShapeDtypeStruct((2,) + stack.shape[1:], stack.dtype),
        scratch_shapes=[pltpu.SemaphoreType.DMA((2,)), pltpu.SemaphoreType.DMA((2,))],
    )(stack)


def _pair_sum(pos3, stack, swapped, own_lo):
    def body(pos_ref, a_ref, b_ref, lo_ref, chip_ref, own_ref):
        oy = pl.program_id(1)
        t = a_ref[0] + b_ref[0]
        chip_ref[0] = t.astype(chip_ref.dtype)

        @pl.when((pos_ref[0] == 0) & (oy == 0))
        def _():
            own_ref[...] = lo_ref[...]

        @pl.when((pos_ref[0] == 1) & (oy == pos_ref[1]))
        def _():
            own_ref[...] = t

    blk = (1, RELAYOUT_ROWS, SHARD_COLS)
    flat = pl.BlockSpec((RELAYOUT_ROWS, SHARD_COLS), lambda i, oy, pos: (i, 0))
    return pl.pallas_call(
        body, name="pair_sum",
        grid_spec=pltpu.PrefetchScalarGridSpec(
            num_scalar_prefetch=1, grid=(D_MODEL // RELAYOUT_ROWS, 2),
            in_specs=[pl.BlockSpec(blk, lambda i, oy, pos: (2 * oy + pos[2], i, 0)),
                      pl.BlockSpec(blk, lambda i, oy, pos: (oy, i, 0)), flat],
            out_specs=[pl.BlockSpec(blk, lambda i, oy, pos: (oy, i, 0)), flat]),
        out_shape=[jax.ShapeDtypeStruct((2, D_MODEL, SHARD_COLS), BF16), jax.ShapeDtypeStruct((D_MODEL, SHARD_COLS), F32)],
        compiler_params=_params(("arbitrary", "arbitrary")),
    )(pos3, stack, swapped, own_lo)


def _adamw_math(w, g, m, v):
    m = ADAM_B1 * m + (1.0 - ADAM_B1) * g
    v = ADAM_B2 * v + (1.0 - ADAM_B2) * (g * g)
    m_hat = m / (1.0 - ADAM_B1 ** ADAM_STEP)
    v_hat = v / (1.0 - ADAM_B2 ** ADAM_STEP)
    delta = -ADAM_LR * (m_hat / (jnp.sqrt(v_hat) + ADAM_EPS) + ADAM_WD * w)
    return delta, m, v


def _adamw_shard(n_recv, g_own, recv, w, m, v, *, rows, name):
    R, C = g_own.shape

    def body(n_ref, g_ref, r_ref, w_ref, m_ref, v_ref, go_ref, d_ref, mo_ref, vo_ref):
        g = g_ref[...]
        for k in range(N_DEV - 1):
            g = g + jnp.where(k < n_ref[0], r_ref[k].astype(F32), 0.0)
        d, mn, vn = _adamw_math(w_ref[...], g, m_ref[...], v_ref[...])
        go_ref[...] = g
        d_ref[...] = d
        mo_ref[...] = mn
        vo_ref[...] = vn

    blk = pl.BlockSpec((rows, C), lambda i: (i, 0))
    return pl.pallas_call(
        body, name=name, grid=(R // rows,),
        in_specs=[pl.BlockSpec(memory_space=pltpu.SMEM), blk,
                  pl.BlockSpec((N_DEV - 1, rows, C), lambda i: (0, i, 0)), blk, blk, blk],
        out_specs=[blk] * 4, out_shape=[jax.ShapeDtypeStruct((R, C), F32)] * 4,
        compiler_params=_params(("arbitrary",)),
    )(n_recv, g_own, recv, w, m, v)


def _minor_rows_view(a):
    return jnp.transpose(a, (2, 0, 1)).reshape(SHARD_COLS * 8, 128)


def _from_minor_rows_view(v):
    return jnp.transpose(v.reshape(SHARD_COLS, 8, 128), (1, 2, 0)).reshape(1, D_MODEL, SHARD_COLS)


def _adamw_w_in(n_recv, g_own, recv, w, m, v, passenger):
    C = SHARD_COLS
    pad = -C % 128
    parts = 4
    rows = passenger.shape[0] // parts

    def body(n_ref, g_ref, r_ref, w_ref, m_ref, v_ref, src_ref, go_ref, d_ref, mo_ref, vo_ref, dst_ref, sems):
        moves = [pltpu.make_async_copy(src_ref.at[pl.ds(p * rows, rows)], dst_ref.at[pl.ds(p * rows, rows)], sems.at[p])
                 for p in range(parts)]
        for mv in moves:
            mv.start()
        for q in range(D_MODEL // 128):
            band = pl.ds(q * 128, 128)
            g = g_ref[band, :]
            for k in range(N_DEV - 1):
                g = g + jnp.where(k < n_ref[0], r_ref[k, band, :].astype(F32), 0.0)
            g = jnp.pad(g, ((0, 0), (0, pad))).T[0:C]
            rows8 = pl.ds(q, C, stride=8)
            d, mn, vn = _adamw_math(w_ref[rows8, :], g, m_ref[rows8, :], v_ref[rows8, :])
            go_ref[rows8, :] = g
            d_ref[rows8, :] = d
            mo_ref[rows8, :] = mn
            vo_ref[rows8, :] = vn
        for mv in moves:
            mv.wait()

    vmem = pl.BlockSpec(memory_space=pltpu.VMEM)
    anywhere = pl.BlockSpec(memory_space=pl.ANY)
    return pl.pallas_call(
        body, name="adamw_w_in",
        out_shape=[jax.ShapeDtypeStruct(w.shape, F32)] * 4 + [jax.ShapeDtypeStruct(passenger.shape, passenger.dtype)],
        in_specs=[pl.BlockSpec(memory_space=pltpu.SMEM)] + [vmem] * 5 + [anywhere],
        out_specs=[vmem] * 4 + [anywhere], scratch_shapes=[pltpu.SemaphoreType.DMA((parts,))],
        compiler_params=_params(),
    )(n_recv, g_own, recv, w, m, v, passenger)


SMALL = ("conv_b", "dt_bias", "a_log", "d_skip", "ssd_norm_w", "attn_sinks", "ln_g", "ln_b")


def _adamw_small(gathered, params):
    n_p = len(SMALL)

    def body(*refs):
        acc = []
        for r in refs[:5]:
            t = r[0]
            for k in range(1, N_DEV):
                t = t + r[k]
            acc.append(t)
        head, conv, norm, scal, sink = acc
        grads = dict(conv_b=conv[4:5, :], dt_bias=scal[0:1, 0:N_HEADS], a_log=scal[1:2, 0:N_HEADS],
                     d_skip=scal[2:3, 0:N_HEADS], ssd_norm_w=norm[0:1, :], attn_sinks=sink[0:1, 0:N_HEADS],
                     ln_g=head[0:1, :], ln_b=head[1:2, :])
        wmv = refs[5:5 + 3 * n_p]
        outs = refs[5 + 3 * n_p:]
        outs[0][...] = head[3:4, 0:1]
        outs[1][...] = conv[0:4, :]
        for i, name in enumerate(SMALL):
            w_ref, m_ref, v_ref = wmv[3 * i:3 * i + 3]
            g = grads[name]
            d, mn, vn = _adamw_math(w_ref[...], g, m_ref[...], v_ref[...])
            for o_ref, val in zip(outs[2 + 4 * i:6 + 4 * i], (g, d, mn, vn)):
                o_ref[...] = val

    flat = [a for name in SMALL for a in params[name]]
    out_shape = [jax.ShapeDtypeStruct((1, 1), F32), jax.ShapeDtypeStruct((4, D_XBC), F32)]
    for name in SMALL:
        out_shape += [jax.ShapeDtypeStruct(params[name][0].shape, F32)] * 4
    res = pl.pallas_call(body, name="adamw_small", out_shape=out_shape, compiler_params=_params())(*gathered, *flat)
    return res[0], res[1], {name: res[2 + 4 * i:6 + 4 * i] for i, name in enumerate(SMALL)}


def _adamw_plain(g, w, m, v):
    def body(g_ref, w_ref, m_ref, v_ref, d_ref, mo_ref, vo_ref):
        d, mn, vn = _adamw_math(w_ref[...], g_ref[...], m_ref[...], v_ref[...])
        d_ref[...] = d
        mo_ref[...] = mn
        vo_ref[...] = vn

    return pl.pallas_call(
        body, name="adamw_conv_w", out_shape=[jax.ShapeDtypeStruct(w.shape, F32)] * 3,
        compiler_params=_params(),
    )(g, w, m, v)


def _lane_pattern(fn):
    return np.asarray([fn(l % HEAD_DIM) for l in range(128)], np.float32)


ROPE_INV = _lane_pattern(lambda r: ROPE_THETA ** (-2.0 * (r % 8) / ROPE_DIM) if r < ROPE_DIM else 0.0)
ROPE_SIN_A = _lane_pattern(lambda r: 1.0 if 8 <= r < ROPE_DIM else 0.0)
ROPE_SIN_B = _lane_pattern(lambda r: -1.0 if r < 8 else 0.0)


def _rope_tables(positions):
    ang = positions.astype(F32)[:, None] * ROPE_INV[None, :]
    sn = jnp.sin(ang)
    return jnp.concatenate([jnp.cos(ang), sn * ROPE_SIN_A[None, :], sn * ROPE_SIN_B[None, :]], axis=1)


def _expansion():
    E = np.arange(1024)[None, :] // HEAD_DIM == np.arange(128)[:, None]
    return jnp.asarray(E, BF16), jnp.asarray(E.T, BF16)


def _ssd_args(conv_w, conv_b, dt_bias, a_log, d_skip, norm_w, E):
    return (conv_w, conv_b, dt_bias.reshape(-1), a_log.reshape(-1), d_skip.reshape(-1), norm_w, E)


def kernel(x, positions, w_in, conv_w, conv_b, dt_bias, a_log, d_skip, ssd_norm_w, attn_sinks, w_out, ln_g, ln_b, loss_target, m_w_in, m_conv_w, m_conv_b, m_dt_bias, m_a_log, m_d_skip, m_ssd_norm_w, m_attn_sinks, m_w_out, m_ln_g, m_ln_b, v_w_in, v_conv_w, v_conv_b, v_dt_bias, v_a_log, v_d_skip, v_ssd_norm_w, v_attn_sinks, v_w_out, v_ln_g, v_ln_b):
    me = _index(*_position())
    me1 = me.reshape(1).astype(jnp.int32)
    x0, target = x[0], loss_target[0]
    bf16_shard = lambda shape: jax.ShapeDtypeStruct(shape, BF16)
    E, ET = _expansion()
    tabs = _rope_tables(positions[0])
    sinks = attn_sinks.reshape(-1)

    w_ssd, w_att = _unpack_w_in(_gather_w_in(w_in[0].astype(BF16)))
    gather_conv_w = _Hosted([conv_w[0]], [jax.ShapeDtypeStruct((N_DEV,) + conv_w.shape[1:], F32)],
                            [_Flow("gather", 0, 0)])

    proj_ssd, proj_att, xb, conv_w_all = _in_proj(x0, w_ssd, w_att, tm=512, comm=gather_conv_w)
    conv_w_f = jnp.transpose(conv_w_all, (1, 0, 2)).reshape(4, D_XBC)
    ssd_args = _ssd_args(conv_w_f, conv_b, dt_bias, a_log, d_skip, ssd_norm_w, E)
    gather_w_out = _Hosted([w_out[0].astype(BF16)], [bf16_shard((N_DEV, 256, D_MODEL))], [_Flow("gather", 0, 0)])
    y, ypre, hprev, pre, w_out_all = _mixer_forward(proj_ssd, proj_att, tabs, sinks, *ssd_args, comm=gather_w_out)
    w_out_f = w_out_all.reshape(2 * D_MODEL, D_MODEL)
    dr, dy, acc_head = _head(y, x0, target, w_out_f, ln_g, ln_b, tm=512)

    dw_out, dw_out_bf16 = _matmul_tn(y, dr, tl=1024, tn=D_MODEL, name="dw_out", emit_bf16=True)
    own_out = lax.dynamic_index_in_dim(dw_out.reshape(N_DEV, 256, D_MODEL), me, axis=0, keepdims=False)
    send_out = _Hosted([dw_out_bf16.reshape(N_DEV, 256, D_MODEL)], [bf16_shard((N_DEV - 1, 256, D_MODEL))],
                       [_Flow("exchange", 0, 0)])
    d_ssd, acc_cw, acc_w, acc_s, recv_out = _ssd_backward(proj_ssd, hprev, ypre, pre, dy, *ssd_args, ET, comm=send_out)
    parts_lo, own_lo, dw_dt_block = _dw_in(me1, xb, d_ssd, 0)
    recv_shape = bf16_shard((N_DEV - 1, D_MODEL, SHARD_COLS))
    send_lo = _Hosted([parts_lo], [recv_shape], [_Flow("exchange", 0, 0, target_x=0, target_c=0)])
    d_att, dsink, recv_in = _swa_backward(proj_att, tabs, sinks, dy, comm=send_lo)
    (stack_hi,) = _dw_in(me1, xb, d_att, 1, dw_dt_block)
    pos3 = jnp.stack(_position()).astype(jnp.int32)
    chip_hi, own_in = _pair_sum(pos3, stack_hi, _pair_swap(stack_hi), own_lo)
    accs = [acc_head, acc_cw, acc_w, acc_s, dsink]
    send_hi = _Hosted([chip_hi, recv_in, parts_lo] + accs,
                      [recv_shape] + [jax.ShapeDtypeStruct((N_DEV,) + a.shape, F32) for a in accs],
                      [_Flow("chip_exchange", 0, 0, target_x=1), _Flow("exchange", 2, 0, target_x=0, target_c=1)]
                      + [_Flow("gather", 3 + i, 1 + i) for i in range(5)], aliases={1: 0})
    dx, recv_in, *gathered = _input_gradient(d_ssd, d_att, w_ssd, w_att, dr, tm=256, comm=send_hi)
    n_recv_in = jnp.where(me < 4, N_DEV - 1, 3).reshape(1).astype(jnp.int32)
    n_recv_out = jnp.full((1,), N_DEV - 1, jnp.int32)

    *updated_in, dx = _adamw_w_in(n_recv_in, own_in, recv_in, _minor_rows_view(w_in), _minor_rows_view(m_w_in),
                                  _minor_rows_view(v_w_in), dx)
    g_in, d_in, nm_in, nv_in = [_from_minor_rows_view(r) for r in updated_in]
    g_out, d_out, nm_out, nv_out = _adamw_shard(n_recv_out, own_out, recv_out, w_out[0], m_w_out[0], v_w_out[0],
                                                rows=256, name="adamw_w_out")
    loss, g_conv_w, small = _adamw_small(gathered, dict(
        conv_b=(conv_b, m_conv_b, v_conv_b), dt_bias=(dt_bias, m_dt_bias, v_dt_bias), a_log=(a_log, m_a_log, v_a_log),
        d_skip=(d_skip, m_d_skip, v_d_skip), ssd_norm_w=(ssd_norm_w, m_ssd_norm_w, v_ssd_norm_w),
        attn_sinks=(attn_sinks, m_attn_sinks, v_attn_sinks), ln_g=(ln_g, m_ln_g, v_ln_g), ln_b=(ln_b, m_ln_b, v_ln_b)))
    g_cw = lax.dynamic_slice_in_dim(g_conv_w, me * (D_XBC // N_DEV), D_XBC // N_DEV, axis=1)
    d_cw, nm_cw, nv_cw = _adamw_plain(g_cw, conv_w[0], m_conv_w[0], v_conv_w[0])

    def leaves(i, big_in, cw, big_out):
        mid = [small[k][i] for k in ("conv_b", "dt_bias", "a_log", "d_skip", "ssd_norm_w", "attn_sinks")]
        return [big_in, cw[None]] + mid + [big_out[None], small["ln_g"][i], small["ln_b"][i]]

    return (loss.reshape(()), dx[None], *leaves(0, g_in, g_cw, g_out), *leaves(1, d_in, d_cw, d_out),
            *leaves(2, nm_in, nm_cw, nm_out), *leaves(3, nv_in, nv_cw, nv_out))
```

```python
import jax
import jax.numpy as jnp
from jax import lax
from jax.experimental import pallas as pl
from jax.experimental.pallas import tpu as pltpu
import numpy as np

F32 = jnp.float32
BF16 = jnp.bfloat16
_MXU = jnp.bfloat16

N_DEV = 8
D_MODEL = 1024
D_SSD = 1024
D_ATT = 1024
HEAD_DIM = 64
N_HEADS = 16
SSD_GROUPS = 2
KV_HEADS = 4
CHUNK = 128
D_XBC = 1536
D_IN_PROJ = 5136
ROPE_DIM = 16
ROPE_THETA = 500000.0
ALPHA = (2.0 * 1) ** 0.25
LN_EPS = 1e-5
RMS_EPS = 1e-5
ATT_SCALE = HEAD_DIM ** -0.5
NEG = -1e30

S_Z, S_XS, S_B, S_C, S_DT, S_W = 0, 1024, 2048, 2304, 2560, 2816
N_SSD_REAL = 2576
A_Q, A_K, A_V, A_G, A_W = 0, 1024, 1280, 1536, 2560

ADAM_LR = 0.001
ADAM_B1 = 0.9
ADAM_B2 = 0.999
ADAM_EPS = 1e-08
ADAM_WD = 0.01
ADAM_STEP = 10

VMEM_LIMIT = 48 * 1024 * 1024
MESH = pl.DeviceIdType.MESH


def _params(sem=None):
    return pltpu.CompilerParams(dimension_semantics=sem, vmem_limit_bytes=VMEM_LIMIT)


def _mm(a, b):
    return jnp.dot(a.astype(_MXU), b.astype(_MXU), preferred_element_type=F32)


def _mm_nt(a, b):
    return lax.dot_general(a.astype(_MXU), b.astype(_MXU), (((1,), (1,)), ((), ())),
                           preferred_element_type=F32)


def _mm_tn(a, b):
    return lax.dot_general(a.astype(_MXU), b.astype(_MXU), (((0,), (0,)), ((), ())),
                           preferred_element_type=F32)


def _split3(v):
    hi = v.astype(BF16)
    r = v - hi.astype(F32)
    mid = r.astype(BF16)
    lo = (r - mid.astype(F32)).astype(BF16)
    return hi, mid, lo


def _mm_exact_r(v, p01):
    hi, mid, lo = _split3(v)
    d = lambda a: jnp.dot(a, p01, preferred_element_type=F32)
    return d(hi) + d(mid) + d(lo)


def _mm_exact_l(p01, v):
    hi, mid, lo = _split3(v)
    d = lambda a: jnp.dot(p01, a, preferred_element_type=F32)
    return d(hi) + d(mid) + d(lo)


def _mm_2pass_r(v, p01):
    hi = v.astype(BF16)
    lo = (v - hi.astype(F32)).astype(BF16)
    return jnp.dot(hi, p01, preferred_element_type=F32) + jnp.dot(lo, p01, preferred_element_type=F32)


def _sigmoid(x):
    return 1.0 / (1.0 + jnp.exp(-x))


def _softplus(x):
    e = jnp.exp(-jnp.abs(x))
    u = 1.0 + e
    log1p = jnp.where(u == 1.0, e, jnp.log(u) * (e / (u - 1.0)))
    return jnp.maximum(x, 0.0) + log1p


def _rows8(rows):
    n = rows[0].shape[1]
    rid = lax.broadcasted_iota(jnp.int32, (8, n), 0)
    out = jnp.zeros((8, n), F32)
    for k, r in enumerate(rows):
        out = out + jnp.where(rid == k, r, 0.0)
    return out


def _colsum(a):
    return jnp.sum(a, axis=0, keepdims=True)


def _in_proj(x, w_ssd, w_att, *, tm, comm=None):
    L, K = x.shape

    def body(x_ref, ws_ref, wa_ref, ps_ref, pa_ref, xb_ref):
        xb = x_ref[...].astype(_MXU)
        xb_ref[...] = xb
        ps_ref[...] = jnp.dot(xb, ws_ref[...], preferred_element_type=F32)
        pa_ref[...] = jnp.dot(xb, wa_ref[...], preferred_element_type=F32)

    row = lambda w: pl.BlockSpec((tm, w), lambda i: (i, 0))
    resident = lambda a: pl.BlockSpec(a.shape, lambda i: (0, 0), pipeline_mode=pl.Buffered(1))
    return _call(
        body, comm, name="in_proj", grid=(L // tm,),
        in_specs=[row(K), resident(w_ssd), resident(w_att)], out_specs=[row(S_W), row(A_W), row(K)],
        out_shape=[jax.ShapeDtypeStruct((L, S_W), F32), jax.ShapeDtypeStruct((L, A_W), F32),
                   jax.ShapeDtypeStruct((L, K), _MXU)],
        scratch_shapes=[], args=(x, w_ssd, w_att))


def _matmul_tn(a, g, *, tl, tn, name, emit_bf16=False):
    L, M = a.shape
    N = g.shape[1]
    last = L // tl - 1

    def body(a_ref, g_ref, o_ref, *rest):
        @pl.when(pl.program_id(1) == 0)
        def _():
            o_ref[...] = jnp.zeros_like(o_ref)

        o_ref[...] += _mm_tn(a_ref[...], g_ref[...])
        if emit_bf16:
            @pl.when(pl.program_id(1) == last)
            def _():
                rest[0][...] = o_ref[...].astype(BF16)

    spec = pl.BlockSpec((M, tn), lambda j, l: (0, j))
    res = pl.pallas_call(
        body, name=name, grid=(N // tn, L // tl),
        in_specs=[pl.BlockSpec((tl, M), lambda j, l: (l, 0)), pl.BlockSpec((tl, tn), lambda j, l: (l, j))],
        out_specs=[spec, spec] if emit_bf16 else [spec],
        out_shape=[jax.ShapeDtypeStruct((M, N), F32)] + ([jax.ShapeDtypeStruct((M, N), BF16)] if emit_bf16 else []),
        compiler_params=_params(("arbitrary", "arbitrary")),
    )(a, g)
    return res if emit_bf16 else res[0]


def _position():
    return lax.axis_index("x"), lax.axis_index("y"), lax.axis_index("c")


def _index(px, py, pc):
    return 4 * px + 2 * py + pc


def _flip(pos, k):
    x, y, c = pos
    return ((1 - x) if (k >> 2) & 1 else x, (1 - y) if (k >> 1) & 1 else y, (1 - c) if k & 1 else c)


def _when(cond, fn):
    if cond is True:
        fn()
    else:
        pl.when(cond)(fn)


def _remote(src, dst, send_sem, recv_sem, peer):
    return pltpu.make_async_remote_copy(src_ref=src, dst_ref=dst, send_sem=send_sem, recv_sem=recv_sem,
                                        device_id=peer, device_id_type=MESH)


class _Flow:
    def __init__(self, kind, operand, result, target_x=None, target_c=None):
        self.kind, self.operand, self.result, self.target_x, self.target_c = kind, operand, result, target_x, target_c

    def owns(self, pos):
        if self.target_x is None:
            return True
        cond = pos[0] == self.target_x
        return cond if self.target_c is None else cond & (pos[2] == self.target_c)


class _Hosted:
    def __init__(self, operands, out_shapes, flows, aliases=None):
        self.operands, self.out_shapes, self.flows = operands, out_shapes, flows
        self.aliases = aliases or {}

    def plan(self, ins, outs, send_sems, recv_sems, local_sems):
        me = _position()
        mi = _index(*me)
        sends, recvs, locals_ = [], [], []
        for row, f in enumerate(self.flows):
            src, dst = ins[f.operand], outs[f.result]
            for k in range(1, N_DEV):
                peer = _flip(me, k)
                sems = (send_sems.at[row, k - 1], recv_sems.at[row, k - 1])
                if f.kind == "exchange":
                    owner = _index(*peer) if f.target_x is None else 2 * peer[1] + peer[2]
                    cp = _remote(src.at[owner], dst.at[k - 1], *sems, peer)
                    sends.append((f.owns(peer), cp))
                    recvs.append((f.owns(me), cp))
                elif f.kind == "chip_exchange":
                    if k & 1:
                        continue
                    cp = _remote(src.at[peer[1]], dst.at[k // 2 - 1], *sems, peer)
                    sends.append((peer[0] == f.target_x, cp))
                    recvs.append((me[0] == f.target_x, cp))
                else:
                    sends.append((True, _remote(src, dst.at[mi], *sems, peer)))
                    recvs.append((True, _remote(src, dst.at[_index(*peer)], *sems, peer)))
            if f.kind == "gather":
                locals_.append(pltpu.make_async_copy(src, dst.at[mi], local_sems.at[row]))

        def start():
            for cp in locals_:
                cp.start()
            for cond, cp in sends:
                _when(cond, cp.start)

        def wait():
            for cond, cp in recvs:
                _when(cond, cp.wait_recv)
            for cond, cp in sends:
                _when(cond, cp.wait_send)
            for cp in locals_:
                cp.wait()

        return start, wait


def _call(body, comm, *, name, grid, in_specs, out_specs, out_shape, scratch_shapes, args, aliases=None):
    io_alias = dict(aliases or {})
    semantics = ("arbitrary",) * len(grid)
    if comm is None:
        return pl.pallas_call(body, name=name, grid=grid, in_specs=in_specs, out_specs=out_specs, out_shape=out_shape,
                              scratch_shapes=scratch_shapes, input_output_aliases=io_alias,
                              compiler_params=_params(semantics))(*args)
    n_in, n_out, n_scr = len(args), len(out_shape), len(scratch_shapes)
    c_in, c_out, rows = len(comm.operands), len(comm.out_shapes), len(comm.flows)

    def hosted(*refs):
        ins, refs = refs[:n_in], refs[n_in:]
        cins, refs = refs[:c_in], refs[c_in:]
        outs, refs = refs[:n_out], refs[n_out:]
        couts, refs = refs[:c_out], refs[c_out:]
        scr, (send_sems, recv_sems, local_sems) = refs[:n_scr], refs[n_scr:]
        start, wait = comm.plan(cins, couts, send_sems, recv_sems, local_sems)
        ids = [pl.program_id(d) for d in range(len(grid))]
        first, last = ids[0] == 0, ids[0] == grid[0] - 1
        for d in range(1, len(grid)):
            first, last = first & (ids[d] == 0), last & (ids[d] == grid[d] - 1)
        pl.when(first)(start)
        body(*ins, *outs, *scr)
        pl.when(last)(wait)

    for ci, co in comm.aliases.items():
        io_alias[n_in + ci] = n_out + co
    any_spec = pl.BlockSpec(memory_space=pl.ANY)
    sems = [pltpu.SemaphoreType.DMA((rows, N_DEV - 1)), pltpu.SemaphoreType.DMA((rows, N_DEV - 1)),
            pltpu.SemaphoreType.DMA((rows,))]
    return pl.pallas_call(
        hosted, name=name, grid=grid, in_specs=list(in_specs) + [any_spec] * c_in,
        out_specs=list(out_specs) + [any_spec] * c_out, out_shape=list(out_shape) + list(comm.out_shapes),
        scratch_shapes=list(scratch_shapes) + sems, input_output_aliases=io_alias,
        compiler_params=_params(semantics))(*args, *comm.operands)


def _head_row(ref, width, rep):
    hid = lax.broadcasted_iota(jnp.int32, (1, width), 1) // rep
    row = jnp.zeros((1, width), F32)
    for h in range(N_HEADS):
        row = jnp.where(hid == h, ref[h], row)
    return row


def _rows_from_above(u_b, s, ext_scr, row, col):
    down = (row - col == s).astype(_MXU)
    return jnp.concatenate([ext_scr[8 - s:16 - s, :], jnp.dot(down, u_b, preferred_element_type=F32)[8:128]], axis=0)


def _ssd_recompute(first, p_ref, halo_ref, cw_ref, cb_ref, dtb_ref, alog_ref, e_ref, ext_scr, pre=None):
    row = lax.broadcasted_iota(jnp.int32, (128, 128), 0)
    col = lax.broadcasted_iota(jnp.int32, (128, 128), 1)
    ext_scr[0:8, :] = jnp.where(first, 0.0, halo_ref[:, S_XS:S_DT])
    if pre is not None:
        ext_scr[8:16, :] = p_ref[0:8, S_XS:S_DT]
    else:
        ext_scr[8:136, :] = p_ref[:, S_XS:S_DT]
        cw = cw_ref[...]
        pre = (cb_ref[0:1, :] + cw[3:4, :] * ext_scr[8:136, :] + cw[2:3, :] * ext_scr[7:135, :]
               + cw[1:2, :] * ext_scr[6:134, :] + cw[0:1, :] * ext_scr[5:133, :])
    sg = _sigmoid(pre)
    act = pre * sg
    lane = lax.broadcasted_iota(jnp.int32, (1, 128), 1)
    A = jnp.where(lane < N_HEADS, -jnp.exp(_head_row(alog_ref, 128, 1)), 0.0)
    raw = p_ref[:, S_DT:S_DT + 128] + _head_row(dtb_ref, 128, 1)
    dt = _softplus(raw)
    dA = dt * A
    tril = (row >= col).astype(BF16)
    acs = _mm_exact_l(tril, dA)
    last = acs[127:128, :]
    ds = jnp.exp(last - acs)
    eo = jnp.exp(acs)
    E = e_ref[...]
    ex = _mm_2pass_r(jnp.concatenate([dt, ds, eo], axis=0), E)
    dt_e, ds_e, eo_e = ex[0:128], ex[128:256], ex[256:384]
    xs_c = act[:, 0:1024]
    X = xs_c * dt_e
    return dict(pre=pre, sg=sg, xs_c=xs_c, Bc=act[:, 1024:1280], Cc=act[:, 1280:1536], A=A, raw=raw, dt=dt,
                acs=acs, acsT=acs.T, eo_e=eo_e, ds_e=ds_e, dt_e=dt_e, cd_e=eo_e[127:128, :],
                X=X, Xd=X * ds_e, row=row, col=col)


def _split_halves(t):
    lo = _lo_half(CHUNK)
    return jnp.concatenate([jnp.where(lo, t, 0.0), jnp.where(lo, 0.0, t)], axis=0)


def _ssd_core(R, hprev):
    causal = R["row"] >= R["col"]
    acs, acsT, X = R["acs"], R["acsT"], R["X"]
    ydiag, yoff, snew = [], [], []
    for g in range(SSD_GROUPS):
        Bg = R["Bc"][:, g * 128:(g + 1) * 128]
        Cg = R["Cc"][:, g * 128:(g + 1) * 128]
        cols = slice(g * 512, (g + 1) * 512)
        CB = _mm_nt(Cg, Bg)
        snew.append(_mm_tn(Bg, R["Xd"][:, cols]))
        yoff.append(_mm(Cg, hprev[:, cols]))
        for j in range(4):
            h0 = g * 8 + 2 * j
            ms = [CB * jnp.exp(jnp.where(causal, acs[:, h:h + 1] - acsT[h:h + 1, :], NEG)) for h in (h0, h0 + 1)]
            ydiag.append(_mm(jnp.concatenate(ms, axis=1), _split_halves(X[:, h0 * HEAD_DIM:h0 * HEAD_DIM + 128])))
    Y = jnp.concatenate(ydiag, axis=1) + jnp.concatenate(yoff, axis=1) * R["eo_e"]
    return Y, jnp.concatenate(snew, axis=1)


def _ssd_forward_step(p_ref, halo_ref, cw_ref, cb_ref, dtb_ref, alog_ref, dsk_ref, nw_ref, e_ref,
                      y_ref, ypre_ref, hprev_ref, pre_ref, h_scr, ext_scr):
    c = pl.program_id(0)
    first = c == 0

    @pl.when(first)
    def _():
        h_scr[...] = jnp.zeros_like(h_scr)

    R = _ssd_recompute(first, p_ref, halo_ref, cw_ref, cb_ref, dtb_ref, alog_ref, e_ref, ext_scr)
    hprev = h_scr[...]
    hprev_ref[...] = hprev
    pre_ref[...] = R["pre"]
    Y, snew = _ssd_core(R, hprev)
    h_scr[...] = hprev * R["cd_e"] + snew
    Y = Y + _head_row(dsk_ref, D_SSD, HEAD_DIM) * R["xs_c"]
    ypre_ref[...] = Y
    z = p_ref[:, S_Z:S_Z + 1024]
    yf = Y * (z * _sigmoid(z))
    outs = []
    for g in range(SSD_GROUPS):
        yg = yf[:, g * 512:(g + 1) * 512]
        r = lax.rsqrt(jnp.mean(yg * yg, axis=-1, keepdims=True) + RMS_EPS)
        outs.append(yg * r)
    y_ref[:, 0:D_SSD] = (jnp.concatenate(outs, axis=1) * nw_ref[0:1, :]).astype(y_ref.dtype)


def _ssd_backward(proj_ssd, hprev_all, ypre, pre, dy, conv_w, conv_b, dt_bias, a_log, d_skip, norm_w, E, ET, comm=None):
    L = proj_ssd.shape[0]
    nc = L // CHUNK

    def body(p_ref, halo_ref, hprev_ref, ypre_ref, pre_ref, dy_ref, cw_ref, cb_ref, dtb_ref, alog_ref, dsk_ref, nw_ref, e_ref,
             et_ref, dp_ref, acc_cw_ref, acc_w_ref, acc_s_ref, dh_scr, ext_scr, ext2_scr, nxt_scr):
        i = pl.program_id(0)
        c = nc - 1 - i
        first = c == 0

        @pl.when(i == 0)
        def _():
            dh_scr[...] = jnp.zeros_like(dh_scr)
            nxt_scr[...] = jnp.zeros_like(nxt_scr)
            acc_cw_ref[...] = jnp.zeros_like(acc_cw_ref)
            acc_w_ref[...] = jnp.zeros_like(acc_w_ref)
            acc_s_ref[...] = jnp.zeros_like(acc_s_ref)

        R = _ssd_recompute(first, p_ref, halo_ref, cw_ref, cb_ref, dtb_ref, alog_ref, e_ref, ext_scr, pre_ref[...])
        hprev = hprev_ref[...]
        xs_c, X, Xd = R["xs_c"], R["X"], R["Xd"]
        acs, acsT = R["acs"], R["acsT"]
        ET = et_ref[...]
        dsk = _head_row(dsk_ref, D_SSD, HEAD_DIM)
        Y = ypre_ref[...]

        z = p_ref[:, S_Z:S_Z + 1024]
        sz = _sigmoid(z)
        silz = z * sz
        yf = Y * silz
        dyv = dy_ref[...]
        nw = nw_ref[0:1, :]
        dyf_parts, dnw_parts = [], []
        for g in range(SSD_GROUPS):
            cols = slice(g * 512, (g + 1) * 512)
            yg = yf[:, cols]
            r = lax.rsqrt(jnp.mean(yg * yg, axis=-1, keepdims=True) + RMS_EPS)
            yn = yg * r
            dyn = dyv[:, cols] * nw[:, cols]
            dnw_parts.append(_colsum(dyv[:, cols] * yn))
            dyf_parts.append(r * (dyn - yn * jnp.mean(dyn * yn, axis=-1, keepdims=True)))
        dyf = jnp.concatenate(dyf_parts, axis=1)
        dY = dyf * silz
        dz = dyf * Y * (sz * (1.0 + z * (1.0 - sz)))

        dhn = dh_scr[...]
        dYo = dY * R["eo_e"]
        causal = R["row"] >= R["col"]
        dacs = jnp.zeros((128, 128), F32)
        dacs_t = jnp.zeros((128, 128), F32)
        dxdiag, dxd, dhprev, dBs, dCs, yoff = [], [], [], [], [], []
        for g in range(SSD_GROUPS):
            Bg = R["Bc"][:, g * 128:(g + 1) * 128]
            Cg = R["Cc"][:, g * 128:(g + 1) * 128]
            cols = slice(g * 512, (g + 1) * 512)
            CB = _mm_nt(Cg, Bg)
            dCB = jnp.zeros((128, 128), F32)
            for j in range(4):
                h0 = g * 8 + 2 * j
                pc = slice(h0 * HEAD_DIM, h0 * HEAD_DIM + 128)
                dYst = _split_halves(dY[:, pc])
                dMst = _mm_nt(dYst, X[:, pc])
                mts = []
                for a, h in enumerate((h0, h0 + 1)):
                    acol = acs[:, h:h + 1]
                    arow = acsT[h:h + 1, :]
                    Lm = jnp.exp(jnp.where(causal, acol - arow, NEG))
                    M = CB * Lm
                    dM = dMst[a * 128:(a + 1) * 128]
                    dCB = dCB + dM * Lm
                    G = dM * M
                    dacs = dacs + jnp.where(R["col"] == h, jnp.sum(G, axis=1, keepdims=True), 0.0)
                    dacs_t = dacs_t + jnp.where(R["row"] == h, jnp.sum(G, axis=0, keepdims=True), 0.0)
                    mts.append(M.T)
                dxdiag.append(_mm(jnp.concatenate(mts, axis=1), dYst))
            dS = dhn[:, cols]
            dxd.append(_mm(Bg, dS))
            yoff.append(_mm(Cg, hprev[:, cols]))
            dhprev.append(_mm_tn(Cg, dYo[:, cols]))
            dCs.append(_mm_nt(dYo[:, cols], hprev[:, cols]) + _mm(dCB, Bg))
            dBs.append(_mm_tn(dCB, Cg) + _mm_nt(Xd[:, cols], dS))
        Yoff = jnp.concatenate(yoff, axis=1) * R["eo_e"]
        dXd = jnp.concatenate(dxd, axis=1)
        dX = jnp.concatenate(dxdiag, axis=1) + dXd * R["ds_e"]
        t_state = dXd * Xd
        dacs = dacs + _mm_2pass_r(dY * Yoff - t_state, ET) - dacs_t.T
        v_last = _colsum(t_state + dhn * hprev * R["cd_e"])
        dlast = _mm_exact_r(jnp.broadcast_to(v_last, (8, 1024)), ET)[0:1, :]
        dacs = dacs + jnp.where(R["row"] == 127, dlast, 0.0)
        triu = (R["col"] >= R["row"]).astype(BF16)
        da = _mm_exact_l(triu, dacs)
        ddt = da * R["A"] + _mm(dX * xs_c, ET)
        ddt_raw = ddt * _sigmoid(R["raw"])
        dxs_c = dX * R["dt_e"] + dY * dsk
        dh_scr[...] = jnp.concatenate(dhprev, axis=1) + dhn * R["cd_e"]

        dact = jnp.concatenate([dxs_c] + dBs + dCs, axis=1)
        pre, sg = R["pre"], R["sg"]
        dpre = dact * (sg * (1.0 + pre * (1.0 - sg)))
        ext2_scr[0:8, :] = dpre[120:128, :]
        ext2_scr[8:16, :] = nxt_scr[...]
        nxt_scr[...] = dpre[0:8, :]
        cw = cw_ref[...]
        u_b, dpre_b = p_ref[:, S_XS:S_DT].astype(_MXU), dpre.astype(_MXU)
        dxbc = cw[3:4, :] * dpre
        taps = [_colsum(dpre * p_ref[:, S_XS:S_DT])]
        for s in (1, 2, 3):
            up = (R["col"] - R["row"] == s).astype(_MXU)
            d_s = jnp.concatenate([jnp.dot(up, dpre_b, preferred_element_type=F32)[0:120],
                                   ext2_scr[s:8 + s, :]], axis=0)
            dxbc = dxbc + cw[3 - s:4 - s, :] * d_s
            taps.append(_colsum(dpre * _rows_from_above(u_b, s, ext_scr, R["row"], R["col"])))
        acc_cw_ref[...] += _rows8(taps[::-1] + [_colsum(dpre)])
        acc_w_ref[...] += _rows8([jnp.concatenate(dnw_parts, axis=1), _colsum(dY * xs_c)])
        acc_s_ref[...] += _rows8([_colsum(ddt_raw), _colsum(da * R["dt"])])

        lane = lax.broadcasted_iota(jnp.int32, (128, 128), 1)
        dp_ref[:, S_Z:S_Z + 1024] = dz.astype(dp_ref.dtype)
        dp_ref[:, S_XS:S_DT] = dxbc.astype(dp_ref.dtype)
        dp_ref[:, S_DT:S_DT + 128] = jnp.where(lane < N_HEADS, ddt_raw, 0.0).astype(dp_ref.dtype)
        dp_ref[:, S_DT + 128:S_W] = jnp.zeros((128, 128), dp_ref.dtype)

        @pl.when(i == nc - 1)
        def _():
            acc = acc_s_ref[...]
            dskip = _mm_exact_r(acc_w_ref[...], ET)[1:2, :]
            acc_s_ref[...] = _rows8([acc[0:1, :], acc[1:2, :] * R["A"], dskip])

    const = lambda shape: pl.BlockSpec(shape, lambda i: (0, 0))
    smem = pl.BlockSpec(memory_space=pltpu.SMEM)
    rev = lambda i: (nc - 1 - i, 0)
    return _call(
        body, comm, name="ssd_bwd", grid=(nc,),
        in_specs=[pl.BlockSpec((CHUNK, S_W), rev),
                  pl.BlockSpec((8, S_W), lambda i: (jnp.maximum((nc - 1 - i) * 16 - 1, 0), 0)),
                  pl.BlockSpec((128, 1024), rev),
                  pl.BlockSpec((CHUNK, D_SSD), rev),
                  pl.BlockSpec((CHUNK, D_XBC), rev),
                  pl.BlockSpec((CHUNK, D_SSD), rev),
                  const((4, D_XBC)), const((1, D_XBC)), smem, smem, smem, const((1, 1024)),
                  const((128, 1024)), const((1024, 128))],
        out_specs=[pl.BlockSpec((CHUNK, S_W), rev), const((8, D_XBC)), const((8, 1024)), const((8, 128))],
        out_shape=[jax.ShapeDtypeStruct((L, S_W), _MXU), jax.ShapeDtypeStruct((8, D_XBC), F32),
                   jax.ShapeDtypeStruct((8, 1024), F32), jax.ShapeDtypeStruct((8, 128), F32)],
        scratch_shapes=[pltpu.VMEM((128, 1024), F32), pltpu.VMEM((16, D_XBC), F32),
                        pltpu.VMEM((16, D_XBC), F32), pltpu.VMEM((8, D_XBC), F32)],
        args=(proj_ssd, proj_ssd, hprev_all, ypre, pre, dy, conv_w, conv_b, dt_bias, a_log, d_skip, norm_w, E, ET))


def _rope(t, tab):
    cos, sa, sb = tab[:, 0:128], tab[:, 128:256], tab[:, 256:384]
    outs = []
    for i in range(t.shape[1] // 128):
        tg = t[:, i * 128:(i + 1) * 128]
        outs.append(tg * cos + pltpu.roll(tg, 8, 1) * sa + pltpu.roll(tg, 120, 1) * sb)
    return jnp.concatenate(outs, axis=1)


def _rope_transposed(d, tab):
    cos, sa, sb = tab[:, 0:128], tab[:, 128:256], tab[:, 256:384]
    outs = []
    for i in range(d.shape[1] // 128):
        dg = d[:, i * 128:(i + 1) * 128]
        outs.append(dg * cos + pltpu.roll(dg * sa, 120, 1) + pltpu.roll(dg * sb, 8, 1))
    return jnp.concatenate(outs, axis=1)


def _lo_half(rows):
    return lax.broadcasted_iota(jnp.int32, (rows, 128), 1) < HEAD_DIM


def _native_half(rows, j):
    lo = _lo_half(rows)
    return lo if j % 2 == 0 else jnp.logical_not(lo)


def _kv_native(t, j):
    p = j // 2
    return jnp.where(_native_half(t.shape[0], j), t[:, p * 128:(p + 1) * 128], 0.0)


def _stack_heads(t, j):
    out = []
    for m in (2 * j, 2 * j + 1):
        pair = t[:, m * 128:(m + 1) * 128]
        swapped = pltpu.roll(pair, HEAD_DIM, 1)
        out += [pair, swapped] if j % 2 == 0 else [swapped, pair]
    return jnp.concatenate(out, axis=0)


def _unstack_heads(s, j):
    out = []
    for m in range(2):
        first, second = s[256 * m:256 * m + 128], s[256 * m + 128:256 * m + 256]
        if j % 2 == 0:
            out.append(first + pltpu.roll(second, HEAD_DIM, 1))
        else:
            out.append(pltpu.roll(first, HEAD_DIM, 1) + second)
    return jnp.concatenate(out, axis=1)


def _keep_native(r, j):
    return jnp.where(_native_half(r.shape[0], j), r, 0.0)


def _sink_row(sink_ref, j):
    hid = lax.broadcasted_iota(jnp.int32, (1, 4 * CHUNK), 1) // CHUNK
    row = jnp.zeros((1, 4 * CHUNK), F32)
    for hh in range(4):
        row = jnp.where(hid == hh, sink_ref[4 * j + hh], row)
    return row


def _from_current():
    si = lax.broadcasted_iota(jnp.int32, (CHUNK, 4 * CHUNK), 0)
    qi = lax.broadcasted_iota(jnp.int32, (CHUNK, 4 * CHUNK), 1) % CHUNK
    return si <= qi


def _fold(full, from_cur, pen=0.0):
    return jnp.where(from_cur, full[CHUNK:2 * CHUNK], full[0:CHUNK] + pen)


def _unfold(t, from_cur):
    c = jnp.where(from_cur, t, 0.0)
    return jnp.concatenate([t - c, c], axis=0)


def _softmax_sink(s, sink):
    mx = jnp.maximum(jnp.max(s, axis=0, keepdims=True), sink)
    p = jnp.exp(s - mx)
    esink = jnp.exp(sink - mx)
    inv = 1.0 / (jnp.sum(p, axis=0, keepdims=True) + esink)
    return p * inv, esink * inv


def _swa_inputs(blk, p_ref, prev_ref, tab_ref, ptab_ref):
    tab = tab_ref[...]
    qr = _rope(p_ref[:, A_Q:A_Q + 1024], tab) * ATT_SCALE
    kk = jnp.concatenate([_rope(prev_ref[:, 0:256], ptab_ref[...]), _rope(p_ref[:, A_K:A_K + 256], tab)], axis=0)
    vv = jnp.concatenate([prev_ref[:, 256:512], p_ref[:, A_V:A_V + 256]], axis=0)
    return tab, qr, kk, vv, jnp.where(blk > 0, 0.0, NEG)


def _swa_forward_step(sink_ref, p_ref, prev_ref, tab_ref, ptab_ref, y_ref):
    n = pl.program_id(0)
    _, qr, kk, vv, pen = _swa_inputs(n, p_ref, prev_ref, tab_ref, ptab_ref)
    from_cur = _from_current()
    outs = []
    for j in range(KV_HEADS):
        s = _fold(_mm_nt(_kv_native(kk, j), _stack_heads(qr, j)), from_cur, pen)
        P, _ = _softmax_sink(s, _sink_row(sink_ref, j))
        outs.append(_unstack_heads(_mm_tn(_unfold(P, from_cur), _kv_native(vv, j)), j))
    g = p_ref[:, A_G:A_G + 1024]
    y_ref[:, D_SSD:D_SSD + D_ATT] = (jnp.concatenate(outs, axis=1) * (g * _sigmoid(g))).astype(y_ref.dtype)


def _mixer_forward(proj_ssd, proj_att, tabs, sinks, conv_w, conv_b, dt_bias, a_log, d_skip, norm_w, E, comm=None):
    L = proj_ssd.shape[0]
    nc = L // CHUNK

    def body(p_ref, halo_ref, cw_ref, cb_ref, dtb_ref, alog_ref, dsk_ref, nw_ref, e_ref,
             sink_ref, pa_ref, prev_ref, tab_ref, ptab_ref, y_ref, ypre_ref, hprev_ref, pre_ref, h_scr, ext_scr):
        _ssd_forward_step(p_ref, halo_ref, cw_ref, cb_ref, dtb_ref, alog_ref, dsk_ref, nw_ref, e_ref,
                          y_ref, ypre_ref, hprev_ref, pre_ref, h_scr, ext_scr)
        _swa_forward_step(sink_ref, pa_ref, prev_ref, tab_ref, ptab_ref, y_ref)

    const = lambda shape: pl.BlockSpec(shape, lambda c: (0, 0))
    smem = pl.BlockSpec(memory_space=pltpu.SMEM)
    rows = lambda w: pl.BlockSpec((CHUNK, w), lambda c: (c, 0))
    return _call(
        body, comm, name="mixer_fwd", grid=(nc,),
        in_specs=[rows(S_W), pl.BlockSpec((8, S_W), lambda c: (jnp.maximum(c * 16 - 1, 0), 0)),
                  const((4, D_XBC)), const((1, D_XBC)), smem, smem, smem, const((1, 1024)), const((128, 1024)),
                  smem, rows(A_W), pl.BlockSpec((CHUNK, 512), lambda c: (jnp.maximum(c - 1, 0), 2)),
                  rows(384), pl.BlockSpec((CHUNK, 384), lambda c: (jnp.maximum(c - 1, 0), 0))],
        out_specs=[rows(D_SSD + D_ATT), rows(D_SSD), pl.BlockSpec((128, 1024), lambda c: (c, 0)), rows(D_XBC)],
        out_shape=[jax.ShapeDtypeStruct((L, D_SSD + D_ATT), _MXU), jax.ShapeDtypeStruct((L, D_SSD), F32),
                   jax.ShapeDtypeStruct((nc * 128, 1024), F32), jax.ShapeDtypeStruct((L, D_XBC), F32)],
        scratch_shapes=[pltpu.VMEM((128, 1024), F32), pltpu.VMEM((136, D_XBC), F32)],
        args=(proj_ssd, proj_ssd, conv_w, conv_b, dt_bias, a_log, d_skip, norm_w, E,
              sinks, proj_att, proj_att, tabs, tabs))


def _swa_backward(proj_att, tabs, sinks, dy, comm=None):
    L = proj_att.shape[0]
    nb = L // CHUNK

    def body(sink_ref, p_ref, prev_ref, tab_ref, ptab_ref, dy_ref, dp_ref, dsink_ref, carry_k, carry_v):
        i = pl.program_id(0)
        n = nb - 1 - i

        @pl.when(i == 0)
        def _():
            carry_k[...] = jnp.zeros_like(carry_k)
            carry_v[...] = jnp.zeros_like(carry_v)
            dsink_ref[...] = jnp.zeros_like(dsink_ref)

        tab, qr, kk, vv, pen = _swa_inputs(n, p_ref, prev_ref, tab_ref, ptab_ref)
        from_cur = _from_current()
        g = p_ref[:, A_G:A_G + 1024]
        sgm = _sigmoid(g)
        dyv = dy_ref[...]
        do_all = dyv * (g * sgm)
        lane8 = lax.broadcasted_iota(jnp.int32, (8, 128), 1)
        hid = lax.broadcasted_iota(jnp.int32, (1, 4 * CHUNK), 1) // CHUNK
        o_parts, dq_parts = [], []
        dk_nat = [jnp.zeros((2 * CHUNK, 128), F32) for _ in range(2)]
        dv_nat = [jnp.zeros((2 * CHUNK, 128), F32) for _ in range(2)]
        dsink = jnp.zeros((8, 128), F32)
        for j in range(KV_HEADS):
            qs = _stack_heads(qr, j)
            kkb, vvb = _kv_native(kk, j), _kv_native(vv, j)
            P, psink = _softmax_sink(_fold(_mm_nt(kkb, qs), from_cur, pen), _sink_row(sink_ref, j))
            p_full = _unfold(P, from_cur)
            o_parts.append(_unstack_heads(_mm_tn(p_full, vvb), j))
            do_s = _stack_heads(do_all, j)
            dP = _fold(_mm_nt(vvb, do_s), from_cur)
            D = jnp.sum(P * dP, axis=0, keepdims=True)
            ds_full = _unfold(P * (dP - D), from_cur)
            sd = psink * D
            for hh in range(4):
                dsink = dsink + jnp.where(lane8 == 4 * j + hh, -jnp.sum(jnp.where(hid == hh, sd, 0.0)), 0.0)
            dq_parts.append(_unstack_heads(_mm_tn(ds_full, kkb), j) * ATT_SCALE)
            dk_nat[j // 2] = dk_nat[j // 2] + _keep_native(_mm(ds_full, qs), j)
            dv_nat[j // 2] = dv_nat[j // 2] + _keep_native(_mm(p_full, do_s), j)
        o = jnp.concatenate(o_parts, axis=1)
        dkk = jnp.concatenate(dk_nat, axis=1)
        dvv = jnp.concatenate(dv_nat, axis=1)
        out = dp_ref.dtype
        dp_ref[:, A_Q:A_Q + 1024] = _rope_transposed(jnp.concatenate(dq_parts, axis=1), tab).astype(out)
        dp_ref[:, A_K:A_K + 256] = _rope_transposed(dkk[CHUNK:2 * CHUNK] + carry_k[...], tab).astype(out)
        dp_ref[:, A_V:A_V + 256] = (dvv[CHUNK:2 * CHUNK] + carry_v[...]).astype(out)
        dp_ref[:, A_G:A_G + 1024] = (dyv * o * (sgm * (1.0 + g * (1.0 - sgm)))).astype(out)
        carry_k[...] = dkk[0:CHUNK]
        carry_v[...] = dvv[0:CHUNK]
        dsink_ref[...] += dsink

    rev = lambda i: (nb - 1 - i, 0)
    prev = lambda i: jnp.maximum(nb - 2 - i, 0)
    return _call(
        body, comm, name="swa_bwd", grid=(nb,),
        in_specs=[pl.BlockSpec(memory_space=pltpu.SMEM),
                  pl.BlockSpec((CHUNK, A_W), rev),
                  pl.BlockSpec((CHUNK, 512), lambda i: (prev(i), 2)),
                  pl.BlockSpec((CHUNK, 384), rev),
                  pl.BlockSpec((CHUNK, 384), lambda i: (prev(i), 0)),
                  pl.BlockSpec((CHUNK, D_ATT), lambda i: (nb - 1 - i, 1))],
        out_specs=[pl.BlockSpec((CHUNK, A_W), rev), pl.BlockSpec((8, 128), lambda i: (0, 0))],
        out_shape=[jax.ShapeDtypeStruct((L, A_W), _MXU), jax.ShapeDtypeStruct((8, 128), F32)],
        scratch_shapes=[pltpu.VMEM((CHUNK, 256), F32), pltpu.VMEM((CHUNK, 256), F32)],
        args=(sinks, proj_att, proj_att, tabs, tabs, dy))


def _head(y, x, target, w_out, ln_g, ln_b, *, tm):
    L = x.shape[0]
    nsteps = L // tm

    def body(y_ref, x_ref, t_ref, wo_ref, g_ref, b_ref, dr_ref, dy_ref, acc_ref):
        i = pl.program_id(0)

        @pl.when(i == 0)
        def _():
            acc_ref[...] = jnp.zeros_like(acc_ref)

        r = ALPHA * x_ref[...] + _mm(y_ref[...], wo_ref[...])
        mu = jnp.mean(r, axis=-1, keepdims=True)
        d = r - mu
        rstd = lax.rsqrt(jnp.mean(d * d, axis=-1, keepdims=True) + LN_EPS)
        xh = d * rstd
        gam = g_ref[0:1, :]
        e = xh * gam + b_ref[0:1, :] - t_ref[...]
        dout = e * (1.0 / D_MODEL)
        dxh = dout * gam
        dr = rstd * (dxh - jnp.mean(dxh, axis=-1, keepdims=True)
                     - xh * jnp.mean(dxh * xh, axis=-1, keepdims=True))
        dr_ref[...] = dr
        dy_ref[...] = _mm_nt(dr, wo_ref[...])
        acc_ref[...] += _rows8([_colsum(dout * xh), _colsum(dout), _colsum(e * e) * (0.5 / D_MODEL)])

        @pl.when(i == nsteps - 1)
        def _():
            acc = acc_ref[...]
            tot = jnp.sum(acc[2:3, :])
            rid = lax.broadcasted_iota(jnp.int32, (8, 1024), 0)
            acc_ref[...] = jnp.where(rid == 3, tot, acc)

    const = lambda shape: pl.BlockSpec(shape, lambda i: (0, 0))
    row = lambda w: pl.BlockSpec((tm, w), lambda i: (i, 0))
    return pl.pallas_call(
        body, name="head", grid=(nsteps,),
        in_specs=[row(2048), row(1024), row(1024), const((2048, 1024)), const((1, 1024)), const((1, 1024))],
        out_specs=[row(1024), row(2048), const((8, 1024))],
        out_shape=[jax.ShapeDtypeStruct((L, D_MODEL), F32), jax.ShapeDtypeStruct((L, 2048), F32),
                   jax.ShapeDtypeStruct((8, 1024), F32)],
        compiler_params=_params(("arbitrary",)),
    )(y, x, target, w_out, ln_g, ln_b)


def _gather_w_in(w_shard):
    R = w_shard.shape[0]
    halves = (pl.ds(0, R // 2), pl.ds(R // 2, R // 2))
    any_spec = pl.BlockSpec(memory_space=pl.ANY)

    def body(in_ref, out_ref, send_sems, recv_sems, local_sem):
        x, y, c = _position()

        def slot(p, half=None):
            s = out_ref.at[_index(*p)]
            return s if half is None else s.at[halves[half]]

        def same_core(p):
            return (p[0], p[1], c)

        def other_core(p):
            return (p[0], p[1], 1 - c)

        me, xn, yn, dg = (x, y), (1 - x, y), (x, 1 - y), (1 - x, 1 - y)

        def copy(k, dst, to, src=None):
            return _remote(dst if src is None else src, dst, send_sems.at[k], recv_sems.at[k], to)

        local = pltpu.make_async_copy(in_ref, slot(same_core(me)), local_sem)
        local.start()
        own = [copy(0, slot(same_core(me)), other_core(me), in_ref), copy(1, slot(same_core(me)), same_core(xn), in_ref),
               copy(2, slot(same_core(me)), same_core(yn), in_ref)]
        for cp in own:
            cp.start()
        copy(1, slot(same_core(xn)), same_core(xn)).wait_recv()
        passed = [copy(4, slot(same_core(xn), 1), same_core(yn)), copy(5, slot(same_core(xn)), other_core(me))]
        for cp in passed:
            cp.start()
        copy(2, slot(same_core(yn)), same_core(yn)).wait_recv()
        more = [copy(3, slot(same_core(yn), 0), same_core(xn)), copy(6, slot(same_core(yn)), other_core(me))]
        for cp in more:
            cp.start()
        passed += more
        for k, half in ((3, 0), (4, 1)):
            copy(k, slot(same_core(dg), half), same_core(xn)).wait_recv()
            fwd = copy(7 + half, slot(same_core(dg), half), other_core(me))
            fwd.start()
            passed.append(fwd)
        copy(0, slot(other_core(me)), other_core(me)).wait_recv()
        copy(5, slot(other_core(xn)), other_core(me)).wait_recv()
        copy(6, slot(other_core(yn)), other_core(me)).wait_recv()
        for half in (0, 1):
            copy(7 + half, slot(other_core(dg), half), other_core(me)).wait_recv()
        for cp in own + passed:
            cp.wait_send()
        local.wait()

    return pl.pallas_call(
        body, name="gather_w_in", in_specs=[any_spec], out_specs=any_spec,
        out_shape=jax.ShapeDtypeStruct((N_DEV,) + w_shard.shape, w_shard.dtype),
        scratch_shapes=[pltpu.SemaphoreType.DMA((9,)), pltpu.SemaphoreType.DMA((9,)), pltpu.SemaphoreType.DMA],
    )(w_shard)


def _input_gradient(d_ssd, d_att, w_ssd, w_att, dr, *, tm, comm=None, stack_hi=None, sum_step=4):
    L = dr.shape[0]
    steps = L // tm

    def matmuls(ds_ref, da_ref, ws_ref, wa_ref, dr_ref, o_ref):
        o_ref[...] = ALPHA * dr_ref[...] + _mm_nt(ds_ref[...], ws_ref[...]) + _mm_nt(da_ref[...], wa_ref[...])

    row = lambda w: pl.BlockSpec((tm, w), lambda i: (i, 0))
    resident = lambda a: pl.BlockSpec(a.shape, lambda i: (0, 0), pipeline_mode=pl.Buffered(1))
    in_specs = [row(S_W), row(A_W), resident(w_ssd), resident(w_att), row(D_MODEL)]
    out_dx = jax.ShapeDtypeStruct((L, D_MODEL), F32)
    args = (d_ssd, d_att, w_ssd, w_att, dr)
    if stack_hi is None:
        return _call(matmuls, comm, name="dx", grid=(steps,), in_specs=in_specs, out_specs=[row(D_MODEL)],
                     out_shape=[out_dx], scratch_shapes=[], args=args)

    block = stack_hi.shape[1:]
    c_in, c_out, rows = len(comm.operands), len(comm.out_shapes), len(comm.flows)
    to_owners = _Hosted(None, None, [_Flow("chip_exchange", 0, 0, target_x=1)])

    def body(ds_ref, da_ref, ws_ref, wa_ref, dr_ref, stack_ref, *refs):
        cins, refs = refs[:c_in], refs[c_in:]
        (o_ref, own_ref), refs = refs[:2], refs[2:]
        couts, refs = refs[:c_out], refs[c_out:]
        (theirs_scr, mine_scr, sums_scr, send_sems, recv_sems, local_sems, sum_send_sems, sum_recv_sems,
         swap_send_sems, swap_recv_sems, mine_sems) = refs
        i = pl.program_id(0)
        x, y, c = _position()
        start, wait = comm.plan(cins, couts, send_sems, recv_sems, local_sems)
        start_sums, wait_sums = to_owners.plan([sums_scr], [couts[0]], sum_send_sems, sum_recv_sems, None)
        swaps = [_remote(stack_ref.at[2 * oy + (1 - c)], theirs_scr.at[oy], swap_send_sems.at[oy], swap_recv_sems.at[oy],
                         (x, y, 1 - c)) for oy in range(2)]
        mine = [pltpu.make_async_copy(stack_ref.at[2 * oy + c], mine_scr.at[oy], mine_sems.at[oy]) for oy in range(2)]

        @pl.when(i == 0)
        def _():
            for cp in swaps + mine:
                cp.start()
            start()

        matmuls(ds_ref, da_ref, ws_ref, wa_ref, dr_ref, o_ref)

        @pl.when(i == sum_step)
        def _():
            for oy in range(2):
                swaps[oy].wait_recv()
                mine[oy].wait()
                t = mine_scr[oy] + theirs_scr[oy]
                sums_scr[oy] = t.astype(sums_scr.dtype)

                @pl.when(y == oy)
                def _(t=t):
                    own_ref[...] = t
            start_sums()

        @pl.when(i == steps - 1)
        def _():
            wait()
            wait_sums()
            for cp in swaps:
                cp.wait_send()

    io_alias = {6 + ci: 2 + co for ci, co in comm.aliases.items()}
    any_spec = pl.BlockSpec(memory_space=pl.ANY)
    dma = pltpu.SemaphoreType.DMA
    return pl.pallas_call(
        body, name="dx", grid=(steps,), in_specs=in_specs + [any_spec] * (1 + c_in),
        out_specs=[row(D_MODEL), pl.BlockSpec(block, lambda i: (0, 0), pipeline_mode=pl.Buffered(1))] + [any_spec] * c_out,
        out_shape=[out_dx, jax.ShapeDtypeStruct(block, F32)] + list(comm.out_shapes),
        scratch_shapes=[pltpu.VMEM((2,) + block, F32), pltpu.VMEM((2,) + block, F32), pltpu.VMEM((2,) + block, BF16),
                        dma((rows, N_DEV - 1)), dma((rows, N_DEV - 1)), dma((rows,)), dma((1, N_DEV - 1)),
                        dma((1, N_DEV - 1)), dma((2,)), dma((2,)), dma((2,))],
        input_output_aliases=io_alias, compiler_params=_params(("arbitrary",)),
    )(*args, stack_hi, *comm.operands)


SHARD_COLS = D_IN_PROJ // N_DEV
SPLIT = N_SSD_REAL - 4 * SHARD_COLS
RELAYOUT_ROWS = 256


def _unpack_w_in(w_all):
    def body(g_ref, ws_ref, wa_ref):
        for j in range(4):
            ws_ref[:, SHARD_COLS * j:SHARD_COLS * (j + 1)] = g_ref[j]
        ws_ref[:, 4 * SHARD_COLS:N_SSD_REAL] = g_ref[4, :, 0:SPLIT]
        ws_ref[:, N_SSD_REAL:S_W] = jnp.zeros((RELAYOUT_ROWS, S_W - N_SSD_REAL), ws_ref.dtype)
        wa_ref[:, 0:SHARD_COLS - SPLIT] = g_ref[4, :, SPLIT:SHARD_COLS]
        for j in range(5, N_DEV):
            lo = SHARD_COLS * (j - 4) - SPLIT
            wa_ref[:, lo:lo + SHARD_COLS] = g_ref[j]

    return pl.pallas_call(
        body, name="unpack_w_in", grid=(D_MODEL // RELAYOUT_ROWS,),
        in_specs=[pl.BlockSpec((N_DEV, RELAYOUT_ROWS, SHARD_COLS), lambda i: (0, i, 0))],
        out_specs=[pl.BlockSpec((RELAYOUT_ROWS, S_W), lambda i: (i, 0)), pl.BlockSpec((RELAYOUT_ROWS, A_W), lambda i: (i, 0))],
        out_shape=[jax.ShapeDtypeStruct((D_MODEL, S_W), w_all.dtype), jax.ShapeDtypeStruct((D_MODEL, A_W), w_all.dtype)],
        compiler_params=_params(("arbitrary",)),
    )(w_all)


def _dw_in(me1, xb, d, half, tail=None, *, tl=1024):
    L, N = d.shape
    steps = L // tl

    def body(me_ref, x_ref, d_ref, *refs):
        if half == 0:
            p_ref, own_ref, tail_ref, acc = refs
        else:
            t_ref, p_ref, acc = refs
        l = pl.program_id(0)

        @pl.when(l == 0)
        def _():
            acc[...] = jnp.zeros_like(acc)

        acc[...] += _mm_tn(x_ref[...], d_ref[...])

        @pl.when(l == steps - 1)
        def _():
            if half == 0:
                me = me_ref[0]

                @pl.when(me >= 4)
                def _():
                    own_ref[...] = jnp.zeros_like(own_ref)

                tail_ref[...] = acc[:, S_DT:S_W]
            for j in range(4):
                if half == 0:
                    pieces = [(0, acc[:, SHARD_COLS * j:SHARD_COLS * (j + 1)])]
                elif j == 0:
                    pieces = [(0, t_ref[:, 4 * SHARD_COLS - S_DT:N_SSD_REAL - S_DT]), (SPLIT, acc[:, 0:SHARD_COLS - SPLIT])]
                else:
                    lo = SHARD_COLS * j - SPLIT
                    pieces = [(0, acc[:, lo:lo + SHARD_COLS])]
                for off, blk in pieces:
                    p_ref[j, :, off:off + blk.shape[1]] = blk.astype(p_ref.dtype)
                    if half == 0:
                        @pl.when(me == j)
                        def _(off=off, blk=blk):
                            own_ref[:, off:off + blk.shape[1]] = blk

    once = pl.Buffered(1)
    whole = lambda shape: pl.BlockSpec(shape, lambda l: (0,) * len(shape), pipeline_mode=once)
    in_specs = [pl.BlockSpec(memory_space=pltpu.SMEM), pl.BlockSpec((tl, D_MODEL), lambda l: (l, 0)),
                pl.BlockSpec((tl, N), lambda l: (l, 0))]
    args = [me1, xb, d]
    if half == 0:
        out_shape = [jax.ShapeDtypeStruct((4, D_MODEL, SHARD_COLS), BF16), jax.ShapeDtypeStruct((D_MODEL, SHARD_COLS), F32),
                     jax.ShapeDtypeStruct((D_MODEL, S_W - S_DT), F32)]
    else:
        in_specs.append(whole(tail.shape))
        args.append(tail)
        out_shape = [jax.ShapeDtypeStruct((4, D_MODEL, SHARD_COLS), F32)]
    return pl.pallas_call(
        body, name="dw_in_%d" % half, grid=(steps,), in_specs=in_specs,
        out_specs=[whole(o.shape) for o in out_shape], out_shape=out_shape,
        scratch_shapes=[pltpu.VMEM((D_MODEL, N), F32)], compiler_params=_params(("arbitrary",)),
    )(*args)


def _adamw_math(w, g, m, v):
    m = ADAM_B1 * m + (1.0 - ADAM_B1) * g
    v = ADAM_B2 * v + (1.0 - ADAM_B2) * (g * g)
    m_hat = m / (1.0 - ADAM_B1 ** ADAM_STEP)
    v_hat = v / (1.0 - ADAM_B2 ** ADAM_STEP)
    delta = -ADAM_LR * (m_hat / (jnp.sqrt(v_hat) + ADAM_EPS) + ADAM_WD * w)
    return delta, m, v


def _adamw_shard(n_recv, g_own, recv, w, m, v, *, rows, name):
    R, C = g_own.shape

    def body(n_ref, g_ref, r_ref, w_ref, m_ref, v_ref, go_ref, d_ref, mo_ref, vo_ref):
        g = g_ref[...]
        for k in range(N_DEV - 1):
            g = g + jnp.where(k < n_ref[0], r_ref[k].astype(F32), 0.0)
        d, mn, vn = _adamw_math(w_ref[...], g, m_ref[...], v_ref[...])
        go_ref[...] = g
        d_ref[...] = d
        mo_ref[...] = mn
        vo_ref[...] = vn

    blk = pl.BlockSpec((rows, C), lambda i: (i, 0))
    return pl.pallas_call(
        body, name=name, grid=(R // rows,),
        in_specs=[pl.BlockSpec(memory_space=pltpu.SMEM), blk,
                  pl.BlockSpec((N_DEV - 1, rows, C), lambda i: (0, i, 0)), blk, blk, blk],
        out_specs=[blk] * 4, out_shape=[jax.ShapeDtypeStruct((R, C), F32)] * 4,
        compiler_params=_params(("arbitrary",)),
    )(n_recv, g_own, recv, w, m, v)


def _minor_rows_view(a):
    return jnp.transpose(a, (2, 0, 1)).reshape(SHARD_COLS * 8, 128)


def _from_minor_rows_view(v):
    return jnp.transpose(v.reshape(SHARD_COLS, 8, 128), (1, 2, 0)).reshape(1, D_MODEL, SHARD_COLS)


def _adamw_w_in(n_recv, g_lo, g_hi, recv, w, m, v):
    C = SHARD_COLS
    pad = -C % 128

    def body(n_ref, lo_ref, hi_ref, r_ref, w_ref, m_ref, v_ref, go_ref, d_ref, mo_ref, vo_ref):
        for q in range(D_MODEL // 128):
            band = pl.ds(q * 128, 128)
            g = jnp.where(n_ref[0] == N_DEV - 1, lo_ref[band, :], hi_ref[band, :])
            for k in range(N_DEV - 1):
                g = g + jnp.where(k < n_ref[0], r_ref[k, band, :].astype(F32), 0.0)
            g = jnp.pad(g, ((0, 0), (0, pad))).T[0:C]
            rows = pl.ds(q, C, stride=8)
            d, mn, vn = _adamw_math(w_ref[rows, :], g, m_ref[rows, :], v_ref[rows, :])
            go_ref[rows, :] = g
            d_ref[rows, :] = d
            mo_ref[rows, :] = mn
            vo_ref[rows, :] = vn

    return pl.pallas_call(
        body, name="adamw_w_in", out_shape=[jax.ShapeDtypeStruct(w.shape, F32)] * 4,
        in_specs=[pl.BlockSpec(memory_space=pltpu.SMEM)] + [pl.BlockSpec(memory_space=pltpu.VMEM)] * 6,
        out_specs=[pl.BlockSpec(memory_space=pltpu.VMEM)] * 4,
        compiler_params=_params(),
    )(n_recv, g_lo, g_hi, recv, w, m, v)


SMALL = ("conv_b", "dt_bias", "a_log", "d_skip", "ssd_norm_w", "attn_sinks", "ln_g", "ln_b")


def _adamw_small(gathered, params):
    n_p = len(SMALL)

    def body(*refs):
        acc = []
        for r in refs[:5]:
            t = r[0]
            for k in range(1, N_DEV):
                t = t + r[k]
            acc.append(t)
        head, conv, norm, scal, sink = acc
        grads = dict(conv_b=conv[4:5, :], dt_bias=scal[0:1, 0:N_HEADS], a_log=scal[1:2, 0:N_HEADS],
                     d_skip=scal[2:3, 0:N_HEADS], ssd_norm_w=norm[0:1, :], attn_sinks=sink[0:1, 0:N_HEADS],
                     ln_g=head[0:1, :], ln_b=head[1:2, :])
        wmv = refs[5:5 + 3 * n_p]
        outs = refs[5 + 3 * n_p:]
        outs[0][...] = head[3:4, 0:1]
        outs[1][...] = conv[0:4, :]
        for i, name in enumerate(SMALL):
            w_ref, m_ref, v_ref = wmv[3 * i:3 * i + 3]
            g = grads[name]
            d, mn, vn = _adamw_math(w_ref[...], g, m_ref[...], v_ref[...])
            for o_ref, val in zip(outs[2 + 4 * i:6 + 4 * i], (g, d, mn, vn)):
                o_ref[...] = val

    flat = [a for name in SMALL for a in params[name]]
    out_shape = [jax.ShapeDtypeStruct((1, 1), F32), jax.ShapeDtypeStruct((4, D_XBC), F32)]
    for name in SMALL:
        out_shape += [jax.ShapeDtypeStruct(params[name][0].shape, F32)] * 4
    res = pl.pallas_call(body, name="adamw_small", out_shape=out_shape, compiler_params=_params())(*gathered, *flat)
    return res[0], res[1], {name: res[2 + 4 * i:6 + 4 * i] for i, name in enumerate(SMALL)}


def _adamw_plain(g, w, m, v):
    def body(g_ref, w_ref, m_ref, v_ref, d_ref, mo_ref, vo_ref):
        d, mn, vn = _adamw_math(w_ref[...], g_ref[...], m_ref[...], v_ref[...])
        d_ref[...] = d
        mo_ref[...] = mn
        vo_ref[...] = vn

    return pl.pallas_call(
        body, name="adamw_conv_w", out_shape=[jax.ShapeDtypeStruct(w.shape, F32)] * 3,
        compiler_params=_params(),
    )(g, w, m, v)


def _lane_pattern(fn):
    return np.asarray([fn(l % HEAD_DIM) for l in range(128)], np.float32)


ROPE_INV = _lane_pattern(lambda r: ROPE_THETA ** (-2.0 * (r % 8) / ROPE_DIM) if r < ROPE_DIM else 0.0)
ROPE_SIN_A = _lane_pattern(lambda r: 1.0 if 8 <= r < ROPE_DIM else 0.0)
ROPE_SIN_B = _lane_pattern(lambda r: -1.0 if r < 8 else 0.0)


def _rope_tables(positions):
    ang = positions.astype(F32)[:, None] * ROPE_INV[None, :]
    sn = jnp.sin(ang)
    return jnp.concatenate([jnp.cos(ang), sn * ROPE_SIN_A[None, :], sn * ROPE_SIN_B[None, :]], axis=1)


def _expansion():
    E = np.arange(1024)[None, :] // HEAD_DIM == np.arange(128)[:, None]
    return jnp.asarray(E, BF16), jnp.asarray(E.T, BF16)


def _ssd_args(conv_w, conv_b, dt_bias, a_log, d_skip, norm_w, E):
    return (conv_w, conv_b, dt_bias.reshape(-1), a_log.reshape(-1), d_skip.reshape(-1), norm_w, E)


def kernel(x, positions, w_in, conv_w, conv_b, dt_bias, a_log, d_skip, ssd_norm_w, attn_sinks, w_out, ln_g, ln_b, loss_target, m_w_in, m_conv_w, m_conv_b, m_dt_bias, m_a_log, m_d_skip, m_ssd_norm_w, m_attn_sinks, m_w_out, m_ln_g, m_ln_b, v_w_in, v_conv_w, v_conv_b, v_dt_bias, v_a_log, v_d_skip, v_ssd_norm_w, v_attn_sinks, v_w_out, v_ln_g, v_ln_b):
    me = _index(*_position())
    me1 = me.reshape(1).astype(jnp.int32)
    x0, target = x[0], loss_target[0]
    bf16_shard = lambda shape: jax.ShapeDtypeStruct(shape, BF16)
    E, ET = _expansion()
    tabs = _rope_tables(positions[0])
    sinks = attn_sinks.reshape(-1)

    w_ssd, w_att = _unpack_w_in(_gather_w_in(w_in[0].astype(BF16)))
    gather_conv_w = _Hosted([conv_w[0]], [jax.ShapeDtypeStruct((N_DEV,) + conv_w.shape[1:], F32)],
                            [_Flow("gather", 0, 0)])

    proj_ssd, proj_att, xb, conv_w_all = _in_proj(x0, w_ssd, w_att, tm=512, comm=gather_conv_w)
    conv_w_f = jnp.transpose(conv_w_all, (1, 0, 2)).reshape(4, D_XBC)
    ssd_args = _ssd_args(conv_w_f, conv_b, dt_bias, a_log, d_skip, ssd_norm_w, E)
    gather_w_out = _Hosted([w_out[0].astype(BF16)], [bf16_shard((N_DEV, 256, D_MODEL))], [_Flow("gather", 0, 0)])
    y, ypre, hprev, pre, w_out_all = _mixer_forward(proj_ssd, proj_att, tabs, sinks, *ssd_args, comm=gather_w_out)
    w_out_f = w_out_all.reshape(2 * D_MODEL, D_MODEL)
    dr, dy, acc_head = _head(y, x0, target, w_out_f, ln_g, ln_b, tm=512)

    dw_out, dw_out_bf16 = _matmul_tn(y, dr, tl=1024, tn=D_MODEL, name="dw_out", emit_bf16=True)
    own_out = lax.dynamic_index_in_dim(dw_out.reshape(N_DEV, 256, D_MODEL), me, axis=0, keepdims=False)
    send_out = _Hosted([dw_out_bf16.reshape(N_DEV, 256, D_MODEL)], [bf16_shard((N_DEV - 1, 256, D_MODEL))],
                       [_Flow("exchange", 0, 0)])
    d_ssd, acc_cw, acc_w, acc_s, recv_out = _ssd_backward(proj_ssd, hprev, ypre, pre, dy, *ssd_args, ET, comm=send_out)
    parts_lo, own_lo, dw_dt_block = _dw_in(me1, xb, d_ssd, 0)
    recv_shape = bf16_shard((N_DEV - 1, D_MODEL, SHARD_COLS))
    send_lo = _Hosted([parts_lo], [recv_shape], [_Flow("exchange", 0, 0, target_x=0, target_c=0)])
    d_att, dsink, recv_in = _swa_backward(proj_att, tabs, sinks, dy, comm=send_lo)
    (stack_hi,) = _dw_in(me1, xb, d_att, 1, dw_dt_block)
    accs = [acc_head, acc_cw, acc_w, acc_s, dsink]
    send_rest = _Hosted([recv_in, parts_lo] + accs,
                        [recv_shape] + [jax.ShapeDtypeStruct((N_DEV,) + a.shape, F32) for a in accs],
                        [_Flow("exchange", 1, 0, target_x=0, target_c=1)] + [_Flow("gather", 2 + i, 1 + i) for i in range(5)],
                        aliases={0: 0})
    dx, own_hi, recv_in, *gathered = _input_gradient(d_ssd, d_att, w_ssd, w_att, dr, tm=256, comm=send_rest,
                                                     stack_hi=stack_hi)
    n_recv_in = jnp.where(me < 4, N_DEV - 1, 3).reshape(1).astype(jnp.int32)
    n_recv_out = jnp.full((1,), N_DEV - 1, jnp.int32)

    g_in, d_in, nm_in, nv_in = [_from_minor_rows_view(r) for r in _adamw_w_in(
        n_recv_in, own_lo, own_hi, recv_in, _minor_rows_view(w_in), _minor_rows_view(m_w_in), _minor_rows_view(v_w_in))]
    g_out, d_out, nm_out, nv_out = _adamw_shard(n_recv_out, own_out, recv_out, w_out[0], m_w_out[0], v_w_out[0],
                                                rows=256, name="adamw_w_out")
    loss, g_conv_w, small = _adamw_small(gathered, dict(
        conv_b=(conv_b, m_conv_b, v_conv_b), dt_bias=(dt_bias, m_dt_bias, v_dt_bias), a_log=(a_log, m_a_log, v_a_log),
        d_skip=(d_skip, m_d_skip, v_d_skip), ssd_norm_w=(ssd_norm_w, m_ssd_norm_w, v_ssd_norm_w),
        attn_sinks=(attn_sinks, m_attn_sinks, v_attn_sinks), ln_g=(ln_g, m_ln_g, v_ln_g), ln_b=(ln_b, m_ln_b, v_ln_b)))
    g_cw = lax.dynamic_slice_in_dim(g_conv_w, me * (D_XBC // N_DEV), D_XBC // N_DEV, axis=1)
    d_cw, nm_cw, nv_cw = _adamw_plain(g_cw, conv_w[0], m_conv_w[0], v_conv_w[0])

    def leaves(i, big_in, cw, big_out):
        mid = [small[k][i] for k in ("conv_b", "dt_bias", "a_log", "d_skip", "ssd_norm_w", "attn_sinks")]
        return [big_in, cw[None]] + mid + [big_out[None], small["ln_g"][i], small["ln_b"][i]]

    return (loss.reshape(()), dx[None], *leaves(0, g_in, g_cw, g_out), *leaves(1, d_in, d_cw, d_out),
            *leaves(2, nm_in, nm_cw, nm_out), *leaves(3, nv_in, nv_cw, nv_out))
```

```python
import jax
import jax.numpy as jnp
from jax import lax
from jax.experimental import pallas as pl
from jax.experimental.pallas import tpu as pltpu
import numpy as np

F32 = jnp.float32
BF16 = jnp.bfloat16
_MXU = jnp.bfloat16

N_DEV = 8
D_MODEL = 1024
D_SSD = 1024
D_ATT = 1024
HEAD_DIM = 64
N_HEADS = 16
SSD_GROUPS = 2
KV_HEADS = 4
CHUNK = 128
D_XBC = 1536
D_IN_PROJ = 5136
ROPE_DIM = 16
ROPE_THETA = 500000.0
ALPHA = (2.0 * 1) ** 0.25
LN_EPS = 1e-5
RMS_EPS = 1e-5
ATT_SCALE = HEAD_DIM ** -0.5
NEG = -1e30

S_Z, S_XS, S_B, S_C, S_DT, S_W = 0, 1024, 2048, 2304, 2560, 2816
N_SSD_REAL = 2576
A_Q, A_K, A_V, A_G, A_W = 0, 1024, 1280, 1536, 2560

ADAM_LR = 0.001
ADAM_B1 = 0.9
ADAM_B2 = 0.999
ADAM_EPS = 1e-08
ADAM_WD = 0.01
ADAM_STEP = 10

VMEM_LIMIT = 48 * 1024 * 1024
MESH = pl.DeviceIdType.MESH


def _params(sem=None):
    return pltpu.CompilerParams(dimension_semantics=sem, vmem_limit_bytes=VMEM_LIMIT)


def _mm(a, b):
    return jnp.dot(a.astype(_MXU), b.astype(_MXU), preferred_element_type=F32)


def _mm_nt(a, b):
    return lax.dot_general(a.astype(_MXU), b.astype(_MXU), (((1,), (1,)), ((), ())),
                           preferred_element_type=F32)


def _mm_tn(a, b):
    return lax.dot_general(a.astype(_MXU), b.astype(_MXU), (((0,), (0,)), ((), ())),
                           preferred_element_type=F32)


def _split3(v):
    hi = v.astype(BF16)
    r = v - hi.astype(F32)
    mid = r.astype(BF16)
    lo = (r - mid.astype(F32)).astype(BF16)
    return hi, mid, lo


def _mm_exact_r(v, p01):
    hi, mid, lo = _split3(v)
    d = lambda a: jnp.dot(a, p01, preferred_element_type=F32)
    return d(hi) + d(mid) + d(lo)


def _mm_exact_l(p01, v):
    hi, mid, lo = _split3(v)
    d = lambda a: jnp.dot(p01, a, preferred_element_type=F32)
    return d(hi) + d(mid) + d(lo)


def _mm_2pass_r(v, p01):
    hi = v.astype(BF16)
    lo = (v - hi.astype(F32)).astype(BF16)
    return jnp.dot(hi, p01, preferred_element_type=F32) + jnp.dot(lo, p01, preferred_element_type=F32)


def _sigmoid(x):
    return 1.0 / (1.0 + jnp.exp(-x))


def _softplus(x):
    e = jnp.exp(-jnp.abs(x))
    u = 1.0 + e
    log1p = jnp.where(u == 1.0, e, jnp.log(u) * (e / (u - 1.0)))
    return jnp.maximum(x, 0.0) + log1p


def _rows8(rows):
    n = rows[0].shape[1]
    rid = lax.broadcasted_iota(jnp.int32, (8, n), 0)
    out = jnp.zeros((8, n), F32)
    for k, r in enumerate(rows):
        out = out + jnp.where(rid == k, r, 0.0)
    return out


def _colsum(a):
    return jnp.sum(a, axis=0, keepdims=True)


def _in_proj(x, w_ssd, w_att, *, tm, comm=None):
    L, K = x.shape

    def body(x_ref, ws_ref, wa_ref, ps_ref, pa_ref, xb_ref):
        xb = x_ref[...].astype(_MXU)
        xb_ref[...] = xb
        ps_ref[...] = jnp.dot(xb, ws_ref[...], preferred_element_type=F32)
        pa_ref[...] = jnp.dot(xb, wa_ref[...], preferred_element_type=F32)

    row = lambda w: pl.BlockSpec((tm, w), lambda i: (i, 0))
    resident = lambda a: pl.BlockSpec(a.shape, lambda i: (0, 0), pipeline_mode=pl.Buffered(1))
    return _call(
        body, comm, name="in_proj", grid=(L // tm,),
        in_specs=[row(K), resident(w_ssd), resident(w_att)], out_specs=[row(S_W), row(A_W), row(K)],
        out_shape=[jax.ShapeDtypeStruct((L, S_W), F32), jax.ShapeDtypeStruct((L, A_W), F32),
                   jax.ShapeDtypeStruct((L, K), _MXU)],
        scratch_shapes=[], args=(x, w_ssd, w_att))


def _matmul_tn(a, g, *, tl, tn, name, emit_bf16=False):
    L, M = a.shape
    N = g.shape[1]
    last = L // tl - 1

    def body(a_ref, g_ref, o_ref, *rest):
        @pl.when(pl.program_id(1) == 0)
        def _():
            o_ref[...] = jnp.zeros_like(o_ref)

        o_ref[...] += _mm_tn(a_ref[...], g_ref[...])
        if emit_bf16:
            @pl.when(pl.program_id(1) == last)
            def _():
                rest[0][...] = o_ref[...].astype(BF16)

    spec = pl.BlockSpec((M, tn), lambda j, l: (0, j))
    res = pl.pallas_call(
        body, name=name, grid=(N // tn, L // tl),
        in_specs=[pl.BlockSpec((tl, M), lambda j, l: (l, 0)), pl.BlockSpec((tl, tn), lambda j, l: (l, j))],
        out_specs=[spec, spec] if emit_bf16 else [spec],
        out_shape=[jax.ShapeDtypeStruct((M, N), F32)] + ([jax.ShapeDtypeStruct((M, N), BF16)] if emit_bf16 else []),
        compiler_params=_params(("arbitrary", "arbitrary")),
    )(a, g)
    return res if emit_bf16 else res[0]


def _position():
    return lax.axis_index("x"), lax.axis_index("y"), lax.axis_index("c")


def _index(px, py, pc):
    return 4 * px + 2 * py + pc


def _flip(pos, k):
    x, y, c = pos
    return ((1 - x) if (k >> 2) & 1 else x, (1 - y) if (k >> 1) & 1 else y, (1 - c) if k & 1 else c)


def _when(cond, fn):
    if cond is True:
        fn()
    else:
        pl.when(cond)(fn)


def _remote(src, dst, send_sem, recv_sem, peer):
    return pltpu.make_async_remote_copy(src_ref=src, dst_ref=dst, send_sem=send_sem, recv_sem=recv_sem,
                                        device_id=peer, device_id_type=MESH)


class _Flow:
    def __init__(self, kind, operand, result, target_x=None, target_c=None):
        self.kind, self.operand, self.result, self.target_x, self.target_c = kind, operand, result, target_x, target_c

    def owns(self, pos):
        if self.target_x is None:
            return True
        cond = pos[0] == self.target_x
        return cond if self.target_c is None else cond & (pos[2] == self.target_c)


class _Hosted:
    def __init__(self, operands, out_shapes, flows, aliases=None):
        self.operands, self.out_shapes, self.flows = operands, out_shapes, flows
        self.aliases = aliases or {}

    def plan(self, ins, outs, send_sems, recv_sems, local_sems):
        me = _position()
        mi = _index(*me)
        sends, recvs, locals_ = [], [], []
        for row, f in enumerate(self.flows):
            src, dst = ins[f.operand], outs[f.result]
            for k in range(1, N_DEV):
                peer = _flip(me, k)
                sems = (send_sems.at[row, k - 1], recv_sems.at[row, k - 1])
                if f.kind == "exchange":
                    owner = _index(*peer) if f.target_x is None else 2 * peer[1] + peer[2]
                    cp = _remote(src.at[owner], dst.at[k - 1], *sems, peer)
                    sends.append((f.owns(peer), cp))
                    recvs.append((f.owns(me), cp))
                else:
                    sends.append((True, _remote(src, dst.at[mi], *sems, peer)))
                    recvs.append((True, _remote(src, dst.at[_index(*peer)], *sems, peer)))
            if f.kind == "gather":
                locals_.append(pltpu.make_async_copy(src, dst.at[mi], local_sems.at[row]))

        def start():
            for cp in locals_:
                cp.start()
            for cond, cp in sends:
                _when(cond, cp.start)

        def wait():
            for cond, cp in recvs:
                _when(cond, cp.wait_recv)
            for cond, cp in sends:
                _when(cond, cp.wait_send)
            for cp in locals_:
                cp.wait()

        return start, wait


def _call(body, comm, *, name, grid, in_specs, out_specs, out_shape, scratch_shapes, args, aliases=None):
    io_alias = dict(aliases or {})
    semantics = ("arbitrary",) * len(grid)
    if comm is None:
        return pl.pallas_call(body, name=name, grid=grid, in_specs=in_specs, out_specs=out_specs, out_shape=out_shape,
                              scratch_shapes=scratch_shapes, input_output_aliases=io_alias,
                              compiler_params=_params(semantics))(*args)
    n_in, n_out, n_scr = len(args), len(out_shape), len(scratch_shapes)
    c_in, c_out, rows = len(comm.operands), len(comm.out_shapes), len(comm.flows)

    def hosted(*refs):
        ins, refs = refs[:n_in], refs[n_in:]
        cins, refs = refs[:c_in], refs[c_in:]
        outs, refs = refs[:n_out], refs[n_out:]
        couts, refs = refs[:c_out], refs[c_out:]
        scr, (send_sems, recv_sems, local_sems) = refs[:n_scr], refs[n_scr:]
        start, wait = comm.plan(cins, couts, send_sems, recv_sems, local_sems)
        ids = [pl.program_id(d) for d in range(len(grid))]
        first, last = ids[0] == 0, ids[0] == grid[0] - 1
        for d in range(1, len(grid)):
            first, last = first & (ids[d] == 0), last & (ids[d] == grid[d] - 1)
        pl.when(first)(start)
        body(*ins, *outs, *scr)
        pl.when(last)(wait)

    for ci, co in comm.aliases.items():
        io_alias[n_in + ci] = n_out + co
    any_spec = pl.BlockSpec(memory_space=pl.ANY)
    sems = [pltpu.SemaphoreType.DMA((rows, N_DEV - 1)), pltpu.SemaphoreType.DMA((rows, N_DEV - 1)),
            pltpu.SemaphoreType.DMA((rows,))]
    return pl.pallas_call(
        hosted, name=name, grid=grid, in_specs=list(in_specs) + [any_spec] * c_in,
        out_specs=list(out_specs) + [any_spec] * c_out, out_shape=list(out_shape) + list(comm.out_shapes),
        scratch_shapes=list(scratch_shapes) + sems, input_output_aliases=io_alias,
        compiler_params=_params(semantics))(*args, *comm.operands)


def _head_row(ref, width, rep):
    hid = lax.broadcasted_iota(jnp.int32, (1, width), 1) // rep
    row = jnp.zeros((1, width), F32)
    for h in range(N_HEADS):
        row = jnp.where(hid == h, ref[h], row)
    return row


def _rows_from_above(u_b, s, ext_scr, row, col):
    down = (row - col == s).astype(_MXU)
    return jnp.concatenate([ext_scr[8 - s:16 - s, :], jnp.dot(down, u_b, preferred_element_type=F32)[8:128]], axis=0)


def _ssd_recompute(first, p_ref, halo_ref, cw_ref, cb_ref, dtb_ref, alog_ref, e_ref, ext_scr, pre=None):
    row = lax.broadcasted_iota(jnp.int32, (128, 128), 0)
    col = lax.broadcasted_iota(jnp.int32, (128, 128), 1)
    ext_scr[0:8, :] = jnp.where(first, 0.0, halo_ref[:, S_XS:S_DT])
    if pre is not None:
        ext_scr[8:16, :] = p_ref[0:8, S_XS:S_DT]
    else:
        ext_scr[8:136, :] = p_ref[:, S_XS:S_DT]
        cw = cw_ref[...]
        pre = (cb_ref[0:1, :] + cw[3:4, :] * ext_scr[8:136, :] + cw[2:3, :] * ext_scr[7:135, :]
               + cw[1:2, :] * ext_scr[6:134, :] + cw[0:1, :] * ext_scr[5:133, :])
    sg = _sigmoid(pre)
    act = pre * sg
    lane = lax.broadcasted_iota(jnp.int32, (1, 128), 1)
    A = jnp.where(lane < N_HEADS, -jnp.exp(_head_row(alog_ref, 128, 1)), 0.0)
    raw = p_ref[:, S_DT:S_DT + 128] + _head_row(dtb_ref, 128, 1)
    dt = _softplus(raw)
    dA = dt * A
    tril = (row >= col).astype(BF16)
    acs = _mm_exact_l(tril, dA)
    last = acs[127:128, :]
    ds = jnp.exp(last - acs)
    eo = jnp.exp(acs)
    E = e_ref[...]
    ex = _mm_2pass_r(jnp.concatenate([dt, ds, eo], axis=0), E)
    dt_e, ds_e, eo_e = ex[0:128], ex[128:256], ex[256:384]
    xs_c = act[:, 0:1024]
    X = xs_c * dt_e
    return dict(pre=pre, sg=sg, xs_c=xs_c, Bc=act[:, 1024:1280], Cc=act[:, 1280:1536], A=A, raw=raw, dt=dt,
                acs=acs, acsT=acs.T, eo_e=eo_e, ds_e=ds_e, dt_e=dt_e, cd_e=eo_e[127:128, :],
                X=X, Xd=X * ds_e, row=row, col=col)


def _split_halves(t):
    lo = _lo_half(CHUNK)
    return jnp.concatenate([jnp.where(lo, t, 0.0), jnp.where(lo, 0.0, t)], axis=0)


def _ssd_core(R, hprev):
    causal = R["row"] >= R["col"]
    acs, acsT, X = R["acs"], R["acsT"], R["X"]
    ydiag, yoff, snew = [], [], []
    for g in range(SSD_GROUPS):
        Bg = R["Bc"][:, g * 128:(g + 1) * 128]
        Cg = R["Cc"][:, g * 128:(g + 1) * 128]
        cols = slice(g * 512, (g + 1) * 512)
        CB = _mm_nt(Cg, Bg)
        snew.append(_mm_tn(Bg, R["Xd"][:, cols]))
        yoff.append(_mm(Cg, hprev[:, cols]))
        for j in range(4):
            h0 = g * 8 + 2 * j
            ms = [CB * jnp.exp(jnp.where(causal, acs[:, h:h + 1] - acsT[h:h + 1, :], NEG)) for h in (h0, h0 + 1)]
            ydiag.append(_mm(jnp.concatenate(ms, axis=1), _split_halves(X[:, h0 * HEAD_DIM:h0 * HEAD_DIM + 128])))
    Y = jnp.concatenate(ydiag, axis=1) + jnp.concatenate(yoff, axis=1) * R["eo_e"]
    return Y, jnp.concatenate(snew, axis=1)


def _ssd_forward_step(p_ref, halo_ref, cw_ref, cb_ref, dtb_ref, alog_ref, dsk_ref, nw_ref, e_ref,
                      y_ref, ypre_ref, hprev_ref, pre_ref, h_scr, ext_scr):
    c = pl.program_id(0)
    first = c == 0

    @pl.when(first)
    def _():
        h_scr[...] = jnp.zeros_like(h_scr)

    R = _ssd_recompute(first, p_ref, halo_ref, cw_ref, cb_ref, dtb_ref, alog_ref, e_ref, ext_scr)
    hprev = h_scr[...]
    hprev_ref[...] = hprev
    pre_ref[...] = R["pre"]
    Y, snew = _ssd_core(R, hprev)
    h_scr[...] = hprev * R["cd_e"] + snew
    Y = Y + _head_row(dsk_ref, D_SSD, HEAD_DIM) * R["xs_c"]
    ypre_ref[...] = Y
    z = p_ref[:, S_Z:S_Z + 1024]
    yf = Y * (z * _sigmoid(z))
    outs = []
    for g in range(SSD_GROUPS):
        yg = yf[:, g * 512:(g + 1) * 512]
        r = lax.rsqrt(jnp.mean(yg * yg, axis=-1, keepdims=True) + RMS_EPS)
        outs.append(yg * r)
    y_ref[:, 0:D_SSD] = (jnp.concatenate(outs, axis=1) * nw_ref[0:1, :]).astype(y_ref.dtype)


def _ssd_backward(proj_ssd, hprev_all, ypre, pre, dy, conv_w, conv_b, dt_bias, a_log, d_skip, norm_w, E, ET, comm=None):
    L = proj_ssd.shape[0]
    nc = L // CHUNK

    def body(p_ref, halo_ref, hprev_ref, ypre_ref, pre_ref, dy_ref, cw_ref, cb_ref, dtb_ref, alog_ref, dsk_ref, nw_ref, e_ref,
             et_ref, dp_ref, acc_cw_ref, acc_w_ref, acc_s_ref, dh_scr, ext_scr, ext2_scr, nxt_scr):
        i = pl.program_id(0)
        c = nc - 1 - i
        first = c == 0

        @pl.when(i == 0)
        def _():
            dh_scr[...] = jnp.zeros_like(dh_scr)
            nxt_scr[...] = jnp.zeros_like(nxt_scr)
            acc_cw_ref[...] = jnp.zeros_like(acc_cw_ref)
            acc_w_ref[...] = jnp.zeros_like(acc_w_ref)
            acc_s_ref[...] = jnp.zeros_like(acc_s_ref)

        R = _ssd_recompute(first, p_ref, halo_ref, cw_ref, cb_ref, dtb_ref, alog_ref, e_ref, ext_scr, pre_ref[...])
        hprev = hprev_ref[...]
        xs_c, X, Xd = R["xs_c"], R["X"], R["Xd"]
        acs, acsT = R["acs"], R["acsT"]
        ET = et_ref[...]
        dsk = _head_row(dsk_ref, D_SSD, HEAD_DIM)
        Y = ypre_ref[...]

        z = p_ref[:, S_Z:S_Z + 1024]
        sz = _sigmoid(z)
        silz = z * sz
        yf = Y * silz
        dyv = dy_ref[...]
        nw = nw_ref[0:1, :]
        dyf_parts, dnw_parts = [], []
        for g in range(SSD_GROUPS):
            cols = slice(g * 512, (g + 1) * 512)
            yg = yf[:, cols]
            r = lax.rsqrt(jnp.mean(yg * yg, axis=-1, keepdims=True) + RMS_EPS)
            yn = yg * r
            dyn = dyv[:, cols] * nw[:, cols]
            dnw_parts.append(_colsum(dyv[:, cols] * yn))
            dyf_parts.append(r * (dyn - yn * jnp.mean(dyn * yn, axis=-1, keepdims=True)))
        dyf = jnp.concatenate(dyf_parts, axis=1)
        dY = dyf * silz
        dz = dyf * Y * (sz * (1.0 + z * (1.0 - sz)))

        dhn = dh_scr[...]
        dYo = dY * R["eo_e"]
        causal = R["row"] >= R["col"]
        dacs = jnp.zeros((128, 128), F32)
        dacs_t = jnp.zeros((128, 128), F32)
        dxdiag, dxd, dhprev, dBs, dCs, yoff = [], [], [], [], [], []
        for g in range(SSD_GROUPS):
            Bg = R["Bc"][:, g * 128:(g + 1) * 128]
            Cg = R["Cc"][:, g * 128:(g + 1) * 128]
            cols = slice(g * 512, (g + 1) * 512)
            CB = _mm_nt(Cg, Bg)
            dCB = jnp.zeros((128, 128), F32)
            for j in range(4):
                h0 = g * 8 + 2 * j
                pc = slice(h0 * HEAD_DIM, h0 * HEAD_DIM + 128)
                dYst = _split_halves(dY[:, pc])
                dMst = _mm_nt(dYst, X[:, pc])
                mts = []
                for a, h in enumerate((h0, h0 + 1)):
                    acol = acs[:, h:h + 1]
                    arow = acsT[h:h + 1, :]
                    Lm = jnp.exp(jnp.where(causal, acol - arow, NEG))
                    M = CB * Lm
                    dM = dMst[a * 128:(a + 1) * 128]
                    dCB = dCB + dM * Lm
                    G = dM * M
                    dacs = dacs + jnp.where(R["col"] == h, jnp.sum(G, axis=1, keepdims=True), 0.0)
                    dacs_t = dacs_t + jnp.where(R["row"] == h, jnp.sum(G, axis=0, keepdims=True), 0.0)
                    mts.append(M.T)
                dxdiag.append(_mm(jnp.concatenate(mts, axis=1), dYst))
            dS = dhn[:, cols]
            dxd.append(_mm(Bg, dS))
            yoff.append(_mm(Cg, hprev[:, cols]))
            dhprev.append(_mm_tn(Cg, dYo[:, cols]))
            dCs.append(_mm_nt(dYo[:, cols], hprev[:, cols]) + _mm(dCB, Bg))
            dBs.append(_mm_tn(dCB, Cg) + _mm_nt(Xd[:, cols], dS))
        Yoff = jnp.concatenate(yoff, axis=1) * R["eo_e"]
        dXd = jnp.concatenate(dxd, axis=1)
        dX = jnp.concatenate(dxdiag, axis=1) + dXd * R["ds_e"]
        t_state = dXd * Xd
        dacs = dacs + _mm_2pass_r(dY * Yoff - t_state, ET) - dacs_t.T
        v_last = _colsum(t_state + dhn * hprev * R["cd_e"])
        dlast = _mm_exact_r(jnp.broadcast_to(v_last, (8, 1024)), ET)[0:1, :]
        dacs = dacs + jnp.where(R["row"] == 127, dlast, 0.0)
        triu = (R["col"] >= R["row"]).astype(BF16)
        da = _mm_exact_l(triu, dacs)
        ddt = da * R["A"] + _mm(dX * xs_c, ET)
        ddt_raw = ddt * _sigmoid(R["raw"])
        dxs_c = dX * R["dt_e"] + dY * dsk
        dh_scr[...] = jnp.concatenate(dhprev, axis=1) + dhn * R["cd_e"]

        dact = jnp.concatenate([dxs_c] + dBs + dCs, axis=1)
        pre, sg = R["pre"], R["sg"]
        dpre = dact * (sg * (1.0 + pre * (1.0 - sg)))
        ext2_scr[0:8, :] = dpre[120:128, :]
        ext2_scr[8:16, :] = nxt_scr[...]
        nxt_scr[...] = dpre[0:8, :]
        cw = cw_ref[...]
        u_b, dpre_b = p_ref[:, S_XS:S_DT].astype(_MXU), dpre.astype(_MXU)
        dxbc = cw[3:4, :] * dpre
        taps = [_colsum(dpre * p_ref[:, S_XS:S_DT])]
        for s in (1, 2, 3):
            up = (R["col"] - R["row"] == s).astype(_MXU)
            d_s = jnp.concatenate([jnp.dot(up, dpre_b, preferred_element_type=F32)[0:120],
                                   ext2_scr[s:8 + s, :]], axis=0)
            dxbc = dxbc + cw[3 - s:4 - s, :] * d_s
            taps.append(_colsum(dpre * _rows_from_above(u_b, s, ext_scr, R["row"], R["col"])))
        acc_cw_ref[...] += _rows8(taps[::-1] + [_colsum(dpre)])
        acc_w_ref[...] += _rows8([jnp.concatenate(dnw_parts, axis=1), _colsum(dY * xs_c)])
        acc_s_ref[...] += _rows8([_colsum(ddt_raw), _colsum(da * R["dt"])])

        lane = lax.broadcasted_iota(jnp.int32, (128, 128), 1)
        dp_ref[:, S_Z:S_Z + 1024] = dz.astype(dp_ref.dtype)
        dp_ref[:, S_XS:S_DT] = dxbc.astype(dp_ref.dtype)
        dp_ref[:, S_DT:S_DT + 128] = jnp.where(lane < N_HEADS, ddt_raw, 0.0).astype(dp_ref.dtype)
        dp_ref[:, S_DT + 128:S_W] = jnp.zeros((128, 128), dp_ref.dtype)

        @pl.when(i == nc - 1)
        def _():
            acc = acc_s_ref[...]
            dskip = _mm_exact_r(acc_w_ref[...], ET)[1:2, :]
            acc_s_ref[...] = _rows8([acc[0:1, :], acc[1:2, :] * R["A"], dskip])

    const = lambda shape: pl.BlockSpec(shape, lambda i: (0, 0))
    smem = pl.BlockSpec(memory_space=pltpu.SMEM)
    rev = lambda i: (nc - 1 - i, 0)
    return _call(
        body, comm, name="ssd_bwd", grid=(nc,),
        in_specs=[pl.BlockSpec((CHUNK, S_W), rev),
                  pl.BlockSpec((8, S_W), lambda i: (jnp.maximum((nc - 1 - i) * 16 - 1, 0), 0)),
                  pl.BlockSpec((128, 1024), rev),
                  pl.BlockSpec((CHUNK, D_SSD), rev),
                  pl.BlockSpec((CHUNK, D_XBC), rev),
                  pl.BlockSpec((CHUNK, D_SSD), rev),
                  const((4, D_XBC)), const((1, D_XBC)), smem, smem, smem, const((1, 1024)),
                  const((128, 1024)), const((1024, 128))],
        out_specs=[pl.BlockSpec((CHUNK, S_W), rev), const((8, D_XBC)), const((8, 1024)), const((8, 128))],
        out_shape=[jax.ShapeDtypeStruct((L, S_W), _MXU), jax.ShapeDtypeStruct((8, D_XBC), F32),
                   jax.ShapeDtypeStruct((8, 1024), F32), jax.ShapeDtypeStruct((8, 128), F32)],
        scratch_shapes=[pltpu.VMEM((128, 1024), F32), pltpu.VMEM((16, D_XBC), F32),
                        pltpu.VMEM((16, D_XBC), F32), pltpu.VMEM((8, D_XBC), F32)],
        args=(proj_ssd, proj_ssd, hprev_all, ypre, pre, dy, conv_w, conv_b, dt_bias, a_log, d_skip, norm_w, E, ET))


def _rope(t, tab):
    cos, sa, sb = tab[:, 0:128], tab[:, 128:256], tab[:, 256:384]
    outs = []
    for i in range(t.shape[1] // 128):
        tg = t[:, i * 128:(i + 1) * 128]
        outs.append(tg * cos + pltpu.roll(tg, 8, 1) * sa + pltpu.roll(tg, 120, 1) * sb)
    return jnp.concatenate(outs, axis=1)


def _rope_transposed(d, tab):
    cos, sa, sb = tab[:, 0:128], tab[:, 128:256], tab[:, 256:384]
    outs = []
    for i in range(d.shape[1] // 128):
        dg = d[:, i * 128:(i + 1) * 128]
        outs.append(dg * cos + pltpu.roll(dg * sa, 120, 1) + pltpu.roll(dg * sb, 8, 1))
    return jnp.concatenate(outs, axis=1)


def _lo_half(rows):
    return lax.broadcasted_iota(jnp.int32, (rows, 128), 1) < HEAD_DIM


def _native_half(rows, j):
    lo = _lo_half(rows)
    return lo if j % 2 == 0 else jnp.logical_not(lo)


def _kv_native(t, j):
    p = j // 2
    return jnp.where(_native_half(t.shape[0], j), t[:, p * 128:(p + 1) * 128], 0.0)


def _stack_heads(t, j):
    out = []
    for m in (2 * j, 2 * j + 1):
        pair = t[:, m * 128:(m + 1) * 128]
        swapped = pltpu.roll(pair, HEAD_DIM, 1)
        out += [pair, swapped] if j % 2 == 0 else [swapped, pair]
    return jnp.concatenate(out, axis=0)


def _unstack_heads(s, j):
    out = []
    for m in range(2):
        first, second = s[256 * m:256 * m + 128], s[256 * m + 128:256 * m + 256]
        if j % 2 == 0:
            out.append(first + pltpu.roll(second, HEAD_DIM, 1))
        else:
            out.append(pltpu.roll(first, HEAD_DIM, 1) + second)
    return jnp.concatenate(out, axis=1)


def _keep_native(r, j):
    return jnp.where(_native_half(r.shape[0], j), r, 0.0)


def _sink_row(sink_ref, j):
    hid = lax.broadcasted_iota(jnp.int32, (1, 4 * CHUNK), 1) // CHUNK
    row = jnp.zeros((1, 4 * CHUNK), F32)
    for hh in range(4):
        row = jnp.where(hid == hh, sink_ref[4 * j + hh], row)
    return row


def _from_current():
    si = lax.broadcasted_iota(jnp.int32, (CHUNK, 4 * CHUNK), 0)
    qi = lax.broadcasted_iota(jnp.int32, (CHUNK, 4 * CHUNK), 1) % CHUNK
    return si <= qi


def _fold(full, from_cur, pen=0.0):
    return jnp.where(from_cur, full[CHUNK:2 * CHUNK], full[0:CHUNK] + pen)


def _unfold(t, from_cur):
    c = jnp.where(from_cur, t, 0.0)
    return jnp.concatenate([t - c, c], axis=0)


def _softmax_sink(s, sink):
    mx = jnp.maximum(jnp.max(s, axis=0, keepdims=True), sink)
    p = jnp.exp(s - mx)
    esink = jnp.exp(sink - mx)
    inv = 1.0 / (jnp.sum(p, axis=0, keepdims=True) + esink)
    return p * inv, esink * inv


def _swa_inputs(blk, p_ref, prev_ref, tab_ref, ptab_ref):
    tab = tab_ref[...]
    qr = _rope(p_ref[:, A_Q:A_Q + 1024], tab) * ATT_SCALE
    kk = jnp.concatenate([_rope(prev_ref[:, 0:256], ptab_ref[...]), _rope(p_ref[:, A_K:A_K + 256], tab)], axis=0)
    vv = jnp.concatenate([prev_ref[:, 256:512], p_ref[:, A_V:A_V + 256]], axis=0)
    return tab, qr, kk, vv, jnp.where(blk > 0, 0.0, NEG)


def _swa_forward_step(sink_ref, p_ref, prev_ref, tab_ref, ptab_ref, y_ref):
    n = pl.program_id(0)
    _, qr, kk, vv, pen = _swa_inputs(n, p_ref, prev_ref, tab_ref, ptab_ref)
    from_cur = _from_current()
    outs = []
    for j in range(KV_HEADS):
        s = _fold(_mm_nt(_kv_native(kk, j), _stack_heads(qr, j)), from_cur, pen)
        P, _ = _softmax_sink(s, _sink_row(sink_ref, j))
        outs.append(_unstack_heads(_mm_tn(_unfold(P, from_cur), _kv_native(vv, j)), j))
    g = p_ref[:, A_G:A_G + 1024]
    y_ref[:, D_SSD:D_SSD + D_ATT] = (jnp.concatenate(outs, axis=1) * (g * _sigmoid(g))).astype(y_ref.dtype)


def _mixer_forward(proj_ssd, proj_att, tabs, sinks, conv_w, conv_b, dt_bias, a_log, d_skip, norm_w, E, comm=None):
    L = proj_ssd.shape[0]
    nc = L // CHUNK

    def body(p_ref, halo_ref, cw_ref, cb_ref, dtb_ref, alog_ref, dsk_ref, nw_ref, e_ref,
             sink_ref, pa_ref, prev_ref, tab_ref, ptab_ref, y_ref, ypre_ref, hprev_ref, pre_ref, h_scr, ext_scr):
        _ssd_forward_step(p_ref, halo_ref, cw_ref, cb_ref, dtb_ref, alog_ref, dsk_ref, nw_ref, e_ref,
                          y_ref, ypre_ref, hprev_ref, pre_ref, h_scr, ext_scr)
        _swa_forward_step(sink_ref, pa_ref, prev_ref, tab_ref, ptab_ref, y_ref)

    const = lambda shape: pl.BlockSpec(shape, lambda c: (0, 0))
    smem = pl.BlockSpec(memory_space=pltpu.SMEM)
    rows = lambda w: pl.BlockSpec((CHUNK, w), lambda c: (c, 0))
    return _call(
        body, comm, name="mixer_fwd", grid=(nc,),
        in_specs=[rows(S_W), pl.BlockSpec((8, S_W), lambda c: (jnp.maximum(c * 16 - 1, 0), 0)),
                  const((4, D_XBC)), const((1, D_XBC)), smem, smem, smem, const((1, 1024)), const((128, 1024)),
                  smem, rows(A_W), pl.BlockSpec((CHUNK, 512), lambda c: (jnp.maximum(c - 1, 0), 2)),
                  rows(384), pl.BlockSpec((CHUNK, 384), lambda c: (jnp.maximum(c - 1, 0), 0))],
        out_specs=[rows(D_SSD + D_ATT), rows(D_SSD), pl.BlockSpec((128, 1024), lambda c: (c, 0)), rows(D_XBC)],
        out_shape=[jax.ShapeDtypeStruct((L, D_SSD + D_ATT), _MXU), jax.ShapeDtypeStruct((L, D_SSD), F32),
                   jax.ShapeDtypeStruct((nc * 128, 1024), F32), jax.ShapeDtypeStruct((L, D_XBC), F32)],
        scratch_shapes=[pltpu.VMEM((128, 1024), F32), pltpu.VMEM((136, D_XBC), F32)],
        args=(proj_ssd, proj_ssd, conv_w, conv_b, dt_bias, a_log, d_skip, norm_w, E,
              sinks, proj_att, proj_att, tabs, tabs))


def _swa_backward(proj_att, tabs, sinks, dy, comm=None):
    L = proj_att.shape[0]
    nb = L // CHUNK

    def body(sink_ref, p_ref, prev_ref, tab_ref, ptab_ref, dy_ref, dp_ref, dsink_ref, carry_k, carry_v):
        i = pl.program_id(0)
        n = nb - 1 - i

        @pl.when(i == 0)
        def _():
            carry_k[...] = jnp.zeros_like(carry_k)
            carry_v[...] = jnp.zeros_like(carry_v)
            dsink_ref[...] = jnp.zeros_like(dsink_ref)

        tab, qr, kk, vv, pen = _swa_inputs(n, p_ref, prev_ref, tab_ref, ptab_ref)
        from_cur = _from_current()
        g = p_ref[:, A_G:A_G + 1024]
        sgm = _sigmoid(g)
        dyv = dy_ref[...]
        do_all = dyv * (g * sgm)
        lane8 = lax.broadcasted_iota(jnp.int32, (8, 128), 1)
        hid = lax.broadcasted_iota(jnp.int32, (1, 4 * CHUNK), 1) // CHUNK
        o_parts, dq_parts = [], []
        dk_nat = [jnp.zeros((2 * CHUNK, 128), F32) for _ in range(2)]
        dv_nat = [jnp.zeros((2 * CHUNK, 128), F32) for _ in range(2)]
        dsink = jnp.zeros((8, 128), F32)
        for j in range(KV_HEADS):
            qs = _stack_heads(qr, j)
            kkb, vvb = _kv_native(kk, j), _kv_native(vv, j)
            P, psink = _softmax_sink(_fold(_mm_nt(kkb, qs), from_cur, pen), _sink_row(sink_ref, j))
            p_full = _unfold(P, from_cur)
            o_parts.append(_unstack_heads(_mm_tn(p_full, vvb), j))
            do_s = _stack_heads(do_all, j)
            dP = _fold(_mm_nt(vvb, do_s), from_cur)
            D = jnp.sum(P * dP, axis=0, keepdims=True)
            ds_full = _unfold(P * (dP - D), from_cur)
            sd = psink * D
            for hh in range(4):
                dsink = dsink + jnp.where(lane8 == 4 * j + hh, -jnp.sum(jnp.where(hid == hh, sd, 0.0)), 0.0)
            dq_parts.append(_unstack_heads(_mm_tn(ds_full, kkb), j) * ATT_SCALE)
            dk_nat[j // 2] = dk_nat[j // 2] + _keep_native(_mm(ds_full, qs), j)
            dv_nat[j // 2] = dv_nat[j // 2] + _keep_native(_mm(p_full, do_s), j)
        o = jnp.concatenate(o_parts, axis=1)
        dkk = jnp.concatenate(dk_nat, axis=1)
        dvv = jnp.concatenate(dv_nat, axis=1)
        out = dp_ref.dtype
        dp_ref[:, A_Q:A_Q + 1024] = _rope_transposed(jnp.concatenate(dq_parts, axis=1), tab).astype(out)
        dp_ref[:, A_K:A_K + 256] = _rope_transposed(dkk[CHUNK:2 * CHUNK] + carry_k[...], tab).astype(out)
        dp_ref[:, A_V:A_V + 256] = (dvv[CHUNK:2 * CHUNK] + carry_v[...]).astype(out)
        dp_ref[:, A_G:A_G + 1024] = (dyv * o * (sgm * (1.0 + g * (1.0 - sgm)))).astype(out)
        carry_k[...] = dkk[0:CHUNK]
        carry_v[...] = dvv[0:CHUNK]
        dsink_ref[...] += dsink

    rev = lambda i: (nb - 1 - i, 0)
    prev = lambda i: jnp.maximum(nb - 2 - i, 0)
    return _call(
        body, comm, name="swa_bwd", grid=(nb,),
        in_specs=[pl.BlockSpec(memory_space=pltpu.SMEM),
                  pl.BlockSpec((CHUNK, A_W), rev),
                  pl.BlockSpec((CHUNK, 512), lambda i: (prev(i), 2)),
                  pl.BlockSpec((CHUNK, 384), rev),
                  pl.BlockSpec((CHUNK, 384), lambda i: (prev(i), 0)),
                  pl.BlockSpec((CHUNK, D_ATT), lambda i: (nb - 1 - i, 1))],
        out_specs=[pl.BlockSpec((CHUNK, A_W), rev), pl.BlockSpec((8, 128), lambda i: (0, 0))],
        out_shape=[jax.ShapeDtypeStruct((L, A_W), _MXU), jax.ShapeDtypeStruct((8, 128), F32)],
        scratch_shapes=[pltpu.VMEM((CHUNK, 256), F32), pltpu.VMEM((CHUNK, 256), F32)],
        args=(sinks, proj_att, proj_att, tabs, tabs, dy))


def _head(y, x, target, w_out, ln_g, ln_b, *, tm):
    L = x.shape[0]
    nsteps = L // tm

    def body(y_ref, x_ref, t_ref, wo_ref, g_ref, b_ref, dr_ref, dy_ref, acc_ref):
        i = pl.program_id(0)

        @pl.when(i == 0)
        def _():
            acc_ref[...] = jnp.zeros_like(acc_ref)

        r = ALPHA * x_ref[...] + _mm(y_ref[...], wo_ref[...])
        mu = jnp.mean(r, axis=-1, keepdims=True)
        d = r - mu
        rstd = lax.rsqrt(jnp.mean(d * d, axis=-1, keepdims=True) + LN_EPS)
        xh = d * rstd
        gam = g_ref[0:1, :]
        e = xh * gam + b_ref[0:1, :] - t_ref[...]
        dout = e * (1.0 / D_MODEL)
        dxh = dout * gam
        dr = rstd * (dxh - jnp.mean(dxh, axis=-1, keepdims=True)
                     - xh * jnp.mean(dxh * xh, axis=-1, keepdims=True))
        dr_ref[...] = dr
        dy_ref[...] = _mm_nt(dr, wo_ref[...])
        acc_ref[...] += _rows8([_colsum(dout * xh), _colsum(dout), _colsum(e * e) * (0.5 / D_MODEL)])

        @pl.when(i == nsteps - 1)
        def _():
            acc = acc_ref[...]
            tot = jnp.sum(acc[2:3, :])
            rid = lax.broadcasted_iota(jnp.int32, (8, 1024), 0)
            acc_ref[...] = jnp.where(rid == 3, tot, acc)

    const = lambda shape: pl.BlockSpec(shape, lambda i: (0, 0))
    row = lambda w: pl.BlockSpec((tm, w), lambda i: (i, 0))
    return pl.pallas_call(
        body, name="head", grid=(nsteps,),
        in_specs=[row(2048), row(1024), row(1024), const((2048, 1024)), const((1, 1024)), const((1, 1024))],
        out_specs=[row(1024), row(2048), const((8, 1024))],
        out_shape=[jax.ShapeDtypeStruct((L, D_MODEL), F32), jax.ShapeDtypeStruct((L, 2048), F32),
                   jax.ShapeDtypeStruct((8, 1024), F32)],
        compiler_params=_params(("arbitrary",)),
    )(y, x, target, w_out, ln_g, ln_b)


def _gather_w_in(w_shard):
    R = w_shard.shape[0]
    halves = (pl.ds(0, R // 2), pl.ds(R // 2, R // 2))
    any_spec = pl.BlockSpec(memory_space=pl.ANY)

    def body(in_ref, out_ref, send_sems, recv_sems, local_sem):
        x, y, c = _position()

        def slot(p, half=None):
            s = out_ref.at[_index(*p)]
            return s if half is None else s.at[halves[half]]

        def same_core(p):
            return (p[0], p[1], c)

        def other_core(p):
            return (p[0], p[1], 1 - c)

        me, xn, yn, dg = (x, y), (1 - x, y), (x, 1 - y), (1 - x, 1 - y)

        def copy(k, dst, to, src=None):
            return _remote(dst if src is None else src, dst, send_sems.at[k], recv_sems.at[k], to)

        local = pltpu.make_async_copy(in_ref, slot(same_core(me)), local_sem)
        local.start()
        own = [copy(0, slot(same_core(me)), other_core(me), in_ref), copy(1, slot(same_core(me)), same_core(xn), in_ref),
               copy(2, slot(same_core(me)), same_core(yn), in_ref)]
        for cp in own:
            cp.start()
        copy(1, slot(same_core(xn)), same_core(xn)).wait_recv()
        passed = [copy(4, slot(same_core(xn), 1), same_core(yn)), copy(5, slot(same_core(xn)), other_core(me))]
        for cp in passed:
            cp.start()
        copy(2, slot(same_core(yn)), same_core(yn)).wait_recv()
        more = [copy(3, slot(same_core(yn), 0), same_core(xn)), copy(6, slot(same_core(yn)), other_core(me))]
        for cp in more:
            cp.start()
        passed += more
        for k, half in ((3, 0), (4, 1)):
            copy(k, slot(same_core(dg), half), same_core(xn)).wait_recv()
            fwd = copy(7 + half, slot(same_core(dg), half), other_core(me))
            fwd.start()
            passed.append(fwd)
        copy(0, slot(other_core(me)), other_core(me)).wait_recv()
        copy(5, slot(other_core(xn)), other_core(me)).wait_recv()
        copy(6, slot(other_core(yn)), other_core(me)).wait_recv()
        for half in (0, 1):
            copy(7 + half, slot(other_core(dg), half), other_core(me)).wait_recv()
        for cp in own + passed:
            cp.wait_send()
        local.wait()

    return pl.pallas_call(
        body, name="gather_w_in", in_specs=[any_spec], out_specs=any_spec,
        out_shape=jax.ShapeDtypeStruct((N_DEV,) + w_shard.shape, w_shard.dtype),
        scratch_shapes=[pltpu.SemaphoreType.DMA((9,)), pltpu.SemaphoreType.DMA((9,)), pltpu.SemaphoreType.DMA],
    )(w_shard)


def _input_gradient(d_ssd, d_att, w_ssd, w_att, dr, *, tm, comm=None, stack_hi=None, sum_steps=(2, 4)):
    L = dr.shape[0]
    steps = L // tm

    def matmuls(ds_ref, da_ref, ws_ref, wa_ref, dr_ref, o_ref):
        o_ref[...] = ALPHA * dr_ref[...] + _mm_nt(ds_ref[...], ws_ref[...]) + _mm_nt(da_ref[...], wa_ref[...])

    row = lambda w: pl.BlockSpec((tm, w), lambda i: (i, 0))
    resident = lambda a: pl.BlockSpec(a.shape, lambda i: (0, 0), pipeline_mode=pl.Buffered(1))
    in_specs = [row(S_W), row(A_W), resident(w_ssd), resident(w_att), row(D_MODEL)]
    out_dx = jax.ShapeDtypeStruct((L, D_MODEL), F32)
    args = (d_ssd, d_att, w_ssd, w_att, dr)
    if stack_hi is None:
        return _call(matmuls, comm, name="dx", grid=(steps,), in_specs=in_specs, out_specs=[row(D_MODEL)],
                     out_shape=[out_dx], scratch_shapes=[], args=args)

    block = stack_hi.shape[1:]
    c_in, c_out, rows = len(comm.operands), len(comm.out_shapes), len(comm.flows)

    def body(ds_ref, da_ref, ws_ref, wa_ref, dr_ref, stack_ref, *refs):
        cins, refs = refs[:c_in], refs[c_in:]
        (o_ref, own_ref), refs = refs[:2], refs[2:]
        couts, refs = refs[:c_out], refs[c_out:]
        (theirs_scr, mine_scr, sums_scr, send_sems, recv_sems, local_sems, sum_send_sems, sum_recv_sems,
         swap_send_sems, swap_recv_sems, mine_sems) = refs
        i = pl.program_id(0)
        me = x, y, c = _position()
        start, wait = comm.plan(cins, couts, send_sems, recv_sems, local_sems)
        swaps = [_remote(stack_ref.at[2 * oy + (1 - c)], theirs_scr.at[oy], swap_send_sems.at[oy], swap_recv_sems.at[oy],
                         (x, y, 1 - c)) for oy in range(2)]
        mine = [pltpu.make_async_copy(stack_ref.at[2 * oy + c], mine_scr.at[oy], mine_sems.at[oy]) for oy in range(2)]
        to_owners = []
        for slot, k in enumerate((2, 4, 6)):
            peer = _flip(me, k)
            to_owners.append((peer, _remote(sums_scr.at[peer[1]], couts[0].at[slot], sum_send_sems.at[slot],
                                            sum_recv_sems.at[slot], peer)))

        @pl.when(i == 0)
        def _():
            for cp in [swaps[0]] + mine:
                cp.start()
            start()

        @pl.when(i == 1)
        def _():
            swaps[1].start()

        matmuls(ds_ref, da_ref, ws_ref, wa_ref, dr_ref, o_ref)

        for oy in range(2):
            @pl.when(i == sum_steps[oy])
            def _(oy=oy):
                swaps[oy].wait_recv()
                mine[oy].wait()
                t = mine_scr[oy] + theirs_scr[oy]
                sums_scr[oy] = t.astype(sums_scr.dtype)

                @pl.when(y == oy)
                def _():
                    own_ref[...] = t

                for peer, cp in to_owners:
                    pl.when((peer[0] == 1) & (peer[1] == oy))(cp.start)

        @pl.when(i == steps - 1)
        def _():
            wait()
            for peer, cp in to_owners:
                pl.when(x == 1)(cp.wait_recv)
                pl.when(peer[0] == 1)(cp.wait_send)
            for cp in swaps:
                cp.wait_send()

    io_alias = {6 + ci: 2 + co for ci, co in comm.aliases.items()}
    any_spec = pl.BlockSpec(memory_space=pl.ANY)
    dma = pltpu.SemaphoreType.DMA
    return pl.pallas_call(
        body, name="dx", grid=(steps,), in_specs=in_specs + [any_spec] * (1 + c_in),
        out_specs=[row(D_MODEL), pl.BlockSpec(block, lambda i: (0, 0), pipeline_mode=pl.Buffered(1))] + [any_spec] * c_out,
        out_shape=[out_dx, jax.ShapeDtypeStruct(block, F32)] + list(comm.out_shapes),
        scratch_shapes=[pltpu.VMEM((2,) + block, F32), pltpu.VMEM((2,) + block, F32), pltpu.VMEM((2,) + block, BF16),
                        dma((rows, N_DEV - 1)), dma((rows, N_DEV - 1)), dma((rows,)), dma((3,)), dma((3,)),
                        dma((2,)), dma((2,)), dma((2,))],
        input_output_aliases=io_alias, compiler_params=_params(("arbitrary",)),
    )(*args, stack_hi, *comm.operands)


SHARD_COLS = D_IN_PROJ // N_DEV
SPLIT = N_SSD_REAL - 4 * SHARD_COLS
RELAYOUT_ROWS = 256


def _unpack_w_in(w_all):
    def body(g_ref, ws_ref, wa_ref):
        for j in range(4):
            ws_ref[:, SHARD_COLS * j:SHARD_COLS * (j + 1)] = g_ref[j]
        ws_ref[:, 4 * SHARD_COLS:N_SSD_REAL] = g_ref[4, :, 0:SPLIT]
        ws_ref[:, N_SSD_REAL:S_W] = jnp.zeros((RELAYOUT_ROWS, S_W - N_SSD_REAL), ws_ref.dtype)
        wa_ref[:, 0:SHARD_COLS - SPLIT] = g_ref[4, :, SPLIT:SHARD_COLS]
        for j in range(5, N_DEV):
            lo = SHARD_COLS * (j - 4) - SPLIT
            wa_ref[:, lo:lo + SHARD_COLS] = g_ref[j]

    return pl.pallas_call(
        body, name="unpack_w_in", grid=(D_MODEL // RELAYOUT_ROWS,),
        in_specs=[pl.BlockSpec((N_DEV, RELAYOUT_ROWS, SHARD_COLS), lambda i: (0, i, 0))],
        out_specs=[pl.BlockSpec((RELAYOUT_ROWS, S_W), lambda i: (i, 0)), pl.BlockSpec((RELAYOUT_ROWS, A_W), lambda i: (i, 0))],
        out_shape=[jax.ShapeDtypeStruct((D_MODEL, S_W), w_all.dtype), jax.ShapeDtypeStruct((D_MODEL, A_W), w_all.dtype)],
        compiler_params=_params(("arbitrary",)),
    )(w_all)


def _dw_in(me1, xb, d, half, tail=None, *, tl=1024):
    L, N = d.shape
    steps = L // tl

    def body(me_ref, x_ref, d_ref, *refs):
        if half == 0:
            p_ref, own_ref, tail_ref, acc = refs
        else:
            t_ref, p_ref, acc = refs
        l = pl.program_id(0)

        @pl.when(l == 0)
        def _():
            acc[...] = jnp.zeros_like(acc)

        acc[...] += _mm_tn(x_ref[...], d_ref[...])

        @pl.when(l == steps - 1)
        def _():
            if half == 0:
                me = me_ref[0]

                @pl.when(me >= 4)
                def _():
                    own_ref[...] = jnp.zeros_like(own_ref)

                tail_ref[...] = acc[:, S_DT:S_W]
            for j in range(4):
                if half == 0:
                    pieces = [(0, acc[:, SHARD_COLS * j:SHARD_COLS * (j + 1)])]
                elif j == 0:
                    pieces = [(0, t_ref[:, 4 * SHARD_COLS - S_DT:N_SSD_REAL - S_DT]), (SPLIT, acc[:, 0:SHARD_COLS - SPLIT])]
                else:
                    lo = SHARD_COLS * j - SPLIT
                    pieces = [(0, acc[:, lo:lo + SHARD_COLS])]
                for off, blk in pieces:
                    p_ref[j, :, off:off + blk.shape[1]] = blk.astype(p_ref.dtype)
                    if half == 0:
                        @pl.when(me == j)
                        def _(off=off, blk=blk):
                            own_ref[:, off:off + blk.shape[1]] = blk

    once = pl.Buffered(1)
    whole = lambda shape: pl.BlockSpec(shape, lambda l: (0,) * len(shape), pipeline_mode=once)
    in_specs = [pl.BlockSpec(memory_space=pltpu.SMEM), pl.BlockSpec((tl, D_MODEL), lambda l: (l, 0)),
                pl.BlockSpec((tl, N), lambda l: (l, 0))]
    args = [me1, xb, d]
    if half == 0:
        out_shape = [jax.ShapeDtypeStruct((4, D_MODEL, SHARD_COLS), BF16), jax.ShapeDtypeStruct((D_MODEL, SHARD_COLS), F32),
                     jax.ShapeDtypeStruct((D_MODEL, S_W - S_DT), F32)]
    else:
        in_specs.append(whole(tail.shape))
        args.append(tail)
        out_shape = [jax.ShapeDtypeStruct((4, D_MODEL, SHARD_COLS), F32)]
    return pl.pallas_call(
        body, name="dw_in_%d" % half, grid=(steps,), in_specs=in_specs,
        out_specs=[whole(o.shape) for o in out_shape], out_shape=out_shape,
        scratch_shapes=[pltpu.VMEM((D_MODEL, N), F32)], compiler_params=_params(("arbitrary",)),
    )(*args)


def _adamw_math(w, g, m, v):
    m = ADAM_B1 * m + (1.0 - ADAM_B1) * g
    v = ADAM_B2 * v + (1.0 - ADAM_B2) * (g * g)
    m_hat = m / (1.0 - ADAM_B1 ** ADAM_STEP)
    v_hat = v / (1.0 - ADAM_B2 ** ADAM_STEP)
    delta = -ADAM_LR * (m_hat / (jnp.sqrt(v_hat) + ADAM_EPS) + ADAM_WD * w)
    return delta, m, v


def _adamw_shard(n_recv, g_own, recv, w, m, v, *, rows, name):
    R, C = g_own.shape

    def body(n_ref, g_ref, r_ref, w_ref, m_ref, v_ref, go_ref, d_ref, mo_ref, vo_ref):
        g = g_ref[...]
        for k in range(N_DEV - 1):
            g = g + jnp.where(k < n_ref[0], r_ref[k].astype(F32), 0.0)
        d, mn, vn = _adamw_math(w_ref[...], g, m_ref[...], v_ref[...])
        go_ref[...] = g
        d_ref[...] = d
        mo_ref[...] = mn
        vo_ref[...] = vn

    blk = pl.BlockSpec((rows, C), lambda i: (i, 0))
    return pl.pallas_call(
        body, name=name, grid=(R // rows,),
        in_specs=[pl.BlockSpec(memory_space=pltpu.SMEM), blk,
                  pl.BlockSpec((N_DEV - 1, rows, C), lambda i: (0, i, 0)), blk, blk, blk],
        out_specs=[blk] * 4, out_shape=[jax.ShapeDtypeStruct((R, C), F32)] * 4,
        compiler_params=_params(("arbitrary",)),
    )(n_recv, g_own, recv, w, m, v)


def _minor_rows_view(a):
    return jnp.transpose(a, (2, 0, 1)).reshape(SHARD_COLS * 8, 128)


def _from_minor_rows_view(v):
    return jnp.transpose(v.reshape(SHARD_COLS, 8, 128), (1, 2, 0)).reshape(1, D_MODEL, SHARD_COLS)


def _adamw_w_in(n_recv, g_lo, g_hi, recv, w, m, v):
    C = SHARD_COLS
    pad = -C % 128

    def body(n_ref, lo_ref, hi_ref, r_ref, w_ref, m_ref, v_ref, go_ref, d_ref, mo_ref, vo_ref):
        for q in range(D_MODEL // 128):
            band = pl.ds(q * 128, 128)
            g = jnp.where(n_ref[0] == N_DEV - 1, lo_ref[band, :], hi_ref[band, :])
            for k in range(N_DEV - 1):
                g = g + jnp.where(k < n_ref[0], r_ref[k, band, :].astype(F32), 0.0)
            g = jnp.pad(g, ((0, 0), (0, pad))).T[0:C]
            rows = pl.ds(q, C, stride=8)
            d, mn, vn = _adamw_math(w_ref[rows, :], g, m_ref[rows, :], v_ref[rows, :])
            go_ref[rows, :] = g
            d_ref[rows, :] = d
            mo_ref[rows, :] = mn
            vo_ref[rows, :] = vn

    return pl.pallas_call(
        body, name="adamw_w_in", out_shape=[jax.ShapeDtypeStruct(w.shape, F32)] * 4,
        in_specs=[pl.BlockSpec(memory_space=pltpu.SMEM)] + [pl.BlockSpec(memory_space=pltpu.VMEM)] * 6,
        out_specs=[pl.BlockSpec(memory_space=pltpu.VMEM)] * 4,
        compiler_params=_params(),
    )(n_recv, g_lo, g_hi, recv, w, m, v)


SMALL = ("conv_b", "dt_bias", "a_log", "d_skip", "ssd_norm_w", "attn_sinks", "ln_g", "ln_b")


def _adamw_small(gathered, params):
    n_p = len(SMALL)

    def body(*refs):
        acc = []
        for r in refs[:5]:
            t = r[0]
            for k in range(1, N_DEV):
                t = t + r[k]
            acc.append(t)
        head, conv, norm, scal, sink = acc
        grads = dict(conv_b=conv[4:5, :], dt_bias=scal[0:1, 0:N_HEADS], a_log=scal[1:2, 0:N_HEADS],
                     d_skip=scal[2:3, 0:N_HEADS], ssd_norm_w=norm[0:1, :], attn_sinks=sink[0:1, 0:N_HEADS],
                     ln_g=head[0:1, :], ln_b=head[1:2, :])
        wmv = refs[5:5 + 3 * n_p]
        outs = refs[5 + 3 * n_p:]
        outs[0][...] = head[3:4, 0:1]
        outs[1][...] = conv[0:4, :]
        for i, name in enumerate(SMALL):
            w_ref, m_ref, v_ref = wmv[3 * i:3 * i + 3]
            g = grads[name]
            d, mn, vn = _adamw_math(w_ref[...], g, m_ref[...], v_ref[...])
            for o_ref, val in zip(outs[2 + 4 * i:6 + 4 * i], (g, d, mn, vn)):
                o_ref[...] = val

    flat = [a for name in SMALL for a in params[name]]
    out_shape = [jax.ShapeDtypeStruct((1, 1), F32), jax.ShapeDtypeStruct((4, D_XBC), F32)]
    for name in SMALL:
        out_shape += [jax.ShapeDtypeStruct(params[name][0].shape, F32)] * 4
    res = pl.pallas_call(body, name="adamw_small", out_shape=out_shape, compiler_params=_params())(*gathered, *flat)
    return res[0], res[1], {name: res[2 + 4 * i:6 + 4 * i] for i, name in enumerate(SMALL)}


def _adamw_plain(g, w, m, v):
    def body(g_ref, w_ref, m_ref, v_ref, d_ref, mo_ref, vo_ref):
        d, mn, vn = _adamw_math(w_ref[...], g_ref[...], m_ref[...], v_ref[...])
        d_ref[...] = d
        mo_ref[...] = mn
        vo_ref[...] = vn

    return pl.pallas_call(
        body, name="adamw_conv_w", out_shape=[jax.ShapeDtypeStruct(w.shape, F32)] * 3,
        compiler_params=_params(),
    )(g, w, m, v)


def _lane_pattern(fn):
    return np.asarray([fn(l % HEAD_DIM) for l in range(128)], np.float32)


ROPE_INV = _lane_pattern(lambda r: ROPE_THETA ** (-2.0 * (r % 8) / ROPE_DIM) if r < ROPE_DIM else 0.0)
ROPE_SIN_A = _lane_pattern(lambda r: 1.0 if 8 <= r < ROPE_DIM else 0.0)
ROPE_SIN_B = _lane_pattern(lambda r: -1.0 if r < 8 else 0.0)


def _rope_tables(positions):
    ang = positions.astype(F32)[:, None] * ROPE_INV[None, :]
    sn = jnp.sin(ang)
    return jnp.concatenate([jnp.cos(ang), sn * ROPE_SIN_A[None, :], sn * ROPE_SIN_B[None, :]], axis=1)


def _expansion():
    E = np.arange(1024)[None, :] // HEAD_DIM == np.arange(128)[:, None]
    return jnp.asarray(E, BF16), jnp.asarray(E.T, BF16)


def _ssd_args(conv_w, conv_b, dt_bias, a_log, d_skip, norm_w, E):
    return (conv_w, conv_b, dt_bias.reshape(-1), a_log.reshape(-1), d_skip.reshape(-1), norm_w, E)


def kernel(x, positions, w_in, conv_w, conv_b, dt_bias, a_log, d_skip, ssd_norm_w, attn_sinks, w_out, ln_g, ln_b, loss_target, m_w_in, m_conv_w, m_conv_b, m_dt_bias, m_a_log, m_d_skip, m_ssd_norm_w, m_attn_sinks, m_w_out, m_ln_g, m_ln_b, v_w_in, v_conv_w, v_conv_b, v_dt_bias, v_a_log, v_d_skip, v_ssd_norm_w, v_attn_sinks, v_w_out, v_ln_g, v_ln_b):
    me = _index(*_position())
    me1 = me.reshape(1).astype(jnp.int32)
    x0, target = x[0], loss_target[0]
    bf16_shard = lambda shape: jax.ShapeDtypeStruct(shape, BF16)
    E, ET = _expansion()
    tabs = _rope_tables(positions[0])
    sinks = attn_sinks.reshape(-1)

    w_ssd, w_att = _unpack_w_in(_gather_w_in(w_in[0].astype(BF16)))
    gather_conv_w = _Hosted([conv_w[0]], [jax.ShapeDtypeStruct((N_DEV,) + conv_w.shape[1:], F32)],
                            [_Flow("gather", 0, 0)])

    proj_ssd, proj_att, xb, conv_w_all = _in_proj(x0, w_ssd, w_att, tm=512, comm=gather_conv_w)
    conv_w_f = jnp.transpose(conv_w_all, (1, 0, 2)).reshape(4, D_XBC)
    ssd_args = _ssd_args(conv_w_f, conv_b, dt_bias, a_log, d_skip, ssd_norm_w, E)
    gather_w_out = _Hosted([w_out[0].astype(BF16)], [bf16_shard((N_DEV, 256, D_MODEL))], [_Flow("gather", 0, 0)])
    y, ypre, hprev, pre, w_out_all = _mixer_forward(proj_ssd, proj_att, tabs, sinks, *ssd_args, comm=gather_w_out)
    w_out_f = w_out_all.reshape(2 * D_MODEL, D_MODEL)
    dr, dy, acc_head = _head(y, x0, target, w_out_f, ln_g, ln_b, tm=512)

    dw_out, dw_out_bf16 = _matmul_tn(y, dr, tl=1024, tn=D_MODEL, name="dw_out", emit_bf16=True)
    own_out = lax.dynamic_index_in_dim(dw_out.reshape(N_DEV, 256, D_MODEL), me, axis=0, keepdims=False)
    send_out = _Hosted([dw_out_bf16.reshape(N_DEV, 256, D_MODEL)], [bf16_shard((N_DEV - 1, 256, D_MODEL))],
                       [_Flow("exchange", 0, 0)])
    d_ssd, acc_cw, acc_w, acc_s, recv_out = _ssd_backward(proj_ssd, hprev, ypre, pre, dy, *ssd_args, ET, comm=send_out)
    parts_lo, own_lo, dw_dt_block = _dw_in(me1, xb, d_ssd, 0)
    recv_shape = bf16_shard((N_DEV - 1, D_MODEL, SHARD_COLS))
    send_lo = _Hosted([parts_lo], [recv_shape], [_Flow("exchange", 0, 0, target_x=0, target_c=0)])
    d_att, dsink, recv_in = _swa_backward(proj_att, tabs, sinks, dy, comm=send_lo)
    (stack_hi,) = _dw_in(me1, xb, d_att, 1, dw_dt_block)
    accs = [acc_head, acc_cw, acc_w, acc_s, dsink]
    send_rest = _Hosted([recv_in, parts_lo] + accs,
                        [recv_shape] + [jax.ShapeDtypeStruct((N_DEV,) + a.shape, F32) for a in accs],
                        [_Flow("exchange", 1, 0, target_x=0, target_c=1)] + [_Flow("gather", 2 + i, 1 + i) for i in range(5)],
                        aliases={0: 0})
    dx, own_hi, recv_in, *gathered = _input_gradient(d_ssd, d_att, w_ssd, w_att, dr, tm=256, comm=send_rest,
                                                     stack_hi=stack_hi)
    n_recv_in = jnp.where(me < 4, N_DEV - 1, 3).reshape(1).astype(jnp.int32)
    n_recv_out = jnp.full((1,), N_DEV - 1, jnp.int32)

    g_in, d_in, nm_in, nv_in = [_from_minor_rows_view(r) for r in _adamw_w_in(
        n_recv_in, own_lo, own_hi, recv_in, _minor_rows_view(w_in), _minor_rows_view(m_w_in), _minor_rows_view(v_w_in))]
    g_out, d_out, nm_out, nv_out = _adamw_shard(n_recv_out, own_out, recv_out, w_out[0], m_w_out[0], v_w_out[0],
                                                rows=256, name="adamw_w_out")
    loss, g_conv_w, small = _adamw_small(gathered, dict(
        conv_b=(conv_b, m_conv_b, v_conv_b), dt_bias=(dt_bias, m_dt_bias, v_dt_bias), a_log=(a_log, m_a_log, v_a_log),
        d_skip=(d_skip, m_d_skip, v_d_skip), ssd_norm_w=(ssd_norm_w, m_ssd_norm_w, v_ssd_norm_w),
        attn_sinks=(attn_sinks, m_attn_sinks, v_attn_sinks), ln_g=(ln_g, m_ln_g, v_ln_g), ln_b=(ln_b, m_ln_b, v_ln_b)))
    g_cw = lax.dynamic_slice_in_dim(g_conv_w, me * (D_XBC // N_DEV), D_XBC // N_DEV, axis=1)
    d_cw, nm_cw, nv_cw = _adamw_plain(g_cw, conv_w[0], m_conv_w[0], v_conv_w[0])

    def leaves(i, big_in, cw, big_out):
        mid = [small[k][i] for k in ("conv_b", "dt_bias", "a_log", "d_skip", "ssd_norm_w", "attn_sinks")]
        return [big_in, cw[None]] + mid + [big_out[None], small["ln_g"][i], small["ln_b"][i]]

    return (loss.reshape(()), dx[None], *leaves(0, g_in, g_cw, g_out), *leaves(1, d_in, d_cw, d_out),
            *leaves(2, nm_in, nm_cw, nm_out), *leaves(3, nv_in, nv_cw, nv_out))
```

```python
import jax
import jax.numpy as jnp
from jax import lax
from jax.experimental import pallas as pl
from jax.experimental.pallas import tpu as pltpu
import numpy as np

F32 = jnp.float32
BF16 = jnp.bfloat16
_MXU = jnp.bfloat16

N_DEV = 8
D_MODEL = 1024
D_SSD = 1024
D_ATT = 1024
HEAD_DIM = 64
N_HEADS = 16
SSD_GROUPS = 2
KV_HEADS = 4
CHUNK = 128
D_XBC = 1536
D_IN_PROJ = 5136
ROPE_DIM = 16
ROPE_THETA = 500000.0
ALPHA = (2.0 * 1) ** 0.25
LN_EPS = 1e-5
RMS_EPS = 1e-5
ATT_SCALE = HEAD_DIM ** -0.5
NEG = -1e30

S_Z, S_XS, S_B, S_C, S_DT, S_W = 0, 1024, 2048, 2304, 2560, 2816
N_SSD_REAL = 2576
A_Q, A_K, A_V, A_G, A_W = 0, 1024, 1280, 1536, 2560

ADAM_LR = 0.001
ADAM_B1 = 0.9
ADAM_B2 = 0.999
ADAM_EPS = 1e-08
ADAM_WD = 0.01
ADAM_STEP = 10

VMEM_LIMIT = 48 * 1024 * 1024
MESH = pl.DeviceIdType.MESH


def _params(sem=None):
    return pltpu.CompilerParams(dimension_semantics=sem, vmem_limit_bytes=VMEM_LIMIT)


def _mm(a, b):
    return jnp.dot(a.astype(_MXU), b.astype(_MXU), preferred_element_type=F32)


def _mm_nt(a, b):
    return lax.dot_general(a.astype(_MXU), b.astype(_MXU), (((1,), (1,)), ((), ())),
                           preferred_element_type=F32)


def _mm_tn(a, b):
    return lax.dot_general(a.astype(_MXU), b.astype(_MXU), (((0,), (0,)), ((), ())),
                           preferred_element_type=F32)


def _split3(v):
    hi = v.astype(BF16)
    r = v - hi.astype(F32)
    mid = r.astype(BF16)
    lo = (r - mid.astype(F32)).astype(BF16)
    return hi, mid, lo


def _mm_exact_r(v, p01):
    hi, mid, lo = _split3(v)
    d = lambda a: jnp.dot(a, p01, preferred_element_type=F32)
    return d(hi) + d(mid) + d(lo)


def _mm_exact_l(p01, v):
    hi, mid, lo = _split3(v)
    d = lambda a: jnp.dot(p01, a, preferred_element_type=F32)
    return d(hi) + d(mid) + d(lo)


def _mm_2pass_r(v, p01):
    hi = v.astype(BF16)
    lo = (v - hi.astype(F32)).astype(BF16)
    return jnp.dot(hi, p01, preferred_element_type=F32) + jnp.dot(lo, p01, preferred_element_type=F32)


def _sigmoid(x):
    return 1.0 / (1.0 + jnp.exp(-x))


def _softplus(x):
    e = jnp.exp(-jnp.abs(x))
    u = 1.0 + e
    log1p = jnp.where(u == 1.0, e, jnp.log(u) * (e / (u - 1.0)))
    return jnp.maximum(x, 0.0) + log1p


def _rows8(rows):
    n = rows[0].shape[1]
    rid = lax.broadcasted_iota(jnp.int32, (8, n), 0)
    out = jnp.zeros((8, n), F32)
    for k, r in enumerate(rows):
        out = out + jnp.where(rid == k, r, 0.0)
    return out


def _colsum(a):
    return jnp.sum(a, axis=0, keepdims=True)


def _in_proj(x, w_ssd, w_att, *, tm, comm=None):
    L, K = x.shape

    def body(x_ref, ws_ref, wa_ref, ps_ref, pa_ref, xb_ref):
        xb = x_ref[...].astype(_MXU)
        xb_ref[...] = xb
        ps_ref[...] = jnp.dot(xb, ws_ref[...], preferred_element_type=F32)
        pa_ref[...] = jnp.dot(xb, wa_ref[...], preferred_element_type=F32)

    row = lambda w: pl.BlockSpec((tm, w), lambda i: (i, 0))
    resident = lambda a: pl.BlockSpec(a.shape, lambda i: (0, 0), pipeline_mode=pl.Buffered(1))
    return _call(
        body, comm, name="in_proj", grid=(L // tm,),
        in_specs=[row(K), resident(w_ssd), resident(w_att)], out_specs=[row(S_W), row(A_W), row(K)],
        out_shape=[jax.ShapeDtypeStruct((L, S_W), F32), jax.ShapeDtypeStruct((L, A_W), F32),
                   jax.ShapeDtypeStruct((L, K), _MXU)],
        scratch_shapes=[], args=(x, w_ssd, w_att))


def _matmul_tn(a, g, *, tl, tn, name, emit_bf16=False):
    L, M = a.shape
    N = g.shape[1]
    last = L // tl - 1

    def body(a_ref, g_ref, o_ref, *rest):
        @pl.when(pl.program_id(1) == 0)
        def _():
            o_ref[...] = jnp.zeros_like(o_ref)

        o_ref[...] += _mm_tn(a_ref[...], g_ref[...])
        if emit_bf16:
            @pl.when(pl.program_id(1) == last)
            def _():
                rest[0][...] = o_ref[...].astype(BF16)

    spec = pl.BlockSpec((M, tn), lambda j, l: (0, j))
    res = pl.pallas_call(
        body, name=name, grid=(N // tn, L // tl),
        in_specs=[pl.BlockSpec((tl, M), lambda j, l: (l, 0)), pl.BlockSpec((tl, tn), lambda j, l: (l, j))],
        out_specs=[spec, spec] if emit_bf16 else [spec],
        out_shape=[jax.ShapeDtypeStruct((M, N), F32)] + ([jax.ShapeDtypeStruct((M, N), BF16)] if emit_bf16 else []),
        compiler_params=_params(("arbitrary", "arbitrary")),
    )(a, g)
    return res if emit_bf16 else res[0]


def _position():
    return lax.axis_index("x"), lax.axis_index("y"), lax.axis_index("c")


def _index(px, py, pc):
    return 4 * px + 2 * py + pc


def _flip(pos, k):
    x, y, c = pos
    return ((1 - x) if (k >> 2) & 1 else x, (1 - y) if (k >> 1) & 1 else y, (1 - c) if k & 1 else c)


def _when(cond, fn):
    if cond is True:
        fn()
    else:
        pl.when(cond)(fn)


def _remote(src, dst, send_sem, recv_sem, peer):
    return pltpu.make_async_remote_copy(src_ref=src, dst_ref=dst, send_sem=send_sem, recv_sem=recv_sem,
                                        device_id=peer, device_id_type=MESH)


class _Flow:
    def __init__(self, kind, operand, result, target_x=None, target_c=None, rows=None):
        self.kind, self.operand, self.result, self.target_x, self.target_c = kind, operand, result, target_x, target_c
        self.rows = rows

    def owns(self, pos):
        if self.target_x is None:
            return True
        cond = pos[0] == self.target_x
        return cond if self.target_c is None else cond & (pos[2] == self.target_c)


class _Hosted:
    def __init__(self, operands, out_shapes, flows, aliases=None):
        self.operands, self.out_shapes, self.flows = operands, out_shapes, flows
        self.aliases = aliases or {}

    def plan(self, ins, outs, send_sems, recv_sems, local_sems):
        me = _position()
        mi = _index(*me)
        sends, recvs, locals_ = [], [], []
        for row, f in enumerate(self.flows):
            src, dst = ins[f.operand], outs[f.result]
            for k in range(1, N_DEV):
                peer = _flip(me, k)
                sems = (send_sems.at[row, k - 1], recv_sems.at[row, k - 1])
                if f.kind == "exchange":
                    owner = _index(*peer) if f.target_x is None else 2 * peer[1] + peer[2]
                    part = (slice(None),) if f.rows is None else (pl.ds(*f.rows),)
                    cp = _remote(src.at[(owner,) + part], dst.at[(k - 1,) + part], *sems, peer)
                    sends.append((f.owns(peer), cp))
                    recvs.append((f.owns(me), cp))
                else:
                    sends.append((True, _remote(src, dst.at[mi], *sems, peer)))
                    recvs.append((True, _remote(src, dst.at[_index(*peer)], *sems, peer)))
            if f.kind == "gather":
                locals_.append(pltpu.make_async_copy(src, dst.at[mi], local_sems.at[row]))

        def start():
            for cp in locals_:
                cp.start()
            for cond, cp in sends:
                _when(cond, cp.start)

        def wait():
            for cond, cp in recvs:
                _when(cond, cp.wait_recv)
            for cond, cp in sends:
                _when(cond, cp.wait_send)
            for cp in locals_:
                cp.wait()

        return start, wait


def _call(body, comm, *, name, grid, in_specs, out_specs, out_shape, scratch_shapes, args, aliases=None):
    io_alias = dict(aliases or {})
    semantics = ("arbitrary",) * len(grid)
    if comm is None:
        return pl.pallas_call(body, name=name, grid=grid, in_specs=in_specs, out_specs=out_specs, out_shape=out_shape,
                              scratch_shapes=scratch_shapes, input_output_aliases=io_alias,
                              compiler_params=_params(semantics))(*args)
    n_in, n_out, n_scr = len(args), len(out_shape), len(scratch_shapes)
    c_in, c_out, rows = len(comm.operands), len(comm.out_shapes), len(comm.flows)

    def hosted(*refs):
        ins, refs = refs[:n_in], refs[n_in:]
        cins, refs = refs[:c_in], refs[c_in:]
        outs, refs = refs[:n_out], refs[n_out:]
        couts, refs = refs[:c_out], refs[c_out:]
        scr, (send_sems, recv_sems, local_sems) = refs[:n_scr], refs[n_scr:]
        start, wait = comm.plan(cins, couts, send_sems, recv_sems, local_sems)
        ids = [pl.program_id(d) for d in range(len(grid))]
        first, last = ids[0] == 0, ids[0] == grid[0] - 1
        for d in range(1, len(grid)):
            first, last = first & (ids[d] == 0), last & (ids[d] == grid[d] - 1)
        pl.when(first)(start)
        body(*ins, *outs, *scr)
        pl.when(last)(wait)

    for ci, co in comm.aliases.items():
        io_alias[n_in + ci] = n_out + co
    any_spec = pl.BlockSpec(memory_space=pl.ANY)
    sems = [pltpu.SemaphoreType.DMA((rows, N_DEV - 1)), pltpu.SemaphoreType.DMA((rows, N_DEV - 1)),
            pltpu.SemaphoreType.DMA((rows,))]
    return pl.pallas_call(
        hosted, name=name, grid=grid, in_specs=list(in_specs) + [any_spec] * c_in,
        out_specs=list(out_specs) + [any_spec] * c_out, out_shape=list(out_shape) + list(comm.out_shapes),
        scratch_shapes=list(scratch_shapes) + sems, input_output_aliases=io_alias,
        compiler_params=_params(semantics))(*args, *comm.operands)


def _head_row(ref, width, rep):
    hid = lax.broadcasted_iota(jnp.int32, (1, width), 1) // rep
    row = jnp.zeros((1, width), F32)
    for h in range(N_HEADS):
        row = jnp.where(hid == h, ref[h], row)
    return row


def _rows_from_above(u_b, s, ext_scr, row, col):
    down = (row - col == s).astype(_MXU)
    return jnp.concatenate([ext_scr[8 - s:16 - s, :], jnp.dot(down, u_b, preferred_element_type=F32)[8:128]], axis=0)


def _ssd_recompute(first, p_ref, halo_ref, cw_ref, cb_ref, dtb_ref, alog_ref, e_ref, ext_scr, pre=None):
    row = lax.broadcasted_iota(jnp.int32, (128, 128), 0)
    col = lax.broadcasted_iota(jnp.int32, (128, 128), 1)
    ext_scr[0:8, :] = jnp.where(first, 0.0, halo_ref[:, S_XS:S_DT])
    if pre is not None:
        ext_scr[8:16, :] = p_ref[0:8, S_XS:S_DT]
    else:
        ext_scr[8:136, :] = p_ref[:, S_XS:S_DT]
        cw = cw_ref[...]
        pre = (cb_ref[0:1, :] + cw[3:4, :] * ext_scr[8:136, :] + cw[2:3, :] * ext_scr[7:135, :]
               + cw[1:2, :] * ext_scr[6:134, :] + cw[0:1, :] * ext_scr[5:133, :])
    sg = _sigmoid(pre)
    act = pre * sg
    lane = lax.broadcasted_iota(jnp.int32, (1, 128), 1)
    A = jnp.where(lane < N_HEADS, -jnp.exp(_head_row(alog_ref, 128, 1)), 0.0)
    raw = p_ref[:, S_DT:S_DT + 128] + _head_row(dtb_ref, 128, 1)
    dt = _softplus(raw)
    dA = dt * A
    tril = (row >= col).astype(BF16)
    acs = _mm_exact_l(tril, dA)
    last = acs[127:128, :]
    ds = jnp.exp(last - acs)
    eo = jnp.exp(acs)
    E = e_ref[...]
    ex = _mm_2pass_r(jnp.concatenate([dt, ds, eo], axis=0), E)
    dt_e, ds_e, eo_e = ex[0:128], ex[128:256], ex[256:384]
    xs_c = act[:, 0:1024]
    X = xs_c * dt_e
    return dict(pre=pre, sg=sg, xs_c=xs_c, Bc=act[:, 1024:1280], Cc=act[:, 1280:1536], A=A, raw=raw, dt=dt,
                acs=acs, acsT=acs.T, eo_e=eo_e, ds_e=ds_e, dt_e=dt_e, cd_e=eo_e[127:128, :],
                X=X, Xd=X * ds_e, row=row, col=col)


def _split_halves(t):
    lo = _lo_half(CHUNK)
    return jnp.concatenate([jnp.where(lo, t, 0.0), jnp.where(lo, 0.0, t)], axis=0)


def _ssd_core(R, hprev):
    causal = R["row"] >= R["col"]
    acs, acsT, X = R["acs"], R["acsT"], R["X"]
    ydiag, yoff, snew = [], [], []
    for g in range(SSD_GROUPS):
        Bg = R["Bc"][:, g * 128:(g + 1) * 128]
        Cg = R["Cc"][:, g * 128:(g + 1) * 128]
        cols = slice(g * 512, (g + 1) * 512)
        CB = _mm_nt(Cg, Bg)
        snew.append(_mm_tn(Bg, R["Xd"][:, cols]))
        yoff.append(_mm(Cg, hprev[:, cols]))
        for j in range(4):
            h0 = g * 8 + 2 * j
            ms = [CB * jnp.exp(jnp.where(causal, acs[:, h:h + 1] - acsT[h:h + 1, :], NEG)) for h in (h0, h0 + 1)]
            ydiag.append(_mm(jnp.concatenate(ms, axis=1), _split_halves(X[:, h0 * HEAD_DIM:h0 * HEAD_DIM + 128])))
    Y = jnp.concatenate(ydiag, axis=1) + jnp.concatenate(yoff, axis=1) * R["eo_e"]
    return Y, jnp.concatenate(snew, axis=1)


def _ssd_forward_step(p_ref, halo_ref, cw_ref, cb_ref, dtb_ref, alog_ref, dsk_ref, nw_ref, e_ref,
                      y_ref, ypre_ref, hprev_ref, pre_ref, h_scr, ext_scr):
    c = pl.program_id(0)
    first = c == 0

    @pl.when(first)
    def _():
        h_scr[...] = jnp.zeros_like(h_scr)

    R = _ssd_recompute(first, p_ref, halo_ref, cw_ref, cb_ref, dtb_ref, alog_ref, e_ref, ext_scr)
    hprev = h_scr[...]
    hprev_ref[...] = hprev
    pre_ref[...] = R["pre"]
    Y, snew = _ssd_core(R, hprev)
    h_scr[...] = hprev * R["cd_e"] + snew
    Y = Y + _head_row(dsk_ref, D_SSD, HEAD_DIM) * R["xs_c"]
    ypre_ref[...] = Y
    z = p_ref[:, S_Z:S_Z + 1024]
    yf = Y * (z * _sigmoid(z))
    outs = []
    for g in range(SSD_GROUPS):
        yg = yf[:, g * 512:(g + 1) * 512]
        r = lax.rsqrt(jnp.mean(yg * yg, axis=-1, keepdims=True) + RMS_EPS)
        outs.append(yg * r)
    y_ref[:, 0:D_SSD] = (jnp.concatenate(outs, axis=1) * nw_ref[0:1, :]).astype(y_ref.dtype)


def _ssd_backward(proj_ssd, hprev_all, ypre, pre, dy, conv_w, conv_b, dt_bias, a_log, d_skip, norm_w, E, ET, comm=None):
    L = proj_ssd.shape[0]
    nc = L // CHUNK

    def body(p_ref, halo_ref, hprev_ref, ypre_ref, pre_ref, dy_ref, cw_ref, cb_ref, dtb_ref, alog_ref, dsk_ref, nw_ref, e_ref,
             et_ref, dp_ref, acc_cw_ref, acc_w_ref, acc_s_ref, dh_scr, ext_scr, ext2_scr, nxt_scr):
        i = pl.program_id(0)
        c = nc - 1 - i
        first = c == 0

        @pl.when(i == 0)
        def _():
            dh_scr[...] = jnp.zeros_like(dh_scr)
            nxt_scr[...] = jnp.zeros_like(nxt_scr)
            acc_cw_ref[...] = jnp.zeros_like(acc_cw_ref)
            acc_w_ref[...] = jnp.zeros_like(acc_w_ref)
            acc_s_ref[...] = jnp.zeros_like(acc_s_ref)

        R = _ssd_recompute(first, p_ref, halo_ref, cw_ref, cb_ref, dtb_ref, alog_ref, e_ref, ext_scr, pre_ref[...])
        hprev = hprev_ref[...]
        xs_c, X, Xd = R["xs_c"], R["X"], R["Xd"]
        acs, acsT = R["acs"], R["acsT"]
        ET = et_ref[...]
        dsk = _head_row(dsk_ref, D_SSD, HEAD_DIM)
        Y = ypre_ref[...]

        z = p_ref[:, S_Z:S_Z + 1024]
        sz = _sigmoid(z)
        silz = z * sz
        yf = Y * silz
        dyv = dy_ref[...]
        nw = nw_ref[0:1, :]
        dyf_parts, dnw_parts = [], []
        for g in range(SSD_GROUPS):
            cols = slice(g * 512, (g + 1) * 512)
            yg = yf[:, cols]
            r = lax.rsqrt(jnp.mean(yg * yg, axis=-1, keepdims=True) + RMS_EPS)
            yn = yg * r
            dyn = dyv[:, cols] * nw[:, cols]
            dnw_parts.append(_colsum(dyv[:, cols] * yn))
            dyf_parts.append(r * (dyn - yn * jnp.mean(dyn * yn, axis=-1, keepdims=True)))
        dyf = jnp.concatenate(dyf_parts, axis=1)
        dY = dyf * silz
        dz = dyf * Y * (sz * (1.0 + z * (1.0 - sz)))

        dhn = dh_scr[...]
        dYo = dY * R["eo_e"]
        causal = R["row"] >= R["col"]
        dacs = jnp.zeros((128, 128), F32)
        dacs_t = jnp.zeros((128, 128), F32)
        dxdiag, dxd, dhprev, dBs, dCs, yoff = [], [], [], [], [], []
        for g in range(SSD_GROUPS):
            Bg = R["Bc"][:, g * 128:(g + 1) * 128]
            Cg = R["Cc"][:, g * 128:(g + 1) * 128]
            cols = slice(g * 512, (g + 1) * 512)
            CB = _mm_nt(Cg, Bg)
            dCB = jnp.zeros((128, 128), F32)
            for j in range(4):
                h0 = g * 8 + 2 * j
                pc = slice(h0 * HEAD_DIM, h0 * HEAD_DIM + 128)
                dYst = _split_halves(dY[:, pc])
                dMst = _mm_nt(dYst, X[:, pc])
                mts = []
                for a, h in enumerate((h0, h0 + 1)):
                    acol = acs[:, h:h + 1]
                    arow = acsT[h:h + 1, :]
                    Lm = jnp.exp(jnp.where(causal, acol - arow, NEG))
                    M = CB * Lm
                    dM = dMst[a * 128:(a + 1) * 128]
                    dCB = dCB + dM * Lm
                    G = dM * M
                    dacs = dacs + jnp.where(R["col"] == h, jnp.sum(G, axis=1, keepdims=True), 0.0)
                    dacs_t = dacs_t + jnp.where(R["row"] == h, jnp.sum(G, axis=0, keepdims=True), 0.0)
                    mts.append(M.T)
                dxdiag.append(_mm(jnp.concatenate(mts, axis=1), dYst))
            dS = dhn[:, cols]
            dxd.append(_mm(Bg, dS))
            yoff.append(_mm(Cg, hprev[:, cols]))
            dhprev.append(_mm_tn(Cg, dYo[:, cols]))
            dCs.append(_mm_nt(dYo[:, cols], hprev[:, cols]) + _mm(dCB, Bg))
            dBs.append(_mm_tn(dCB, Cg) + _mm_nt(Xd[:, cols], dS))
        Yoff = jnp.concatenate(yoff, axis=1) * R["eo_e"]
        dXd = jnp.concatenate(dxd, axis=1)
        dX = jnp.concatenate(dxdiag, axis=1) + dXd * R["ds_e"]
        t_state = dXd * Xd
        dacs = dacs + _mm_2pass_r(dY * Yoff - t_state, ET) - dacs_t.T
        v_last = _colsum(t_state + dhn * hprev * R["cd_e"])
        dlast = _mm_exact_r(jnp.broadcast_to(v_last, (8, 1024)), ET)[0:1, :]
        dacs = dacs + jnp.where(R["row"] == 127, dlast, 0.0)
        triu = (R["col"] >= R["row"]).astype(BF16)
        da = _mm_exact_l(triu, dacs)
        ddt = da * R["A"] + _mm(dX * xs_c, ET)
        ddt_raw = ddt * _sigmoid(R["raw"])
        dxs_c = dX * R["dt_e"] + dY * dsk
        dh_scr[...] = jnp.concatenate(dhprev, axis=1) + dhn * R["cd_e"]

        dact = jnp.concatenate([dxs_c] + dBs + dCs, axis=1)
        pre, sg = R["pre"], R["sg"]
        dpre = dact * (sg * (1.0 + pre * (1.0 - sg)))
        ext2_scr[0:8, :] = dpre[120:128, :]
        ext2_scr[8:16, :] = nxt_scr[...]
        nxt_scr[...] = dpre[0:8, :]
        cw = cw_ref[...]
        u_b, dpre_b = p_ref[:, S_XS:S_DT].astype(_MXU), dpre.astype(_MXU)
        dxbc = cw[3:4, :] * dpre
        taps = [_colsum(dpre * p_ref[:, S_XS:S_DT])]
        for s in (1, 2, 3):
            up = (R["col"] - R["row"] == s).astype(_MXU)
            d_s = jnp.concatenate([jnp.dot(up, dpre_b, preferred_element_type=F32)[0:120],
                                   ext2_scr[s:8 + s, :]], axis=0)
            dxbc = dxbc + cw[3 - s:4 - s, :] * d_s
            taps.append(_colsum(dpre * _rows_from_above(u_b, s, ext_scr, R["row"], R["col"])))
        acc_cw_ref[...] += _rows8(taps[::-1] + [_colsum(dpre)])
        acc_w_ref[...] += _rows8([jnp.concatenate(dnw_parts, axis=1), _colsum(dY * xs_c)])
        acc_s_ref[...] += _rows8([_colsum(ddt_raw), _colsum(da * R["dt"])])

        lane = lax.broadcasted_iota(jnp.int32, (128, 128), 1)
        dp_ref[:, S_Z:S_Z + 1024] = dz.astype(dp_ref.dtype)
        dp_ref[:, S_XS:S_DT] = dxbc.astype(dp_ref.dtype)
        dp_ref[:, S_DT:S_DT + 128] = jnp.where(lane < N_HEADS, ddt_raw, 0.0).astype(dp_ref.dtype)
        dp_ref[:, S_DT + 128:S_W] = jnp.zeros((128, 128), dp_ref.dtype)

        @pl.when(i == nc - 1)
        def _():
            acc = acc_s_ref[...]
            dskip = _mm_exact_r(acc_w_ref[...], ET)[1:2, :]
            acc_s_ref[...] = _rows8([acc[0:1, :], acc[1:2, :] * R["A"], dskip])

    const = lambda shape: pl.BlockSpec(shape, lambda i: (0, 0))
    smem = pl.BlockSpec(memory_space=pltpu.SMEM)
    rev = lambda i: (nc - 1 - i, 0)
    return _call(
        body, comm, name="ssd_bwd", grid=(nc,),
        in_specs=[pl.BlockSpec((CHUNK, S_W), rev),
                  pl.BlockSpec((8, S_W), lambda i: (jnp.maximum((nc - 1 - i) * 16 - 1, 0), 0)),
                  pl.BlockSpec((128, 1024), rev),
                  pl.BlockSpec((CHUNK, D_SSD), rev),
                  pl.BlockSpec((CHUNK, D_XBC), rev),
                  pl.BlockSpec((CHUNK, D_SSD), rev),
                  const((4, D_XBC)), const((1, D_XBC)), smem, smem, smem, const((1, 1024)),
                  const((128, 1024)), const((1024, 128))],
        out_specs=[pl.BlockSpec((CHUNK, S_W), rev), const((8, D_XBC)), const((8, 1024)), const((8, 128))],
        out_shape=[jax.ShapeDtypeStruct((L, S_W), _MXU), jax.ShapeDtypeStruct((8, D_XBC), F32),
                   jax.ShapeDtypeStruct((8, 1024), F32), jax.ShapeDtypeStruct((8, 128), F32)],
        scratch_shapes=[pltpu.VMEM((128, 1024), F32), pltpu.VMEM((16, D_XBC), F32),
                        pltpu.VMEM((16, D_XBC), F32), pltpu.VMEM((8, D_XBC), F32)],
        args=(proj_ssd, proj_ssd, hprev_all, ypre, pre, dy, conv_w, conv_b, dt_bias, a_log, d_skip, norm_w, E, ET))


def _rope(t, tab):
    cos, sa, sb = tab[:, 0:128], tab[:, 128:256], tab[:, 256:384]
    outs = []
    for i in range(t.shape[1] // 128):
        tg = t[:, i * 128:(i + 1) * 128]
        outs.append(tg * cos + pltpu.roll(tg, 8, 1) * sa + pltpu.roll(tg, 120, 1) * sb)
    return jnp.concatenate(outs, axis=1)


def _rope_transposed(d, tab):
    cos, sa, sb = tab[:, 0:128], tab[:, 128:256], tab[:, 256:384]
    outs = []
    for i in range(d.shape[1] // 128):
        dg = d[:, i * 128:(i + 1) * 128]
        outs.append(dg * cos + pltpu.roll(dg * sa, 120, 1) + pltpu.roll(dg * sb, 8, 1))
    return jnp.concatenate(outs, axis=1)


def _lo_half(rows):
    return lax.broadcasted_iota(jnp.int32, (rows, 128), 1) < HEAD_DIM


def _native_half(rows, j):
    lo = _lo_half(rows)
    return lo if j % 2 == 0 else jnp.logical_not(lo)


def _kv_native(t, j):
    p = j // 2
    return jnp.where(_native_half(t.shape[0], j), t[:, p * 128:(p + 1) * 128], 0.0)


def _stack_heads(t, j):
    out = []
    for m in (2 * j, 2 * j + 1):
        pair = t[:, m * 128:(m + 1) * 128]
        swapped = pltpu.roll(pair, HEAD_DIM, 1)
        out += [pair, swapped] if j % 2 == 0 else [swapped, pair]
    return jnp.concatenate(out, axis=0)


def _unstack_heads(s, j):
    out = []
    for m in range(2):
        first, second = s[256 * m:256 * m + 128], s[256 * m + 128:256 * m + 256]
        if j % 2 == 0:
            out.append(first + pltpu.roll(second, HEAD_DIM, 1))
        else:
            out.append(pltpu.roll(first, HEAD_DIM, 1) + second)
    return jnp.concatenate(out, axis=1)


def _keep_native(r, j):
    return jnp.where(_native_half(r.shape[0], j), r, 0.0)


def _sink_row(sink_ref, j):
    hid = lax.broadcasted_iota(jnp.int32, (1, 4 * CHUNK), 1) // CHUNK
    row = jnp.zeros((1, 4 * CHUNK), F32)
    for hh in range(4):
        row = jnp.where(hid == hh, sink_ref[4 * j + hh], row)
    return row


def _from_current():
    si = lax.broadcasted_iota(jnp.int32, (CHUNK, 4 * CHUNK), 0)
    qi = lax.broadcasted_iota(jnp.int32, (CHUNK, 4 * CHUNK), 1) % CHUNK
    return si <= qi


def _fold(full, from_cur, pen=0.0):
    return jnp.where(from_cur, full[CHUNK:2 * CHUNK], full[0:CHUNK] + pen)


def _unfold(t, from_cur):
    c = jnp.where(from_cur, t, 0.0)
    return jnp.concatenate([t - c, c], axis=0)


def _softmax_sink(s, sink):
    mx = jnp.maximum(jnp.max(s, axis=0, keepdims=True), sink)
    p = jnp.exp(s - mx)
    esink = jnp.exp(sink - mx)
    inv = 1.0 / (jnp.sum(p, axis=0, keepdims=True) + esink)
    return p * inv, esink * inv


def _swa_inputs(blk, p_ref, prev_ref, tab_ref, ptab_ref):
    tab = tab_ref[...]
    qr = _rope(p_ref[:, A_Q:A_Q + 1024], tab) * ATT_SCALE
    kk = jnp.concatenate([_rope(prev_ref[:, 0:256], ptab_ref[...]), _rope(p_ref[:, A_K:A_K + 256], tab)], axis=0)
    vv = jnp.concatenate([prev_ref[:, 256:512], p_ref[:, A_V:A_V + 256]], axis=0)
    return tab, qr, kk, vv, jnp.where(blk > 0, 0.0, NEG)


def _swa_forward_step(sink_ref, p_ref, prev_ref, tab_ref, ptab_ref, y_ref):
    n = pl.program_id(0)
    _, qr, kk, vv, pen = _swa_inputs(n, p_ref, prev_ref, tab_ref, ptab_ref)
    from_cur = _from_current()
    outs = []
    for j in range(KV_HEADS):
        s = _fold(_mm_nt(_kv_native(kk, j), _stack_heads(qr, j)), from_cur, pen)
        P, _ = _softmax_sink(s, _sink_row(sink_ref, j))
        outs.append(_unstack_heads(_mm_tn(_unfold(P, from_cur), _kv_native(vv, j)), j))
    g = p_ref[:, A_G:A_G + 1024]
    y_ref[:, D_SSD:D_SSD + D_ATT] = (jnp.concatenate(outs, axis=1) * (g * _sigmoid(g))).astype(y_ref.dtype)


def _mixer_forward(proj_ssd, proj_att, tabs, sinks, conv_w, conv_b, dt_bias, a_log, d_skip, norm_w, E, comm=None):
    L = proj_ssd.shape[0]
    nc = L // CHUNK

    def body(p_ref, halo_ref, cw_ref, cb_ref, dtb_ref, alog_ref, dsk_ref, nw_ref, e_ref,
             sink_ref, pa_ref, prev_ref, tab_ref, ptab_ref, y_ref, ypre_ref, hprev_ref, pre_ref, h_scr, ext_scr):
        _ssd_forward_step(p_ref, halo_ref, cw_ref, cb_ref, dtb_ref, alog_ref, dsk_ref, nw_ref, e_ref,
                          y_ref, ypre_ref, hprev_ref, pre_ref, h_scr, ext_scr)
        _swa_forward_step(sink_ref, pa_ref, prev_ref, tab_ref, ptab_ref, y_ref)

    const = lambda shape: pl.BlockSpec(shape, lambda c: (0, 0))
    smem = pl.BlockSpec(memory_space=pltpu.SMEM)
    rows = lambda w: pl.BlockSpec((CHUNK, w), lambda c: (c, 0))
    return _call(
        body, comm, name="mixer_fwd", grid=(nc,),
        in_specs=[rows(S_W), pl.BlockSpec((8, S_W), lambda c: (jnp.maximum(c * 16 - 1, 0), 0)),
                  const((4, D_XBC)), const((1, D_XBC)), smem, smem, smem, const((1, 1024)), const((128, 1024)),
                  smem, rows(A_W), pl.BlockSpec((CHUNK, 512), lambda c: (jnp.maximum(c - 1, 0), 2)),
                  rows(384), pl.BlockSpec((CHUNK, 384), lambda c: (jnp.maximum(c - 1, 0), 0))],
        out_specs=[rows(D_SSD + D_ATT), rows(D_SSD), pl.BlockSpec((128, 1024), lambda c: (c, 0)), rows(D_XBC)],
        out_shape=[jax.ShapeDtypeStruct((L, D_SSD + D_ATT), _MXU), jax.ShapeDtypeStruct((L, D_SSD), F32),
                   jax.ShapeDtypeStruct((nc * 128, 1024), F32), jax.ShapeDtypeStruct((L, D_XBC), F32)],
        scratch_shapes=[pltpu.VMEM((128, 1024), F32), pltpu.VMEM((136, D_XBC), F32)],
        args=(proj_ssd, proj_ssd, conv_w, conv_b, dt_bias, a_log, d_skip, norm_w, E,
              sinks, proj_att, proj_att, tabs, tabs))


def _swa_backward(proj_att, tabs, sinks, dy, comm=None):
    L = proj_att.shape[0]
    nb = L // CHUNK

    def body(sink_ref, p_ref, prev_ref, tab_ref, ptab_ref, dy_ref, dp_ref, dsink_ref, carry_k, carry_v):
        i = pl.program_id(0)
        n = nb - 1 - i

        @pl.when(i == 0)
        def _():
            carry_k[...] = jnp.zeros_like(carry_k)
            carry_v[...] = jnp.zeros_like(carry_v)
            dsink_ref[...] = jnp.zeros_like(dsink_ref)

        tab, qr, kk, vv, pen = _swa_inputs(n, p_ref, prev_ref, tab_ref, ptab_ref)
        from_cur = _from_current()
        g = p_ref[:, A_G:A_G + 1024]
        sgm = _sigmoid(g)
        dyv = dy_ref[...]
        do_all = dyv * (g * sgm)
        lane8 = lax.broadcasted_iota(jnp.int32, (8, 128), 1)
        hid = lax.broadcasted_iota(jnp.int32, (1, 4 * CHUNK), 1) // CHUNK
        o_parts, dq_parts = [], []
        dk_nat = [jnp.zeros((2 * CHUNK, 128), F32) for _ in range(2)]
        dv_nat = [jnp.zeros((2 * CHUNK, 128), F32) for _ in range(2)]
        dsink = jnp.zeros((8, 128), F32)
        for j in range(KV_HEADS):
            qs = _stack_heads(qr, j)
            kkb, vvb = _kv_native(kk, j), _kv_native(vv, j)
            P, psink = _softmax_sink(_fold(_mm_nt(kkb, qs), from_cur, pen), _sink_row(sink_ref, j))
            p_full = _unfold(P, from_cur)
            o_parts.append(_unstack_heads(_mm_tn(p_full, vvb), j))
            do_s = _stack_heads(do_all, j)
            dP = _fold(_mm_nt(vvb, do_s), from_cur)
            D = jnp.sum(P * dP, axis=0, keepdims=True)
            ds_full = _unfold(P * (dP - D), from_cur)
            sd = psink * D
            for hh in range(4):
                dsink = dsink + jnp.where(lane8 == 4 * j + hh, -jnp.sum(jnp.where(hid == hh, sd, 0.0)), 0.0)
            dq_parts.append(_unstack_heads(_mm_tn(ds_full, kkb), j) * ATT_SCALE)
            dk_nat[j // 2] = dk_nat[j // 2] + _keep_native(_mm(ds_full, qs), j)
            dv_nat[j // 2] = dv_nat[j // 2] + _keep_native(_mm(p_full, do_s), j)
        o = jnp.concatenate(o_parts, axis=1)
        dkk = jnp.concatenate(dk_nat, axis=1)
        dvv = jnp.concatenate(dv_nat, axis=1)
        out = dp_ref.dtype
        dp_ref[:, A_Q:A_Q + 1024] = _rope_transposed(jnp.concatenate(dq_parts, axis=1), tab).astype(out)
        dp_ref[:, A_K:A_K + 256] = _rope_transposed(dkk[CHUNK:2 * CHUNK] + carry_k[...], tab).astype(out)
        dp_ref[:, A_V:A_V + 256] = (dvv[CHUNK:2 * CHUNK] + carry_v[...]).astype(out)
        dp_ref[:, A_G:A_G + 1024] = (dyv * o * (sgm * (1.0 + g * (1.0 - sgm)))).astype(out)
        carry_k[...] = dkk[0:CHUNK]
        carry_v[...] = dvv[0:CHUNK]
        dsink_ref[...] += dsink

    rev = lambda i: (nb - 1 - i, 0)
    prev = lambda i: jnp.maximum(nb - 2 - i, 0)
    return _call(
        body, comm, name="swa_bwd", grid=(nb,),
        in_specs=[pl.BlockSpec(memory_space=pltpu.SMEM),
                  pl.BlockSpec((CHUNK, A_W), rev),
                  pl.BlockSpec((CHUNK, 512), lambda i: (prev(i), 2)),
                  pl.BlockSpec((CHUNK, 384), rev),
                  pl.BlockSpec((CHUNK, 384), lambda i: (prev(i), 0)),
                  pl.BlockSpec((CHUNK, D_ATT), lambda i: (nb - 1 - i, 1))],
        out_specs=[pl.BlockSpec((CHUNK, A_W), rev), pl.BlockSpec((8, 128), lambda i: (0, 0))],
        out_shape=[jax.ShapeDtypeStruct((L, A_W), _MXU), jax.ShapeDtypeStruct((8, 128), F32)],
        scratch_shapes=[pltpu.VMEM((CHUNK, 256), F32), pltpu.VMEM((CHUNK, 256), F32)],
        args=(sinks, proj_att, proj_att, tabs, tabs, dy))


def _head(y, x, target, w_out, ln_g, ln_b, *, tm):
    L = x.shape[0]
    nsteps = L // tm

    def body(y_ref, x_ref, t_ref, wo_ref, g_ref, b_ref, dr_ref, dy_ref, acc_ref):
        i = pl.program_id(0)

        @pl.when(i == 0)
        def _():
            acc_ref[...] = jnp.zeros_like(acc_ref)

        r = ALPHA * x_ref[...] + _mm(y_ref[...], wo_ref[...])
        mu = jnp.mean(r, axis=-1, keepdims=True)
        d = r - mu
        rstd = lax.rsqrt(jnp.mean(d * d, axis=-1, keepdims=True) + LN_EPS)
        xh = d * rstd
        gam = g_ref[0:1, :]
        e = xh * gam + b_ref[0:1, :] - t_ref[...]
        dout = e * (1.0 / D_MODEL)
        dxh = dout * gam
        dr = rstd * (dxh - jnp.mean(dxh, axis=-1, keepdims=True)
                     - xh * jnp.mean(dxh * xh, axis=-1, keepdims=True))
        dr_ref[...] = dr
        dy_ref[...] = _mm_nt(dr, wo_ref[...])
        acc_ref[...] += _rows8([_colsum(dout * xh), _colsum(dout), _colsum(e * e) * (0.5 / D_MODEL)])

        @pl.when(i == nsteps - 1)
        def _():
            acc = acc_ref[...]
            tot = jnp.sum(acc[2:3, :])
            rid = lax.broadcasted_iota(jnp.int32, (8, 1024), 0)
            acc_ref[...] = jnp.where(rid == 3, tot, acc)

    const = lambda shape: pl.BlockSpec(shape, lambda i: (0, 0))
    row = lambda w: pl.BlockSpec((tm, w), lambda i: (i, 0))
    return pl.pallas_call(
        body, name="head", grid=(nsteps,),
        in_specs=[row(2048), row(1024), row(1024), const((2048, 1024)), const((1, 1024)), const((1, 1024))],
        out_specs=[row(1024), row(2048), const((8, 1024))],
        out_shape=[jax.ShapeDtypeStruct((L, D_MODEL), F32), jax.ShapeDtypeStruct((L, 2048), F32),
                   jax.ShapeDtypeStruct((8, 1024), F32)],
        compiler_params=_params(("arbitrary",)),
    )(y, x, target, w_out, ln_g, ln_b)


def _gather_w_in(w_shard):
    R = w_shard.shape[0]
    halves = (pl.ds(0, R // 2), pl.ds(R // 2, R // 2))
    any_spec = pl.BlockSpec(memory_space=pl.ANY)

    def body(in_ref, out_ref, send_sems, recv_sems, local_sem):
        x, y, c = _position()

        def slot(p, half=None):
            s = out_ref.at[_index(*p)]
            return s if half is None else s.at[halves[half]]

        def same_core(p):
            return (p[0], p[1], c)

        def other_core(p):
            return (p[0], p[1], 1 - c)

        me, xn, yn, dg = (x, y), (1 - x, y), (x, 1 - y), (1 - x, 1 - y)

        def copy(k, dst, to, src=None):
            return _remote(dst if src is None else src, dst, send_sems.at[k], recv_sems.at[k], to)

        local = pltpu.make_async_copy(in_ref, slot(same_core(me)), local_sem)
        local.start()
        own = [copy(0, slot(same_core(me)), other_core(me), in_ref), copy(1, slot(same_core(me)), same_core(xn), in_ref),
               copy(2, slot(same_core(me)), same_core(yn), in_ref)]
        for cp in own:
            cp.start()
        copy(1, slot(same_core(xn)), same_core(xn)).wait_recv()
        passed = [copy(4, slot(same_core(xn), 1), same_core(yn)), copy(5, slot(same_core(xn)), other_core(me))]
        for cp in passed:
            cp.start()
        copy(2, slot(same_core(yn)), same_core(yn)).wait_recv()
        more = [copy(3, slot(same_core(yn), 0), same_core(xn)), copy(6, slot(same_core(yn)), other_core(me))]
        for cp in more:
            cp.start()
        passed += more
        for k, half in ((3, 0), (4, 1)):
            copy(k, slot(same_core(dg), half), same_core(xn)).wait_recv()
            fwd = copy(7 + half, slot(same_core(dg), half), other_core(me))
            fwd.start()
            passed.append(fwd)
        copy(0, slot(other_core(me)), other_core(me)).wait_recv()
        copy(5, slot(other_core(xn)), other_core(me)).wait_recv()
        copy(6, slot(other_core(yn)), other_core(me)).wait_recv()
        for half in (0, 1):
            copy(7 + half, slot(other_core(dg), half), other_core(me)).wait_recv()
        for cp in own + passed:
            cp.wait_send()
        local.wait()

    return pl.pallas_call(
        body, name="gather_w_in", in_specs=[any_spec], out_specs=any_spec,
        out_shape=jax.ShapeDtypeStruct((N_DEV,) + w_shard.shape, w_shard.dtype),
        scratch_shapes=[pltpu.SemaphoreType.DMA((9,)), pltpu.SemaphoreType.DMA((9,)), pltpu.SemaphoreType.DMA],
    )(w_shard)


def _input_gradient(d_ssd, d_att, w_ssd, w_att, dr, *, tm, comm=None, stack_hi=None, sum_steps=(2, 4)):
    L = dr.shape[0]
    steps = L // tm

    def matmuls(ds_ref, da_ref, ws_ref, wa_ref, dr_ref, o_ref):
        o_ref[...] = ALPHA * dr_ref[...] + _mm_nt(ds_ref[...], ws_ref[...]) + _mm_nt(da_ref[...], wa_ref[...])

    row = lambda w: pl.BlockSpec((tm, w), lambda i: (i, 0))
    resident = lambda a: pl.BlockSpec(a.shape, lambda i: (0, 0), pipeline_mode=pl.Buffered(1))
    in_specs = [row(S_W), row(A_W), resident(w_ssd), resident(w_att), row(D_MODEL)]
    out_dx = jax.ShapeDtypeStruct((L, D_MODEL), F32)
    args = (d_ssd, d_att, w_ssd, w_att, dr)
    if stack_hi is None:
        return _call(matmuls, comm, name="dx", grid=(steps,), in_specs=in_specs, out_specs=[row(D_MODEL)],
                     out_shape=[out_dx], scratch_shapes=[], args=args)

    block = stack_hi.shape[1:]
    c_in, c_out, rows = len(comm.operands), len(comm.out_shapes), len(comm.flows)

    def body(ds_ref, da_ref, ws_ref, wa_ref, dr_ref, stack_ref, *refs):
        cins, refs = refs[:c_in], refs[c_in:]
        (o_ref, own_ref), refs = refs[:2], refs[2:]
        couts, refs = refs[:c_out], refs[c_out:]
        (theirs_scr, mine_scr, sums_scr, send_sems, recv_sems, local_sems, sum_send_sems, sum_recv_sems,
         swap_send_sems, swap_recv_sems, mine_sems) = refs
        i = pl.program_id(0)
        me = x, y, c = _position()
        start, wait = comm.plan(cins, couts, send_sems, recv_sems, local_sems)
        swaps = [_remote(stack_ref.at[2 * oy + (1 - c)], theirs_scr.at[oy], swap_send_sems.at[oy], swap_recv_sems.at[oy],
                         (x, y, 1 - c)) for oy in range(2)]
        mine = [pltpu.make_async_copy(stack_ref.at[2 * oy + c], mine_scr.at[oy], mine_sems.at[oy]) for oy in range(2)]
        to_owners = []
        for slot, k in enumerate((2, 4, 6)):
            peer = _flip(me, k)
            to_owners.append((peer, _remote(sums_scr.at[peer[1]], couts[0].at[slot], sum_send_sems.at[slot],
                                            sum_recv_sems.at[slot], peer)))

        @pl.when(i == 0)
        def _():
            for cp in [swaps[0]] + mine:
                cp.start()
            start()

        @pl.when(i == 1)
        def _():
            swaps[1].start()

        matmuls(ds_ref, da_ref, ws_ref, wa_ref, dr_ref, o_ref)

        for oy in range(2):
            @pl.when(i == sum_steps[oy])
            def _(oy=oy):
                swaps[oy].wait_recv()
                mine[oy].wait()
                t = mine_scr[oy] + theirs_scr[oy]
                sums_scr[oy] = t.astype(sums_scr.dtype)

                @pl.when(y == oy)
                def _():
                    own_ref[...] = t

                for peer, cp in to_owners:
                    pl.when((peer[0] == 1) & (peer[1] == oy))(cp.start)

        @pl.when(i == steps - 1)
        def _():
            wait()
            for peer, cp in to_owners:
                pl.when(x == 1)(cp.wait_recv)
                pl.when(peer[0] == 1)(cp.wait_send)
            for cp in swaps:
                cp.wait_send()

    io_alias = {6 + ci: 2 + co for ci, co in comm.aliases.items()}
    any_spec = pl.BlockSpec(memory_space=pl.ANY)
    dma = pltpu.SemaphoreType.DMA
    return pl.pallas_call(
        body, name="dx", grid=(steps,), in_specs=in_specs + [any_spec] * (1 + c_in),
        out_specs=[row(D_MODEL), pl.BlockSpec(block, lambda i: (0, 0), pipeline_mode=pl.Buffered(1))] + [any_spec] * c_out,
        out_shape=[out_dx, jax.ShapeDtypeStruct(block, F32)] + list(comm.out_shapes),
        scratch_shapes=[pltpu.VMEM((2,) + block, F32), pltpu.VMEM((2,) + block, F32), pltpu.VMEM((2,) + block, BF16),
                        dma((rows, N_DEV - 1)), dma((rows, N_DEV - 1)), dma((rows,)), dma((3,)), dma((3,)),
                        dma((2,)), dma((2,)), dma((2,))],
        input_output_aliases=io_alias, compiler_params=_params(("arbitrary",)),
    )(*args, stack_hi, *comm.operands)


SHARD_COLS = D_IN_PROJ // N_DEV
SPLIT = N_SSD_REAL - 4 * SHARD_COLS
RELAYOUT_ROWS = 256


def _unpack_w_in(w_all):
    def body(g_ref, ws_ref, wa_ref):
        for j in range(4):
            ws_ref[:, SHARD_COLS * j:SHARD_COLS * (j + 1)] = g_ref[j]
        ws_ref[:, 4 * SHARD_COLS:N_SSD_REAL] = g_ref[4, :, 0:SPLIT]
        ws_ref[:, N_SSD_REAL:S_W] = jnp.zeros((RELAYOUT_ROWS, S_W - N_SSD_REAL), ws_ref.dtype)
        wa_ref[:, 0:SHARD_COLS - SPLIT] = g_ref[4, :, SPLIT:SHARD_COLS]
        for j in range(5, N_DEV):
            lo = SHARD_COLS * (j - 4) - SPLIT
            wa_ref[:, lo:lo + SHARD_COLS] = g_ref[j]

    return pl.pallas_call(
        body, name="unpack_w_in", grid=(D_MODEL // RELAYOUT_ROWS,),
        in_specs=[pl.BlockSpec((N_DEV, RELAYOUT_ROWS, SHARD_COLS), lambda i: (0, i, 0))],
        out_specs=[pl.BlockSpec((RELAYOUT_ROWS, S_W), lambda i: (i, 0)), pl.BlockSpec((RELAYOUT_ROWS, A_W), lambda i: (i, 0))],
        out_shape=[jax.ShapeDtypeStruct((D_MODEL, S_W), w_all.dtype), jax.ShapeDtypeStruct((D_MODEL, A_W), w_all.dtype)],
        compiler_params=_params(("arbitrary",)),
    )(w_all)


def _dw_in(me1, xb, d, half, tail=None, *, tl=1024):
    L, N = d.shape
    steps = L // tl

    def body(me_ref, x_ref, d_ref, *refs):
        if half == 0:
            p_ref, own_ref, tail_ref, acc = refs
        else:
            t_ref, p_ref, acc = refs
        l = pl.program_id(0)

        @pl.when(l == 0)
        def _():
            acc[...] = jnp.zeros_like(acc)

        acc[...] += _mm_tn(x_ref[...], d_ref[...])

        @pl.when(l == steps - 1)
        def _():
            if half == 0:
                me = me_ref[0]

                @pl.when(me >= 4)
                def _():
                    own_ref[...] = jnp.zeros_like(own_ref)

                tail_ref[...] = acc[:, S_DT:S_W]
            for j in range(4):
                if half == 0:
                    pieces = [(0, acc[:, SHARD_COLS * j:SHARD_COLS * (j + 1)])]
                elif j == 0:
                    pieces = [(0, t_ref[:, 4 * SHARD_COLS - S_DT:N_SSD_REAL - S_DT]), (SPLIT, acc[:, 0:SHARD_COLS - SPLIT])]
                else:
                    lo = SHARD_COLS * j - SPLIT
                    pieces = [(0, acc[:, lo:lo + SHARD_COLS])]
                for off, blk in pieces:
                    p_ref[j, :, off:off + blk.shape[1]] = blk.astype(p_ref.dtype)
                    if half == 0:
                        @pl.when(me == j)
                        def _(off=off, blk=blk):
                            own_ref[:, off:off + blk.shape[1]] = blk

    once = pl.Buffered(1)
    whole = lambda shape: pl.BlockSpec(shape, lambda l: (0,) * len(shape), pipeline_mode=once)
    in_specs = [pl.BlockSpec(memory_space=pltpu.SMEM), pl.BlockSpec((tl, D_MODEL), lambda l: (l, 0)),
                pl.BlockSpec((tl, N), lambda l: (l, 0))]
    args = [me1, xb, d]
    if half == 0:
        out_shape = [jax.ShapeDtypeStruct((4, D_MODEL, SHARD_COLS), BF16), jax.ShapeDtypeStruct((D_MODEL, SHARD_COLS), F32),
                     jax.ShapeDtypeStruct((D_MODEL, S_W - S_DT), F32)]
    else:
        in_specs.append(whole(tail.shape))
        args.append(tail)
        out_shape = [jax.ShapeDtypeStruct((4, D_MODEL, SHARD_COLS), F32)]
    return pl.pallas_call(
        body, name="dw_in_%d" % half, grid=(steps,), in_specs=in_specs,
        out_specs=[whole(o.shape) for o in out_shape], out_shape=out_shape,
        scratch_shapes=[pltpu.VMEM((D_MODEL, N), F32)], compiler_params=_params(("arbitrary",)),
    )(*args)


def _adamw_math(w, g, m, v):
    m = ADAM_B1 * m + (1.0 - ADAM_B1) * g
    v = ADAM_B2 * v + (1.0 - ADAM_B2) * (g * g)
    m_hat = m / (1.0 - ADAM_B1 ** ADAM_STEP)
    v_hat = v / (1.0 - ADAM_B2 ** ADAM_STEP)
    delta = -ADAM_LR * (m_hat / (jnp.sqrt(v_hat) + ADAM_EPS) + ADAM_WD * w)
    return delta, m, v


def _adamw_shard(n_recv, g_own, recv, w, m, v, *, rows, name):
    R, C = g_own.shape

    def body(n_ref, g_ref, r_ref, w_ref, m_ref, v_ref, go_ref, d_ref, mo_ref, vo_ref):
        g = g_ref[...]
        for k in range(N_DEV - 1):
            g = g + jnp.where(k < n_ref[0], r_ref[k].astype(F32), 0.0)
        d, mn, vn = _adamw_math(w_ref[...], g, m_ref[...], v_ref[...])
        go_ref[...] = g
        d_ref[...] = d
        mo_ref[...] = mn
        vo_ref[...] = vn

    blk = pl.BlockSpec((rows, C), lambda i: (i, 0))
    return pl.pallas_call(
        body, name=name, grid=(R // rows,),
        in_specs=[pl.BlockSpec(memory_space=pltpu.SMEM), blk,
                  pl.BlockSpec((N_DEV - 1, rows, C), lambda i: (0, i, 0)), blk, blk, blk],
        out_specs=[blk] * 4, out_shape=[jax.ShapeDtypeStruct((R, C), F32)] * 4,
        compiler_params=_params(("arbitrary",)),
    )(n_recv, g_own, recv, w, m, v)


def _minor_rows_view(a):
    return jnp.transpose(a, (2, 0, 1)).reshape(SHARD_COLS * 8, 128)


def _from_minor_rows_view(v):
    return jnp.transpose(v.reshape(SHARD_COLS, 8, 128), (1, 2, 0)).reshape(1, D_MODEL, SHARD_COLS)


def _adamw_w_in(n_recv, g_lo, g_hi, recv, w, m, v):
    C = SHARD_COLS
    pad = -C % 128

    def body(n_ref, lo_ref, hi_ref, r_ref, w_ref, m_ref, v_ref, go_ref, d_ref, mo_ref, vo_ref):
        for q in range(D_MODEL // 128):
            band = pl.ds(q * 128, 128)
            g = jnp.where(n_ref[0] == N_DEV - 1, lo_ref[band, :], hi_ref[band, :])
            for k in range(N_DEV - 1):
                g = g + jnp.where(k < n_ref[0], r_ref[k, band, :].astype(F32), 0.0)
            g = jnp.pad(g, ((0, 0), (0, pad))).T[0:C]
            rows = pl.ds(q, C, stride=8)
            d, mn, vn = _adamw_math(w_ref[rows, :], g, m_ref[rows, :], v_ref[rows, :])
            go_ref[rows, :] = g
            d_ref[rows, :] = d
            mo_ref[rows, :] = mn
            vo_ref[rows, :] = vn

    return pl.pallas_call(
        body, name="adamw_w_in", out_shape=[jax.ShapeDtypeStruct(w.shape, F32)] * 4,
        in_specs=[pl.BlockSpec(memory_space=pltpu.SMEM)] + [pl.BlockSpec(memory_space=pltpu.VMEM)] * 6,
        out_specs=[pl.BlockSpec(memory_space=pltpu.VMEM)] * 4,
        compiler_params=_params(),
    )(n_recv, g_lo, g_hi, recv, w, m, v)


SMALL = ("conv_b", "dt_bias", "a_log", "d_skip", "ssd_norm_w", "attn_sinks", "ln_g", "ln_b")


def _adamw_small(gathered, params):
    n_p = len(SMALL)

    def body(*refs):
        acc = []
        for r in refs[:5]:
            t = r[0]
            for k in range(1, N_DEV):
                t = t + r[k]
            acc.append(t)
        head, conv, norm, scal, sink = acc
        grads = dict(conv_b=conv[4:5, :], dt_bias=scal[0:1, 0:N_HEADS], a_log=scal[1:2, 0:N_HEADS],
                     d_skip=scal[2:3, 0:N_HEADS], ssd_norm_w=norm[0:1, :], attn_sinks=sink[0:1, 0:N_HEADS],
                     ln_g=head[0:1, :], ln_b=head[1:2, :])
        wmv = refs[5:5 + 3 * n_p]
        outs = refs[5 + 3 * n_p:]
        outs[0][...] = head[3:4, 0:1]
        outs[1][...] = conv[0:4, :]
        for i, name in enumerate(SMALL):
            w_ref, m_ref, v_ref = wmv[3 * i:3 * i + 3]
            g = grads[name]
            d, mn, vn = _adamw_math(w_ref[...], g, m_ref[...], v_ref[...])
            for o_ref, val in zip(outs[2 + 4 * i:6 + 4 * i], (g, d, mn, vn)):
                o_ref[...] = val

    flat = [a for name in SMALL for a in params[name]]
    out_shape = [jax.ShapeDtypeStruct((1, 1), F32), jax.ShapeDtypeStruct((4, D_XBC), F32)]
    for name in SMALL:
        out_shape += [jax.ShapeDtypeStruct(params[name][0].shape, F32)] * 4
    res = pl.pallas_call(body, name="adamw_small", out_shape=out_shape, compiler_params=_params())(*gathered, *flat)
    return res[0], res[1], {name: res[2 + 4 * i:6 + 4 * i] for i, name in enumerate(SMALL)}


def _adamw_plain(g, w, m, v):
    def body(g_ref, w_ref, m_ref, v_ref, d_ref, mo_ref, vo_ref):
        d, mn, vn = _adamw_math(w_ref[...], g_ref[...], m_ref[...], v_ref[...])
        d_ref[...] = d
        mo_ref[...] = mn
        vo_ref[...] = vn

    return pl.pallas_call(
        body, name="adamw_conv_w", out_shape=[jax.ShapeDtypeStruct(w.shape, F32)] * 3,
        compiler_params=_params(),
    )(g, w, m, v)


def _lane_pattern(fn):
    return np.asarray([fn(l % HEAD_DIM) for l in range(128)], np.float32)


ROPE_INV = _lane_pattern(lambda r: ROPE_THETA ** (-2.0 * (r % 8) / ROPE_DIM) if r < ROPE_DIM else 0.0)
ROPE_SIN_A = _lane_pattern(lambda r: 1.0 if 8 <= r < ROPE_DIM else 0.0)
ROPE_SIN_B = _lane_pattern(lambda r: -1.0 if r < 8 else 0.0)


def _rope_tables(positions):
    ang = positions.astype(F32)[:, None] * ROPE_INV[None, :]
    sn = jnp.sin(ang)
    return jnp.concatenate([jnp.cos(ang), sn * ROPE_SIN_A[None, :], sn * ROPE_SIN_B[None, :]], axis=1)


def _expansion():
    E = np.arange(1024)[None, :] // HEAD_DIM == np.arange(128)[:, None]
    return jnp.asarray(E, BF16), jnp.asarray(E.T, BF16)


def _ssd_args(conv_w, conv_b, dt_bias, a_log, d_skip, norm_w, E):
    return (conv_w, conv_b, dt_bias.reshape(-1), a_log.reshape(-1), d_skip.reshape(-1), norm_w, E)


def kernel(x, positions, w_in, conv_w, conv_b, dt_bias, a_log, d_skip, ssd_norm_w, attn_sinks, w_out, ln_g, ln_b, loss_target, m_w_in, m_conv_w, m_conv_b, m_dt_bias, m_a_log, m_d_skip, m_ssd_norm_w, m_attn_sinks, m_w_out, m_ln_g, m_ln_b, v_w_in, v_conv_w, v_conv_b, v_dt_bias, v_a_log, v_d_skip, v_ssd_norm_w, v_attn_sinks, v_w_out, v_ln_g, v_ln_b):
    me = _index(*_position())
    me1 = me.reshape(1).astype(jnp.int32)
    x0, target = x[0], loss_target[0]
    bf16_shard = lambda shape: jax.ShapeDtypeStruct(shape, BF16)
    E, ET = _expansion()
    tabs = _rope_tables(positions[0])
    sinks = attn_sinks.reshape(-1)

    w_ssd, w_att = _unpack_w_in(_gather_w_in(w_in[0].astype(BF16)))
    gather_conv_w = _Hosted([conv_w[0]], [jax.ShapeDtypeStruct((N_DEV,) + conv_w.shape[1:], F32)],
                            [_Flow("gather", 0, 0)])

    proj_ssd, proj_att, xb, conv_w_all = _in_proj(x0, w_ssd, w_att, tm=512, comm=gather_conv_w)
    conv_w_f = jnp.transpose(conv_w_all, (1, 0, 2)).reshape(4, D_XBC)
    ssd_args = _ssd_args(conv_w_f, conv_b, dt_bias, a_log, d_skip, ssd_norm_w, E)
    gather_w_out = _Hosted([w_out[0].astype(BF16)], [bf16_shard((N_DEV, 256, D_MODEL))], [_Flow("gather", 0, 0)])
    y, ypre, hprev, pre, w_out_all = _mixer_forward(proj_ssd, proj_att, tabs, sinks, *ssd_args, comm=gather_w_out)
    w_out_f = w_out_all.reshape(2 * D_MODEL, D_MODEL)
    dr, dy, acc_head = _head(y, x0, target, w_out_f, ln_g, ln_b, tm=512)

    dw_out, dw_out_bf16 = _matmul_tn(y, dr, tl=1024, tn=D_MODEL, name="dw_out", emit_bf16=True)
    own_out = lax.dynamic_index_in_dim(dw_out.reshape(N_DEV, 256, D_MODEL), me, axis=0, keepdims=False)
    send_out = _Hosted([dw_out_bf16.reshape(N_DEV, 256, D_MODEL)], [bf16_shard((N_DEV - 1, 256, D_MODEL))],
                       [_Flow("exchange", 0, 0)])
    d_ssd, acc_cw, acc_w, acc_s, recv_out = _ssd_backward(proj_ssd, hprev, ypre, pre, dy, *ssd_args, ET, comm=send_out)
    parts_lo, own_lo, dw_dt_block = _dw_in(me1, xb, d_ssd, 0)
    recv_shape = bf16_shard((N_DEV - 1, D_MODEL, SHARD_COLS))
    early_rows = D_MODEL // 4
    send_lo = _Hosted([parts_lo], [recv_shape], [_Flow("exchange", 0, 0, target_x=0, target_c=0),
                                                 _Flow("exchange", 0, 0, target_x=0, target_c=1, rows=(0, early_rows))])
    d_att, dsink, recv_in = _swa_backward(proj_att, tabs, sinks, dy, comm=send_lo)
    (stack_hi,) = _dw_in(me1, xb, d_att, 1, dw_dt_block)
    accs = [acc_head, acc_cw, acc_w, acc_s, dsink]
    send_rest = _Hosted([recv_in, parts_lo] + accs,
                        [recv_shape] + [jax.ShapeDtypeStruct((N_DEV,) + a.shape, F32) for a in accs],
                        [_Flow("exchange", 1, 0, target_x=0, target_c=1, rows=(early_rows, D_MODEL - early_rows))]
                        + [_Flow("gather", 2 + i, 1 + i) for i in range(5)],
                        aliases={0: 0})
    dx, own_hi, recv_in, *gathered = _input_gradient(d_ssd, d_att, w_ssd, w_att, dr, tm=256, comm=send_rest,
                                                     stack_hi=stack_hi)
    n_recv_in = jnp.where(me < 4, N_DEV - 1, 3).reshape(1).astype(jnp.int32)
    n_recv_out = jnp.full((1,), N_DEV - 1, jnp.int32)

    g_in, d_in, nm_in, nv_in = [_from_minor_rows_view(r) for r in _adamw_w_in(
        n_recv_in, own_lo, own_hi, recv_in, _minor_rows_view(w_in), _minor_rows_view(m_w_in), _minor_rows_view(v_w_in))]
    g_out, d_out, nm_out, nv_out = _adamw_shard(n_recv_out, own_out, recv_out, w_out[0], m_w_out[0], v_w_out[0],
                                                rows=256, name="adamw_w_out")
    loss, g_conv_w, small = _adamw_small(gathered, dict(
        conv_b=(conv_b, m_conv_b, v_conv_b), dt_bias=(dt_bias, m_dt_bias, v_dt_bias), a_log=(a_log, m_a_log, v_a_log),
        d_skip=(d_skip, m_d_skip, v_d_skip), ssd_norm_w=(ssd_norm_w, m_ssd_norm_w, v_ssd_norm_w),
        attn_sinks=(attn_sinks, m_attn_sinks, v_attn_sinks), ln_g=(ln_g, m_ln_g, v_ln_g), ln_b=(ln_b, m_ln_b, v_ln_b)))
    g_cw = lax.dynamic_slice_in_dim(g_conv_w, me * (D_XBC // N_DEV), D_XBC // N_DEV, axis=1)
    d_cw, nm_cw, nv_cw = _adamw_plain(g_cw, conv_w[0], m_conv_w[0], v_conv_w[0])

    def leaves(i, big_in, cw, big_out):
        mid = [small[k][i] for k in ("conv_b", "dt_bias", "a_log", "d_skip", "ssd_norm_w", "attn_sinks")]
        return [big_in, cw[None]] + mid + [big_out[None], small["ln_g"][i], small["ln_b"][i]]

    return (loss.reshape(()), dx[None], *leaves(0, g_in, g_cw, g_out), *leaves(1, d_in, d_cw, d_out),
            *leaves(2, nm_in, nm_cw, nm_out), *leaves(3, nv_in, nv_cw, nv_out))
```

```python
import jax
import jax.numpy as jnp
from jax import lax
from jax.experimental import pallas as pl
from jax.experimental.pallas import tpu as pltpu
import numpy as np

F32 = jnp.float32
BF16 = jnp.bfloat16
_MXU = jnp.bfloat16

N_DEV = 8
D_MODEL = 1024
D_SSD = 1024
D_ATT = 1024
HEAD_DIM = 64
N_HEADS = 16
SSD_GROUPS = 2
KV_HEADS = 4
CHUNK = 128
D_XBC = 1536
D_IN_PROJ = 5136
ROPE_DIM = 16
ROPE_THETA = 500000.0
ALPHA = (2.0 * 1) ** 0.25
LN_EPS = 1e-5
RMS_EPS = 1e-5
ATT_SCALE = HEAD_DIM ** -0.5
NEG = -1e30

S_Z, S_XS, S_B, S_C, S_DT, S_W = 0, 1024, 2048, 2304, 2560, 2816
N_SSD_REAL = 2576
A_Q, A_K, A_V, A_G, A_W = 0, 1024, 1280, 1536, 2560

ADAM_LR = 0.001
ADAM_B1 = 0.9
ADAM_B2 = 0.999
ADAM_EPS = 1e-08
ADAM_WD = 0.01
ADAM_STEP = 10

VMEM_LIMIT = 48 * 1024 * 1024
MESH = pl.DeviceIdType.MESH


def _params(sem=None):
    return pltpu.CompilerParams(dimension_semantics=sem, vmem_limit_bytes=VMEM_LIMIT)


def _mm(a, b):
    return jnp.dot(a.astype(_MXU), b.astype(_MXU), preferred_element_type=F32)


def _mm_nt(a, b):
    return lax.dot_general(a.astype(_MXU), b.astype(_MXU), (((1,), (1,)), ((), ())),
                           preferred_element_type=F32)


def _mm_tn(a, b):
    return lax.dot_general(a.astype(_MXU), b.astype(_MXU), (((0,), (0,)), ((), ())),
                           preferred_element_type=F32)


def _split3(v):
    hi = v.astype(BF16)
    r = v - hi.astype(F32)
    mid = r.astype(BF16)
    lo = (r - mid.astype(F32)).astype(BF16)
    return hi, mid, lo


def _mm_exact_r(v, p01):
    hi, mid, lo = _split3(v)
    d = lambda a: jnp.dot(a, p01, preferred_element_type=F32)
    return d(hi) + d(mid) + d(lo)


def _mm_exact_l(p01, v):
    hi, mid, lo = _split3(v)
    d = lambda a: jnp.dot(p01, a, preferred_element_type=F32)
    return d(hi) + d(mid) + d(lo)


def _mm_2pass_r(v, p01):
    hi = v.astype(BF16)
    lo = (v - hi.astype(F32)).astype(BF16)
    return jnp.dot(hi, p01, preferred_element_type=F32) + jnp.dot(lo, p01, preferred_element_type=F32)


def _sigmoid(x):
    return 1.0 / (1.0 + jnp.exp(-x))


def _softplus(x):
    e = jnp.exp(-jnp.abs(x))
    u = 1.0 + e
    log1p = jnp.where(u == 1.0, e, jnp.log(u) * (e / (u - 1.0)))
    return jnp.maximum(x, 0.0) + log1p


def _rows8(rows):
    n = rows[0].shape[1]
    rid = lax.broadcasted_iota(jnp.int32, (8, n), 0)
    out = jnp.zeros((8, n), F32)
    for k, r in enumerate(rows):
        out = out + jnp.where(rid == k, r, 0.0)
    return out


def _colsum(a):
    return jnp.sum(a, axis=0, keepdims=True)


def _in_proj(x, w_ssd, w_att, *, tm, comm=None):
    L, K = x.shape

    def body(x_ref, ws_ref, wa_ref, ps_ref, pa_ref, xb_ref):
        xb = x_ref[...].astype(_MXU)
        xb_ref[...] = xb
        ps_ref[...] = jnp.dot(xb, ws_ref[...], preferred_element_type=F32)
        pa_ref[...] = jnp.dot(xb, wa_ref[...], preferred_element_type=F32)

    row = lambda w: pl.BlockSpec((tm, w), lambda i: (i, 0))
    resident = lambda a: pl.BlockSpec(a.shape, lambda i: (0, 0), pipeline_mode=pl.Buffered(1))
    return _call(
        body, comm, name="in_proj", grid=(L // tm,),
        in_specs=[row(K), resident(w_ssd), resident(w_att)], out_specs=[row(S_W), row(A_W), row(K)],
        out_shape=[jax.ShapeDtypeStruct((L, S_W), F32), jax.ShapeDtypeStruct((L, A_W), F32),
                   jax.ShapeDtypeStruct((L, K), _MXU)],
        scratch_shapes=[], args=(x, w_ssd, w_att))


def _matmul_tn(a, g, *, tl, tn, name, emit_bf16=False):
    L, M = a.shape
    N = g.shape[1]
    last = L // tl - 1

    def body(a_ref, g_ref, o_ref, *rest):
        @pl.when(pl.program_id(1) == 0)
        def _():
            o_ref[...] = jnp.zeros_like(o_ref)

        o_ref[...] += _mm_tn(a_ref[...], g_ref[...])
        if emit_bf16:
            @pl.when(pl.program_id(1) == last)
            def _():
                rest[0][...] = o_ref[...].astype(BF16)

    spec = pl.BlockSpec((M, tn), lambda j, l: (0, j))
    res = pl.pallas_call(
        body, name=name, grid=(N // tn, L // tl),
        in_specs=[pl.BlockSpec((tl, M), lambda j, l: (l, 0)), pl.BlockSpec((tl, tn), lambda j, l: (l, j))],
        out_specs=[spec, spec] if emit_bf16 else [spec],
        out_shape=[jax.ShapeDtypeStruct((M, N), F32)] + ([jax.ShapeDtypeStruct((M, N), BF16)] if emit_bf16 else []),
        compiler_params=_params(("arbitrary", "arbitrary")),
    )(a, g)
    return res if emit_bf16 else res[0]


def _position():
    return lax.axis_index("x"), lax.axis_index("y"), lax.axis_index("c")


def _index(px, py, pc):
    return 4 * px + 2 * py + pc


def _flip(pos, k):
    x, y, c = pos
    return ((1 - x) if (k >> 2) & 1 else x, (1 - y) if (k >> 1) & 1 else y, (1 - c) if k & 1 else c)


def _when(cond, fn):
    if cond is True:
        fn()
    else:
        pl.when(cond)(fn)


def _remote(src, dst, send_sem, recv_sem, peer):
    return pltpu.make_async_remote_copy(src_ref=src, dst_ref=dst, send_sem=send_sem, recv_sem=recv_sem,
                                        device_id=peer, device_id_type=MESH)


class _Flow:
    def __init__(self, kind, operand, result, target_x=None, target_c=None, rows=None):
        self.kind, self.operand, self.result, self.target_x, self.target_c = kind, operand, result, target_x, target_c
        self.rows = rows

    def owns(self, pos):
        if self.target_x is None:
            return True
        cond = pos[0] == self.target_x
        return cond if self.target_c is None else cond & (pos[2] == self.target_c)


class _Hosted:
    def __init__(self, operands, out_shapes, flows, aliases=None):
        self.operands, self.out_shapes, self.flows = operands, out_shapes, flows
        self.aliases = aliases or {}

    def plan(self, ins, outs, send_sems, recv_sems, local_sems):
        me = _position()
        mi = _index(*me)
        sends, recvs, locals_ = [], [], []
        for row, f in enumerate(self.flows):
            src, dst = ins[f.operand], outs[f.result]
            for k in range(1, N_DEV):
                peer = _flip(me, k)
                sems = (send_sems.at[row, k - 1], recv_sems.at[row, k - 1])
                if f.kind == "exchange":
                    owner = _index(*peer) if f.target_x is None else 2 * peer[1] + peer[2]
                    part = (slice(None),) if f.rows is None else (pl.ds(*f.rows),)
                    cp = _remote(src.at[(owner,) + part], dst.at[(k - 1,) + part], *sems, peer)
                    sends.append((f.owns(peer), cp))
                    recvs.append((f.owns(me), cp))
                else:
                    sends.append((True, _remote(src, dst.at[mi], *sems, peer)))
                    recvs.append((True, _remote(src, dst.at[_index(*peer)], *sems, peer)))
            if f.kind == "gather":
                locals_.append(pltpu.make_async_copy(src, dst.at[mi], local_sems.at[row]))

        def start():
            for cp in locals_:
                cp.start()
            for cond, cp in sends:
                _when(cond, cp.start)

        def wait():
            for cond, cp in recvs:
                _when(cond, cp.wait_recv)
            for cond, cp in sends:
                _when(cond, cp.wait_send)
            for cp in locals_:
                cp.wait()

        return start, wait


def _call(body, comm, *, name, grid, in_specs, out_specs, out_shape, scratch_shapes, args, aliases=None):
    io_alias = dict(aliases or {})
    semantics = ("arbitrary",) * len(grid)
    if comm is None:
        return pl.pallas_call(body, name=name, grid=grid, in_specs=in_specs, out_specs=out_specs, out_shape=out_shape,
                              scratch_shapes=scratch_shapes, input_output_aliases=io_alias,
                              compiler_params=_params(semantics))(*args)
    n_in, n_out, n_scr = len(args), len(out_shape), len(scratch_shapes)
    c_in, c_out, rows = len(comm.operands), len(comm.out_shapes), len(comm.flows)

    def hosted(*refs):
        ins, refs = refs[:n_in], refs[n_in:]
        cins, refs = refs[:c_in], refs[c_in:]
        outs, refs = refs[:n_out], refs[n_out:]
        couts, refs = refs[:c_out], refs[c_out:]
        scr, (send_sems, recv_sems, local_sems) = refs[:n_scr], refs[n_scr:]
        start, wait = comm.plan(cins, couts, send_sems, recv_sems, local_sems)
        ids = [pl.program_id(d) for d in range(len(grid))]
        first, last = ids[0] == 0, ids[0] == grid[0] - 1
        for d in range(1, len(grid)):
            first, last = first & (ids[d] == 0), last & (ids[d] == grid[d] - 1)
        pl.when(first)(start)
        body(*ins, *outs, *scr)
        pl.when(last)(wait)

    for ci, co in comm.aliases.items():
        io_alias[n_in + ci] = n_out + co
    any_spec = pl.BlockSpec(memory_space=pl.ANY)
    sems = [pltpu.SemaphoreType.DMA((rows, N_DEV - 1)), pltpu.SemaphoreType.DMA((rows, N_DEV - 1)),
            pltpu.SemaphoreType.DMA((rows,))]
    return pl.pallas_call(
        hosted, name=name, grid=grid, in_specs=list(in_specs) + [any_spec] * c_in,
        out_specs=list(out_specs) + [any_spec] * c_out, out_shape=list(out_shape) + list(comm.out_shapes),
        scratch_shapes=list(scratch_shapes) + sems, input_output_aliases=io_alias,
        compiler_params=_params(semantics))(*args, *comm.operands)


def _head_row(ref, width, rep):
    hid = lax.broadcasted_iota(jnp.int32, (1, width), 1) // rep
    row = jnp.zeros((1, width), F32)
    for h in range(N_HEADS):
        row = jnp.where(hid == h, ref[h], row)
    return row


def _rows_from_above(u_b, s, ext_scr, row, col):
    down = (row - col == s).astype(_MXU)
    return jnp.concatenate([ext_scr[8 - s:16 - s, :], jnp.dot(down, u_b, preferred_element_type=F32)[8:128]], axis=0)


def _ssd_recompute(first, p_ref, halo_ref, cw_ref, cb_ref, dtb_ref, alog_ref, e_ref, ext_scr, pre=None):
    row = lax.broadcasted_iota(jnp.int32, (128, 128), 0)
    col = lax.broadcasted_iota(jnp.int32, (128, 128), 1)
    ext_scr[0:8, :] = jnp.where(first, 0.0, halo_ref[:, S_XS:S_DT])
    if pre is not None:
        ext_scr[8:16, :] = p_ref[0:8, S_XS:S_DT]
    else:
        ext_scr[8:136, :] = p_ref[:, S_XS:S_DT]
        cw = cw_ref[...]
        pre = (cb_ref[0:1, :] + cw[3:4, :] * ext_scr[8:136, :] + cw[2:3, :] * ext_scr[7:135, :]
               + cw[1:2, :] * ext_scr[6:134, :] + cw[0:1, :] * ext_scr[5:133, :])
    sg = _sigmoid(pre)
    act = pre * sg
    lane = lax.broadcasted_iota(jnp.int32, (1, 128), 1)
    A = jnp.where(lane < N_HEADS, -jnp.exp(_head_row(alog_ref, 128, 1)), 0.0)
    raw = p_ref[:, S_DT:S_DT + 128] + _head_row(dtb_ref, 128, 1)
    dt = _softplus(raw)
    dA = dt * A
    tril = (row >= col).astype(BF16)
    acs = _mm_exact_l(tril, dA)
    last = acs[127:128, :]
    ds = jnp.exp(last - acs)
    eo = jnp.exp(acs)
    E = e_ref[...]
    ex = _mm_2pass_r(jnp.concatenate([dt, ds, eo], axis=0), E)
    dt_e, ds_e, eo_e = ex[0:128], ex[128:256], ex[256:384]
    xs_c = act[:, 0:1024]
    X = xs_c * dt_e
    return dict(pre=pre, sg=sg, xs_c=xs_c, Bc=act[:, 1024:1280], Cc=act[:, 1280:1536], A=A, raw=raw, dt=dt,
                acs=acs, acsT=acs.T, eo_e=eo_e, ds_e=ds_e, dt_e=dt_e, cd_e=eo_e[127:128, :],
                X=X, Xd=X * ds_e, row=row, col=col)


def _split_halves(t):
    lo = _lo_half(CHUNK)
    return jnp.concatenate([jnp.where(lo, t, 0.0), jnp.where(lo, 0.0, t)], axis=0)


def _ssd_core(R, hprev):
    causal = R["row"] >= R["col"]
    acs, acsT, X = R["acs"], R["acsT"], R["X"]
    ydiag, yoff, snew = [], [], []
    for g in range(SSD_GROUPS):
        Bg = R["Bc"][:, g * 128:(g + 1) * 128]
        Cg = R["Cc"][:, g * 128:(g + 1) * 128]
        cols = slice(g * 512, (g + 1) * 512)
        CB = _mm_nt(Cg, Bg)
        snew.append(_mm_tn(Bg, R["Xd"][:, cols]))
        yoff.append(_mm(Cg, hprev[:, cols]))
        for j in range(4):
            h0 = g * 8 + 2 * j
            ms = [CB * jnp.exp(jnp.where(causal, acs[:, h:h + 1] - acsT[h:h + 1, :], NEG)) for h in (h0, h0 + 1)]
            ydiag.append(_mm(jnp.concatenate(ms, axis=1), _split_halves(X[:, h0 * HEAD_DIM:h0 * HEAD_DIM + 128])))
    Y = jnp.concatenate(ydiag, axis=1) + jnp.concatenate(yoff, axis=1) * R["eo_e"]
    return Y, jnp.concatenate(snew, axis=1)


def _ssd_forward_step(p_ref, halo_ref, cw_ref, cb_ref, dtb_ref, alog_ref, dsk_ref, nw_ref, e_ref,
                      y_ref, ypre_ref, hprev_ref, pre_ref, h_scr, ext_scr):
    c = pl.program_id(0)
    first = c == 0

    @pl.when(first)
    def _():
        h_scr[...] = jnp.zeros_like(h_scr)

    R = _ssd_recompute(first, p_ref, halo_ref, cw_ref, cb_ref, dtb_ref, alog_ref, e_ref, ext_scr)
    hprev = h_scr[...]
    hprev_ref[...] = hprev
    pre_ref[...] = R["pre"]
    Y, snew = _ssd_core(R, hprev)
    h_scr[...] = hprev * R["cd_e"] + snew
    Y = Y + _head_row(dsk_ref, D_SSD, HEAD_DIM) * R["xs_c"]
    ypre_ref[...] = Y
    z = p_ref[:, S_Z:S_Z + 1024]
    yf = Y * (z * _sigmoid(z))
    outs = []
    for g in range(SSD_GROUPS):
        yg = yf[:, g * 512:(g + 1) * 512]
        r = lax.rsqrt(jnp.mean(yg * yg, axis=-1, keepdims=True) + RMS_EPS)
        outs.append(yg * r)
    y_ref[:, 0:D_SSD] = (jnp.concatenate(outs, axis=1) * nw_ref[0:1, :]).astype(y_ref.dtype)


def _ssd_backward(proj_ssd, hprev_all, ypre, pre, dy, conv_w, conv_b, dt_bias, a_log, d_skip, norm_w, E, ET, comm=None):
    L = proj_ssd.shape[0]
    nc = L // CHUNK

    def body(p_ref, halo_ref, hprev_ref, ypre_ref, pre_ref, dy_ref, cw_ref, cb_ref, dtb_ref, alog_ref, dsk_ref, nw_ref, e_ref,
             et_ref, dp_ref, acc_cw_ref, acc_w_ref, acc_s_ref, dh_scr, ext_scr, ext2_scr, nxt_scr):
        i = pl.program_id(0)
        c = nc - 1 - i
        first = c == 0

        @pl.when(i == 0)
        def _():
            dh_scr[...] = jnp.zeros_like(dh_scr)
            nxt_scr[...] = jnp.zeros_like(nxt_scr)
            acc_cw_ref[...] = jnp.zeros_like(acc_cw_ref)
            acc_w_ref[...] = jnp.zeros_like(acc_w_ref)
            acc_s_ref[...] = jnp.zeros_like(acc_s_ref)

        R = _ssd_recompute(first, p_ref, halo_ref, cw_ref, cb_ref, dtb_ref, alog_ref, e_ref, ext_scr, pre_ref[...])
        hprev = hprev_ref[...]
        xs_c, X, Xd = R["xs_c"], R["X"], R["Xd"]
        acs, acsT = R["acs"], R["acsT"]
        ET = et_ref[...]
        dsk = _head_row(dsk_ref, D_SSD, HEAD_DIM)
        Y = ypre_ref[...]

        z = p_ref[:, S_Z:S_Z + 1024]
        sz = _sigmoid(z)
        silz = z * sz
        yf = Y * silz
        dyv = dy_ref[...]
        nw = nw_ref[0:1, :]
        dyf_parts, dnw_parts = [], []
        for g in range(SSD_GROUPS):
            cols = slice(g * 512, (g + 1) * 512)
            yg = yf[:, cols]
            r = lax.rsqrt(jnp.mean(yg * yg, axis=-1, keepdims=True) + RMS_EPS)
            yn = yg * r
            dyn = dyv[:, cols] * nw[:, cols]
            dnw_parts.append(_colsum(dyv[:, cols] * yn))
            dyf_parts.append(r * (dyn - yn * jnp.mean(dyn * yn, axis=-1, keepdims=True)))
        dyf = jnp.concatenate(dyf_parts, axis=1)
        dY = dyf * silz
        dz = dyf * Y * (sz * (1.0 + z * (1.0 - sz)))

        dhn = dh_scr[...]
        dYo = dY * R["eo_e"]
        causal = R["row"] >= R["col"]
        dacs = jnp.zeros((128, 128), F32)
        dacs_t = jnp.zeros((128, 128), F32)
        dxdiag, dxd, dhprev, dBs, dCs, yoff = [], [], [], [], [], []
        for g in range(SSD_GROUPS):
            Bg = R["Bc"][:, g * 128:(g + 1) * 128]
            Cg = R["Cc"][:, g * 128:(g + 1) * 128]
            cols = slice(g * 512, (g + 1) * 512)
            CB = _mm_nt(Cg, Bg)
            dCB = jnp.zeros((128, 128), F32)
            for j in range(4):
                h0 = g * 8 + 2 * j
                pc = slice(h0 * HEAD_DIM, h0 * HEAD_DIM + 128)
                dYst = _split_halves(dY[:, pc])
                dMst = _mm_nt(dYst, X[:, pc])
                mts = []
                for a, h in enumerate((h0, h0 + 1)):
                    acol = acs[:, h:h + 1]
                    arow = acsT[h:h + 1, :]
                    Lm = jnp.exp(jnp.where(causal, acol - arow, NEG))
                    M = CB * Lm
                    dM = dMst[a * 128:(a + 1) * 128]
                    dCB = dCB + dM * Lm
                    G = dM * M
                    dacs = dacs + jnp.where(R["col"] == h, jnp.sum(G, axis=1, keepdims=True), 0.0)
                    dacs_t = dacs_t + jnp.where(R["row"] == h, jnp.sum(G, axis=0, keepdims=True), 0.0)
                    mts.append(M.T)
                dxdiag.append(_mm(jnp.concatenate(mts, axis=1), dYst))
            dS = dhn[:, cols]
            dxd.append(_mm(Bg, dS))
            yoff.append(_mm(Cg, hprev[:, cols]))
            dhprev.append(_mm_tn(Cg, dYo[:, cols]))
            dCs.append(_mm_nt(dYo[:, cols], hprev[:, cols]) + _mm(dCB, Bg))
            dBs.append(_mm_tn(dCB, Cg) + _mm_nt(Xd[:, cols], dS))
        Yoff = jnp.concatenate(yoff, axis=1) * R["eo_e"]
        dXd = jnp.concatenate(dxd, axis=1)
        dX = jnp.concatenate(dxdiag, axis=1) + dXd * R["ds_e"]
        t_state = dXd * Xd
        dacs = dacs + _mm_2pass_r(dY * Yoff - t_state, ET) - dacs_t.T
        v_last = _colsum(t_state + dhn * hprev * R["cd_e"])
        dlast = _mm_exact_r(jnp.broadcast_to(v_last, (8, 1024)), ET)[0:1, :]
        dacs = dacs + jnp.where(R["row"] == 127, dlast, 0.0)
        triu = (R["col"] >= R["row"]).astype(BF16)
        da = _mm_exact_l(triu, dacs)
        ddt = da * R["A"] + _mm(dX * xs_c, ET)
        ddt_raw = ddt * _sigmoid(R["raw"])
        dxs_c = dX * R["dt_e"] + dY * dsk
        dh_scr[...] = jnp.concatenate(dhprev, axis=1) + dhn * R["cd_e"]

        dact = jnp.concatenate([dxs_c] + dBs + dCs, axis=1)
        pre, sg = R["pre"], R["sg"]
        dpre = dact * (sg * (1.0 + pre * (1.0 - sg)))
        ext2_scr[0:8, :] = dpre[120:128, :]
        ext2_scr[8:16, :] = nxt_scr[...]
        nxt_scr[...] = dpre[0:8, :]
        cw = cw_ref[...]
        u_b, dpre_b = p_ref[:, S_XS:S_DT].astype(_MXU), dpre.astype(_MXU)
        dxbc = cw[3:4, :] * dpre
        taps = [_colsum(dpre * p_ref[:, S_XS:S_DT])]
        for s in (1, 2, 3):
            up = (R["col"] - R["row"] == s).astype(_MXU)
            d_s = jnp.concatenate([jnp.dot(up, dpre_b, preferred_element_type=F32)[0:120],
                                   ext2_scr[s:8 + s, :]], axis=0)
            dxbc = dxbc + cw[3 - s:4 - s, :] * d_s
            taps.append(_colsum(dpre * _rows_from_above(u_b, s, ext_scr, R["row"], R["col"])))
        acc_cw_ref[...] += _rows8(taps[::-1] + [_colsum(dpre)])
        acc_w_ref[...] += _rows8([jnp.concatenate(dnw_parts, axis=1), _colsum(dY * xs_c)])
        acc_s_ref[...] += _rows8([_colsum(ddt_raw), _colsum(da * R["dt"])])

        lane = lax.broadcasted_iota(jnp.int32, (128, 128), 1)
        dp_ref[:, S_Z:S_Z + 1024] = dz.astype(dp_ref.dtype)
        dp_ref[:, S_XS:S_DT] = dxbc.astype(dp_ref.dtype)
        dp_ref[:, S_DT:S_DT + 128] = jnp.where(lane < N_HEADS, ddt_raw, 0.0).astype(dp_ref.dtype)
        dp_ref[:, S_DT + 128:S_W] = jnp.zeros((128, 128), dp_ref.dtype)

        @pl.when(i == nc - 1)
        def _():
            acc = acc_s_ref[...]
            dskip = _mm_exact_r(acc_w_ref[...], ET)[1:2, :]
            acc_s_ref[...] = _rows8([acc[0:1, :], acc[1:2, :] * R["A"], dskip])

    const = lambda shape: pl.BlockSpec(shape, lambda i: (0, 0))
    smem = pl.BlockSpec(memory_space=pltpu.SMEM)
    rev = lambda i: (nc - 1 - i, 0)
    return _call(
        body, comm, name="ssd_bwd", grid=(nc,),
        in_specs=[pl.BlockSpec((CHUNK, S_W), rev),
                  pl.BlockSpec((8, S_W), lambda i: (jnp.maximum((nc - 1 - i) * 16 - 1, 0), 0)),
                  pl.BlockSpec((128, 1024), rev),
                  pl.BlockSpec((CHUNK, D_SSD), rev),
                  pl.BlockSpec((CHUNK, D_XBC), rev),
                  pl.BlockSpec((CHUNK, D_SSD), rev),
                  const((4, D_XBC)), const((1, D_XBC)), smem, smem, smem, const((1, 1024)),
                  const((128, 1024)), const((1024, 128))],
        out_specs=[pl.BlockSpec((CHUNK, S_W), rev), const((8, D_XBC)), const((8, 1024)), const((8, 128))],
        out_shape=[jax.ShapeDtypeStruct((L, S_W), _MXU), jax.ShapeDtypeStruct((8, D_XBC), F32),
                   jax.ShapeDtypeStruct((8, 1024), F32), jax.ShapeDtypeStruct((8, 128), F32)],
        scratch_shapes=[pltpu.VMEM((128, 1024), F32), pltpu.VMEM((16, D_XBC), F32),
                        pltpu.VMEM((16, D_XBC), F32), pltpu.VMEM((8, D_XBC), F32)],
        args=(proj_ssd, proj_ssd, hprev_all, ypre, pre, dy, conv_w, conv_b, dt_bias, a_log, d_skip, norm_w, E, ET))


def _rope(t, tab):
    cos, sa, sb = tab[:, 0:128], tab[:, 128:256], tab[:, 256:384]
    outs = []
    for i in range(t.shape[1] // 128):
        tg = t[:, i * 128:(i + 1) * 128]
        outs.append(tg * cos + pltpu.roll(tg, 8, 1) * sa + pltpu.roll(tg, 120, 1) * sb)
    return jnp.concatenate(outs, axis=1)


def _rope_transposed(d, tab):
    cos, sa, sb = tab[:, 0:128], tab[:, 128:256], tab[:, 256:384]
    outs = []
    for i in range(d.shape[1] // 128):
        dg = d[:, i * 128:(i + 1) * 128]
        outs.append(dg * cos + pltpu.roll(dg * sa, 120, 1) + pltpu.roll(dg * sb, 8, 1))
    return jnp.concatenate(outs, axis=1)


def _lo_half(rows):
    return lax.broadcasted_iota(jnp.int32, (rows, 128), 1) < HEAD_DIM


def _native_half(rows, j):
    lo = _lo_half(rows)
    return lo if j % 2 == 0 else jnp.logical_not(lo)


def _kv_native(t, j):
    p = j // 2
    return jnp.where(_native_half(t.shape[0], j), t[:, p * 128:(p + 1) * 128], 0.0)


def _stack_heads(t, j):
    out = []
    for m in (2 * j, 2 * j + 1):
        pair = t[:, m * 128:(m + 1) * 128]
        swapped = pltpu.roll(pair, HEAD_DIM, 1)
        out += [pair, swapped] if j % 2 == 0 else [swapped, pair]
    return jnp.concatenate(out, axis=0)


def _unstack_heads(s, j):
    out = []
    for m in range(2):
        first, second = s[256 * m:256 * m + 128], s[256 * m + 128:256 * m + 256]
        if j % 2 == 0:
            out.append(first + pltpu.roll(second, HEAD_DIM, 1))
        else:
            out.append(pltpu.roll(first, HEAD_DIM, 1) + second)
    return jnp.concatenate(out, axis=1)


def _keep_native(r, j):
    return jnp.where(_native_half(r.shape[0], j), r, 0.0)


def _sink_row(sink_ref, j):
    hid = lax.broadcasted_iota(jnp.int32, (1, 4 * CHUNK), 1) // CHUNK
    row = jnp.zeros((1, 4 * CHUNK), F32)
    for hh in range(4):
        row = jnp.where(hid == hh, sink_ref[4 * j + hh], row)
    return row


def _from_current():
    si = lax.broadcasted_iota(jnp.int32, (CHUNK, 4 * CHUNK), 0)
    qi = lax.broadcasted_iota(jnp.int32, (CHUNK, 4 * CHUNK), 1) % CHUNK
    return si <= qi


def _fold(full, from_cur, pen=0.0):
    return jnp.where(from_cur, full[CHUNK:2 * CHUNK], full[0:CHUNK] + pen)


def _unfold(t, from_cur):
    c = jnp.where(from_cur, t, 0.0)
    return jnp.concatenate([t - c, c], axis=0)


def _softmax_sink(s, sink):
    mx = jnp.maximum(jnp.max(s, axis=0, keepdims=True), sink)
    p = jnp.exp(s - mx)
    esink = jnp.exp(sink - mx)
    inv = 1.0 / (jnp.sum(p, axis=0, keepdims=True) + esink)
    return p * inv, esink * inv


def _swa_inputs(blk, p_ref, prev_ref, tab_ref, ptab_ref):
    tab = tab_ref[...]
    qr = _rope(p_ref[:, A_Q:A_Q + 1024], tab) * ATT_SCALE
    kk = jnp.concatenate([_rope(prev_ref[:, 0:256], ptab_ref[...]), _rope(p_ref[:, A_K:A_K + 256], tab)], axis=0)
    vv = jnp.concatenate([prev_ref[:, 256:512], p_ref[:, A_V:A_V + 256]], axis=0)
    return tab, qr, kk, vv, jnp.where(blk > 0, 0.0, NEG)


def _swa_forward_step(sink_ref, p_ref, prev_ref, tab_ref, ptab_ref, y_ref):
    n = pl.program_id(0)
    _, qr, kk, vv, pen = _swa_inputs(n, p_ref, prev_ref, tab_ref, ptab_ref)
    from_cur = _from_current()
    outs = []
    for j in range(KV_HEADS):
        s = _fold(_mm_nt(_kv_native(kk, j), _stack_heads(qr, j)), from_cur, pen)
        P, _ = _softmax_sink(s, _sink_row(sink_ref, j))
        outs.append(_unstack_heads(_mm_tn(_unfold(P, from_cur), _kv_native(vv, j)), j))
    g = p_ref[:, A_G:A_G + 1024]
    y_ref[:, D_SSD:D_SSD + D_ATT] = (jnp.concatenate(outs, axis=1) * (g * _sigmoid(g))).astype(y_ref.dtype)


def _mixer_forward(proj_ssd, proj_att, tabs, sinks, conv_w, conv_b, dt_bias, a_log, d_skip, norm_w, E, comm=None):
    L = proj_ssd.shape[0]
    nc = L // CHUNK

    def body(p_ref, halo_ref, cw_ref, cb_ref, dtb_ref, alog_ref, dsk_ref, nw_ref, e_ref,
             sink_ref, pa_ref, prev_ref, tab_ref, ptab_ref, y_ref, ypre_ref, hprev_ref, pre_ref, h_scr, ext_scr):
        _ssd_forward_step(p_ref, halo_ref, cw_ref, cb_ref, dtb_ref, alog_ref, dsk_ref, nw_ref, e_ref,
                          y_ref, ypre_ref, hprev_ref, pre_ref, h_scr, ext_scr)
        _swa_forward_step(sink_ref, pa_ref, prev_ref, tab_ref, ptab_ref, y_ref)

    const = lambda shape: pl.BlockSpec(shape, lambda c: (0, 0))
    smem = pl.BlockSpec(memory_space=pltpu.SMEM)
    rows = lambda w: pl.BlockSpec((CHUNK, w), lambda c: (c, 0))
    return _call(
        body, comm, name="mixer_fwd", grid=(nc,),
        in_specs=[rows(S_W), pl.BlockSpec((8, S_W), lambda c: (jnp.maximum(c * 16 - 1, 0), 0)),
                  const((4, D_XBC)), const((1, D_XBC)), smem, smem, smem, const((1, 1024)), const((128, 1024)),
                  smem, rows(A_W), pl.BlockSpec((CHUNK, 512), lambda c: (jnp.maximum(c - 1, 0), 2)),
                  rows(384), pl.BlockSpec((CHUNK, 384), lambda c: (jnp.maximum(c - 1, 0), 0))],
        out_specs=[rows(D_SSD + D_ATT), rows(D_SSD), pl.BlockSpec((128, 1024), lambda c: (c, 0)), rows(D_XBC)],
        out_shape=[jax.ShapeDtypeStruct((L, D_SSD + D_ATT), _MXU), jax.ShapeDtypeStruct((L, D_SSD), F32),
                   jax.ShapeDtypeStruct((nc * 128, 1024), F32), jax.ShapeDtypeStruct((L, D_XBC), F32)],
        scratch_shapes=[pltpu.VMEM((128, 1024), F32), pltpu.VMEM((136, D_XBC), F32)],
        args=(proj_ssd, proj_ssd, conv_w, conv_b, dt_bias, a_log, d_skip, norm_w, E,
              sinks, proj_att, proj_att, tabs, tabs))


def _swa_backward(proj_att, tabs, sinks, dy, comm=None):
    L = proj_att.shape[0]
    nb = L // CHUNK

    def body(sink_ref, p_ref, prev_ref, tab_ref, ptab_ref, dy_ref, dp_ref, dsink_ref, carry_k, carry_v):
        i = pl.program_id(0)
        n = nb - 1 - i

        @pl.when(i == 0)
        def _():
            carry_k[...] = jnp.zeros_like(carry_k)
            carry_v[...] = jnp.zeros_like(carry_v)
            dsink_ref[...] = jnp.zeros_like(dsink_ref)

        tab, qr, kk, vv, pen = _swa_inputs(n, p_ref, prev_ref, tab_ref, ptab_ref)
        from_cur = _from_current()
        g = p_ref[:, A_G:A_G + 1024]
        sgm = _sigmoid(g)
        dyv = dy_ref[...]
        do_all = dyv * (g * sgm)
        lane8 = lax.broadcasted_iota(jnp.int32, (8, 128), 1)
        hid = lax.broadcasted_iota(jnp.int32, (1, 4 * CHUNK), 1) // CHUNK
        o_parts, dq_parts = [], []
        dk_nat = [jnp.zeros((2 * CHUNK, 128), F32) for _ in range(2)]
        dv_nat = [jnp.zeros((2 * CHUNK, 128), F32) for _ in range(2)]
        dsink = jnp.zeros((8, 128), F32)
        for j in range(KV_HEADS):
            qs = _stack_heads(qr, j)
            kkb, vvb = _kv_native(kk, j), _kv_native(vv, j)
            P, psink = _softmax_sink(_fold(_mm_nt(kkb, qs), from_cur, pen), _sink_row(sink_ref, j))
            p_full = _unfold(P, from_cur)
            o_parts.append(_unstack_heads(_mm_tn(p_full, vvb), j))
            do_s = _stack_heads(do_all, j)
            dP = _fold(_mm_nt(vvb, do_s), from_cur)
            D = jnp.sum(P * dP, axis=0, keepdims=True)
            ds_full = _unfold(P * (dP - D), from_cur)
            sd = psink * D
            for hh in range(4):
                dsink = dsink + jnp.where(lane8 == 4 * j + hh, -jnp.sum(jnp.where(hid == hh, sd, 0.0)), 0.0)
            dq_parts.append(_unstack_heads(_mm_tn(ds_full, kkb), j) * ATT_SCALE)
            dk_nat[j // 2] = dk_nat[j // 2] + _keep_native(_mm(ds_full, qs), j)
            dv_nat[j // 2] = dv_nat[j // 2] + _keep_native(_mm(p_full, do_s), j)
        o = jnp.concatenate(o_parts, axis=1)
        dkk = jnp.concatenate(dk_nat, axis=1)
        dvv = jnp.concatenate(dv_nat, axis=1)
        out = dp_ref.dtype
        dp_ref[:, A_Q:A_Q + 1024] = _rope_transposed(jnp.concatenate(dq_parts, axis=1), tab).astype(out)
        dp_ref[:, A_K:A_K + 256] = _rope_transposed(dkk[CHUNK:2 * CHUNK] + carry_k[...], tab).astype(out)
        dp_ref[:, A_V:A_V + 256] = (dvv[CHUNK:2 * CHUNK] + carry_v[...]).astype(out)
        dp_ref[:, A_G:A_G + 1024] = (dyv * o * (sgm * (1.0 + g * (1.0 - sgm)))).astype(out)
        carry_k[...] = dkk[0:CHUNK]
        carry_v[...] = dvv[0:CHUNK]
        dsink_ref[...] += dsink

    rev = lambda i: (nb - 1 - i, 0)
    prev = lambda i: jnp.maximum(nb - 2 - i, 0)
    return _call(
        body, comm, name="swa_bwd", grid=(nb,),
        in_specs=[pl.BlockSpec(memory_space=pltpu.SMEM),
                  pl.BlockSpec((CHUNK, A_W), rev),
                  pl.BlockSpec((CHUNK, 512), lambda i: (prev(i), 2)),
                  pl.BlockSpec((CHUNK, 384), rev),
                  pl.BlockSpec((CHUNK, 384), lambda i: (prev(i), 0)),
                  pl.BlockSpec((CHUNK, D_ATT), lambda i: (nb - 1 - i, 1))],
        out_specs=[pl.BlockSpec((CHUNK, A_W), rev), pl.BlockSpec((8, 128), lambda i: (0, 0))],
        out_shape=[jax.ShapeDtypeStruct((L, A_W), _MXU), jax.ShapeDtypeStruct((8, 128), F32)],
        scratch_shapes=[pltpu.VMEM((CHUNK, 256), F32), pltpu.VMEM((CHUNK, 256), F32)],
        args=(sinks, proj_att, proj_att, tabs, tabs, dy))


def _head(y, x, target, w_out, ln_g, ln_b, *, tm):
    L = x.shape[0]
    nsteps = L // tm

    def body(y_ref, x_ref, t_ref, wo_ref, g_ref, b_ref, dr_ref, dy_ref, acc_ref):
        i = pl.program_id(0)

        @pl.when(i == 0)
        def _():
            acc_ref[...] = jnp.zeros_like(acc_ref)

        r = ALPHA * x_ref[...] + _mm(y_ref[...], wo_ref[...])
        mu = jnp.mean(r, axis=-1, keepdims=True)
        d = r - mu
        rstd = lax.rsqrt(jnp.mean(d * d, axis=-1, keepdims=True) + LN_EPS)
        xh = d * rstd
        gam = g_ref[0:1, :]
        e = xh * gam + b_ref[0:1, :] - t_ref[...]
        dout = e * (1.0 / D_MODEL)
        dxh = dout * gam
        dr = rstd * (dxh - jnp.mean(dxh, axis=-1, keepdims=True)
                     - xh * jnp.mean(dxh * xh, axis=-1, keepdims=True))
        dr_ref[...] = dr
        dy_ref[...] = _mm_nt(dr, wo_ref[...])
        acc_ref[...] += _rows8([_colsum(dout * xh), _colsum(dout), _colsum(e * e) * (0.5 / D_MODEL)])

        @pl.when(i == nsteps - 1)
        def _():
            acc = acc_ref[...]
            tot = jnp.sum(acc[2:3, :])
            rid = lax.broadcasted_iota(jnp.int32, (8, 1024), 0)
            acc_ref[...] = jnp.where(rid == 3, tot, acc)

    const = lambda shape: pl.BlockSpec(shape, lambda i: (0, 0))
    row = lambda w: pl.BlockSpec((tm, w), lambda i: (i, 0))
    return pl.pallas_call(
        body, name="head", grid=(nsteps,),
        in_specs=[row(2048), row(1024), row(1024), const((2048, 1024)), const((1, 1024)), const((1, 1024))],
        out_specs=[row(1024), row(2048), const((8, 1024))],
        out_shape=[jax.ShapeDtypeStruct((L, D_MODEL), F32), jax.ShapeDtypeStruct((L, 2048), F32),
                   jax.ShapeDtypeStruct((8, 1024), F32)],
        compiler_params=_params(("arbitrary",)),
    )(y, x, target, w_out, ln_g, ln_b)


def _gather_w_in(w_shard):
    R = w_shard.shape[0]
    halves = (pl.ds(0, R // 2), pl.ds(R // 2, R // 2))
    any_spec = pl.BlockSpec(memory_space=pl.ANY)

    def body(in_ref, out_ref, send_sems, recv_sems, local_sem):
        x, y, c = _position()

        def slot(p, half=None):
            s = out_ref.at[_index(*p)]
            return s if half is None else s.at[halves[half]]

        def same_core(p):
            return (p[0], p[1], c)

        def other_core(p):
            return (p[0], p[1], 1 - c)

        me, xn, yn, dg = (x, y), (1 - x, y), (x, 1 - y), (1 - x, 1 - y)

        def copy(k, dst, to, src=None):
            return _remote(dst if src is None else src, dst, send_sems.at[k], recv_sems.at[k], to)

        local = pltpu.make_async_copy(in_ref, slot(same_core(me)), local_sem)
        local.start()
        own = [copy(0, slot(same_core(me)), other_core(me), in_ref), copy(1, slot(same_core(me)), same_core(xn), in_ref),
               copy(2, slot(same_core(me)), same_core(yn), in_ref)]
        for cp in own:
            cp.start()
        copy(1, slot(same_core(xn)), same_core(xn)).wait_recv()
        passed = [copy(4, slot(same_core(xn), 1), same_core(yn)), copy(5, slot(same_core(xn)), other_core(me))]
        for cp in passed:
            cp.start()
        copy(2, slot(same_core(yn)), same_core(yn)).wait_recv()
        more = [copy(3, slot(same_core(yn), 0), same_core(xn)), copy(6, slot(same_core(yn)), other_core(me))]
        for cp in more:
            cp.start()
        passed += more
        for k, half in ((3, 0), (4, 1)):
            copy(k, slot(same_core(dg), half), same_core(xn)).wait_recv()
            fwd = copy(7 + half, slot(same_core(dg), half), other_core(me))
            fwd.start()
            passed.append(fwd)
        copy(0, slot(other_core(me)), other_core(me)).wait_recv()
        copy(5, slot(other_core(xn)), other_core(me)).wait_recv()
        copy(6, slot(other_core(yn)), other_core(me)).wait_recv()
        for half in (0, 1):
            copy(7 + half, slot(other_core(dg), half), other_core(me)).wait_recv()
        for cp in own + passed:
            cp.wait_send()
        local.wait()

    return pl.pallas_call(
        body, name="gather_w_in", in_specs=[any_spec], out_specs=any_spec,
        out_shape=jax.ShapeDtypeStruct((N_DEV,) + w_shard.shape, w_shard.dtype),
        scratch_shapes=[pltpu.SemaphoreType.DMA((9,)), pltpu.SemaphoreType.DMA((9,)), pltpu.SemaphoreType.DMA],
    )(w_shard)


def _input_gradient(d_ssd, d_att, w_ssd, w_att, dr, *, tm, comm=None, stack_hi=None):
    L = dr.shape[0]
    steps = L // tm

    def matmuls(ds_ref, da_ref, ws_ref, wa_ref, dr_ref, o_ref):
        o_ref[...] = ALPHA * dr_ref[...] + _mm_nt(ds_ref[...], ws_ref[...]) + _mm_nt(da_ref[...], wa_ref[...])

    row = lambda w: pl.BlockSpec((tm, w), lambda i: (i, 0))
    resident = lambda a: pl.BlockSpec(a.shape, lambda i: (0, 0), pipeline_mode=pl.Buffered(1))
    in_specs = [row(S_W), row(A_W), resident(w_ssd), resident(w_att), row(D_MODEL)]
    out_dx = jax.ShapeDtypeStruct((L, D_MODEL), F32)
    args = (d_ssd, d_att, w_ssd, w_att, dr)
    if stack_hi is None:
        return _call(matmuls, comm, name="dx", grid=(steps,), in_specs=in_specs, out_specs=[row(D_MODEL)],
                     out_shape=[out_dx], scratch_shapes=[], args=args)

    block = stack_hi.shape[1:]
    c_in, c_out, rows = len(comm.operands), len(comm.out_shapes), len(comm.flows)
    FIRST_STEP, SECOND_STEP, SEND_STEPS, REDUCE_STEPS = 2, 4, (2, 4, 6, 8), (5, 8, 10, 13)
    chunks = len(SEND_STEPS)
    chunk_rows = block[0] // chunks
    assert FIRST_STEP <= SEND_STEPS[0] and SECOND_STEP < REDUCE_STEPS[0] and REDUCE_STEPS[-1] < steps - 1

    def body(ds_ref, da_ref, ws_ref, wa_ref, dr_ref, stack_ref, *refs):
        cins, refs = refs[:c_in], refs[c_in:]
        (o_ref, own_ref), refs = refs[:2], refs[2:]
        couts, refs = refs[:c_out], refs[c_out:]
        (theirs_scr, mine_scr, first_scr, across_scr, out_scr, send_sems, recv_sems, local_sems, y_send_sems, y_recv_sems,
         x_send_sems, x_recv_sems, swap_send_sems, swap_recv_sems, mine_sems) = refs
        i = pl.program_id(0)
        x, y, c = _position()
        sibling, across, owner = (x, y, 1 - c), (x, 1 - y, c), (1, y, c)
        order = (1 - y, y)
        start, wait = comm.plan(cins, couts, send_sems, recv_sems, local_sems)
        swaps = [_remote(stack_ref.at[2 * order[j] + (1 - c)], theirs_scr.at[j], swap_send_sems.at[j], swap_recv_sems.at[j],
                         sibling) for j in range(2)]
        mine = [pltpu.make_async_copy(stack_ref.at[2 * order[j] + c], mine_scr.at[j], mine_sems.at[j]) for j in range(2)]
        part = [pl.ds(j * chunk_rows, chunk_rows) for j in range(chunks)]
        y_sems = lambda j: (y_send_sems.at[j], y_recv_sems.at[j])
        to_neighbour = [_remote(first_scr.at[part[j]], across_scr.at[part[j]], *y_sems(j), across) for j in range(chunks)]
        to_owner_y = [_remote(first_scr.at[part[j]], couts[0].at[0, part[j]], *y_sems(j), across) for j in range(chunks)]
        to_owner_x = [_remote(out_scr.at[part[j]], couts[0].at[1, part[j]], x_send_sems.at[j], x_recv_sems.at[j], owner)
                      for j in range(chunks)]

        @pl.when(i == 0)
        def _():
            for cp in [swaps[0]] + mine:
                cp.start()
            start()

        @pl.when(i == 1)
        def _():
            swaps[1].start()

        matmuls(ds_ref, da_ref, ws_ref, wa_ref, dr_ref, o_ref)

        @pl.when(i == FIRST_STEP)
        def _():
            swaps[0].wait_recv()
            mine[0].wait()
            first_scr[...] = (mine_scr[0] + theirs_scr[0]).astype(first_scr.dtype)

        for j in range(chunks):
            pl.when((i == SEND_STEPS[j]) & (x == 0))(to_neighbour[j].start)
            pl.when((i == SEND_STEPS[j]) & (x == 1))(to_owner_y[j].start)

        @pl.when(i == SECOND_STEP)
        def _():
            swaps[1].wait_recv()
            mine[1].wait()
            t = mine_scr[1] + theirs_scr[1]
            own_ref[...] = t
            mine_scr[1] = t

        for j in range(chunks):
            @pl.when((i == REDUCE_STEPS[j]) & (x == 0))
            def _(j=j):
                to_neighbour[j].wait_recv()
                t = mine_scr[1, part[j], :] + across_scr[part[j], :].astype(F32)
                out_scr[part[j], :] = t.astype(out_scr.dtype)
                to_owner_x[j].start()

        @pl.when(i == steps - 1)
        def _():
            wait()
            for cp in swaps:
                cp.wait_send()
            for j in range(chunks):
                @pl.when(x == 0)
                def _(j=j):
                    to_neighbour[j].wait_send()
                    to_owner_x[j].wait_send()

                @pl.when(x == 1)
                def _(j=j):
                    to_owner_y[j].wait_send()
                    to_owner_y[j].wait_recv()
                    to_owner_x[j].wait_recv()

    io_alias = {6 + ci: 2 + co for ci, co in comm.aliases.items()}
    any_spec = pl.BlockSpec(memory_space=pl.ANY)
    dma = pltpu.SemaphoreType.DMA
    return pl.pallas_call(
        body, name="dx", grid=(steps,), in_specs=in_specs + [any_spec] * (1 + c_in),
        out_specs=[row(D_MODEL), pl.BlockSpec(block, lambda i: (0, 0), pipeline_mode=pl.Buffered(1))] + [any_spec] * c_out,
        out_shape=[out_dx, jax.ShapeDtypeStruct(block, F32)] + list(comm.out_shapes),
        scratch_shapes=[pltpu.VMEM((2,) + block, F32), pltpu.VMEM((2,) + block, F32), pltpu.VMEM(block, BF16),
                        pltpu.VMEM(block, BF16), pltpu.VMEM(block, BF16),
                        dma((rows, N_DEV - 1)), dma((rows, N_DEV - 1)), dma((rows,)), dma((chunks,)), dma((chunks,)),
                        dma((chunks,)), dma((chunks,)), dma((2,)), dma((2,)), dma((2,))],
        input_output_aliases=io_alias, compiler_params=_params(("arbitrary",)),
    )(*args, stack_hi, *comm.operands)


SHARD_COLS = D_IN_PROJ // N_DEV
SPLIT = N_SSD_REAL - 4 * SHARD_COLS
RELAYOUT_ROWS = 256


def _unpack_w_in(w_all):
    def body(g_ref, ws_ref, wa_ref):
        for j in range(4):
            ws_ref[:, SHARD_COLS * j:SHARD_COLS * (j + 1)] = g_ref[j]
        ws_ref[:, 4 * SHARD_COLS:N_SSD_REAL] = g_ref[4, :, 0:SPLIT]
        ws_ref[:, N_SSD_REAL:S_W] = jnp.zeros((RELAYOUT_ROWS, S_W - N_SSD_REAL), ws_ref.dtype)
        wa_ref[:, 0:SHARD_COLS - SPLIT] = g_ref[4, :, SPLIT:SHARD_COLS]
        for j in range(5, N_DEV):
            lo = SHARD_COLS * (j - 4) - SPLIT
            wa_ref[:, lo:lo + SHARD_COLS] = g_ref[j]

    return pl.pallas_call(
        body, name="unpack_w_in", grid=(D_MODEL // RELAYOUT_ROWS,),
        in_specs=[pl.BlockSpec((N_DEV, RELAYOUT_ROWS, SHARD_COLS), lambda i: (0, i, 0))],
        out_specs=[pl.BlockSpec((RELAYOUT_ROWS, S_W), lambda i: (i, 0)), pl.BlockSpec((RELAYOUT_ROWS, A_W), lambda i: (i, 0))],
        out_shape=[jax.ShapeDtypeStruct((D_MODEL, S_W), w_all.dtype), jax.ShapeDtypeStruct((D_MODEL, A_W), w_all.dtype)],
        compiler_params=_params(("arbitrary",)),
    )(w_all)


def _dw_in(me1, xb, d, half, tail=None, *, tl=1024, comm=None):
    L, N = d.shape
    steps = L // tl

    def body(me_ref, x_ref, d_ref, *refs):
        if half == 0:
            p_ref, own_ref, tail_ref, acc = refs
        else:
            t_ref, p_ref, acc = refs
        l = pl.program_id(0)

        @pl.when(l == 0)
        def _():
            acc[...] = jnp.zeros_like(acc)

        acc[...] += _mm_tn(x_ref[...], d_ref[...])

        @pl.when(l == steps - 1)
        def _():
            if half == 0:
                me = me_ref[0]

                @pl.when(me >= 4)
                def _():
                    own_ref[...] = jnp.zeros_like(own_ref)

                tail_ref[...] = acc[:, S_DT:S_W]
            for j in range(4):
                if half == 0:
                    pieces = [(0, acc[:, SHARD_COLS * j:SHARD_COLS * (j + 1)])]
                elif j == 0:
                    pieces = [(0, t_ref[:, 4 * SHARD_COLS - S_DT:N_SSD_REAL - S_DT]), (SPLIT, acc[:, 0:SHARD_COLS - SPLIT])]
                else:
                    lo = SHARD_COLS * j - SPLIT
                    pieces = [(0, acc[:, lo:lo + SHARD_COLS])]
                for off, blk in pieces:
                    p_ref[j, :, off:off + blk.shape[1]] = blk.astype(p_ref.dtype)
                    if half == 0:
                        @pl.when(me == j)
                        def _(off=off, blk=blk):
                            own_ref[:, off:off + blk.shape[1]] = blk

    once = pl.Buffered(1)
    whole = lambda shape: pl.BlockSpec(shape, lambda l: (0,) * len(shape), pipeline_mode=once)
    in_specs = [pl.BlockSpec(memory_space=pltpu.SMEM), pl.BlockSpec((tl, D_MODEL), lambda l: (l, 0)),
                pl.BlockSpec((tl, N), lambda l: (l, 0))]
    args = [me1, xb, d]
    if half == 0:
        out_shape = [jax.ShapeDtypeStruct((4, D_MODEL, SHARD_COLS), BF16), jax.ShapeDtypeStruct((D_MODEL, SHARD_COLS), F32),
                     jax.ShapeDtypeStruct((D_MODEL, S_W - S_DT), F32)]
    else:
        in_specs.append(whole(tail.shape))
        args.append(tail)
        out_shape = [jax.ShapeDtypeStruct((4, D_MODEL, SHARD_COLS), F32)]
    return _call(body, comm, name="dw_in_%d" % half, grid=(steps,), in_specs=in_specs,
                 out_specs=[whole(o.shape) for o in out_shape], out_shape=out_shape,
                 scratch_shapes=[pltpu.VMEM((D_MODEL, N), F32)], args=args)


def _adamw_math(w, g, m, v):
    m = ADAM_B1 * m + (1.0 - ADAM_B1) * g
    v = ADAM_B2 * v + (1.0 - ADAM_B2) * (g * g)
    m_hat = m / (1.0 - ADAM_B1 ** ADAM_STEP)
    v_hat = v / (1.0 - ADAM_B2 ** ADAM_STEP)
    delta = -ADAM_LR * (m_hat / (jnp.sqrt(v_hat) + ADAM_EPS) + ADAM_WD * w)
    return delta, m, v


def _adamw_shard(n_recv, g_own, recv, w, m, v, *, rows, name):
    R, C = g_own.shape

    def body(n_ref, g_ref, r_ref, w_ref, m_ref, v_ref, go_ref, d_ref, mo_ref, vo_ref):
        g = g_ref[...]
        for k in range(N_DEV - 1):
            g = g + jnp.where(k < n_ref[0], r_ref[k].astype(F32), 0.0)
        d, mn, vn = _adamw_math(w_ref[...], g, m_ref[...], v_ref[...])
        go_ref[...] = g
        d_ref[...] = d
        mo_ref[...] = mn
        vo_ref[...] = vn

    blk = pl.BlockSpec((rows, C), lambda i: (i, 0))
    return pl.pallas_call(
        body, name=name, grid=(R // rows,),
        in_specs=[pl.BlockSpec(memory_space=pltpu.SMEM), blk,
                  pl.BlockSpec((N_DEV - 1, rows, C), lambda i: (0, i, 0)), blk, blk, blk],
        out_specs=[blk] * 4, out_shape=[jax.ShapeDtypeStruct((R, C), F32)] * 4,
        compiler_params=_params(("arbitrary",)),
    )(n_recv, g_own, recv, w, m, v)


def _minor_rows_view(a):
    return jnp.transpose(a, (2, 0, 1)).reshape(SHARD_COLS * 8, 128)


def _from_minor_rows_view(v):
    return jnp.transpose(v.reshape(SHARD_COLS, 8, 128), (1, 2, 0)).reshape(1, D_MODEL, SHARD_COLS)


def _adamw_w_in(n_recv, g_lo, g_hi, recv, w, m, v):
    C = SHARD_COLS
    pad = -C % 128

    def body(n_ref, lo_ref, hi_ref, r_ref, w_ref, m_ref, v_ref, go_ref, d_ref, mo_ref, vo_ref):
        for q in range(D_MODEL // 128):
            band = pl.ds(q * 128, 128)
            g = jnp.where(n_ref[0] == N_DEV - 1, lo_ref[band, :], hi_ref[band, :])
            for k in range(N_DEV - 1):
                g = g + jnp.where(k < n_ref[0], r_ref[k, band, :].astype(F32), 0.0)
            g = jnp.pad(g, ((0, 0), (0, pad))).T[0:C]
            rows = pl.ds(q, C, stride=8)
            d, mn, vn = _adamw_math(w_ref[rows, :], g, m_ref[rows, :], v_ref[rows, :])
            go_ref[rows, :] = g
            d_ref[rows, :] = d
            mo_ref[rows, :] = mn
            vo_ref[rows, :] = vn

    return pl.pallas_call(
        body, name="adamw_w_in", out_shape=[jax.ShapeDtypeStruct(w.shape, F32)] * 4,
        in_specs=[pl.BlockSpec(memory_space=pltpu.SMEM)] + [pl.BlockSpec(memory_space=pltpu.VMEM)] * 6,
        out_specs=[pl.BlockSpec(memory_space=pltpu.VMEM)] * 4,
        compiler_params=_params(),
    )(n_recv, g_lo, g_hi, recv, w, m, v)


SMALL = ("conv_b", "dt_bias", "a_log", "d_skip", "ssd_norm_w", "attn_sinks", "ln_g", "ln_b")


def _adamw_small(gathered, params):
    n_p = len(SMALL)

    def body(*refs):
        acc = []
        for r in refs[:5]:
            t = r[0]
            for k in range(1, N_DEV):
                t = t + r[k]
            acc.append(t)
        head, conv, norm, scal, sink = acc
        grads = dict(conv_b=conv[4:5, :], dt_bias=scal[0:1, 0:N_HEADS], a_log=scal[1:2, 0:N_HEADS],
                     d_skip=scal[2:3, 0:N_HEADS], ssd_norm_w=norm[0:1, :], attn_sinks=sink[0:1, 0:N_HEADS],
                     ln_g=head[0:1, :], ln_b=head[1:2, :])
        wmv = refs[5:5 + 3 * n_p]
        outs = refs[5 + 3 * n_p:]
        outs[0][...] = head[3:4, 0:1]
        outs[1][...] = conv[0:4, :]
        for i, name in enumerate(SMALL):
            w_ref, m_ref, v_ref = wmv[3 * i:3 * i + 3]
            g = grads[name]
            d, mn, vn = _adamw_math(w_ref[...], g, m_ref[...], v_ref[...])
            for o_ref, val in zip(outs[2 + 4 * i:6 + 4 * i], (g, d, mn, vn)):
                o_ref[...] = val

    flat = [a for name in SMALL for a in params[name]]
    out_shape = [jax.ShapeDtypeStruct((1, 1), F32), jax.ShapeDtypeStruct((4, D_XBC), F32)]
    for name in SMALL:
        out_shape += [jax.ShapeDtypeStruct(params[name][0].shape, F32)] * 4
    res = pl.pallas_call(body, name="adamw_small", out_shape=out_shape, compiler_params=_params())(*gathered, *flat)
    return res[0], res[1], {name: res[2 + 4 * i:6 + 4 * i] for i, name in enumerate(SMALL)}


def _adamw_plain(g, w, m, v):
    def body(g_ref, w_ref, m_ref, v_ref, d_ref, mo_ref, vo_ref):
        d, mn, vn = _adamw_math(w_ref[...], g_ref[...], m_ref[...], v_ref[...])
        d_ref[...] = d
        mo_ref[...] = mn
        vo_ref[...] = vn

    return pl.pallas_call(
        body, name="adamw_conv_w", out_shape=[jax.ShapeDtypeStruct(w.shape, F32)] * 3,
        compiler_params=_params(),
    )(g, w, m, v)


def _lane_pattern(fn):
    return np.asarray([fn(l % HEAD_DIM) for l in range(128)], np.float32)


ROPE_INV = _lane_pattern(lambda r: ROPE_THETA ** (-2.0 * (r % 8) / ROPE_DIM) if r < ROPE_DIM else 0.0)
ROPE_SIN_A = _lane_pattern(lambda r: 1.0 if 8 <= r < ROPE_DIM else 0.0)
ROPE_SIN_B = _lane_pattern(lambda r: -1.0 if r < 8 else 0.0)


def _rope_tables(positions):
    ang = positions.astype(F32)[:, None] * ROPE_INV[None, :]
    sn = jnp.sin(ang)
    return jnp.concatenate([jnp.cos(ang), sn * ROPE_SIN_A[None, :], sn * ROPE_SIN_B[None, :]], axis=1)


def _expansion():
    E = np.arange(1024)[None, :] // HEAD_DIM == np.arange(128)[:, None]
    return jnp.asarray(E, BF16), jnp.asarray(E.T, BF16)


def _ssd_args(conv_w, conv_b, dt_bias, a_log, d_skip, norm_w, E):
    return (conv_w, conv_b, dt_bias.reshape(-1), a_log.reshape(-1), d_skip.reshape(-1), norm_w, E)


def kernel(x, positions, w_in, conv_w, conv_b, dt_bias, a_log, d_skip, ssd_norm_w, attn_sinks, w_out, ln_g, ln_b, loss_target, m_w_in, m_conv_w, m_conv_b, m_dt_bias, m_a_log, m_d_skip, m_ssd_norm_w, m_attn_sinks, m_w_out, m_ln_g, m_ln_b, v_w_in, v_conv_w, v_conv_b, v_dt_bias, v_a_log, v_d_skip, v_ssd_norm_w, v_attn_sinks, v_w_out, v_ln_g, v_ln_b):
    me = _index(*_position())
    me1 = me.reshape(1).astype(jnp.int32)
    x0, target = x[0], loss_target[0]
    bf16_shard = lambda shape: jax.ShapeDtypeStruct(shape, BF16)
    E, ET = _expansion()
    tabs = _rope_tables(positions[0])
    sinks = attn_sinks.reshape(-1)

    w_ssd, w_att = _unpack_w_in(_gather_w_in(w_in[0].astype(BF16)))
    gather_conv_w = _Hosted([conv_w[0]], [jax.ShapeDtypeStruct((N_DEV,) + conv_w.shape[1:], F32)],
                            [_Flow("gather", 0, 0)])

    proj_ssd, proj_att, xb, conv_w_all = _in_proj(x0, w_ssd, w_att, tm=512, comm=gather_conv_w)
    conv_w_f = jnp.transpose(conv_w_all, (1, 0, 2)).reshape(4, D_XBC)
    ssd_args = _ssd_args(conv_w_f, conv_b, dt_bias, a_log, d_skip, ssd_norm_w, E)
    gather_w_out = _Hosted([w_out[0].astype(BF16)], [bf16_shard((N_DEV, 256, D_MODEL))], [_Flow("gather", 0, 0)])
    y, ypre, hprev, pre, w_out_all = _mixer_forward(proj_ssd, proj_att, tabs, sinks, *ssd_args, comm=gather_w_out)
    w_out_f = w_out_all.reshape(2 * D_MODEL, D_MODEL)
    dr, dy, acc_head = _head(y, x0, target, w_out_f, ln_g, ln_b, tm=512)

    dw_out, dw_out_bf16 = _matmul_tn(y, dr, tl=1024, tn=D_MODEL, name="dw_out", emit_bf16=True)
    own_out = lax.dynamic_index_in_dim(dw_out.reshape(N_DEV, 256, D_MODEL), me, axis=0, keepdims=False)
    send_out = _Hosted([dw_out_bf16.reshape(N_DEV, 256, D_MODEL)], [bf16_shard((N_DEV - 1, 256, D_MODEL))],
                       [_Flow("exchange", 0, 0)])
    d_ssd, acc_cw, acc_w, acc_s, recv_out = _ssd_backward(proj_ssd, hprev, ypre, pre, dy, *ssd_args, ET, comm=send_out)
    parts_lo, own_lo, dw_dt_block = _dw_in(me1, xb, d_ssd, 0)
    recv_shape = bf16_shard((N_DEV - 1, D_MODEL, SHARD_COLS))
    quarter = D_MODEL // 4
    lo_flow = lambda operand, c, rows=None: _Flow("exchange", operand, 0, target_x=0, target_c=c, rows=rows)
    send_lo = _Hosted([parts_lo], [recv_shape], [lo_flow(0, 0), lo_flow(0, 1, (0, quarter))])
    d_att, dsink, recv_in = _swa_backward(proj_att, tabs, sinks, dy, comm=send_lo)
    send_more = _Hosted([recv_in, parts_lo], [recv_shape], [lo_flow(1, 1, (quarter, 2 * quarter))], aliases={0: 0})
    stack_hi, recv_in = _dw_in(me1, xb, d_att, 1, dw_dt_block, comm=send_more)
    accs = [acc_head, acc_cw, acc_w, acc_s, dsink]
    send_rest = _Hosted([recv_in, parts_lo] + accs,
                        [recv_shape] + [jax.ShapeDtypeStruct((N_DEV,) + a.shape, F32) for a in accs],
                        [lo_flow(1, 1, (3 * quarter, quarter))] + [_Flow("gather", 2 + i, 1 + i) for i in range(5)],
                        aliases={0: 0})
    dx, own_hi, recv_in, *gathered = _input_gradient(d_ssd, d_att, w_ssd, w_att, dr, tm=256, comm=send_rest,
                                                     stack_hi=stack_hi)
    n_recv_in = jnp.where(me < 4, N_DEV - 1, 2).reshape(1).astype(jnp.int32)
    n_recv_out = jnp.full((1,), N_DEV - 1, jnp.int32)

    g_in, d_in, nm_in, nv_in = [_from_minor_rows_view(r) for r in _adamw_w_in(
        n_recv_in, own_lo, own_hi, recv_in, _minor_rows_view(w_in), _minor_rows_view(m_w_in), _minor_rows_view(v_w_in))]
    g_out, d_out, nm_out, nv_out = _adamw_shard(n_recv_out, own_out, recv_out, w_out[0], m_w_out[0], v_w_out[0],
                                                rows=256, name="adamw_w_out")
    loss, g_conv_w, small = _adamw_small(gathered, dict(
        conv_b=(conv_b, m_conv_b, v_conv_b), dt_bias=(dt_bias, m_dt_bias, v_dt_bias), a_log=(a_log, m_a_log, v_a_log),
        d_skip=(d_skip, m_d_skip, v_d_skip), ssd_norm_w=(ssd_norm_w, m_ssd_norm_w, v_ssd_norm_w),
        attn_sinks=(attn_sinks, m_attn_sinks, v_attn_sinks), ln_g=(ln_g, m_ln_g, v_ln_g), ln_b=(ln_b, m_ln_b, v_ln_b)))
    g_cw = lax.dynamic_slice_in_dim(g_conv_w, me * (D_XBC // N_DEV), D_XBC // N_DEV, axis=1)
    d_cw, nm_cw, nv_cw = _adamw_plain(g_cw, conv_w[0], m_conv_w[0], v_conv_w[0])

    def leaves(i, big_in, cw, big_out):
        mid = [small[k][i] for k in ("conv_b", "dt_bias", "a_log", "d_skip", "ssd_norm_w", "attn_sinks")]
        return [big_in, cw[None]] + mid + [big_out[None], small["ln_g"][i], small["ln_b"][i]]

    return (loss.reshape(()), dx[None], *leaves(0, g_in, g_cw, g_out), *leaves(1, d_in, d_cw, d_out),
            *leaves(2, nm_in, nm_cw, nm_out), *leaves(3, nv_in, nv_cw, nv_out))
```

```python
import jax
import jax.numpy as jnp
from jax import lax
from jax.experimental import pallas as pl
from jax.experimental.pallas import tpu as pltpu
import numpy as np

F32 = jnp.float32
BF16 = jnp.bfloat16
_MXU = jnp.bfloat16

N_DEV = 8
D_MODEL = 1024
D_SSD = 1024
D_ATT = 1024
HEAD_DIM = 64
N_HEADS = 16
SSD_GROUPS = 2
KV_HEADS = 4
CHUNK = 128
D_XBC = 1536
D_IN_PROJ = 5136
ROPE_DIM = 16
ROPE_THETA = 500000.0
ALPHA = (2.0 * 1) ** 0.25
LN_EPS = 1e-5
RMS_EPS = 1e-5
ATT_SCALE = HEAD_DIM ** -0.5
NEG = -1e30

S_Z, S_XS, S_B, S_C, S_DT, S_W = 0, 1024, 2048, 2304, 2560, 2816
N_SSD_REAL = 2576
A_Q, A_K, A_V, A_G, A_W = 0, 1024, 1280, 1536, 2560

ADAM_LR = 0.001
ADAM_B1 = 0.9
ADAM_B2 = 0.999
ADAM_EPS = 1e-08
ADAM_WD = 0.01
ADAM_STEP = 10

VMEM_LIMIT = 48 * 1024 * 1024
MESH = pl.DeviceIdType.MESH


def _params(sem=None):
    return pltpu.CompilerParams(dimension_semantics=sem, vmem_limit_bytes=VMEM_LIMIT)


def _mm(a, b):
    return jnp.dot(a.astype(_MXU), b.astype(_MXU), preferred_element_type=F32)


def _mm_nt(a, b):
    return lax.dot_general(a.astype(_MXU), b.astype(_MXU), (((1,), (1,)), ((), ())),
                           preferred_element_type=F32)


def _mm_tn(a, b):
    return lax.dot_general(a.astype(_MXU), b.astype(_MXU), (((0,), (0,)), ((), ())),
                           preferred_element_type=F32)


def _split3(v):
    hi = v.astype(BF16)
    r = v - hi.astype(F32)
    mid = r.astype(BF16)
    lo = (r - mid.astype(F32)).astype(BF16)
    return hi, mid, lo


def _mm_exact_r(v, p01):
    hi, mid, lo = _split3(v)
    d = lambda a: jnp.dot(a, p01, preferred_element_type=F32)
    return d(hi) + d(mid) + d(lo)


def _mm_exact_l(p01, v):
    hi, mid, lo = _split3(v)
    d = lambda a: jnp.dot(p01, a, preferred_element_type=F32)
    return d(hi) + d(mid) + d(lo)


def _mm_2pass_r(v, p01):
    hi = v.astype(BF16)
    lo = (v - hi.astype(F32)).astype(BF16)
    return jnp.dot(hi, p01, preferred_element_type=F32) + jnp.dot(lo, p01, preferred_element_type=F32)


def _sigmoid(x):
    return 1.0 / (1.0 + jnp.exp(-x))


def _softplus(x):
    e = jnp.exp(-jnp.abs(x))
    u = 1.0 + e
    log1p = jnp.where(u == 1.0, e, jnp.log(u) * (e / (u - 1.0)))
    return jnp.maximum(x, 0.0) + log1p


def _rows8(rows):
    n = rows[0].shape[1]
    rid = lax.broadcasted_iota(jnp.int32, (8, n), 0)
    out = jnp.zeros((8, n), F32)
    for k, r in enumerate(rows):
        out = out + jnp.where(rid == k, r, 0.0)
    return out


def _colsum(a):
    return jnp.sum(a, axis=0, keepdims=True)


def _in_proj(x, w_ssd, w_att, *, tm, comm=None):
    L, K = x.shape

    def body(x_ref, ws_ref, wa_ref, ps_ref, pa_ref, xb_ref):
        xb = x_ref[...].astype(_MXU)
        xb_ref[...] = xb
        ps_ref[...] = jnp.dot(xb, ws_ref[...], preferred_element_type=F32)
        pa_ref[...] = jnp.dot(xb, wa_ref[...], preferred_element_type=F32)

    row = lambda w: pl.BlockSpec((tm, w), lambda i: (i, 0))
    resident = lambda a: pl.BlockSpec(a.shape, lambda i: (0, 0), pipeline_mode=pl.Buffered(1))
    return _call(
        body, comm, name="in_proj", grid=(L // tm,),
        in_specs=[row(K), resident(w_ssd), resident(w_att)], out_specs=[row(S_W), row(A_W), row(K)],
        out_shape=[jax.ShapeDtypeStruct((L, S_W), F32), jax.ShapeDtypeStruct((L, A_W), F32),
                   jax.ShapeDtypeStruct((L, K), _MXU)],
        scratch_shapes=[], args=(x, w_ssd, w_att))


def _matmul_tn(a, g, *, tl, tn, name, emit_bf16=False):
    L, M = a.shape
    N = g.shape[1]
    last = L // tl - 1

    def body(a_ref, g_ref, o_ref, *rest):
        @pl.when(pl.program_id(1) == 0)
        def _():
            o_ref[...] = jnp.zeros_like(o_ref)

        o_ref[...] += _mm_tn(a_ref[...], g_ref[...])
        if emit_bf16:
            @pl.when(pl.program_id(1) == last)
            def _():
                rest[0][...] = o_ref[...].astype(BF16)

    spec = pl.BlockSpec((M, tn), lambda j, l: (0, j))
    res = pl.pallas_call(
        body, name=name, grid=(N // tn, L // tl),
        in_specs=[pl.BlockSpec((tl, M), lambda j, l: (l, 0)), pl.BlockSpec((tl, tn), lambda j, l: (l, j))],
        out_specs=[spec, spec] if emit_bf16 else [spec],
        out_shape=[jax.ShapeDtypeStruct((M, N), F32)] + ([jax.ShapeDtypeStruct((M, N), BF16)] if emit_bf16 else []),
        compiler_params=_params(("arbitrary", "arbitrary")),
    )(a, g)
    return res if emit_bf16 else res[0]


def _position():
    return lax.axis_index("x"), lax.axis_index("y"), lax.axis_index("c")


def _index(px, py, pc):
    return 4 * px + 2 * py + pc


def _flip(pos, k):
    x, y, c = pos
    return ((1 - x) if (k >> 2) & 1 else x, (1 - y) if (k >> 1) & 1 else y, (1 - c) if k & 1 else c)


def _remote(src, dst, send_sem, recv_sem, peer):
    return pltpu.make_async_remote_copy(src_ref=src, dst_ref=dst, send_sem=send_sem, recv_sem=recv_sem,
                                        device_id=peer, device_id_type=MESH)


class _Flow:
    def __init__(self, kind, operand, result):
        self.kind, self.operand, self.result = kind, operand, result


class _Hosted:
    def __init__(self, operands, out_shapes, flows):
        self.operands, self.out_shapes, self.flows = operands, out_shapes, flows

    def plan(self, ins, outs, send_sems, recv_sems, local_sems):
        me = _position()
        mi = _index(*me)
        sends, recvs, locals_ = [], [], []
        for row, f in enumerate(self.flows):
            src, dst = ins[f.operand], outs[f.result]
            for k in range(1, N_DEV):
                peer = _flip(me, k)
                sems = (send_sems.at[row, k - 1], recv_sems.at[row, k - 1])
                if f.kind == "exchange":
                    sends.append(_remote(src.at[_index(*peer)], dst.at[k - 1], *sems, peer))
                    recvs.append(sends[-1])
                else:
                    sends.append(_remote(src, dst.at[mi], *sems, peer))
                    recvs.append(_remote(src, dst.at[_index(*peer)], *sems, peer))
            if f.kind == "gather":
                locals_.append(pltpu.make_async_copy(src, dst.at[mi], local_sems.at[row]))

        def start():
            for cp in locals_ + sends:
                cp.start()

        def wait():
            for cp in recvs:
                cp.wait_recv()
            for cp in sends:
                cp.wait_send()
            for cp in locals_:
                cp.wait()

        return start, wait


class _OwnerReduce:
    FIRST_STEP, SECOND_STEP, SEND_STEPS, REDUCE_STEPS = 2, 4, (2, 4, 6, 8), (5, 8, 10, 13)

    def __init__(self, stack, target_x):
        self.stack, self.target_x = stack, target_x
        block = stack.shape[1:]
        self.chunks = len(self.SEND_STEPS)
        self.chunk_rows = block[0] // self.chunks
        self.out_shapes = [jax.ShapeDtypeStruct(block, F32), jax.ShapeDtypeStruct((2,) + block, BF16)]
        self.out_specs = [pl.BlockSpec(block, lambda *_: (0, 0), pipeline_mode=pl.Buffered(1)),
                          pl.BlockSpec(memory_space=pl.ANY)]
        dma = pltpu.SemaphoreType.DMA
        self.scratch_shapes = ([pltpu.VMEM((2,) + block, F32)] * 2 + [pltpu.VMEM(block, BF16)] * 3
                               + [dma((self.chunks,))] * 4 + [dma((2,))] * 3)

    def plan(self, i, steps, stack_ref, own_ref, recv_ref, scratch):
        assert self.FIRST_STEP <= self.SEND_STEPS[0] and self.SECOND_STEP < self.REDUCE_STEPS[0] < steps - 1
        (theirs_scr, mine_scr, first_scr, across_scr, out_scr, y_send_sems, y_recv_sems, x_send_sems, x_recv_sems,
         swap_send_sems, swap_recv_sems, mine_sems) = scratch
        x, y, c = _position()
        owners_side = x == self.target_x
        other_side = x != self.target_x
        sibling, across, owner = (x, y, 1 - c), (x, 1 - y, c), (self.target_x, y, c)
        order = (1 - y, y)
        swaps = [_remote(stack_ref.at[2 * order[j] + (1 - c)], theirs_scr.at[j], swap_send_sems.at[j], swap_recv_sems.at[j],
                         sibling) for j in range(2)]
        mine = [pltpu.make_async_copy(stack_ref.at[2 * order[j] + c], mine_scr.at[j], mine_sems.at[j]) for j in range(2)]
        chunks = range(self.chunks)
        part = [pl.ds(j * self.chunk_rows, self.chunk_rows) for j in chunks]
        y_sems = lambda j: (y_send_sems.at[j], y_recv_sems.at[j])
        to_neighbour = [_remote(first_scr.at[part[j]], across_scr.at[part[j]], *y_sems(j), across) for j in chunks]
        to_owner_y = [_remote(first_scr.at[part[j]], recv_ref.at[0, part[j]], *y_sems(j), across) for j in chunks]
        to_owner_x = [_remote(out_scr.at[part[j]], recv_ref.at[1, part[j]], x_send_sems.at[j], x_recv_sems.at[j], owner)
                      for j in chunks]

        def before():
            @pl.when(i == 0)
            def _():
                for cp in [swaps[0]] + mine:
                    cp.start()

            pl.when(i == 1)(swaps[1].start)

        def after():
            @pl.when(i == self.FIRST_STEP)
            def _():
                swaps[0].wait_recv()
                mine[0].wait()
                first_scr[...] = (mine_scr[0] + theirs_scr[0]).astype(first_scr.dtype)

            for j in chunks:
                pl.when((i == self.SEND_STEPS[j]) & other_side)(to_neighbour[j].start)
                pl.when((i == self.SEND_STEPS[j]) & owners_side)(to_owner_y[j].start)

            @pl.when(i == self.SECOND_STEP)
            def _():
                swaps[1].wait_recv()
                mine[1].wait()
                t = mine_scr[1] + theirs_scr[1]
                own_ref[...] = t
                mine_scr[1] = t

            for j in chunks:
                @pl.when((i == self.REDUCE_STEPS[j]) & other_side)
                def _(j=j):
                    to_neighbour[j].wait_recv()
                    t = mine_scr[1, part[j], :] + across_scr[part[j], :].astype(F32)
                    out_scr[part[j], :] = t.astype(out_scr.dtype)
                    to_owner_x[j].start()

            @pl.when(i == steps - 1)
            def _():
                for cp in swaps:
                    cp.wait_send()
                for j in chunks:
                    @pl.when(other_side)
                    def _(j=j):
                        to_neighbour[j].wait_send()
                        to_owner_x[j].wait_send()

                    @pl.when(owners_side)
                    def _(j=j):
                        to_owner_y[j].wait_send()
                        to_owner_y[j].wait_recv()
                        to_owner_x[j].wait_recv()

        return before, after


def _call(body, comm, *, name, grid, in_specs, out_specs, out_shape, scratch_shapes, args, reduce=None):
    semantics = ("arbitrary",) * len(grid)
    if comm is None and reduce is None:
        return pl.pallas_call(body, name=name, grid=grid, in_specs=in_specs, out_specs=out_specs, out_shape=out_shape,
                              scratch_shapes=scratch_shapes, compiler_params=_params(semantics))(*args)
    n_in, n_out, n_scr = len(args), len(out_shape), len(scratch_shapes)
    c_operands, c_shapes, flows = (comm.operands, comm.out_shapes, comm.flows) if comm else ([], [], [])
    c_in, c_out, rows = len(c_operands), len(c_shapes), max(len(flows), 1)
    r_in = 0 if reduce is None else 1

    def hosted(*refs):
        ins, refs = refs[:n_in], refs[n_in:]
        cins, refs = refs[:c_in], refs[c_in:]
        rins, refs = refs[:r_in], refs[r_in:]
        outs, refs = refs[:n_out], refs[n_out:]
        couts, refs = refs[:c_out], refs[c_out:]
        routs, refs = refs[:2 * r_in], refs[2 * r_in:]
        scr, refs = refs[:n_scr], refs[n_scr:]
        (send_sems, recv_sems, local_sems), r_scr = refs[:3], refs[3:]
        ids = [pl.program_id(d) for d in range(len(grid))]
        first, last = ids[0] == 0, ids[0] == grid[0] - 1
        for d in range(1, len(grid)):
            first, last = first & (ids[d] == 0), last & (ids[d] == grid[d] - 1)
        before = after = lambda: None
        if reduce is not None:
            before, after = reduce.plan(ids[0], grid[0], rins[0], *routs, r_scr)
        if comm is not None:
            start, wait = comm.plan(cins, couts, send_sems, recv_sems, local_sems)
            pl.when(first)(start)
        before()
        body(*ins, *outs, *scr)
        after()
        if comm is not None:
            pl.when(last)(wait)

    any_spec = pl.BlockSpec(memory_space=pl.ANY)
    sems = [pltpu.SemaphoreType.DMA((rows, N_DEV - 1)), pltpu.SemaphoreType.DMA((rows, N_DEV - 1)),
            pltpu.SemaphoreType.DMA((rows,))]
    r_operands, r_specs, r_shapes, r_scratch = ([reduce.stack], reduce.out_specs, reduce.out_shapes,
                                                reduce.scratch_shapes) if reduce else ([], [], [], [])
    return pl.pallas_call(
        hosted, name=name, grid=grid, in_specs=list(in_specs) + [any_spec] * (c_in + r_in),
        out_specs=list(out_specs) + [any_spec] * c_out + r_specs, out_shape=list(out_shape) + list(c_shapes) + r_shapes,
        scratch_shapes=list(scratch_shapes) + sems + r_scratch,
        compiler_params=_params(semantics))(*args, *c_operands, *r_operands)


def _head_row(ref, width, rep):
    hid = lax.broadcasted_iota(jnp.int32, (1, width), 1) // rep
    row = jnp.zeros((1, width), F32)
    for h in range(N_HEADS):
        row = jnp.where(hid == h, ref[h], row)
    return row


def _rows_from_above(u_b, s, ext_scr, row, col):
    down = (row - col == s).astype(_MXU)
    return jnp.concatenate([ext_scr[8 - s:16 - s, :], jnp.dot(down, u_b, preferred_element_type=F32)[8:128]], axis=0)


def _ssd_recompute(first, p_ref, halo_ref, cw_ref, cb_ref, dtb_ref, alog_ref, e_ref, ext_scr, pre=None):
    row = lax.broadcasted_iota(jnp.int32, (128, 128), 0)
    col = lax.broadcasted_iota(jnp.int32, (128, 128), 1)
    ext_scr[0:8, :] = jnp.where(first, 0.0, halo_ref[:, S_XS:S_DT])
    if pre is not None:
        ext_scr[8:16, :] = p_ref[0:8, S_XS:S_DT]
    else:
        ext_scr[8:136, :] = p_ref[:, S_XS:S_DT]
        cw = cw_ref[...]
        pre = (cb_ref[0:1, :] + cw[3:4, :] * ext_scr[8:136, :] + cw[2:3, :] * ext_scr[7:135, :]
               + cw[1:2, :] * ext_scr[6:134, :] + cw[0:1, :] * ext_scr[5:133, :])
    sg = _sigmoid(pre)
    act = pre * sg
    lane = lax.broadcasted_iota(jnp.int32, (1, 128), 1)
    A = jnp.where(lane < N_HEADS, -jnp.exp(_head_row(alog_ref, 128, 1)), 0.0)
    raw = p_ref[:, S_DT:S_DT + 128] + _head_row(dtb_ref, 128, 1)
    dt = _softplus(raw)
    dA = dt * A
    tril = (row >= col).astype(BF16)
    acs = _mm_exact_l(tril, dA)
    last = acs[127:128, :]
    ds = jnp.exp(last - acs)
    eo = jnp.exp(acs)
    E = e_ref[...]
    ex = _mm_2pass_r(jnp.concatenate([dt, ds, eo], axis=0), E)
    dt_e, ds_e, eo_e = ex[0:128], ex[128:256], ex[256:384]
    xs_c = act[:, 0:1024]
    X = xs_c * dt_e
    return dict(pre=pre, sg=sg, xs_c=xs_c, Bc=act[:, 1024:1280], Cc=act[:, 1280:1536], A=A, raw=raw, dt=dt,
                acs=acs, acsT=acs.T, eo_e=eo_e, ds_e=ds_e, dt_e=dt_e, cd_e=eo_e[127:128, :],
                X=X, Xd=X * ds_e, row=row, col=col)


def _split_halves(t):
    lo = _lo_half(CHUNK)
    return jnp.concatenate([jnp.where(lo, t, 0.0), jnp.where(lo, 0.0, t)], axis=0)


def _ssd_core(R, hprev):
    causal = R["row"] >= R["col"]
    acs, acsT, X = R["acs"], R["acsT"], R["X"]
    ydiag, yoff, snew = [], [], []
    for g in range(SSD_GROUPS):
        Bg = R["Bc"][:, g * 128:(g + 1) * 128]
        Cg = R["Cc"][:, g * 128:(g + 1) * 128]
        cols = slice(g * 512, (g + 1) * 512)
        CB = _mm_nt(Cg, Bg)
        snew.append(_mm_tn(Bg, R["Xd"][:, cols]))
        yoff.append(_mm(Cg, hprev[:, cols]))
        for j in range(4):
            h0 = g * 8 + 2 * j
            ms = [CB * jnp.exp(jnp.where(causal, acs[:, h:h + 1] - acsT[h:h + 1, :], NEG)) for h in (h0, h0 + 1)]
            ydiag.append(_mm(jnp.concatenate(ms, axis=1), _split_halves(X[:, h0 * HEAD_DIM:h0 * HEAD_DIM + 128])))
    Y = jnp.concatenate(ydiag, axis=1) + jnp.concatenate(yoff, axis=1) * R["eo_e"]
    return Y, jnp.concatenate(snew, axis=1)


def _ssd_forward_step(p_ref, halo_ref, cw_ref, cb_ref, dtb_ref, alog_ref, dsk_ref, nw_ref, e_ref,
                      y_ref, ypre_ref, hprev_ref, pre_ref, h_scr, ext_scr):
    c = pl.program_id(0)
    first = c == 0

    @pl.when(first)
    def _():
        h_scr[...] = jnp.zeros_like(h_scr)

    R = _ssd_recompute(first, p_ref, halo_ref, cw_ref, cb_ref, dtb_ref, alog_ref, e_ref, ext_scr)
    hprev = h_scr[...]
    hprev_ref[...] = hprev
    pre_ref[...] = R["pre"]
    Y, snew = _ssd_core(R, hprev)
    h_scr[...] = hprev * R["cd_e"] + snew
    Y = Y + _head_row(dsk_ref, D_SSD, HEAD_DIM) * R["xs_c"]
    ypre_ref[...] = Y
    z = p_ref[:, S_Z:S_Z + 1024]
    yf = Y * (z * _sigmoid(z))
    outs = []
    for g in range(SSD_GROUPS):
        yg = yf[:, g * 512:(g + 1) * 512]
        r = lax.rsqrt(jnp.mean(yg * yg, axis=-1, keepdims=True) + RMS_EPS)
        outs.append(yg * r)
    y_ref[:, 0:D_SSD] = (jnp.concatenate(outs, axis=1) * nw_ref[0:1, :]).astype(y_ref.dtype)


def _ssd_backward(proj_ssd, hprev_all, ypre, pre, dy, conv_w, conv_b, dt_bias, a_log, d_skip, norm_w, E, ET, comm=None):
    L = proj_ssd.shape[0]
    nc = L // CHUNK

    def body(p_ref, halo_ref, hprev_ref, ypre_ref, pre_ref, dy_ref, cw_ref, cb_ref, dtb_ref, alog_ref, dsk_ref, nw_ref, e_ref,
             et_ref, dp_ref, acc_cw_ref, acc_w_ref, acc_s_ref, dh_scr, ext_scr, ext2_scr, nxt_scr):
        i = pl.program_id(0)
        c = nc - 1 - i
        first = c == 0

        @pl.when(i == 0)
        def _():
            dh_scr[...] = jnp.zeros_like(dh_scr)
            nxt_scr[...] = jnp.zeros_like(nxt_scr)
            acc_cw_ref[...] = jnp.zeros_like(acc_cw_ref)
            acc_w_ref[...] = jnp.zeros_like(acc_w_ref)
            acc_s_ref[...] = jnp.zeros_like(acc_s_ref)

        R = _ssd_recompute(first, p_ref, halo_ref, cw_ref, cb_ref, dtb_ref, alog_ref, e_ref, ext_scr, pre_ref[...])
        hprev = hprev_ref[...]
        xs_c, X, Xd = R["xs_c"], R["X"], R["Xd"]
        acs, acsT = R["acs"], R["acsT"]
        ET = et_ref[...]
        dsk = _head_row(dsk_ref, D_SSD, HEAD_DIM)
        Y = ypre_ref[...]

        z = p_ref[:, S_Z:S_Z + 1024]
        sz = _sigmoid(z)
        silz = z * sz
        yf = Y * silz
        dyv = dy_ref[...]
        nw = nw_ref[0:1, :]
        dyf_parts, dnw_parts = [], []
        for g in range(SSD_GROUPS):
            cols = slice(g * 512, (g + 1) * 512)
            yg = yf[:, cols]
            r = lax.rsqrt(jnp.mean(yg * yg, axis=-1, keepdims=True) + RMS_EPS)
            yn = yg * r
            dyn = dyv[:, cols] * nw[:, cols]
            dnw_parts.append(_colsum(dyv[:, cols] * yn))
            dyf_parts.append(r * (dyn - yn * jnp.mean(dyn * yn, axis=-1, keepdims=True)))
        dyf = jnp.concatenate(dyf_parts, axis=1)
        dY = dyf * silz
        dz = dyf * Y * (sz * (1.0 + z * (1.0 - sz)))

        dhn = dh_scr[...]
        dYo = dY * R["eo_e"]
        causal = R["row"] >= R["col"]
        dacs = jnp.zeros((128, 128), F32)
        dacs_t = jnp.zeros((128, 128), F32)
        dxdiag, dxd, dhprev, dBs, dCs, yoff = [], [], [], [], [], []
        for g in range(SSD_GROUPS):
            Bg = R["Bc"][:, g * 128:(g + 1) * 128]
            Cg = R["Cc"][:, g * 128:(g + 1) * 128]
            cols = slice(g * 512, (g + 1) * 512)
            CB = _mm_nt(Cg, Bg)
            dCB = jnp.zeros((128, 128), F32)
            for j in range(4):
                h0 = g * 8 + 2 * j
                pc = slice(h0 * HEAD_DIM, h0 * HEAD_DIM + 128)
                dYst = _split_halves(dY[:, pc])
                dMst = _mm_nt(dYst, X[:, pc])
                mts = []
                for a, h in enumerate((h0, h0 + 1)):
                    acol = acs[:, h:h + 1]
                    arow = acsT[h:h + 1, :]
                    Lm = jnp.exp(jnp.where(causal, acol - arow, NEG))
                    M = CB * Lm
                    dM = dMst[a * 128:(a + 1) * 128]
                    dCB = dCB + dM * Lm
                    G = dM * M
                    dacs = dacs + jnp.where(R["col"] == h, jnp.sum(G, axis=1, keepdims=True), 0.0)
                    dacs_t = dacs_t + jnp.where(R["row"] == h, jnp.sum(G, axis=0, keepdims=True), 0.0)
                    mts.append(M.T)
                dxdiag.append(_mm(jnp.concatenate(mts, axis=1), dYst))
            dS = dhn[:, cols]
            dxd.append(_mm(Bg, dS))
            yoff.append(_mm(Cg, hprev[:, cols]))
            dhprev.append(_mm_tn(Cg, dYo[:, cols]))
            dCs.append(_mm_nt(dYo[:, cols], hprev[:, cols]) + _mm(dCB, Bg))
            dBs.append(_mm_tn(dCB, Cg) + _mm_nt(Xd[:, cols], dS))
        Yoff = jnp.concatenate(yoff, axis=1) * R["eo_e"]
        dXd = jnp.concatenate(dxd, axis=1)
        dX = jnp.concatenate(dxdiag, axis=1) + dXd * R["ds_e"]
        t_state = dXd * Xd
        dacs = dacs + _mm_2pass_r(dY * Yoff - t_state, ET) - dacs_t.T
        v_last = _colsum(t_state + dhn * hprev * R["cd_e"])
        dlast = _mm_exact_r(jnp.broadcast_to(v_last, (8, 1024)), ET)[0:1, :]
        dacs = dacs + jnp.where(R["row"] == 127, dlast, 0.0)
        triu = (R["col"] >= R["row"]).astype(BF16)
        da = _mm_exact_l(triu, dacs)
        ddt = da * R["A"] + _mm(dX * xs_c, ET)
        ddt_raw = ddt * _sigmoid(R["raw"])
        dxs_c = dX * R["dt_e"] + dY * dsk
        dh_scr[...] = jnp.concatenate(dhprev, axis=1) + dhn * R["cd_e"]

        dact = jnp.concatenate([dxs_c] + dBs + dCs, axis=1)
        pre, sg = R["pre"], R["sg"]
        dpre = dact * (sg * (1.0 + pre * (1.0 - sg)))
        ext2_scr[0:8, :] = dpre[120:128, :]
        ext2_scr[8:16, :] = nxt_scr[...]
        nxt_scr[...] = dpre[0:8, :]
        cw = cw_ref[...]
        u_b, dpre_b = p_ref[:, S_XS:S_DT].astype(_MXU), dpre.astype(_MXU)
        dxbc = cw[3:4, :] * dpre
        taps = [_colsum(dpre * p_ref[:, S_XS:S_DT])]
        for s in (1, 2, 3):
            up = (R["col"] - R["row"] == s).astype(_MXU)
            d_s = jnp.concatenate([jnp.dot(up, dpre_b, preferred_element_type=F32)[0:120],
                                   ext2_scr[s:8 + s, :]], axis=0)
            dxbc = dxbc + cw[3 - s:4 - s, :] * d_s
            taps.append(_colsum(dpre * _rows_from_above(u_b, s, ext_scr, R["row"], R["col"])))
        acc_cw_ref[...] += _rows8(taps[::-1] + [_colsum(dpre)])
        acc_w_ref[...] += _rows8([jnp.concatenate(dnw_parts, axis=1), _colsum(dY * xs_c)])
        acc_s_ref[...] += _rows8([_colsum(ddt_raw), _colsum(da * R["dt"])])

        lane = lax.broadcasted_iota(jnp.int32, (128, 128), 1)
        dp_ref[:, S_Z:S_Z + 1024] = dz.astype(dp_ref.dtype)
        dp_ref[:, S_XS:S_DT] = dxbc.astype(dp_ref.dtype)
        dp_ref[:, S_DT:S_DT + 128] = jnp.where(lane < N_HEADS, ddt_raw, 0.0).astype(dp_ref.dtype)
        dp_ref[:, S_DT + 128:S_W] = jnp.zeros((128, 128), dp_ref.dtype)

        @pl.when(i == nc - 1)
        def _():
            acc = acc_s_ref[...]
            dskip = _mm_exact_r(acc_w_ref[...], ET)[1:2, :]
            acc_s_ref[...] = _rows8([acc[0:1, :], acc[1:2, :] * R["A"], dskip])

    const = lambda shape: pl.BlockSpec(shape, lambda i: (0, 0))
    smem = pl.BlockSpec(memory_space=pltpu.SMEM)
    rev = lambda i: (nc - 1 - i, 0)
    return _call(
        body, comm, name="ssd_bwd", grid=(nc,),
        in_specs=[pl.BlockSpec((CHUNK, S_W), rev),
                  pl.BlockSpec((8, S_W), lambda i: (jnp.maximum((nc - 1 - i) * 16 - 1, 0), 0)),
                  pl.BlockSpec((128, 1024), rev),
                  pl.BlockSpec((CHUNK, D_SSD), rev),
                  pl.BlockSpec((CHUNK, D_XBC), rev),
                  pl.BlockSpec((CHUNK, D_SSD), rev),
                  const((4, D_XBC)), const((1, D_XBC)), smem, smem, smem, const((1, 1024)),
                  const((128, 1024)), const((1024, 128))],
        out_specs=[pl.BlockSpec((CHUNK, S_W), rev), const((8, D_XBC)), const((8, 1024)), const((8, 128))],
        out_shape=[jax.ShapeDtypeStruct((L, S_W), _MXU), jax.ShapeDtypeStruct((8, D_XBC), F32),
                   jax.ShapeDtypeStruct((8, 1024), F32), jax.ShapeDtypeStruct((8, 128), F32)],
        scratch_shapes=[pltpu.VMEM((128, 1024), F32), pltpu.VMEM((16, D_XBC), F32),
                        pltpu.VMEM((16, D_XBC), F32), pltpu.VMEM((8, D_XBC), F32)],
        args=(proj_ssd, proj_ssd, hprev_all, ypre, pre, dy, conv_w, conv_b, dt_bias, a_log, d_skip, norm_w, E, ET))


def _rope(t, tab):
    cos, sa, sb = tab[:, 0:128], tab[:, 128:256], tab[:, 256:384]
    outs = []
    for i in range(t.shape[1] // 128):
        tg = t[:, i * 128:(i + 1) * 128]
        outs.append(tg * cos + pltpu.roll(tg, 8, 1) * sa + pltpu.roll(tg, 120, 1) * sb)
    return jnp.concatenate(outs, axis=1)


def _rope_transposed(d, tab):
    cos, sa, sb = tab[:, 0:128], tab[:, 128:256], tab[:, 256:384]
    outs = []
    for i in range(d.shape[1] // 128):
        dg = d[:, i * 128:(i + 1) * 128]
        outs.append(dg * cos + pltpu.roll(dg * sa, 120, 1) + pltpu.roll(dg * sb, 8, 1))
    return jnp.concatenate(outs, axis=1)


def _lo_half(rows):
    return lax.broadcasted_iota(jnp.int32, (rows, 128), 1) < HEAD_DIM


def _native_half(rows, j):
    lo = _lo_half(rows)
    return lo if j % 2 == 0 else jnp.logical_not(lo)


def _kv_native(t, j):
    p = j // 2
    return jnp.where(_native_half(t.shape[0], j), t[:, p * 128:(p + 1) * 128], 0.0)


def _stack_heads(t, j):
    out = []
    for m in (2 * j, 2 * j + 1):
        pair = t[:, m * 128:(m + 1) * 128]
        swapped = pltpu.roll(pair, HEAD_DIM, 1)
        out += [pair, swapped] if j % 2 == 0 else [swapped, pair]
    return jnp.concatenate(out, axis=0)


def _unstack_heads(s, j):
    out = []
    for m in range(2):
        first, second = s[256 * m:256 * m + 128], s[256 * m + 128:256 * m + 256]
        if j % 2 == 0:
            out.append(first + pltpu.roll(second, HEAD_DIM, 1))
        else:
            out.append(pltpu.roll(first, HEAD_DIM, 1) + second)
    return jnp.concatenate(out, axis=1)


def _keep_native(r, j):
    return jnp.where(_native_half(r.shape[0], j), r, 0.0)


def _sink_row(sink_ref, j):
    hid = lax.broadcasted_iota(jnp.int32, (1, 4 * CHUNK), 1) // CHUNK
    row = jnp.zeros((1, 4 * CHUNK), F32)
    for hh in range(4):
        row = jnp.where(hid == hh, sink_ref[4 * j + hh], row)
    return row


def _from_current():
    si = lax.broadcasted_iota(jnp.int32, (CHUNK, 4 * CHUNK), 0)
    qi = lax.broadcasted_iota(jnp.int32, (CHUNK, 4 * CHUNK), 1) % CHUNK
    return si <= qi


def _fold(full, from_cur, pen=0.0):
    return jnp.where(from_cur, full[CHUNK:2 * CHUNK], full[0:CHUNK] + pen)


def _unfold(t, from_cur):
    c = jnp.where(from_cur, t, 0.0)
    return jnp.concatenate([t - c, c], axis=0)


def _softmax_sink(s, sink):
    mx = jnp.maximum(jnp.max(s, axis=0, keepdims=True), sink)
    p = jnp.exp(s - mx)
    esink = jnp.exp(sink - mx)
    inv = 1.0 / (jnp.sum(p, axis=0, keepdims=True) + esink)
    return p * inv, esink * inv


def _swa_inputs(blk, p_ref, prev_ref, tab_ref, ptab_ref):
    tab = tab_ref[...]
    qr = _rope(p_ref[:, A_Q:A_Q + 1024], tab) * ATT_SCALE
    kk = jnp.concatenate([_rope(prev_ref[:, 0:256], ptab_ref[...]), _rope(p_ref[:, A_K:A_K + 256], tab)], axis=0)
    vv = jnp.concatenate([prev_ref[:, 256:512], p_ref[:, A_V:A_V + 256]], axis=0)
    return tab, qr, kk, vv, jnp.where(blk > 0, 0.0, NEG)


def _swa_forward_step(sink_ref, p_ref, prev_ref, tab_ref, ptab_ref, y_ref):
    n = pl.program_id(0)
    _, qr, kk, vv, pen = _swa_inputs(n, p_ref, prev_ref, tab_ref, ptab_ref)
    from_cur = _from_current()
    outs = []
    for j in range(KV_HEADS):
        s = _fold(_mm_nt(_kv_native(kk, j), _stack_heads(qr, j)), from_cur, pen)
        P, _ = _softmax_sink(s, _sink_row(sink_ref, j))
        outs.append(_unstack_heads(_mm_tn(_unfold(P, from_cur), _kv_native(vv, j)), j))
    g = p_ref[:, A_G:A_G + 1024]
    y_ref[:, D_SSD:D_SSD + D_ATT] = (jnp.concatenate(outs, axis=1) * (g * _sigmoid(g))).astype(y_ref.dtype)


def _mixer_forward(proj_ssd, proj_att, tabs, sinks, conv_w, conv_b, dt_bias, a_log, d_skip, norm_w, E, comm=None):
    L = proj_ssd.shape[0]
    nc = L // CHUNK

    def body(p_ref, halo_ref, cw_ref, cb_ref, dtb_ref, alog_ref, dsk_ref, nw_ref, e_ref,
             sink_ref, pa_ref, prev_ref, tab_ref, ptab_ref, y_ref, ypre_ref, hprev_ref, pre_ref, h_scr, ext_scr):
        _ssd_forward_step(p_ref, halo_ref, cw_ref, cb_ref, dtb_ref, alog_ref, dsk_ref, nw_ref, e_ref,
                          y_ref, ypre_ref, hprev_ref, pre_ref, h_scr, ext_scr)
        _swa_forward_step(sink_ref, pa_ref, prev_ref, tab_ref, ptab_ref, y_ref)

    const = lambda shape: pl.BlockSpec(shape, lambda c: (0, 0))
    smem = pl.BlockSpec(memory_space=pltpu.SMEM)
    rows = lambda w: pl.BlockSpec((CHUNK, w), lambda c: (c, 0))
    return _call(
        body, comm, name="mixer_fwd", grid=(nc,),
        in_specs=[rows(S_W), pl.BlockSpec((8, S_W), lambda c: (jnp.maximum(c * 16 - 1, 0), 0)),
                  const((4, D_XBC)), const((1, D_XBC)), smem, smem, smem, const((1, 1024)), const((128, 1024)),
                  smem, rows(A_W), pl.BlockSpec((CHUNK, 512), lambda c: (jnp.maximum(c - 1, 0), 2)),
                  rows(384), pl.BlockSpec((CHUNK, 384), lambda c: (jnp.maximum(c - 1, 0), 0))],
        out_specs=[rows(D_SSD + D_ATT), rows(D_SSD), pl.BlockSpec((128, 1024), lambda c: (c, 0)), rows(D_XBC)],
        out_shape=[jax.ShapeDtypeStruct((L, D_SSD + D_ATT), _MXU), jax.ShapeDtypeStruct((L, D_SSD), F32),
                   jax.ShapeDtypeStruct((nc * 128, 1024), F32), jax.ShapeDtypeStruct((L, D_XBC), F32)],
        scratch_shapes=[pltpu.VMEM((128, 1024), F32), pltpu.VMEM((136, D_XBC), F32)],
        args=(proj_ssd, proj_ssd, conv_w, conv_b, dt_bias, a_log, d_skip, norm_w, E,
              sinks, proj_att, proj_att, tabs, tabs))


def _swa_backward(proj_att, tabs, sinks, dy, reduce=None):
    L = proj_att.shape[0]
    nb = L // CHUNK

    def body(sink_ref, p_ref, prev_ref, tab_ref, ptab_ref, dy_ref, dp_ref, dsink_ref, carry_k, carry_v):
        i = pl.program_id(0)
        n = nb - 1 - i

        @pl.when(i == 0)
        def _():
            carry_k[...] = jnp.zeros_like(carry_k)
            carry_v[...] = jnp.zeros_like(carry_v)
            dsink_ref[...] = jnp.zeros_like(dsink_ref)

        tab, qr, kk, vv, pen = _swa_inputs(n, p_ref, prev_ref, tab_ref, ptab_ref)
        from_cur = _from_current()
        g = p_ref[:, A_G:A_G + 1024]
        sgm = _sigmoid(g)
        dyv = dy_ref[...]
        do_all = dyv * (g * sgm)
        lane8 = lax.broadcasted_iota(jnp.int32, (8, 128), 1)
        hid = lax.broadcasted_iota(jnp.int32, (1, 4 * CHUNK), 1) // CHUNK
        o_parts, dq_parts = [], []
        dk_nat = [jnp.zeros((2 * CHUNK, 128), F32) for _ in range(2)]
        dv_nat = [jnp.zeros((2 * CHUNK, 128), F32) for _ in range(2)]
        dsink = jnp.zeros((8, 128), F32)
        for j in range(KV_HEADS):
            qs = _stack_heads(qr, j)
            kkb, vvb = _kv_native(kk, j), _kv_native(vv, j)
            P, psink = _softmax_sink(_fold(_mm_nt(kkb, qs), from_cur, pen), _sink_row(sink_ref, j))
            p_full = _unfold(P, from_cur)
            o_parts.append(_unstack_heads(_mm_tn(p_full, vvb), j))
            do_s = _stack_heads(do_all, j)
            dP = _fold(_mm_nt(vvb, do_s), from_cur)
            D = jnp.sum(P * dP, axis=0, keepdims=True)
            ds_full = _unfold(P * (dP - D), from_cur)
            sd = psink * D
            for hh in range(4):
                dsink = dsink + jnp.where(lane8 == 4 * j + hh, -jnp.sum(jnp.where(hid == hh, sd, 0.0)), 0.0)
            dq_parts.append(_unstack_heads(_mm_tn(ds_full, kkb), j) * ATT_SCALE)
            dk_nat[j // 2] = dk_nat[j // 2] + _keep_native(_mm(ds_full, qs), j)
            dv_nat[j // 2] = dv_nat[j // 2] + _keep_native(_mm(p_full, do_s), j)
        o = jnp.concatenate(o_parts, axis=1)
        dkk = jnp.concatenate(dk_nat, axis=1)
        dvv = jnp.concatenate(dv_nat, axis=1)
        out = dp_ref.dtype
        dp_ref[:, A_Q:A_Q + 1024] = _rope_transposed(jnp.concatenate(dq_parts, axis=1), tab).astype(out)
        dp_ref[:, A_K:A_K + 256] = _rope_transposed(dkk[CHUNK:2 * CHUNK] + carry_k[...], tab).astype(out)
        dp_ref[:, A_V:A_V + 256] = (dvv[CHUNK:2 * CHUNK] + carry_v[...]).astype(out)
        dp_ref[:, A_G:A_G + 1024] = (dyv * o * (sgm * (1.0 + g * (1.0 - sgm)))).astype(out)
        carry_k[...] = dkk[0:CHUNK]
        carry_v[...] = dvv[0:CHUNK]
        dsink_ref[...] += dsink

    rev = lambda i: (nb - 1 - i, 0)
    prev = lambda i: jnp.maximum(nb - 2 - i, 0)
    return _call(
        body, None, name="swa_bwd", grid=(nb,),
        in_specs=[pl.BlockSpec(memory_space=pltpu.SMEM),
                  pl.BlockSpec((CHUNK, A_W), rev),
                  pl.BlockSpec((CHUNK, 512), lambda i: (prev(i), 2)),
                  pl.BlockSpec((CHUNK, 384), rev),
                  pl.BlockSpec((CHUNK, 384), lambda i: (prev(i), 0)),
                  pl.BlockSpec((CHUNK, D_ATT), lambda i: (nb - 1 - i, 1))],
        out_specs=[pl.BlockSpec((CHUNK, A_W), rev), pl.BlockSpec((8, 128), lambda i: (0, 0))],
        out_shape=[jax.ShapeDtypeStruct((L, A_W), _MXU), jax.ShapeDtypeStruct((8, 128), F32)],
        scratch_shapes=[pltpu.VMEM((CHUNK, 256), F32), pltpu.VMEM((CHUNK, 256), F32)],
        args=(sinks, proj_att, proj_att, tabs, tabs, dy), reduce=reduce)


def _head(y, x, target, w_out, ln_g, ln_b, *, tm):
    L = x.shape[0]
    nsteps = L // tm

    def body(y_ref, x_ref, t_ref, wo_ref, g_ref, b_ref, dr_ref, dy_ref, acc_ref):
        i = pl.program_id(0)

        @pl.when(i == 0)
        def _():
            acc_ref[...] = jnp.zeros_like(acc_ref)

        r = ALPHA * x_ref[...] + _mm(y_ref[...], wo_ref[...])
        mu = jnp.mean(r, axis=-1, keepdims=True)
        d = r - mu
        rstd = lax.rsqrt(jnp.mean(d * d, axis=-1, keepdims=True) + LN_EPS)
        xh = d * rstd
        gam = g_ref[0:1, :]
        e = xh * gam + b_ref[0:1, :] - t_ref[...]
        dout = e * (1.0 / D_MODEL)
        dxh = dout * gam
        dr = rstd * (dxh - jnp.mean(dxh, axis=-1, keepdims=True)
                     - xh * jnp.mean(dxh * xh, axis=-1, keepdims=True))
        dr_ref[...] = dr
        dy_ref[...] = _mm_nt(dr, wo_ref[...])
        acc_ref[...] += _rows8([_colsum(dout * xh), _colsum(dout), _colsum(e * e) * (0.5 / D_MODEL)])

        @pl.when(i == nsteps - 1)
        def _():
            acc = acc_ref[...]
            tot = jnp.sum(acc[2:3, :])
            rid = lax.broadcasted_iota(jnp.int32, (8, 1024), 0)
            acc_ref[...] = jnp.where(rid == 3, tot, acc)

    const = lambda shape: pl.BlockSpec(shape, lambda i: (0, 0))
    row = lambda w: pl.BlockSpec((tm, w), lambda i: (i, 0))
    return pl.pallas_call(
        body, name="head", grid=(nsteps,),
        in_specs=[row(2048), row(1024), row(1024), const((2048, 1024)), const((1, 1024)), const((1, 1024))],
        out_specs=[row(1024), row(2048), const((8, 1024))],
        out_shape=[jax.ShapeDtypeStruct((L, D_MODEL), F32), jax.ShapeDtypeStruct((L, 2048), F32),
                   jax.ShapeDtypeStruct((8, 1024), F32)],
        compiler_params=_params(("arbitrary",)),
    )(y, x, target, w_out, ln_g, ln_b)


def _gather_w_in(w_shard):
    R = w_shard.shape[0]
    halves = (pl.ds(0, R // 2), pl.ds(R // 2, R // 2))
    any_spec = pl.BlockSpec(memory_space=pl.ANY)

    def body(in_ref, out_ref, send_sems, recv_sems, local_sem):
        x, y, c = _position()

        def slot(p, half=None):
            s = out_ref.at[_index(*p)]
            return s if half is None else s.at[halves[half]]

        def same_core(p):
            return (p[0], p[1], c)

        def other_core(p):
            return (p[0], p[1], 1 - c)

        me, xn, yn, dg = (x, y), (1 - x, y), (x, 1 - y), (1 - x, 1 - y)

        def copy(k, dst, to, src=None):
            return _remote(dst if src is None else src, dst, send_sems.at[k], recv_sems.at[k], to)

        local = pltpu.make_async_copy(in_ref, slot(same_core(me)), local_sem)
        local.start()
        own = [copy(0, slot(same_core(me)), other_core(me), in_ref), copy(1, slot(same_core(me)), same_core(xn), in_ref),
               copy(2, slot(same_core(me)), same_core(yn), in_ref)]
        for cp in own:
            cp.start()
        copy(1, slot(same_core(xn)), same_core(xn)).wait_recv()
        passed = [copy(4, slot(same_core(xn), 1), same_core(yn)), copy(5, slot(same_core(xn)), other_core(me))]
        for cp in passed:
            cp.start()
        copy(2, slot(same_core(yn)), same_core(yn)).wait_recv()
        more = [copy(3, slot(same_core(yn), 0), same_core(xn)), copy(6, slot(same_core(yn)), other_core(me))]
        for cp in more:
            cp.start()
        passed += more
        for k, half in ((3, 0), (4, 1)):
            copy(k, slot(same_core(dg), half), same_core(xn)).wait_recv()
            fwd = copy(7 + half, slot(same_core(dg), half), other_core(me))
            fwd.start()
            passed.append(fwd)
        copy(0, slot(other_core(me)), other_core(me)).wait_recv()
        copy(5, slot(other_core(xn)), other_core(me)).wait_recv()
        copy(6, slot(other_core(yn)), other_core(me)).wait_recv()
        for half in (0, 1):
            copy(7 + half, slot(other_core(dg), half), other_core(me)).wait_recv()
        for cp in own + passed:
            cp.wait_send()
        local.wait()

    return pl.pallas_call(
        body, name="gather_w_in", in_specs=[any_spec], out_specs=any_spec,
        out_shape=jax.ShapeDtypeStruct((N_DEV,) + w_shard.shape, w_shard.dtype),
        scratch_shapes=[pltpu.SemaphoreType.DMA((9,)), pltpu.SemaphoreType.DMA((9,)), pltpu.SemaphoreType.DMA],
    )(w_shard)


def _input_gradient(d_ssd, d_att, w_ssd, w_att, dr, *, tm, comm=None, reduce=None):
    L = dr.shape[0]

    def body(ds_ref, da_ref, ws_ref, wa_ref, dr_ref, o_ref):
        o_ref[...] = ALPHA * dr_ref[...] + _mm_nt(ds_ref[...], ws_ref[...]) + _mm_nt(da_ref[...], wa_ref[...])

    row = lambda w: pl.BlockSpec((tm, w), lambda i: (i, 0))
    resident = lambda a: pl.BlockSpec(a.shape, lambda i: (0, 0), pipeline_mode=pl.Buffered(1))
    return _call(body, comm, name="dx", grid=(L // tm,),
                 in_specs=[row(S_W), row(A_W), resident(w_ssd), resident(w_att), row(D_MODEL)],
                 out_specs=[row(D_MODEL)], out_shape=[jax.ShapeDtypeStruct((L, D_MODEL), F32)],
                 scratch_shapes=[], args=(d_ssd, d_att, w_ssd, w_att, dr), reduce=reduce)


SHARD_COLS = D_IN_PROJ // N_DEV
SPLIT = N_SSD_REAL - 4 * SHARD_COLS
RELAYOUT_ROWS = 256


def _unpack_w_in(w_all):
    def body(g_ref, ws_ref, wa_ref):
        for j in range(4):
            ws_ref[:, SHARD_COLS * j:SHARD_COLS * (j + 1)] = g_ref[j]
        ws_ref[:, 4 * SHARD_COLS:N_SSD_REAL] = g_ref[4, :, 0:SPLIT]
        ws_ref[:, N_SSD_REAL:S_W] = jnp.zeros((RELAYOUT_ROWS, S_W - N_SSD_REAL), ws_ref.dtype)
        wa_ref[:, 0:SHARD_COLS - SPLIT] = g_ref[4, :, SPLIT:SHARD_COLS]
        for j in range(5, N_DEV):
            lo = SHARD_COLS * (j - 4) - SPLIT
            wa_ref[:, lo:lo + SHARD_COLS] = g_ref[j]

    return pl.pallas_call(
        body, name="unpack_w_in", grid=(D_MODEL // RELAYOUT_ROWS,),
        in_specs=[pl.BlockSpec((N_DEV, RELAYOUT_ROWS, SHARD_COLS), lambda i: (0, i, 0))],
        out_specs=[pl.BlockSpec((RELAYOUT_ROWS, S_W), lambda i: (i, 0)), pl.BlockSpec((RELAYOUT_ROWS, A_W), lambda i: (i, 0))],
        out_shape=[jax.ShapeDtypeStruct((D_MODEL, S_W), w_all.dtype), jax.ShapeDtypeStruct((D_MODEL, A_W), w_all.dtype)],
        compiler_params=_params(("arbitrary",)),
    )(w_all)


def _dw_in(xb, d, half, tail=None, *, tl=1024):
    L, N = d.shape
    steps = L // tl

    def body(x_ref, d_ref, *refs):
        if half == 0:
            p_ref, tail_ref, acc = refs
        else:
            t_ref, p_ref, acc = refs
        l = pl.program_id(0)

        @pl.when(l == 0)
        def _():
            acc[...] = jnp.zeros_like(acc)

        acc[...] += _mm_tn(x_ref[...], d_ref[...])

        @pl.when(l == steps - 1)
        def _():
            if half == 0:
                tail_ref[...] = acc[:, S_DT:S_W]
            for j in range(4):
                if half == 0:
                    pieces = [(0, acc[:, SHARD_COLS * j:SHARD_COLS * (j + 1)])]
                elif j == 0:
                    pieces = [(0, t_ref[:, 4 * SHARD_COLS - S_DT:N_SSD_REAL - S_DT]), (SPLIT, acc[:, 0:SHARD_COLS - SPLIT])]
                else:
                    lo = SHARD_COLS * j - SPLIT
                    pieces = [(0, acc[:, lo:lo + SHARD_COLS])]
                for off, blk in pieces:
                    p_ref[j, :, off:off + blk.shape[1]] = blk

    once = pl.Buffered(1)
    whole = lambda shape: pl.BlockSpec(shape, lambda l: (0,) * len(shape), pipeline_mode=once)
    in_specs = [pl.BlockSpec((tl, D_MODEL), lambda l: (l, 0)), pl.BlockSpec((tl, N), lambda l: (l, 0))]
    args = [xb, d]
    out_shape = [jax.ShapeDtypeStruct((4, D_MODEL, SHARD_COLS), F32)]
    if half == 0:
        out_shape.append(jax.ShapeDtypeStruct((D_MODEL, S_W - S_DT), F32))
    else:
        in_specs.append(whole(tail.shape))
        args.append(tail)
    return pl.pallas_call(
        body, name="dw_in_%d" % half, grid=(steps,), in_specs=in_specs,
        out_specs=[whole(o.shape) for o in out_shape], out_shape=out_shape,
        scratch_shapes=[pltpu.VMEM((D_MODEL, N), F32)], compiler_params=_params(("arbitrary",)),
    )(*args)


def _adamw_math(w, g, m, v):
    m = ADAM_B1 * m + (1.0 - ADAM_B1) * g
    v = ADAM_B2 * v + (1.0 - ADAM_B2) * (g * g)
    m_hat = m / (1.0 - ADAM_B1 ** ADAM_STEP)
    v_hat = v / (1.0 - ADAM_B2 ** ADAM_STEP)
    delta = -ADAM_LR * (m_hat / (jnp.sqrt(v_hat) + ADAM_EPS) + ADAM_WD * w)
    return delta, m, v


def _adamw_shard(n_recv, g_own, recv, w, m, v, *, rows, name):
    R, C = g_own.shape

    def body(n_ref, g_ref, r_ref, w_ref, m_ref, v_ref, go_ref, d_ref, mo_ref, vo_ref):
        g = g_ref[...]
        for k in range(N_DEV - 1):
            g = g + jnp.where(k < n_ref[0], r_ref[k].astype(F32), 0.0)
        d, mn, vn = _adamw_math(w_ref[...], g, m_ref[...], v_ref[...])
        go_ref[...] = g
        d_ref[...] = d
        mo_ref[...] = mn
        vo_ref[...] = vn

    blk = pl.BlockSpec((rows, C), lambda i: (i, 0))
    return pl.pallas_call(
        body, name=name, grid=(R // rows,),
        in_specs=[pl.BlockSpec(memory_space=pltpu.SMEM), blk,
                  pl.BlockSpec((N_DEV - 1, rows, C), lambda i: (0, i, 0)), blk, blk, blk],
        out_specs=[blk] * 4, out_shape=[jax.ShapeDtypeStruct((R, C), F32)] * 4,
        compiler_params=_params(("arbitrary",)),
    )(n_recv, g_own, recv, w, m, v)


def _minor_rows_view(a):
    return jnp.transpose(a, (2, 0, 1)).reshape(SHARD_COLS * 8, 128)


def _from_minor_rows_view(v):
    return jnp.transpose(v.reshape(SHARD_COLS, 8, 128), (1, 2, 0)).reshape(1, D_MODEL, SHARD_COLS)


def _adamw_w_in(is_lo, own_lo, own_hi, recv_lo, recv_hi, w, m, v):
    C = SHARD_COLS
    pad = -C % 128

    def body(lo_ref, ol_ref, oh_ref, rl_ref, rh_ref, w_ref, m_ref, v_ref, go_ref, d_ref, mo_ref, vo_ref):
        lo = lo_ref[0] == 1
        for q in range(D_MODEL // 128):
            band = pl.ds(q * 128, 128)
            g = jnp.where(lo, ol_ref[band, :], oh_ref[band, :])
            for k in range(2):
                g = g + jnp.where(lo, rl_ref[k, band, :], rh_ref[k, band, :]).astype(F32)
            g = jnp.pad(g, ((0, 0), (0, pad))).T[0:C]
            rows = pl.ds(q, C, stride=8)
            d, mn, vn = _adamw_math(w_ref[rows, :], g, m_ref[rows, :], v_ref[rows, :])
            go_ref[rows, :] = g
            d_ref[rows, :] = d
            mo_ref[rows, :] = mn
            vo_ref[rows, :] = vn

    return pl.pallas_call(
        body, name="adamw_w_in", out_shape=[jax.ShapeDtypeStruct(w.shape, F32)] * 4,
        in_specs=[pl.BlockSpec(memory_space=pltpu.SMEM)] + [pl.BlockSpec(memory_space=pltpu.VMEM)] * 7,
        out_specs=[pl.BlockSpec(memory_space=pltpu.VMEM)] * 4,
        compiler_params=_params(),
    )(is_lo, own_lo, own_hi, recv_lo, recv_hi, w, m, v)


SMALL = ("conv_b", "dt_bias", "a_log", "d_skip", "ssd_norm_w", "attn_sinks", "ln_g", "ln_b")


def _adamw_small(gathered, params):
    n_p = len(SMALL)

    def body(*refs):
        acc = []
        for r in refs[:5]:
            t = r[0]
            for k in range(1, N_DEV):
                t = t + r[k]
            acc.append(t)
        head, conv, norm, scal, sink = acc
        grads = dict(conv_b=conv[4:5, :], dt_bias=scal[0:1, 0:N_HEADS], a_log=scal[1:2, 0:N_HEADS],
                     d_skip=scal[2:3, 0:N_HEADS], ssd_norm_w=norm[0:1, :], attn_sinks=sink[0:1, 0:N_HEADS],
                     ln_g=head[0:1, :], ln_b=head[1:2, :])
        wmv = refs[5:5 + 3 * n_p]
        outs = refs[5 + 3 * n_p:]
        outs[0][...] = head[3:4, 0:1]
        outs[1][...] = conv[0:4, :]
        for i, name in enumerate(SMALL):
            w_ref, m_ref, v_ref = wmv[3 * i:3 * i + 3]
            g = grads[name]
            d, mn, vn = _adamw_math(w_ref[...], g, m_ref[...], v_ref[...])
            for o_ref, val in zip(outs[2 + 4 * i:6 + 4 * i], (g, d, mn, vn)):
                o_ref[...] = val

    flat = [a for name in SMALL for a in params[name]]
    out_shape = [jax.ShapeDtypeStruct((1, 1), F32), jax.ShapeDtypeStruct((4, D_XBC), F32)]
    for name in SMALL:
        out_shape += [jax.ShapeDtypeStruct(params[name][0].shape, F32)] * 4
    res = pl.pallas_call(body, name="adamw_small", out_shape=out_shape, compiler_params=_params())(*gathered, *flat)
    return res[0], res[1], {name: res[2 + 4 * i:6 + 4 * i] for i, name in enumerate(SMALL)}


def _adamw_plain(g, w, m, v):
    def body(g_ref, w_ref, m_ref, v_ref, d_ref, mo_ref, vo_ref):
        d, mn, vn = _adamw_math(w_ref[...], g_ref[...], m_ref[...], v_ref[...])
        d_ref[...] = d
        mo_ref[...] = mn
        vo_ref[...] = vn

    return pl.pallas_call(
        body, name="adamw_conv_w", out_shape=[jax.ShapeDtypeStruct(w.shape, F32)] * 3,
        compiler_params=_params(),
    )(g, w, m, v)


def _lane_pattern(fn):
    return np.asarray([fn(l % HEAD_DIM) for l in range(128)], np.float32)


ROPE_INV = _lane_pattern(lambda r: ROPE_THETA ** (-2.0 * (r % 8) / ROPE_DIM) if r < ROPE_DIM else 0.0)
ROPE_SIN_A = _lane_pattern(lambda r: 1.0 if 8 <= r < ROPE_DIM else 0.0)
ROPE_SIN_B = _lane_pattern(lambda r: -1.0 if r < 8 else 0.0)


def _rope_tables(positions):
    ang = positions.astype(F32)[:, None] * ROPE_INV[None, :]
    sn = jnp.sin(ang)
    return jnp.concatenate([jnp.cos(ang), sn * ROPE_SIN_A[None, :], sn * ROPE_SIN_B[None, :]], axis=1)


def _expansion():
    E = np.arange(1024)[None, :] // HEAD_DIM == np.arange(128)[:, None]
    return jnp.asarray(E, BF16), jnp.asarray(E.T, BF16)


def _ssd_args(conv_w, conv_b, dt_bias, a_log, d_skip, norm_w, E):
    return (conv_w, conv_b, dt_bias.reshape(-1), a_log.reshape(-1), d_skip.reshape(-1), norm_w, E)


def kernel(x, positions, w_in, conv_w, conv_b, dt_bias, a_log, d_skip, ssd_norm_w, attn_sinks, w_out, ln_g, ln_b, loss_target, m_w_in, m_conv_w, m_conv_b, m_dt_bias, m_a_log, m_d_skip, m_ssd_norm_w, m_attn_sinks, m_w_out, m_ln_g, m_ln_b, v_w_in, v_conv_w, v_conv_b, v_dt_bias, v_a_log, v_d_skip, v_ssd_norm_w, v_attn_sinks, v_w_out, v_ln_g, v_ln_b):
    me = _index(*_position())
    x0, target = x[0], loss_target[0]
    bf16_shard = lambda shape: jax.ShapeDtypeStruct(shape, BF16)
    E, ET = _expansion()
    tabs = _rope_tables(positions[0])
    sinks = attn_sinks.reshape(-1)

    w_ssd, w_att = _unpack_w_in(_gather_w_in(w_in[0].astype(BF16)))
    gather_conv_w = _Hosted([conv_w[0]], [jax.ShapeDtypeStruct((N_DEV,) + conv_w.shape[1:], F32)],
                            [_Flow("gather", 0, 0)])

    proj_ssd, proj_att, xb, conv_w_all = _in_proj(x0, w_ssd, w_att, tm=512, comm=gather_conv_w)
    conv_w_f = jnp.transpose(conv_w_all, (1, 0, 2)).reshape(4, D_XBC)
    ssd_args = _ssd_args(conv_w_f, conv_b, dt_bias, a_log, d_skip, ssd_norm_w, E)
    gather_w_out = _Hosted([w_out[0].astype(BF16)], [bf16_shard((N_DEV, 256, D_MODEL))], [_Flow("gather", 0, 0)])
    y, ypre, hprev, pre, w_out_all = _mixer_forward(proj_ssd, proj_att, tabs, sinks, *ssd_args, comm=gather_w_out)
    w_out_f = w_out_all.reshape(2 * D_MODEL, D_MODEL)
    dr, dy, acc_head = _head(y, x0, target, w_out_f, ln_g, ln_b, tm=512)

    dw_out, dw_out_bf16 = _matmul_tn(y, dr, tl=1024, tn=D_MODEL, name="dw_out", emit_bf16=True)
    own_out = lax.dynamic_index_in_dim(dw_out.reshape(N_DEV, 256, D_MODEL), me, axis=0, keepdims=False)
    send_out = _Hosted([dw_out_bf16.reshape(N_DEV, 256, D_MODEL)], [bf16_shard((N_DEV - 1, 256, D_MODEL))],
                       [_Flow("exchange", 0, 0)])
    d_ssd, acc_cw, acc_w, acc_s, recv_out = _ssd_backward(proj_ssd, hprev, ypre, pre, dy, *ssd_args, ET, comm=send_out)
    stack_lo, dw_dt_block = _dw_in(xb, d_ssd, 0)
    d_att, dsink, own_lo, recv_lo = _swa_backward(proj_att, tabs, sinks, dy, reduce=_OwnerReduce(stack_lo, 0))
    (stack_hi,) = _dw_in(xb, d_att, 1, dw_dt_block)
    accs = [acc_head, acc_cw, acc_w, acc_s, dsink]
    gather_accs = _Hosted(accs, [jax.ShapeDtypeStruct((N_DEV,) + a.shape, F32) for a in accs],
                          [_Flow("gather", i, i) for i in range(5)])
    dx, *gathered, own_hi, recv_hi = _input_gradient(d_ssd, d_att, w_ssd, w_att, dr, tm=256, comm=gather_accs,
                                                     reduce=_OwnerReduce(stack_hi, 1))
    is_lo = (me < 4).reshape(1).astype(jnp.int32)
    n_recv_out = jnp.full((1,), N_DEV - 1, jnp.int32)

    g_in, d_in, nm_in, nv_in = [_from_minor_rows_view(r) for r in _adamw_w_in(
        is_lo, own_lo, own_hi, recv_lo, recv_hi, _minor_rows_view(w_in), _minor_rows_view(m_w_in), _minor_rows_view(v_w_in))]
    g_out, d_out, nm_out, nv_out = _adamw_shard(n_recv_out, own_out, recv_out, w_out[0], m_w_out[0], v_w_out[0],
                                                rows=256, name="adamw_w_out")
    loss, g_conv_w, small = _adamw_small(gathered, dict(
        conv_b=(conv_b, m_conv_b, v_conv_b), dt_bias=(dt_bias, m_dt_bias, v_dt_bias), a_log=(a_log, m_a_log, v_a_log),
        d_skip=(d_skip, m_d_skip, v_d_skip), ssd_norm_w=(ssd_norm_w, m_ssd_norm_w, v_ssd_norm_w),
        attn_sinks=(attn_sinks, m_attn_sinks, v_attn_sinks), ln_g=(ln_g, m_ln_g, v_ln_g), ln_b=(ln_b, m_ln_b, v_ln_b)))
    g_cw = lax.dynamic_slice_in_dim(g_conv_w, me * (D_XBC // N_DEV), D_XBC // N_DEV, axis=1)
    d_cw, nm_cw, nv_cw = _adamw_plain(g_cw, conv_w[0], m_conv_w[0], v_conv_w[0])

    def leaves(i, big_in, cw, big_out):
        mid = [small[k][i] for k in ("conv_b", "dt_bias", "a_log", "d_skip", "ssd_norm_w", "attn_sinks")]
        return [big_in, cw[None]] + mid + [big_out[None], small["ln_g"][i], small["ln_b"][i]]

    return (loss.reshape(()), dx[None], *leaves(0, g_in, g_cw, g_out), *leaves(1, d_in, d_cw, d_out),
            *leaves(2, nm_in, nm_cw, nm_out), *leaves(3, nv_in, nv_cw, nv_out))
```

```python
import jax
import jax.numpy as jnp
from jax import lax
from jax.experimental import pallas as pl
from jax.experimental.pallas import tpu as pltpu
import numpy as np

F32 = jnp.float32
BF16 = jnp.bfloat16
_MXU = jnp.bfloat16

N_DEV = 8
D_MODEL = 1024
D_SSD = 1024
D_ATT = 1024
HEAD_DIM = 64
N_HEADS = 16
SSD_GROUPS = 2
KV_HEADS = 4
CHUNK = 128
D_XBC = 1536
D_IN_PROJ = 5136
ROPE_DIM = 16
ROPE_THETA = 500000.0
ALPHA = (2.0 * 1) ** 0.25
LN_EPS = 1e-5
RMS_EPS = 1e-5
ATT_SCALE = HEAD_DIM ** -0.5
NEG = -1e30

S_Z, S_XS, S_B, S_C, S_DT, S_W = 0, 1024, 2048, 2304, 2560, 2816
N_SSD_REAL = 2576
A_Q, A_K, A_V, A_G, A_W = 0, 1024, 1280, 1536, 2560

ADAM_LR = 0.001
ADAM_B1 = 0.9
ADAM_B2 = 0.999
ADAM_EPS = 1e-08
ADAM_WD = 0.01
ADAM_STEP = 10

VMEM_LIMIT = 48 * 1024 * 1024
MESH = pl.DeviceIdType.MESH


def _params(sem=None):
    return pltpu.CompilerParams(dimension_semantics=sem, vmem_limit_bytes=VMEM_LIMIT)


def _mm(a, b):
    return jnp.dot(a.astype(_MXU), b.astype(_MXU), preferred_element_type=F32)


def _mm_nt(a, b):
    return lax.dot_general(a.astype(_MXU), b.astype(_MXU), (((1,), (1,)), ((), ())),
                           preferred_element_type=F32)


def _mm_tn(a, b):
    return lax.dot_general(a.astype(_MXU), b.astype(_MXU), (((0,), (0,)), ((), ())),
                           preferred_element_type=F32)


def _split3(v):
    hi = v.astype(BF16)
    r = v - hi.astype(F32)
    mid = r.astype(BF16)
    lo = (r - mid.astype(F32)).astype(BF16)
    return hi, mid, lo


def _mm_exact_r(v, p01):
    hi, mid, lo = _split3(v)
    d = lambda a: jnp.dot(a, p01, preferred_element_type=F32)
    return d(hi) + d(mid) + d(lo)


def _mm_exact_l(p01, v):
    hi, mid, lo = _split3(v)
    d = lambda a: jnp.dot(p01, a, preferred_element_type=F32)
    return d(hi) + d(mid) + d(lo)


def _mm_2pass_r(v, p01):
    hi = v.astype(BF16)
    lo = (v - hi.astype(F32)).astype(BF16)
    return jnp.dot(hi, p01, preferred_element_type=F32) + jnp.dot(lo, p01, preferred_element_type=F32)


def _sigmoid(x):
    return 1.0 / (1.0 + jnp.exp(-x))


def _softplus(x):
    e = jnp.exp(-jnp.abs(x))
    u = 1.0 + e
    log1p = jnp.where(u == 1.0, e, jnp.log(u) * (e / (u - 1.0)))
    return jnp.maximum(x, 0.0) + log1p


def _rows8(rows):
    n = rows[0].shape[1]
    rid = lax.broadcasted_iota(jnp.int32, (8, n), 0)
    out = jnp.zeros((8, n), F32)
    for k, r in enumerate(rows):
        out = out + jnp.where(rid == k, r, 0.0)
    return out


def _colsum(a):
    return jnp.sum(a, axis=0, keepdims=True)


def _in_proj(x, w_ssd, w_att, *, tm, comm=None):
    L, K = x.shape

    def body(x_ref, ws_ref, wa_ref, ps_ref, pa_ref, xb_ref):
        xb = x_ref[...].astype(_MXU)
        xb_ref[...] = xb
        ps_ref[...] = jnp.dot(xb, ws_ref[...], preferred_element_type=F32)
        pa_ref[...] = jnp.dot(xb, wa_ref[...], preferred_element_type=F32)

    row = lambda w: pl.BlockSpec((tm, w), lambda i: (i, 0))
    resident = lambda a: pl.BlockSpec(a.shape, lambda i: (0, 0), pipeline_mode=pl.Buffered(1))
    return _call(
        body, comm, name="in_proj", grid=(L // tm,),
        in_specs=[row(K), resident(w_ssd), resident(w_att)], out_specs=[row(S_W), row(A_W), row(K)],
        out_shape=[jax.ShapeDtypeStruct((L, S_W), F32), jax.ShapeDtypeStruct((L, A_W), F32),
                   jax.ShapeDtypeStruct((L, K), _MXU)],
        scratch_shapes=[], args=(x, w_ssd, w_att))


def _matmul_tn(a, g, *, tl, tn, name, emit_bf16=False):
    L, M = a.shape
    N = g.shape[1]
    last = L // tl - 1

    def body(a_ref, g_ref, o_ref, *rest):
        @pl.when(pl.program_id(1) == 0)
        def _():
            o_ref[...] = jnp.zeros_like(o_ref)

        o_ref[...] += _mm_tn(a_ref[...], g_ref[...])
        if emit_bf16:
            @pl.when(pl.program_id(1) == last)
            def _():
                rest[0][...] = o_ref[...].astype(BF16)

    spec = pl.BlockSpec((M, tn), lambda j, l: (0, j))
    res = pl.pallas_call(
        body, name=name, grid=(N // tn, L // tl),
        in_specs=[pl.BlockSpec((tl, M), lambda j, l: (l, 0)), pl.BlockSpec((tl, tn), lambda j, l: (l, j))],
        out_specs=[spec, spec] if emit_bf16 else [spec],
        out_shape=[jax.ShapeDtypeStruct((M, N), F32)] + ([jax.ShapeDtypeStruct((M, N), BF16)] if emit_bf16 else []),
        compiler_params=_params(("arbitrary", "arbitrary")),
    )(a, g)
    return res if emit_bf16 else res[0]


def _position():
    return lax.axis_index("x"), lax.axis_index("y"), lax.axis_index("c")


def _index(px, py, pc):
    return 4 * px + 2 * py + pc


def _flip(pos, k):
    x, y, c = pos
    return ((1 - x) if (k >> 2) & 1 else x, (1 - y) if (k >> 1) & 1 else y, (1 - c) if k & 1 else c)


def _remote(src, dst, send_sem, recv_sem, peer):
    return pltpu.make_async_remote_copy(src_ref=src, dst_ref=dst, send_sem=send_sem, recv_sem=recv_sem,
                                        device_id=peer, device_id_type=MESH)


class _Flow:
    def __init__(self, kind, operand, result):
        self.kind, self.operand, self.result = kind, operand, result


class _Hosted:
    def __init__(self, operands, out_shapes, flows):
        self.operands, self.out_shapes, self.flows = operands, out_shapes, flows

    def plan(self, ins, outs, send_sems, recv_sems, local_sems):
        me = _position()
        mi = _index(*me)
        sends, recvs, locals_ = [], [], []
        for row, f in enumerate(self.flows):
            src, dst = ins[f.operand], outs[f.result]
            for k in range(1, N_DEV):
                peer = _flip(me, k)
                sems = (send_sems.at[row, k - 1], recv_sems.at[row, k - 1])
                if f.kind == "exchange":
                    sends.append(_remote(src.at[_index(*peer)], dst.at[k - 1], *sems, peer))
                    recvs.append(sends[-1])
                else:
                    sends.append(_remote(src, dst.at[mi], *sems, peer))
                    recvs.append(_remote(src, dst.at[_index(*peer)], *sems, peer))
            if f.kind == "gather":
                locals_.append(pltpu.make_async_copy(src, dst.at[mi], local_sems.at[row]))

        def start():
            for cp in locals_ + sends:
                cp.start()

        def wait():
            for cp in recvs:
                cp.wait_recv()
            for cp in sends:
                cp.wait_send()
            for cp in locals_:
                cp.wait()

        return start, wait


class _OwnerReduce:
    FIRST_STEP, SECOND_STEP, SEND_STEPS, REDUCE_STEPS = 2, 4, (2, 4, 6, 8), (5, 8, 10, 13)

    def __init__(self, stack, target_x):
        self.stack, self.target_x = stack, target_x
        block = stack.shape[1:]
        self.chunks = len(self.SEND_STEPS)
        self.chunk_rows = block[0] // self.chunks
        self.out_shapes = [jax.ShapeDtypeStruct(block, F32), jax.ShapeDtypeStruct((2,) + block, BF16)]
        self.out_specs = [pl.BlockSpec(block, lambda *_: (0, 0), pipeline_mode=pl.Buffered(1)),
                          pl.BlockSpec(memory_space=pl.ANY)]
        dma = pltpu.SemaphoreType.DMA
        self.scratch_shapes = ([pltpu.VMEM((2,) + block, F32)] * 2 + [pltpu.VMEM(block, BF16)] * 3
                               + [dma((self.chunks,))] * 4 + [dma((2,))] * 3)

    def plan(self, i, steps, stack_ref, own_ref, recv_ref, scratch):
        assert self.FIRST_STEP <= self.SEND_STEPS[0] and self.SECOND_STEP < self.REDUCE_STEPS[0] < steps - 1
        (theirs_scr, mine_scr, first_scr, across_scr, out_scr, y_send_sems, y_recv_sems, x_send_sems, x_recv_sems,
         swap_send_sems, swap_recv_sems, mine_sems) = scratch
        x, y, c = _position()
        owners_side = x == self.target_x
        other_side = x != self.target_x
        sibling, across, owner = (x, y, 1 - c), (x, 1 - y, c), (self.target_x, y, c)
        order = (1 - y, y)
        swaps = [_remote(stack_ref.at[2 * order[j] + (1 - c)], theirs_scr.at[j], swap_send_sems.at[j], swap_recv_sems.at[j],
                         sibling) for j in range(2)]
        mine = [pltpu.make_async_copy(stack_ref.at[2 * order[j] + c], mine_scr.at[j], mine_sems.at[j]) for j in range(2)]
        chunks = range(self.chunks)
        part = [pl.ds(j * self.chunk_rows, self.chunk_rows) for j in chunks]
        y_sems = lambda j: (y_send_sems.at[j], y_recv_sems.at[j])
        to_neighbour = [_remote(first_scr.at[part[j]], across_scr.at[part[j]], *y_sems(j), across) for j in chunks]
        to_owner_y = [_remote(first_scr.at[part[j]], recv_ref.at[0, part[j]], *y_sems(j), across) for j in chunks]
        to_owner_x = [_remote(out_scr.at[part[j]], recv_ref.at[1, part[j]], x_send_sems.at[j], x_recv_sems.at[j], owner)
                      for j in chunks]

        def before():
            @pl.when(i == 0)
            def _():
                for cp in [swaps[0]] + mine:
                    cp.start()

            pl.when(i == 1)(swaps[1].start)

        def after():
            @pl.when(i == self.FIRST_STEP)
            def _():
                swaps[0].wait_recv()
                mine[0].wait()
                first_scr[...] = (mine_scr[0] + theirs_scr[0]).astype(first_scr.dtype)

            for j in chunks:
                pl.when((i == self.SEND_STEPS[j]) & other_side)(to_neighbour[j].start)
                pl.when((i == self.SEND_STEPS[j]) & owners_side)(to_owner_y[j].start)

            @pl.when(i == self.SECOND_STEP)
            def _():
                swaps[1].wait_recv()
                mine[1].wait()
                t = mine_scr[1] + theirs_scr[1]
                own_ref[...] = t
                mine_scr[1] = t

            for j in chunks:
                @pl.when((i == self.REDUCE_STEPS[j]) & other_side)
                def _(j=j):
                    to_neighbour[j].wait_recv()
                    t = mine_scr[1, part[j], :] + across_scr[part[j], :].astype(F32)
                    out_scr[part[j], :] = t.astype(out_scr.dtype)
                    to_owner_x[j].start()

            @pl.when(i == steps - 1)
            def _():
                for cp in swaps:
                    cp.wait_send()
                for j in chunks:
                    @pl.when(other_side)
                    def _(j=j):
                        to_neighbour[j].wait_send()
                        to_owner_x[j].wait_send()

                    @pl.when(owners_side)
                    def _(j=j):
                        to_owner_y[j].wait_send()
                        to_owner_y[j].wait_recv()
                        to_owner_x[j].wait_recv()

        return before, after


def _call(body, comm, *, name, grid, in_specs, out_specs, out_shape, scratch_shapes, args, reduce=None):
    semantics = ("arbitrary",) * len(grid)
    if comm is None and reduce is None:
        return pl.pallas_call(body, name=name, grid=grid, in_specs=in_specs, out_specs=out_specs, out_shape=out_shape,
                              scratch_shapes=scratch_shapes, compiler_params=_params(semantics))(*args)
    n_in, n_out, n_scr = len(args), len(out_shape), len(scratch_shapes)
    c_operands, c_shapes, flows = (comm.operands, comm.out_shapes, comm.flows) if comm else ([], [], [])
    c_in, c_out, rows = len(c_operands), len(c_shapes), max(len(flows), 1)
    r_in = 0 if reduce is None else 1

    def hosted(*refs):
        ins, refs = refs[:n_in], refs[n_in:]
        cins, refs = refs[:c_in], refs[c_in:]
        rins, refs = refs[:r_in], refs[r_in:]
        outs, refs = refs[:n_out], refs[n_out:]
        couts, refs = refs[:c_out], refs[c_out:]
        routs, refs = refs[:2 * r_in], refs[2 * r_in:]
        scr, refs = refs[:n_scr], refs[n_scr:]
        (send_sems, recv_sems, local_sems), r_scr = refs[:3], refs[3:]
        ids = [pl.program_id(d) for d in range(len(grid))]
        first, last = ids[0] == 0, ids[0] == grid[0] - 1
        for d in range(1, len(grid)):
            first, last = first & (ids[d] == 0), last & (ids[d] == grid[d] - 1)
        before = after = lambda: None
        if reduce is not None:
            before, after = reduce.plan(ids[0], grid[0], rins[0], *routs, r_scr)
        if comm is not None:
            start, wait = comm.plan(cins, couts, send_sems, recv_sems, local_sems)
            pl.when(first)(start)
        before()
        body(*ins, *outs, *scr)
        after()
        if comm is not None:
            pl.when(last)(wait)

    any_spec = pl.BlockSpec(memory_space=pl.ANY)
    sems = [pltpu.SemaphoreType.DMA((rows, N_DEV - 1)), pltpu.SemaphoreType.DMA((rows, N_DEV - 1)),
            pltpu.SemaphoreType.DMA((rows,))]
    r_operands, r_specs, r_shapes, r_scratch = ([reduce.stack], reduce.out_specs, reduce.out_shapes,
                                                reduce.scratch_shapes) if reduce else ([], [], [], [])
    return pl.pallas_call(
        hosted, name=name, grid=grid, in_specs=list(in_specs) + [any_spec] * (c_in + r_in),
        out_specs=list(out_specs) + [any_spec] * c_out + r_specs, out_shape=list(out_shape) + list(c_shapes) + r_shapes,
        scratch_shapes=list(scratch_shapes) + sems + r_scratch,
        compiler_params=_params(semantics))(*args, *c_operands, *r_operands)


def _head_row(ref, width, rep):
    hid = lax.broadcasted_iota(jnp.int32, (1, width), 1) // rep
    row = jnp.zeros((1, width), F32)
    for h in range(N_HEADS):
        row = jnp.where(hid == h, ref[h], row)
    return row


def _rows_from_above(u_b, s, ext_scr, row, col):
    down = (row - col == s).astype(_MXU)
    return jnp.concatenate([ext_scr[8 - s:16 - s, :], jnp.dot(down, u_b, preferred_element_type=F32)[8:128]], axis=0)


def _ssd_recompute(first, p_ref, halo_ref, cw_ref, cb_ref, dtb_ref, alog_ref, e_ref, ext_scr, pre=None):
    row = lax.broadcasted_iota(jnp.int32, (128, 128), 0)
    col = lax.broadcasted_iota(jnp.int32, (128, 128), 1)
    ext_scr[0:8, :] = jnp.where(first, 0.0, halo_ref[:, S_XS:S_DT])
    if pre is not None:
        ext_scr[8:16, :] = p_ref[0:8, S_XS:S_DT]
    else:
        ext_scr[8:136, :] = p_ref[:, S_XS:S_DT]
        cw = cw_ref[...]
        pre = (cb_ref[0:1, :] + cw[3:4, :] * ext_scr[8:136, :] + cw[2:3, :] * ext_scr[7:135, :]
               + cw[1:2, :] * ext_scr[6:134, :] + cw[0:1, :] * ext_scr[5:133, :])
    sg = _sigmoid(pre)
    act = pre * sg
    lane = lax.broadcasted_iota(jnp.int32, (1, 128), 1)
    A = jnp.where(lane < N_HEADS, -jnp.exp(_head_row(alog_ref, 128, 1)), 0.0)
    raw = p_ref[:, S_DT:S_DT + 128] + _head_row(dtb_ref, 128, 1)
    dt = _softplus(raw)
    dA = dt * A
    tril = (row >= col).astype(BF16)
    acs = _mm_exact_l(tril, dA)
    last = acs[127:128, :]
    ds = jnp.exp(last - acs)
    eo = jnp.exp(acs)
    E = e_ref[...]
    ex = _mm_2pass_r(jnp.concatenate([dt, ds, eo], axis=0), E)
    dt_e, ds_e, eo_e = ex[0:128], ex[128:256], ex[256:384]
    xs_c = act[:, 0:1024]
    X = xs_c * dt_e
    return dict(pre=pre, sg=sg, xs_c=xs_c, Bc=act[:, 1024:1280], Cc=act[:, 1280:1536], A=A, raw=raw, dt=dt,
                acs=acs, acsT=acs.T, eo_e=eo_e, ds_e=ds_e, dt_e=dt_e, cd_e=eo_e[127:128, :],
                X=X, Xd=X * ds_e, row=row, col=col)


def _split_halves(t):
    lo = _lo_half(CHUNK)
    return jnp.concatenate([jnp.where(lo, t, 0.0), jnp.where(lo, 0.0, t)], axis=0)


def _ssd_core(R, hprev):
    causal = R["row"] >= R["col"]
    acs, acsT, X = R["acs"], R["acsT"], R["X"]
    ydiag, yoff, snew = [], [], []
    for g in range(SSD_GROUPS):
        Bg = R["Bc"][:, g * 128:(g + 1) * 128]
        Cg = R["Cc"][:, g * 128:(g + 1) * 128]
        cols = slice(g * 512, (g + 1) * 512)
        CB = _mm_nt(Cg, Bg)
        snew.append(_mm_tn(Bg, R["Xd"][:, cols]))
        yoff.append(_mm(Cg, hprev[:, cols]))
        for j in range(4):
            h0 = g * 8 + 2 * j
            ms = [CB * jnp.exp(jnp.where(causal, acs[:, h:h + 1] - acsT[h:h + 1, :], NEG)) for h in (h0, h0 + 1)]
            ydiag.append(_mm(jnp.concatenate(ms, axis=1), _split_halves(X[:, h0 * HEAD_DIM:h0 * HEAD_DIM + 128])))
    Y = jnp.concatenate(ydiag, axis=1) + jnp.concatenate(yoff, axis=1) * R["eo_e"]
    return Y, jnp.concatenate(snew, axis=1)


def _ssd_forward_step(p_ref, halo_ref, cw_ref, cb_ref, dtb_ref, alog_ref, dsk_ref, nw_ref, e_ref,
                      y_ref, ypre_ref, hprev_ref, pre_ref, h_scr, ext_scr):
    c = pl.program_id(0)
    first = c == 0

    @pl.when(first)
    def _():
        h_scr[...] = jnp.zeros_like(h_scr)

    R = _ssd_recompute(first, p_ref, halo_ref, cw_ref, cb_ref, dtb_ref, alog_ref, e_ref, ext_scr)
    hprev = h_scr[...]
    hprev_ref[...] = hprev
    pre_ref[...] = R["pre"]
    Y, snew = _ssd_core(R, hprev)
    h_scr[...] = hprev * R["cd_e"] + snew
    Y = Y + _head_row(dsk_ref, D_SSD, HEAD_DIM) * R["xs_c"]
    ypre_ref[...] = Y
    z = p_ref[:, S_Z:S_Z + 1024]
    yf = Y * (z * _sigmoid(z))
    outs = []
    for g in range(SSD_GROUPS):
        yg = yf[:, g * 512:(g + 1) * 512]
        r = lax.rsqrt(jnp.mean(yg * yg, axis=-1, keepdims=True) + RMS_EPS)
        outs.append(yg * r)
    y_ref[:, 0:D_SSD] = (jnp.concatenate(outs, axis=1) * nw_ref[0:1, :]).astype(y_ref.dtype)


def _ssd_backward(proj_ssd, hprev_all, ypre, pre, dy, conv_w, conv_b, dt_bias, a_log, d_skip, norm_w, E, ET, comm=None):
    L = proj_ssd.shape[0]
    nc = L // CHUNK

    def body(p_ref, halo_ref, hprev_ref, ypre_ref, pre_ref, dy_ref, cw_ref, cb_ref, dtb_ref, alog_ref, dsk_ref, nw_ref, e_ref,
             et_ref, dp_ref, acc_cw_ref, acc_w_ref, acc_s_ref, dh_scr, ext_scr, ext2_scr, nxt_scr):
        i = pl.program_id(0)
        c = nc - 1 - i
        first = c == 0

        @pl.when(i == 0)
        def _():
            dh_scr[...] = jnp.zeros_like(dh_scr)
            nxt_scr[...] = jnp.zeros_like(nxt_scr)
            acc_cw_ref[...] = jnp.zeros_like(acc_cw_ref)
            acc_w_ref[...] = jnp.zeros_like(acc_w_ref)
            acc_s_ref[...] = jnp.zeros_like(acc_s_ref)

        R = _ssd_recompute(first, p_ref, halo_ref, cw_ref, cb_ref, dtb_ref, alog_ref, e_ref, ext_scr, pre_ref[...])
        hprev = hprev_ref[...]
        xs_c, X, Xd = R["xs_c"], R["X"], R["Xd"]
        acs, acsT = R["acs"], R["acsT"]
        ET = et_ref[...]
        dsk = _head_row(dsk_ref, D_SSD, HEAD_DIM)
        Y = ypre_ref[...]

        z = p_ref[:, S_Z:S_Z + 1024]
        sz = _sigmoid(z)
        silz = z * sz
        yf = Y * silz
        dyv = dy_ref[...]
        nw = nw_ref[0:1, :]
        dyf_parts, dnw_parts = [], []
        for g in range(SSD_GROUPS):
            cols = slice(g * 512, (g + 1) * 512)
            yg = yf[:, cols]
            r = lax.rsqrt(jnp.mean(yg * yg, axis=-1, keepdims=True) + RMS_EPS)
            yn = yg * r
            dyn = dyv[:, cols] * nw[:, cols]
            dnw_parts.append(_colsum(dyv[:, cols] * yn))
            dyf_parts.append(r * (dyn - yn * jnp.mean(dyn * yn, axis=-1, keepdims=True)))
        dyf = jnp.concatenate(dyf_parts, axis=1)
        dY = dyf * silz
        dz = dyf * Y * (sz * (1.0 + z * (1.0 - sz)))

        dhn = dh_scr[...]
        dYo = dY * R["eo_e"]
        causal = R["row"] >= R["col"]
        dacs = jnp.zeros((128, 128), F32)
        dacs_t = jnp.zeros((128, 128), F32)
        dxdiag, dxd, dhprev, dBs, dCs, yoff = [], [], [], [], [], []
        for g in range(SSD_GROUPS):
            Bg = R["Bc"][:, g * 128:(g + 1) * 128]
            Cg = R["Cc"][:, g * 128:(g + 1) * 128]
            cols = slice(g * 512, (g + 1) * 512)
            CB = _mm_nt(Cg, Bg)
            dCB = jnp.zeros((128, 128), F32)
            for j in range(4):
                h0 = g * 8 + 2 * j
                pc = slice(h0 * HEAD_DIM, h0 * HEAD_DIM + 128)
                dYst = _split_halves(dY[:, pc])
                dMst = _mm_nt(dYst, X[:, pc])
                mts = []
                for a, h in enumerate((h0, h0 + 1)):
                    acol = acs[:, h:h + 1]
                    arow = acsT[h:h + 1, :]
                    Lm = jnp.exp(jnp.where(causal, acol - arow, NEG))
                    M = CB * Lm
                    dM = dMst[a * 128:(a + 1) * 128]
                    dCB = dCB + dM * Lm
                    G = dM * M
                    dacs = dacs + jnp.where(R["col"] == h, jnp.sum(G, axis=1, keepdims=True), 0.0)
                    dacs_t = dacs_t + jnp.where(R["row"] == h, jnp.sum(G, axis=0, keepdims=True), 0.0)
                    mts.append(M.T)
                dxdiag.append(_mm(jnp.concatenate(mts, axis=1), dYst))
            dS = dhn[:, cols]
            dxd.append(_mm(Bg, dS))
            yoff.append(_mm(Cg, hprev[:, cols]))
            dhprev.append(_mm_tn(Cg, dYo[:, cols]))
            dCs.append(_mm_nt(dYo[:, cols], hprev[:, cols]) + _mm(dCB, Bg))
            dBs.append(_mm_tn(dCB, Cg) + _mm_nt(Xd[:, cols], dS))
        Yoff = jnp.concatenate(yoff, axis=1) * R["eo_e"]
        dXd = jnp.concatenate(dxd, axis=1)
        dX = jnp.concatenate(dxdiag, axis=1) + dXd * R["ds_e"]
        t_state = dXd * Xd
        dacs = dacs + _mm_2pass_r(dY * Yoff - t_state, ET) - dacs_t.T
        v_last = _colsum(t_state + dhn * hprev * R["cd_e"])
        dlast = _mm_exact_r(jnp.broadcast_to(v_last, (8, 1024)), ET)[0:1, :]
        dacs = dacs + jnp.where(R["row"] == 127, dlast, 0.0)
        triu = (R["col"] >= R["row"]).astype(BF16)
        da = _mm_exact_l(triu, dacs)
        ddt = da * R["A"] + _mm(dX * xs_c, ET)
        ddt_raw = ddt * _sigmoid(R["raw"])
        dxs_c = dX * R["dt_e"] + dY * dsk
        dh_scr[...] = jnp.concatenate(dhprev, axis=1) + dhn * R["cd_e"]

        dact = jnp.concatenate([dxs_c] + dBs + dCs, axis=1)
        pre, sg = R["pre"], R["sg"]
        dpre = dact * (sg * (1.0 + pre * (1.0 - sg)))
        ext2_scr[0:8, :] = dpre[120:128, :]
        ext2_scr[8:16, :] = nxt_scr[...]
        nxt_scr[...] = dpre[0:8, :]
        cw = cw_ref[...]
        u_b, dpre_b = p_ref[:, S_XS:S_DT].astype(_MXU), dpre.astype(_MXU)
        dxbc = cw[3:4, :] * dpre
        taps = [_colsum(dpre * p_ref[:, S_XS:S_DT])]
        for s in (1, 2, 3):
            up = (R["col"] - R["row"] == s).astype(_MXU)
            d_s = jnp.concatenate([jnp.dot(up, dpre_b, preferred_element_type=F32)[0:120],
                                   ext2_scr[s:8 + s, :]], axis=0)
            dxbc = dxbc + cw[3 - s:4 - s, :] * d_s
            taps.append(_colsum(dpre * _rows_from_above(u_b, s, ext_scr, R["row"], R["col"])))
        acc_cw_ref[...] += _rows8(taps[::-1] + [_colsum(dpre)])
        acc_w_ref[...] += _rows8([jnp.concatenate(dnw_parts, axis=1), _colsum(dY * xs_c)])
        acc_s_ref[...] += _rows8([_colsum(ddt_raw), _colsum(da * R["dt"])])

        lane = lax.broadcasted_iota(jnp.int32, (128, 128), 1)
        dp_ref[:, S_Z:S_Z + 1024] = dz.astype(dp_ref.dtype)
        dp_ref[:, S_XS:S_DT] = dxbc.astype(dp_ref.dtype)
        dp_ref[:, S_DT:S_DT + 128] = jnp.where(lane < N_HEADS, ddt_raw, 0.0).astype(dp_ref.dtype)
        dp_ref[:, S_DT + 128:S_W] = jnp.zeros((128, 128), dp_ref.dtype)

        @pl.when(i == nc - 1)
        def _():
            acc = acc_s_ref[...]
            dskip = _mm_exact_r(acc_w_ref[...], ET)[1:2, :]
            acc_s_ref[...] = _rows8([acc[0:1, :], acc[1:2, :] * R["A"], dskip])

    const = lambda shape: pl.BlockSpec(shape, lambda i: (0, 0))
    smem = pl.BlockSpec(memory_space=pltpu.SMEM)
    rev = lambda i: (nc - 1 - i, 0)
    return _call(
        body, comm, name="ssd_bwd", grid=(nc,),
        in_specs=[pl.BlockSpec((CHUNK, S_W), rev),
                  pl.BlockSpec((8, S_W), lambda i: (jnp.maximum((nc - 1 - i) * 16 - 1, 0), 0)),
                  pl.BlockSpec((128, 1024), rev),
                  pl.BlockSpec((CHUNK, D_SSD), rev),
                  pl.BlockSpec((CHUNK, D_XBC), rev),
                  pl.BlockSpec((CHUNK, D_SSD), rev),
                  const((4, D_XBC)), const((1, D_XBC)), smem, smem, smem, const((1, 1024)),
                  const((128, 1024)), const((1024, 128))],
        out_specs=[pl.BlockSpec((CHUNK, S_W), rev), const((8, D_XBC)), const((8, 1024)), const((8, 128))],
        out_shape=[jax.ShapeDtypeStruct((L, S_W), _MXU), jax.ShapeDtypeStruct((8, D_XBC), F32),
                   jax.ShapeDtypeStruct((8, 1024), F32), jax.ShapeDtypeStruct((8, 128), F32)],
        scratch_shapes=[pltpu.VMEM((128, 1024), F32), pltpu.VMEM((16, D_XBC), F32),
                        pltpu.VMEM((16, D_XBC), F32), pltpu.VMEM((8, D_XBC), F32)],
        args=(proj_ssd, proj_ssd, hprev_all, ypre, pre, dy, conv_w, conv_b, dt_bias, a_log, d_skip, norm_w, E, ET))


def _rope(t, tab):
    cos, sa, sb = tab[:, 0:128], tab[:, 128:256], tab[:, 256:384]
    outs = []
    for i in range(t.shape[1] // 128):
        tg = t[:, i * 128:(i + 1) * 128]
        outs.append(tg * cos + pltpu.roll(tg, 8, 1) * sa + pltpu.roll(tg, 120, 1) * sb)
    return jnp.concatenate(outs, axis=1)


def _rope_transposed(d, tab):
    cos, sa, sb = tab[:, 0:128], tab[:, 128:256], tab[:, 256:384]
    outs = []
    for i in range(d.shape[1] // 128):
        dg = d[:, i * 128:(i + 1) * 128]
        outs.append(dg * cos + pltpu.roll(dg * sa, 120, 1) + pltpu.roll(dg * sb, 8, 1))
    return jnp.concatenate(outs, axis=1)


def _lo_half(rows):
    return lax.broadcasted_iota(jnp.int32, (rows, 128), 1) < HEAD_DIM


def _native_half(rows, j):
    lo = _lo_half(rows)
    return lo if j % 2 == 0 else jnp.logical_not(lo)


def _kv_native(t, j):
    p = j // 2
    return jnp.where(_native_half(t.shape[0], j), t[:, p * 128:(p + 1) * 128], 0.0)


def _stack_heads(t, j):
    out = []
    for m in (2 * j, 2 * j + 1):
        pair = t[:, m * 128:(m + 1) * 128]
        swapped = pltpu.roll(pair, HEAD_DIM, 1)
        out += [pair, swapped] if j % 2 == 0 else [swapped, pair]
    return jnp.concatenate(out, axis=0)


def _unstack_heads(s, j):
    out = []
    for m in range(2):
        first, second = s[256 * m:256 * m + 128], s[256 * m + 128:256 * m + 256]
        if j % 2 == 0:
            out.append(first + pltpu.roll(second, HEAD_DIM, 1))
        else:
            out.append(pltpu.roll(first, HEAD_DIM, 1) + second)
    return jnp.concatenate(out, axis=1)


def _keep_native(r, j):
    return jnp.where(_native_half(r.shape[0], j), r, 0.0)


def _sink_row(sink_ref, j):
    hid = lax.broadcasted_iota(jnp.int32, (1, 4 * CHUNK), 1) // CHUNK
    row = jnp.zeros((1, 4 * CHUNK), F32)
    for hh in range(4):
        row = jnp.where(hid == hh, sink_ref[4 * j + hh], row)
    return row


def _from_current():
    si = lax.broadcasted_iota(jnp.int32, (CHUNK, 4 * CHUNK), 0)
    qi = lax.broadcasted_iota(jnp.int32, (CHUNK, 4 * CHUNK), 1) % CHUNK
    return si <= qi


def _fold(full, from_cur, pen=0.0):
    return jnp.where(from_cur, full[CHUNK:2 * CHUNK], full[0:CHUNK] + pen)


def _unfold(t, from_cur):
    c = jnp.where(from_cur, t, 0.0)
    return jnp.concatenate([t - c, c], axis=0)


def _softmax_sink(s, sink):
    mx = jnp.maximum(jnp.max(s, axis=0, keepdims=True), sink)
    p = jnp.exp(s - mx)
    esink = jnp.exp(sink - mx)
    inv = 1.0 / (jnp.sum(p, axis=0, keepdims=True) + esink)
    return p * inv, esink * inv


def _swa_inputs(blk, p_ref, prev_ref, tab_ref, ptab_ref):
    tab = tab_ref[...]
    qr = _rope(p_ref[:, A_Q:A_Q + 1024], tab) * ATT_SCALE
    kk = jnp.concatenate([_rope(prev_ref[:, 0:256], ptab_ref[...]), _rope(p_ref[:, A_K:A_K + 256], tab)], axis=0)
    vv = jnp.concatenate([prev_ref[:, 256:512], p_ref[:, A_V:A_V + 256]], axis=0)
    return tab, qr, kk, vv, jnp.where(blk > 0, 0.0, NEG)


def _swa_forward_step(sink_ref, p_ref, prev_ref, tab_ref, ptab_ref, y_ref):
    n = pl.program_id(0)
    _, qr, kk, vv, pen = _swa_inputs(n, p_ref, prev_ref, tab_ref, ptab_ref)
    from_cur = _from_current()
    outs = []
    for j in range(KV_HEADS):
        s = _fold(_mm_nt(_kv_native(kk, j), _stack_heads(qr, j)), from_cur, pen)
        P, _ = _softmax_sink(s, _sink_row(sink_ref, j))
        outs.append(_unstack_heads(_mm_tn(_unfold(P, from_cur), _kv_native(vv, j)), j))
    g = p_ref[:, A_G:A_G + 1024]
    y_ref[:, D_SSD:D_SSD + D_ATT] = (jnp.concatenate(outs, axis=1) * (g * _sigmoid(g))).astype(y_ref.dtype)


def _mixer_forward(proj_ssd, proj_att, tabs, sinks, conv_w, conv_b, dt_bias, a_log, d_skip, norm_w, E, comm=None):
    L = proj_ssd.shape[0]
    nc = L // CHUNK

    def body(p_ref, halo_ref, cw_ref, cb_ref, dtb_ref, alog_ref, dsk_ref, nw_ref, e_ref,
             sink_ref, pa_ref, prev_ref, tab_ref, ptab_ref, y_ref, ypre_ref, hprev_ref, pre_ref, h_scr, ext_scr):
        _ssd_forward_step(p_ref, halo_ref, cw_ref, cb_ref, dtb_ref, alog_ref, dsk_ref, nw_ref, e_ref,
                          y_ref, ypre_ref, hprev_ref, pre_ref, h_scr, ext_scr)
        _swa_forward_step(sink_ref, pa_ref, prev_ref, tab_ref, ptab_ref, y_ref)

    const = lambda shape: pl.BlockSpec(shape, lambda c: (0, 0))
    smem = pl.BlockSpec(memory_space=pltpu.SMEM)
    rows = lambda w: pl.BlockSpec((CHUNK, w), lambda c: (c, 0))
    return _call(
        body, comm, name="mixer_fwd", grid=(nc,),
        in_specs=[rows(S_W), pl.BlockSpec((8, S_W), lambda c: (jnp.maximum(c * 16 - 1, 0), 0)),
                  const((4, D_XBC)), const((1, D_XBC)), smem, smem, smem, const((1, 1024)), const((128, 1024)),
                  smem, rows(A_W), pl.BlockSpec((CHUNK, 512), lambda c: (jnp.maximum(c - 1, 0), 2)),
                  rows(384), pl.BlockSpec((CHUNK, 384), lambda c: (jnp.maximum(c - 1, 0), 0))],
        out_specs=[rows(D_SSD + D_ATT), rows(D_SSD), pl.BlockSpec((128, 1024), lambda c: (c, 0)), rows(D_XBC)],
        out_shape=[jax.ShapeDtypeStruct((L, D_SSD + D_ATT), _MXU), jax.ShapeDtypeStruct((L, D_SSD), F32),
                   jax.ShapeDtypeStruct((nc * 128, 1024), F32), jax.ShapeDtypeStruct((L, D_XBC), F32)],
        scratch_shapes=[pltpu.VMEM((128, 1024), F32), pltpu.VMEM((136, D_XBC), F32)],
        args=(proj_ssd, proj_ssd, conv_w, conv_b, dt_bias, a_log, d_skip, norm_w, E,
              sinks, proj_att, proj_att, tabs, tabs))


def _swa_backward(proj_att, tabs, sinks, dy, reduce=None):
    L = proj_att.shape[0]
    nb = L // CHUNK

    def body(sink_ref, p_ref, prev_ref, tab_ref, ptab_ref, dy_ref, dp_ref, dsink_ref, carry_k, carry_v):
        i = pl.program_id(0)
        n = nb - 1 - i

        @pl.when(i == 0)
        def _():
            carry_k[...] = jnp.zeros_like(carry_k)
            carry_v[...] = jnp.zeros_like(carry_v)
            dsink_ref[...] = jnp.zeros_like(dsink_ref)

        tab, qr, kk, vv, pen = _swa_inputs(n, p_ref, prev_ref, tab_ref, ptab_ref)
        from_cur = _from_current()
        g = p_ref[:, A_G:A_G + 1024]
        sgm = _sigmoid(g)
        dyv = dy_ref[...]
        do_all = dyv * (g * sgm)
        lane8 = lax.broadcasted_iota(jnp.int32, (8, 128), 1)
        hid = lax.broadcasted_iota(jnp.int32, (1, 4 * CHUNK), 1) // CHUNK
        o_parts, dq_parts = [], []
        dk_nat = [jnp.zeros((2 * CHUNK, 128), F32) for _ in range(2)]
        dv_nat = [jnp.zeros((2 * CHUNK, 128), F32) for _ in range(2)]
        dsink = jnp.zeros((8, 128), F32)
        for j in range(KV_HEADS):
            qs = _stack_heads(qr, j)
            kkb, vvb = _kv_native(kk, j), _kv_native(vv, j)
            P, psink = _softmax_sink(_fold(_mm_nt(kkb, qs), from_cur, pen), _sink_row(sink_ref, j))
            p_full = _unfold(P, from_cur)
            o_parts.append(_unstack_heads(_mm_tn(p_full, vvb), j))
            do_s = _stack_heads(do_all, j)
            dP = _fold(_mm_nt(vvb, do_s), from_cur)
            D = jnp.sum(P * dP, axis=0, keepdims=True)
            ds_full = _unfold(P * (dP - D), from_cur)
            sd = psink * D
            for hh in range(4):
                dsink = dsink + jnp.where(lane8 == 4 * j + hh, -jnp.sum(jnp.where(hid == hh, sd, 0.0)), 0.0)
            dq_parts.append(_unstack_heads(_mm_tn(ds_full, kkb), j) * ATT_SCALE)
            dk_nat[j // 2] = dk_nat[j // 2] + _keep_native(_mm(ds_full, qs), j)
            dv_nat[j // 2] = dv_nat[j // 2] + _keep_native(_mm(p_full, do_s), j)
        o = jnp.concatenate(o_parts, axis=1)
        dkk = jnp.concatenate(dk_nat, axis=1)
        dvv = jnp.concatenate(dv_nat, axis=1)
        out = dp_ref.dtype
        dp_ref[:, A_Q:A_Q + 1024] = _rope_transposed(jnp.concatenate(dq_parts, axis=1), tab).astype(out)
        dp_ref[:, A_K:A_K + 256] = _rope_transposed(dkk[CHUNK:2 * CHUNK] + carry_k[...], tab).astype(out)
        dp_ref[:, A_V:A_V + 256] = (dvv[CHUNK:2 * CHUNK] + carry_v[...]).astype(out)
        dp_ref[:, A_G:A_G + 1024] = (dyv * o * (sgm * (1.0 + g * (1.0 - sgm)))).astype(out)
        carry_k[...] = dkk[0:CHUNK]
        carry_v[...] = dvv[0:CHUNK]
        dsink_ref[...] += dsink

    rev = lambda i: (nb - 1 - i, 0)
    prev = lambda i: jnp.maximum(nb - 2 - i, 0)
    return _call(
        body, None, name="swa_bwd", grid=(nb,),
        in_specs=[pl.BlockSpec(memory_space=pltpu.SMEM),
                  pl.BlockSpec((CHUNK, A_W), rev),
                  pl.BlockSpec((CHUNK, 512), lambda i: (prev(i), 2)),
                  pl.BlockSpec((CHUNK, 384), rev),
                  pl.BlockSpec((CHUNK, 384), lambda i: (prev(i), 0)),
                  pl.BlockSpec((CHUNK, D_ATT), lambda i: (nb - 1 - i, 1))],
        out_specs=[pl.BlockSpec((CHUNK, A_W), rev), pl.BlockSpec((8, 128), lambda i: (0, 0))],
        out_shape=[jax.ShapeDtypeStruct((L, A_W), _MXU), jax.ShapeDtypeStruct((8, 128), F32)],
        scratch_shapes=[pltpu.VMEM((CHUNK, 256), F32), pltpu.VMEM((CHUNK, 256), F32)],
        args=(sinks, proj_att, proj_att, tabs, tabs, dy), reduce=reduce)


def _head(y, x, target, w_out, ln_g, ln_b, *, tm):
    L = x.shape[0]
    nsteps = L // tm

    def body(y_ref, x_ref, t_ref, wo_ref, g_ref, b_ref, dr_ref, dy_ref, acc_ref):
        i = pl.program_id(0)

        @pl.when(i == 0)
        def _():
            acc_ref[...] = jnp.zeros_like(acc_ref)

        r = ALPHA * x_ref[...] + _mm(y_ref[...], wo_ref[...])
        mu = jnp.mean(r, axis=-1, keepdims=True)
        d = r - mu
        rstd = lax.rsqrt(jnp.mean(d * d, axis=-1, keepdims=True) + LN_EPS)
        xh = d * rstd
        gam = g_ref[0:1, :]
        e = xh * gam + b_ref[0:1, :] - t_ref[...]
        dout = e * (1.0 / D_MODEL)
        dxh = dout * gam
        dr = rstd * (dxh - jnp.mean(dxh, axis=-1, keepdims=True)
                     - xh * jnp.mean(dxh * xh, axis=-1, keepdims=True))
        dr_ref[...] = dr
        dy_ref[...] = _mm_nt(dr, wo_ref[...])
        acc_ref[...] += _rows8([_colsum(dout * xh), _colsum(dout), _colsum(e * e) * (0.5 / D_MODEL)])

        @pl.when(i == nsteps - 1)
        def _():
            acc = acc_ref[...]
            tot = jnp.sum(acc[2:3, :])
            rid = lax.broadcasted_iota(jnp.int32, (8, 1024), 0)
            acc_ref[...] = jnp.where(rid == 3, tot, acc)

    const = lambda shape: pl.BlockSpec(shape, lambda i: (0, 0))
    row = lambda w: pl.BlockSpec((tm, w), lambda i: (i, 0))
    return pl.pallas_call(
        body, name="head", grid=(nsteps,),
        in_specs=[row(2048), row(1024), row(1024), const((2048, 1024)), const((1, 1024)), const((1, 1024))],
        out_specs=[row(1024), row(2048), const((8, 1024))],
        out_shape=[jax.ShapeDtypeStruct((L, D_MODEL), F32), jax.ShapeDtypeStruct((L, 2048), F32),
                   jax.ShapeDtypeStruct((8, 1024), F32)],
        compiler_params=_params(("arbitrary",)),
    )(y, x, target, w_out, ln_g, ln_b)


def _gather_w_in(w_shard, positions):
    R = w_shard.shape[0]
    halves = (pl.ds(0, R // 2), pl.ds(R // 2, R // 2))
    any_spec = pl.BlockSpec(memory_space=pl.ANY)
    vmem = pl.BlockSpec(memory_space=pltpu.VMEM)

    def body(in_ref, pos_ref, inv_ref, out_ref, tab_ref, send_sems, recv_sems, local_sem):
        x, y, c = _position()

        def slot(p, half=None):
            s = out_ref.at[_index(*p)]
            return s if half is None else s.at[halves[half]]

        def same_core(p):
            return (p[0], p[1], c)

        def other_core(p):
            return (p[0], p[1], 1 - c)

        me, xn, yn, dg = (x, y), (1 - x, y), (x, 1 - y), (1 - x, 1 - y)

        def copy(k, dst, to, src=None):
            return _remote(dst if src is None else src, dst, send_sems.at[k], recv_sems.at[k], to)

        local = pltpu.make_async_copy(in_ref, slot(same_core(me)), local_sem)
        local.start()
        own = [copy(0, slot(same_core(me)), other_core(me), in_ref), copy(1, slot(same_core(me)), same_core(xn), in_ref),
               copy(2, slot(same_core(me)), same_core(yn), in_ref)]
        for cp in own:
            cp.start()
        _rope_tables(pos_ref, inv_ref, tab_ref)
        copy(1, slot(same_core(xn)), same_core(xn)).wait_recv()
        passed = [copy(4, slot(same_core(xn), 1), same_core(yn)), copy(5, slot(same_core(xn)), other_core(me))]
        for cp in passed:
            cp.start()
        copy(2, slot(same_core(yn)), same_core(yn)).wait_recv()
        more = [copy(3, slot(same_core(yn), 0), same_core(xn)), copy(6, slot(same_core(yn)), other_core(me))]
        for cp in more:
            cp.start()
        passed += more
        for k, half in ((3, 0), (4, 1)):
            copy(k, slot(same_core(dg), half), same_core(xn)).wait_recv()
            fwd = copy(7 + half, slot(same_core(dg), half), other_core(me))
            fwd.start()
            passed.append(fwd)
        copy(0, slot(other_core(me)), other_core(me)).wait_recv()
        copy(5, slot(other_core(xn)), other_core(me)).wait_recv()
        copy(6, slot(other_core(yn)), other_core(me)).wait_recv()
        for half in (0, 1):
            copy(7 + half, slot(other_core(dg), half), other_core(me)).wait_recv()
        for cp in own + passed:
            cp.wait_send()
        local.wait()

    return pl.pallas_call(
        body, name="gather_w_in", in_specs=[any_spec, vmem, vmem], out_specs=[any_spec, vmem],
        out_shape=[jax.ShapeDtypeStruct((N_DEV,) + w_shard.shape, w_shard.dtype),
                   jax.ShapeDtypeStruct((positions.shape[0], 384), F32)],
        scratch_shapes=[pltpu.SemaphoreType.DMA((9,)), pltpu.SemaphoreType.DMA((9,)), pltpu.SemaphoreType.DMA],
        compiler_params=_params(),
    )(w_shard, positions, jnp.asarray(ROPE_INV)[None, :])


def _input_gradient(d_ssd, d_att, w_ssd, w_att, dr, *, tm, comm=None, reduce=None):
    L = dr.shape[0]

    def body(ds_ref, da_ref, ws_ref, wa_ref, dr_ref, o_ref):
        o_ref[...] = ALPHA * dr_ref[...] + _mm_nt(ds_ref[...], ws_ref[...]) + _mm_nt(da_ref[...], wa_ref[...])

    row = lambda w: pl.BlockSpec((tm, w), lambda i: (i, 0))
    resident = lambda a: pl.BlockSpec(a.shape, lambda i: (0, 0), pipeline_mode=pl.Buffered(1))
    return _call(body, comm, name="dx", grid=(L // tm,),
                 in_specs=[row(S_W), row(A_W), resident(w_ssd), resident(w_att), row(D_MODEL)],
                 out_specs=[row(D_MODEL)], out_shape=[jax.ShapeDtypeStruct((L, D_MODEL), F32)],
                 scratch_shapes=[], args=(d_ssd, d_att, w_ssd, w_att, dr), reduce=reduce)


SHARD_COLS = D_IN_PROJ // N_DEV
SPLIT = N_SSD_REAL - 4 * SHARD_COLS
RELAYOUT_ROWS = 256


def _unpack_w_in(w_all):
    def body(g_ref, ws_ref, wa_ref):
        for j in range(4):
            ws_ref[:, SHARD_COLS * j:SHARD_COLS * (j + 1)] = g_ref[j]
        ws_ref[:, 4 * SHARD_COLS:N_SSD_REAL] = g_ref[4, :, 0:SPLIT]
        ws_ref[:, N_SSD_REAL:S_W] = jnp.zeros((RELAYOUT_ROWS, S_W - N_SSD_REAL), ws_ref.dtype)
        wa_ref[:, 0:SHARD_COLS - SPLIT] = g_ref[4, :, SPLIT:SHARD_COLS]
        for j in range(5, N_DEV):
            lo = SHARD_COLS * (j - 4) - SPLIT
            wa_ref[:, lo:lo + SHARD_COLS] = g_ref[j]

    return pl.pallas_call(
        body, name="unpack_w_in", grid=(D_MODEL // RELAYOUT_ROWS,),
        in_specs=[pl.BlockSpec((N_DEV, RELAYOUT_ROWS, SHARD_COLS), lambda i: (0, i, 0))],
        out_specs=[pl.BlockSpec((RELAYOUT_ROWS, S_W), lambda i: (i, 0)), pl.BlockSpec((RELAYOUT_ROWS, A_W), lambda i: (i, 0))],
        out_shape=[jax.ShapeDtypeStruct((D_MODEL, S_W), w_all.dtype), jax.ShapeDtypeStruct((D_MODEL, A_W), w_all.dtype)],
        compiler_params=_params(("arbitrary",)),
    )(w_all)


def _dw_in(xb, d, half, tail=None, *, tl=1024):
    L, N = d.shape
    steps = L // tl

    def body(x_ref, d_ref, *refs):
        if half == 0:
            p_ref, tail_ref, acc = refs
        else:
            t_ref, p_ref, acc = refs
        l = pl.program_id(0)

        @pl.when(l == 0)
        def _():
            acc[...] = jnp.zeros_like(acc)

        acc[...] += _mm_tn(x_ref[...], d_ref[...])

        @pl.when(l == steps - 1)
        def _():
            if half == 0:
                tail_ref[...] = acc[:, S_DT:S_W]
            for j in range(4):
                if half == 0:
                    pieces = [(0, acc[:, SHARD_COLS * j:SHARD_COLS * (j + 1)])]
                elif j == 0:
                    pieces = [(0, t_ref[:, 4 * SHARD_COLS - S_DT:N_SSD_REAL - S_DT]), (SPLIT, acc[:, 0:SHARD_COLS - SPLIT])]
                else:
                    lo = SHARD_COLS * j - SPLIT
                    pieces = [(0, acc[:, lo:lo + SHARD_COLS])]
                for off, blk in pieces:
                    p_ref[j, :, off:off + blk.shape[1]] = blk

    once = pl.Buffered(1)
    whole = lambda shape: pl.BlockSpec(shape, lambda l: (0,) * len(shape), pipeline_mode=once)
    in_specs = [pl.BlockSpec((tl, D_MODEL), lambda l: (l, 0)), pl.BlockSpec((tl, N), lambda l: (l, 0))]
    args = [xb, d]
    out_shape = [jax.ShapeDtypeStruct((4, D_MODEL, SHARD_COLS), F32)]
    if half == 0:
        out_shape.append(jax.ShapeDtypeStruct((D_MODEL, S_W - S_DT), F32))
    else:
        in_specs.append(whole(tail.shape))
        args.append(tail)
    return pl.pallas_call(
        body, name="dw_in_%d" % half, grid=(steps,), in_specs=in_specs,
        out_specs=[whole(o.shape) for o in out_shape], out_shape=out_shape,
        scratch_shapes=[pltpu.VMEM((D_MODEL, N), F32)], compiler_params=_params(("arbitrary",)),
    )(*args)


def _adamw_math(w, g, m, v):
    m = ADAM_B1 * m + (1.0 - ADAM_B1) * g
    v = ADAM_B2 * v + (1.0 - ADAM_B2) * (g * g)
    m_hat = m / (1.0 - ADAM_B1 ** ADAM_STEP)
    v_hat = v / (1.0 - ADAM_B2 ** ADAM_STEP)
    delta = -ADAM_LR * (m_hat / (jnp.sqrt(v_hat) + ADAM_EPS) + ADAM_WD * w)
    return delta, m, v


def _adamw_shard(n_recv, g_own, recv, w, m, v, *, rows, name):
    R, C = g_own.shape

    def body(n_ref, g_ref, r_ref, w_ref, m_ref, v_ref, go_ref, d_ref, mo_ref, vo_ref):
        g = g_ref[...]
        for k in range(N_DEV - 1):
            g = g + jnp.where(k < n_ref[0], r_ref[k].astype(F32), 0.0)
        d, mn, vn = _adamw_math(w_ref[...], g, m_ref[...], v_ref[...])
        go_ref[...] = g
        d_ref[...] = d
        mo_ref[...] = mn
        vo_ref[...] = vn

    blk = pl.BlockSpec((rows, C), lambda i: (i, 0))
    return pl.pallas_call(
        body, name=name, grid=(R // rows,),
        in_specs=[pl.BlockSpec(memory_space=pltpu.SMEM), blk,
                  pl.BlockSpec((N_DEV - 1, rows, C), lambda i: (0, i, 0)), blk, blk, blk],
        out_specs=[blk] * 4, out_shape=[jax.ShapeDtypeStruct((R, C), F32)] * 4,
        compiler_params=_params(("arbitrary",)),
    )(n_recv, g_own, recv, w, m, v)


def _minor_rows_view(a):
    return jnp.transpose(a, (2, 0, 1)).reshape(SHARD_COLS * 8, 128)


def _from_minor_rows_view(v):
    return jnp.transpose(v.reshape(SHARD_COLS, 8, 128), (1, 2, 0)).reshape(1, D_MODEL, SHARD_COLS)


def _adamw_w_in(is_lo, own_lo, own_hi, recv_lo, recv_hi, w, m, v):
    C = SHARD_COLS
    pad = -C % 128

    def body(lo_ref, ol_ref, oh_ref, rl_ref, rh_ref, w_ref, m_ref, v_ref, go_ref, d_ref, mo_ref, vo_ref):
        lo = lo_ref[0] == 1
        for q in range(D_MODEL // 128):
            band = pl.ds(q * 128, 128)
            g = jnp.where(lo, ol_ref[band, :], oh_ref[band, :])
            for k in range(2):
                g = g + jnp.where(lo, rl_ref[k, band, :], rh_ref[k, band, :]).astype(F32)
            g = jnp.pad(g, ((0, 0), (0, pad))).T[0:C]
            rows = pl.ds(q, C, stride=8)
            d, mn, vn = _adamw_math(w_ref[rows, :], g, m_ref[rows, :], v_ref[rows, :])
            go_ref[rows, :] = g
            d_ref[rows, :] = d
            mo_ref[rows, :] = mn
            vo_ref[rows, :] = vn

    return pl.pallas_call(
        body, name="adamw_w_in", out_shape=[jax.ShapeDtypeStruct(w.shape, F32)] * 4,
        in_specs=[pl.BlockSpec(memory_space=pltpu.SMEM)] + [pl.BlockSpec(memory_space=pltpu.VMEM)] * 7,
        out_specs=[pl.BlockSpec(memory_space=pltpu.VMEM)] * 4,
        compiler_params=_params(),
    )(is_lo, own_lo, own_hi, recv_lo, recv_hi, w, m, v)


SMALL = ("conv_b", "dt_bias", "a_log", "d_skip", "ssd_norm_w", "attn_sinks", "ln_g", "ln_b")


def _adamw_small(gathered, params):
    n_p = len(SMALL)

    def body(*refs):
        acc = []
        for r in refs[:5]:
            t = r[0]
            for k in range(1, N_DEV):
                t = t + r[k]
            acc.append(t)
        head, conv, norm, scal, sink = acc
        grads = dict(conv_b=conv[4:5, :], dt_bias=scal[0:1, 0:N_HEADS], a_log=scal[1:2, 0:N_HEADS],
                     d_skip=scal[2:3, 0:N_HEADS], ssd_norm_w=norm[0:1, :], attn_sinks=sink[0:1, 0:N_HEADS],
                     ln_g=head[0:1, :], ln_b=head[1:2, :])
        wmv = refs[5:5 + 3 * n_p]
        outs = refs[5 + 3 * n_p:]
        outs[0][...] = head[3:4, 0:1]
        outs[1][...] = conv[0:4, :]
        for i, name in enumerate(SMALL):
            w_ref, m_ref, v_ref = wmv[3 * i:3 * i + 3]
            g = grads[name]
            d, mn, vn = _adamw_math(w_ref[...], g, m_ref[...], v_ref[...])
            for o_ref, val in zip(outs[2 + 4 * i:6 + 4 * i], (g, d, mn, vn)):
                o_ref[...] = val

    flat = [a for name in SMALL for a in params[name]]
    out_shape = [jax.ShapeDtypeStruct((1, 1), F32), jax.ShapeDtypeStruct((4, D_XBC), F32)]
    for name in SMALL:
        out_shape += [jax.ShapeDtypeStruct(params[name][0].shape, F32)] * 4
    res = pl.pallas_call(body, name="adamw_small", out_shape=out_shape, compiler_params=_params())(*gathered, *flat)
    return res[0], res[1], {name: res[2 + 4 * i:6 + 4 * i] for i, name in enumerate(SMALL)}


def _adamw_plain(g, w, m, v):
    def body(g_ref, w_ref, m_ref, v_ref, d_ref, mo_ref, vo_ref):
        d, mn, vn = _adamw_math(w_ref[...], g_ref[...], m_ref[...], v_ref[...])
        d_ref[...] = d
        mo_ref[...] = mn
        vo_ref[...] = vn

    return pl.pallas_call(
        body, name="adamw_conv_w", out_shape=[jax.ShapeDtypeStruct(w.shape, F32)] * 3,
        compiler_params=_params(),
    )(g, w, m, v)


def _lane_pattern(fn):
    return np.asarray([fn(l % HEAD_DIM) for l in range(128)], np.float32)


ROPE_INV = _lane_pattern(lambda r: ROPE_THETA ** (-2.0 * (r % 8) / ROPE_DIM) if r < ROPE_DIM else 0.0)


def _rope_tables(pos_ref, inv_ref, tab_ref):
    lane = lax.broadcasted_iota(jnp.int32, (1, 128), 1) % HEAD_DIM
    upper = jnp.where((lane >= ROPE_DIM // 2) & (lane < ROPE_DIM), 1.0, 0.0)
    lower = jnp.where(lane < ROPE_DIM // 2, -1.0, 0.0)

    def block(r, carry):
        rows = pl.ds(pl.multiple_of(r * CHUNK, CHUNK), CHUNK)
        ang = pos_ref[rows, :].astype(F32) * inv_ref[...]
        sn = jnp.sin(ang)
        tab_ref[rows, 0:128] = jnp.cos(ang)
        tab_ref[rows, 128:256] = sn * upper
        tab_ref[rows, 256:384] = sn * lower
        return carry

    lax.fori_loop(0, pos_ref.shape[0] // CHUNK, block, 0)


def _expansion():
    E = np.arange(1024)[None, :] // HEAD_DIM == np.arange(128)[:, None]
    return jnp.asarray(E, BF16), jnp.asarray(E.T, BF16)


def _ssd_args(conv_w, conv_b, dt_bias, a_log, d_skip, norm_w, E):
    return (conv_w, conv_b, dt_bias.reshape(-1), a_log.reshape(-1), d_skip.reshape(-1), norm_w, E)


def kernel(x, positions, w_in, conv_w, conv_b, dt_bias, a_log, d_skip, ssd_norm_w, attn_sinks, w_out, ln_g, ln_b, loss_target, m_w_in, m_conv_w, m_conv_b, m_dt_bias, m_a_log, m_d_skip, m_ssd_norm_w, m_attn_sinks, m_w_out, m_ln_g, m_ln_b, v_w_in, v_conv_w, v_conv_b, v_dt_bias, v_a_log, v_d_skip, v_ssd_norm_w, v_attn_sinks, v_w_out, v_ln_g, v_ln_b):
    me = _index(*_position())
    x0, target = x[0], loss_target[0]
    bf16_shard = lambda shape: jax.ShapeDtypeStruct(shape, BF16)
    E, ET = _expansion()
    sinks = attn_sinks.reshape(-1)

    w_all, tabs = _gather_w_in(w_in[0].astype(BF16), positions[0][:, None])
    w_ssd, w_att = _unpack_w_in(w_all)
    gather_conv_w = _Hosted([conv_w[0]], [jax.ShapeDtypeStruct((N_DEV,) + conv_w.shape[1:], F32)],
                            [_Flow("gather", 0, 0)])

    proj_ssd, proj_att, xb, conv_w_all = _in_proj(x0, w_ssd, w_att, tm=512, comm=gather_conv_w)
    conv_w_f = jnp.transpose(conv_w_all, (1, 0, 2)).reshape(4, D_XBC)
    ssd_args = _ssd_args(conv_w_f, conv_b, dt_bias, a_log, d_skip, ssd_norm_w, E)
    gather_w_out = _Hosted([w_out[0].astype(BF16)], [bf16_shard((N_DEV, 256, D_MODEL))], [_Flow("gather", 0, 0)])
    y, ypre, hprev, pre, w_out_all = _mixer_forward(proj_ssd, proj_att, tabs, sinks, *ssd_args, comm=gather_w_out)
    w_out_f = w_out_all.reshape(2 * D_MODEL, D_MODEL)
    dr, dy, acc_head = _head(y, x0, target, w_out_f, ln_g, ln_b, tm=512)

    dw_out, dw_out_bf16 = _matmul_tn(y, dr, tl=1024, tn=D_MODEL, name="dw_out", emit_bf16=True)
    own_out = lax.dynamic_index_in_dim(dw_out.reshape(N_DEV, 256, D_MODEL), me, axis=0, keepdims=False)
    send_out = _Hosted([dw_out_bf16.reshape(N_DEV, 256, D_MODEL)], [bf16_shard((N_DEV - 1, 256, D_MODEL))],
                       [_Flow("exchange", 0, 0)])
    d_ssd, acc_cw, acc_w, acc_s, recv_out = _ssd_backward(proj_ssd, hprev, ypre, pre, dy, *ssd_args, ET, comm=send_out)
    stack_lo, dw_dt_block = _dw_in(xb, d_ssd, 0)
    d_att, dsink, own_lo, recv_lo = _swa_backward(proj_att, tabs, sinks, dy, reduce=_OwnerReduce(stack_lo, 0))
    (stack_hi,) = _dw_in(xb, d_att, 1, dw_dt_block)
    accs = [acc_head, acc_cw, acc_w, acc_s, dsink]
    gather_accs = _Hosted(accs, [jax.ShapeDtypeStruct((N_DEV,) + a.shape, F32) for a in accs],
                          [_Flow("gather", i, i) for i in range(5)])
    dx, *gathered, own_hi, recv_hi = _input_gradient(d_ssd, d_att, w_ssd, w_att, dr, tm=256, comm=gather_accs,
                                                     reduce=_OwnerReduce(stack_hi, 1))
    is_lo = (me < 4).reshape(1).astype(jnp.int32)
    n_recv_out = jnp.full((1,), N_DEV - 1, jnp.int32)

    g_in, d_in, nm_in, nv_in = [_from_minor_rows_view(r) for r in _adamw_w_in(
        is_lo, own_lo, own_hi, recv_lo, recv_hi, _minor_rows_view(w_in), _minor_rows_view(m_w_in), _minor_rows_view(v_w_in))]
    g_out, d_out, nm_out, nv_out = _adamw_shard(n_recv_out, own_out, recv_out, w_out[0], m_w_out[0], v_w_out[0],
                                                rows=256, name="adamw_w_out")
    loss, g_conv_w, small = _adamw_small(gathered, dict(
        conv_b=(conv_b, m_conv_b, v_conv_b), dt_bias=(dt_bias, m_dt_bias, v_dt_bias), a_log=(a_log, m_a_log, v_a_log),
        d_skip=(d_skip, m_d_skip, v_d_skip), ssd_norm_w=(ssd_norm_w, m_ssd_norm_w, v_ssd_norm_w),
        attn_sinks=(attn_sinks, m_attn_sinks, v_attn_sinks), ln_g=(ln_g, m_ln_g, v_ln_g), ln_b=(ln_b, m_ln_b, v_ln_b)))
    g_cw = lax.dynamic_slice_in_dim(g_conv_w, me * (D_XBC // N_DEV), D_XBC // N_DEV, axis=1)
    d_cw, nm_cw, nv_cw = _adamw_plain(g_cw, conv_w[0], m_conv_w[0], v_conv_w[0])

    def leaves(i, big_in, cw, big_out):
        mid = [small[k][i] for k in ("conv_b", "dt_bias", "a_log", "d_skip", "ssd_norm_w", "attn_sinks")]
        return [big_in, cw[None]] + mid + [big_out[None], small["ln_g"][i], small["ln_b"][i]]

    return (loss.reshape(()), dx[None], *leaves(0, g_in, g_cw, g_out), *leaves(1, d_in, d_cw, d_out),
            *leaves(2, nm_in, nm_cw, nm_out), *leaves(3, nv_in, nv_cw, nv_out))
```

```python
import jax
import jax.numpy as jnp
from jax import lax
from jax.experimental import pallas as pl
from jax.experimental.pallas import tpu as pltpu
import numpy as np

F32 = jnp.float32
BF16 = jnp.bfloat16
_MXU = jnp.bfloat16

N_DEV = 8
D_MODEL = 1024
D_SSD = 1024
D_ATT = 1024
HEAD_DIM = 64
N_HEADS = 16
SSD_GROUPS = 2
KV_HEADS = 4
CHUNK = 128
D_XBC = 1536
D_IN_PROJ = 5136
ROPE_DIM = 16
ROPE_THETA = 500000.0
ALPHA = (2.0 * 1) ** 0.25
LN_EPS = 1e-5
RMS_EPS = 1e-5
ATT_SCALE = HEAD_DIM ** -0.5
NEG = -1e30

S_Z, S_XS, S_B, S_C, S_DT, S_W = 0, 1024, 2048, 2304, 2560, 2816
N_SSD_REAL = 2576
A_Q, A_K, A_V, A_G, A_W = 0, 1024, 1280, 1536, 2560

ADAM_LR = 0.001
ADAM_B1 = 0.9
ADAM_B2 = 0.999
ADAM_EPS = 1e-08
ADAM_WD = 0.01
ADAM_STEP = 10

VMEM_LIMIT = 48 * 1024 * 1024
MESH = pl.DeviceIdType.MESH


def _params(sem=None):
    return pltpu.CompilerParams(dimension_semantics=sem, vmem_limit_bytes=VMEM_LIMIT)


def _mm(a, b):
    return jnp.dot(a.astype(_MXU), b.astype(_MXU), preferred_element_type=F32)


def _mm_nt(a, b):
    return lax.dot_general(a.astype(_MXU), b.astype(_MXU), (((1,), (1,)), ((), ())),
                           preferred_element_type=F32)


def _mm_tn(a, b):
    return lax.dot_general(a.astype(_MXU), b.astype(_MXU), (((0,), (0,)), ((), ())),
                           preferred_element_type=F32)


def _split3(v):
    hi = v.astype(BF16)
    r = v - hi.astype(F32)
    mid = r.astype(BF16)
    lo = (r - mid.astype(F32)).astype(BF16)
    return hi, mid, lo


def _mm_exact_r(v, p01):
    hi, mid, lo = _split3(v)
    d = lambda a: jnp.dot(a, p01, preferred_element_type=F32)
    return d(hi) + d(mid) + d(lo)


def _mm_exact_l(p01, v):
    hi, mid, lo = _split3(v)
    d = lambda a: jnp.dot(p01, a, preferred_element_type=F32)
    return d(hi) + d(mid) + d(lo)


def _mm_2pass_r(v, p01):
    hi = v.astype(BF16)
    lo = (v - hi.astype(F32)).astype(BF16)
    return jnp.dot(hi, p01, preferred_element_type=F32) + jnp.dot(lo, p01, preferred_element_type=F32)


def _sigmoid(x):
    return 1.0 / (1.0 + jnp.exp(-x))


def _softplus(x):
    e = jnp.exp(-jnp.abs(x))
    u = 1.0 + e
    log1p = jnp.where(u == 1.0, e, jnp.log(u) * (e / (u - 1.0)))
    return jnp.maximum(x, 0.0) + log1p


def _rows8(rows):
    n = rows[0].shape[1]
    rid = lax.broadcasted_iota(jnp.int32, (8, n), 0)
    out = jnp.zeros((8, n), F32)
    for k, r in enumerate(rows):
        out = out + jnp.where(rid == k, r, 0.0)
    return out


def _colsum(a):
    return jnp.sum(a, axis=0, keepdims=True)


def _in_proj(x, w_ssd, w_att, *, tm, comm=None):
    L, K = x.shape

    def body(x_ref, ws_ref, wa_ref, ps_ref, pa_ref, xb_ref):
        xb = x_ref[...].astype(_MXU)
        xb_ref[...] = xb
        ps_ref[...] = jnp.dot(xb, ws_ref[...], preferred_element_type=F32)
        pa_ref[...] = jnp.dot(xb, wa_ref[...], preferred_element_type=F32)

    row = lambda w: pl.BlockSpec((tm, w), lambda i: (i, 0))
    resident = lambda a: pl.BlockSpec(a.shape, lambda i: (0, 0), pipeline_mode=pl.Buffered(1))
    return _call(
        body, comm, name="in_proj", grid=(L // tm,),
        in_specs=[row(K), resident(w_ssd), resident(w_att)], out_specs=[row(S_W), row(A_W), row(K)],
        out_shape=[jax.ShapeDtypeStruct((L, S_W), F32), jax.ShapeDtypeStruct((L, A_W), F32),
                   jax.ShapeDtypeStruct((L, K), _MXU)],
        scratch_shapes=[], args=(x, w_ssd, w_att))


def _matmul_tn(a, g, *, tl, tn, name, emit_bf16=False):
    L, M = a.shape
    N = g.shape[1]
    last = L // tl - 1

    def body(a_ref, g_ref, o_ref, *rest):
        @pl.when(pl.program_id(1) == 0)
        def _():
            o_ref[...] = jnp.zeros_like(o_ref)

        o_ref[...] += _mm_tn(a_ref[...], g_ref[...])
        if emit_bf16:
            @pl.when(pl.program_id(1) == last)
            def _():
                rest[0][...] = o_ref[...].astype(BF16)

    spec = pl.BlockSpec((M, tn), lambda j, l: (0, j))
    res = pl.pallas_call(
        body, name=name, grid=(N // tn, L // tl),
        in_specs=[pl.BlockSpec((tl, M), lambda j, l: (l, 0)), pl.BlockSpec((tl, tn), lambda j, l: (l, j))],
        out_specs=[spec, spec] if emit_bf16 else [spec],
        out_shape=[jax.ShapeDtypeStruct((M, N), F32)] + ([jax.ShapeDtypeStruct((M, N), BF16)] if emit_bf16 else []),
        compiler_params=_params(("arbitrary", "arbitrary")),
    )(a, g)
    return res if emit_bf16 else res[0]


def _position():
    return lax.axis_index("x"), lax.axis_index("y"), lax.axis_index("c")


def _index(px, py, pc):
    return 4 * px + 2 * py + pc


def _flip(pos, k):
    x, y, c = pos
    return ((1 - x) if (k >> 2) & 1 else x, (1 - y) if (k >> 1) & 1 else y, (1 - c) if k & 1 else c)


def _remote(src, dst, send_sem, recv_sem, peer):
    return pltpu.make_async_remote_copy(src_ref=src, dst_ref=dst, send_sem=send_sem, recv_sem=recv_sem,
                                        device_id=peer, device_id_type=MESH)


class _Flow:
    def __init__(self, kind, operand, result):
        self.kind, self.operand, self.result = kind, operand, result


class _Hosted:
    def __init__(self, operands, out_shapes, flows):
        self.operands, self.out_shapes, self.flows = operands, out_shapes, flows

    def plan(self, ins, outs, send_sems, recv_sems, local_sems):
        me = _position()
        mi = _index(*me)
        sends, recvs, locals_ = [], [], []
        for row, f in enumerate(self.flows):
            src, dst = ins[f.operand], outs[f.result]
            for k in range(1, N_DEV):
                peer = _flip(me, k)
                sems = (send_sems.at[row, k - 1], recv_sems.at[row, k - 1])
                if f.kind == "exchange":
                    sends.append(_remote(src.at[_index(*peer)], dst.at[k - 1], *sems, peer))
                    recvs.append(sends[-1])
                else:
                    sends.append(_remote(src, dst.at[mi], *sems, peer))
                    recvs.append(_remote(src, dst.at[_index(*peer)], *sems, peer))
            if f.kind == "gather":
                locals_.append(pltpu.make_async_copy(src, dst.at[mi], local_sems.at[row]))

        def start():
            for cp in locals_ + sends:
                cp.start()

        def wait():
            for cp in recvs:
                cp.wait_recv()
            for cp in sends:
                cp.wait_send()
            for cp in locals_:
                cp.wait()

        return start, wait


class _OwnerReduce:
    FIRST_STEP, SECOND_STEP, SEND_STEPS, REDUCE_STEPS = 2, 4, (2, 4, 6, 8), (5, 8, 10, 13)

    def __init__(self, stack, target_x):
        self.stack, self.target_x = stack, target_x
        block = stack.shape[1:]
        self.chunks = len(self.SEND_STEPS)
        self.chunk_rows = block[0] // self.chunks
        self.out_shapes = [jax.ShapeDtypeStruct(block, F32), jax.ShapeDtypeStruct((2,) + block, BF16)]
        self.out_specs = [pl.BlockSpec(block, lambda *_: (0, 0), pipeline_mode=pl.Buffered(1)),
                          pl.BlockSpec(memory_space=pl.ANY)]
        dma = pltpu.SemaphoreType.DMA
        self.scratch_shapes = ([pltpu.VMEM((2,) + block, F32)] * 2 + [pltpu.VMEM(block, BF16)] * 3
                               + [dma((self.chunks,))] * 4 + [dma((2,))] * 3)

    def plan(self, i, steps, stack_ref, own_ref, recv_ref, scratch):
        assert self.FIRST_STEP <= self.SEND_STEPS[0] and self.SECOND_STEP < self.REDUCE_STEPS[0] < steps - 1
        (theirs_scr, mine_scr, first_scr, across_scr, out_scr, y_send_sems, y_recv_sems, x_send_sems, x_recv_sems,
         swap_send_sems, swap_recv_sems, mine_sems) = scratch
        x, y, c = _position()
        owners_side = x == self.target_x
        other_side = x != self.target_x
        sibling, across, owner = (x, y, 1 - c), (x, 1 - y, c), (self.target_x, y, c)
        order = (1 - y, y)
        swaps = [_remote(stack_ref.at[2 * order[j] + (1 - c)], theirs_scr.at[j], swap_send_sems.at[j], swap_recv_sems.at[j],
                         sibling) for j in range(2)]
        mine = [pltpu.make_async_copy(stack_ref.at[2 * order[j] + c], mine_scr.at[j], mine_sems.at[j]) for j in range(2)]
        chunks = range(self.chunks)
        part = [pl.ds(j * self.chunk_rows, self.chunk_rows) for j in chunks]
        y_sems = lambda j: (y_send_sems.at[j], y_recv_sems.at[j])
        to_neighbour = [_remote(first_scr.at[part[j]], across_scr.at[part[j]], *y_sems(j), across) for j in chunks]
        to_owner_y = [_remote(first_scr.at[part[j]], recv_ref.at[0, part[j]], *y_sems(j), across) for j in chunks]
        to_owner_x = [_remote(out_scr.at[part[j]], recv_ref.at[1, part[j]], x_send_sems.at[j], x_recv_sems.at[j], owner)
                      for j in chunks]

        def before():
            @pl.when(i == 0)
            def _():
                for cp in [swaps[0]] + mine:
                    cp.start()

            pl.when(i == 1)(swaps[1].start)

        def after():
            @pl.when(i == self.FIRST_STEP)
            def _():
                swaps[0].wait_recv()
                mine[0].wait()
                first_scr[...] = (mine_scr[0] + theirs_scr[0]).astype(first_scr.dtype)

            for j in chunks:
                pl.when((i == self.SEND_STEPS[j]) & other_side)(to_neighbour[j].start)
                pl.when((i == self.SEND_STEPS[j]) & owners_side)(to_owner_y[j].start)

            @pl.when(i == self.SECOND_STEP)
            def _():
                swaps[1].wait_recv()
                mine[1].wait()
                t = mine_scr[1] + theirs_scr[1]
                own_ref[...] = t
                mine_scr[1] = t

            for j in chunks:
                @pl.when((i == self.REDUCE_STEPS[j]) & other_side)
                def _(j=j):
                    to_neighbour[j].wait_recv()
                    t = mine_scr[1, part[j], :] + across_scr[part[j], :].astype(F32)
                    out_scr[part[j], :] = t.astype(out_scr.dtype)
                    to_owner_x[j].start()

            @pl.when(i == steps - 1)
            def _():
                for cp in swaps:
                    cp.wait_send()
                for j in chunks:
                    @pl.when(other_side)
                    def _(j=j):
                        to_neighbour[j].wait_send()
                        to_owner_x[j].wait_send()

                    @pl.when(owners_side)
                    def _(j=j):
                        to_owner_y[j].wait_send()
                        to_owner_y[j].wait_recv()
                        to_owner_x[j].wait_recv()

        return before, after


def _call(body, comm, *, name, grid, in_specs, out_specs, out_shape, scratch_shapes, args, reduce=None):
    semantics = ("arbitrary",) * len(grid)
    if comm is None and reduce is None:
        return pl.pallas_call(body, name=name, grid=grid, in_specs=in_specs, out_specs=out_specs, out_shape=out_shape,
                              scratch_shapes=scratch_shapes, compiler_params=_params(semantics))(*args)
    n_in, n_out, n_scr = len(args), len(out_shape), len(scratch_shapes)
    c_operands, c_shapes, flows = (comm.operands, comm.out_shapes, comm.flows) if comm else ([], [], [])
    c_in, c_out, rows = len(c_operands), len(c_shapes), max(len(flows), 1)
    r_in = 0 if reduce is None else 1

    def hosted(*refs):
        ins, refs = refs[:n_in], refs[n_in:]
        cins, refs = refs[:c_in], refs[c_in:]
        rins, refs = refs[:r_in], refs[r_in:]
        outs, refs = refs[:n_out], refs[n_out:]
        couts, refs = refs[:c_out], refs[c_out:]
        routs, refs = refs[:2 * r_in], refs[2 * r_in:]
        scr, refs = refs[:n_scr], refs[n_scr:]
        (send_sems, recv_sems, local_sems), r_scr = refs[:3], refs[3:]
        ids = [pl.program_id(d) for d in range(len(grid))]
        first, last = ids[0] == 0, ids[0] == grid[0] - 1
        for d in range(1, len(grid)):
            first, last = first & (ids[d] == 0), last & (ids[d] == grid[d] - 1)
        before = after = lambda: None
        if reduce is not None:
            before, after = reduce.plan(ids[0], grid[0], rins[0], *routs, r_scr)
        if comm is not None:
            start, wait = comm.plan(cins, couts, send_sems, recv_sems, local_sems)
            pl.when(first)(start)
        before()
        body(*ins, *outs, *scr)
        after()
        if comm is not None:
            pl.when(last)(wait)

    any_spec = pl.BlockSpec(memory_space=pl.ANY)
    sems = [pltpu.SemaphoreType.DMA((rows, N_DEV - 1)), pltpu.SemaphoreType.DMA((rows, N_DEV - 1)),
            pltpu.SemaphoreType.DMA((rows,))]
    r_operands, r_specs, r_shapes, r_scratch = ([reduce.stack], reduce.out_specs, reduce.out_shapes,
                                                reduce.scratch_shapes) if reduce else ([], [], [], [])
    return pl.pallas_call(
        hosted, name=name, grid=grid, in_specs=list(in_specs) + [any_spec] * (c_in + r_in),
        out_specs=list(out_specs) + [any_spec] * c_out + r_specs, out_shape=list(out_shape) + list(c_shapes) + r_shapes,
        scratch_shapes=list(scratch_shapes) + sems + r_scratch,
        compiler_params=_params(semantics))(*args, *c_operands, *r_operands)


def _head_row(ref, width, rep):
    hid = lax.broadcasted_iota(jnp.int32, (1, width), 1) // rep
    row = jnp.zeros((1, width), F32)
    for h in range(N_HEADS):
        row = jnp.where(hid == h, ref[h], row)
    return row


def _rows_from_above(u_b, s, ext_scr, row, col):
    down = (row - col == s).astype(_MXU)
    return jnp.concatenate([ext_scr[8 - s:16 - s, :], jnp.dot(down, u_b, preferred_element_type=F32)[8:128]], axis=0)


def _ssd_recompute(first, p_ref, halo_ref, cw_ref, cb_ref, dtb_ref, alog_ref, e_ref, ext_scr, pre=None):
    row = lax.broadcasted_iota(jnp.int32, (128, 128), 0)
    col = lax.broadcasted_iota(jnp.int32, (128, 128), 1)
    ext_scr[0:8, :] = jnp.where(first, 0.0, halo_ref[:, S_XS:S_DT])
    if pre is not None:
        ext_scr[8:16, :] = p_ref[0:8, S_XS:S_DT]
    else:
        ext_scr[8:136, :] = p_ref[:, S_XS:S_DT]
        cw = cw_ref[...]
        pre = (cb_ref[0:1, :] + cw[3:4, :] * ext_scr[8:136, :] + cw[2:3, :] * ext_scr[7:135, :]
               + cw[1:2, :] * ext_scr[6:134, :] + cw[0:1, :] * ext_scr[5:133, :])
    sg = _sigmoid(pre)
    act = pre * sg
    lane = lax.broadcasted_iota(jnp.int32, (1, 128), 1)
    A = jnp.where(lane < N_HEADS, -jnp.exp(_head_row(alog_ref, 128, 1)), 0.0)
    raw = p_ref[:, S_DT:S_DT + 128] + _head_row(dtb_ref, 128, 1)
    dt = _softplus(raw)
    dA = dt * A
    tril = (row >= col).astype(BF16)
    acs = _mm_exact_l(tril, dA)
    last = acs[127:128, :]
    ds = jnp.exp(last - acs)
    eo = jnp.exp(acs)
    E = e_ref[...]
    ex = _mm_2pass_r(jnp.concatenate([dt, ds, eo], axis=0), E)
    dt_e, ds_e, eo_e = ex[0:128], ex[128:256], ex[256:384]
    xs_c = act[:, 0:1024]
    X = xs_c * dt_e
    return dict(pre=pre, sg=sg, xs_c=xs_c, Bc=act[:, 1024:1280], Cc=act[:, 1280:1536], A=A, raw=raw, dt=dt,
                acs=acs, acsT=acs.T, eo_e=eo_e, ds_e=ds_e, dt_e=dt_e, cd_e=eo_e[127:128, :],
                X=X, Xd=X * ds_e, row=row, col=col)


def _split_halves(t):
    lo = _lo_half(CHUNK)
    return jnp.concatenate([jnp.where(lo, t, 0.0), jnp.where(lo, 0.0, t)], axis=0)


def _ssd_core(R, hprev):
    causal = R["row"] >= R["col"]
    acs, acsT, X = R["acs"], R["acsT"], R["X"]
    ydiag, yoff, snew = [], [], []
    for g in range(SSD_GROUPS):
        Bg = R["Bc"][:, g * 128:(g + 1) * 128]
        Cg = R["Cc"][:, g * 128:(g + 1) * 128]
        cols = slice(g * 512, (g + 1) * 512)
        CB = _mm_nt(Cg, Bg)
        snew.append(_mm_tn(Bg, R["Xd"][:, cols]))
        yoff.append(_mm(Cg, hprev[:, cols]))
        for j in range(4):
            h0 = g * 8 + 2 * j
            ms = [CB * jnp.exp(jnp.where(causal, acs[:, h:h + 1] - acsT[h:h + 1, :], NEG)) for h in (h0, h0 + 1)]
            ydiag.append(_mm(jnp.concatenate(ms, axis=1), _split_halves(X[:, h0 * HEAD_DIM:h0 * HEAD_DIM + 128])))
    Y = jnp.concatenate(ydiag, axis=1) + jnp.concatenate(yoff, axis=1) * R["eo_e"]
    return Y, jnp.concatenate(snew, axis=1)


def _ssd_forward_step(p_ref, halo_ref, cw_ref, cb_ref, dtb_ref, alog_ref, dsk_ref, nw_ref, e_ref,
                      y_ref, ypre_ref, hprev_ref, pre_ref, h_scr, ext_scr):
    c = pl.program_id(0)
    first = c == 0

    @pl.when(first)
    def _():
        h_scr[...] = jnp.zeros_like(h_scr)

    R = _ssd_recompute(first, p_ref, halo_ref, cw_ref, cb_ref, dtb_ref, alog_ref, e_ref, ext_scr)
    hprev = h_scr[...]
    hprev_ref[...] = hprev
    pre_ref[...] = R["pre"]
    Y, snew = _ssd_core(R, hprev)
    h_scr[...] = hprev * R["cd_e"] + snew
    Y = Y + _head_row(dsk_ref, D_SSD, HEAD_DIM) * R["xs_c"]
    ypre_ref[...] = Y
    z = p_ref[:, S_Z:S_Z + 1024]
    yf = Y * (z * _sigmoid(z))
    outs = []
    for g in range(SSD_GROUPS):
        yg = yf[:, g * 512:(g + 1) * 512]
        r = lax.rsqrt(jnp.mean(yg * yg, axis=-1, keepdims=True) + RMS_EPS)
        outs.append(yg * r)
    y_ref[:, 0:D_SSD] = (jnp.concatenate(outs, axis=1) * nw_ref[0:1, :]).astype(y_ref.dtype)


def _ssd_backward(proj_ssd, hprev_all, ypre, pre, dy, conv_w, conv_b, dt_bias, a_log, d_skip, norm_w, E, ET, comm=None):
    L = proj_ssd.shape[0]
    nc = L // CHUNK

    def body(p_ref, halo_ref, hprev_ref, ypre_ref, pre_ref, dy_ref, cw_ref, cb_ref, dtb_ref, alog_ref, dsk_ref, nw_ref, e_ref,
             et_ref, dp_ref, acc_cw_ref, acc_w_ref, acc_s_ref, dh_scr, ext_scr, ext2_scr, nxt_scr):
        i = pl.program_id(0)
        c = nc - 1 - i
        first = c == 0

        @pl.when(i == 0)
        def _():
            dh_scr[...] = jnp.zeros_like(dh_scr)
            nxt_scr[...] = jnp.zeros_like(nxt_scr)
            acc_cw_ref[...] = jnp.zeros_like(acc_cw_ref)
            acc_w_ref[...] = jnp.zeros_like(acc_w_ref)
            acc_s_ref[...] = jnp.zeros_like(acc_s_ref)

        R = _ssd_recompute(first, p_ref, halo_ref, cw_ref, cb_ref, dtb_ref, alog_ref, e_ref, ext_scr, pre_ref[...])
        hprev = hprev_ref[...]
        xs_c, X, Xd = R["xs_c"], R["X"], R["Xd"]
        acs, acsT = R["acs"], R["acsT"]
        ET = et_ref[...]
        dsk = _head_row(dsk_ref, D_SSD, HEAD_DIM)
        Y = ypre_ref[...]

        z = p_ref[:, S_Z:S_Z + 1024]
        sz = _sigmoid(z)
        silz = z * sz
        yf = Y * silz
        dyv = dy_ref[...]
        nw = nw_ref[0:1, :]
        dyf_parts, dnw_parts = [], []
        for g in range(SSD_GROUPS):
            cols = slice(g * 512, (g + 1) * 512)
            yg = yf[:, cols]
            r = lax.rsqrt(jnp.mean(yg * yg, axis=-1, keepdims=True) + RMS_EPS)
            yn = yg * r
            dyn = dyv[:, cols] * nw[:, cols]
            dnw_parts.append(_colsum(dyv[:, cols] * yn))
            dyf_parts.append(r * (dyn - yn * jnp.mean(dyn * yn, axis=-1, keepdims=True)))
        dyf = jnp.concatenate(dyf_parts, axis=1)
        dY = dyf * silz
        dz = dyf * Y * (sz * (1.0 + z * (1.0 - sz)))

        dhn = dh_scr[...]
        dYo = dY * R["eo_e"]
        causal = R["row"] >= R["col"]
        dacs = jnp.zeros((128, 128), F32)
        dacs_t = jnp.zeros((128, 128), F32)
        dxdiag, dxd, dhprev, dBs, dCs, yoff = [], [], [], [], [], []
        for g in range(SSD_GROUPS):
            Bg = R["Bc"][:, g * 128:(g + 1) * 128]
            Cg = R["Cc"][:, g * 128:(g + 1) * 128]
            cols = slice(g * 512, (g + 1) * 512)
            CB = _mm_nt(Cg, Bg)
            dCB = jnp.zeros((128, 128), F32)
            for j in range(4):
                h0 = g * 8 + 2 * j
                pc = slice(h0 * HEAD_DIM, h0 * HEAD_DIM + 128)
                dYst = _split_halves(dY[:, pc])
                dMst = _mm_nt(dYst, X[:, pc])
                mts = []
                for a, h in enumerate((h0, h0 + 1)):
                    acol = acs[:, h:h + 1]
                    arow = acsT[h:h + 1, :]
                    Lm = jnp.exp(jnp.where(causal, acol - arow, NEG))
                    M = CB * Lm
                    dM = dMst[a * 128:(a + 1) * 128]
                    dCB = dCB + dM * Lm
                    G = dM * M
                    dacs = dacs + jnp.where(R["col"] == h, jnp.sum(G, axis=1, keepdims=True), 0.0)
                    dacs_t = dacs_t + jnp.where(R["row"] == h, jnp.sum(G, axis=0, keepdims=True), 0.0)
                    mts.append(M.T)
                dxdiag.append(_mm(jnp.concatenate(mts, axis=1), dYst))
            dS = dhn[:, cols]
            dxd.append(_mm(Bg, dS))
            yoff.append(_mm(Cg, hprev[:, cols]))
            dhprev.append(_mm_tn(Cg, dYo[:, cols]))
            dCs.append(_mm_nt(dYo[:, cols], hprev[:, cols]) + _mm(dCB, Bg))
            dBs.append(_mm_tn(dCB, Cg) + _mm_nt(Xd[:, cols], dS))
        Yoff = jnp.concatenate(yoff, axis=1) * R["eo_e"]
        dXd = jnp.concatenate(dxd, axis=1)
        dX = jnp.concatenate(dxdiag, axis=1) + dXd * R["ds_e"]
        t_state = dXd * Xd
        dacs = dacs + _mm_2pass_r(dY * Yoff - t_state, ET) - dacs_t.T
        v_last = _colsum(t_state + dhn * hprev * R["cd_e"])
        dlast = _mm_exact_r(jnp.broadcast_to(v_last, (8, 1024)), ET)[0:1, :]
        dacs = dacs + jnp.where(R["row"] == 127, dlast, 0.0)
        triu = (R["col"] >= R["row"]).astype(BF16)
        da = _mm_exact_l(triu, dacs)
        ddt = da * R["A"] + _mm(dX * xs_c, ET)
        ddt_raw = ddt * _sigmoid(R["raw"])
        dxs_c = dX * R["dt_e"] + dY * dsk
        dh_scr[...] = jnp.concatenate(dhprev, axis=1) + dhn * R["cd_e"]

        dact = jnp.concatenate([dxs_c] + dBs + dCs, axis=1)
        pre, sg = R["pre"], R["sg"]
        dpre = dact * (sg * (1.0 + pre * (1.0 - sg)))
        ext2_scr[0:8, :] = dpre[120:128, :]
        ext2_scr[8:16, :] = nxt_scr[...]
        nxt_scr[...] = dpre[0:8, :]
        cw = cw_ref[...]
        u_b, dpre_b = p_ref[:, S_XS:S_DT].astype(_MXU), dpre.astype(_MXU)
        dxbc = cw[3:4, :] * dpre
        taps = [_colsum(dpre * p_ref[:, S_XS:S_DT])]
        for s in (1, 2, 3):
            up = (R["col"] - R["row"] == s).astype(_MXU)
            d_s = jnp.concatenate([jnp.dot(up, dpre_b, preferred_element_type=F32)[0:120],
                                   ext2_scr[s:8 + s, :]], axis=0)
            dxbc = dxbc + cw[3 - s:4 - s, :] * d_s
            taps.append(_colsum(dpre * _rows_from_above(u_b, s, ext_scr, R["row"], R["col"])))
        acc_cw_ref[...] += _rows8(taps[::-1] + [_colsum(dpre)])
        acc_w_ref[...] += _rows8([jnp.concatenate(dnw_parts, axis=1), _colsum(dY * xs_c)])
        acc_s_ref[...] += _rows8([_colsum(ddt_raw), _colsum(da * R["dt"])])

        lane = lax.broadcasted_iota(jnp.int32, (128, 128), 1)
        dp_ref[:, S_Z:S_Z + 1024] = dz.astype(dp_ref.dtype)
        dp_ref[:, S_XS:S_DT] = dxbc.astype(dp_ref.dtype)
        dp_ref[:, S_DT:S_DT + 128] = jnp.where(lane < N_HEADS, ddt_raw, 0.0).astype(dp_ref.dtype)
        dp_ref[:, S_DT + 128:S_W] = jnp.zeros((128, 128), dp_ref.dtype)

        @pl.when(i == nc - 1)
        def _():
            acc = acc_s_ref[...]
            dskip = _mm_exact_r(acc_w_ref[...], ET)[1:2, :]
            acc_s_ref[...] = _rows8([acc[0:1, :], acc[1:2, :] * R["A"], dskip])

    const = lambda shape: pl.BlockSpec(shape, lambda i: (0, 0))
    smem = pl.BlockSpec(memory_space=pltpu.SMEM)
    rev = lambda i: (nc - 1 - i, 0)
    return _call(
        body, comm, name="ssd_bwd", grid=(nc,),
        in_specs=[pl.BlockSpec((CHUNK, S_W), rev),
                  pl.BlockSpec((8, S_W), lambda i: (jnp.maximum((nc - 1 - i) * 16 - 1, 0), 0)),
                  pl.BlockSpec((128, 1024), rev),
                  pl.BlockSpec((CHUNK, D_SSD), rev),
                  pl.BlockSpec((CHUNK, D_XBC), rev),
                  pl.BlockSpec((CHUNK, D_SSD), rev),
                  const((4, D_XBC)), const((1, D_XBC)), smem, smem, smem, const((1, 1024)),
                  const((128, 1024)), const((1024, 128))],
        out_specs=[pl.BlockSpec((CHUNK, S_W), rev), const((8, D_XBC)), const((8, 1024)), const((8, 128))],
        out_shape=[jax.ShapeDtypeStruct((L, S_W), _MXU), jax.ShapeDtypeStruct((8, D_XBC), F32),
                   jax.ShapeDtypeStruct((8, 1024), F32), jax.ShapeDtypeStruct((8, 128), F32)],
        scratch_shapes=[pltpu.VMEM((128, 1024), F32), pltpu.VMEM((16, D_XBC), F32),
                        pltpu.VMEM((16, D_XBC), F32), pltpu.VMEM((8, D_XBC), F32)],
        args=(proj_ssd, proj_ssd, hprev_all, ypre, pre, dy, conv_w, conv_b, dt_bias, a_log, d_skip, norm_w, E, ET))


def _rope(t, tab):
    cos, sa, sb = tab[:, 0:128], tab[:, 128:256], tab[:, 256:384]
    outs = []
    for i in range(t.shape[1] // 128):
        tg = t[:, i * 128:(i + 1) * 128]
        outs.append(tg * cos + pltpu.roll(tg, 8, 1) * sa + pltpu.roll(tg, 120, 1) * sb)
    return jnp.concatenate(outs, axis=1)


def _rope_transposed(d, tab):
    cos, sa, sb = tab[:, 0:128], tab[:, 128:256], tab[:, 256:384]
    outs = []
    for i in range(d.shape[1] // 128):
        dg = d[:, i * 128:(i + 1) * 128]
        outs.append(dg * cos + pltpu.roll(dg * sa, 120, 1) + pltpu.roll(dg * sb, 8, 1))
    return jnp.concatenate(outs, axis=1)


def _lo_half(rows):
    return lax.broadcasted_iota(jnp.int32, (rows, 128), 1) < HEAD_DIM


def _native_half(rows, j):
    lo = _lo_half(rows)
    return lo if j % 2 == 0 else jnp.logical_not(lo)


def _kv_native(t, j):
    p = j // 2
    return jnp.where(_native_half(t.shape[0], j), t[:, p * 128:(p + 1) * 128], 0.0)


def _stack_heads(t, j):
    out = []
    for m in (2 * j, 2 * j + 1):
        pair = t[:, m * 128:(m + 1) * 128]
        swapped = pltpu.roll(pair, HEAD_DIM, 1)
        out += [pair, swapped] if j % 2 == 0 else [swapped, pair]
    return jnp.concatenate(out, axis=0)


def _unstack_heads(s, j):
    out = []
    for m in range(2):
        first, second = s[256 * m:256 * m + 128], s[256 * m + 128:256 * m + 256]
        if j % 2 == 0:
            out.append(first + pltpu.roll(second, HEAD_DIM, 1))
        else:
            out.append(pltpu.roll(first, HEAD_DIM, 1) + second)
    return jnp.concatenate(out, axis=1)


def _keep_native(r, j):
    return jnp.where(_native_half(r.shape[0], j), r, 0.0)


def _sink_row(sink_ref, j):
    hid = lax.broadcasted_iota(jnp.int32, (1, 4 * CHUNK), 1) // CHUNK
    row = jnp.zeros((1, 4 * CHUNK), F32)
    for hh in range(4):
        row = jnp.where(hid == hh, sink_ref[4 * j + hh], row)
    return row


def _from_current():
    si = lax.broadcasted_iota(jnp.int32, (CHUNK, 4 * CHUNK), 0)
    qi = lax.broadcasted_iota(jnp.int32, (CHUNK, 4 * CHUNK), 1) % CHUNK
    return si <= qi


def _fold(full, from_cur, pen=0.0):
    return jnp.where(from_cur, full[CHUNK:2 * CHUNK], full[0:CHUNK] + pen)


def _unfold(t, from_cur):
    c = jnp.where(from_cur, t, 0.0)
    return jnp.concatenate([t - c, c], axis=0)


def _softmax_sink(s, sink):
    mx = jnp.maximum(jnp.max(s, axis=0, keepdims=True), sink)
    p = jnp.exp(s - mx)
    esink = jnp.exp(sink - mx)
    inv = 1.0 / (jnp.sum(p, axis=0, keepdims=True) + esink)
    return p * inv, esink * inv


def _swa_inputs(blk, p_ref, prev_ref, tab_ref, ptab_ref):
    tab = tab_ref[...]
    qr = _rope(p_ref[:, A_Q:A_Q + 1024], tab) * ATT_SCALE
    kk = jnp.concatenate([_rope(prev_ref[:, 0:256], ptab_ref[...]), _rope(p_ref[:, A_K:A_K + 256], tab)], axis=0)
    vv = jnp.concatenate([prev_ref[:, 256:512], p_ref[:, A_V:A_V + 256]], axis=0)
    return tab, qr, kk, vv, jnp.where(blk > 0, 0.0, NEG)


def _swa_forward_step(sink_ref, p_ref, prev_ref, tab_ref, ptab_ref, y_ref):
    n = pl.program_id(0)
    _, qr, kk, vv, pen = _swa_inputs(n, p_ref, prev_ref, tab_ref, ptab_ref)
    from_cur = _from_current()
    outs = []
    for j in range(KV_HEADS):
        s = _fold(_mm_nt(_kv_native(kk, j), _stack_heads(qr, j)), from_cur, pen)
        P, _ = _softmax_sink(s, _sink_row(sink_ref, j))
        outs.append(_unstack_heads(_mm_tn(_unfold(P, from_cur), _kv_native(vv, j)), j))
    g = p_ref[:, A_G:A_G + 1024]
    y_ref[:, D_SSD:D_SSD + D_ATT] = (jnp.concatenate(outs, axis=1) * (g * _sigmoid(g))).astype(y_ref.dtype)


def _mixer_forward(proj_ssd, proj_att, tabs, sinks, conv_w, conv_b, dt_bias, a_log, d_skip, norm_w, E, comm=None):
    L = proj_ssd.shape[0]
    nc = L // CHUNK

    def body(p_ref, halo_ref, cw_ref, cb_ref, dtb_ref, alog_ref, dsk_ref, nw_ref, e_ref,
             sink_ref, pa_ref, prev_ref, tab_ref, ptab_ref, y_ref, ypre_ref, hprev_ref, pre_ref, h_scr, ext_scr):
        _ssd_forward_step(p_ref, halo_ref, cw_ref, cb_ref, dtb_ref, alog_ref, dsk_ref, nw_ref, e_ref,
                          y_ref, ypre_ref, hprev_ref, pre_ref, h_scr, ext_scr)
        _swa_forward_step(sink_ref, pa_ref, prev_ref, tab_ref, ptab_ref, y_ref)

    const = lambda shape: pl.BlockSpec(shape, lambda c: (0, 0))
    smem = pl.BlockSpec(memory_space=pltpu.SMEM)
    rows = lambda w: pl.BlockSpec((CHUNK, w), lambda c: (c, 0))
    return _call(
        body, comm, name="mixer_fwd", grid=(nc,),
        in_specs=[rows(S_W), pl.BlockSpec((8, S_W), lambda c: (jnp.maximum(c * 16 - 1, 0), 0)),
                  const((4, D_XBC)), const((1, D_XBC)), smem, smem, smem, const((1, 1024)), const((128, 1024)),
                  smem, rows(A_W), pl.BlockSpec((CHUNK, 512), lambda c: (jnp.maximum(c - 1, 0), 2)),
                  rows(384), pl.BlockSpec((CHUNK, 384), lambda c: (jnp.maximum(c - 1, 0), 0))],
        out_specs=[rows(D_SSD + D_ATT), rows(D_SSD), pl.BlockSpec((128, 1024), lambda c: (c, 0)), rows(D_XBC)],
        out_shape=[jax.ShapeDtypeStruct((L, D_SSD + D_ATT), _MXU), jax.ShapeDtypeStruct((L, D_SSD), F32),
                   jax.ShapeDtypeStruct((nc * 128, 1024), F32), jax.ShapeDtypeStruct((L, D_XBC), F32)],
        scratch_shapes=[pltpu.VMEM((128, 1024), F32), pltpu.VMEM((136, D_XBC), F32)],
        args=(proj_ssd, proj_ssd, conv_w, conv_b, dt_bias, a_log, d_skip, norm_w, E,
              sinks, proj_att, proj_att, tabs, tabs))


def _swa_backward(proj_att, tabs, sinks, dy, reduce=None):
    L = proj_att.shape[0]
    nb = L // CHUNK

    def body(sink_ref, p_ref, prev_ref, tab_ref, ptab_ref, dy_ref, dp_ref, dsink_ref, carry_k, carry_v):
        i = pl.program_id(0)
        n = nb - 1 - i

        @pl.when(i == 0)
        def _():
            carry_k[...] = jnp.zeros_like(carry_k)
            carry_v[...] = jnp.zeros_like(carry_v)
            dsink_ref[...] = jnp.zeros_like(dsink_ref)

        tab, qr, kk, vv, pen = _swa_inputs(n, p_ref, prev_ref, tab_ref, ptab_ref)
        from_cur = _from_current()
        g = p_ref[:, A_G:A_G + 1024]
        sgm = _sigmoid(g)
        dyv = dy_ref[...]
        do_all = dyv * (g * sgm)
        lane8 = lax.broadcasted_iota(jnp.int32, (8, 128), 1)
        hid = lax.broadcasted_iota(jnp.int32, (1, 4 * CHUNK), 1) // CHUNK
        o_parts, dq_parts = [], []
        dk_nat = [jnp.zeros((2 * CHUNK, 128), F32) for _ in range(2)]
        dv_nat = [jnp.zeros((2 * CHUNK, 128), F32) for _ in range(2)]
        dsink = jnp.zeros((8, 128), F32)
        for j in range(KV_HEADS):
            qs = _stack_heads(qr, j)
            kkb, vvb = _kv_native(kk, j), _kv_native(vv, j)
            P, psink = _softmax_sink(_fold(_mm_nt(kkb, qs), from_cur, pen), _sink_row(sink_ref, j))
            p_full = _unfold(P, from_cur)
            o_parts.append(_unstack_heads(_mm_tn(p_full, vvb), j))
            do_s = _stack_heads(do_all, j)
            dP = _fold(_mm_nt(vvb, do_s), from_cur)
            D = jnp.sum(P * dP, axis=0, keepdims=True)
            ds_full = _unfold(P * (dP - D), from_cur)
            sd = psink * D
            for hh in range(4):
                dsink = dsink + jnp.where(lane8 == 4 * j + hh, -jnp.sum(jnp.where(hid == hh, sd, 0.0)), 0.0)
            dq_parts.append(_unstack_heads(_mm_tn(ds_full, kkb), j) * ATT_SCALE)
            dk_nat[j // 2] = dk_nat[j // 2] + _keep_native(_mm(ds_full, qs), j)
            dv_nat[j // 2] = dv_nat[j // 2] + _keep_native(_mm(p_full, do_s), j)
        o = jnp.concatenate(o_parts, axis=1)
        dkk = jnp.concatenate(dk_nat, axis=1)
        dvv = jnp.concatenate(dv_nat, axis=1)
        out = dp_ref.dtype
        dp_ref[:, A_Q:A_Q + 1024] = _rope_transposed(jnp.concatenate(dq_parts, axis=1), tab).astype(out)
        dp_ref[:, A_K:A_K + 256] = _rope_transposed(dkk[CHUNK:2 * CHUNK] + carry_k[...], tab).astype(out)
        dp_ref[:, A_V:A_V + 256] = (dvv[CHUNK:2 * CHUNK] + carry_v[...]).astype(out)
        dp_ref[:, A_G:A_G + 1024] = (dyv * o * (sgm * (1.0 + g * (1.0 - sgm)))).astype(out)
        carry_k[...] = dkk[0:CHUNK]
        carry_v[...] = dvv[0:CHUNK]
        dsink_ref[...] += dsink

    rev = lambda i: (nb - 1 - i, 0)
    prev = lambda i: jnp.maximum(nb - 2 - i, 0)
    return _call(
        body, None, name="swa_bwd", grid=(nb,),
        in_specs=[pl.BlockSpec(memory_space=pltpu.SMEM),
                  pl.BlockSpec((CHUNK, A_W), rev),
                  pl.BlockSpec((CHUNK, 512), lambda i: (prev(i), 2)),
                  pl.BlockSpec((CHUNK, 384), rev),
                  pl.BlockSpec((CHUNK, 384), lambda i: (prev(i), 0)),
                  pl.BlockSpec((CHUNK, D_ATT), lambda i: (nb - 1 - i, 1))],
        out_specs=[pl.BlockSpec((CHUNK, A_W), rev), pl.BlockSpec((8, 128), lambda i: (0, 0))],
        out_shape=[jax.ShapeDtypeStruct((L, A_W), _MXU), jax.ShapeDtypeStruct((8, 128), F32)],
        scratch_shapes=[pltpu.VMEM((CHUNK, 256), F32), pltpu.VMEM((CHUNK, 256), F32)],
        args=(sinks, proj_att, proj_att, tabs, tabs, dy), reduce=reduce)


def _head(y, x, target, w_out, ln_g, ln_b, *, tm):
    L = x.shape[0]
    nsteps = L // tm

    def body(y_ref, x_ref, t_ref, wo_ref, g_ref, b_ref, dr_ref, dy_ref, acc_ref):
        i = pl.program_id(0)

        @pl.when(i == 0)
        def _():
            acc_ref[...] = jnp.zeros_like(acc_ref)

        r = ALPHA * x_ref[...] + _mm(y_ref[...], wo_ref[...])
        mu = jnp.mean(r, axis=-1, keepdims=True)
        d = r - mu
        rstd = lax.rsqrt(jnp.mean(d * d, axis=-1, keepdims=True) + LN_EPS)
        xh = d * rstd
        gam = g_ref[0:1, :]
        e = xh * gam + b_ref[0:1, :] - t_ref[...]
        dout = e * (1.0 / D_MODEL)
        dxh = dout * gam
        dr = rstd * (dxh - jnp.mean(dxh, axis=-1, keepdims=True)
                     - xh * jnp.mean(dxh * xh, axis=-1, keepdims=True))
        dr_ref[...] = dr
        dy_ref[...] = _mm_nt(dr, wo_ref[...])
        acc_ref[...] += _rows8([_colsum(dout * xh), _colsum(dout), _colsum(e * e) * (0.5 / D_MODEL)])

        @pl.when(i == nsteps - 1)
        def _():
            acc = acc_ref[...]
            tot = jnp.sum(acc[2:3, :])
            rid = lax.broadcasted_iota(jnp.int32, (8, 1024), 0)
            acc_ref[...] = jnp.where(rid == 3, tot, acc)

    const = lambda shape: pl.BlockSpec(shape, lambda i: (0, 0))
    row = lambda w: pl.BlockSpec((tm, w), lambda i: (i, 0))
    return pl.pallas_call(
        body, name="head", grid=(nsteps,),
        in_specs=[row(2048), row(1024), row(1024), const((2048, 1024)), const((1, 1024)), const((1, 1024))],
        out_specs=[row(1024), row(2048), const((8, 1024))],
        out_shape=[jax.ShapeDtypeStruct((L, D_MODEL), F32), jax.ShapeDtypeStruct((L, 2048), F32),
                   jax.ShapeDtypeStruct((8, 1024), F32)],
        compiler_params=_params(("arbitrary",)),
    )(y, x, target, w_out, ln_g, ln_b)


def _gather_w_in(w_shard, positions):
    R = w_shard.shape[0]
    halves = (pl.ds(0, R // 2), pl.ds(R // 2, R // 2))
    any_spec = pl.BlockSpec(memory_space=pl.ANY)
    vmem = pl.BlockSpec(memory_space=pltpu.VMEM)

    def body(in_ref, pos_ref, inv_ref, out_ref, tab_ref, send_sems, recv_sems, local_sem):
        x, y, c = _position()

        def slot(p, half=None):
            s = out_ref.at[_index(*p)]
            return s if half is None else s.at[halves[half]]

        def same_core(p):
            return (p[0], p[1], c)

        def other_core(p):
            return (p[0], p[1], 1 - c)

        me, xn, yn, dg = (x, y), (1 - x, y), (x, 1 - y), (1 - x, 1 - y)

        def copy(k, dst, to, src=None):
            return _remote(dst if src is None else src, dst, send_sems.at[k], recv_sems.at[k], to)

        local = pltpu.make_async_copy(in_ref, slot(same_core(me)), local_sem)
        local.start()
        own = [copy(0, slot(same_core(me)), other_core(me), in_ref), copy(1, slot(same_core(me)), same_core(xn), in_ref),
               copy(2, slot(same_core(me)), same_core(yn), in_ref)]
        for cp in own:
            cp.start()
        _rope_tables(pos_ref, inv_ref, tab_ref)
        copy(1, slot(same_core(xn)), same_core(xn)).wait_recv()
        passed = [copy(4, slot(same_core(xn), 1), same_core(yn)), copy(5, slot(same_core(xn)), other_core(me))]
        for cp in passed:
            cp.start()
        copy(2, slot(same_core(yn)), same_core(yn)).wait_recv()
        more = [copy(3, slot(same_core(yn), 0), same_core(xn)), copy(6, slot(same_core(yn)), other_core(me))]
        for cp in more:
            cp.start()
        passed += more
        for k, half in ((3, 0), (4, 1)):
            copy(k, slot(same_core(dg), half), same_core(xn)).wait_recv()
            fwd = copy(7 + half, slot(same_core(dg), half), other_core(me))
            fwd.start()
            passed.append(fwd)
        copy(0, slot(other_core(me)), other_core(me)).wait_recv()
        copy(5, slot(other_core(xn)), other_core(me)).wait_recv()
        copy(6, slot(other_core(yn)), other_core(me)).wait_recv()
        for half in (0, 1):
            copy(7 + half, slot(other_core(dg), half), other_core(me)).wait_recv()
        for cp in own + passed:
            cp.wait_send()
        local.wait()

    return pl.pallas_call(
        body, name="gather_w_in", in_specs=[any_spec, vmem, vmem], out_specs=[any_spec, vmem],
        out_shape=[jax.ShapeDtypeStruct((N_DEV,) + w_shard.shape, w_shard.dtype),
                   jax.ShapeDtypeStruct((positions.size, 384), F32)],
        scratch_shapes=[pltpu.SemaphoreType.DMA((9,)), pltpu.SemaphoreType.DMA((9,)), pltpu.SemaphoreType.DMA],
        compiler_params=_params(),
    )(w_shard, positions, jnp.asarray(ROPE_INV)[None, :])


def _input_gradient(d_ssd, d_att, w_ssd, w_att, dr, *, tm, comm=None, reduce=None):
    L = dr.shape[0]

    def body(ds_ref, da_ref, ws_ref, wa_ref, dr_ref, o_ref):
        o_ref[...] = ALPHA * dr_ref[...] + _mm_nt(ds_ref[...], ws_ref[...]) + _mm_nt(da_ref[...], wa_ref[...])

    row = lambda w: pl.BlockSpec((tm, w), lambda i: (i, 0))
    resident = lambda a: pl.BlockSpec(a.shape, lambda i: (0, 0), pipeline_mode=pl.Buffered(1))
    return _call(body, comm, name="dx", grid=(L // tm,),
                 in_specs=[row(S_W), row(A_W), resident(w_ssd), resident(w_att), row(D_MODEL)],
                 out_specs=[row(D_MODEL)], out_shape=[jax.ShapeDtypeStruct((L, D_MODEL), F32)],
                 scratch_shapes=[], args=(d_ssd, d_att, w_ssd, w_att, dr), reduce=reduce)


SHARD_COLS = D_IN_PROJ // N_DEV
SPLIT = N_SSD_REAL - 4 * SHARD_COLS
RELAYOUT_ROWS = 256


def _unpack_w_in(w_all):
    def body(g_ref, ws_ref, wa_ref):
        for j in range(4):
            ws_ref[:, SHARD_COLS * j:SHARD_COLS * (j + 1)] = g_ref[j]
        ws_ref[:, 4 * SHARD_COLS:N_SSD_REAL] = g_ref[4, :, 0:SPLIT]
        ws_ref[:, N_SSD_REAL:S_W] = jnp.zeros((RELAYOUT_ROWS, S_W - N_SSD_REAL), ws_ref.dtype)
        wa_ref[:, 0:SHARD_COLS - SPLIT] = g_ref[4, :, SPLIT:SHARD_COLS]
        for j in range(5, N_DEV):
            lo = SHARD_COLS * (j - 4) - SPLIT
            wa_ref[:, lo:lo + SHARD_COLS] = g_ref[j]

    return pl.pallas_call(
        body, name="unpack_w_in", grid=(D_MODEL // RELAYOUT_ROWS,),
        in_specs=[pl.BlockSpec((N_DEV, RELAYOUT_ROWS, SHARD_COLS), lambda i: (0, i, 0))],
        out_specs=[pl.BlockSpec((RELAYOUT_ROWS, S_W), lambda i: (i, 0)), pl.BlockSpec((RELAYOUT_ROWS, A_W), lambda i: (i, 0))],
        out_shape=[jax.ShapeDtypeStruct((D_MODEL, S_W), w_all.dtype), jax.ShapeDtypeStruct((D_MODEL, A_W), w_all.dtype)],
        compiler_params=_params(("arbitrary",)),
    )(w_all)


def _dw_in(xb, d, half, tail=None, *, tl=1024):
    L, N = d.shape
    steps = L // tl

    def body(x_ref, d_ref, *refs):
        if half == 0:
            p_ref, tail_ref, acc = refs
        else:
            t_ref, p_ref, acc = refs
        l = pl.program_id(0)

        @pl.when(l == 0)
        def _():
            acc[...] = jnp.zeros_like(acc)

        acc[...] += _mm_tn(x_ref[...], d_ref[...])

        @pl.when(l == steps - 1)
        def _():
            if half == 0:
                tail_ref[...] = acc[:, S_DT:S_W]
            for j in range(4):
                if half == 0:
                    pieces = [(0, acc[:, SHARD_COLS * j:SHARD_COLS * (j + 1)])]
                elif j == 0:
                    pieces = [(0, t_ref[:, 4 * SHARD_COLS - S_DT:N_SSD_REAL - S_DT]), (SPLIT, acc[:, 0:SHARD_COLS - SPLIT])]
                else:
                    lo = SHARD_COLS * j - SPLIT
                    pieces = [(0, acc[:, lo:lo + SHARD_COLS])]
                for off, blk in pieces:
                    p_ref[j, :, off:off + blk.shape[1]] = blk

    once = pl.Buffered(1)
    whole = lambda shape: pl.BlockSpec(shape, lambda l: (0,) * len(shape), pipeline_mode=once)
    in_specs = [pl.BlockSpec((tl, D_MODEL), lambda l: (l, 0)), pl.BlockSpec((tl, N), lambda l: (l, 0))]
    args = [xb, d]
    out_shape = [jax.ShapeDtypeStruct((4, D_MODEL, SHARD_COLS), F32)]
    if half == 0:
        out_shape.append(jax.ShapeDtypeStruct((D_MODEL, S_W - S_DT), F32))
    else:
        in_specs.append(whole(tail.shape))
        args.append(tail)
    return pl.pallas_call(
        body, name="dw_in_%d" % half, grid=(steps,), in_specs=in_specs,
        out_specs=[whole(o.shape) for o in out_shape], out_shape=out_shape,
        scratch_shapes=[pltpu.VMEM((D_MODEL, N), F32)], compiler_params=_params(("arbitrary",)),
    )(*args)


def _adamw_math(w, g, m, v):
    m = ADAM_B1 * m + (1.0 - ADAM_B1) * g
    v = ADAM_B2 * v + (1.0 - ADAM_B2) * (g * g)
    m_hat = m / (1.0 - ADAM_B1 ** ADAM_STEP)
    v_hat = v / (1.0 - ADAM_B2 ** ADAM_STEP)
    delta = -ADAM_LR * (m_hat / (jnp.sqrt(v_hat) + ADAM_EPS) + ADAM_WD * w)
    return delta, m, v


def _adamw_shard(g_own, recv, w, m, v, *, rows, name):
    R, C = g_own.shape

    def body(g_ref, r_ref, w_ref, m_ref, v_ref, go_ref, d_ref, mo_ref, vo_ref):
        g = g_ref[...]
        for k in range(N_DEV - 1):
            g = g + r_ref[k].astype(F32)
        d, mn, vn = _adamw_math(w_ref[...], g, m_ref[...], v_ref[...])
        go_ref[...] = g
        d_ref[...] = d
        mo_ref[...] = mn
        vo_ref[...] = vn

    blk = pl.BlockSpec((rows, C), lambda i: (i, 0))
    return pl.pallas_call(
        body, name=name, grid=(R // rows,),
        in_specs=[blk, pl.BlockSpec((N_DEV - 1, rows, C), lambda i: (0, i, 0)), blk, blk, blk],
        out_specs=[blk] * 4, out_shape=[jax.ShapeDtypeStruct((R, C), F32)] * 4,
        compiler_params=_params(("arbitrary",)),
    )(g_own, recv, w, m, v)


def _minor_rows_view(a):
    return jnp.transpose(a, (2, 0, 1)).reshape(SHARD_COLS * 8, 128)


def _from_minor_rows_view(v):
    return jnp.transpose(v.reshape(SHARD_COLS, 8, 128), (1, 2, 0)).reshape(1, D_MODEL, SHARD_COLS)


def _adamw_w_in(is_lo, own_lo, own_hi, recv_lo, recv_hi, w, m, v):
    C = SHARD_COLS
    pad = -C % 128

    def body(lo_ref, ol_ref, oh_ref, rl_ref, rh_ref, w_ref, m_ref, v_ref, go_ref, d_ref, mo_ref, vo_ref):
        lo = lo_ref[0] == 1
        for q in range(D_MODEL // 128):
            band = pl.ds(q * 128, 128)
            g = jnp.where(lo, ol_ref[band, :], oh_ref[band, :])
            for k in range(2):
                g = g + jnp.where(lo, rl_ref[k, band, :], rh_ref[k, band, :]).astype(F32)
            g = jnp.pad(g, ((0, 0), (0, pad))).T[0:C]
            rows = pl.ds(q, C, stride=8)
            d, mn, vn = _adamw_math(w_ref[rows, :], g, m_ref[rows, :], v_ref[rows, :])
            go_ref[rows, :] = g
            d_ref[rows, :] = d
            mo_ref[rows, :] = mn
            vo_ref[rows, :] = vn

    return pl.pallas_call(
        body, name="adamw_w_in", out_shape=[jax.ShapeDtypeStruct(w.shape, F32)] * 4,
        in_specs=[pl.BlockSpec(memory_space=pltpu.SMEM)] + [pl.BlockSpec(memory_space=pltpu.VMEM)] * 7,
        out_specs=[pl.BlockSpec(memory_space=pltpu.VMEM)] * 4,
        compiler_params=_params(),
    )(is_lo, own_lo, own_hi, recv_lo, recv_hi, w, m, v)


SMALL = ("conv_b", "dt_bias", "a_log", "d_skip", "ssd_norm_w", "attn_sinks", "ln_g", "ln_b")


def _adamw_small(gathered, params):
    n_p = len(SMALL)

    def body(*refs):
        acc = []
        for r in refs[:5]:
            t = r[0]
            for k in range(1, N_DEV):
                t = t + r[k]
            acc.append(t)
        head, conv, norm, scal, sink = acc
        grads = dict(conv_b=conv[4:5, :], dt_bias=scal[0:1, 0:N_HEADS], a_log=scal[1:2, 0:N_HEADS],
                     d_skip=scal[2:3, 0:N_HEADS], ssd_norm_w=norm[0:1, :], attn_sinks=sink[0:1, 0:N_HEADS],
                     ln_g=head[0:1, :], ln_b=head[1:2, :])
        wmv = refs[5:5 + 3 * n_p]
        outs = refs[5 + 3 * n_p:]
        outs[0][...] = head[3:4, 0:1]
        outs[1][...] = conv[0:4, :]
        for i, name in enumerate(SMALL):
            w_ref, m_ref, v_ref = wmv[3 * i:3 * i + 3]
            g = grads[name]
            d, mn, vn = _adamw_math(w_ref[...], g, m_ref[...], v_ref[...])
            for o_ref, val in zip(outs[2 + 4 * i:6 + 4 * i], (g, d, mn, vn)):
                o_ref[...] = val

    flat = [a for name in SMALL for a in params[name]]
    out_shape = [jax.ShapeDtypeStruct((1, 1), F32), jax.ShapeDtypeStruct((4, D_XBC), F32)]
    for name in SMALL:
        out_shape += [jax.ShapeDtypeStruct(params[name][0].shape, F32)] * 4
    res = pl.pallas_call(body, name="adamw_small", out_shape=out_shape, compiler_params=_params())(*gathered, *flat)
    return res[0], res[1], {name: res[2 + 4 * i:6 + 4 * i] for i, name in enumerate(SMALL)}


def _adamw_plain(g, w, m, v):
    def body(g_ref, w_ref, m_ref, v_ref, d_ref, mo_ref, vo_ref):
        d, mn, vn = _adamw_math(w_ref[...], g_ref[...], m_ref[...], v_ref[...])
        d_ref[...] = d
        mo_ref[...] = mn
        vo_ref[...] = vn

    return pl.pallas_call(
        body, name="adamw_conv_w", out_shape=[jax.ShapeDtypeStruct(w.shape, F32)] * 3,
        compiler_params=_params(),
    )(g, w, m, v)


def _lane_pattern(fn):
    return np.asarray([fn(l % HEAD_DIM) for l in range(128)], np.float32)


ROPE_INV = _lane_pattern(lambda r: ROPE_THETA ** (-2.0 * (r % 8) / ROPE_DIM) if r < ROPE_DIM else 0.0)


def _rope_tables(pos_ref, inv_ref, tab_ref):
    lane = lax.broadcasted_iota(jnp.int32, (1, 128), 1) % HEAD_DIM
    upper = jnp.where((lane >= ROPE_DIM // 2) & (lane < ROPE_DIM), 1.0, 0.0)
    lower = jnp.where(lane < ROPE_DIM // 2, -1.0, 0.0)

    def block(r, carry):
        rows = pl.ds(pl.multiple_of(r * CHUNK, CHUNK), CHUNK)
        pos = jnp.broadcast_to(pos_ref[pl.ds(r, 1), :].astype(F32), (CHUNK, 128)).T
        ang = pos * inv_ref[...]
        sn = jnp.sin(ang)
        tab_ref[rows, 0:128] = jnp.cos(ang)
        tab_ref[rows, 128:256] = sn * upper
        tab_ref[rows, 256:384] = sn * lower
        return carry

    lax.fori_loop(0, pos_ref.shape[0], block, 0)


def _expansion():
    E = np.arange(1024)[None, :] // HEAD_DIM == np.arange(128)[:, None]
    return jnp.asarray(E, BF16), jnp.asarray(E.T, BF16)


def _ssd_args(conv_w, conv_b, dt_bias, a_log, d_skip, norm_w, E):
    return (conv_w, conv_b, dt_bias.reshape(-1), a_log.reshape(-1), d_skip.reshape(-1), norm_w, E)


def kernel(x, positions, w_in, conv_w, conv_b, dt_bias, a_log, d_skip, ssd_norm_w, attn_sinks, w_out, ln_g, ln_b, loss_target, m_w_in, m_conv_w, m_conv_b, m_dt_bias, m_a_log, m_d_skip, m_ssd_norm_w, m_attn_sinks, m_w_out, m_ln_g, m_ln_b, v_w_in, v_conv_w, v_conv_b, v_dt_bias, v_a_log, v_d_skip, v_ssd_norm_w, v_attn_sinks, v_w_out, v_ln_g, v_ln_b):
    me = _index(*_position())
    x0, target = x[0], loss_target[0]
    bf16_shard = lambda shape: jax.ShapeDtypeStruct(shape, BF16)
    E, ET = _expansion()
    sinks = attn_sinks.reshape(-1)

    w_all, tabs = _gather_w_in(w_in[0].astype(BF16), positions[0].reshape(-1, 128))
    w_ssd, w_att = _unpack_w_in(w_all)
    gather_conv_w = _Hosted([conv_w[0]], [jax.ShapeDtypeStruct((N_DEV,) + conv_w.shape[1:], F32)],
                            [_Flow("gather", 0, 0)])

    proj_ssd, proj_att, xb, conv_w_all = _in_proj(x0, w_ssd, w_att, tm=512, comm=gather_conv_w)
    conv_w_f = jnp.transpose(conv_w_all, (1, 0, 2)).reshape(4, D_XBC)
    ssd_args = _ssd_args(conv_w_f, conv_b, dt_bias, a_log, d_skip, ssd_norm_w, E)
    gather_w_out = _Hosted([w_out[0].astype(BF16)], [bf16_shard((N_DEV, 256, D_MODEL))], [_Flow("gather", 0, 0)])
    y, ypre, hprev, pre, w_out_all = _mixer_forward(proj_ssd, proj_att, tabs, sinks, *ssd_args, comm=gather_w_out)
    w_out_f = w_out_all.reshape(2 * D_MODEL, D_MODEL)
    dr, dy, acc_head = _head(y, x0, target, w_out_f, ln_g, ln_b, tm=512)

    dw_out, dw_out_bf16 = _matmul_tn(y, dr, tl=1024, tn=D_MODEL, name="dw_out", emit_bf16=True)
    own_out = lax.dynamic_index_in_dim(dw_out.reshape(N_DEV, 256, D_MODEL), me, axis=0, keepdims=False)
    send_out = _Hosted([dw_out_bf16.reshape(N_DEV, 256, D_MODEL)], [bf16_shard((N_DEV - 1, 256, D_MODEL))],
                       [_Flow("exchange", 0, 0)])
    d_ssd, acc_cw, acc_w, acc_s, recv_out = _ssd_backward(proj_ssd, hprev, ypre, pre, dy, *ssd_args, ET, comm=send_out)
    stack_lo, dw_dt_block = _dw_in(xb, d_ssd, 0)
    d_att, dsink, own_lo, recv_lo = _swa_backward(proj_att, tabs, sinks, dy, reduce=_OwnerReduce(stack_lo, 0))
    (stack_hi,) = _dw_in(xb, d_att, 1, dw_dt_block)
    accs = [acc_head, acc_cw, acc_w, acc_s, dsink]
    gather_accs = _Hosted(accs, [jax.ShapeDtypeStruct((N_DEV,) + a.shape, F32) for a in accs],
                          [_Flow("gather", i, i) for i in range(5)])
    dx, *gathered, own_hi, recv_hi = _input_gradient(d_ssd, d_att, w_ssd, w_att, dr, tm=256, comm=gather_accs,
                                                     reduce=_OwnerReduce(stack_hi, 1))
    is_lo = (me < 4).reshape(1).astype(jnp.int32)

    g_in, d_in, nm_in, nv_in = [_from_minor_rows_view(r) for r in _adamw_w_in(
        is_lo, own_lo, own_hi, recv_lo, recv_hi, _minor_rows_view(w_in), _minor_rows_view(m_w_in), _minor_rows_view(v_w_in))]
    g_out, d_out, nm_out, nv_out = _adamw_shard(own_out, recv_out, w_out[0], m_w_out[0], v_w_out[0],
                                                rows=256, name="adamw_w_out")
    loss, g_conv_w, small = _adamw_small(gathered, dict(
        conv_b=(conv_b, m_conv_b, v_conv_b), dt_bias=(dt_bias, m_dt_bias, v_dt_bias), a_log=(a_log, m_a_log, v_a_log),
        d_skip=(d_skip, m_d_skip, v_d_skip), ssd_norm_w=(ssd_norm_w, m_ssd_norm_w, v_ssd_norm_w),
        attn_sinks=(attn_sinks, m_attn_sinks, v_attn_sinks), ln_g=(ln_g, m_ln_g, v_ln_g), ln_b=(ln_b, m_ln_b, v_ln_b)))
    g_cw = lax.dynamic_slice_in_dim(g_conv_w, me * (D_XBC // N_DEV), D_XBC // N_DEV, axis=1)
    d_cw, nm_cw, nv_cw = _adamw_plain(g_cw, conv_w[0], m_conv_w[0], v_conv_w[0])

    def leaves(i, big_in, cw, big_out):
        mid = [small[k][i] for k in ("conv_b", "dt_bias", "a_log", "d_skip", "ssd_norm_w", "attn_sinks")]
        return [big_in, cw[None]] + mid + [big_out[None], small["ln_g"][i], small["ln_b"][i]]

    return (loss.reshape(()), dx[None], *leaves(0, g_in, g_cw, g_out), *leaves(1, d_in, d_cw, d_out),
            *leaves(2, nm_in, nm_cw, nm_out), *leaves(3, nv_in, nv_cw, nv_out))
```

```python
import jax
import jax.numpy as jnp
from jax import lax
from jax.experimental import pallas as pl
from jax.experimental.pallas import tpu as pltpu
import numpy as np

F32 = jnp.float32
BF16 = jnp.bfloat16
_MXU = jnp.bfloat16

N_DEV = 8
D_MODEL = 1024
D_SSD = 1024
D_ATT = 1024
HEAD_DIM = 64
N_HEADS = 16
SSD_GROUPS = 2
KV_HEADS = 4
CHUNK = 128
D_XBC = 1536
D_IN_PROJ = 5136
ROPE_DIM = 16
ROPE_THETA = 500000.0
ALPHA = (2.0 * 1) ** 0.25
LN_EPS = 1e-5
RMS_EPS = 1e-5
ATT_SCALE = HEAD_DIM ** -0.5
NEG = -1e30

S_Z, S_XS, S_B, S_C, S_DT, S_W = 0, 1024, 2048, 2304, 2560, 2816
N_SSD_REAL = 2576
A_Q, A_K, A_V, A_G, A_W = 0, 1024, 1280, 1536, 2560

ADAM_LR = 0.001
ADAM_B1 = 0.9
ADAM_B2 = 0.999
ADAM_EPS = 1e-08
ADAM_WD = 0.01
ADAM_STEP = 10

VMEM_LIMIT = 48 * 1024 * 1024
MESH = pl.DeviceIdType.MESH


def _params(sem=None):
    return pltpu.CompilerParams(dimension_semantics=sem, vmem_limit_bytes=VMEM_LIMIT)


def _mm(a, b):
    return jnp.dot(a.astype(_MXU), b.astype(_MXU), preferred_element_type=F32)


def _mm_nt(a, b):
    return lax.dot_general(a.astype(_MXU), b.astype(_MXU), (((1,), (1,)), ((), ())),
                           preferred_element_type=F32)


def _mm_tn(a, b):
    return lax.dot_general(a.astype(_MXU), b.astype(_MXU), (((0,), (0,)), ((), ())),
                           preferred_element_type=F32)


def _split3(v):
    hi = v.astype(BF16)
    r = v - hi.astype(F32)
    mid = r.astype(BF16)
    lo = (r - mid.astype(F32)).astype(BF16)
    return hi, mid, lo


def _mm_exact_r(v, p01):
    hi, mid, lo = _split3(v)
    d = lambda a: jnp.dot(a, p01, preferred_element_type=F32)
    return d(hi) + d(mid) + d(lo)


def _mm_exact_l(p01, v):
    hi, mid, lo = _split3(v)
    d = lambda a: jnp.dot(p01, a, preferred_element_type=F32)
    return d(hi) + d(mid) + d(lo)


def _mm_2pass_r(v, p01):
    hi = v.astype(BF16)
    lo = (v - hi.astype(F32)).astype(BF16)
    return jnp.dot(hi, p01, preferred_element_type=F32) + jnp.dot(lo, p01, preferred_element_type=F32)


def _sigmoid(x):
    return 1.0 / (1.0 + jnp.exp(-x))


def _softplus(x):
    e = jnp.exp(-jnp.abs(x))
    u = 1.0 + e
    log1p = jnp.where(u == 1.0, e, jnp.log(u) * (e / (u - 1.0)))
    return jnp.maximum(x, 0.0) + log1p


def _rows8(rows):
    n = rows[0].shape[1]
    rid = lax.broadcasted_iota(jnp.int32, (8, n), 0)
    out = jnp.zeros((8, n), F32)
    for k, r in enumerate(rows):
        out = out + jnp.where(rid == k, r, 0.0)
    return out


def _colsum(a):
    return jnp.sum(a, axis=0, keepdims=True)


def _in_proj(x, w_ssd, w_att, *, tm, comm=None):
    L, K = x.shape

    def body(x_ref, ws_ref, wa_ref, ps_ref, pa_ref, xb_ref):
        xb = x_ref[...].astype(_MXU)
        xb_ref[...] = xb
        ps_ref[...] = jnp.dot(xb, ws_ref[...], preferred_element_type=F32)
        pa_ref[...] = jnp.dot(xb, wa_ref[...], preferred_element_type=F32)

    row = lambda w: pl.BlockSpec((tm, w), lambda i: (i, 0))
    resident = lambda a: pl.BlockSpec(a.shape, lambda i: (0, 0), pipeline_mode=pl.Buffered(1))
    return _call(
        body, comm, name="in_proj", grid=(L // tm,),
        in_specs=[row(K), resident(w_ssd), resident(w_att)], out_specs=[row(S_W), row(A_W), row(K)],
        out_shape=[jax.ShapeDtypeStruct((L, S_W), F32), jax.ShapeDtypeStruct((L, A_W), F32),
                   jax.ShapeDtypeStruct((L, K), _MXU)],
        scratch_shapes=[], args=(x, w_ssd, w_att))


def _matmul_tn(a, g, *, tl, tn, name, emit_bf16=False):
    L, M = a.shape
    N = g.shape[1]
    last = L // tl - 1

    def body(a_ref, g_ref, o_ref, *rest):
        @pl.when(pl.program_id(1) == 0)
        def _():
            o_ref[...] = jnp.zeros_like(o_ref)

        o_ref[...] += _mm_tn(a_ref[...], g_ref[...])
        if emit_bf16:
            @pl.when(pl.program_id(1) == last)
            def _():
                rest[0][...] = o_ref[...].astype(BF16)

    spec = pl.BlockSpec((M, tn), lambda j, l: (0, j))
    res = pl.pallas_call(
        body, name=name, grid=(N // tn, L // tl),
        in_specs=[pl.BlockSpec((tl, M), lambda j, l: (l, 0)), pl.BlockSpec((tl, tn), lambda j, l: (l, j))],
        out_specs=[spec, spec] if emit_bf16 else [spec],
        out_shape=[jax.ShapeDtypeStruct((M, N), F32)] + ([jax.ShapeDtypeStruct((M, N), BF16)] if emit_bf16 else []),
        compiler_params=_params(("arbitrary", "arbitrary")),
    )(a, g)
    return res if emit_bf16 else res[0]


def _position():
    return lax.axis_index("x"), lax.axis_index("y"), lax.axis_index("c")


def _index(px, py, pc):
    return 4 * px + 2 * py + pc


def _flip(pos, k):
    x, y, c = pos
    return ((1 - x) if (k >> 2) & 1 else x, (1 - y) if (k >> 1) & 1 else y, (1 - c) if k & 1 else c)


def _remote(src, dst, send_sem, recv_sem, peer):
    return pltpu.make_async_remote_copy(src_ref=src, dst_ref=dst, send_sem=send_sem, recv_sem=recv_sem,
                                        device_id=peer, device_id_type=MESH)


class _Flow:
    def __init__(self, kind, operand, result):
        self.kind, self.operand, self.result = kind, operand, result


class _Hosted:
    def __init__(self, operands, out_shapes, flows):
        self.operands, self.out_shapes, self.flows = operands, out_shapes, flows

    def plan(self, ins, outs, send_sems, recv_sems, local_sems):
        me = _position()
        mi = _index(*me)
        sends, recvs, locals_ = [], [], []
        for row, f in enumerate(self.flows):
            src, dst = ins[f.operand], outs[f.result]
            for k in range(1, N_DEV):
                peer = _flip(me, k)
                sems = (send_sems.at[row, k - 1], recv_sems.at[row, k - 1])
                if f.kind == "exchange":
                    sends.append(_remote(src.at[_index(*peer)], dst.at[k - 1], *sems, peer))
                    recvs.append(sends[-1])
                else:
                    sends.append(_remote(src, dst.at[mi], *sems, peer))
                    recvs.append(_remote(src, dst.at[_index(*peer)], *sems, peer))
            if f.kind == "gather":
                locals_.append(pltpu.make_async_copy(src, dst.at[mi], local_sems.at[row]))

        def start():
            for cp in locals_ + sends:
                cp.start()

        def wait():
            for cp in recvs:
                cp.wait_recv()
            for cp in sends:
                cp.wait_send()
            for cp in locals_:
                cp.wait()

        return start, wait


class _OwnerReduce:
    FIRST_STEP, SECOND_STEP = 2, 4
    SEND_STEPS, REDUCE_STEPS = (2, 3, 4, 5, 6, 7, 8, 9), (5, 6, 7, 8, 10, 11, 13, 14)

    def __init__(self, stack, target_x):
        self.stack, self.target_x = stack, target_x
        block = stack.shape[1:]
        self.chunks = len(self.SEND_STEPS)
        self.chunk_rows = block[0] // self.chunks
        self.out_shapes = [jax.ShapeDtypeStruct(block, F32), jax.ShapeDtypeStruct((2,) + block, BF16)]
        self.out_specs = [pl.BlockSpec(block, lambda *_: (0, 0), pipeline_mode=pl.Buffered(1)),
                          pl.BlockSpec(memory_space=pl.ANY)]
        dma = pltpu.SemaphoreType.DMA
        self.scratch_shapes = ([pltpu.VMEM((2,) + block, F32)] * 2 + [pltpu.VMEM(block, BF16)] * 3
                               + [dma((self.chunks,))] * 4 + [dma((2,))] * 3)

    def plan(self, i, steps, stack_ref, own_ref, recv_ref, scratch):
        assert self.FIRST_STEP <= self.SEND_STEPS[0] and self.SECOND_STEP < self.REDUCE_STEPS[0] < steps - 1
        (theirs_scr, mine_scr, first_scr, across_scr, out_scr, y_send_sems, y_recv_sems, x_send_sems, x_recv_sems,
         swap_send_sems, swap_recv_sems, mine_sems) = scratch
        x, y, c = _position()
        owners_side = x == self.target_x
        other_side = x != self.target_x
        sibling, across, owner = (x, y, 1 - c), (x, 1 - y, c), (self.target_x, y, c)
        order = (1 - y, y)
        swaps = [_remote(stack_ref.at[2 * order[j] + (1 - c)], theirs_scr.at[j], swap_send_sems.at[j], swap_recv_sems.at[j],
                         sibling) for j in range(2)]
        mine = [pltpu.make_async_copy(stack_ref.at[2 * order[j] + c], mine_scr.at[j], mine_sems.at[j]) for j in range(2)]
        chunks = range(self.chunks)
        part = [pl.ds(j * self.chunk_rows, self.chunk_rows) for j in chunks]
        y_sems = lambda j: (y_send_sems.at[j], y_recv_sems.at[j])
        to_neighbour = [_remote(first_scr.at[part[j]], across_scr.at[part[j]], *y_sems(j), across) for j in chunks]
        to_owner_y = [_remote(first_scr.at[part[j]], recv_ref.at[0, part[j]], *y_sems(j), across) for j in chunks]
        to_owner_x = [_remote(out_scr.at[part[j]], recv_ref.at[1, part[j]], x_send_sems.at[j], x_recv_sems.at[j], owner)
                      for j in chunks]

        def before():
            @pl.when(i == 0)
            def _():
                for cp in [swaps[0]] + mine:
                    cp.start()

            pl.when(i == 1)(swaps[1].start)

        def after():
            @pl.when(i == self.FIRST_STEP)
            def _():
                swaps[0].wait_recv()
                mine[0].wait()
                first_scr[...] = (mine_scr[0] + theirs_scr[0]).astype(first_scr.dtype)

            for j in chunks:
                pl.when((i == self.SEND_STEPS[j]) & other_side)(to_neighbour[j].start)
                pl.when((i == self.SEND_STEPS[j]) & owners_side)(to_owner_y[j].start)

            @pl.when(i == self.SECOND_STEP)
            def _():
                swaps[1].wait_recv()
                mine[1].wait()
                t = mine_scr[1] + theirs_scr[1]
                own_ref[...] = t
                mine_scr[1] = t

            for j in chunks:
                @pl.when((i == self.REDUCE_STEPS[j]) & other_side)
                def _(j=j):
                    to_neighbour[j].wait_recv()
                    t = mine_scr[1, part[j], :] + across_scr[part[j], :].astype(F32)
                    out_scr[part[j], :] = t.astype(out_scr.dtype)
                    to_owner_x[j].start()

            @pl.when(i == steps - 1)
            def _():
                for cp in swaps:
                    cp.wait_send()
                for j in chunks:
                    @pl.when(other_side)
                    def _(j=j):
                        to_neighbour[j].wait_send()
                        to_owner_x[j].wait_send()

                    @pl.when(owners_side)
                    def _(j=j):
                        to_owner_y[j].wait_send()
                        to_owner_y[j].wait_recv()
                        to_owner_x[j].wait_recv()

        return before, after


def _call(body, comm, *, name, grid, in_specs, out_specs, out_shape, scratch_shapes, args, reduce=None):
    semantics = ("arbitrary",) * len(grid)
    if comm is None and reduce is None:
        return pl.pallas_call(body, name=name, grid=grid, in_specs=in_specs, out_specs=out_specs, out_shape=out_shape,
                              scratch_shapes=scratch_shapes, compiler_params=_params(semantics))(*args)
    n_in, n_out, n_scr = len(args), len(out_shape), len(scratch_shapes)
    c_operands, c_shapes, flows = (comm.operands, comm.out_shapes, comm.flows) if comm else ([], [], [])
    c_in, c_out, rows = len(c_operands), len(c_shapes), max(len(flows), 1)
    r_in = 0 if reduce is None else 1

    def hosted(*refs):
        ins, refs = refs[:n_in], refs[n_in:]
        cins, refs = refs[:c_in], refs[c_in:]
        rins, refs = refs[:r_in], refs[r_in:]
        outs, refs = refs[:n_out], refs[n_out:]
        couts, refs = refs[:c_out], refs[c_out:]
        routs, refs = refs[:2 * r_in], refs[2 * r_in:]
        scr, refs = refs[:n_scr], refs[n_scr:]
        (send_sems, recv_sems, local_sems), r_scr = refs[:3], refs[3:]
        ids = [pl.program_id(d) for d in range(len(grid))]
        first, last = ids[0] == 0, ids[0] == grid[0] - 1
        for d in range(1, len(grid)):
            first, last = first & (ids[d] == 0), last & (ids[d] == grid[d] - 1)
        before = after = lambda: None
        if reduce is not None:
            before, after = reduce.plan(ids[0], grid[0], rins[0], *routs, r_scr)
        if comm is not None:
            start, wait = comm.plan(cins, couts, send_sems, recv_sems, local_sems)
            pl.when(first)(start)
        before()
        body(*ins, *outs, *scr)
        after()
        if comm is not None:
            pl.when(last)(wait)

    any_spec = pl.BlockSpec(memory_space=pl.ANY)
    sems = [pltpu.SemaphoreType.DMA((rows, N_DEV - 1)), pltpu.SemaphoreType.DMA((rows, N_DEV - 1)),
            pltpu.SemaphoreType.DMA((rows,))]
    r_operands, r_specs, r_shapes, r_scratch = ([reduce.stack], reduce.out_specs, reduce.out_shapes,
                                                reduce.scratch_shapes) if reduce else ([], [], [], [])
    return pl.pallas_call(
        hosted, name=name, grid=grid, in_specs=list(in_specs) + [any_spec] * (c_in + r_in),
        out_specs=list(out_specs) + [any_spec] * c_out + r_specs, out_shape=list(out_shape) + list(c_shapes) + r_shapes,
        scratch_shapes=list(scratch_shapes) + sems + r_scratch,
        compiler_params=_params(semantics))(*args, *c_operands, *r_operands)


def _head_row(ref, width, rep):
    hid = lax.broadcasted_iota(jnp.int32, (1, width), 1) // rep
    row = jnp.zeros((1, width), F32)
    for h in range(N_HEADS):
        row = jnp.where(hid == h, ref[h], row)
    return row


def _rows_from_above(u_b, s, ext_scr, row, col):
    down = (row - col == s).astype(_MXU)
    return jnp.concatenate([ext_scr[8 - s:16 - s, :], jnp.dot(down, u_b, preferred_element_type=F32)[8:128]], axis=0)


def _ssd_recompute(first, p_ref, halo_ref, cw_ref, cb_ref, dtb_ref, alog_ref, e_ref, ext_scr, pre=None):
    row = lax.broadcasted_iota(jnp.int32, (128, 128), 0)
    col = lax.broadcasted_iota(jnp.int32, (128, 128), 1)
    ext_scr[0:8, :] = jnp.where(first, 0.0, halo_ref[:, S_XS:S_DT])
    if pre is not None:
        ext_scr[8:16, :] = p_ref[0:8, S_XS:S_DT]
    else:
        ext_scr[8:136, :] = p_ref[:, S_XS:S_DT]
        cw = cw_ref[...]
        pre = (cb_ref[0:1, :] + cw[3:4, :] * ext_scr[8:136, :] + cw[2:3, :] * ext_scr[7:135, :]
               + cw[1:2, :] * ext_scr[6:134, :] + cw[0:1, :] * ext_scr[5:133, :])
    sg = _sigmoid(pre)
    act = pre * sg
    lane = lax.broadcasted_iota(jnp.int32, (1, 128), 1)
    A = jnp.where(lane < N_HEADS, -jnp.exp(_head_row(alog_ref, 128, 1)), 0.0)
    raw = p_ref[:, S_DT:S_DT + 128] + _head_row(dtb_ref, 128, 1)
    dt = _softplus(raw)
    dA = dt * A
    tril = (row >= col).astype(BF16)
    acs = _mm_exact_l(tril, dA)
    last = acs[127:128, :]
    ds = jnp.exp(last - acs)
    eo = jnp.exp(acs)
    E = e_ref[...]
    ex = _mm_2pass_r(jnp.concatenate([dt, ds, eo], axis=0), E)
    dt_e, ds_e, eo_e = ex[0:128], ex[128:256], ex[256:384]
    xs_c = act[:, 0:1024]
    X = xs_c * dt_e
    return dict(pre=pre, sg=sg, xs_c=xs_c, Bc=act[:, 1024:1280], Cc=act[:, 1280:1536], A=A, raw=raw, dt=dt,
                acs=acs, acsT=acs.T, eo_e=eo_e, ds_e=ds_e, dt_e=dt_e, cd_e=eo_e[127:128, :],
                X=X, Xd=X * ds_e, row=row, col=col)


def _split_halves(t):
    lo = _lo_half(CHUNK)
    return jnp.concatenate([jnp.where(lo, t, 0.0), jnp.where(lo, 0.0, t)], axis=0)


def _ssd_core(R, hprev):
    causal = R["row"] >= R["col"]
    acs, acsT, X = R["acs"], R["acsT"], R["X"]
    ydiag, yoff, snew = [], [], []
    for g in range(SSD_GROUPS):
        Bg = R["Bc"][:, g * 128:(g + 1) * 128]
        Cg = R["Cc"][:, g * 128:(g + 1) * 128]
        cols = slice(g * 512, (g + 1) * 512)
        CB = _mm_nt(Cg, Bg)
        snew.append(_mm_tn(Bg, R["Xd"][:, cols]))
        yoff.append(_mm(Cg, hprev[:, cols]))
        for j in range(4):
            h0 = g * 8 + 2 * j
            ms = [CB * jnp.exp(jnp.where(causal, acs[:, h:h + 1] - acsT[h:h + 1, :], NEG)) for h in (h0, h0 + 1)]
            ydiag.append(_mm(jnp.concatenate(ms, axis=1), _split_halves(X[:, h0 * HEAD_DIM:h0 * HEAD_DIM + 128])))
    Y = jnp.concatenate(ydiag, axis=1) + jnp.concatenate(yoff, axis=1) * R["eo_e"]
    return Y, jnp.concatenate(snew, axis=1)


def _ssd_forward_step(p_ref, halo_ref, cw_ref, cb_ref, dtb_ref, alog_ref, dsk_ref, nw_ref, e_ref,
                      y_ref, ypre_ref, hprev_ref, pre_ref, h_scr, ext_scr):
    c = pl.program_id(0)
    first = c == 0

    @pl.when(first)
    def _():
        h_scr[...] = jnp.zeros_like(h_scr)

    R = _ssd_recompute(first, p_ref, halo_ref, cw_ref, cb_ref, dtb_ref, alog_ref, e_ref, ext_scr)
    hprev = h_scr[...]
    hprev_ref[...] = hprev
    pre_ref[...] = R["pre"]
    Y, snew = _ssd_core(R, hprev)
    h_scr[...] = hprev * R["cd_e"] + snew
    Y = Y + _head_row(dsk_ref, D_SSD, HEAD_DIM) * R["xs_c"]
    ypre_ref[...] = Y
    z = p_ref[:, S_Z:S_Z + 1024]
    yf = Y * (z * _sigmoid(z))
    outs = []
    for g in range(SSD_GROUPS):
        yg = yf[:, g * 512:(g + 1) * 512]
        r = lax.rsqrt(jnp.mean(yg * yg, axis=-1, keepdims=True) + RMS_EPS)
        outs.append(yg * r)
    y_ref[:, 0:D_SSD] = (jnp.concatenate(outs, axis=1) * nw_ref[0:1, :]).astype(y_ref.dtype)


def _ssd_backward(proj_ssd, hprev_all, ypre, pre, dy, conv_w, conv_b, dt_bias, a_log, d_skip, norm_w, E, ET, comm=None):
    L = proj_ssd.shape[0]
    nc = L // CHUNK

    def body(p_ref, halo_ref, hprev_ref, ypre_ref, pre_ref, dy_ref, cw_ref, cb_ref, dtb_ref, alog_ref, dsk_ref, nw_ref, e_ref,
             et_ref, dp_ref, acc_cw_ref, acc_w_ref, acc_s_ref, dh_scr, ext_scr, ext2_scr, nxt_scr):
        i = pl.program_id(0)
        c = nc - 1 - i
        first = c == 0

        @pl.when(i == 0)
        def _():
            dh_scr[...] = jnp.zeros_like(dh_scr)
            nxt_scr[...] = jnp.zeros_like(nxt_scr)
            acc_cw_ref[...] = jnp.zeros_like(acc_cw_ref)
            acc_w_ref[...] = jnp.zeros_like(acc_w_ref)
            acc_s_ref[...] = jnp.zeros_like(acc_s_ref)

        R = _ssd_recompute(first, p_ref, halo_ref, cw_ref, cb_ref, dtb_ref, alog_ref, e_ref, ext_scr, pre_ref[...])
        hprev = hprev_ref[...]
        xs_c, X, Xd = R["xs_c"], R["X"], R["Xd"]
        acs, acsT = R["acs"], R["acsT"]
        ET = et_ref[...]
        dsk = _head_row(dsk_ref, D_SSD, HEAD_DIM)
        Y = ypre_ref[...]

        z = p_ref[:, S_Z:S_Z + 1024]
        sz = _sigmoid(z)
        silz = z * sz
        yf = Y * silz
        dyv = dy_ref[...]
        nw = nw_ref[0:1, :]
        dyf_parts, dnw_parts = [], []
        for g in range(SSD_GROUPS):
            cols = slice(g * 512, (g + 1) * 512)
            yg = yf[:, cols]
            r = lax.rsqrt(jnp.mean(yg * yg, axis=-1, keepdims=True) + RMS_EPS)
            yn = yg * r
            dyn = dyv[:, cols] * nw[:, cols]
            dnw_parts.append(_colsum(dyv[:, cols] * yn))
            dyf_parts.append(r * (dyn - yn * jnp.mean(dyn * yn, axis=-1, keepdims=True)))
        dyf = jnp.concatenate(dyf_parts, axis=1)
        dY = dyf * silz
        dz = dyf * Y * (sz * (1.0 + z * (1.0 - sz)))

        dhn = dh_scr[...]
        dYo = dY * R["eo_e"]
        causal = R["row"] >= R["col"]
        dacs = jnp.zeros((128, 128), F32)
        dacs_t = jnp.zeros((128, 128), F32)
        dxdiag, dxd, dhprev, dBs, dCs, yoff = [], [], [], [], [], []
        for g in range(SSD_GROUPS):
            Bg = R["Bc"][:, g * 128:(g + 1) * 128]
            Cg = R["Cc"][:, g * 128:(g + 1) * 128]
            cols = slice(g * 512, (g + 1) * 512)
            CB = _mm_nt(Cg, Bg)
            dCB = jnp.zeros((128, 128), F32)
            for j in range(4):
                h0 = g * 8 + 2 * j
                pc = slice(h0 * HEAD_DIM, h0 * HEAD_DIM + 128)
                dYst = _split_halves(dY[:, pc])
                dMst = _mm_nt(dYst, X[:, pc])
                mts = []
                for a, h in enumerate((h0, h0 + 1)):
                    acol = acs[:, h:h + 1]
                    arow = acsT[h:h + 1, :]
                    Lm = jnp.exp(jnp.where(causal, acol - arow, NEG))
                    M = CB * Lm
                    dM = dMst[a * 128:(a + 1) * 128]
                    dCB = dCB + dM * Lm
                    G = dM * M
                    dacs = dacs + jnp.where(R["col"] == h, jnp.sum(G, axis=1, keepdims=True), 0.0)
                    dacs_t = dacs_t + jnp.where(R["row"] == h, jnp.sum(G, axis=0, keepdims=True), 0.0)
                    mts.append(M.T)
                dxdiag.append(_mm(jnp.concatenate(mts, axis=1), dYst))
            dS = dhn[:, cols]
            dxd.append(_mm(Bg, dS))
            yoff.append(_mm(Cg, hprev[:, cols]))
            dhprev.append(_mm_tn(Cg, dYo[:, cols]))
            dCs.append(_mm_nt(dYo[:, cols], hprev[:, cols]) + _mm(dCB, Bg))
            dBs.append(_mm_tn(dCB, Cg) + _mm_nt(Xd[:, cols], dS))
        Yoff = jnp.concatenate(yoff, axis=1) * R["eo_e"]
        dXd = jnp.concatenate(dxd, axis=1)
        dX = jnp.concatenate(dxdiag, axis=1) + dXd * R["ds_e"]
        t_state = dXd * Xd
        dacs = dacs + _mm_2pass_r(dY * Yoff - t_state, ET) - dacs_t.T
        v_last = _colsum(t_state + dhn * hprev * R["cd_e"])
        dlast = _mm_exact_r(jnp.broadcast_to(v_last, (8, 1024)), ET)[0:1, :]
        dacs = dacs + jnp.where(R["row"] == 127, dlast, 0.0)
        triu = (R["col"] >= R["row"]).astype(BF16)
        da = _mm_exact_l(triu, dacs)
        ddt = da * R["A"] + _mm(dX * xs_c, ET)
        ddt_raw = ddt * _sigmoid(R["raw"])
        dxs_c = dX * R["dt_e"] + dY * dsk
        dh_scr[...] = jnp.concatenate(dhprev, axis=1) + dhn * R["cd_e"]

        dact = jnp.concatenate([dxs_c] + dBs + dCs, axis=1)
        pre, sg = R["pre"], R["sg"]
        dpre = dact * (sg * (1.0 + pre * (1.0 - sg)))
        ext2_scr[0:8, :] = dpre[120:128, :]
        ext2_scr[8:16, :] = nxt_scr[...]
        nxt_scr[...] = dpre[0:8, :]
        cw = cw_ref[...]
        u_b, dpre_b = p_ref[:, S_XS:S_DT].astype(_MXU), dpre.astype(_MXU)
        dxbc = cw[3:4, :] * dpre
        taps = [_colsum(dpre * p_ref[:, S_XS:S_DT])]
        for s in (1, 2, 3):
            up = (R["col"] - R["row"] == s).astype(_MXU)
            d_s = jnp.concatenate([jnp.dot(up, dpre_b, preferred_element_type=F32)[0:120],
                                   ext2_scr[s:8 + s, :]], axis=0)
            dxbc = dxbc + cw[3 - s:4 - s, :] * d_s
            taps.append(_colsum(dpre * _rows_from_above(u_b, s, ext_scr, R["row"], R["col"])))
        acc_cw_ref[...] += _rows8(taps[::-1] + [_colsum(dpre)])
        acc_w_ref[...] += _rows8([jnp.concatenate(dnw_parts, axis=1), _colsum(dY * xs_c)])
        acc_s_ref[...] += _rows8([_colsum(ddt_raw), _colsum(da * R["dt"])])

        lane = lax.broadcasted_iota(jnp.int32, (128, 128), 1)
        dp_ref[:, S_Z:S_Z + 1024] = dz.astype(dp_ref.dtype)
        dp_ref[:, S_XS:S_DT] = dxbc.astype(dp_ref.dtype)
        dp_ref[:, S_DT:S_DT + 128] = jnp.where(lane < N_HEADS, ddt_raw, 0.0).astype(dp_ref.dtype)
        dp_ref[:, S_DT + 128:S_W] = jnp.zeros((128, 128), dp_ref.dtype)

        @pl.when(i == nc - 1)
        def _():
            acc = acc_s_ref[...]
            dskip = _mm_exact_r(acc_w_ref[...], ET)[1:2, :]
            acc_s_ref[...] = _rows8([acc[0:1, :], acc[1:2, :] * R["A"], dskip])

    const = lambda shape: pl.BlockSpec(shape, lambda i: (0, 0))
    smem = pl.BlockSpec(memory_space=pltpu.SMEM)
    rev = lambda i: (nc - 1 - i, 0)
    return _call(
        body, comm, name="ssd_bwd", grid=(nc,),
        in_specs=[pl.BlockSpec((CHUNK, S_W), rev),
                  pl.BlockSpec((8, S_W), lambda i: (jnp.maximum((nc - 1 - i) * 16 - 1, 0), 0)),
                  pl.BlockSpec((128, 1024), rev),
                  pl.BlockSpec((CHUNK, D_SSD), rev),
                  pl.BlockSpec((CHUNK, D_XBC), rev),
                  pl.BlockSpec((CHUNK, D_SSD), rev),
                  const((4, D_XBC)), const((1, D_XBC)), smem, smem, smem, const((1, 1024)),
                  const((128, 1024)), const((1024, 128))],
        out_specs=[pl.BlockSpec((CHUNK, S_W), rev), const((8, D_XBC)), const((8, 1024)), const((8, 128))],
        out_shape=[jax.ShapeDtypeStruct((L, S_W), _MXU), jax.ShapeDtypeStruct((8, D_XBC), F32),
                   jax.ShapeDtypeStruct((8, 1024), F32), jax.ShapeDtypeStruct((8, 128), F32)],
        scratch_shapes=[pltpu.VMEM((128, 1024), F32), pltpu.VMEM((16, D_XBC), F32),
                        pltpu.VMEM((16, D_XBC), F32), pltpu.VMEM((8, D_XBC), F32)],
        args=(proj_ssd, proj_ssd, hprev_all, ypre, pre, dy, conv_w, conv_b, dt_bias, a_log, d_skip, norm_w, E, ET))


def _rope(t, tab):
    cos, sa, sb = tab[:, 0:128], tab[:, 128:256], tab[:, 256:384]
    outs = []
    for i in range(t.shape[1] // 128):
        tg = t[:, i * 128:(i + 1) * 128]
        outs.append(tg * cos + pltpu.roll(tg, 8, 1) * sa + pltpu.roll(tg, 120, 1) * sb)
    return jnp.concatenate(outs, axis=1)


def _rope_transposed(d, tab):
    cos, sa, sb = tab[:, 0:128], tab[:, 128:256], tab[:, 256:384]
    outs = []
    for i in range(d.shape[1] // 128):
        dg = d[:, i * 128:(i + 1) * 128]
        outs.append(dg * cos + pltpu.roll(dg * sa, 120, 1) + pltpu.roll(dg * sb, 8, 1))
    return jnp.concatenate(outs, axis=1)


def _lo_half(rows):
    return lax.broadcasted_iota(jnp.int32, (rows, 128), 1) < HEAD_DIM


def _native_half(rows, j):
    lo = _lo_half(rows)
    return lo if j % 2 == 0 else jnp.logical_not(lo)


def _kv_native(t, j):
    p = j // 2
    return jnp.where(_native_half(t.shape[0], j), t[:, p * 128:(p + 1) * 128], 0.0)


def _stack_heads(t, j):
    out = []
    for m in (2 * j, 2 * j + 1):
        pair = t[:, m * 128:(m + 1) * 128]
        swapped = pltpu.roll(pair, HEAD_DIM, 1)
        out += [pair, swapped] if j % 2 == 0 else [swapped, pair]
    return jnp.concatenate(out, axis=0)


def _unstack_heads(s, j):
    out = []
    for m in range(2):
        first, second = s[256 * m:256 * m + 128], s[256 * m + 128:256 * m + 256]
        if j % 2 == 0:
            out.append(first + pltpu.roll(second, HEAD_DIM, 1))
        else:
            out.append(pltpu.roll(first, HEAD_DIM, 1) + second)
    return jnp.concatenate(out, axis=1)


def _keep_native(r, j):
    return jnp.where(_native_half(r.shape[0], j), r, 0.0)


def _sink_row(sink_ref, j):
    hid = lax.broadcasted_iota(jnp.int32, (1, 4 * CHUNK), 1) // CHUNK
    row = jnp.zeros((1, 4 * CHUNK), F32)
    for hh in range(4):
        row = jnp.where(hid == hh, sink_ref[4 * j + hh], row)
    return row


def _from_current():
    si = lax.broadcasted_iota(jnp.int32, (CHUNK, 4 * CHUNK), 0)
    qi = lax.broadcasted_iota(jnp.int32, (CHUNK, 4 * CHUNK), 1) % CHUNK
    return si <= qi


def _fold(full, from_cur, pen=0.0):
    return jnp.where(from_cur, full[CHUNK:2 * CHUNK], full[0:CHUNK] + pen)


def _unfold(t, from_cur):
    c = jnp.where(from_cur, t, 0.0)
    return jnp.concatenate([t - c, c], axis=0)


def _softmax_sink(s, sink):
    mx = jnp.maximum(jnp.max(s, axis=0, keepdims=True), sink)
    p = jnp.exp(s - mx)
    esink = jnp.exp(sink - mx)
    inv = 1.0 / (jnp.sum(p, axis=0, keepdims=True) + esink)
    return p * inv, esink * inv


def _swa_inputs(blk, p_ref, prev_ref, tab_ref, ptab_ref):
    tab = tab_ref[...]
    qr = _rope(p_ref[:, A_Q:A_Q + 1024], tab) * ATT_SCALE
    kk = jnp.concatenate([_rope(prev_ref[:, 0:256], ptab_ref[...]), _rope(p_ref[:, A_K:A_K + 256], tab)], axis=0)
    vv = jnp.concatenate([prev_ref[:, 256:512], p_ref[:, A_V:A_V + 256]], axis=0)
    return tab, qr, kk, vv, jnp.where(blk > 0, 0.0, NEG)


def _swa_forward_step(sink_ref, p_ref, prev_ref, tab_ref, ptab_ref, y_ref):
    n = pl.program_id(0)
    _, qr, kk, vv, pen = _swa_inputs(n, p_ref, prev_ref, tab_ref, ptab_ref)
    from_cur = _from_current()
    outs = []
    for j in range(KV_HEADS):
        s = _fold(_mm_nt(_kv_native(kk, j), _stack_heads(qr, j)), from_cur, pen)
        P, _ = _softmax_sink(s, _sink_row(sink_ref, j))
        outs.append(_unstack_heads(_mm_tn(_unfold(P, from_cur), _kv_native(vv, j)), j))
    g = p_ref[:, A_G:A_G + 1024]
    y_ref[:, D_SSD:D_SSD + D_ATT] = (jnp.concatenate(outs, axis=1) * (g * _sigmoid(g))).astype(y_ref.dtype)


def _mixer_forward(proj_ssd, proj_att, tabs, sinks, conv_w, conv_b, dt_bias, a_log, d_skip, norm_w, E, comm=None):
    L = proj_ssd.shape[0]
    nc = L // CHUNK

    def body(p_ref, halo_ref, cw_ref, cb_ref, dtb_ref, alog_ref, dsk_ref, nw_ref, e_ref,
             sink_ref, pa_ref, prev_ref, tab_ref, ptab_ref, y_ref, ypre_ref, hprev_ref, pre_ref, h_scr, ext_scr):
        _ssd_forward_step(p_ref, halo_ref, cw_ref, cb_ref, dtb_ref, alog_ref, dsk_ref, nw_ref, e_ref,
                          y_ref, ypre_ref, hprev_ref, pre_ref, h_scr, ext_scr)
        _swa_forward_step(sink_ref, pa_ref, prev_ref, tab_ref, ptab_ref, y_ref)

    const = lambda shape: pl.BlockSpec(shape, lambda c: (0, 0))
    smem = pl.BlockSpec(memory_space=pltpu.SMEM)
    rows = lambda w: pl.BlockSpec((CHUNK, w), lambda c: (c, 0))
    return _call(
        body, comm, name="mixer_fwd", grid=(nc,),
        in_specs=[rows(S_W), pl.BlockSpec((8, S_W), lambda c: (jnp.maximum(c * 16 - 1, 0), 0)),
                  const((4, D_XBC)), const((1, D_XBC)), smem, smem, smem, const((1, 1024)), const((128, 1024)),
                  smem, rows(A_W), pl.BlockSpec((CHUNK, 512), lambda c: (jnp.maximum(c - 1, 0), 2)),
                  rows(384), pl.BlockSpec((CHUNK, 384), lambda c: (jnp.maximum(c - 1, 0), 0))],
        out_specs=[rows(D_SSD + D_ATT), rows(D_SSD), pl.BlockSpec((128, 1024), lambda c: (c, 0)), rows(D_XBC)],
        out_shape=[jax.ShapeDtypeStruct((L, D_SSD + D_ATT), _MXU), jax.ShapeDtypeStruct((L, D_SSD), F32),
                   jax.ShapeDtypeStruct((nc * 128, 1024), F32), jax.ShapeDtypeStruct((L, D_XBC), F32)],
        scratch_shapes=[pltpu.VMEM((128, 1024), F32), pltpu.VMEM((136, D_XBC), F32)],
        args=(proj_ssd, proj_ssd, conv_w, conv_b, dt_bias, a_log, d_skip, norm_w, E,
              sinks, proj_att, proj_att, tabs, tabs))


def _swa_backward(proj_att, tabs, sinks, dy, reduce=None):
    L = proj_att.shape[0]
    nb = L // CHUNK

    def body(sink_ref, p_ref, prev_ref, tab_ref, ptab_ref, dy_ref, dp_ref, dsink_ref, carry_k, carry_v):
        i = pl.program_id(0)
        n = nb - 1 - i

        @pl.when(i == 0)
        def _():
            carry_k[...] = jnp.zeros_like(carry_k)
            carry_v[...] = jnp.zeros_like(carry_v)
            dsink_ref[...] = jnp.zeros_like(dsink_ref)

        tab, qr, kk, vv, pen = _swa_inputs(n, p_ref, prev_ref, tab_ref, ptab_ref)
        from_cur = _from_current()
        g = p_ref[:, A_G:A_G + 1024]
        sgm = _sigmoid(g)
        dyv = dy_ref[...]
        do_all = dyv * (g * sgm)
        lane8 = lax.broadcasted_iota(jnp.int32, (8, 128), 1)
        hid = lax.broadcasted_iota(jnp.int32, (1, 4 * CHUNK), 1) // CHUNK
        o_parts, dq_parts = [], []
        dk_nat = [jnp.zeros((2 * CHUNK, 128), F32) for _ in range(2)]
        dv_nat = [jnp.zeros((2 * CHUNK, 128), F32) for _ in range(2)]
        dsink = jnp.zeros((8, 128), F32)
        for j in range(KV_HEADS):
            qs = _stack_heads(qr, j)
            kkb, vvb = _kv_native(kk, j), _kv_native(vv, j)
            P, psink = _softmax_sink(_fold(_mm_nt(kkb, qs), from_cur, pen), _sink_row(sink_ref, j))
            p_full = _unfold(P, from_cur)
            o_parts.append(_unstack_heads(_mm_tn(p_full, vvb), j))
            do_s = _stack_heads(do_all, j)
            dP = _fold(_mm_nt(vvb, do_s), from_cur)
            D = jnp.sum(P * dP, axis=0, keepdims=True)
            ds_full = _unfold(P * (dP - D), from_cur)
            sd = psink * D
            for hh in range(4):
                dsink = dsink + jnp.where(lane8 == 4 * j + hh, -jnp.sum(jnp.where(hid == hh, sd, 0.0)), 0.0)
            dq_parts.append(_unstack_heads(_mm_tn(ds_full, kkb), j) * ATT_SCALE)
            dk_nat[j // 2] = dk_nat[j // 2] + _keep_native(_mm(ds_full, qs), j)
            dv_nat[j // 2] = dv_nat[j // 2] + _keep_native(_mm(p_full, do_s), j)
        o = jnp.concatenate(o_parts, axis=1)
        dkk = jnp.concatenate(dk_nat, axis=1)
        dvv = jnp.concatenate(dv_nat, axis=1)
        out = dp_ref.dtype
        dp_ref[:, A_Q:A_Q + 1024] = _rope_transposed(jnp.concatenate(dq_parts, axis=1), tab).astype(out)
        dp_ref[:, A_K:A_K + 256] = _rope_transposed(dkk[CHUNK:2 * CHUNK] + carry_k[...], tab).astype(out)
        dp_ref[:, A_V:A_V + 256] = (dvv[CHUNK:2 * CHUNK] + carry_v[...]).astype(out)
        dp_ref[:, A_G:A_G + 1024] = (dyv * o * (sgm * (1.0 + g * (1.0 - sgm)))).astype(out)
        carry_k[...] = dkk[0:CHUNK]
        carry_v[...] = dvv[0:CHUNK]
        dsink_ref[...] += dsink

    rev = lambda i: (nb - 1 - i, 0)
    prev = lambda i: jnp.maximum(nb - 2 - i, 0)
    return _call(
        body, None, name="swa_bwd", grid=(nb,),
        in_specs=[pl.BlockSpec(memory_space=pltpu.SMEM),
                  pl.BlockSpec((CHUNK, A_W), rev),
                  pl.BlockSpec((CHUNK, 512), lambda i: (prev(i), 2)),
                  pl.BlockSpec((CHUNK, 384), rev),
                  pl.BlockSpec((CHUNK, 384), lambda i: (prev(i), 0)),
                  pl.BlockSpec((CHUNK, D_ATT), lambda i: (nb - 1 - i, 1))],
        out_specs=[pl.BlockSpec((CHUNK, A_W), rev), pl.BlockSpec((8, 128), lambda i: (0, 0))],
        out_shape=[jax.ShapeDtypeStruct((L, A_W), _MXU), jax.ShapeDtypeStruct((8, 128), F32)],
        scratch_shapes=[pltpu.VMEM((CHUNK, 256), F32), pltpu.VMEM((CHUNK, 256), F32)],
        args=(sinks, proj_att, proj_att, tabs, tabs, dy), reduce=reduce)


def _head(y, x, target, w_out, ln_g, ln_b, *, tm):
    L = x.shape[0]
    nsteps = L // tm

    def body(y_ref, x_ref, t_ref, wo_ref, g_ref, b_ref, dr_ref, dy_ref, acc_ref):
        i = pl.program_id(0)

        @pl.when(i == 0)
        def _():
            acc_ref[...] = jnp.zeros_like(acc_ref)

        r = ALPHA * x_ref[...] + _mm(y_ref[...], wo_ref[...])
        mu = jnp.mean(r, axis=-1, keepdims=True)
        d = r - mu
        rstd = lax.rsqrt(jnp.mean(d * d, axis=-1, keepdims=True) + LN_EPS)
        xh = d * rstd
        gam = g_ref[0:1, :]
        e = xh * gam + b_ref[0:1, :] - t_ref[...]
        dout = e * (1.0 / D_MODEL)
        dxh = dout * gam
        dr = rstd * (dxh - jnp.mean(dxh, axis=-1, keepdims=True)
                     - xh * jnp.mean(dxh * xh, axis=-1, keepdims=True))
        dr_ref[...] = dr
        dy_ref[...] = _mm_nt(dr, wo_ref[...])
        acc_ref[...] += _rows8([_colsum(dout * xh), _colsum(dout), _colsum(e * e) * (0.5 / D_MODEL)])

        @pl.when(i == nsteps - 1)
        def _():
            acc = acc_ref[...]
            tot = jnp.sum(acc[2:3, :])
            rid = lax.broadcasted_iota(jnp.int32, (8, 1024), 0)
            acc_ref[...] = jnp.where(rid == 3, tot, acc)

    const = lambda shape: pl.BlockSpec(shape, lambda i: (0, 0))
    row = lambda w: pl.BlockSpec((tm, w), lambda i: (i, 0))
    return pl.pallas_call(
        body, name="head", grid=(nsteps,),
        in_specs=[row(2048), row(1024), row(1024), const((2048, 1024)), const((1, 1024)), const((1, 1024))],
        out_specs=[row(1024), row(2048), const((8, 1024))],
        out_shape=[jax.ShapeDtypeStruct((L, D_MODEL), F32), jax.ShapeDtypeStruct((L, 2048), F32),
                   jax.ShapeDtypeStruct((8, 1024), F32)],
        compiler_params=_params(("arbitrary",)),
    )(y, x, target, w_out, ln_g, ln_b)


def _gather_w_in(w_shard, positions):
    R = w_shard.shape[0]
    halves = (pl.ds(0, R // 2), pl.ds(R // 2, R // 2))
    any_spec = pl.BlockSpec(memory_space=pl.ANY)
    vmem = pl.BlockSpec(memory_space=pltpu.VMEM)

    def body(in_ref, pos_ref, inv_ref, out_ref, tab_ref, send_sems, recv_sems, local_sem):
        x, y, c = _position()

        def slot(p, half=None):
            s = out_ref.at[_index(*p)]
            return s if half is None else s.at[halves[half]]

        def same_core(p):
            return (p[0], p[1], c)

        def other_core(p):
            return (p[0], p[1], 1 - c)

        me, xn, yn, dg = (x, y), (1 - x, y), (x, 1 - y), (1 - x, 1 - y)

        def copy(k, dst, to, src=None):
            return _remote(dst if src is None else src, dst, send_sems.at[k], recv_sems.at[k], to)

        local = pltpu.make_async_copy(in_ref, slot(same_core(me)), local_sem)
        local.start()
        own = [copy(0, slot(same_core(me)), other_core(me), in_ref), copy(1, slot(same_core(me)), same_core(xn), in_ref),
               copy(2, slot(same_core(me)), same_core(yn), in_ref)]
        for cp in own:
            cp.start()
        _rope_tables(pos_ref, inv_ref, tab_ref)
        copy(1, slot(same_core(xn)), same_core(xn)).wait_recv()
        passed = [copy(4, slot(same_core(xn), 1), same_core(yn)), copy(5, slot(same_core(xn)), other_core(me))]
        for cp in passed:
            cp.start()
        copy(2, slot(same_core(yn)), same_core(yn)).wait_recv()
        more = [copy(3, slot(same_core(yn), 0), same_core(xn)), copy(6, slot(same_core(yn)), other_core(me))]
        for cp in more:
            cp.start()
        passed += more
        for k, half in ((3, 0), (4, 1)):
            copy(k, slot(same_core(dg), half), same_core(xn)).wait_recv()
            fwd = copy(7 + half, slot(same_core(dg), half), other_core(me))
            fwd.start()
            passed.append(fwd)
        copy(0, slot(other_core(me)), other_core(me)).wait_recv()
        copy(5, slot(other_core(xn)), other_core(me)).wait_recv()
        copy(6, slot(other_core(yn)), other_core(me)).wait_recv()
        for half in (0, 1):
            copy(7 + half, slot(other_core(dg), half), other_core(me)).wait_recv()
        for cp in own + passed:
            cp.wait_send()
        local.wait()

    return pl.pallas_call(
        body, name="gather_w_in", in_specs=[any_spec, vmem, vmem], out_specs=[any_spec, vmem],
        out_shape=[jax.ShapeDtypeStruct((N_DEV,) + w_shard.shape, w_shard.dtype),
                   jax.ShapeDtypeStruct((positions.size, 384), F32)],
        scratch_shapes=[pltpu.SemaphoreType.DMA((9,)), pltpu.SemaphoreType.DMA((9,)), pltpu.SemaphoreType.DMA],
        compiler_params=_params(),
    )(w_shard, positions, jnp.asarray(ROPE_INV)[None, :])


def _input_gradient(d_ssd, d_att, w_ssd, w_att, dr, *, tm, comm=None, reduce=None):
    L = dr.shape[0]

    def body(ds_ref, da_ref, ws_ref, wa_ref, dr_ref, o_ref):
        o_ref[...] = ALPHA * dr_ref[...] + _mm_nt(ds_ref[...], ws_ref[...]) + _mm_nt(da_ref[...], wa_ref[...])

    row = lambda w: pl.BlockSpec((tm, w), lambda i: (i, 0))
    resident = lambda a: pl.BlockSpec(a.shape, lambda i: (0, 0), pipeline_mode=pl.Buffered(1))
    return _call(body, comm, name="dx", grid=(L // tm,),
                 in_specs=[row(S_W), row(A_W), resident(w_ssd), resident(w_att), row(D_MODEL)],
                 out_specs=[row(D_MODEL)], out_shape=[jax.ShapeDtypeStruct((L, D_MODEL), F32)],
                 scratch_shapes=[], args=(d_ssd, d_att, w_ssd, w_att, dr), reduce=reduce)


SHARD_COLS = D_IN_PROJ // N_DEV
SPLIT = N_SSD_REAL - 4 * SHARD_COLS
RELAYOUT_ROWS = 256


def _unpack_w_in(w_all):
    def body(g_ref, ws_ref, wa_ref):
        for j in range(4):
            ws_ref[:, SHARD_COLS * j:SHARD_COLS * (j + 1)] = g_ref[j]
        ws_ref[:, 4 * SHARD_COLS:N_SSD_REAL] = g_ref[4, :, 0:SPLIT]
        ws_ref[:, N_SSD_REAL:S_W] = jnp.zeros((RELAYOUT_ROWS, S_W - N_SSD_REAL), ws_ref.dtype)
        wa_ref[:, 0:SHARD_COLS - SPLIT] = g_ref[4, :, SPLIT:SHARD_COLS]
        for j in range(5, N_DEV):
            lo = SHARD_COLS * (j - 4) - SPLIT
            wa_ref[:, lo:lo + SHARD_COLS] = g_ref[j]

    return pl.pallas_call(
        body, name="unpack_w_in", grid=(D_MODEL // RELAYOUT_ROWS,),
        in_specs=[pl.BlockSpec((N_DEV, RELAYOUT_ROWS, SHARD_COLS), lambda i: (0, i, 0))],
        out_specs=[pl.BlockSpec((RELAYOUT_ROWS, S_W), lambda i: (i, 0)), pl.BlockSpec((RELAYOUT_ROWS, A_W), lambda i: (i, 0))],
        out_shape=[jax.ShapeDtypeStruct((D_MODEL, S_W), w_all.dtype), jax.ShapeDtypeStruct((D_MODEL, A_W), w_all.dtype)],
        compiler_params=_params(("arbitrary",)),
    )(w_all)


def _dw_in(xb, d, half, tail=None, *, tl=1024):
    L, N = d.shape
    steps = L // tl

    def body(x_ref, d_ref, *refs):
        if half == 0:
            p_ref, tail_ref, acc = refs
        else:
            t_ref, p_ref, acc = refs
        l = pl.program_id(0)

        @pl.when(l == 0)
        def _():
            acc[...] = jnp.zeros_like(acc)

        acc[...] += _mm_tn(x_ref[...], d_ref[...])

        @pl.when(l == steps - 1)
        def _():
            if half == 0:
                tail_ref[...] = acc[:, S_DT:S_W]
            for j in range(4):
                if half == 0:
                    pieces = [(0, acc[:, SHARD_COLS * j:SHARD_COLS * (j + 1)])]
                elif j == 0:
                    pieces = [(0, t_ref[:, 4 * SHARD_COLS - S_DT:N_SSD_REAL - S_DT]), (SPLIT, acc[:, 0:SHARD_COLS - SPLIT])]
                else:
                    lo = SHARD_COLS * j - SPLIT
                    pieces = [(0, acc[:, lo:lo + SHARD_COLS])]
                for off, blk in pieces:
                    p_ref[j, :, off:off + blk.shape[1]] = blk

    once = pl.Buffered(1)
    whole = lambda shape: pl.BlockSpec(shape, lambda l: (0,) * len(shape), pipeline_mode=once)
    in_specs = [pl.BlockSpec((tl, D_MODEL), lambda l: (l, 0)), pl.BlockSpec((tl, N), lambda l: (l, 0))]
    args = [xb, d]
    out_shape = [jax.ShapeDtypeStruct((4, D_MODEL, SHARD_COLS), F32)]
    if half == 0:
        out_shape.append(jax.ShapeDtypeStruct((D_MODEL, S_W - S_DT), F32))
    else:
        in_specs.append(whole(tail.shape))
        args.append(tail)
    return pl.pallas_call(
        body, name="dw_in_%d" % half, grid=(steps,), in_specs=in_specs,
        out_specs=[whole(o.shape) for o in out_shape], out_shape=out_shape,
        scratch_shapes=[pltpu.VMEM((D_MODEL, N), F32)], compiler_params=_params(("arbitrary",)),
    )(*args)


def _adamw_math(w, g, m, v):
    m = ADAM_B1 * m + (1.0 - ADAM_B1) * g
    v = ADAM_B2 * v + (1.0 - ADAM_B2) * (g * g)
    m_hat = m / (1.0 - ADAM_B1 ** ADAM_STEP)
    v_hat = v / (1.0 - ADAM_B2 ** ADAM_STEP)
    delta = -ADAM_LR * (m_hat / (jnp.sqrt(v_hat) + ADAM_EPS) + ADAM_WD * w)
    return delta, m, v


def _adamw_shard(g_own, recv, w, m, v, *, rows, name):
    R, C = g_own.shape

    def body(g_ref, r_ref, w_ref, m_ref, v_ref, go_ref, d_ref, mo_ref, vo_ref):
        g = g_ref[...]
        for k in range(N_DEV - 1):
            g = g + r_ref[k].astype(F32)
        d, mn, vn = _adamw_math(w_ref[...], g, m_ref[...], v_ref[...])
        go_ref[...] = g
        d_ref[...] = d
        mo_ref[...] = mn
        vo_ref[...] = vn

    blk = pl.BlockSpec((rows, C), lambda i: (i, 0))
    return pl.pallas_call(
        body, name=name, grid=(R // rows,),
        in_specs=[blk, pl.BlockSpec((N_DEV - 1, rows, C), lambda i: (0, i, 0)), blk, blk, blk],
        out_specs=[blk] * 4, out_shape=[jax.ShapeDtypeStruct((R, C), F32)] * 4,
        compiler_params=_params(("arbitrary",)),
    )(g_own, recv, w, m, v)


def _minor_rows_view(a):
    return jnp.transpose(a, (2, 0, 1)).reshape(SHARD_COLS * 8, 128)


def _from_minor_rows_view(v):
    return jnp.transpose(v.reshape(SHARD_COLS, 8, 128), (1, 2, 0)).reshape(1, D_MODEL, SHARD_COLS)


def _adamw_w_in(is_lo, own_lo, own_hi, recv_lo, recv_hi, w, m, v):
    C = SHARD_COLS
    pad = -C % 128

    def body(lo_ref, ol_ref, oh_ref, rl_ref, rh_ref, w_ref, m_ref, v_ref, go_ref, d_ref, mo_ref, vo_ref):
        lo = lo_ref[0] == 1
        for q in range(D_MODEL // 128):
            band = pl.ds(q * 128, 128)
            g = jnp.where(lo, ol_ref[band, :], oh_ref[band, :])
            for k in range(2):
                g = g + jnp.where(lo, rl_ref[k, band, :], rh_ref[k, band, :]).astype(F32)
            g = jnp.pad(g, ((0, 0), (0, pad))).T[0:C]
            rows = pl.ds(q, C, stride=8)
            d, mn, vn = _adamw_math(w_ref[rows, :], g, m_ref[rows, :], v_ref[rows, :])
            go_ref[rows, :] = g
            d_ref[rows, :] = d
            mo_ref[rows, :] = mn
            vo_ref[rows, :] = vn

    return pl.pallas_call(
        body, name="adamw_w_in", out_shape=[jax.ShapeDtypeStruct(w.shape, F32)] * 4,
        in_specs=[pl.BlockSpec(memory_space=pltpu.SMEM)] + [pl.BlockSpec(memory_space=pltpu.VMEM)] * 7,
        out_specs=[pl.BlockSpec(memory_space=pltpu.VMEM)] * 4,
        compiler_params=_params(),
    )(is_lo, own_lo, own_hi, recv_lo, recv_hi, w, m, v)


SMALL = ("conv_b", "dt_bias", "a_log", "d_skip", "ssd_norm_w", "attn_sinks", "ln_g", "ln_b")


def _adamw_small(gathered, params):
    n_p = len(SMALL)

    def body(*refs):
        acc = []
        for r in refs[:5]:
            t = r[0]
            for k in range(1, N_DEV):
                t = t + r[k]
            acc.append(t)
        head, conv, norm, scal, sink = acc
        grads = dict(conv_b=conv[4:5, :], dt_bias=scal[0:1, 0:N_HEADS], a_log=scal[1:2, 0:N_HEADS],
                     d_skip=scal[2:3, 0:N_HEADS], ssd_norm_w=norm[0:1, :], attn_sinks=sink[0:1, 0:N_HEADS],
                     ln_g=head[0:1, :], ln_b=head[1:2, :])
        wmv = refs[5:5 + 3 * n_p]
        outs = refs[5 + 3 * n_p:]
        outs[0][...] = head[3:4, 0:1]
        outs[1][...] = conv[0:4, :]
        for i, name in enumerate(SMALL):
            w_ref, m_ref, v_ref = wmv[3 * i:3 * i + 3]
            g = grads[name]
            d, mn, vn = _adamw_math(w_ref[...], g, m_ref[...], v_ref[...])
            for o_ref, val in zip(outs[2 + 4 * i:6 + 4 * i], (g, d, mn, vn)):
                o_ref[...] = val

    flat = [a for name in SMALL for a in params[name]]
    out_shape = [jax.ShapeDtypeStruct((1, 1), F32), jax.ShapeDtypeStruct((4, D_XBC), F32)]
    for name in SMALL:
        out_shape += [jax.ShapeDtypeStruct(params[name][0].shape, F32)] * 4
    res = pl.pallas_call(body, name="adamw_small", out_shape=out_shape, compiler_params=_params())(*gathered, *flat)
    return res[0], res[1], {name: res[2 + 4 * i:6 + 4 * i] for i, name in enumerate(SMALL)}


def _adamw_plain(g, w, m, v):
    def body(g_ref, w_ref, m_ref, v_ref, d_ref, mo_ref, vo_ref):
        d, mn, vn = _adamw_math(w_ref[...], g_ref[...], m_ref[...], v_ref[...])
        d_ref[...] = d
        mo_ref[...] = mn
        vo_ref[...] = vn

    return pl.pallas_call(
        body, name="adamw_conv_w", out_shape=[jax.ShapeDtypeStruct(w.shape, F32)] * 3,
        compiler_params=_params(),
    )(g, w, m, v)


def _lane_pattern(fn):
    return np.asarray([fn(l % HEAD_DIM) for l in range(128)], np.float32)


ROPE_INV = _lane_pattern(lambda r: ROPE_THETA ** (-2.0 * (r % 8) / ROPE_DIM) if r < ROPE_DIM else 0.0)


def _rope_tables(pos_ref, inv_ref, tab_ref):
    lane = lax.broadcasted_iota(jnp.int32, (1, 128), 1) % HEAD_DIM
    upper = jnp.where((lane >= ROPE_DIM // 2) & (lane < ROPE_DIM), 1.0, 0.0)
    lower = jnp.where(lane < ROPE_DIM // 2, -1.0, 0.0)

    def block(r, carry):
        rows = pl.ds(pl.multiple_of(r * CHUNK, CHUNK), CHUNK)
        pos = jnp.broadcast_to(pos_ref[pl.ds(r, 1), :].astype(F32), (CHUNK, 128)).T
        ang = pos * inv_ref[...]
        sn = jnp.sin(ang)
        tab_ref[rows, 0:128] = jnp.cos(ang)
        tab_ref[rows, 128:256] = sn * upper
        tab_ref[rows, 256:384] = sn * lower
        return carry

    lax.fori_loop(0, pos_ref.shape[0], block, 0)


def _expansion():
    E = np.arange(1024)[None, :] // HEAD_DIM == np.arange(128)[:, None]
    return jnp.asarray(E, BF16), jnp.asarray(E.T, BF16)


def _ssd_args(conv_w, conv_b, dt_bias, a_log, d_skip, norm_w, E):
    return (conv_w, conv_b, dt_bias.reshape(-1), a_log.reshape(-1), d_skip.reshape(-1), norm_w, E)


def kernel(x, positions, w_in, conv_w, conv_b, dt_bias, a_log, d_skip, ssd_norm_w, attn_sinks, w_out, ln_g, ln_b, loss_target, m_w_in, m_conv_w, m_conv_b, m_dt_bias, m_a_log, m_d_skip, m_ssd_norm_w, m_attn_sinks, m_w_out, m_ln_g, m_ln_b, v_w_in, v_conv_w, v_conv_b, v_dt_bias, v_a_log, v_d_skip, v_ssd_norm_w, v_attn_sinks, v_w_out, v_ln_g, v_ln_b):
    me = _index(*_position())
    x0, target = x[0], loss_target[0]
    bf16_shard = lambda shape: jax.ShapeDtypeStruct(shape, BF16)
    E, ET = _expansion()
    sinks = attn_sinks.reshape(-1)

    w_all, tabs = _gather_w_in(w_in[0].astype(BF16), positions[0].reshape(-1, 128))
    w_ssd, w_att = _unpack_w_in(w_all)
    gather_conv_w = _Hosted([conv_w[0]], [jax.ShapeDtypeStruct((N_DEV,) + conv_w.shape[1:], F32)],
                            [_Flow("gather", 0, 0)])

    proj_ssd, proj_att, xb, conv_w_all = _in_proj(x0, w_ssd, w_att, tm=512, comm=gather_conv_w)
    conv_w_f = jnp.transpose(conv_w_all, (1, 0, 2)).reshape(4, D_XBC)
    ssd_args = _ssd_args(conv_w_f, conv_b, dt_bias, a_log, d_skip, ssd_norm_w, E)
    gather_w_out = _Hosted([w_out[0].astype(BF16)], [bf16_shard((N_DEV, 256, D_MODEL))], [_Flow("gather", 0, 0)])
    y, ypre, hprev, pre, w_out_all = _mixer_forward(proj_ssd, proj_att, tabs, sinks, *ssd_args, comm=gather_w_out)
    w_out_f = w_out_all.reshape(2 * D_MODEL, D_MODEL)
    dr, dy, acc_head = _head(y, x0, target, w_out_f, ln_g, ln_b, tm=512)

    dw_out, dw_out_bf16 = _matmul_tn(y, dr, tl=1024, tn=D_MODEL, name="dw_out", emit_bf16=True)
    own_out = lax.dynamic_index_in_dim(dw_out.reshape(N_DEV, 256, D_MODEL), me, axis=0, keepdims=False)
    send_out = _Hosted([dw_out_bf16.reshape(N_DEV, 256, D_MODEL)], [bf16_shard((N_DEV - 1, 256, D_MODEL))],
                       [_Flow("exchange", 0, 0)])
    d_ssd, acc_cw, acc_w, acc_s, recv_out = _ssd_backward(proj_ssd, hprev, ypre, pre, dy, *ssd_args, ET, comm=send_out)
    stack_lo, dw_dt_block = _dw_in(xb, d_ssd, 0)
    d_att, dsink, own_lo, recv_lo = _swa_backward(proj_att, tabs, sinks, dy, reduce=_OwnerReduce(stack_lo, 0))
    (stack_hi,) = _dw_in(xb, d_att, 1, dw_dt_block)
    accs = [acc_head, acc_cw, acc_w, acc_s, dsink]
    gather_accs = _Hosted(accs, [jax.ShapeDtypeStruct((N_DEV,) + a.shape, F32) for a in accs],
                          [_Flow("gather", i, i) for i in range(5)])
    dx, *gathered, own_hi, recv_hi = _input_gradient(d_ssd, d_att, w_ssd, w_att, dr, tm=256, comm=gather_accs,
                                                     reduce=_OwnerReduce(stack_hi, 1))
    is_lo = (me < 4).reshape(1).astype(jnp.int32)

    g_in, d_in, nm_in, nv_in = [_from_minor_rows_view(r) for r in _adamw_w_in(
        is_lo, own_lo, own_hi, recv_lo, recv_hi, _minor_rows_view(w_in), _minor_rows_view(m_w_in), _minor_rows_view(v_w_in))]
    g_out, d_out, nm_out, nv_out = _adamw_shard(own_out, recv_out, w_out[0], m_w_out[0], v_w_out[0],
                                                rows=256, name="adamw_w_out")
    loss, g_conv_w, small = _adamw_small(gathered, dict(
        conv_b=(conv_b, m_conv_b, v_conv_b), dt_bias=(dt_bias, m_dt_bias, v_dt_bias), a_log=(a_log, m_a_log, v_a_log),
        d_skip=(d_skip, m_d_skip, v_d_skip), ssd_norm_w=(ssd_norm_w, m_ssd_norm_w, v_ssd_norm_w),
        attn_sinks=(attn_sinks, m_attn_sinks, v_attn_sinks), ln_g=(ln_g, m_ln_g, v_ln_g), ln_b=(ln_b, m_ln_b, v_ln_b)))
    g_cw = lax.dynamic_slice_in_dim(g_conv_w, me * (D_XBC // N_DEV), D_XBC // N_DEV, axis=1)
    d_cw, nm_cw, nv_cw = _adamw_plain(g_cw, conv_w[0], m_conv_w[0], v_conv_w[0])

    def leaves(i, big_in, cw, big_out):
        mid = [small[k][i] for k in ("conv_b", "dt_bias", "a_log", "d_skip", "ssd_norm_w", "attn_sinks")]
        return [big_in, cw[None]] + mid + [big_out[None], small["ln_g"][i], small["ln_b"][i]]

    return (loss.reshape(()), dx[None], *leaves(0, g_in, g_cw, g_out), *leaves(1, d_in, d_cw, d_out),
            *leaves(2, nm_in, nm_cw, nm_out), *leaves(3, nv_in, nv_cw, nv_out))
```

```python
import jax
import jax.numpy as jnp
from jax import lax
from jax.experimental import pallas as pl
from jax.experimental.pallas import tpu as pltpu
import numpy as np

F32 = jnp.float32
BF16 = jnp.bfloat16
_MXU = jnp.bfloat16

N_DEV = 8
D_MODEL = 1024
D_SSD = 1024
D_ATT = 1024
HEAD_DIM = 64
N_HEADS = 16
SSD_GROUPS = 2
KV_HEADS = 4
CHUNK = 128
D_XBC = 1536
D_IN_PROJ = 5136
ROPE_DIM = 16
ROPE_THETA = 500000.0
ALPHA = (2.0 * 1) ** 0.25
LN_EPS = 1e-5
RMS_EPS = 1e-5
ATT_SCALE = HEAD_DIM ** -0.5
NEG = -1e30

S_Z, S_XS, S_B, S_C, S_DT, S_W = 0, 1024, 2048, 2304, 2560, 2816
N_SSD_REAL = 2576
A_Q, A_K, A_V, A_G, A_W = 0, 1024, 1280, 1536, 2560

ADAM_LR = 0.001
ADAM_B1 = 0.9
ADAM_B2 = 0.999
ADAM_EPS = 1e-08
ADAM_WD = 0.01
ADAM_STEP = 10

VMEM_LIMIT = 48 * 1024 * 1024
MESH = pl.DeviceIdType.MESH


def _params(sem=None):
    return pltpu.CompilerParams(dimension_semantics=sem, vmem_limit_bytes=VMEM_LIMIT)


def _mm(a, b):
    return jnp.dot(a.astype(_MXU), b.astype(_MXU), preferred_element_type=F32)


def _mm_nt(a, b):
    return lax.dot_general(a.astype(_MXU), b.astype(_MXU), (((1,), (1,)), ((), ())),
                           preferred_element_type=F32)


def _mm_tn(a, b):
    return lax.dot_general(a.astype(_MXU), b.astype(_MXU), (((0,), (0,)), ((), ())),
                           preferred_element_type=F32)


def _split3(v):
    hi = v.astype(BF16)
    r = v - hi.astype(F32)
    mid = r.astype(BF16)
    lo = (r - mid.astype(F32)).astype(BF16)
    return hi, mid, lo


def _mm_exact_r(v, p01):
    hi, mid, lo = _split3(v)
    d = lambda a: jnp.dot(a, p01, preferred_element_type=F32)
    return d(hi) + d(mid) + d(lo)


def _mm_exact_l(p01, v):
    hi, mid, lo = _split3(v)
    d = lambda a: jnp.dot(p01, a, preferred_element_type=F32)
    return d(hi) + d(mid) + d(lo)


def _mm_2pass_r(v, p01):
    hi = v.astype(BF16)
    lo = (v - hi.astype(F32)).astype(BF16)
    return jnp.dot(hi, p01, preferred_element_type=F32) + jnp.dot(lo, p01, preferred_element_type=F32)


def _sigmoid(x):
    return 1.0 / (1.0 + jnp.exp(-x))


def _softplus(x):
    e = jnp.exp(-jnp.abs(x))
    u = 1.0 + e
    log1p = jnp.where(u == 1.0, e, jnp.log(u) * (e / (u - 1.0)))
    return jnp.maximum(x, 0.0) + log1p


def _rows8(rows):
    n = rows[0].shape[1]
    rid = lax.broadcasted_iota(jnp.int32, (8, n), 0)
    out = jnp.zeros((8, n), F32)
    for k, r in enumerate(rows):
        out = out + jnp.where(rid == k, r, 0.0)
    return out


def _colsum(a):
    return jnp.sum(a, axis=0, keepdims=True)


def _in_proj(x, w_ssd, w_att, *, tm, comm=None):
    L, K = x.shape

    def body(x_ref, ws_ref, wa_ref, ps_ref, pa_ref, xb_ref):
        xb = x_ref[...].astype(_MXU)
        xb_ref[...] = xb
        ps_ref[...] = jnp.dot(xb, ws_ref[...], preferred_element_type=F32)
        pa_ref[...] = jnp.dot(xb, wa_ref[...], preferred_element_type=F32)

    row = lambda w: pl.BlockSpec((tm, w), lambda i: (i, 0))
    resident = lambda a: pl.BlockSpec(a.shape, lambda i: (0, 0), pipeline_mode=pl.Buffered(1))
    return _call(
        body, comm, name="in_proj", grid=(L // tm,),
        in_specs=[row(K), resident(w_ssd), resident(w_att)], out_specs=[row(S_W), row(A_W), row(K)],
        out_shape=[jax.ShapeDtypeStruct((L, S_W), F32), jax.ShapeDtypeStruct((L, A_W), F32),
                   jax.ShapeDtypeStruct((L, K), _MXU)],
        scratch_shapes=[], args=(x, w_ssd, w_att))


def _matmul_tn(a, g, *, tl, tn, name, emit_bf16=False):
    L, M = a.shape
    N = g.shape[1]
    last = L // tl - 1

    def body(a_ref, g_ref, o_ref, *rest):
        @pl.when(pl.program_id(1) == 0)
        def _():
            o_ref[...] = jnp.zeros_like(o_ref)

        o_ref[...] += _mm_tn(a_ref[...], g_ref[...])
        if emit_bf16:
            @pl.when(pl.program_id(1) == last)
            def _():
                rest[0][...] = o_ref[...].astype(BF16)

    spec = pl.BlockSpec((M, tn), lambda j, l: (0, j))
    res = pl.pallas_call(
        body, name=name, grid=(N // tn, L // tl),
        in_specs=[pl.BlockSpec((tl, M), lambda j, l: (l, 0)), pl.BlockSpec((tl, tn), lambda j, l: (l, j))],
        out_specs=[spec, spec] if emit_bf16 else [spec],
        out_shape=[jax.ShapeDtypeStruct((M, N), F32)] + ([jax.ShapeDtypeStruct((M, N), BF16)] if emit_bf16 else []),
        compiler_params=_params(("arbitrary", "arbitrary")),
    )(a, g)
    return res if emit_bf16 else res[0]


def _position():
    return lax.axis_index("x"), lax.axis_index("y"), lax.axis_index("c")


def _index(px, py, pc):
    return 4 * px + 2 * py + pc


def _flip(pos, k):
    x, y, c = pos
    return ((1 - x) if (k >> 2) & 1 else x, (1 - y) if (k >> 1) & 1 else y, (1 - c) if k & 1 else c)


def _remote(src, dst, send_sem, recv_sem, peer):
    return pltpu.make_async_remote_copy(src_ref=src, dst_ref=dst, send_sem=send_sem, recv_sem=recv_sem,
                                        device_id=peer, device_id_type=MESH)


class _Flow:
    def __init__(self, kind, operand, result):
        self.kind, self.operand, self.result = kind, operand, result


class _Hosted:
    def __init__(self, operands, out_shapes, flows):
        self.operands, self.out_shapes, self.flows = operands, out_shapes, flows

    def plan(self, ins, outs, send_sems, recv_sems, local_sems):
        me = _position()
        mi = _index(*me)
        sends, recvs, locals_ = [], [], []
        for row, f in enumerate(self.flows):
            src, dst = ins[f.operand], outs[f.result]
            for k in range(1, N_DEV):
                peer = _flip(me, k)
                sems = (send_sems.at[row, k - 1], recv_sems.at[row, k - 1])
                if f.kind == "exchange":
                    sends.append(_remote(src.at[_index(*peer)], dst.at[k - 1], *sems, peer))
                    recvs.append(sends[-1])
                else:
                    sends.append(_remote(src, dst.at[mi], *sems, peer))
                    recvs.append(_remote(src, dst.at[_index(*peer)], *sems, peer))
            if f.kind == "gather":
                locals_.append(pltpu.make_async_copy(src, dst.at[mi], local_sems.at[row]))

        def start():
            for cp in locals_ + sends:
                cp.start()

        def wait():
            for cp in recvs:
                cp.wait_recv()
            for cp in sends:
                cp.wait_send()
            for cp in locals_:
                cp.wait()

        return start, wait


class _OwnerReduce:
    TIGHT = (2, 4, (2, 4, 6, 8), (5, 8, 10, 13))
    RELAXED = (3, 6, (3, 6, 9, 12), (9, 12, 15, 18))

    def __init__(self, stack, target_x, schedule):
        self.stack, self.target_x = stack, target_x
        self.FIRST_STEP, self.SECOND_STEP, self.SEND_STEPS, self.REDUCE_STEPS = schedule
        block = stack.shape[1:]
        self.chunks = len(self.SEND_STEPS)
        self.chunk_rows = block[0] // self.chunks
        self.out_shapes = [jax.ShapeDtypeStruct(block, F32), jax.ShapeDtypeStruct((2,) + block, BF16)]
        self.out_specs = [pl.BlockSpec(block, lambda *_: (0, 0), pipeline_mode=pl.Buffered(1)),
                          pl.BlockSpec(memory_space=pl.ANY)]
        dma = pltpu.SemaphoreType.DMA
        self.scratch_shapes = ([pltpu.VMEM((2,) + block, F32)] * 2 + [pltpu.VMEM(block, BF16)] * 3
                               + [dma((self.chunks,))] * 4 + [dma((2,))] * 3)

    def plan(self, i, steps, stack_ref, own_ref, recv_ref, scratch):
        assert self.FIRST_STEP <= self.SEND_STEPS[0] and self.SECOND_STEP < self.REDUCE_STEPS[0]
        assert self.REDUCE_STEPS[-1] < steps - 1
        (theirs_scr, mine_scr, first_scr, across_scr, out_scr, y_send_sems, y_recv_sems, x_send_sems, x_recv_sems,
         swap_send_sems, swap_recv_sems, mine_sems) = scratch
        x, y, c = _position()
        owners_side = x == self.target_x
        other_side = x != self.target_x
        sibling, across, owner = (x, y, 1 - c), (x, 1 - y, c), (self.target_x, y, c)
        order = (1 - y, y)
        swaps = [_remote(stack_ref.at[2 * order[j] + (1 - c)], theirs_scr.at[j], swap_send_sems.at[j], swap_recv_sems.at[j],
                         sibling) for j in range(2)]
        mine = [pltpu.make_async_copy(stack_ref.at[2 * order[j] + c], mine_scr.at[j], mine_sems.at[j]) for j in range(2)]
        chunks = range(self.chunks)
        part = [pl.ds(j * self.chunk_rows, self.chunk_rows) for j in chunks]
        y_sems = lambda j: (y_send_sems.at[j], y_recv_sems.at[j])
        to_neighbour = [_remote(first_scr.at[part[j]], across_scr.at[part[j]], *y_sems(j), across) for j in chunks]
        to_owner_y = [_remote(first_scr.at[part[j]], recv_ref.at[0, part[j]], *y_sems(j), across) for j in chunks]
        to_owner_x = [_remote(out_scr.at[part[j]], recv_ref.at[1, part[j]], x_send_sems.at[j], x_recv_sems.at[j], owner)
                      for j in chunks]

        def before():
            @pl.when(i == 0)
            def _():
                for cp in [swaps[0]] + mine:
                    cp.start()

            pl.when(i == 1)(swaps[1].start)

        def after():
            @pl.when(i == self.FIRST_STEP)
            def _():
                swaps[0].wait_recv()
                mine[0].wait()
                first_scr[...] = (mine_scr[0] + theirs_scr[0]).astype(first_scr.dtype)

            for j in chunks:
                pl.when((i == self.SEND_STEPS[j]) & other_side)(to_neighbour[j].start)
                pl.when((i == self.SEND_STEPS[j]) & owners_side)(to_owner_y[j].start)

            @pl.when(i == self.SECOND_STEP)
            def _():
                swaps[1].wait_recv()
                mine[1].wait()
                t = mine_scr[1] + theirs_scr[1]
                own_ref[...] = t
                mine_scr[1] = t

            for j in chunks:
                @pl.when((i == self.REDUCE_STEPS[j]) & other_side)
                def _(j=j):
                    to_neighbour[j].wait_recv()
                    t = mine_scr[1, part[j], :] + across_scr[part[j], :].astype(F32)
                    out_scr[part[j], :] = t.astype(out_scr.dtype)
                    to_owner_x[j].start()

            @pl.when(i == steps - 1)
            def _():
                for cp in swaps:
                    cp.wait_send()
                for j in chunks:
                    @pl.when(other_side)
                    def _(j=j):
                        to_neighbour[j].wait_send()
                        to_owner_x[j].wait_send()

                    @pl.when(owners_side)
                    def _(j=j):
                        to_owner_y[j].wait_send()
                        to_owner_y[j].wait_recv()
                        to_owner_x[j].wait_recv()

        return before, after


def _call(body, comm, *, name, grid, in_specs, out_specs, out_shape, scratch_shapes, args, reduce=None):
    semantics = ("arbitrary",) * len(grid)
    if comm is None and reduce is None:
        return pl.pallas_call(body, name=name, grid=grid, in_specs=in_specs, out_specs=out_specs, out_shape=out_shape,
                              scratch_shapes=scratch_shapes, compiler_params=_params(semantics))(*args)
    n_in, n_out, n_scr = len(args), len(out_shape), len(scratch_shapes)
    c_operands, c_shapes, flows = (comm.operands, comm.out_shapes, comm.flows) if comm else ([], [], [])
    c_in, c_out, rows = len(c_operands), len(c_shapes), max(len(flows), 1)
    r_in = 0 if reduce is None else 1

    def hosted(*refs):
        ins, refs = refs[:n_in], refs[n_in:]
        cins, refs = refs[:c_in], refs[c_in:]
        rins, refs = refs[:r_in], refs[r_in:]
        outs, refs = refs[:n_out], refs[n_out:]
        couts, refs = refs[:c_out], refs[c_out:]
        routs, refs = refs[:2 * r_in], refs[2 * r_in:]
        scr, refs = refs[:n_scr], refs[n_scr:]
        (send_sems, recv_sems, local_sems), r_scr = refs[:3], refs[3:]
        ids = [pl.program_id(d) for d in range(len(grid))]
        first, last = ids[0] == 0, ids[0] == grid[0] - 1
        for d in range(1, len(grid)):
            first, last = first & (ids[d] == 0), last & (ids[d] == grid[d] - 1)
        before = after = lambda: None
        if reduce is not None:
            before, after = reduce.plan(ids[0], grid[0], rins[0], *routs, r_scr)
        if comm is not None:
            start, wait = comm.plan(cins, couts, send_sems, recv_sems, local_sems)
            pl.when(first)(start)
        before()
        body(*ins, *outs, *scr)
        after()
        if comm is not None:
            pl.when(last)(wait)

    any_spec = pl.BlockSpec(memory_space=pl.ANY)
    sems = [pltpu.SemaphoreType.DMA((rows, N_DEV - 1)), pltpu.SemaphoreType.DMA((rows, N_DEV - 1)),
            pltpu.SemaphoreType.DMA((rows,))]
    r_operands, r_specs, r_shapes, r_scratch = ([reduce.stack], reduce.out_specs, reduce.out_shapes,
                                                reduce.scratch_shapes) if reduce else ([], [], [], [])
    return pl.pallas_call(
        hosted, name=name, grid=grid, in_specs=list(in_specs) + [any_spec] * (c_in + r_in),
        out_specs=list(out_specs) + [any_spec] * c_out + r_specs, out_shape=list(out_shape) + list(c_shapes) + r_shapes,
        scratch_shapes=list(scratch_shapes) + sems + r_scratch,
        compiler_params=_params(semantics))(*args, *c_operands, *r_operands)


def _head_row(ref, width, rep):
    hid = lax.broadcasted_iota(jnp.int32, (1, width), 1) // rep
    row = jnp.zeros((1, width), F32)
    for h in range(N_HEADS):
        row = jnp.where(hid == h, ref[h], row)
    return row


def _rows_from_above(u_b, s, ext_scr, row, col):
    down = (row - col == s).astype(_MXU)
    return jnp.concatenate([ext_scr[8 - s:16 - s, :], jnp.dot(down, u_b, preferred_element_type=F32)[8:128]], axis=0)


def _ssd_recompute(first, p_ref, halo_ref, cw_ref, cb_ref, dtb_ref, alog_ref, e_ref, ext_scr, pre=None):
    row = lax.broadcasted_iota(jnp.int32, (128, 128), 0)
    col = lax.broadcasted_iota(jnp.int32, (128, 128), 1)
    ext_scr[0:8, :] = jnp.where(first, 0.0, halo_ref[:, S_XS:S_DT])
    if pre is not None:
        ext_scr[8:16, :] = p_ref[0:8, S_XS:S_DT]
    else:
        ext_scr[8:136, :] = p_ref[:, S_XS:S_DT]
        cw = cw_ref[...]
        pre = (cb_ref[0:1, :] + cw[3:4, :] * ext_scr[8:136, :] + cw[2:3, :] * ext_scr[7:135, :]
               + cw[1:2, :] * ext_scr[6:134, :] + cw[0:1, :] * ext_scr[5:133, :])
    sg = _sigmoid(pre)
    act = pre * sg
    lane = lax.broadcasted_iota(jnp.int32, (1, 128), 1)
    A = jnp.where(lane < N_HEADS, -jnp.exp(_head_row(alog_ref, 128, 1)), 0.0)
    raw = p_ref[:, S_DT:S_DT + 128] + _head_row(dtb_ref, 128, 1)
    dt = _softplus(raw)
    dA = dt * A
    tril = (row >= col).astype(BF16)
    acs = _mm_exact_l(tril, dA)
    last = acs[127:128, :]
    ds = jnp.exp(last - acs)
    eo = jnp.exp(acs)
    E = e_ref[...]
    ex = _mm_2pass_r(jnp.concatenate([dt, ds, eo], axis=0), E)
    dt_e, ds_e, eo_e = ex[0:128], ex[128:256], ex[256:384]
    xs_c = act[:, 0:1024]
    X = xs_c * dt_e
    return dict(pre=pre, sg=sg, xs_c=xs_c, Bc=act[:, 1024:1280], Cc=act[:, 1280:1536], A=A, raw=raw, dt=dt,
                acs=acs, acsT=acs.T, eo_e=eo_e, ds_e=ds_e, dt_e=dt_e, cd_e=eo_e[127:128, :],
                X=X, Xd=X * ds_e, row=row, col=col)


def _split_halves(t):
    lo = _lo_half(CHUNK)
    return jnp.concatenate([jnp.where(lo, t, 0.0), jnp.where(lo, 0.0, t)], axis=0)


def _ssd_core(R, hprev):
    causal = R["row"] >= R["col"]
    acs, acsT, X = R["acs"], R["acsT"], R["X"]
    ydiag, yoff, snew = [], [], []
    for g in range(SSD_GROUPS):
        Bg = R["Bc"][:, g * 128:(g + 1) * 128]
        Cg = R["Cc"][:, g * 128:(g + 1) * 128]
        cols = slice(g * 512, (g + 1) * 512)
        CB = _mm_nt(Cg, Bg)
        snew.append(_mm_tn(Bg, R["Xd"][:, cols]))
        yoff.append(_mm(Cg, hprev[:, cols]))
        for j in range(4):
            h0 = g * 8 + 2 * j
            ms = [CB * jnp.exp(jnp.where(causal, acs[:, h:h + 1] - acsT[h:h + 1, :], NEG)) for h in (h0, h0 + 1)]
            ydiag.append(_mm(jnp.concatenate(ms, axis=1), _split_halves(X[:, h0 * HEAD_DIM:h0 * HEAD_DIM + 128])))
    Y = jnp.concatenate(ydiag, axis=1) + jnp.concatenate(yoff, axis=1) * R["eo_e"]
    return Y, jnp.concatenate(snew, axis=1)


def _ssd_forward_step(p_ref, halo_ref, cw_ref, cb_ref, dtb_ref, alog_ref, dsk_ref, nw_ref, e_ref,
                      y_ref, ypre_ref, hprev_ref, pre_ref, h_scr, ext_scr):
    c = pl.program_id(0)
    first = c == 0

    @pl.when(first)
    def _():
        h_scr[...] = jnp.zeros_like(h_scr)

    R = _ssd_recompute(first, p_ref, halo_ref, cw_ref, cb_ref, dtb_ref, alog_ref, e_ref, ext_scr)
    hprev = h_scr[...]
    hprev_ref[...] = hprev
    pre_ref[...] = R["pre"]
    Y, snew = _ssd_core(R, hprev)
    h_scr[...] = hprev * R["cd_e"] + snew
    Y = Y + _head_row(dsk_ref, D_SSD, HEAD_DIM) * R["xs_c"]
    ypre_ref[...] = Y
    z = p_ref[:, S_Z:S_Z + 1024]
    yf = Y * (z * _sigmoid(z))
    outs = []
    for g in range(SSD_GROUPS):
        yg = yf[:, g * 512:(g + 1) * 512]
        r = lax.rsqrt(jnp.mean(yg * yg, axis=-1, keepdims=True) + RMS_EPS)
        outs.append(yg * r)
    y_ref[:, 0:D_SSD] = (jnp.concatenate(outs, axis=1) * nw_ref[0:1, :]).astype(y_ref.dtype)


def _ssd_backward(proj_ssd, hprev_all, ypre, pre, dy, conv_w, conv_b, dt_bias, a_log, d_skip, norm_w, E, ET, comm=None):
    L = proj_ssd.shape[0]
    nc = L // CHUNK

    def body(p_ref, halo_ref, hprev_ref, ypre_ref, pre_ref, dy_ref, cw_ref, cb_ref, dtb_ref, alog_ref, dsk_ref, nw_ref, e_ref,
             et_ref, dp_ref, acc_cw_ref, acc_w_ref, acc_s_ref, dh_scr, ext_scr, ext2_scr, nxt_scr):
        i = pl.program_id(0)
        c = nc - 1 - i
        first = c == 0

        @pl.when(i == 0)
        def _():
            dh_scr[...] = jnp.zeros_like(dh_scr)
            nxt_scr[...] = jnp.zeros_like(nxt_scr)
            acc_cw_ref[...] = jnp.zeros_like(acc_cw_ref)
            acc_w_ref[...] = jnp.zeros_like(acc_w_ref)
            acc_s_ref[...] = jnp.zeros_like(acc_s_ref)

        R = _ssd_recompute(first, p_ref, halo_ref, cw_ref, cb_ref, dtb_ref, alog_ref, e_ref, ext_scr, pre_ref[...])
        hprev = hprev_ref[...]
        xs_c, X, Xd = R["xs_c"], R["X"], R["Xd"]
        acs, acsT = R["acs"], R["acsT"]
        ET = et_ref[...]
        dsk = _head_row(dsk_ref, D_SSD, HEAD_DIM)
        Y = ypre_ref[...]

        z = p_ref[:, S_Z:S_Z + 1024]
        sz = _sigmoid(z)
        silz = z * sz
        yf = Y * silz
        dyv = dy_ref[...]
        nw = nw_ref[0:1, :]
        dyf_parts, dnw_parts = [], []
        for g in range(SSD_GROUPS):
            cols = slice(g * 512, (g + 1) * 512)
            yg = yf[:, cols]
            r = lax.rsqrt(jnp.mean(yg * yg, axis=-1, keepdims=True) + RMS_EPS)
            yn = yg * r
            dyn = dyv[:, cols] * nw[:, cols]
            dnw_parts.append(_colsum(dyv[:, cols] * yn))
            dyf_parts.append(r * (dyn - yn * jnp.mean(dyn * yn, axis=-1, keepdims=True)))
        dyf = jnp.concatenate(dyf_parts, axis=1)
        dY = dyf * silz
        dz = dyf * Y * (sz * (1.0 + z * (1.0 - sz)))

        dhn = dh_scr[...]
        dYo = dY * R["eo_e"]
        causal = R["row"] >= R["col"]
        dacs = jnp.zeros((128, 128), F32)
        dacs_t = jnp.zeros((128, 128), F32)
        dxdiag, dxd, dhprev, dBs, dCs, yoff = [], [], [], [], [], []
        for g in range(SSD_GROUPS):
            Bg = R["Bc"][:, g * 128:(g + 1) * 128]
            Cg = R["Cc"][:, g * 128:(g + 1) * 128]
            cols = slice(g * 512, (g + 1) * 512)
            CB = _mm_nt(Cg, Bg)
            dCB = jnp.zeros((128, 128), F32)
            for j in range(4):
                h0 = g * 8 + 2 * j
                pc = slice(h0 * HEAD_DIM, h0 * HEAD_DIM + 128)
                dYst = _split_halves(dY[:, pc])
                dMst = _mm_nt(dYst, X[:, pc])
                mts = []
                for a, h in enumerate((h0, h0 + 1)):
                    acol = acs[:, h:h + 1]
                    arow = acsT[h:h + 1, :]
                    Lm = jnp.exp(jnp.where(causal, acol - arow, NEG))
                    M = CB * Lm
                    dM = dMst[a * 128:(a + 1) * 128]
                    dCB = dCB + dM * Lm
                    G = dM * M
                    dacs = dacs + jnp.where(R["col"] == h, jnp.sum(G, axis=1, keepdims=True), 0.0)
                    dacs_t = dacs_t + jnp.where(R["row"] == h, jnp.sum(G, axis=0, keepdims=True), 0.0)
                    mts.append(M.T)
                dxdiag.append(_mm(jnp.concatenate(mts, axis=1), dYst))
            dS = dhn[:, cols]
            dxd.append(_mm(Bg, dS))
            yoff.append(_mm(Cg, hprev[:, cols]))
            dhprev.append(_mm_tn(Cg, dYo[:, cols]))
            dCs.append(_mm_nt(dYo[:, cols], hprev[:, cols]) + _mm(dCB, Bg))
            dBs.append(_mm_tn(dCB, Cg) + _mm_nt(Xd[:, cols], dS))
        Yoff = jnp.concatenate(yoff, axis=1) * R["eo_e"]
        dXd = jnp.concatenate(dxd, axis=1)
        dX = jnp.concatenate(dxdiag, axis=1) + dXd * R["ds_e"]
        t_state = dXd * Xd
        dacs = dacs + _mm_2pass_r(dY * Yoff - t_state, ET) - dacs_t.T
        v_last = _colsum(t_state + dhn * hprev * R["cd_e"])
        dlast = _mm_exact_r(jnp.broadcast_to(v_last, (8, 1024)), ET)[0:1, :]
        dacs = dacs + jnp.where(R["row"] == 127, dlast, 0.0)
        triu = (R["col"] >= R["row"]).astype(BF16)
        da = _mm_exact_l(triu, dacs)
        ddt = da * R["A"] + _mm(dX * xs_c, ET)
        ddt_raw = ddt * _sigmoid(R["raw"])
        dxs_c = dX * R["dt_e"] + dY * dsk
        dh_scr[...] = jnp.concatenate(dhprev, axis=1) + dhn * R["cd_e"]

        dact = jnp.concatenate([dxs_c] + dBs + dCs, axis=1)
        pre, sg = R["pre"], R["sg"]
        dpre = dact * (sg * (1.0 + pre * (1.0 - sg)))
        ext2_scr[0:8, :] = dpre[120:128, :]
        ext2_scr[8:16, :] = nxt_scr[...]
        nxt_scr[...] = dpre[0:8, :]
        cw = cw_ref[...]
        u_b, dpre_b = p_ref[:, S_XS:S_DT].astype(_MXU), dpre.astype(_MXU)
        dxbc = cw[3:4, :] * dpre
        taps = [_colsum(dpre * p_ref[:, S_XS:S_DT])]
        for s in (1, 2, 3):
            up = (R["col"] - R["row"] == s).astype(_MXU)
            d_s = jnp.concatenate([jnp.dot(up, dpre_b, preferred_element_type=F32)[0:120],
                                   ext2_scr[s:8 + s, :]], axis=0)
            dxbc = dxbc + cw[3 - s:4 - s, :] * d_s
            taps.append(_colsum(dpre * _rows_from_above(u_b, s, ext_scr, R["row"], R["col"])))
        acc_cw_ref[...] += _rows8(taps[::-1] + [_colsum(dpre)])
        acc_w_ref[...] += _rows8([jnp.concatenate(dnw_parts, axis=1), _colsum(dY * xs_c)])
        acc_s_ref[...] += _rows8([_colsum(ddt_raw), _colsum(da * R["dt"])])

        lane = lax.broadcasted_iota(jnp.int32, (128, 128), 1)
        dp_ref[:, S_Z:S_Z + 1024] = dz.astype(dp_ref.dtype)
        dp_ref[:, S_XS:S_DT] = dxbc.astype(dp_ref.dtype)
        dp_ref[:, S_DT:S_DT + 128] = jnp.where(lane < N_HEADS, ddt_raw, 0.0).astype(dp_ref.dtype)
        dp_ref[:, S_DT + 128:S_W] = jnp.zeros((128, 128), dp_ref.dtype)

        @pl.when(i == nc - 1)
        def _():
            acc = acc_s_ref[...]
            dskip = _mm_exact_r(acc_w_ref[...], ET)[1:2, :]
            acc_s_ref[...] = _rows8([acc[0:1, :], acc[1:2, :] * R["A"], dskip])

    const = lambda shape: pl.BlockSpec(shape, lambda i: (0, 0))
    smem = pl.BlockSpec(memory_space=pltpu.SMEM)
    rev = lambda i: (nc - 1 - i, 0)
    return _call(
        body, comm, name="ssd_bwd", grid=(nc,),
        in_specs=[pl.BlockSpec((CHUNK, S_W), rev),
                  pl.BlockSpec((8, S_W), lambda i: (jnp.maximum((nc - 1 - i) * 16 - 1, 0), 0)),
                  pl.BlockSpec((128, 1024), rev),
                  pl.BlockSpec((CHUNK, D_SSD), rev),
                  pl.BlockSpec((CHUNK, D_XBC), rev),
                  pl.BlockSpec((CHUNK, D_SSD), rev),
                  const((4, D_XBC)), const((1, D_XBC)), smem, smem, smem, const((1, 1024)),
                  const((128, 1024)), const((1024, 128))],
        out_specs=[pl.BlockSpec((CHUNK, S_W), rev), const((8, D_XBC)), const((8, 1024)), const((8, 128))],
        out_shape=[jax.ShapeDtypeStruct((L, S_W), _MXU), jax.ShapeDtypeStruct((8, D_XBC), F32),
                   jax.ShapeDtypeStruct((8, 1024), F32), jax.ShapeDtypeStruct((8, 128), F32)],
        scratch_shapes=[pltpu.VMEM((128, 1024), F32), pltpu.VMEM((16, D_XBC), F32),
                        pltpu.VMEM((16, D_XBC), F32), pltpu.VMEM((8, D_XBC), F32)],
        args=(proj_ssd, proj_ssd, hprev_all, ypre, pre, dy, conv_w, conv_b, dt_bias, a_log, d_skip, norm_w, E, ET))


def _rope(t, tab):
    cos, sa, sb = tab[:, 0:128], tab[:, 128:256], tab[:, 256:384]
    outs = []
    for i in range(t.shape[1] // 128):
        tg = t[:, i * 128:(i + 1) * 128]
        outs.append(tg * cos + pltpu.roll(tg, 8, 1) * sa + pltpu.roll(tg, 120, 1) * sb)
    return jnp.concatenate(outs, axis=1)


def _rope_transposed(d, tab):
    cos, sa, sb = tab[:, 0:128], tab[:, 128:256], tab[:, 256:384]
    outs = []
    for i in range(d.shape[1] // 128):
        dg = d[:, i * 128:(i + 1) * 128]
        outs.append(dg * cos + pltpu.roll(dg * sa, 120, 1) + pltpu.roll(dg * sb, 8, 1))
    return jnp.concatenate(outs, axis=1)


def _lo_half(rows):
    return lax.broadcasted_iota(jnp.int32, (rows, 128), 1) < HEAD_DIM


def _native_half(rows, j):
    lo = _lo_half(rows)
    return lo if j % 2 == 0 else jnp.logical_not(lo)


def _kv_native(t, j):
    p = j // 2
    return jnp.where(_native_half(t.shape[0], j), t[:, p * 128:(p + 1) * 128], 0.0)


def _stack_heads(t, j):
    out = []
    for m in (2 * j, 2 * j + 1):
        pair = t[:, m * 128:(m + 1) * 128]
        swapped = pltpu.roll(pair, HEAD_DIM, 1)
        out += [pair, swapped] if j % 2 == 0 else [swapped, pair]
    return jnp.concatenate(out, axis=0)


def _unstack_heads(s, j):
    out = []
    for m in range(2):
        first, second = s[256 * m:256 * m + 128], s[256 * m + 128:256 * m + 256]
        if j % 2 == 0:
            out.append(first + pltpu.roll(second, HEAD_DIM, 1))
        else:
            out.append(pltpu.roll(first, HEAD_DIM, 1) + second)
    return jnp.concatenate(out, axis=1)


def _keep_native(r, j):
    return jnp.where(_native_half(r.shape[0], j), r, 0.0)


def _sink_row(sink_ref, j):
    hid = lax.broadcasted_iota(jnp.int32, (1, 4 * CHUNK), 1) // CHUNK
    row = jnp.zeros((1, 4 * CHUNK), F32)
    for hh in range(4):
        row = jnp.where(hid == hh, sink_ref[4 * j + hh], row)
    return row


def _from_current():
    si = lax.broadcasted_iota(jnp.int32, (CHUNK, 4 * CHUNK), 0)
    qi = lax.broadcasted_iota(jnp.int32, (CHUNK, 4 * CHUNK), 1) % CHUNK
    return si <= qi


def _fold(full, from_cur, pen=0.0):
    return jnp.where(from_cur, full[CHUNK:2 * CHUNK], full[0:CHUNK] + pen)


def _unfold(t, from_cur):
    c = jnp.where(from_cur, t, 0.0)
    return jnp.concatenate([t - c, c], axis=0)


def _softmax_sink(s, sink):
    mx = jnp.maximum(jnp.max(s, axis=0, keepdims=True), sink)
    p = jnp.exp(s - mx)
    esink = jnp.exp(sink - mx)
    inv = 1.0 / (jnp.sum(p, axis=0, keepdims=True) + esink)
    return p * inv, esink * inv


def _swa_inputs(blk, p_ref, prev_ref, tab_ref, ptab_ref):
    tab = tab_ref[...]
    qr = _rope(p_ref[:, A_Q:A_Q + 1024], tab) * ATT_SCALE
    kk = jnp.concatenate([_rope(prev_ref[:, 0:256], ptab_ref[...]), _rope(p_ref[:, A_K:A_K + 256], tab)], axis=0)
    vv = jnp.concatenate([prev_ref[:, 256:512], p_ref[:, A_V:A_V + 256]], axis=0)
    return tab, qr, kk, vv, jnp.where(blk > 0, 0.0, NEG)


def _swa_forward_step(sink_ref, p_ref, prev_ref, tab_ref, ptab_ref, y_ref):
    n = pl.program_id(0)
    _, qr, kk, vv, pen = _swa_inputs(n, p_ref, prev_ref, tab_ref, ptab_ref)
    from_cur = _from_current()
    outs = []
    for j in range(KV_HEADS):
        s = _fold(_mm_nt(_kv_native(kk, j), _stack_heads(qr, j)), from_cur, pen)
        P, _ = _softmax_sink(s, _sink_row(sink_ref, j))
        outs.append(_unstack_heads(_mm_tn(_unfold(P, from_cur), _kv_native(vv, j)), j))
    g = p_ref[:, A_G:A_G + 1024]
    y_ref[:, D_SSD:D_SSD + D_ATT] = (jnp.concatenate(outs, axis=1) * (g * _sigmoid(g))).astype(y_ref.dtype)


def _mixer_forward(proj_ssd, proj_att, tabs, sinks, conv_w, conv_b, dt_bias, a_log, d_skip, norm_w, E, comm=None):
    L = proj_ssd.shape[0]
    nc = L // CHUNK

    def body(p_ref, halo_ref, cw_ref, cb_ref, dtb_ref, alog_ref, dsk_ref, nw_ref, e_ref,
             sink_ref, pa_ref, prev_ref, tab_ref, ptab_ref, y_ref, ypre_ref, hprev_ref, pre_ref, h_scr, ext_scr):
        _ssd_forward_step(p_ref, halo_ref, cw_ref, cb_ref, dtb_ref, alog_ref, dsk_ref, nw_ref, e_ref,
                          y_ref, ypre_ref, hprev_ref, pre_ref, h_scr, ext_scr)
        _swa_forward_step(sink_ref, pa_ref, prev_ref, tab_ref, ptab_ref, y_ref)

    const = lambda shape: pl.BlockSpec(shape, lambda c: (0, 0))
    smem = pl.BlockSpec(memory_space=pltpu.SMEM)
    rows = lambda w: pl.BlockSpec((CHUNK, w), lambda c: (c, 0))
    return _call(
        body, comm, name="mixer_fwd", grid=(nc,),
        in_specs=[rows(S_W), pl.BlockSpec((8, S_W), lambda c: (jnp.maximum(c * 16 - 1, 0), 0)),
                  const((4, D_XBC)), const((1, D_XBC)), smem, smem, smem, const((1, 1024)), const((128, 1024)),
                  smem, rows(A_W), pl.BlockSpec((CHUNK, 512), lambda c: (jnp.maximum(c - 1, 0), 2)),
                  rows(384), pl.BlockSpec((CHUNK, 384), lambda c: (jnp.maximum(c - 1, 0), 0))],
        out_specs=[rows(D_SSD + D_ATT), rows(D_SSD), pl.BlockSpec((128, 1024), lambda c: (c, 0)), rows(D_XBC)],
        out_shape=[jax.ShapeDtypeStruct((L, D_SSD + D_ATT), _MXU), jax.ShapeDtypeStruct((L, D_SSD), F32),
                   jax.ShapeDtypeStruct((nc * 128, 1024), F32), jax.ShapeDtypeStruct((L, D_XBC), F32)],
        scratch_shapes=[pltpu.VMEM((128, 1024), F32), pltpu.VMEM((136, D_XBC), F32)],
        args=(proj_ssd, proj_ssd, conv_w, conv_b, dt_bias, a_log, d_skip, norm_w, E,
              sinks, proj_att, proj_att, tabs, tabs))


def _swa_backward(proj_att, tabs, sinks, dy, reduce=None):
    L = proj_att.shape[0]
    nb = L // CHUNK

    def body(sink_ref, p_ref, prev_ref, tab_ref, ptab_ref, dy_ref, dp_ref, dsink_ref, carry_k, carry_v):
        i = pl.program_id(0)
        n = nb - 1 - i

        @pl.when(i == 0)
        def _():
            carry_k[...] = jnp.zeros_like(carry_k)
            carry_v[...] = jnp.zeros_like(carry_v)
            dsink_ref[...] = jnp.zeros_like(dsink_ref)

        tab, qr, kk, vv, pen = _swa_inputs(n, p_ref, prev_ref, tab_ref, ptab_ref)
        from_cur = _from_current()
        g = p_ref[:, A_G:A_G + 1024]
        sgm = _sigmoid(g)
        dyv = dy_ref[...]
        do_all = dyv * (g * sgm)
        lane8 = lax.broadcasted_iota(jnp.int32, (8, 128), 1)
        hid = lax.broadcasted_iota(jnp.int32, (1, 4 * CHUNK), 1) // CHUNK
        o_parts, dq_parts = [], []
        dk_nat = [jnp.zeros((2 * CHUNK, 128), F32) for _ in range(2)]
        dv_nat = [jnp.zeros((2 * CHUNK, 128), F32) for _ in range(2)]
        dsink = jnp.zeros((8, 128), F32)
        for j in range(KV_HEADS):
            qs = _stack_heads(qr, j)
            kkb, vvb = _kv_native(kk, j), _kv_native(vv, j)
            P, psink = _softmax_sink(_fold(_mm_nt(kkb, qs), from_cur, pen), _sink_row(sink_ref, j))
            p_full = _unfold(P, from_cur)
            o_parts.append(_unstack_heads(_mm_tn(p_full, vvb), j))
            do_s = _stack_heads(do_all, j)
            dP = _fold(_mm_nt(vvb, do_s), from_cur)
            D = jnp.sum(P * dP, axis=0, keepdims=True)
            ds_full = _unfold(P * (dP - D), from_cur)
            sd = psink * D
            for hh in range(4):
                dsink = dsink + jnp.where(lane8 == 4 * j + hh, -jnp.sum(jnp.where(hid == hh, sd, 0.0)), 0.0)
            dq_parts.append(_unstack_heads(_mm_tn(ds_full, kkb), j) * ATT_SCALE)
            dk_nat[j // 2] = dk_nat[j // 2] + _keep_native(_mm(ds_full, qs), j)
            dv_nat[j // 2] = dv_nat[j // 2] + _keep_native(_mm(p_full, do_s), j)
        o = jnp.concatenate(o_parts, axis=1)
        dkk = jnp.concatenate(dk_nat, axis=1)
        dvv = jnp.concatenate(dv_nat, axis=1)
        out = dp_ref.dtype
        dp_ref[:, A_Q:A_Q + 1024] = _rope_transposed(jnp.concatenate(dq_parts, axis=1), tab).astype(out)
        dp_ref[:, A_K:A_K + 256] = _rope_transposed(dkk[CHUNK:2 * CHUNK] + carry_k[...], tab).astype(out)
        dp_ref[:, A_V:A_V + 256] = (dvv[CHUNK:2 * CHUNK] + carry_v[...]).astype(out)
        dp_ref[:, A_G:A_G + 1024] = (dyv * o * (sgm * (1.0 + g * (1.0 - sgm)))).astype(out)
        carry_k[...] = dkk[0:CHUNK]
        carry_v[...] = dvv[0:CHUNK]
        dsink_ref[...] += dsink

    rev = lambda i: (nb - 1 - i, 0)
    prev = lambda i: jnp.maximum(nb - 2 - i, 0)
    return _call(
        body, None, name="swa_bwd", grid=(nb,),
        in_specs=[pl.BlockSpec(memory_space=pltpu.SMEM),
                  pl.BlockSpec((CHUNK, A_W), rev),
                  pl.BlockSpec((CHUNK, 512), lambda i: (prev(i), 2)),
                  pl.BlockSpec((CHUNK, 384), rev),
                  pl.BlockSpec((CHUNK, 384), lambda i: (prev(i), 0)),
                  pl.BlockSpec((CHUNK, D_ATT), lambda i: (nb - 1 - i, 1))],
        out_specs=[pl.BlockSpec((CHUNK, A_W), rev), pl.BlockSpec((8, 128), lambda i: (0, 0))],
        out_shape=[jax.ShapeDtypeStruct((L, A_W), _MXU), jax.ShapeDtypeStruct((8, 128), F32)],
        scratch_shapes=[pltpu.VMEM((CHUNK, 256), F32), pltpu.VMEM((CHUNK, 256), F32)],
        args=(sinks, proj_att, proj_att, tabs, tabs, dy), reduce=reduce)


def _head(y, x, target, w_out, ln_g, ln_b, *, tm):
    L = x.shape[0]
    nsteps = L // tm

    def body(y_ref, x_ref, t_ref, wo_ref, g_ref, b_ref, dr_ref, dy_ref, acc_ref):
        i = pl.program_id(0)

        @pl.when(i == 0)
        def _():
            acc_ref[...] = jnp.zeros_like(acc_ref)

        r = ALPHA * x_ref[...] + _mm(y_ref[...], wo_ref[...])
        mu = jnp.mean(r, axis=-1, keepdims=True)
        d = r - mu
        rstd = lax.rsqrt(jnp.mean(d * d, axis=-1, keepdims=True) + LN_EPS)
        xh = d * rstd
        gam = g_ref[0:1, :]
        e = xh * gam + b_ref[0:1, :] - t_ref[...]
        dout = e * (1.0 / D_MODEL)
        dxh = dout * gam
        dr = rstd * (dxh - jnp.mean(dxh, axis=-1, keepdims=True)
                     - xh * jnp.mean(dxh * xh, axis=-1, keepdims=True))
        dr_ref[...] = dr
        dy_ref[...] = _mm_nt(dr, wo_ref[...])
        acc_ref[...] += _rows8([_colsum(dout * xh), _colsum(dout), _colsum(e * e) * (0.5 / D_MODEL)])

        @pl.when(i == nsteps - 1)
        def _():
            acc = acc_ref[...]
            tot = jnp.sum(acc[2:3, :])
            rid = lax.broadcasted_iota(jnp.int32, (8, 1024), 0)
            acc_ref[...] = jnp.where(rid == 3, tot, acc)

    const = lambda shape: pl.BlockSpec(shape, lambda i: (0, 0))
    row = lambda w: pl.BlockSpec((tm, w), lambda i: (i, 0))
    return pl.pallas_call(
        body, name="head", grid=(nsteps,),
        in_specs=[row(2048), row(1024), row(1024), const((2048, 1024)), const((1, 1024)), const((1, 1024))],
        out_specs=[row(1024), row(2048), const((8, 1024))],
        out_shape=[jax.ShapeDtypeStruct((L, D_MODEL), F32), jax.ShapeDtypeStruct((L, 2048), F32),
                   jax.ShapeDtypeStruct((8, 1024), F32)],
        compiler_params=_params(("arbitrary",)),
    )(y, x, target, w_out, ln_g, ln_b)


def _gather_w_in(w_shard, positions):
    R = w_shard.shape[0]
    halves = (pl.ds(0, R // 2), pl.ds(R // 2, R // 2))
    any_spec = pl.BlockSpec(memory_space=pl.ANY)
    vmem = pl.BlockSpec(memory_space=pltpu.VMEM)

    def body(in_ref, pos_ref, inv_ref, out_ref, tab_ref, send_sems, recv_sems, local_sem):
        x, y, c = _position()

        def slot(p, half=None):
            s = out_ref.at[_index(*p)]
            return s if half is None else s.at[halves[half]]

        def same_core(p):
            return (p[0], p[1], c)

        def other_core(p):
            return (p[0], p[1], 1 - c)

        me, xn, yn, dg = (x, y), (1 - x, y), (x, 1 - y), (1 - x, 1 - y)

        def copy(k, dst, to, src=None):
            return _remote(dst if src is None else src, dst, send_sems.at[k], recv_sems.at[k], to)

        local = pltpu.make_async_copy(in_ref, slot(same_core(me)), local_sem)
        local.start()
        own = [copy(0, slot(same_core(me)), other_core(me), in_ref), copy(1, slot(same_core(me)), same_core(xn), in_ref),
               copy(2, slot(same_core(me)), same_core(yn), in_ref)]
        for cp in own:
            cp.start()
        _rope_tables(pos_ref, inv_ref, tab_ref)
        copy(1, slot(same_core(xn)), same_core(xn)).wait_recv()
        passed = [copy(4, slot(same_core(xn), 1), same_core(yn)), copy(5, slot(same_core(xn)), other_core(me))]
        for cp in passed:
            cp.start()
        copy(2, slot(same_core(yn)), same_core(yn)).wait_recv()
        more = [copy(3, slot(same_core(yn), 0), same_core(xn)), copy(6, slot(same_core(yn)), other_core(me))]
        for cp in more:
            cp.start()
        passed += more
        for k, half in ((3, 0), (4, 1)):
            copy(k, slot(same_core(dg), half), same_core(xn)).wait_recv()
            fwd = copy(7 + half, slot(same_core(dg), half), other_core(me))
            fwd.start()
            passed.append(fwd)
        copy(0, slot(other_core(me)), other_core(me)).wait_recv()
        copy(5, slot(other_core(xn)), other_core(me)).wait_recv()
        copy(6, slot(other_core(yn)), other_core(me)).wait_recv()
        for half in (0, 1):
            copy(7 + half, slot(other_core(dg), half), other_core(me)).wait_recv()
        for cp in own + passed:
            cp.wait_send()
        local.wait()

    return pl.pallas_call(
        body, name="gather_w_in", in_specs=[any_spec, vmem, vmem], out_specs=[any_spec, vmem],
        out_shape=[jax.ShapeDtypeStruct((N_DEV,) + w_shard.shape, w_shard.dtype),
                   jax.ShapeDtypeStruct((positions.size, 384), F32)],
        scratch_shapes=[pltpu.SemaphoreType.DMA((9,)), pltpu.SemaphoreType.DMA((9,)), pltpu.SemaphoreType.DMA],
        compiler_params=_params(),
    )(w_shard, positions, jnp.asarray(ROPE_INV)[None, :])


def _input_gradient(d_ssd, d_att, w_ssd, w_att, dr, *, tm, comm=None, reduce=None):
    L = dr.shape[0]

    def body(ds_ref, da_ref, ws_ref, wa_ref, dr_ref, o_ref):
        o_ref[...] = ALPHA * dr_ref[...] + _mm_nt(ds_ref[...], ws_ref[...]) + _mm_nt(da_ref[...], wa_ref[...])

    row = lambda w: pl.BlockSpec((tm, w), lambda i: (i, 0))
    resident = lambda a: pl.BlockSpec(a.shape, lambda i: (0, 0), pipeline_mode=pl.Buffered(1))
    return _call(body, comm, name="dx", grid=(L // tm,),
                 in_specs=[row(S_W), row(A_W), resident(w_ssd), resident(w_att), row(D_MODEL)],
                 out_specs=[row(D_MODEL)], out_shape=[jax.ShapeDtypeStruct((L, D_MODEL), F32)],
                 scratch_shapes=[], args=(d_ssd, d_att, w_ssd, w_att, dr), reduce=reduce)


SHARD_COLS = D_IN_PROJ // N_DEV
SPLIT = N_SSD_REAL - 4 * SHARD_COLS
RELAYOUT_ROWS = 256


def _unpack_w_in(w_all):
    def body(g_ref, ws_ref, wa_ref):
        for j in range(4):
            ws_ref[:, SHARD_COLS * j:SHARD_COLS * (j + 1)] = g_ref[j]
        ws_ref[:, 4 * SHARD_COLS:N_SSD_REAL] = g_ref[4, :, 0:SPLIT]
        ws_ref[:, N_SSD_REAL:S_W] = jnp.zeros((RELAYOUT_ROWS, S_W - N_SSD_REAL), ws_ref.dtype)
        wa_ref[:, 0:SHARD_COLS - SPLIT] = g_ref[4, :, SPLIT:SHARD_COLS]
        for j in range(5, N_DEV):
            lo = SHARD_COLS * (j - 4) - SPLIT
            wa_ref[:, lo:lo + SHARD_COLS] = g_ref[j]

    return pl.pallas_call(
        body, name="unpack_w_in", grid=(D_MODEL // RELAYOUT_ROWS,),
        in_specs=[pl.BlockSpec((N_DEV, RELAYOUT_ROWS, SHARD_COLS), lambda i: (0, i, 0))],
        out_specs=[pl.BlockSpec((RELAYOUT_ROWS, S_W), lambda i: (i, 0)), pl.BlockSpec((RELAYOUT_ROWS, A_W), lambda i: (i, 0))],
        out_shape=[jax.ShapeDtypeStruct((D_MODEL, S_W), w_all.dtype), jax.ShapeDtypeStruct((D_MODEL, A_W), w_all.dtype)],
        compiler_params=_params(("arbitrary",)),
    )(w_all)


def _dw_in(xb, d, half, tail=None, *, tl=1024):
    L, N = d.shape
    steps = L // tl

    def body(x_ref, d_ref, *refs):
        if half == 0:
            p_ref, tail_ref, acc = refs
        else:
            t_ref, p_ref, acc = refs
        l = pl.program_id(0)

        @pl.when(l == 0)
        def _():
            acc[...] = jnp.zeros_like(acc)

        acc[...] += _mm_tn(x_ref[...], d_ref[...])

        @pl.when(l == steps - 1)
        def _():
            if half == 0:
                tail_ref[...] = acc[:, S_DT:S_W]
            for j in range(4):
                if half == 0:
                    pieces = [(0, acc[:, SHARD_COLS * j:SHARD_COLS * (j + 1)])]
                elif j == 0:
                    pieces = [(0, t_ref[:, 4 * SHARD_COLS - S_DT:N_SSD_REAL - S_DT]), (SPLIT, acc[:, 0:SHARD_COLS - SPLIT])]
                else:
                    lo = SHARD_COLS * j - SPLIT
                    pieces = [(0, acc[:, lo:lo + SHARD_COLS])]
                for off, blk in pieces:
                    p_ref[j, :, off:off + blk.shape[1]] = blk

    once = pl.Buffered(1)
    whole = lambda shape: pl.BlockSpec(shape, lambda l: (0,) * len(shape), pipeline_mode=once)
    in_specs = [pl.BlockSpec((tl, D_MODEL), lambda l: (l, 0)), pl.BlockSpec((tl, N), lambda l: (l, 0))]
    args = [xb, d]
    out_shape = [jax.ShapeDtypeStruct((4, D_MODEL, SHARD_COLS), F32)]
    if half == 0:
        out_shape.append(jax.ShapeDtypeStruct((D_MODEL, S_W - S_DT), F32))
    else:
        in_specs.append(whole(tail.shape))
        args.append(tail)
    return pl.pallas_call(
        body, name="dw_in_%d" % half, grid=(steps,), in_specs=in_specs,
        out_specs=[whole(o.shape) for o in out_shape], out_shape=out_shape,
        scratch_shapes=[pltpu.VMEM((D_MODEL, N), F32)], compiler_params=_params(("arbitrary",)),
    )(*args)


def _adamw_math(w, g, m, v):
    m = ADAM_B1 * m + (1.0 - ADAM_B1) * g
    v = ADAM_B2 * v + (1.0 - ADAM_B2) * (g * g)
    m_hat = m / (1.0 - ADAM_B1 ** ADAM_STEP)
    v_hat = v / (1.0 - ADAM_B2 ** ADAM_STEP)
    delta = -ADAM_LR * (m_hat / (jnp.sqrt(v_hat) + ADAM_EPS) + ADAM_WD * w)
    return delta, m, v


def _adamw_shard(g_own, recv, w, m, v, *, rows, name):
    R, C = g_own.shape

    def body(g_ref, r_ref, w_ref, m_ref, v_ref, go_ref, d_ref, mo_ref, vo_ref):
        g = g_ref[...]
        for k in range(N_DEV - 1):
            g = g + r_ref[k].astype(F32)
        d, mn, vn = _adamw_math(w_ref[...], g, m_ref[...], v_ref[...])
        go_ref[...] = g
        d_ref[...] = d
        mo_ref[...] = mn
        vo_ref[...] = vn

    blk = pl.BlockSpec((rows, C), lambda i: (i, 0))
    return pl.pallas_call(
        body, name=name, grid=(R // rows,),
        in_specs=[blk, pl.BlockSpec((N_DEV - 1, rows, C), lambda i: (0, i, 0)), blk, blk, blk],
        out_specs=[blk] * 4, out_shape=[jax.ShapeDtypeStruct((R, C), F32)] * 4,
        compiler_params=_params(("arbitrary",)),
    )(g_own, recv, w, m, v)


def _minor_rows_view(a):
    return jnp.transpose(a, (2, 0, 1)).reshape(SHARD_COLS * 8, 128)


def _from_minor_rows_view(v):
    return jnp.transpose(v.reshape(SHARD_COLS, 8, 128), (1, 2, 0)).reshape(1, D_MODEL, SHARD_COLS)


def _adamw_w_in(is_lo, own_lo, own_hi, recv_lo, recv_hi, w, m, v):
    C = SHARD_COLS
    pad = -C % 128

    def body(lo_ref, ol_ref, oh_ref, rl_ref, rh_ref, w_ref, m_ref, v_ref, go_ref, d_ref, mo_ref, vo_ref):
        lo = lo_ref[0] == 1
        for q in range(D_MODEL // 128):
            band = pl.ds(q * 128, 128)
            g = jnp.where(lo, ol_ref[band, :], oh_ref[band, :])
            for k in range(2):
                g = g + jnp.where(lo, rl_ref[k, band, :], rh_ref[k, band, :]).astype(F32)
            g = jnp.pad(g, ((0, 0), (0, pad))).T[0:C]
            rows = pl.ds(q, C, stride=8)
            d, mn, vn = _adamw_math(w_ref[rows, :], g, m_ref[rows, :], v_ref[rows, :])
            go_ref[rows, :] = g
            d_ref[rows, :] = d
            mo_ref[rows, :] = mn
            vo_ref[rows, :] = vn

    return pl.pallas_call(
        body, name="adamw_w_in", out_shape=[jax.ShapeDtypeStruct(w.shape, F32)] * 4,
        in_specs=[pl.BlockSpec(memory_space=pltpu.SMEM)] + [pl.BlockSpec(memory_space=pltpu.VMEM)] * 7,
        out_specs=[pl.BlockSpec(memory_space=pltpu.VMEM)] * 4,
        compiler_params=_params(),
    )(is_lo, own_lo, own_hi, recv_lo, recv_hi, w, m, v)


SMALL = ("conv_b", "dt_bias", "a_log", "d_skip", "ssd_norm_w", "attn_sinks", "ln_g", "ln_b")


def _adamw_small(gathered, params):
    n_p = len(SMALL)

    def body(*refs):
        acc = []
        for r in refs[:5]:
            t = r[0]
            for k in range(1, N_DEV):
                t = t + r[k]
            acc.append(t)
        head, conv, norm, scal, sink = acc
        grads = dict(conv_b=conv[4:5, :], dt_bias=scal[0:1, 0:N_HEADS], a_log=scal[1:2, 0:N_HEADS],
                     d_skip=scal[2:3, 0:N_HEADS], ssd_norm_w=norm[0:1, :], attn_sinks=sink[0:1, 0:N_HEADS],
                     ln_g=head[0:1, :], ln_b=head[1:2, :])
        wmv = refs[5:5 + 3 * n_p]
        outs = refs[5 + 3 * n_p:]
        outs[0][...] = head[3:4, 0:1]
        outs[1][...] = conv[0:4, :]
        for i, name in enumerate(SMALL):
            w_ref, m_ref, v_ref = wmv[3 * i:3 * i + 3]
            g = grads[name]
            d, mn, vn = _adamw_math(w_ref[...], g, m_ref[...], v_ref[...])
            for o_ref, val in zip(outs[2 + 4 * i:6 + 4 * i], (g, d, mn, vn)):
                o_ref[...] = val

    flat = [a for name in SMALL for a in params[name]]
    out_shape = [jax.ShapeDtypeStruct((1, 1), F32), jax.ShapeDtypeStruct((4, D_XBC), F32)]
    for name in SMALL:
        out_shape += [jax.ShapeDtypeStruct(params[name][0].shape, F32)] * 4
    res = pl.pallas_call(body, name="adamw_small", out_shape=out_shape, compiler_params=_params())(*gathered, *flat)
    return res[0], res[1], {name: res[2 + 4 * i:6 + 4 * i] for i, name in enumerate(SMALL)}


def _adamw_plain(g, w, m, v):
    def body(g_ref, w_ref, m_ref, v_ref, d_ref, mo_ref, vo_ref):
        d, mn, vn = _adamw_math(w_ref[...], g_ref[...], m_ref[...], v_ref[...])
        d_ref[...] = d
        mo_ref[...] = mn
        vo_ref[...] = vn

    return pl.pallas_call(
        body, name="adamw_conv_w", out_shape=[jax.ShapeDtypeStruct(w.shape, F32)] * 3,
        compiler_params=_params(),
    )(g, w, m, v)


def _lane_pattern(fn):
    return np.asarray([fn(l % HEAD_DIM) for l in range(128)], np.float32)


ROPE_INV = _lane_pattern(lambda r: ROPE_THETA ** (-2.0 * (r % 8) / ROPE_DIM) if r < ROPE_DIM else 0.0)


def _rope_tables(pos_ref, inv_ref, tab_ref):
    lane = lax.broadcasted_iota(jnp.int32, (1, 128), 1) % HEAD_DIM
    upper = jnp.where((lane >= ROPE_DIM // 2) & (lane < ROPE_DIM), 1.0, 0.0)
    lower = jnp.where(lane < ROPE_DIM // 2, -1.0, 0.0)

    def block(r, carry):
        rows = pl.ds(pl.multiple_of(r * CHUNK, CHUNK), CHUNK)
        pos = jnp.broadcast_to(pos_ref[pl.ds(r, 1), :].astype(F32), (CHUNK, 128)).T
        ang = pos * inv_ref[...]
        sn = jnp.sin(ang)
        tab_ref[rows, 0:128] = jnp.cos(ang)
        tab_ref[rows, 128:256] = sn * upper
        tab_ref[rows, 256:384] = sn * lower
        return carry

    lax.fori_loop(0, pos_ref.shape[0], block, 0)


def _expansion():
    E = np.arange(1024)[None, :] // HEAD_DIM == np.arange(128)[:, None]
    return jnp.asarray(E, BF16), jnp.asarray(E.T, BF16)


def _ssd_args(conv_w, conv_b, dt_bias, a_log, d_skip, norm_w, E):
    return (conv_w, conv_b, dt_bias.reshape(-1), a_log.reshape(-1), d_skip.reshape(-1), norm_w, E)


def kernel(x, positions, w_in, conv_w, conv_b, dt_bias, a_log, d_skip, ssd_norm_w, attn_sinks, w_out, ln_g, ln_b, loss_target, m_w_in, m_conv_w, m_conv_b, m_dt_bias, m_a_log, m_d_skip, m_ssd_norm_w, m_attn_sinks, m_w_out, m_ln_g, m_ln_b, v_w_in, v_conv_w, v_conv_b, v_dt_bias, v_a_log, v_d_skip, v_ssd_norm_w, v_attn_sinks, v_w_out, v_ln_g, v_ln_b):
    me = _index(*_position())
    x0, target = x[0], loss_target[0]
    bf16_shard = lambda shape: jax.ShapeDtypeStruct(shape, BF16)
    E, ET = _expansion()
    sinks = attn_sinks.reshape(-1)

    w_all, tabs = _gather_w_in(w_in[0].astype(BF16), positions[0].reshape(-1, 128))
    w_ssd, w_att = _unpack_w_in(w_all)
    gather_conv_w = _Hosted([conv_w[0]], [jax.ShapeDtypeStruct((N_DEV,) + conv_w.shape[1:], F32)],
                            [_Flow("gather", 0, 0)])

    proj_ssd, proj_att, xb, conv_w_all = _in_proj(x0, w_ssd, w_att, tm=512, comm=gather_conv_w)
    conv_w_f = jnp.transpose(conv_w_all, (1, 0, 2)).reshape(4, D_XBC)
    ssd_args = _ssd_args(conv_w_f, conv_b, dt_bias, a_log, d_skip, ssd_norm_w, E)
    gather_w_out = _Hosted([w_out[0].astype(BF16)], [bf16_shard((N_DEV, 256, D_MODEL))], [_Flow("gather", 0, 0)])
    y, ypre, hprev, pre, w_out_all = _mixer_forward(proj_ssd, proj_att, tabs, sinks, *ssd_args, comm=gather_w_out)
    w_out_f = w_out_all.reshape(2 * D_MODEL, D_MODEL)
    dr, dy, acc_head = _head(y, x0, target, w_out_f, ln_g, ln_b, tm=512)

    dw_out, dw_out_bf16 = _matmul_tn(y, dr, tl=1024, tn=D_MODEL, name="dw_out", emit_bf16=True)
    own_out = lax.dynamic_index_in_dim(dw_out.reshape(N_DEV, 256, D_MODEL), me, axis=0, keepdims=False)
    send_out = _Hosted([dw_out_bf16.reshape(N_DEV, 256, D_MODEL)], [bf16_shard((N_DEV - 1, 256, D_MODEL))],
                       [_Flow("exchange", 0, 0)])
    d_ssd, acc_cw, acc_w, acc_s, recv_out = _ssd_backward(proj_ssd, hprev, ypre, pre, dy, *ssd_args, ET, comm=send_out)
    stack_lo, dw_dt_block = _dw_in(xb, d_ssd, 0)
    reduce_lo = _OwnerReduce(stack_lo, 0, _OwnerReduce.RELAXED)
    d_att, dsink, own_lo, recv_lo = _swa_backward(proj_att, tabs, sinks, dy, reduce=reduce_lo)
    (stack_hi,) = _dw_in(xb, d_att, 1, dw_dt_block)
    accs = [acc_head, acc_cw, acc_w, acc_s, dsink]
    gather_accs = _Hosted(accs, [jax.ShapeDtypeStruct((N_DEV,) + a.shape, F32) for a in accs],
                          [_Flow("gather", i, i) for i in range(5)])
    dx, *gathered, own_hi, recv_hi = _input_gradient(d_ssd, d_att, w_ssd, w_att, dr, tm=256, comm=gather_accs,
                                                     reduce=_OwnerReduce(stack_hi, 1, _OwnerReduce.TIGHT))
    is_lo = (me < 4).reshape(1).astype(jnp.int32)

    g_in, d_in, nm_in, nv_in = [_from_minor_rows_view(r) for r in _adamw_w_in(
        is_lo, own_lo, own_hi, recv_lo, recv_hi, _minor_rows_view(w_in), _minor_rows_view(m_w_in), _minor_rows_view(v_w_in))]
    g_out, d_out, nm_out, nv_out = _adamw_shard(own_out, recv_out, w_out[0], m_w_out[0], v_w_out[0],
                                                rows=256, name="adamw_w_out")
    loss, g_conv_w, small = _adamw_small(gathered, dict(
        conv_b=(conv_b, m_conv_b, v_conv_b), dt_bias=(dt_bias, m_dt_bias, v_dt_bias), a_log=(a_log, m_a_log, v_a_log),
        d_skip=(d_skip, m_d_skip, v_d_skip), ssd_norm_w=(ssd_norm_w, m_ssd_norm_w, v_ssd_norm_w),
        attn_sinks=(attn_sinks, m_attn_sinks, v_attn_sinks), ln_g=(ln_g, m_ln_g, v_ln_g), ln_b=(ln_b, m_ln_b, v_ln_b)))
    g_cw = lax.dynamic_slice_in_dim(g_conv_w, me * (D_XBC // N_DEV), D_XBC // N_DEV, axis=1)
    d_cw, nm_cw, nv_cw = _adamw_plain(g_cw, conv_w[0], m_conv_w[0], v_conv_w[0])

    def leaves(i, big_in, cw, big_out):
        mid = [small[k][i] for k in ("conv_b", "dt_bias", "a_log", "d_skip", "ssd_norm_w", "attn_sinks")]
        return [big_in, cw[None]] + mid + [big_out[None], small["ln_g"][i], small["ln_b"][i]]

    return (loss.reshape(()), dx[None], *leaves(0, g_in, g_cw, g_out), *leaves(1, d_in, d_cw, d_out),
            *leaves(2, nm_in, nm_cw, nm_out), *leaves(3, nv_in, nv_cw, nv_out))
```

```python
import jax
import jax.numpy as jnp
from jax import lax
from jax.experimental import pallas as pl
from jax.experimental.pallas import tpu as pltpu
import numpy as np

F32 = jnp.float32
BF16 = jnp.bfloat16
_MXU = jnp.bfloat16

N_DEV = 8
D_MODEL = 1024
D_SSD = 1024
D_ATT = 1024
HEAD_DIM = 64
N_HEADS = 16
SSD_GROUPS = 2
KV_HEADS = 4
CHUNK = 128
D_XBC = 1536
D_IN_PROJ = 5136
ROPE_DIM = 16
ROPE_THETA = 500000.0
ALPHA = (2.0 * 1) ** 0.25
LN_EPS = 1e-5
RMS_EPS = 1e-5
ATT_SCALE = HEAD_DIM ** -0.5
NEG = -1e30

S_Z, S_XS, S_B, S_C, S_DT, S_W = 0, 1024, 2048, 2304, 2560, 2816
N_SSD_REAL = 2576
A_Q, A_K, A_V, A_G, A_W = 0, 1024, 1280, 1536, 2560

ADAM_LR = 0.001
ADAM_B1 = 0.9
ADAM_B2 = 0.999
ADAM_EPS = 1e-08
ADAM_WD = 0.01
ADAM_STEP = 10

VMEM_LIMIT = 48 * 1024 * 1024
MESH = pl.DeviceIdType.MESH


def _params(sem=None):
    return pltpu.CompilerParams(dimension_semantics=sem, vmem_limit_bytes=VMEM_LIMIT)


def _mm(a, b):
    return jnp.dot(a.astype(_MXU), b.astype(_MXU), preferred_element_type=F32)


def _mm_nt(a, b):
    return lax.dot_general(a.astype(_MXU), b.astype(_MXU), (((1,), (1,)), ((), ())),
                           preferred_element_type=F32)


def _mm_tn(a, b):
    return lax.dot_general(a.astype(_MXU), b.astype(_MXU), (((0,), (0,)), ((), ())),
                           preferred_element_type=F32)


def _split3(v):
    hi = v.astype(BF16)
    r = v - hi.astype(F32)
    mid = r.astype(BF16)
    lo = (r - mid.astype(F32)).astype(BF16)
    return hi, mid, lo


def _mm_exact_r(v, p01):
    hi, mid, lo = _split3(v)
    d = lambda a: jnp.dot(a, p01, preferred_element_type=F32)
    return d(hi) + d(mid) + d(lo)


def _mm_exact_l(p01, v):
    hi, mid, lo = _split3(v)
    d = lambda a: jnp.dot(p01, a, preferred_element_type=F32)
    return d(hi) + d(mid) + d(lo)


def _mm_2pass_r(v, p01):
    hi = v.astype(BF16)
    lo = (v - hi.astype(F32)).astype(BF16)
    return jnp.dot(hi, p01, preferred_element_type=F32) + jnp.dot(lo, p01, preferred_element_type=F32)


def _sigmoid(x):
    return 1.0 / (1.0 + jnp.exp(-x))


def _softplus(x):
    e = jnp.exp(-jnp.abs(x))
    u = 1.0 + e
    log1p = jnp.where(u == 1.0, e, jnp.log(u) * (e / (u - 1.0)))
    return jnp.maximum(x, 0.0) + log1p


def _rows8(rows):
    n = rows[0].shape[1]
    rid = lax.broadcasted_iota(jnp.int32, (8, n), 0)
    out = jnp.zeros((8, n), F32)
    for k, r in enumerate(rows):
        out = out + jnp.where(rid == k, r, 0.0)
    return out


def _colsum(a):
    return jnp.sum(a, axis=0, keepdims=True)


def _in_proj(x, w_ssd, w_att, *, tm, comm=None):
    L, K = x.shape

    def body(x_ref, ws_ref, wa_ref, ps_ref, pa_ref, xb_ref):
        xb = x_ref[...].astype(_MXU)
        xb_ref[...] = xb
        ps_ref[...] = jnp.dot(xb, ws_ref[...], preferred_element_type=F32)
        pa_ref[...] = jnp.dot(xb, wa_ref[...], preferred_element_type=F32)

    row = lambda w: pl.BlockSpec((tm, w), lambda i: (i, 0))
    resident = lambda a: pl.BlockSpec(a.shape, lambda i: (0, 0), pipeline_mode=pl.Buffered(1))
    return _call(
        body, comm, name="in_proj", grid=(L // tm,),
        in_specs=[row(K), resident(w_ssd), resident(w_att)], out_specs=[row(S_W), row(A_W), row(K)],
        out_shape=[jax.ShapeDtypeStruct((L, S_W), F32), jax.ShapeDtypeStruct((L, A_W), F32),
                   jax.ShapeDtypeStruct((L, K), _MXU)],
        scratch_shapes=[], args=(x, w_ssd, w_att))


def _matmul_tn(a, g, *, tl, tn, name, emit_bf16=False):
    L, M = a.shape
    N = g.shape[1]
    last = L // tl - 1

    def body(a_ref, g_ref, o_ref, *rest):
        @pl.when(pl.program_id(1) == 0)
        def _():
            o_ref[...] = jnp.zeros_like(o_ref)

        o_ref[...] += _mm_tn(a_ref[...], g_ref[...])
        if emit_bf16:
            @pl.when(pl.program_id(1) == last)
            def _():
                rest[0][...] = o_ref[...].astype(BF16)

    spec = pl.BlockSpec((M, tn), lambda j, l: (0, j))
    res = pl.pallas_call(
        body, name=name, grid=(N // tn, L // tl),
        in_specs=[pl.BlockSpec((tl, M), lambda j, l: (l, 0)), pl.BlockSpec((tl, tn), lambda j, l: (l, j))],
        out_specs=[spec, spec] if emit_bf16 else [spec],
        out_shape=[jax.ShapeDtypeStruct((M, N), F32)] + ([jax.ShapeDtypeStruct((M, N), BF16)] if emit_bf16 else []),
        compiler_params=_params(("arbitrary", "arbitrary")),
    )(a, g)
    return res if emit_bf16 else res[0]


def _position():
    return lax.axis_index("x"), lax.axis_index("y"), lax.axis_index("c")


def _index(px, py, pc):
    return 4 * px + 2 * py + pc


def _flip(pos, k):
    x, y, c = pos
    return ((1 - x) if (k >> 2) & 1 else x, (1 - y) if (k >> 1) & 1 else y, (1 - c) if k & 1 else c)


def _remote(src, dst, send_sem, recv_sem, peer):
    return pltpu.make_async_remote_copy(src_ref=src, dst_ref=dst, send_sem=send_sem, recv_sem=recv_sem,
                                        device_id=peer, device_id_type=MESH)


class _Flow:
    def __init__(self, kind, operand, result):
        self.kind, self.operand, self.result = kind, operand, result


class _Hosted:
    def __init__(self, operands, out_shapes, flows):
        self.operands, self.out_shapes, self.flows = operands, out_shapes, flows

    def plan(self, ins, outs, send_sems, recv_sems, local_sems):
        me = _position()
        mi = _index(*me)
        sends, recvs, locals_ = [], [], []
        for row, f in enumerate(self.flows):
            src, dst = ins[f.operand], outs[f.result]
            for k in range(1, N_DEV):
                peer = _flip(me, k)
                sems = (send_sems.at[row, k - 1], recv_sems.at[row, k - 1])
                if f.kind == "exchange":
                    sends.append(_remote(src.at[_index(*peer)], dst.at[k - 1], *sems, peer))
                    recvs.append(sends[-1])
                else:
                    sends.append(_remote(src, dst.at[mi], *sems, peer))
                    recvs.append(_remote(src, dst.at[_index(*peer)], *sems, peer))
            if f.kind == "gather":
                locals_.append(pltpu.make_async_copy(src, dst.at[mi], local_sems.at[row]))

        def start():
            for cp in locals_ + sends:
                cp.start()

        def wait():
            for cp in recvs:
                cp.wait_recv()
            for cp in sends:
                cp.wait_send()
            for cp in locals_:
                cp.wait()

        return start, wait


class _OwnerReduce:
    FIRST_STEP, SECOND_STEP, SEND_STEPS, REDUCE_STEPS = 2, 4, (2, 4, 6, 8), (5, 8, 10, 13)

    def __init__(self, stack, target_x):
        self.stack, self.target_x = stack, target_x
        block = stack.shape[1:]
        self.chunks = len(self.SEND_STEPS)
        self.chunk_rows = block[0] // self.chunks
        self.out_shapes = [jax.ShapeDtypeStruct(block, F32), jax.ShapeDtypeStruct((2,) + block, BF16)]
        self.out_specs = [pl.BlockSpec(block, lambda *_: (0, 0), pipeline_mode=pl.Buffered(1)),
                          pl.BlockSpec(memory_space=pl.ANY)]
        dma = pltpu.SemaphoreType.DMA
        self.scratch_shapes = ([pltpu.VMEM((2,) + block, F32)] * 2 + [pltpu.VMEM(block, BF16)] * 3
                               + [dma((self.chunks,))] * 4 + [dma((2,))] * 3)

    def plan(self, i, steps, stack_ref, own_ref, recv_ref, scratch):
        assert self.FIRST_STEP <= self.SEND_STEPS[0] and self.SECOND_STEP < self.REDUCE_STEPS[0] < steps - 1
        (theirs_scr, mine_scr, first_scr, across_scr, out_scr, y_send_sems, y_recv_sems, x_send_sems, x_recv_sems,
         swap_send_sems, swap_recv_sems, mine_sems) = scratch
        x, y, c = _position()
        owners_side = x == self.target_x
        other_side = x != self.target_x
        sibling, across, owner = (x, y, 1 - c), (x, 1 - y, c), (self.target_x, y, c)
        order = (1 - y, y)
        swaps = [_remote(stack_ref.at[2 * order[j] + (1 - c)], theirs_scr.at[j], swap_send_sems.at[j], swap_recv_sems.at[j],
                         sibling) for j in range(2)]
        mine = [pltpu.make_async_copy(stack_ref.at[2 * order[j] + c], mine_scr.at[j], mine_sems.at[j]) for j in range(2)]
        chunks = range(self.chunks)
        part = [pl.ds(j * self.chunk_rows, self.chunk_rows) for j in chunks]
        y_sems = lambda j: (y_send_sems.at[j], y_recv_sems.at[j])
        to_neighbour = [_remote(first_scr.at[part[j]], across_scr.at[part[j]], *y_sems(j), across) for j in chunks]
        to_owner_y = [_remote(first_scr.at[part[j]], recv_ref.at[0, part[j]], *y_sems(j), across) for j in chunks]
        to_owner_x = [_remote(out_scr.at[part[j]], recv_ref.at[1, part[j]], x_send_sems.at[j], x_recv_sems.at[j], owner)
                      for j in chunks]

        def before():
            @pl.when(i <= 1)
            def _():
                @pl.when(i == 0)
                def _():
                    for cp in [swaps[0]] + mine:
                        cp.start()

                pl.when(i == 1)(swaps[1].start)

        def after():
            @pl.when(i <= self.REDUCE_STEPS[-1])
            def _():
                @pl.when(i == self.FIRST_STEP)
                def _():
                    swaps[0].wait_recv()
                    mine[0].wait()
                    first_scr[...] = (mine_scr[0] + theirs_scr[0]).astype(first_scr.dtype)

                for j in chunks:
                    pl.when((i == self.SEND_STEPS[j]) & other_side)(to_neighbour[j].start)
                    pl.when((i == self.SEND_STEPS[j]) & owners_side)(to_owner_y[j].start)

                @pl.when(i == self.SECOND_STEP)
                def _():
                    swaps[1].wait_recv()
                    mine[1].wait()
                    t = mine_scr[1] + theirs_scr[1]
                    own_ref[...] = t
                    mine_scr[1] = t

                for j in chunks:
                    @pl.when((i == self.REDUCE_STEPS[j]) & other_side)
                    def _(j=j):
                        to_neighbour[j].wait_recv()
                        t = mine_scr[1, part[j], :] + across_scr[part[j], :].astype(F32)
                        out_scr[part[j], :] = t.astype(out_scr.dtype)
                        to_owner_x[j].start()

            @pl.when(i == steps - 1)
            def _():
                for cp in swaps:
                    cp.wait_send()
                for j in chunks:
                    @pl.when(other_side)
                    def _(j=j):
                        to_neighbour[j].wait_send()
                        to_owner_x[j].wait_send()

                    @pl.when(owners_side)
                    def _(j=j):
                        to_owner_y[j].wait_send()
                        to_owner_y[j].wait_recv()
                        to_owner_x[j].wait_recv()

        return before, after


def _call(body, comm, *, name, grid, in_specs, out_specs, out_shape, scratch_shapes, args, reduce=None):
    semantics = ("arbitrary",) * len(grid)
    if comm is None and reduce is None:
        return pl.pallas_call(body, name=name, grid=grid, in_specs=in_specs, out_specs=out_specs, out_shape=out_shape,
                              scratch_shapes=scratch_shapes, compiler_params=_params(semantics))(*args)
    n_in, n_out, n_scr = len(args), len(out_shape), len(scratch_shapes)
    c_operands, c_shapes, flows = (comm.operands, comm.out_shapes, comm.flows) if comm else ([], [], [])
    c_in, c_out, rows = len(c_operands), len(c_shapes), max(len(flows), 1)
    r_in = 0 if reduce is None else 1

    def hosted(*refs):
        ins, refs = refs[:n_in], refs[n_in:]
        cins, refs = refs[:c_in], refs[c_in:]
        rins, refs = refs[:r_in], refs[r_in:]
        outs, refs = refs[:n_out], refs[n_out:]
        couts, refs = refs[:c_out], refs[c_out:]
        routs, refs = refs[:2 * r_in], refs[2 * r_in:]
        scr, refs = refs[:n_scr], refs[n_scr:]
        (send_sems, recv_sems, local_sems), r_scr = refs[:3], refs[3:]
        ids = [pl.program_id(d) for d in range(len(grid))]
        first, last = ids[0] == 0, ids[0] == grid[0] - 1
        for d in range(1, len(grid)):
            first, last = first & (ids[d] == 0), last & (ids[d] == grid[d] - 1)
        before = after = lambda: None
        if reduce is not None:
            before, after = reduce.plan(ids[0], grid[0], rins[0], *routs, r_scr)
        if comm is not None:
            start, wait = comm.plan(cins, couts, send_sems, recv_sems, local_sems)
            pl.when(first)(start)
        before()
        body(*ins, *outs, *scr)
        after()
        if comm is not None:
            pl.when(last)(wait)

    any_spec = pl.BlockSpec(memory_space=pl.ANY)
    sems = [pltpu.SemaphoreType.DMA((rows, N_DEV - 1)), pltpu.SemaphoreType.DMA((rows, N_DEV - 1)),
            pltpu.SemaphoreType.DMA((rows,))]
    r_operands, r_specs, r_shapes, r_scratch = ([reduce.stack], reduce.out_specs, reduce.out_shapes,
                                                reduce.scratch_shapes) if reduce else ([], [], [], [])
    return pl.pallas_call(
        hosted, name=name, grid=grid, in_specs=list(in_specs) + [any_spec] * (c_in + r_in),
        out_specs=list(out_specs) + [any_spec] * c_out + r_specs, out_shape=list(out_shape) + list(c_shapes) + r_shapes,
        scratch_shapes=list(scratch_shapes) + sems + r_scratch,
        compiler_params=_params(semantics))(*args, *c_operands, *r_operands)


def _head_row(ref, width, rep):
    hid = lax.broadcasted_iota(jnp.int32, (1, width), 1) // rep
    row = jnp.zeros((1, width), F32)
    for h in range(N_HEADS):
        row = jnp.where(hid == h, ref[h], row)
    return row


def _rows_from_above(u_b, s, ext_scr, row, col):
    down = (row - col == s).astype(_MXU)
    return jnp.concatenate([ext_scr[8 - s:16 - s, :], jnp.dot(down, u_b, preferred_element_type=F32)[8:128]], axis=0)


def _ssd_recompute(first, p_ref, halo_ref, cw_ref, cb_ref, dtb_ref, alog_ref, e_ref, ext_scr, pre=None):
    row = lax.broadcasted_iota(jnp.int32, (128, 128), 0)
    col = lax.broadcasted_iota(jnp.int32, (128, 128), 1)
    ext_scr[0:8, :] = jnp.where(first, 0.0, halo_ref[:, S_XS:S_DT])
    if pre is not None:
        ext_scr[8:16, :] = p_ref[0:8, S_XS:S_DT]
    else:
        ext_scr[8:136, :] = p_ref[:, S_XS:S_DT]
        cw = cw_ref[...]
        pre = (cb_ref[0:1, :] + cw[3:4, :] * ext_scr[8:136, :] + cw[2:3, :] * ext_scr[7:135, :]
               + cw[1:2, :] * ext_scr[6:134, :] + cw[0:1, :] * ext_scr[5:133, :])
    sg = _sigmoid(pre)
    act = pre * sg
    lane = lax.broadcasted_iota(jnp.int32, (1, 128), 1)
    A = jnp.where(lane < N_HEADS, -jnp.exp(_head_row(alog_ref, 128, 1)), 0.0)
    raw = p_ref[:, S_DT:S_DT + 128] + _head_row(dtb_ref, 128, 1)
    dt = _softplus(raw)
    dA = dt * A
    tril = (row >= col).astype(BF16)
    acs = _mm_exact_l(tril, dA)
    last = acs[127:128, :]
    ds = jnp.exp(last - acs)
    eo = jnp.exp(acs)
    E = e_ref[...]
    ex = _mm_2pass_r(jnp.concatenate([dt, ds, eo], axis=0), E)
    dt_e, ds_e, eo_e = ex[0:128], ex[128:256], ex[256:384]
    xs_c = act[:, 0:1024]
    X = xs_c * dt_e
    return dict(pre=pre, sg=sg, xs_c=xs_c, Bc=act[:, 1024:1280], Cc=act[:, 1280:1536], A=A, raw=raw, dt=dt,
                acs=acs, acsT=acs.T, eo_e=eo_e, ds_e=ds_e, dt_e=dt_e, cd_e=eo_e[127:128, :],
                X=X, Xd=X * ds_e, row=row, col=col)


def _split_halves(t):
    lo = _lo_half(CHUNK)
    return jnp.concatenate([jnp.where(lo, t, 0.0), jnp.where(lo, 0.0, t)], axis=0)


def _ssd_core(R, hprev):
    causal = R["row"] >= R["col"]
    acs, acsT, X = R["acs"], R["acsT"], R["X"]
    ydiag, yoff, snew = [], [], []
    for g in range(SSD_GROUPS):
        Bg = R["Bc"][:, g * 128:(g + 1) * 128]
        Cg = R["Cc"][:, g * 128:(g + 1) * 128]
        cols = slice(g * 512, (g + 1) * 512)
        CB = _mm_nt(Cg, Bg)
        snew.append(_mm_tn(Bg, R["Xd"][:, cols]))
        yoff.append(_mm(Cg, hprev[:, cols]))
        for j in range(4):
            h0 = g * 8 + 2 * j
            ms = [CB * jnp.exp(jnp.where(causal, acs[:, h:h + 1] - acsT[h:h + 1, :], NEG)) for h in (h0, h0 + 1)]
            ydiag.append(_mm(jnp.concatenate(ms, axis=1), _split_halves(X[:, h0 * HEAD_DIM:h0 * HEAD_DIM + 128])))
    Y = jnp.concatenate(ydiag, axis=1) + jnp.concatenate(yoff, axis=1) * R["eo_e"]
    return Y, jnp.concatenate(snew, axis=1)


def _ssd_forward_step(p_ref, halo_ref, cw_ref, cb_ref, dtb_ref, alog_ref, dsk_ref, nw_ref, e_ref,
                      y_ref, ypre_ref, hprev_ref, pre_ref, h_scr, ext_scr):
    c = pl.program_id(0)
    first = c == 0

    @pl.when(first)
    def _():
        h_scr[...] = jnp.zeros_like(h_scr)

    R = _ssd_recompute(first, p_ref, halo_ref, cw_ref, cb_ref, dtb_ref, alog_ref, e_ref, ext_scr)
    hprev = h_scr[...]
    hprev_ref[...] = hprev
    pre_ref[...] = R["pre"]
    Y, snew = _ssd_core(R, hprev)
    h_scr[...] = hprev * R["cd_e"] + snew
    Y = Y + _head_row(dsk_ref, D_SSD, HEAD_DIM) * R["xs_c"]
    ypre_ref[...] = Y
    z = p_ref[:, S_Z:S_Z + 1024]
    yf = Y * (z * _sigmoid(z))
    outs = []
    for g in range(SSD_GROUPS):
        yg = yf[:, g * 512:(g + 1) * 512]
        r = lax.rsqrt(jnp.mean(yg * yg, axis=-1, keepdims=True) + RMS_EPS)
        outs.append(yg * r)
    y_ref[:, 0:D_SSD] = (jnp.concatenate(outs, axis=1) * nw_ref[0:1, :]).astype(y_ref.dtype)


def _ssd_backward(proj_ssd, hprev_all, ypre, pre, dy, conv_w, conv_b, dt_bias, a_log, d_skip, norm_w, E, ET, comm=None):
    L = proj_ssd.shape[0]
    nc = L // CHUNK

    def body(p_ref, halo_ref, hprev_ref, ypre_ref, pre_ref, dy_ref, cw_ref, cb_ref, dtb_ref, alog_ref, dsk_ref, nw_ref, e_ref,
             et_ref, dp_ref, acc_cw_ref, acc_w_ref, acc_s_ref, dh_scr, ext_scr, ext2_scr, nxt_scr):
        i = pl.program_id(0)
        c = nc - 1 - i
        first = c == 0

        @pl.when(i == 0)
        def _():
            dh_scr[...] = jnp.zeros_like(dh_scr)
            nxt_scr[...] = jnp.zeros_like(nxt_scr)
            acc_cw_ref[...] = jnp.zeros_like(acc_cw_ref)
            acc_w_ref[...] = jnp.zeros_like(acc_w_ref)
            acc_s_ref[...] = jnp.zeros_like(acc_s_ref)

        R = _ssd_recompute(first, p_ref, halo_ref, cw_ref, cb_ref, dtb_ref, alog_ref, e_ref, ext_scr, pre_ref[...])
        hprev = hprev_ref[...]
        xs_c, X, Xd = R["xs_c"], R["X"], R["Xd"]
        acs, acsT = R["acs"], R["acsT"]
        ET = et_ref[...]
        dsk = _head_row(dsk_ref, D_SSD, HEAD_DIM)
        Y = ypre_ref[...]

        z = p_ref[:, S_Z:S_Z + 1024]
        sz = _sigmoid(z)
        silz = z * sz
        yf = Y * silz
        dyv = dy_ref[...]
        nw = nw_ref[0:1, :]
        dyf_parts, dnw_parts = [], []
        for g in range(SSD_GROUPS):
            cols = slice(g * 512, (g + 1) * 512)
            yg = yf[:, cols]
            r = lax.rsqrt(jnp.mean(yg * yg, axis=-1, keepdims=True) + RMS_EPS)
            yn = yg * r
            dyn = dyv[:, cols] * nw[:, cols]
            dnw_parts.append(_colsum(dyv[:, cols] * yn))
            dyf_parts.append(r * (dyn - yn * jnp.mean(dyn * yn, axis=-1, keepdims=True)))
        dyf = jnp.concatenate(dyf_parts, axis=1)
        dY = dyf * silz
        dz = dyf * Y * (sz * (1.0 + z * (1.0 - sz)))

        dhn = dh_scr[...]
        dYo = dY * R["eo_e"]
        causal = R["row"] >= R["col"]
        dacs = jnp.zeros((128, 128), F32)
        dacs_t = jnp.zeros((128, 128), F32)
        dxdiag, dxd, dhprev, dBs, dCs, yoff = [], [], [], [], [], []
        for g in range(SSD_GROUPS):
            Bg = R["Bc"][:, g * 128:(g + 1) * 128]
            Cg = R["Cc"][:, g * 128:(g + 1) * 128]
            cols = slice(g * 512, (g + 1) * 512)
            CB = _mm_nt(Cg, Bg)
            dCB = jnp.zeros((128, 128), F32)
            for j in range(4):
                h0 = g * 8 + 2 * j
                pc = slice(h0 * HEAD_DIM, h0 * HEAD_DIM + 128)
                dYst = _split_halves(dY[:, pc])
                dMst = _mm_nt(dYst, X[:, pc])
                mts = []
                for a, h in enumerate((h0, h0 + 1)):
                    acol = acs[:, h:h + 1]
                    arow = acsT[h:h + 1, :]
                    Lm = jnp.exp(jnp.where(causal, acol - arow, NEG))
                    M = CB * Lm
                    dM = dMst[a * 128:(a + 1) * 128]
                    dCB = dCB + dM * Lm
                    G = dM * M
                    dacs = dacs + jnp.where(R["col"] == h, jnp.sum(G, axis=1, keepdims=True), 0.0)
                    dacs_t = dacs_t + jnp.where(R["row"] == h, jnp.sum(G, axis=0, keepdims=True), 0.0)
                    mts.append(M.T)
                dxdiag.append(_mm(jnp.concatenate(mts, axis=1), dYst))
            dS = dhn[:, cols]
            dxd.append(_mm(Bg, dS))
            yoff.append(_mm(Cg, hprev[:, cols]))
            dhprev.append(_mm_tn(Cg, dYo[:, cols]))
            dCs.append(_mm_nt(dYo[:, cols], hprev[:, cols]) + _mm(dCB, Bg))
            dBs.append(_mm_tn(dCB, Cg) + _mm_nt(Xd[:, cols], dS))
        Yoff = jnp.concatenate(yoff, axis=1) * R["eo_e"]
        dXd = jnp.concatenate(dxd, axis=1)
        dX = jnp.concatenate(dxdiag, axis=1) + dXd * R["ds_e"]
        t_state = dXd * Xd
        dacs = dacs + _mm_2pass_r(dY * Yoff - t_state, ET) - dacs_t.T
        v_last = _colsum(t_state + dhn * hprev * R["cd_e"])
        dlast = _mm_exact_r(jnp.broadcast_to(v_last, (8, 1024)), ET)[0:1, :]
        dacs = dacs + jnp.where(R["row"] == 127, dlast, 0.0)
        triu = (R["col"] >= R["row"]).astype(BF16)
        da = _mm_exact_l(triu, dacs)
        ddt = da * R["A"] + _mm(dX * xs_c, ET)
        ddt_raw = ddt * _sigmoid(R["raw"])
        dxs_c = dX * R["dt_e"] + dY * dsk
        dh_scr[...] = jnp.concatenate(dhprev, axis=1) + dhn * R["cd_e"]

        dact = jnp.concatenate([dxs_c] + dBs + dCs, axis=1)
        pre, sg = R["pre"], R["sg"]
        dpre = dact * (sg * (1.0 + pre * (1.0 - sg)))
        ext2_scr[0:8, :] = dpre[120:128, :]
        ext2_scr[8:16, :] = nxt_scr[...]
        nxt_scr[...] = dpre[0:8, :]
        cw = cw_ref[...]
        u_b, dpre_b = p_ref[:, S_XS:S_DT].astype(_MXU), dpre.astype(_MXU)
        dxbc = cw[3:4, :] * dpre
        taps = [_colsum(dpre * p_ref[:, S_XS:S_DT])]
        for s in (1, 2, 3):
            up = (R["col"] - R["row"] == s).astype(_MXU)
            d_s = jnp.concatenate([jnp.dot(up, dpre_b, preferred_element_type=F32)[0:120],
                                   ext2_scr[s:8 + s, :]], axis=0)
            dxbc = dxbc + cw[3 - s:4 - s, :] * d_s
            taps.append(_colsum(dpre * _rows_from_above(u_b, s, ext_scr, R["row"], R["col"])))
        acc_cw_ref[...] += _rows8(taps[::-1] + [_colsum(dpre)])
        acc_w_ref[...] += _rows8([jnp.concatenate(dnw_parts, axis=1), _colsum(dY * xs_c)])
        acc_s_ref[...] += _rows8([_colsum(ddt_raw), _colsum(da * R["dt"])])

        lane = lax.broadcasted_iota(jnp.int32, (128, 128), 1)
        dp_ref[:, S_Z:S_Z + 1024] = dz.astype(dp_ref.dtype)
        dp_ref[:, S_XS:S_DT] = dxbc.astype(dp_ref.dtype)
        dp_ref[:, S_DT:S_DT + 128] = jnp.where(lane < N_HEADS, ddt_raw, 0.0).astype(dp_ref.dtype)
        dp_ref[:, S_DT + 128:S_W] = jnp.zeros((128, 128), dp_ref.dtype)

        @pl.when(i == nc - 1)
        def _():
            acc = acc_s_ref[...]
            dskip = _mm_exact_r(acc_w_ref[...], ET)[1:2, :]
            acc_s_ref[...] = _rows8([acc[0:1, :], acc[1:2, :] * R["A"], dskip])

    const = lambda shape: pl.BlockSpec(shape, lambda i: (0, 0))
    smem = pl.BlockSpec(memory_space=pltpu.SMEM)
    rev = lambda i: (nc - 1 - i, 0)
    return _call(
        body, comm, name="ssd_bwd", grid=(nc,),
        in_specs=[pl.BlockSpec((CHUNK, S_W), rev),
                  pl.BlockSpec((8, S_W), lambda i: (jnp.maximum((nc - 1 - i) * 16 - 1, 0), 0)),
                  pl.BlockSpec((128, 1024), rev),
                  pl.BlockSpec((CHUNK, D_SSD), rev),
                  pl.BlockSpec((CHUNK, D_XBC), rev),
                  pl.BlockSpec((CHUNK, D_SSD), rev),
                  const((4, D_XBC)), const((1, D_XBC)), smem, smem, smem, const((1, 1024)),
                  const((128, 1024)), const((1024, 128))],
        out_specs=[pl.BlockSpec((CHUNK, S_W), rev), const((8, D_XBC)), const((8, 1024)), const((8, 128))],
        out_shape=[jax.ShapeDtypeStruct((L, S_W), _MXU), jax.ShapeDtypeStruct((8, D_XBC), F32),
                   jax.ShapeDtypeStruct((8, 1024), F32), jax.ShapeDtypeStruct((8, 128), F32)],
        scratch_shapes=[pltpu.VMEM((128, 1024), F32), pltpu.VMEM((16, D_XBC), F32),
                        pltpu.VMEM((16, D_XBC), F32), pltpu.VMEM((8, D_XBC), F32)],
        args=(proj_ssd, proj_ssd, hprev_all, ypre, pre, dy, conv_w, conv_b, dt_bias, a_log, d_skip, norm_w, E, ET))


def _rope(t, tab):
    cos, sa, sb = tab[:, 0:128], tab[:, 128:256], tab[:, 256:384]
    outs = []
    for i in range(t.shape[1] // 128):
        tg = t[:, i * 128:(i + 1) * 128]
        outs.append(tg * cos + pltpu.roll(tg, 8, 1) * sa + pltpu.roll(tg, 120, 1) * sb)
    return jnp.concatenate(outs, axis=1)


def _rope_transposed(d, tab):
    cos, sa, sb = tab[:, 0:128], tab[:, 128:256], tab[:, 256:384]
    outs = []
    for i in range(d.shape[1] // 128):
        dg = d[:, i * 128:(i + 1) * 128]
        outs.append(dg * cos + pltpu.roll(dg * sa, 120, 1) + pltpu.roll(dg * sb, 8, 1))
    return jnp.concatenate(outs, axis=1)


def _lo_half(rows):
    return lax.broadcasted_iota(jnp.int32, (rows, 128), 1) < HEAD_DIM


def _native_half(rows, j):
    lo = _lo_half(rows)
    return lo if j % 2 == 0 else jnp.logical_not(lo)


def _kv_native(t, j):
    p = j // 2
    return jnp.where(_native_half(t.shape[0], j), t[:, p * 128:(p + 1) * 128], 0.0)


def _stack_heads(t, j):
    out = []
    for m in (2 * j, 2 * j + 1):
        pair = t[:, m * 128:(m + 1) * 128]
        swapped = pltpu.roll(pair, HEAD_DIM, 1)
        out += [pair, swapped] if j % 2 == 0 else [swapped, pair]
    return jnp.concatenate(out, axis=0)


def _unstack_heads(s, j):
    out = []
    for m in range(2):
        first, second = s[256 * m:256 * m + 128], s[256 * m + 128:256 * m + 256]
        if j % 2 == 0:
            out.append(first + pltpu.roll(second, HEAD_DIM, 1))
        else:
            out.append(pltpu.roll(first, HEAD_DIM, 1) + second)
    return jnp.concatenate(out, axis=1)


def _keep_native(r, j):
    return jnp.where(_native_half(r.shape[0], j), r, 0.0)


def _sink_row(sink_ref, j):
    hid = lax.broadcasted_iota(jnp.int32, (1, 4 * CHUNK), 1) // CHUNK
    row = jnp.zeros((1, 4 * CHUNK), F32)
    for hh in range(4):
        row = jnp.where(hid == hh, sink_ref[4 * j + hh], row)
    return row


def _from_current():
    si = lax.broadcasted_iota(jnp.int32, (CHUNK, 4 * CHUNK), 0)
    qi = lax.broadcasted_iota(jnp.int32, (CHUNK, 4 * CHUNK), 1) % CHUNK
    return si <= qi


def _fold(full, from_cur, pen=0.0):
    return jnp.where(from_cur, full[CHUNK:2 * CHUNK], full[0:CHUNK] + pen)


def _unfold(t, from_cur):
    c = jnp.where(from_cur, t, 0.0)
    return jnp.concatenate([t - c, c], axis=0)


def _softmax_sink(s, sink):
    mx = jnp.maximum(jnp.max(s, axis=0, keepdims=True), sink)
    p = jnp.exp(s - mx)
    esink = jnp.exp(sink - mx)
    inv = 1.0 / (jnp.sum(p, axis=0, keepdims=True) + esink)
    return p * inv, esink * inv


def _swa_inputs(blk, p_ref, prev_ref, tab_ref, ptab_ref):
    tab = tab_ref[...]
    qr = _rope(p_ref[:, A_Q:A_Q + 1024], tab) * ATT_SCALE
    kk = jnp.concatenate([_rope(prev_ref[:, 0:256], ptab_ref[...]), _rope(p_ref[:, A_K:A_K + 256], tab)], axis=0)
    vv = jnp.concatenate([prev_ref[:, 256:512], p_ref[:, A_V:A_V + 256]], axis=0)
    return tab, qr, kk, vv, jnp.where(blk > 0, 0.0, NEG)


def _swa_forward_step(sink_ref, p_ref, prev_ref, tab_ref, ptab_ref, y_ref):
    n = pl.program_id(0)
    _, qr, kk, vv, pen = _swa_inputs(n, p_ref, prev_ref, tab_ref, ptab_ref)
    from_cur = _from_current()
    outs = []
    for j in range(KV_HEADS):
        s = _fold(_mm_nt(_kv_native(kk, j), _stack_heads(qr, j)), from_cur, pen)
        P, _ = _softmax_sink(s, _sink_row(sink_ref, j))
        outs.append(_unstack_heads(_mm_tn(_unfold(P, from_cur), _kv_native(vv, j)), j))
    g = p_ref[:, A_G:A_G + 1024]
    y_ref[:, D_SSD:D_SSD + D_ATT] = (jnp.concatenate(outs, axis=1) * (g * _sigmoid(g))).astype(y_ref.dtype)


def _mixer_forward(proj_ssd, proj_att, tabs, sinks, conv_w, conv_b, dt_bias, a_log, d_skip, norm_w, E, comm=None):
    L = proj_ssd.shape[0]
    nc = L // CHUNK

    def body(p_ref, halo_ref, cw_ref, cb_ref, dtb_ref, alog_ref, dsk_ref, nw_ref, e_ref,
             sink_ref, pa_ref, prev_ref, tab_ref, ptab_ref, y_ref, ypre_ref, hprev_ref, pre_ref, h_scr, ext_scr):
        _ssd_forward_step(p_ref, halo_ref, cw_ref, cb_ref, dtb_ref, alog_ref, dsk_ref, nw_ref, e_ref,
                          y_ref, ypre_ref, hprev_ref, pre_ref, h_scr, ext_scr)
        _swa_forward_step(sink_ref, pa_ref, prev_ref, tab_ref, ptab_ref, y_ref)

    const = lambda shape: pl.BlockSpec(shape, lambda c: (0, 0))
    smem = pl.BlockSpec(memory_space=pltpu.SMEM)
    rows = lambda w: pl.BlockSpec((CHUNK, w), lambda c: (c, 0))
    return _call(
        body, comm, name="mixer_fwd", grid=(nc,),
        in_specs=[rows(S_W), pl.BlockSpec((8, S_W), lambda c: (jnp.maximum(c * 16 - 1, 0), 0)),
                  const((4, D_XBC)), const((1, D_XBC)), smem, smem, smem, const((1, 1024)), const((128, 1024)),
                  smem, rows(A_W), pl.BlockSpec((CHUNK, 512), lambda c: (jnp.maximum(c - 1, 0), 2)),
                  rows(384), pl.BlockSpec((CHUNK, 384), lambda c: (jnp.maximum(c - 1, 0), 0))],
        out_specs=[rows(D_SSD + D_ATT), rows(D_SSD), pl.BlockSpec((128, 1024), lambda c: (c, 0)), rows(D_XBC)],
        out_shape=[jax.ShapeDtypeStruct((L, D_SSD + D_ATT), _MXU), jax.ShapeDtypeStruct((L, D_SSD), F32),
                   jax.ShapeDtypeStruct((nc * 128, 1024), F32), jax.ShapeDtypeStruct((L, D_XBC), F32)],
        scratch_shapes=[pltpu.VMEM((128, 1024), F32), pltpu.VMEM((136, D_XBC), F32)],
        args=(proj_ssd, proj_ssd, conv_w, conv_b, dt_bias, a_log, d_skip, norm_w, E,
              sinks, proj_att, proj_att, tabs, tabs))


def _swa_backward(proj_att, tabs, sinks, dy, reduce=None):
    L = proj_att.shape[0]
    nb = L // CHUNK

    def body(sink_ref, p_ref, prev_ref, tab_ref, ptab_ref, dy_ref, dp_ref, dsink_ref, carry_k, carry_v):
        i = pl.program_id(0)
        n = nb - 1 - i

        @pl.when(i == 0)
        def _():
            carry_k[...] = jnp.zeros_like(carry_k)
            carry_v[...] = jnp.zeros_like(carry_v)
            dsink_ref[...] = jnp.zeros_like(dsink_ref)

        tab, qr, kk, vv, pen = _swa_inputs(n, p_ref, prev_ref, tab_ref, ptab_ref)
        from_cur = _from_current()
        g = p_ref[:, A_G:A_G + 1024]
        sgm = _sigmoid(g)
        dyv = dy_ref[...]
        do_all = dyv * (g * sgm)
        lane8 = lax.broadcasted_iota(jnp.int32, (8, 128), 1)
        hid = lax.broadcasted_iota(jnp.int32, (1, 4 * CHUNK), 1) // CHUNK
        o_parts, dq_parts = [], []
        dk_nat = [jnp.zeros((2 * CHUNK, 128), F32) for _ in range(2)]
        dv_nat = [jnp.zeros((2 * CHUNK, 128), F32) for _ in range(2)]
        dsink = jnp.zeros((8, 128), F32)
        for j in range(KV_HEADS):
            qs = _stack_heads(qr, j)
            kkb, vvb = _kv_native(kk, j), _kv_native(vv, j)
            P, psink = _softmax_sink(_fold(_mm_nt(kkb, qs), from_cur, pen), _sink_row(sink_ref, j))
            p_full = _unfold(P, from_cur)
            o_parts.append(_unstack_heads(_mm_tn(p_full, vvb), j))
            do_s = _stack_heads(do_all, j)
            dP = _fold(_mm_nt(vvb, do_s), from_cur)
            D = jnp.sum(P * dP, axis=0, keepdims=True)
            ds_full = _unfold(P * (dP - D), from_cur)
            sd = psink * D
            for hh in range(4):
                dsink = dsink + jnp.where(lane8 == 4 * j + hh, -jnp.sum(jnp.where(hid == hh, sd, 0.0)), 0.0)
            dq_parts.append(_unstack_heads(_mm_tn(ds_full, kkb), j) * ATT_SCALE)
            dk_nat[j // 2] = dk_nat[j // 2] + _keep_native(_mm(ds_full, qs), j)
            dv_nat[j // 2] = dv_nat[j // 2] + _keep_native(_mm(p_full, do_s), j)
        o = jnp.concatenate(o_parts, axis=1)
        dkk = jnp.concatenate(dk_nat, axis=1)
        dvv = jnp.concatenate(dv_nat, axis=1)
        out = dp_ref.dtype
        dp_ref[:, A_Q:A_Q + 1024] = _rope_transposed(jnp.concatenate(dq_parts, axis=1), tab).astype(out)
        dp_ref[:, A_K:A_K + 256] = _rope_transposed(dkk[CHUNK:2 * CHUNK] + carry_k[...], tab).astype(out)
        dp_ref[:, A_V:A_V + 256] = (dvv[CHUNK:2 * CHUNK] + carry_v[...]).astype(out)
        dp_ref[:, A_G:A_G + 1024] = (dyv * o * (sgm * (1.0 + g * (1.0 - sgm)))).astype(out)
        carry_k[...] = dkk[0:CHUNK]
        carry_v[...] = dvv[0:CHUNK]
        dsink_ref[...] += dsink

    rev = lambda i: (nb - 1 - i, 0)
    prev = lambda i: jnp.maximum(nb - 2 - i, 0)
    return _call(
        body, None, name="swa_bwd", grid=(nb,),
        in_specs=[pl.BlockSpec(memory_space=pltpu.SMEM),
                  pl.BlockSpec((CHUNK, A_W), rev),
                  pl.BlockSpec((CHUNK, 512), lambda i: (prev(i), 2)),
                  pl.BlockSpec((CHUNK, 384), rev),
                  pl.BlockSpec((CHUNK, 384), lambda i: (prev(i), 0)),
                  pl.BlockSpec((CHUNK, D_ATT), lambda i: (nb - 1 - i, 1))],
        out_specs=[pl.BlockSpec((CHUNK, A_W), rev), pl.BlockSpec((8, 128), lambda i: (0, 0))],
        out_shape=[jax.ShapeDtypeStruct((L, A_W), _MXU), jax.ShapeDtypeStruct((8, 128), F32)],
        scratch_shapes=[pltpu.VMEM((CHUNK, 256), F32), pltpu.VMEM((CHUNK, 256), F32)],
        args=(sinks, proj_att, proj_att, tabs, tabs, dy), reduce=reduce)


def _head(y, x, target, w_out, ln_g, ln_b, *, tm):
    L = x.shape[0]
    nsteps = L // tm

    def body(y_ref, x_ref, t_ref, wo_ref, g_ref, b_ref, dr_ref, dy_ref, acc_ref):
        i = pl.program_id(0)

        @pl.when(i == 0)
        def _():
            acc_ref[...] = jnp.zeros_like(acc_ref)

        r = ALPHA * x_ref[...] + _mm(y_ref[...], wo_ref[...])
        mu = jnp.mean(r, axis=-1, keepdims=True)
        d = r - mu
        rstd = lax.rsqrt(jnp.mean(d * d, axis=-1, keepdims=True) + LN_EPS)
        xh = d * rstd
        gam = g_ref[0:1, :]
        e = xh * gam + b_ref[0:1, :] - t_ref[...]
        dout = e * (1.0 / D_MODEL)
        dxh = dout * gam
        dr = rstd * (dxh - jnp.mean(dxh, axis=-1, keepdims=True)
                     - xh * jnp.mean(dxh * xh, axis=-1, keepdims=True))
        dr_ref[...] = dr
        dy_ref[...] = _mm_nt(dr, wo_ref[...])
        acc_ref[...] += _rows8([_colsum(dout * xh), _colsum(dout), _colsum(e * e) * (0.5 / D_MODEL)])

        @pl.when(i == nsteps - 1)
        def _():
            acc = acc_ref[...]
            tot = jnp.sum(acc[2:3, :])
            rid = lax.broadcasted_iota(jnp.int32, (8, 1024), 0)
            acc_ref[...] = jnp.where(rid == 3, tot, acc)

    const = lambda shape: pl.BlockSpec(shape, lambda i: (0, 0))
    row = lambda w: pl.BlockSpec((tm, w), lambda i: (i, 0))
    return pl.pallas_call(
        body, name="head", grid=(nsteps,),
        in_specs=[row(2048), row(1024), row(1024), const((2048, 1024)), const((1, 1024)), const((1, 1024))],
        out_specs=[row(1024), row(2048), const((8, 1024))],
        out_shape=[jax.ShapeDtypeStruct((L, D_MODEL), F32), jax.ShapeDtypeStruct((L, 2048), F32),
                   jax.ShapeDtypeStruct((8, 1024), F32)],
        compiler_params=_params(("arbitrary",)),
    )(y, x, target, w_out, ln_g, ln_b)


def _gather_w_in(w_shard, positions):
    R = w_shard.shape[0]
    halves = (pl.ds(0, R // 2), pl.ds(R // 2, R // 2))
    any_spec = pl.BlockSpec(memory_space=pl.ANY)
    vmem = pl.BlockSpec(memory_space=pltpu.VMEM)

    def body(in_ref, pos_ref, inv_ref, out_ref, tab_ref, tab_scr, send_sems, recv_sems, local_sem, tab_sem):
        x, y, c = _position()

        def slot(p, half=None):
            s = out_ref.at[_index(*p)]
            return s if half is None else s.at[halves[half]]

        def same_core(p):
            return (p[0], p[1], c)

        def other_core(p):
            return (p[0], p[1], 1 - c)

        me, xn, yn, dg = (x, y), (1 - x, y), (x, 1 - y), (1 - x, 1 - y)

        def copy(k, dst, to, src=None):
            return _remote(dst if src is None else src, dst, send_sems.at[k], recv_sems.at[k], to)

        local = pltpu.make_async_copy(in_ref, slot(same_core(me)), local_sem)
        local.start()
        own = [copy(0, slot(same_core(me)), other_core(me), in_ref), copy(1, slot(same_core(me)), same_core(xn), in_ref),
               copy(2, slot(same_core(me)), same_core(yn), in_ref)]
        for cp in own:
            cp.start()
        _rope_tables(pos_ref, inv_ref, tab_scr)
        tab_out = pltpu.make_async_copy(tab_scr, tab_ref, tab_sem)
        tab_out.start()
        copy(1, slot(same_core(xn)), same_core(xn)).wait_recv()
        passed = [copy(4, slot(same_core(xn), 1), same_core(yn)), copy(5, slot(same_core(xn)), other_core(me))]
        for cp in passed:
            cp.start()
        copy(2, slot(same_core(yn)), same_core(yn)).wait_recv()
        more = [copy(3, slot(same_core(yn), 0), same_core(xn)), copy(6, slot(same_core(yn)), other_core(me))]
        for cp in more:
            cp.start()
        passed += more
        for k, half in ((3, 0), (4, 1)):
            copy(k, slot(same_core(dg), half), same_core(xn)).wait_recv()
            fwd = copy(7 + half, slot(same_core(dg), half), other_core(me))
            fwd.start()
            passed.append(fwd)
        copy(0, slot(other_core(me)), other_core(me)).wait_recv()
        copy(5, slot(other_core(xn)), other_core(me)).wait_recv()
        copy(6, slot(other_core(yn)), other_core(me)).wait_recv()
        for half in (0, 1):
            copy(7 + half, slot(other_core(dg), half), other_core(me)).wait_recv()
        for cp in own + passed:
            cp.wait_send()
        local.wait()
        tab_out.wait()

    return pl.pallas_call(
        body, name="gather_w_in", in_specs=[any_spec, vmem, vmem], out_specs=[any_spec, any_spec],
        out_shape=[jax.ShapeDtypeStruct((N_DEV,) + w_shard.shape, w_shard.dtype),
                   jax.ShapeDtypeStruct((positions.size, 384), F32)],
        scratch_shapes=[pltpu.VMEM((positions.size, 384), F32), pltpu.SemaphoreType.DMA((9,)),
                        pltpu.SemaphoreType.DMA((9,)), pltpu.SemaphoreType.DMA, pltpu.SemaphoreType.DMA],
        compiler_params=_params(),
    )(w_shard, positions, jnp.asarray(ROPE_INV)[None, :])


def _input_gradient(d_ssd, d_att, w_ssd, w_att, dr, *, tm, comm=None, reduce=None):
    L = dr.shape[0]

    def body(ds_ref, da_ref, ws_ref, wa_ref, dr_ref, o_ref):
        o_ref[...] = ALPHA * dr_ref[...] + _mm_nt(ds_ref[...], ws_ref[...]) + _mm_nt(da_ref[...], wa_ref[...])

    row = lambda w: pl.BlockSpec((tm, w), lambda i: (i, 0))
    resident = lambda a: pl.BlockSpec(a.shape, lambda i: (0, 0), pipeline_mode=pl.Buffered(1))
    return _call(body, comm, name="dx", grid=(L // tm,),
                 in_specs=[row(S_W), row(A_W), resident(w_ssd), resident(w_att), row(D_MODEL)],
                 out_specs=[row(D_MODEL)], out_shape=[jax.ShapeDtypeStruct((L, D_MODEL), F32)],
                 scratch_shapes=[], args=(d_ssd, d_att, w_ssd, w_att, dr), reduce=reduce)


SHARD_COLS = D_IN_PROJ // N_DEV
SPLIT = N_SSD_REAL - 4 * SHARD_COLS
RELAYOUT_ROWS = 256


def _unpack_w_in(w_all):
    def body(g_ref, ws_ref, wa_ref):
        for j in range(4):
            ws_ref[:, SHARD_COLS * j:SHARD_COLS * (j + 1)] = g_ref[j]
        ws_ref[:, 4 * SHARD_COLS:N_SSD_REAL] = g_ref[4, :, 0:SPLIT]
        ws_ref[:, N_SSD_REAL:S_W] = jnp.zeros((RELAYOUT_ROWS, S_W - N_SSD_REAL), ws_ref.dtype)
        wa_ref[:, 0:SHARD_COLS - SPLIT] = g_ref[4, :, SPLIT:SHARD_COLS]
        for j in range(5, N_DEV):
            lo = SHARD_COLS * (j - 4) - SPLIT
            wa_ref[:, lo:lo + SHARD_COLS] = g_ref[j]

    return pl.pallas_call(
        body, name="unpack_w_in", grid=(D_MODEL // RELAYOUT_ROWS,),
        in_specs=[pl.BlockSpec((N_DEV, RELAYOUT_ROWS, SHARD_COLS), lambda i: (0, i, 0))],
        out_specs=[pl.BlockSpec((RELAYOUT_ROWS, S_W), lambda i: (i, 0)), pl.BlockSpec((RELAYOUT_ROWS, A_W), lambda i: (i, 0))],
        out_shape=[jax.ShapeDtypeStruct((D_MODEL, S_W), w_all.dtype), jax.ShapeDtypeStruct((D_MODEL, A_W), w_all.dtype)],
        compiler_params=_params(("arbitrary",)),
    )(w_all)


def _dw_in(xb, d, half, tail=None, *, tl=1024):
    L, N = d.shape
    steps = L // tl

    def body(x_ref, d_ref, *refs):
        if half == 0:
            p_ref, tail_ref, acc = refs
        else:
            t_ref, p_ref, acc = refs
        l = pl.program_id(0)

        @pl.when(l == 0)
        def _():
            acc[...] = jnp.zeros_like(acc)

        acc[...] += _mm_tn(x_ref[...], d_ref[...])

        @pl.when(l == steps - 1)
        def _():
            if half == 0:
                tail_ref[...] = acc[:, S_DT:S_W]
            for j in range(4):
                if half == 0:
                    pieces = [(0, acc[:, SHARD_COLS * j:SHARD_COLS * (j + 1)])]
                elif j == 0:
                    pieces = [(0, t_ref[:, 4 * SHARD_COLS - S_DT:N_SSD_REAL - S_DT]), (SPLIT, acc[:, 0:SHARD_COLS - SPLIT])]
                else:
                    lo = SHARD_COLS * j - SPLIT
                    pieces = [(0, acc[:, lo:lo + SHARD_COLS])]
                for off, blk in pieces:
                    p_ref[j, :, off:off + blk.shape[1]] = blk

    once = pl.Buffered(1)
    whole = lambda shape: pl.BlockSpec(shape, lambda l: (0,) * len(shape), pipeline_mode=once)
    in_specs = [pl.BlockSpec((tl, D_MODEL), lambda l: (l, 0)), pl.BlockSpec((tl, N), lambda l: (l, 0))]
    args = [xb, d]
    out_shape = [jax.ShapeDtypeStruct((4, D_MODEL, SHARD_COLS), F32)]
    if half == 0:
        out_shape.append(jax.ShapeDtypeStruct((D_MODEL, S_W - S_DT), F32))
    else:
        in_specs.append(whole(tail.shape))
        args.append(tail)
    return pl.pallas_call(
        body, name="dw_in_%d" % half, grid=(steps,), in_specs=in_specs,
        out_specs=[whole(o.shape) for o in out_shape], out_shape=out_shape,
        scratch_shapes=[pltpu.VMEM((D_MODEL, N), F32)], compiler_params=_params(("arbitrary",)),
    )(*args)


def _adamw_math(w, g, m, v):
    m = ADAM_B1 * m + (1.0 - ADAM_B1) * g
    v = ADAM_B2 * v + (1.0 - ADAM_B2) * (g * g)
    m_hat = m / (1.0 - ADAM_B1 ** ADAM_STEP)
    v_hat = v / (1.0 - ADAM_B2 ** ADAM_STEP)
    delta = -ADAM_LR * (m_hat / (jnp.sqrt(v_hat) + ADAM_EPS) + ADAM_WD * w)
    return delta, m, v


def _adamw_shard(g_own, recv, w, m, v, *, rows, name):
    R, C = g_own.shape

    def body(g_ref, r_ref, w_ref, m_ref, v_ref, go_ref, d_ref, mo_ref, vo_ref):
        g = g_ref[...]
        for k in range(N_DEV - 1):
            g = g + r_ref[k].astype(F32)
        d, mn, vn = _adamw_math(w_ref[...], g, m_ref[...], v_ref[...])
        go_ref[...] = g
        d_ref[...] = d
        mo_ref[...] = mn
        vo_ref[...] = vn

    blk = pl.BlockSpec((rows, C), lambda i: (i, 0))
    return pl.pallas_call(
        body, name=name, grid=(R // rows,),
        in_specs=[blk, pl.BlockSpec((N_DEV - 1, rows, C), lambda i: (0, i, 0)), blk, blk, blk],
        out_specs=[blk] * 4, out_shape=[jax.ShapeDtypeStruct((R, C), F32)] * 4,
        compiler_params=_params(("arbitrary",)),
    )(g_own, recv, w, m, v)


def _minor_rows_view(a):
    return jnp.transpose(a, (2, 0, 1)).reshape(SHARD_COLS * 8, 128)


def _from_minor_rows_view(v):
    return jnp.transpose(v.reshape(SHARD_COLS, 8, 128), (1, 2, 0)).reshape(1, D_MODEL, SHARD_COLS)


def _adamw_w_in(is_lo, own_lo, own_hi, recv_lo, recv_hi, w, m, v):
    C = SHARD_COLS
    pad = -C % 128

    def body(lo_ref, ol_ref, oh_ref, rl_ref, rh_ref, w_ref, m_ref, v_ref, go_ref, d_ref, mo_ref, vo_ref):
        lo = lo_ref[0] == 1
        for q in range(D_MODEL // 128):
            band = pl.ds(q * 128, 128)
            g = jnp.where(lo, ol_ref[band, :], oh_ref[band, :])
            for k in range(2):
                g = g + jnp.where(lo, rl_ref[k, band, :], rh_ref[k, band, :]).astype(F32)
            g = jnp.pad(g, ((0, 0), (0, pad))).T[0:C]
            rows = pl.ds(q, C, stride=8)
            d, mn, vn = _adamw_math(w_ref[rows, :], g, m_ref[rows, :], v_ref[rows, :])
            go_ref[rows, :] = g
            d_ref[rows, :] = d
            mo_ref[rows, :] = mn
            vo_ref[rows, :] = vn

    return pl.pallas_call(
        body, name="adamw_w_in", out_shape=[jax.ShapeDtypeStruct(w.shape, F32)] * 4,
        in_specs=[pl.BlockSpec(memory_space=pltpu.SMEM)] + [pl.BlockSpec(memory_space=pltpu.VMEM)] * 7,
        out_specs=[pl.BlockSpec(memory_space=pltpu.VMEM)] * 4,
        compiler_params=_params(),
    )(is_lo, own_lo, own_hi, recv_lo, recv_hi, w, m, v)


SMALL = ("conv_b", "dt_bias", "a_log", "d_skip", "ssd_norm_w", "attn_sinks", "ln_g", "ln_b")


def _adamw_small(gathered, params):
    n_p = len(SMALL)

    def body(*refs):
        acc = []
        for r in refs[:5]:
            t = r[0]
            for k in range(1, N_DEV):
                t = t + r[k]
            acc.append(t)
        head, conv, norm, scal, sink = acc
        grads = dict(conv_b=conv[4:5, :], dt_bias=scal[0:1, 0:N_HEADS], a_log=scal[1:2, 0:N_HEADS],
                     d_skip=scal[2:3, 0:N_HEADS], ssd_norm_w=norm[0:1, :], attn_sinks=sink[0:1, 0:N_HEADS],
                     ln_g=head[0:1, :], ln_b=head[1:2, :])
        wmv = refs[5:5 + 3 * n_p]
        outs = refs[5 + 3 * n_p:]
        outs[0][...] = head[3:4, 0:1]
        outs[1][...] = conv[0:4, :]
        for i, name in enumerate(SMALL):
            w_ref, m_ref, v_ref = wmv[3 * i:3 * i + 3]
            g = grads[name]
            d, mn, vn = _adamw_math(w_ref[...], g, m_ref[...], v_ref[...])
            for o_ref, val in zip(outs[2 + 4 * i:6 + 4 * i], (g, d, mn, vn)):
                o_ref[...] = val

    flat = [a for name in SMALL for a in params[name]]
    out_shape = [jax.ShapeDtypeStruct((1, 1), F32), jax.ShapeDtypeStruct((4, D_XBC), F32)]
    for name in SMALL:
        out_shape += [jax.ShapeDtypeStruct(params[name][0].shape, F32)] * 4
    res = pl.pallas_call(body, name="adamw_small", out_shape=out_shape, compiler_params=_params())(*gathered, *flat)
    return res[0], res[1], {name: res[2 + 4 * i:6 + 4 * i] for i, name in enumerate(SMALL)}


def _adamw_plain(g, w, m, v):
    def body(g_ref, w_ref, m_ref, v_ref, d_ref, mo_ref, vo_ref):
        d, mn, vn = _adamw_math(w_ref[...], g_ref[...], m_ref[...], v_ref[...])
        d_ref[...] = d
        mo_ref[...] = mn
        vo_ref[...] = vn

    return pl.pallas_call(
        body, name="adamw_conv_w", out_shape=[jax.ShapeDtypeStruct(w.shape, F32)] * 3,
        compiler_params=_params(),
    )(g, w, m, v)


def _lane_pattern(fn):
    return np.asarray([fn(l % HEAD_DIM) for l in range(128)], np.float32)


ROPE_INV = _lane_pattern(lambda r: ROPE_THETA ** (-2.0 * (r % 8) / ROPE_DIM) if r < ROPE_DIM else 0.0)


def _rope_tables(pos_ref, inv_ref, tab_ref):
    lane = lax.broadcasted_iota(jnp.int32, (1, 128), 1) % HEAD_DIM
    upper = jnp.where((lane >= ROPE_DIM // 2) & (lane < ROPE_DIM), 1.0, 0.0)
    lower = jnp.where(lane < ROPE_DIM // 2, -1.0, 0.0)

    def block(r, carry):
        rows = pl.ds(pl.multiple_of(r * CHUNK, CHUNK), CHUNK)
        pos = jnp.broadcast_to(pos_ref[pl.ds(r, 1), :].astype(F32), (CHUNK, 128)).T
        ang = pos * inv_ref[...]
        sn = jnp.sin(ang)
        tab_ref[rows, 0:128] = jnp.cos(ang)
        tab_ref[rows, 128:256] = sn * upper
        tab_ref[rows, 256:384] = sn * lower
        return carry

    lax.fori_loop(0, pos_ref.shape[0], block, 0)


def _expansion():
    E = np.arange(1024)[None, :] // HEAD_DIM == np.arange(128)[:, None]
    return jnp.asarray(E, BF16), jnp.asarray(E.T, BF16)


def _ssd_args(conv_w, conv_b, dt_bias, a_log, d_skip, norm_w, E):
    return (conv_w, conv_b, dt_bias.reshape(-1), a_log.reshape(-1), d_skip.reshape(-1), norm_w, E)


def kernel(x, positions, w_in, conv_w, conv_b, dt_bias, a_log, d_skip, ssd_norm_w, attn_sinks, w_out, ln_g, ln_b, loss_target, m_w_in, m_conv_w, m_conv_b, m_dt_bias, m_a_log, m_d_skip, m_ssd_norm_w, m_attn_sinks, m_w_out, m_ln_g, m_ln_b, v_w_in, v_conv_w, v_conv_b, v_dt_bias, v_a_log, v_d_skip, v_ssd_norm_w, v_attn_sinks, v_w_out, v_ln_g, v_ln_b):
    me = _index(*_position())
    x0, target = x[0], loss_target[0]
    bf16_shard = lambda shape: jax.ShapeDtypeStruct(shape, BF16)
    E, ET = _expansion()
    sinks = attn_sinks.reshape(-1)

    w_all, tabs = _gather_w_in(w_in[0].astype(BF16), positions[0].reshape(-1, 128))
    w_ssd, w_att = _unpack_w_in(w_all)
    gather_conv_w = _Hosted([conv_w[0]], [jax.ShapeDtypeStruct((N_DEV,) + conv_w.shape[1:], F32)],
                            [_Flow("gather", 0, 0)])

    proj_ssd, proj_att, xb, conv_w_all = _in_proj(x0, w_ssd, w_att, tm=512, comm=gather_conv_w)
    conv_w_f = jnp.transpose(conv_w_all, (1, 0, 2)).reshape(4, D_XBC)
    ssd_args = _ssd_args(conv_w_f, conv_b, dt_bias, a_log, d_skip, ssd_norm_w, E)
    gather_w_out = _Hosted([w_out[0].astype(BF16)], [bf16_shard((N_DEV, 256, D_MODEL))], [_Flow("gather", 0, 0)])
    y, ypre, hprev, pre, w_out_all = _mixer_forward(proj_ssd, proj_att, tabs, sinks, *ssd_args, comm=gather_w_out)
    w_out_f = w_out_all.reshape(2 * D_MODEL, D_MODEL)
    dr, dy, acc_head = _head(y, x0, target, w_out_f, ln_g, ln_b, tm=512)

    dw_out, dw_out_bf16 = _matmul_tn(y, dr, tl=1024, tn=D_MODEL, name="dw_out", emit_bf16=True)
    own_out = lax.dynamic_index_in_dim(dw_out.reshape(N_DEV, 256, D_MODEL), me, axis=0, keepdims=False)
    send_out = _Hosted([dw_out_bf16.reshape(N_DEV, 256, D_MODEL)], [bf16_shard((N_DEV - 1, 256, D_MODEL))],
                       [_Flow("exchange", 0, 0)])
    d_ssd, acc_cw, acc_w, acc_s, recv_out = _ssd_backward(proj_ssd, hprev, ypre, pre, dy, *ssd_args, ET, comm=send_out)
    stack_lo, dw_dt_block = _dw_in(xb, d_ssd, 0)
    d_att, dsink, own_lo, recv_lo = _swa_backward(proj_att, tabs, sinks, dy, reduce=_OwnerReduce(stack_lo, 0))
    (stack_hi,) = _dw_in(xb, d_att, 1, dw_dt_block)
    accs = [acc_head, acc_cw, acc_w, acc_s, dsink]
    gather_accs = _Hosted(accs, [jax.ShapeDtypeStruct((N_DEV,) + a.shape, F32) for a in accs],
                          [_Flow("gather", i, i) for i in range(5)])
    dx, *gathered, own_hi, recv_hi = _input_gradient(d_ssd, d_att, w_ssd, w_att, dr, tm=256, comm=gather_accs,
                                                     reduce=_OwnerReduce(stack_hi, 1))
    is_lo = (me < 4).reshape(1).astype(jnp.int32)

    g_in, d_in, nm_in, nv_in = [_from_minor_rows_view(r) for r in _adamw_w_in(
        is_lo, own_lo, own_hi, recv_lo, recv_hi, _minor_rows_view(w_in), _minor_rows_view(m_w_in), _minor_rows_view(v_w_in))]
    g_out, d_out, nm_out, nv_out = _adamw_shard(own_out, recv_out, w_out[0], m_w_out[0], v_w_out[0],
                                                rows=256, name="adamw_w_out")
    loss, g_conv_w, small = _adamw_small(gathered, dict(
        conv_b=(conv_b, m_conv_b, v_conv_b), dt_bias=(dt_bias, m_dt_bias, v_dt_bias), a_log=(a_log, m_a_log, v_a_log),
        d_skip=(d_skip, m_d_skip, v_d_skip), ssd_norm_w=(ssd_norm_w, m_ssd_norm_w, v_ssd_norm_w),
        attn_sinks=(attn_sinks, m_attn_sinks, v_attn_sinks), ln_g=(ln_g, m_ln_g, v_ln_g), ln_b=(ln_b, m_ln_b, v_ln_b)))
    g_cw = lax.dynamic_slice_in_dim(g_conv_w, me * (D_XBC // N_DEV), D_XBC // N_DEV, axis=1)
    d_cw, nm_cw, nv_cw = _adamw_plain(g_cw, conv_w[0], m_conv_w[0], v_conv_w[0])

    def leaves(i, big_in, cw, big_out):
        mid = [small[k][i] for k in ("conv_b", "dt_bias", "a_log", "d_skip", "ssd_norm_w", "attn_sinks")]
        return [big_in, cw[None]] + mid + [big_out[None], small["ln_g"][i], small["ln_b"][i]]

    return (loss.reshape(()), dx[None], *leaves(0, g_in, g_cw, g_out), *leaves(1, d_in, d_cw, d_out),
            *leaves(2, nm_in, nm_cw, nm_out), *leaves(3, nv_in, nv_cw, nv_out))
```

```python
import jax
import jax.numpy as jnp
from jax import lax
from jax.experimental import pallas as pl
from jax.experimental.pallas import tpu as pltpu
import numpy as np

F32 = jnp.float32
BF16 = jnp.bfloat16
_MXU = jnp.bfloat16

N_DEV = 8
D_MODEL = 1024
D_SSD = 1024
D_ATT = 1024
HEAD_DIM = 64
N_HEADS = 16
SSD_GROUPS = 2
KV_HEADS = 4
CHUNK = 128
D_XBC = 1536
D_IN_PROJ = 5136
ROPE_DIM = 16
ROPE_THETA = 500000.0
ALPHA = (2.0 * 1) ** 0.25
LN_EPS = 1e-5
RMS_EPS = 1e-5
ATT_SCALE = HEAD_DIM ** -0.5
NEG = -1e30

S_Z, S_XS, S_B, S_C, S_DT, S_W = 0, 1024, 2048, 2304, 2560, 2816
N_SSD_REAL = 2576
A_Q, A_K, A_V, A_G, A_W = 0, 1024, 1280, 1536, 2560

ADAM_LR = 0.001
ADAM_B1 = 0.9
ADAM_B2 = 0.999
ADAM_EPS = 1e-08
ADAM_WD = 0.01
ADAM_STEP = 10

VMEM_LIMIT = 48 * 1024 * 1024
MESH = pl.DeviceIdType.MESH


def _params(sem=None):
    return pltpu.CompilerParams(dimension_semantics=sem, vmem_limit_bytes=VMEM_LIMIT)


def _mm(a, b):
    return jnp.dot(a.astype(_MXU), b.astype(_MXU), preferred_element_type=F32)


def _mm_nt(a, b):
    return lax.dot_general(a.astype(_MXU), b.astype(_MXU), (((1,), (1,)), ((), ())),
                           preferred_element_type=F32)


def _mm_tn(a, b):
    return lax.dot_general(a.astype(_MXU), b.astype(_MXU), (((0,), (0,)), ((), ())),
                           preferred_element_type=F32)


def _split3(v):
    hi = v.astype(BF16)
    r = v - hi.astype(F32)
    mid = r.astype(BF16)
    lo = (r - mid.astype(F32)).astype(BF16)
    return hi, mid, lo


def _mm_exact_r(v, p01):
    hi, mid, lo = _split3(v)
    d = lambda a: jnp.dot(a, p01, preferred_element_type=F32)
    return d(hi) + d(mid) + d(lo)


def _mm_exact_l(p01, v):
    hi, mid, lo = _split3(v)
    d = lambda a: jnp.dot(p01, a, preferred_element_type=F32)
    return d(hi) + d(mid) + d(lo)


def _mm_2pass_r(v, p01):
    hi = v.astype(BF16)
    lo = (v - hi.astype(F32)).astype(BF16)
    return jnp.dot(hi, p01, preferred_element_type=F32) + jnp.dot(lo, p01, preferred_element_type=F32)


def _sigmoid(x):
    return 1.0 / (1.0 + jnp.exp(-x))


def _softplus(x):
    e = jnp.exp(-jnp.abs(x))
    u = 1.0 + e
    log1p = jnp.where(u == 1.0, e, jnp.log(u) * (e / (u - 1.0)))
    return jnp.maximum(x, 0.0) + log1p


def _rows8(rows):
    n = rows[0].shape[1]
    rid = lax.broadcasted_iota(jnp.int32, (8, n), 0)
    out = jnp.zeros((8, n), F32)
    for k, r in enumerate(rows):
        out = out + jnp.where(rid == k, r, 0.0)
    return out


def _colsum(a):
    return jnp.sum(a, axis=0, keepdims=True)


def _in_proj(x, w_ssd, w_att, *, tm, comm=None):
    L, K = x.shape

    def body(x_ref, ws_ref, wa_ref, ps_ref, pa_ref, xb_ref):
        xb = x_ref[...].astype(_MXU)
        xb_ref[...] = xb
        ps_ref[...] = jnp.dot(xb, ws_ref[...], preferred_element_type=F32)
        pa_ref[...] = jnp.dot(xb, wa_ref[...], preferred_element_type=F32)

    row = lambda w: pl.BlockSpec((tm, w), lambda i: (i, 0))
    resident = lambda a: pl.BlockSpec(a.shape, lambda i: (0, 0), pipeline_mode=pl.Buffered(1))
    return _call(
        body, comm, name="in_proj", grid=(L // tm,),
        in_specs=[row(K), resident(w_ssd), resident(w_att)], out_specs=[row(S_W), row(A_W), row(K)],
        out_shape=[jax.ShapeDtypeStruct((L, S_W), F32), jax.ShapeDtypeStruct((L, A_W), F32),
                   jax.ShapeDtypeStruct((L, K), _MXU)],
        scratch_shapes=[], args=(x, w_ssd, w_att))


def _matmul_tn(a, g, *, tl, tn, name, emit_bf16=False):
    L, M = a.shape
    N = g.shape[1]
    last = L // tl - 1

    def body(a_ref, g_ref, o_ref, *rest):
        @pl.when(pl.program_id(1) == 0)
        def _():
            o_ref[...] = jnp.zeros_like(o_ref)

        o_ref[...] += _mm_tn(a_ref[...], g_ref[...])
        if emit_bf16:
            @pl.when(pl.program_id(1) == last)
            def _():
                rest[0][...] = o_ref[...].astype(BF16)

    spec = pl.BlockSpec((M, tn), lambda j, l: (0, j))
    res = pl.pallas_call(
        body, name=name, grid=(N // tn, L // tl),
        in_specs=[pl.BlockSpec((tl, M), lambda j, l: (l, 0)), pl.BlockSpec((tl, tn), lambda j, l: (l, j))],
        out_specs=[spec, spec] if emit_bf16 else [spec],
        out_shape=[jax.ShapeDtypeStruct((M, N), F32)] + ([jax.ShapeDtypeStruct((M, N), BF16)] if emit_bf16 else []),
        compiler_params=_params(("arbitrary", "arbitrary")),
    )(a, g)
    return res if emit_bf16 else res[0]


def _position():
    return lax.axis_index("x"), lax.axis_index("y"), lax.axis_index("c")


def _index(px, py, pc):
    return 4 * px + 2 * py + pc


def _flip(pos, k):
    x, y, c = pos
    return ((1 - x) if (k >> 2) & 1 else x, (1 - y) if (k >> 1) & 1 else y, (1 - c) if k & 1 else c)


def _remote(src, dst, send_sem, recv_sem, peer):
    return pltpu.make_async_remote_copy(src_ref=src, dst_ref=dst, send_sem=send_sem, recv_sem=recv_sem,
                                        device_id=peer, device_id_type=MESH)


class _Flow:
    def __init__(self, kind, operand, result):
        self.kind, self.operand, self.result = kind, operand, result


class _Hosted:
    def __init__(self, operands, out_shapes, flows):
        self.operands, self.out_shapes, self.flows = operands, out_shapes, flows

    def plan(self, ins, outs, send_sems, recv_sems, local_sems):
        me = _position()
        mi = _index(*me)
        sends, recvs, locals_ = [], [], []
        for row, f in enumerate(self.flows):
            src, dst = ins[f.operand], outs[f.result]
            for k in range(1, N_DEV):
                peer = _flip(me, k)
                sems = (send_sems.at[row, k - 1], recv_sems.at[row, k - 1])
                if f.kind == "exchange":
                    sends.append(_remote(src.at[_index(*peer)], dst.at[k - 1], *sems, peer))
                    recvs.append(sends[-1])
                else:
                    sends.append(_remote(src, dst.at[mi], *sems, peer))
                    recvs.append(_remote(src, dst.at[_index(*peer)], *sems, peer))
            if f.kind == "gather":
                locals_.append(pltpu.make_async_copy(src, dst.at[mi], local_sems.at[row]))

        def start():
            for cp in locals_ + sends:
                cp.start()

        def wait():
            for cp in recvs:
                cp.wait_recv()
            for cp in sends:
                cp.wait_send()
            for cp in locals_:
                cp.wait()

        return start, wait


class _OwnerReduce:
    FIRST_STEP, SECOND_STEP, SEND_STEPS, REDUCE_STEPS = 2, 4, (2, 4, 6, 8), (5, 8, 10, 13)

    def __init__(self, stack, target_x):
        self.stack, self.target_x = stack, target_x
        block = stack.shape[1:]
        self.chunks = len(self.SEND_STEPS)
        self.chunk_rows = block[0] // self.chunks
        self.out_shapes = [jax.ShapeDtypeStruct(block, F32), jax.ShapeDtypeStruct((2,) + block, BF16)]
        self.out_specs = [pl.BlockSpec(block, lambda *_: (0, 0), pipeline_mode=pl.Buffered(1)),
                          pl.BlockSpec(memory_space=pl.ANY)]
        dma = pltpu.SemaphoreType.DMA
        self.scratch_shapes = ([pltpu.VMEM((2,) + block, F32)] * 2 + [pltpu.VMEM(block, BF16)] * 3
                               + [dma((self.chunks,))] * 4 + [dma((2,))] * 3)

    def plan(self, i, steps, stack_ref, own_ref, recv_ref, scratch):
        assert self.FIRST_STEP <= self.SEND_STEPS[0] and self.SECOND_STEP < self.REDUCE_STEPS[0] < steps - 1
        (theirs_scr, mine_scr, first_scr, across_scr, out_scr, y_send_sems, y_recv_sems, x_send_sems, x_recv_sems,
         swap_send_sems, swap_recv_sems, mine_sems) = scratch
        x, y, c = _position()
        owners_side = x == self.target_x
        other_side = x != self.target_x
        sibling, across, owner = (x, y, 1 - c), (x, 1 - y, c), (self.target_x, y, c)
        order = (1 - y, y)
        swaps = [_remote(stack_ref.at[2 * order[j] + (1 - c)], theirs_scr.at[j], swap_send_sems.at[j], swap_recv_sems.at[j],
                         sibling) for j in range(2)]
        mine = [pltpu.make_async_copy(stack_ref.at[2 * order[j] + c], mine_scr.at[j], mine_sems.at[j]) for j in range(2)]
        chunks = range(self.chunks)
        part = [pl.ds(j * self.chunk_rows, self.chunk_rows) for j in chunks]
        y_sems = lambda j: (y_send_sems.at[j], y_recv_sems.at[j])
        to_neighbour = [_remote(first_scr.at[part[j]], across_scr.at[part[j]], *y_sems(j), across) for j in chunks]
        to_owner_y = [_remote(first_scr.at[part[j]], recv_ref.at[0, part[j]], *y_sems(j), across) for j in chunks]
        to_owner_x = [_remote(out_scr.at[part[j]], recv_ref.at[1, part[j]], x_send_sems.at[j], x_recv_sems.at[j], owner)
                      for j in chunks]

        def before():
            @pl.when(i <= 1)
            def _():
                @pl.when(i == 0)
                def _():
                    for cp in [swaps[0]] + mine:
                        cp.start()

                pl.when(i == 1)(swaps[1].start)

        def after():
            @pl.when(i <= self.REDUCE_STEPS[-1])
            def _():
                @pl.when(i == self.FIRST_STEP)
                def _():
                    swaps[0].wait_recv()
                    mine[0].wait()
                    first_scr[...] = (mine_scr[0] + theirs_scr[0]).astype(first_scr.dtype)

                for j in chunks:
                    pl.when((i == self.SEND_STEPS[j]) & other_side)(to_neighbour[j].start)
                    pl.when((i == self.SEND_STEPS[j]) & owners_side)(to_owner_y[j].start)

                @pl.when(i == self.SECOND_STEP)
                def _():
                    swaps[1].wait_recv()
                    mine[1].wait()
                    t = mine_scr[1] + theirs_scr[1]
                    own_ref[...] = t
                    mine_scr[1] = t

                for j in chunks:
                    @pl.when((i == self.REDUCE_STEPS[j]) & other_side)
                    def _(j=j):
                        to_neighbour[j].wait_recv()
                        t = mine_scr[1, part[j], :] + across_scr[part[j], :].astype(F32)
                        out_scr[part[j], :] = t.astype(out_scr.dtype)
                        to_owner_x[j].start()

            @pl.when(i == steps - 1)
            def _():
                for cp in swaps:
                    cp.wait_send()
                for j in chunks:
                    @pl.when(other_side)
                    def _(j=j):
                        to_neighbour[j].wait_send()
                        to_owner_x[j].wait_send()

                    @pl.when(owners_side)
                    def _(j=j):
                        to_owner_y[j].wait_send()
                        to_owner_y[j].wait_recv()
                        to_owner_x[j].wait_recv()

        return before, after


def _call(body, comm, *, name, grid, in_specs, out_specs, out_shape, scratch_shapes, args, reduce=None):
    semantics = ("arbitrary",) * len(grid)
    if comm is None and reduce is None:
        return pl.pallas_call(body, name=name, grid=grid, in_specs=in_specs, out_specs=out_specs, out_shape=out_shape,
                              scratch_shapes=scratch_shapes, compiler_params=_params(semantics))(*args)
    n_in, n_out, n_scr = len(args), len(out_shape), len(scratch_shapes)
    c_operands, c_shapes, flows = (comm.operands, comm.out_shapes, comm.flows) if comm else ([], [], [])
    c_in, c_out, rows = len(c_operands), len(c_shapes), max(len(flows), 1)
    r_in = 0 if reduce is None else 1

    def hosted(*refs):
        ins, refs = refs[:n_in], refs[n_in:]
        cins, refs = refs[:c_in], refs[c_in:]
        rins, refs = refs[:r_in], refs[r_in:]
        outs, refs = refs[:n_out], refs[n_out:]
        couts, refs = refs[:c_out], refs[c_out:]
        routs, refs = refs[:2 * r_in], refs[2 * r_in:]
        scr, refs = refs[:n_scr], refs[n_scr:]
        (send_sems, recv_sems, local_sems), r_scr = refs[:3], refs[3:]
        ids = [pl.program_id(d) for d in range(len(grid))]
        first, last = ids[0] == 0, ids[0] == grid[0] - 1
        for d in range(1, len(grid)):
            first, last = first & (ids[d] == 0), last & (ids[d] == grid[d] - 1)
        before = after = lambda: None
        if reduce is not None:
            before, after = reduce.plan(ids[0], grid[0], rins[0], *routs, r_scr)
        if comm is not None:
            start, wait = comm.plan(cins, couts, send_sems, recv_sems, local_sems)
            pl.when(first)(start)
        before()
        body(*ins, *outs, *scr)
        after()
        if comm is not None:
            pl.when(last)(wait)

    any_spec = pl.BlockSpec(memory_space=pl.ANY)
    sems = [pltpu.SemaphoreType.DMA((rows, N_DEV - 1)), pltpu.SemaphoreType.DMA((rows, N_DEV - 1)),
            pltpu.SemaphoreType.DMA((rows,))]
    r_operands, r_specs, r_shapes, r_scratch = ([reduce.stack], reduce.out_specs, reduce.out_shapes,
                                                reduce.scratch_shapes) if reduce else ([], [], [], [])
    return pl.pallas_call(
        hosted, name=name, grid=grid, in_specs=list(in_specs) + [any_spec] * (c_in + r_in),
        out_specs=list(out_specs) + [any_spec] * c_out + r_specs, out_shape=list(out_shape) + list(c_shapes) + r_shapes,
        scratch_shapes=list(scratch_shapes) + sems + r_scratch,
        compiler_params=_params(semantics))(*args, *c_operands, *r_operands)


def _head_row(ref, width, rep):
    hid = lax.broadcasted_iota(jnp.int32, (1, width), 1) // rep
    row = jnp.zeros((1, width), F32)
    for h in range(N_HEADS):
        row = jnp.where(hid == h, ref[h], row)
    return row


def _rows_from_above(u_b, s, ext_scr, row, col):
    down = (row - col == s).astype(_MXU)
    return jnp.concatenate([ext_scr[8 - s:16 - s, :], jnp.dot(down, u_b, preferred_element_type=F32)[8:128]], axis=0)


def _ssd_recompute(first, p_ref, halo_ref, cw_ref, cb_ref, dtb_ref, alog_ref, e_ref, ext_scr, pre=None):
    row = lax.broadcasted_iota(jnp.int32, (128, 128), 0)
    col = lax.broadcasted_iota(jnp.int32, (128, 128), 1)
    ext_scr[0:8, :] = jnp.where(first, 0.0, halo_ref[:, S_XS:S_DT])
    if pre is not None:
        ext_scr[8:16, :] = p_ref[0:8, S_XS:S_DT]
    else:
        ext_scr[8:136, :] = p_ref[:, S_XS:S_DT]
        cw = cw_ref[...]
        pre = (cb_ref[0:1, :] + cw[3:4, :] * ext_scr[8:136, :] + cw[2:3, :] * ext_scr[7:135, :]
               + cw[1:2, :] * ext_scr[6:134, :] + cw[0:1, :] * ext_scr[5:133, :])
    sg = _sigmoid(pre)
    act = pre * sg
    lane = lax.broadcasted_iota(jnp.int32, (1, 128), 1)
    A = jnp.where(lane < N_HEADS, -jnp.exp(_head_row(alog_ref, 128, 1)), 0.0)
    raw = p_ref[:, S_DT:S_DT + 128] + _head_row(dtb_ref, 128, 1)
    dt = _softplus(raw)
    dA = dt * A
    tril = (row >= col).astype(BF16)
    acs = _mm_exact_l(tril, dA)
    last = acs[127:128, :]
    ds = jnp.exp(last - acs)
    eo = jnp.exp(acs)
    E = e_ref[...]
    ex = _mm_2pass_r(jnp.concatenate([dt, ds, eo], axis=0), E)
    dt_e, ds_e, eo_e = ex[0:128], ex[128:256], ex[256:384]
    xs_c = act[:, 0:1024]
    X = xs_c * dt_e
    return dict(pre=pre, sg=sg, xs_c=xs_c, Bc=act[:, 1024:1280], Cc=act[:, 1280:1536], A=A, raw=raw, dt=dt,
                acs=acs, acsT=acs.T, eo_e=eo_e, ds_e=ds_e, dt_e=dt_e, cd_e=eo_e[127:128, :],
                X=X, Xd=X * ds_e, row=row, col=col)


def _split_halves(t):
    lo = _lo_half(CHUNK)
    return jnp.concatenate([jnp.where(lo, t, 0.0), jnp.where(lo, 0.0, t)], axis=0)


def _ssd_core(R, hprev):
    causal = R["row"] >= R["col"]
    acs, acsT, X = R["acs"], R["acsT"], R["X"]
    ydiag, yoff, snew = [], [], []
    for g in range(SSD_GROUPS):
        Bg = R["Bc"][:, g * 128:(g + 1) * 128]
        Cg = R["Cc"][:, g * 128:(g + 1) * 128]
        cols = slice(g * 512, (g + 1) * 512)
        CB = _mm_nt(Cg, Bg)
        snew.append(_mm_tn(Bg, R["Xd"][:, cols]))
        yoff.append(_mm(Cg, hprev[:, cols]))
        for j in range(4):
            h0 = g * 8 + 2 * j
            ms = [CB * jnp.exp(jnp.where(causal, acs[:, h:h + 1] - acsT[h:h + 1, :], NEG)) for h in (h0, h0 + 1)]
            ydiag.append(_mm(jnp.concatenate(ms, axis=1), _split_halves(X[:, h0 * HEAD_DIM:h0 * HEAD_DIM + 128])))
    Y = jnp.concatenate(ydiag, axis=1) + jnp.concatenate(yoff, axis=1) * R["eo_e"]
    return Y, jnp.concatenate(snew, axis=1)


def _ssd_forward_step(p_ref, halo_ref, cw_ref, cb_ref, dtb_ref, alog_ref, dsk_ref, nw_ref, e_ref,
                      y_ref, ypre_ref, hprev_ref, pre_ref, h_scr, ext_scr):
    c = pl.program_id(0)
    first = c == 0

    @pl.when(first)
    def _():
        h_scr[...] = jnp.zeros_like(h_scr)

    R = _ssd_recompute(first, p_ref, halo_ref, cw_ref, cb_ref, dtb_ref, alog_ref, e_ref, ext_scr)
    hprev = h_scr[...]
    hprev_ref[...] = hprev
    pre_ref[...] = R["pre"]
    Y, snew = _ssd_core(R, hprev)
    h_scr[...] = hprev * R["cd_e"] + snew
    Y = Y + _head_row(dsk_ref, D_SSD, HEAD_DIM) * R["xs_c"]
    ypre_ref[...] = Y
    z = p_ref[:, S_Z:S_Z + 1024]
    yf = Y * (z * _sigmoid(z))
    outs = []
    for g in range(SSD_GROUPS):
        yg = yf[:, g * 512:(g + 1) * 512]
        r = lax.rsqrt(jnp.mean(yg * yg, axis=-1, keepdims=True) + RMS_EPS)
        outs.append(yg * r)
    y_ref[:, 0:D_SSD] = (jnp.concatenate(outs, axis=1) * nw_ref[0:1, :]).astype(y_ref.dtype)


def _ssd_backward(proj_ssd, hprev_all, ypre, pre, dy, conv_w, conv_b, dt_bias, a_log, d_skip, norm_w, E, ET, comm=None):
    L = proj_ssd.shape[0]
    nc = L // CHUNK

    def body(p_ref, halo_ref, hprev_ref, ypre_ref, pre_ref, dy_ref, cw_ref, cb_ref, dtb_ref, alog_ref, dsk_ref, nw_ref, e_ref,
             et_ref, dp_ref, acc_cw_ref, acc_w_ref, acc_s_ref, dh_scr, ext_scr, ext2_scr, nxt_scr):
        i = pl.program_id(0)
        c = nc - 1 - i
        first = c == 0

        @pl.when(i == 0)
        def _():
            dh_scr[...] = jnp.zeros_like(dh_scr)
            nxt_scr[...] = jnp.zeros_like(nxt_scr)
            acc_cw_ref[...] = jnp.zeros_like(acc_cw_ref)
            acc_w_ref[...] = jnp.zeros_like(acc_w_ref)
            acc_s_ref[...] = jnp.zeros_like(acc_s_ref)

        R = _ssd_recompute(first, p_ref, halo_ref, cw_ref, cb_ref, dtb_ref, alog_ref, e_ref, ext_scr, pre_ref[...])
        hprev = hprev_ref[...]
        xs_c, X, Xd = R["xs_c"], R["X"], R["Xd"]
        acs, acsT = R["acs"], R["acsT"]
        ET = et_ref[...]
        dsk = _head_row(dsk_ref, D_SSD, HEAD_DIM)
        Y = ypre_ref[...]

        z = p_ref[:, S_Z:S_Z + 1024]
        sz = _sigmoid(z)
        silz = z * sz
        yf = Y * silz
        dyv = dy_ref[...]
        nw = nw_ref[0:1, :]
        dyf_parts, dnw_parts = [], []
        for g in range(SSD_GROUPS):
            cols = slice(g * 512, (g + 1) * 512)
            yg = yf[:, cols]
            r = lax.rsqrt(jnp.mean(yg * yg, axis=-1, keepdims=True) + RMS_EPS)
            yn = yg * r
            dyn = dyv[:, cols] * nw[:, cols]
            dnw_parts.append(_colsum(dyv[:, cols] * yn))
            dyf_parts.append(r * (dyn - yn * jnp.mean(dyn * yn, axis=-1, keepdims=True)))
        dyf = jnp.concatenate(dyf_parts, axis=1)
        dY = dyf * silz
        dz = dyf * Y * (sz * (1.0 + z * (1.0 - sz)))

        dhn = dh_scr[...]
        dYo = dY * R["eo_e"]
        causal = R["row"] >= R["col"]
        dacs = jnp.zeros((128, 128), F32)
        dacs_t = jnp.zeros((128, 128), F32)
        dxdiag, dxd, dhprev, dBs, dCs, yoff = [], [], [], [], [], []
        for g in range(SSD_GROUPS):
            Bg = R["Bc"][:, g * 128:(g + 1) * 128]
            Cg = R["Cc"][:, g * 128:(g + 1) * 128]
            cols = slice(g * 512, (g + 1) * 512)
            CB = _mm_nt(Cg, Bg)
            dCB = jnp.zeros((128, 128), F32)
            for j in range(4):
                h0 = g * 8 + 2 * j
                pc = slice(h0 * HEAD_DIM, h0 * HEAD_DIM + 128)
                dYst = _split_halves(dY[:, pc])
                dMst = _mm_nt(dYst, X[:, pc])
                mts = []
                for a, h in enumerate((h0, h0 + 1)):
                    acol = acs[:, h:h + 1]
                    arow = acsT[h:h + 1, :]
                    Lm = jnp.exp(jnp.where(causal, acol - arow, NEG))
                    M = CB * Lm
                    dM = dMst[a * 128:(a + 1) * 128]
                    dCB = dCB + dM * Lm
                    G = dM * M
                    dacs = dacs + jnp.where(R["col"] == h, jnp.sum(G, axis=1, keepdims=True), 0.0)
                    dacs_t = dacs_t + jnp.where(R["row"] == h, jnp.sum(G, axis=0, keepdims=True), 0.0)
                    mts.append(M.T)
                dxdiag.append(_mm(jnp.concatenate(mts, axis=1), dYst))
            dS = dhn[:, cols]
            dxd.append(_mm(Bg, dS))
            yoff.append(_mm(Cg, hprev[:, cols]))
            dhprev.append(_mm_tn(Cg, dYo[:, cols]))
            dCs.append(_mm_nt(dYo[:, cols], hprev[:, cols]) + _mm(dCB, Bg))
            dBs.append(_mm_tn(dCB, Cg) + _mm_nt(Xd[:, cols], dS))
        Yoff = jnp.concatenate(yoff, axis=1) * R["eo_e"]
        dXd = jnp.concatenate(dxd, axis=1)
        dX = jnp.concatenate(dxdiag, axis=1) + dXd * R["ds_e"]
        t_state = dXd * Xd
        dacs = dacs + _mm_2pass_r(dY * Yoff - t_state, ET) - dacs_t.T
        v_last = _colsum(t_state + dhn * hprev * R["cd_e"])
        dlast = _mm_exact_r(jnp.broadcast_to(v_last, (8, 1024)), ET)[0:1, :]
        dacs = dacs + jnp.where(R["row"] == 127, dlast, 0.0)
        triu = (R["col"] >= R["row"]).astype(BF16)
        da = _mm_exact_l(triu, dacs)
        ddt = da * R["A"] + _mm(dX * xs_c, ET)
        ddt_raw = ddt * _sigmoid(R["raw"])
        dxs_c = dX * R["dt_e"] + dY * dsk
        dh_scr[...] = jnp.concatenate(dhprev, axis=1) + dhn * R["cd_e"]

        dact = jnp.concatenate([dxs_c] + dBs + dCs, axis=1)
        pre, sg = R["pre"], R["sg"]
        dpre = dact * (sg * (1.0 + pre * (1.0 - sg)))
        ext2_scr[0:8, :] = dpre[120:128, :]
        ext2_scr[8:16, :] = nxt_scr[...]
        nxt_scr[...] = dpre[0:8, :]
        cw = cw_ref[...]
        u_b, dpre_b = p_ref[:, S_XS:S_DT].astype(_MXU), dpre.astype(_MXU)
        dxbc = cw[3:4, :] * dpre
        taps = [_colsum(dpre * p_ref[:, S_XS:S_DT])]
        for s in (1, 2, 3):
            up = (R["col"] - R["row"] == s).astype(_MXU)
            d_s = jnp.concatenate([jnp.dot(up, dpre_b, preferred_element_type=F32)[0:120],
                                   ext2_scr[s:8 + s, :]], axis=0)
            dxbc = dxbc + cw[3 - s:4 - s, :] * d_s
            taps.append(_colsum(dpre * _rows_from_above(u_b, s, ext_scr, R["row"], R["col"])))
        acc_cw_ref[...] += _rows8(taps[::-1] + [_colsum(dpre)])
        acc_w_ref[...] += _rows8([jnp.concatenate(dnw_parts, axis=1), _colsum(dY * xs_c)])
        acc_s_ref[...] += _rows8([_colsum(ddt_raw), _colsum(da * R["dt"])])

        lane = lax.broadcasted_iota(jnp.int32, (128, 128), 1)
        dp_ref[:, S_Z:S_Z + 1024] = dz.astype(dp_ref.dtype)
        dp_ref[:, S_XS:S_DT] = dxbc.astype(dp_ref.dtype)
        dp_ref[:, S_DT:S_DT + 128] = jnp.where(lane < N_HEADS, ddt_raw, 0.0).astype(dp_ref.dtype)
        dp_ref[:, S_DT + 128:S_W] = jnp.zeros((128, 128), dp_ref.dtype)

        @pl.when(i == nc - 1)
        def _():
            acc = acc_s_ref[...]
            dskip = _mm_exact_r(acc_w_ref[...], ET)[1:2, :]
            acc_s_ref[...] = _rows8([acc[0:1, :], acc[1:2, :] * R["A"], dskip])

    const = lambda shape: pl.BlockSpec(shape, lambda i: (0, 0))
    smem = pl.BlockSpec(memory_space=pltpu.SMEM)
    rev = lambda i: (nc - 1 - i, 0)
    return _call(
        body, comm, name="ssd_bwd", grid=(nc,),
        in_specs=[pl.BlockSpec((CHUNK, S_W), rev),
                  pl.BlockSpec((8, S_W), lambda i: (jnp.maximum((nc - 1 - i) * 16 - 1, 0), 0)),
                  pl.BlockSpec((128, 1024), rev),
                  pl.BlockSpec((CHUNK, D_SSD), rev),
                  pl.BlockSpec((CHUNK, D_XBC), rev),
                  pl.BlockSpec((CHUNK, D_SSD), rev),
                  const((4, D_XBC)), const((1, D_XBC)), smem, smem, smem, const((1, 1024)),
                  const((128, 1024)), const((1024, 128))],
        out_specs=[pl.BlockSpec((CHUNK, S_W), rev), const((8, D_XBC)), const((8, 1024)), const((8, 128))],
        out_shape=[jax.ShapeDtypeStruct((L, S_W), _MXU), jax.ShapeDtypeStruct((8, D_XBC), F32),
                   jax.ShapeDtypeStruct((8, 1024), F32), jax.ShapeDtypeStruct((8, 128), F32)],
        scratch_shapes=[pltpu.VMEM((128, 1024), F32), pltpu.VMEM((16, D_XBC), F32),
                        pltpu.VMEM((16, D_XBC), F32), pltpu.VMEM((8, D_XBC), F32)],
        args=(proj_ssd, proj_ssd, hprev_all, ypre, pre, dy, conv_w, conv_b, dt_bias, a_log, d_skip, norm_w, E, ET))


def _rope(t, tab):
    cos, sa, sb = tab[:, 0:128], tab[:, 128:256], tab[:, 256:384]
    outs = []
    for i in range(t.shape[1] // 128):
        tg = t[:, i * 128:(i + 1) * 128]
        outs.append(tg * cos + pltpu.roll(tg, 8, 1) * sa + pltpu.roll(tg, 120, 1) * sb)
    return jnp.concatenate(outs, axis=1)


def _rope_transposed(d, tab):
    cos, sa, sb = tab[:, 0:128], tab[:, 128:256], tab[:, 256:384]
    outs = []
    for i in range(d.shape[1] // 128):
        dg = d[:, i * 128:(i + 1) * 128]
        outs.append(dg * cos + pltpu.roll(dg * sa, 120, 1) + pltpu.roll(dg * sb, 8, 1))
    return jnp.concatenate(outs, axis=1)


def _lo_half(rows):
    return lax.broadcasted_iota(jnp.int32, (rows, 128), 1) < HEAD_DIM


def _native_half(rows, j):
    lo = _lo_half(rows)
    return lo if j % 2 == 0 else jnp.logical_not(lo)


def _kv_native(t, j):
    p = j // 2
    return jnp.where(_native_half(t.shape[0], j), t[:, p * 128:(p + 1) * 128], 0.0)


def _stack_heads(t, j):
    out = []
    for m in (2 * j, 2 * j + 1):
        pair = t[:, m * 128:(m + 1) * 128]
        swapped = pltpu.roll(pair, HEAD_DIM, 1)
        out += [pair, swapped] if j % 2 == 0 else [swapped, pair]
    return jnp.concatenate(out, axis=0)


def _unstack_heads(s, j):
    out = []
    for m in range(2):
        first, second = s[256 * m:256 * m + 128], s[256 * m + 128:256 * m + 256]
        if j % 2 == 0:
            out.append(first + pltpu.roll(second, HEAD_DIM, 1))
        else:
            out.append(pltpu.roll(first, HEAD_DIM, 1) + second)
    return jnp.concatenate(out, axis=1)


def _keep_native(r, j):
    return jnp.where(_native_half(r.shape[0], j), r, 0.0)


def _sink_row(sink_ref, j):
    hid = lax.broadcasted_iota(jnp.int32, (1, 4 * CHUNK), 1) // CHUNK
    row = jnp.zeros((1, 4 * CHUNK), F32)
    for hh in range(4):
        row = jnp.where(hid == hh, sink_ref[4 * j + hh], row)
    return row


def _from_current():
    si = lax.broadcasted_iota(jnp.int32, (CHUNK, 4 * CHUNK), 0)
    qi = lax.broadcasted_iota(jnp.int32, (CHUNK, 4 * CHUNK), 1) % CHUNK
    return si <= qi


def _fold(full, from_cur, pen=0.0):
    return jnp.where(from_cur, full[CHUNK:2 * CHUNK], full[0:CHUNK] + pen)


def _unfold(t, from_cur):
    c = jnp.where(from_cur, t, 0.0)
    return jnp.concatenate([t - c, c], axis=0)


def _softmax_sink(s, sink):
    mx = jnp.maximum(jnp.max(s, axis=0, keepdims=True), sink)
    p = jnp.exp(s - mx)
    esink = jnp.exp(sink - mx)
    inv = 1.0 / (jnp.sum(p, axis=0, keepdims=True) + esink)
    return p * inv, esink * inv


def _swa_inputs(blk, p_ref, prev_ref, tab_ref, ptab_ref):
    tab = tab_ref[...]
    qr = _rope(p_ref[:, A_Q:A_Q + 1024], tab) * ATT_SCALE
    kk = jnp.concatenate([_rope(prev_ref[:, 0:256], ptab_ref[...]), _rope(p_ref[:, A_K:A_K + 256], tab)], axis=0)
    vv = jnp.concatenate([prev_ref[:, 256:512], p_ref[:, A_V:A_V + 256]], axis=0)
    return tab, qr, kk, vv, jnp.where(blk > 0, 0.0, NEG)


def _swa_forward_step(sink_ref, p_ref, prev_ref, tab_ref, ptab_ref, y_ref):
    n = pl.program_id(0)
    _, qr, kk, vv, pen = _swa_inputs(n, p_ref, prev_ref, tab_ref, ptab_ref)
    from_cur = _from_current()
    outs = []
    for j in range(KV_HEADS):
        s = _fold(_mm_nt(_kv_native(kk, j), _stack_heads(qr, j)), from_cur, pen)
        P, _ = _softmax_sink(s, _sink_row(sink_ref, j))
        outs.append(_unstack_heads(_mm_tn(_unfold(P, from_cur), _kv_native(vv, j)), j))
    g = p_ref[:, A_G:A_G + 1024]
    y_ref[:, D_SSD:D_SSD + D_ATT] = (jnp.concatenate(outs, axis=1) * (g * _sigmoid(g))).astype(y_ref.dtype)


def _mixer_forward(proj_ssd, proj_att, tabs, sinks, conv_w, conv_b, dt_bias, a_log, d_skip, norm_w, E, comm=None):
    L = proj_ssd.shape[0]
    nc = L // CHUNK

    def body(p_ref, halo_ref, cw_ref, cb_ref, dtb_ref, alog_ref, dsk_ref, nw_ref, e_ref,
             sink_ref, pa_ref, prev_ref, tab_ref, ptab_ref, y_ref, ypre_ref, hprev_ref, pre_ref, h_scr, ext_scr):
        _ssd_forward_step(p_ref, halo_ref, cw_ref, cb_ref, dtb_ref, alog_ref, dsk_ref, nw_ref, e_ref,
                          y_ref, ypre_ref, hprev_ref, pre_ref, h_scr, ext_scr)
        _swa_forward_step(sink_ref, pa_ref, prev_ref, tab_ref, ptab_ref, y_ref)

    const = lambda shape: pl.BlockSpec(shape, lambda c: (0, 0))
    smem = pl.BlockSpec(memory_space=pltpu.SMEM)
    rows = lambda w: pl.BlockSpec((CHUNK, w), lambda c: (c, 0))
    return _call(
        body, comm, name="mixer_fwd", grid=(nc,),
        in_specs=[rows(S_W), pl.BlockSpec((8, S_W), lambda c: (jnp.maximum(c * 16 - 1, 0), 0)),
                  const((4, D_XBC)), const((1, D_XBC)), smem, smem, smem, const((1, 1024)), const((128, 1024)),
                  smem, rows(A_W), pl.BlockSpec((CHUNK, 512), lambda c: (jnp.maximum(c - 1, 0), 2)),
                  rows(384), pl.BlockSpec((CHUNK, 384), lambda c: (jnp.maximum(c - 1, 0), 0))],
        out_specs=[rows(D_SSD + D_ATT), rows(D_SSD), pl.BlockSpec((128, 1024), lambda c: (c, 0)), rows(D_XBC)],
        out_shape=[jax.ShapeDtypeStruct((L, D_SSD + D_ATT), _MXU), jax.ShapeDtypeStruct((L, D_SSD), F32),
                   jax.ShapeDtypeStruct((nc * 128, 1024), F32), jax.ShapeDtypeStruct((L, D_XBC), F32)],
        scratch_shapes=[pltpu.VMEM((128, 1024), F32), pltpu.VMEM((136, D_XBC), F32)],
        args=(proj_ssd, proj_ssd, conv_w, conv_b, dt_bias, a_log, d_skip, norm_w, E,
              sinks, proj_att, proj_att, tabs, tabs))


def _swa_backward(proj_att, tabs, sinks, dy, reduce=None):
    L = proj_att.shape[0]
    nb = L // CHUNK

    def body(sink_ref, p_ref, prev_ref, tab_ref, ptab_ref, dy_ref, dp_ref, dsink_ref, carry_k, carry_v):
        i = pl.program_id(0)
        n = nb - 1 - i

        @pl.when(i == 0)
        def _():
            carry_k[...] = jnp.zeros_like(carry_k)
            carry_v[...] = jnp.zeros_like(carry_v)
            dsink_ref[...] = jnp.zeros_like(dsink_ref)

        tab, qr, kk, vv, pen = _swa_inputs(n, p_ref, prev_ref, tab_ref, ptab_ref)
        from_cur = _from_current()
        g = p_ref[:, A_G:A_G + 1024]
        sgm = _sigmoid(g)
        dyv = dy_ref[...]
        do_all = dyv * (g * sgm)
        lane8 = lax.broadcasted_iota(jnp.int32, (8, 128), 1)
        hid = lax.broadcasted_iota(jnp.int32, (1, 4 * CHUNK), 1) // CHUNK
        o_parts, dq_parts = [], []
        dk_nat = [jnp.zeros((2 * CHUNK, 128), F32) for _ in range(2)]
        dv_nat = [jnp.zeros((2 * CHUNK, 128), F32) for _ in range(2)]
        dsink = jnp.zeros((8, 128), F32)
        for j in range(KV_HEADS):
            qs = _stack_heads(qr, j)
            kkb, vvb = _kv_native(kk, j), _kv_native(vv, j)
            P, psink = _softmax_sink(_fold(_mm_nt(kkb, qs), from_cur, pen), _sink_row(sink_ref, j))
            p_full = _unfold(P, from_cur)
            o_parts.append(_unstack_heads(_mm_tn(p_full, vvb), j))
            do_s = _stack_heads(do_all, j)
            dP = _fold(_mm_nt(vvb, do_s), from_cur)
            D = jnp.sum(P * dP, axis=0, keepdims=True)
            ds_full = _unfold(P * (dP - D), from_cur)
            sd = psink * D
            for hh in range(4):
                dsink = dsink + jnp.where(lane8 == 4 * j + hh, -jnp.sum(jnp.where(hid == hh, sd, 0.0)), 0.0)
            dq_parts.append(_unstack_heads(_mm_tn(ds_full, kkb), j) * ATT_SCALE)
            dk_nat[j // 2] = dk_nat[j // 2] + _keep_native(_mm(ds_full, qs), j)
            dv_nat[j // 2] = dv_nat[j // 2] + _keep_native(_mm(p_full, do_s), j)
        o = jnp.concatenate(o_parts, axis=1)
        dkk = jnp.concatenate(dk_nat, axis=1)
        dvv = jnp.concatenate(dv_nat, axis=1)
        out = dp_ref.dtype
        dp_ref[:, A_Q:A_Q + 1024] = _rope_transposed(jnp.concatenate(dq_parts, axis=1), tab).astype(out)
        dp_ref[:, A_K:A_K + 256] = _rope_transposed(dkk[CHUNK:2 * CHUNK] + carry_k[...], tab).astype(out)
        dp_ref[:, A_V:A_V + 256] = (dvv[CHUNK:2 * CHUNK] + carry_v[...]).astype(out)
        dp_ref[:, A_G:A_G + 1024] = (dyv * o * (sgm * (1.0 + g * (1.0 - sgm)))).astype(out)
        carry_k[...] = dkk[0:CHUNK]
        carry_v[...] = dvv[0:CHUNK]
        dsink_ref[...] += dsink

    rev = lambda i: (nb - 1 - i, 0)
    prev = lambda i: jnp.maximum(nb - 2 - i, 0)
    return _call(
        body, None, name="swa_bwd", grid=(nb,),
        in_specs=[pl.BlockSpec(memory_space=pltpu.SMEM),
                  pl.BlockSpec((CHUNK, A_W), rev),
                  pl.BlockSpec((CHUNK, 512), lambda i: (prev(i), 2)),
                  pl.BlockSpec((CHUNK, 384), rev),
                  pl.BlockSpec((CHUNK, 384), lambda i: (prev(i), 0)),
                  pl.BlockSpec((CHUNK, D_ATT), lambda i: (nb - 1 - i, 1))],
        out_specs=[pl.BlockSpec((CHUNK, A_W), rev), pl.BlockSpec((8, 128), lambda i: (0, 0))],
        out_shape=[jax.ShapeDtypeStruct((L, A_W), _MXU), jax.ShapeDtypeStruct((8, 128), F32)],
        scratch_shapes=[pltpu.VMEM((CHUNK, 256), F32), pltpu.VMEM((CHUNK, 256), F32)],
        args=(sinks, proj_att, proj_att, tabs, tabs, dy), reduce=reduce)


def _head(y, x, target, w_out, ln_g, ln_b, *, tm):
    L = x.shape[0]
    nsteps = L // tm

    def body(y_ref, x_ref, t_ref, wo_ref, g_ref, b_ref, dr_ref, dy_ref, acc_ref):
        i = pl.program_id(0)

        @pl.when(i == 0)
        def _():
            acc_ref[...] = jnp.zeros_like(acc_ref)

        r = ALPHA * x_ref[...] + _mm(y_ref[...], wo_ref[...])
        mu = jnp.mean(r, axis=-1, keepdims=True)
        d = r - mu
        rstd = lax.rsqrt(jnp.mean(d * d, axis=-1, keepdims=True) + LN_EPS)
        xh = d * rstd
        gam = g_ref[0:1, :]
        e = xh * gam + b_ref[0:1, :] - t_ref[...]
        dout = e * (1.0 / D_MODEL)
        dxh = dout * gam
        dr = rstd * (dxh - jnp.mean(dxh, axis=-1, keepdims=True)
                     - xh * jnp.mean(dxh * xh, axis=-1, keepdims=True))
        dr_ref[...] = dr
        dy_ref[...] = _mm_nt(dr, wo_ref[...])
        acc_ref[...] += _rows8([_colsum(dout * xh), _colsum(dout), _colsum(e * e) * (0.5 / D_MODEL)])

        @pl.when(i == nsteps - 1)
        def _():
            acc = acc_ref[...]
            tot = jnp.sum(acc[2:3, :])
            rid = lax.broadcasted_iota(jnp.int32, (8, 1024), 0)
            acc_ref[...] = jnp.where(rid == 3, tot, acc)

    const = lambda shape: pl.BlockSpec(shape, lambda i: (0, 0))
    row = lambda w: pl.BlockSpec((tm, w), lambda i: (i, 0))
    return pl.pallas_call(
        body, name="head", grid=(nsteps,),
        in_specs=[row(2048), row(1024), row(1024), const((2048, 1024)), const((1, 1024)), const((1, 1024))],
        out_specs=[row(1024), row(2048), const((8, 1024))],
        out_shape=[jax.ShapeDtypeStruct((L, D_MODEL), F32), jax.ShapeDtypeStruct((L, 2048), F32),
                   jax.ShapeDtypeStruct((8, 1024), F32)],
        compiler_params=_params(("arbitrary",)),
    )(y, x, target, w_out, ln_g, ln_b)


def _gather_w_in(w_shard, positions):
    R = w_shard.shape[0]
    halves = (pl.ds(0, R // 2), pl.ds(R // 2, R // 2))
    any_spec = pl.BlockSpec(memory_space=pl.ANY)
    vmem = pl.BlockSpec(memory_space=pltpu.VMEM)

    def body(in_ref, pos_ref, inv_ref, out_ref, tab_ref, tab_scr, send_sems, recv_sems, local_sem, tab_sem):
        x, y, c = _position()

        def slot(p, half=None):
            s = out_ref.at[_index(*p)]
            return s if half is None else s.at[halves[half]]

        def same_core(p):
            return (p[0], p[1], c)

        def other_core(p):
            return (p[0], p[1], 1 - c)

        me, xn, yn, dg = (x, y), (1 - x, y), (x, 1 - y), (1 - x, 1 - y)

        def copy(k, dst, to, src=None):
            return _remote(dst if src is None else src, dst, send_sems.at[k], recv_sems.at[k], to)

        local = pltpu.make_async_copy(in_ref, slot(same_core(me)), local_sem)
        local.start()
        own = [copy(0, slot(same_core(me)), other_core(me), in_ref), copy(1, slot(same_core(me)), same_core(xn), in_ref),
               copy(2, slot(same_core(me)), same_core(yn), in_ref)]
        for cp in own:
            cp.start()
        _rope_tables(pos_ref, inv_ref, tab_scr)
        tab_out = pltpu.make_async_copy(tab_scr, tab_ref, tab_sem)
        tab_out.start()
        copy(1, slot(same_core(xn)), same_core(xn)).wait_recv()
        passed = [copy(4, slot(same_core(xn), 1), same_core(yn)), copy(5, slot(same_core(xn)), other_core(me))]
        for cp in passed:
            cp.start()
        copy(2, slot(same_core(yn)), same_core(yn)).wait_recv()
        more = [copy(3, slot(same_core(yn), 0), same_core(xn)), copy(6, slot(same_core(yn)), other_core(me))]
        for cp in more:
            cp.start()
        passed += more
        for k, half in ((3, 0), (4, 1)):
            copy(k, slot(same_core(dg), half), same_core(xn)).wait_recv()
            fwd = copy(7 + half, slot(same_core(dg), half), other_core(me))
            fwd.start()
            passed.append(fwd)
        copy(0, slot(other_core(me)), other_core(me)).wait_recv()
        copy(5, slot(other_core(xn)), other_core(me)).wait_recv()
        copy(6, slot(other_core(yn)), other_core(me)).wait_recv()
        for half in (0, 1):
            copy(7 + half, slot(other_core(dg), half), other_core(me)).wait_recv()
        for cp in own + passed:
            cp.wait_send()
        local.wait()
        tab_out.wait()

    return pl.pallas_call(
        body, name="gather_w_in", in_specs=[any_spec, vmem, vmem], out_specs=[any_spec, any_spec],
        out_shape=[jax.ShapeDtypeStruct((N_DEV,) + w_shard.shape, w_shard.dtype),
                   jax.ShapeDtypeStruct((positions.size, 384), F32)],
        scratch_shapes=[pltpu.VMEM((positions.size, 384), F32), pltpu.SemaphoreType.DMA((9,)),
                        pltpu.SemaphoreType.DMA((9,)), pltpu.SemaphoreType.DMA, pltpu.SemaphoreType.DMA],
        compiler_params=_params(),
    )(w_shard, positions, jnp.asarray(ROPE_INV)[None, :])


def _input_gradient(d_ssd, d_att, w_ssd, w_att, dr, *, tm, comm=None, reduce=None):
    L = dr.shape[0]

    def body(ds_ref, da_ref, ws_ref, wa_ref, dr_ref, o_ref):
        o_ref[...] = ALPHA * dr_ref[...] + _mm_nt(ds_ref[...], ws_ref[...]) + _mm_nt(da_ref[...], wa_ref[...])

    row = lambda w: pl.BlockSpec((tm, w), lambda i: (i, 0))
    resident = lambda a: pl.BlockSpec(a.shape, lambda i: (0, 0), pipeline_mode=pl.Buffered(1))
    return _call(body, comm, name="dx", grid=(L // tm,),
                 in_specs=[row(S_W), row(A_W), resident(w_ssd), resident(w_att), row(D_MODEL)],
                 out_specs=[row(D_MODEL)], out_shape=[jax.ShapeDtypeStruct((L, D_MODEL), F32)],
                 scratch_shapes=[], args=(d_ssd, d_att, w_ssd, w_att, dr), reduce=reduce)


SHARD_COLS = D_IN_PROJ // N_DEV
SPLIT = N_SSD_REAL - 4 * SHARD_COLS
RELAYOUT_ROWS = 256


def _unpack_w_in(w_all):
    def body(g_ref, ws_ref, wa_ref):
        for j in range(4):
            ws_ref[:, SHARD_COLS * j:SHARD_COLS * (j + 1)] = g_ref[j]
        ws_ref[:, 4 * SHARD_COLS:N_SSD_REAL] = g_ref[4, :, 0:SPLIT]
        ws_ref[:, N_SSD_REAL:S_W] = jnp.zeros((RELAYOUT_ROWS, S_W - N_SSD_REAL), ws_ref.dtype)
        wa_ref[:, 0:SHARD_COLS - SPLIT] = g_ref[4, :, SPLIT:SHARD_COLS]
        for j in range(5, N_DEV):
            lo = SHARD_COLS * (j - 4) - SPLIT
            wa_ref[:, lo:lo + SHARD_COLS] = g_ref[j]

    return pl.pallas_call(
        body, name="unpack_w_in", grid=(D_MODEL // RELAYOUT_ROWS,),
        in_specs=[pl.BlockSpec((N_DEV, RELAYOUT_ROWS, SHARD_COLS), lambda i: (0, i, 0))],
        out_specs=[pl.BlockSpec((RELAYOUT_ROWS, S_W), lambda i: (i, 0)), pl.BlockSpec((RELAYOUT_ROWS, A_W), lambda i: (i, 0))],
        out_shape=[jax.ShapeDtypeStruct((D_MODEL, S_W), w_all.dtype), jax.ShapeDtypeStruct((D_MODEL, A_W), w_all.dtype)],
        compiler_params=_params(("arbitrary",)),
    )(w_all)


def _dw_in(xb, d, half, tail=None, *, tl=1024):
    L, N = d.shape
    steps = L // tl

    def body(x_ref, d_ref, *refs):
        if half == 0:
            p_ref, tail_ref, acc, p_scr, p_sems = refs
        else:
            t_ref, p_ref, acc, p_scr, p_sems = refs
        l = pl.program_id(0)

        @pl.when(l == 0)
        def _():
            acc[...] = jnp.zeros_like(acc)

        acc[...] += _mm_tn(x_ref[...], d_ref[...])

        @pl.when(l == steps - 1)
        def _():
            if half == 0:
                tail_ref[...] = acc[:, S_DT:S_W]
            outs = []
            for j in range(4):
                if half == 0:
                    pieces = [(0, acc[:, SHARD_COLS * j:SHARD_COLS * (j + 1)])]
                elif j == 0:
                    pieces = [(0, t_ref[:, 4 * SHARD_COLS - S_DT:N_SSD_REAL - S_DT]), (SPLIT, acc[:, 0:SHARD_COLS - SPLIT])]
                else:
                    lo = SHARD_COLS * j - SPLIT
                    pieces = [(0, acc[:, lo:lo + SHARD_COLS])]
                for off, blk in pieces:
                    p_scr[j, :, off:off + blk.shape[1]] = blk
                outs.append(pltpu.make_async_copy(p_scr.at[j], p_ref.at[j], p_sems.at[j]))
                outs[-1].start()
            for cp in outs:
                cp.wait()

    once = pl.Buffered(1)
    whole = lambda shape: pl.BlockSpec(shape, lambda l: (0,) * len(shape), pipeline_mode=once)
    in_specs = [pl.BlockSpec((tl, D_MODEL), lambda l: (l, 0)), pl.BlockSpec((tl, N), lambda l: (l, 0))]
    args = [xb, d]
    stack = jax.ShapeDtypeStruct((4, D_MODEL, SHARD_COLS), F32)
    out_shape, out_specs = [stack], [pl.BlockSpec(memory_space=pl.ANY)]
    if half == 0:
        out_shape.append(jax.ShapeDtypeStruct((D_MODEL, S_W - S_DT), F32))
        out_specs.append(whole(out_shape[-1].shape))
    else:
        in_specs.append(whole(tail.shape))
        args.append(tail)
    return pl.pallas_call(
        body, name="dw_in_%d" % half, grid=(steps,), in_specs=in_specs, out_specs=out_specs, out_shape=out_shape,
        scratch_shapes=[pltpu.VMEM((D_MODEL, N), F32), pltpu.VMEM(stack.shape, F32), pltpu.SemaphoreType.DMA((4,))],
        compiler_params=_params(("arbitrary",)),
    )(*args)


def _adamw_math(w, g, m, v):
    m = ADAM_B1 * m + (1.0 - ADAM_B1) * g
    v = ADAM_B2 * v + (1.0 - ADAM_B2) * (g * g)
    m_hat = m / (1.0 - ADAM_B1 ** ADAM_STEP)
    v_hat = v / (1.0 - ADAM_B2 ** ADAM_STEP)
    delta = -ADAM_LR * (m_hat / (jnp.sqrt(v_hat) + ADAM_EPS) + ADAM_WD * w)
    return delta, m, v


def _adamw_shard(g_own, recv, w, m, v, *, rows, name):
    R, C = g_own.shape

    def body(g_ref, r_ref, w_ref, m_ref, v_ref, go_ref, d_ref, mo_ref, vo_ref):
        g = g_ref[...]
        for k in range(N_DEV - 1):
            g = g + r_ref[k].astype(F32)
        d, mn, vn = _adamw_math(w_ref[...], g, m_ref[...], v_ref[...])
        go_ref[...] = g
        d_ref[...] = d
        mo_ref[...] = mn
        vo_ref[...] = vn

    blk = pl.BlockSpec((rows, C), lambda i: (i, 0))
    return pl.pallas_call(
        body, name=name, grid=(R // rows,),
        in_specs=[blk, pl.BlockSpec((N_DEV - 1, rows, C), lambda i: (0, i, 0)), blk, blk, blk],
        out_specs=[blk] * 4, out_shape=[jax.ShapeDtypeStruct((R, C), F32)] * 4,
        compiler_params=_params(("arbitrary",)),
    )(g_own, recv, w, m, v)


def _minor_rows_view(a):
    return jnp.transpose(a, (2, 0, 1)).reshape(SHARD_COLS * 8, 128)


def _from_minor_rows_view(v):
    return jnp.transpose(v.reshape(SHARD_COLS, 8, 128), (1, 2, 0)).reshape(1, D_MODEL, SHARD_COLS)


def _adamw_w_in(is_lo, own_lo, own_hi, recv_lo, recv_hi, w, m, v):
    C = SHARD_COLS
    pad = -C % 128

    def body(lo_ref, ol_ref, oh_ref, rl_ref, rh_ref, w_ref, m_ref, v_ref, go_ref, d_ref, mo_ref, vo_ref):
        lo = lo_ref[0] == 1
        for q in range(D_MODEL // 128):
            band = pl.ds(q * 128, 128)
            g = jnp.where(lo, ol_ref[band, :], oh_ref[band, :])
            for k in range(2):
                g = g + jnp.where(lo, rl_ref[k, band, :], rh_ref[k, band, :]).astype(F32)
            g = jnp.pad(g, ((0, 0), (0, pad))).T[0:C]
            rows = pl.ds(q, C, stride=8)
            d, mn, vn = _adamw_math(w_ref[rows, :], g, m_ref[rows, :], v_ref[rows, :])
            go_ref[rows, :] = g
            d_ref[rows, :] = d
            mo_ref[rows, :] = mn
            vo_ref[rows, :] = vn

    return pl.pallas_call(
        body, name="adamw_w_in", out_shape=[jax.ShapeDtypeStruct(w.shape, F32)] * 4,
        in_specs=[pl.BlockSpec(memory_space=pltpu.SMEM)] + [pl.BlockSpec(memory_space=pltpu.VMEM)] * 7,
        out_specs=[pl.BlockSpec(memory_space=pltpu.VMEM)] * 4,
        compiler_params=_params(),
    )(is_lo, own_lo, own_hi, recv_lo, recv_hi, w, m, v)


SMALL = ("conv_b", "dt_bias", "a_log", "d_skip", "ssd_norm_w", "attn_sinks", "ln_g", "ln_b")


def _adamw_small(gathered, params):
    n_p = len(SMALL)

    def body(*refs):
        acc = []
        for r in refs[:5]:
            t = r[0]
            for k in range(1, N_DEV):
                t = t + r[k]
            acc.append(t)
        head, conv, norm, scal, sink = acc
        grads = dict(conv_b=conv[4:5, :], dt_bias=scal[0:1, 0:N_HEADS], a_log=scal[1:2, 0:N_HEADS],
                     d_skip=scal[2:3, 0:N_HEADS], ssd_norm_w=norm[0:1, :], attn_sinks=sink[0:1, 0:N_HEADS],
                     ln_g=head[0:1, :], ln_b=head[1:2, :])
        wmv = refs[5:5 + 3 * n_p]
        outs = refs[5 + 3 * n_p:]
        outs[0][...] = head[3:4, 0:1]
        outs[1][...] = conv[0:4, :]
        for i, name in enumerate(SMALL):
            w_ref, m_ref, v_ref = wmv[3 * i:3 * i + 3]
            g = grads[name]
            d, mn, vn = _adamw_math(w_ref[...], g, m_ref[...], v_ref[...])
            for o_ref, val in zip(outs[2 + 4 * i:6 + 4 * i], (g, d, mn, vn)):
                o_ref[...] = val

    flat = [a for name in SMALL for a in params[name]]
    out_shape = [jax.ShapeDtypeStruct((1, 1), F32), jax.ShapeDtypeStruct((4, D_XBC), F32)]
    for name in SMALL:
        out_shape += [jax.ShapeDtypeStruct(params[name][0].shape, F32)] * 4
    res = pl.pallas_call(body, name="adamw_small", out_shape=out_shape, compiler_params=_params())(*gathered, *flat)
    return res[0], res[1], {name: res[2 + 4 * i:6 + 4 * i] for i, name in enumerate(SMALL)}


def _adamw_plain(g, w, m, v):
    def body(g_ref, w_ref, m_ref, v_ref, d_ref, mo_ref, vo_ref):
        d, mn, vn = _adamw_math(w_ref[...], g_ref[...], m_ref[...], v_ref[...])
        d_ref[...] = d
        mo_ref[...] = mn
        vo_ref[...] = vn

    return pl.pallas_call(
        body, name="adamw_conv_w", out_shape=[jax.ShapeDtypeStruct(w.shape, F32)] * 3,
        compiler_params=_params(),
    )(g, w, m, v)


def _lane_pattern(fn):
    return np.asarray([fn(l % HEAD_DIM) for l in range(128)], np.float32)


ROPE_INV = _lane_pattern(lambda r: ROPE_THETA ** (-2.0 * (r % 8) / ROPE_DIM) if r < ROPE_DIM else 0.0)


def _rope_tables(pos_ref, inv_ref, tab_ref):
    lane = lax.broadcasted_iota(jnp.int32, (1, 128), 1) % HEAD_DIM
    upper = jnp.where((lane >= ROPE_DIM // 2) & (lane < ROPE_DIM), 1.0, 0.0)
    lower = jnp.where(lane < ROPE_DIM // 2, -1.0, 0.0)

    def block(r, carry):
        rows = pl.ds(pl.multiple_of(r * CHUNK, CHUNK), CHUNK)
        pos = jnp.broadcast_to(pos_ref[pl.ds(r, 1), :].astype(F32), (CHUNK, 128)).T
        ang = pos * inv_ref[...]
        sn = jnp.sin(ang)
        tab_ref[rows, 0:128] = jnp.cos(ang)
        tab_ref[rows, 128:256] = sn * upper
        tab_ref[rows, 256:384] = sn * lower
        return carry

    lax.fori_loop(0, pos_ref.shape[0], block, 0)


def _expansion():
    E = np.arange(1024)[None, :] // HEAD_DIM == np.arange(128)[:, None]
    return jnp.asarray(E, BF16), jnp.asarray(E.T, BF16)


def _ssd_args(conv_w, conv_b, dt_bias, a_log, d_skip, norm_w, E):
    return (conv_w, conv_b, dt_bias.reshape(-1), a_log.reshape(-1), d_skip.reshape(-1), norm_w, E)


def kernel(x, positions, w_in, conv_w, conv_b, dt_bias, a_log, d_skip, ssd_norm_w, attn_sinks, w_out, ln_g, ln_b, loss_target, m_w_in, m_conv_w, m_conv_b, m_dt_bias, m_a_log, m_d_skip, m_ssd_norm_w, m_attn_sinks, m_w_out, m_ln_g, m_ln_b, v_w_in, v_conv_w, v_conv_b, v_dt_bias, v_a_log, v_d_skip, v_ssd_norm_w, v_attn_sinks, v_w_out, v_ln_g, v_ln_b):
    me = _index(*_position())
    x0, target = x[0], loss_target[0]
    bf16_shard = lambda shape: jax.ShapeDtypeStruct(shape, BF16)
    E, ET = _expansion()
    sinks = attn_sinks.reshape(-1)

    w_all, tabs = _gather_w_in(w_in[0].astype(BF16), positions[0].reshape(-1, 128))
    w_ssd, w_att = _unpack_w_in(w_all)
    gather_conv_w = _Hosted([conv_w[0]], [jax.ShapeDtypeStruct((N_DEV,) + conv_w.shape[1:], F32)],
                            [_Flow("gather", 0, 0)])

    proj_ssd, proj_att, xb, conv_w_all = _in_proj(x0, w_ssd, w_att, tm=512, comm=gather_conv_w)
    conv_w_f = jnp.transpose(conv_w_all, (1, 0, 2)).reshape(4, D_XBC)
    ssd_args = _ssd_args(conv_w_f, conv_b, dt_bias, a_log, d_skip, ssd_norm_w, E)
    gather_w_out = _Hosted([w_out[0].astype(BF16)], [bf16_shard((N_DEV, 256, D_MODEL))], [_Flow("gather", 0, 0)])
    y, ypre, hprev, pre, w_out_all = _mixer_forward(proj_ssd, proj_att, tabs, sinks, *ssd_args, comm=gather_w_out)
    w_out_f = w_out_all.reshape(2 * D_MODEL, D_MODEL)
    dr, dy, acc_head = _head(y, x0, target, w_out_f, ln_g, ln_b, tm=512)

    dw_out, dw_out_bf16 = _matmul_tn(y, dr, tl=1024, tn=D_MODEL, name="dw_out", emit_bf16=True)
    own_out = lax.dynamic_index_in_dim(dw_out.reshape(N_DEV, 256, D_MODEL), me, axis=0, keepdims=False)
    send_out = _Hosted([dw_out_bf16.reshape(N_DEV, 256, D_MODEL)], [bf16_shard((N_DEV - 1, 256, D_MODEL))],
                       [_Flow("exchange", 0, 0)])
    d_ssd, acc_cw, acc_w, acc_s, recv_out = _ssd_backward(proj_ssd, hprev, ypre, pre, dy, *ssd_args, ET, comm=send_out)
    stack_lo, dw_dt_block = _dw_in(xb, d_ssd, 0)
    d_att, dsink, own_lo, recv_lo = _swa_backward(proj_att, tabs, sinks, dy, reduce=_OwnerReduce(stack_lo, 0))
    (stack_hi,) = _dw_in(xb, d_att, 1, dw_dt_block)
    accs = [acc_head, acc_cw, acc_w, acc_s, dsink]
    gather_accs = _Hosted(accs, [jax.ShapeDtypeStruct((N_DEV,) + a.shape, F32) for a in accs],
                          [_Flow("gather", i, i) for i in range(5)])
    dx, *gathered, own_hi, recv_hi = _input_gradient(d_ssd, d_att, w_ssd, w_att, dr, tm=256, comm=gather_accs,
                                                     reduce=_OwnerReduce(stack_hi, 1))
    is_lo = (me < 4).reshape(1).astype(jnp.int32)

    g_in, d_in, nm_in, nv_in = [_from_minor_rows_view(r) for r in _adamw_w_in(
        is_lo, own_lo, own_hi, recv_lo, recv_hi, _minor_rows_view(w_in), _minor_rows_view(m_w_in), _minor_rows_view(v_w_in))]
    g_out, d_out, nm_out, nv_out = _adamw_shard(own_out, recv_out, w_out[0], m_w_out[0], v_w_out[0],
                                                rows=256, name="adamw_w_out")
    loss, g_conv_w, small = _adamw_small(gathered, dict(
        conv_b=(conv_b, m_conv_b, v_conv_b), dt_bias=(dt_bias, m_dt_bias, v_dt_bias), a_log=(a_log, m_a_log, v_a_log),
        d_skip=(d_skip, m_d_skip, v_d_skip), ssd_norm_w=(ssd_norm_w, m_ssd_norm_w, v_ssd_norm_w),
        attn_sinks=(attn_sinks, m_attn_sinks, v_attn_sinks), ln_g=(ln_g, m_ln_g, v_ln_g), ln_b=(ln_b, m_ln_b, v_ln_b)))
    g_cw = lax.dynamic_slice_in_dim(g_conv_w, me * (D_XBC // N_DEV), D_XBC // N_DEV, axis=1)
    d_cw, nm_cw, nv_cw = _adamw_plain(g_cw, conv_w[0], m_conv_w[0], v_conv_w[0])

    def leaves(i, big_in, cw, big_out):
        mid = [small[k][i] for k in ("conv_b", "dt_bias", "a_log", "d_skip", "ssd_norm_w", "attn_sinks")]
        return [big_in, cw[None]] + mid + [big_out[None], small["ln_g"][i], small["ln_b"][i]]

    return (loss.reshape(()), dx[None], *leaves(0, g_in, g_cw, g_out), *leaves(1, d_in, d_cw, d_out),
            *leaves(2, nm_in, nm_cw, nm_out), *leaves(3, nv_in, nv_cw, nv_out))
```

```python
import jax
import jax.numpy as jnp
from jax import lax
from jax.experimental import pallas as pl
from jax.experimental.pallas import tpu as pltpu
import numpy as np

F32 = jnp.float32
BF16 = jnp.bfloat16
_MXU = jnp.bfloat16

N_DEV = 8
D_MODEL = 1024
D_SSD = 1024
D_ATT = 1024
HEAD_DIM = 64
N_HEADS = 16
SSD_GROUPS = 2
KV_HEADS = 4
CHUNK = 128
D_XBC = 1536
D_IN_PROJ = 5136
ROPE_DIM = 16
ROPE_THETA = 500000.0
ALPHA = (2.0 * 1) ** 0.25
LN_EPS = 1e-5
RMS_EPS = 1e-5
ATT_SCALE = HEAD_DIM ** -0.5
NEG = -1e30

S_Z, S_XS, S_B, S_C, S_DT, S_W = 0, 1024, 2048, 2304, 2560, 2816
N_SSD_REAL = 2576
A_Q, A_K, A_V, A_G, A_W = 0, 1024, 1280, 1536, 2560

ADAM_LR = 0.001
ADAM_B1 = 0.9
ADAM_B2 = 0.999
ADAM_EPS = 1e-08
ADAM_WD = 0.01
ADAM_STEP = 10

VMEM_LIMIT = 48 * 1024 * 1024
MESH = pl.DeviceIdType.MESH


def _params(sem=None):
    return pltpu.CompilerParams(dimension_semantics=sem, vmem_limit_bytes=VMEM_LIMIT)


def _mm(a, b):
    return jnp.dot(a.astype(_MXU), b.astype(_MXU), preferred_element_type=F32)


def _mm_nt(a, b):
    return lax.dot_general(a.astype(_MXU), b.astype(_MXU), (((1,), (1,)), ((), ())),
                           preferred_element_type=F32)


def _mm_tn(a, b):
    return lax.dot_general(a.astype(_MXU), b.astype(_MXU), (((0,), (0,)), ((), ())),
                           preferred_element_type=F32)


def _split3(v):
    hi = v.astype(BF16)
    r = v - hi.astype(F32)
    mid = r.astype(BF16)
    lo = (r - mid.astype(F32)).astype(BF16)
    return hi, mid, lo


def _mm_exact_r(v, p01):
    hi, mid, lo = _split3(v)
    d = lambda a: jnp.dot(a, p01, preferred_element_type=F32)
    return d(hi) + d(mid) + d(lo)


def _mm_exact_l(p01, v):
    hi, mid, lo = _split3(v)
    d = lambda a: jnp.dot(p01, a, preferred_element_type=F32)
    return d(hi) + d(mid) + d(lo)


def _mm_2pass_r(v, p01):
    hi = v.astype(BF16)
    lo = (v - hi.astype(F32)).astype(BF16)
    return jnp.dot(hi, p01, preferred_element_type=F32) + jnp.dot(lo, p01, preferred_element_type=F32)


def _sigmoid(x):
    return 1.0 / (1.0 + jnp.exp(-x))


def _softplus(x):
    e = jnp.exp(-jnp.abs(x))
    u = 1.0 + e
    log1p = jnp.where(u == 1.0, e, jnp.log(u) * (e / (u - 1.0)))
    return jnp.maximum(x, 0.0) + log1p


def _rows8(rows):
    n = rows[0].shape[1]
    rid = lax.broadcasted_iota(jnp.int32, (8, n), 0)
    out = jnp.zeros((8, n), F32)
    for k, r in enumerate(rows):
        out = out + jnp.where(rid == k, r, 0.0)
    return out


def _colsum(a):
    return jnp.sum(a, axis=0, keepdims=True)


def _in_proj(x, w_ssd, w_att, *, tm, comm=None):
    L, K = x.shape

    def body(x_ref, ws_ref, wa_ref, ps_ref, pa_ref, xb_ref):
        xb = x_ref[...].astype(_MXU)
        xb_ref[...] = xb
        ps_ref[...] = jnp.dot(xb, ws_ref[...], preferred_element_type=F32)
        pa_ref[...] = jnp.dot(xb, wa_ref[...], preferred_element_type=F32)

    row = lambda w: pl.BlockSpec((tm, w), lambda i: (i, 0))
    resident = lambda a: pl.BlockSpec(a.shape, lambda i: (0, 0), pipeline_mode=pl.Buffered(1))
    return _call(
        body, comm, name="in_proj", grid=(L // tm,),
        in_specs=[row(K), resident(w_ssd), resident(w_att)], out_specs=[row(S_W), row(A_W), row(K)],
        out_shape=[jax.ShapeDtypeStruct((L, S_W), F32), jax.ShapeDtypeStruct((L, A_W), F32),
                   jax.ShapeDtypeStruct((L, K), _MXU)],
        scratch_shapes=[], args=(x, w_ssd, w_att))


def _dw_out(me1, y, dr, *, tl):
    L, M = y.shape
    N = dr.shape[1]
    steps = L // tl
    own_rows = M // N_DEV

    def body(me_ref, y_ref, dr_ref, b_ref, own_ref, acc):
        l = pl.program_id(0)

        @pl.when(l == 0)
        def _():
            acc[...] = jnp.zeros_like(acc)

        acc[...] += _mm_tn(y_ref[...], dr_ref[...])

        @pl.when(l == steps - 1)
        def _():
            b_ref[...] = acc[...].astype(b_ref.dtype)
            own_ref[...] = acc[pl.ds(pl.multiple_of(me_ref[0] * own_rows, own_rows), own_rows), :]

    whole = lambda shape: pl.BlockSpec(shape, lambda l: (0, 0), pipeline_mode=pl.Buffered(1))
    return pl.pallas_call(
        body, name="dw_out", grid=(steps,),
        in_specs=[pl.BlockSpec(memory_space=pltpu.SMEM), pl.BlockSpec((tl, M), lambda l: (l, 0)),
                  pl.BlockSpec((tl, N), lambda l: (l, 0))],
        out_specs=[whole((M, N)), whole((own_rows, N))],
        out_shape=[jax.ShapeDtypeStruct((M, N), BF16), jax.ShapeDtypeStruct((own_rows, N), F32)],
        scratch_shapes=[pltpu.VMEM((M, N), F32)], compiler_params=_params(("arbitrary",)),
    )(me1, y, dr)


def _position():
    return lax.axis_index("x"), lax.axis_index("y"), lax.axis_index("c")


def _index(px, py, pc):
    return 4 * px + 2 * py + pc


def _flip(pos, k):
    x, y, c = pos
    return ((1 - x) if (k >> 2) & 1 else x, (1 - y) if (k >> 1) & 1 else y, (1 - c) if k & 1 else c)


def _remote(src, dst, send_sem, recv_sem, peer):
    return pltpu.make_async_remote_copy(src_ref=src, dst_ref=dst, send_sem=send_sem, recv_sem=recv_sem,
                                        device_id=peer, device_id_type=MESH)


class _Flow:
    def __init__(self, kind, operand, result):
        self.kind, self.operand, self.result = kind, operand, result


class _Hosted:
    def __init__(self, operands, out_shapes, flows):
        self.operands, self.out_shapes, self.flows = operands, out_shapes, flows

    def plan(self, ins, outs, send_sems, recv_sems, local_sems):
        me = _position()
        mi = _index(*me)
        sends, recvs, locals_ = [], [], []
        for row, f in enumerate(self.flows):
            src, dst = ins[f.operand], outs[f.result]
            for k in range(1, N_DEV):
                peer = _flip(me, k)
                sems = (send_sems.at[row, k - 1], recv_sems.at[row, k - 1])
                if f.kind == "exchange":
                    sends.append(_remote(src.at[_index(*peer)], dst.at[k - 1], *sems, peer))
                    recvs.append(sends[-1])
                else:
                    sends.append(_remote(src, dst.at[mi], *sems, peer))
                    recvs.append(_remote(src, dst.at[_index(*peer)], *sems, peer))
            if f.kind == "gather":
                locals_.append(pltpu.make_async_copy(src, dst.at[mi], local_sems.at[row]))

        def start():
            for cp in locals_ + sends:
                cp.start()

        def wait():
            for cp in recvs:
                cp.wait_recv()
            for cp in sends:
                cp.wait_send()
            for cp in locals_:
                cp.wait()

        return start, wait


class _OwnerReduce:
    FIRST_STEP, SECOND_STEP, SEND_STEPS, REDUCE_STEPS = 2, 4, (2, 4, 6, 8), (5, 8, 10, 13)

    def __init__(self, stack, target_x):
        self.stack, self.target_x = stack, target_x
        block = stack.shape[1:]
        self.chunks = len(self.SEND_STEPS)
        self.chunk_rows = block[0] // self.chunks
        self.out_shapes = [jax.ShapeDtypeStruct(block, F32), jax.ShapeDtypeStruct((2,) + block, BF16)]
        self.out_specs = [pl.BlockSpec(block, lambda *_: (0, 0), pipeline_mode=pl.Buffered(1)),
                          pl.BlockSpec(memory_space=pl.ANY)]
        dma = pltpu.SemaphoreType.DMA
        self.scratch_shapes = ([pltpu.VMEM((2,) + block, F32)] * 2 + [pltpu.VMEM(block, BF16)] * 3
                               + [dma((self.chunks,))] * 4 + [dma((2,))] * 3)

    def plan(self, i, steps, stack_ref, own_ref, recv_ref, scratch):
        assert self.FIRST_STEP <= self.SEND_STEPS[0] and self.SECOND_STEP < self.REDUCE_STEPS[0] < steps - 1
        (theirs_scr, mine_scr, first_scr, across_scr, out_scr, y_send_sems, y_recv_sems, x_send_sems, x_recv_sems,
         swap_send_sems, swap_recv_sems, mine_sems) = scratch
        x, y, c = _position()
        owners_side = x == self.target_x
        other_side = x != self.target_x
        sibling, across, owner = (x, y, 1 - c), (x, 1 - y, c), (self.target_x, y, c)
        order = (1 - y, y)
        swaps = [_remote(stack_ref.at[2 * order[j] + (1 - c)], theirs_scr.at[j], swap_send_sems.at[j], swap_recv_sems.at[j],
                         sibling) for j in range(2)]
        mine = [pltpu.make_async_copy(stack_ref.at[2 * order[j] + c], mine_scr.at[j], mine_sems.at[j]) for j in range(2)]
        chunks = range(self.chunks)
        part = [pl.ds(j * self.chunk_rows, self.chunk_rows) for j in chunks]
        y_sems = lambda j: (y_send_sems.at[j], y_recv_sems.at[j])
        to_neighbour = [_remote(first_scr.at[part[j]], across_scr.at[part[j]], *y_sems(j), across) for j in chunks]
        to_owner_y = [_remote(first_scr.at[part[j]], recv_ref.at[0, part[j]], *y_sems(j), across) for j in chunks]
        to_owner_x = [_remote(out_scr.at[part[j]], recv_ref.at[1, part[j]], x_send_sems.at[j], x_recv_sems.at[j], owner)
                      for j in chunks]

        def before():
            @pl.when(i <= 1)
            def _():
                @pl.when(i == 0)
                def _():
                    for cp in [swaps[0]] + mine:
                        cp.start()

                pl.when(i == 1)(swaps[1].start)

        def after():
            @pl.when(i <= self.REDUCE_STEPS[-1])
            def _():
                @pl.when(i == self.FIRST_STEP)
                def _():
                    swaps[0].wait_recv()
                    mine[0].wait()
                    first_scr[...] = (mine_scr[0] + theirs_scr[0]).astype(first_scr.dtype)

                for j in chunks:
                    pl.when((i == self.SEND_STEPS[j]) & other_side)(to_neighbour[j].start)
                    pl.when((i == self.SEND_STEPS[j]) & owners_side)(to_owner_y[j].start)

                @pl.when(i == self.SECOND_STEP)
                def _():
                    swaps[1].wait_recv()
                    mine[1].wait()
                    t = mine_scr[1] + theirs_scr[1]
                    own_ref[...] = t
                    mine_scr[1] = t

                for j in chunks:
                    @pl.when((i == self.REDUCE_STEPS[j]) & other_side)
                    def _(j=j):
                        to_neighbour[j].wait_recv()
                        t = mine_scr[1, part[j], :] + across_scr[part[j], :].astype(F32)
                        out_scr[part[j], :] = t.astype(out_scr.dtype)
                        to_owner_x[j].start()

            @pl.when(i == steps - 1)
            def _():
                for cp in swaps:
                    cp.wait_send()
                for j in chunks:
                    @pl.when(other_side)
                    def _(j=j):
                        to_neighbour[j].wait_send()
                        to_owner_x[j].wait_send()

                    @pl.when(owners_side)
                    def _(j=j):
                        to_owner_y[j].wait_send()
                        to_owner_y[j].wait_recv()
                        to_owner_x[j].wait_recv()

        return before, after


def _call(body, comm, *, name, grid, in_specs, out_specs, out_shape, scratch_shapes, args, reduce=None):
    semantics = ("arbitrary",) * len(grid)
    if comm is None and reduce is None:
        return pl.pallas_call(body, name=name, grid=grid, in_specs=in_specs, out_specs=out_specs, out_shape=out_shape,
                              scratch_shapes=scratch_shapes, compiler_params=_params(semantics))(*args)
    n_in, n_out, n_scr = len(args), len(out_shape), len(scratch_shapes)
    c_operands, c_shapes, flows = (comm.operands, comm.out_shapes, comm.flows) if comm else ([], [], [])
    c_in, c_out, rows = len(c_operands), len(c_shapes), max(len(flows), 1)
    r_in = 0 if reduce is None else 1

    def hosted(*refs):
        ins, refs = refs[:n_in], refs[n_in:]
        cins, refs = refs[:c_in], refs[c_in:]
        rins, refs = refs[:r_in], refs[r_in:]
        outs, refs = refs[:n_out], refs[n_out:]
        couts, refs = refs[:c_out], refs[c_out:]
        routs, refs = refs[:2 * r_in], refs[2 * r_in:]
        scr, refs = refs[:n_scr], refs[n_scr:]
        (send_sems, recv_sems, local_sems), r_scr = refs[:3], refs[3:]
        ids = [pl.program_id(d) for d in range(len(grid))]
        first, last = ids[0] == 0, ids[0] == grid[0] - 1
        for d in range(1, len(grid)):
            first, last = first & (ids[d] == 0), last & (ids[d] == grid[d] - 1)
        before = after = lambda: None
        if reduce is not None:
            before, after = reduce.plan(ids[0], grid[0], rins[0], *routs, r_scr)
        if comm is not None:
            start, wait = comm.plan(cins, couts, send_sems, recv_sems, local_sems)
            pl.when(first)(start)
        before()
        body(*ins, *outs, *scr)
        after()
        if comm is not None:
            pl.when(last)(wait)

    any_spec = pl.BlockSpec(memory_space=pl.ANY)
    sems = [pltpu.SemaphoreType.DMA((rows, N_DEV - 1)), pltpu.SemaphoreType.DMA((rows, N_DEV - 1)),
            pltpu.SemaphoreType.DMA((rows,))]
    r_operands, r_specs, r_shapes, r_scratch = ([reduce.stack], reduce.out_specs, reduce.out_shapes,
                                                reduce.scratch_shapes) if reduce else ([], [], [], [])
    return pl.pallas_call(
        hosted, name=name, grid=grid, in_specs=list(in_specs) + [any_spec] * (c_in + r_in),
        out_specs=list(out_specs) + [any_spec] * c_out + r_specs, out_shape=list(out_shape) + list(c_shapes) + r_shapes,
        scratch_shapes=list(scratch_shapes) + sems + r_scratch,
        compiler_params=_params(semantics))(*args, *c_operands, *r_operands)


def _head_row(ref, width, rep):
    hid = lax.broadcasted_iota(jnp.int32, (1, width), 1) // rep
    row = jnp.zeros((1, width), F32)
    for h in range(N_HEADS):
        row = jnp.where(hid == h, ref[h], row)
    return row


def _rows_from_above(u_b, s, ext_scr, row, col):
    down = (row - col == s).astype(_MXU)
    return jnp.concatenate([ext_scr[8 - s:16 - s, :], jnp.dot(down, u_b, preferred_element_type=F32)[8:128]], axis=0)


def _ssd_recompute(first, p_ref, halo_ref, cw_ref, cb_ref, dtb_ref, alog_ref, e_ref, ext_scr, pre=None):
    row = lax.broadcasted_iota(jnp.int32, (128, 128), 0)
    col = lax.broadcasted_iota(jnp.int32, (128, 128), 1)
    ext_scr[0:8, :] = jnp.where(first, 0.0, halo_ref[:, S_XS:S_DT])
    if pre is not None:
        ext_scr[8:16, :] = p_ref[0:8, S_XS:S_DT]
    else:
        ext_scr[8:136, :] = p_ref[:, S_XS:S_DT]
        cw = cw_ref[...]
        pre = (cb_ref[0:1, :] + cw[3:4, :] * ext_scr[8:136, :] + cw[2:3, :] * ext_scr[7:135, :]
               + cw[1:2, :] * ext_scr[6:134, :] + cw[0:1, :] * ext_scr[5:133, :])
    sg = _sigmoid(pre)
    act = pre * sg
    lane = lax.broadcasted_iota(jnp.int32, (1, 128), 1)
    A = jnp.where(lane < N_HEADS, -jnp.exp(_head_row(alog_ref, 128, 1)), 0.0)
    raw = p_ref[:, S_DT:S_DT + 128] + _head_row(dtb_ref, 128, 1)
    dt = _softplus(raw)
    dA = dt * A
    tril = (row >= col).astype(BF16)
    acs = _mm_exact_l(tril, dA)
    last = acs[127:128, :]
    ds = jnp.exp(last - acs)
    eo = jnp.exp(acs)
    E = e_ref[...]
    ex = _mm_2pass_r(jnp.concatenate([dt, ds, eo], axis=0), E)
    dt_e, ds_e, eo_e = ex[0:128], ex[128:256], ex[256:384]
    xs_c = act[:, 0:1024]
    X = xs_c * dt_e
    return dict(pre=pre, sg=sg, xs_c=xs_c, Bc=act[:, 1024:1280], Cc=act[:, 1280:1536], A=A, raw=raw, dt=dt,
                acs=acs, acsT=acs.T, eo_e=eo_e, ds_e=ds_e, dt_e=dt_e, cd_e=eo_e[127:128, :],
                X=X, Xd=X * ds_e, row=row, col=col)


def _split_halves(t):
    lo = _lo_half(CHUNK)
    return jnp.concatenate([jnp.where(lo, t, 0.0), jnp.where(lo, 0.0, t)], axis=0)


def _ssd_core(R, hprev):
    causal = R["row"] >= R["col"]
    acs, acsT, X = R["acs"], R["acsT"], R["X"]
    ydiag, yoff, snew = [], [], []
    for g in range(SSD_GROUPS):
        Bg = R["Bc"][:, g * 128:(g + 1) * 128]
        Cg = R["Cc"][:, g * 128:(g + 1) * 128]
        cols = slice(g * 512, (g + 1) * 512)
        CB = _mm_nt(Cg, Bg)
        snew.append(_mm_tn(Bg, R["Xd"][:, cols]))
        yoff.append(_mm(Cg, hprev[:, cols]))
        for j in range(4):
            h0 = g * 8 + 2 * j
            ms = [CB * jnp.exp(jnp.where(causal, acs[:, h:h + 1] - acsT[h:h + 1, :], NEG)) for h in (h0, h0 + 1)]
            ydiag.append(_mm(jnp.concatenate(ms, axis=1), _split_halves(X[:, h0 * HEAD_DIM:h0 * HEAD_DIM + 128])))
    Y = jnp.concatenate(ydiag, axis=1) + jnp.concatenate(yoff, axis=1) * R["eo_e"]
    return Y, jnp.concatenate(snew, axis=1)


def _ssd_forward_step(p_ref, halo_ref, cw_ref, cb_ref, dtb_ref, alog_ref, dsk_ref, nw_ref, e_ref,
                      y_ref, ypre_ref, hprev_ref, pre_ref, h_scr, ext_scr):
    c = pl.program_id(0)
    first = c == 0

    @pl.when(first)
    def _():
        h_scr[...] = jnp.zeros_like(h_scr)

    R = _ssd_recompute(first, p_ref, halo_ref, cw_ref, cb_ref, dtb_ref, alog_ref, e_ref, ext_scr)
    hprev = h_scr[...]
    hprev_ref[...] = hprev
    pre_ref[...] = R["pre"]
    Y, snew = _ssd_core(R, hprev)
    h_scr[...] = hprev * R["cd_e"] + snew
    Y = Y + _head_row(dsk_ref, D_SSD, HEAD_DIM) * R["xs_c"]
    ypre_ref[...] = Y
    z = p_ref[:, S_Z:S_Z + 1024]
    yf = Y * (z * _sigmoid(z))
    outs = []
    for g in range(SSD_GROUPS):
        yg = yf[:, g * 512:(g + 1) * 512]
        r = lax.rsqrt(jnp.mean(yg * yg, axis=-1, keepdims=True) + RMS_EPS)
        outs.append(yg * r)
    y_ref[:, 0:D_SSD] = (jnp.concatenate(outs, axis=1) * nw_ref[0:1, :]).astype(y_ref.dtype)


def _ssd_backward(proj_ssd, hprev_all, ypre, pre, dy, conv_w, conv_b, dt_bias, a_log, d_skip, norm_w, E, ET, comm=None):
    L = proj_ssd.shape[0]
    nc = L // CHUNK

    def body(p_ref, halo_ref, hprev_ref, ypre_ref, pre_ref, dy_ref, cw_ref, cb_ref, dtb_ref, alog_ref, dsk_ref, nw_ref, e_ref,
             et_ref, dp_ref, acc_cw_ref, acc_w_ref, acc_s_ref, dh_scr, ext_scr, ext2_scr, nxt_scr):
        i = pl.program_id(0)
        c = nc - 1 - i
        first = c == 0

        @pl.when(i == 0)
        def _():
            dh_scr[...] = jnp.zeros_like(dh_scr)
            nxt_scr[...] = jnp.zeros_like(nxt_scr)
            acc_cw_ref[...] = jnp.zeros_like(acc_cw_ref)
            acc_w_ref[...] = jnp.zeros_like(acc_w_ref)
            acc_s_ref[...] = jnp.zeros_like(acc_s_ref)

        R = _ssd_recompute(first, p_ref, halo_ref, cw_ref, cb_ref, dtb_ref, alog_ref, e_ref, ext_scr, pre_ref[...])
        hprev = hprev_ref[...]
        xs_c, X, Xd = R["xs_c"], R["X"], R["Xd"]
        acs, acsT = R["acs"], R["acsT"]
        ET = et_ref[...]
        dsk = _head_row(dsk_ref, D_SSD, HEAD_DIM)
        Y = ypre_ref[...]

        z = p_ref[:, S_Z:S_Z + 1024]
        sz = _sigmoid(z)
        silz = z * sz
        yf = Y * silz
        dyv = dy_ref[...]
        nw = nw_ref[0:1, :]
        dyf_parts, dnw_parts = [], []
        for g in range(SSD_GROUPS):
            cols = slice(g * 512, (g + 1) * 512)
            yg = yf[:, cols]
            r = lax.rsqrt(jnp.mean(yg * yg, axis=-1, keepdims=True) + RMS_EPS)
            yn = yg * r
            dyn = dyv[:, cols] * nw[:, cols]
            dnw_parts.append(_colsum(dyv[:, cols] * yn))
            dyf_parts.append(r * (dyn - yn * jnp.mean(dyn * yn, axis=-1, keepdims=True)))
        dyf = jnp.concatenate(dyf_parts, axis=1)
        dY = dyf * silz
        dz = dyf * Y * (sz * (1.0 + z * (1.0 - sz)))

        dhn = dh_scr[...]
        dYo = dY * R["eo_e"]
        causal = R["row"] >= R["col"]
        dacs = jnp.zeros((128, 128), F32)
        dacs_t = jnp.zeros((128, 128), F32)
        dxdiag, dxd, dhprev, dBs, dCs, yoff = [], [], [], [], [], []
        for g in range(SSD_GROUPS):
            Bg = R["Bc"][:, g * 128:(g + 1) * 128]
            Cg = R["Cc"][:, g * 128:(g + 1) * 128]
            cols = slice(g * 512, (g + 1) * 512)
            CB = _mm_nt(Cg, Bg)
            dCB = jnp.zeros((128, 128), F32)
            for j in range(4):
                h0 = g * 8 + 2 * j
                pc = slice(h0 * HEAD_DIM, h0 * HEAD_DIM + 128)
                dYst = _split_halves(dY[:, pc])
                dMst = _mm_nt(dYst, X[:, pc])
                mts = []
                for a, h in enumerate((h0, h0 + 1)):
                    acol = acs[:, h:h + 1]
                    arow = acsT[h:h + 1, :]
                    Lm = jnp.exp(jnp.where(causal, acol - arow, NEG))
                    M = CB * Lm
                    dM = dMst[a * 128:(a + 1) * 128]
                    dCB = dCB + dM * Lm
                    G = dM * M
                    dacs = dacs + jnp.where(R["col"] == h, jnp.sum(G, axis=1, keepdims=True), 0.0)
                    dacs_t = dacs_t + jnp.where(R["row"] == h, jnp.sum(G, axis=0, keepdims=True), 0.0)
                    mts.append(M.T)
                dxdiag.append(_mm(jnp.concatenate(mts, axis=1), dYst))
            dS = dhn[:, cols]
            dxd.append(_mm(Bg, dS))
            yoff.append(_mm(Cg, hprev[:, cols]))
            dhprev.append(_mm_tn(Cg, dYo[:, cols]))
            dCs.append(_mm_nt(dYo[:, cols], hprev[:, cols]) + _mm(dCB, Bg))
            dBs.append(_mm_tn(dCB, Cg) + _mm_nt(Xd[:, cols], dS))
        Yoff = jnp.concatenate(yoff, axis=1) * R["eo_e"]
        dXd = jnp.concatenate(dxd, axis=1)
        dX = jnp.concatenate(dxdiag, axis=1) + dXd * R["ds_e"]
        t_state = dXd * Xd
        dacs = dacs + _mm_2pass_r(dY * Yoff - t_state, ET) - dacs_t.T
        v_last = _colsum(t_state + dhn * hprev * R["cd_e"])
        dlast = _mm_exact_r(jnp.broadcast_to(v_last, (8, 1024)), ET)[0:1, :]
        dacs = dacs + jnp.where(R["row"] == 127, dlast, 0.0)
        triu = (R["col"] >= R["row"]).astype(BF16)
        da = _mm_exact_l(triu, dacs)
        ddt = da * R["A"] + _mm(dX * xs_c, ET)
        ddt_raw = ddt * _sigmoid(R["raw"])
        dxs_c = dX * R["dt_e"] + dY * dsk
        dh_scr[...] = jnp.concatenate(dhprev, axis=1) + dhn * R["cd_e"]

        dact = jnp.concatenate([dxs_c] + dBs + dCs, axis=1)
        pre, sg = R["pre"], R["sg"]
        dpre = dact * (sg * (1.0 + pre * (1.0 - sg)))
        ext2_scr[0:8, :] = dpre[120:128, :]
        ext2_scr[8:16, :] = nxt_scr[...]
        nxt_scr[...] = dpre[0:8, :]
        cw = cw_ref[...]
        u_b, dpre_b = p_ref[:, S_XS:S_DT].astype(_MXU), dpre.astype(_MXU)
        dxbc = cw[3:4, :] * dpre
        taps = [_colsum(dpre * p_ref[:, S_XS:S_DT])]
        for s in (1, 2, 3):
            up = (R["col"] - R["row"] == s).astype(_MXU)
            d_s = jnp.concatenate([jnp.dot(up, dpre_b, preferred_element_type=F32)[0:120],
                                   ext2_scr[s:8 + s, :]], axis=0)
            dxbc = dxbc + cw[3 - s:4 - s, :] * d_s
            taps.append(_colsum(dpre * _rows_from_above(u_b, s, ext_scr, R["row"], R["col"])))
        acc_cw_ref[...] += _rows8(taps[::-1] + [_colsum(dpre)])
        acc_w_ref[...] += _rows8([jnp.concatenate(dnw_parts, axis=1), _colsum(dY * xs_c)])
        acc_s_ref[...] += _rows8([_colsum(ddt_raw), _colsum(da * R["dt"])])

        lane = lax.broadcasted_iota(jnp.int32, (128, 128), 1)
        dp_ref[:, S_Z:S_Z + 1024] = dz.astype(dp_ref.dtype)
        dp_ref[:, S_XS:S_DT] = dxbc.astype(dp_ref.dtype)
        dp_ref[:, S_DT:S_DT + 128] = jnp.where(lane < N_HEADS, ddt_raw, 0.0).astype(dp_ref.dtype)
        dp_ref[:, S_DT + 128:S_W] = jnp.zeros((128, 128), dp_ref.dtype)

        @pl.when(i == nc - 1)
        def _():
            acc = acc_s_ref[...]
            dskip = _mm_exact_r(acc_w_ref[...], ET)[1:2, :]
            acc_s_ref[...] = _rows8([acc[0:1, :], acc[1:2, :] * R["A"], dskip])

    const = lambda shape: pl.BlockSpec(shape, lambda i: (0, 0))
    smem = pl.BlockSpec(memory_space=pltpu.SMEM)
    rev = lambda i: (nc - 1 - i, 0)
    return _call(
        body, comm, name="ssd_bwd", grid=(nc,),
        in_specs=[pl.BlockSpec((CHUNK, S_W), rev),
                  pl.BlockSpec((8, S_W), lambda i: (jnp.maximum((nc - 1 - i) * 16 - 1, 0), 0)),
                  pl.BlockSpec((128, 1024), rev),
                  pl.BlockSpec((CHUNK, D_SSD), rev),
                  pl.BlockSpec((CHUNK, D_XBC), rev),
                  pl.BlockSpec((CHUNK, D_SSD), rev),
                  const((4, D_XBC)), const((1, D_XBC)), smem, smem, smem, const((1, 1024)),
                  const((128, 1024)), const((1024, 128))],
        out_specs=[pl.BlockSpec((CHUNK, S_W), rev), const((8, D_XBC)), const((8, 1024)), const((8, 128))],
        out_shape=[jax.ShapeDtypeStruct((L, S_W), _MXU), jax.ShapeDtypeStruct((8, D_XBC), F32),
                   jax.ShapeDtypeStruct((8, 1024), F32), jax.ShapeDtypeStruct((8, 128), F32)],
        scratch_shapes=[pltpu.VMEM((128, 1024), F32), pltpu.VMEM((16, D_XBC), F32),
                        pltpu.VMEM((16, D_XBC), F32), pltpu.VMEM((8, D_XBC), F32)],
        args=(proj_ssd, proj_ssd, hprev_all, ypre, pre, dy, conv_w, conv_b, dt_bias, a_log, d_skip, norm_w, E, ET))


def _rope(t, tab):
    cos, sa, sb = tab[:, 0:128], tab[:, 128:256], tab[:, 256:384]
    outs = []
    for i in range(t.shape[1] // 128):
        tg = t[:, i * 128:(i + 1) * 128]
        outs.append(tg * cos + pltpu.roll(tg, 8, 1) * sa + pltpu.roll(tg, 120, 1) * sb)
    return jnp.concatenate(outs, axis=1)


def _rope_transposed(d, tab):
    cos, sa, sb = tab[:, 0:128], tab[:, 128:256], tab[:, 256:384]
    outs = []
    for i in range(d.shape[1] // 128):
        dg = d[:, i * 128:(i + 1) * 128]
        outs.append(dg * cos + pltpu.roll(dg * sa, 120, 1) + pltpu.roll(dg * sb, 8, 1))
    return jnp.concatenate(outs, axis=1)


def _lo_half(rows):
    return lax.broadcasted_iota(jnp.int32, (rows, 128), 1) < HEAD_DIM


def _native_half(rows, j):
    lo = _lo_half(rows)
    return lo if j % 2 == 0 else jnp.logical_not(lo)


def _kv_native(t, j):
    p = j // 2
    return jnp.where(_native_half(t.shape[0], j), t[:, p * 128:(p + 1) * 128], 0.0)


def _stack_heads(t, j):
    out = []
    for m in (2 * j, 2 * j + 1):
        pair = t[:, m * 128:(m + 1) * 128]
        swapped = pltpu.roll(pair, HEAD_DIM, 1)
        out += [pair, swapped] if j % 2 == 0 else [swapped, pair]
    return jnp.concatenate(out, axis=0)


def _unstack_heads(s, j):
    out = []
    for m in range(2):
        first, second = s[256 * m:256 * m + 128], s[256 * m + 128:256 * m + 256]
        if j % 2 == 0:
            out.append(first + pltpu.roll(second, HEAD_DIM, 1))
        else:
            out.append(pltpu.roll(first, HEAD_DIM, 1) + second)
    return jnp.concatenate(out, axis=1)


def _keep_native(r, j):
    return jnp.where(_native_half(r.shape[0], j), r, 0.0)


def _sink_row(sink_ref, j):
    hid = lax.broadcasted_iota(jnp.int32, (1, 4 * CHUNK), 1) // CHUNK
    row = jnp.zeros((1, 4 * CHUNK), F32)
    for hh in range(4):
        row = jnp.where(hid == hh, sink_ref[4 * j + hh], row)
    return row


def _from_current():
    si = lax.broadcasted_iota(jnp.int32, (CHUNK, 4 * CHUNK), 0)
    qi = lax.broadcasted_iota(jnp.int32, (CHUNK, 4 * CHUNK), 1) % CHUNK
    return si <= qi


def _fold(full, from_cur, pen=0.0):
    return jnp.where(from_cur, full[CHUNK:2 * CHUNK], full[0:CHUNK] + pen)


def _unfold(t, from_cur):
    c = jnp.where(from_cur, t, 0.0)
    return jnp.concatenate([t - c, c], axis=0)


def _softmax_sink(s, sink):
    mx = jnp.maximum(jnp.max(s, axis=0, keepdims=True), sink)
    p = jnp.exp(s - mx)
    esink = jnp.exp(sink - mx)
    inv = 1.0 / (jnp.sum(p, axis=0, keepdims=True) + esink)
    return p * inv, esink * inv


def _swa_inputs(blk, p_ref, prev_ref, tab_ref, ptab_ref):
    tab = tab_ref[...]
    qr = _rope(p_ref[:, A_Q:A_Q + 1024], tab) * ATT_SCALE
    kk = jnp.concatenate([_rope(prev_ref[:, 0:256], ptab_ref[...]), _rope(p_ref[:, A_K:A_K + 256], tab)], axis=0)
    vv = jnp.concatenate([prev_ref[:, 256:512], p_ref[:, A_V:A_V + 256]], axis=0)
    return tab, qr, kk, vv, jnp.where(blk > 0, 0.0, NEG)


def _swa_forward_step(sink_ref, p_ref, prev_ref, tab_ref, ptab_ref, y_ref):
    n = pl.program_id(0)
    _, qr, kk, vv, pen = _swa_inputs(n, p_ref, prev_ref, tab_ref, ptab_ref)
    from_cur = _from_current()
    outs = []
    for j in range(KV_HEADS):
        s = _fold(_mm_nt(_kv_native(kk, j), _stack_heads(qr, j)), from_cur, pen)
        P, _ = _softmax_sink(s, _sink_row(sink_ref, j))
        outs.append(_unstack_heads(_mm_tn(_unfold(P, from_cur), _kv_native(vv, j)), j))
    g = p_ref[:, A_G:A_G + 1024]
    y_ref[:, D_SSD:D_SSD + D_ATT] = (jnp.concatenate(outs, axis=1) * (g * _sigmoid(g))).astype(y_ref.dtype)


def _mixer_forward(proj_ssd, proj_att, tabs, sinks, conv_w, conv_b, dt_bias, a_log, d_skip, norm_w, E, comm=None):
    L = proj_ssd.shape[0]
    nc = L // CHUNK

    def body(p_ref, halo_ref, cw_ref, cb_ref, dtb_ref, alog_ref, dsk_ref, nw_ref, e_ref,
             sink_ref, pa_ref, prev_ref, tab_ref, ptab_ref, y_ref, ypre_ref, hprev_ref, pre_ref, h_scr, ext_scr):
        _ssd_forward_step(p_ref, halo_ref, cw_ref, cb_ref, dtb_ref, alog_ref, dsk_ref, nw_ref, e_ref,
                          y_ref, ypre_ref, hprev_ref, pre_ref, h_scr, ext_scr)
        _swa_forward_step(sink_ref, pa_ref, prev_ref, tab_ref, ptab_ref, y_ref)

    const = lambda shape: pl.BlockSpec(shape, lambda c: (0, 0))
    smem = pl.BlockSpec(memory_space=pltpu.SMEM)
    rows = lambda w: pl.BlockSpec((CHUNK, w), lambda c: (c, 0))
    return _call(
        body, comm, name="mixer_fwd", grid=(nc,),
        in_specs=[rows(S_W), pl.BlockSpec((8, S_W), lambda c: (jnp.maximum(c * 16 - 1, 0), 0)),
                  const((4, D_XBC)), const((1, D_XBC)), smem, smem, smem, const((1, 1024)), const((128, 1024)),
                  smem, rows(A_W), pl.BlockSpec((CHUNK, 512), lambda c: (jnp.maximum(c - 1, 0), 2)),
                  rows(384), pl.BlockSpec((CHUNK, 384), lambda c: (jnp.maximum(c - 1, 0), 0))],
        out_specs=[rows(D_SSD + D_ATT), rows(D_SSD), pl.BlockSpec((128, 1024), lambda c: (c, 0)), rows(D_XBC)],
        out_shape=[jax.ShapeDtypeStruct((L, D_SSD + D_ATT), _MXU), jax.ShapeDtypeStruct((L, D_SSD), F32),
                   jax.ShapeDtypeStruct((nc * 128, 1024), F32), jax.ShapeDtypeStruct((L, D_XBC), F32)],
        scratch_shapes=[pltpu.VMEM((128, 1024), F32), pltpu.VMEM((136, D_XBC), F32)],
        args=(proj_ssd, proj_ssd, conv_w, conv_b, dt_bias, a_log, d_skip, norm_w, E,
              sinks, proj_att, proj_att, tabs, tabs))


def _swa_backward(proj_att, tabs, sinks, dy, reduce=None):
    L = proj_att.shape[0]
    nb = L // CHUNK

    def body(sink_ref, p_ref, prev_ref, tab_ref, ptab_ref, dy_ref, dp_ref, dsink_ref, carry_k, carry_v):
        i = pl.program_id(0)
        n = nb - 1 - i

        @pl.when(i == 0)
        def _():
            carry_k[...] = jnp.zeros_like(carry_k)
            carry_v[...] = jnp.zeros_like(carry_v)
            dsink_ref[...] = jnp.zeros_like(dsink_ref)

        tab, qr, kk, vv, pen = _swa_inputs(n, p_ref, prev_ref, tab_ref, ptab_ref)
        from_cur = _from_current()
        g = p_ref[:, A_G:A_G + 1024]
        sgm = _sigmoid(g)
        dyv = dy_ref[...]
        do_all = dyv * (g * sgm)
        lane8 = lax.broadcasted_iota(jnp.int32, (8, 128), 1)
        hid = lax.broadcasted_iota(jnp.int32, (1, 4 * CHUNK), 1) // CHUNK
        o_parts, dq_parts = [], []
        dk_nat = [jnp.zeros((2 * CHUNK, 128), F32) for _ in range(2)]
        dv_nat = [jnp.zeros((2 * CHUNK, 128), F32) for _ in range(2)]
        dsink = jnp.zeros((8, 128), F32)
        for j in range(KV_HEADS):
            qs = _stack_heads(qr, j)
            kkb, vvb = _kv_native(kk, j), _kv_native(vv, j)
            P, psink = _softmax_sink(_fold(_mm_nt(kkb, qs), from_cur, pen), _sink_row(sink_ref, j))
            p_full = _unfold(P, from_cur)
            o_parts.append(_unstack_heads(_mm_tn(p_full, vvb), j))
            do_s = _stack_heads(do_all, j)
            dP = _fold(_mm_nt(vvb, do_s), from_cur)
            D = jnp.sum(P * dP, axis=0, keepdims=True)
            ds_full = _unfold(P * (dP - D), from_cur)
            sd = psink * D
            for hh in range(4):
                dsink = dsink + jnp.where(lane8 == 4 * j + hh, -jnp.sum(jnp.where(hid == hh, sd, 0.0)), 0.0)
            dq_parts.append(_unstack_heads(_mm_tn(ds_full, kkb), j) * ATT_SCALE)
            dk_nat[j // 2] = dk_nat[j // 2] + _keep_native(_mm(ds_full, qs), j)
            dv_nat[j // 2] = dv_nat[j // 2] + _keep_native(_mm(p_full, do_s), j)
        o = jnp.concatenate(o_parts, axis=1)
        dkk = jnp.concatenate(dk_nat, axis=1)
        dvv = jnp.concatenate(dv_nat, axis=1)
        out = dp_ref.dtype
        dp_ref[:, A_Q:A_Q + 1024] = _rope_transposed(jnp.concatenate(dq_parts, axis=1), tab).astype(out)
        dp_ref[:, A_K:A_K + 256] = _rope_transposed(dkk[CHUNK:2 * CHUNK] + carry_k[...], tab).astype(out)
        dp_ref[:, A_V:A_V + 256] = (dvv[CHUNK:2 * CHUNK] + carry_v[...]).astype(out)
        dp_ref[:, A_G:A_G + 1024] = (dyv * o * (sgm * (1.0 + g * (1.0 - sgm)))).astype(out)
        carry_k[...] = dkk[0:CHUNK]
        carry_v[...] = dvv[0:CHUNK]
        dsink_ref[...] += dsink

    rev = lambda i: (nb - 1 - i, 0)
    prev = lambda i: jnp.maximum(nb - 2 - i, 0)
    return _call(
        body, None, name="swa_bwd", grid=(nb,),
        in_specs=[pl.BlockSpec(memory_space=pltpu.SMEM),
                  pl.BlockSpec((CHUNK, A_W), rev),
                  pl.BlockSpec((CHUNK, 512), lambda i: (prev(i), 2)),
                  pl.BlockSpec((CHUNK, 384), rev),
                  pl.BlockSpec((CHUNK, 384), lambda i: (prev(i), 0)),
                  pl.BlockSpec((CHUNK, D_ATT), lambda i: (nb - 1 - i, 1))],
        out_specs=[pl.BlockSpec((CHUNK, A_W), rev), pl.BlockSpec((8, 128), lambda i: (0, 0))],
        out_shape=[jax.ShapeDtypeStruct((L, A_W), _MXU), jax.ShapeDtypeStruct((8, 128), F32)],
        scratch_shapes=[pltpu.VMEM((CHUNK, 256), F32), pltpu.VMEM((CHUNK, 256), F32)],
        args=(sinks, proj_att, proj_att, tabs, tabs, dy), reduce=reduce)


def _head(y, x, target, w_out, ln_g, ln_b, *, tm):
    L = x.shape[0]
    nsteps = L // tm

    def body(y_ref, x_ref, t_ref, wo_ref, g_ref, b_ref, dr_ref, dy_ref, acc_ref):
        i = pl.program_id(0)

        @pl.when(i == 0)
        def _():
            acc_ref[...] = jnp.zeros_like(acc_ref)

        r = ALPHA * x_ref[...] + _mm(y_ref[...], wo_ref[...])
        mu = jnp.mean(r, axis=-1, keepdims=True)
        d = r - mu
        rstd = lax.rsqrt(jnp.mean(d * d, axis=-1, keepdims=True) + LN_EPS)
        xh = d * rstd
        gam = g_ref[0:1, :]
        e = xh * gam + b_ref[0:1, :] - t_ref[...]
        dout = e * (1.0 / D_MODEL)
        dxh = dout * gam
        dr = rstd * (dxh - jnp.mean(dxh, axis=-1, keepdims=True)
                     - xh * jnp.mean(dxh * xh, axis=-1, keepdims=True))
        dr_ref[...] = dr
        dy_ref[...] = _mm_nt(dr, wo_ref[...])
        acc_ref[...] += _rows8([_colsum(dout * xh), _colsum(dout), _colsum(e * e) * (0.5 / D_MODEL)])

        @pl.when(i == nsteps - 1)
        def _():
            acc = acc_ref[...]
            tot = jnp.sum(acc[2:3, :])
            rid = lax.broadcasted_iota(jnp.int32, (8, 1024), 0)
            acc_ref[...] = jnp.where(rid == 3, tot, acc)

    const = lambda shape: pl.BlockSpec(shape, lambda i: (0, 0))
    row = lambda w: pl.BlockSpec((tm, w), lambda i: (i, 0))
    return pl.pallas_call(
        body, name="head", grid=(nsteps,),
        in_specs=[row(2048), row(1024), row(1024), const((2048, 1024)), const((1, 1024)), const((1, 1024))],
        out_specs=[row(1024), row(2048), const((8, 1024))],
        out_shape=[jax.ShapeDtypeStruct((L, D_MODEL), F32), jax.ShapeDtypeStruct((L, 2048), F32),
                   jax.ShapeDtypeStruct((8, 1024), F32)],
        compiler_params=_params(("arbitrary",)),
    )(y, x, target, w_out, ln_g, ln_b)


def _gather_w_in(w_shard, positions):
    R = w_shard.shape[0]
    halves = (pl.ds(0, R // 2), pl.ds(R // 2, R // 2))
    any_spec = pl.BlockSpec(memory_space=pl.ANY)
    vmem = pl.BlockSpec(memory_space=pltpu.VMEM)

    def body(in_ref, pos_ref, inv_ref, out_ref, tab_ref, tab_scr, send_sems, recv_sems, local_sem, tab_sem):
        x, y, c = _position()

        def slot(p, half=None):
            s = out_ref.at[_index(*p)]
            return s if half is None else s.at[halves[half]]

        def same_core(p):
            return (p[0], p[1], c)

        def other_core(p):
            return (p[0], p[1], 1 - c)

        me, xn, yn, dg = (x, y), (1 - x, y), (x, 1 - y), (1 - x, 1 - y)

        def copy(k, dst, to, src=None):
            return _remote(dst if src is None else src, dst, send_sems.at[k], recv_sems.at[k], to)

        local = pltpu.make_async_copy(in_ref, slot(same_core(me)), local_sem)
        local.start()
        own = [copy(0, slot(same_core(me)), other_core(me), in_ref), copy(1, slot(same_core(me)), same_core(xn), in_ref),
               copy(2, slot(same_core(me)), same_core(yn), in_ref)]
        for cp in own:
            cp.start()
        _rope_tables(pos_ref, inv_ref, tab_scr)
        tab_out = pltpu.make_async_copy(tab_scr, tab_ref, tab_sem)
        tab_out.start()
        copy(1, slot(same_core(xn)), same_core(xn)).wait_recv()
        passed = [copy(4, slot(same_core(xn), 1), same_core(yn)), copy(5, slot(same_core(xn)), other_core(me))]
        for cp in passed:
            cp.start()
        copy(2, slot(same_core(yn)), same_core(yn)).wait_recv()
        more = [copy(3, slot(same_core(yn), 0), same_core(xn)), copy(6, slot(same_core(yn)), other_core(me))]
        for cp in more:
            cp.start()
        passed += more
        for k, half in ((3, 0), (4, 1)):
            copy(k, slot(same_core(dg), half), same_core(xn)).wait_recv()
            fwd = copy(7 + half, slot(same_core(dg), half), other_core(me))
            fwd.start()
            passed.append(fwd)
        copy(0, slot(other_core(me)), other_core(me)).wait_recv()
        copy(5, slot(other_core(xn)), other_core(me)).wait_recv()
        copy(6, slot(other_core(yn)), other_core(me)).wait_recv()
        for half in (0, 1):
            copy(7 + half, slot(other_core(dg), half), other_core(me)).wait_recv()
        for cp in own + passed:
            cp.wait_send()
        local.wait()
        tab_out.wait()

    return pl.pallas_call(
        body, name="gather_w_in", in_specs=[any_spec, vmem, vmem], out_specs=[any_spec, any_spec],
        out_shape=[jax.ShapeDtypeStruct((N_DEV,) + w_shard.shape, w_shard.dtype),
                   jax.ShapeDtypeStruct((positions.size, 384), F32)],
        scratch_shapes=[pltpu.VMEM((positions.size, 384), F32), pltpu.SemaphoreType.DMA((9,)),
                        pltpu.SemaphoreType.DMA((9,)), pltpu.SemaphoreType.DMA, pltpu.SemaphoreType.DMA],
        compiler_params=_params(),
    )(w_shard, positions, jnp.asarray(ROPE_INV)[None, :])


def _input_gradient(d_ssd, d_att, w_ssd, w_att, dr, *, tm, comm=None, reduce=None):
    L = dr.shape[0]

    def body(ds_ref, da_ref, ws_ref, wa_ref, dr_ref, o_ref):
        o_ref[...] = ALPHA * dr_ref[...] + _mm_nt(ds_ref[...], ws_ref[...]) + _mm_nt(da_ref[...], wa_ref[...])

    row = lambda w: pl.BlockSpec((tm, w), lambda i: (i, 0))
    resident = lambda a: pl.BlockSpec(a.shape, lambda i: (0, 0), pipeline_mode=pl.Buffered(1))
    return _call(body, comm, name="dx", grid=(L // tm,),
                 in_specs=[row(S_W), row(A_W), resident(w_ssd), resident(w_att), row(D_MODEL)],
                 out_specs=[row(D_MODEL)], out_shape=[jax.ShapeDtypeStruct((L, D_MODEL), F32)],
                 scratch_shapes=[], args=(d_ssd, d_att, w_ssd, w_att, dr), reduce=reduce)


SHARD_COLS = D_IN_PROJ // N_DEV
SPLIT = N_SSD_REAL - 4 * SHARD_COLS
RELAYOUT_ROWS = 256


def _unpack_w_in(w_all):
    def body(g_ref, ws_ref, wa_ref):
        for j in range(4):
            ws_ref[:, SHARD_COLS * j:SHARD_COLS * (j + 1)] = g_ref[j]
        ws_ref[:, 4 * SHARD_COLS:N_SSD_REAL] = g_ref[4, :, 0:SPLIT]
        ws_ref[:, N_SSD_REAL:S_W] = jnp.zeros((RELAYOUT_ROWS, S_W - N_SSD_REAL), ws_ref.dtype)
        wa_ref[:, 0:SHARD_COLS - SPLIT] = g_ref[4, :, SPLIT:SHARD_COLS]
        for j in range(5, N_DEV):
            lo = SHARD_COLS * (j - 4) - SPLIT
            wa_ref[:, lo:lo + SHARD_COLS] = g_ref[j]

    return pl.pallas_call(
        body, name="unpack_w_in", grid=(D_MODEL // RELAYOUT_ROWS,),
        in_specs=[pl.BlockSpec((N_DEV, RELAYOUT_ROWS, SHARD_COLS), lambda i: (0, i, 0))],
        out_specs=[pl.BlockSpec((RELAYOUT_ROWS, S_W), lambda i: (i, 0)), pl.BlockSpec((RELAYOUT_ROWS, A_W), lambda i: (i, 0))],
        out_shape=[jax.ShapeDtypeStruct((D_MODEL, S_W), w_all.dtype), jax.ShapeDtypeStruct((D_MODEL, A_W), w_all.dtype)],
        compiler_params=_params(("arbitrary",)),
    )(w_all)


def _dw_in(xb, d, half, tail=None, *, tl=1024):
    L, N = d.shape
    steps = L // tl

    def body(x_ref, d_ref, *refs):
        if half == 0:
            p_ref, tail_ref, acc, p_scr, p_sems = refs
        else:
            t_ref, p_ref, acc, p_scr, p_sems = refs
        l = pl.program_id(0)

        @pl.when(l == 0)
        def _():
            acc[...] = jnp.zeros_like(acc)

        acc[...] += _mm_tn(x_ref[...], d_ref[...])

        @pl.when(l == steps - 1)
        def _():
            if half == 0:
                tail_ref[...] = acc[:, S_DT:S_W]
            outs = []
            for j in range(4):
                if half == 0:
                    pieces = [(0, acc[:, SHARD_COLS * j:SHARD_COLS * (j + 1)])]
                elif j == 0:
                    pieces = [(0, t_ref[:, 4 * SHARD_COLS - S_DT:N_SSD_REAL - S_DT]), (SPLIT, acc[:, 0:SHARD_COLS - SPLIT])]
                else:
                    lo = SHARD_COLS * j - SPLIT
                    pieces = [(0, acc[:, lo:lo + SHARD_COLS])]
                for off, blk in pieces:
                    p_scr[j, :, off:off + blk.shape[1]] = blk
                outs.append(pltpu.make_async_copy(p_scr.at[j], p_ref.at[j], p_sems.at[j]))
                outs[-1].start()
            for cp in outs:
                cp.wait()

    once = pl.Buffered(1)
    whole = lambda shape: pl.BlockSpec(shape, lambda l: (0,) * len(shape), pipeline_mode=once)
    in_specs = [pl.BlockSpec((tl, D_MODEL), lambda l: (l, 0)), pl.BlockSpec((tl, N), lambda l: (l, 0))]
    args = [xb, d]
    stack = jax.ShapeDtypeStruct((4, D_MODEL, SHARD_COLS), F32)
    out_shape, out_specs = [stack], [pl.BlockSpec(memory_space=pl.ANY)]
    if half == 0:
        out_shape.append(jax.ShapeDtypeStruct((D_MODEL, S_W - S_DT), F32))
        out_specs.append(whole(out_shape[-1].shape))
    else:
        in_specs.append(whole(tail.shape))
        args.append(tail)
    return pl.pallas_call(
        body, name="dw_in_%d" % half, grid=(steps,), in_specs=in_specs, out_specs=out_specs, out_shape=out_shape,
        scratch_shapes=[pltpu.VMEM((D_MODEL, N), F32), pltpu.VMEM(stack.shape, F32), pltpu.SemaphoreType.DMA((4,))],
        compiler_params=_params(("arbitrary",)),
    )(*args)


def _adamw_math(w, g, m, v):
    m = ADAM_B1 * m + (1.0 - ADAM_B1) * g
    v = ADAM_B2 * v + (1.0 - ADAM_B2) * (g * g)
    m_hat = m / (1.0 - ADAM_B1 ** ADAM_STEP)
    v_hat = v / (1.0 - ADAM_B2 ** ADAM_STEP)
    delta = -ADAM_LR * (m_hat / (jnp.sqrt(v_hat) + ADAM_EPS) + ADAM_WD * w)
    return delta, m, v


def _adamw_shard(g_own, recv, w, m, v, *, rows, name):
    R, C = g_own.shape

    def body(g_ref, r_ref, w_ref, m_ref, v_ref, go_ref, d_ref, mo_ref, vo_ref):
        g = g_ref[...]
        for k in range(N_DEV - 1):
            g = g + r_ref[k].astype(F32)
        d, mn, vn = _adamw_math(w_ref[...], g, m_ref[...], v_ref[...])
        go_ref[...] = g
        d_ref[...] = d
        mo_ref[...] = mn
        vo_ref[...] = vn

    blk = pl.BlockSpec((rows, C), lambda i: (i, 0))
    return pl.pallas_call(
        body, name=name, grid=(R // rows,),
        in_specs=[blk, pl.BlockSpec((N_DEV - 1, rows, C), lambda i: (0, i, 0)), blk, blk, blk],
        out_specs=[blk] * 4, out_shape=[jax.ShapeDtypeStruct((R, C), F32)] * 4,
        compiler_params=_params(("arbitrary",)),
    )(g_own, recv, w, m, v)


def _minor_rows_view(a):
    return jnp.transpose(a, (2, 0, 1)).reshape(SHARD_COLS * 8, 128)


def _from_minor_rows_view(v):
    return jnp.transpose(v.reshape(SHARD_COLS, 8, 128), (1, 2, 0)).reshape(1, D_MODEL, SHARD_COLS)


def _adamw_w_in(is_lo, own_lo, own_hi, recv_lo, recv_hi, w, m, v):
    C = SHARD_COLS
    pad = -C % 128

    def body(lo_ref, ol_ref, oh_ref, rl_ref, rh_ref, w_ref, m_ref, v_ref, go_ref, d_ref, mo_ref, vo_ref):
        lo = lo_ref[0] == 1
        for q in range(D_MODEL // 128):
            band = pl.ds(q * 128, 128)
            g = jnp.where(lo, ol_ref[band, :], oh_ref[band, :])
            for k in range(2):
                g = g + jnp.where(lo, rl_ref[k, band, :], rh_ref[k, band, :]).astype(F32)
            g = jnp.pad(g, ((0, 0), (0, pad))).T[0:C]
            rows = pl.ds(q, C, stride=8)
            d, mn, vn = _adamw_math(w_ref[rows, :], g, m_ref[rows, :], v_ref[rows, :])
            go_ref[rows, :] = g
            d_ref[rows, :] = d
            mo_ref[rows, :] = mn
            vo_ref[rows, :] = vn

    return pl.pallas_call(
        body, name="adamw_w_in", out_shape=[jax.ShapeDtypeStruct(w.shape, F32)] * 4,
        in_specs=[pl.BlockSpec(memory_space=pltpu.SMEM)] + [pl.BlockSpec(memory_space=pltpu.VMEM)] * 7,
        out_specs=[pl.BlockSpec(memory_space=pltpu.VMEM)] * 4,
        compiler_params=_params(),
    )(is_lo, own_lo, own_hi, recv_lo, recv_hi, w, m, v)


SMALL = ("conv_b", "dt_bias", "a_log", "d_skip", "ssd_norm_w", "attn_sinks", "ln_g", "ln_b")


def _adamw_small(gathered, params):
    n_p = len(SMALL)

    def body(*refs):
        acc = []
        for r in refs[:5]:
            t = r[0]
            for k in range(1, N_DEV):
                t = t + r[k]
            acc.append(t)
        head, conv, norm, scal, sink = acc
        grads = dict(conv_b=conv[4:5, :], dt_bias=scal[0:1, 0:N_HEADS], a_log=scal[1:2, 0:N_HEADS],
                     d_skip=scal[2:3, 0:N_HEADS], ssd_norm_w=norm[0:1, :], attn_sinks=sink[0:1, 0:N_HEADS],
                     ln_g=head[0:1, :], ln_b=head[1:2, :])
        wmv = refs[5:5 + 3 * n_p]
        outs = refs[5 + 3 * n_p:]
        outs[0][...] = head[3:4, 0:1]
        outs[1][...] = conv[0:4, :]
        for i, name in enumerate(SMALL):
            w_ref, m_ref, v_ref = wmv[3 * i:3 * i + 3]
            g = grads[name]
            d, mn, vn = _adamw_math(w_ref[...], g, m_ref[...], v_ref[...])
            for o_ref, val in zip(outs[2 + 4 * i:6 + 4 * i], (g, d, mn, vn)):
                o_ref[...] = val

    flat = [a for name in SMALL for a in params[name]]
    out_shape = [jax.ShapeDtypeStruct((1, 1), F32), jax.ShapeDtypeStruct((4, D_XBC), F32)]
    for name in SMALL:
        out_shape += [jax.ShapeDtypeStruct(params[name][0].shape, F32)] * 4
    res = pl.pallas_call(body, name="adamw_small", out_shape=out_shape, compiler_params=_params())(*gathered, *flat)
    return res[0], res[1], {name: res[2 + 4 * i:6 + 4 * i] for i, name in enumerate(SMALL)}


def _adamw_plain(g, w, m, v):
    def body(g_ref, w_ref, m_ref, v_ref, d_ref, mo_ref, vo_ref):
        d, mn, vn = _adamw_math(w_ref[...], g_ref[...], m_ref[...], v_ref[...])
        d_ref[...] = d
        mo_ref[...] = mn
        vo_ref[...] = vn

    return pl.pallas_call(
        body, name="adamw_conv_w", out_shape=[jax.ShapeDtypeStruct(w.shape, F32)] * 3,
        compiler_params=_params(),
    )(g, w, m, v)


def _lane_pattern(fn):
    return np.asarray([fn(l % HEAD_DIM) for l in range(128)], np.float32)


ROPE_INV = _lane_pattern(lambda r: ROPE_THETA ** (-2.0 * (r % 8) / ROPE_DIM) if r < ROPE_DIM else 0.0)


def _rope_tables(pos_ref, inv_ref, tab_ref):
    lane = lax.broadcasted_iota(jnp.int32, (1, 128), 1) % HEAD_DIM
    upper = jnp.where((lane >= ROPE_DIM // 2) & (lane < ROPE_DIM), 1.0, 0.0)
    lower = jnp.where(lane < ROPE_DIM // 2, -1.0, 0.0)

    def block(r, carry):
        rows = pl.ds(pl.multiple_of(r * CHUNK, CHUNK), CHUNK)
        pos = jnp.broadcast_to(pos_ref[pl.ds(r, 1), :].astype(F32), (CHUNK, 128)).T
        ang = pos * inv_ref[...]
        sn = jnp.sin(ang)
        tab_ref[rows, 0:128] = jnp.cos(ang)
        tab_ref[rows, 128:256] = sn * upper
        tab_ref[rows, 256:384] = sn * lower
        return carry

    lax.fori_loop(0, pos_ref.shape[0], block, 0)


def _expansion():
    E = np.arange(1024)[None, :] // HEAD_DIM == np.arange(128)[:, None]
    return jnp.asarray(E, BF16), jnp.asarray(E.T, BF16)


def _ssd_args(conv_w, conv_b, dt_bias, a_log, d_skip, norm_w, E):
    return (conv_w, conv_b, dt_bias.reshape(-1), a_log.reshape(-1), d_skip.reshape(-1), norm_w, E)


def kernel(x, positions, w_in, conv_w, conv_b, dt_bias, a_log, d_skip, ssd_norm_w, attn_sinks, w_out, ln_g, ln_b, loss_target, m_w_in, m_conv_w, m_conv_b, m_dt_bias, m_a_log, m_d_skip, m_ssd_norm_w, m_attn_sinks, m_w_out, m_ln_g, m_ln_b, v_w_in, v_conv_w, v_conv_b, v_dt_bias, v_a_log, v_d_skip, v_ssd_norm_w, v_attn_sinks, v_w_out, v_ln_g, v_ln_b):
    me = _index(*_position())
    x0, target = x[0], loss_target[0]
    bf16_shard = lambda shape: jax.ShapeDtypeStruct(shape, BF16)
    E, ET = _expansion()
    sinks = attn_sinks.reshape(-1)

    w_all, tabs = _gather_w_in(w_in[0].astype(BF16), positions[0].reshape(-1, 128))
    w_ssd, w_att = _unpack_w_in(w_all)
    gather_conv_w = _Hosted([conv_w[0]], [jax.ShapeDtypeStruct((N_DEV,) + conv_w.shape[1:], F32)],
                            [_Flow("gather", 0, 0)])

    proj_ssd, proj_att, xb, conv_w_all = _in_proj(x0, w_ssd, w_att, tm=512, comm=gather_conv_w)
    conv_w_f = jnp.transpose(conv_w_all, (1, 0, 2)).reshape(4, D_XBC)
    ssd_args = _ssd_args(conv_w_f, conv_b, dt_bias, a_log, d_skip, ssd_norm_w, E)
    gather_w_out = _Hosted([w_out[0].astype(BF16)], [bf16_shard((N_DEV, 256, D_MODEL))], [_Flow("gather", 0, 0)])
    y, ypre, hprev, pre, w_out_all = _mixer_forward(proj_ssd, proj_att, tabs, sinks, *ssd_args, comm=gather_w_out)
    w_out_f = w_out_all.reshape(2 * D_MODEL, D_MODEL)
    dr, dy, acc_head = _head(y, x0, target, w_out_f, ln_g, ln_b, tm=512)

    dw_out_bf16, own_out = _dw_out(me.reshape(1).astype(jnp.int32), y, dr, tl=1024)
    send_out = _Hosted([dw_out_bf16.reshape(N_DEV, 256, D_MODEL)], [bf16_shard((N_DEV - 1, 256, D_MODEL))],
                       [_Flow("exchange", 0, 0)])
    d_ssd, acc_cw, acc_w, acc_s, recv_out = _ssd_backward(proj_ssd, hprev, ypre, pre, dy, *ssd_args, ET, comm=send_out)
    stack_lo, dw_dt_block = _dw_in(xb, d_ssd, 0)
    d_att, dsink, own_lo, recv_lo = _swa_backward(proj_att, tabs, sinks, dy, reduce=_OwnerReduce(stack_lo, 0))
    (stack_hi,) = _dw_in(xb, d_att, 1, dw_dt_block)
    accs = [acc_head, acc_cw, acc_w, acc_s, dsink]
    gather_accs = _Hosted(accs, [jax.ShapeDtypeStruct((N_DEV,) + a.shape, F32) for a in accs],
                          [_Flow("gather", i, i) for i in range(5)])
    dx, *gathered, own_hi, recv_hi = _input_gradient(d_ssd, d_att, w_ssd, w_att, dr, tm=256, comm=gather_accs,
                                                     reduce=_OwnerReduce(stack_hi, 1))
    is_lo = (me < 4).reshape(1).astype(jnp.int32)

    g_in, d_in, nm_in, nv_in = [_from_minor_rows_view(r) for r in _adamw_w_in(
        is_lo, own_lo, own_hi, recv_lo, recv_hi, _minor_rows_view(w_in), _minor_rows_view(m_w_in), _minor_rows_view(v_w_in))]
    g_out, d_out, nm_out, nv_out = _adamw_shard(own_out, recv_out, w_out[0], m_w_out[0], v_w_out[0],
                                                rows=256, name="adamw_w_out")
    loss, g_conv_w, small = _adamw_small(gathered, dict(
        conv_b=(conv_b, m_conv_b, v_conv_b), dt_bias=(dt_bias, m_dt_bias, v_dt_bias), a_log=(a_log, m_a_log, v_a_log),
        d_skip=(d_skip, m_d_skip, v_d_skip), ssd_norm_w=(ssd_norm_w, m_ssd_norm_w, v_ssd_norm_w),
        attn_sinks=(attn_sinks, m_attn_sinks, v_attn_sinks), ln_g=(ln_g, m_ln_g, v_ln_g), ln_b=(ln_b, m_ln_b, v_ln_b)))
    g_cw = lax.dynamic_slice_in_dim(g_conv_w, me * (D_XBC // N_DEV), D_XBC // N_DEV, axis=1)
    d_cw, nm_cw, nv_cw = _adamw_plain(g_cw, conv_w[0], m_conv_w[0], v_conv_w[0])

    def leaves(i, big_in, cw, big_out):
        mid = [small[k][i] for k in ("conv_b", "dt_bias", "a_log", "d_skip", "ssd_norm_w", "attn_sinks")]
        return [big_in, cw[None]] + mid + [big_out[None], small["ln_g"][i], small["ln_b"][i]]

    return (loss.reshape(()), dx[None], *leaves(0, g_in, g_cw, g_out), *leaves(1, d_in, d_cw, d_out),
            *leaves(2, nm_in, nm_cw, nm_out), *leaves(3, nv_in, nv_cw, nv_out))
```

```python
import jax
import jax.numpy as jnp
from jax import lax
from jax.experimental import pallas as pl
from jax.experimental.pallas import tpu as pltpu
import numpy as np

F32 = jnp.float32
BF16 = jnp.bfloat16
_MXU = jnp.bfloat16

N_DEV = 8
D_MODEL = 1024
D_SSD = 1024
D_ATT = 1024
HEAD_DIM = 64
N_HEADS = 16
SSD_GROUPS = 2
KV_HEADS = 4
CHUNK = 128
D_XBC = 1536
D_IN_PROJ = 5136
ROPE_DIM = 16
ROPE_THETA = 500000.0
ALPHA = (2.0 * 1) ** 0.25
LN_EPS = 1e-5
RMS_EPS = 1e-5
ATT_SCALE = HEAD_DIM ** -0.5
NEG = -1e30

S_Z, S_XS, S_B, S_C, S_DT, S_W = 0, 1024, 2048, 2304, 2560, 2816
N_SSD_REAL = 2576
A_Q, A_K, A_V, A_G, A_W = 0, 1024, 1280, 1536, 2560

ADAM_LR = 0.001
ADAM_B1 = 0.9
ADAM_B2 = 0.999
ADAM_EPS = 1e-08
ADAM_WD = 0.01
ADAM_STEP = 10

VMEM_LIMIT = 48 * 1024 * 1024
MESH = pl.DeviceIdType.MESH


def _params(sem=None):
    return pltpu.CompilerParams(dimension_semantics=sem, vmem_limit_bytes=VMEM_LIMIT)


def _mm(a, b):
    return jnp.dot(a.astype(_MXU), b.astype(_MXU), preferred_element_type=F32)


def _mm_nt(a, b):
    return lax.dot_general(a.astype(_MXU), b.astype(_MXU), (((1,), (1,)), ((), ())),
                           preferred_element_type=F32)


def _mm_tn(a, b):
    return lax.dot_general(a.astype(_MXU), b.astype(_MXU), (((0,), (0,)), ((), ())),
                           preferred_element_type=F32)


def _split3(v):
    hi = v.astype(BF16)
    r = v - hi.astype(F32)
    mid = r.astype(BF16)
    lo = (r - mid.astype(F32)).astype(BF16)
    return hi, mid, lo


def _mm_exact_r(v, p01):
    hi, mid, lo = _split3(v)
    d = lambda a: jnp.dot(a, p01, preferred_element_type=F32)
    return d(hi) + d(mid) + d(lo)


def _mm_exact_l(p01, v):
    hi, mid, lo = _split3(v)
    d = lambda a: jnp.dot(p01, a, preferred_element_type=F32)
    return d(hi) + d(mid) + d(lo)


def _mm_2pass_r(v, p01):
    hi = v.astype(BF16)
    lo = (v - hi.astype(F32)).astype(BF16)
    return jnp.dot(hi, p01, preferred_element_type=F32) + jnp.dot(lo, p01, preferred_element_type=F32)


def _sigmoid(x):
    return 1.0 / (1.0 + jnp.exp(-x))


def _softplus(x):
    e = jnp.exp(-jnp.abs(x))
    u = 1.0 + e
    log1p = jnp.where(u == 1.0, e, jnp.log(u) * (e / (u - 1.0)))
    return jnp.maximum(x, 0.0) + log1p


def _rows8(rows):
    n = rows[0].shape[1]
    rid = lax.broadcasted_iota(jnp.int32, (8, n), 0)
    out = jnp.zeros((8, n), F32)
    for k, r in enumerate(rows):
        out = out + jnp.where(rid == k, r, 0.0)
    return out


def _colsum(a):
    return jnp.sum(a, axis=0, keepdims=True)


def _in_proj(x, w_ssd, w_att, *, tm, comm=None):
    L, K = x.shape

    def body(x_ref, ws_ref, wa_ref, ps_ref, pa_ref, xb_ref):
        xb = x_ref[...].astype(_MXU)
        xb_ref[...] = xb
        ps_ref[...] = jnp.dot(xb, ws_ref[...], preferred_element_type=F32)
        pa_ref[...] = jnp.dot(xb, wa_ref[...], preferred_element_type=F32)

    row = lambda w: pl.BlockSpec((tm, w), lambda i: (i, 0))
    resident = lambda a: pl.BlockSpec(a.shape, lambda i: (0, 0), pipeline_mode=pl.Buffered(1))
    return _call(
        body, comm, name="in_proj", grid=(L // tm,),
        in_specs=[row(K), resident(w_ssd), resident(w_att)], out_specs=[row(S_W), row(A_W), row(K)],
        out_shape=[jax.ShapeDtypeStruct((L, S_W), F32), jax.ShapeDtypeStruct((L, A_W), F32),
                   jax.ShapeDtypeStruct((L, K), _MXU)],
        scratch_shapes=[], args=(x, w_ssd, w_att))


def _matmul_tn(a, g, *, tl, tn, name, emit_bf16=False):
    L, M = a.shape
    N = g.shape[1]
    last = L // tl - 1

    def body(a_ref, g_ref, o_ref, *rest):
        @pl.when(pl.program_id(1) == 0)
        def _():
            o_ref[...] = jnp.zeros_like(o_ref)

        o_ref[...] += _mm_tn(a_ref[...], g_ref[...])
        if emit_bf16:
            @pl.when(pl.program_id(1) == last)
            def _():
                rest[0][...] = o_ref[...].astype(BF16)

    spec = pl.BlockSpec((M, tn), lambda j, l: (0, j))
    res = pl.pallas_call(
        body, name=name, grid=(N // tn, L // tl),
        in_specs=[pl.BlockSpec((tl, M), lambda j, l: (l, 0)), pl.BlockSpec((tl, tn), lambda j, l: (l, j))],
        out_specs=[spec, spec] if emit_bf16 else [spec],
        out_shape=[jax.ShapeDtypeStruct((M, N), F32)] + ([jax.ShapeDtypeStruct((M, N), BF16)] if emit_bf16 else []),
        compiler_params=_params(("arbitrary", "arbitrary")),
    )(a, g)
    return res if emit_bf16 else res[0]


def _position():
    return lax.axis_index("x"), lax.axis_index("y"), lax.axis_index("c")


def _index(px, py, pc):
    return 4 * px + 2 * py + pc


def _flip(pos, k):
    x, y, c = pos
    return ((1 - x) if (k >> 2) & 1 else x, (1 - y) if (k >> 1) & 1 else y, (1 - c) if k & 1 else c)


def _remote(src, dst, send_sem, recv_sem, peer):
    return pltpu.make_async_remote_copy(src_ref=src, dst_ref=dst, send_sem=send_sem, recv_sem=recv_sem,
                                        device_id=peer, device_id_type=MESH)


class _Flow:
    def __init__(self, kind, operand, result):
        self.kind, self.operand, self.result = kind, operand, result


class _Hosted:
    def __init__(self, operands, out_shapes, flows):
        self.operands, self.out_shapes, self.flows = operands, out_shapes, flows

    def plan(self, ins, outs, send_sems, recv_sems, local_sems):
        me = _position()
        mi = _index(*me)
        sends, recvs, locals_ = [], [], []
        for row, f in enumerate(self.flows):
            src, dst = ins[f.operand], outs[f.result]
            for k in range(1, N_DEV):
                peer = _flip(me, k)
                sems = (send_sems.at[row, k - 1], recv_sems.at[row, k - 1])
                if f.kind == "exchange":
                    sends.append(_remote(src.at[_index(*peer)], dst.at[k - 1], *sems, peer))
                    recvs.append(sends[-1])
                else:
                    sends.append(_remote(src, dst.at[mi], *sems, peer))
                    recvs.append(_remote(src, dst.at[_index(*peer)], *sems, peer))
            if f.kind == "gather":
                locals_.append(pltpu.make_async_copy(src, dst.at[mi], local_sems.at[row]))

        def start():
            for cp in locals_ + sends:
                cp.start()

        def wait():
            for cp in recvs:
                cp.wait_recv()
            for cp in sends:
                cp.wait_send()
            for cp in locals_:
                cp.wait()

        return start, wait


class _OwnerReduce:
    FIRST_STEP, SECOND_STEP, SEND_STEPS, REDUCE_STEPS = 2, 4, (2, 4, 6, 8), (5, 8, 10, 13)

    def __init__(self, stack, target_x):
        self.stack, self.target_x = stack, target_x
        block = stack.shape[1:]
        self.chunks = len(self.SEND_STEPS)
        self.chunk_rows = block[0] // self.chunks
        self.out_shapes = [jax.ShapeDtypeStruct(block, F32), jax.ShapeDtypeStruct((2,) + block, BF16)]
        self.out_specs = [pl.BlockSpec(block, lambda *_: (0, 0), pipeline_mode=pl.Buffered(1)),
                          pl.BlockSpec(memory_space=pl.ANY)]
        dma = pltpu.SemaphoreType.DMA
        self.scratch_shapes = ([pltpu.VMEM((2,) + block, F32)] * 2 + [pltpu.VMEM(block, BF16)] * 3
                               + [dma((self.chunks,))] * 4 + [dma((2,))] * 3)

    def plan(self, i, steps, stack_ref, own_ref, recv_ref, scratch):
        assert self.FIRST_STEP <= self.SEND_STEPS[0] and self.SECOND_STEP < self.REDUCE_STEPS[0] < steps - 1
        (theirs_scr, mine_scr, first_scr, across_scr, out_scr, y_send_sems, y_recv_sems, x_send_sems, x_recv_sems,
         swap_send_sems, swap_recv_sems, mine_sems) = scratch
        x, y, c = _position()
        owners_side = x == self.target_x
        other_side = x != self.target_x
        sibling, across, owner = (x, y, 1 - c), (x, 1 - y, c), (self.target_x, y, c)
        order = (1 - y, y)
        swaps = [_remote(stack_ref.at[2 * order[j] + (1 - c)], theirs_scr.at[j], swap_send_sems.at[j], swap_recv_sems.at[j],
                         sibling) for j in range(2)]
        mine = [pltpu.make_async_copy(stack_ref.at[2 * order[j] + c], mine_scr.at[j], mine_sems.at[j]) for j in range(2)]
        chunks = range(self.chunks)
        part = [pl.ds(j * self.chunk_rows, self.chunk_rows) for j in chunks]
        y_sems = lambda j: (y_send_sems.at[j], y_recv_sems.at[j])
        to_neighbour = [_remote(first_scr.at[part[j]], across_scr.at[part[j]], *y_sems(j), across) for j in chunks]
        to_owner_y = [_remote(first_scr.at[part[j]], recv_ref.at[0, part[j]], *y_sems(j), across) for j in chunks]
        to_owner_x = [_remote(out_scr.at[part[j]], recv_ref.at[1, part[j]], x_send_sems.at[j], x_recv_sems.at[j], owner)
                      for j in chunks]

        def before():
            @pl.when(i <= 1)
            def _():
                @pl.when(i == 0)
                def _():
                    for cp in [swaps[0]] + mine:
                        cp.start()

                pl.when(i == 1)(swaps[1].start)

        def after():
            @pl.when(i <= self.REDUCE_STEPS[-1])
            def _():
                @pl.when(i == self.FIRST_STEP)
                def _():
                    swaps[0].wait_recv()
                    mine[0].wait()
                    first_scr[...] = (mine_scr[0] + theirs_scr[0]).astype(first_scr.dtype)

                for j in chunks:
                    pl.when((i == self.SEND_STEPS[j]) & other_side)(to_neighbour[j].start)
                    pl.when((i == self.SEND_STEPS[j]) & owners_side)(to_owner_y[j].start)

                @pl.when(i == self.SECOND_STEP)
                def _():
                    swaps[1].wait_recv()
                    mine[1].wait()
                    t = mine_scr[1] + theirs_scr[1]
                    own_ref[...] = t
                    mine_scr[1] = t

                for j in chunks:
                    @pl.when((i == self.REDUCE_STEPS[j]) & other_side)
                    def _(j=j):
                        to_neighbour[j].wait_recv()
                        t = mine_scr[1, part[j], :] + across_scr[part[j], :].astype(F32)
                        out_scr[part[j], :] = t.astype(out_scr.dtype)
                        to_owner_x[j].start()

            @pl.when(i == steps - 1)
            def _():
                for cp in swaps:
                    cp.wait_send()
                for j in chunks:
                    @pl.when(other_side)
                    def _(j=j):
                        to_neighbour[j].wait_send()
                        to_owner_x[j].wait_send()

                    @pl.when(owners_side)
                    def _(j=j):
                        to_owner_y[j].wait_send()
                        to_owner_y[j].wait_recv()
                        to_owner_x[j].wait_recv()

        return before, after


def _call(body, comm, *, name, grid, in_specs, out_specs, out_shape, scratch_shapes, args, reduce=None):
    semantics = ("arbitrary",) * len(grid)
    if comm is None and reduce is None:
        return pl.pallas_call(body, name=name, grid=grid, in_specs=in_specs, out_specs=out_specs, out_shape=out_shape,
                              scratch_shapes=scratch_shapes, compiler_params=_params(semantics))(*args)
    n_in, n_out, n_scr = len(args), len(out_shape), len(scratch_shapes)
    c_operands, c_shapes, flows = (comm.operands, comm.out_shapes, comm.flows) if comm else ([], [], [])
    c_in, c_out, rows = len(c_operands), len(c_shapes), max(len(flows), 1)
    r_in = 0 if reduce is None else 1

    def hosted(*refs):
        ins, refs = refs[:n_in], refs[n_in:]
        cins, refs = refs[:c_in], refs[c_in:]
        rins, refs = refs[:r_in], refs[r_in:]
        outs, refs = refs[:n_out], refs[n_out:]
        couts, refs = refs[:c_out], refs[c_out:]
        routs, refs = refs[:2 * r_in], refs[2 * r_in:]
        scr, refs = refs[:n_scr], refs[n_scr:]
        (send_sems, recv_sems, local_sems), r_scr = refs[:3], refs[3:]
        ids = [pl.program_id(d) for d in range(len(grid))]
        first, last = ids[0] == 0, ids[0] == grid[0] - 1
        for d in range(1, len(grid)):
            first, last = first & (ids[d] == 0), last & (ids[d] == grid[d] - 1)
        before = after = lambda: None
        if reduce is not None:
            before, after = reduce.plan(ids[0], grid[0], rins[0], *routs, r_scr)
        if comm is not None:
            start, wait = comm.plan(cins, couts, send_sems, recv_sems, local_sems)
            pl.when(first)(start)
        before()
        body(*ins, *outs, *scr)
        after()
        if comm is not None:
            pl.when(last)(wait)

    any_spec = pl.BlockSpec(memory_space=pl.ANY)
    sems = [pltpu.SemaphoreType.DMA((rows, N_DEV - 1)), pltpu.SemaphoreType.DMA((rows, N_DEV - 1)),
            pltpu.SemaphoreType.DMA((rows,))]
    r_operands, r_specs, r_shapes, r_scratch = ([reduce.stack], reduce.out_specs, reduce.out_shapes,
                                                reduce.scratch_shapes) if reduce else ([], [], [], [])
    return pl.pallas_call(
        hosted, name=name, grid=grid, in_specs=list(in_specs) + [any_spec] * (c_in + r_in),
        out_specs=list(out_specs) + [any_spec] * c_out + r_specs, out_shape=list(out_shape) + list(c_shapes) + r_shapes,
        scratch_shapes=list(scratch_shapes) + sems + r_scratch,
        compiler_params=_params(semantics))(*args, *c_operands, *r_operands)


def _head_row(ref, width, rep):
    hid = lax.broadcasted_iota(jnp.int32, (1, width), 1) // rep
    row = jnp.zeros((1, width), F32)
    for h in range(N_HEADS):
        row = jnp.where(hid == h, ref[h], row)
    return row


def _rows_from_above(u_b, s, ext_scr, row, col):
    down = (row - col == s).astype(_MXU)
    return jnp.concatenate([ext_scr[8 - s:16 - s, :], jnp.dot(down, u_b, preferred_element_type=F32)[8:128]], axis=0)


def _ssd_recompute(first, p_ref, halo_ref, cw_ref, cb_ref, dtb_ref, alog_ref, e_ref, ext_scr, pre=None):
    row = lax.broadcasted_iota(jnp.int32, (128, 128), 0)
    col = lax.broadcasted_iota(jnp.int32, (128, 128), 1)
    ext_scr[0:8, :] = jnp.where(first, 0.0, halo_ref[:, S_XS:S_DT])
    if pre is not None:
        ext_scr[8:16, :] = p_ref[0:8, S_XS:S_DT]
    else:
        ext_scr[8:136, :] = p_ref[:, S_XS:S_DT]
        cw = cw_ref[...]
        pre = (cb_ref[0:1, :] + cw[3:4, :] * ext_scr[8:136, :] + cw[2:3, :] * ext_scr[7:135, :]
               + cw[1:2, :] * ext_scr[6:134, :] + cw[0:1, :] * ext_scr[5:133, :])
    sg = _sigmoid(pre)
    act = pre * sg
    lane = lax.broadcasted_iota(jnp.int32, (1, 128), 1)
    A = jnp.where(lane < N_HEADS, -jnp.exp(_head_row(alog_ref, 128, 1)), 0.0)
    raw = p_ref[:, S_DT:S_DT + 128] + _head_row(dtb_ref, 128, 1)
    dt = _softplus(raw)
    dA = dt * A
    tril = (row >= col).astype(BF16)
    acs = _mm_exact_l(tril, dA)
    last = acs[127:128, :]
    ds = jnp.exp(last - acs)
    eo = jnp.exp(acs)
    E = e_ref[...]
    ex = _mm_2pass_r(jnp.concatenate([dt, ds, eo], axis=0), E)
    dt_e, ds_e, eo_e = ex[0:128], ex[128:256], ex[256:384]
    xs_c = act[:, 0:1024]
    X = xs_c * dt_e
    return dict(pre=pre, sg=sg, xs_c=xs_c, Bc=act[:, 1024:1280], Cc=act[:, 1280:1536], A=A, raw=raw, dt=dt,
                acs=acs, acsT=acs.T, eo_e=eo_e, ds_e=ds_e, dt_e=dt_e, cd_e=eo_e[127:128, :],
                X=X, Xd=X * ds_e, row=row, col=col)


def _split_halves(t):
    lo = _lo_half(CHUNK)
    return jnp.concatenate([jnp.where(lo, t, 0.0), jnp.where(lo, 0.0, t)], axis=0)


def _ssd_core(R, hprev):
    causal = R["row"] >= R["col"]
    acs, acsT, X = R["acs"], R["acsT"], R["X"]
    ydiag, yoff, snew = [], [], []
    for g in range(SSD_GROUPS):
        Bg = R["Bc"][:, g * 128:(g + 1) * 128]
        Cg = R["Cc"][:, g * 128:(g + 1) * 128]
        cols = slice(g * 512, (g + 1) * 512)
        CB = _mm_nt(Cg, Bg)
        snew.append(_mm_tn(Bg, R["Xd"][:, cols]))
        yoff.append(_mm(Cg, hprev[:, cols]))
        for j in range(4):
            h0 = g * 8 + 2 * j
            ms = [CB * jnp.exp(jnp.where(causal, acs[:, h:h + 1] - acsT[h:h + 1, :], NEG)) for h in (h0, h0 + 1)]
            ydiag.append(_mm(jnp.concatenate(ms, axis=1), _split_halves(X[:, h0 * HEAD_DIM:h0 * HEAD_DIM + 128])))
    Y = jnp.concatenate(ydiag, axis=1) + jnp.concatenate(yoff, axis=1) * R["eo_e"]
    return Y, jnp.concatenate(snew, axis=1)


def _ssd_forward_step(p_ref, halo_ref, cw_ref, cb_ref, dtb_ref, alog_ref, dsk_ref, nw_ref, e_ref,
                      y_ref, ypre_ref, hprev_ref, pre_ref, h_scr, ext_scr):
    c = pl.program_id(0)
    first = c == 0

    @pl.when(first)
    def _():
        h_scr[...] = jnp.zeros_like(h_scr)

    R = _ssd_recompute(first, p_ref, halo_ref, cw_ref, cb_ref, dtb_ref, alog_ref, e_ref, ext_scr)
    hprev = h_scr[...]
    hprev_ref[...] = hprev
    pre_ref[...] = R["pre"]
    Y, snew = _ssd_core(R, hprev)
    h_scr[...] = hprev * R["cd_e"] + snew
    Y = Y + _head_row(dsk_ref, D_SSD, HEAD_DIM) * R["xs_c"]
    ypre_ref[...] = Y
    z = p_ref[:, S_Z:S_Z + 1024]
    yf = Y * (z * _sigmoid(z))
    outs = []
    for g in range(SSD_GROUPS):
        yg = yf[:, g * 512:(g + 1) * 512]
        r = lax.rsqrt(jnp.mean(yg * yg, axis=-1, keepdims=True) + RMS_EPS)
        outs.append(yg * r)
    y_ref[:, 0:D_SSD] = (jnp.concatenate(outs, axis=1) * nw_ref[0:1, :]).astype(y_ref.dtype)


def _ssd_backward(proj_ssd, hprev_all, ypre, pre, dy, conv_w, conv_b, dt_bias, a_log, d_skip, norm_w, E, ET, comm=None):
    L = proj_ssd.shape[0]
    nc = L // CHUNK

    def body(p_ref, halo_ref, hprev_ref, ypre_ref, pre_ref, dy_ref, cw_ref, cb_ref, dtb_ref, alog_ref, dsk_ref, nw_ref, e_ref,
             et_ref, dp_ref, acc_cw_ref, acc_w_ref, acc_s_ref, dh_scr, ext_scr, ext2_scr, nxt_scr):
        i = pl.program_id(0)
        c = nc - 1 - i
        first = c == 0

        @pl.when(i == 0)
        def _():
            dh_scr[...] = jnp.zeros_like(dh_scr)
            nxt_scr[...] = jnp.zeros_like(nxt_scr)
            acc_cw_ref[...] = jnp.zeros_like(acc_cw_ref)
            acc_w_ref[...] = jnp.zeros_like(acc_w_ref)
            acc_s_ref[...] = jnp.zeros_like(acc_s_ref)

        R = _ssd_recompute(first, p_ref, halo_ref, cw_ref, cb_ref, dtb_ref, alog_ref, e_ref, ext_scr, pre_ref[...])
        hprev = hprev_ref[...]
        xs_c, X, Xd = R["xs_c"], R["X"], R["Xd"]
        acs, acsT = R["acs"], R["acsT"]
        ET = et_ref[...]
        dsk = _head_row(dsk_ref, D_SSD, HEAD_DIM)
        Y = ypre_ref[...]

        z = p_ref[:, S_Z:S_Z + 1024]
        sz = _sigmoid(z)
        silz = z * sz
        yf = Y * silz
        dyv = dy_ref[...]
        nw = nw_ref[0:1, :]
        dyf_parts, dnw_parts = [], []
        for g in range(SSD_GROUPS):
            cols = slice(g * 512, (g + 1) * 512)
            yg = yf[:, cols]
            r = lax.rsqrt(jnp.mean(yg * yg, axis=-1, keepdims=True) + RMS_EPS)
            yn = yg * r
            dyn = dyv[:, cols] * nw[:, cols]
            dnw_parts.append(_colsum(dyv[:, cols] * yn))
            dyf_parts.append(r * (dyn - yn * jnp.mean(dyn * yn, axis=-1, keepdims=True)))
        dyf = jnp.concatenate(dyf_parts, axis=1)
        dY = dyf * silz
        dz = dyf * Y * (sz * (1.0 + z * (1.0 - sz)))

        dhn = dh_scr[...]
        dYo = dY * R["eo_e"]
        causal = R["row"] >= R["col"]
        dacs = jnp.zeros((128, 128), F32)
        dacs_t = jnp.zeros((128, 128), F32)
        dxdiag, dxd, dhprev, dBs, dCs, yoff = [], [], [], [], [], []
        for g in range(SSD_GROUPS):
            Bg = R["Bc"][:, g * 128:(g + 1) * 128]
            Cg = R["Cc"][:, g * 128:(g + 1) * 128]
            cols = slice(g * 512, (g + 1) * 512)
            CB = _mm_nt(Cg, Bg)
            dCB = jnp.zeros((128, 128), F32)
            for j in range(4):
                h0 = g * 8 + 2 * j
                pc = slice(h0 * HEAD_DIM, h0 * HEAD_DIM + 128)
                dYst = _split_halves(dY[:, pc])
                dMst = _mm_nt(dYst, X[:, pc])
                mts = []
                for a, h in enumerate((h0, h0 + 1)):
                    acol = acs[:, h:h + 1]
                    arow = acsT[h:h + 1, :]
                    Lm = jnp.exp(jnp.where(causal, acol - arow, NEG))
                    M = CB * Lm
                    dM = dMst[a * 128:(a + 1) * 128]
                    dCB = dCB + dM * Lm
                    G = dM * M
                    dacs = dacs + jnp.where(R["col"] == h, jnp.sum(G, axis=1, keepdims=True), 0.0)
                    dacs_t = dacs_t + jnp.where(R["row"] == h, jnp.sum(G, axis=0, keepdims=True), 0.0)
                    mts.append(M.T)
                dxdiag.append(_mm(jnp.concatenate(mts, axis=1), dYst))
            dS = dhn[:, cols]
            dxd.append(_mm(Bg, dS))
            yoff.append(_mm(Cg, hprev[:, cols]))
            dhprev.append(_mm_tn(Cg, dYo[:, cols]))
            dCs.append(_mm_nt(dYo[:, cols], hprev[:, cols]) + _mm(dCB, Bg))
            dBs.append(_mm_tn(dCB, Cg) + _mm_nt(Xd[:, cols], dS))
        Yoff = jnp.concatenate(yoff, axis=1) * R["eo_e"]
        dXd = jnp.concatenate(dxd, axis=1)
        dX = jnp.concatenate(dxdiag, axis=1) + dXd * R["ds_e"]
        t_state = dXd * Xd
        dacs = dacs + _mm_2pass_r(dY * Yoff - t_state, ET) - dacs_t.T
        v_last = _colsum(t_state + dhn * hprev * R["cd_e"])
        dlast = _mm_exact_r(jnp.broadcast_to(v_last, (8, 1024)), ET)[0:1, :]
        dacs = dacs + jnp.where(R["row"] == 127, dlast, 0.0)
        triu = (R["col"] >= R["row"]).astype(BF16)
        da = _mm_exact_l(triu, dacs)
        ddt = da * R["A"] + _mm(dX * xs_c, ET)
        ddt_raw = ddt * _sigmoid(R["raw"])
        dxs_c = dX * R["dt_e"] + dY * dsk
        dh_scr[...] = jnp.concatenate(dhprev, axis=1) + dhn * R["cd_e"]

        dact = jnp.concatenate([dxs_c] + dBs + dCs, axis=1)
        pre, sg = R["pre"], R["sg"]
        dpre = dact * (sg * (1.0 + pre * (1.0 - sg)))
        ext2_scr[0:8, :] = dpre[120:128, :]
        ext2_scr[8:16, :] = nxt_scr[...]
        nxt_scr[...] = dpre[0:8, :]
        cw = cw_ref[...]
        u_b, dpre_b = p_ref[:, S_XS:S_DT].astype(_MXU), dpre.astype(_MXU)
        dxbc = cw[3:4, :] * dpre
        taps = [_colsum(dpre * p_ref[:, S_XS:S_DT])]
        for s in (1, 2, 3):
            up = (R["col"] - R["row"] == s).astype(_MXU)
            d_s = jnp.concatenate([jnp.dot(up, dpre_b, preferred_element_type=F32)[0:120],
                                   ext2_scr[s:8 + s, :]], axis=0)
            dxbc = dxbc + cw[3 - s:4 - s, :] * d_s
            taps.append(_colsum(dpre * _rows_from_above(u_b, s, ext_scr, R["row"], R["col"])))
        acc_cw_ref[...] += _rows8(taps[::-1] + [_colsum(dpre)])
        acc_w_ref[...] += _rows8([jnp.concatenate(dnw_parts, axis=1), _colsum(dY * xs_c)])
        acc_s_ref[...] += _rows8([_colsum(ddt_raw), _colsum(da * R["dt"])])

        lane = lax.broadcasted_iota(jnp.int32, (128, 128), 1)
        dp_ref[:, S_Z:S_Z + 1024] = dz.astype(dp_ref.dtype)
        dp_ref[:, S_XS:S_DT] = dxbc.astype(dp_ref.dtype)
        dp_ref[:, S_DT:S_DT + 128] = jnp.where(lane < N_HEADS, ddt_raw, 0.0).astype(dp_ref.dtype)
        dp_ref[:, S_DT + 128:S_W] = jnp.zeros((128, 128), dp_ref.dtype)

        @pl.when(i == nc - 1)
        def _():
            acc = acc_s_ref[...]
            dskip = _mm_exact_r(acc_w_ref[...], ET)[1:2, :]
            acc_s_ref[...] = _rows8([acc[0:1, :], acc[1:2, :] * R["A"], dskip])

    const = lambda shape: pl.BlockSpec(shape, lambda i: (0, 0))
    smem = pl.BlockSpec(memory_space=pltpu.SMEM)
    rev = lambda i: (nc - 1 - i, 0)
    return _call(
        body, comm, name="ssd_bwd", grid=(nc,),
        in_specs=[pl.BlockSpec((CHUNK, S_W), rev),
                  pl.BlockSpec((8, S_W), lambda i: (jnp.maximum((nc - 1 - i) * 16 - 1, 0), 0)),
                  pl.BlockSpec((128, 1024), rev),
                  pl.BlockSpec((CHUNK, D_SSD), rev),
                  pl.BlockSpec((CHUNK, D_XBC), rev),
                  pl.BlockSpec((CHUNK, D_SSD), rev),
                  const((4, D_XBC)), const((1, D_XBC)), smem, smem, smem, const((1, 1024)),
                  const((128, 1024)), const((1024, 128))],
        out_specs=[pl.BlockSpec((CHUNK, S_W), rev), const((8, D_XBC)), const((8, 1024)), const((8, 128))],
        out_shape=[jax.ShapeDtypeStruct((L, S_W), _MXU), jax.ShapeDtypeStruct((8, D_XBC), F32),
                   jax.ShapeDtypeStruct((8, 1024), F32), jax.ShapeDtypeStruct((8, 128), F32)],
        scratch_shapes=[pltpu.VMEM((128, 1024), F32), pltpu.VMEM((16, D_XBC), F32),
                        pltpu.VMEM((16, D_XBC), F32), pltpu.VMEM((8, D_XBC), F32)],
        args=(proj_ssd, proj_ssd, hprev_all, ypre, pre, dy, conv_w, conv_b, dt_bias, a_log, d_skip, norm_w, E, ET))


def _rope(t, tab):
    cos, sa, sb = tab[:, 0:128], tab[:, 128:256], tab[:, 256:384]
    outs = []
    for i in range(t.shape[1] // 128):
        tg = t[:, i * 128:(i + 1) * 128]
        outs.append(tg * cos + pltpu.roll(tg, 8, 1) * sa + pltpu.roll(tg, 120, 1) * sb)
    return jnp.concatenate(outs, axis=1)


def _rope_transposed(d, tab):
    cos, sa, sb = tab[:, 0:128], tab[:, 128:256], tab[:, 256:384]
    outs = []
    for i in range(d.shape[1] // 128):
        dg = d[:, i * 128:(i + 1) * 128]
        outs.append(dg * cos + pltpu.roll(dg * sa, 120, 1) + pltpu.roll(dg * sb, 8, 1))
    return jnp.concatenate(outs, axis=1)


def _lo_half(rows):
    return lax.broadcasted_iota(jnp.int32, (rows, 128), 1) < HEAD_DIM


def _native_half(rows, j):
    lo = _lo_half(rows)
    return lo if j % 2 == 0 else jnp.logical_not(lo)


def _kv_native(t, j):
    p = j // 2
    return jnp.where(_native_half(t.shape[0], j), t[:, p * 128:(p + 1) * 128], 0.0)


def _stack_heads(t, j):
    out = []
    for m in (2 * j, 2 * j + 1):
        pair = t[:, m * 128:(m + 1) * 128]
        swapped = pltpu.roll(pair, HEAD_DIM, 1)
        out += [pair, swapped] if j % 2 == 0 else [swapped, pair]
    return jnp.concatenate(out, axis=0)


def _unstack_heads(s, j):
    out = []
    for m in range(2):
        first, second = s[256 * m:256 * m + 128], s[256 * m + 128:256 * m + 256]
        if j % 2 == 0:
            out.append(first + pltpu.roll(second, HEAD_DIM, 1))
        else:
            out.append(pltpu.roll(first, HEAD_DIM, 1) + second)
    return jnp.concatenate(out, axis=1)


def _keep_native(r, j):
    return jnp.where(_native_half(r.shape[0], j), r, 0.0)


def _sink_row(sink_ref, j):
    hid = lax.broadcasted_iota(jnp.int32, (1, 4 * CHUNK), 1) // CHUNK
    row = jnp.zeros((1, 4 * CHUNK), F32)
    for hh in range(4):
        row = jnp.where(hid == hh, sink_ref[4 * j + hh], row)
    return row


def _from_current():
    si = lax.broadcasted_iota(jnp.int32, (CHUNK, 4 * CHUNK), 0)
    qi = lax.broadcasted_iota(jnp.int32, (CHUNK, 4 * CHUNK), 1) % CHUNK
    return si <= qi


def _fold(full, from_cur, pen=0.0):
    return jnp.where(from_cur, full[CHUNK:2 * CHUNK], full[0:CHUNK] + pen)


def _unfold(t, from_cur):
    c = jnp.where(from_cur, t, 0.0)
    return jnp.concatenate([t - c, c], axis=0)


def _softmax_sink(s, sink):
    mx = jnp.maximum(jnp.max(s, axis=0, keepdims=True), sink)
    p = jnp.exp(s - mx)
    esink = jnp.exp(sink - mx)
    inv = 1.0 / (jnp.sum(p, axis=0, keepdims=True) + esink)
    return p * inv, esink * inv


def _swa_inputs(blk, p_ref, prev_ref, tab_ref, ptab_ref):
    tab = tab_ref[...]
    qr = _rope(p_ref[:, A_Q:A_Q + 1024], tab) * ATT_SCALE
    kk = jnp.concatenate([_rope(prev_ref[:, 0:256], ptab_ref[...]), _rope(p_ref[:, A_K:A_K + 256], tab)], axis=0)
    vv = jnp.concatenate([prev_ref[:, 256:512], p_ref[:, A_V:A_V + 256]], axis=0)
    return tab, qr, kk, vv, jnp.where(blk > 0, 0.0, NEG)


def _swa_forward_step(sink_ref, p_ref, prev_ref, tab_ref, ptab_ref, y_ref):
    n = pl.program_id(0)
    _, qr, kk, vv, pen = _swa_inputs(n, p_ref, prev_ref, tab_ref, ptab_ref)
    from_cur = _from_current()
    outs = []
    for j in range(KV_HEADS):
        s = _fold(_mm_nt(_kv_native(kk, j), _stack_heads(qr, j)), from_cur, pen)
        P, _ = _softmax_sink(s, _sink_row(sink_ref, j))
        outs.append(_unstack_heads(_mm_tn(_unfold(P, from_cur), _kv_native(vv, j)), j))
    g = p_ref[:, A_G:A_G + 1024]
    y_ref[:, D_SSD:D_SSD + D_ATT] = (jnp.concatenate(outs, axis=1) * (g * _sigmoid(g))).astype(y_ref.dtype)


def _mixer_forward(proj_ssd, proj_att, tabs, sinks, conv_w, conv_b, dt_bias, a_log, d_skip, norm_w, E, comm=None):
    L = proj_ssd.shape[0]
    nc = L // CHUNK

    def body(p_ref, halo_ref, cw_ref, cb_ref, dtb_ref, alog_ref, dsk_ref, nw_ref, e_ref,
             sink_ref, pa_ref, prev_ref, tab_ref, ptab_ref, y_ref, ypre_ref, hprev_ref, pre_ref, h_scr, ext_scr):
        _ssd_forward_step(p_ref, halo_ref, cw_ref, cb_ref, dtb_ref, alog_ref, dsk_ref, nw_ref, e_ref,
                          y_ref, ypre_ref, hprev_ref, pre_ref, h_scr, ext_scr)
        _swa_forward_step(sink_ref, pa_ref, prev_ref, tab_ref, ptab_ref, y_ref)

    const = lambda shape: pl.BlockSpec(shape, lambda c: (0, 0))
    smem = pl.BlockSpec(memory_space=pltpu.SMEM)
    rows = lambda w: pl.BlockSpec((CHUNK, w), lambda c: (c, 0))
    return _call(
        body, comm, name="mixer_fwd", grid=(nc,),
        in_specs=[rows(S_W), pl.BlockSpec((8, S_W), lambda c: (jnp.maximum(c * 16 - 1, 0), 0)),
                  const((4, D_XBC)), const((1, D_XBC)), smem, smem, smem, const((1, 1024)), const((128, 1024)),
                  smem, rows(A_W), pl.BlockSpec((CHUNK, 512), lambda c: (jnp.maximum(c - 1, 0), 2)),
                  rows(384), pl.BlockSpec((CHUNK, 384), lambda c: (jnp.maximum(c - 1, 0), 0))],
        out_specs=[rows(D_SSD + D_ATT), rows(D_SSD), pl.BlockSpec((128, 1024), lambda c: (c, 0)), rows(D_XBC)],
        out_shape=[jax.ShapeDtypeStruct((L, D_SSD + D_ATT), _MXU), jax.ShapeDtypeStruct((L, D_SSD), F32),
                   jax.ShapeDtypeStruct((nc * 128, 1024), F32), jax.ShapeDtypeStruct((L, D_XBC), F32)],
        scratch_shapes=[pltpu.VMEM((128, 1024), F32), pltpu.VMEM((136, D_XBC), F32)],
        args=(proj_ssd, proj_ssd, conv_w, conv_b, dt_bias, a_log, d_skip, norm_w, E,
              sinks, proj_att, proj_att, tabs, tabs))


def _swa_backward(proj_att, tabs, sinks, dy, reduce=None):
    L = proj_att.shape[0]
    nb = L // CHUNK

    def body(sink_ref, p_ref, prev_ref, tab_ref, ptab_ref, dy_ref, dp_ref, dsink_ref, carry_k, carry_v):
        i = pl.program_id(0)
        n = nb - 1 - i

        @pl.when(i == 0)
        def _():
            carry_k[...] = jnp.zeros_like(carry_k)
            carry_v[...] = jnp.zeros_like(carry_v)
            dsink_ref[...] = jnp.zeros_like(dsink_ref)

        tab, qr, kk, vv, pen = _swa_inputs(n, p_ref, prev_ref, tab_ref, ptab_ref)
        from_cur = _from_current()
        g = p_ref[:, A_G:A_G + 1024]
        sgm = _sigmoid(g)
        dyv = dy_ref[...]
        do_all = dyv * (g * sgm)
        lane8 = lax.broadcasted_iota(jnp.int32, (8, 128), 1)
        hid = lax.broadcasted_iota(jnp.int32, (1, 4 * CHUNK), 1) // CHUNK
        o_parts, dq_parts = [], []
        dk_nat = [jnp.zeros((2 * CHUNK, 128), F32) for _ in range(2)]
        dv_nat = [jnp.zeros((2 * CHUNK, 128), F32) for _ in range(2)]
        dsink = jnp.zeros((8, 128), F32)
        for j in range(KV_HEADS):
            qs = _stack_heads(qr, j)
            kkb, vvb = _kv_native(kk, j), _kv_native(vv, j)
            P, psink = _softmax_sink(_fold(_mm_nt(kkb, qs), from_cur, pen), _sink_row(sink_ref, j))
            p_full = _unfold(P, from_cur)
            o_parts.append(_unstack_heads(_mm_tn(p_full, vvb), j))
            do_s = _stack_heads(do_all, j)
            dP = _fold(_mm_nt(vvb, do_s), from_cur)
            D = jnp.sum(P * dP, axis=0, keepdims=True)
            ds_full = _unfold(P * (dP - D), from_cur)
            sd = psink * D
            for hh in range(4):
                dsink = dsink + jnp.where(lane8 == 4 * j + hh, -jnp.sum(jnp.where(hid == hh, sd, 0.0)), 0.0)
            dq_parts.append(_unstack_heads(_mm_tn(ds_full, kkb), j) * ATT_SCALE)
            dk_nat[j // 2] = dk_nat[j // 2] + _keep_native(_mm(ds_full, qs), j)
            dv_nat[j // 2] = dv_nat[j // 2] + _keep_native(_mm(p_full, do_s), j)
        o = jnp.concatenate(o_parts, axis=1)
        dkk = jnp.concatenate(dk_nat, axis=1)
        dvv = jnp.concatenate(dv_nat, axis=1)
        out = dp_ref.dtype
        dp_ref[:, A_Q:A_Q + 1024] = _rope_transposed(jnp.concatenate(dq_parts, axis=1), tab).astype(out)
        dp_ref[:, A_K:A_K + 256] = _rope_transposed(dkk[CHUNK:2 * CHUNK] + carry_k[...], tab).astype(out)
        dp_ref[:, A_V:A_V + 256] = (dvv[CHUNK:2 * CHUNK] + carry_v[...]).astype(out)
        dp_ref[:, A_G:A_G + 1024] = (dyv * o * (sgm * (1.0 + g * (1.0 - sgm)))).astype(out)
        carry_k[...] = dkk[0:CHUNK]
        carry_v[...] = dvv[0:CHUNK]
        dsink_ref[...] += dsink

    rev = lambda i: (nb - 1 - i, 0)
    prev = lambda i: jnp.maximum(nb - 2 - i, 0)
    return _call(
        body, None, name="swa_bwd", grid=(nb,),
        in_specs=[pl.BlockSpec(memory_space=pltpu.SMEM),
                  pl.BlockSpec((CHUNK, A_W), rev),
                  pl.BlockSpec((CHUNK, 512), lambda i: (prev(i), 2)),
                  pl.BlockSpec((CHUNK, 384), rev),
                  pl.BlockSpec((CHUNK, 384), lambda i: (prev(i), 0)),
                  pl.BlockSpec((CHUNK, D_ATT), lambda i: (nb - 1 - i, 1))],
        out_specs=[pl.BlockSpec((CHUNK, A_W), rev), pl.BlockSpec((8, 128), lambda i: (0, 0))],
        out_shape=[jax.ShapeDtypeStruct((L, A_W), _MXU), jax.ShapeDtypeStruct((8, 128), F32)],
        scratch_shapes=[pltpu.VMEM((CHUNK, 256), F32), pltpu.VMEM((CHUNK, 256), F32)],
        args=(sinks, proj_att, proj_att, tabs, tabs, dy), reduce=reduce)


def _head(y, x, target, w_out, ln_g, ln_b, *, tm):
    L = x.shape[0]
    nsteps = L // tm

    def body(y_ref, x_ref, t_ref, wo_ref, g_ref, b_ref, dr_ref, dy_ref, acc_ref):
        i = pl.program_id(0)

        @pl.when(i == 0)
        def _():
            acc_ref[...] = jnp.zeros_like(acc_ref)

        r = ALPHA * x_ref[...] + _mm(y_ref[...], wo_ref[...])
        mu = jnp.mean(r, axis=-1, keepdims=True)
        d = r - mu
        rstd = lax.rsqrt(jnp.mean(d * d, axis=-1, keepdims=True) + LN_EPS)
        xh = d * rstd
        gam = g_ref[0:1, :]
        e = xh * gam + b_ref[0:1, :] - t_ref[...]
        dout = e * (1.0 / D_MODEL)
        dxh = dout * gam
        dr = rstd * (dxh - jnp.mean(dxh, axis=-1, keepdims=True)
                     - xh * jnp.mean(dxh * xh, axis=-1, keepdims=True))
        dr_ref[...] = dr
        dy_ref[...] = _mm_nt(dr, wo_ref[...])
        acc_ref[...] += _rows8([_colsum(dout * xh), _colsum(dout), _colsum(e * e) * (0.5 / D_MODEL)])

        @pl.when(i == nsteps - 1)
        def _():
            acc = acc_ref[...]
            tot = jnp.sum(acc[2:3, :])
            rid = lax.broadcasted_iota(jnp.int32, (8, 1024), 0)
            acc_ref[...] = jnp.where(rid == 3, tot, acc)

    const = lambda shape: pl.BlockSpec(shape, lambda i: (0, 0))
    row = lambda w: pl.BlockSpec((tm, w), lambda i: (i, 0))
    return pl.pallas_call(
        body, name="head", grid=(nsteps,),
        in_specs=[row(2048), row(1024), row(1024), const((2048, 1024)), const((1, 1024)), const((1, 1024))],
        out_specs=[row(1024), row(2048), const((8, 1024))],
        out_shape=[jax.ShapeDtypeStruct((L, D_MODEL), F32), jax.ShapeDtypeStruct((L, 2048), F32),
                   jax.ShapeDtypeStruct((8, 1024), F32)],
        compiler_params=_params(("arbitrary",)),
    )(y, x, target, w_out, ln_g, ln_b)


def _gather_w_in(w_shard, positions):
    R = w_shard.shape[0]
    halves = (pl.ds(0, R // 2), pl.ds(R // 2, R // 2))
    any_spec = pl.BlockSpec(memory_space=pl.ANY)
    vmem = pl.BlockSpec(memory_space=pltpu.VMEM)

    def body(in_ref, pos_ref, inv_ref, out_ref, tab_ref, tab_scr, send_sems, recv_sems, local_sem, tab_sem):
        x, y, c = _position()

        def slot(p, half=None):
            s = out_ref.at[_index(*p)]
            return s if half is None else s.at[halves[half]]

        def same_core(p):
            return (p[0], p[1], c)

        def other_core(p):
            return (p[0], p[1], 1 - c)

        me, xn, yn, dg = (x, y), (1 - x, y), (x, 1 - y), (1 - x, 1 - y)

        def copy(k, dst, to, src=None):
            return _remote(dst if src is None else src, dst, send_sems.at[k], recv_sems.at[k], to)

        local = pltpu.make_async_copy(in_ref, slot(same_core(me)), local_sem)
        local.start()
        own = [copy(0, slot(same_core(me)), other_core(me), in_ref), copy(1, slot(same_core(me)), same_core(xn), in_ref),
               copy(2, slot(same_core(me)), same_core(yn), in_ref)]
        for cp in own:
            cp.start()
        _rope_tables(pos_ref, inv_ref, tab_scr)
        tab_out = pltpu.make_async_copy(tab_scr, tab_ref, tab_sem)
        tab_out.start()
        copy(1, slot(same_core(xn)), same_core(xn)).wait_recv()
        passed = [copy(4, slot(same_core(xn), 1), same_core(yn)), copy(5, slot(same_core(xn)), other_core(me))]
        for cp in passed:
            cp.start()
        copy(2, slot(same_core(yn)), same_core(yn)).wait_recv()
        more = [copy(3, slot(same_core(yn), 0), same_core(xn)), copy(6, slot(same_core(yn)), other_core(me))]
        for cp in more:
            cp.start()
        passed += more
        for k, half in ((3, 0), (4, 1)):
            copy(k, slot(same_core(dg), half), same_core(xn)).wait_recv()
            fwd = copy(7 + half, slot(same_core(dg), half), other_core(me))
            fwd.start()
            passed.append(fwd)
        copy(0, slot(other_core(me)), other_core(me)).wait_recv()
        copy(5, slot(other_core(xn)), other_core(me)).wait_recv()
        copy(6, slot(other_core(yn)), other_core(me)).wait_recv()
        for half in (0, 1):
            copy(7 + half, slot(other_core(dg), half), other_core(me)).wait_recv()
        for cp in own + passed:
            cp.wait_send()
        local.wait()
        tab_out.wait()

    return pl.pallas_call(
        body, name="gather_w_in", in_specs=[any_spec, vmem, vmem], out_specs=[any_spec, any_spec],
        out_shape=[jax.ShapeDtypeStruct((N_DEV,) + w_shard.shape, w_shard.dtype),
                   jax.ShapeDtypeStruct((positions.size, 384), F32)],
        scratch_shapes=[pltpu.VMEM((positions.size, 384), F32), pltpu.SemaphoreType.DMA((9,)),
                        pltpu.SemaphoreType.DMA((9,)), pltpu.SemaphoreType.DMA, pltpu.SemaphoreType.DMA],
        compiler_params=_params(),
    )(w_shard, positions, jnp.asarray(ROPE_INV)[None, :])


def _input_gradient(d_ssd, d_att, w_ssd, w_att, dr, *, tm, comm=None, reduce=None):
    L = dr.shape[0]

    def body(ds_ref, da_ref, ws_ref, wa_ref, dr_ref, o_ref):
        o_ref[...] = ALPHA * dr_ref[...] + _mm_nt(ds_ref[...], ws_ref[...]) + _mm_nt(da_ref[...], wa_ref[...])

    row = lambda w: pl.BlockSpec((tm, w), lambda i: (i, 0))
    resident = lambda a: pl.BlockSpec(a.shape, lambda i: (0, 0), pipeline_mode=pl.Buffered(1))
    return _call(body, comm, name="dx", grid=(L // tm,),
                 in_specs=[row(S_W), row(A_W), resident(w_ssd), resident(w_att), row(D_MODEL)],
                 out_specs=[row(D_MODEL)], out_shape=[jax.ShapeDtypeStruct((L, D_MODEL), F32)],
                 scratch_shapes=[], args=(d_ssd, d_att, w_ssd, w_att, dr), reduce=reduce)


SHARD_COLS = D_IN_PROJ // N_DEV
SPLIT = N_SSD_REAL - 4 * SHARD_COLS
RELAYOUT_ROWS = 256


def _unpack_w_in(w_all):
    def body(g_ref, ws_ref, wa_ref):
        for j in range(4):
            ws_ref[:, SHARD_COLS * j:SHARD_COLS * (j + 1)] = g_ref[j]
        ws_ref[:, 4 * SHARD_COLS:N_SSD_REAL] = g_ref[4, :, 0:SPLIT]
        ws_ref[:, N_SSD_REAL:S_W] = jnp.zeros((RELAYOUT_ROWS, S_W - N_SSD_REAL), ws_ref.dtype)
        wa_ref[:, 0:SHARD_COLS - SPLIT] = g_ref[4, :, SPLIT:SHARD_COLS]
        for j in range(5, N_DEV):
            lo = SHARD_COLS * (j - 4) - SPLIT
            wa_ref[:, lo:lo + SHARD_COLS] = g_ref[j]

    return pl.pallas_call(
        body, name="unpack_w_in", grid=(D_MODEL // RELAYOUT_ROWS,),
        in_specs=[pl.BlockSpec((N_DEV, RELAYOUT_ROWS, SHARD_COLS), lambda i: (0, i, 0))],
        out_specs=[pl.BlockSpec((RELAYOUT_ROWS, S_W), lambda i: (i, 0)), pl.BlockSpec((RELAYOUT_ROWS, A_W), lambda i: (i, 0))],
        out_shape=[jax.ShapeDtypeStruct((D_MODEL, S_W), w_all.dtype), jax.ShapeDtypeStruct((D_MODEL, A_W), w_all.dtype)],
        compiler_params=_params(("arbitrary",)),
    )(w_all)


def _dw_in(xb, d, half, tail=None, *, tl=1024):
    L, N = d.shape
    steps = L // tl

    def body(x_ref, d_ref, *refs):
        if half == 0:
            p_ref, tail_ref, acc, p_scr, p_sems = refs
        else:
            t_ref, p_ref, acc, p_scr, p_sems = refs
        l = pl.program_id(0)

        @pl.when(l == 0)
        def _():
            acc[...] = jnp.zeros_like(acc)

        acc[...] += _mm_tn(x_ref[...], d_ref[...])

        @pl.when(l == steps - 1)
        def _():
            if half == 0:
                tail_ref[...] = acc[:, S_DT:S_W]
            outs = []
            for j in range(4):
                if half == 0:
                    pieces = [(0, acc[:, SHARD_COLS * j:SHARD_COLS * (j + 1)])]
                elif j == 0:
                    pieces = [(0, t_ref[:, 4 * SHARD_COLS - S_DT:N_SSD_REAL - S_DT]), (SPLIT, acc[:, 0:SHARD_COLS - SPLIT])]
                else:
                    lo = SHARD_COLS * j - SPLIT
                    pieces = [(0, acc[:, lo:lo + SHARD_COLS])]
                for off, blk in pieces:
                    p_scr[j, :, off:off + blk.shape[1]] = blk
                outs.append(pltpu.make_async_copy(p_scr.at[j], p_ref.at[j], p_sems.at[j]))
                outs[-1].start()
            for cp in outs:
                cp.wait()

    once = pl.Buffered(1)
    whole = lambda shape: pl.BlockSpec(shape, lambda l: (0,) * len(shape), pipeline_mode=once)
    in_specs = [pl.BlockSpec((tl, D_MODEL), lambda l: (l, 0)), pl.BlockSpec((tl, N), lambda l: (l, 0))]
    args = [xb, d]
    stack = jax.ShapeDtypeStruct((4, D_MODEL, SHARD_COLS), F32)
    out_shape, out_specs = [stack], [pl.BlockSpec(memory_space=pl.ANY)]
    if half == 0:
        out_shape.append(jax.ShapeDtypeStruct((D_MODEL, S_W - S_DT), F32))
        out_specs.append(whole(out_shape[-1].shape))
    else:
        in_specs.append(whole(tail.shape))
        args.append(tail)
    return pl.pallas_call(
        body, name="dw_in_%d" % half, grid=(steps,), in_specs=in_specs, out_specs=out_specs, out_shape=out_shape,
        scratch_shapes=[pltpu.VMEM((D_MODEL, N), F32), pltpu.VMEM(stack.shape, F32), pltpu.SemaphoreType.DMA((4,))],
        compiler_params=_params(("arbitrary",)),
    )(*args)


def _adamw_math(w, g, m, v):
    m = ADAM_B1 * m + (1.0 - ADAM_B1) * g
    v = ADAM_B2 * v + (1.0 - ADAM_B2) * (g * g)
    m_hat = m / (1.0 - ADAM_B1 ** ADAM_STEP)
    v_hat = v / (1.0 - ADAM_B2 ** ADAM_STEP)
    delta = -ADAM_LR * (m_hat / (jnp.sqrt(v_hat) + ADAM_EPS) + ADAM_WD * w)
    return delta, m, v


def _adamw_shard(g_own, recv, w, m, v, *, rows, name):
    R, C = g_own.shape

    def body(g_ref, r_ref, w_ref, m_ref, v_ref, go_ref, d_ref, mo_ref, vo_ref):
        g = g_ref[...]
        for k in range(N_DEV - 1):
            g = g + r_ref[k].astype(F32)
        d, mn, vn = _adamw_math(w_ref[...], g, m_ref[...], v_ref[...])
        go_ref[...] = g
        d_ref[...] = d
        mo_ref[...] = mn
        vo_ref[...] = vn

    blk = pl.BlockSpec((rows, C), lambda i: (i, 0))
    return pl.pallas_call(
        body, name=name, grid=(R // rows,),
        in_specs=[blk, pl.BlockSpec((N_DEV - 1, rows, C), lambda i: (0, i, 0)), blk, blk, blk],
        out_specs=[blk] * 4, out_shape=[jax.ShapeDtypeStruct((R, C), F32)] * 4,
        compiler_params=_params(("arbitrary",)),
    )(g_own, recv, w, m, v)


def _minor_rows_view(a):
    return jnp.transpose(a, (2, 0, 1)).reshape(SHARD_COLS * 8, 128)


def _from_minor_rows_view(v):
    return jnp.transpose(v.reshape(SHARD_COLS, 8, 128), (1, 2, 0)).reshape(1, D_MODEL, SHARD_COLS)


def _adamw_w_in(is_lo, own_lo, own_hi, recv_lo, recv_hi, w, m, v):
    C = SHARD_COLS
    pad = -C % 128
    bands = [pl.ds(q * 128, 128) for q in range(D_MODEL // 128)]

    def body(lo_ref, ol_ref, oh_ref, rl_ref, rh_ref, w_ref, m_ref, v_ref, go_ref, d_ref, mo_ref, vo_ref,
             own_scr, recv_scr, sems):
        def fetch(own_src, recv_src):
            return [(pltpu.make_async_copy(own_src.at[band, :], own_scr.at[band, :], sems.at[0, q]),
                     pltpu.make_async_copy(recv_src.at[:, band, :], recv_scr.at[:, band, :], sems.at[1, q]))
                    for q, band in enumerate(bands)]

        from_lo, from_hi = fetch(ol_ref, rl_ref), fetch(oh_ref, rh_ref)

        @pl.when(lo_ref[0] == 1)
        def _():
            for pair in from_lo:
                for cp in pair:
                    cp.start()

        @pl.when(lo_ref[0] != 1)
        def _():
            for pair in from_hi:
                for cp in pair:
                    cp.start()

        for q, band in enumerate(bands):
            for cp in from_lo[q]:
                cp.wait()
            g = own_scr[band, :] + recv_scr[0, band, :].astype(F32) + recv_scr[1, band, :].astype(F32)
            g = jnp.pad(g, ((0, 0), (0, pad))).T[0:C]
            rows = pl.ds(q, C, stride=8)
            d, mn, vn = _adamw_math(w_ref[rows, :], g, m_ref[rows, :], v_ref[rows, :])
            go_ref[rows, :] = g
            d_ref[rows, :] = d
            mo_ref[rows, :] = mn
            vo_ref[rows, :] = vn

    vmem = pl.BlockSpec(memory_space=pltpu.VMEM)
    return pl.pallas_call(
        body, name="adamw_w_in", out_shape=[jax.ShapeDtypeStruct(w.shape, F32)] * 4,
        in_specs=[pl.BlockSpec(memory_space=pltpu.SMEM)] + [pl.BlockSpec(memory_space=pl.ANY)] * 4 + [vmem] * 3,
        out_specs=[vmem] * 4,
        scratch_shapes=[pltpu.VMEM(own_lo.shape, F32), pltpu.VMEM(recv_lo.shape, BF16),
                        pltpu.SemaphoreType.DMA((2, len(bands)))],
        compiler_params=_params(),
    )(is_lo, own_lo, own_hi, recv_lo, recv_hi, w, m, v)


SMALL = ("conv_b", "dt_bias", "a_log", "d_skip", "ssd_norm_w", "attn_sinks", "ln_g", "ln_b")


def _adamw_small(gathered, params):
    n_p = len(SMALL)

    def body(*refs):
        acc = []
        for r in refs[:5]:
            t = r[0]
            for k in range(1, N_DEV):
                t = t + r[k]
            acc.append(t)
        head, conv, norm, scal, sink = acc
        grads = dict(conv_b=conv[4:5, :], dt_bias=scal[0:1, 0:N_HEADS], a_log=scal[1:2, 0:N_HEADS],
                     d_skip=scal[2:3, 0:N_HEADS], ssd_norm_w=norm[0:1, :], attn_sinks=sink[0:1, 0:N_HEADS],
                     ln_g=head[0:1, :], ln_b=head[1:2, :])
        wmv = refs[5:5 + 3 * n_p]
        outs = refs[5 + 3 * n_p:]
        outs[0][...] = head[3:4, 0:1]
        outs[1][...] = conv[0:4, :]
        for i, name in enumerate(SMALL):
            w_ref, m_ref, v_ref = wmv[3 * i:3 * i + 3]
            g = grads[name]
            d, mn, vn = _adamw_math(w_ref[...], g, m_ref[...], v_ref[...])
            for o_ref, val in zip(outs[2 + 4 * i:6 + 4 * i], (g, d, mn, vn)):
                o_ref[...] = val

    flat = [a for name in SMALL for a in params[name]]
    out_shape = [jax.ShapeDtypeStruct((1, 1), F32), jax.ShapeDtypeStruct((4, D_XBC), F32)]
    for name in SMALL:
        out_shape += [jax.ShapeDtypeStruct(params[name][0].shape, F32)] * 4
    res = pl.pallas_call(body, name="adamw_small", out_shape=out_shape, compiler_params=_params())(*gathered, *flat)
    return res[0], res[1], {name: res[2 + 4 * i:6 + 4 * i] for i, name in enumerate(SMALL)}


def _adamw_plain(g, w, m, v):
    def body(g_ref, w_ref, m_ref, v_ref, d_ref, mo_ref, vo_ref):
        d, mn, vn = _adamw_math(w_ref[...], g_ref[...], m_ref[...], v_ref[...])
        d_ref[...] = d
        mo_ref[...] = mn
        vo_ref[...] = vn

    return pl.pallas_call(
        body, name="adamw_conv_w", out_shape=[jax.ShapeDtypeStruct(w.shape, F32)] * 3,
        compiler_params=_params(),
    )(g, w, m, v)


def _lane_pattern(fn):
    return np.asarray([fn(l % HEAD_DIM) for l in range(128)], np.float32)


ROPE_INV = _lane_pattern(lambda r: ROPE_THETA ** (-2.0 * (r % 8) / ROPE_DIM) if r < ROPE_DIM else 0.0)


def _rope_tables(pos_ref, inv_ref, tab_ref):
    lane = lax.broadcasted_iota(jnp.int32, (1, 128), 1) % HEAD_DIM
    upper = jnp.where((lane >= ROPE_DIM // 2) & (lane < ROPE_DIM), 1.0, 0.0)
    lower = jnp.where(lane < ROPE_DIM // 2, -1.0, 0.0)

    def block(r, carry):
        rows = pl.ds(pl.multiple_of(r * CHUNK, CHUNK), CHUNK)
        pos = jnp.broadcast_to(pos_ref[pl.ds(r, 1), :].astype(F32), (CHUNK, 128)).T
        ang = pos * inv_ref[...]
        sn = jnp.sin(ang)
        tab_ref[rows, 0:128] = jnp.cos(ang)
        tab_ref[rows, 128:256] = sn * upper
        tab_ref[rows, 256:384] = sn * lower
        return carry

    lax.fori_loop(0, pos_ref.shape[0], block, 0)


def _expansion():
    E = np.arange(1024)[None, :] // HEAD_DIM == np.arange(128)[:, None]
    return jnp.asarray(E, BF16), jnp.asarray(E.T, BF16)


def _ssd_args(conv_w, conv_b, dt_bias, a_log, d_skip, norm_w, E):
    return (conv_w, conv_b, dt_bias.reshape(-1), a_log.reshape(-1), d_skip.reshape(-1), norm_w, E)


def kernel(x, positions, w_in, conv_w, conv_b, dt_bias, a_log, d_skip, ssd_norm_w, attn_sinks, w_out, ln_g, ln_b, loss_target, m_w_in, m_conv_w, m_conv_b, m_dt_bias, m_a_log, m_d_skip, m_ssd_norm_w, m_attn_sinks, m_w_out, m_ln_g, m_ln_b, v_w_in, v_conv_w, v_conv_b, v_dt_bias, v_a_log, v_d_skip, v_ssd_norm_w, v_attn_sinks, v_w_out, v_ln_g, v_ln_b):
    me = _index(*_position())
    x0, target = x[0], loss_target[0]
    bf16_shard = lambda shape: jax.ShapeDtypeStruct(shape, BF16)
    E, ET = _expansion()
    sinks = attn_sinks.reshape(-1)

    w_all, tabs = _gather_w_in(w_in[0].astype(BF16), positions[0].reshape(-1, 128))
    w_ssd, w_att = _unpack_w_in(w_all)
    gather_conv_w = _Hosted([conv_w[0]], [jax.ShapeDtypeStruct((N_DEV,) + conv_w.shape[1:], F32)],
                            [_Flow("gather", 0, 0)])

    proj_ssd, proj_att, xb, conv_w_all = _in_proj(x0, w_ssd, w_att, tm=512, comm=gather_conv_w)
    conv_w_f = jnp.transpose(conv_w_all, (1, 0, 2)).reshape(4, D_XBC)
    ssd_args = _ssd_args(conv_w_f, conv_b, dt_bias, a_log, d_skip, ssd_norm_w, E)
    gather_w_out = _Hosted([w_out[0].astype(BF16)], [bf16_shard((N_DEV, 256, D_MODEL))], [_Flow("gather", 0, 0)])
    y, ypre, hprev, pre, w_out_all = _mixer_forward(proj_ssd, proj_att, tabs, sinks, *ssd_args, comm=gather_w_out)
    w_out_f = w_out_all.reshape(2 * D_MODEL, D_MODEL)
    dr, dy, acc_head = _head(y, x0, target, w_out_f, ln_g, ln_b, tm=512)

    dw_out, dw_out_bf16 = _matmul_tn(y, dr, tl=1024, tn=D_MODEL, name="dw_out", emit_bf16=True)
    own_out = lax.dynamic_index_in_dim(dw_out.reshape(N_DEV, 256, D_MODEL), me, axis=0, keepdims=False)
    send_out = _Hosted([dw_out_bf16.reshape(N_DEV, 256, D_MODEL)], [bf16_shard((N_DEV - 1, 256, D_MODEL))],
                       [_Flow("exchange", 0, 0)])
    d_ssd, acc_cw, acc_w, acc_s, recv_out = _ssd_backward(proj_ssd, hprev, ypre, pre, dy, *ssd_args, ET, comm=send_out)
    stack_lo, dw_dt_block = _dw_in(xb, d_ssd, 0)
    d_att, dsink, own_lo, recv_lo = _swa_backward(proj_att, tabs, sinks, dy, reduce=_OwnerReduce(stack_lo, 0))
    (stack_hi,) = _dw_in(xb, d_att, 1, dw_dt_block)
    accs = [acc_head, acc_cw, acc_w, acc_s, dsink]
    gather_accs = _Hosted(accs, [jax.ShapeDtypeStruct((N_DEV,) + a.shape, F32) for a in accs],
                          [_Flow("gather", i, i) for i in range(5)])
    dx, *gathered, own_hi, recv_hi = _input_gradient(d_ssd, d_att, w_ssd, w_att, dr, tm=256, comm=gather_accs,
                                                     reduce=_OwnerReduce(stack_hi, 1))
    is_lo = (me < 4).reshape(1).astype(jnp.int32)

    g_in, d_in, nm_in, nv_in = [_from_minor_rows_view(r) for r in _adamw_w_in(
        is_lo, own_lo, own_hi, recv_lo, recv_hi, _minor_rows_view(w_in), _minor_rows_view(m_w_in), _minor_rows_view(v_w_in))]
    g_out, d_out, nm_out, nv_out = _adamw_shard(own_out, recv_out, w_out[0], m_w_out[0], v_w_out[0],
                                                rows=256, name="adamw_w_out")
    loss, g_conv_w, small = _adamw_small(gathered, dict(
        conv_b=(conv_b, m_conv_b, v_conv_b), dt_bias=(dt_bias, m_dt_bias, v_dt_bias), a_log=(a_log, m_a_log, v_a_log),
        d_skip=(d_skip, m_d_skip, v_d_skip), ssd_norm_w=(ssd_norm_w, m_ssd_norm_w, v_ssd_norm_w),
        attn_sinks=(attn_sinks, m_attn_sinks, v_attn_sinks), ln_g=(ln_g, m_ln_g, v_ln_g), ln_b=(ln_b, m_ln_b, v_ln_b)))
    g_cw = lax.dynamic_slice_in_dim(g_conv_w, me * (D_XBC // N_DEV), D_XBC // N_DEV, axis=1)
    d_cw, nm_cw, nv_cw = _adamw_plain(g_cw, conv_w[0], m_conv_w[0], v_conv_w[0])

    def leaves(i, big_in, cw, big_out):
        mid = [small[k][i] for k in ("conv_b", "dt_bias", "a_log", "d_skip", "ssd_norm_w", "attn_sinks")]
        return [big_in, cw[None]] + mid + [big_out[None], small["ln_g"][i], small["ln_b"][i]]

    return (loss.reshape(()), dx[None], *leaves(0, g_in, g_cw, g_out), *leaves(1, d_in, d_cw, d_out),
            *leaves(2, nm_in, nm_cw, nm_out), *leaves(3, nv_in, nv_cw, nv_out))
```

```python
import jax
import jax.numpy as jnp
from jax import lax
from jax.experimental import pallas as pl
from jax.experimental.pallas import tpu as pltpu
import numpy as np

F32 = jnp.float32
BF16 = jnp.bfloat16
_MXU = jnp.bfloat16

N_DEV = 8
D_MODEL = 1024
D_SSD = 1024
D_ATT = 1024
HEAD_DIM = 64
N_HEADS = 16
SSD_GROUPS = 2
KV_HEADS = 4
CHUNK = 128
D_XBC = 1536
D_IN_PROJ = 5136
ROPE_DIM = 16
ROPE_THETA = 500000.0
ALPHA = (2.0 * 1) ** 0.25
LN_EPS = 1e-5
RMS_EPS = 1e-5
ATT_SCALE = HEAD_DIM ** -0.5
NEG = -1e30

S_Z, S_XS, S_B, S_C, S_DT, S_W = 0, 1024, 2048, 2304, 2560, 2816
N_SSD_REAL = 2576
A_Q, A_K, A_V, A_G, A_W = 0, 1024, 1280, 1536, 2560

ADAM_LR = 0.001
ADAM_B1 = 0.9
ADAM_B2 = 0.999
ADAM_EPS = 1e-08
ADAM_WD = 0.01
ADAM_STEP = 10

VMEM_LIMIT = 48 * 1024 * 1024
MESH = pl.DeviceIdType.MESH


def _params(sem=None):
    return pltpu.CompilerParams(dimension_semantics=sem, vmem_limit_bytes=VMEM_LIMIT)


def _mm(a, b):
    return jnp.dot(a.astype(_MXU), b.astype(_MXU), preferred_element_type=F32)


def _mm_nt(a, b):
    return lax.dot_general(a.astype(_MXU), b.astype(_MXU), (((1,), (1,)), ((), ())),
                           preferred_element_type=F32)


def _mm_tn(a, b):
    return lax.dot_general(a.astype(_MXU), b.astype(_MXU), (((0,), (0,)), ((), ())),
                           preferred_element_type=F32)


def _split3(v):
    hi = v.astype(BF16)
    r = v - hi.astype(F32)
    mid = r.astype(BF16)
    lo = (r - mid.astype(F32)).astype(BF16)
    return hi, mid, lo


def _mm_exact_r(v, p01):
    hi, mid, lo = _split3(v)
    d = lambda a: jnp.dot(a, p01, preferred_element_type=F32)
    return d(hi) + d(mid) + d(lo)


def _mm_exact_l(p01, v):
    hi, mid, lo = _split3(v)
    d = lambda a: jnp.dot(p01, a, preferred_element_type=F32)
    return d(hi) + d(mid) + d(lo)


def _mm_2pass_r(v, p01):
    hi = v.astype(BF16)
    lo = (v - hi.astype(F32)).astype(BF16)
    return jnp.dot(hi, p01, preferred_element_type=F32) + jnp.dot(lo, p01, preferred_element_type=F32)


def _sigmoid(x):
    return 1.0 / (1.0 + jnp.exp(-x))


def _softplus(x):
    e = jnp.exp(-jnp.abs(x))
    u = 1.0 + e
    log1p = jnp.where(u == 1.0, e, jnp.log(u) * (e / (u - 1.0)))
    return jnp.maximum(x, 0.0) + log1p


def _rows8(rows):
    n = rows[0].shape[1]
    rid = lax.broadcasted_iota(jnp.int32, (8, n), 0)
    out = jnp.zeros((8, n), F32)
    for k, r in enumerate(rows):
        out = out + jnp.where(rid == k, r, 0.0)
    return out


def _colsum(a):
    return jnp.sum(a, axis=0, keepdims=True)


def _in_proj(x, w_ssd, w_att, *, tm, comm=None):
    L, K = x.shape

    def body(x_ref, ws_ref, wa_ref, ps_ref, pa_ref, xb_ref):
        xb = x_ref[...].astype(_MXU)
        xb_ref[...] = xb
        ps_ref[...] = jnp.dot(xb, ws_ref[...], preferred_element_type=F32)
        pa_ref[...] = jnp.dot(xb, wa_ref[...], preferred_element_type=F32)

    row = lambda w: pl.BlockSpec((tm, w), lambda i: (i, 0))
    resident = lambda a: pl.BlockSpec(a.shape, lambda i: (0, 0), pipeline_mode=pl.Buffered(1))
    return _call(
        body, comm, name="in_proj", grid=(L // tm,),
        in_specs=[row(K), resident(w_ssd), resident(w_att)], out_specs=[row(S_W), row(A_W), row(K)],
        out_shape=[jax.ShapeDtypeStruct((L, S_W), F32), jax.ShapeDtypeStruct((L, A_W), F32),
                   jax.ShapeDtypeStruct((L, K), _MXU)],
        scratch_shapes=[], args=(x, w_ssd, w_att))


def _dw_out(me1, y, dr, *, tl):
    L, M = y.shape
    N = dr.shape[1]
    steps = L // tl
    own_rows = M // N_DEV

    def body(me_ref, y_ref, dr_ref, b_ref, own_ref, acc):
        l = pl.program_id(0)

        @pl.when(l == 0)
        def _():
            acc[...] = jnp.zeros_like(acc)

        acc[...] += _mm_tn(y_ref[...], dr_ref[...])

        @pl.when(l == steps - 1)
        def _():
            b_ref[...] = acc[...].astype(b_ref.dtype)
            own_ref[...] = acc[pl.ds(pl.multiple_of(me_ref[0] * own_rows, own_rows), own_rows), :]

    whole = lambda shape: pl.BlockSpec(shape, lambda l: (0, 0), pipeline_mode=pl.Buffered(1))
    return pl.pallas_call(
        body, name="dw_out", grid=(steps,),
        in_specs=[pl.BlockSpec(memory_space=pltpu.SMEM), pl.BlockSpec((tl, M), lambda l: (l, 0)),
                  pl.BlockSpec((tl, N), lambda l: (l, 0))],
        out_specs=[whole((M, N)), whole((own_rows, N))],
        out_shape=[jax.ShapeDtypeStruct((M, N), BF16), jax.ShapeDtypeStruct((own_rows, N), F32)],
        scratch_shapes=[pltpu.VMEM((M, N), F32)], compiler_params=_params(("arbitrary",)),
    )(me1, y, dr)


def _position():
    return lax.axis_index("x"), lax.axis_index("y"), lax.axis_index("c")


def _index(px, py, pc):
    return 4 * px + 2 * py + pc


def _flip(pos, k):
    x, y, c = pos
    return ((1 - x) if (k >> 2) & 1 else x, (1 - y) if (k >> 1) & 1 else y, (1 - c) if k & 1 else c)


def _remote(src, dst, send_sem, recv_sem, peer):
    return pltpu.make_async_remote_copy(src_ref=src, dst_ref=dst, send_sem=send_sem, recv_sem=recv_sem,
                                        device_id=peer, device_id_type=MESH)


class _Flow:
    def __init__(self, kind, operand, result):
        self.kind, self.operand, self.result = kind, operand, result


class _Hosted:
    def __init__(self, operands, out_shapes, flows):
        self.operands, self.out_shapes, self.flows = operands, out_shapes, flows

    def plan(self, ins, outs, send_sems, recv_sems, local_sems):
        me = _position()
        mi = _index(*me)
        sends, recvs, locals_ = [], [], []
        for row, f in enumerate(self.flows):
            src, dst = ins[f.operand], outs[f.result]
            for k in range(1, N_DEV):
                peer = _flip(me, k)
                sems = (send_sems.at[row, k - 1], recv_sems.at[row, k - 1])
                if f.kind == "exchange":
                    sends.append(_remote(src.at[_index(*peer)], dst.at[k - 1], *sems, peer))
                    recvs.append(sends[-1])
                else:
                    sends.append(_remote(src, dst.at[mi], *sems, peer))
                    recvs.append(_remote(src, dst.at[_index(*peer)], *sems, peer))
            if f.kind == "gather":
                locals_.append(pltpu.make_async_copy(src, dst.at[mi], local_sems.at[row]))

        def start():
            for cp in locals_ + sends:
                cp.start()

        def wait():
            for cp in recvs:
                cp.wait_recv()
            for cp in sends:
                cp.wait_send()
            for cp in locals_:
                cp.wait()

        return start, wait


class _OwnerReduce:
    FIRST_STEP, SECOND_STEP, SEND_STEPS, REDUCE_STEPS = 2, 4, (2, 4, 6, 8), (5, 8, 10, 13)

    def __init__(self, stack, target_x):
        self.stack, self.target_x = stack, target_x
        block = stack.shape[1:]
        self.chunks = len(self.SEND_STEPS)
        self.chunk_rows = block[0] // self.chunks
        self.out_shapes = [jax.ShapeDtypeStruct(block, F32), jax.ShapeDtypeStruct((2,) + block, BF16)]
        self.out_specs = [pl.BlockSpec(block, lambda *_: (0, 0), pipeline_mode=pl.Buffered(1)),
                          pl.BlockSpec(memory_space=pl.ANY)]
        dma = pltpu.SemaphoreType.DMA
        self.scratch_shapes = ([pltpu.VMEM((2,) + block, F32)] * 2 + [pltpu.VMEM(block, BF16)] * 3
                               + [dma((self.chunks,))] * 4 + [dma((2,))] * 3)

    def plan(self, i, steps, stack_ref, own_ref, recv_ref, scratch):
        assert self.FIRST_STEP <= self.SEND_STEPS[0] and self.SECOND_STEP < self.REDUCE_STEPS[0] < steps - 1
        (theirs_scr, mine_scr, first_scr, across_scr, out_scr, y_send_sems, y_recv_sems, x_send_sems, x_recv_sems,
         swap_send_sems, swap_recv_sems, mine_sems) = scratch
        x, y, c = _position()
        owners_side = x == self.target_x
        other_side = x != self.target_x
        sibling, across, owner = (x, y, 1 - c), (x, 1 - y, c), (self.target_x, y, c)
        order = (1 - y, y)
        swaps = [_remote(stack_ref.at[2 * order[j] + (1 - c)], theirs_scr.at[j], swap_send_sems.at[j], swap_recv_sems.at[j],
                         sibling) for j in range(2)]
        mine = [pltpu.make_async_copy(stack_ref.at[2 * order[j] + c], mine_scr.at[j], mine_sems.at[j]) for j in range(2)]
        chunks = range(self.chunks)
        part = [pl.ds(j * self.chunk_rows, self.chunk_rows) for j in chunks]
        y_sems = lambda j: (y_send_sems.at[j], y_recv_sems.at[j])
        to_neighbour = [_remote(first_scr.at[part[j]], across_scr.at[part[j]], *y_sems(j), across) for j in chunks]
        to_owner_y = [_remote(first_scr.at[part[j]], recv_ref.at[0, part[j]], *y_sems(j), across) for j in chunks]
        to_owner_x = [_remote(out_scr.at[part[j]], recv_ref.at[1, part[j]], x_send_sems.at[j], x_recv_sems.at[j], owner)
                      for j in chunks]

        def before():
            @pl.when(i <= 1)
            def _():
                @pl.when(i == 0)
                def _():
                    for cp in [swaps[0]] + mine:
                        cp.start()

                pl.when(i == 1)(swaps[1].start)

        def after():
            @pl.when(i <= self.REDUCE_STEPS[-1])
            def _():
                @pl.when(i == self.FIRST_STEP)
                def _():
                    swaps[0].wait_recv()
                    mine[0].wait()
                    first_scr[...] = (mine_scr[0] + theirs_scr[0]).astype(first_scr.dtype)

                for j in chunks:
                    pl.when((i == self.SEND_STEPS[j]) & other_side)(to_neighbour[j].start)
                    pl.when((i == self.SEND_STEPS[j]) & owners_side)(to_owner_y[j].start)

                @pl.when(i == self.SECOND_STEP)
                def _():
                    swaps[1].wait_recv()
                    mine[1].wait()
                    t = mine_scr[1] + theirs_scr[1]
                    own_ref[...] = t
                    mine_scr[1] = t

                for j in chunks:
                    @pl.when((i == self.REDUCE_STEPS[j]) & other_side)
                    def _(j=j):
                        to_neighbour[j].wait_recv()
                        t = mine_scr[1, part[j], :] + across_scr[part[j], :].astype(F32)
                        out_scr[part[j], :] = t.astype(out_scr.dtype)
                        to_owner_x[j].start()

            @pl.when(i == steps - 1)
            def _():
                for cp in swaps:
                    cp.wait_send()
                for j in chunks:
                    @pl.when(other_side)
                    def _(j=j):
                        to_neighbour[j].wait_send()
                        to_owner_x[j].wait_send()

                    @pl.when(owners_side)
                    def _(j=j):
                        to_owner_y[j].wait_send()
                        to_owner_y[j].wait_recv()
                        to_owner_x[j].wait_recv()

        return before, after


def _call(body, comm, *, name, grid, in_specs, out_specs, out_shape, scratch_shapes, args, reduce=None):
    semantics = ("arbitrary",) * len(grid)
    if comm is None and reduce is None:
        return pl.pallas_call(body, name=name, grid=grid, in_specs=in_specs, out_specs=out_specs, out_shape=out_shape,
                              scratch_shapes=scratch_shapes, compiler_params=_params(semantics))(*args)
    n_in, n_out, n_scr = len(args), len(out_shape), len(scratch_shapes)
    c_operands, c_shapes, flows = (comm.operands, comm.out_shapes, comm.flows) if comm else ([], [], [])
    c_in, c_out, rows = len(c_operands), len(c_shapes), max(len(flows), 1)
    r_in = 0 if reduce is None else 1

    def hosted(*refs):
        ins, refs = refs[:n_in], refs[n_in:]
        cins, refs = refs[:c_in], refs[c_in:]
        rins, refs = refs[:r_in], refs[r_in:]
        outs, refs = refs[:n_out], refs[n_out:]
        couts, refs = refs[:c_out], refs[c_out:]
        routs, refs = refs[:2 * r_in], refs[2 * r_in:]
        scr, refs = refs[:n_scr], refs[n_scr:]
        (send_sems, recv_sems, local_sems), r_scr = refs[:3], refs[3:]
        ids = [pl.program_id(d) for d in range(len(grid))]
        first, last = ids[0] == 0, ids[0] == grid[0] - 1
        for d in range(1, len(grid)):
            first, last = first & (ids[d] == 0), last & (ids[d] == grid[d] - 1)
        before = after = lambda: None
        if reduce is not None:
            before, after = reduce.plan(ids[0], grid[0], rins[0], *routs, r_scr)
        if comm is not None:
            start, wait = comm.plan(cins, couts, send_sems, recv_sems, local_sems)
            pl.when(first)(start)
        before()
        body(*ins, *outs, *scr)
        after()
        if comm is not None:
            pl.when(last)(wait)

    any_spec = pl.BlockSpec(memory_space=pl.ANY)
    sems = [pltpu.SemaphoreType.DMA((rows, N_DEV - 1)), pltpu.SemaphoreType.DMA((rows, N_DEV - 1)),
            pltpu.SemaphoreType.DMA((rows,))]
    r_operands, r_specs, r_shapes, r_scratch = ([reduce.stack], reduce.out_specs, reduce.out_shapes,
                                                reduce.scratch_shapes) if reduce else ([], [], [], [])
    return pl.pallas_call(
        hosted, name=name, grid=grid, in_specs=list(in_specs) + [any_spec] * (c_in + r_in),
        out_specs=list(out_specs) + [any_spec] * c_out + r_specs, out_shape=list(out_shape) + list(c_shapes) + r_shapes,
        scratch_shapes=list(scratch_shapes) + sems + r_scratch,
        compiler_params=_params(semantics))(*args, *c_operands, *r_operands)


def _head_row(ref, width, rep):
    hid = lax.broadcasted_iota(jnp.int32, (1, width), 1) // rep
    row = jnp.zeros((1, width), F32)
    for h in range(N_HEADS):
        row = jnp.where(hid == h, ref[h], row)
    return row


def _rows_from_above(u_b, s, ext_scr, row, col):
    down = (row - col == s).astype(_MXU)
    return jnp.concatenate([ext_scr[8 - s:16 - s, :], jnp.dot(down, u_b, preferred_element_type=F32)[8:128]], axis=0)


def _ssd_recompute(first, p_ref, halo_ref, cw_ref, cb_ref, dtb_ref, alog_ref, e_ref, ext_scr, pre=None):
    row = lax.broadcasted_iota(jnp.int32, (128, 128), 0)
    col = lax.broadcasted_iota(jnp.int32, (128, 128), 1)
    ext_scr[0:8, :] = jnp.where(first, 0.0, halo_ref[:, S_XS:S_DT])
    if pre is not None:
        ext_scr[8:16, :] = p_ref[0:8, S_XS:S_DT]
    else:
        ext_scr[8:136, :] = p_ref[:, S_XS:S_DT]
        cw = cw_ref[...]
        pre = (cb_ref[0:1, :] + cw[3:4, :] * ext_scr[8:136, :] + cw[2:3, :] * ext_scr[7:135, :]
               + cw[1:2, :] * ext_scr[6:134, :] + cw[0:1, :] * ext_scr[5:133, :])
    sg = _sigmoid(pre)
    act = pre * sg
    lane = lax.broadcasted_iota(jnp.int32, (1, 128), 1)
    A = jnp.where(lane < N_HEADS, -jnp.exp(_head_row(alog_ref, 128, 1)), 0.0)
    raw = p_ref[:, S_DT:S_DT + 128] + _head_row(dtb_ref, 128, 1)
    dt = _softplus(raw)
    dA = dt * A
    tril = (row >= col).astype(BF16)
    acs = _mm_exact_l(tril, dA)
    last = acs[127:128, :]
    ds = jnp.exp(last - acs)
    eo = jnp.exp(acs)
    E = e_ref[...]
    ex = _mm_2pass_r(jnp.concatenate([dt, ds, eo], axis=0), E)
    dt_e, ds_e, eo_e = ex[0:128], ex[128:256], ex[256:384]
    xs_c = act[:, 0:1024]
    X = xs_c * dt_e
    return dict(pre=pre, sg=sg, xs_c=xs_c, Bc=act[:, 1024:1280], Cc=act[:, 1280:1536], A=A, raw=raw, dt=dt,
                acs=acs, acsT=acs.T, eo_e=eo_e, ds_e=ds_e, dt_e=dt_e, cd_e=eo_e[127:128, :],
                X=X, Xd=X * ds_e, row=row, col=col)


def _split_halves(t):
    lo = _lo_half(CHUNK)
    return jnp.concatenate([jnp.where(lo, t, 0.0), jnp.where(lo, 0.0, t)], axis=0)


def _ssd_core(R, hprev):
    causal = R["row"] >= R["col"]
    acs, acsT, X = R["acs"], R["acsT"], R["X"]
    ydiag, yoff, snew = [], [], []
    for g in range(SSD_GROUPS):
        Bg = R["Bc"][:, g * 128:(g + 1) * 128]
        Cg = R["Cc"][:, g * 128:(g + 1) * 128]
        cols = slice(g * 512, (g + 1) * 512)
        CB = _mm_nt(Cg, Bg)
        snew.append(_mm_tn(Bg, R["Xd"][:, cols]))
        yoff.append(_mm(Cg, hprev[:, cols]))
        for j in range(4):
            h0 = g * 8 + 2 * j
            ms = [CB * jnp.exp(jnp.where(causal, acs[:, h:h + 1] - acsT[h:h + 1, :], NEG)) for h in (h0, h0 + 1)]
            ydiag.append(_mm(jnp.concatenate(ms, axis=1), _split_halves(X[:, h0 * HEAD_DIM:h0 * HEAD_DIM + 128])))
    Y = jnp.concatenate(ydiag, axis=1) + jnp.concatenate(yoff, axis=1) * R["eo_e"]
    return Y, jnp.concatenate(snew, axis=1)


def _ssd_forward_step(p_ref, halo_ref, cw_ref, cb_ref, dtb_ref, alog_ref, dsk_ref, nw_ref, e_ref,
                      y_ref, ypre_ref, hprev_ref, pre_ref, h_scr, ext_scr):
    c = pl.program_id(0)
    first = c == 0

    @pl.when(first)
    def _():
        h_scr[...] = jnp.zeros_like(h_scr)

    R = _ssd_recompute(first, p_ref, halo_ref, cw_ref, cb_ref, dtb_ref, alog_ref, e_ref, ext_scr)
    hprev = h_scr[...]
    hprev_ref[...] = hprev
    pre_ref[...] = R["pre"]
    Y, snew = _ssd_core(R, hprev)
    h_scr[...] = hprev * R["cd_e"] + snew
    Y = Y + _head_row(dsk_ref, D_SSD, HEAD_DIM) * R["xs_c"]
    ypre_ref[...] = Y
    z = p_ref[:, S_Z:S_Z + 1024]
    yf = Y * (z * _sigmoid(z))
    outs = []
    for g in range(SSD_GROUPS):
        yg = yf[:, g * 512:(g + 1) * 512]
        r = lax.rsqrt(jnp.mean(yg * yg, axis=-1, keepdims=True) + RMS_EPS)
        outs.append(yg * r)
    y_ref[:, 0:D_SSD] = (jnp.concatenate(outs, axis=1) * nw_ref[0:1, :]).astype(y_ref.dtype)


def _ssd_backward(proj_ssd, hprev_all, ypre, pre, dy, conv_w, conv_b, dt_bias, a_log, d_skip, norm_w, E, ET, comm=None):
    L = proj_ssd.shape[0]
    nc = L // CHUNK

    def body(p_ref, halo_ref, hprev_ref, ypre_ref, pre_ref, dy_ref, cw_ref, cb_ref, dtb_ref, alog_ref, dsk_ref, nw_ref, e_ref,
             et_ref, dp_ref, acc_cw_ref, acc_w_ref, acc_s_ref, dh_scr, ext_scr, ext2_scr, nxt_scr):
        i = pl.program_id(0)
        c = nc - 1 - i
        first = c == 0

        @pl.when(i == 0)
        def _():
            dh_scr[...] = jnp.zeros_like(dh_scr)
            nxt_scr[...] = jnp.zeros_like(nxt_scr)
            acc_cw_ref[...] = jnp.zeros_like(acc_cw_ref)
            acc_w_ref[...] = jnp.zeros_like(acc_w_ref)
            acc_s_ref[...] = jnp.zeros_like(acc_s_ref)

        R = _ssd_recompute(first, p_ref, halo_ref, cw_ref, cb_ref, dtb_ref, alog_ref, e_ref, ext_scr, pre_ref[...])
        hprev = hprev_ref[...]
        xs_c, X, Xd = R["xs_c"], R["X"], R["Xd"]
        acs, acsT = R["acs"], R["acsT"]
        ET = et_ref[...]
        dsk = _head_row(dsk_ref, D_SSD, HEAD_DIM)
        Y = ypre_ref[...]

        z = p_ref[:, S_Z:S_Z + 1024]
        sz = _sigmoid(z)
        silz = z * sz
        yf = Y * silz
        dyv = dy_ref[...]
        nw = nw_ref[0:1, :]
        dyf_parts, dnw_parts = [], []
        for g in range(SSD_GROUPS):
            cols = slice(g * 512, (g + 1) * 512)
            yg = yf[:, cols]
            r = lax.rsqrt(jnp.mean(yg * yg, axis=-1, keepdims=True) + RMS_EPS)
            yn = yg * r
            dyn = dyv[:, cols] * nw[:, cols]
            dnw_parts.append(_colsum(dyv[:, cols] * yn))
            dyf_parts.append(r * (dyn - yn * jnp.mean(dyn * yn, axis=-1, keepdims=True)))
        dyf = jnp.concatenate(dyf_parts, axis=1)
        dY = dyf * silz
        dz = dyf * Y * (sz * (1.0 + z * (1.0 - sz)))

        dhn = dh_scr[...]
        dYo = dY * R["eo_e"]
        causal = R["row"] >= R["col"]
        dacs = jnp.zeros((128, 128), F32)
        dacs_t = jnp.zeros((128, 128), F32)
        dxdiag, dxd, dhprev, dBs, dCs, yoff = [], [], [], [], [], []
        for g in range(SSD_GROUPS):
            Bg = R["Bc"][:, g * 128:(g + 1) * 128]
            Cg = R["Cc"][:, g * 128:(g + 1) * 128]
            cols = slice(g * 512, (g + 1) * 512)
            CB = _mm_nt(Cg, Bg)
            dCB = jnp.zeros((128, 128), F32)
            for j in range(4):
                h0 = g * 8 + 2 * j
                pc = slice(h0 * HEAD_DIM, h0 * HEAD_DIM + 128)
                dYst = _split_halves(dY[:, pc])
                dMst = _mm_nt(dYst, X[:, pc])
                mts = []
                for a, h in enumerate((h0, h0 + 1)):
                    acol = acs[:, h:h + 1]
                    arow = acsT[h:h + 1, :]
                    Lm = jnp.exp(jnp.where(causal, acol - arow, NEG))
                    M = CB * Lm
                    dM = dMst[a * 128:(a + 1) * 128]
                    dCB = dCB + dM * Lm
                    G = dM * M
                    dacs = dacs + jnp.where(R["col"] == h, jnp.sum(G, axis=1, keepdims=True), 0.0)
                    dacs_t = dacs_t + jnp.where(R["row"] == h, jnp.sum(G, axis=0, keepdims=True), 0.0)
                    mts.append(M.T)
                dxdiag.append(_mm(jnp.concatenate(mts, axis=1), dYst))
            dS = dhn[:, cols]
            dxd.append(_mm(Bg, dS))
            yoff.append(_mm(Cg, hprev[:, cols]))
            dhprev.append(_mm_tn(Cg, dYo[:, cols]))
            dCs.append(_mm_nt(dYo[:, cols], hprev[:, cols]) + _mm(dCB, Bg))
            dBs.append(_mm_tn(dCB, Cg) + _mm_nt(Xd[:, cols], dS))
        Yoff = jnp.concatenate(yoff, axis=1) * R["eo_e"]
        dXd = jnp.concatenate(dxd, axis=1)
        dX = jnp.concatenate(dxdiag, axis=1) + dXd * R["ds_e"]
        t_state = dXd * Xd
        dacs = dacs + _mm_2pass_r(dY * Yoff - t_state, ET) - dacs_t.T
        v_last = _colsum(t_state + dhn * hprev * R["cd_e"])
        dlast = _mm_exact_r(jnp.broadcast_to(v_last, (8, 1024)), ET)[0:1, :]
        dacs = dacs + jnp.where(R["row"] == 127, dlast, 0.0)
        triu = (R["col"] >= R["row"]).astype(BF16)
        da = _mm_exact_l(triu, dacs)
        ddt = da * R["A"] + _mm(dX * xs_c, ET)
        ddt_raw = ddt * _sigmoid(R["raw"])
        dxs_c = dX * R["dt_e"] + dY * dsk
        dh_scr[...] = jnp.concatenate(dhprev, axis=1) + dhn * R["cd_e"]

        dact = jnp.concatenate([dxs_c] + dBs + dCs, axis=1)
        pre, sg = R["pre"], R["sg"]
        dpre = dact * (sg * (1.0 + pre * (1.0 - sg)))
        ext2_scr[0:8, :] = dpre[120:128, :]
        ext2_scr[8:16, :] = nxt_scr[...]
        nxt_scr[...] = dpre[0:8, :]
        cw = cw_ref[...]
        u_b, dpre_b = p_ref[:, S_XS:S_DT].astype(_MXU), dpre.astype(_MXU)
        dxbc = cw[3:4, :] * dpre
        taps = [_colsum(dpre * p_ref[:, S_XS:S_DT])]
        for s in (1, 2, 3):
            up = (R["col"] - R["row"] == s).astype(_MXU)
            d_s = jnp.concatenate([jnp.dot(up, dpre_b, preferred_element_type=F32)[0:120],
                                   ext2_scr[s:8 + s, :]], axis=0)
            dxbc = dxbc + cw[3 - s:4 - s, :] * d_s
            taps.append(_colsum(dpre * _rows_from_above(u_b, s, ext_scr, R["row"], R["col"])))
        acc_cw_ref[...] += _rows8(taps[::-1] + [_colsum(dpre)])
        acc_w_ref[...] += _rows8([jnp.concatenate(dnw_parts, axis=1), _colsum(dY * xs_c)])
        acc_s_ref[...] += _rows8([_colsum(ddt_raw), _colsum(da * R["dt"])])

        lane = lax.broadcasted_iota(jnp.int32, (128, 128), 1)
        dp_ref[:, S_Z:S_Z + 1024] = dz.astype(dp_ref.dtype)
        dp_ref[:, S_XS:S_DT] = dxbc.astype(dp_ref.dtype)
        dp_ref[:, S_DT:S_DT + 128] = jnp.where(lane < N_HEADS, ddt_raw, 0.0).astype(dp_ref.dtype)
        dp_ref[:, S_DT + 128:S_W] = jnp.zeros((128, 128), dp_ref.dtype)

        @pl.when(i == nc - 1)
        def _():
            acc = acc_s_ref[...]
            dskip = _mm_exact_r(acc_w_ref[...], ET)[1:2, :]
            acc_s_ref[...] = _rows8([acc[0:1, :], acc[1:2, :] * R["A"], dskip])

    const = lambda shape: pl.BlockSpec(shape, lambda i: (0, 0))
    smem = pl.BlockSpec(memory_space=pltpu.SMEM)
    rev = lambda i: (nc - 1 - i, 0)
    return _call(
        body, comm, name="ssd_bwd", grid=(nc,),
        in_specs=[pl.BlockSpec((CHUNK, S_W), rev),
                  pl.BlockSpec((8, S_W), lambda i: (jnp.maximum((nc - 1 - i) * 16 - 1, 0), 0)),
                  pl.BlockSpec((128, 1024), rev),
                  pl.BlockSpec((CHUNK, D_SSD), rev),
                  pl.BlockSpec((CHUNK, D_XBC), rev),
                  pl.BlockSpec((CHUNK, D_SSD), rev),
                  const((4, D_XBC)), const((1, D_XBC)), smem, smem, smem, const((1, 1024)),
                  const((128, 1024)), const((1024, 128))],
        out_specs=[pl.BlockSpec((CHUNK, S_W), rev), const((8, D_XBC)), const((8, 1024)), const((8, 128))],
        out_shape=[jax.ShapeDtypeStruct((L, S_W), _MXU), jax.ShapeDtypeStruct((8, D_XBC), F32),
                   jax.ShapeDtypeStruct((8, 1024), F32), jax.ShapeDtypeStruct((8, 128), F32)],
        scratch_shapes=[pltpu.VMEM((128, 1024), F32), pltpu.VMEM((16, D_XBC), F32),
                        pltpu.VMEM((16, D_XBC), F32), pltpu.VMEM((8, D_XBC), F32)],
        args=(proj_ssd, proj_ssd, hprev_all, ypre, pre, dy, conv_w, conv_b, dt_bias, a_log, d_skip, norm_w, E, ET))


def _rope(t, tab):
    cos, sa, sb = tab[:, 0:128], tab[:, 128:256], tab[:, 256:384]
    outs = []
    for i in range(t.shape[1] // 128):
        tg = t[:, i * 128:(i + 1) * 128]
        outs.append(tg * cos + pltpu.roll(tg, 8, 1) * sa + pltpu.roll(tg, 120, 1) * sb)
    return jnp.concatenate(outs, axis=1)


def _rope_transposed(d, tab):
    cos, sa, sb = tab[:, 0:128], tab[:, 128:256], tab[:, 256:384]
    outs = []
    for i in range(d.shape[1] // 128):
        dg = d[:, i * 128:(i + 1) * 128]
        outs.append(dg * cos + pltpu.roll(dg * sa, 120, 1) + pltpu.roll(dg * sb, 8, 1))
    return jnp.concatenate(outs, axis=1)


def _lo_half(rows):
    return lax.broadcasted_iota(jnp.int32, (rows, 128), 1) < HEAD_DIM


def _native_half(rows, j):
    lo = _lo_half(rows)
    return lo if j % 2 == 0 else jnp.logical_not(lo)


def _kv_native(t, j):
    p = j // 2
    return jnp.where(_native_half(t.shape[0], j), t[:, p * 128:(p + 1) * 128], 0.0)


def _stack_heads(t, j):
    out = []
    for m in (2 * j, 2 * j + 1):
        pair = t[:, m * 128:(m + 1) * 128]
        swapped = pltpu.roll(pair, HEAD_DIM, 1)
        out += [pair, swapped] if j % 2 == 0 else [swapped, pair]
    return jnp.concatenate(out, axis=0)


def _unstack_heads(s, j):
    out = []
    for m in range(2):
        first, second = s[256 * m:256 * m + 128], s[256 * m + 128:256 * m + 256]
        if j % 2 == 0:
            out.append(first + pltpu.roll(second, HEAD_DIM, 1))
        else:
            out.append(pltpu.roll(first, HEAD_DIM, 1) + second)
    return jnp.concatenate(out, axis=1)


def _keep_native(r, j):
    return jnp.where(_native_half(r.shape[0], j), r, 0.0)


def _sink_row(sink_ref, j):
    hid = lax.broadcasted_iota(jnp.int32, (1, 4 * CHUNK), 1) // CHUNK
    row = jnp.zeros((1, 4 * CHUNK), F32)
    for hh in range(4):
        row = jnp.where(hid == hh, sink_ref[4 * j + hh], row)
    return row


def _from_current():
    si = lax.broadcasted_iota(jnp.int32, (CHUNK, 4 * CHUNK), 0)
    qi = lax.broadcasted_iota(jnp.int32, (CHUNK, 4 * CHUNK), 1) % CHUNK
    return si <= qi


def _fold(full, from_cur, pen=0.0):
    return jnp.where(from_cur, full[CHUNK:2 * CHUNK], full[0:CHUNK] + pen)


def _unfold(t, from_cur):
    c = jnp.where(from_cur, t, 0.0)
    return jnp.concatenate([t - c, c], axis=0)


def _softmax_sink(s, sink):
    mx = jnp.maximum(jnp.max(s, axis=0, keepdims=True), sink)
    p = jnp.exp(s - mx)
    esink = jnp.exp(sink - mx)
    inv = 1.0 / (jnp.sum(p, axis=0, keepdims=True) + esink)
    return p * inv, esink * inv


def _swa_inputs(blk, p_ref, prev_ref, tab_ref, ptab_ref):
    tab = tab_ref[...]
    qr = _rope(p_ref[:, A_Q:A_Q + 1024], tab) * ATT_SCALE
    kk = jnp.concatenate([_rope(prev_ref[:, 0:256], ptab_ref[...]), _rope(p_ref[:, A_K:A_K + 256], tab)], axis=0)
    vv = jnp.concatenate([prev_ref[:, 256:512], p_ref[:, A_V:A_V + 256]], axis=0)
    return tab, qr, kk, vv, jnp.where(blk > 0, 0.0, NEG)


def _swa_forward_step(sink_ref, p_ref, prev_ref, tab_ref, ptab_ref, y_ref):
    n = pl.program_id(0)
    _, qr, kk, vv, pen = _swa_inputs(n, p_ref, prev_ref, tab_ref, ptab_ref)
    from_cur = _from_current()
    outs = []
    for j in range(KV_HEADS):
        s = _fold(_mm_nt(_kv_native(kk, j), _stack_heads(qr, j)), from_cur, pen)
        P, _ = _softmax_sink(s, _sink_row(sink_ref, j))
        outs.append(_unstack_heads(_mm_tn(_unfold(P, from_cur), _kv_native(vv, j)), j))
    g = p_ref[:, A_G:A_G + 1024]
    y_ref[:, D_SSD:D_SSD + D_ATT] = (jnp.concatenate(outs, axis=1) * (g * _sigmoid(g))).astype(y_ref.dtype)


def _mixer_forward(proj_ssd, proj_att, tabs, sinks, conv_w, conv_b, dt_bias, a_log, d_skip, norm_w, E, comm=None):
    L = proj_ssd.shape[0]
    nc = L // CHUNK

    def body(p_ref, halo_ref, cw_ref, cb_ref, dtb_ref, alog_ref, dsk_ref, nw_ref, e_ref,
             sink_ref, pa_ref, prev_ref, tab_ref, ptab_ref, y_ref, ypre_ref, hprev_ref, pre_ref, h_scr, ext_scr):
        _ssd_forward_step(p_ref, halo_ref, cw_ref, cb_ref, dtb_ref, alog_ref, dsk_ref, nw_ref, e_ref,
                          y_ref, ypre_ref, hprev_ref, pre_ref, h_scr, ext_scr)
        _swa_forward_step(sink_ref, pa_ref, prev_ref, tab_ref, ptab_ref, y_ref)

    const = lambda shape: pl.BlockSpec(shape, lambda c: (0, 0))
    smem = pl.BlockSpec(memory_space=pltpu.SMEM)
    rows = lambda w: pl.BlockSpec((CHUNK, w), lambda c: (c, 0))
    return _call(
        body, comm, name="mixer_fwd", grid=(nc,),
        in_specs=[rows(S_W), pl.BlockSpec((8, S_W), lambda c: (jnp.maximum(c * 16 - 1, 0), 0)),
                  const((4, D_XBC)), const((1, D_XBC)), smem, smem, smem, const((1, 1024)), const((128, 1024)),
                  smem, rows(A_W), pl.BlockSpec((CHUNK, 512), lambda c: (jnp.maximum(c - 1, 0), 2)),
                  rows(384), pl.BlockSpec((CHUNK, 384), lambda c: (jnp.maximum(c - 1, 0), 0))],
        out_specs=[rows(D_SSD + D_ATT), rows(D_SSD), pl.BlockSpec((128, 1024), lambda c: (c, 0)), rows(D_XBC)],
        out_shape=[jax.ShapeDtypeStruct((L, D_SSD + D_ATT), _MXU), jax.ShapeDtypeStruct((L, D_SSD), F32),
                   jax.ShapeDtypeStruct((nc * 128, 1024), F32), jax.ShapeDtypeStruct((L, D_XBC), F32)],
        scratch_shapes=[pltpu.VMEM((128, 1024), F32), pltpu.VMEM((136, D_XBC), F32)],
        args=(proj_ssd, proj_ssd, conv_w, conv_b, dt_bias, a_log, d_skip, norm_w, E,
              sinks, proj_att, proj_att, tabs, tabs))


def _swa_backward(proj_att, tabs, sinks, dy, reduce=None):
    L = proj_att.shape[0]
    nb = L // CHUNK

    def body(sink_ref, p_ref, prev_ref, tab_ref, ptab_ref, dy_ref, dp_ref, dsink_ref, carry_k, carry_v):
        i = pl.program_id(0)
        n = nb - 1 - i

        @pl.when(i == 0)
        def _():
            carry_k[...] = jnp.zeros_like(carry_k)
            carry_v[...] = jnp.zeros_like(carry_v)
            dsink_ref[...] = jnp.zeros_like(dsink_ref)

        tab, qr, kk, vv, pen = _swa_inputs(n, p_ref, prev_ref, tab_ref, ptab_ref)
        from_cur = _from_current()
        g = p_ref[:, A_G:A_G + 1024]
        sgm = _sigmoid(g)
        dyv = dy_ref[...]
        do_all = dyv * (g * sgm)
        lane8 = lax.broadcasted_iota(jnp.int32, (8, 128), 1)
        hid = lax.broadcasted_iota(jnp.int32, (1, 4 * CHUNK), 1) // CHUNK
        o_parts, dq_parts = [], []
        dk_nat = [jnp.zeros((2 * CHUNK, 128), F32) for _ in range(2)]
        dv_nat = [jnp.zeros((2 * CHUNK, 128), F32) for _ in range(2)]
        dsink = jnp.zeros((8, 128), F32)
        for j in range(KV_HEADS):
            qs = _stack_heads(qr, j)
            kkb, vvb = _kv_native(kk, j), _kv_native(vv, j)
            P, psink = _softmax_sink(_fold(_mm_nt(kkb, qs), from_cur, pen), _sink_row(sink_ref, j))
            p_full = _unfold(P, from_cur)
            o_parts.append(_unstack_heads(_mm_tn(p_full, vvb), j))
            do_s = _stack_heads(do_all, j)
            dP = _fold(_mm_nt(vvb, do_s), from_cur)
            D = jnp.sum(P * dP, axis=0, keepdims=True)
            ds_full = _unfold(P * (dP - D), from_cur)
            sd = psink * D
            for hh in range(4):
                dsink = dsink + jnp.where(lane8 == 4 * j + hh, -jnp.sum(jnp.where(hid == hh, sd, 0.0)), 0.0)
            dq_parts.append(_unstack_heads(_mm_tn(ds_full, kkb), j) * ATT_SCALE)
            dk_nat[j // 2] = dk_nat[j // 2] + _keep_native(_mm(ds_full, qs), j)
            dv_nat[j // 2] = dv_nat[j // 2] + _keep_native(_mm(p_full, do_s), j)
        o = jnp.concatenate(o_parts, axis=1)
        dkk = jnp.concatenate(dk_nat, axis=1)
        dvv = jnp.concatenate(dv_nat, axis=1)
        out = dp_ref.dtype
        dp_ref[:, A_Q:A_Q + 1024] = _rope_transposed(jnp.concatenate(dq_parts, axis=1), tab).astype(out)
        dp_ref[:, A_K:A_K + 256] = _rope_transposed(dkk[CHUNK:2 * CHUNK] + carry_k[...], tab).astype(out)
        dp_ref[:, A_V:A_V + 256] = (dvv[CHUNK:2 * CHUNK] + carry_v[...]).astype(out)
        dp_ref[:, A_G:A_G + 1024] = (dyv * o * (sgm * (1.0 + g * (1.0 - sgm)))).astype(out)
        carry_k[...] = dkk[0:CHUNK]
        carry_v[...] = dvv[0:CHUNK]
        dsink_ref[...] += dsink

    rev = lambda i: (nb - 1 - i, 0)
    prev = lambda i: jnp.maximum(nb - 2 - i, 0)
    return _call(
        body, None, name="swa_bwd", grid=(nb,),
        in_specs=[pl.BlockSpec(memory_space=pltpu.SMEM),
                  pl.BlockSpec((CHUNK, A_W), rev),
                  pl.BlockSpec((CHUNK, 512), lambda i: (prev(i), 2)),
                  pl.BlockSpec((CHUNK, 384), rev),
                  pl.BlockSpec((CHUNK, 384), lambda i: (prev(i), 0)),
                  pl.BlockSpec((CHUNK, D_ATT), lambda i: (nb - 1 - i, 1))],
        out_specs=[pl.BlockSpec((CHUNK, A_W), rev), pl.BlockSpec((8, 128), lambda i: (0, 0))],
        out_shape=[jax.ShapeDtypeStruct((L, A_W), _MXU), jax.ShapeDtypeStruct((8, 128), F32)],
        scratch_shapes=[pltpu.VMEM((CHUNK, 256), F32), pltpu.VMEM((CHUNK, 256), F32)],
        args=(sinks, proj_att, proj_att, tabs, tabs, dy), reduce=reduce)


def _head(y, x, target, w_out, ln_g, ln_b, *, tm):
    L = x.shape[0]
    nsteps = L // tm

    def body(y_ref, x_ref, t_ref, wo_ref, g_ref, b_ref, dr_ref, dy_ref, acc_ref):
        i = pl.program_id(0)

        @pl.when(i == 0)
        def _():
            acc_ref[...] = jnp.zeros_like(acc_ref)

        r = ALPHA * x_ref[...] + _mm(y_ref[...], wo_ref[...])
        mu = jnp.mean(r, axis=-1, keepdims=True)
        d = r - mu
        rstd = lax.rsqrt(jnp.mean(d * d, axis=-1, keepdims=True) + LN_EPS)
        xh = d * rstd
        gam = g_ref[0:1, :]
        e = xh * gam + b_ref[0:1, :] - t_ref[...]
        dout = e * (1.0 / D_MODEL)
        dxh = dout * gam
        dr = rstd * (dxh - jnp.mean(dxh, axis=-1, keepdims=True)
                     - xh * jnp.mean(dxh * xh, axis=-1, keepdims=True))
        dr_ref[...] = dr
        dy_ref[...] = _mm_nt(dr, wo_ref[...])
        acc_ref[...] += _rows8([_colsum(dout * xh), _colsum(dout), _colsum(e * e) * (0.5 / D_MODEL)])

        @pl.when(i == nsteps - 1)
        def _():
            acc = acc_ref[...]
            tot = jnp.sum(acc[2:3, :])
            rid = lax.broadcasted_iota(jnp.int32, (8, 1024), 0)
            acc_ref[...] = jnp.where(rid == 3, tot, acc)

    const = lambda shape: pl.BlockSpec(shape, lambda i: (0, 0))
    row = lambda w: pl.BlockSpec((tm, w), lambda i: (i, 0))
    return pl.pallas_call(
        body, name="head", grid=(nsteps,),
        in_specs=[row(2048), row(1024), row(1024), const((2048, 1024)), const((1, 1024)), const((1, 1024))],
        out_specs=[row(1024), row(2048), const((8, 1024))],
        out_shape=[jax.ShapeDtypeStruct((L, D_MODEL), F32), jax.ShapeDtypeStruct((L, 2048), F32),
                   jax.ShapeDtypeStruct((8, 1024), F32)],
        compiler_params=_params(("arbitrary",)),
    )(y, x, target, w_out, ln_g, ln_b)


def _gather_w_in(w_shard, positions):
    R = w_shard.shape[0]
    halves = (pl.ds(0, R // 2), pl.ds(R // 2, R // 2))
    any_spec = pl.BlockSpec(memory_space=pl.ANY)
    vmem = pl.BlockSpec(memory_space=pltpu.VMEM)

    def body(in_ref, pos_ref, inv_ref, out_ref, tab_ref, tab_scr, send_sems, recv_sems, local_sem, tab_sem):
        x, y, c = _position()

        def slot(p, half=None):
            s = out_ref.at[_index(*p)]
            return s if half is None else s.at[halves[half]]

        def same_core(p):
            return (p[0], p[1], c)

        def other_core(p):
            return (p[0], p[1], 1 - c)

        me, xn, yn, dg = (x, y), (1 - x, y), (x, 1 - y), (1 - x, 1 - y)

        def copy(k, dst, to, src=None):
            return _remote(dst if src is None else src, dst, send_sems.at[k], recv_sems.at[k], to)

        local = pltpu.make_async_copy(in_ref, slot(same_core(me)), local_sem)
        local.start()
        own = [copy(0, slot(same_core(me)), other_core(me), in_ref), copy(1, slot(same_core(me)), same_core(xn), in_ref),
               copy(2, slot(same_core(me)), same_core(yn), in_ref)]
        for cp in own:
            cp.start()
        _rope_tables(pos_ref, inv_ref, tab_scr)
        tab_out = pltpu.make_async_copy(tab_scr, tab_ref, tab_sem)
        tab_out.start()
        copy(1, slot(same_core(xn)), same_core(xn)).wait_recv()
        passed = [copy(4, slot(same_core(xn), 1), same_core(yn)), copy(5, slot(same_core(xn)), other_core(me))]
        for cp in passed:
            cp.start()
        copy(2, slot(same_core(yn)), same_core(yn)).wait_recv()
        more = [copy(3, slot(same_core(yn), 0), same_core(xn)), copy(6, slot(same_core(yn)), other_core(me))]
        for cp in more:
            cp.start()
        passed += more
        for k, half in ((3, 0), (4, 1)):
            copy(k, slot(same_core(dg), half), same_core(xn)).wait_recv()
            fwd = copy(7 + half, slot(same_core(dg), half), other_core(me))
            fwd.start()
            passed.append(fwd)
        copy(0, slot(other_core(me)), other_core(me)).wait_recv()
        copy(5, slot(other_core(xn)), other_core(me)).wait_recv()
        copy(6, slot(other_core(yn)), other_core(me)).wait_recv()
        for half in (0, 1):
            copy(7 + half, slot(other_core(dg), half), other_core(me)).wait_recv()
        for cp in own + passed:
            cp.wait_send()
        local.wait()
        tab_out.wait()

    return pl.pallas_call(
        body, name="gather_w_in", in_specs=[any_spec, vmem, vmem], out_specs=[any_spec, any_spec],
        out_shape=[jax.ShapeDtypeStruct((N_DEV,) + w_shard.shape, w_shard.dtype),
                   jax.ShapeDtypeStruct((positions.size, 384), F32)],
        scratch_shapes=[pltpu.VMEM((positions.size, 384), F32), pltpu.SemaphoreType.DMA((9,)),
                        pltpu.SemaphoreType.DMA((9,)), pltpu.SemaphoreType.DMA, pltpu.SemaphoreType.DMA],
        compiler_params=_params(),
    )(w_shard, positions, jnp.asarray(ROPE_INV)[None, :])


def _input_gradient(d_ssd, d_att, w_ssd, w_att, dr, *, tm, comm=None, reduce=None):
    L = dr.shape[0]

    def body(ds_ref, da_ref, ws_ref, wa_ref, dr_ref, o_ref):
        o_ref[...] = ALPHA * dr_ref[...] + _mm_nt(ds_ref[...], ws_ref[...]) + _mm_nt(da_ref[...], wa_ref[...])

    row = lambda w: pl.BlockSpec((tm, w), lambda i: (i, 0))
    resident = lambda a: pl.BlockSpec(a.shape, lambda i: (0, 0), pipeline_mode=pl.Buffered(1))
    return _call(body, comm, name="dx", grid=(L // tm,),
                 in_specs=[row(S_W), row(A_W), resident(w_ssd), resident(w_att), row(D_MODEL)],
                 out_specs=[row(D_MODEL)], out_shape=[jax.ShapeDtypeStruct((L, D_MODEL), F32)],
                 scratch_shapes=[], args=(d_ssd, d_att, w_ssd, w_att, dr), reduce=reduce)


SHARD_COLS = D_IN_PROJ // N_DEV
SPLIT = N_SSD_REAL - 4 * SHARD_COLS
RELAYOUT_ROWS = 256


def _unpack_w_in(w_all):
    def body(g_ref, ws_ref, wa_ref):
        for j in range(4):
            ws_ref[:, SHARD_COLS * j:SHARD_COLS * (j + 1)] = g_ref[j]
        ws_ref[:, 4 * SHARD_COLS:N_SSD_REAL] = g_ref[4, :, 0:SPLIT]
        ws_ref[:, N_SSD_REAL:S_W] = jnp.zeros((RELAYOUT_ROWS, S_W - N_SSD_REAL), ws_ref.dtype)
        wa_ref[:, 0:SHARD_COLS - SPLIT] = g_ref[4, :, SPLIT:SHARD_COLS]
        for j in range(5, N_DEV):
            lo = SHARD_COLS * (j - 4) - SPLIT
            wa_ref[:, lo:lo + SHARD_COLS] = g_ref[j]

    return pl.pallas_call(
        body, name="unpack_w_in", grid=(D_MODEL // RELAYOUT_ROWS,),
        in_specs=[pl.BlockSpec((N_DEV, RELAYOUT_ROWS, SHARD_COLS), lambda i: (0, i, 0))],
        out_specs=[pl.BlockSpec((RELAYOUT_ROWS, S_W), lambda i: (i, 0)), pl.BlockSpec((RELAYOUT_ROWS, A_W), lambda i: (i, 0))],
        out_shape=[jax.ShapeDtypeStruct((D_MODEL, S_W), w_all.dtype), jax.ShapeDtypeStruct((D_MODEL, A_W), w_all.dtype)],
        compiler_params=_params(("arbitrary",)),
    )(w_all)


def _dw_in(xb, d, half, tail=None, *, tl=1024):
    L, N = d.shape
    steps = L // tl

    def body(x_ref, d_ref, *refs):
        if half == 0:
            p_ref, tail_ref, acc, p_scr, p_sems = refs
        else:
            t_ref, p_ref, acc, p_scr, p_sems = refs
        l = pl.program_id(0)

        @pl.when(l == 0)
        def _():
            acc[...] = jnp.zeros_like(acc)

        acc[...] += _mm_tn(x_ref[...], d_ref[...])

        @pl.when(l == steps - 1)
        def _():
            if half == 0:
                tail_ref[...] = acc[:, S_DT:S_W]
            outs = []
            for j in range(4):
                if half == 0:
                    pieces = [(0, acc[:, SHARD_COLS * j:SHARD_COLS * (j + 1)])]
                elif j == 0:
                    pieces = [(0, t_ref[:, 4 * SHARD_COLS - S_DT:N_SSD_REAL - S_DT]), (SPLIT, acc[:, 0:SHARD_COLS - SPLIT])]
                else:
                    lo = SHARD_COLS * j - SPLIT
                    pieces = [(0, acc[:, lo:lo + SHARD_COLS])]
                for off, blk in pieces:
                    p_scr[j, :, off:off + blk.shape[1]] = blk
                outs.append(pltpu.make_async_copy(p_scr.at[j], p_ref.at[j], p_sems.at[j]))
                outs[-1].start()
            for cp in outs:
                cp.wait()

    once = pl.Buffered(1)
    whole = lambda shape: pl.BlockSpec(shape, lambda l: (0,) * len(shape), pipeline_mode=once)
    in_specs = [pl.BlockSpec((tl, D_MODEL), lambda l: (l, 0)), pl.BlockSpec((tl, N), lambda l: (l, 0))]
    args = [xb, d]
    stack = jax.ShapeDtypeStruct((4, D_MODEL, SHARD_COLS), F32)
    out_shape, out_specs = [stack], [pl.BlockSpec(memory_space=pl.ANY)]
    if half == 0:
        out_shape.append(jax.ShapeDtypeStruct((D_MODEL, S_W - S_DT), F32))
        out_specs.append(whole(out_shape[-1].shape))
    else:
        in_specs.append(whole(tail.shape))
        args.append(tail)
    return pl.pallas_call(
        body, name="dw_in_%d" % half, grid=(steps,), in_specs=in_specs, out_specs=out_specs, out_shape=out_shape,
        scratch_shapes=[pltpu.VMEM((D_MODEL, N), F32), pltpu.VMEM(stack.shape, F32), pltpu.SemaphoreType.DMA((4,))],
        compiler_params=_params(("arbitrary",)),
    )(*args)


def _adamw_math(w, g, m, v):
    m = ADAM_B1 * m + (1.0 - ADAM_B1) * g
    v = ADAM_B2 * v + (1.0 - ADAM_B2) * (g * g)
    m_hat = m / (1.0 - ADAM_B1 ** ADAM_STEP)
    v_hat = v / (1.0 - ADAM_B2 ** ADAM_STEP)
    delta = -ADAM_LR * (m_hat / (jnp.sqrt(v_hat) + ADAM_EPS) + ADAM_WD * w)
    return delta, m, v


def _adamw_shard(g_own, recv, w, m, v, *, rows, name):
    R, C = g_own.shape

    def body(g_ref, r_ref, w_ref, m_ref, v_ref, go_ref, d_ref, mo_ref, vo_ref):
        g = g_ref[...]
        for k in range(N_DEV - 1):
            g = g + r_ref[k].astype(F32)
        d, mn, vn = _adamw_math(w_ref[...], g, m_ref[...], v_ref[...])
        go_ref[...] = g
        d_ref[...] = d
        mo_ref[...] = mn
        vo_ref[...] = vn

    blk = pl.BlockSpec((rows, C), lambda i: (i, 0))
    return pl.pallas_call(
        body, name=name, grid=(R // rows,),
        in_specs=[blk, pl.BlockSpec((N_DEV - 1, rows, C), lambda i: (0, i, 0)), blk, blk, blk],
        out_specs=[blk] * 4, out_shape=[jax.ShapeDtypeStruct((R, C), F32)] * 4,
        compiler_params=_params(("arbitrary",)),
    )(g_own, recv, w, m, v)


def _minor_rows_view(a):
    return jnp.transpose(a, (2, 0, 1)).reshape(SHARD_COLS * 8, 128)


def _from_minor_rows_view(v):
    return jnp.transpose(v.reshape(SHARD_COLS, 8, 128), (1, 2, 0)).reshape(1, D_MODEL, SHARD_COLS)


def _adamw_w_in(is_lo, own_lo, own_hi, recv_lo, recv_hi, w, m, v):
    C = SHARD_COLS
    pad = -C % 128
    bands = [pl.ds(q * 128, 128) for q in range(D_MODEL // 128)]

    def body(lo_ref, ol_ref, oh_ref, rl_ref, rh_ref, w_ref, m_ref, v_ref, go_ref, d_ref, mo_ref, vo_ref,
             own_scr, recv_scr, sems):
        def fetch(own_src, recv_src):
            return [(pltpu.make_async_copy(own_src.at[band, :], own_scr.at[band, :], sems.at[0, q]),
                     pltpu.make_async_copy(recv_src.at[:, band, :], recv_scr.at[:, band, :], sems.at[1, q]))
                    for q, band in enumerate(bands)]

        from_lo, from_hi = fetch(ol_ref, rl_ref), fetch(oh_ref, rh_ref)

        @pl.when(lo_ref[0] == 1)
        def _():
            for pair in from_lo:
                for cp in pair:
                    cp.start()

        @pl.when(lo_ref[0] != 1)
        def _():
            for pair in from_hi:
                for cp in pair:
                    cp.start()

        for q, band in enumerate(bands):
            for cp in from_lo[q]:
                cp.wait()
            g = own_scr[band, :] + recv_scr[0, band, :].astype(F32) + recv_scr[1, band, :].astype(F32)
            g = jnp.pad(g, ((0, 0), (0, pad))).T[0:C]
            rows = pl.ds(q, C, stride=8)
            d, mn, vn = _adamw_math(w_ref[rows, :], g, m_ref[rows, :], v_ref[rows, :])
            go_ref[rows, :] = g
            d_ref[rows, :] = d
            mo_ref[rows, :] = mn
            vo_ref[rows, :] = vn

    vmem = pl.BlockSpec(memory_space=pltpu.VMEM)
    return pl.pallas_call(
        body, name="adamw_w_in", out_shape=[jax.ShapeDtypeStruct(w.shape, F32)] * 4,
        in_specs=[pl.BlockSpec(memory_space=pltpu.SMEM)] + [pl.BlockSpec(memory_space=pl.ANY)] * 4 + [vmem] * 3,
        out_specs=[vmem] * 4,
        scratch_shapes=[pltpu.VMEM(own_lo.shape, F32), pltpu.VMEM(recv_lo.shape, BF16),
                        pltpu.SemaphoreType.DMA((2, len(bands)))],
        compiler_params=_params(),
    )(is_lo, own_lo, own_hi, recv_lo, recv_hi, w, m, v)


SMALL = ("conv_b", "dt_bias", "a_log", "d_skip", "ssd_norm_w", "attn_sinks", "ln_g", "ln_b")


def _adamw_small(gathered, params):
    n_p = len(SMALL)

    def body(*refs):
        acc = []
        for r in refs[:5]:
            t = r[0]
            for k in range(1, N_DEV):
                t = t + r[k]
            acc.append(t)
        head, conv, norm, scal, sink = acc
        grads = dict(conv_b=conv[4:5, :], dt_bias=scal[0:1, 0:N_HEADS], a_log=scal[1:2, 0:N_HEADS],
                     d_skip=scal[2:3, 0:N_HEADS], ssd_norm_w=norm[0:1, :], attn_sinks=sink[0:1, 0:N_HEADS],
                     ln_g=head[0:1, :], ln_b=head[1:2, :])
        wmv = refs[5:5 + 3 * n_p]
        outs = refs[5 + 3 * n_p:]
        outs[0][...] = head[3:4, 0:1]
        outs[1][...] = conv[0:4, :]
        for i, name in enumerate(SMALL):
            w_ref, m_ref, v_ref = wmv[3 * i:3 * i + 3]
            g = grads[name]
            d, mn, vn = _adamw_math(w_ref[...], g, m_ref[...], v_ref[...])
            for o_ref, val in zip(outs[2 + 4 * i:6 + 4 * i], (g, d, mn, vn)):
                o_ref[...] = val

    flat = [a for name in SMALL for a in params[name]]
    out_shape = [jax.ShapeDtypeStruct((1, 1), F32), jax.ShapeDtypeStruct((4, D_XBC), F32)]
    for name in SMALL:
        out_shape += [jax.ShapeDtypeStruct(params[name][0].shape, F32)] * 4
    res = pl.pallas_call(body, name="adamw_small", out_shape=out_shape, compiler_params=_params())(*gathered, *flat)
    return res[0], res[1], {name: res[2 + 4 * i:6 + 4 * i] for i, name in enumerate(SMALL)}


def _adamw_plain(g, w, m, v):
    def body(g_ref, w_ref, m_ref, v_ref, d_ref, mo_ref, vo_ref):
        d, mn, vn = _adamw_math(w_ref[...], g_ref[...], m_ref[...], v_ref[...])
        d_ref[...] = d
        mo_ref[...] = mn
        vo_ref[...] = vn

    return pl.pallas_call(
        body, name="adamw_conv_w", out_shape=[jax.ShapeDtypeStruct(w.shape, F32)] * 3,
        compiler_params=_params(),
    )(g, w, m, v)


def _lane_pattern(fn):
    return np.asarray([fn(l % HEAD_DIM) for l in range(128)], np.float32)


ROPE_INV = _lane_pattern(lambda r: ROPE_THETA ** (-2.0 * (r % 8) / ROPE_DIM) if r < ROPE_DIM else 0.0)


def _rope_tables(pos_ref, inv_ref, tab_ref):
    lane = lax.broadcasted_iota(jnp.int32, (1, 128), 1) % HEAD_DIM
    upper = jnp.where((lane >= ROPE_DIM // 2) & (lane < ROPE_DIM), 1.0, 0.0)
    lower = jnp.where(lane < ROPE_DIM // 2, -1.0, 0.0)

    def block(r, carry):
        rows = pl.ds(pl.multiple_of(r * CHUNK, CHUNK), CHUNK)
        pos = jnp.broadcast_to(pos_ref[pl.ds(r, 1), :].astype(F32), (CHUNK, 128)).T
        ang = pos * inv_ref[...]
        sn = jnp.sin(ang)
        tab_ref[rows, 0:128] = jnp.cos(ang)
        tab_ref[rows, 128:256] = sn * upper
        tab_ref[rows, 256:384] = sn * lower
        return carry

    lax.fori_loop(0, pos_ref.shape[0], block, 0)


def _expansion():
    E = np.arange(1024)[None, :] // HEAD_DIM == np.arange(128)[:, None]
    return jnp.asarray(E, BF16), jnp.asarray(E.T, BF16)


def _ssd_args(conv_w, conv_b, dt_bias, a_log, d_skip, norm_w, E):
    return (conv_w, conv_b, dt_bias.reshape(-1), a_log.reshape(-1), d_skip.reshape(-1), norm_w, E)


def kernel(x, positions, w_in, conv_w, conv_b, dt_bias, a_log, d_skip, ssd_norm_w, attn_sinks, w_out, ln_g, ln_b, loss_target, m_w_in, m_conv_w, m_conv_b, m_dt_bias, m_a_log, m_d_skip, m_ssd_norm_w, m_attn_sinks, m_w_out, m_ln_g, m_ln_b, v_w_in, v_conv_w, v_conv_b, v_dt_bias, v_a_log, v_d_skip, v_ssd_norm_w, v_attn_sinks, v_w_out, v_ln_g, v_ln_b):
    me = _index(*_position())
    x0, target = x[0], loss_target[0]
    bf16_shard = lambda shape: jax.ShapeDtypeStruct(shape, BF16)
    E, ET = _expansion()
    sinks = attn_sinks.reshape(-1)

    w_all, tabs = _gather_w_in(w_in[0].astype(BF16), positions[0].reshape(-1, 128))
    w_ssd, w_att = _unpack_w_in(w_all)
    gather_conv_w = _Hosted([conv_w[0]], [jax.ShapeDtypeStruct((N_DEV,) + conv_w.shape[1:], F32)],
                            [_Flow("gather", 0, 0)])

    proj_ssd, proj_att, xb, conv_w_all = _in_proj(x0, w_ssd, w_att, tm=512, comm=gather_conv_w)
    conv_w_f = jnp.transpose(conv_w_all, (1, 0, 2)).reshape(4, D_XBC)
    ssd_args = _ssd_args(conv_w_f, conv_b, dt_bias, a_log, d_skip, ssd_norm_w, E)
    gather_w_out = _Hosted([w_out[0].astype(BF16)], [bf16_shard((N_DEV, 256, D_MODEL))], [_Flow("gather", 0, 0)])
    y, ypre, hprev, pre, w_out_all = _mixer_forward(proj_ssd, proj_att, tabs, sinks, *ssd_args, comm=gather_w_out)
    w_out_f = w_out_all.reshape(2 * D_MODEL, D_MODEL)
    dr, dy, acc_head = _head(y, x0, target, w_out_f, ln_g, ln_b, tm=512)

    dw_out_bf16, own_out = _dw_out(me.reshape(1).astype(jnp.int32), y, dr, tl=1024)
    send_out = _Hosted([dw_out_bf16.reshape(N_DEV, 256, D_MODEL)], [bf16_shard((N_DEV - 1, 256, D_MODEL))],
                       [_Flow("exchange", 0, 0)])
    d_ssd, acc_cw, acc_w, acc_s, recv_out = _ssd_backward(proj_ssd, hprev, ypre, pre, dy, *ssd_args, ET, comm=send_out)
    stack_lo, dw_dt_block = _dw_in(xb, d_ssd, 0)
    d_att, dsink, own_lo, recv_lo = _swa_backward(proj_att, tabs, sinks, dy, reduce=_OwnerReduce(stack_lo, 0))
    (stack_hi,) = _dw_in(xb, d_att, 1, dw_dt_block)
    accs = [acc_head, acc_cw, acc_w, acc_s, dsink]
    gather_accs = _Hosted(accs, [jax.ShapeDtypeStruct((N_DEV,) + a.shape, F32) for a in accs],
                          [_Flow("gather", i, i) for i in range(5)])
    dx, *gathered, own_hi, recv_hi = _input_gradient(d_ssd, d_att, w_ssd, w_att, dr, tm=256, comm=gather_accs,
                                                     reduce=_OwnerReduce(stack_hi, 1))
    is_lo = (me < 4).reshape(1).astype(jnp.int32)

    g_in, d_in, nm_in, nv_in = [_from_minor_rows_view(r) for r in _adamw_w_in(
        is_lo, own_lo, own_hi, recv_lo, recv_hi, _minor_rows_view(w_in), _minor_rows_view(m_w_in), _minor_rows_view(v_w_in))]
    g_out, d_out, nm_out, nv_out = _adamw_shard(own_out, recv_out, w_out[0], m_w_out[0], v_w_out[0],
                                                rows=256, name="adamw_w_out")
    loss, g_conv_w, small = _adamw_small(gathered, dict(
        conv_b=(conv_b, m_conv_b, v_conv_b), dt_bias=(dt_bias, m_dt_bias, v_dt_bias), a_log=(a_log, m_a_log, v_a_log),
        d_skip=(d_skip, m_d_skip, v_d_skip), ssd_norm_w=(ssd_norm_w, m_ssd_norm_w, v_ssd_norm_w),
        attn_sinks=(attn_sinks, m_attn_sinks, v_attn_sinks), ln_g=(ln_g, m_ln_g, v_ln_g), ln_b=(ln_b, m_ln_b, v_ln_b)))
    g_cw = lax.dynamic_slice_in_dim(g_conv_w, me * (D_XBC // N_DEV), D_XBC // N_DEV, axis=1)
    d_cw, nm_cw, nv_cw = _adamw_plain(g_cw, conv_w[0], m_conv_w[0], v_conv_w[0])

    def leaves(i, big_in, cw, big_out):
        mid = [small[k][i] for k in ("conv_b", "dt_bias", "a_log", "d_skip", "ssd_norm_w", "attn_sinks")]
        return [big_in, cw[None]] + mid + [big_out[None], small["ln_g"][i], small["ln_b"][i]]

    return (loss.reshape(()), dx[None], *leaves(0, g_in, g_cw, g_out), *leaves(1, d_in, d_cw, d_out),
            *leaves(2, nm_in, nm_cw, nm_out), *leaves(3, nv_in, nv_cw, nv_out))
```

```python
import jax
import jax.numpy as jnp
from jax import lax
from jax.experimental import pallas as pl
from jax.experimental.pallas import tpu as pltpu
import numpy as np

F32 = jnp.float32
BF16 = jnp.bfloat16
_MXU = jnp.bfloat16

N_DEV = 8
D_MODEL = 1024
D_SSD = 1024
D_ATT = 1024
HEAD_DIM = 64
N_HEADS = 16
SSD_GROUPS = 2
KV_HEADS = 4
CHUNK = 128
D_XBC = 1536
D_IN_PROJ = 5136
ROPE_DIM = 16
ROPE_THETA = 500000.0
ALPHA = (2.0 * 1) ** 0.25
LN_EPS = 1e-5
RMS_EPS = 1e-5
ATT_SCALE = HEAD_DIM ** -0.5
NEG = -1e30

S_Z, S_XS, S_B, S_C, S_DT, S_W = 0, 1024, 2048, 2304, 2560, 2816
N_SSD_REAL = 2576
A_Q, A_K, A_V, A_G, A_W = 0, 1024, 1280, 1536, 2560

ADAM_LR = 0.001
ADAM_B1 = 0.9
ADAM_B2 = 0.999
ADAM_EPS = 1e-08
ADAM_WD = 0.01
ADAM_STEP = 10

VMEM_LIMIT = 48 * 1024 * 1024
MESH = pl.DeviceIdType.MESH


def _params(sem=None):
    return pltpu.CompilerParams(dimension_semantics=sem, vmem_limit_bytes=VMEM_LIMIT)


def _mm(a, b):
    return jnp.dot(a.astype(_MXU), b.astype(_MXU), preferred_element_type=F32)


def _mm_nt(a, b):
    return lax.dot_general(a.astype(_MXU), b.astype(_MXU), (((1,), (1,)), ((), ())),
                           preferred_element_type=F32)


def _mm_tn(a, b):
    return lax.dot_general(a.astype(_MXU), b.astype(_MXU), (((0,), (0,)), ((), ())),
                           preferred_element_type=F32)


def _split3(v):
    hi = v.astype(BF16)
    r = v - hi.astype(F32)
    mid = r.astype(BF16)
    lo = (r - mid.astype(F32)).astype(BF16)
    return hi, mid, lo


def _mm_exact_r(v, p01):
    hi, mid, lo = _split3(v)
    d = lambda a: jnp.dot(a, p01, preferred_element_type=F32)
    return d(hi) + d(mid) + d(lo)


def _mm_exact_l(p01, v):
    hi, mid, lo = _split3(v)
    d = lambda a: jnp.dot(p01, a, preferred_element_type=F32)
    return d(hi) + d(mid) + d(lo)


def _mm_2pass_r(v, p01):
    hi = v.astype(BF16)
    lo = (v - hi.astype(F32)).astype(BF16)
    return jnp.dot(hi, p01, preferred_element_type=F32) + jnp.dot(lo, p01, preferred_element_type=F32)


def _sigmoid(x):
    return 1.0 / (1.0 + jnp.exp(-x))


def _softplus(x):
    e = jnp.exp(-jnp.abs(x))
    u = 1.0 + e
    log1p = jnp.where(u == 1.0, e, jnp.log(u) * (e / (u - 1.0)))
    return jnp.maximum(x, 0.0) + log1p


def _rows8(rows):
    n = rows[0].shape[1]
    rid = lax.broadcasted_iota(jnp.int32, (8, n), 0)
    out = jnp.zeros((8, n), F32)
    for k, r in enumerate(rows):
        out = out + jnp.where(rid == k, r, 0.0)
    return out


def _colsum(a):
    return jnp.sum(a, axis=0, keepdims=True)


def _in_proj(x, w_ssd, w_att, *, tm, comm=None):
    L, K = x.shape
    steps = L // tm

    def body(x_ref, ws_ref, wa_hbm, ps_hbm, pa_ref, xb_ref, wa_scr, ps_buf, wa_sem, ps_sems):
        i = pl.program_id(0)
        slot = i % 2
        load_wa = pltpu.make_async_copy(wa_hbm, wa_scr, wa_sem)

        def store(s, step):
            rows = pl.ds(pl.multiple_of(step * tm, tm), tm)
            return pltpu.make_async_copy(ps_buf.at[s], ps_hbm.at[rows, :], ps_sems.at[s])

        pl.when(i == 0)(load_wa.start)
        pl.when(i >= 2)(store(slot, i - 2).wait)
        xb = x_ref[...].astype(_MXU)
        xb_ref[...] = xb
        ps_buf[slot] = jnp.dot(xb, ws_ref[...], preferred_element_type=F32)
        store(slot, i).start()
        pl.when(i == 0)(load_wa.wait)
        pa_ref[...] = jnp.dot(xb, wa_scr[...], preferred_element_type=F32)

        @pl.when(i == steps - 1)
        def _():
            store(slot, i).wait()
            if steps > 1:
                store(1 - slot, i - 1).wait()

    row = lambda w: pl.BlockSpec((tm, w), lambda i: (i, 0))
    any_spec = pl.BlockSpec(memory_space=pl.ANY)
    return _call(
        body, comm, name="in_proj", grid=(steps,),
        in_specs=[row(K), pl.BlockSpec(w_ssd.shape, lambda i: (0, 0), pipeline_mode=pl.Buffered(1)), any_spec],
        out_specs=[any_spec, row(A_W), row(K)],
        out_shape=[jax.ShapeDtypeStruct((L, S_W), F32), jax.ShapeDtypeStruct((L, A_W), F32),
                   jax.ShapeDtypeStruct((L, K), _MXU)],
        scratch_shapes=[pltpu.VMEM(w_att.shape, w_att.dtype), pltpu.VMEM((2, tm, S_W), F32),
                        pltpu.SemaphoreType.DMA, pltpu.SemaphoreType.DMA((2,))],
        args=(x, w_ssd, w_att))


def _matmul_tn(a, g, *, tl, tn, name, emit_bf16=False):
    L, M = a.shape
    N = g.shape[1]
    last = L // tl - 1

    def body(a_ref, g_ref, o_ref, *rest):
        @pl.when(pl.program_id(1) == 0)
        def _():
            o_ref[...] = jnp.zeros_like(o_ref)

        o_ref[...] += _mm_tn(a_ref[...], g_ref[...])
        if emit_bf16:
            @pl.when(pl.program_id(1) == last)
            def _():
                rest[0][...] = o_ref[...].astype(BF16)

    spec = pl.BlockSpec((M, tn), lambda j, l: (0, j))
    res = pl.pallas_call(
        body, name=name, grid=(N // tn, L // tl),
        in_specs=[pl.BlockSpec((tl, M), lambda j, l: (l, 0)), pl.BlockSpec((tl, tn), lambda j, l: (l, j))],
        out_specs=[spec, spec] if emit_bf16 else [spec],
        out_shape=[jax.ShapeDtypeStruct((M, N), F32)] + ([jax.ShapeDtypeStruct((M, N), BF16)] if emit_bf16 else []),
        compiler_params=_params(("arbitrary", "arbitrary")),
    )(a, g)
    return res if emit_bf16 else res[0]


def _position():
    return lax.axis_index("x"), lax.axis_index("y"), lax.axis_index("c")


def _index(px, py, pc):
    return 4 * px + 2 * py + pc


def _flip(pos, k):
    x, y, c = pos
    return ((1 - x) if (k >> 2) & 1 else x, (1 - y) if (k >> 1) & 1 else y, (1 - c) if k & 1 else c)


def _remote(src, dst, send_sem, recv_sem, peer):
    return pltpu.make_async_remote_copy(src_ref=src, dst_ref=dst, send_sem=send_sem, recv_sem=recv_sem,
                                        device_id=peer, device_id_type=MESH)


class _Flow:
    def __init__(self, kind, operand, result):
        self.kind, self.operand, self.result = kind, operand, result


class _Hosted:
    def __init__(self, operands, out_shapes, flows):
        self.operands, self.out_shapes, self.flows = operands, out_shapes, flows

    def plan(self, ins, outs, send_sems, recv_sems, local_sems):
        me = _position()
        mi = _index(*me)
        sends, recvs, locals_ = [], [], []
        for row, f in enumerate(self.flows):
            src, dst = ins[f.operand], outs[f.result]
            for k in range(1, N_DEV):
                peer = _flip(me, k)
                sems = (send_sems.at[row, k - 1], recv_sems.at[row, k - 1])
                if f.kind == "exchange":
                    sends.append(_remote(src.at[_index(*peer)], dst.at[k - 1], *sems, peer))
                    recvs.append(sends[-1])
                else:
                    sends.append(_remote(src, dst.at[mi], *sems, peer))
                    recvs.append(_remote(src, dst.at[_index(*peer)], *sems, peer))
            if f.kind == "gather":
                locals_.append(pltpu.make_async_copy(src, dst.at[mi], local_sems.at[row]))

        def start():
            for cp in locals_ + sends:
                cp.start()

        def wait():
            for cp in recvs:
                cp.wait_recv()
            for cp in sends:
                cp.wait_send()
            for cp in locals_:
                cp.wait()

        return start, wait


class _OwnerReduce:
    FIRST_STEP, SECOND_STEP, SEND_STEPS, REDUCE_STEPS = 2, 4, (2, 4, 6, 8), (5, 8, 10, 13)

    def __init__(self, stack, target_x):
        self.stack, self.target_x = stack, target_x
        block = stack.shape[1:]
        self.chunks = len(self.SEND_STEPS)
        self.chunk_rows = block[0] // self.chunks
        self.out_shapes = [jax.ShapeDtypeStruct(block, F32), jax.ShapeDtypeStruct((2,) + block, BF16)]
        self.out_specs = [pl.BlockSpec(block, lambda *_: (0, 0), pipeline_mode=pl.Buffered(1)),
                          pl.BlockSpec(memory_space=pl.ANY)]
        dma = pltpu.SemaphoreType.DMA
        self.scratch_shapes = ([pltpu.VMEM((2,) + block, F32)] * 2 + [pltpu.VMEM(block, BF16)] * 3
                               + [dma((self.chunks,))] * 4 + [dma((2,))] * 3)

    def plan(self, i, steps, stack_ref, own_ref, recv_ref, scratch):
        assert self.FIRST_STEP <= self.SEND_STEPS[0] and self.SECOND_STEP < self.REDUCE_STEPS[0] < steps - 1
        (theirs_scr, mine_scr, first_scr, across_scr, out_scr, y_send_sems, y_recv_sems, x_send_sems, x_recv_sems,
         swap_send_sems, swap_recv_sems, mine_sems) = scratch
        x, y, c = _position()
        owners_side = x == self.target_x
        other_side = x != self.target_x
        sibling, across, owner = (x, y, 1 - c), (x, 1 - y, c), (self.target_x, y, c)
        order = (1 - y, y)
        swaps = [_remote(stack_ref.at[2 * order[j] + (1 - c)], theirs_scr.at[j], swap_send_sems.at[j], swap_recv_sems.at[j],
                         sibling) for j in range(2)]
        mine = [pltpu.make_async_copy(stack_ref.at[2 * order[j] + c], mine_scr.at[j], mine_sems.at[j]) for j in range(2)]
        chunks = range(self.chunks)
        part = [pl.ds(j * self.chunk_rows, self.chunk_rows) for j in chunks]
        y_sems = lambda j: (y_send_sems.at[j], y_recv_sems.at[j])
        to_neighbour = [_remote(first_scr.at[part[j]], across_scr.at[part[j]], *y_sems(j), across) for j in chunks]
        to_owner_y = [_remote(first_scr.at[part[j]], recv_ref.at[0, part[j]], *y_sems(j), across) for j in chunks]
        to_owner_x = [_remote(out_scr.at[part[j]], recv_ref.at[1, part[j]], x_send_sems.at[j], x_recv_sems.at[j], owner)
                      for j in chunks]

        def before():
            @pl.when(i <= 1)
            def _():
                @pl.when(i == 0)
                def _():
                    for cp in [swaps[0]] + mine:
                        cp.start()

                pl.when(i == 1)(swaps[1].start)

        def after():
            @pl.when(i <= self.REDUCE_STEPS[-1])
            def _():
                @pl.when(i == self.FIRST_STEP)
                def _():
                    swaps[0].wait_recv()
                    mine[0].wait()
                    first_scr[...] = (mine_scr[0] + theirs_scr[0]).astype(first_scr.dtype)

                for j in chunks:
                    pl.when((i == self.SEND_STEPS[j]) & other_side)(to_neighbour[j].start)
                    pl.when((i == self.SEND_STEPS[j]) & owners_side)(to_owner_y[j].start)

                @pl.when(i == self.SECOND_STEP)
                def _():
                    swaps[1].wait_recv()
                    mine[1].wait()
                    t = mine_scr[1] + theirs_scr[1]
                    own_ref[...] = t
                    mine_scr[1] = t

                for j in chunks:
                    @pl.when((i == self.REDUCE_STEPS[j]) & other_side)
                    def _(j=j):
                        to_neighbour[j].wait_recv()
                        t = mine_scr[1, part[j], :] + across_scr[part[j], :].astype(F32)
                        out_scr[part[j], :] = t.astype(out_scr.dtype)
                        to_owner_x[j].start()

            @pl.when(i == steps - 1)
            def _():
                for cp in swaps:
                    cp.wait_send()
                for j in chunks:
                    @pl.when(other_side)
                    def _(j=j):
                        to_neighbour[j].wait_send()
                        to_owner_x[j].wait_send()

                    @pl.when(owners_side)
                    def _(j=j):
                        to_owner_y[j].wait_send()
                        to_owner_y[j].wait_recv()
                        to_owner_x[j].wait_recv()

        return before, after


def _call(body, comm, *, name, grid, in_specs, out_specs, out_shape, scratch_shapes, args, reduce=None):
    semantics = ("arbitrary",) * len(grid)
    if comm is None and reduce is None:
        return pl.pallas_call(body, name=name, grid=grid, in_specs=in_specs, out_specs=out_specs, out_shape=out_shape,
                              scratch_shapes=scratch_shapes, compiler_params=_params(semantics))(*args)
    n_in, n_out, n_scr = len(args), len(out_shape), len(scratch_shapes)
    c_operands, c_shapes, flows = (comm.operands, comm.out_shapes, comm.flows) if comm else ([], [], [])
    c_in, c_out, rows = len(c_operands), len(c_shapes), max(len(flows), 1)
    r_in = 0 if reduce is None else 1

    def hosted(*refs):
        ins, refs = refs[:n_in], refs[n_in:]
        cins, refs = refs[:c_in], refs[c_in:]
        rins, refs = refs[:r_in], refs[r_in:]
        outs, refs = refs[:n_out], refs[n_out:]
        couts, refs = refs[:c_out], refs[c_out:]
        routs, refs = refs[:2 * r_in], refs[2 * r_in:]
        scr, refs = refs[:n_scr], refs[n_scr:]
        (send_sems, recv_sems, local_sems), r_scr = refs[:3], refs[3:]
        ids = [pl.program_id(d) for d in range(len(grid))]
        first, last = ids[0] == 0, ids[0] == grid[0] - 1
        for d in range(1, len(grid)):
            first, last = first & (ids[d] == 0), last & (ids[d] == grid[d] - 1)
        before = after = lambda: None
        if reduce is not None:
            before, after = reduce.plan(ids[0], grid[0], rins[0], *routs, r_scr)
        if comm is not None:
            start, wait = comm.plan(cins, couts, send_sems, recv_sems, local_sems)
            pl.when(first)(start)
        before()
        body(*ins, *outs, *scr)
        after()
        if comm is not None:
            pl.when(last)(wait)

    any_spec = pl.BlockSpec(memory_space=pl.ANY)
    sems = [pltpu.SemaphoreType.DMA((rows, N_DEV - 1)), pltpu.SemaphoreType.DMA((rows, N_DEV - 1)),
            pltpu.SemaphoreType.DMA((rows,))]
    r_operands, r_specs, r_shapes, r_scratch = ([reduce.stack], reduce.out_specs, reduce.out_shapes,
                                                reduce.scratch_shapes) if reduce else ([], [], [], [])
    return pl.pallas_call(
        hosted, name=name, grid=grid, in_specs=list(in_specs) + [any_spec] * (c_in + r_in),
        out_specs=list(out_specs) + [any_spec] * c_out + r_specs, out_shape=list(out_shape) + list(c_shapes) + r_shapes,
        scratch_shapes=list(scratch_shapes) + sems + r_scratch,
        compiler_params=_params(semantics))(*args, *c_operands, *r_operands)


def _head_row(ref, width, rep):
    hid = lax.broadcasted_iota(jnp.int32, (1, width), 1) // rep
    row = jnp.zeros((1, width), F32)
    for h in range(N_HEADS):
        row = jnp.where(hid == h, ref[h], row)
    return row


def _rows_from_above(u_b, s, ext_scr, row, col):
    down = (row - col == s).astype(_MXU)
    return jnp.concatenate([ext_scr[8 - s:16 - s, :], jnp.dot(down, u_b, preferred_element_type=F32)[8:128]], axis=0)


def _ssd_recompute(first, p_ref, halo_ref, cw_ref, cb_ref, dtb_ref, alog_ref, e_ref, ext_scr, pre=None):
    row = lax.broadcasted_iota(jnp.int32, (128, 128), 0)
    col = lax.broadcasted_iota(jnp.int32, (128, 128), 1)
    ext_scr[0:8, :] = jnp.where(first, 0.0, halo_ref[:, S_XS:S_DT])
    if pre is not None:
        ext_scr[8:16, :] = p_ref[0:8, S_XS:S_DT]
    else:
        ext_scr[8:136, :] = p_ref[:, S_XS:S_DT]
        cw = cw_ref[...]
        pre = (cb_ref[0:1, :] + cw[3:4, :] * ext_scr[8:136, :] + cw[2:3, :] * ext_scr[7:135, :]
               + cw[1:2, :] * ext_scr[6:134, :] + cw[0:1, :] * ext_scr[5:133, :])
    sg = _sigmoid(pre)
    act = pre * sg
    lane = lax.broadcasted_iota(jnp.int32, (1, 128), 1)
    A = jnp.where(lane < N_HEADS, -jnp.exp(_head_row(alog_ref, 128, 1)), 0.0)
    raw = p_ref[:, S_DT:S_DT + 128] + _head_row(dtb_ref, 128, 1)
    dt = _softplus(raw)
    dA = dt * A
    tril = (row >= col).astype(BF16)
    acs = _mm_exact_l(tril, dA)
    last = acs[127:128, :]
    ds = jnp.exp(last - acs)
    eo = jnp.exp(acs)
    E = e_ref[...]
    ex = _mm_2pass_r(jnp.concatenate([dt, ds, eo], axis=0), E)
    dt_e, ds_e, eo_e = ex[0:128], ex[128:256], ex[256:384]
    xs_c = act[:, 0:1024]
    X = xs_c * dt_e
    return dict(pre=pre, sg=sg, xs_c=xs_c, Bc=act[:, 1024:1280], Cc=act[:, 1280:1536], A=A, raw=raw, dt=dt,
                acs=acs, acsT=acs.T, eo_e=eo_e, ds_e=ds_e, dt_e=dt_e, cd_e=eo_e[127:128, :],
                X=X, Xd=X * ds_e, row=row, col=col)


def _split_halves(t):
    lo = _lo_half(CHUNK)
    return jnp.concatenate([jnp.where(lo, t, 0.0), jnp.where(lo, 0.0, t)], axis=0)


def _ssd_core(R, hprev):
    causal = R["row"] >= R["col"]
    acs, acsT, X = R["acs"], R["acsT"], R["X"]
    ydiag, yoff, snew = [], [], []
    for g in range(SSD_GROUPS):
        Bg = R["Bc"][:, g * 128:(g + 1) * 128]
        Cg = R["Cc"][:, g * 128:(g + 1) * 128]
        cols = slice(g * 512, (g + 1) * 512)
        CB = _mm_nt(Cg, Bg)
        snew.append(_mm_tn(Bg, R["Xd"][:, cols]))
        yoff.append(_mm(Cg, hprev[:, cols]))
        for j in range(4):
            h0 = g * 8 + 2 * j
            ms = [CB * jnp.exp(jnp.where(causal, acs[:, h:h + 1] - acsT[h:h + 1, :], NEG)) for h in (h0, h0 + 1)]
            ydiag.append(_mm(jnp.concatenate(ms, axis=1), _split_halves(X[:, h0 * HEAD_DIM:h0 * HEAD_DIM + 128])))
    Y = jnp.concatenate(ydiag, axis=1) + jnp.concatenate(yoff, axis=1) * R["eo_e"]
    return Y, jnp.concatenate(snew, axis=1)


def _ssd_forward_step(p_ref, halo_ref, cw_ref, cb_ref, dtb_ref, alog_ref, dsk_ref, nw_ref, e_ref,
                      y_ref, ypre_ref, hprev_ref, pre_ref, h_scr, ext_scr):
    c = pl.program_id(0)
    first = c == 0

    @pl.when(first)
    def _():
        h_scr[...] = jnp.zeros_like(h_scr)

    R = _ssd_recompute(first, p_ref, halo_ref, cw_ref, cb_ref, dtb_ref, alog_ref, e_ref, ext_scr)
    hprev = h_scr[...]
    hprev_ref[...] = hprev
    pre_ref[...] = R["pre"]
    Y, snew = _ssd_core(R, hprev)
    h_scr[...] = hprev * R["cd_e"] + snew
    Y = Y + _head_row(dsk_ref, D_SSD, HEAD_DIM) * R["xs_c"]
    ypre_ref[...] = Y
    z = p_ref[:, S_Z:S_Z + 1024]
    yf = Y * (z * _sigmoid(z))
    outs = []
    for g in range(SSD_GROUPS):
        yg = yf[:, g * 512:(g + 1) * 512]
        r = lax.rsqrt(jnp.mean(yg * yg, axis=-1, keepdims=True) + RMS_EPS)
        outs.append(yg * r)
    y_ref[:, 0:D_SSD] = (jnp.concatenate(outs, axis=1) * nw_ref[0:1, :]).astype(y_ref.dtype)


def _ssd_backward(proj_ssd, hprev_all, ypre, pre, dy, conv_w, conv_b, dt_bias, a_log, d_skip, norm_w, E, ET, comm=None):
    L = proj_ssd.shape[0]
    nc = L // CHUNK

    def body(p_ref, halo_ref, hprev_ref, ypre_ref, pre_ref, dy_ref, cw_ref, cb_ref, dtb_ref, alog_ref, dsk_ref, nw_ref, e_ref,
             et_ref, dp_ref, acc_cw_ref, acc_w_ref, acc_s_ref, dh_scr, ext_scr, ext2_scr, nxt_scr):
        i = pl.program_id(0)
        c = nc - 1 - i
        first = c == 0

        @pl.when(i == 0)
        def _():
            dh_scr[...] = jnp.zeros_like(dh_scr)
            nxt_scr[...] = jnp.zeros_like(nxt_scr)
            acc_cw_ref[...] = jnp.zeros_like(acc_cw_ref)
            acc_w_ref[...] = jnp.zeros_like(acc_w_ref)
            acc_s_ref[...] = jnp.zeros_like(acc_s_ref)

        R = _ssd_recompute(first, p_ref, halo_ref, cw_ref, cb_ref, dtb_ref, alog_ref, e_ref, ext_scr, pre_ref[...])
        hprev = hprev_ref[...]
        xs_c, X, Xd = R["xs_c"], R["X"], R["Xd"]
        acs, acsT = R["acs"], R["acsT"]
        ET = et_ref[...]
        dsk = _head_row(dsk_ref, D_SSD, HEAD_DIM)
        Y = ypre_ref[...]

        z = p_ref[:, S_Z:S_Z + 1024]
        sz = _sigmoid(z)
        silz = z * sz
        yf = Y * silz
        dyv = dy_ref[...]
        nw = nw_ref[0:1, :]
        dyf_parts, dnw_parts = [], []
        for g in range(SSD_GROUPS):
            cols = slice(g * 512, (g + 1) * 512)
            yg = yf[:, cols]
            r = lax.rsqrt(jnp.mean(yg * yg, axis=-1, keepdims=True) + RMS_EPS)
            yn = yg * r
            dyn = dyv[:, cols] * nw[:, cols]
            dnw_parts.append(_colsum(dyv[:, cols] * yn))
            dyf_parts.append(r * (dyn - yn * jnp.mean(dyn * yn, axis=-1, keepdims=True)))
        dyf = jnp.concatenate(dyf_parts, axis=1)
        dY = dyf * silz
        dz = dyf * Y * (sz * (1.0 + z * (1.0 - sz)))

        dhn = dh_scr[...]
        dYo = dY * R["eo_e"]
        causal = R["row"] >= R["col"]
        dacs = jnp.zeros((128, 128), F32)
        dacs_t = jnp.zeros((128, 128), F32)
        dxdiag, dxd, dhprev, dBs, dCs, yoff = [], [], [], [], [], []
        for g in range(SSD_GROUPS):
            Bg = R["Bc"][:, g * 128:(g + 1) * 128]
            Cg = R["Cc"][:, g * 128:(g + 1) * 128]
            cols = slice(g * 512, (g + 1) * 512)
            CB = _mm_nt(Cg, Bg)
            dCB = jnp.zeros((128, 128), F32)
            for j in range(4):
                h0 = g * 8 + 2 * j
                pc = slice(h0 * HEAD_DIM, h0 * HEAD_DIM + 128)
                dYst = _split_halves(dY[:, pc])
                dMst = _mm_nt(dYst, X[:, pc])
                mts = []
                for a, h in enumerate((h0, h0 + 1)):
                    acol = acs[:, h:h + 1]
                    arow = acsT[h:h + 1, :]
                    Lm = jnp.exp(jnp.where(causal, acol - arow, NEG))
                    M = CB * Lm
                    dM = dMst[a * 128:(a + 1) * 128]
                    dCB = dCB + dM * Lm
                    G = dM * M
                    dacs = dacs + jnp.where(R["col"] == h, jnp.sum(G, axis=1, keepdims=True), 0.0)
                    dacs_t = dacs_t + jnp.where(R["row"] == h, jnp.sum(G, axis=0, keepdims=True), 0.0)
                    mts.append(M.T)
                dxdiag.append(_mm(jnp.concatenate(mts, axis=1), dYst))
            dS = dhn[:, cols]
            dxd.append(_mm(Bg, dS))
            yoff.append(_mm(Cg, hprev[:, cols]))
            dhprev.append(_mm_tn(Cg, dYo[:, cols]))
            dCs.append(_mm_nt(dYo[:, cols], hprev[:, cols]) + _mm(dCB, Bg))
            dBs.append(_mm_tn(dCB, Cg) + _mm_nt(Xd[:, cols], dS))
        Yoff = jnp.concatenate(yoff, axis=1) * R["eo_e"]
        dXd = jnp.concatenate(dxd, axis=1)
        dX = jnp.concatenate(dxdiag, axis=1) + dXd * R["ds_e"]
        t_state = dXd * Xd
        dacs = dacs + _mm_2pass_r(dY * Yoff - t_state, ET) - dacs_t.T
        v_last = _colsum(t_state + dhn * hprev * R["cd_e"])
        dlast = _mm_exact_r(jnp.broadcast_to(v_last, (8, 1024)), ET)[0:1, :]
        dacs = dacs + jnp.where(R["row"] == 127, dlast, 0.0)
        triu = (R["col"] >= R["row"]).astype(BF16)
        da = _mm_exact_l(triu, dacs)
        ddt = da * R["A"] + _mm(dX * xs_c, ET)
        ddt_raw = ddt * _sigmoid(R["raw"])
        dxs_c = dX * R["dt_e"] + dY * dsk
        dh_scr[...] = jnp.concatenate(dhprev, axis=1) + dhn * R["cd_e"]

        dact = jnp.concatenate([dxs_c] + dBs + dCs, axis=1)
        pre, sg = R["pre"], R["sg"]
        dpre = dact * (sg * (1.0 + pre * (1.0 - sg)))
        ext2_scr[0:8, :] = dpre[120:128, :]
        ext2_scr[8:16, :] = nxt_scr[...]
        nxt_scr[...] = dpre[0:8, :]
        cw = cw_ref[...]
        u_b, dpre_b = p_ref[:, S_XS:S_DT].astype(_MXU), dpre.astype(_MXU)
        dxbc = cw[3:4, :] * dpre
        taps = [_colsum(dpre * p_ref[:, S_XS:S_DT])]
        for s in (1, 2, 3):
            up = (R["col"] - R["row"] == s).astype(_MXU)
            d_s = jnp.concatenate([jnp.dot(up, dpre_b, preferred_element_type=F32)[0:120],
                                   ext2_scr[s:8 + s, :]], axis=0)
            dxbc = dxbc + cw[3 - s:4 - s, :] * d_s
            taps.append(_colsum(dpre * _rows_from_above(u_b, s, ext_scr, R["row"], R["col"])))
        acc_cw_ref[...] += _rows8(taps[::-1] + [_colsum(dpre)])
        acc_w_ref[...] += _rows8([jnp.concatenate(dnw_parts, axis=1), _colsum(dY * xs_c)])
        acc_s_ref[...] += _rows8([_colsum(ddt_raw), _colsum(da * R["dt"])])

        lane = lax.broadcasted_iota(jnp.int32, (128, 128), 1)
        dp_ref[:, S_Z:S_Z + 1024] = dz.astype(dp_ref.dtype)
        dp_ref[:, S_XS:S_DT] = dxbc.astype(dp_ref.dtype)
        dp_ref[:, S_DT:S_DT + 128] = jnp.where(lane < N_HEADS, ddt_raw, 0.0).astype(dp_ref.dtype)
        dp_ref[:, S_DT + 128:S_W] = jnp.zeros((128, 128), dp_ref.dtype)

        @pl.when(i == nc - 1)
        def _():
            acc = acc_s_ref[...]
            dskip = _mm_exact_r(acc_w_ref[...], ET)[1:2, :]
            acc_s_ref[...] = _rows8([acc[0:1, :], acc[1:2, :] * R["A"], dskip])

    const = lambda shape: pl.BlockSpec(shape, lambda i: (0, 0))
    smem = pl.BlockSpec(memory_space=pltpu.SMEM)
    rev = lambda i: (nc - 1 - i, 0)
    return _call(
        body, comm, name="ssd_bwd", grid=(nc,),
        in_specs=[pl.BlockSpec((CHUNK, S_W), rev),
                  pl.BlockSpec((8, S_W), lambda i: (jnp.maximum((nc - 1 - i) * 16 - 1, 0), 0)),
                  pl.BlockSpec((128, 1024), rev),
                  pl.BlockSpec((CHUNK, D_SSD), rev),
                  pl.BlockSpec((CHUNK, D_XBC), rev),
                  pl.BlockSpec((CHUNK, D_SSD), rev),
                  const((4, D_XBC)), const((1, D_XBC)), smem, smem, smem, const((1, 1024)),
                  const((128, 1024)), const((1024, 128))],
        out_specs=[pl.BlockSpec((CHUNK, S_W), rev), const((8, D_XBC)), const((8, 1024)), const((8, 128))],
        out_shape=[jax.ShapeDtypeStruct((L, S_W), _MXU), jax.ShapeDtypeStruct((8, D_XBC), F32),
                   jax.ShapeDtypeStruct((8, 1024), F32), jax.ShapeDtypeStruct((8, 128), F32)],
        scratch_shapes=[pltpu.VMEM((128, 1024), F32), pltpu.VMEM((16, D_XBC), F32),
                        pltpu.VMEM((16, D_XBC), F32), pltpu.VMEM((8, D_XBC), F32)],
        args=(proj_ssd, proj_ssd, hprev_all, ypre, pre, dy, conv_w, conv_b, dt_bias, a_log, d_skip, norm_w, E, ET))


def _rope(t, tab):
    cos, sa, sb = tab[:, 0:128], tab[:, 128:256], tab[:, 256:384]
    outs = []
    for i in range(t.shape[1] // 128):
        tg = t[:, i * 128:(i + 1) * 128]
        outs.append(tg * cos + pltpu.roll(tg, 8, 1) * sa + pltpu.roll(tg, 120, 1) * sb)
    return jnp.concatenate(outs, axis=1)


def _rope_transposed(d, tab):
    cos, sa, sb = tab[:, 0:128], tab[:, 128:256], tab[:, 256:384]
    outs = []
    for i in range(d.shape[1] // 128):
        dg = d[:, i * 128:(i + 1) * 128]
        outs.append(dg * cos + pltpu.roll(dg * sa, 120, 1) + pltpu.roll(dg * sb, 8, 1))
    return jnp.concatenate(outs, axis=1)


def _lo_half(rows):
    return lax.broadcasted_iota(jnp.int32, (rows, 128), 1) < HEAD_DIM


def _native_half(rows, j):
    lo = _lo_half(rows)
    return lo if j % 2 == 0 else jnp.logical_not(lo)


def _kv_native(t, j):
    p = j // 2
    return jnp.where(_native_half(t.shape[0], j), t[:, p * 128:(p + 1) * 128], 0.0)


def _stack_heads(t, j):
    out = []
    for m in (2 * j, 2 * j + 1):
        pair = t[:, m * 128:(m + 1) * 128]
        swapped = pltpu.roll(pair, HEAD_DIM, 1)
        out += [pair, swapped] if j % 2 == 0 else [swapped, pair]
    return jnp.concatenate(out, axis=0)


def _unstack_heads(s, j):
    out = []
    for m in range(2):
        first, second = s[256 * m:256 * m + 128], s[256 * m + 128:256 * m + 256]
        if j % 2 == 0:
            out.append(first + pltpu.roll(second, HEAD_DIM, 1))
        else:
            out.append(pltpu.roll(first, HEAD_DIM, 1) + second)
    return jnp.concatenate(out, axis=1)


def _keep_native(r, j):
    return jnp.where(_native_half(r.shape[0], j), r, 0.0)


def _sink_row(sink_ref, j):
    hid = lax.broadcasted_iota(jnp.int32, (1, 4 * CHUNK), 1) // CHUNK
    row = jnp.zeros((1, 4 * CHUNK), F32)
    for hh in range(4):
        row = jnp.where(hid == hh, sink_ref[4 * j + hh], row)
    return row


def _from_current():
    si = lax.broadcasted_iota(jnp.int32, (CHUNK, 4 * CHUNK), 0)
    qi = lax.broadcasted_iota(jnp.int32, (CHUNK, 4 * CHUNK), 1) % CHUNK
    return si <= qi


def _fold(full, from_cur, pen=0.0):
    return jnp.where(from_cur, full[CHUNK:2 * CHUNK], full[0:CHUNK] + pen)


def _unfold(t, from_cur):
    c = jnp.where(from_cur, t, 0.0)
    return jnp.concatenate([t - c, c], axis=0)


def _softmax_sink(s, sink):
    mx = jnp.maximum(jnp.max(s, axis=0, keepdims=True), sink)
    p = jnp.exp(s - mx)
    esink = jnp.exp(sink - mx)
    inv = 1.0 / (jnp.sum(p, axis=0, keepdims=True) + esink)
    return p * inv, esink * inv


def _swa_inputs(blk, p_ref, prev_ref, tab_ref, ptab_ref):
    tab = tab_ref[...]
    qr = _rope(p_ref[:, A_Q:A_Q + 1024], tab) * ATT_SCALE
    kk = jnp.concatenate([_rope(prev_ref[:, 0:256], ptab_ref[...]), _rope(p_ref[:, A_K:A_K + 256], tab)], axis=0)
    vv = jnp.concatenate([prev_ref[:, 256:512], p_ref[:, A_V:A_V + 256]], axis=0)
    return tab, qr, kk, vv, jnp.where(blk > 0, 0.0, NEG)


def _swa_forward_step(sink_ref, p_ref, prev_ref, tab_ref, ptab_ref, y_ref):
    n = pl.program_id(0)
    _, qr, kk, vv, pen = _swa_inputs(n, p_ref, prev_ref, tab_ref, ptab_ref)
    from_cur = _from_current()
    outs = []
    for j in range(KV_HEADS):
        s = _fold(_mm_nt(_kv_native(kk, j), _stack_heads(qr, j)), from_cur, pen)
        P, _ = _softmax_sink(s, _sink_row(sink_ref, j))
        outs.append(_unstack_heads(_mm_tn(_unfold(P, from_cur), _kv_native(vv, j)), j))
    g = p_ref[:, A_G:A_G + 1024]
    y_ref[:, D_SSD:D_SSD + D_ATT] = (jnp.concatenate(outs, axis=1) * (g * _sigmoid(g))).astype(y_ref.dtype)


def _mixer_forward(proj_ssd, proj_att, tabs, sinks, conv_w, conv_b, dt_bias, a_log, d_skip, norm_w, E, comm=None):
    L = proj_ssd.shape[0]
    nc = L // CHUNK

    def body(p_ref, halo_ref, cw_ref, cb_ref, dtb_ref, alog_ref, dsk_ref, nw_ref, e_ref,
             sink_ref, pa_ref, prev_ref, tab_ref, ptab_ref, y_ref, ypre_ref, hprev_ref, pre_ref, h_scr, ext_scr):
        _ssd_forward_step(p_ref, halo_ref, cw_ref, cb_ref, dtb_ref, alog_ref, dsk_ref, nw_ref, e_ref,
                          y_ref, ypre_ref, hprev_ref, pre_ref, h_scr, ext_scr)
        _swa_forward_step(sink_ref, pa_ref, prev_ref, tab_ref, ptab_ref, y_ref)

    const = lambda shape: pl.BlockSpec(shape, lambda c: (0, 0))
    smem = pl.BlockSpec(memory_space=pltpu.SMEM)
    rows = lambda w: pl.BlockSpec((CHUNK, w), lambda c: (c, 0))
    return _call(
        body, comm, name="mixer_fwd", grid=(nc,),
        in_specs=[rows(S_W), pl.BlockSpec((8, S_W), lambda c: (jnp.maximum(c * 16 - 1, 0), 0)),
                  const((4, D_XBC)), const((1, D_XBC)), smem, smem, smem, const((1, 1024)), const((128, 1024)),
                  smem, rows(A_W), pl.BlockSpec((CHUNK, 512), lambda c: (jnp.maximum(c - 1, 0), 2)),
                  rows(384), pl.BlockSpec((CHUNK, 384), lambda c: (jnp.maximum(c - 1, 0), 0))],
        out_specs=[rows(D_SSD + D_ATT), rows(D_SSD), pl.BlockSpec((128, 1024), lambda c: (c, 0)), rows(D_XBC)],
        out_shape=[jax.ShapeDtypeStruct((L, D_SSD + D_ATT), _MXU), jax.ShapeDtypeStruct((L, D_SSD), F32),
                   jax.ShapeDtypeStruct((nc * 128, 1024), F32), jax.ShapeDtypeStruct((L, D_XBC), F32)],
        scratch_shapes=[pltpu.VMEM((128, 1024), F32), pltpu.VMEM((136, D_XBC), F32)],
        args=(proj_ssd, proj_ssd, conv_w, conv_b, dt_bias, a_log, d_skip, norm_w, E,
              sinks, proj_att, proj_att, tabs, tabs))


def _swa_backward(proj_att, tabs, sinks, dy, reduce=None):
    L = proj_att.shape[0]
    nb = L // CHUNK

    def body(sink_ref, p_ref, prev_ref, tab_ref, ptab_ref, dy_ref, dp_ref, dsink_ref, carry_k, carry_v):
        i = pl.program_id(0)
        n = nb - 1 - i

        @pl.when(i == 0)
        def _():
            carry_k[...] = jnp.zeros_like(carry_k)
            carry_v[...] = jnp.zeros_like(carry_v)
            dsink_ref[...] = jnp.zeros_like(dsink_ref)

        tab, qr, kk, vv, pen = _swa_inputs(n, p_ref, prev_ref, tab_ref, ptab_ref)
        from_cur = _from_current()
        g = p_ref[:, A_G:A_G + 1024]
        sgm = _sigmoid(g)
        dyv = dy_ref[...]
        do_all = dyv * (g * sgm)
        lane8 = lax.broadcasted_iota(jnp.int32, (8, 128), 1)
        hid = lax.broadcasted_iota(jnp.int32, (1, 4 * CHUNK), 1) // CHUNK
        o_parts, dq_parts = [], []
        dk_nat = [jnp.zeros((2 * CHUNK, 128), F32) for _ in range(2)]
        dv_nat = [jnp.zeros((2 * CHUNK, 128), F32) for _ in range(2)]
        dsink = jnp.zeros((8, 128), F32)
        for j in range(KV_HEADS):
            qs = _stack_heads(qr, j)
            kkb, vvb = _kv_native(kk, j), _kv_native(vv, j)
            P, psink = _softmax_sink(_fold(_mm_nt(kkb, qs), from_cur, pen), _sink_row(sink_ref, j))
            p_full = _unfold(P, from_cur)
            o_parts.append(_unstack_heads(_mm_tn(p_full, vvb), j))
            do_s = _stack_heads(do_all, j)
            dP = _fold(_mm_nt(vvb, do_s), from_cur)
            D = jnp.sum(P * dP, axis=0, keepdims=True)
            ds_full = _unfold(P * (dP - D), from_cur)
            sd = psink * D
            for hh in range(4):
                dsink = dsink + jnp.where(lane8 == 4 * j + hh, -jnp.sum(jnp.where(hid == hh, sd, 0.0)), 0.0)
            dq_parts.append(_unstack_heads(_mm_tn(ds_full, kkb), j) * ATT_SCALE)
            dk_nat[j // 2] = dk_nat[j // 2] + _keep_native(_mm(ds_full, qs), j)
            dv_nat[j // 2] = dv_nat[j // 2] + _keep_native(_mm(p_full, do_s), j)
        o = jnp.concatenate(o_parts, axis=1)
        dkk = jnp.concatenate(dk_nat, axis=1)
        dvv = jnp.concatenate(dv_nat, axis=1)
        out = dp_ref.dtype
        dp_ref[:, A_Q:A_Q + 1024] = _rope_transposed(jnp.concatenate(dq_parts, axis=1), tab).astype(out)
        dp_ref[:, A_K:A_K + 256] = _rope_transposed(dkk[CHUNK:2 * CHUNK] + carry_k[...], tab).astype(out)
        dp_ref[:, A_V:A_V + 256] = (dvv[CHUNK:2 * CHUNK] + carry_v[...]).astype(out)
        dp_ref[:, A_G:A_G + 1024] = (dyv * o * (sgm * (1.0 + g * (1.0 - sgm)))).astype(out)
        carry_k[...] = dkk[0:CHUNK]
        carry_v[...] = dvv[0:CHUNK]
        dsink_ref[...] += dsink

    rev = lambda i: (nb - 1 - i, 0)
    prev = lambda i: jnp.maximum(nb - 2 - i, 0)
    return _call(
        body, None, name="swa_bwd", grid=(nb,),
        in_specs=[pl.BlockSpec(memory_space=pltpu.SMEM),
                  pl.BlockSpec((CHUNK, A_W), rev),
                  pl.BlockSpec((CHUNK, 512), lambda i: (prev(i), 2)),
                  pl.BlockSpec((CHUNK, 384), rev),
                  pl.BlockSpec((CHUNK, 384), lambda i: (prev(i), 0)),
                  pl.BlockSpec((CHUNK, D_ATT), lambda i: (nb - 1 - i, 1))],
        out_specs=[pl.BlockSpec((CHUNK, A_W), rev), pl.BlockSpec((8, 128), lambda i: (0, 0))],
        out_shape=[jax.ShapeDtypeStruct((L, A_W), _MXU), jax.ShapeDtypeStruct((8, 128), F32)],
        scratch_shapes=[pltpu.VMEM((CHUNK, 256), F32), pltpu.VMEM((CHUNK, 256), F32)],
        args=(sinks, proj_att, proj_att, tabs, tabs, dy), reduce=reduce)


def _head(y, x, target, w_out, ln_g, ln_b, *, tm):
    L = x.shape[0]
    nsteps = L // tm

    def body(y_ref, x_ref, t_ref, wo_ref, g_ref, b_ref, dr_ref, dy_ref, acc_ref):
        i = pl.program_id(0)

        @pl.when(i == 0)
        def _():
            acc_ref[...] = jnp.zeros_like(acc_ref)

        r = ALPHA * x_ref[...] + _mm(y_ref[...], wo_ref[...])
        mu = jnp.mean(r, axis=-1, keepdims=True)
        d = r - mu
        rstd = lax.rsqrt(jnp.mean(d * d, axis=-1, keepdims=True) + LN_EPS)
        xh = d * rstd
        gam = g_ref[0:1, :]
        e = xh * gam + b_ref[0:1, :] - t_ref[...]
        dout = e * (1.0 / D_MODEL)
        dxh = dout * gam
        dr = rstd * (dxh - jnp.mean(dxh, axis=-1, keepdims=True)
                     - xh * jnp.mean(dxh * xh, axis=-1, keepdims=True))
        dr_ref[...] = dr
        dy_ref[...] = _mm_nt(dr, wo_ref[...])
        acc_ref[...] += _rows8([_colsum(dout * xh), _colsum(dout), _colsum(e * e) * (0.5 / D_MODEL)])

        @pl.when(i == nsteps - 1)
        def _():
            acc = acc_ref[...]
            tot = jnp.sum(acc[2:3, :])
            rid = lax.broadcasted_iota(jnp.int32, (8, 1024), 0)
            acc_ref[...] = jnp.where(rid == 3, tot, acc)

    const = lambda shape: pl.BlockSpec(shape, lambda i: (0, 0))
    row = lambda w: pl.BlockSpec((tm, w), lambda i: (i, 0))
    return pl.pallas_call(
        body, name="head", grid=(nsteps,),
        in_specs=[row(2048), row(1024), row(1024), const((2048, 1024)), const((1, 1024)), const((1, 1024))],
        out_specs=[row(1024), row(2048), const((8, 1024))],
        out_shape=[jax.ShapeDtypeStruct((L, D_MODEL), F32), jax.ShapeDtypeStruct((L, 2048), F32),
                   jax.ShapeDtypeStruct((8, 1024), F32)],
        compiler_params=_params(("arbitrary",)),
    )(y, x, target, w_out, ln_g, ln_b)


def _gather_w_in(w_shard, positions):
    R = w_shard.shape[0]
    halves = (pl.ds(0, R // 2), pl.ds(R // 2, R // 2))
    any_spec = pl.BlockSpec(memory_space=pl.ANY)
    vmem = pl.BlockSpec(memory_space=pltpu.VMEM)

    def body(in_ref, pos_ref, inv_ref, out_ref, tab_ref, tab_scr, send_sems, recv_sems, local_sem, tab_sem):
        x, y, c = _position()

        def slot(p, half=None):
            s = out_ref.at[_index(*p)]
            return s if half is None else s.at[halves[half]]

        def same_core(p):
            return (p[0], p[1], c)

        def other_core(p):
            return (p[0], p[1], 1 - c)

        me, xn, yn, dg = (x, y), (1 - x, y), (x, 1 - y), (1 - x, 1 - y)

        def copy(k, dst, to, src=None):
            return _remote(dst if src is None else src, dst, send_sems.at[k], recv_sems.at[k], to)

        local = pltpu.make_async_copy(in_ref, slot(same_core(me)), local_sem)
        local.start()
        own = [copy(0, slot(same_core(me)), other_core(me), in_ref), copy(1, slot(same_core(me)), same_core(xn), in_ref),
               copy(2, slot(same_core(me)), same_core(yn), in_ref)]
        for cp in own:
            cp.start()
        _rope_tables(pos_ref, inv_ref, tab_scr)
        tab_out = pltpu.make_async_copy(tab_scr, tab_ref, tab_sem)
        tab_out.start()
        copy(1, slot(same_core(xn)), same_core(xn)).wait_recv()
        passed = [copy(4, slot(same_core(xn), 1), same_core(yn)), copy(5, slot(same_core(xn)), other_core(me))]
        for cp in passed:
            cp.start()
        copy(2, slot(same_core(yn)), same_core(yn)).wait_recv()
        more = [copy(3, slot(same_core(yn), 0), same_core(xn)), copy(6, slot(same_core(yn)), other_core(me))]
        for cp in more:
            cp.start()
        passed += more
        for k, half in ((3, 0), (4, 1)):
            copy(k, slot(same_core(dg), half), same_core(xn)).wait_recv()
            fwd = copy(7 + half, slot(same_core(dg), half), other_core(me))
            fwd.start()
            passed.append(fwd)
        copy(0, slot(other_core(me)), other_core(me)).wait_recv()
        copy(5, slot(other_core(xn)), other_core(me)).wait_recv()
        copy(6, slot(other_core(yn)), other_core(me)).wait_recv()
        for half in (0, 1):
            copy(7 + half, slot(other_core(dg), half), other_core(me)).wait_recv()
        for cp in own + passed:
            cp.wait_send()
        local.wait()
        tab_out.wait()

    return pl.pallas_call(
        body, name="gather_w_in", in_specs=[any_spec, vmem, vmem], out_specs=[any_spec, any_spec],
        out_shape=[jax.ShapeDtypeStruct((N_DEV,) + w_shard.shape, w_shard.dtype),
                   jax.ShapeDtypeStruct((positions.size, 384), F32)],
        scratch_shapes=[pltpu.VMEM((positions.size, 384), F32), pltpu.SemaphoreType.DMA((9,)),
                        pltpu.SemaphoreType.DMA((9,)), pltpu.SemaphoreType.DMA, pltpu.SemaphoreType.DMA],
        compiler_params=_params(),
    )(w_shard, positions, jnp.asarray(ROPE_INV)[None, :])


def _input_gradient(d_ssd, d_att, w_ssd, w_att, dr, *, tm, comm=None, reduce=None):
    L = dr.shape[0]

    def body(ds_ref, da_ref, ws_ref, wa_ref, dr_ref, o_ref):
        o_ref[...] = ALPHA * dr_ref[...] + _mm_nt(ds_ref[...], ws_ref[...]) + _mm_nt(da_ref[...], wa_ref[...])

    row = lambda w: pl.BlockSpec((tm, w), lambda i: (i, 0))
    resident = lambda a: pl.BlockSpec(a.shape, lambda i: (0, 0), pipeline_mode=pl.Buffered(1))
    return _call(body, comm, name="dx", grid=(L // tm,),
                 in_specs=[row(S_W), row(A_W), resident(w_ssd), resident(w_att), row(D_MODEL)],
                 out_specs=[row(D_MODEL)], out_shape=[jax.ShapeDtypeStruct((L, D_MODEL), F32)],
                 scratch_shapes=[], args=(d_ssd, d_att, w_ssd, w_att, dr), reduce=reduce)


SHARD_COLS = D_IN_PROJ // N_DEV
SPLIT = N_SSD_REAL - 4 * SHARD_COLS
RELAYOUT_ROWS = 256


def _unpack_w_in(w_all):
    def body(g_ref, ws_ref, wa_ref):
        for j in range(4):
            ws_ref[:, SHARD_COLS * j:SHARD_COLS * (j + 1)] = g_ref[j]
        ws_ref[:, 4 * SHARD_COLS:N_SSD_REAL] = g_ref[4, :, 0:SPLIT]
        ws_ref[:, N_SSD_REAL:S_W] = jnp.zeros((RELAYOUT_ROWS, S_W - N_SSD_REAL), ws_ref.dtype)
        wa_ref[:, 0:SHARD_COLS - SPLIT] = g_ref[4, :, SPLIT:SHARD_COLS]
        for j in range(5, N_DEV):
            lo = SHARD_COLS * (j - 4) - SPLIT
            wa_ref[:, lo:lo + SHARD_COLS] = g_ref[j]

    return pl.pallas_call(
        body, name="unpack_w_in", grid=(D_MODEL // RELAYOUT_ROWS,),
        in_specs=[pl.BlockSpec((N_DEV, RELAYOUT_ROWS, SHARD_COLS), lambda i: (0, i, 0))],
        out_specs=[pl.BlockSpec((RELAYOUT_ROWS, S_W), lambda i: (i, 0)), pl.BlockSpec((RELAYOUT_ROWS, A_W), lambda i: (i, 0))],
        out_shape=[jax.ShapeDtypeStruct((D_MODEL, S_W), w_all.dtype), jax.ShapeDtypeStruct((D_MODEL, A_W), w_all.dtype)],
        compiler_params=_params(("arbitrary",)),
    )(w_all)


def _dw_in(xb, d, half, tail=None, *, tl=1024):
    L, N = d.shape
    steps = L // tl

    def body(x_ref, d_ref, *refs):
        if half == 0:
            p_ref, tail_ref, acc, p_scr, p_sems = refs
        else:
            t_ref, p_ref, acc, p_scr, p_sems = refs
        l = pl.program_id(0)

        @pl.when(l == 0)
        def _():
            acc[...] = jnp.zeros_like(acc)

        acc[...] += _mm_tn(x_ref[...], d_ref[...])

        @pl.when(l == steps - 1)
        def _():
            if half == 0:
                tail_ref[...] = acc[:, S_DT:S_W]
            outs = []
            for j in range(4):
                if half == 0:
                    pieces = [(0, acc[:, SHARD_COLS * j:SHARD_COLS * (j + 1)])]
                elif j == 0:
                    pieces = [(0, t_ref[:, 4 * SHARD_COLS - S_DT:N_SSD_REAL - S_DT]), (SPLIT, acc[:, 0:SHARD_COLS - SPLIT])]
                else:
                    lo = SHARD_COLS * j - SPLIT
                    pieces = [(0, acc[:, lo:lo + SHARD_COLS])]
                for off, blk in pieces:
                    p_scr[j, :, off:off + blk.shape[1]] = blk
                outs.append(pltpu.make_async_copy(p_scr.at[j], p_ref.at[j], p_sems.at[j]))
                outs[-1].start()
            for cp in outs:
                cp.wait()

    once = pl.Buffered(1)
    whole = lambda shape: pl.BlockSpec(shape, lambda l: (0,) * len(shape), pipeline_mode=once)
    in_specs = [pl.BlockSpec((tl, D_MODEL), lambda l: (l, 0)), pl.BlockSpec((tl, N), lambda l: (l, 0))]
    args = [xb, d]
    stack = jax.ShapeDtypeStruct((4, D_MODEL, SHARD_COLS), F32)
    out_shape, out_specs = [stack], [pl.BlockSpec(memory_space=pl.ANY)]
    if half == 0:
        out_shape.append(jax.ShapeDtypeStruct((D_MODEL, S_W - S_DT), F32))
        out_specs.append(whole(out_shape[-1].shape))
    else:
        in_specs.append(whole(tail.shape))
        args.append(tail)
    return pl.pallas_call(
        body, name="dw_in_%d" % half, grid=(steps,), in_specs=in_specs, out_specs=out_specs, out_shape=out_shape,
        scratch_shapes=[pltpu.VMEM((D_MODEL, N), F32), pltpu.VMEM(stack.shape, F32), pltpu.SemaphoreType.DMA((4,))],
        compiler_params=_params(("arbitrary",)),
    )(*args)


def _adamw_math(w, g, m, v):
    m = ADAM_B1 * m + (1.0 - ADAM_B1) * g
    v = ADAM_B2 * v + (1.0 - ADAM_B2) * (g * g)
    m_hat = m / (1.0 - ADAM_B1 ** ADAM_STEP)
    v_hat = v / (1.0 - ADAM_B2 ** ADAM_STEP)
    delta = -ADAM_LR * (m_hat / (jnp.sqrt(v_hat) + ADAM_EPS) + ADAM_WD * w)
    return delta, m, v


def _adamw_shard(g_own, recv, w, m, v, *, rows, name):
    R, C = g_own.shape

    def body(g_ref, r_ref, w_ref, m_ref, v_ref, go_ref, d_ref, mo_ref, vo_ref):
        g = g_ref[...]
        for k in range(N_DEV - 1):
            g = g + r_ref[k].astype(F32)
        d, mn, vn = _adamw_math(w_ref[...], g, m_ref[...], v_ref[...])
        go_ref[...] = g
        d_ref[...] = d
        mo_ref[...] = mn
        vo_ref[...] = vn

    blk = pl.BlockSpec((rows, C), lambda i: (i, 0))
    return pl.pallas_call(
        body, name=name, grid=(R // rows,),
        in_specs=[blk, pl.BlockSpec((N_DEV - 1, rows, C), lambda i: (0, i, 0)), blk, blk, blk],
        out_specs=[blk] * 4, out_shape=[jax.ShapeDtypeStruct((R, C), F32)] * 4,
        compiler_params=_params(("arbitrary",)),
    )(g_own, recv, w, m, v)


def _minor_rows_view(a):
    return jnp.transpose(a, (2, 0, 1)).reshape(SHARD_COLS * 8, 128)


def _from_minor_rows_view(v):
    return jnp.transpose(v.reshape(SHARD_COLS, 8, 128), (1, 2, 0)).reshape(1, D_MODEL, SHARD_COLS)


def _adamw_w_in(is_lo, own_lo, own_hi, recv_lo, recv_hi, w, m, v):
    C = SHARD_COLS
    pad = -C % 128
    bands = [pl.ds(q * 128, 128) for q in range(D_MODEL // 128)]

    def body(lo_ref, ol_ref, oh_ref, rl_ref, rh_ref, w_ref, m_ref, v_ref, go_ref, d_ref, mo_ref, vo_ref,
             own_scr, recv_scr, sems):
        def fetch(own_src, recv_src):
            return [(pltpu.make_async_copy(own_src.at[band, :], own_scr.at[band, :], sems.at[0, q]),
                     pltpu.make_async_copy(recv_src.at[:, band, :], recv_scr.at[:, band, :], sems.at[1, q]))
                    for q, band in enumerate(bands)]

        from_lo, from_hi = fetch(ol_ref, rl_ref), fetch(oh_ref, rh_ref)

        @pl.when(lo_ref[0] == 1)
        def _():
            for pair in from_lo:
                for cp in pair:
                    cp.start()

        @pl.when(lo_ref[0] != 1)
        def _():
            for pair in from_hi:
                for cp in pair:
                    cp.start()

        for q, band in enumerate(bands):
            for cp in from_lo[q]:
                cp.wait()
            g = own_scr[band, :] + recv_scr[0, band, :].astype(F32) + recv_scr[1, band, :].astype(F32)
            g = jnp.pad(g, ((0, 0), (0, pad))).T[0:C]
            rows = pl.ds(q, C, stride=8)
            d, mn, vn = _adamw_math(w_ref[rows, :], g, m_ref[rows, :], v_ref[rows, :])
            go_ref[rows, :] = g
            d_ref[rows, :] = d
            mo_ref[rows, :] = mn
            vo_ref[rows, :] = vn

    vmem = pl.BlockSpec(memory_space=pltpu.VMEM)
    return pl.pallas_call(
        body, name="adamw_w_in", out_shape=[jax.ShapeDtypeStruct(w.shape, F32)] * 4,
        in_specs=[pl.BlockSpec(memory_space=pltpu.SMEM)] + [pl.BlockSpec(memory_space=pl.ANY)] * 4 + [vmem] * 3,
        out_specs=[vmem] * 4,
        scratch_shapes=[pltpu.VMEM(own_lo.shape, F32), pltpu.VMEM(recv_lo.shape, BF16),
                        pltpu.SemaphoreType.DMA((2, len(bands)))],
        compiler_params=_params(),
    )(is_lo, own_lo, own_hi, recv_lo, recv_hi, w, m, v)


SMALL = ("conv_b", "dt_bias", "a_log", "d_skip", "ssd_norm_w", "attn_sinks", "ln_g", "ln_b")


def _adamw_small(gathered, params):
    n_p = len(SMALL)

    def body(*refs):
        acc = []
        for r in refs[:5]:
            t = r[0]
            for k in range(1, N_DEV):
                t = t + r[k]
            acc.append(t)
        head, conv, norm, scal, sink = acc
        grads = dict(conv_b=conv[4:5, :], dt_bias=scal[0:1, 0:N_HEADS], a_log=scal[1:2, 0:N_HEADS],
                     d_skip=scal[2:3, 0:N_HEADS], ssd_norm_w=norm[0:1, :], attn_sinks=sink[0:1, 0:N_HEADS],
                     ln_g=head[0:1, :], ln_b=head[1:2, :])
        wmv = refs[5:5 + 3 * n_p]
        outs = refs[5 + 3 * n_p:]
        outs[0][...] = head[3:4, 0:1]
        outs[1][...] = conv[0:4, :]
        for i, name in enumerate(SMALL):
            w_ref, m_ref, v_ref = wmv[3 * i:3 * i + 3]
            g = grads[name]
            d, mn, vn = _adamw_math(w_ref[...], g, m_ref[...], v_ref[...])
            for o_ref, val in zip(outs[2 + 4 * i:6 + 4 * i], (g, d, mn, vn)):
                o_ref[...] = val

    flat = [a for name in SMALL for a in params[name]]
    out_shape = [jax.ShapeDtypeStruct((1, 1), F32), jax.ShapeDtypeStruct((4, D_XBC), F32)]
    for name in SMALL:
        out_shape += [jax.ShapeDtypeStruct(params[name][0].shape, F32)] * 4
    res = pl.pallas_call(body, name="adamw_small", out_shape=out_shape, compiler_params=_params())(*gathered, *flat)
    return res[0], res[1], {name: res[2 + 4 * i:6 + 4 * i] for i, name in enumerate(SMALL)}


def _adamw_plain(g, w, m, v):
    def body(g_ref, w_ref, m_ref, v_ref, d_ref, mo_ref, vo_ref):
        d, mn, vn = _adamw_math(w_ref[...], g_ref[...], m_ref[...], v_ref[...])
        d_ref[...] = d
        mo_ref[...] = mn
        vo_ref[...] = vn

    return pl.pallas_call(
        body, name="adamw_conv_w", out_shape=[jax.ShapeDtypeStruct(w.shape, F32)] * 3,
        compiler_params=_params(),
    )(g, w, m, v)


def _lane_pattern(fn):
    return np.asarray([fn(l % HEAD_DIM) for l in range(128)], np.float32)


ROPE_INV = _lane_pattern(lambda r: ROPE_THETA ** (-2.0 * (r % 8) / ROPE_DIM) if r < ROPE_DIM else 0.0)


def _rope_tables(pos_ref, inv_ref, tab_ref):
    lane = lax.broadcasted_iota(jnp.int32, (1, 128), 1) % HEAD_DIM
    upper = jnp.where((lane >= ROPE_DIM // 2) & (lane < ROPE_DIM), 1.0, 0.0)
    lower = jnp.where(lane < ROPE_DIM // 2, -1.0, 0.0)

    def block(r, carry):
        rows = pl.ds(pl.multiple_of(r * CHUNK, CHUNK), CHUNK)
        pos = jnp.broadcast_to(pos_ref[pl.ds(r, 1), :].astype(F32), (CHUNK, 128)).T
        ang = pos * inv_ref[...]
        sn = jnp.sin(ang)
        tab_ref[rows, 0:128] = jnp.cos(ang)
        tab_ref[rows, 128:256] = sn * upper
        tab_ref[rows, 256:384] = sn * lower
        return carry

    lax.fori_loop(0, pos_ref.shape[0], block, 0)


def _expansion():
    E = np.arange(1024)[None, :] // HEAD_DIM == np.arange(128)[:, None]
    return jnp.asarray(E, BF16), jnp.asarray(E.T, BF16)


def _ssd_args(conv_w, conv_b, dt_bias, a_log, d_skip, norm_w, E):
    return (conv_w, conv_b, dt_bias.reshape(-1), a_log.reshape(-1), d_skip.reshape(-1), norm_w, E)


def kernel(x, positions, w_in, conv_w, conv_b, dt_bias, a_log, d_skip, ssd_norm_w, attn_sinks, w_out, ln_g, ln_b, loss_target, m_w_in, m_conv_w, m_conv_b, m_dt_bias, m_a_log, m_d_skip, m_ssd_norm_w, m_attn_sinks, m_w_out, m_ln_g, m_ln_b, v_w_in, v_conv_w, v_conv_b, v_dt_bias, v_a_log, v_d_skip, v_ssd_norm_w, v_attn_sinks, v_w_out, v_ln_g, v_ln_b):
    me = _index(*_position())
    x0, target = x[0], loss_target[0]
    bf16_shard = lambda shape: jax.ShapeDtypeStruct(shape, BF16)
    E, ET = _expansion()
    sinks = attn_sinks.reshape(-1)

    w_all, tabs = _gather_w_in(w_in[0].astype(BF16), positions[0].reshape(-1, 128))
    w_ssd, w_att = _unpack_w_in(w_all)
    gather_conv_w = _Hosted([conv_w[0]], [jax.ShapeDtypeStruct((N_DEV,) + conv_w.shape[1:], F32)],
                            [_Flow("gather", 0, 0)])

    proj_ssd, proj_att, xb, conv_w_all = _in_proj(x0, w_ssd, w_att, tm=512, comm=gather_conv_w)
    conv_w_f = jnp.transpose(conv_w_all, (1, 0, 2)).reshape(4, D_XBC)
    ssd_args = _ssd_args(conv_w_f, conv_b, dt_bias, a_log, d_skip, ssd_norm_w, E)
    gather_w_out = _Hosted([w_out[0].astype(BF16)], [bf16_shard((N_DEV, 256, D_MODEL))], [_Flow("gather", 0, 0)])
    y, ypre, hprev, pre, w_out_all = _mixer_forward(proj_ssd, proj_att, tabs, sinks, *ssd_args, comm=gather_w_out)
    w_out_f = w_out_all.reshape(2 * D_MODEL, D_MODEL)
    dr, dy, acc_head = _head(y, x0, target, w_out_f, ln_g, ln_b, tm=512)

    dw_out, dw_out_bf16 = _matmul_tn(y, dr, tl=1024, tn=D_MODEL, name="dw_out", emit_bf16=True)
    own_out = lax.dynamic_index_in_dim(dw_out.reshape(N_DEV, 256, D_MODEL), me, axis=0, keepdims=False)
    send_out = _Hosted([dw_out_bf16.reshape(N_DEV, 256, D_MODEL)], [bf16_shard((N_DEV - 1, 256, D_MODEL))],
                       [_Flow("exchange", 0, 0)])
    d_ssd, acc_cw, acc_w, acc_s, recv_out = _ssd_backward(proj_ssd, hprev, ypre, pre, dy, *ssd_args, ET, comm=send_out)
    stack_lo, dw_dt_block = _dw_in(xb, d_ssd, 0)
    d_att, dsink, own_lo, recv_lo = _swa_backward(proj_att, tabs, sinks, dy, reduce=_OwnerReduce(stack_lo, 0))
    (stack_hi,) = _dw_in(xb, d_att, 1, dw_dt_block)
    accs = [acc_head, acc_cw, acc_w, acc_s, dsink]
    gather_accs = _Hosted(accs, [jax.ShapeDtypeStruct((N_DEV,) + a.shape, F32) for a in accs],
                          [_Flow("gather", i, i) for i in range(5)])
    dx, *gathered, own_hi, recv_hi = _input_gradient(d_ssd, d_att, w_ssd, w_att, dr, tm=256, comm=gather_accs,
                                                     reduce=_OwnerReduce(stack_hi, 1))
    is_lo = (me < 4).reshape(1).astype(jnp.int32)

    g_in, d_in, nm_in, nv_in = [_from_minor_rows_view(r) for r in _adamw_w_in(
        is_lo, own_lo, own_hi, recv_lo, recv_hi, _minor_rows_view(w_in), _minor_rows_view(m_w_in), _minor_rows_view(v_w_in))]
    g_out, d_out, nm_out, nv_out = _adamw_shard(own_out, recv_out, w_out[0], m_w_out[0], v_w_out[0],
                                                rows=256, name="adamw_w_out")
    loss, g_conv_w, small = _adamw_small(gathered, dict(
        conv_b=(conv_b, m_conv_b, v_conv_b), dt_bias=(dt_bias, m_dt_bias, v_dt_bias), a_log=(a_log, m_a_log, v_a_log),
        d_skip=(d_skip, m_d_skip, v_d_skip), ssd_norm_w=(ssd_norm_w, m_ssd_norm_w, v_ssd_norm_w),
        attn_sinks=(attn_sinks, m_attn_sinks, v_attn_sinks), ln_g=(ln_g, m_ln_g, v_ln_g), ln_b=(ln_b, m_ln_b, v_ln_b)))
    g_cw = lax.dynamic_slice_in_dim(g_conv_w, me * (D_XBC // N_DEV), D_XBC // N_DEV, axis=1)
    d_cw, nm_cw, nv_cw = _adamw_plain(g_cw, conv_w[0], m_conv_w[0], v_conv_w[0])

    def leaves(i, big_in, cw, big_out):
        mid = [small[k][i] for k in ("conv_b", "dt_bias", "a_log", "d_skip", "ssd_norm_w", "attn_sinks")]
        return [big_in, cw[None]] + mid + [big_out[None], small["ln_g"][i], small["ln_b"][i]]

    return (loss.reshape(()), dx[None], *leaves(0, g_in, g_cw, g_out), *leaves(1, d_in, d_cw, d_out),
            *leaves(2, nm_in, nm_cw, nm_out), *leaves(3, nv_in, nv_cw, nv_out))
```

```python
import jax
import jax.numpy as jnp
from jax import lax
from jax.experimental import pallas as pl
from jax.experimental.pallas import tpu as pltpu
import numpy as np

F32 = jnp.float32
BF16 = jnp.bfloat16
_MXU = jnp.bfloat16

N_DEV = 8
D_MODEL = 1024
D_SSD = 1024
D_ATT = 1024
HEAD_DIM = 64
N_HEADS = 16
SSD_GROUPS = 2
KV_HEADS = 4
CHUNK = 128
D_XBC = 1536
D_IN_PROJ = 5136
ROPE_DIM = 16
ROPE_THETA = 500000.0
ALPHA = (2.0 * 1) ** 0.25
LN_EPS = 1e-5
RMS_EPS = 1e-5
ATT_SCALE = HEAD_DIM ** -0.5
NEG = -1e30

S_Z, S_XS, S_B, S_C, S_DT, S_W = 0, 1024, 2048, 2304, 2560, 2816
N_SSD_REAL = 2576
A_Q, A_K, A_V, A_G, A_W = 0, 1024, 1280, 1536, 2560

ADAM_LR = 0.001
ADAM_B1 = 0.9
ADAM_B2 = 0.999
ADAM_EPS = 1e-08
ADAM_WD = 0.01
ADAM_STEP = 10

VMEM_LIMIT = 48 * 1024 * 1024
MESH = pl.DeviceIdType.MESH


def _params(sem=None):
    return pltpu.CompilerParams(dimension_semantics=sem, vmem_limit_bytes=VMEM_LIMIT)


def _mm(a, b):
    return jnp.dot(a.astype(_MXU), b.astype(_MXU), preferred_element_type=F32)


def _mm_nt(a, b):
    return lax.dot_general(a.astype(_MXU), b.astype(_MXU), (((1,), (1,)), ((), ())),
                           preferred_element_type=F32)


def _mm_tn(a, b):
    return lax.dot_general(a.astype(_MXU), b.astype(_MXU), (((0,), (0,)), ((), ())),
                           preferred_element_type=F32)


def _split3(v):
    hi = v.astype(BF16)
    r = v - hi.astype(F32)
    mid = r.astype(BF16)
    lo = (r - mid.astype(F32)).astype(BF16)
    return hi, mid, lo


def _mm_exact_r(v, p01):
    hi, mid, lo = _split3(v)
    d = lambda a: jnp.dot(a, p01, preferred_element_type=F32)
    return d(hi) + d(mid) + d(lo)


def _mm_exact_l(p01, v):
    hi, mid, lo = _split3(v)
    d = lambda a: jnp.dot(p01, a, preferred_element_type=F32)
    return d(hi) + d(mid) + d(lo)


def _mm_2pass_r(v, p01):
    hi = v.astype(BF16)
    lo = (v - hi.astype(F32)).astype(BF16)
    return jnp.dot(hi, p01, preferred_element_type=F32) + jnp.dot(lo, p01, preferred_element_type=F32)


def _sigmoid(x):
    return 1.0 / (1.0 + jnp.exp(-x))


def _softplus(x):
    e = jnp.exp(-jnp.abs(x))
    u = 1.0 + e
    log1p = jnp.where(u == 1.0, e, jnp.log(u) * (e / (u - 1.0)))
    return jnp.maximum(x, 0.0) + log1p


def _rows8(rows):
    n = rows[0].shape[1]
    rid = lax.broadcasted_iota(jnp.int32, (8, n), 0)
    out = jnp.zeros((8, n), F32)
    for k, r in enumerate(rows):
        out = out + jnp.where(rid == k, r, 0.0)
    return out


def _colsum(a):
    return jnp.sum(a, axis=0, keepdims=True)


def _in_proj(x, w_ssd, w_att, *, tm, comm=None):
    L, K = x.shape

    def body(x_ref, ws_ref, wa_ref, ps_ref, pa_ref, xb_ref):
        xb = x_ref[...].astype(_MXU)
        xb_ref[...] = xb
        ps_ref[...] = jnp.dot(xb, ws_ref[...], preferred_element_type=F32)
        pa_ref[...] = jnp.dot(xb, wa_ref[...], preferred_element_type=F32)

    row = lambda w: pl.BlockSpec((tm, w), lambda i: (i, 0))
    resident = lambda a: pl.BlockSpec(a.shape, lambda i: (0, 0), pipeline_mode=pl.Buffered(1))
    return _call(
        body, comm, name="in_proj", grid=(L // tm,),
        in_specs=[row(K), resident(w_ssd), resident(w_att)], out_specs=[row(S_W), row(A_W), row(K)],
        out_shape=[jax.ShapeDtypeStruct((L, S_W), F32), jax.ShapeDtypeStruct((L, A_W), F32),
                   jax.ShapeDtypeStruct((L, K), _MXU)],
        scratch_shapes=[], args=(x, w_ssd, w_att))


def _matmul_tn(a, g, *, tl, tn, name, emit_bf16=False):
    L, M = a.shape
    N = g.shape[1]
    last = L // tl - 1

    def body(a_ref, g_ref, o_ref, *rest):
        @pl.when(pl.program_id(1) == 0)
        def _():
            o_ref[...] = jnp.zeros_like(o_ref)

        o_ref[...] += _mm_tn(a_ref[...], g_ref[...])
        if emit_bf16:
            @pl.when(pl.program_id(1) == last)
            def _():
                rest[0][...] = o_ref[...].astype(BF16)

    spec = pl.BlockSpec((M, tn), lambda j, l: (0, j))
    res = pl.pallas_call(
        body, name=name, grid=(N // tn, L // tl),
        in_specs=[pl.BlockSpec((tl, M), lambda j, l: (l, 0)), pl.BlockSpec((tl, tn), lambda j, l: (l, j))],
        out_specs=[spec, spec] if emit_bf16 else [spec],
        out_shape=[jax.ShapeDtypeStruct((M, N), F32)] + ([jax.ShapeDtypeStruct((M, N), BF16)] if emit_bf16 else []),
        compiler_params=_params(("arbitrary", "arbitrary")),
    )(a, g)
    return res if emit_bf16 else res[0]


def _position():
    return lax.axis_index("x"), lax.axis_index("y"), lax.axis_index("c")


def _index(px, py, pc):
    return 4 * px + 2 * py + pc


def _flip(pos, k):
    x, y, c = pos
    return ((1 - x) if (k >> 2) & 1 else x, (1 - y) if (k >> 1) & 1 else y, (1 - c) if k & 1 else c)


def _remote(src, dst, send_sem, recv_sem, peer):
    return pltpu.make_async_remote_copy(src_ref=src, dst_ref=dst, send_sem=send_sem, recv_sem=recv_sem,
                                        device_id=peer, device_id_type=MESH)


class _Flow:
    def __init__(self, kind, operand, result):
        self.kind, self.operand, self.result = kind, operand, result


class _Hosted:
    def __init__(self, operands, out_shapes, flows):
        self.operands, self.out_shapes, self.flows = operands, out_shapes, flows

    def plan(self, ins, outs, send_sems, recv_sems, local_sems):
        me = _position()
        mi = _index(*me)
        sends, recvs, locals_ = [], [], []
        for row, f in enumerate(self.flows):
            src, dst = ins[f.operand], outs[f.result]
            for k in range(1, N_DEV):
                peer = _flip(me, k)
                sems = (send_sems.at[row, k - 1], recv_sems.at[row, k - 1])
                if f.kind == "exchange":
                    sends.append(_remote(src.at[_index(*peer)], dst.at[k - 1], *sems, peer))
                    recvs.append(sends[-1])
                else:
                    sends.append(_remote(src, dst.at[mi], *sems, peer))
                    recvs.append(_remote(src, dst.at[_index(*peer)], *sems, peer))
            if f.kind == "gather":
                locals_.append(pltpu.make_async_copy(src, dst.at[mi], local_sems.at[row]))

        def start():
            for cp in locals_ + sends:
                cp.start()

        def wait():
            for cp in recvs:
                cp.wait_recv()
            for cp in sends:
                cp.wait_send()
            for cp in locals_:
                cp.wait()

        return start, wait


class _OwnerReduce:
    FIRST_STEP, SECOND_STEP, SEND_STEPS, REDUCE_STEPS = 2, 4, (2, 4, 6, 8), (5, 8, 10, 13)

    def __init__(self, stack, target_x):
        self.stack, self.target_x = stack, target_x
        block = stack.shape[1:]
        self.chunks = len(self.SEND_STEPS)
        self.chunk_rows = block[0] // self.chunks
        self.out_shapes = [jax.ShapeDtypeStruct(block, F32), jax.ShapeDtypeStruct((2,) + block, BF16)]
        self.out_specs = [pl.BlockSpec(block, lambda *_: (0, 0), pipeline_mode=pl.Buffered(1)),
                          pl.BlockSpec(memory_space=pl.ANY)]
        dma = pltpu.SemaphoreType.DMA
        self.scratch_shapes = ([pltpu.VMEM((2,) + block, F32)] * 2 + [pltpu.VMEM(block, BF16)] * 3
                               + [dma((self.chunks,))] * 4 + [dma((2,))] * 3)

    def plan(self, i, steps, stack_ref, own_ref, recv_ref, scratch):
        assert self.FIRST_STEP <= self.SEND_STEPS[0] and self.SECOND_STEP < self.REDUCE_STEPS[0] < steps - 1
        (theirs_scr, mine_scr, first_scr, across_scr, out_scr, y_send_sems, y_recv_sems, x_send_sems, x_recv_sems,
         swap_send_sems, swap_recv_sems, mine_sems) = scratch
        x, y, c = _position()
        owners_side = x == self.target_x
        other_side = x != self.target_x
        sibling, across, owner = (x, y, 1 - c), (x, 1 - y, c), (self.target_x, y, c)
        order = (1 - y, y)
        swaps = [_remote(stack_ref.at[2 * order[j] + (1 - c)], theirs_scr.at[j], swap_send_sems.at[j], swap_recv_sems.at[j],
                         sibling) for j in range(2)]
        mine = [pltpu.make_async_copy(stack_ref.at[2 * order[j] + c], mine_scr.at[j], mine_sems.at[j]) for j in range(2)]
        chunks = range(self.chunks)
        part = [pl.ds(j * self.chunk_rows, self.chunk_rows) for j in chunks]
        y_sems = lambda j: (y_send_sems.at[j], y_recv_sems.at[j])
        to_neighbour = [_remote(first_scr.at[part[j]], across_scr.at[part[j]], *y_sems(j), across) for j in chunks]
        to_owner_y = [_remote(first_scr.at[part[j]], recv_ref.at[0, part[j]], *y_sems(j), across) for j in chunks]
        to_owner_x = [_remote(out_scr.at[part[j]], recv_ref.at[1, part[j]], x_send_sems.at[j], x_recv_sems.at[j], owner)
                      for j in chunks]

        def before():
            @pl.when(i <= 1)
            def _():
                @pl.when(i == 0)
                def _():
                    for cp in [swaps[0]] + mine:
                        cp.start()

                pl.when(i == 1)(swaps[1].start)

        def after():
            @pl.when(i <= self.REDUCE_STEPS[-1])
            def _():
                @pl.when(i == self.FIRST_STEP)
                def _():
                    swaps[0].wait_recv()
                    mine[0].wait()
                    first_scr[...] = (mine_scr[0] + theirs_scr[0]).astype(first_scr.dtype)

                for j in chunks:
                    pl.when((i == self.SEND_STEPS[j]) & other_side)(to_neighbour[j].start)
                    pl.when((i == self.SEND_STEPS[j]) & owners_side)(to_owner_y[j].start)

                @pl.when(i == self.SECOND_STEP)
                def _():
                    swaps[1].wait_recv()
                    mine[1].wait()
                    t = mine_scr[1] + theirs_scr[1]
                    own_ref[...] = t
                    mine_scr[1] = t

                for j in chunks:
                    @pl.when((i == self.REDUCE_STEPS[j]) & other_side)
                    def _(j=j):
                        to_neighbour[j].wait_recv()
                        t = mine_scr[1, part[j], :] + across_scr[part[j], :].astype(F32)
                        out_scr[part[j], :] = t.astype(out_scr.dtype)
                        to_owner_x[j].start()

            @pl.when(i == steps - 1)
            def _():
                for cp in swaps:
                    cp.wait_send()
                for j in chunks:
                    @pl.when(other_side)
                    def _(j=j):
                        to_neighbour[j].wait_send()
                        to_owner_x[j].wait_send()

                    @pl.when(owners_side)
                    def _(j=j):
                        to_owner_y[j].wait_send()
                        to_owner_y[j].wait_recv()
                        to_owner_x[j].wait_recv()

        return before, after


def _call(body, comm, *, name, grid, in_specs, out_specs, out_shape, scratch_shapes, args, reduce=None):
    semantics = ("arbitrary",) * len(grid)
    if comm is None and reduce is None:
        return pl.pallas_call(body, name=name, grid=grid, in_specs=in_specs, out_specs=out_specs, out_shape=out_shape,
                              scratch_shapes=scratch_shapes, compiler_params=_params(semantics))(*args)
    n_in, n_out, n_scr = len(args), len(out_shape), len(scratch_shapes)
    c_operands, c_shapes, flows = (comm.operands, comm.out_shapes, comm.flows) if comm else ([], [], [])
    c_in, c_out, rows = len(c_operands), len(c_shapes), max(len(flows), 1)
    r_in = 0 if reduce is None else 1

    def hosted(*refs):
        ins, refs = refs[:n_in], refs[n_in:]
        cins, refs = refs[:c_in], refs[c_in:]
        rins, refs = refs[:r_in], refs[r_in:]
        outs, refs = refs[:n_out], refs[n_out:]
        couts, refs = refs[:c_out], refs[c_out:]
        routs, refs = refs[:2 * r_in], refs[2 * r_in:]
        scr, refs = refs[:n_scr], refs[n_scr:]
        (send_sems, recv_sems, local_sems), r_scr = refs[:3], refs[3:]
        ids = [pl.program_id(d) for d in range(len(grid))]
        first, last = ids[0] == 0, ids[0] == grid[0] - 1
        for d in range(1, len(grid)):
            first, last = first & (ids[d] == 0), last & (ids[d] == grid[d] - 1)
        before = after = lambda: None
        if reduce is not None:
            before, after = reduce.plan(ids[0], grid[0], rins[0], *routs, r_scr)
        if comm is not None:
            start, wait = comm.plan(cins, couts, send_sems, recv_sems, local_sems)
            pl.when(first)(start)
        before()
        body(*ins, *outs, *scr)
        after()
        if comm is not None:
            pl.when(last)(wait)

    any_spec = pl.BlockSpec(memory_space=pl.ANY)
    sems = [pltpu.SemaphoreType.DMA((rows, N_DEV - 1)), pltpu.SemaphoreType.DMA((rows, N_DEV - 1)),
            pltpu.SemaphoreType.DMA((rows,))]
    r_operands, r_specs, r_shapes, r_scratch = ([reduce.stack], reduce.out_specs, reduce.out_shapes,
                                                reduce.scratch_shapes) if reduce else ([], [], [], [])
    return pl.pallas_call(
        hosted, name=name, grid=grid, in_specs=list(in_specs) + [any_spec] * (c_in + r_in),
        out_specs=list(out_specs) + [any_spec] * c_out + r_specs, out_shape=list(out_shape) + list(c_shapes) + r_shapes,
        scratch_shapes=list(scratch_shapes) + sems + r_scratch,
        compiler_params=_params(semantics))(*args, *c_operands, *r_operands)


def _head_row(ref, width, rep):
    hid = lax.broadcasted_iota(jnp.int32, (1, width), 1) // rep
    row = jnp.zeros((1, width), F32)
    for h in range(N_HEADS):
        row = jnp.where(hid == h, ref[h], row)
    return row


def _rows_from_above(u_b, s, ext_scr, row, col):
    down = (row - col == s).astype(_MXU)
    return jnp.concatenate([ext_scr[8 - s:16 - s, :], jnp.dot(down, u_b, preferred_element_type=F32)[8:128]], axis=0)


def _ssd_recompute(first, p_ref, halo_ref, cw_ref, cb_ref, dtb_ref, alog_ref, e_ref, ext_scr, pre=None):
    row = lax.broadcasted_iota(jnp.int32, (128, 128), 0)
    col = lax.broadcasted_iota(jnp.int32, (128, 128), 1)
    ext_scr[0:8, :] = jnp.where(first, 0.0, halo_ref[:, S_XS:S_DT])
    if pre is not None:
        ext_scr[8:16, :] = p_ref[0:8, S_XS:S_DT]
    else:
        ext_scr[8:136, :] = p_ref[:, S_XS:S_DT]
        cw = cw_ref[...]
        pre = (cb_ref[0:1, :] + cw[3:4, :] * ext_scr[8:136, :] + cw[2:3, :] * ext_scr[7:135, :]
               + cw[1:2, :] * ext_scr[6:134, :] + cw[0:1, :] * ext_scr[5:133, :])
    sg = _sigmoid(pre)
    act = pre * sg
    lane = lax.broadcasted_iota(jnp.int32, (1, 128), 1)
    A = jnp.where(lane < N_HEADS, -jnp.exp(_head_row(alog_ref, 128, 1)), 0.0)
    raw = p_ref[:, S_DT:S_DT + 128] + _head_row(dtb_ref, 128, 1)
    dt = _softplus(raw)
    dA = dt * A
    tril = (row >= col).astype(BF16)
    acs = _mm_exact_l(tril, dA)
    last = acs[127:128, :]
    ds = jnp.exp(last - acs)
    eo = jnp.exp(acs)
    E = e_ref[...]
    ex = _mm_2pass_r(jnp.concatenate([dt, ds, eo], axis=0), E)
    dt_e, ds_e, eo_e = ex[0:128], ex[128:256], ex[256:384]
    xs_c = act[:, 0:1024]
    X = xs_c * dt_e
    return dict(pre=pre, sg=sg, xs_c=xs_c, Bc=act[:, 1024:1280], Cc=act[:, 1280:1536], A=A, raw=raw, dt=dt,
                acs=acs, acsT=acs.T, eo_e=eo_e, ds_e=ds_e, dt_e=dt_e, cd_e=eo_e[127:128, :],
                X=X, Xd=X * ds_e, row=row, col=col)


def _split_halves(t):
    lo = _lo_half(CHUNK)
    return jnp.concatenate([jnp.where(lo, t, 0.0), jnp.where(lo, 0.0, t)], axis=0)


def _ssd_core(R, hprev):
    causal = R["row"] >= R["col"]
    acs, acsT, X = R["acs"], R["acsT"], R["X"]
    ydiag, yoff, snew = [], [], []
    for g in range(SSD_GROUPS):
        Bg = R["Bc"][:, g * 128:(g + 1) * 128]
        Cg = R["Cc"][:, g * 128:(g + 1) * 128]
        cols = slice(g * 512, (g + 1) * 512)
        CB = _mm_nt(Cg, Bg)
        snew.append(_mm_tn(Bg, R["Xd"][:, cols]))
        yoff.append(_mm(Cg, hprev[:, cols]))
        for j in range(4):
            h0 = g * 8 + 2 * j
            ms = [CB * jnp.exp(jnp.where(causal, acs[:, h:h + 1] - acsT[h:h + 1, :], NEG)) for h in (h0, h0 + 1)]
            ydiag.append(_mm(jnp.concatenate(ms, axis=1), _split_halves(X[:, h0 * HEAD_DIM:h0 * HEAD_DIM + 128])))
    Y = jnp.concatenate(ydiag, axis=1) + jnp.concatenate(yoff, axis=1) * R["eo_e"]
    return Y, jnp.concatenate(snew, axis=1)


def _ssd_forward_step(p_ref, halo_ref, cw_ref, cb_ref, dtb_ref, alog_ref, dsk_ref, nw_ref, e_ref,
                      y_ref, ypre_ref, hprev_ref, pre_ref, h_scr, ext_scr):
    c = pl.program_id(0)
    first = c == 0

    @pl.when(first)
    def _():
        h_scr[...] = jnp.zeros_like(h_scr)

    R = _ssd_recompute(first, p_ref, halo_ref, cw_ref, cb_ref, dtb_ref, alog_ref, e_ref, ext_scr)
    hprev = h_scr[...]
    hprev_ref[...] = hprev
    pre_ref[...] = R["pre"]
    Y, snew = _ssd_core(R, hprev)
    h_scr[...] = hprev * R["cd_e"] + snew
    Y = Y + _head_row(dsk_ref, D_SSD, HEAD_DIM) * R["xs_c"]
    ypre_ref[...] = Y
    z = p_ref[:, S_Z:S_Z + 1024]
    yf = Y * (z * _sigmoid(z))
    outs = []
    for g in range(SSD_GROUPS):
        yg = yf[:, g * 512:(g + 1) * 512]
        r = lax.rsqrt(jnp.mean(yg * yg, axis=-1, keepdims=True) + RMS_EPS)
        outs.append(yg * r)
    y_ref[:, 0:D_SSD] = (jnp.concatenate(outs, axis=1) * nw_ref[0:1, :]).astype(y_ref.dtype)


def _ssd_backward(proj_ssd, hprev_all, ypre, pre, dy, conv_w, conv_b, dt_bias, a_log, d_skip, norm_w, E, ET, comm=None):
    L = proj_ssd.shape[0]
    nc = L // CHUNK

    def body(p_ref, halo_ref, hprev_ref, ypre_ref, pre_ref, dy_ref, cw_ref, cb_ref, dtb_ref, alog_ref, dsk_ref, nw_ref, e_ref,
             et_ref, dp_ref, acc_cw_ref, acc_w_ref, acc_s_ref, dh_scr, ext_scr, ext2_scr, nxt_scr):
        i = pl.program_id(0)
        c = nc - 1 - i
        first = c == 0

        @pl.when(i == 0)
        def _():
            dh_scr[...] = jnp.zeros_like(dh_scr)
            nxt_scr[...] = jnp.zeros_like(nxt_scr)
            acc_cw_ref[...] = jnp.zeros_like(acc_cw_ref)
            acc_w_ref[...] = jnp.zeros_like(acc_w_ref)
            acc_s_ref[...] = jnp.zeros_like(acc_s_ref)

        R = _ssd_recompute(first, p_ref, halo_ref, cw_ref, cb_ref, dtb_ref, alog_ref, e_ref, ext_scr, pre_ref[...])
        xs_c, X, Xd = R["xs_c"], R["X"], R["Xd"]
        acs, acsT = R["acs"], R["acsT"]
        dsk = _head_row(dsk_ref, D_SSD, HEAD_DIM)

        z = p_ref[:, S_Z:S_Z + 1024]
        sz = _sigmoid(z)
        silz = z * sz
        yf = ypre_ref[...] * silz
        nw = nw_ref[0:1, :]
        dyf_parts, dnw_parts = [], []
        for g in range(SSD_GROUPS):
            cols = slice(g * 512, (g + 1) * 512)
            yg = yf[:, cols]
            r = lax.rsqrt(jnp.mean(yg * yg, axis=-1, keepdims=True) + RMS_EPS)
            yn = yg * r
            dyn = dy_ref[:, cols] * nw[:, cols]
            dnw_parts.append(_colsum(dy_ref[:, cols] * yn))
            dyf_parts.append(r * (dyn - yn * jnp.mean(dyn * yn, axis=-1, keepdims=True)))
        dyf = jnp.concatenate(dyf_parts, axis=1)
        dY = dyf * silz
        dz = dyf * ypre_ref[...] * (sz * (1.0 + z * (1.0 - sz)))

        dYo = dY * R["eo_e"]
        causal = R["row"] >= R["col"]
        dacs = jnp.zeros((128, 128), F32)
        dacs_t = jnp.zeros((128, 128), F32)
        dxdiag, dxd, dhprev, dBs, dCs, yoff = [], [], [], [], [], []
        for g in range(SSD_GROUPS):
            Bg = R["Bc"][:, g * 128:(g + 1) * 128]
            Cg = R["Cc"][:, g * 128:(g + 1) * 128]
            cols = slice(g * 512, (g + 1) * 512)
            CB = _mm_nt(Cg, Bg)
            dCB = jnp.zeros((128, 128), F32)
            for j in range(4):
                h0 = g * 8 + 2 * j
                pc = slice(h0 * HEAD_DIM, h0 * HEAD_DIM + 128)
                dYst = _split_halves(dY[:, pc])
                dMst = _mm_nt(dYst, X[:, pc])
                mts = []
                for a, h in enumerate((h0, h0 + 1)):
                    acol = acs[:, h:h + 1]
                    arow = acsT[h:h + 1, :]
                    Lm = jnp.exp(jnp.where(causal, acol - arow, NEG))
                    M = CB * Lm
                    dM = dMst[a * 128:(a + 1) * 128]
                    dCB = dCB + dM * Lm
                    G = dM * M
                    dacs = dacs + jnp.where(R["col"] == h, jnp.sum(G, axis=1, keepdims=True), 0.0)
                    dacs_t = dacs_t + jnp.where(R["row"] == h, jnp.sum(G, axis=0, keepdims=True), 0.0)
                    mts.append(M.T)
                dxdiag.append(_mm(jnp.concatenate(mts, axis=1), dYst))
            dS = dh_scr[:, cols]
            dxd.append(_mm(Bg, dS))
            yoff.append(_mm(Cg, hprev_ref[:, cols]))
            dhprev.append(_mm_tn(Cg, dYo[:, cols]))
            dCs.append(_mm_nt(dYo[:, cols], hprev_ref[:, cols]) + _mm(dCB, Bg))
            dBs.append(_mm_tn(dCB, Cg) + _mm_nt(Xd[:, cols], dS))
        Yoff = jnp.concatenate(yoff, axis=1) * R["eo_e"]
        dXd = jnp.concatenate(dxd, axis=1)
        dX = jnp.concatenate(dxdiag, axis=1) + dXd * R["ds_e"]
        t_state = dXd * Xd
        dacs = dacs + _mm_2pass_r(dY * Yoff - t_state, et_ref[...]) - dacs_t.T
        v_last = _colsum(t_state + dh_scr[...] * hprev_ref[...] * R["cd_e"])
        dlast = _mm_exact_r(jnp.broadcast_to(v_last, (8, 1024)), et_ref[...])[0:1, :]
        dacs = dacs + jnp.where(R["row"] == 127, dlast, 0.0)
        triu = (R["col"] >= R["row"]).astype(BF16)
        da = _mm_exact_l(triu, dacs)
        ddt = da * R["A"] + _mm(dX * xs_c, et_ref[...])
        ddt_raw = ddt * _sigmoid(R["raw"])
        dxs_c = dX * R["dt_e"] + dY * dsk
        dh_scr[...] = jnp.concatenate(dhprev, axis=1) + dh_scr[...] * R["cd_e"]

        dact = jnp.concatenate([dxs_c] + dBs + dCs, axis=1)
        pre, sg = R["pre"], R["sg"]
        dpre = dact * (sg * (1.0 + pre * (1.0 - sg)))
        ext2_scr[0:8, :] = dpre[120:128, :]
        ext2_scr[8:16, :] = nxt_scr[...]
        nxt_scr[...] = dpre[0:8, :]
        cw = cw_ref[...]
        u_b, dpre_b = p_ref[:, S_XS:S_DT].astype(_MXU), dpre.astype(_MXU)
        dxbc = cw[3:4, :] * dpre
        taps = [_colsum(dpre * p_ref[:, S_XS:S_DT])]
        for s in (1, 2, 3):
            up = (R["col"] - R["row"] == s).astype(_MXU)
            d_s = jnp.concatenate([jnp.dot(up, dpre_b, preferred_element_type=F32)[0:120],
                                   ext2_scr[s:8 + s, :]], axis=0)
            dxbc = dxbc + cw[3 - s:4 - s, :] * d_s
            taps.append(_colsum(dpre * _rows_from_above(u_b, s, ext_scr, R["row"], R["col"])))
        acc_cw_ref[...] += _rows8(taps[::-1] + [_colsum(dpre)])
        acc_w_ref[...] += _rows8([jnp.concatenate(dnw_parts, axis=1), _colsum(dY * xs_c)])
        acc_s_ref[...] += _rows8([_colsum(ddt_raw), _colsum(da * R["dt"])])

        lane = lax.broadcasted_iota(jnp.int32, (128, 128), 1)
        dp_ref[:, S_Z:S_Z + 1024] = dz.astype(dp_ref.dtype)
        dp_ref[:, S_XS:S_DT] = dxbc.astype(dp_ref.dtype)
        dp_ref[:, S_DT:S_DT + 128] = jnp.where(lane < N_HEADS, ddt_raw, 0.0).astype(dp_ref.dtype)
        dp_ref[:, S_DT + 128:S_W] = jnp.zeros((128, 128), dp_ref.dtype)

        @pl.when(i == nc - 1)
        def _():
            acc = acc_s_ref[...]
            dskip = _mm_exact_r(acc_w_ref[...], et_ref[...])[1:2, :]
            acc_s_ref[...] = _rows8([acc[0:1, :], acc[1:2, :] * R["A"], dskip])

    const = lambda shape: pl.BlockSpec(shape, lambda i: (0, 0))
    smem = pl.BlockSpec(memory_space=pltpu.SMEM)
    rev = lambda i: (nc - 1 - i, 0)
    return _call(
        body, comm, name="ssd_bwd", grid=(nc,),
        in_specs=[pl.BlockSpec((CHUNK, S_W), rev),
                  pl.BlockSpec((8, S_W), lambda i: (jnp.maximum((nc - 1 - i) * 16 - 1, 0), 0)),
                  pl.BlockSpec((128, 1024), rev),
                  pl.BlockSpec((CHUNK, D_SSD), rev),
                  pl.BlockSpec((CHUNK, D_XBC), rev),
                  pl.BlockSpec((CHUNK, D_SSD), rev),
                  const((4, D_XBC)), const((1, D_XBC)), smem, smem, smem, const((1, 1024)),
                  const((128, 1024)), const((1024, 128))],
        out_specs=[pl.BlockSpec((CHUNK, S_W), rev), const((8, D_XBC)), const((8, 1024)), const((8, 128))],
        out_shape=[jax.ShapeDtypeStruct((L, S_W), _MXU), jax.ShapeDtypeStruct((8, D_XBC), F32),
                   jax.ShapeDtypeStruct((8, 1024), F32), jax.ShapeDtypeStruct((8, 128), F32)],
        scratch_shapes=[pltpu.VMEM((128, 1024), F32), pltpu.VMEM((16, D_XBC), F32),
                        pltpu.VMEM((16, D_XBC), F32), pltpu.VMEM((8, D_XBC), F32)],
        args=(proj_ssd, proj_ssd, hprev_all, ypre, pre, dy, conv_w, conv_b, dt_bias, a_log, d_skip, norm_w, E, ET))


def _rope(t, tab):
    cos, sa, sb = tab[:, 0:128], tab[:, 128:256], tab[:, 256:384]
    outs = []
    for i in range(t.shape[1] // 128):
        tg = t[:, i * 128:(i + 1) * 128]
        outs.append(tg * cos + pltpu.roll(tg, 8, 1) * sa + pltpu.roll(tg, 120, 1) * sb)
    return jnp.concatenate(outs, axis=1)


def _rope_transposed(d, tab):
    cos, sa, sb = tab[:, 0:128], tab[:, 128:256], tab[:, 256:384]
    outs = []
    for i in range(d.shape[1] // 128):
        dg = d[:, i * 128:(i + 1) * 128]
        outs.append(dg * cos + pltpu.roll(dg * sa, 120, 1) + pltpu.roll(dg * sb, 8, 1))
    return jnp.concatenate(outs, axis=1)


def _lo_half(rows):
    return lax.broadcasted_iota(jnp.int32, (rows, 128), 1) < HEAD_DIM


def _native_half(rows, j):
    lo = _lo_half(rows)
    return lo if j % 2 == 0 else jnp.logical_not(lo)


def _kv_native(t, j):
    p = j // 2
    return jnp.where(_native_half(t.shape[0], j), t[:, p * 128:(p + 1) * 128], 0.0)


def _stack_heads(t, j):
    out = []
    for m in (2 * j, 2 * j + 1):
        pair = t[:, m * 128:(m + 1) * 128]
        swapped = pltpu.roll(pair, HEAD_DIM, 1)
        out += [pair, swapped] if j % 2 == 0 else [swapped, pair]
    return jnp.concatenate(out, axis=0)


def _unstack_heads(s, j):
    out = []
    for m in range(2):
        first, second = s[256 * m:256 * m + 128], s[256 * m + 128:256 * m + 256]
        if j % 2 == 0:
            out.append(first + pltpu.roll(second, HEAD_DIM, 1))
        else:
            out.append(pltpu.roll(first, HEAD_DIM, 1) + second)
    return jnp.concatenate(out, axis=1)


def _keep_native(r, j):
    return jnp.where(_native_half(r.shape[0], j), r, 0.0)


def _sink_row(sink_ref, j):
    hid = lax.broadcasted_iota(jnp.int32, (1, 4 * CHUNK), 1) // CHUNK
    row = jnp.zeros((1, 4 * CHUNK), F32)
    for hh in range(4):
        row = jnp.where(hid == hh, sink_ref[4 * j + hh], row)
    return row


def _from_current():
    si = lax.broadcasted_iota(jnp.int32, (CHUNK, 4 * CHUNK), 0)
    qi = lax.broadcasted_iota(jnp.int32, (CHUNK, 4 * CHUNK), 1) % CHUNK
    return si <= qi


def _fold(full, from_cur, pen=0.0):
    return jnp.where(from_cur, full[CHUNK:2 * CHUNK], full[0:CHUNK] + pen)


def _unfold(t, from_cur):
    c = jnp.where(from_cur, t, 0.0)
    return jnp.concatenate([t - c, c], axis=0)


def _softmax_sink(s, sink):
    mx = jnp.maximum(jnp.max(s, axis=0, keepdims=True), sink)
    p = jnp.exp(s - mx)
    esink = jnp.exp(sink - mx)
    inv = 1.0 / (jnp.sum(p, axis=0, keepdims=True) + esink)
    return p * inv, esink * inv


def _swa_inputs(blk, p_ref, prev_ref, tab_ref, ptab_ref):
    tab = tab_ref[...]
    qr = _rope(p_ref[:, A_Q:A_Q + 1024], tab) * ATT_SCALE
    kk = jnp.concatenate([_rope(prev_ref[:, 0:256], ptab_ref[...]), _rope(p_ref[:, A_K:A_K + 256], tab)], axis=0)
    vv = jnp.concatenate([prev_ref[:, 256:512], p_ref[:, A_V:A_V + 256]], axis=0)
    return tab, qr, kk, vv, jnp.where(blk > 0, 0.0, NEG)


def _swa_forward_step(sink_ref, p_ref, prev_ref, tab_ref, ptab_ref, y_ref):
    n = pl.program_id(0)
    _, qr, kk, vv, pen = _swa_inputs(n, p_ref, prev_ref, tab_ref, ptab_ref)
    from_cur = _from_current()
    outs = []
    for j in range(KV_HEADS):
        s = _fold(_mm_nt(_kv_native(kk, j), _stack_heads(qr, j)), from_cur, pen)
        P, _ = _softmax_sink(s, _sink_row(sink_ref, j))
        outs.append(_unstack_heads(_mm_tn(_unfold(P, from_cur), _kv_native(vv, j)), j))
    g = p_ref[:, A_G:A_G + 1024]
    y_ref[:, D_SSD:D_SSD + D_ATT] = (jnp.concatenate(outs, axis=1) * (g * _sigmoid(g))).astype(y_ref.dtype)


def _mixer_forward(proj_ssd, proj_att, tabs, sinks, conv_w, conv_b, dt_bias, a_log, d_skip, norm_w, E, comm=None):
    L = proj_ssd.shape[0]
    nc = L // CHUNK

    def body(p_ref, halo_ref, cw_ref, cb_ref, dtb_ref, alog_ref, dsk_ref, nw_ref, e_ref,
             sink_ref, pa_ref, prev_ref, tab_ref, ptab_ref, y_ref, ypre_ref, hprev_ref, pre_ref, h_scr, ext_scr):
        _ssd_forward_step(p_ref, halo_ref, cw_ref, cb_ref, dtb_ref, alog_ref, dsk_ref, nw_ref, e_ref,
                          y_ref, ypre_ref, hprev_ref, pre_ref, h_scr, ext_scr)
        _swa_forward_step(sink_ref, pa_ref, prev_ref, tab_ref, ptab_ref, y_ref)

    const = lambda shape: pl.BlockSpec(shape, lambda c: (0, 0))
    smem = pl.BlockSpec(memory_space=pltpu.SMEM)
    rows = lambda w: pl.BlockSpec((CHUNK, w), lambda c: (c, 0))
    return _call(
        body, comm, name="mixer_fwd", grid=(nc,),
        in_specs=[rows(S_W), pl.BlockSpec((8, S_W), lambda c: (jnp.maximum(c * 16 - 1, 0), 0)),
                  const((4, D_XBC)), const((1, D_XBC)), smem, smem, smem, const((1, 1024)), const((128, 1024)),
                  smem, rows(A_W), pl.BlockSpec((CHUNK, 512), lambda c: (jnp.maximum(c - 1, 0), 2)),
                  rows(384), pl.BlockSpec((CHUNK, 384), lambda c: (jnp.maximum(c - 1, 0), 0))],
        out_specs=[rows(D_SSD + D_ATT), rows(D_SSD), pl.BlockSpec((128, 1024), lambda c: (c, 0)), rows(D_XBC)],
        out_shape=[jax.ShapeDtypeStruct((L, D_SSD + D_ATT), _MXU), jax.ShapeDtypeStruct((L, D_SSD), F32),
                   jax.ShapeDtypeStruct((nc * 128, 1024), F32), jax.ShapeDtypeStruct((L, D_XBC), F32)],
        scratch_shapes=[pltpu.VMEM((128, 1024), F32), pltpu.VMEM((136, D_XBC), F32)],
        args=(proj_ssd, proj_ssd, conv_w, conv_b, dt_bias, a_log, d_skip, norm_w, E,
              sinks, proj_att, proj_att, tabs, tabs))


def _swa_backward(proj_att, tabs, sinks, dy, reduce=None):
    L = proj_att.shape[0]
    nb = L // CHUNK

    def body(sink_ref, p_ref, prev_ref, tab_ref, ptab_ref, dy_ref, dp_ref, dsink_ref, carry_k, carry_v):
        i = pl.program_id(0)
        n = nb - 1 - i

        @pl.when(i == 0)
        def _():
            carry_k[...] = jnp.zeros_like(carry_k)
            carry_v[...] = jnp.zeros_like(carry_v)
            dsink_ref[...] = jnp.zeros_like(dsink_ref)

        tab, qr, kk, vv, pen = _swa_inputs(n, p_ref, prev_ref, tab_ref, ptab_ref)
        from_cur = _from_current()
        g = p_ref[:, A_G:A_G + 1024]
        sgm = _sigmoid(g)
        dyv = dy_ref[...]
        do_all = dyv * (g * sgm)
        lane8 = lax.broadcasted_iota(jnp.int32, (8, 128), 1)
        hid = lax.broadcasted_iota(jnp.int32, (1, 4 * CHUNK), 1) // CHUNK
        o_parts, dq_parts = [], []
        dk_nat = [jnp.zeros((2 * CHUNK, 128), F32) for _ in range(2)]
        dv_nat = [jnp.zeros((2 * CHUNK, 128), F32) for _ in range(2)]
        dsink = jnp.zeros((8, 128), F32)
        for j in range(KV_HEADS):
            qs = _stack_heads(qr, j)
            kkb, vvb = _kv_native(kk, j), _kv_native(vv, j)
            P, psink = _softmax_sink(_fold(_mm_nt(kkb, qs), from_cur, pen), _sink_row(sink_ref, j))
            p_full = _unfold(P, from_cur)
            o_parts.append(_unstack_heads(_mm_tn(p_full, vvb), j))
            do_s = _stack_heads(do_all, j)
            dP = _fold(_mm_nt(vvb, do_s), from_cur)
            D = jnp.sum(P * dP, axis=0, keepdims=True)
            ds_full = _unfold(P * (dP - D), from_cur)
            sd = psink * D
            for hh in range(4):
                dsink = dsink + jnp.where(lane8 == 4 * j + hh, -jnp.sum(jnp.where(hid == hh, sd, 0.0)), 0.0)
            dq_parts.append(_unstack_heads(_mm_tn(ds_full, kkb), j) * ATT_SCALE)
            dk_nat[j // 2] = dk_nat[j // 2] + _keep_native(_mm(ds_full, qs), j)
            dv_nat[j // 2] = dv_nat[j // 2] + _keep_native(_mm(p_full, do_s), j)
        o = jnp.concatenate(o_parts, axis=1)
        dkk = jnp.concatenate(dk_nat, axis=1)
        dvv = jnp.concatenate(dv_nat, axis=1)
        out = dp_ref.dtype
        dp_ref[:, A_Q:A_Q + 1024] = _rope_transposed(jnp.concatenate(dq_parts, axis=1), tab).astype(out)
        dp_ref[:, A_K:A_K + 256] = _rope_transposed(dkk[CHUNK:2 * CHUNK] + carry_k[...], tab).astype(out)
        dp_ref[:, A_V:A_V + 256] = (dvv[CHUNK:2 * CHUNK] + carry_v[...]).astype(out)
        dp_ref[:, A_G:A_G + 1024] = (dyv * o * (sgm * (1.0 + g * (1.0 - sgm)))).astype(out)
        carry_k[...] = dkk[0:CHUNK]
        carry_v[...] = dvv[0:CHUNK]
        dsink_ref[...] += dsink

    rev = lambda i: (nb - 1 - i, 0)
    prev = lambda i: jnp.maximum(nb - 2 - i, 0)
    return _call(
        body, None, name="swa_bwd", grid=(nb,),
        in_specs=[pl.BlockSpec(memory_space=pltpu.SMEM),
                  pl.BlockSpec((CHUNK, A_W), rev),
                  pl.BlockSpec((CHUNK, 512), lambda i: (prev(i), 2)),
                  pl.BlockSpec((CHUNK, 384), rev),
                  pl.BlockSpec((CHUNK, 384), lambda i: (prev(i), 0)),
                  pl.BlockSpec((CHUNK, D_ATT), lambda i: (nb - 1 - i, 1))],
        out_specs=[pl.BlockSpec((CHUNK, A_W), rev), pl.BlockSpec((8, 128), lambda i: (0, 0))],
        out_shape=[jax.ShapeDtypeStruct((L, A_W), _MXU), jax.ShapeDtypeStruct((8, 128), F32)],
        scratch_shapes=[pltpu.VMEM((CHUNK, 256), F32), pltpu.VMEM((CHUNK, 256), F32)],
        args=(sinks, proj_att, proj_att, tabs, tabs, dy), reduce=reduce)


def _head(y, x, target, w_out, ln_g, ln_b, *, tm):
    L = x.shape[0]
    nsteps = L // tm

    def body(y_ref, x_ref, t_ref, wo_ref, g_ref, b_ref, dr_ref, dy_ref, acc_ref):
        i = pl.program_id(0)

        @pl.when(i == 0)
        def _():
            acc_ref[...] = jnp.zeros_like(acc_ref)

        r = ALPHA * x_ref[...] + _mm(y_ref[...], wo_ref[...])
        mu = jnp.mean(r, axis=-1, keepdims=True)
        d = r - mu
        rstd = lax.rsqrt(jnp.mean(d * d, axis=-1, keepdims=True) + LN_EPS)
        xh = d * rstd
        gam = g_ref[0:1, :]
        e = xh * gam + b_ref[0:1, :] - t_ref[...]
        dout = e * (1.0 / D_MODEL)
        dxh = dout * gam
        dr = rstd * (dxh - jnp.mean(dxh, axis=-1, keepdims=True)
                     - xh * jnp.mean(dxh * xh, axis=-1, keepdims=True))
        dr_ref[...] = dr
        dy_ref[...] = _mm_nt(dr, wo_ref[...])
        acc_ref[...] += _rows8([_colsum(dout * xh), _colsum(dout), _colsum(e * e) * (0.5 / D_MODEL)])

        @pl.when(i == nsteps - 1)
        def _():
            acc = acc_ref[...]
            tot = jnp.sum(acc[2:3, :])
            rid = lax.broadcasted_iota(jnp.int32, (8, 1024), 0)
            acc_ref[...] = jnp.where(rid == 3, tot, acc)

    const = lambda shape: pl.BlockSpec(shape, lambda i: (0, 0))
    row = lambda w: pl.BlockSpec((tm, w), lambda i: (i, 0))
    return pl.pallas_call(
        body, name="head", grid=(nsteps,),
        in_specs=[row(2048), row(1024), row(1024), const((2048, 1024)), const((1, 1024)), const((1, 1024))],
        out_specs=[row(1024), row(2048), const((8, 1024))],
        out_shape=[jax.ShapeDtypeStruct((L, D_MODEL), F32), jax.ShapeDtypeStruct((L, 2048), F32),
                   jax.ShapeDtypeStruct((8, 1024), F32)],
        compiler_params=_params(("arbitrary",)),
    )(y, x, target, w_out, ln_g, ln_b)


def _gather_w_in(w_shard, positions):
    R = w_shard.shape[0]
    halves = (pl.ds(0, R // 2), pl.ds(R // 2, R // 2))
    any_spec = pl.BlockSpec(memory_space=pl.ANY)
    vmem = pl.BlockSpec(memory_space=pltpu.VMEM)

    def body(in_ref, pos_ref, inv_ref, out_ref, tab_ref, tab_scr, send_sems, recv_sems, local_sem, tab_sem):
        x, y, c = _position()

        def slot(p, half=None):
            s = out_ref.at[_index(*p)]
            return s if half is None else s.at[halves[half]]

        def same_core(p):
            return (p[0], p[1], c)

        def other_core(p):
            return (p[0], p[1], 1 - c)

        me, xn, yn, dg = (x, y), (1 - x, y), (x, 1 - y), (1 - x, 1 - y)

        def copy(k, dst, to, src=None):
            return _remote(dst if src is None else src, dst, send_sems.at[k], recv_sems.at[k], to)

        local = pltpu.make_async_copy(in_ref, slot(same_core(me)), local_sem)
        local.start()
        own = [copy(0, slot(same_core(me)), other_core(me), in_ref), copy(1, slot(same_core(me)), same_core(xn), in_ref),
               copy(2, slot(same_core(me)), same_core(yn), in_ref)]
        for cp in own:
            cp.start()
        _rope_tables(pos_ref, inv_ref, tab_scr)
        tab_out = pltpu.make_async_copy(tab_scr, tab_ref, tab_sem)
        tab_out.start()
        copy(1, slot(same_core(xn)), same_core(xn)).wait_recv()
        passed = [copy(4, slot(same_core(xn), 1), same_core(yn)), copy(5, slot(same_core(xn)), other_core(me))]
        for cp in passed:
            cp.start()
        copy(2, slot(same_core(yn)), same_core(yn)).wait_recv()
        more = [copy(3, slot(same_core(yn), 0), same_core(xn)), copy(6, slot(same_core(yn)), other_core(me))]
        for cp in more:
            cp.start()
        passed += more
        for k, half in ((3, 0), (4, 1)):
            copy(k, slot(same_core(dg), half), same_core(xn)).wait_recv()
            fwd = copy(7 + half, slot(same_core(dg), half), other_core(me))
            fwd.start()
            passed.append(fwd)
        copy(0, slot(other_core(me)), other_core(me)).wait_recv()
        copy(5, slot(other_core(xn)), other_core(me)).wait_recv()
        copy(6, slot(other_core(yn)), other_core(me)).wait_recv()
        for half in (0, 1):
            copy(7 + half, slot(other_core(dg), half), other_core(me)).wait_recv()
        for cp in own + passed:
            cp.wait_send()
        local.wait()
        tab_out.wait()

    return pl.pallas_call(
        body, name="gather_w_in", in_specs=[any_spec, vmem, vmem], out_specs=[any_spec, any_spec],
        out_shape=[jax.ShapeDtypeStruct((N_DEV,) + w_shard.shape, w_shard.dtype),
                   jax.ShapeDtypeStruct((positions.size, 384), F32)],
        scratch_shapes=[pltpu.VMEM((positions.size, 384), F32), pltpu.SemaphoreType.DMA((9,)),
                        pltpu.SemaphoreType.DMA((9,)), pltpu.SemaphoreType.DMA, pltpu.SemaphoreType.DMA],
        compiler_params=_params(),
    )(w_shard, positions, jnp.asarray(ROPE_INV)[None, :])


def _input_gradient(d_ssd, d_att, w_ssd, w_att, dr, *, tm, comm=None, reduce=None):
    L = dr.shape[0]

    def body(ds_ref, da_ref, ws_ref, wa_ref, dr_ref, o_ref):
        o_ref[...] = ALPHA * dr_ref[...] + _mm_nt(ds_ref[...], ws_ref[...]) + _mm_nt(da_ref[...], wa_ref[...])

    row = lambda w: pl.BlockSpec((tm, w), lambda i: (i, 0))
    resident = lambda a: pl.BlockSpec(a.shape, lambda i: (0, 0), pipeline_mode=pl.Buffered(1))
    return _call(body, comm, name="dx", grid=(L // tm,),
                 in_specs=[row(S_W), row(A_W), resident(w_ssd), resident(w_att), row(D_MODEL)],
                 out_specs=[row(D_MODEL)], out_shape=[jax.ShapeDtypeStruct((L, D_MODEL), F32)],
                 scratch_shapes=[], args=(d_ssd, d_att, w_ssd, w_att, dr), reduce=reduce)


SHARD_COLS = D_IN_PROJ // N_DEV
SPLIT = N_SSD_REAL - 4 * SHARD_COLS
RELAYOUT_ROWS = 256


def _unpack_w_in(w_all):
    def body(g_ref, ws_ref, wa_ref):
        for j in range(4):
            ws_ref[:, SHARD_COLS * j:SHARD_COLS * (j + 1)] = g_ref[j]
        ws_ref[:, 4 * SHARD_COLS:N_SSD_REAL] = g_ref[4, :, 0:SPLIT]
        ws_ref[:, N_SSD_REAL:S_W] = jnp.zeros((RELAYOUT_ROWS, S_W - N_SSD_REAL), ws_ref.dtype)
        wa_ref[:, 0:SHARD_COLS - SPLIT] = g_ref[4, :, SPLIT:SHARD_COLS]
        for j in range(5, N_DEV):
            lo = SHARD_COLS * (j - 4) - SPLIT
            wa_ref[:, lo:lo + SHARD_COLS] = g_ref[j]

    return pl.pallas_call(
        body, name="unpack_w_in", grid=(D_MODEL // RELAYOUT_ROWS,),
        in_specs=[pl.BlockSpec((N_DEV, RELAYOUT_ROWS, SHARD_COLS), lambda i: (0, i, 0))],
        out_specs=[pl.BlockSpec((RELAYOUT_ROWS, S_W), lambda i: (i, 0)), pl.BlockSpec((RELAYOUT_ROWS, A_W), lambda i: (i, 0))],
        out_shape=[jax.ShapeDtypeStruct((D_MODEL, S_W), w_all.dtype), jax.ShapeDtypeStruct((D_MODEL, A_W), w_all.dtype)],
        compiler_params=_params(("arbitrary",)),
    )(w_all)


def _dw_in(xb, d, half, tail=None, *, tl=1024):
    L, N = d.shape
    steps = L // tl

    def body(x_ref, d_ref, *refs):
        if half == 0:
            p_ref, tail_ref, acc, p_scr, p_sems = refs
        else:
            t_ref, p_ref, acc, p_scr, p_sems = refs
        l = pl.program_id(0)

        @pl.when(l == 0)
        def _():
            acc[...] = jnp.zeros_like(acc)

        acc[...] += _mm_tn(x_ref[...], d_ref[...])

        @pl.when(l == steps - 1)
        def _():
            if half == 0:
                tail_ref[...] = acc[:, S_DT:S_W]
            outs = []
            for j in range(4):
                if half == 0:
                    pieces = [(0, acc[:, SHARD_COLS * j:SHARD_COLS * (j + 1)])]
                elif j == 0:
                    pieces = [(0, t_ref[:, 4 * SHARD_COLS - S_DT:N_SSD_REAL - S_DT]), (SPLIT, acc[:, 0:SHARD_COLS - SPLIT])]
                else:
                    lo = SHARD_COLS * j - SPLIT
                    pieces = [(0, acc[:, lo:lo + SHARD_COLS])]
                for off, blk in pieces:
                    p_scr[j, :, off:off + blk.shape[1]] = blk
                outs.append(pltpu.make_async_copy(p_scr.at[j], p_ref.at[j], p_sems.at[j]))
                outs[-1].start()
            for cp in outs:
                cp.wait()

    once = pl.Buffered(1)
    whole = lambda shape: pl.BlockSpec(shape, lambda l: (0,) * len(shape), pipeline_mode=once)
    in_specs = [pl.BlockSpec((tl, D_MODEL), lambda l: (l, 0)), pl.BlockSpec((tl, N), lambda l: (l, 0))]
    args = [xb, d]
    stack = jax.ShapeDtypeStruct((4, D_MODEL, SHARD_COLS), F32)
    out_shape, out_specs = [stack], [pl.BlockSpec(memory_space=pl.ANY)]
    if half == 0:
        out_shape.append(jax.ShapeDtypeStruct((D_MODEL, S_W - S_DT), F32))
        out_specs.append(whole(out_shape[-1].shape))
    else:
        in_specs.append(whole(tail.shape))
        args.append(tail)
    return pl.pallas_call(
        body, name="dw_in_%d" % half, grid=(steps,), in_specs=in_specs, out_specs=out_specs, out_shape=out_shape,
        scratch_shapes=[pltpu.VMEM((D_MODEL, N), F32), pltpu.VMEM(stack.shape, F32), pltpu.SemaphoreType.DMA((4,))],
        compiler_params=_params(("arbitrary",)),
    )(*args)


def _adamw_math(w, g, m, v):
    m = ADAM_B1 * m + (1.0 - ADAM_B1) * g
    v = ADAM_B2 * v + (1.0 - ADAM_B2) * (g * g)
    m_hat = m / (1.0 - ADAM_B1 ** ADAM_STEP)
    v_hat = v / (1.0 - ADAM_B2 ** ADAM_STEP)
    delta = -ADAM_LR * (m_hat / (jnp.sqrt(v_hat) + ADAM_EPS) + ADAM_WD * w)
    return delta, m, v


def _adamw_shard(g_own, recv, w, m, v, *, rows, name):
    R, C = g_own.shape

    def body(g_ref, r_ref, w_ref, m_ref, v_ref, go_ref, d_ref, mo_ref, vo_ref):
        g = g_ref[...]
        for k in range(N_DEV - 1):
            g = g + r_ref[k].astype(F32)
        d, mn, vn = _adamw_math(w_ref[...], g, m_ref[...], v_ref[...])
        go_ref[...] = g
        d_ref[...] = d
        mo_ref[...] = mn
        vo_ref[...] = vn

    blk = pl.BlockSpec((rows, C), lambda i: (i, 0))
    return pl.pallas_call(
        body, name=name, grid=(R // rows,),
        in_specs=[blk, pl.BlockSpec((N_DEV - 1, rows, C), lambda i: (0, i, 0)), blk, blk, blk],
        out_specs=[blk] * 4, out_shape=[jax.ShapeDtypeStruct((R, C), F32)] * 4,
        compiler_params=_params(("arbitrary",)),
    )(g_own, recv, w, m, v)


def _minor_rows_view(a):
    return jnp.transpose(a, (2, 0, 1)).reshape(SHARD_COLS * 8, 128)


def _from_minor_rows_view(v):
    return jnp.transpose(v.reshape(SHARD_COLS, 8, 128), (1, 2, 0)).reshape(1, D_MODEL, SHARD_COLS)


def _adamw_w_in(is_lo, own_lo, own_hi, recv_lo, recv_hi, w, m, v):
    C = SHARD_COLS
    pad = -C % 128
    bands = [pl.ds(q * 128, 128) for q in range(D_MODEL // 128)]

    def body(lo_ref, ol_ref, oh_ref, rl_ref, rh_ref, w_ref, m_ref, v_ref, go_ref, d_ref, mo_ref, vo_ref,
             own_scr, recv_scr, sems):
        def fetch(own_src, recv_src):
            return [(pltpu.make_async_copy(own_src.at[band, :], own_scr.at[band, :], sems.at[0, q]),
                     pltpu.make_async_copy(recv_src.at[:, band, :], recv_scr.at[:, band, :], sems.at[1, q]))
                    for q, band in enumerate(bands)]

        from_lo, from_hi = fetch(ol_ref, rl_ref), fetch(oh_ref, rh_ref)

        @pl.when(lo_ref[0] == 1)
        def _():
            for pair in from_lo:
                for cp in pair:
                    cp.start()

        @pl.when(lo_ref[0] != 1)
        def _():
            for pair in from_hi:
                for cp in pair:
                    cp.start()

        for q, band in enumerate(bands):
            for cp in from_lo[q]:
                cp.wait()
            g = own_scr[band, :] + recv_scr[0, band, :].astype(F32) + recv_scr[1, band, :].astype(F32)
            g = jnp.pad(g, ((0, 0), (0, pad))).T[0:C]
            rows = pl.ds(q, C, stride=8)
            d, mn, vn = _adamw_math(w_ref[rows, :], g, m_ref[rows, :], v_ref[rows, :])
            go_ref[rows, :] = g
            d_ref[rows, :] = d
            mo_ref[rows, :] = mn
            vo_ref[rows, :] = vn

    vmem = pl.BlockSpec(memory_space=pltpu.VMEM)
    return pl.pallas_call(
        body, name="adamw_w_in", out_shape=[jax.ShapeDtypeStruct(w.shape, F32)] * 4,
        in_specs=[pl.BlockSpec(memory_space=pltpu.SMEM)] + [pl.BlockSpec(memory_space=pl.ANY)] * 4 + [vmem] * 3,
        out_specs=[vmem] * 4,
        scratch_shapes=[pltpu.VMEM(own_lo.shape, F32), pltpu.VMEM(recv_lo.shape, BF16),
                        pltpu.SemaphoreType.DMA((2, len(bands)))],
        compiler_params=_params(),
    )(is_lo, own_lo, own_hi, recv_lo, recv_hi, w, m, v)


SMALL = ("conv_b", "dt_bias", "a_log", "d_skip", "ssd_norm_w", "attn_sinks", "ln_g", "ln_b")


def _adamw_small(gathered, params):
    n_p = len(SMALL)

    def body(*refs):
        acc = []
        for r in refs[:5]:
            t = r[0]
            for k in range(1, N_DEV):
                t = t + r[k]
            acc.append(t)
        head, conv, norm, scal, sink = acc
        grads = dict(conv_b=conv[4:5, :], dt_bias=scal[0:1, 0:N_HEADS], a_log=scal[1:2, 0:N_HEADS],
                     d_skip=scal[2:3, 0:N_HEADS], ssd_norm_w=norm[0:1, :], attn_sinks=sink[0:1, 0:N_HEADS],
                     ln_g=head[0:1, :], ln_b=head[1:2, :])
        wmv = refs[5:5 + 3 * n_p]
        outs = refs[5 + 3 * n_p:]
        outs[0][...] = head[3:4, 0:1]
        outs[1][...] = conv[0:4, :]
        for i, name in enumerate(SMALL):
            w_ref, m_ref, v_ref = wmv[3 * i:3 * i + 3]
            g = grads[name]
            d, mn, vn = _adamw_math(w_ref[...], g, m_ref[...], v_ref[...])
            for o_ref, val in zip(outs[2 + 4 * i:6 + 4 * i], (g, d, mn, vn)):
                o_ref[...] = val

    flat = [a for name in SMALL for a in params[name]]
    out_shape = [jax.ShapeDtypeStruct((1, 1), F32), jax.ShapeDtypeStruct((4, D_XBC), F32)]
    for name in SMALL:
        out_shape += [jax.ShapeDtypeStruct(params[name][0].shape, F32)] * 4
    res = pl.pallas_call(body, name="adamw_small", out_shape=out_shape, compiler_params=_params())(*gathered, *flat)
    return res[0], res[1], {name: res[2 + 4 * i:6 + 4 * i] for i, name in enumerate(SMALL)}


def _adamw_plain(g, w, m, v):
    def body(g_ref, w_ref, m_ref, v_ref, d_ref, mo_ref, vo_ref):
        d, mn, vn = _adamw_math(w_ref[...], g_ref[...], m_ref[...], v_ref[...])
        d_ref[...] = d
        mo_ref[...] = mn
        vo_ref[...] = vn

    return pl.pallas_call(
        body, name="adamw_conv_w", out_shape=[jax.ShapeDtypeStruct(w.shape, F32)] * 3,
        compiler_params=_params(),
    )(g, w, m, v)


def _lane_pattern(fn):
    return np.asarray([fn(l % HEAD_DIM) for l in range(128)], np.float32)


ROPE_INV = _lane_pattern(lambda r: ROPE_THETA ** (-2.0 * (r % 8) / ROPE_DIM) if r < ROPE_DIM else 0.0)


def _rope_tables(pos_ref, inv_ref, tab_ref):
    lane = lax.broadcasted_iota(jnp.int32, (1, 128), 1) % HEAD_DIM
    upper = jnp.where((lane >= ROPE_DIM // 2) & (lane < ROPE_DIM), 1.0, 0.0)
    lower = jnp.where(lane < ROPE_DIM // 2, -1.0, 0.0)

    def block(r, carry):
        rows = pl.ds(pl.multiple_of(r * CHUNK, CHUNK), CHUNK)
        pos = jnp.broadcast_to(pos_ref[pl.ds(r, 1), :].astype(F32), (CHUNK, 128)).T
        ang = pos * inv_ref[...]
        sn = jnp.sin(ang)
        tab_ref[rows, 0:128] = jnp.cos(ang)
        tab_ref[rows, 128:256] = sn * upper
        tab_ref[rows, 256:384] = sn * lower
        return carry

    lax.fori_loop(0, pos_ref.shape[0], block, 0)


def _expansion():
    E = np.arange(1024)[None, :] // HEAD_DIM == np.arange(128)[:, None]
    return jnp.asarray(E, BF16), jnp.asarray(E.T, BF16)


def _ssd_args(conv_w, conv_b, dt_bias, a_log, d_skip, norm_w, E):
    return (conv_w, conv_b, dt_bias.reshape(-1), a_log.reshape(-1), d_skip.reshape(-1), norm_w, E)


def kernel(x, positions, w_in, conv_w, conv_b, dt_bias, a_log, d_skip, ssd_norm_w, attn_sinks, w_out, ln_g, ln_b, loss_target, m_w_in, m_conv_w, m_conv_b, m_dt_bias, m_a_log, m_d_skip, m_ssd_norm_w, m_attn_sinks, m_w_out, m_ln_g, m_ln_b, v_w_in, v_conv_w, v_conv_b, v_dt_bias, v_a_log, v_d_skip, v_ssd_norm_w, v_attn_sinks, v_w_out, v_ln_g, v_ln_b):
    me = _index(*_position())
    x0, target = x[0], loss_target[0]
    bf16_shard = lambda shape: jax.ShapeDtypeStruct(shape, BF16)
    E, ET = _expansion()
    sinks = attn_sinks.reshape(-1)

    w_all, tabs = _gather_w_in(w_in[0].astype(BF16), positions[0].reshape(-1, 128))
    w_ssd, w_att = _unpack_w_in(w_all)
    gather_conv_w = _Hosted([conv_w[0]], [jax.ShapeDtypeStruct((N_DEV,) + conv_w.shape[1:], F32)],
                            [_Flow("gather", 0, 0)])

    proj_ssd, proj_att, xb, conv_w_all = _in_proj(x0, w_ssd, w_att, tm=512, comm=gather_conv_w)
    conv_w_f = jnp.transpose(conv_w_all, (1, 0, 2)).reshape(4, D_XBC)
    ssd_args = _ssd_args(conv_w_f, conv_b, dt_bias, a_log, d_skip, ssd_norm_w, E)
    gather_w_out = _Hosted([w_out[0].astype(BF16)], [bf16_shard((N_DEV, 256, D_MODEL))], [_Flow("gather", 0, 0)])
    y, ypre, hprev, pre, w_out_all = _mixer_forward(proj_ssd, proj_att, tabs, sinks, *ssd_args, comm=gather_w_out)
    w_out_f = w_out_all.reshape(2 * D_MODEL, D_MODEL)
    dr, dy, acc_head = _head(y, x0, target, w_out_f, ln_g, ln_b, tm=512)

    dw_out, dw_out_bf16 = _matmul_tn(y, dr, tl=1024, tn=D_MODEL, name="dw_out", emit_bf16=True)
    own_out = lax.dynamic_index_in_dim(dw_out.reshape(N_DEV, 256, D_MODEL), me, axis=0, keepdims=False)
    send_out = _Hosted([dw_out_bf16.reshape(N_DEV, 256, D_MODEL)], [bf16_shard((N_DEV - 1, 256, D_MODEL))],
                       [_Flow("exchange", 0, 0)])
    d_ssd, acc_cw, acc_w, acc_s, recv_out = _ssd_backward(proj_ssd, hprev, ypre, pre, dy, *ssd_args, ET, comm=send_out)
    stack_lo, dw_dt_block = _dw_in(xb, d_ssd, 0)
    d_att, dsink, own_lo, recv_lo = _swa_backward(proj_att, tabs, sinks, dy, reduce=_OwnerReduce(stack_lo, 0))
    (stack_hi,) = _dw_in(xb, d_att, 1, dw_dt_block)
    accs = [acc_head, acc_cw, acc_w, acc_s, dsink]
    gather_accs = _Hosted(accs, [jax.ShapeDtypeStruct((N_DEV,) + a.shape, F32) for a in accs],
                          [_Flow("gather", i, i) for i in range(5)])
    dx, *gathered, own_hi, recv_hi = _input_gradient(d_ssd, d_att, w_ssd, w_att, dr, tm=256, comm=gather_accs,
                                                     reduce=_OwnerReduce(stack_hi, 1))
    is_lo = (me < 4).reshape(1).astype(jnp.int32)

    g_in, d_in, nm_in, nv_in = [_from_minor_rows_view(r) for r in _adamw_w_in(
        is_lo, own_lo, own_hi, recv_lo, recv_hi, _minor_rows_view(w_in), _minor_rows_view(m_w_in), _minor_rows_view(v_w_in))]
    g_out, d_out, nm_out, nv_out = _adamw_shard(own_out, recv_out, w_out[0], m_w_out[0], v_w_out[0],
                                                rows=256, name="adamw_w_out")
    loss, g_conv_w, small = _adamw_small(gathered, dict(
        conv_b=(conv_b, m_conv_b, v_conv_b), dt_bias=(dt_bias, m_dt_bias, v_dt_bias), a_log=(a_log, m_a_log, v_a_log),
        d_skip=(d_skip, m_d_skip, v_d_skip), ssd_norm_w=(ssd_norm_w, m_ssd_norm_w, v_ssd_norm_w),
        attn_sinks=(attn_sinks, m_attn_sinks, v_attn_sinks), ln_g=(ln_g, m_ln_g, v_ln_g), ln_b=(ln_b, m_ln_b, v_ln_b)))
    g_cw = lax.dynamic_slice_in_dim(g_conv_w, me * (D_XBC // N_DEV), D_XBC // N_DEV, axis=1)
    d_cw, nm_cw, nv_cw = _adamw_plain(g_cw, conv_w[0], m_conv_w[0], v_conv_w[0])

    def leaves(i, big_in, cw, big_out):
        mid = [small[k][i] for k in ("conv_b", "dt_bias", "a_log", "d_skip", "ssd_norm_w", "attn_sinks")]
        return [big_in, cw[None]] + mid + [big_out[None], small["ln_g"][i], small["ln_b"][i]]

    return (loss.reshape(()), dx[None], *leaves(0, g_in, g_cw, g_out), *leaves(1, d_in, d_cw, d_out),
            *leaves(2, nm_in, nm_cw, nm_out), *leaves(3, nv_in, nv_cw, nv_out))
```

```python
import jax
import jax.numpy as jnp
from jax import lax
from jax.experimental import pallas as pl
from jax.experimental.pallas import tpu as pltpu
import numpy as np

F32 = jnp.float32
BF16 = jnp.bfloat16
_MXU = jnp.bfloat16

N_DEV = 8
D_MODEL = 1024
D_SSD = 1024
D_ATT = 1024
HEAD_DIM = 64
N_HEADS = 16
SSD_GROUPS = 2
KV_HEADS = 4
CHUNK = 128
D_XBC = 1536
D_IN_PROJ = 5136
ROPE_DIM = 16
ROPE_THETA = 500000.0
ALPHA = (2.0 * 1) ** 0.25
LN_EPS = 1e-5
RMS_EPS = 1e-5
ATT_SCALE = HEAD_DIM ** -0.5
NEG = -1e30

S_Z, S_XS, S_B, S_C, S_DT, S_W = 0, 1024, 2048, 2304, 2560, 2816
N_SSD_REAL = 2576
A_Q, A_K, A_V, A_G, A_W = 0, 1024, 1280, 1536, 2560

ADAM_LR = 0.001
ADAM_B1 = 0.9
ADAM_B2 = 0.999
ADAM_EPS = 1e-08
ADAM_WD = 0.01
ADAM_STEP = 10

VMEM_LIMIT = 48 * 1024 * 1024
MESH = pl.DeviceIdType.MESH


def _params(sem=None):
    return pltpu.CompilerParams(dimension_semantics=sem, vmem_limit_bytes=VMEM_LIMIT)


def _mm(a, b):
    return jnp.dot(a.astype(_MXU), b.astype(_MXU), preferred_element_type=F32)


def _mm_nt(a, b):
    return lax.dot_general(a.astype(_MXU), b.astype(_MXU), (((1,), (1,)), ((), ())),
                           preferred_element_type=F32)


def _mm_tn(a, b):
    return lax.dot_general(a.astype(_MXU), b.astype(_MXU), (((0,), (0,)), ((), ())),
                           preferred_element_type=F32)


def _split3(v):
    hi = v.astype(BF16)
    r = v - hi.astype(F32)
    mid = r.astype(BF16)
    lo = (r - mid.astype(F32)).astype(BF16)
    return hi, mid, lo


def _mm_exact_r(v, p01):
    hi, mid, lo = _split3(v)
    d = lambda a: jnp.dot(a, p01, preferred_element_type=F32)
    return d(hi) + d(mid) + d(lo)


def _mm_exact_l(p01, v):
    hi, mid, lo = _split3(v)
    d = lambda a: jnp.dot(p01, a, preferred_element_type=F32)
    return d(hi) + d(mid) + d(lo)


def _mm_2pass_r(v, p01):
    hi = v.astype(BF16)
    lo = (v - hi.astype(F32)).astype(BF16)
    return jnp.dot(hi, p01, preferred_element_type=F32) + jnp.dot(lo, p01, preferred_element_type=F32)


def _sigmoid(x):
    return 1.0 / (1.0 + jnp.exp(-x))


def _softplus(x):
    e = jnp.exp(-jnp.abs(x))
    u = 1.0 + e
    log1p = jnp.where(u == 1.0, e, jnp.log(u) * (e / (u - 1.0)))
    return jnp.maximum(x, 0.0) + log1p


def _rows8(rows):
    n = rows[0].shape[1]
    rid = lax.broadcasted_iota(jnp.int32, (8, n), 0)
    out = jnp.zeros((8, n), F32)
    for k, r in enumerate(rows):
        out = out + jnp.where(rid == k, r, 0.0)
    return out


def _colsum(a):
    return jnp.sum(a, axis=0, keepdims=True)


def _in_proj(x, w_ssd, w_att, *, tm, comm=None):
    L, K = x.shape

    def body(x_ref, ws_ref, wa_ref, ps_ref, pa_ref):
        xb = x_ref[...].astype(_MXU)
        ps_ref[...] = jnp.dot(xb, ws_ref[...], preferred_element_type=F32)
        pa_ref[...] = jnp.dot(xb, wa_ref[...], preferred_element_type=F32)

    row = lambda w: pl.BlockSpec((tm, w), lambda i: (i, 0))
    resident = lambda a: pl.BlockSpec(a.shape, lambda i: (0, 0), pipeline_mode=pl.Buffered(1))
    return _call(
        body, comm, name="in_proj", grid=(L // tm,),
        in_specs=[row(K), resident(w_ssd), resident(w_att)], out_specs=[row(S_W), row(A_W)],
        out_shape=[jax.ShapeDtypeStruct((L, S_W), F32), jax.ShapeDtypeStruct((L, A_W), F32)],
        scratch_shapes=[], args=(x, w_ssd, w_att))


def _matmul_tn(a, g, *, tl, tn, name, emit_bf16=False):
    L, M = a.shape
    N = g.shape[1]
    last = L // tl - 1

    def body(a_ref, g_ref, o_ref, *rest):
        @pl.when(pl.program_id(1) == 0)
        def _():
            o_ref[...] = jnp.zeros_like(o_ref)

        o_ref[...] += _mm_tn(a_ref[...], g_ref[...])
        if emit_bf16:
            @pl.when(pl.program_id(1) == last)
            def _():
                rest[0][...] = o_ref[...].astype(BF16)

    spec = pl.BlockSpec((M, tn), lambda j, l: (0, j))
    res = pl.pallas_call(
        body, name=name, grid=(N // tn, L // tl),
        in_specs=[pl.BlockSpec((tl, M), lambda j, l: (l, 0)), pl.BlockSpec((tl, tn), lambda j, l: (l, j))],
        out_specs=[spec, spec] if emit_bf16 else [spec],
        out_shape=[jax.ShapeDtypeStruct((M, N), F32)] + ([jax.ShapeDtypeStruct((M, N), BF16)] if emit_bf16 else []),
        compiler_params=_params(("arbitrary", "arbitrary")),
    )(a, g)
    return res if emit_bf16 else res[0]


def _position():
    return lax.axis_index("x"), lax.axis_index("y"), lax.axis_index("c")


def _index(px, py, pc):
    return 4 * px + 2 * py + pc


def _flip(pos, k):
    x, y, c = pos
    return ((1 - x) if (k >> 2) & 1 else x, (1 - y) if (k >> 1) & 1 else y, (1 - c) if k & 1 else c)


def _remote(src, dst, send_sem, recv_sem, peer):
    return pltpu.make_async_remote_copy(src_ref=src, dst_ref=dst, send_sem=send_sem, recv_sem=recv_sem,
                                        device_id=peer, device_id_type=MESH)


class _Flow:
    def __init__(self, kind, operand, result):
        self.kind, self.operand, self.result = kind, operand, result


class _Hosted:
    def __init__(self, operands, out_shapes, flows):
        self.operands, self.out_shapes, self.flows = operands, out_shapes, flows

    def plan(self, ins, outs, send_sems, recv_sems, local_sems):
        me = _position()
        mi = _index(*me)
        sends, recvs, locals_ = [], [], []
        for row, f in enumerate(self.flows):
            src, dst = ins[f.operand], outs[f.result]
            for k in range(1, N_DEV):
                peer = _flip(me, k)
                sems = (send_sems.at[row, k - 1], recv_sems.at[row, k - 1])
                if f.kind == "exchange":
                    sends.append(_remote(src.at[_index(*peer)], dst.at[k - 1], *sems, peer))
                    recvs.append(sends[-1])
                else:
                    sends.append(_remote(src, dst.at[mi], *sems, peer))
                    recvs.append(_remote(src, dst.at[_index(*peer)], *sems, peer))
            if f.kind == "gather":
                locals_.append(pltpu.make_async_copy(src, dst.at[mi], local_sems.at[row]))

        def start():
            for cp in locals_ + sends:
                cp.start()

        def wait():
            for cp in recvs:
                cp.wait_recv()
            for cp in sends:
                cp.wait_send()
            for cp in locals_:
                cp.wait()

        return start, wait


class _OwnerReduce:
    FIRST_STEP, SECOND_STEP, SEND_STEPS, REDUCE_STEPS = 2, 4, (2, 4, 6, 8), (5, 8, 10, 13)

    def __init__(self, stack, target_x):
        self.stack, self.target_x = stack, target_x
        block = stack.shape[1:]
        self.chunks = len(self.SEND_STEPS)
        self.chunk_rows = block[0] // self.chunks
        self.out_shapes = [jax.ShapeDtypeStruct(block, F32), jax.ShapeDtypeStruct((2,) + block, BF16)]
        self.out_specs = [pl.BlockSpec(block, lambda *_: (0, 0), pipeline_mode=pl.Buffered(1)),
                          pl.BlockSpec(memory_space=pl.ANY)]
        dma = pltpu.SemaphoreType.DMA
        self.scratch_shapes = ([pltpu.VMEM((2,) + block, F32)] * 2 + [pltpu.VMEM(block, BF16)] * 3
                               + [dma((self.chunks,))] * 4 + [dma((2,))] * 3)

    def plan(self, i, steps, stack_ref, own_ref, recv_ref, scratch):
        assert self.FIRST_STEP <= self.SEND_STEPS[0] and self.SECOND_STEP < self.REDUCE_STEPS[0] < steps - 1
        (theirs_scr, mine_scr, first_scr, across_scr, out_scr, y_send_sems, y_recv_sems, x_send_sems, x_recv_sems,
         swap_send_sems, swap_recv_sems, mine_sems) = scratch
        x, y, c = _position()
        owners_side = x == self.target_x
        other_side = x != self.target_x
        sibling, across, owner = (x, y, 1 - c), (x, 1 - y, c), (self.target_x, y, c)
        order = (1 - y, y)
        swaps = [_remote(stack_ref.at[2 * order[j] + (1 - c)], theirs_scr.at[j], swap_send_sems.at[j], swap_recv_sems.at[j],
                         sibling) for j in range(2)]
        mine = [pltpu.make_async_copy(stack_ref.at[2 * order[j] + c], mine_scr.at[j], mine_sems.at[j]) for j in range(2)]
        chunks = range(self.chunks)
        part = [pl.ds(j * self.chunk_rows, self.chunk_rows) for j in chunks]
        y_sems = lambda j: (y_send_sems.at[j], y_recv_sems.at[j])
        to_neighbour = [_remote(first_scr.at[part[j]], across_scr.at[part[j]], *y_sems(j), across) for j in chunks]
        to_owner_y = [_remote(first_scr.at[part[j]], recv_ref.at[0, part[j]], *y_sems(j), across) for j in chunks]
        to_owner_x = [_remote(out_scr.at[part[j]], recv_ref.at[1, part[j]], x_send_sems.at[j], x_recv_sems.at[j], owner)
                      for j in chunks]

        def before():
            @pl.when(i <= 1)
            def _():
                @pl.when(i == 0)
                def _():
                    for cp in [swaps[0]] + mine:
                        cp.start()

                pl.when(i == 1)(swaps[1].start)

        def after():
            @pl.when(i <= self.REDUCE_STEPS[-1])
            def _():
                @pl.when(i == self.FIRST_STEP)
                def _():
                    swaps[0].wait_recv()
                    mine[0].wait()
                    first_scr[...] = (mine_scr[0] + theirs_scr[0]).astype(first_scr.dtype)

                for j in chunks:
                    pl.when((i == self.SEND_STEPS[j]) & other_side)(to_neighbour[j].start)
                    pl.when((i == self.SEND_STEPS[j]) & owners_side)(to_owner_y[j].start)

                @pl.when(i == self.SECOND_STEP)
                def _():
                    swaps[1].wait_recv()
                    mine[1].wait()
                    t = mine_scr[1] + theirs_scr[1]
                    own_ref[...] = t
                    mine_scr[1] = t

                for j in chunks:
                    @pl.when((i == self.REDUCE_STEPS[j]) & other_side)
                    def _(j=j):
                        to_neighbour[j].wait_recv()
                        t = mine_scr[1, part[j], :] + across_scr[part[j], :].astype(F32)
                        out_scr[part[j], :] = t.astype(out_scr.dtype)
                        to_owner_x[j].start()

            @pl.when(i == steps - 1)
            def _():
                for cp in swaps:
                    cp.wait_send()
                for j in chunks:
                    @pl.when(other_side)
                    def _(j=j):
                        to_neighbour[j].wait_send()
                        to_owner_x[j].wait_send()

                    @pl.when(owners_side)
                    def _(j=j):
                        to_owner_y[j].wait_send()
                        to_owner_y[j].wait_recv()
                        to_owner_x[j].wait_recv()

        return before, after


def _call(body, comm, *, name, grid, in_specs, out_specs, out_shape, scratch_shapes, args, reduce=None):
    semantics = ("arbitrary",) * len(grid)
    if comm is None and reduce is None:
        return pl.pallas_call(body, name=name, grid=grid, in_specs=in_specs, out_specs=out_specs, out_shape=out_shape,
                              scratch_shapes=scratch_shapes, compiler_params=_params(semantics))(*args)
    n_in, n_out, n_scr = len(args), len(out_shape), len(scratch_shapes)
    c_operands, c_shapes, flows = (comm.operands, comm.out_shapes, comm.flows) if comm else ([], [], [])
    c_in, c_out, rows = len(c_operands), len(c_shapes), max(len(flows), 1)
    r_in = 0 if reduce is None else 1

    def hosted(*refs):
        ins, refs = refs[:n_in], refs[n_in:]
        cins, refs = refs[:c_in], refs[c_in:]
        rins, refs = refs[:r_in], refs[r_in:]
        outs, refs = refs[:n_out], refs[n_out:]
        couts, refs = refs[:c_out], refs[c_out:]
        routs, refs = refs[:2 * r_in], refs[2 * r_in:]
        scr, refs = refs[:n_scr], refs[n_scr:]
        (send_sems, recv_sems, local_sems), r_scr = refs[:3], refs[3:]
        ids = [pl.program_id(d) for d in range(len(grid))]
        first, last = ids[0] == 0, ids[0] == grid[0] - 1
        for d in range(1, len(grid)):
            first, last = first & (ids[d] == 0), last & (ids[d] == grid[d] - 1)
        before = after = lambda: None
        if reduce is not None:
            before, after = reduce.plan(ids[0], grid[0], rins[0], *routs, r_scr)
        if comm is not None:
            start, wait = comm.plan(cins, couts, send_sems, recv_sems, local_sems)
            pl.when(first)(start)
        before()
        body(*ins, *outs, *scr)
        after()
        if comm is not None:
            pl.when(last)(wait)

    any_spec = pl.BlockSpec(memory_space=pl.ANY)
    sems = [pltpu.SemaphoreType.DMA((rows, N_DEV - 1)), pltpu.SemaphoreType.DMA((rows, N_DEV - 1)),
            pltpu.SemaphoreType.DMA((rows,))]
    r_operands, r_specs, r_shapes, r_scratch = ([reduce.stack], reduce.out_specs, reduce.out_shapes,
                                                reduce.scratch_shapes) if reduce else ([], [], [], [])
    return pl.pallas_call(
        hosted, name=name, grid=grid, in_specs=list(in_specs) + [any_spec] * (c_in + r_in),
        out_specs=list(out_specs) + [any_spec] * c_out + r_specs, out_shape=list(out_shape) + list(c_shapes) + r_shapes,
        scratch_shapes=list(scratch_shapes) + sems + r_scratch,
        compiler_params=_params(semantics))(*args, *c_operands, *r_operands)


def _head_row(ref, width, rep):
    hid = lax.broadcasted_iota(jnp.int32, (1, width), 1) // rep
    row = jnp.zeros((1, width), F32)
    for h in range(N_HEADS):
        row = jnp.where(hid == h, ref[h], row)
    return row


def _rows_from_above(u_b, s, ext_scr, row, col):
    down = (row - col == s).astype(_MXU)
    return jnp.concatenate([ext_scr[8 - s:16 - s, :], jnp.dot(down, u_b, preferred_element_type=F32)[8:128]], axis=0)


def _ssd_recompute(first, p_ref, halo_ref, cw_ref, cb_ref, dtb_ref, alog_ref, e_ref, ext_scr, pre=None):
    row = lax.broadcasted_iota(jnp.int32, (128, 128), 0)
    col = lax.broadcasted_iota(jnp.int32, (128, 128), 1)
    ext_scr[0:8, :] = jnp.where(first, 0.0, halo_ref[:, S_XS:S_DT])
    if pre is not None:
        ext_scr[8:16, :] = p_ref[0:8, S_XS:S_DT]
    else:
        ext_scr[8:136, :] = p_ref[:, S_XS:S_DT]
        cw = cw_ref[...]
        pre = (cb_ref[0:1, :] + cw[3:4, :] * ext_scr[8:136, :] + cw[2:3, :] * ext_scr[7:135, :]
               + cw[1:2, :] * ext_scr[6:134, :] + cw[0:1, :] * ext_scr[5:133, :])
    sg = _sigmoid(pre)
    act = pre * sg
    lane = lax.broadcasted_iota(jnp.int32, (1, 128), 1)
    A = jnp.where(lane < N_HEADS, -jnp.exp(_head_row(alog_ref, 128, 1)), 0.0)
    raw = p_ref[:, S_DT:S_DT + 128] + _head_row(dtb_ref, 128, 1)
    dt = _softplus(raw)
    dA = dt * A
    tril = (row >= col).astype(BF16)
    acs = _mm_exact_l(tril, dA)
    last = acs[127:128, :]
    ds = jnp.exp(last - acs)
    eo = jnp.exp(acs)
    E = e_ref[...]
    ex = _mm_2pass_r(jnp.concatenate([dt, ds, eo], axis=0), E)
    dt_e, ds_e, eo_e = ex[0:128], ex[128:256], ex[256:384]
    xs_c = act[:, 0:1024]
    X = xs_c * dt_e
    return dict(pre=pre, sg=sg, xs_c=xs_c, Bc=act[:, 1024:1280], Cc=act[:, 1280:1536], A=A, raw=raw, dt=dt,
                acs=acs, acsT=acs.T, eo_e=eo_e, ds_e=ds_e, dt_e=dt_e, cd_e=eo_e[127:128, :],
                X=X, Xd=X * ds_e, row=row, col=col)


def _split_halves(t):
    lo = _lo_half(CHUNK)
    return jnp.concatenate([jnp.where(lo, t, 0.0), jnp.where(lo, 0.0, t)], axis=0)


def _ssd_core(R, hprev):
    causal = R["row"] >= R["col"]
    acs, acsT, X = R["acs"], R["acsT"], R["X"]
    ydiag, yoff, snew = [], [], []
    for g in range(SSD_GROUPS):
        Bg = R["Bc"][:, g * 128:(g + 1) * 128]
        Cg = R["Cc"][:, g * 128:(g + 1) * 128]
        cols = slice(g * 512, (g + 1) * 512)
        CB = _mm_nt(Cg, Bg)
        snew.append(_mm_tn(Bg, R["Xd"][:, cols]))
        yoff.append(_mm(Cg, hprev[:, cols]))
        for j in range(4):
            h0 = g * 8 + 2 * j
            ms = [CB * jnp.exp(jnp.where(causal, acs[:, h:h + 1] - acsT[h:h + 1, :], NEG)) for h in (h0, h0 + 1)]
            ydiag.append(_mm(jnp.concatenate(ms, axis=1), _split_halves(X[:, h0 * HEAD_DIM:h0 * HEAD_DIM + 128])))
    Y = jnp.concatenate(ydiag, axis=1) + jnp.concatenate(yoff, axis=1) * R["eo_e"]
    return Y, jnp.concatenate(snew, axis=1)


def _ssd_forward_step(p_ref, halo_ref, cw_ref, cb_ref, dtb_ref, alog_ref, dsk_ref, nw_ref, e_ref,
                      y_ref, ypre_ref, hprev_ref, pre_ref, h_scr, ext_scr):
    c = pl.program_id(0)
    first = c == 0

    @pl.when(first)
    def _():
        h_scr[...] = jnp.zeros_like(h_scr)

    R = _ssd_recompute(first, p_ref, halo_ref, cw_ref, cb_ref, dtb_ref, alog_ref, e_ref, ext_scr)
    hprev = h_scr[...]
    hprev_ref[...] = hprev
    pre_ref[...] = R["pre"]
    Y, snew = _ssd_core(R, hprev)
    h_scr[...] = hprev * R["cd_e"] + snew
    Y = Y + _head_row(dsk_ref, D_SSD, HEAD_DIM) * R["xs_c"]
    ypre_ref[...] = Y
    z = p_ref[:, S_Z:S_Z + 1024]
    yf = Y * (z * _sigmoid(z))
    outs = []
    for g in range(SSD_GROUPS):
        yg = yf[:, g * 512:(g + 1) * 512]
        r = lax.rsqrt(jnp.mean(yg * yg, axis=-1, keepdims=True) + RMS_EPS)
        outs.append(yg * r)
    y_ref[:, 0:D_SSD] = (jnp.concatenate(outs, axis=1) * nw_ref[0:1, :]).astype(y_ref.dtype)


def _ssd_backward(proj_ssd, hprev_all, ypre, pre, dy, conv_w, conv_b, dt_bias, a_log, d_skip, norm_w, E, ET, comm=None):
    L = proj_ssd.shape[0]
    nc = L // CHUNK

    def body(p_ref, halo_ref, hprev_ref, ypre_ref, pre_ref, dy_ref, cw_ref, cb_ref, dtb_ref, alog_ref, dsk_ref, nw_ref, e_ref,
             et_ref, dp_ref, acc_cw_ref, acc_w_ref, acc_s_ref, dh_scr, ext_scr, ext2_scr, nxt_scr):
        i = pl.program_id(0)
        c = nc - 1 - i
        first = c == 0

        @pl.when(i == 0)
        def _():
            dh_scr[...] = jnp.zeros_like(dh_scr)
            nxt_scr[...] = jnp.zeros_like(nxt_scr)
            acc_cw_ref[...] = jnp.zeros_like(acc_cw_ref)
            acc_w_ref[...] = jnp.zeros_like(acc_w_ref)
            acc_s_ref[...] = jnp.zeros_like(acc_s_ref)

        R = _ssd_recompute(first, p_ref, halo_ref, cw_ref, cb_ref, dtb_ref, alog_ref, e_ref, ext_scr, pre_ref[...])
        hprev = hprev_ref[...]
        xs_c, X, Xd = R["xs_c"], R["X"], R["Xd"]
        acs, acsT = R["acs"], R["acsT"]
        ET = et_ref[...]
        dsk = _head_row(dsk_ref, D_SSD, HEAD_DIM)
        Y = ypre_ref[...]

        z = p_ref[:, S_Z:S_Z + 1024]
        sz = _sigmoid(z)
        silz = z * sz
        yf = Y * silz
        dyv = dy_ref[...]
        nw = nw_ref[0:1, :]
        dyf_parts, dnw_parts = [], []
        for g in range(SSD_GROUPS):
            cols = slice(g * 512, (g + 1) * 512)
            yg = yf[:, cols]
            r = lax.rsqrt(jnp.mean(yg * yg, axis=-1, keepdims=True) + RMS_EPS)
            yn = yg * r
            dyn = dyv[:, cols] * nw[:, cols]
            dnw_parts.append(_colsum(dyv[:, cols] * yn))
            dyf_parts.append(r * (dyn - yn * jnp.mean(dyn * yn, axis=-1, keepdims=True)))
        dyf = jnp.concatenate(dyf_parts, axis=1)
        dY = dyf * silz
        dz = dyf * Y * (sz * (1.0 + z * (1.0 - sz)))

        dhn = dh_scr[...]
        dYo = dY * R["eo_e"]
        causal = R["row"] >= R["col"]
        dacs = jnp.zeros((128, 128), F32)
        dacs_t = jnp.zeros((128, 128), F32)
        dxdiag, dxd, dhprev, dBs, dCs, yoff = [], [], [], [], [], []
        for g in range(SSD_GROUPS):
            Bg = R["Bc"][:, g * 128:(g + 1) * 128]
            Cg = R["Cc"][:, g * 128:(g + 1) * 128]
            cols = slice(g * 512, (g + 1) * 512)
            CB = _mm_nt(Cg, Bg)
            dCB = jnp.zeros((128, 128), F32)
            for j in range(4):
                h0 = g * 8 + 2 * j
                pc = slice(h0 * HEAD_DIM, h0 * HEAD_DIM + 128)
                dYst = _split_halves(dY[:, pc])
                dMst = _mm_nt(dYst, X[:, pc])
                mts = []
                for a, h in enumerate((h0, h0 + 1)):
                    acol = acs[:, h:h + 1]
                    arow = acsT[h:h + 1, :]
                    Lm = jnp.exp(jnp.where(causal, acol - arow, NEG))
                    M = CB * Lm
                    dM = dMst[a * 128:(a + 1) * 128]
                    dCB = dCB + dM * Lm
                    G = dM * M
                    dacs = dacs + jnp.where(R["col"] == h, jnp.sum(G, axis=1, keepdims=True), 0.0)
                    dacs_t = dacs_t + jnp.where(R["row"] == h, jnp.sum(G, axis=0, keepdims=True), 0.0)
                    mts.append(M.T)
                dxdiag.append(_mm(jnp.concatenate(mts, axis=1), dYst))
            dS = dhn[:, cols]
            dxd.append(_mm(Bg, dS))
            yoff.append(_mm(Cg, hprev[:, cols]))
            dhprev.append(_mm_tn(Cg, dYo[:, cols]))
            dCs.append(_mm_nt(dYo[:, cols], hprev[:, cols]) + _mm(dCB, Bg))
            dBs.append(_mm_tn(dCB, Cg) + _mm_nt(Xd[:, cols], dS))
        Yoff = jnp.concatenate(yoff, axis=1) * R["eo_e"]
        dXd = jnp.concatenate(dxd, axis=1)
        dX = jnp.concatenate(dxdiag, axis=1) + dXd * R["ds_e"]
        t_state = dXd * Xd
        dacs = dacs + _mm_2pass_r(dY * Yoff - t_state, ET) - dacs_t.T
        v_last = _colsum(t_state + dhn * hprev * R["cd_e"])
        dlast = _mm_exact_r(jnp.broadcast_to(v_last, (8, 1024)), ET)[0:1, :]
        dacs = dacs + jnp.where(R["row"] == 127, dlast, 0.0)
        triu = (R["col"] >= R["row"]).astype(BF16)
        da = _mm_exact_l(triu, dacs)
        ddt = da * R["A"] + _mm(dX * xs_c, ET)
        ddt_raw = ddt * _sigmoid(R["raw"])
        dxs_c = dX * R["dt_e"] + dY * dsk
        dh_scr[...] = jnp.concatenate(dhprev, axis=1) + dhn * R["cd_e"]

        dact = jnp.concatenate([dxs_c] + dBs + dCs, axis=1)
        pre, sg = R["pre"], R["sg"]
        dpre = dact * (sg * (1.0 + pre * (1.0 - sg)))
        ext2_scr[0:8, :] = dpre[120:128, :]
        ext2_scr[8:16, :] = nxt_scr[...]
        nxt_scr[...] = dpre[0:8, :]
        cw = cw_ref[...]
        u_b, dpre_b = p_ref[:, S_XS:S_DT].astype(_MXU), dpre.astype(_MXU)
        dxbc = cw[3:4, :] * dpre
        taps = [_colsum(dpre * p_ref[:, S_XS:S_DT])]
        for s in (1, 2, 3):
            up = (R["col"] - R["row"] == s).astype(_MXU)
            d_s = jnp.concatenate([jnp.dot(up, dpre_b, preferred_element_type=F32)[0:120],
                                   ext2_scr[s:8 + s, :]], axis=0)
            dxbc = dxbc + cw[3 - s:4 - s, :] * d_s
            taps.append(_colsum(dpre * _rows_from_above(u_b, s, ext_scr, R["row"], R["col"])))
        acc_cw_ref[...] += _rows8(taps[::-1] + [_colsum(dpre)])
        acc_w_ref[...] += _rows8([jnp.concatenate(dnw_parts, axis=1), _colsum(dY * xs_c)])
        acc_s_ref[...] += _rows8([_colsum(ddt_raw), _colsum(da * R["dt"])])

        lane = lax.broadcasted_iota(jnp.int32, (128, 128), 1)
        dp_ref[:, S_Z:S_Z + 1024] = dz.astype(dp_ref.dtype)
        dp_ref[:, S_XS:S_DT] = dxbc.astype(dp_ref.dtype)
        dp_ref[:, S_DT:S_DT + 128] = jnp.where(lane < N_HEADS, ddt_raw, 0.0).astype(dp_ref.dtype)
        dp_ref[:, S_DT + 128:S_W] = jnp.zeros((128, 128), dp_ref.dtype)

        @pl.when(i == nc - 1)
        def _():
            acc = acc_s_ref[...]
            dskip = _mm_exact_r(acc_w_ref[...], ET)[1:2, :]
            acc_s_ref[...] = _rows8([acc[0:1, :], acc[1:2, :] * R["A"], dskip])

    const = lambda shape: pl.BlockSpec(shape, lambda i: (0, 0))
    smem = pl.BlockSpec(memory_space=pltpu.SMEM)
    rev = lambda i: (nc - 1 - i, 0)
    return _call(
        body, comm, name="ssd_bwd", grid=(nc,),
        in_specs=[pl.BlockSpec((CHUNK, S_W), rev),
                  pl.BlockSpec((8, S_W), lambda i: (jnp.maximum((nc - 1 - i) * 16 - 1, 0), 0)),
                  pl.BlockSpec((128, 1024), rev),
                  pl.BlockSpec((CHUNK, D_SSD), rev),
                  pl.BlockSpec((CHUNK, D_XBC), rev),
                  pl.BlockSpec((CHUNK, D_SSD), rev),
                  const((4, D_XBC)), const((1, D_XBC)), smem, smem, smem, const((1, 1024)),
                  const((128, 1024)), const((1024, 128))],
        out_specs=[pl.BlockSpec((CHUNK, S_W), rev), const((8, D_XBC)), const((8, 1024)), const((8, 128))],
        out_shape=[jax.ShapeDtypeStruct((L, S_W), _MXU), jax.ShapeDtypeStruct((8, D_XBC), F32),
                   jax.ShapeDtypeStruct((8, 1024), F32), jax.ShapeDtypeStruct((8, 128), F32)],
        scratch_shapes=[pltpu.VMEM((128, 1024), F32), pltpu.VMEM((16, D_XBC), F32),
                        pltpu.VMEM((16, D_XBC), F32), pltpu.VMEM((8, D_XBC), F32)],
        args=(proj_ssd, proj_ssd, hprev_all, ypre, pre, dy, conv_w, conv_b, dt_bias, a_log, d_skip, norm_w, E, ET))


def _rope(t, tab):
    cos, sa, sb = tab[:, 0:128], tab[:, 128:256], tab[:, 256:384]
    outs = []
    for i in range(t.shape[1] // 128):
        tg = t[:, i * 128:(i + 1) * 128]
        outs.append(tg * cos + pltpu.roll(tg, 8, 1) * sa + pltpu.roll(tg, 120, 1) * sb)
    return jnp.concatenate(outs, axis=1)


def _rope_transposed(d, tab):
    cos, sa, sb = tab[:, 0:128], tab[:, 128:256], tab[:, 256:384]
    outs = []
    for i in range(d.shape[1] // 128):
        dg = d[:, i * 128:(i + 1) * 128]
        outs.append(dg * cos + pltpu.roll(dg * sa, 120, 1) + pltpu.roll(dg * sb, 8, 1))
    return jnp.concatenate(outs, axis=1)


def _lo_half(rows):
    return lax.broadcasted_iota(jnp.int32, (rows, 128), 1) < HEAD_DIM


def _native_half(rows, j):
    lo = _lo_half(rows)
    return lo if j % 2 == 0 else jnp.logical_not(lo)


def _kv_native(t, j):
    p = j // 2
    return jnp.where(_native_half(t.shape[0], j), t[:, p * 128:(p + 1) * 128], 0.0)


def _stack_heads(t, j):
    out = []
    for m in (2 * j, 2 * j + 1):
        pair = t[:, m * 128:(m + 1) * 128]
        swapped = pltpu.roll(pair, HEAD_DIM, 1)
        out += [pair, swapped] if j % 2 == 0 else [swapped, pair]
    return jnp.concatenate(out, axis=0)


def _unstack_heads(s, j):
    out = []
    for m in range(2):
        first, second = s[256 * m:256 * m + 128], s[256 * m + 128:256 * m + 256]
        if j % 2 == 0:
            out.append(first + pltpu.roll(second, HEAD_DIM, 1))
        else:
            out.append(pltpu.roll(first, HEAD_DIM, 1) + second)
    return jnp.concatenate(out, axis=1)


def _keep_native(r, j):
    return jnp.where(_native_half(r.shape[0], j), r, 0.0)


def _sink_row(sink_ref, j):
    hid = lax.broadcasted_iota(jnp.int32, (1, 4 * CHUNK), 1) // CHUNK
    row = jnp.zeros((1, 4 * CHUNK), F32)
    for hh in range(4):
        row = jnp.where(hid == hh, sink_ref[4 * j + hh], row)
    return row


def _from_current():
    si = lax.broadcasted_iota(jnp.int32, (CHUNK, 4 * CHUNK), 0)
    qi = lax.broadcasted_iota(jnp.int32, (CHUNK, 4 * CHUNK), 1) % CHUNK
    return si <= qi


def _fold(full, from_cur, pen=0.0):
    return jnp.where(from_cur, full[CHUNK:2 * CHUNK], full[0:CHUNK] + pen)


def _unfold(t, from_cur):
    c = jnp.where(from_cur, t, 0.0)
    return jnp.concatenate([t - c, c], axis=0)


def _softmax_sink(s, sink):
    mx = jnp.maximum(jnp.max(s, axis=0, keepdims=True), sink)
    p = jnp.exp(s - mx)
    esink = jnp.exp(sink - mx)
    inv = 1.0 / (jnp.sum(p, axis=0, keepdims=True) + esink)
    return p * inv, esink * inv


def _swa_inputs(blk, p_ref, prev_ref, tab_ref, ptab_ref):
    tab = tab_ref[...]
    qr = _rope(p_ref[:, A_Q:A_Q + 1024], tab) * ATT_SCALE
    kk = jnp.concatenate([_rope(prev_ref[:, 0:256], ptab_ref[...]), _rope(p_ref[:, A_K:A_K + 256], tab)], axis=0)
    vv = jnp.concatenate([prev_ref[:, 256:512], p_ref[:, A_V:A_V + 256]], axis=0)
    return tab, qr, kk, vv, jnp.where(blk > 0, 0.0, NEG)


def _swa_forward_step(sink_ref, p_ref, prev_ref, tab_ref, ptab_ref, y_ref):
    n = pl.program_id(0)
    _, qr, kk, vv, pen = _swa_inputs(n, p_ref, prev_ref, tab_ref, ptab_ref)
    from_cur = _from_current()
    outs = []
    for j in range(KV_HEADS):
        s = _fold(_mm_nt(_kv_native(kk, j), _stack_heads(qr, j)), from_cur, pen)
        P, _ = _softmax_sink(s, _sink_row(sink_ref, j))
        outs.append(_unstack_heads(_mm_tn(_unfold(P, from_cur), _kv_native(vv, j)), j))
    g = p_ref[:, A_G:A_G + 1024]
    y_ref[:, D_SSD:D_SSD + D_ATT] = (jnp.concatenate(outs, axis=1) * (g * _sigmoid(g))).astype(y_ref.dtype)


def _mixer_forward(proj_ssd, proj_att, tabs, sinks, conv_w, conv_b, dt_bias, a_log, d_skip, norm_w, E, comm=None):
    L = proj_ssd.shape[0]
    nc = L // CHUNK

    def body(p_ref, halo_ref, cw_ref, cb_ref, dtb_ref, alog_ref, dsk_ref, nw_ref, e_ref,
             sink_ref, pa_ref, prev_ref, tab_ref, ptab_ref, y_ref, ypre_ref, hprev_ref, pre_ref, h_scr, ext_scr):
        _ssd_forward_step(p_ref, halo_ref, cw_ref, cb_ref, dtb_ref, alog_ref, dsk_ref, nw_ref, e_ref,
                          y_ref, ypre_ref, hprev_ref, pre_ref, h_scr, ext_scr)
        _swa_forward_step(sink_ref, pa_ref, prev_ref, tab_ref, ptab_ref, y_ref)

    const = lambda shape: pl.BlockSpec(shape, lambda c: (0, 0))
    smem = pl.BlockSpec(memory_space=pltpu.SMEM)
    rows = lambda w: pl.BlockSpec((CHUNK, w), lambda c: (c, 0))
    return _call(
        body, comm, name="mixer_fwd", grid=(nc,),
        in_specs=[rows(S_W), pl.BlockSpec((8, S_W), lambda c: (jnp.maximum(c * 16 - 1, 0), 0)),
                  const((4, D_XBC)), const((1, D_XBC)), smem, smem, smem, const((1, 1024)), const((128, 1024)),
                  smem, rows(A_W), pl.BlockSpec((CHUNK, 512), lambda c: (jnp.maximum(c - 1, 0), 2)),
                  rows(384), pl.BlockSpec((CHUNK, 384), lambda c: (jnp.maximum(c - 1, 0), 0))],
        out_specs=[rows(D_SSD + D_ATT), rows(D_SSD), pl.BlockSpec((128, 1024), lambda c: (c, 0)), rows(D_XBC)],
        out_shape=[jax.ShapeDtypeStruct((L, D_SSD + D_ATT), _MXU), jax.ShapeDtypeStruct((L, D_SSD), F32),
                   jax.ShapeDtypeStruct((nc * 128, 1024), F32), jax.ShapeDtypeStruct((L, D_XBC), F32)],
        scratch_shapes=[pltpu.VMEM((128, 1024), F32), pltpu.VMEM((136, D_XBC), F32)],
        args=(proj_ssd, proj_ssd, conv_w, conv_b, dt_bias, a_log, d_skip, norm_w, E,
              sinks, proj_att, proj_att, tabs, tabs))


def _swa_backward(proj_att, tabs, sinks, dy, reduce=None):
    L = proj_att.shape[0]
    nb = L // CHUNK

    def body(sink_ref, p_ref, prev_ref, tab_ref, ptab_ref, dy_ref, dp_ref, dsink_ref, carry_k, carry_v):
        i = pl.program_id(0)
        n = nb - 1 - i

        @pl.when(i == 0)
        def _():
            carry_k[...] = jnp.zeros_like(carry_k)
            carry_v[...] = jnp.zeros_like(carry_v)
            dsink_ref[...] = jnp.zeros_like(dsink_ref)

        tab, qr, kk, vv, pen = _swa_inputs(n, p_ref, prev_ref, tab_ref, ptab_ref)
        from_cur = _from_current()
        g = p_ref[:, A_G:A_G + 1024]
        sgm = _sigmoid(g)
        dyv = dy_ref[...]
        do_all = dyv * (g * sgm)
        lane8 = lax.broadcasted_iota(jnp.int32, (8, 128), 1)
        hid = lax.broadcasted_iota(jnp.int32, (1, 4 * CHUNK), 1) // CHUNK
        o_parts, dq_parts = [], []
        dk_nat = [jnp.zeros((2 * CHUNK, 128), F32) for _ in range(2)]
        dv_nat = [jnp.zeros((2 * CHUNK, 128), F32) for _ in range(2)]
        dsink = jnp.zeros((8, 128), F32)
        for j in range(KV_HEADS):
            qs = _stack_heads(qr, j)
            kkb, vvb = _kv_native(kk, j), _kv_native(vv, j)
            P, psink = _softmax_sink(_fold(_mm_nt(kkb, qs), from_cur, pen), _sink_row(sink_ref, j))
            p_full = _unfold(P, from_cur)
            o_parts.append(_unstack_heads(_mm_tn(p_full, vvb), j))
            do_s = _stack_heads(do_all, j)
            dP = _fold(_mm_nt(vvb, do_s), from_cur)
            D = jnp.sum(P * dP, axis=0, keepdims=True)
            ds_full = _unfold(P * (dP - D), from_cur)
            sd = psink * D
            for hh in range(4):
                dsink = dsink + jnp.where(lane8 == 4 * j + hh, -jnp.sum(jnp.where(hid == hh, sd, 0.0)), 0.0)
            dq_parts.append(_unstack_heads(_mm_tn(ds_full, kkb), j) * ATT_SCALE)
            dk_nat[j // 2] = dk_nat[j // 2] + _keep_native(_mm(ds_full, qs), j)
            dv_nat[j // 2] = dv_nat[j // 2] + _keep_native(_mm(p_full, do_s), j)
        o = jnp.concatenate(o_parts, axis=1)
        dkk = jnp.concatenate(dk_nat, axis=1)
        dvv = jnp.concatenate(dv_nat, axis=1)
        out = dp_ref.dtype
        dp_ref[:, A_Q:A_Q + 1024] = _rope_transposed(jnp.concatenate(dq_parts, axis=1), tab).astype(out)
        dp_ref[:, A_K:A_K + 256] = _rope_transposed(dkk[CHUNK:2 * CHUNK] + carry_k[...], tab).astype(out)
        dp_ref[:, A_V:A_V + 256] = (dvv[CHUNK:2 * CHUNK] + carry_v[...]).astype(out)
        dp_ref[:, A_G:A_G + 1024] = (dyv * o * (sgm * (1.0 + g * (1.0 - sgm)))).astype(out)
        carry_k[...] = dkk[0:CHUNK]
        carry_v[...] = dvv[0:CHUNK]
        dsink_ref[...] += dsink

    rev = lambda i: (nb - 1 - i, 0)
    prev = lambda i: jnp.maximum(nb - 2 - i, 0)
    return _call(
        body, None, name="swa_bwd", grid=(nb,),
        in_specs=[pl.BlockSpec(memory_space=pltpu.SMEM),
                  pl.BlockSpec((CHUNK, A_W), rev),
                  pl.BlockSpec((CHUNK, 512), lambda i: (prev(i), 2)),
                  pl.BlockSpec((CHUNK, 384), rev),
                  pl.BlockSpec((CHUNK, 384), lambda i: (prev(i), 0)),
                  pl.BlockSpec((CHUNK, D_ATT), lambda i: (nb - 1 - i, 1))],
        out_specs=[pl.BlockSpec((CHUNK, A_W), rev), pl.BlockSpec((8, 128), lambda i: (0, 0))],
        out_shape=[jax.ShapeDtypeStruct((L, A_W), _MXU), jax.ShapeDtypeStruct((8, 128), F32)],
        scratch_shapes=[pltpu.VMEM((CHUNK, 256), F32), pltpu.VMEM((CHUNK, 256), F32)],
        args=(sinks, proj_att, proj_att, tabs, tabs, dy), reduce=reduce)


def _head(y, x, target, w_out, ln_g, ln_b, *, tm):
    L = x.shape[0]
    nsteps = L // tm

    def body(y_ref, x_ref, t_ref, wo_ref, g_ref, b_ref, dr_ref, dy_ref, acc_ref):
        i = pl.program_id(0)

        @pl.when(i == 0)
        def _():
            acc_ref[...] = jnp.zeros_like(acc_ref)

        r = ALPHA * x_ref[...] + _mm(y_ref[...], wo_ref[...])
        mu = jnp.mean(r, axis=-1, keepdims=True)
        d = r - mu
        rstd = lax.rsqrt(jnp.mean(d * d, axis=-1, keepdims=True) + LN_EPS)
        xh = d * rstd
        gam = g_ref[0:1, :]
        e = xh * gam + b_ref[0:1, :] - t_ref[...]
        dout = e * (1.0 / D_MODEL)
        dxh = dout * gam
        dr = rstd * (dxh - jnp.mean(dxh, axis=-1, keepdims=True)
                     - xh * jnp.mean(dxh * xh, axis=-1, keepdims=True))
        dr_ref[...] = dr
        dy_ref[...] = _mm_nt(dr, wo_ref[...])
        acc_ref[...] += _rows8([_colsum(dout * xh), _colsum(dout), _colsum(e * e) * (0.5 / D_MODEL)])

        @pl.when(i == nsteps - 1)
        def _():
            acc = acc_ref[...]
            tot = jnp.sum(acc[2:3, :])
            rid = lax.broadcasted_iota(jnp.int32, (8, 1024), 0)
            acc_ref[...] = jnp.where(rid == 3, tot, acc)

    const = lambda shape: pl.BlockSpec(shape, lambda i: (0, 0))
    row = lambda w: pl.BlockSpec((tm, w), lambda i: (i, 0))
    return pl.pallas_call(
        body, name="head", grid=(nsteps,),
        in_specs=[row(2048), row(1024), row(1024), const((2048, 1024)), const((1, 1024)), const((1, 1024))],
        out_specs=[row(1024), row(2048), const((8, 1024))],
        out_shape=[jax.ShapeDtypeStruct((L, D_MODEL), F32), jax.ShapeDtypeStruct((L, 2048), F32),
                   jax.ShapeDtypeStruct((8, 1024), F32)],
        compiler_params=_params(("arbitrary",)),
    )(y, x, target, w_out, ln_g, ln_b)


def _gather_w_in(w_shard, positions):
    R = w_shard.shape[0]
    halves = (pl.ds(0, R // 2), pl.ds(R // 2, R // 2))
    any_spec = pl.BlockSpec(memory_space=pl.ANY)
    vmem = pl.BlockSpec(memory_space=pltpu.VMEM)

    def body(in_ref, pos_ref, inv_ref, out_ref, tab_ref, tab_scr, send_sems, recv_sems, local_sem, tab_sem):
        x, y, c = _position()

        def slot(p, half=None):
            s = out_ref.at[_index(*p)]
            return s if half is None else s.at[halves[half]]

        def same_core(p):
            return (p[0], p[1], c)

        def other_core(p):
            return (p[0], p[1], 1 - c)

        me, xn, yn, dg = (x, y), (1 - x, y), (x, 1 - y), (1 - x, 1 - y)

        def copy(k, dst, to, src=None):
            return _remote(dst if src is None else src, dst, send_sems.at[k], recv_sems.at[k], to)

        local = pltpu.make_async_copy(in_ref, slot(same_core(me)), local_sem)
        local.start()
        own = [copy(0, slot(same_core(me)), other_core(me), in_ref), copy(1, slot(same_core(me)), same_core(xn), in_ref),
               copy(2, slot(same_core(me)), same_core(yn), in_ref)]
        for cp in own:
            cp.start()
        _rope_tables(pos_ref, inv_ref, tab_scr)
        tab_out = pltpu.make_async_copy(tab_scr, tab_ref, tab_sem)
        tab_out.start()
        copy(1, slot(same_core(xn)), same_core(xn)).wait_recv()
        passed = [copy(4, slot(same_core(xn), 1), same_core(yn)), copy(5, slot(same_core(xn)), other_core(me))]
        for cp in passed:
            cp.start()
        copy(2, slot(same_core(yn)), same_core(yn)).wait_recv()
        more = [copy(3, slot(same_core(yn), 0), same_core(xn)), copy(6, slot(same_core(yn)), other_core(me))]
        for cp in more:
            cp.start()
        passed += more
        for k, half in ((3, 0), (4, 1)):
            copy(k, slot(same_core(dg), half), same_core(xn)).wait_recv()
            fwd = copy(7 + half, slot(same_core(dg), half), other_core(me))
            fwd.start()
            passed.append(fwd)
        copy(0, slot(other_core(me)), other_core(me)).wait_recv()
        copy(5, slot(other_core(xn)), other_core(me)).wait_recv()
        copy(6, slot(other_core(yn)), other_core(me)).wait_recv()
        for half in (0, 1):
            copy(7 + half, slot(other_core(dg), half), other_core(me)).wait_recv()
        for cp in own + passed:
            cp.wait_send()
        local.wait()
        tab_out.wait()

    return pl.pallas_call(
        body, name="gather_w_in", in_specs=[any_spec, vmem, vmem], out_specs=[any_spec, any_spec],
        out_shape=[jax.ShapeDtypeStruct((N_DEV,) + w_shard.shape, w_shard.dtype),
                   jax.ShapeDtypeStruct((positions.size, 384), F32)],
        scratch_shapes=[pltpu.VMEM((positions.size, 384), F32), pltpu.SemaphoreType.DMA((9,)),
                        pltpu.SemaphoreType.DMA((9,)), pltpu.SemaphoreType.DMA, pltpu.SemaphoreType.DMA],
        compiler_params=_params(),
    )(w_shard, positions, jnp.asarray(ROPE_INV)[None, :])


def _input_gradient(d_ssd, d_att, w_ssd, w_att, dr, *, tm, comm=None, reduce=None):
    L = dr.shape[0]

    def body(ds_ref, da_ref, ws_ref, wa_ref, dr_ref, o_ref):
        o_ref[...] = ALPHA * dr_ref[...] + _mm_nt(ds_ref[...], ws_ref[...]) + _mm_nt(da_ref[...], wa_ref[...])

    row = lambda w: pl.BlockSpec((tm, w), lambda i: (i, 0))
    resident = lambda a: pl.BlockSpec(a.shape, lambda i: (0, 0), pipeline_mode=pl.Buffered(1))
    return _call(body, comm, name="dx", grid=(L // tm,),
                 in_specs=[row(S_W), row(A_W), resident(w_ssd), resident(w_att), row(D_MODEL)],
                 out_specs=[row(D_MODEL)], out_shape=[jax.ShapeDtypeStruct((L, D_MODEL), F32)],
                 scratch_shapes=[], args=(d_ssd, d_att, w_ssd, w_att, dr), reduce=reduce)


SHARD_COLS = D_IN_PROJ // N_DEV
SPLIT = N_SSD_REAL - 4 * SHARD_COLS
RELAYOUT_ROWS = 256


def _unpack_w_in(w_all):
    def body(g_ref, ws_ref, wa_ref):
        for j in range(4):
            ws_ref[:, SHARD_COLS * j:SHARD_COLS * (j + 1)] = g_ref[j]
        ws_ref[:, 4 * SHARD_COLS:N_SSD_REAL] = g_ref[4, :, 0:SPLIT]
        ws_ref[:, N_SSD_REAL:S_W] = jnp.zeros((RELAYOUT_ROWS, S_W - N_SSD_REAL), ws_ref.dtype)
        wa_ref[:, 0:SHARD_COLS - SPLIT] = g_ref[4, :, SPLIT:SHARD_COLS]
        for j in range(5, N_DEV):
            lo = SHARD_COLS * (j - 4) - SPLIT
            wa_ref[:, lo:lo + SHARD_COLS] = g_ref[j]

    return pl.pallas_call(
        body, name="unpack_w_in", grid=(D_MODEL // RELAYOUT_ROWS,),
        in_specs=[pl.BlockSpec((N_DEV, RELAYOUT_ROWS, SHARD_COLS), lambda i: (0, i, 0))],
        out_specs=[pl.BlockSpec((RELAYOUT_ROWS, S_W), lambda i: (i, 0)), pl.BlockSpec((RELAYOUT_ROWS, A_W), lambda i: (i, 0))],
        out_shape=[jax.ShapeDtypeStruct((D_MODEL, S_W), w_all.dtype), jax.ShapeDtypeStruct((D_MODEL, A_W), w_all.dtype)],
        compiler_params=_params(("arbitrary",)),
    )(w_all)


def _dw_in(x, d, half, tail=None, *, tl=1024):
    L, N = d.shape
    steps = L // tl

    def body(x_ref, d_ref, *refs):
        if half == 0:
            p_ref, tail_ref, acc, p_scr, p_sems = refs
        else:
            t_ref, p_ref, acc, p_scr, p_sems = refs
        l = pl.program_id(0)

        @pl.when(l == 0)
        def _():
            acc[...] = jnp.zeros_like(acc)

        acc[...] += _mm_tn(x_ref[...].astype(_MXU), d_ref[...])

        @pl.when(l == steps - 1)
        def _():
            if half == 0:
                tail_ref[...] = acc[:, S_DT:S_W]
            outs = []
            for j in range(4):
                if half == 0:
                    pieces = [(0, acc[:, SHARD_COLS * j:SHARD_COLS * (j + 1)])]
                elif j == 0:
                    pieces = [(0, t_ref[:, 4 * SHARD_COLS - S_DT:N_SSD_REAL - S_DT]), (SPLIT, acc[:, 0:SHARD_COLS - SPLIT])]
                else:
                    lo = SHARD_COLS * j - SPLIT
                    pieces = [(0, acc[:, lo:lo + SHARD_COLS])]
                for off, blk in pieces:
                    p_scr[j, :, off:off + blk.shape[1]] = blk
                outs.append(pltpu.make_async_copy(p_scr.at[j], p_ref.at[j], p_sems.at[j]))
                outs[-1].start()
            for cp in outs:
                cp.wait()

    once = pl.Buffered(1)
    whole = lambda shape: pl.BlockSpec(shape, lambda l: (0,) * len(shape), pipeline_mode=once)
    in_specs = [pl.BlockSpec((tl, D_MODEL), lambda l: (l, 0)), pl.BlockSpec((tl, N), lambda l: (l, 0))]
    args = [x, d]
    stack = jax.ShapeDtypeStruct((4, D_MODEL, SHARD_COLS), F32)
    out_shape, out_specs = [stack], [pl.BlockSpec(memory_space=pl.ANY)]
    if half == 0:
        out_shape.append(jax.ShapeDtypeStruct((D_MODEL, S_W - S_DT), F32))
        out_specs.append(whole(out_shape[-1].shape))
    else:
        in_specs.append(whole(tail.shape))
        args.append(tail)
    return pl.pallas_call(
        body, name="dw_in_%d" % half, grid=(steps,), in_specs=in_specs, out_specs=out_specs, out_shape=out_shape,
        scratch_shapes=[pltpu.VMEM((D_MODEL, N), F32), pltpu.VMEM(stack.shape, F32), pltpu.SemaphoreType.DMA((4,))],
        compiler_params=_params(("arbitrary",)),
    )(*args)


def _adamw_math(w, g, m, v):
    m = ADAM_B1 * m + (1.0 - ADAM_B1) * g
    v = ADAM_B2 * v + (1.0 - ADAM_B2) * (g * g)
    m_hat = m / (1.0 - ADAM_B1 ** ADAM_STEP)
    v_hat = v / (1.0 - ADAM_B2 ** ADAM_STEP)
    delta = -ADAM_LR * (m_hat / (jnp.sqrt(v_hat) + ADAM_EPS) + ADAM_WD * w)
    return delta, m, v


def _adamw_shard(g_own, recv, w, m, v, *, rows, name):
    R, C = g_own.shape

    def body(g_ref, r_ref, w_ref, m_ref, v_ref, go_ref, d_ref, mo_ref, vo_ref):
        g = g_ref[...]
        for k in range(N_DEV - 1):
            g = g + r_ref[k].astype(F32)
        d, mn, vn = _adamw_math(w_ref[...], g, m_ref[...], v_ref[...])
        go_ref[...] = g
        d_ref[...] = d
        mo_ref[...] = mn
        vo_ref[...] = vn

    blk = pl.BlockSpec((rows, C), lambda i: (i, 0))
    return pl.pallas_call(
        body, name=name, grid=(R // rows,),
        in_specs=[blk, pl.BlockSpec((N_DEV - 1, rows, C), lambda i: (0, i, 0)), blk, blk, blk],
        out_specs=[blk] * 4, out_shape=[jax.ShapeDtypeStruct((R, C), F32)] * 4,
        compiler_params=_params(("arbitrary",)),
    )(g_own, recv, w, m, v)


def _minor_rows_view(a):
    return jnp.transpose(a, (2, 0, 1)).reshape(SHARD_COLS * 8, 128)


def _from_minor_rows_view(v):
    return jnp.transpose(v.reshape(SHARD_COLS, 8, 128), (1, 2, 0)).reshape(1, D_MODEL, SHARD_COLS)


def _adamw_w_in(is_lo, own_lo, own_hi, recv_lo, recv_hi, w, m, v):
    C = SHARD_COLS
    pad = -C % 128
    bands = [pl.ds(q * 128, 128) for q in range(D_MODEL // 128)]

    def body(lo_ref, ol_ref, oh_ref, rl_ref, rh_ref, w_ref, m_ref, v_ref, go_ref, d_ref, mo_ref, vo_ref,
             own_scr, recv_scr, sems):
        def fetch(own_src, recv_src):
            return [(pltpu.make_async_copy(own_src.at[band, :], own_scr.at[band, :], sems.at[0, q]),
                     pltpu.make_async_copy(recv_src.at[:, band, :], recv_scr.at[:, band, :], sems.at[1, q]))
                    for q, band in enumerate(bands)]

        from_lo, from_hi = fetch(ol_ref, rl_ref), fetch(oh_ref, rh_ref)

        @pl.when(lo_ref[0] == 1)
        def _():
            for pair in from_lo:
                for cp in pair:
                    cp.start()

        @pl.when(lo_ref[0] != 1)
        def _():
            for pair in from_hi:
                for cp in pair:
                    cp.start()

        for q, band in enumerate(bands):
            for cp in from_lo[q]:
                cp.wait()
            g = own_scr[band, :] + recv_scr[0, band, :].astype(F32) + recv_scr[1, band, :].astype(F32)
            g = jnp.pad(g, ((0, 0), (0, pad))).T[0:C]
            rows = pl.ds(q, C, stride=8)
            d, mn, vn = _adamw_math(w_ref[rows, :], g, m_ref[rows, :], v_ref[rows, :])
            go_ref[rows, :] = g
            d_ref[rows, :] = d
            mo_ref[rows, :] = mn
            vo_ref[rows, :] = vn

    vmem = pl.BlockSpec(memory_space=pltpu.VMEM)
    return pl.pallas_call(
        body, name="adamw_w_in", out_shape=[jax.ShapeDtypeStruct(w.shape, F32)] * 4,
        in_specs=[pl.BlockSpec(memory_space=pltpu.SMEM)] + [pl.BlockSpec(memory_space=pl.ANY)] * 4 + [vmem] * 3,
        out_specs=[vmem] * 4,
        scratch_shapes=[pltpu.VMEM(own_lo.shape, F32), pltpu.VMEM(recv_lo.shape, BF16),
                        pltpu.SemaphoreType.DMA((2, len(bands)))],
        compiler_params=_params(),
    )(is_lo, own_lo, own_hi, recv_lo, recv_hi, w, m, v)


SMALL = ("conv_b", "dt_bias", "a_log", "d_skip", "ssd_norm_w", "attn_sinks", "ln_g", "ln_b")


def _adamw_small(gathered, params):
    n_p = len(SMALL)

    def body(*refs):
        acc = []
        for r in refs[:5]:
            t = r[0]
            for k in range(1, N_DEV):
                t = t + r[k]
            acc.append(t)
        head, conv, norm, scal, sink = acc
        grads = dict(conv_b=conv[4:5, :], dt_bias=scal[0:1, 0:N_HEADS], a_log=scal[1:2, 0:N_HEADS],
                     d_skip=scal[2:3, 0:N_HEADS], ssd_norm_w=norm[0:1, :], attn_sinks=sink[0:1, 0:N_HEADS],
                     ln_g=head[0:1, :], ln_b=head[1:2, :])
        wmv = refs[5:5 + 3 * n_p]
        outs = refs[5 + 3 * n_p:]
        outs[0][...] = head[3:4, 0:1]
        outs[1][...] = conv[0:4, :]
        for i, name in enumerate(SMALL):
            w_ref, m_ref, v_ref = wmv[3 * i:3 * i + 3]
            g = grads[name]
            d, mn, vn = _adamw_math(w_ref[...], g, m_ref[...], v_ref[...])
            for o_ref, val in zip(outs[2 + 4 * i:6 + 4 * i], (g, d, mn, vn)):
                o_ref[...] = val

    flat = [a for name in SMALL for a in params[name]]
    out_shape = [jax.ShapeDtypeStruct((1, 1), F32), jax.ShapeDtypeStruct((4, D_XBC), F32)]
    for name in SMALL:
        out_shape += [jax.ShapeDtypeStruct(params[name][0].shape, F32)] * 4
    res = pl.pallas_call(body, name="adamw_small", out_shape=out_shape, compiler_params=_params())(*gathered, *flat)
    return res[0], res[1], {name: res[2 + 4 * i:6 + 4 * i] for i, name in enumerate(SMALL)}


def _adamw_plain(g, w, m, v):
    def body(g_ref, w_ref, m_ref, v_ref, d_ref, mo_ref, vo_ref):
        d, mn, vn = _adamw_math(w_ref[...], g_ref[...], m_ref[...], v_ref[...])
        d_ref[...] = d
        mo_ref[...] = mn
        vo_ref[...] = vn

    return pl.pallas_call(
        body, name="adamw_conv_w", out_shape=[jax.ShapeDtypeStruct(w.shape, F32)] * 3,
        compiler_params=_params(),
    )(g, w, m, v)


def _lane_pattern(fn):
    return np.asarray([fn(l % HEAD_DIM) for l in range(128)], np.float32)


ROPE_INV = _lane_pattern(lambda r: ROPE_THETA ** (-2.0 * (r % 8) / ROPE_DIM) if r < ROPE_DIM else 0.0)


def _rope_tables(pos_ref, inv_ref, tab_ref):
    lane = lax.broadcasted_iota(jnp.int32, (1, 128), 1) % HEAD_DIM
    upper = jnp.where((lane >= ROPE_DIM // 2) & (lane < ROPE_DIM), 1.0, 0.0)
    lower = jnp.where(lane < ROPE_DIM // 2, -1.0, 0.0)

    def block(r, carry):
        rows = pl.ds(pl.multiple_of(r * CHUNK, CHUNK), CHUNK)
        pos = jnp.broadcast_to(pos_ref[pl.ds(r, 1), :].astype(F32), (CHUNK, 128)).T
        ang = pos * inv_ref[...]
        sn = jnp.sin(ang)
        tab_ref[rows, 0:128] = jnp.cos(ang)
        tab_ref[rows, 128:256] = sn * upper
        tab_ref[rows, 256:384] = sn * lower
        return carry

    lax.fori_loop(0, pos_ref.shape[0], block, 0)


def _expansion():
    E = np.arange(1024)[None, :] // HEAD_DIM == np.arange(128)[:, None]
    return jnp.asarray(E, BF16), jnp.asarray(E.T, BF16)


def _ssd_args(conv_w, conv_b, dt_bias, a_log, d_skip, norm_w, E):
    return (conv_w, conv_b, dt_bias.reshape(-1), a_log.reshape(-1), d_skip.reshape(-1), norm_w, E)


def kernel(x, positions, w_in, conv_w, conv_b, dt_bias, a_log, d_skip, ssd_norm_w, attn_sinks, w_out, ln_g, ln_b, loss_target, m_w_in, m_conv_w, m_conv_b, m_dt_bias, m_a_log, m_d_skip, m_ssd_norm_w, m_attn_sinks, m_w_out, m_ln_g, m_ln_b, v_w_in, v_conv_w, v_conv_b, v_dt_bias, v_a_log, v_d_skip, v_ssd_norm_w, v_attn_sinks, v_w_out, v_ln_g, v_ln_b):
    me = _index(*_position())
    x0, target = x[0], loss_target[0]
    bf16_shard = lambda shape: jax.ShapeDtypeStruct(shape, BF16)
    E, ET = _expansion()
    sinks = attn_sinks.reshape(-1)

    w_all, tabs = _gather_w_in(w_in[0].astype(BF16), positions[0].reshape(-1, 128))
    w_ssd, w_att = _unpack_w_in(w_all)
    gather_conv_w = _Hosted([conv_w[0]], [jax.ShapeDtypeStruct((N_DEV,) + conv_w.shape[1:], F32)],
                            [_Flow("gather", 0, 0)])

    proj_ssd, proj_att, conv_w_all = _in_proj(x0, w_ssd, w_att, tm=512, comm=gather_conv_w)
    conv_w_f = jnp.transpose(conv_w_all, (1, 0, 2)).reshape(4, D_XBC)
    ssd_args = _ssd_args(conv_w_f, conv_b, dt_bias, a_log, d_skip, ssd_norm_w, E)
    gather_w_out = _Hosted([w_out[0].astype(BF16)], [bf16_shard((N_DEV, 256, D_MODEL))], [_Flow("gather", 0, 0)])
    y, ypre, hprev, pre, w_out_all = _mixer_forward(proj_ssd, proj_att, tabs, sinks, *ssd_args, comm=gather_w_out)
    w_out_f = w_out_all.reshape(2 * D_MODEL, D_MODEL)
    dr, dy, acc_head = _head(y, x0, target, w_out_f, ln_g, ln_b, tm=512)

    dw_out, dw_out_bf16 = _matmul_tn(y, dr, tl=1024, tn=D_MODEL, name="dw_out", emit_bf16=True)
    own_out = lax.dynamic_index_in_dim(dw_out.reshape(N_DEV, 256, D_MODEL), me, axis=0, keepdims=False)
    send_out = _Hosted([dw_out_bf16.reshape(N_DEV, 256, D_MODEL)], [bf16_shard((N_DEV - 1, 256, D_MODEL))],
                       [_Flow("exchange", 0, 0)])
    d_ssd, acc_cw, acc_w, acc_s, recv_out = _ssd_backward(proj_ssd, hprev, ypre, pre, dy, *ssd_args, ET, comm=send_out)
    stack_lo, dw_dt_block = _dw_in(x0, d_ssd, 0)
    d_att, dsink, own_lo, recv_lo = _swa_backward(proj_att, tabs, sinks, dy, reduce=_OwnerReduce(stack_lo, 0))
    (stack_hi,) = _dw_in(x0, d_att, 1, dw_dt_block)
    accs = [acc_head, acc_cw, acc_w, acc_s, dsink]
    gather_accs = _Hosted(accs, [jax.ShapeDtypeStruct((N_DEV,) + a.shape, F32) for a in accs],
                          [_Flow("gather", i, i) for i in range(5)])
    dx, *gathered, own_hi, recv_hi = _input_gradient(d_ssd, d_att, w_ssd, w_att, dr, tm=256, comm=gather_accs,
                                                     reduce=_OwnerReduce(stack_hi, 1))
    is_lo = (me < 4).reshape(1).astype(jnp.int32)

    g_in, d_in, nm_in, nv_in = [_from_minor_rows_view(r) for r in _adamw_w_in(
        is_lo, own_lo, own_hi, recv_lo, recv_hi, _minor_rows_view(w_in), _minor_rows_view(m_w_in), _minor_rows_view(v_w_in))]
    g_out, d_out, nm_out, nv_out = _adamw_shard(own_out, recv_out, w_out[0], m_w_out[0], v_w_out[0],
                                                rows=256, name="adamw_w_out")
    loss, g_conv_w, small = _adamw_small(gathered, dict(
        conv_b=(conv_b, m_conv_b, v_conv_b), dt_bias=(dt_bias, m_dt_bias, v_dt_bias), a_log=(a_log, m_a_log, v_a_log),
        d_skip=(d_skip, m_d_skip, v_d_skip), ssd_norm_w=(ssd_norm_w, m_ssd_norm_w, v_ssd_norm_w),
        attn_sinks=(attn_sinks, m_attn_sinks, v_attn_sinks), ln_g=(ln_g, m_ln_g, v_ln_g), ln_b=(ln_b, m_ln_b, v_ln_b)))
    g_cw = lax.dynamic_slice_in_dim(g_conv_w, me * (D_XBC // N_DEV), D_XBC // N_DEV, axis=1)
    d_cw, nm_cw, nv_cw = _adamw_plain(g_cw, conv_w[0], m_conv_w[0], v_conv_w[0])

    def leaves(i, big_in, cw, big_out):
        mid = [small[k][i] for k in ("conv_b", "dt_bias", "a_log", "d_skip", "ssd_norm_w", "attn_sinks")]
        return [big_in, cw[None]] + mid + [big_out[None], small["ln_g"][i], small["ln_b"][i]]

    return (loss.reshape(()), dx[None], *leaves(0, g_in, g_cw, g_out), *leaves(1, d_in, d_cw, d_out),
            *leaves(2, nm_in, nm_cw, nm_out), *leaves(3, nv_in, nv_cw, nv_out))
```
